```python
import math
import jax, jax.numpy as jnp
from jax import lax
import numpy as np

D_MODEL = 1024
BATCH = 8
SEQ = 8192
DEPTH = 2

N_A_LAYERS = DEPTH // 2
N_B_LAYERS = DEPTH - N_A_LAYERS

EXPAND = 2
A_WIDTH = EXPAND * D_MODEL
A_GROUPS = 8
A_GROUP_DIM = A_WIDTH // A_GROUPS
CHUNK = 128

HEAD_DIM = 64
N_Q_HEADS = D_MODEL // HEAD_DIM
N_KV_HEADS = max(1, N_Q_HEADS // 8)
Q_PER_KV = N_Q_HEADS // N_KV_HEADS
B_WIDTH = N_Q_HEADS * HEAD_DIM
KV_WIDTH = N_KV_HEADS * HEAD_DIM
WINDOW = 128

REL_BUCKETS = 32
REL_MAX_DIST = 128

ALPHA = (2.0 * DEPTH) ** 0.25
BETA = (8.0 * DEPTH) ** -0.25
LN_EPS = 1e-5
NEG_INF = -1e30

kernel_name = "yoco_gmlp_swa_sink_hybrid"


def layer_norm(x, g, b):
    xf = x.astype(jnp.float32)
    mu = jnp.mean(xf, axis=-1, keepdims=True)
    var = jnp.mean(jnp.square(xf - mu), axis=-1, keepdims=True)
    y = (xf - mu) * lax.rsqrt(var + LN_EPS)
    return (y * g.astype(jnp.float32) + b.astype(jnp.float32)).astype(x.dtype)


def rel_bucket(d):
    max_exact = REL_BUCKETS // 2
    df = jnp.maximum(d, 1).astype(jnp.float32)
    large = max_exact + (jnp.log(df / max_exact) / math.log(REL_MAX_DIST / max_exact)
                         * (REL_BUCKETS - max_exact)).astype(jnp.int32)
    large = jnp.minimum(large, REL_BUCKETS - 1)
    return jnp.where(d < max_exact, d, large)


def sgu_branch(h, w_in, ln_g, ln_b, w_spatial, b_spatial, w_out):
    bsz, seq, _ = h.shape
    nc = seq // CHUNK
    u, v, z = jnp.split(h @ w_in, 3, axis=-1)
    v = layer_norm(v, ln_g, ln_b).reshape(bsz, nc, CHUNK, A_GROUPS, A_GROUP_DIM)
    tri = jnp.tril(jnp.ones((CHUNK, CHUNK), dtype=bool))
    ws = jnp.where(tri, w_spatial, jnp.zeros((), w_spatial.dtype))
    s = jnp.einsum("gts,bcsgd->bctgd", ws, v) + b_spatial.T[:, :, None]
    y = u * s.reshape(bsz, seq, A_WIDTH) * jax.nn.silu(z)
    return y @ w_out


def shared_kv_bands(h, w_kv):
    bsz, seq, _ = h.shape
    nc = seq // CHUNK
    k, v = jnp.split(h @ w_kv, 2, axis=-1)

    def band(t):
        t = t.reshape(bsz, seq, N_KV_HEADS, HEAD_DIM)
        prev = jnp.pad(t, ((0, 0), (CHUNK, 0), (0, 0), (0, 0)))[:, :seq]
        prev = prev.reshape(bsz, nc, CHUNK, N_KV_HEADS, HEAD_DIM)
        cur = t.reshape(bsz, nc, CHUNK, N_KV_HEADS, HEAD_DIM)
        return jnp.concatenate([prev, cur], axis=2)

    return band(k), band(v)


def band_bias_and_mask(rel_bias, nc):
    t = jnp.arange(CHUNK, dtype=jnp.int32)[:, None]
    j = jnp.arange(2 * CHUNK, dtype=jnp.int32)[None, :]
    d = t + CHUNK - j
    in_window = (d >= 0) & (d < WINDOW)
    bias = rel_bias[rel_bucket(jnp.clip(d, 0, REL_MAX_DIST - 1))]
    bias = jnp.transpose(bias, (2, 0, 1)).astype(jnp.float32)
    bias = bias.reshape(N_KV_HEADS, Q_PER_KV, 1, CHUNK, 2 * CHUNK)
    has_prev = (jnp.arange(nc)[:, None, None] > 0) | (j[None] >= CHUNK)
    mask = in_window[None] & has_prev
    return bias, mask


def swa_branch(h, k_band, v_band, bias, mask, w_in, sinks, w_out):
    bsz, seq, _ = h.shape
    nc = seq // CHUNK
    q, z = jnp.split(h @ w_in, 2, axis=-1)
    q = q.reshape(bsz, nc, CHUNK, N_KV_HEADS, Q_PER_KV, HEAD_DIM)
    scores = jnp.einsum("bcqkgd,bcjkd->bkgcqj", q, k_band)
    logits = jnp.where(mask, scores.astype(jnp.float32) * (HEAD_DIM ** -0.5) + bias, NEG_INF)
    sink = sinks.astype(jnp.float32).reshape(N_KV_HEADS, Q_PER_KV, 1, 1, 1)
    m = jnp.maximum(jnp.max(logits, axis=-1, keepdims=True), sink)
    e = jnp.exp(logits - m)
    p = e / (jnp.sum(e, axis=-1, keepdims=True) + jnp.exp(sink - m))
    o = jnp.einsum("bkgcqj,bcjkd->bcqkgd", p.astype(v_band.dtype), v_band)
    y = o.reshape(bsz, seq, B_WIDTH) * jax.nn.silu(z)
    return y @ w_out


def _fwd_setup_inputs(seed: int = 0) -> dict:
    key = jax.random.key(seed)
    ks = jax.random.split(key, 16)
    f32 = jnp.float32
    nrm = lambda k, shape, s: jax.random.normal(k, shape, f32) * s
    return {
        "x": nrm(ks[0], (BATCH, SEQ, D_MODEL), 1.0),
        "w_in_a": nrm(ks[1], (N_A_LAYERS, D_MODEL, 3 * A_WIDTH), D_MODEL ** -0.5),
        "sgu_ln_g": 1.0 + nrm(ks[2], (N_A_LAYERS, A_WIDTH), 0.1),
        "sgu_ln_b": nrm(ks[3], (N_A_LAYERS, A_WIDTH), 0.1),
        "w_spatial": nrm(ks[4], (N_A_LAYERS, A_GROUPS, CHUNK, CHUNK), 0.1),
        "b_spatial": 1.0 + nrm(ks[5], (N_A_LAYERS, A_GROUPS, CHUNK), 0.1),
        "w_out_a": nrm(ks[6], (N_A_LAYERS, A_WIDTH, D_MODEL), BETA * A_WIDTH ** -0.5),
        "w_kv": nrm(ks[7], (D_MODEL, 2 * KV_WIDTH), D_MODEL ** -0.5),
        "w_in_b": nrm(ks[8], (N_B_LAYERS, D_MODEL, 2 * B_WIDTH), D_MODEL ** -0.5),
        "attn_sinks": nrm(ks[9], (N_B_LAYERS, N_Q_HEADS), 0.5),
        "rel_bias": nrm(ks[10], (REL_BUCKETS, N_Q_HEADS), 0.5),
        "w_out_b": nrm(ks[11], (N_B_LAYERS, B_WIDTH, D_MODEL), BETA * B_WIDTH ** -0.5),
        "post_ln_g": 1.0 + nrm(ks[12], (DEPTH, D_MODEL), 0.1),
        "post_ln_b": nrm(ks[13], (DEPTH, D_MODEL), 0.1),
    }


def _fwd_reference(x, w_in_a, sgu_ln_g, sgu_ln_b, w_spatial, b_spatial, w_out_a, w_kv,
              w_in_b, attn_sinks, rel_bias, w_out_b, post_ln_g, post_ln_b):
    nc = x.shape[1] // CHUNK
    bias, mask = band_bias_and_mask(rel_bias, nc)
    h = x
    k_band = None
    v_band = None
    for layer in range(DEPTH):
        if layer < N_A_LAYERS:
            i = layer
            sub = sgu_branch(h, w_in_a[i], sgu_ln_g[i], sgu_ln_b[i], w_spatial[i],
                             b_spatial[i], w_out_a[i])
        else:
            i = layer - N_A_LAYERS
            if i == 0:
                k_band, v_band = shared_kv_bands(h, w_kv)
            sub = swa_branch(h, k_band, v_band, bias, mask, w_in_b[i], attn_sinks[i], w_out_b[i])
        h = layer_norm(ALPHA * h + sub, post_ln_g[layer], post_ln_b[layer])
    return h


import jax as _jax
import jax.numpy as _jnp

TWIN_FORMAT = 'train_step'
FWD_PARAMS = ['x', 'w_in_a', 'sgu_ln_g', 'sgu_ln_b', 'w_spatial', 'b_spatial', 'w_out_a', 'w_kv', 'w_in_b', 'attn_sinks', 'rel_bias', 'w_out_b', 'post_ln_g', 'post_ln_b']
TWIN_WEIGHTS = ['w_in_a', 'sgu_ln_g', 'sgu_ln_b', 'w_spatial', 'b_spatial', 'w_out_a', 'w_kv', 'w_in_b', 'attn_sinks', 'rel_bias', 'w_out_b', 'post_ln_g', 'post_ln_b']
TWIN_DIFF_INPUT = 'x'
TWIN_INPUTS = ['x', 'w_in_a', 'sgu_ln_g', 'sgu_ln_b', 'w_spatial', 'b_spatial', 'w_out_a', 'w_kv', 'w_in_b', 'attn_sinks', 'rel_bias', 'w_out_b', 'post_ln_g', 'post_ln_b', 'loss_target', 'm_w_in_a', 'm_sgu_ln_g', 'm_sgu_ln_b', 'm_w_spatial', 'm_b_spatial', 'm_w_out_a', 'm_w_kv', 'm_w_in_b', 'm_attn_sinks', 'm_rel_bias', 'm_w_out_b', 'm_post_ln_g', 'm_post_ln_b', 'v_w_in_a', 'v_sgu_ln_g', 'v_sgu_ln_b', 'v_w_spatial', 'v_b_spatial', 'v_w_out_a', 'v_w_kv', 'v_w_in_b', 'v_attn_sinks', 'v_rel_bias', 'v_w_out_b', 'v_post_ln_g', 'v_post_ln_b']
TWIN_OUTPUTS = ['loss', 'grad_x', 'grad_w_in_a', 'grad_sgu_ln_g', 'grad_sgu_ln_b', 'grad_w_spatial', 'grad_b_spatial', 'grad_w_out_a', 'grad_w_kv', 'grad_w_in_b', 'grad_attn_sinks', 'grad_rel_bias', 'grad_w_out_b', 'grad_post_ln_g', 'grad_post_ln_b', 'delta_w_in_a', 'delta_sgu_ln_g', 'delta_sgu_ln_b', 'delta_w_spatial', 'delta_b_spatial', 'delta_w_out_a', 'delta_w_kv', 'delta_w_in_b', 'delta_attn_sinks', 'delta_rel_bias', 'delta_w_out_b', 'delta_post_ln_g', 'delta_post_ln_b', 'new_m_w_in_a', 'new_m_sgu_ln_g', 'new_m_sgu_ln_b', 'new_m_w_spatial', 'new_m_b_spatial', 'new_m_w_out_a', 'new_m_w_kv', 'new_m_w_in_b', 'new_m_attn_sinks', 'new_m_rel_bias', 'new_m_w_out_b', 'new_m_post_ln_g', 'new_m_post_ln_b', 'new_v_w_in_a', 'new_v_sgu_ln_g', 'new_v_sgu_ln_b', 'new_v_w_spatial', 'new_v_b_spatial', 'new_v_w_out_a', 'new_v_w_kv', 'new_v_w_in_b', 'new_v_attn_sinks', 'new_v_rel_bias', 'new_v_w_out_b', 'new_v_post_ln_g', 'new_v_post_ln_b']
TWIN_LEAF_KINDS = {'loss': 'loss', 'grad_x': 'grad_x', 'grad_w_in_a': 'grad_w', 'grad_sgu_ln_g': 'grad_w', 'grad_sgu_ln_b': 'grad_w', 'grad_w_spatial': 'grad_w', 'grad_b_spatial': 'grad_w', 'grad_w_out_a': 'grad_w', 'grad_w_kv': 'grad_w', 'grad_w_in_b': 'grad_w', 'grad_attn_sinks': 'grad_w', 'grad_rel_bias': 'grad_w', 'grad_w_out_b': 'grad_w', 'grad_post_ln_g': 'grad_w', 'grad_post_ln_b': 'grad_w', 'delta_w_in_a': 'delta_w', 'delta_sgu_ln_g': 'delta_w', 'delta_sgu_ln_b': 'delta_w', 'delta_w_spatial': 'delta_w', 'delta_b_spatial': 'delta_w', 'delta_w_out_a': 'delta_w', 'delta_w_kv': 'delta_w', 'delta_w_in_b': 'delta_w', 'delta_attn_sinks': 'delta_w', 'delta_rel_bias': 'delta_w', 'delta_w_out_b': 'delta_w', 'delta_post_ln_g': 'delta_w', 'delta_post_ln_b': 'delta_w', 'new_m_w_in_a': 'new_m', 'new_m_sgu_ln_g': 'new_m', 'new_m_sgu_ln_b': 'new_m', 'new_m_w_spatial': 'new_m', 'new_m_b_spatial': 'new_m', 'new_m_w_out_a': 'new_m', 'new_m_w_kv': 'new_m', 'new_m_w_in_b': 'new_m', 'new_m_attn_sinks': 'new_m', 'new_m_rel_bias': 'new_m', 'new_m_w_out_b': 'new_m', 'new_m_post_ln_g': 'new_m', 'new_m_post_ln_b': 'new_m', 'new_v_w_in_a': 'new_v', 'new_v_sgu_ln_g': 'new_v', 'new_v_sgu_ln_b': 'new_v', 'new_v_w_spatial': 'new_v', 'new_v_b_spatial': 'new_v', 'new_v_w_out_a': 'new_v', 'new_v_w_kv': 'new_v', 'new_v_w_in_b': 'new_v', 'new_v_attn_sinks': 'new_v', 'new_v_rel_bias': 'new_v', 'new_v_w_out_b': 'new_v', 'new_v_post_ln_g': 'new_v', 'new_v_post_ln_b': 'new_v'}


def _forward(args):
    return _fwd_reference(*[args[k] for k in FWD_PARAMS])


def _output_shape():
    def fwd():
        inp = _fwd_setup_inputs(0)
        return _fwd_reference(*[inp[k] for k in FWD_PARAMS])
    out = _jax.eval_shape(fwd)
    return out.shape, out.dtype

N_MICROBATCH = 1
ADAM_LR = 0.001
ADAM_B1 = 0.9
ADAM_B2 = 0.999
ADAM_EPS = 1e-08
ADAM_WD = 0.01
ADAM_STEP = 10
PER_EXAMPLE_BATCH_AXIS = {'x': 0, 'loss_target': 0}
SHARED_INPUTS = []
_WEIGHT_DTYPES = {'w_in_a': _jnp.float32, 'sgu_ln_g': _jnp.float32, 'sgu_ln_b': _jnp.float32, 'w_spatial': _jnp.float32, 'b_spatial': _jnp.float32, 'w_out_a': _jnp.float32, 'w_kv': _jnp.float32, 'w_in_b': _jnp.float32, 'attn_sinks': _jnp.float32, 'rel_bias': _jnp.float32, 'w_out_b': _jnp.float32, 'post_ln_g': _jnp.float32, 'post_ln_b': _jnp.float32}
MOMENT_SCALE = {'w_in_a': 4.474777e-02, 'sgu_ln_g': 2.962687e-02, 'sgu_ln_b': 2.975867e-02, 'w_spatial': 3.743288e-02, 'b_spatial': 5.265106e-02, 'w_out_a': 1.437036e-01, 'w_kv': 6.204472e-02, 'w_in_b': 1.271944e-02, 'attn_sinks': 8.968462e-03, 'rel_bias': 1.391726e-02, 'w_out_b': 6.938915e-02, 'post_ln_g': 4.713206e+01, 'post_ln_b': 7.830972e+00}


def _to_microbatches(a, axis):
    t = _jnp.moveaxis(a, axis, 0)
    t = t.reshape((N_MICROBATCH, t.shape[0] // N_MICROBATCH) + t.shape[1:])
    return _jnp.moveaxis(t, 1, axis + 1)


def setup_inputs(seed: int = 0) -> dict:
    inp = _fwd_setup_inputs(seed)
    key = _jax.random.fold_in(_jax.random.key(seed), 7919)
    shape, _ = _output_shape()
    out = dict(inp)
    out["loss_target"] = _jax.random.normal(_jax.random.fold_in(key, 0), shape, _jnp.float32)
    for i, name in enumerate(TWIN_WEIGHTS):
        w = inp[name].astype(_jnp.float32)
        if MOMENT_SCALE is None:
            s = _jnp.sqrt(_jnp.mean(_jnp.square(w)) + 1e-30)
        else:
            s = MOMENT_SCALE[name]
        km, kv = _jax.random.split(_jax.random.fold_in(key, i + 1))
        out[name] = w
        out["m_" + name] = s * _jax.random.normal(km, w.shape, _jnp.float32)
        out["v_" + name] = (s * s) * _jax.random.uniform(kv, w.shape, _jnp.float32, 0.5, 1.5)
    if N_MICROBATCH > 1:
        for name, axis in PER_EXAMPLE_BATCH_AXIS.items():
            out[name] = _to_microbatches(out[name], axis)
    return {'x': out['x'], 'w_in_a': out['w_in_a'], 'sgu_ln_g': out['sgu_ln_g'], 'sgu_ln_b': out['sgu_ln_b'], 'w_spatial': out['w_spatial'], 'b_spatial': out['b_spatial'], 'w_out_a': out['w_out_a'], 'w_kv': out['w_kv'], 'w_in_b': out['w_in_b'], 'attn_sinks': out['attn_sinks'], 'rel_bias': out['rel_bias'], 'w_out_b': out['w_out_b'], 'post_ln_g': out['post_ln_g'], 'post_ln_b': out['post_ln_b'], 'loss_target': out['loss_target'], 'm_w_in_a': out['m_w_in_a'], 'm_sgu_ln_g': out['m_sgu_ln_g'], 'm_sgu_ln_b': out['m_sgu_ln_b'], 'm_w_spatial': out['m_w_spatial'], 'm_b_spatial': out['m_b_spatial'], 'm_w_out_a': out['m_w_out_a'], 'm_w_kv': out['m_w_kv'], 'm_w_in_b': out['m_w_in_b'], 'm_attn_sinks': out['m_attn_sinks'], 'm_rel_bias': out['m_rel_bias'], 'm_w_out_b': out['m_w_out_b'], 'm_post_ln_g': out['m_post_ln_g'], 'm_post_ln_b': out['m_post_ln_b'], 'v_w_in_a': out['v_w_in_a'], 'v_sgu_ln_g': out['v_sgu_ln_g'], 'v_sgu_ln_b': out['v_sgu_ln_b'], 'v_w_spatial': out['v_w_spatial'], 'v_b_spatial': out['v_b_spatial'], 'v_w_out_a': out['v_w_out_a'], 'v_w_kv': out['v_w_kv'], 'v_w_in_b': out['v_w_in_b'], 'v_attn_sinks': out['v_attn_sinks'], 'v_rel_bias': out['v_rel_bias'], 'v_w_out_b': out['v_w_out_b'], 'v_post_ln_g': out['v_post_ln_g'], 'v_post_ln_b': out['v_post_ln_b']}


def _loss(weights, diff, rest, loss_target):
    with _jax.named_scope("forward"):
        args = {**rest, TWIN_DIFF_INPUT: diff, **{k: w.astype(_WEIGHT_DTYPES[k]) for k, w in weights.items()}}
        y = _forward(args)
    with _jax.named_scope("loss_head"):
        err = _jnp.square(y.astype(_jnp.float32) - loss_target)
        return 0.5 * _jnp.sum(_jnp.mean(err, axis=-1)) if err.ndim else 0.5 * err


def _adamw(w, g, m, v):
    m = ADAM_B1 * m + (1.0 - ADAM_B1) * g
    v = ADAM_B2 * v + (1.0 - ADAM_B2) * _jnp.square(g)
    m_hat = m / (1.0 - ADAM_B1 ** ADAM_STEP)
    v_hat = v / (1.0 - ADAM_B2 ** ADAM_STEP)
    delta = -ADAM_LR * (m_hat / (_jnp.sqrt(v_hat) + ADAM_EPS) + ADAM_WD * w)
    return delta, m, v


def reference(x, w_in_a, sgu_ln_g, sgu_ln_b, w_spatial, b_spatial, w_out_a, w_kv, w_in_b, attn_sinks, rel_bias, w_out_b, post_ln_g, post_ln_b, loss_target, m_w_in_a, m_sgu_ln_g, m_sgu_ln_b, m_w_spatial, m_b_spatial, m_w_out_a, m_w_kv, m_w_in_b, m_attn_sinks, m_rel_bias, m_w_out_b, m_post_ln_g, m_post_ln_b, v_w_in_a, v_sgu_ln_g, v_sgu_ln_b, v_w_spatial, v_b_spatial, v_w_out_a, v_w_kv, v_w_in_b, v_attn_sinks, v_rel_bias, v_w_out_b, v_post_ln_g, v_post_ln_b):
    given = dict(x=x, w_in_a=w_in_a, sgu_ln_g=sgu_ln_g, sgu_ln_b=sgu_ln_b, w_spatial=w_spatial, b_spatial=b_spatial, w_out_a=w_out_a, w_kv=w_kv, w_in_b=w_in_b, attn_sinks=attn_sinks, rel_bias=rel_bias, w_out_b=w_out_b, post_ln_g=post_ln_g, post_ln_b=post_ln_b, loss_target=loss_target, m_w_in_a=m_w_in_a, m_sgu_ln_g=m_sgu_ln_g, m_sgu_ln_b=m_sgu_ln_b, m_w_spatial=m_w_spatial, m_b_spatial=m_b_spatial, m_w_out_a=m_w_out_a, m_w_kv=m_w_kv, m_w_in_b=m_w_in_b, m_attn_sinks=m_attn_sinks, m_rel_bias=m_rel_bias, m_w_out_b=m_w_out_b, m_post_ln_g=m_post_ln_g, m_post_ln_b=m_post_ln_b, v_w_in_a=v_w_in_a, v_sgu_ln_g=v_sgu_ln_g, v_sgu_ln_b=v_sgu_ln_b, v_w_spatial=v_w_spatial, v_b_spatial=v_b_spatial, v_w_out_a=v_w_out_a, v_w_kv=v_w_kv, v_w_in_b=v_w_in_b, v_attn_sinks=v_attn_sinks, v_rel_bias=v_rel_bias, v_w_out_b=v_w_out_b, v_post_ln_g=v_post_ln_g, v_post_ln_b=v_post_ln_b)
    weights = {n: given[n] for n in TWIN_WEIGHTS}
    shared = {n: given[n] for n in SHARED_INPUTS}
    per_example = {n: given[n] for n in ['x']}
    grad_fn = _jax.value_and_grad(_loss, argnums=(0, 1))

    def one_microbatch(ex, loss_target):
        ex = dict(ex)
        diff = ex.pop(TWIN_DIFF_INPUT)
        return grad_fn(weights, diff, {**shared, **ex}, loss_target)

    if N_MICROBATCH == 1:
        loss, (grad_w, grad_x) = one_microbatch(per_example, given["loss_target"])
    else:
        def body(carry, xs):
            loss_sum, grad_sum = carry
            l_k, (gw_k, gx_k) = one_microbatch(xs[0], xs[1])
            with _jax.named_scope("update"):
                return (loss_sum + l_k, _jax.tree.map(_jnp.add, grad_sum, gw_k)), gx_k

        init = (_jnp.zeros((), _jnp.float32), _jax.tree.map(_jnp.zeros_like, weights))
        (loss, grad_w), grad_x = _jax.lax.scan(body, init, (per_example, given["loss_target"]))
    with _jax.named_scope("update"):
        delta_w, new_m, new_v = {}, {}, {}
        for n in TWIN_WEIGHTS:
            delta_w[n], new_m[n], new_v[n] = _adamw(weights[n], grad_w[n], given["m_" + n], given["v_" + n])
    return (loss, grad_x, *[grad_w[n] for n in TWIN_WEIGHTS], *[delta_w[n] for n in TWIN_WEIGHTS],
            *[new_m[n] for n in TWIN_WEIGHTS], *[new_v[n] for n in TWIN_WEIGHTS])
```

```python
import functools

import jax
import jax.numpy as jnp
import numpy as np
from jax import lax
from jax.experimental import pallas as pl
from jax.experimental.pallas import tpu as pltpu

F32 = jnp.float32
BF16 = jnp.bfloat16

D_MODEL = 1024
A_WIDTH = 2048
A_GROUPS = 8
A_GROUP_DIM = A_WIDTH // A_GROUPS
CHUNK = 128
HEAD_DIM = 64
N_Q_HEADS = 16
N_KV_HEADS = 2
PAIRS_PER_KV = N_Q_HEADS // N_KV_HEADS // 2
N_PAIRS = N_Q_HEADS // 2
B_WIDTH = N_Q_HEADS * HEAD_DIM
KV_WIDTH = N_KV_HEADS * HEAD_DIM
REL_BUCKETS = 32
REL_MAX_DIST = 128
DEPTH = 2
ALPHA = (2.0 * DEPTH) ** 0.25
LN_EPS = 1e-5
NEG_INF = -1e30
SCALE = HEAD_DIM ** -0.5

ADAM_LR = 0.001
ADAM_B1 = 0.9
ADAM_B2 = 0.999
ADAM_EPS = 1e-08
ADAM_WD = 0.01
ADAM_STEP = 10

N_DEV = 8
MESH = pl.DeviceIdType.MESH
TOKENS_PER_STEP = 256
VMEM_LIMIT_BYTES = 60 * 1024 * 1024
SMALL_ROWS = 144

ANY = pl.BlockSpec(memory_space=pl.ANY)
VMEM = pl.BlockSpec(memory_space=pltpu.VMEM)
SMEM = pl.BlockSpec(memory_space=pltpu.SMEM)


def _nn(a, b):
    return jnp.dot(a, b, preferred_element_type=F32)


def _nt(a, b):
    return lax.dot_general(a, b, (((1,), (1,)), ((), ())), preferred_element_type=F32)


def _tn(a, b):
    return lax.dot_general(a, b, (((0,), (0,)), ((), ())), preferred_element_type=F32)


def _resident(shape):
    nd = len(shape)
    return pl.BlockSpec(shape, lambda i: (0,) * nd, pipeline_mode=pl.Buffered(1))


def _rows(tm, width, reverse_of=None):
    if reverse_of is None:
        return pl.BlockSpec((tm, width), lambda i: (i, 0))
    return pl.BlockSpec((tm, width), lambda i: (reverse_of - 1 - i, 0))


def _seq_params():
    return pltpu.CompilerParams(dimension_semantics=("arbitrary",), vmem_limit_bytes=VMEM_LIMIT_BYTES)


def _ln_normalize(r):
    mu = jnp.mean(r, axis=-1, keepdims=True)
    xc = r - mu
    var = jnp.mean(xc * xc, axis=-1, keepdims=True)
    rstd = lax.rsqrt(var + LN_EPS)
    return xc * rstd, rstd


def _ln_backward(dxhat, xhat, rstd):
    m1 = jnp.mean(dxhat, axis=-1, keepdims=True)
    m2 = jnp.mean(dxhat * xhat, axis=-1, keepdims=True)
    return rstd * (dxhat - m1 - xhat * m2)


def _silu_and_grad(z):
    sig = jax.nn.sigmoid(z)
    return z * sig, sig * (1.0 + z * (1.0 - sig))


def _band_tables():
    t = np.arange(CHUNK, dtype=np.int32)[:, None]
    j = np.arange(2 * CHUNK, dtype=np.int32)[None, :]
    d = t + CHUNK - j
    in_window = (d >= 0) & (d < CHUNK)
    dc = np.clip(d, 0, REL_MAX_DIST - 1)
    max_exact = REL_BUCKETS // 2
    df = np.maximum(dc, 1).astype(np.float32)
    large = max_exact + (np.log(df / np.float32(max_exact)) / np.float32(np.log(REL_MAX_DIST / max_exact))
                         * np.float32(REL_BUCKETS - max_exact)).astype(np.int32)
    large = np.minimum(large, REL_BUCKETS - 1)
    bucket = np.where(dc < max_exact, dc, large)
    return np.where(in_window, bucket, -1).astype(np.int32)


def _gather_weights(w_in_a, w_out_a, w_kv, w_in_b, w_out_b, ln_gb):
    shards = (w_in_a, w_out_a, w_kv, w_in_b, w_out_b)
    n = len(shards) + 1

    def body(*refs):
        ins, outs = refs[:n], refs[n:2 * n]
        stage = refs[2 * n:3 * n - 1]
        send_sems, recv_sems, local_sems = refs[3 * n - 1:]
        x, y, c = lax.axis_index("x"), lax.axis_index("y"), lax.axis_index("c")
        me, sibling = (x, y, c), (x, y, 1 - c)
        chips = [(1 - x, y), (x, 1 - y), (1 - x, 1 - y)]
        srcs = []
        for i in range(n - 1):
            stage[i][...] = ins[i][...].astype(BF16)
            srcs.append(stage[i])
        srcs.append(ins[n - 1])

        def slot(a, px, py, pc):
            return outs[a].at[4 * px + 2 * py + pc]

        def copy(a, k, block, to, src=None):
            return pltpu.make_async_remote_copy(
                src_ref=slot(a, *block) if src is None else src, dst_ref=slot(a, *block),
                send_sem=send_sems.at[a, k], recv_sem=recv_sems.at[a, k], device_id=to, device_id_type=MESH)

        mine, first, passed = [], [], []
        for a in range(n):
            mine.append(pltpu.make_async_copy(srcs[a], slot(a, *me), local_sems.at[a]))
            mine[-1].start()
            first.append(copy(a, 0, me, sibling, srcs[a]))
            for j, chip in enumerate(chips):
                first.append(copy(a, 1 + j, me, (*chip, c), srcs[a]))
        for cp in first:
            cp.start()
        for j, chip in enumerate(chips):
            for a in range(n):
                copy(a, 1 + j, (*chip, c), me).wait_recv()
                passed.append(copy(a, 4 + j, (*chip, c), sibling))
                passed[-1].start()
        for a in range(n):
            copy(a, 0, sibling, me).wait_recv()
            for j, chip in enumerate(chips):
                copy(a, 4 + j, (*chip, 1 - c), me).wait_recv()
        for cp in first + passed:
            cp.wait_send()
        for cp in mine:
            cp.wait()

    ins = shards + (ln_gb,)
    out_shape = [jax.ShapeDtypeStruct((N_DEV,) + s.shape, BF16) for s in shards]
    out_shape.append(jax.ShapeDtypeStruct((N_DEV,) + ln_gb.shape, F32))
    return pl.pallas_call(
        body, name="gather_weights", out_shape=out_shape,
        in_specs=[VMEM] * n, out_specs=[ANY] * n,
        scratch_shapes=[pltpu.VMEM(s.shape, BF16) for s in shards] + [
            pltpu.SemaphoreType.DMA((n, 7)), pltpu.SemaphoreType.DMA((n, 7)), pltpu.SemaphoreType.DMA((n,))],
        compiler_params=pltpu.CompilerParams(vmem_limit_bytes=VMEM_LIMIT_BYTES),
    )(*ins)


def _bias_table(rel_bias, bucket):
    def body(rel_ref, bucket_ref, out_ref):
        bk = bucket_ref[...]
        col = lax.broadcasted_iota(jnp.int32, bk.shape, 1)
        for h in range(N_Q_HEADS):
            def step(b, acc):
                return jnp.where(bk == b, rel_ref[b, h], acc)
            tab = lax.fori_loop(0, REL_BUCKETS, step, jnp.full(bk.shape, NEG_INF, F32))
            half = slice((h % 2) * 2 * CHUNK, (h % 2 + 1) * 2 * CHUNK)
            out_ref[1, h // 2, :, half] = tab
            out_ref[0, h // 2, :, half] = jnp.where(col < CHUNK, NEG_INF, tab)

    return pl.pallas_call(
        body, name="bias_table", out_shape=jax.ShapeDtypeStruct((2, N_PAIRS, CHUNK, 4 * CHUNK), F32),
        in_specs=[SMEM, VMEM], out_specs=VMEM,
    )(rel_bias, bucket)


def _rel_bias_grad(dlogits, bucket):
    def body(dl_ref, bucket_ref, out_ref):
        bk = bucket_ref[...]
        row = lax.broadcasted_iota(jnp.int32, (REL_BUCKETS, CHUNK), 0)
        lane = lax.broadcasted_iota(jnp.int32, (REL_BUCKETS, CHUNK), 1)

        def step(b, acc):
            for h in range(N_Q_HEADS):
                dl = dl_ref[h // 2, :, (h % 2) * 2 * CHUNK:(h % 2 + 1) * 2 * CHUNK]
                s = jnp.sum(jnp.where(bk == b, dl, 0.0))
                acc = jnp.where((row == b) & (lane == h), s, acc)
            return acc

        out_ref[...] = lax.fori_loop(0, REL_BUCKETS, step, jnp.zeros((REL_BUCKETS, CHUNK), F32))

    return pl.pallas_call(
        body, name="rel_bias_grad", out_shape=jax.ShapeDtypeStruct((REL_BUCKETS, CHUNK), F32),
        in_specs=[VMEM, VMEM], out_specs=VMEM,
    )(dlogits, bucket)


def _masked_spatial(ws_ref, g):
    t = lax.broadcasted_iota(jnp.int32, (CHUNK, CHUNK), 0)
    s = lax.broadcasted_iota(jnp.int32, (CHUNK, CHUNK), 1)
    return jnp.where(s <= t, ws_ref[g], 0.0).astype(BF16)


def _spatial_gate(ws_ref, bsp_t_ref, vn, s_ref, n_chunks):
    for g in range(A_GROUPS):
        w = _masked_spatial(ws_ref, g)
        cols = slice(g * A_GROUP_DIM, (g + 1) * A_GROUP_DIM)
        for ci in range(n_chunks):
            rows = slice(ci * CHUNK, (ci + 1) * CHUNK)
            s_ref[rows, cols] = _nn(w, vn[rows, cols]) + bsp_t_ref[:, g:g + 1]


def _fwd_a(x, wa, ln_gb, w_spatial, bsp_t, woa):
    seq = x.shape[0]
    tm = min(TOKENS_PER_STEP, seq)
    n_chunks = tm // CHUNK
    shard_cols = wa.shape[2]

    def body(x_ref, wa_ref, ln_ref, ws_ref, bsp_ref, woa_ref, p_ref, r0_ref, xb_ref, s_ref):
        xv = x_ref[...]
        xb = xv.astype(BF16)
        xb_ref[...] = xb
        for k in range(N_DEV):
            p_ref[:, k * shard_cols:(k + 1) * shard_cols] = _nn(xb, wa_ref[k])
        vhat, _ = _ln_normalize(p_ref[:, A_WIDTH:2 * A_WIDTH])
        vn = (vhat * ln_ref[0:1, :] + ln_ref[1:2, :]).astype(BF16)
        _spatial_gate(ws_ref, bsp_ref, vn, s_ref, n_chunks)
        z = p_ref[:, 2 * A_WIDTH:]
        y = p_ref[:, :A_WIDTH] * s_ref[...] * (z * jax.nn.sigmoid(z))
        r0_ref[...] = ALPHA * xv + _nn(y.astype(BF16), woa_ref[...])

    return pl.pallas_call(
        body, name="fwd_a", grid=(seq // tm,),
        out_shape=[jax.ShapeDtypeStruct((seq, 3 * A_WIDTH), F32), jax.ShapeDtypeStruct((seq, D_MODEL), F32),
                   jax.ShapeDtypeStruct((seq, D_MODEL), BF16)],
        in_specs=[_rows(tm, D_MODEL), _resident(wa.shape), _resident(ln_gb.shape), _resident(w_spatial.shape),
                  _resident(bsp_t.shape), _resident(woa.shape)],
        out_specs=[_rows(tm, 3 * A_WIDTH), _rows(tm, D_MODEL), _rows(tm, D_MODEL)],
        scratch_shapes=[pltpu.VMEM((tm, A_WIDTH), F32)],
        compiler_params=_seq_params(),
    )(x, wa, ln_gb, w_spatial, bsp_t, woa)


def _expand_heads(kv):
    lo = lax.broadcasted_iota(jnp.int32, kv.shape, 1) < HEAD_DIM
    swapped = pltpu.roll(kv, HEAD_DIM, 1)
    zero = jnp.zeros_like(kv)
    return (jnp.where(lo, kv, zero).astype(BF16), jnp.where(lo, zero, swapped).astype(BF16),
            jnp.where(lo, swapped, zero).astype(BF16), jnp.where(lo, zero, kv).astype(BF16))


def _fold_heads(acc, kvh):
    top, bot = acc[:2 * CHUNK], acc[2 * CHUNK:]
    lo = lax.broadcasted_iota(jnp.int32, top.shape, 1) < HEAD_DIM
    t = jnp.where(lo, top, 0.0)
    b = jnp.where(lo, 0.0, bot)
    if kvh == 0:
        return t + pltpu.roll(b, HEAD_DIM, 1)
    return pltpu.roll(t, HEAD_DIM, 1) + b


def _softmax_pair(logits, sink_ref, pair):
    ps, sinks = [], []
    for half in range(2):
        lg = logits[:, half * 2 * CHUNK:(half + 1) * 2 * CHUNK]
        sk = sink_ref[0, 2 * pair + half]
        m = jnp.maximum(jnp.max(lg, axis=-1, keepdims=True), sk)
        e = jnp.exp(lg - m)
        es = jnp.exp(sk - m)
        inv = 1.0 / (jnp.sum(e, axis=-1, keepdims=True) + es)
        ps.append(e * inv)
        sinks.append(es * inv)
    return jnp.concatenate(ps, axis=1), sinks


def _fwd_b(r0, target, wb, wkv, wob, post_g, post_b, sinks, bias):
    seq = r0.shape[0]
    tm = min(TOKENS_PER_STEP, seq)
    n_chunks = tm // CHUNK
    shard_cols = wb.shape[2]

    def body(r0_ref, tgt_ref, wb_ref, wkv_ref, wob_ref, pg_ref, pb_ref, sink_ref, bias_ref,
             q_ref, z_ref, kv_ref, dr1_ref, loss_ref, dg_ref, db_ref, ek_ref, ev_ref, o_ref):
        step = pl.program_id(0)

        @pl.when(step == 0)
        def _():
            ek_ref[0] = jnp.zeros(ek_ref.shape[1:], BF16)
            ev_ref[0] = jnp.zeros(ev_ref.shape[1:], BF16)
            loss_ref[...] = jnp.zeros_like(loss_ref)
            dg_ref[...] = jnp.zeros_like(dg_ref)
            db_ref[...] = jnp.zeros_like(db_ref)

        @pl.when(step > 0)
        def _():
            ek_ref[0] = ek_ref[n_chunks]
            ev_ref[0] = ev_ref[n_chunks]

        xhat0, _ = _ln_normalize(r0_ref[...])
        h1 = xhat0 * pg_ref[0:1, :] + pb_ref[0:1, :]
        h1b = h1.astype(BF16)
        for k in range(N_DEV):
            qz = _nn(h1b, wb_ref[k])
            lo, hi = k * shard_cols, (k + 1) * shard_cols
            if hi <= B_WIDTH:
                q_ref[:, lo:hi] = qz.astype(BF16)
            else:
                z_ref[:, lo - B_WIDTH:hi - B_WIDTH] = qz
        kv = _nn(h1b, wkv_ref[...])
        kv_ref[...] = kv.astype(BF16)
        for ci in range(n_chunks):
            rows = slice(ci * CHUNK, (ci + 1) * CHUNK)
            for i, e in enumerate(_expand_heads(kv[rows, :KV_WIDTH])):
                ek_ref[ci + 1, i] = e
            for i, e in enumerate(_expand_heads(kv[rows, KV_WIDTH:])):
                ev_ref[ci + 1, i] = e

        for ci in range(n_chunks):
            rows = slice(ci * CHUNK, (ci + 1) * CHUNK)
            sel = jnp.where(step * n_chunks + ci > 0, 1, 0)
            for kvh in range(N_KV_HEADS):
                kblk = jnp.concatenate([ek_ref[ci, 2 * kvh], ek_ref[ci + 1, 2 * kvh],
                                        ek_ref[ci, 2 * kvh + 1], ek_ref[ci + 1, 2 * kvh + 1]], axis=0)
                vblk = jnp.concatenate([ev_ref[ci, 2 * kvh], ev_ref[ci + 1, 2 * kvh],
                                        ev_ref[ci, 2 * kvh + 1], ev_ref[ci + 1, 2 * kvh + 1]], axis=0)
                for pp in range(PAIRS_PER_KV):
                    pair = kvh * PAIRS_PER_KV + pp
                    cols = slice(pair * CHUNK, (pair + 1) * CHUNK)
                    logits = _nt(q_ref[rows, cols], kblk) * SCALE + bias_ref[sel, pair]
                    p, _ = _softmax_pair(logits, sink_ref, pair)
                    o_ref[rows, cols] = _nn(p.astype(BF16), vblk)

        z = z_ref[...]
        y = o_ref[...] * (z * jax.nn.sigmoid(z))
        r1 = ALPHA * h1 + _nn(y.astype(BF16), wob_ref[...])
        xhat1, rstd1 = _ln_normalize(r1)
        g1 = pg_ref[1:2, :]
        diff = xhat1 * g1 + pb_ref[1:2, :] - tgt_ref[...]
        loss_ref[...] += 0.5 * jnp.sum(jnp.mean(diff * diff, axis=-1, keepdims=True), axis=0, keepdims=True)
        dh2 = diff * (1.0 / D_MODEL)
        dg_ref[...] += jnp.sum(dh2 * xhat1, axis=0, keepdims=True)
        db_ref[...] += jnp.sum(dh2, axis=0, keepdims=True)
        dr1_ref[...] = _ln_backward(dh2 * g1, xhat1, rstd1)

    vec = pl.BlockSpec((1, D_MODEL), lambda i: (0, 0))
    return pl.pallas_call(
        body, name="fwd_b", grid=(seq // tm,),
        out_shape=[jax.ShapeDtypeStruct((seq, B_WIDTH), BF16), jax.ShapeDtypeStruct((seq, B_WIDTH), F32),
                   jax.ShapeDtypeStruct((seq, 2 * KV_WIDTH), BF16), jax.ShapeDtypeStruct((seq, D_MODEL), F32),
                   jax.ShapeDtypeStruct((1, 1), F32), jax.ShapeDtypeStruct((1, D_MODEL), F32),
                   jax.ShapeDtypeStruct((1, D_MODEL), F32)],
        in_specs=[_rows(tm, D_MODEL), _rows(tm, D_MODEL), _resident(wb.shape), _resident(wkv.shape),
                  _resident(wob.shape), _resident(post_g.shape), _resident(post_b.shape), SMEM,
                  _resident(bias.shape)],
        out_specs=[_rows(tm, B_WIDTH), _rows(tm, B_WIDTH), _rows(tm, 2 * KV_WIDTH), _rows(tm, D_MODEL),
                   pl.BlockSpec((1, 1), lambda i: (0, 0)), vec, vec],
        scratch_shapes=[pltpu.VMEM((n_chunks + 1, 4, CHUNK, CHUNK), BF16),
                        pltpu.VMEM((n_chunks + 1, 4, CHUNK, CHUNK), BF16),
                        pltpu.VMEM((tm, B_WIDTH), F32)],
        compiler_params=_seq_params(),
    )(r0, target, wb, wkv, wob, post_g, post_b, sinks, bias)


def _bwd_b(r0, dr1, q, z, kv, wb, wkv, wob, post_g, post_b, sinks, bias):
    seq = r0.shape[0]
    tm = min(TOKENS_PER_STEP, seq)
    n_chunks = tm // CHUNK
    n_blocks = seq // tm
    shard_cols = wb.shape[2]

    def body(r0_ref, dr1_ref, q_ref, z_ref, kv_ref, kvp_ref, wb_ref, wkv_ref, wob_ref, pg_ref, pb_ref, sink_ref,
             bias_ref, dh1_ref, dwb_out, dwkv_out, dwob_out, dsink_ref, dl_out,
             dwb_acc, dwkv_acc, dwob_acc, dl_acc, carry_ref, ek_ref, ev_ref, y_ref, dqz_ref, dkv_ref):
        step = pl.program_id(0)
        block = n_blocks - 1 - step

        @pl.when(step == 0)
        def _():
            dwb_acc[...] = jnp.zeros_like(dwb_acc)
            dwkv_acc[...] = jnp.zeros_like(dwkv_acc)
            dwob_acc[...] = jnp.zeros_like(dwob_acc)
            dl_acc[...] = jnp.zeros_like(dl_acc)
            carry_ref[...] = jnp.zeros_like(carry_ref)
            dsink_ref[...] = jnp.zeros_like(dsink_ref)

        xhat0, _ = _ln_normalize(r0_ref[...])
        h1b = (xhat0 * pg_ref[0:1, :] + pb_ref[0:1, :]).astype(BF16)
        dr1 = dr1_ref[...]
        dr1b = dr1.astype(BF16)
        dy = _nt(dr1b, wob_ref[...])
        sg, dsg = _silu_and_grad(z_ref[...])
        do_all = dy * sg
        dyz = dy * dsg

        for ci in range(n_chunks + 1):
            src = kvp_ref[...] if ci == 0 else kv_ref[(ci - 1) * CHUNK:ci * CHUNK, :]
            src = src.astype(F32)
            for i, e in enumerate(_expand_heads(src[:, :KV_WIDTH])):
                ek_ref[ci, i] = e
            for i, e in enumerate(_expand_heads(src[:, KV_WIDTH:])):
                ev_ref[ci, i] = e

        lane = lax.broadcasted_iota(jnp.int32, (1, CHUNK), 1)
        dsink = jnp.zeros((1, CHUNK), F32)
        for ci in reversed(range(n_chunks)):
            rows = slice(ci * CHUNK, (ci + 1) * CHUNK)
            sel = jnp.where(block * n_chunks + ci > 0, 1, 0)
            dk_band = jnp.zeros((2 * CHUNK, KV_WIDTH), F32)
            dv_band = jnp.zeros((2 * CHUNK, KV_WIDTH), F32)
            for kvh in range(N_KV_HEADS):
                kblk = jnp.concatenate([ek_ref[ci, 2 * kvh], ek_ref[ci + 1, 2 * kvh],
                                        ek_ref[ci, 2 * kvh + 1], ek_ref[ci + 1, 2 * kvh + 1]], axis=0)
                vblk = jnp.concatenate([ev_ref[ci, 2 * kvh], ev_ref[ci + 1, 2 * kvh],
                                        ev_ref[ci, 2 * kvh + 1], ev_ref[ci + 1, 2 * kvh + 1]], axis=0)
                dk_acc = jnp.zeros((4 * CHUNK, CHUNK), F32)
                dv_acc = jnp.zeros((4 * CHUNK, CHUNK), F32)
                for pp in range(PAIRS_PER_KV):
                    pair = kvh * PAIRS_PER_KV + pp
                    cols = slice(pair * CHUNK, (pair + 1) * CHUNK)
                    qp = q_ref[rows, cols]
                    logits = _nt(qp, kblk) * SCALE + bias_ref[sel, pair]
                    p, psink = _softmax_pair(logits, sink_ref, pair)
                    pb16 = p.astype(BF16)
                    o = _nn(pb16, vblk)
                    y_ref[rows, cols] = (o * sg[rows, cols]).astype(BF16)
                    dqz_ref[rows, B_WIDTH + pair * CHUNK:B_WIDTH + (pair + 1) * CHUNK] = (
                        o * dyz[rows, cols]).astype(BF16)
                    dob = do_all[rows, cols].astype(BF16)
                    dp = _nt(dob, vblk)
                    dlogits = []
                    for half in range(2):
                        hs = slice(half * 2 * CHUNK, (half + 1) * 2 * CHUNK)
                        delta = jnp.sum(dp[:, hs] * p[:, hs], axis=-1, keepdims=True)
                        dlogits.append(p[:, hs] * (dp[:, hs] - delta))
                        ds = -jnp.sum(psink[half] * delta, axis=0, keepdims=True)
                        dsink = dsink + jnp.where(lane == 2 * pair + half, ds, 0.0)
                    dlog = jnp.concatenate(dlogits, axis=1)
                    dl_acc[pair] += dlog
                    dsc = (dlog * SCALE).astype(BF16)
                    dqz_ref[rows, cols] = _nn(dsc, kblk).astype(BF16)
                    dk_acc = dk_acc + _tn(dsc, qp)
                    dv_acc = dv_acc + _tn(pb16, dob)
                dk_band = dk_band + _fold_heads(dk_acc, kvh)
                dv_band = dv_band + _fold_heads(dv_acc, kvh)
            band = jnp.concatenate([dk_band, dv_band], axis=1)
            dkv_ref[rows, :] = (band[CHUNK:] + carry_ref[...]).astype(BF16)
            carry_ref[...] = band[:CHUNK]
        dsink_ref[...] += dsink

        dqz = dqz_ref[...]
        dkv = dkv_ref[...]
        dh1 = ALPHA * dr1 + _nt(dkv, wkv_ref[...])
        for k in range(N_DEV):
            part = dqz[:, k * shard_cols:(k + 1) * shard_cols]
            dh1 = dh1 + _nt(part, wb_ref[k])
            dwb_acc[k] += _tn(h1b, part)
        dh1_ref[...] = dh1
        dwkv_acc[...] += _tn(h1b, dkv)
        dwob_acc[...] += _tn(y_ref[...], dr1b)

        @pl.when(step == n_blocks - 1)
        def _():
            pltpu.sync_copy(dwb_acc, dwb_out)
            pltpu.sync_copy(dwkv_acc, dwkv_out)
            pltpu.sync_copy(dwob_acc, dwob_out)
            pltpu.sync_copy(dl_acc, dl_out)

    rev = functools.partial(_rows, tm, reverse_of=n_blocks)
    prev_chunk = pl.BlockSpec(
        (CHUNK, 2 * KV_WIDTH), lambda i: (jnp.maximum((n_blocks - 1 - i) * n_chunks - 1, 0), 0))
    return pl.pallas_call(
        body, name="bwd_b", grid=(n_blocks,),
        out_shape=[jax.ShapeDtypeStruct((seq, D_MODEL), F32), jax.ShapeDtypeStruct(wb.shape, F32),
                   jax.ShapeDtypeStruct(wkv.shape, F32), jax.ShapeDtypeStruct(wob.shape, F32),
                   jax.ShapeDtypeStruct((1, CHUNK), F32),
                   jax.ShapeDtypeStruct((N_PAIRS, CHUNK, 4 * CHUNK), F32)],
        in_specs=[rev(D_MODEL), rev(D_MODEL), rev(B_WIDTH), rev(B_WIDTH), rev(2 * KV_WIDTH), prev_chunk,
                  _resident(wb.shape), _resident(wkv.shape), _resident(wob.shape), _resident(post_g.shape),
                  _resident(post_b.shape), SMEM, _resident(bias.shape)],
        out_specs=[rev(D_MODEL), ANY, ANY, ANY, pl.BlockSpec((1, CHUNK), lambda i: (0, 0)), ANY],
        scratch_shapes=[pltpu.VMEM(wb.shape, F32), pltpu.VMEM(wkv.shape, F32), pltpu.VMEM(wob.shape, F32),
                        pltpu.VMEM((N_PAIRS, CHUNK, 4 * CHUNK), F32), pltpu.VMEM((CHUNK, 2 * KV_WIDTH), F32),
                        pltpu.VMEM((n_chunks + 1, 4, CHUNK, CHUNK), BF16),
                        pltpu.VMEM((n_chunks + 1, 4, CHUNK, CHUNK), BF16),
                        pltpu.VMEM((tm, B_WIDTH), BF16), pltpu.VMEM((tm, 2 * B_WIDTH), BF16),
                        pltpu.VMEM((tm, 2 * KV_WIDTH), BF16)],
        compiler_params=_seq_params(),
    )(r0, dr1, q, z, kv, kv, wb, wkv, wob, post_g, post_b, sinks, bias)


def _bwd_a(dh1, r0, p, ln_gb, w_spatial, bsp_t, woa, post_g):
    seq = r0.shape[0]
    tm = min(TOKENS_PER_STEP, seq)
    n_chunks = tm // CHUNK
    n_blocks = seq // tm

    def body(dh1_ref, r0_ref, p_ref, ln_ref, ws_ref, bsp_ref, woa_ref, pg_ref,
             dp_ref, dr0_ref, dwoa_out, dws_ref, dbsp_ref, dln_ref, dpost_ref,
             dwoa_acc, s_ref, ds_ref, dvn_ref):
        step = pl.program_id(0)

        @pl.when(step == 0)
        def _():
            dwoa_acc[...] = jnp.zeros_like(dwoa_acc)
            dws_ref[...] = jnp.zeros_like(dws_ref)
            dbsp_ref[...] = jnp.zeros_like(dbsp_ref)
            dln_ref[...] = jnp.zeros_like(dln_ref)
            dpost_ref[...] = jnp.zeros_like(dpost_ref)

        dh1 = dh1_ref[...]
        xhat0, rstd0 = _ln_normalize(r0_ref[...])
        dpost_ref[0:1, :] += jnp.sum(dh1 * xhat0, axis=0, keepdims=True)
        dpost_ref[1:2, :] += jnp.sum(dh1, axis=0, keepdims=True)
        dr0 = _ln_backward(dh1 * pg_ref[0:1, :], xhat0, rstd0)
        dr0_ref[...] = dr0
        dr0b = dr0.astype(BF16)

        vhat, rstd_v = _ln_normalize(p_ref[:, A_WIDTH:2 * A_WIDTH])
        ln_g = ln_ref[0:1, :]
        vn = (vhat * ln_g + ln_ref[1:2, :]).astype(BF16)
        _spatial_gate(ws_ref, bsp_ref, vn, s_ref, n_chunks)
        u = p_ref[:, :A_WIDTH]
        s = s_ref[...]
        sg, dsg = _silu_and_grad(p_ref[:, 2 * A_WIDTH:])
        dwoa_acc[...] += _tn((u * s * sg).astype(BF16), dr0b)
        dy = _nt(dr0b, woa_ref[...])
        dp_ref[:, :A_WIDTH] = (dy * s * sg).astype(BF16)
        dp_ref[:, 2 * A_WIDTH:] = (dy * u * s * dsg).astype(BF16)
        ds_ref[...] = dy * u * sg

        lane = lax.broadcasted_iota(jnp.int32, (CHUNK, CHUNK), 1)
        dbsp = jnp.zeros((CHUNK, CHUNK), F32)
        for g in range(A_GROUPS):
            w = _masked_spatial(ws_ref, g)
            cols = slice(g * A_GROUP_DIM, (g + 1) * A_GROUP_DIM)
            dws = jnp.zeros((CHUNK, CHUNK), F32)
            for ci in range(n_chunks):
                rows = slice(ci * CHUNK, (ci + 1) * CHUNK)
                ds = ds_ref[rows, cols]
                dsb = ds.astype(BF16)
                dvn_ref[rows, cols] = _tn(w, dsb)
                dws = dws + _nt(dsb, vn[rows, cols])
                dbsp = dbsp + jnp.where(lane == g, jnp.sum(ds, axis=-1, keepdims=True), 0.0)
            dws_ref[g] += dws
        dbsp_ref[...] += dbsp

        dvn = dvn_ref[...]
        dln_ref[0:1, :] += jnp.sum(dvn * vhat, axis=0, keepdims=True)
        dln_ref[1:2, :] += jnp.sum(dvn, axis=0, keepdims=True)
        dp_ref[:, A_WIDTH:2 * A_WIDTH] = _ln_backward(dvn * ln_g, vhat, rstd_v).astype(BF16)

        @pl.when(step == n_blocks - 1)
        def _():
            t = lax.broadcasted_iota(jnp.int32, (CHUNK, CHUNK), 0)
            for g in range(A_GROUPS):
                dws_ref[g] = jnp.where(lane <= t, dws_ref[g], 0.0)
            pltpu.sync_copy(dwoa_acc, dwoa_out)

    def fixed(shape):
        nd = len(shape)
        return pl.BlockSpec(shape, lambda i: (0,) * nd)

    return pl.pallas_call(
        body, name="bwd_a", grid=(n_blocks,),
        out_shape=[jax.ShapeDtypeStruct((seq, 3 * A_WIDTH), BF16), jax.ShapeDtypeStruct((seq, D_MODEL), F32),
                   jax.ShapeDtypeStruct(woa.shape, F32), jax.ShapeDtypeStruct(w_spatial.shape, F32),
                   jax.ShapeDtypeStruct((CHUNK, CHUNK), F32), jax.ShapeDtypeStruct((2, A_WIDTH), F32),
                   jax.ShapeDtypeStruct((2, D_MODEL), F32)],
        in_specs=[_rows(tm, D_MODEL), _rows(tm, D_MODEL), _rows(tm, 3 * A_WIDTH), _resident(ln_gb.shape),
                  _resident(w_spatial.shape), _resident(bsp_t.shape), _resident(woa.shape),
                  _resident(post_g.shape)],
        out_specs=[_rows(tm, 3 * A_WIDTH), _rows(tm, D_MODEL), ANY, fixed(w_spatial.shape),
                   fixed((CHUNK, CHUNK)), fixed((2, A_WIDTH)), fixed((2, D_MODEL))],
        scratch_shapes=[pltpu.VMEM(woa.shape, F32), pltpu.VMEM((tm, A_WIDTH), F32),
                        pltpu.VMEM((tm, A_WIDTH), F32), pltpu.VMEM((tm, A_WIDTH), F32)],
        compiler_params=_seq_params(),
    )(dh1, r0, p, ln_gb, w_spatial, bsp_t, woa, post_g)


def _input_grad(dp, dr0, wa):
    seq = dr0.shape[0]
    tm = min(2 * TOKENS_PER_STEP, seq)
    shard_cols = wa.shape[2]

    def body(dp_ref, dr0_ref, wa_ref, dx_ref):
        dx = ALPHA * dr0_ref[...]
        for k in range(N_DEV):
            dx = dx + _nt(dp_ref[:, k * shard_cols:(k + 1) * shard_cols], wa_ref[k])
        dx_ref[...] = dx

    return pl.pallas_call(
        body, name="input_grad", grid=(seq // tm,),
        out_shape=jax.ShapeDtypeStruct((seq, D_MODEL), F32),
        in_specs=[_rows(tm, 3 * A_WIDTH), _rows(tm, D_MODEL), _resident(wa.shape)],
        out_specs=_rows(tm, D_MODEL),
        compiler_params=_seq_params(),
    )(dp, dr0, wa)


def _in_proj_weight_grad(xb, dp, shard_cols):
    seq = xb.shape[0]
    tk = min(1024, seq)
    n_k = seq // tk

    def body(x_ref, dp_ref, out_ref):
        @pl.when(pl.program_id(1) == 0)
        def _():
            out_ref[...] = jnp.zeros_like(out_ref)

        out_ref[...] += _tn(x_ref[...], dp_ref[...])

    return pl.pallas_call(
        body, name="in_proj_weight_grad", grid=(N_DEV, n_k),
        out_shape=jax.ShapeDtypeStruct((N_DEV, D_MODEL, shard_cols), F32),
        in_specs=[pl.BlockSpec((tk, D_MODEL), lambda j, k: (k, 0)),
                  pl.BlockSpec((tk, shard_cols), lambda j, k: (k, j))],
        out_specs=pl.BlockSpec((None, D_MODEL, shard_cols), lambda j, k: (j, 0, 0)),
        compiler_params=pltpu.CompilerParams(dimension_semantics=("arbitrary", "arbitrary"),
                                             vmem_limit_bytes=VMEM_LIMIT_BYTES),
    )(xb, dp)


def _pair_exchange(grads):
    n = len(grads)

    def body(*refs):
        ins, landed, own = refs[:n], refs[n:2 * n], refs[2 * n:3 * n]
        send_sems, recv_sems, local_sems = refs[3 * n:]
        x, y, c = lax.axis_index("x"), lax.axis_index("y"), lax.axis_index("c")
        remote, local = [], []
        for a in range(n):
            remote.append(pltpu.make_async_remote_copy(
                src_ref=ins[a].at[:, pl.ds(1 - c, 1)], dst_ref=landed[a],
                send_sem=send_sems.at[a], recv_sem=recv_sems.at[a], device_id=(x, y, 1 - c), device_id_type=MESH))
            local.append(pltpu.make_async_copy(ins[a].at[:, pl.ds(c, 1)], own[a], local_sems.at[a]))
            remote[-1].start()
            local[-1].start()
        for cp in remote + local:
            cp.wait()

    half = [jax.ShapeDtypeStruct((4, 1) + g.shape[2:], F32) for g in grads]
    outs = pl.pallas_call(
        body, name="pair_exchange", out_shape=half + half,
        in_specs=[ANY] * n, out_specs=[ANY] * (2 * n),
        scratch_shapes=[pltpu.SemaphoreType.DMA((n,)), pltpu.SemaphoreType.DMA((n,)), pltpu.SemaphoreType.DMA((n,))],
    )(*grads)
    return outs[:n], outs[n:]


def _row_tile(rows, cols, itemsize=4):
    if rows * cols * itemsize <= (1 << 20):
        return rows
    t = 8
    while 2 * t * cols * itemsize <= (1 << 20) and rows % (2 * t) == 0:
        t *= 2
    assert rows % t == 0, (rows, cols)
    return t


def _pair_sum(own, landed, dtype):
    rows, cols = own.shape
    tr = _row_tile(rows, cols)

    def body(a_ref, b_ref, out_ref):
        out_ref[...] = (a_ref[...] + b_ref[...]).astype(dtype)

    spec = pl.BlockSpec((tr, cols), lambda i: (i, 0))
    return pl.pallas_call(
        body, name="pair_sum", grid=(rows // tr,), out_shape=jax.ShapeDtypeStruct((rows, cols), dtype),
        in_specs=[spec, spec], out_specs=spec,
        compiler_params=pltpu.CompilerParams(dimension_semantics=("parallel",)),
    )(own, landed)


def _chip_exchange(sums, own, landed):
    n = len(sums)

    def body(*refs):
        s_in, own_in, land_in = refs[:n], refs[n:2 * n], refs[2 * n:3 * n]
        got, mine_a, mine_b = refs[3 * n:4 * n], refs[4 * n:5 * n], refs[5 * n:6 * n]
        send_sems, recv_sems, local_sems = refs[6 * n:]
        x, y, c = lax.axis_index("x"), lax.axis_index("y"), lax.axis_index("c")
        chips = [(1 - x, y), (x, 1 - y), (1 - x, 1 - y)]
        copies = []
        for a in range(n):
            for k, (px, py) in enumerate(chips):
                copies.append(pltpu.make_async_remote_copy(
                    src_ref=s_in[a].at[2 * px + py], dst_ref=got[a].at[k],
                    send_sem=send_sems.at[a, k], recv_sem=recv_sems.at[a, k],
                    device_id=(px, py, c), device_id_type=MESH))
                copies[-1].start()
            copies.append(pltpu.make_async_copy(own_in[a].at[2 * x + y], mine_a[a], local_sems.at[a, 0]))
            copies[-1].start()
            copies.append(pltpu.make_async_copy(land_in[a].at[2 * x + y], mine_b[a], local_sems.at[a, 1]))
            copies[-1].start()
        for cp in copies:
            cp.wait()

    got_shape = [jax.ShapeDtypeStruct((3,) + s.shape[1:], s.dtype) for s in sums]
    mine_shape = [jax.ShapeDtypeStruct(s.shape[1:], F32) for s in sums]
    outs = pl.pallas_call(
        body, name="chip_exchange", out_shape=got_shape + mine_shape + mine_shape,
        in_specs=[ANY] * (3 * n), out_specs=[ANY] * (3 * n),
        scratch_shapes=[pltpu.SemaphoreType.DMA((n, 3)), pltpu.SemaphoreType.DMA((n, 3)),
                        pltpu.SemaphoreType.DMA((n, 2))],
    )(*sums, *own, *landed)
    return outs[:n], outs[n:2 * n], outs[2 * n:]


def _adamw_math(w, g, m, v):
    m = ADAM_B1 * m + (1.0 - ADAM_B1) * g
    v = ADAM_B2 * v + (1.0 - ADAM_B2) * (g * g)
    m_hat = m / (1.0 - ADAM_B1 ** ADAM_STEP)
    v_hat = v / (1.0 - ADAM_B2 ** ADAM_STEP)
    delta = -ADAM_LR * (m_hat / (jnp.sqrt(v_hat) + ADAM_EPS) + ADAM_WD * w)
    return delta, m, v


def _adamw_shard(mine_a, mine_b, got, w, m, v):
    rows, cols = w.shape
    tr = _row_tile(rows, cols)

    def body(a_ref, b_ref, got_ref, w_ref, m_ref, v_ref, g_out, d_out, m_out, v_out):
        g = a_ref[...] + b_ref[...]
        for k in range(3):
            g = g + got_ref[k].astype(F32)
        g_out[...] = g
        d_out[...], m_out[...], v_out[...] = _adamw_math(w_ref[...], g, m_ref[...], v_ref[...])

    spec = pl.BlockSpec((tr, cols), lambda i: (i, 0))
    out = jax.ShapeDtypeStruct((rows, cols), F32)
    return pl.pallas_call(
        body, name="adamw_shard", grid=(rows // tr,), out_shape=[out] * 4,
        in_specs=[spec, spec, pl.BlockSpec((3, tr, cols), lambda i: (0, i, 0)), spec, spec, spec],
        out_specs=[spec] * 4,
        compiler_params=pltpu.CompilerParams(dimension_semantics=("parallel",)),
    )(mine_a, mine_b, got, w, m, v)


def _gather_small(mine_a, mine_b, got):
    rows, cols = mine_a.shape

    def body(a_ref, b_ref, got_ref, out_ref, send_sems, recv_sems):
        x, y, c = lax.axis_index("x"), lax.axis_index("y"), lax.axis_index("c")
        me, sibling = (x, y, c), (x, y, 1 - c)
        chips = [(1 - x, y), (x, 1 - y), (1 - x, 1 - y)]

        def slot(px, py, pc):
            return out_ref.at[4 * px + 2 * py + pc]

        def copy(k, block, to):
            return pltpu.make_async_remote_copy(
                src_ref=slot(*block), dst_ref=slot(*block), send_sem=send_sems.at[k], recv_sem=recv_sems.at[k],
                device_id=to, device_id_type=MESH)

        out_ref[4 * x + 2 * y + c] = ((a_ref[...] + b_ref[...]) + got_ref[0]) + got_ref[1] + got_ref[2]
        first = [copy(0, me, sibling)] + [copy(1 + j, me, (*chip, c)) for j, chip in enumerate(chips)]
        for cp in first:
            cp.start()
        passed = [copy(4 + j, (*chip, c), sibling) for j, chip in enumerate(chips)]
        for j, chip in enumerate(chips):
            copy(1 + j, (*chip, c), me).wait_recv()
            passed[j].start()
        copy(0, sibling, me).wait_recv()
        for j, chip in enumerate(chips):
            copy(4 + j, (*chip, 1 - c), me).wait_recv()
        for cp in first + passed:
            cp.wait_send()

    return pl.pallas_call(
        body, name="gather_small", out_shape=jax.ShapeDtypeStruct((N_DEV, rows, cols), F32),
        in_specs=[VMEM] * 3, out_specs=VMEM,
        scratch_shapes=[pltpu.SemaphoreType.DMA((7,)), pltpu.SemaphoreType.DMA((7,))],
    )(mine_a, mine_b, got)


def _adamw_small(params):
    n = len(params)

    def body(*refs):
        ins, outs = refs[:4 * n], refs[4 * n:]
        for i in range(n):
            w_ref, g_ref, m_ref, v_ref = ins[4 * i:4 * i + 4]
            d, m, v = _adamw_math(w_ref[...], g_ref[...], m_ref[...], v_ref[...])
            outs[3 * i][...] = d
            outs[3 * i + 1][...] = m
            outs[3 * i + 2][...] = v

    flat = [a for p in params for a in p]
    out_shape = [jax.ShapeDtypeStruct(p[0].shape, F32) for p in params for _ in range(3)]
    outs = pl.pallas_call(
        body, name="adamw_small", out_shape=out_shape, in_specs=[VMEM] * (4 * n), out_specs=[VMEM] * (3 * n),
    )(*flat)
    return [tuple(outs[3 * i:3 * i + 3]) for i in range(n)]


def _pack_small(parts):
    rows = []
    for a in parts:
        r, c = a.shape
        rows.append(jnp.pad(a, ((0, (-r) % 8), (0, CHUNK - c))))
    flat = jnp.concatenate(rows, axis=0)
    return jnp.pad(flat, ((0, N_DEV * SMALL_ROWS - flat.shape[0]), (0, 0))).reshape(N_DEV, SMALL_ROWS, CHUNK)


def _unpack_small(flat, shapes):
    out, at = [], 0
    for r, c in shapes:
        out.append(flat[at:at + r, :c])
        at += r + (-r) % 8
    return out


def kernel(x, w_in_a, sgu_ln_g, sgu_ln_b, w_spatial, b_spatial, w_out_a, w_kv, w_in_b, attn_sinks, rel_bias, w_out_b, post_ln_g, post_ln_b, loss_target, m_w_in_a, m_sgu_ln_g, m_sgu_ln_b, m_w_spatial, m_b_spatial, m_w_out_a, m_w_kv, m_w_in_b, m_attn_sinks, m_rel_bias, m_w_out_b, m_post_ln_g, m_post_ln_b, v_w_in_a, v_sgu_ln_g, v_sgu_ln_b, v_w_spatial, v_b_spatial, v_w_out_a, v_w_kv, v_w_in_b, v_attn_sinks, v_rel_bias, v_w_out_b, v_post_ln_g, v_post_ln_b):
    seq = x.shape[1]
    me = 4 * lax.axis_index("x") + 2 * lax.axis_index("y") + lax.axis_index("c")
    bucket = jnp.asarray(_band_tables())

    ln_shard = jnp.concatenate([sgu_ln_g, sgu_ln_b], axis=0)
    wa, woa, wkv, wb, wob, ln_all = _gather_weights(w_in_a[0], w_out_a[0], w_kv, w_in_b[0], w_out_b[0], ln_shard)
    woa = woa.reshape(A_WIDTH, D_MODEL)
    wkv = wkv.reshape(D_MODEL, 2 * KV_WIDTH)
    wob = wob.reshape(B_WIDTH, D_MODEL)
    ln_gb = jnp.transpose(ln_all, (1, 0, 2)).reshape(2, A_WIDTH)
    ws = w_spatial[0]
    bsp_t = jnp.pad(b_spatial[0].T, ((0, 0), (0, CHUNK - A_GROUPS)))
    bias = _bias_table(rel_bias, bucket)

    x2, tgt = x[0], loss_target[0]
    p, r0, xb = _fwd_a(x2, wa, ln_gb, ws, bsp_t, woa)
    q, z, kv, dr1, loss, dg1, db1 = _fwd_b(r0, tgt, wb, wkv, wob, post_ln_g, post_ln_b, attn_sinks, bias)
    dh1, dwb, dwkv, dwob, dsink, dlog = _bwd_b(r0, dr1, q, z, kv, wb, wkv, wob, post_ln_g, post_ln_b, attn_sinks,
                                               bias)
    drel = _rel_bias_grad(dlog, bucket)
    dp, dr0, dwoa, dws, dbsp_t, dln, dpost0 = _bwd_a(dh1, r0, p, ln_gb, ws, bsp_t, woa, post_ln_g)
    dx = _input_grad(dp, dr0, wa)
    dwa = _in_proj_weight_grad(xb, dp, wa.shape[2])

    small_shapes = [(A_GROUPS * CHUNK, CHUNK), (A_GROUPS, CHUNK), (1, N_Q_HEADS), (REL_BUCKETS, N_Q_HEADS),
                    (16, CHUNK), (16, CHUNK), (16, CHUNK), (16, CHUNK)]
    small = _pack_small([
        dws.reshape(A_GROUPS * CHUNK, CHUNK), dbsp_t[:, :A_GROUPS].T, dsink[:, :N_Q_HEADS], drel[:, :N_Q_HEADS],
        jnp.concatenate([dpost0[0:1], dg1], axis=0).reshape(16, CHUNK),
        jnp.concatenate([dpost0[1:2], db1], axis=0).reshape(16, CHUNK),
        dln[0].reshape(16, CHUNK), dln[1].reshape(16, CHUNK)])
    full = [dwa, dwoa.reshape(N_DEV, A_WIDTH // N_DEV, D_MODEL), dwkv.reshape(N_DEV, D_MODEL // N_DEV, 2 * KV_WIDTH),
            dwb, dwob.reshape(N_DEV, B_WIDTH // N_DEV, D_MODEL), small]
    wire = [BF16] * 5 + [F32]
    landed, own = _pair_exchange([g.reshape((4, 2) + g.shape[1:]) for g in full])
    landed = [a.reshape((4,) + a.shape[2:]) for a in landed]
    own = [a.reshape((4,) + a.shape[2:]) for a in own]
    sums = []
    for o, l, dt in zip(own, landed, wire):
        flat = (4 * o.shape[1], o.shape[2])
        sums.append(_pair_sum(o.reshape(flat), l.reshape(flat), dt).reshape(o.shape))
    got, mine_a, mine_b = _chip_exchange(sums, own, landed)

    big = [(w_in_a, m_w_in_a, v_w_in_a), (w_out_a, m_w_out_a, v_w_out_a), (w_kv, m_w_kv, v_w_kv),
           (w_in_b, m_w_in_b, v_w_in_b), (w_out_b, m_w_out_b, v_w_out_b)]
    big_out = []
    for i, (w, m, v) in enumerate(big):
        shape2 = mine_a[i].shape
        res = _adamw_shard(mine_a[i], mine_b[i], got[i], w.reshape(shape2), m.reshape(shape2), v.reshape(shape2))
        big_out.append([r.reshape(w.shape) for r in res])

    small_all = _gather_small(mine_a[5], mine_b[5], got[5]).reshape(N_DEV * SMALL_ROWS, CHUNK)
    g_ws, g_bsp, g_sink, g_rel, g_pg, g_pb, g_lng, g_lnb = _unpack_small(small_all, small_shapes)
    g_ws = g_ws.reshape(w_spatial.shape)
    g_bsp = g_bsp.reshape(b_spatial.shape)
    g_pg = g_pg.reshape(post_ln_g.shape)
    g_pb = g_pb.reshape(post_ln_b.shape)
    g_lng = lax.dynamic_slice(g_lng.reshape(1, A_WIDTH), (0, me * sgu_ln_g.shape[1]), sgu_ln_g.shape)
    g_lnb = lax.dynamic_slice(g_lnb.reshape(1, A_WIDTH), (0, me * sgu_ln_b.shape[1]), sgu_ln_b.shape)
    small_params = [(sgu_ln_g, g_lng, m_sgu_ln_g, v_sgu_ln_g), (sgu_ln_b, g_lnb, m_sgu_ln_b, v_sgu_ln_b),
                    (w_spatial, g_ws, m_w_spatial, v_w_spatial), (b_spatial, g_bsp, m_b_spatial, v_b_spatial),
                    (attn_sinks, g_sink, m_attn_sinks, v_attn_sinks), (rel_bias, g_rel, m_rel_bias, v_rel_bias),
                    (post_ln_g, g_pg, m_post_ln_g, v_post_ln_g), (post_ln_b, g_pb, m_post_ln_b, v_post_ln_b)]
    flat2 = [tuple(a.reshape(-1, a.shape[-1]) for a in p) for p in small_params]
    small_out = [tuple(r.reshape(p[0].shape) for r in res) for res, p in zip(_adamw_small(flat2), small_params)]

    loss_all = lax.psum(loss[0, 0], ("x", "y", "c"))
    order = [("big", 0), ("small", 0), ("small", 1), ("small", 2), ("small", 3), ("big", 1), ("big", 2), ("big", 3),
             ("small", 4), ("small", 5), ("big", 4), ("small", 6), ("small", 7)]
    grads, deltas, new_m, new_v = [], [], [], []
    for kind, i in order:
        if kind == "big":
            g, d, m, v = big_out[i]
        else:
            g = small_params[i][1]
            d, m, v = small_out[i]
        grads.append(g)
        deltas.append(d)
        new_m.append(m)
        new_v.append(v)
    return (loss_all, dx.reshape(x.shape), *grads, *deltas, *new_m, *new_v)
```

```python
import functools

import jax
import jax.numpy as jnp
import numpy as np
from jax import lax
from jax.experimental import pallas as pl
from jax.experimental.pallas import tpu as pltpu

F32 = jnp.float32
BF16 = jnp.bfloat16

D_MODEL = 1024
A_WIDTH = 2048
A_GROUPS = 8
A_GROUP_DIM = A_WIDTH // A_GROUPS
CHUNK = 128
HEAD_DIM = 64
N_Q_HEADS = 16
N_KV_HEADS = 2
PAIRS_PER_KV = N_Q_HEADS // N_KV_HEADS // 2
N_PAIRS = N_Q_HEADS // 2
B_WIDTH = N_Q_HEADS * HEAD_DIM
KV_WIDTH = N_KV_HEADS * HEAD_DIM
REL_BUCKETS = 32
REL_MAX_DIST = 128
DEPTH = 2
ALPHA = (2.0 * DEPTH) ** 0.25
LN_EPS = 1e-5
NEG_INF = -1e30
SCALE = HEAD_DIM ** -0.5

ADAM_LR = 0.001
ADAM_B1 = 0.9
ADAM_B2 = 0.999
ADAM_EPS = 1e-08
ADAM_WD = 0.01
ADAM_STEP = 10

N_DEV = 8
MESH = pl.DeviceIdType.MESH
TOKENS_PER_STEP = 256
VMEM_LIMIT_BYTES = 60 * 1024 * 1024
SMALL_ROWS = 144

ANY = pl.BlockSpec(memory_space=pl.ANY)
VMEM = pl.BlockSpec(memory_space=pltpu.VMEM)
SMEM = pl.BlockSpec(memory_space=pltpu.SMEM)


def _nn(a, b):
    return jnp.dot(a, b, preferred_element_type=F32)


def _nt(a, b):
    return lax.dot_general(a, b, (((1,), (1,)), ((), ())), preferred_element_type=F32)


def _tn(a, b):
    return lax.dot_general(a, b, (((0,), (0,)), ((), ())), preferred_element_type=F32)


def _resident(shape):
    nd = len(shape)
    return pl.BlockSpec(shape, lambda i: (0,) * nd, pipeline_mode=pl.Buffered(1))


def _rows(tm, width, reverse_of=None):
    if reverse_of is None:
        return pl.BlockSpec((tm, width), lambda i: (i, 0))
    return pl.BlockSpec((tm, width), lambda i: (reverse_of - 1 - i, 0))


def _seq_params():
    return pltpu.CompilerParams(dimension_semantics=("arbitrary",), vmem_limit_bytes=VMEM_LIMIT_BYTES)


def _ln_normalize(r):
    mu = jnp.mean(r, axis=-1, keepdims=True)
    xc = r - mu
    var = jnp.mean(xc * xc, axis=-1, keepdims=True)
    rstd = lax.rsqrt(var + LN_EPS)
    return xc * rstd, rstd


def _ln_backward(dxhat, xhat, rstd):
    m1 = jnp.mean(dxhat, axis=-1, keepdims=True)
    m2 = jnp.mean(dxhat * xhat, axis=-1, keepdims=True)
    return rstd * (dxhat - m1 - xhat * m2)


def _silu_and_grad(z):
    sig = jax.nn.sigmoid(z)
    return z * sig, sig * (1.0 + z * (1.0 - sig))


def _band_tables():
    t = np.arange(CHUNK, dtype=np.int32)[:, None]
    j = np.arange(2 * CHUNK, dtype=np.int32)[None, :]
    d = t + CHUNK - j
    in_window = (d >= 0) & (d < CHUNK)
    dc = np.clip(d, 0, REL_MAX_DIST - 1)
    max_exact = REL_BUCKETS // 2
    df = np.maximum(dc, 1).astype(np.float32)
    large = max_exact + (np.log(df / np.float32(max_exact)) / np.float32(np.log(REL_MAX_DIST / max_exact))
                         * np.float32(REL_BUCKETS - max_exact)).astype(np.int32)
    large = np.minimum(large, REL_BUCKETS - 1)
    bucket = np.where(dc < max_exact, dc, large)
    return np.where(in_window, bucket, -1).astype(np.int32)


def _gather_weights(w_in_a, w_out_a, w_kv, w_in_b, w_out_b, ln_gb):
    shards = (w_in_a, w_out_a, w_kv, w_in_b, w_out_b)
    n = len(shards) + 1

    def body(*refs):
        ins, outs = refs[:n], refs[n:2 * n]
        stage = refs[2 * n:3 * n - 1]
        send_sems, recv_sems, local_sems = refs[3 * n - 1:]
        x, y, c = lax.axis_index("x"), lax.axis_index("y"), lax.axis_index("c")
        me, sibling = (x, y, c), (x, y, 1 - c)
        chips = [(1 - x, y), (x, 1 - y), (1 - x, 1 - y)]
        srcs = []
        for i in range(n - 1):
            stage[i][...] = ins[i][...].astype(BF16)
            srcs.append(stage[i])
        srcs.append(ins[n - 1])

        def slot(a, px, py, pc):
            return outs[a].at[4 * px + 2 * py + pc]

        def copy(a, k, block, to, src=None):
            return pltpu.make_async_remote_copy(
                src_ref=slot(a, *block) if src is None else src, dst_ref=slot(a, *block),
                send_sem=send_sems.at[a, k], recv_sem=recv_sems.at[a, k], device_id=to, device_id_type=MESH)

        mine, first, passed = [], [], []
        for a in range(n):
            mine.append(pltpu.make_async_copy(srcs[a], slot(a, *me), local_sems.at[a]))
            mine[-1].start()
            first.append(copy(a, 0, me, sibling, srcs[a]))
            for j, chip in enumerate(chips):
                first.append(copy(a, 1 + j, me, (*chip, c), srcs[a]))
        for cp in first:
            cp.start()
        for j, chip in enumerate(chips):
            for a in range(n):
                copy(a, 1 + j, (*chip, c), me).wait_recv()
                passed.append(copy(a, 4 + j, (*chip, c), sibling))
                passed[-1].start()
        for a in range(n):
            copy(a, 0, sibling, me).wait_recv()
            for j, chip in enumerate(chips):
                copy(a, 4 + j, (*chip, 1 - c), me).wait_recv()
        for cp in first + passed:
            cp.wait_send()
        for cp in mine:
            cp.wait()

    ins = shards + (ln_gb,)
    out_shape = [jax.ShapeDtypeStruct((N_DEV,) + s.shape, BF16) for s in shards]
    out_shape.append(jax.ShapeDtypeStruct((N_DEV,) + ln_gb.shape, F32))
    return pl.pallas_call(
        body, name="gather_weights", out_shape=out_shape,
        in_specs=[VMEM] * n, out_specs=[ANY] * n,
        scratch_shapes=[pltpu.VMEM(s.shape, BF16) for s in shards] + [
            pltpu.SemaphoreType.DMA((n, 7)), pltpu.SemaphoreType.DMA((n, 7)), pltpu.SemaphoreType.DMA((n,))],
        compiler_params=pltpu.CompilerParams(vmem_limit_bytes=VMEM_LIMIT_BYTES),
    )(*ins)


def _bias_table(rel_bias, bucket):
    def body(rel_ref, bucket_ref, out_ref):
        bk = bucket_ref[...]
        col = lax.broadcasted_iota(jnp.int32, bk.shape, 1)
        for h in range(N_Q_HEADS):
            def step(b, acc):
                return jnp.where(bk == b, rel_ref[b, h], acc)
            tab = lax.fori_loop(0, REL_BUCKETS, step, jnp.full(bk.shape, NEG_INF, F32))
            half = slice((h % 2) * 2 * CHUNK, (h % 2 + 1) * 2 * CHUNK)
            out_ref[1, h // 2, :, half] = tab
            out_ref[0, h // 2, :, half] = jnp.where(col < CHUNK, NEG_INF, tab)

    return pl.pallas_call(
        body, name="bias_table", out_shape=jax.ShapeDtypeStruct((2, N_PAIRS, CHUNK, 4 * CHUNK), F32),
        in_specs=[SMEM, VMEM], out_specs=VMEM,
    )(rel_bias, bucket)


def _rel_bias_grad(dlogits, bucket):
    def body(dl_ref, bucket_ref, out_ref):
        bk = bucket_ref[...]
        row = lax.broadcasted_iota(jnp.int32, (REL_BUCKETS, CHUNK), 0)
        lane = lax.broadcasted_iota(jnp.int32, (REL_BUCKETS, CHUNK), 1)

        def step(b, acc):
            for h in range(N_Q_HEADS):
                dl = dl_ref[h // 2, :, (h % 2) * 2 * CHUNK:(h % 2 + 1) * 2 * CHUNK]
                s = jnp.sum(jnp.where(bk == b, dl, 0.0))
                acc = jnp.where((row == b) & (lane == h), s, acc)
            return acc

        out_ref[...] = lax.fori_loop(0, REL_BUCKETS, step, jnp.zeros((REL_BUCKETS, CHUNK), F32))

    return pl.pallas_call(
        body, name="rel_bias_grad", out_shape=jax.ShapeDtypeStruct((REL_BUCKETS, CHUNK), F32),
        in_specs=[VMEM, VMEM], out_specs=VMEM,
    )(dlogits, bucket)


def _masked_spatial(ws_ref, g):
    t = lax.broadcasted_iota(jnp.int32, (CHUNK, CHUNK), 0)
    s = lax.broadcasted_iota(jnp.int32, (CHUNK, CHUNK), 1)
    return jnp.where(s <= t, ws_ref[g], 0.0).astype(BF16)


def _spatial_gate(ws_ref, bsp_t_ref, vn, s_ref, n_chunks):
    for g in range(A_GROUPS):
        w = _masked_spatial(ws_ref, g)
        cols = slice(g * A_GROUP_DIM, (g + 1) * A_GROUP_DIM)
        for ci in range(n_chunks):
            rows = slice(ci * CHUNK, (ci + 1) * CHUNK)
            s_ref[rows, cols] = _nn(w, vn[rows, cols]) + bsp_t_ref[:, g:g + 1]


def _fwd_a(x, wa, ln_gb, w_spatial, bsp_t, woa):
    seq = x.shape[0]
    tm = min(TOKENS_PER_STEP, seq)
    n_chunks = tm // CHUNK
    shard_cols = wa.shape[2]

    def body(x_ref, wa_ref, ln_ref, ws_ref, bsp_ref, woa_ref, p_ref, r0_ref, xb_ref, s_ref):
        xv = x_ref[...]
        xb = xv.astype(BF16)
        xb_ref[...] = xb
        for k in range(N_DEV):
            p_ref[:, k * shard_cols:(k + 1) * shard_cols] = _nn(xb, wa_ref[k])
        vhat, _ = _ln_normalize(p_ref[:, A_WIDTH:2 * A_WIDTH])
        vn = (vhat * ln_ref[0:1, :] + ln_ref[1:2, :]).astype(BF16)
        _spatial_gate(ws_ref, bsp_ref, vn, s_ref, n_chunks)
        z = p_ref[:, 2 * A_WIDTH:]
        y = p_ref[:, :A_WIDTH] * s_ref[...] * (z * jax.nn.sigmoid(z))
        r0_ref[...] = ALPHA * xv + _nn(y.astype(BF16), woa_ref[...])

    return pl.pallas_call(
        body, name="fwd_a", grid=(seq // tm,),
        out_shape=[jax.ShapeDtypeStruct((seq, 3 * A_WIDTH), F32), jax.ShapeDtypeStruct((seq, D_MODEL), F32),
                   jax.ShapeDtypeStruct((seq, D_MODEL), BF16)],
        in_specs=[_rows(tm, D_MODEL), _resident(wa.shape), _resident(ln_gb.shape), _resident(w_spatial.shape),
                  _resident(bsp_t.shape), _resident(woa.shape)],
        out_specs=[_rows(tm, 3 * A_WIDTH), _rows(tm, D_MODEL), _rows(tm, D_MODEL)],
        scratch_shapes=[pltpu.VMEM((tm, A_WIDTH), F32)],
        compiler_params=_seq_params(),
    )(x, wa, ln_gb, w_spatial, bsp_t, woa)


def _expand_heads(kv):
    lo = lax.broadcasted_iota(jnp.int32, kv.shape, 1) < HEAD_DIM
    swapped = pltpu.roll(kv, HEAD_DIM, 1)
    zero = jnp.zeros_like(kv)
    return (jnp.where(lo, kv, zero).astype(BF16), jnp.where(lo, zero, swapped).astype(BF16),
            jnp.where(lo, swapped, zero).astype(BF16), jnp.where(lo, zero, kv).astype(BF16))


def _fold_heads(acc, kvh):
    top, bot = acc[:2 * CHUNK], acc[2 * CHUNK:]
    lo = lax.broadcasted_iota(jnp.int32, top.shape, 1) < HEAD_DIM
    t = jnp.where(lo, top, 0.0)
    b = jnp.where(lo, 0.0, bot)
    if kvh == 0:
        return t + pltpu.roll(b, HEAD_DIM, 1)
    return pltpu.roll(t, HEAD_DIM, 1) + b


def _softmax_pair(logits, sink_ref, pair):
    ps, sinks = [], []
    for half in range(2):
        lg = logits[:, half * 2 * CHUNK:(half + 1) * 2 * CHUNK]
        sk = sink_ref[0, 2 * pair + half]
        m = jnp.maximum(jnp.max(lg, axis=-1, keepdims=True), sk)
        e = jnp.exp(lg - m)
        es = jnp.exp(sk - m)
        inv = 1.0 / (jnp.sum(e, axis=-1, keepdims=True) + es)
        ps.append(e * inv)
        sinks.append(es * inv)
    return jnp.concatenate(ps, axis=1), sinks


def _fwd_b(r0, target, wb, wkv, wob, post_g, post_b, sinks, bias):
    seq = r0.shape[0]
    tm = min(TOKENS_PER_STEP, seq)
    n_chunks = tm // CHUNK
    shard_cols = wb.shape[2]

    def body(r0_ref, tgt_ref, wb_ref, wkv_ref, wob_ref, pg_ref, pb_ref, sink_ref, bias_ref,
             q_ref, z_ref, kv_ref, dr1_ref, loss_ref, dg_ref, db_ref, ek_ref, ev_ref, o_ref):
        step = pl.program_id(0)

        @pl.when(step == 0)
        def _():
            ek_ref[0] = jnp.zeros(ek_ref.shape[1:], BF16)
            ev_ref[0] = jnp.zeros(ev_ref.shape[1:], BF16)
            loss_ref[...] = jnp.zeros_like(loss_ref)
            dg_ref[...] = jnp.zeros_like(dg_ref)
            db_ref[...] = jnp.zeros_like(db_ref)

        @pl.when(step > 0)
        def _():
            ek_ref[0] = ek_ref[n_chunks]
            ev_ref[0] = ev_ref[n_chunks]

        xhat0, _ = _ln_normalize(r0_ref[...])
        h1 = xhat0 * pg_ref[0:1, :] + pb_ref[0:1, :]
        h1b = h1.astype(BF16)
        for k in range(N_DEV):
            qz = _nn(h1b, wb_ref[k])
            lo, hi = k * shard_cols, (k + 1) * shard_cols
            if hi <= B_WIDTH:
                q_ref[:, lo:hi] = qz.astype(BF16)
            else:
                z_ref[:, lo - B_WIDTH:hi - B_WIDTH] = qz
        kv = _nn(h1b, wkv_ref[...])
        kv_ref[...] = kv.astype(BF16)
        for ci in range(n_chunks):
            rows = slice(ci * CHUNK, (ci + 1) * CHUNK)
            for i, e in enumerate(_expand_heads(kv[rows, :KV_WIDTH])):
                ek_ref[ci + 1, i] = e
            for i, e in enumerate(_expand_heads(kv[rows, KV_WIDTH:])):
                ev_ref[ci + 1, i] = e

        for ci in range(n_chunks):
            rows = slice(ci * CHUNK, (ci + 1) * CHUNK)
            sel = jnp.where(step * n_chunks + ci > 0, 1, 0)
            for kvh in range(N_KV_HEADS):
                kblk = jnp.concatenate([ek_ref[ci, 2 * kvh], ek_ref[ci + 1, 2 * kvh],
                                        ek_ref[ci, 2 * kvh + 1], ek_ref[ci + 1, 2 * kvh + 1]], axis=0)
                vblk = jnp.concatenate([ev_ref[ci, 2 * kvh], ev_ref[ci + 1, 2 * kvh],
                                        ev_ref[ci, 2 * kvh + 1], ev_ref[ci + 1, 2 * kvh + 1]], axis=0)
                for pp in range(PAIRS_PER_KV):
                    pair = kvh * PAIRS_PER_KV + pp
                    cols = slice(pair * CHUNK, (pair + 1) * CHUNK)
                    logits = _nt(q_ref[rows, cols], kblk) * SCALE + bias_ref[sel, pair]
                    p, _ = _softmax_pair(logits, sink_ref, pair)
                    o_ref[rows, cols] = _nn(p.astype(BF16), vblk)

        z = z_ref[...]
        y = o_ref[...] * (z * jax.nn.sigmoid(z))
        r1 = ALPHA * h1 + _nn(y.astype(BF16), wob_ref[...])
        xhat1, rstd1 = _ln_normalize(r1)
        g1 = pg_ref[1:2, :]
        diff = xhat1 * g1 + pb_ref[1:2, :] - tgt_ref[...]
        loss_ref[...] += 0.5 * jnp.sum(jnp.mean(diff * diff, axis=-1, keepdims=True), axis=0, keepdims=True)
        dh2 = diff * (1.0 / D_MODEL)
        dg_ref[...] += jnp.sum(dh2 * xhat1, axis=0, keepdims=True)
        db_ref[...] += jnp.sum(dh2, axis=0, keepdims=True)
        dr1_ref[...] = _ln_backward(dh2 * g1, xhat1, rstd1)

    vec = pl.BlockSpec((1, D_MODEL), lambda i: (0, 0))
    return pl.pallas_call(
        body, name="fwd_b", grid=(seq // tm,),
        out_shape=[jax.ShapeDtypeStruct((seq, B_WIDTH), BF16), jax.ShapeDtypeStruct((seq, B_WIDTH), F32),
                   jax.ShapeDtypeStruct((seq, 2 * KV_WIDTH), BF16), jax.ShapeDtypeStruct((seq, D_MODEL), F32),
                   jax.ShapeDtypeStruct((1, 1), F32), jax.ShapeDtypeStruct((1, D_MODEL), F32),
                   jax.ShapeDtypeStruct((1, D_MODEL), F32)],
        in_specs=[_rows(tm, D_MODEL), _rows(tm, D_MODEL), _resident(wb.shape), _resident(wkv.shape),
                  _resident(wob.shape), _resident(post_g.shape), _resident(post_b.shape), SMEM,
                  _resident(bias.shape)],
        out_specs=[_rows(tm, B_WIDTH), _rows(tm, B_WIDTH), _rows(tm, 2 * KV_WIDTH), _rows(tm, D_MODEL),
                   pl.BlockSpec((1, 1), lambda i: (0, 0)), vec, vec],
        scratch_shapes=[pltpu.VMEM((n_chunks + 1, 4, CHUNK, CHUNK), BF16),
                        pltpu.VMEM((n_chunks + 1, 4, CHUNK, CHUNK), BF16),
                        pltpu.VMEM((tm, B_WIDTH), F32)],
        compiler_params=_seq_params(),
    )(r0, target, wb, wkv, wob, post_g, post_b, sinks, bias)


def _bwd_b(r0, dr1, q, z, kv, wb, wkv, wob, post_g, post_b, sinks, bias):
    seq = r0.shape[0]
    tm = min(TOKENS_PER_STEP, seq)
    n_chunks = tm // CHUNK
    n_blocks = seq // tm
    shard_cols = wb.shape[2]

    def body(r0_ref, dr1_ref, q_ref, z_ref, kv_ref, kvp_ref, wb_ref, wkv_ref, wob_ref, pg_ref, pb_ref, sink_ref,
             bias_ref, dh1_ref, dwb_out, dwkv_out, dwob_out, dsink_ref, dl_out,
             dwb_acc, dwkv_acc, dwob_acc, dl_acc, carry_ref, ek_ref, ev_ref, y_ref, dqz_ref, dkv_ref):
        step = pl.program_id(0)
        block = n_blocks - 1 - step

        @pl.when(step == 0)
        def _():
            dwb_acc[...] = jnp.zeros_like(dwb_acc)
            dwkv_acc[...] = jnp.zeros_like(dwkv_acc)
            dwob_acc[...] = jnp.zeros_like(dwob_acc)
            dl_acc[...] = jnp.zeros_like(dl_acc)
            carry_ref[...] = jnp.zeros_like(carry_ref)
            dsink_ref[...] = jnp.zeros_like(dsink_ref)

        xhat0, _ = _ln_normalize(r0_ref[...])
        h1b = (xhat0 * pg_ref[0:1, :] + pb_ref[0:1, :]).astype(BF16)
        dr1 = dr1_ref[...]
        dr1b = dr1.astype(BF16)
        dy = _nt(dr1b, wob_ref[...])
        sg, dsg = _silu_and_grad(z_ref[...])
        do_all = dy * sg
        dyz = dy * dsg

        for ci in range(n_chunks + 1):
            src = kvp_ref[...] if ci == 0 else kv_ref[(ci - 1) * CHUNK:ci * CHUNK, :]
            src = src.astype(F32)
            for i, e in enumerate(_expand_heads(src[:, :KV_WIDTH])):
                ek_ref[ci, i] = e
            for i, e in enumerate(_expand_heads(src[:, KV_WIDTH:])):
                ev_ref[ci, i] = e

        lane = lax.broadcasted_iota(jnp.int32, (1, CHUNK), 1)
        dsink = jnp.zeros((1, CHUNK), F32)
        for ci in reversed(range(n_chunks)):
            rows = slice(ci * CHUNK, (ci + 1) * CHUNK)
            sel = jnp.where(block * n_chunks + ci > 0, 1, 0)
            dk_band = jnp.zeros((2 * CHUNK, KV_WIDTH), F32)
            dv_band = jnp.zeros((2 * CHUNK, KV_WIDTH), F32)
            for kvh in range(N_KV_HEADS):
                kblk = jnp.concatenate([ek_ref[ci, 2 * kvh], ek_ref[ci + 1, 2 * kvh],
                                        ek_ref[ci, 2 * kvh + 1], ek_ref[ci + 1, 2 * kvh + 1]], axis=0)
                vblk = jnp.concatenate([ev_ref[ci, 2 * kvh], ev_ref[ci + 1, 2 * kvh],
                                        ev_ref[ci, 2 * kvh + 1], ev_ref[ci + 1, 2 * kvh + 1]], axis=0)
                dk_acc = jnp.zeros((4 * CHUNK, CHUNK), F32)
                dv_acc = jnp.zeros((4 * CHUNK, CHUNK), F32)
                for pp in range(PAIRS_PER_KV):
                    pair = kvh * PAIRS_PER_KV + pp
                    cols = slice(pair * CHUNK, (pair + 1) * CHUNK)
                    qp = q_ref[rows, cols]
                    logits = _nt(qp, kblk) * SCALE + bias_ref[sel, pair]
                    p, psink = _softmax_pair(logits, sink_ref, pair)
                    pb16 = p.astype(BF16)
                    o = _nn(pb16, vblk)
                    y_ref[rows, cols] = (o * sg[rows, cols]).astype(BF16)
                    dqz_ref[rows, B_WIDTH + pair * CHUNK:B_WIDTH + (pair + 1) * CHUNK] = (
                        o * dyz[rows, cols]).astype(BF16)
                    dob = do_all[rows, cols].astype(BF16)
                    dp = _nt(dob, vblk)
                    dlogits = []
                    for half in range(2):
                        hs = slice(half * 2 * CHUNK, (half + 1) * 2 * CHUNK)
                        delta = jnp.sum(dp[:, hs] * p[:, hs], axis=-1, keepdims=True)
                        dlogits.append(p[:, hs] * (dp[:, hs] - delta))
                        ds = -jnp.sum(psink[half] * delta, axis=0, keepdims=True)
                        dsink = dsink + jnp.where(lane == 2 * pair + half, ds, 0.0)
                    dlog = jnp.concatenate(dlogits, axis=1)
                    dl_acc[pair] += dlog
                    dsc = (dlog * SCALE).astype(BF16)
                    dqz_ref[rows, cols] = _nn(dsc, kblk).astype(BF16)
                    dk_acc = dk_acc + _tn(dsc, qp)
                    dv_acc = dv_acc + _tn(pb16, dob)
                dk_band = dk_band + _fold_heads(dk_acc, kvh)
                dv_band = dv_band + _fold_heads(dv_acc, kvh)
            band = jnp.concatenate([dk_band, dv_band], axis=1)
            dkv_ref[rows, :] = (band[CHUNK:] + carry_ref[...]).astype(BF16)
            carry_ref[...] = band[:CHUNK]
        dsink_ref[...] += dsink

        dqz = dqz_ref[...]
        dkv = dkv_ref[...]
        dh1 = ALPHA * dr1 + _nt(dkv, wkv_ref[...])
        for k in range(N_DEV):
            part = dqz[:, k * shard_cols:(k + 1) * shard_cols]
            dh1 = dh1 + _nt(part, wb_ref[k])
            dwb_acc[k] += _tn(h1b, part)
        dh1_ref[...] = dh1
        dwkv_acc[...] += _tn(h1b, dkv)
        dwob_acc[...] += _tn(y_ref[...], dr1b)

        @pl.when(step == n_blocks - 1)
        def _():
            pltpu.sync_copy(dwb_acc, dwb_out)
            pltpu.sync_copy(dwkv_acc, dwkv_out)
            pltpu.sync_copy(dwob_acc, dwob_out)
            pltpu.sync_copy(dl_acc, dl_out)

    rev = functools.partial(_rows, tm, reverse_of=n_blocks)
    prev_chunk = pl.BlockSpec(
        (CHUNK, 2 * KV_WIDTH), lambda i: (jnp.maximum((n_blocks - 1 - i) * n_chunks - 1, 0), 0))
    return pl.pallas_call(
        body, name="bwd_b", grid=(n_blocks,),
        out_shape=[jax.ShapeDtypeStruct((seq, D_MODEL), F32), jax.ShapeDtypeStruct(wb.shape, F32),
                   jax.ShapeDtypeStruct(wkv.shape, F32), jax.ShapeDtypeStruct(wob.shape, F32),
                   jax.ShapeDtypeStruct((1, CHUNK), F32),
                   jax.ShapeDtypeStruct((N_PAIRS, CHUNK, 4 * CHUNK), F32)],
        in_specs=[rev(D_MODEL), rev(D_MODEL), rev(B_WIDTH), rev(B_WIDTH), rev(2 * KV_WIDTH), prev_chunk,
                  _resident(wb.shape), _resident(wkv.shape), _resident(wob.shape), _resident(post_g.shape),
                  _resident(post_b.shape), SMEM, _resident(bias.shape)],
        out_specs=[rev(D_MODEL), ANY, ANY, ANY, pl.BlockSpec((1, CHUNK), lambda i: (0, 0)), ANY],
        scratch_shapes=[pltpu.VMEM(wb.shape, F32), pltpu.VMEM(wkv.shape, F32), pltpu.VMEM(wob.shape, F32),
                        pltpu.VMEM((N_PAIRS, CHUNK, 4 * CHUNK), F32), pltpu.VMEM((CHUNK, 2 * KV_WIDTH), F32),
                        pltpu.VMEM((n_chunks + 1, 4, CHUNK, CHUNK), BF16),
                        pltpu.VMEM((n_chunks + 1, 4, CHUNK, CHUNK), BF16),
                        pltpu.VMEM((tm, B_WIDTH), BF16), pltpu.VMEM((tm, 2 * B_WIDTH), BF16),
                        pltpu.VMEM((tm, 2 * KV_WIDTH), BF16)],
        compiler_params=_seq_params(),
    )(r0, dr1, q, z, kv, kv, wb, wkv, wob, post_g, post_b, sinks, bias)


def _bwd_a(dh1, r0, p, ln_gb, w_spatial, bsp_t, woa, post_g):
    seq = r0.shape[0]
    tm = min(TOKENS_PER_STEP, seq)
    n_chunks = tm // CHUNK
    n_blocks = seq // tm

    def body(dh1_ref, r0_ref, p_ref, ln_ref, ws_ref, bsp_ref, woa_ref, pg_ref,
             dp_ref, dr0_ref, dwoa_out, dws_ref, dbsp_ref, dln_ref, dpost_ref,
             dwoa_acc, s_ref, ds_ref, dvn_ref):
        step = pl.program_id(0)

        @pl.when(step == 0)
        def _():
            dwoa_acc[...] = jnp.zeros_like(dwoa_acc)
            dws_ref[...] = jnp.zeros_like(dws_ref)
            dbsp_ref[...] = jnp.zeros_like(dbsp_ref)
            dln_ref[...] = jnp.zeros_like(dln_ref)
            dpost_ref[...] = jnp.zeros_like(dpost_ref)

        dh1 = dh1_ref[...]
        xhat0, rstd0 = _ln_normalize(r0_ref[...])
        dpost_ref[0:1, :] += jnp.sum(dh1 * xhat0, axis=0, keepdims=True)
        dpost_ref[1:2, :] += jnp.sum(dh1, axis=0, keepdims=True)
        dr0 = _ln_backward(dh1 * pg_ref[0:1, :], xhat0, rstd0)
        dr0_ref[...] = dr0
        dr0b = dr0.astype(BF16)

        vhat, rstd_v = _ln_normalize(p_ref[:, A_WIDTH:2 * A_WIDTH])
        ln_g = ln_ref[0:1, :]
        vn = (vhat * ln_g + ln_ref[1:2, :]).astype(BF16)
        _spatial_gate(ws_ref, bsp_ref, vn, s_ref, n_chunks)
        u = p_ref[:, :A_WIDTH]
        s = s_ref[...]
        sg, dsg = _silu_and_grad(p_ref[:, 2 * A_WIDTH:])
        dwoa_acc[...] += _tn((u * s * sg).astype(BF16), dr0b)
        dy = _nt(dr0b, woa_ref[...])
        dp_ref[:, :A_WIDTH] = (dy * s * sg).astype(BF16)
        dp_ref[:, 2 * A_WIDTH:] = (dy * u * s * dsg).astype(BF16)
        ds_ref[...] = dy * u * sg

        lane = lax.broadcasted_iota(jnp.int32, (CHUNK, CHUNK), 1)
        dbsp = jnp.zeros((CHUNK, CHUNK), F32)
        for g in range(A_GROUPS):
            w = _masked_spatial(ws_ref, g)
            cols = slice(g * A_GROUP_DIM, (g + 1) * A_GROUP_DIM)
            dws = jnp.zeros((CHUNK, CHUNK), F32)
            for ci in range(n_chunks):
                rows = slice(ci * CHUNK, (ci + 1) * CHUNK)
                ds = ds_ref[rows, cols]
                dsb = ds.astype(BF16)
                dvn_ref[rows, cols] = _tn(w, dsb)
                dws = dws + _nt(dsb, vn[rows, cols])
                dbsp = dbsp + jnp.where(lane == g, jnp.sum(ds, axis=-1, keepdims=True), 0.0)
            dws_ref[g] += dws
        dbsp_ref[...] += dbsp

        dvn = dvn_ref[...]
        dln_ref[0:1, :] += jnp.sum(dvn * vhat, axis=0, keepdims=True)
        dln_ref[1:2, :] += jnp.sum(dvn, axis=0, keepdims=True)
        dp_ref[:, A_WIDTH:2 * A_WIDTH] = _ln_backward(dvn * ln_g, vhat, rstd_v).astype(BF16)

        @pl.when(step == n_blocks - 1)
        def _():
            t = lax.broadcasted_iota(jnp.int32, (CHUNK, CHUNK), 0)
            for g in range(A_GROUPS):
                dws_ref[g] = jnp.where(lane <= t, dws_ref[g], 0.0)
            pltpu.sync_copy(dwoa_acc, dwoa_out)

    def fixed(shape):
        nd = len(shape)
        return pl.BlockSpec(shape, lambda i: (0,) * nd)

    return pl.pallas_call(
        body, name="bwd_a", grid=(n_blocks,),
        out_shape=[jax.ShapeDtypeStruct((seq, 3 * A_WIDTH), BF16), jax.ShapeDtypeStruct((seq, D_MODEL), F32),
                   jax.ShapeDtypeStruct(woa.shape, F32), jax.ShapeDtypeStruct(w_spatial.shape, F32),
                   jax.ShapeDtypeStruct((CHUNK, CHUNK), F32), jax.ShapeDtypeStruct((2, A_WIDTH), F32),
                   jax.ShapeDtypeStruct((2, D_MODEL), F32)],
        in_specs=[_rows(tm, D_MODEL), _rows(tm, D_MODEL), _rows(tm, 3 * A_WIDTH), _resident(ln_gb.shape),
                  _resident(w_spatial.shape), _resident(bsp_t.shape), _resident(woa.shape),
                  _resident(post_g.shape)],
        out_specs=[_rows(tm, 3 * A_WIDTH), _rows(tm, D_MODEL), ANY, fixed(w_spatial.shape),
                   fixed((CHUNK, CHUNK)), fixed((2, A_WIDTH)), fixed((2, D_MODEL))],
        scratch_shapes=[pltpu.VMEM(woa.shape, F32), pltpu.VMEM((tm, A_WIDTH), F32),
                        pltpu.VMEM((tm, A_WIDTH), F32), pltpu.VMEM((tm, A_WIDTH), F32)],
        compiler_params=_seq_params(),
    )(dh1, r0, p, ln_gb, w_spatial, bsp_t, woa, post_g)


def _input_grad(dp, dr0, wa):
    seq = dr0.shape[0]
    tm = min(2 * TOKENS_PER_STEP, seq)
    shard_cols = wa.shape[2]

    def body(dp_ref, dr0_ref, wa_ref, dx_ref):
        dx = ALPHA * dr0_ref[...]
        for k in range(N_DEV):
            dx = dx + _nt(dp_ref[:, k * shard_cols:(k + 1) * shard_cols], wa_ref[k])
        dx_ref[...] = dx

    return pl.pallas_call(
        body, name="input_grad", grid=(seq // tm,),
        out_shape=jax.ShapeDtypeStruct((seq, D_MODEL), F32),
        in_specs=[_rows(tm, 3 * A_WIDTH), _rows(tm, D_MODEL), _resident(wa.shape)],
        out_specs=_rows(tm, D_MODEL),
        compiler_params=_seq_params(),
    )(dp, dr0, wa)


def _in_proj_weight_grad(xb, dp, shard_cols):
    seq = xb.shape[0]
    tk = min(1024, seq)
    n_k = seq // tk

    def body(x_ref, dp_ref, out_ref):
        @pl.when(pl.program_id(1) == 0)
        def _():
            out_ref[...] = jnp.zeros_like(out_ref)

        out_ref[...] += _tn(x_ref[...], dp_ref[...])

    return pl.pallas_call(
        body, name="in_proj_weight_grad", grid=(N_DEV, n_k),
        out_shape=jax.ShapeDtypeStruct((N_DEV, D_MODEL, shard_cols), F32),
        in_specs=[pl.BlockSpec((tk, D_MODEL), lambda j, k: (k, 0)),
                  pl.BlockSpec((tk, shard_cols), lambda j, k: (k, j))],
        out_specs=pl.BlockSpec((None, D_MODEL, shard_cols), lambda j, k: (j, 0, 0)),
        compiler_params=pltpu.CompilerParams(dimension_semantics=("arbitrary", "arbitrary"),
                                             vmem_limit_bytes=VMEM_LIMIT_BYTES),
    )(xb, dp)


def _pair_exchange(grads):
    n = len(grads)

    def body(*refs):
        ins, landed = refs[:n], refs[n:2 * n]
        send_sems, recv_sems = refs[2 * n:]
        x, y, c = lax.axis_index("x"), lax.axis_index("y"), lax.axis_index("c")
        copies = []
        for a in range(n):
            copies.append(pltpu.make_async_remote_copy(
                src_ref=ins[a].at[:, pl.ds(1 - c, 1)], dst_ref=landed[a],
                send_sem=send_sems.at[a], recv_sem=recv_sems.at[a], device_id=(x, y, 1 - c), device_id_type=MESH))
            copies[-1].start()
        for cp in copies:
            cp.wait()

    return pl.pallas_call(
        body, name="pair_exchange", out_shape=[jax.ShapeDtypeStruct((4, 1) + g.shape[2:], F32) for g in grads],
        in_specs=[ANY] * n, out_specs=[ANY] * n,
        scratch_shapes=[pltpu.SemaphoreType.DMA((n,)), pltpu.SemaphoreType.DMA((n,))],
    )(*grads)


def _row_tile(rows, cols, itemsize=4):
    if rows * cols * itemsize <= (1 << 20):
        return rows
    t = 8
    while 2 * t * cols * itemsize <= (1 << 20) and rows % (2 * t) == 0:
        t *= 2
    assert rows % t == 0, (rows, cols)
    return t


def _pair_sum(place, grad, landed, dtype):
    _, _, rows, cols = grad.shape
    tr = _row_tile(rows, cols)

    def body(place_ref, a_ref, b_ref, out_ref):
        out_ref[...] = (a_ref[...] + b_ref[...]).astype(dtype)

    return pl.pallas_call(
        body, name="pair_sum", out_shape=jax.ShapeDtypeStruct((4, rows, cols), dtype),
        grid_spec=pltpu.PrefetchScalarGridSpec(
            num_scalar_prefetch=1, grid=(4, rows // tr),
            in_specs=[pl.BlockSpec((None, None, tr, cols), lambda k, i, pr: (k, pr[1], i, 0)),
                      pl.BlockSpec((None, None, tr, cols), lambda k, i, pr: (k, 0, i, 0))],
            out_specs=pl.BlockSpec((None, tr, cols), lambda k, i, pr: (k, i, 0))),
        compiler_params=pltpu.CompilerParams(dimension_semantics=("parallel", "parallel")),
    )(place, grad, landed)


def _chip_exchange(sums):
    n = len(sums)

    def body(*refs):
        s_in, got = refs[:n], refs[n:2 * n]
        send_sems, recv_sems = refs[2 * n:]
        x, y, c = lax.axis_index("x"), lax.axis_index("y"), lax.axis_index("c")
        chips = [(1 - x, y), (x, 1 - y), (1 - x, 1 - y)]
        copies = []
        for a in range(n):
            for k, (px, py) in enumerate(chips):
                copies.append(pltpu.make_async_remote_copy(
                    src_ref=s_in[a].at[2 * px + py], dst_ref=got[a].at[k],
                    send_sem=send_sems.at[a, k], recv_sem=recv_sems.at[a, k],
                    device_id=(px, py, c), device_id_type=MESH))
                copies[-1].start()
        for cp in copies:
            cp.wait()

    return pl.pallas_call(
        body, name="chip_exchange", out_shape=[jax.ShapeDtypeStruct((3,) + s.shape[1:], s.dtype) for s in sums],
        in_specs=[ANY] * n, out_specs=[ANY] * n,
        scratch_shapes=[pltpu.SemaphoreType.DMA((n, 3)), pltpu.SemaphoreType.DMA((n, 3))],
    )(*sums)


def _adamw_math(w, g, m, v):
    m = ADAM_B1 * m + (1.0 - ADAM_B1) * g
    v = ADAM_B2 * v + (1.0 - ADAM_B2) * (g * g)
    m_hat = m / (1.0 - ADAM_B1 ** ADAM_STEP)
    v_hat = v / (1.0 - ADAM_B2 ** ADAM_STEP)
    delta = -ADAM_LR * (m_hat / (jnp.sqrt(v_hat) + ADAM_EPS) + ADAM_WD * w)
    return delta, m, v


def _adamw_shard(place, grad, landed, got, w, m, v):
    rows, cols = w.shape
    tr = _row_tile(rows, cols)

    def body(place_ref, a_ref, b_ref, got_ref, w_ref, m_ref, v_ref, g_out, d_out, m_out, v_out):
        g = a_ref[...] + b_ref[...]
        for k in range(3):
            g = g + got_ref[k].astype(F32)
        g_out[...] = g
        d_out[...], m_out[...], v_out[...] = _adamw_math(w_ref[...], g, m_ref[...], v_ref[...])

    spec = pl.BlockSpec((tr, cols), lambda i, pr: (i, 0))
    out = jax.ShapeDtypeStruct((rows, cols), F32)
    return pl.pallas_call(
        body, name="adamw_shard", out_shape=[out] * 4,
        grid_spec=pltpu.PrefetchScalarGridSpec(
            num_scalar_prefetch=1, grid=(rows // tr,),
            in_specs=[pl.BlockSpec((None, None, tr, cols), lambda i, pr: (pr[0], pr[1], i, 0)),
                      pl.BlockSpec((None, None, tr, cols), lambda i, pr: (pr[0], 0, i, 0)),
                      pl.BlockSpec((3, tr, cols), lambda i, pr: (0, i, 0)), spec, spec, spec],
            out_specs=[spec] * 4),
        compiler_params=pltpu.CompilerParams(dimension_semantics=("parallel",)),
    )(place, grad, landed, got, w, m, v)


def _gather_small(grad, landed, got):
    rows, cols = got.shape[1:]

    def body(grad_ref, landed_ref, got_ref, out_ref, send_sems, recv_sems):
        x, y, c = lax.axis_index("x"), lax.axis_index("y"), lax.axis_index("c")
        me, sibling = (x, y, c), (x, y, 1 - c)
        chips = [(1 - x, y), (x, 1 - y), (1 - x, 1 - y)]

        def slot(px, py, pc):
            return out_ref.at[4 * px + 2 * py + pc]

        def copy(k, block, to):
            return pltpu.make_async_remote_copy(
                src_ref=slot(*block), dst_ref=slot(*block), send_sem=send_sems.at[k], recv_sem=recv_sems.at[k],
                device_id=to, device_id_type=MESH)

        mine = grad_ref[2 * x + y, c] + landed_ref[2 * x + y, 0]
        out_ref[4 * x + 2 * y + c] = ((mine + got_ref[0]) + got_ref[1]) + got_ref[2]
        first = [copy(0, me, sibling)] + [copy(1 + j, me, (*chip, c)) for j, chip in enumerate(chips)]
        for cp in first:
            cp.start()
        passed = [copy(4 + j, (*chip, c), sibling) for j, chip in enumerate(chips)]
        for j, chip in enumerate(chips):
            copy(1 + j, (*chip, c), me).wait_recv()
            passed[j].start()
        copy(0, sibling, me).wait_recv()
        for j, chip in enumerate(chips):
            copy(4 + j, (*chip, 1 - c), me).wait_recv()
        for cp in first + passed:
            cp.wait_send()

    return pl.pallas_call(
        body, name="gather_small", out_shape=jax.ShapeDtypeStruct((N_DEV, rows, cols), F32),
        in_specs=[VMEM] * 3, out_specs=VMEM,
        scratch_shapes=[pltpu.SemaphoreType.DMA((7,)), pltpu.SemaphoreType.DMA((7,))],
    )(grad, landed, got)


def _adamw_small(params):
    n = len(params)

    def body(*refs):
        ins, outs = refs[:4 * n], refs[4 * n:]
        for i in range(n):
            w_ref, g_ref, m_ref, v_ref = ins[4 * i:4 * i + 4]
            d, m, v = _adamw_math(w_ref[...], g_ref[...], m_ref[...], v_ref[...])
            outs[3 * i][...] = d
            outs[3 * i + 1][...] = m
            outs[3 * i + 2][...] = v

    flat = [a for p in params for a in p]
    out_shape = [jax.ShapeDtypeStruct(p[0].shape, F32) for p in params for _ in range(3)]
    outs = pl.pallas_call(
        body, name="adamw_small", out_shape=out_shape, in_specs=[VMEM] * (4 * n), out_specs=[VMEM] * (3 * n),
    )(*flat)
    return [tuple(outs[3 * i:3 * i + 3]) for i in range(n)]


def _pack_small(parts):
    rows = []
    for a in parts:
        r, c = a.shape
        rows.append(jnp.pad(a, ((0, (-r) % 8), (0, CHUNK - c))))
    flat = jnp.concatenate(rows, axis=0)
    return jnp.pad(flat, ((0, N_DEV * SMALL_ROWS - flat.shape[0]), (0, 0))).reshape(N_DEV, SMALL_ROWS, CHUNK)


def _unpack_small(flat, shapes):
    out, at = [], 0
    for r, c in shapes:
        out.append(flat[at:at + r, :c])
        at += r + (-r) % 8
    return out


def kernel(x, w_in_a, sgu_ln_g, sgu_ln_b, w_spatial, b_spatial, w_out_a, w_kv, w_in_b, attn_sinks, rel_bias, w_out_b, post_ln_g, post_ln_b, loss_target, m_w_in_a, m_sgu_ln_g, m_sgu_ln_b, m_w_spatial, m_b_spatial, m_w_out_a, m_w_kv, m_w_in_b, m_attn_sinks, m_rel_bias, m_w_out_b, m_post_ln_g, m_post_ln_b, v_w_in_a, v_sgu_ln_g, v_sgu_ln_b, v_w_spatial, v_b_spatial, v_w_out_a, v_w_kv, v_w_in_b, v_attn_sinks, v_rel_bias, v_w_out_b, v_post_ln_g, v_post_ln_b):
    seq = x.shape[1]
    me = 4 * lax.axis_index("x") + 2 * lax.axis_index("y") + lax.axis_index("c")
    bucket = jnp.asarray(_band_tables())

    ln_shard = jnp.concatenate([sgu_ln_g, sgu_ln_b], axis=0)
    wa, woa, wkv, wb, wob, ln_all = _gather_weights(w_in_a[0], w_out_a[0], w_kv, w_in_b[0], w_out_b[0], ln_shard)
    woa = woa.reshape(A_WIDTH, D_MODEL)
    wkv = wkv.reshape(D_MODEL, 2 * KV_WIDTH)
    wob = wob.reshape(B_WIDTH, D_MODEL)
    ln_gb = jnp.transpose(ln_all, (1, 0, 2)).reshape(2, A_WIDTH)
    ws = w_spatial[0]
    bsp_t = jnp.pad(b_spatial[0].T, ((0, 0), (0, CHUNK - A_GROUPS)))
    bias = _bias_table(rel_bias, bucket)

    x2, tgt = x[0], loss_target[0]
    p, r0, xb = _fwd_a(x2, wa, ln_gb, ws, bsp_t, woa)
    q, z, kv, dr1, loss, dg1, db1 = _fwd_b(r0, tgt, wb, wkv, wob, post_ln_g, post_ln_b, attn_sinks, bias)
    dh1, dwb, dwkv, dwob, dsink, dlog = _bwd_b(r0, dr1, q, z, kv, wb, wkv, wob, post_ln_g, post_ln_b, attn_sinks,
                                               bias)
    drel = _rel_bias_grad(dlog, bucket)
    dp, dr0, dwoa, dws, dbsp_t, dln, dpost0 = _bwd_a(dh1, r0, p, ln_gb, ws, bsp_t, woa, post_ln_g)
    dx = _input_grad(dp, dr0, wa)
    dwa = _in_proj_weight_grad(xb, dp, wa.shape[2])

    small_shapes = [(A_GROUPS * CHUNK, CHUNK), (A_GROUPS, CHUNK), (1, N_Q_HEADS), (REL_BUCKETS, N_Q_HEADS),
                    (16, CHUNK), (16, CHUNK), (16, CHUNK), (16, CHUNK)]
    small = _pack_small([
        dws.reshape(A_GROUPS * CHUNK, CHUNK), dbsp_t[:, :A_GROUPS].T, dsink[:, :N_Q_HEADS], drel[:, :N_Q_HEADS],
        jnp.concatenate([dpost0[0:1], dg1], axis=0).reshape(16, CHUNK),
        jnp.concatenate([dpost0[1:2], db1], axis=0).reshape(16, CHUNK),
        dln[0].reshape(16, CHUNK), dln[1].reshape(16, CHUNK)])
    full = [dwa, dwoa.reshape(N_DEV, A_WIDTH // N_DEV, D_MODEL), dwkv.reshape(N_DEV, D_MODEL // N_DEV, 2 * KV_WIDTH),
            dwb, dwob.reshape(N_DEV, B_WIDTH // N_DEV, D_MODEL), small]
    wire = [BF16] * 5 + [F32]
    place = jnp.stack([2 * lax.axis_index("x") + lax.axis_index("y"), lax.axis_index("c")]).astype(jnp.int32)
    full = [g.reshape((4, 2) + g.shape[1:]) for g in full]
    landed = _pair_exchange(full)
    got = _chip_exchange([_pair_sum(place, g, l, dt) for g, l, dt in zip(full, landed, wire)])

    big = [(w_in_a, m_w_in_a, v_w_in_a), (w_out_a, m_w_out_a, v_w_out_a), (w_kv, m_w_kv, v_w_kv),
           (w_in_b, m_w_in_b, v_w_in_b), (w_out_b, m_w_out_b, v_w_out_b)]
    big_out = []
    for i, (w, m, v) in enumerate(big):
        shape2 = got[i].shape[1:]
        res = _adamw_shard(place, full[i], landed[i], got[i], w.reshape(shape2), m.reshape(shape2),
                           v.reshape(shape2))
        big_out.append([r.reshape(w.shape) for r in res])

    small_all = _gather_small(full[5], landed[5], got[5]).reshape(N_DEV * SMALL_ROWS, CHUNK)
    g_ws, g_bsp, g_sink, g_rel, g_pg, g_pb, g_lng, g_lnb = _unpack_small(small_all, small_shapes)
    g_ws = g_ws.reshape(w_spatial.shape)
    g_bsp = g_bsp.reshape(b_spatial.shape)
    g_pg = g_pg.reshape(post_ln_g.shape)
    g_pb = g_pb.reshape(post_ln_b.shape)
    g_lng = lax.dynamic_slice(g_lng.reshape(1, A_WIDTH), (0, me * sgu_ln_g.shape[1]), sgu_ln_g.shape)
    g_lnb = lax.dynamic_slice(g_lnb.reshape(1, A_WIDTH), (0, me * sgu_ln_b.shape[1]), sgu_ln_b.shape)
    small_params = [(sgu_ln_g, g_lng, m_sgu_ln_g, v_sgu_ln_g), (sgu_ln_b, g_lnb, m_sgu_ln_b, v_sgu_ln_b),
                    (w_spatial, g_ws, m_w_spatial, v_w_spatial), (b_spatial, g_bsp, m_b_spatial, v_b_spatial),
                    (attn_sinks, g_sink, m_attn_sinks, v_attn_sinks), (rel_bias, g_rel, m_rel_bias, v_rel_bias),
                    (post_ln_g, g_pg, m_post_ln_g, v_post_ln_g), (post_ln_b, g_pb, m_post_ln_b, v_post_ln_b)]
    flat2 = [tuple(a.reshape(-1, a.shape[-1]) for a in p) for p in small_params]
    small_out = [tuple(r.reshape(p[0].shape) for r in res) for res, p in zip(_adamw_small(flat2), small_params)]

    loss_all = lax.psum(loss[0, 0], ("x", "y", "c"))
    order = [("big", 0), ("small", 0), ("small", 1), ("small", 2), ("small", 3), ("big", 1), ("big", 2), ("big", 3),
             ("small", 4), ("small", 5), ("big", 4), ("small", 6), ("small", 7)]
    grads, deltas, new_m, new_v = [], [], [], []
    for kind, i in order:
        if kind == "big":
            g, d, m, v = big_out[i]
        else:
            g = small_params[i][1]
            d, m, v = small_out[i]
        grads.append(g)
        deltas.append(d)
        new_m.append(m)
        new_v.append(v)
    return (loss_all, dx.reshape(x.shape), *grads, *deltas, *new_m, *new_v)
```

```python
import functools

import jax
import jax.numpy as jnp
import numpy as np
from jax import lax
from jax.experimental import pallas as pl
from jax.experimental.pallas import tpu as pltpu

F32 = jnp.float32
BF16 = jnp.bfloat16

D_MODEL = 1024
A_WIDTH = 2048
A_GROUPS = 8
A_GROUP_DIM = A_WIDTH // A_GROUPS
CHUNK = 128
HEAD_DIM = 64
N_Q_HEADS = 16
N_KV_HEADS = 2
PAIRS_PER_KV = N_Q_HEADS // N_KV_HEADS // 2
N_PAIRS = N_Q_HEADS // 2
B_WIDTH = N_Q_HEADS * HEAD_DIM
KV_WIDTH = N_KV_HEADS * HEAD_DIM
REL_BUCKETS = 32
REL_MAX_DIST = 128
DEPTH = 2
ALPHA = (2.0 * DEPTH) ** 0.25
LN_EPS = 1e-5
NEG_INF = -1e30
SCALE = HEAD_DIM ** -0.5

ADAM_LR = 0.001
ADAM_B1 = 0.9
ADAM_B2 = 0.999
ADAM_EPS = 1e-08
ADAM_WD = 0.01
ADAM_STEP = 10

N_DEV = 8
MESH = pl.DeviceIdType.MESH
TOKENS_PER_STEP = 256
VMEM_LIMIT_BYTES = 60 * 1024 * 1024
SMALL_ROWS = 144

ANY = pl.BlockSpec(memory_space=pl.ANY)
VMEM = pl.BlockSpec(memory_space=pltpu.VMEM)
SMEM = pl.BlockSpec(memory_space=pltpu.SMEM)


def _nn(a, b):
    return jnp.dot(a, b, preferred_element_type=F32)


def _nt(a, b):
    return lax.dot_general(a, b, (((1,), (1,)), ((), ())), preferred_element_type=F32)


def _tn(a, b):
    return lax.dot_general(a, b, (((0,), (0,)), ((), ())), preferred_element_type=F32)


def _resident(shape):
    nd = len(shape)
    return pl.BlockSpec(shape, lambda i: (0,) * nd, pipeline_mode=pl.Buffered(1))


def _rows(tm, width, reverse_of=None):
    if reverse_of is None:
        return pl.BlockSpec((tm, width), lambda i: (i, 0))
    return pl.BlockSpec((tm, width), lambda i: (reverse_of - 1 - i, 0))


def _seq_params():
    return pltpu.CompilerParams(dimension_semantics=("arbitrary",), vmem_limit_bytes=VMEM_LIMIT_BYTES)


def _ln_normalize(r):
    mu = jnp.mean(r, axis=-1, keepdims=True)
    xc = r - mu
    var = jnp.mean(xc * xc, axis=-1, keepdims=True)
    rstd = lax.rsqrt(var + LN_EPS)
    return xc * rstd, rstd


def _ln_backward(dxhat, xhat, rstd):
    m1 = jnp.mean(dxhat, axis=-1, keepdims=True)
    m2 = jnp.mean(dxhat * xhat, axis=-1, keepdims=True)
    return rstd * (dxhat - m1 - xhat * m2)


def _silu_and_grad(z):
    sig = jax.nn.sigmoid(z)
    return z * sig, sig * (1.0 + z * (1.0 - sig))


def _band_tables():
    t = np.arange(CHUNK, dtype=np.int32)[:, None]
    j = np.arange(2 * CHUNK, dtype=np.int32)[None, :]
    d = t + CHUNK - j
    in_window = (d >= 0) & (d < CHUNK)
    dc = np.clip(d, 0, REL_MAX_DIST - 1)
    max_exact = REL_BUCKETS // 2
    df = np.maximum(dc, 1).astype(np.float32)
    large = max_exact + (np.log(df / np.float32(max_exact)) / np.float32(np.log(REL_MAX_DIST / max_exact))
                         * np.float32(REL_BUCKETS - max_exact)).astype(np.int32)
    large = np.minimum(large, REL_BUCKETS - 1)
    bucket = np.where(dc < max_exact, dc, large)
    return np.where(in_window, bucket, -1).astype(np.int32)


def _gather_weights(w_in_a, w_out_a, w_kv, w_in_b, w_out_b, ln_gb):
    shards = (w_in_a, w_out_a, w_kv, w_in_b, w_out_b)
    n = len(shards) + 1

    def body(*refs):
        ins, outs = refs[:n], refs[n:2 * n]
        stage = refs[2 * n:3 * n - 1]
        send_sems, recv_sems, local_sems = refs[3 * n - 1:]
        x, y, c = lax.axis_index("x"), lax.axis_index("y"), lax.axis_index("c")
        me, sibling = (x, y, c), (x, y, 1 - c)
        chips = [(1 - x, y), (x, 1 - y), (1 - x, 1 - y)]
        srcs = []
        for i in range(n - 1):
            stage[i][...] = ins[i][...].astype(BF16)
            srcs.append(stage[i])
        srcs.append(ins[n - 1])

        def slot(a, px, py, pc):
            return outs[a].at[4 * px + 2 * py + pc]

        def copy(a, k, block, to, src=None):
            return pltpu.make_async_remote_copy(
                src_ref=slot(a, *block) if src is None else src, dst_ref=slot(a, *block),
                send_sem=send_sems.at[a, k], recv_sem=recv_sems.at[a, k], device_id=to, device_id_type=MESH)

        mine, first, passed = [], [], []
        for a in range(n):
            mine.append(pltpu.make_async_copy(srcs[a], slot(a, *me), local_sems.at[a]))
            mine[-1].start()
            first.append(copy(a, 0, me, sibling, srcs[a]))
            for j, chip in enumerate(chips):
                first.append(copy(a, 1 + j, me, (*chip, c), srcs[a]))
        for cp in first:
            cp.start()
        for j, chip in enumerate(chips):
            for a in range(n):
                copy(a, 1 + j, (*chip, c), me).wait_recv()
                passed.append(copy(a, 4 + j, (*chip, c), sibling))
                passed[-1].start()
        for a in range(n):
            copy(a, 0, sibling, me).wait_recv()
            for j, chip in enumerate(chips):
                copy(a, 4 + j, (*chip, 1 - c), me).wait_recv()
        for cp in first + passed:
            cp.wait_send()
        for cp in mine:
            cp.wait()

    ins = shards + (ln_gb,)
    out_shape = [jax.ShapeDtypeStruct((N_DEV,) + s.shape, BF16) for s in shards]
    out_shape.append(jax.ShapeDtypeStruct((N_DEV,) + ln_gb.shape, F32))
    return pl.pallas_call(
        body, name="gather_weights", out_shape=out_shape,
        in_specs=[VMEM] * n, out_specs=[ANY] * n,
        scratch_shapes=[pltpu.VMEM(s.shape, BF16) for s in shards] + [
            pltpu.SemaphoreType.DMA((n, 7)), pltpu.SemaphoreType.DMA((n, 7)), pltpu.SemaphoreType.DMA((n,))],
        compiler_params=pltpu.CompilerParams(vmem_limit_bytes=VMEM_LIMIT_BYTES),
    )(*ins)


def _bias_table(rel_bias, bucket):
    def body(rel_ref, bucket_ref, out_ref):
        bk = bucket_ref[...]
        col = lax.broadcasted_iota(jnp.int32, bk.shape, 1)
        for h in range(N_Q_HEADS):
            def step(b, acc):
                return jnp.where(bk == b, rel_ref[b, h], acc)
            tab = lax.fori_loop(0, REL_BUCKETS, step, jnp.full(bk.shape, NEG_INF, F32))
            pair = h // 2
            rows = slice((pair % PAIRS_PER_KV) * CHUNK, (pair % PAIRS_PER_KV + 1) * CHUNK)
            half = slice((h % 2) * 2 * CHUNK, (h % 2 + 1) * 2 * CHUNK)
            out_ref[1, pair // PAIRS_PER_KV, rows, half] = tab
            out_ref[0, pair // PAIRS_PER_KV, rows, half] = jnp.where(col < CHUNK, NEG_INF, tab)

    return pl.pallas_call(
        body, name="bias_table",
        out_shape=jax.ShapeDtypeStruct((2, N_KV_HEADS, PAIRS_PER_KV * CHUNK, 4 * CHUNK), F32),
        in_specs=[SMEM, VMEM], out_specs=VMEM,
    )(rel_bias, bucket)


def _rel_bias_grad(dlogits, bucket):
    def body(dl_ref, bucket_ref, out_ref):
        bk = bucket_ref[...]
        row = lax.broadcasted_iota(jnp.int32, (REL_BUCKETS, CHUNK), 0)
        lane = lax.broadcasted_iota(jnp.int32, (REL_BUCKETS, CHUNK), 1)

        def step(b, acc):
            for h in range(N_Q_HEADS):
                pair = h // 2
                dl = dl_ref[pair // PAIRS_PER_KV, (pair % PAIRS_PER_KV) * CHUNK:(pair % PAIRS_PER_KV + 1) * CHUNK,
                            (h % 2) * 2 * CHUNK:(h % 2 + 1) * 2 * CHUNK]
                s = jnp.sum(jnp.where(bk == b, dl, 0.0))
                acc = jnp.where((row == b) & (lane == h), s, acc)
            return acc

        out_ref[...] = lax.fori_loop(0, REL_BUCKETS, step, jnp.zeros((REL_BUCKETS, CHUNK), F32))

    return pl.pallas_call(
        body, name="rel_bias_grad", out_shape=jax.ShapeDtypeStruct((REL_BUCKETS, CHUNK), F32),
        in_specs=[VMEM, VMEM], out_specs=VMEM,
    )(dlogits, bucket)


def _masked_spatial(ws_ref, g):
    t = lax.broadcasted_iota(jnp.int32, (CHUNK, CHUNK), 0)
    s = lax.broadcasted_iota(jnp.int32, (CHUNK, CHUNK), 1)
    return jnp.where(s <= t, ws_ref[g], 0.0).astype(BF16)


def _spatial_gate(ws_ref, bsp_t_ref, vn, s_ref, n_chunks):
    for g in range(A_GROUPS):
        w = _masked_spatial(ws_ref, g)
        cols = slice(g * A_GROUP_DIM, (g + 1) * A_GROUP_DIM)
        for ci in range(n_chunks):
            rows = slice(ci * CHUNK, (ci + 1) * CHUNK)
            s_ref[rows, cols] = _nn(w, vn[rows, cols]) + bsp_t_ref[:, g:g + 1]


def _fwd_a(x, wa, ln_gb, w_spatial, bsp_t, woa):
    seq = x.shape[0]
    tm = min(TOKENS_PER_STEP, seq)
    n_chunks = tm // CHUNK
    shard_cols = wa.shape[2]

    def body(x_ref, wa_ref, ln_ref, ws_ref, bsp_ref, woa_ref, p_ref, r0_ref, xb_ref, s_ref):
        xv = x_ref[...]
        xb = xv.astype(BF16)
        xb_ref[...] = xb
        for k in range(N_DEV):
            p_ref[:, k * shard_cols:(k + 1) * shard_cols] = _nn(xb, wa_ref[k])
        vhat, _ = _ln_normalize(p_ref[:, A_WIDTH:2 * A_WIDTH])
        vn = (vhat * ln_ref[0:1, :] + ln_ref[1:2, :]).astype(BF16)
        _spatial_gate(ws_ref, bsp_ref, vn, s_ref, n_chunks)
        z = p_ref[:, 2 * A_WIDTH:]
        y = p_ref[:, :A_WIDTH] * s_ref[...] * (z * jax.nn.sigmoid(z))
        r0_ref[...] = ALPHA * xv + _nn(y.astype(BF16), woa_ref[...])

    return pl.pallas_call(
        body, name="fwd_a", grid=(seq // tm,),
        out_shape=[jax.ShapeDtypeStruct((seq, 3 * A_WIDTH), F32), jax.ShapeDtypeStruct((seq, D_MODEL), F32),
                   jax.ShapeDtypeStruct((seq, D_MODEL), BF16)],
        in_specs=[_rows(tm, D_MODEL), _resident(wa.shape), _resident(ln_gb.shape), _resident(w_spatial.shape),
                  _resident(bsp_t.shape), _resident(woa.shape)],
        out_specs=[_rows(tm, 3 * A_WIDTH), _rows(tm, D_MODEL), _rows(tm, D_MODEL)],
        scratch_shapes=[pltpu.VMEM((tm, A_WIDTH), F32)],
        compiler_params=_seq_params(),
    )(x, wa, ln_gb, w_spatial, bsp_t, woa)


def _expand_heads(kv):
    lo = lax.broadcasted_iota(jnp.int32, kv.shape, 1) < HEAD_DIM
    swapped = pltpu.roll(kv, HEAD_DIM, 1)
    zero = jnp.zeros_like(kv)
    return (jnp.where(lo, kv, zero).astype(BF16), jnp.where(lo, zero, swapped).astype(BF16),
            jnp.where(lo, swapped, zero).astype(BF16), jnp.where(lo, zero, kv).astype(BF16))


def _fold_heads(acc, kvh):
    top, bot = acc[:2 * CHUNK], acc[2 * CHUNK:]
    lo = lax.broadcasted_iota(jnp.int32, top.shape, 1) < HEAD_DIM
    t = jnp.where(lo, top, 0.0)
    b = jnp.where(lo, 0.0, bot)
    if kvh == 0:
        return t + pltpu.roll(b, HEAD_DIM, 1)
    return pltpu.roll(t, HEAD_DIM, 1) + b


STACK = PAIRS_PER_KV * CHUNK


def _pair_cols(kvh, pp):
    pair = kvh * PAIRS_PER_KV + pp
    return slice(pair * CHUNK, (pair + 1) * CHUNK)


def _stack_pairs(ref_or_val, rows, kvh):
    return jnp.concatenate([ref_or_val[rows, _pair_cols(kvh, pp)] for pp in range(PAIRS_PER_KV)], axis=0)


def _sink_columns(sink_ref):
    blk = lax.broadcasted_iota(jnp.int32, (STACK, 1), 0) // CHUNK
    out = []
    for kvh in range(N_KV_HEADS):
        halves = []
        for half in range(2):
            col = jnp.zeros((STACK, 1), F32)
            for pp in range(PAIRS_PER_KV):
                col = jnp.where(blk == pp, sink_ref[0, 2 * (kvh * PAIRS_PER_KV + pp) + half], col)
            halves.append(col)
        out.append(halves)
    return out


def _softmax_stack(logits, sink_cols):
    ps, sinks = [], []
    for half in range(2):
        lg = logits[:, half * 2 * CHUNK:(half + 1) * 2 * CHUNK]
        sk = sink_cols[half]
        m = jnp.maximum(jnp.max(lg, axis=-1, keepdims=True), sk)
        e = jnp.exp(lg - m)
        es = jnp.exp(sk - m)
        inv = 1.0 / (jnp.sum(e, axis=-1, keepdims=True) + es)
        ps.append(e * inv)
        sinks.append(es * inv)
    return jnp.concatenate(ps, axis=1), sinks


def _band_operands(ek_ref, ev_ref, ci, kvh):
    kblk = jnp.concatenate([ek_ref[ci, 2 * kvh], ek_ref[ci + 1, 2 * kvh],
                            ek_ref[ci, 2 * kvh + 1], ek_ref[ci + 1, 2 * kvh + 1]], axis=0)
    vblk = jnp.concatenate([ev_ref[ci, 2 * kvh], ev_ref[ci + 1, 2 * kvh],
                            ev_ref[ci, 2 * kvh + 1], ev_ref[ci + 1, 2 * kvh + 1]], axis=0)
    return kblk, vblk


def _fwd_b(r0, target, wb, wkv, wob, post_g, post_b, sinks, bias):
    seq = r0.shape[0]
    tm = min(TOKENS_PER_STEP, seq)
    n_chunks = tm // CHUNK
    shard_cols = wb.shape[2]

    def body(r0_ref, tgt_ref, wb_ref, wkv_ref, wob_ref, pg_ref, pb_ref, sink_ref, bias_ref,
             q_ref, z_ref, kv_ref, dr1_ref, loss_ref, dg_ref, db_ref, ek_ref, ev_ref, o_ref):
        step = pl.program_id(0)

        @pl.when(step == 0)
        def _():
            ek_ref[0] = jnp.zeros(ek_ref.shape[1:], BF16)
            ev_ref[0] = jnp.zeros(ev_ref.shape[1:], BF16)
            loss_ref[...] = jnp.zeros_like(loss_ref)
            dg_ref[...] = jnp.zeros_like(dg_ref)
            db_ref[...] = jnp.zeros_like(db_ref)

        @pl.when(step > 0)
        def _():
            ek_ref[0] = ek_ref[n_chunks]
            ev_ref[0] = ev_ref[n_chunks]

        xhat0, _ = _ln_normalize(r0_ref[...])
        h1 = xhat0 * pg_ref[0:1, :] + pb_ref[0:1, :]
        h1b = h1.astype(BF16)
        for k in range(N_DEV):
            qz = _nn(h1b, wb_ref[k])
            lo, hi = k * shard_cols, (k + 1) * shard_cols
            if hi <= B_WIDTH:
                q_ref[:, lo:hi] = (qz * SCALE).astype(BF16)
            else:
                z_ref[:, lo - B_WIDTH:hi - B_WIDTH] = qz
        kv = _nn(h1b, wkv_ref[...])
        kv_ref[...] = kv.astype(BF16)
        for ci in range(n_chunks):
            rows = slice(ci * CHUNK, (ci + 1) * CHUNK)
            for i, e in enumerate(_expand_heads(kv[rows, :KV_WIDTH])):
                ek_ref[ci + 1, i] = e
            for i, e in enumerate(_expand_heads(kv[rows, KV_WIDTH:])):
                ev_ref[ci + 1, i] = e

        sink_cols = _sink_columns(sink_ref)
        for ci in range(n_chunks):
            rows = slice(ci * CHUNK, (ci + 1) * CHUNK)
            sel = jnp.where(step * n_chunks + ci > 0, 1, 0)
            for kvh in range(N_KV_HEADS):
                kblk, vblk = _band_operands(ek_ref, ev_ref, ci, kvh)
                logits = _nt(_stack_pairs(q_ref, rows, kvh), kblk) + bias_ref[sel, kvh]
                p, _ = _softmax_stack(logits, sink_cols[kvh])
                o = _nn(p.astype(BF16), vblk)
                for pp in range(PAIRS_PER_KV):
                    o_ref[rows, _pair_cols(kvh, pp)] = o[pp * CHUNK:(pp + 1) * CHUNK]

        z = z_ref[...]
        y = o_ref[...] * (z * jax.nn.sigmoid(z))
        r1 = ALPHA * h1 + _nn(y.astype(BF16), wob_ref[...])
        xhat1, rstd1 = _ln_normalize(r1)
        g1 = pg_ref[1:2, :]
        diff = xhat1 * g1 + pb_ref[1:2, :] - tgt_ref[...]
        loss_ref[...] += 0.5 * jnp.sum(jnp.mean(diff * diff, axis=-1, keepdims=True), axis=0, keepdims=True)
        dh2 = diff * (1.0 / D_MODEL)
        dg_ref[...] += jnp.sum(dh2 * xhat1, axis=0, keepdims=True)
        db_ref[...] += jnp.sum(dh2, axis=0, keepdims=True)
        dr1_ref[...] = _ln_backward(dh2 * g1, xhat1, rstd1)

    vec = pl.BlockSpec((1, D_MODEL), lambda i: (0, 0))
    return pl.pallas_call(
        body, name="fwd_b", grid=(seq // tm,),
        out_shape=[jax.ShapeDtypeStruct((seq, B_WIDTH), BF16), jax.ShapeDtypeStruct((seq, B_WIDTH), F32),
                   jax.ShapeDtypeStruct((seq, 2 * KV_WIDTH), BF16), jax.ShapeDtypeStruct((seq, D_MODEL), F32),
                   jax.ShapeDtypeStruct((1, 1), F32), jax.ShapeDtypeStruct((1, D_MODEL), F32),
                   jax.ShapeDtypeStruct((1, D_MODEL), F32)],
        in_specs=[_rows(tm, D_MODEL), _rows(tm, D_MODEL), _resident(wb.shape), _resident(wkv.shape),
                  _resident(wob.shape), _resident(post_g.shape), _resident(post_b.shape), SMEM,
                  _resident(bias.shape)],
        out_specs=[_rows(tm, B_WIDTH), _rows(tm, B_WIDTH), _rows(tm, 2 * KV_WIDTH), _rows(tm, D_MODEL),
                   pl.BlockSpec((1, 1), lambda i: (0, 0)), vec, vec],
        scratch_shapes=[pltpu.VMEM((n_chunks + 1, 4, CHUNK, CHUNK), BF16),
                        pltpu.VMEM((n_chunks + 1, 4, CHUNK, CHUNK), BF16),
                        pltpu.VMEM((tm, B_WIDTH), F32)],
        compiler_params=_seq_params(),
    )(r0, target, wb, wkv, wob, post_g, post_b, sinks, bias)


def _bwd_b(r0, dr1, q, z, kv, wb, wkv, wob, post_g, post_b, sinks, bias):
    seq = r0.shape[0]
    tm = min(TOKENS_PER_STEP, seq)
    n_chunks = tm // CHUNK
    n_blocks = seq // tm
    shard_cols = wb.shape[2]

    def body(r0_ref, dr1_ref, q_ref, z_ref, kv_ref, kvp_ref, wb_ref, wkv_ref, wob_ref, pg_ref, pb_ref, sink_ref,
             bias_ref, dh1_ref, dwb_out, dwkv_out, dwob_out, dsink_ref, dl_out,
             dwb_acc, dwkv_acc, dwob_acc, dl_acc, dsink_acc, carry_ref, ek_ref, ev_ref, y_ref, dqz_ref, dkv_ref):
        step = pl.program_id(0)
        block = n_blocks - 1 - step

        @pl.when(step == 0)
        def _():
            dwb_acc[...] = jnp.zeros_like(dwb_acc)
            dwkv_acc[...] = jnp.zeros_like(dwkv_acc)
            dwob_acc[...] = jnp.zeros_like(dwob_acc)
            dl_acc[...] = jnp.zeros_like(dl_acc)
            carry_ref[...] = jnp.zeros_like(carry_ref)
            dsink_acc[...] = jnp.zeros_like(dsink_acc)

        xhat0, _ = _ln_normalize(r0_ref[...])
        h1b = (xhat0 * pg_ref[0:1, :] + pb_ref[0:1, :]).astype(BF16)
        dr1 = dr1_ref[...]
        dr1b = dr1.astype(BF16)
        dy = _nt(dr1b, wob_ref[...])
        sg, dsg = _silu_and_grad(z_ref[...])
        do_all = dy * sg
        dyz = dy * dsg

        for ci in range(n_chunks + 1):
            src = kvp_ref[...] if ci == 0 else kv_ref[(ci - 1) * CHUNK:ci * CHUNK, :]
            src = src.astype(F32)
            for i, e in enumerate(_expand_heads(src[:, :KV_WIDTH])):
                ek_ref[ci, i] = e
            for i, e in enumerate(_expand_heads(src[:, KV_WIDTH:])):
                ev_ref[ci, i] = e

        sink_cols = _sink_columns(sink_ref)
        for ci in reversed(range(n_chunks)):
            rows = slice(ci * CHUNK, (ci + 1) * CHUNK)
            sel = jnp.where(block * n_chunks + ci > 0, 1, 0)
            dk_band = jnp.zeros((2 * CHUNK, KV_WIDTH), F32)
            dv_band = jnp.zeros((2 * CHUNK, KV_WIDTH), F32)
            for kvh in range(N_KV_HEADS):
                kblk, vblk = _band_operands(ek_ref, ev_ref, ci, kvh)
                qs = _stack_pairs(q_ref, rows, kvh)
                p, psink = _softmax_stack(_nt(qs, kblk) + bias_ref[sel, kvh], sink_cols[kvh])
                pb16 = p.astype(BF16)
                o = _nn(pb16, vblk)
                y = (o * _stack_pairs(sg, rows, kvh)).astype(BF16)
                dz = (o * _stack_pairs(dyz, rows, kvh)).astype(BF16)
                dob = _stack_pairs(do_all, rows, kvh).astype(BF16)
                dp = _nt(dob, vblk)
                dlogits = []
                for half in range(2):
                    hs = slice(half * 2 * CHUNK, (half + 1) * 2 * CHUNK)
                    delta = jnp.sum(dp[:, hs] * p[:, hs], axis=-1, keepdims=True)
                    dlogits.append(p[:, hs] * (dp[:, hs] - delta))
                    dsink_acc[2 * kvh + half] += psink[half] * delta
                dlog = jnp.concatenate(dlogits, axis=1)
                dl_acc[kvh] += dlog
                dsc = dlog.astype(BF16)
                dq = (_nn(dsc, kblk) * SCALE).astype(BF16)
                for pp in range(PAIRS_PER_KV):
                    blk = slice(pp * CHUNK, (pp + 1) * CHUNK)
                    cols = _pair_cols(kvh, pp)
                    y_ref[rows, cols] = y[blk]
                    dqz_ref[rows, cols] = dq[blk]
                    dqz_ref[rows, B_WIDTH + cols.start:B_WIDTH + cols.stop] = dz[blk]
                dk_band = dk_band + _fold_heads(_tn(dsc, qs), kvh)
                dv_band = dv_band + _fold_heads(_tn(pb16, dob), kvh)
            band = jnp.concatenate([dk_band, dv_band], axis=1)
            dkv_ref[rows, :] = (band[CHUNK:] + carry_ref[...]).astype(BF16)
            carry_ref[...] = band[:CHUNK]

        dqz = dqz_ref[...]
        dkv = dkv_ref[...]
        dh1 = ALPHA * dr1 + _nt(dkv, wkv_ref[...])
        for k in range(N_DEV):
            part = dqz[:, k * shard_cols:(k + 1) * shard_cols]
            dh1 = dh1 + _nt(part, wb_ref[k])
            dwb_acc[k] += _tn(h1b, part)
        dh1_ref[...] = dh1
        dwkv_acc[...] += _tn(h1b, dkv)
        dwob_acc[...] += _tn(y_ref[...], dr1b)

        @pl.when(step == n_blocks - 1)
        def _():
            pltpu.sync_copy(dwb_acc, dwb_out)
            pltpu.sync_copy(dwkv_acc, dwkv_out)
            pltpu.sync_copy(dwob_acc, dwob_out)
            pltpu.sync_copy(dl_acc, dl_out)
            lane = lax.broadcasted_iota(jnp.int32, (1, CHUNK), 1)
            dsink = jnp.zeros((1, CHUNK), F32)
            for kvh in range(N_KV_HEADS):
                for half in range(2):
                    for pp in range(PAIRS_PER_KV):
                        head = 2 * (kvh * PAIRS_PER_KV + pp) + half
                        tot = jnp.sum(dsink_acc[2 * kvh + half, pp * CHUNK:(pp + 1) * CHUNK, :], axis=0, keepdims=True)
                        dsink = dsink - jnp.where(lane == head, tot, 0.0)
            dsink_ref[...] = dsink

    rev = functools.partial(_rows, tm, reverse_of=n_blocks)
    prev_chunk = pl.BlockSpec(
        (CHUNK, 2 * KV_WIDTH), lambda i: (jnp.maximum((n_blocks - 1 - i) * n_chunks - 1, 0), 0))
    return pl.pallas_call(
        body, name="bwd_b", grid=(n_blocks,),
        out_shape=[jax.ShapeDtypeStruct((seq, D_MODEL), F32), jax.ShapeDtypeStruct(wb.shape, F32),
                   jax.ShapeDtypeStruct(wkv.shape, F32), jax.ShapeDtypeStruct(wob.shape, F32),
                   jax.ShapeDtypeStruct((1, CHUNK), F32),
                   jax.ShapeDtypeStruct((N_KV_HEADS, STACK, 4 * CHUNK), F32)],
        in_specs=[rev(D_MODEL), rev(D_MODEL), rev(B_WIDTH), rev(B_WIDTH), rev(2 * KV_WIDTH), prev_chunk,
                  _resident(wb.shape), _resident(wkv.shape), _resident(wob.shape), _resident(post_g.shape),
                  _resident(post_b.shape), SMEM, _resident(bias.shape)],
        out_specs=[rev(D_MODEL), ANY, ANY, ANY, pl.BlockSpec((1, CHUNK), lambda i: (0, 0)), ANY],
        scratch_shapes=[pltpu.VMEM(wb.shape, F32), pltpu.VMEM(wkv.shape, F32), pltpu.VMEM(wob.shape, F32),
                        pltpu.VMEM((N_KV_HEADS, STACK, 4 * CHUNK), F32), pltpu.VMEM((2 * N_KV_HEADS, STACK, 1), F32),
                        pltpu.VMEM((CHUNK, 2 * KV_WIDTH), F32),
                        pltpu.VMEM((n_chunks + 1, 4, CHUNK, CHUNK), BF16),
                        pltpu.VMEM((n_chunks + 1, 4, CHUNK, CHUNK), BF16),
                        pltpu.VMEM((tm, B_WIDTH), BF16), pltpu.VMEM((tm, 2 * B_WIDTH), BF16),
                        pltpu.VMEM((tm, 2 * KV_WIDTH), BF16)],
        compiler_params=_seq_params(),
    )(r0, dr1, q, z, kv, kv, wb, wkv, wob, post_g, post_b, sinks, bias)


def _bwd_a(dh1, r0, p, ln_gb, w_spatial, bsp_t, woa, post_g):
    seq = r0.shape[0]
    tm = min(TOKENS_PER_STEP, seq)
    n_chunks = tm // CHUNK
    n_blocks = seq // tm

    def body(dh1_ref, r0_ref, p_ref, ln_ref, ws_ref, bsp_ref, woa_ref, pg_ref,
             dp_ref, dr0_ref, dwoa_out, dws_ref, dbsp_ref, dln_ref, dpost_ref,
             dwoa_acc, s_ref, ds_ref, dvn_ref):
        step = pl.program_id(0)

        @pl.when(step == 0)
        def _():
            dwoa_acc[...] = jnp.zeros_like(dwoa_acc)
            dws_ref[...] = jnp.zeros_like(dws_ref)
            dbsp_ref[...] = jnp.zeros_like(dbsp_ref)
            dln_ref[...] = jnp.zeros_like(dln_ref)
            dpost_ref[...] = jnp.zeros_like(dpost_ref)

        dh1 = dh1_ref[...]
        xhat0, rstd0 = _ln_normalize(r0_ref[...])
        dpost_ref[0:1, :] += jnp.sum(dh1 * xhat0, axis=0, keepdims=True)
        dpost_ref[1:2, :] += jnp.sum(dh1, axis=0, keepdims=True)
        dr0 = _ln_backward(dh1 * pg_ref[0:1, :], xhat0, rstd0)
        dr0_ref[...] = dr0
        dr0b = dr0.astype(BF16)

        vhat, rstd_v = _ln_normalize(p_ref[:, A_WIDTH:2 * A_WIDTH])
        ln_g = ln_ref[0:1, :]
        vn = (vhat * ln_g + ln_ref[1:2, :]).astype(BF16)
        _spatial_gate(ws_ref, bsp_ref, vn, s_ref, n_chunks)
        u = p_ref[:, :A_WIDTH]
        s = s_ref[...]
        sg, dsg = _silu_and_grad(p_ref[:, 2 * A_WIDTH:])
        dwoa_acc[...] += _tn((u * s * sg).astype(BF16), dr0b)
        dy = _nt(dr0b, woa_ref[...])
        dp_ref[:, :A_WIDTH] = (dy * s * sg).astype(BF16)
        dp_ref[:, 2 * A_WIDTH:] = (dy * u * s * dsg).astype(BF16)
        ds_ref[...] = dy * u * sg

        lane = lax.broadcasted_iota(jnp.int32, (CHUNK, CHUNK), 1)
        dbsp = jnp.zeros((CHUNK, CHUNK), F32)
        for g in range(A_GROUPS):
            w = _masked_spatial(ws_ref, g)
            cols = slice(g * A_GROUP_DIM, (g + 1) * A_GROUP_DIM)
            dws = jnp.zeros((CHUNK, CHUNK), F32)
            for ci in range(n_chunks):
                rows = slice(ci * CHUNK, (ci + 1) * CHUNK)
                ds = ds_ref[rows, cols]
                dsb = ds.astype(BF16)
                dvn_ref[rows, cols] = _tn(w, dsb)
                dws = dws + _nt(dsb, vn[rows, cols])
                dbsp = dbsp + jnp.where(lane == g, jnp.sum(ds, axis=-1, keepdims=True), 0.0)
            dws_ref[g] += dws
        dbsp_ref[...] += dbsp

        dvn = dvn_ref[...]
        dln_ref[0:1, :] += jnp.sum(dvn * vhat, axis=0, keepdims=True)
        dln_ref[1:2, :] += jnp.sum(dvn, axis=0, keepdims=True)
        dp_ref[:, A_WIDTH:2 * A_WIDTH] = _ln_backward(dvn * ln_g, vhat, rstd_v).astype(BF16)

        @pl.when(step == n_blocks - 1)
        def _():
            t = lax.broadcasted_iota(jnp.int32, (CHUNK, CHUNK), 0)
            for g in range(A_GROUPS):
                dws_ref[g] = jnp.where(lane <= t, dws_ref[g], 0.0)
            pltpu.sync_copy(dwoa_acc, dwoa_out)

    def fixed(shape):
        nd = len(shape)
        return pl.BlockSpec(shape, lambda i: (0,) * nd)

    return pl.pallas_call(
        body, name="bwd_a", grid=(n_blocks,),
        out_shape=[jax.ShapeDtypeStruct((seq, 3 * A_WIDTH), BF16), jax.ShapeDtypeStruct((seq, D_MODEL), F32),
                   jax.ShapeDtypeStruct(woa.shape, F32), jax.ShapeDtypeStruct(w_spatial.shape, F32),
                   jax.ShapeDtypeStruct((CHUNK, CHUNK), F32), jax.ShapeDtypeStruct((2, A_WIDTH), F32),
                   jax.ShapeDtypeStruct((2, D_MODEL), F32)],
        in_specs=[_rows(tm, D_MODEL), _rows(tm, D_MODEL), _rows(tm, 3 * A_WIDTH), _resident(ln_gb.shape),
                  _resident(w_spatial.shape), _resident(bsp_t.shape), _resident(woa.shape),
                  _resident(post_g.shape)],
        out_specs=[_rows(tm, 3 * A_WIDTH), _rows(tm, D_MODEL), ANY, fixed(w_spatial.shape),
                   fixed((CHUNK, CHUNK)), fixed((2, A_WIDTH)), fixed((2, D_MODEL))],
        scratch_shapes=[pltpu.VMEM(woa.shape, F32), pltpu.VMEM((tm, A_WIDTH), F32),
                        pltpu.VMEM((tm, A_WIDTH), F32), pltpu.VMEM((tm, A_WIDTH), F32)],
        compiler_params=_seq_params(),
    )(dh1, r0, p, ln_gb, w_spatial, bsp_t, woa, post_g)


def _input_grad(dp, dr0, wa):
    seq = dr0.shape[0]
    tm = min(2 * TOKENS_PER_STEP, seq)
    shard_cols = wa.shape[2]

    def body(dp_ref, dr0_ref, wa_ref, dx_ref):
        dx = ALPHA * dr0_ref[...]
        for k in range(N_DEV):
            dx = dx + _nt(dp_ref[:, k * shard_cols:(k + 1) * shard_cols], wa_ref[k])
        dx_ref[...] = dx

    return pl.pallas_call(
        body, name="input_grad", grid=(seq // tm,),
        out_shape=jax.ShapeDtypeStruct((seq, D_MODEL), F32),
        in_specs=[_rows(tm, 3 * A_WIDTH), _rows(tm, D_MODEL), _resident(wa.shape)],
        out_specs=_rows(tm, D_MODEL),
        compiler_params=_seq_params(),
    )(dp, dr0, wa)


def _in_proj_weight_grad(xb, dp, shard_cols):
    seq = xb.shape[0]
    tk = min(1024, seq)
    n_k = seq // tk

    def body(x_ref, dp_ref, out_ref):
        @pl.when(pl.program_id(1) == 0)
        def _():
            out_ref[...] = jnp.zeros_like(out_ref)

        out_ref[...] += _tn(x_ref[...], dp_ref[...])

    return pl.pallas_call(
        body, name="in_proj_weight_grad", grid=(N_DEV, n_k),
        out_shape=jax.ShapeDtypeStruct((N_DEV, D_MODEL, shard_cols), F32),
        in_specs=[pl.BlockSpec((tk, D_MODEL), lambda j, k: (k, 0)),
                  pl.BlockSpec((tk, shard_cols), lambda j, k: (k, j))],
        out_specs=pl.BlockSpec((None, D_MODEL, shard_cols), lambda j, k: (j, 0, 0)),
        compiler_params=pltpu.CompilerParams(dimension_semantics=("arbitrary", "arbitrary"),
                                             vmem_limit_bytes=VMEM_LIMIT_BYTES),
    )(xb, dp)


def _pair_exchange(grads):
    n = len(grads)

    def body(*refs):
        ins, landed = refs[:n], refs[n:2 * n]
        send_sems, recv_sems = refs[2 * n:]
        x, y, c = lax.axis_index("x"), lax.axis_index("y"), lax.axis_index("c")
        copies = []
        for a in range(n):
            copies.append(pltpu.make_async_remote_copy(
                src_ref=ins[a].at[:, pl.ds(1 - c, 1)], dst_ref=landed[a],
                send_sem=send_sems.at[a], recv_sem=recv_sems.at[a], device_id=(x, y, 1 - c), device_id_type=MESH))
            copies[-1].start()
        for cp in copies:
            cp.wait()

    return pl.pallas_call(
        body, name="pair_exchange", out_shape=[jax.ShapeDtypeStruct((4, 1) + g.shape[2:], F32) for g in grads],
        in_specs=[ANY] * n, out_specs=[ANY] * n,
        scratch_shapes=[pltpu.SemaphoreType.DMA((n,)), pltpu.SemaphoreType.DMA((n,))],
    )(*grads)


def _row_tile(rows, cols, itemsize=4):
    if rows * cols * itemsize <= (1 << 20):
        return rows
    t = 8
    while 2 * t * cols * itemsize <= (1 << 20) and rows % (2 * t) == 0:
        t *= 2
    assert rows % t == 0, (rows, cols)
    return t


def _pair_sum(place, grad, landed, dtype):
    _, _, rows, cols = grad.shape
    tr = _row_tile(rows, cols)

    def body(place_ref, a_ref, b_ref, out_ref):
        out_ref[...] = (a_ref[...] + b_ref[...]).astype(dtype)

    return pl.pallas_call(
        body, name="pair_sum", out_shape=jax.ShapeDtypeStruct((4, rows, cols), dtype),
        grid_spec=pltpu.PrefetchScalarGridSpec(
            num_scalar_prefetch=1, grid=(4, rows // tr),
            in_specs=[pl.BlockSpec((None, None, tr, cols), lambda k, i, pr: (k, pr[1], i, 0)),
                      pl.BlockSpec((None, None, tr, cols), lambda k, i, pr: (k, 0, i, 0))],
            out_specs=pl.BlockSpec((None, tr, cols), lambda k, i, pr: (k, i, 0))),
        compiler_params=pltpu.CompilerParams(dimension_semantics=("parallel", "parallel")),
    )(place, grad, landed)


def _chip_exchange(sums):
    n = len(sums)

    def body(*refs):
        s_in, got = refs[:n], refs[n:2 * n]
        send_sems, recv_sems = refs[2 * n:]
        x, y, c = lax.axis_index("x"), lax.axis_index("y"), lax.axis_index("c")
        chips = [(1 - x, y), (x, 1 - y), (1 - x, 1 - y)]
        copies = []
        for a in range(n):
            for k, (px, py) in enumerate(chips):
                copies.append(pltpu.make_async_remote_copy(
                    src_ref=s_in[a].at[2 * px + py], dst_ref=got[a].at[k],
                    send_sem=send_sems.at[a, k], recv_sem=recv_sems.at[a, k],
                    device_id=(px, py, c), device_id_type=MESH))
                copies[-1].start()
        for cp in copies:
            cp.wait()

    return pl.pallas_call(
        body, name="chip_exchange", out_shape=[jax.ShapeDtypeStruct((3,) + s.shape[1:], s.dtype) for s in sums],
        in_specs=[ANY] * n, out_specs=[ANY] * n,
        scratch_shapes=[pltpu.SemaphoreType.DMA((n, 3)), pltpu.SemaphoreType.DMA((n, 3))],
    )(*sums)


def _adamw_math(w, g, m, v):
    m = ADAM_B1 * m + (1.0 - ADAM_B1) * g
    v = ADAM_B2 * v + (1.0 - ADAM_B2) * (g * g)
    m_hat = m / (1.0 - ADAM_B1 ** ADAM_STEP)
    v_hat = v / (1.0 - ADAM_B2 ** ADAM_STEP)
    delta = -ADAM_LR * (m_hat / (jnp.sqrt(v_hat) + ADAM_EPS) + ADAM_WD * w)
    return delta, m, v


def _adamw_shard(place, grad, landed, got, w, m, v):
    rows, cols = w.shape
    tr = _row_tile(rows, cols)

    def body(place_ref, a_ref, b_ref, got_ref, w_ref, m_ref, v_ref, g_out, d_out, m_out, v_out):
        g = a_ref[...] + b_ref[...]
        for k in range(3):
            g = g + got_ref[k].astype(F32)
        g_out[...] = g
        d_out[...], m_out[...], v_out[...] = _adamw_math(w_ref[...], g, m_ref[...], v_ref[...])

    spec = pl.BlockSpec((tr, cols), lambda i, pr: (i, 0))
    out = jax.ShapeDtypeStruct((rows, cols), F32)
    return pl.pallas_call(
        body, name="adamw_shard", out_shape=[out] * 4,
        grid_spec=pltpu.PrefetchScalarGridSpec(
            num_scalar_prefetch=1, grid=(rows // tr,),
            in_specs=[pl.BlockSpec((None, None, tr, cols), lambda i, pr: (pr[0], pr[1], i, 0)),
                      pl.BlockSpec((None, None, tr, cols), lambda i, pr: (pr[0], 0, i, 0)),
                      pl.BlockSpec((3, tr, cols), lambda i, pr: (0, i, 0)), spec, spec, spec],
            out_specs=[spec] * 4),
        compiler_params=pltpu.CompilerParams(dimension_semantics=("parallel",)),
    )(place, grad, landed, got, w, m, v)


def _gather_small(grad, landed, got):
    rows, cols = got.shape[1:]

    def body(grad_ref, landed_ref, got_ref, out_ref, send_sems, recv_sems):
        x, y, c = lax.axis_index("x"), lax.axis_index("y"), lax.axis_index("c")
        me, sibling = (x, y, c), (x, y, 1 - c)
        chips = [(1 - x, y), (x, 1 - y), (1 - x, 1 - y)]

        def slot(px, py, pc):
            return out_ref.at[4 * px + 2 * py + pc]

        def copy(k, block, to):
            return pltpu.make_async_remote_copy(
                src_ref=slot(*block), dst_ref=slot(*block), send_sem=send_sems.at[k], recv_sem=recv_sems.at[k],
                device_id=to, device_id_type=MESH)

        mine = grad_ref[2 * x + y, c] + landed_ref[2 * x + y, 0]
        out_ref[4 * x + 2 * y + c] = ((mine + got_ref[0]) + got_ref[1]) + got_ref[2]
        first = [copy(0, me, sibling)] + [copy(1 + j, me, (*chip, c)) for j, chip in enumerate(chips)]
        for cp in first:
            cp.start()
        passed = [copy(4 + j, (*chip, c), sibling) for j, chip in enumerate(chips)]
        for j, chip in enumerate(chips):
            copy(1 + j, (*chip, c), me).wait_recv()
            passed[j].start()
        copy(0, sibling, me).wait_recv()
        for j, chip in enumerate(chips):
            copy(4 + j, (*chip, 1 - c), me).wait_recv()
        for cp in first + passed:
            cp.wait_send()

    return pl.pallas_call(
        body, name="gather_small", out_shape=jax.ShapeDtypeStruct((N_DEV, rows, cols), F32),
        in_specs=[VMEM] * 3, out_specs=VMEM,
        scratch_shapes=[pltpu.SemaphoreType.DMA((7,)), pltpu.SemaphoreType.DMA((7,))],
    )(grad, landed, got)


def _adamw_small(params):
    n = len(params)

    def body(*refs):
        ins, outs = refs[:4 * n], refs[4 * n:]
        for i in range(n):
            w_ref, g_ref, m_ref, v_ref = ins[4 * i:4 * i + 4]
            d, m, v = _adamw_math(w_ref[...], g_ref[...], m_ref[...], v_ref[...])
            outs[3 * i][...] = d
            outs[3 * i + 1][...] = m
            outs[3 * i + 2][...] = v

    flat = [a for p in params for a in p]
    out_shape = [jax.ShapeDtypeStruct(p[0].shape, F32) for p in params for _ in range(3)]
    outs = pl.pallas_call(
        body, name="adamw_small", out_shape=out_shape, in_specs=[VMEM] * (4 * n), out_specs=[VMEM] * (3 * n),
    )(*flat)
    return [tuple(outs[3 * i:3 * i + 3]) for i in range(n)]


def _pack_small(parts):
    rows = []
    for a in parts:
        r, c = a.shape
        rows.append(jnp.pad(a, ((0, (-r) % 8), (0, CHUNK - c))))
    flat = jnp.concatenate(rows, axis=0)
    return jnp.pad(flat, ((0, N_DEV * SMALL_ROWS - flat.shape[0]), (0, 0))).reshape(N_DEV, SMALL_ROWS, CHUNK)


def _unpack_small(flat, shapes):
    out, at = [], 0
    for r, c in shapes:
        out.append(flat[at:at + r, :c])
        at += r + (-r) % 8
    return out


def kernel(x, w_in_a, sgu_ln_g, sgu_ln_b, w_spatial, b_spatial, w_out_a, w_kv, w_in_b, attn_sinks, rel_bias, w_out_b, post_ln_g, post_ln_b, loss_target, m_w_in_a, m_sgu_ln_g, m_sgu_ln_b, m_w_spatial, m_b_spatial, m_w_out_a, m_w_kv, m_w_in_b, m_attn_sinks, m_rel_bias, m_w_out_b, m_post_ln_g, m_post_ln_b, v_w_in_a, v_sgu_ln_g, v_sgu_ln_b, v_w_spatial, v_b_spatial, v_w_out_a, v_w_kv, v_w_in_b, v_attn_sinks, v_rel_bias, v_w_out_b, v_post_ln_g, v_post_ln_b):
    seq = x.shape[1]
    me = 4 * lax.axis_index("x") + 2 * lax.axis_index("y") + lax.axis_index("c")
    bucket = jnp.asarray(_band_tables())

    ln_shard = jnp.concatenate([sgu_ln_g, sgu_ln_b], axis=0)
    wa, woa, wkv, wb, wob, ln_all = _gather_weights(w_in_a[0], w_out_a[0], w_kv, w_in_b[0], w_out_b[0], ln_shard)
    woa = woa.reshape(A_WIDTH, D_MODEL)
    wkv = wkv.reshape(D_MODEL, 2 * KV_WIDTH)
    wob = wob.reshape(B_WIDTH, D_MODEL)
    ln_gb = jnp.transpose(ln_all, (1, 0, 2)).reshape(2, A_WIDTH)
    ws = w_spatial[0]
    bsp_t = jnp.pad(b_spatial[0].T, ((0, 0), (0, CHUNK - A_GROUPS)))
    bias = _bias_table(rel_bias, bucket)

    x2, tgt = x[0], loss_target[0]
    p, r0, xb = _fwd_a(x2, wa, ln_gb, ws, bsp_t, woa)
    q, z, kv, dr1, loss, dg1, db1 = _fwd_b(r0, tgt, wb, wkv, wob, post_ln_g, post_ln_b, attn_sinks, bias)
    dh1, dwb, dwkv, dwob, dsink, dlog = _bwd_b(r0, dr1, q, z, kv, wb, wkv, wob, post_ln_g, post_ln_b, attn_sinks,
                                               bias)
    drel = _rel_bias_grad(dlog, bucket)
    dp, dr0, dwoa, dws, dbsp_t, dln, dpost0 = _bwd_a(dh1, r0, p, ln_gb, ws, bsp_t, woa, post_ln_g)
    dx = _input_grad(dp, dr0, wa)
    dwa = _in_proj_weight_grad(xb, dp, wa.shape[2])

    small_shapes = [(A_GROUPS * CHUNK, CHUNK), (A_GROUPS, CHUNK), (1, N_Q_HEADS), (REL_BUCKETS, N_Q_HEADS),
                    (16, CHUNK), (16, CHUNK), (16, CHUNK), (16, CHUNK)]
    small = _pack_small([
        dws.reshape(A_GROUPS * CHUNK, CHUNK), dbsp_t[:, :A_GROUPS].T, dsink[:, :N_Q_HEADS], drel[:, :N_Q_HEADS],
        jnp.concatenate([dpost0[0:1], dg1], axis=0).reshape(16, CHUNK),
        jnp.concatenate([dpost0[1:2], db1], axis=0).reshape(16, CHUNK),
        dln[0].reshape(16, CHUNK), dln[1].reshape(16, CHUNK)])
    full = [dwa, dwoa.reshape(N_DEV, A_WIDTH // N_DEV, D_MODEL), dwkv.reshape(N_DEV, D_MODEL // N_DEV, 2 * KV_WIDTH),
            dwb, dwob.reshape(N_DEV, B_WIDTH // N_DEV, D_MODEL), small]
    wire = [BF16] * 5 + [F32]
    place = jnp.stack([2 * lax.axis_index("x") + lax.axis_index("y"), lax.axis_index("c")]).astype(jnp.int32)
    full = [g.reshape((4, 2) + g.shape[1:]) for g in full]
    landed = _pair_exchange(full)
    got = _chip_exchange([_pair_sum(place, g, l, dt) for g, l, dt in zip(full, landed, wire)])

    big = [(w_in_a, m_w_in_a, v_w_in_a), (w_out_a, m_w_out_a, v_w_out_a), (w_kv, m_w_kv, v_w_kv),
           (w_in_b, m_w_in_b, v_w_in_b), (w_out_b, m_w_out_b, v_w_out_b)]
    big_out = []
    for i, (w, m, v) in enumerate(big):
        shape2 = got[i].shape[1:]
        res = _adamw_shard(place, full[i], landed[i], got[i], w.reshape(shape2), m.reshape(shape2),
                           v.reshape(shape2))
        big_out.append([r.reshape(w.shape) for r in res])

    small_all = _gather_small(full[5], landed[5], got[5]).reshape(N_DEV * SMALL_ROWS, CHUNK)
    g_ws, g_bsp, g_sink, g_rel, g_pg, g_pb, g_lng, g_lnb = _unpack_small(small_all, small_shapes)
    g_ws = g_ws.reshape(w_spatial.shape)
    g_bsp = g_bsp.reshape(b_spatial.shape)
    g_pg = g_pg.reshape(post_ln_g.shape)
    g_pb = g_pb.reshape(post_ln_b.shape)
    g_lng = lax.dynamic_slice(g_lng.reshape(1, A_WIDTH), (0, me * sgu_ln_g.shape[1]), sgu_ln_g.shape)
    g_lnb = lax.dynamic_slice(g_lnb.reshape(1, A_WIDTH), (0, me * sgu_ln_b.shape[1]), sgu_ln_b.shape)
    small_params = [(sgu_ln_g, g_lng, m_sgu_ln_g, v_sgu_ln_g), (sgu_ln_b, g_lnb, m_sgu_ln_b, v_sgu_ln_b),
                    (w_spatial, g_ws, m_w_spatial, v_w_spatial), (b_spatial, g_bsp, m_b_spatial, v_b_spatial),
                    (attn_sinks, g_sink, m_attn_sinks, v_attn_sinks), (rel_bias, g_rel, m_rel_bias, v_rel_bias),
                    (post_ln_g, g_pg, m_post_ln_g, v_post_ln_g), (post_ln_b, g_pb, m_post_ln_b, v_post_ln_b)]
    flat2 = [tuple(a.reshape(-1, a.shape[-1]) for a in p) for p in small_params]
    small_out = [tuple(r.reshape(p[0].shape) for r in res) for res, p in zip(_adamw_small(flat2), small_params)]

    loss_all = lax.psum(loss[0, 0], ("x", "y", "c"))
    order = [("big", 0), ("small", 0), ("small", 1), ("small", 2), ("small", 3), ("big", 1), ("big", 2), ("big", 3),
             ("small", 4), ("small", 5), ("big", 4), ("small", 6), ("small", 7)]
    grads, deltas, new_m, new_v = [], [], [], []
    for kind, i in order:
        if kind == "big":
            g, d, m, v = big_out[i]
        else:
            g = small_params[i][1]
            d, m, v = small_out[i]
        grads.append(g)
        deltas.append(d)
        new_m.append(m)
        new_v.append(v)
    return (loss_all, dx.reshape(x.shape), *grads, *deltas, *new_m, *new_v)
```

```python
import functools

import jax
import jax.numpy as jnp
import numpy as np
from jax import lax
from jax.experimental import pallas as pl
from jax.experimental.pallas import tpu as pltpu

F32 = jnp.float32
BF16 = jnp.bfloat16

D_MODEL = 1024
A_WIDTH = 2048
A_GROUPS = 8
A_GROUP_DIM = A_WIDTH // A_GROUPS
CHUNK = 128
HEAD_DIM = 64
N_Q_HEADS = 16
N_KV_HEADS = 2
PAIRS_PER_KV = N_Q_HEADS // N_KV_HEADS // 2
N_PAIRS = N_Q_HEADS // 2
B_WIDTH = N_Q_HEADS * HEAD_DIM
KV_WIDTH = N_KV_HEADS * HEAD_DIM
REL_BUCKETS = 32
REL_MAX_DIST = 128
DEPTH = 2
ALPHA = (2.0 * DEPTH) ** 0.25
LN_EPS = 1e-5
NEG_INF = -1e30
SCALE = HEAD_DIM ** -0.5

ADAM_LR = 0.001
ADAM_B1 = 0.9
ADAM_B2 = 0.999
ADAM_EPS = 1e-08
ADAM_WD = 0.01
ADAM_STEP = 10

N_DEV = 8
MESH = pl.DeviceIdType.MESH
TOKENS_IN_PROJ = 512
TOKENS_A = 256
TOKENS_B = 512
TOKENS_DX = 512
VMEM_LIMIT_BYTES = 60 * 1024 * 1024
SMALL_ROWS = 144

ANY = pl.BlockSpec(memory_space=pl.ANY)
VMEM = pl.BlockSpec(memory_space=pltpu.VMEM)
SMEM = pl.BlockSpec(memory_space=pltpu.SMEM)


def _nn(a, b):
    return jnp.dot(a, b, preferred_element_type=F32)


def _nt(a, b):
    return lax.dot_general(a, b, (((1,), (1,)), ((), ())), preferred_element_type=F32)


def _tn(a, b):
    return lax.dot_general(a, b, (((0,), (0,)), ((), ())), preferred_element_type=F32)


def _resident(shape):
    nd = len(shape)
    return pl.BlockSpec(shape, lambda i: (0,) * nd, pipeline_mode=pl.Buffered(1))


def _rows(tm, width, reverse_of=None):
    if reverse_of is None:
        return pl.BlockSpec((tm, width), lambda i: (i, 0))
    return pl.BlockSpec((tm, width), lambda i: (reverse_of - 1 - i, 0))


def _seq_params():
    return pltpu.CompilerParams(dimension_semantics=("arbitrary",), vmem_limit_bytes=VMEM_LIMIT_BYTES)


def _ln_normalize(r):
    mu = jnp.mean(r, axis=-1, keepdims=True)
    xc = r - mu
    var = jnp.mean(xc * xc, axis=-1, keepdims=True)
    rstd = lax.rsqrt(var + LN_EPS)
    return xc * rstd, rstd


def _ln_backward(dxhat, xhat, rstd):
    m1 = jnp.mean(dxhat, axis=-1, keepdims=True)
    m2 = jnp.mean(dxhat * xhat, axis=-1, keepdims=True)
    return rstd * (dxhat - m1 - xhat * m2)


def _silu_and_grad(z):
    sig = jax.nn.sigmoid(z)
    return z * sig, sig * (1.0 + z * (1.0 - sig))


def _band_tables():
    t = np.arange(CHUNK, dtype=np.int32)[:, None]
    j = np.arange(2 * CHUNK, dtype=np.int32)[None, :]
    d = t + CHUNK - j
    in_window = (d >= 0) & (d < CHUNK)
    dc = np.clip(d, 0, REL_MAX_DIST - 1)
    max_exact = REL_BUCKETS // 2
    df = np.maximum(dc, 1).astype(np.float32)
    large = max_exact + (np.log(df / np.float32(max_exact)) / np.float32(np.log(REL_MAX_DIST / max_exact))
                         * np.float32(REL_BUCKETS - max_exact)).astype(np.int32)
    large = np.minimum(large, REL_BUCKETS - 1)
    bucket = np.where(dc < max_exact, dc, large)
    return np.where(in_window, bucket, -1).astype(np.int32)


def _use_order():
    x, y, c = lax.axis_index("x"), lax.axis_index("y"), lax.axis_index("c")
    blocks = [(x, y, c), (x, y, 1 - c)]
    for px, py in [(1 - x, y), (x, 1 - y), (1 - x, 1 - y)]:
        blocks += [(px, py, c), (px, py, 1 - c)]
    return blocks


ARRIVAL_OF_PHASE = (None, 0, 1, 4, 2, 5, 3, 6)


def _gather_in_proj(order, x, w_in_a, w_out_a, w_kv, w_in_b, w_out_b, ln_gb):
    seq = x.shape[0]
    tm = min(TOKENS_IN_PROJ, seq)
    n_blocks = seq // tm
    shard_cols = w_in_a.shape[1]
    small = (w_out_a, w_kv, w_in_b, w_out_b)
    n = len(small) + 2

    def body(order_ref, x_ref, *refs):
        ins, refs = refs[:n], refs[n:]
        p_ref, outs, refs = refs[0], refs[1:n + 1], refs[n + 1:]
        wa_vmem, stage = refs[0], refs[1:n - 1]
        send_sems, recv_sems, local_sems, out_sems = refs[n - 1:]
        phase, blk = pl.program_id(0), pl.program_id(1)
        x_, y_, c_ = lax.axis_index("x"), lax.axis_index("y"), lax.axis_index("c")
        me, sibling = (x_, y_, c_), (x_, y_, 1 - c_)
        chips = [(1 - x_, y_), (x_, 1 - y_), (1 - x_, 1 - y_)]
        use = _use_order()

        def index(px, py, pc):
            return 4 * px + 2 * py + pc

        def slot(a, block):
            return (wa_vmem if a == 0 else outs[a]).at[index(*block)]

        def own(a):
            return slot(0, me) if a == 0 else (ins[a] if a == n - 1 else stage[a - 1])

        def copy(a, k, block, to, src=None):
            return pltpu.make_async_remote_copy(
                src_ref=slot(a, block) if src is None else src, dst_ref=slot(a, block),
                send_sem=send_sems.at[a, k], recv_sem=recv_sems.at[a, k], device_id=to, device_id_type=MESH)

        def keep(a):
            return pltpu.make_async_copy(own(a), slot(a, me), local_sems.at[a])

        def write_out(k):
            return pltpu.make_async_copy(slot(0, use[k]), outs[0].at[index(*use[k])], out_sems.at[k])

        def pass_on(a, j):
            copy(a, 1 + j, (*chips[j], c_), me).wait_recv()
            copy(a, 4 + j, (*chips[j], c_), sibling).start()

        @pl.when((phase == 0) & (blk == 0))
        def _():
            wa_vmem[index(*me)] = ins[0][...].astype(BF16)
            for a in range(1, n - 1):
                stage[a - 1][...] = ins[a][...].astype(BF16)
            for a in range(n):
                copy(a, 0, me, sibling, own(a)).start()
                for j, chip in enumerate(chips):
                    copy(a, 1 + j, me, (*chip, c_), own(a)).start()
                if a:
                    keep(a).start()
            write_out(0).start()

        for k in range(1, N_DEV):
            @pl.when((phase == k) & (blk == 0))
            def _(k=k):
                if k % 2 == 0:
                    pass_on(0, k // 2 - 1)
                else:
                    copy(0, ARRIVAL_OF_PHASE[k], use[k], me).wait_recv()
                write_out(k).start()
                if k == N_DEV - 2:
                    for a in range(1, n):
                        pass_on(a, 0)
                        pass_on(a, 1)

        p_ref[...] = _nn(x_ref[...].astype(BF16), wa_vmem[order_ref[phase]])

        @pl.when((phase == N_DEV - 1) & (blk == n_blocks - 1))
        def _():
            for a in range(1, n):
                pass_on(a, 2)
            for a in range(1, n):
                copy(a, 0, sibling, me).wait_recv()
                for j, chip in enumerate(chips):
                    copy(a, 4 + j, (*chip, 1 - c_), me).wait_recv()
            for a in range(n):
                for k in range(7):
                    copy(a, k, me, me).wait_send()
                if a:
                    keep(a).wait()
            for k in range(N_DEV):
                write_out(k).wait()

    gathered = [jax.ShapeDtypeStruct((N_DEV,) + w_in_a.shape, BF16)]
    gathered += [jax.ShapeDtypeStruct((N_DEV,) + s.shape, BF16) for s in small]
    gathered.append(jax.ShapeDtypeStruct((N_DEV,) + ln_gb.shape, F32))
    return pl.pallas_call(
        body, name="gather_in_proj",
        out_shape=[jax.ShapeDtypeStruct((seq, N_DEV * shard_cols), F32)] + gathered,
        grid_spec=pltpu.PrefetchScalarGridSpec(
            num_scalar_prefetch=1, grid=(N_DEV, n_blocks),
            in_specs=[pl.BlockSpec((tm, D_MODEL), lambda k, i, order: (i, 0))] + [VMEM] * n,
            out_specs=[pl.BlockSpec((tm, shard_cols), lambda k, i, order: (i, order[k]))] + [ANY] * n,
            scratch_shapes=[pltpu.VMEM((N_DEV,) + w_in_a.shape, BF16)] + [pltpu.VMEM(s.shape, BF16) for s in small] + [
                pltpu.SemaphoreType.DMA((n, 7)), pltpu.SemaphoreType.DMA((n, 7)), pltpu.SemaphoreType.DMA((n,)),
                pltpu.SemaphoreType.DMA((N_DEV,))]),
        compiler_params=pltpu.CompilerParams(dimension_semantics=("arbitrary", "arbitrary"),
                                             vmem_limit_bytes=VMEM_LIMIT_BYTES),
    )(order, x, w_in_a, *small, ln_gb)


def _bias_table(rel_bias, bucket):
    def body(rel_ref, bucket_ref, out_ref):
        bk = bucket_ref[...]
        col = lax.broadcasted_iota(jnp.int32, bk.shape, 1)
        for h in range(N_Q_HEADS):
            def step(b, acc):
                return jnp.where(bk == b, rel_ref[b, h], acc)
            tab = lax.fori_loop(0, REL_BUCKETS, step, jnp.full(bk.shape, NEG_INF, F32))
            pair = h // 2
            rows = slice((pair % PAIRS_PER_KV) * CHUNK, (pair % PAIRS_PER_KV + 1) * CHUNK)
            half = slice((h % 2) * 2 * CHUNK, (h % 2 + 1) * 2 * CHUNK)
            out_ref[1, pair // PAIRS_PER_KV, rows, half] = tab
            out_ref[0, pair // PAIRS_PER_KV, rows, half] = jnp.where(col < CHUNK, NEG_INF, tab)

    return pl.pallas_call(
        body, name="bias_table",
        out_shape=jax.ShapeDtypeStruct((2, N_KV_HEADS, PAIRS_PER_KV * CHUNK, 4 * CHUNK), F32),
        in_specs=[SMEM, VMEM], out_specs=VMEM,
    )(rel_bias, bucket)


def _rel_bias_grad(dlogits, bucket):
    def body(dl_ref, bucket_ref, out_ref):
        bk = bucket_ref[...]
        row = lax.broadcasted_iota(jnp.int32, (REL_BUCKETS, CHUNK), 0)
        lane = lax.broadcasted_iota(jnp.int32, (REL_BUCKETS, CHUNK), 1)

        def step(b, acc):
            for h in range(N_Q_HEADS):
                pair = h // 2
                dl = dl_ref[pair // PAIRS_PER_KV, (pair % PAIRS_PER_KV) * CHUNK:(pair % PAIRS_PER_KV + 1) * CHUNK,
                            (h % 2) * 2 * CHUNK:(h % 2 + 1) * 2 * CHUNK]
                s = jnp.sum(jnp.where(bk == b, dl, 0.0))
                acc = jnp.where((row == b) & (lane == h), s, acc)
            return acc

        out_ref[...] = lax.fori_loop(0, REL_BUCKETS, step, jnp.zeros((REL_BUCKETS, CHUNK), F32))

    return pl.pallas_call(
        body, name="rel_bias_grad", out_shape=jax.ShapeDtypeStruct((REL_BUCKETS, CHUNK), F32),
        in_specs=[VMEM, VMEM], out_specs=VMEM,
    )(dlogits, bucket)


def _masked_spatial(ws_ref, g):
    t = lax.broadcasted_iota(jnp.int32, (CHUNK, CHUNK), 0)
    s = lax.broadcasted_iota(jnp.int32, (CHUNK, CHUNK), 1)
    return jnp.where(s <= t, ws_ref[g], 0.0).astype(BF16)


def _spatial_gate(ws_ref, bsp_t_ref, vn, s_ref, n_chunks):
    for g in range(A_GROUPS):
        w = _masked_spatial(ws_ref, g)
        cols = slice(g * A_GROUP_DIM, (g + 1) * A_GROUP_DIM)
        for ci in range(n_chunks):
            rows = slice(ci * CHUNK, (ci + 1) * CHUNK)
            s_ref[rows, cols] = _nn(w, vn[rows, cols]) + bsp_t_ref[:, g:g + 1]


def _fwd_a(x, p, ln_gb, w_spatial, bsp_t, woa):
    seq = x.shape[0]
    tm = min(TOKENS_A, seq)
    n_chunks = tm // CHUNK

    def body(x_ref, p_ref, ln_ref, ws_ref, bsp_ref, woa_ref, r0_ref, xb_ref, s_ref):
        xv = x_ref[...]
        xb_ref[...] = xv.astype(BF16)
        vhat, _ = _ln_normalize(p_ref[:, A_WIDTH:2 * A_WIDTH])
        vn = (vhat * ln_ref[0:1, :] + ln_ref[1:2, :]).astype(BF16)
        _spatial_gate(ws_ref, bsp_ref, vn, s_ref, n_chunks)
        z = p_ref[:, 2 * A_WIDTH:]
        y = p_ref[:, :A_WIDTH] * s_ref[...] * (z * jax.nn.sigmoid(z))
        r0_ref[...] = ALPHA * xv + _nn(y.astype(BF16), woa_ref[...])

    return pl.pallas_call(
        body, name="fwd_a", grid=(seq // tm,),
        out_shape=[jax.ShapeDtypeStruct((seq, D_MODEL), F32), jax.ShapeDtypeStruct((seq, D_MODEL), BF16)],
        in_specs=[_rows(tm, D_MODEL), _rows(tm, 3 * A_WIDTH), _resident(ln_gb.shape), _resident(w_spatial.shape),
                  _resident(bsp_t.shape), _resident(woa.shape)],
        out_specs=[_rows(tm, D_MODEL), _rows(tm, D_MODEL)],
        scratch_shapes=[pltpu.VMEM((tm, A_WIDTH), F32)],
        compiler_params=_seq_params(),
    )(x, p, ln_gb, w_spatial, bsp_t, woa)


def _expand_heads(kv):
    lo = lax.broadcasted_iota(jnp.int32, kv.shape, 1) < HEAD_DIM
    swapped = pltpu.roll(kv, HEAD_DIM, 1)
    zero = jnp.zeros_like(kv)
    return (jnp.where(lo, kv, zero).astype(BF16), jnp.where(lo, zero, swapped).astype(BF16),
            jnp.where(lo, swapped, zero).astype(BF16), jnp.where(lo, zero, kv).astype(BF16))


def _fold_heads(acc, kvh):
    top, bot = acc[:2 * CHUNK], acc[2 * CHUNK:]
    lo = lax.broadcasted_iota(jnp.int32, top.shape, 1) < HEAD_DIM
    t = jnp.where(lo, top, 0.0)
    b = jnp.where(lo, 0.0, bot)
    if kvh == 0:
        return t + pltpu.roll(b, HEAD_DIM, 1)
    return pltpu.roll(t, HEAD_DIM, 1) + b


STACK = PAIRS_PER_KV * CHUNK


def _pair_cols(kvh, pp):
    pair = kvh * PAIRS_PER_KV + pp
    return slice(pair * CHUNK, (pair + 1) * CHUNK)


def _stack_pairs(ref_or_val, rows, kvh):
    return jnp.concatenate([ref_or_val[rows, _pair_cols(kvh, pp)] for pp in range(PAIRS_PER_KV)], axis=0)


def _sink_columns(sink_ref):
    blk = lax.broadcasted_iota(jnp.int32, (STACK, 1), 0) // CHUNK
    out = []
    for kvh in range(N_KV_HEADS):
        halves = []
        for half in range(2):
            col = jnp.zeros((STACK, 1), F32)
            for pp in range(PAIRS_PER_KV):
                col = jnp.where(blk == pp, sink_ref[0, 2 * (kvh * PAIRS_PER_KV + pp) + half], col)
            halves.append(col)
        out.append(halves)
    return out


def _softmax_stack(logits, sink_cols):
    ps, sinks = [], []
    for half in range(2):
        lg = logits[:, half * 2 * CHUNK:(half + 1) * 2 * CHUNK]
        sk = sink_cols[half]
        m = jnp.maximum(jnp.max(lg, axis=-1, keepdims=True), sk)
        e = jnp.exp(lg - m)
        es = jnp.exp(sk - m)
        inv = 1.0 / (jnp.sum(e, axis=-1, keepdims=True) + es)
        ps.append(e * inv)
        sinks.append(es * inv)
    return jnp.concatenate(ps, axis=1), sinks


def _band_operands(ek_ref, ev_ref, ci, kvh):
    kblk = jnp.concatenate([ek_ref[ci, 2 * kvh], ek_ref[ci + 1, 2 * kvh],
                            ek_ref[ci, 2 * kvh + 1], ek_ref[ci + 1, 2 * kvh + 1]], axis=0)
    vblk = jnp.concatenate([ev_ref[ci, 2 * kvh], ev_ref[ci + 1, 2 * kvh],
                            ev_ref[ci, 2 * kvh + 1], ev_ref[ci + 1, 2 * kvh + 1]], axis=0)
    return kblk, vblk


def _fwd_b(r0, target, wb, wkv, wob, post_g, post_b, sinks, bias):
    seq = r0.shape[0]
    tm = min(TOKENS_B, seq)
    n_chunks = tm // CHUNK
    shard_cols = wb.shape[2]

    def body(r0_ref, tgt_ref, wb_ref, wkv_ref, wob_ref, pg_ref, pb_ref, sink_ref, bias_ref,
             q_ref, z_ref, kv_ref, dr1_ref, loss_ref, dg_ref, db_ref, ek_ref, ev_ref, o_ref):
        step = pl.program_id(0)

        @pl.when(step == 0)
        def _():
            ek_ref[0] = jnp.zeros(ek_ref.shape[1:], BF16)
            ev_ref[0] = jnp.zeros(ev_ref.shape[1:], BF16)
            loss_ref[...] = jnp.zeros_like(loss_ref)
            dg_ref[...] = jnp.zeros_like(dg_ref)
            db_ref[...] = jnp.zeros_like(db_ref)

        @pl.when(step > 0)
        def _():
            ek_ref[0] = ek_ref[n_chunks]
            ev_ref[0] = ev_ref[n_chunks]

        xhat0, _ = _ln_normalize(r0_ref[...])
        h1 = xhat0 * pg_ref[0:1, :] + pb_ref[0:1, :]
        h1b = h1.astype(BF16)
        for k in range(N_DEV):
            qz = _nn(h1b, wb_ref[k])
            lo, hi = k * shard_cols, (k + 1) * shard_cols
            if hi <= B_WIDTH:
                q_ref[:, lo:hi] = (qz * SCALE).astype(BF16)
            else:
                z_ref[:, lo - B_WIDTH:hi - B_WIDTH] = qz
        kv = _nn(h1b, wkv_ref[...])
        kv_ref[...] = kv.astype(BF16)
        for ci in range(n_chunks):
            rows = slice(ci * CHUNK, (ci + 1) * CHUNK)
            for i, e in enumerate(_expand_heads(kv[rows, :KV_WIDTH])):
                ek_ref[ci + 1, i] = e
            for i, e in enumerate(_expand_heads(kv[rows, KV_WIDTH:])):
                ev_ref[ci + 1, i] = e

        sink_cols = _sink_columns(sink_ref)
        for ci in range(n_chunks):
            rows = slice(ci * CHUNK, (ci + 1) * CHUNK)
            sel = jnp.where(step * n_chunks + ci > 0, 1, 0)
            for kvh in range(N_KV_HEADS):
                kblk, vblk = _band_operands(ek_ref, ev_ref, ci, kvh)
                logits = _nt(_stack_pairs(q_ref, rows, kvh), kblk) + bias_ref[sel, kvh]
                p, _ = _softmax_stack(logits, sink_cols[kvh])
                o = _nn(p.astype(BF16), vblk)
                for pp in range(PAIRS_PER_KV):
                    o_ref[rows, _pair_cols(kvh, pp)] = o[pp * CHUNK:(pp + 1) * CHUNK]

        z = z_ref[...]
        y = o_ref[...] * (z * jax.nn.sigmoid(z))
        r1 = ALPHA * h1 + _nn(y.astype(BF16), wob_ref[...])
        xhat1, rstd1 = _ln_normalize(r1)
        g1 = pg_ref[1:2, :]
        diff = xhat1 * g1 + pb_ref[1:2, :] - tgt_ref[...]
        loss_ref[...] += 0.5 * jnp.sum(jnp.mean(diff * diff, axis=-1, keepdims=True), axis=0, keepdims=True)
        dh2 = diff * (1.0 / D_MODEL)
        dg_ref[...] += jnp.sum(dh2 * xhat1, axis=0, keepdims=True)
        db_ref[...] += jnp.sum(dh2, axis=0, keepdims=True)
        dr1_ref[...] = _ln_backward(dh2 * g1, xhat1, rstd1)

    vec = pl.BlockSpec((1, D_MODEL), lambda i: (0, 0))
    return pl.pallas_call(
        body, name="fwd_b", grid=(seq // tm,),
        out_shape=[jax.ShapeDtypeStruct((seq, B_WIDTH), BF16), jax.ShapeDtypeStruct((seq, B_WIDTH), F32),
                   jax.ShapeDtypeStruct((seq, 2 * KV_WIDTH), BF16), jax.ShapeDtypeStruct((seq, D_MODEL), F32),
                   jax.ShapeDtypeStruct((1, 1), F32), jax.ShapeDtypeStruct((1, D_MODEL), F32),
                   jax.ShapeDtypeStruct((1, D_MODEL), F32)],
        in_specs=[_rows(tm, D_MODEL), _rows(tm, D_MODEL), _resident(wb.shape), _resident(wkv.shape),
                  _resident(wob.shape), _resident(post_g.shape), _resident(post_b.shape), SMEM,
                  _resident(bias.shape)],
        out_specs=[_rows(tm, B_WIDTH), _rows(tm, B_WIDTH), _rows(tm, 2 * KV_WIDTH), _rows(tm, D_MODEL),
                   pl.BlockSpec((1, 1), lambda i: (0, 0)), vec, vec],
        scratch_shapes=[pltpu.VMEM((n_chunks + 1, 4, CHUNK, CHUNK), BF16),
                        pltpu.VMEM((n_chunks + 1, 4, CHUNK, CHUNK), BF16),
                        pltpu.VMEM((tm, B_WIDTH), F32)],
        compiler_params=_seq_params(),
    )(r0, target, wb, wkv, wob, post_g, post_b, sinks, bias)


def _bwd_b(r0, dr1, q, z, kv, wb, wkv, wob, post_g, post_b, sinks, bias):
    seq = r0.shape[0]
    tm = min(TOKENS_B, seq)
    n_chunks = tm // CHUNK
    n_blocks = seq // tm
    shard_cols = wb.shape[2]

    def body(r0_ref, dr1_ref, q_ref, z_ref, kv_ref, kvp_ref, wb_ref, wkv_ref, wob_ref, pg_ref, pb_ref, sink_ref,
             bias_ref, dh1_ref, dwb_out, dwkv_out, dwob_out, dsink_ref, dl_out,
             dwb_acc, dwkv_acc, dwob_acc, dl_acc, dsink_acc, carry_ref, ek_ref, ev_ref, y_ref, dqz_ref, dkv_ref):
        step = pl.program_id(0)
        block = n_blocks - 1 - step

        @pl.when(step == 0)
        def _():
            dwb_acc[...] = jnp.zeros_like(dwb_acc)
            dwkv_acc[...] = jnp.zeros_like(dwkv_acc)
            dwob_acc[...] = jnp.zeros_like(dwob_acc)
            dl_acc[...] = jnp.zeros_like(dl_acc)
            carry_ref[...] = jnp.zeros_like(carry_ref)
            dsink_acc[...] = jnp.zeros_like(dsink_acc)

        xhat0, _ = _ln_normalize(r0_ref[...])
        h1b = (xhat0 * pg_ref[0:1, :] + pb_ref[0:1, :]).astype(BF16)
        dr1 = dr1_ref[...]
        dr1b = dr1.astype(BF16)
        dy = _nt(dr1b, wob_ref[...])
        sg, dsg = _silu_and_grad(z_ref[...])
        do_all = dy * sg
        dyz = dy * dsg

        for ci in range(n_chunks + 1):
            src = kvp_ref[...] if ci == 0 else kv_ref[(ci - 1) * CHUNK:ci * CHUNK, :]
            src = src.astype(F32)
            for i, e in enumerate(_expand_heads(src[:, :KV_WIDTH])):
                ek_ref[ci, i] = e
            for i, e in enumerate(_expand_heads(src[:, KV_WIDTH:])):
                ev_ref[ci, i] = e

        sink_cols = _sink_columns(sink_ref)
        for ci in reversed(range(n_chunks)):
            rows = slice(ci * CHUNK, (ci + 1) * CHUNK)
            sel = jnp.where(block * n_chunks + ci > 0, 1, 0)
            dk_band = jnp.zeros((2 * CHUNK, KV_WIDTH), F32)
            dv_band = jnp.zeros((2 * CHUNK, KV_WIDTH), F32)
            for kvh in range(N_KV_HEADS):
                kblk, vblk = _band_operands(ek_ref, ev_ref, ci, kvh)
                qs = _stack_pairs(q_ref, rows, kvh)
                p, psink = _softmax_stack(_nt(qs, kblk) + bias_ref[sel, kvh], sink_cols[kvh])
                pb16 = p.astype(BF16)
                o = _nn(pb16, vblk)
                y = (o * _stack_pairs(sg, rows, kvh)).astype(BF16)
                dz = (o * _stack_pairs(dyz, rows, kvh)).astype(BF16)
                dob = _stack_pairs(do_all, rows, kvh).astype(BF16)
                dp = _nt(dob, vblk)
                dlogits = []
                for half in range(2):
                    hs = slice(half * 2 * CHUNK, (half + 1) * 2 * CHUNK)
                    delta = jnp.sum(dp[:, hs] * p[:, hs], axis=-1, keepdims=True)
                    dlogits.append(p[:, hs] * (dp[:, hs] - delta))
                    dsink_acc[2 * kvh + half] += psink[half] * delta
                dlog = jnp.concatenate(dlogits, axis=1)
                dl_acc[kvh] += dlog
                dsc = dlog.astype(BF16)
                dq = (_nn(dsc, kblk) * SCALE).astype(BF16)
                for pp in range(PAIRS_PER_KV):
                    blk = slice(pp * CHUNK, (pp + 1) * CHUNK)
                    cols = _pair_cols(kvh, pp)
                    y_ref[rows, cols] = y[blk]
                    dqz_ref[rows, cols] = dq[blk]
                    dqz_ref[rows, B_WIDTH + cols.start:B_WIDTH + cols.stop] = dz[blk]
                dk_band = dk_band + _fold_heads(_tn(dsc, qs), kvh)
                dv_band = dv_band + _fold_heads(_tn(pb16, dob), kvh)
            band = jnp.concatenate([dk_band, dv_band], axis=1)
            dkv_ref[rows, :] = (band[CHUNK:] + carry_ref[...]).astype(BF16)
            carry_ref[...] = band[:CHUNK]

        dqz = dqz_ref[...]
        dkv = dkv_ref[...]
        dh1 = ALPHA * dr1 + _nt(dkv, wkv_ref[...])
        for k in range(N_DEV):
            part = dqz[:, k * shard_cols:(k + 1) * shard_cols]
            dh1 = dh1 + _nt(part, wb_ref[k])
            dwb_acc[k] += _tn(h1b, part)
        dh1_ref[...] = dh1
        dwkv_acc[...] += _tn(h1b, dkv)
        dwob_acc[...] += _tn(y_ref[...], dr1b)

        @pl.when(step == n_blocks - 1)
        def _():
            pltpu.sync_copy(dwb_acc, dwb_out)
            pltpu.sync_copy(dwkv_acc, dwkv_out)
            pltpu.sync_copy(dwob_acc, dwob_out)
            pltpu.sync_copy(dl_acc, dl_out)
            lane = lax.broadcasted_iota(jnp.int32, (1, CHUNK), 1)
            dsink = jnp.zeros((1, CHUNK), F32)
            for kvh in range(N_KV_HEADS):
                for half in range(2):
                    for pp in range(PAIRS_PER_KV):
                        head = 2 * (kvh * PAIRS_PER_KV + pp) + half
                        tot = jnp.sum(dsink_acc[2 * kvh + half, pp * CHUNK:(pp + 1) * CHUNK, :], axis=0, keepdims=True)
                        dsink = dsink - jnp.where(lane == head, tot, 0.0)
            dsink_ref[...] = dsink

    rev = functools.partial(_rows, tm, reverse_of=n_blocks)
    prev_chunk = pl.BlockSpec(
        (CHUNK, 2 * KV_WIDTH), lambda i: (jnp.maximum((n_blocks - 1 - i) * n_chunks - 1, 0), 0))
    return pl.pallas_call(
        body, name="bwd_b", grid=(n_blocks,),
        out_shape=[jax.ShapeDtypeStruct((seq, D_MODEL), F32), jax.ShapeDtypeStruct(wb.shape, F32),
                   jax.ShapeDtypeStruct(wkv.shape, F32), jax.ShapeDtypeStruct(wob.shape, F32),
                   jax.ShapeDtypeStruct((1, CHUNK), F32),
                   jax.ShapeDtypeStruct((N_KV_HEADS, STACK, 4 * CHUNK), F32)],
        in_specs=[rev(D_MODEL), rev(D_MODEL), rev(B_WIDTH), rev(B_WIDTH), rev(2 * KV_WIDTH), prev_chunk,
                  _resident(wb.shape), _resident(wkv.shape), _resident(wob.shape), _resident(post_g.shape),
                  _resident(post_b.shape), SMEM, _resident(bias.shape)],
        out_specs=[rev(D_MODEL), ANY, ANY, ANY, pl.BlockSpec((1, CHUNK), lambda i: (0, 0)), ANY],
        scratch_shapes=[pltpu.VMEM(wb.shape, F32), pltpu.VMEM(wkv.shape, F32), pltpu.VMEM(wob.shape, F32),
                        pltpu.VMEM((N_KV_HEADS, STACK, 4 * CHUNK), F32), pltpu.VMEM((2 * N_KV_HEADS, STACK, 1), F32),
                        pltpu.VMEM((CHUNK, 2 * KV_WIDTH), F32),
                        pltpu.VMEM((n_chunks + 1, 4, CHUNK, CHUNK), BF16),
                        pltpu.VMEM((n_chunks + 1, 4, CHUNK, CHUNK), BF16),
                        pltpu.VMEM((tm, B_WIDTH), BF16), pltpu.VMEM((tm, 2 * B_WIDTH), BF16),
                        pltpu.VMEM((tm, 2 * KV_WIDTH), BF16)],
        compiler_params=_seq_params(),
    )(r0, dr1, q, z, kv, kv, wb, wkv, wob, post_g, post_b, sinks, bias)


def _bwd_a(dh1, r0, p, ln_gb, w_spatial, bsp_t, woa, post_g):
    seq = r0.shape[0]
    tm = min(TOKENS_A, seq)
    n_chunks = tm // CHUNK
    n_blocks = seq // tm

    def body(dh1_ref, r0_ref, p_ref, ln_ref, ws_ref, bsp_ref, woa_ref, pg_ref,
             dp_ref, dr0_ref, dwoa_out, dws_ref, dbsp_ref, dln_ref, dpost_ref,
             dwoa_acc, s_ref, ds_ref, dvn_ref):
        step = pl.program_id(0)

        @pl.when(step == 0)
        def _():
            dwoa_acc[...] = jnp.zeros_like(dwoa_acc)
            dws_ref[...] = jnp.zeros_like(dws_ref)
            dbsp_ref[...] = jnp.zeros_like(dbsp_ref)
            dln_ref[...] = jnp.zeros_like(dln_ref)
            dpost_ref[...] = jnp.zeros_like(dpost_ref)

        dh1 = dh1_ref[...]
        xhat0, rstd0 = _ln_normalize(r0_ref[...])
        dpost_ref[0:1, :] += jnp.sum(dh1 * xhat0, axis=0, keepdims=True)
        dpost_ref[1:2, :] += jnp.sum(dh1, axis=0, keepdims=True)
        dr0 = _ln_backward(dh1 * pg_ref[0:1, :], xhat0, rstd0)
        dr0_ref[...] = dr0
        dr0b = dr0.astype(BF16)

        vhat, rstd_v = _ln_normalize(p_ref[:, A_WIDTH:2 * A_WIDTH])
        ln_g = ln_ref[0:1, :]
        vn = (vhat * ln_g + ln_ref[1:2, :]).astype(BF16)
        _spatial_gate(ws_ref, bsp_ref, vn, s_ref, n_chunks)
        u = p_ref[:, :A_WIDTH]
        s = s_ref[...]
        sg, dsg = _silu_and_grad(p_ref[:, 2 * A_WIDTH:])
        dwoa_acc[...] += _tn((u * s * sg).astype(BF16), dr0b)
        dy = _nt(dr0b, woa_ref[...])
        dp_ref[:, :A_WIDTH] = (dy * s * sg).astype(BF16)
        dp_ref[:, 2 * A_WIDTH:] = (dy * u * s * dsg).astype(BF16)
        ds_ref[...] = dy * u * sg

        lane = lax.broadcasted_iota(jnp.int32, (CHUNK, CHUNK), 1)
        dbsp = jnp.zeros((CHUNK, CHUNK), F32)
        for g in range(A_GROUPS):
            w = _masked_spatial(ws_ref, g)
            cols = slice(g * A_GROUP_DIM, (g + 1) * A_GROUP_DIM)
            dws = jnp.zeros((CHUNK, CHUNK), F32)
            for ci in range(n_chunks):
                rows = slice(ci * CHUNK, (ci + 1) * CHUNK)
                ds = ds_ref[rows, cols]
                dsb = ds.astype(BF16)
                dvn_ref[rows, cols] = _tn(w, dsb)
                dws = dws + _nt(dsb, vn[rows, cols])
                dbsp = dbsp + jnp.where(lane == g, jnp.sum(ds, axis=-1, keepdims=True), 0.0)
            dws_ref[g] += dws
        dbsp_ref[...] += dbsp

        dvn = dvn_ref[...]
        dln_ref[0:1, :] += jnp.sum(dvn * vhat, axis=0, keepdims=True)
        dln_ref[1:2, :] += jnp.sum(dvn, axis=0, keepdims=True)
        dp_ref[:, A_WIDTH:2 * A_WIDTH] = _ln_backward(dvn * ln_g, vhat, rstd_v).astype(BF16)

        @pl.when(step == n_blocks - 1)
        def _():
            t = lax.broadcasted_iota(jnp.int32, (CHUNK, CHUNK), 0)
            for g in range(A_GROUPS):
                dws_ref[g] = jnp.where(lane <= t, dws_ref[g], 0.0)
            pltpu.sync_copy(dwoa_acc, dwoa_out)

    def fixed(shape):
        nd = len(shape)
        return pl.BlockSpec(shape, lambda i: (0,) * nd)

    return pl.pallas_call(
        body, name="bwd_a", grid=(n_blocks,),
        out_shape=[jax.ShapeDtypeStruct((seq, 3 * A_WIDTH), BF16), jax.ShapeDtypeStruct((seq, D_MODEL), F32),
                   jax.ShapeDtypeStruct(woa.shape, F32), jax.ShapeDtypeStruct(w_spatial.shape, F32),
                   jax.ShapeDtypeStruct((CHUNK, CHUNK), F32), jax.ShapeDtypeStruct((2, A_WIDTH), F32),
                   jax.ShapeDtypeStruct((2, D_MODEL), F32)],
        in_specs=[_rows(tm, D_MODEL), _rows(tm, D_MODEL), _rows(tm, 3 * A_WIDTH), _resident(ln_gb.shape),
                  _resident(w_spatial.shape), _resident(bsp_t.shape), _resident(woa.shape),
                  _resident(post_g.shape)],
        out_specs=[_rows(tm, 3 * A_WIDTH), _rows(tm, D_MODEL), ANY, fixed(w_spatial.shape),
                   fixed((CHUNK, CHUNK)), fixed((2, A_WIDTH)), fixed((2, D_MODEL))],
        scratch_shapes=[pltpu.VMEM(woa.shape, F32), pltpu.VMEM((tm, A_WIDTH), F32),
                        pltpu.VMEM((tm, A_WIDTH), F32), pltpu.VMEM((tm, A_WIDTH), F32)],
        compiler_params=_seq_params(),
    )(dh1, r0, p, ln_gb, w_spatial, bsp_t, woa, post_g)


def _input_grad(dp, dr0, wa):
    seq = dr0.shape[0]
    tm = min(TOKENS_DX, seq)
    shard_cols = wa.shape[2]

    def body(dp_ref, dr0_ref, wa_ref, dx_ref):
        dx = ALPHA * dr0_ref[...]
        for k in range(N_DEV):
            dx = dx + _nt(dp_ref[:, k * shard_cols:(k + 1) * shard_cols], wa_ref[k])
        dx_ref[...] = dx

    return pl.pallas_call(
        body, name="input_grad", grid=(seq // tm,),
        out_shape=jax.ShapeDtypeStruct((seq, D_MODEL), F32),
        in_specs=[_rows(tm, 3 * A_WIDTH), _rows(tm, D_MODEL), _resident(wa.shape)],
        out_specs=_rows(tm, D_MODEL),
        compiler_params=_seq_params(),
    )(dp, dr0, wa)


def _in_proj_weight_grad(xb, dp, shard_cols):
    seq = xb.shape[0]
    tk = min(1024, seq)
    n_k = seq // tk

    def body(x_ref, dp_ref, out_ref):
        @pl.when(pl.program_id(1) == 0)
        def _():
            out_ref[...] = jnp.zeros_like(out_ref)

        out_ref[...] += _tn(x_ref[...], dp_ref[...])

    return pl.pallas_call(
        body, name="in_proj_weight_grad", grid=(N_DEV, n_k),
        out_shape=jax.ShapeDtypeStruct((N_DEV, D_MODEL, shard_cols), F32),
        in_specs=[pl.BlockSpec((tk, D_MODEL), lambda j, k: (k, 0)),
                  pl.BlockSpec((tk, shard_cols), lambda j, k: (k, j))],
        out_specs=pl.BlockSpec((None, D_MODEL, shard_cols), lambda j, k: (j, 0, 0)),
        compiler_params=pltpu.CompilerParams(dimension_semantics=("arbitrary", "arbitrary"),
                                             vmem_limit_bytes=VMEM_LIMIT_BYTES),
    )(xb, dp)


def _pair_exchange(grads):
    n = len(grads)

    def body(*refs):
        ins, landed = refs[:n], refs[n:2 * n]
        send_sems, recv_sems = refs[2 * n:]
        x, y, c = lax.axis_index("x"), lax.axis_index("y"), lax.axis_index("c")
        copies = []
        for a in range(n):
            copies.append(pltpu.make_async_remote_copy(
                src_ref=ins[a].at[:, pl.ds(1 - c, 1)], dst_ref=landed[a],
                send_sem=send_sems.at[a], recv_sem=recv_sems.at[a], device_id=(x, y, 1 - c), device_id_type=MESH))
            copies[-1].start()
        for cp in copies:
            cp.wait()

    return pl.pallas_call(
        body, name="pair_exchange", out_shape=[jax.ShapeDtypeStruct((4, 1) + g.shape[2:], F32) for g in grads],
        in_specs=[ANY] * n, out_specs=[ANY] * n,
        scratch_shapes=[pltpu.SemaphoreType.DMA((n,)), pltpu.SemaphoreType.DMA((n,))],
    )(*grads)


def _row_tile(rows, cols, itemsize=4):
    if rows * cols * itemsize <= (1 << 20):
        return rows
    t = 8
    while 2 * t * cols * itemsize <= (1 << 20) and rows % (2 * t) == 0:
        t *= 2
    assert rows % t == 0, (rows, cols)
    return t


def _pair_sum(place, grad, landed, dtype):
    _, _, rows, cols = grad.shape
    tr = _row_tile(rows, cols)

    def body(place_ref, a_ref, b_ref, out_ref):
        out_ref[...] = (a_ref[...] + b_ref[...]).astype(dtype)

    return pl.pallas_call(
        body, name="pair_sum", out_shape=jax.ShapeDtypeStruct((4, rows, cols), dtype),
        grid_spec=pltpu.PrefetchScalarGridSpec(
            num_scalar_prefetch=1, grid=(4, rows // tr),
            in_specs=[pl.BlockSpec((None, None, tr, cols), lambda k, i, pr: (k, pr[1], i, 0)),
                      pl.BlockSpec((None, None, tr, cols), lambda k, i, pr: (k, 0, i, 0))],
            out_specs=pl.BlockSpec((None, tr, cols), lambda k, i, pr: (k, i, 0))),
        compiler_params=pltpu.CompilerParams(dimension_semantics=("parallel", "parallel")),
    )(place, grad, landed)


def _chip_exchange(sums):
    n = len(sums)

    def body(*refs):
        s_in, got = refs[:n], refs[n:2 * n]
        send_sems, recv_sems = refs[2 * n:]
        x, y, c = lax.axis_index("x"), lax.axis_index("y"), lax.axis_index("c")
        chips = [(1 - x, y), (x, 1 - y), (1 - x, 1 - y)]
        copies = []
        for a in range(n):
            for k, (px, py) in enumerate(chips):
                copies.append(pltpu.make_async_remote_copy(
                    src_ref=s_in[a].at[2 * px + py], dst_ref=got[a].at[k],
                    send_sem=send_sems.at[a, k], recv_sem=recv_sems.at[a, k],
                    device_id=(px, py, c), device_id_type=MESH))
                copies[-1].start()
        for cp in copies:
            cp.wait()

    return pl.pallas_call(
        body, name="chip_exchange", out_shape=[jax.ShapeDtypeStruct((3,) + s.shape[1:], s.dtype) for s in sums],
        in_specs=[ANY] * n, out_specs=[ANY] * n,
        scratch_shapes=[pltpu.SemaphoreType.DMA((n, 3)), pltpu.SemaphoreType.DMA((n, 3))],
    )(*sums)


def _adamw_math(w, g, m, v):
    m = ADAM_B1 * m + (1.0 - ADAM_B1) * g
    v = ADAM_B2 * v + (1.0 - ADAM_B2) * (g * g)
    m_hat = m / (1.0 - ADAM_B1 ** ADAM_STEP)
    v_hat = v / (1.0 - ADAM_B2 ** ADAM_STEP)
    delta = -ADAM_LR * (m_hat / (jnp.sqrt(v_hat) + ADAM_EPS) + ADAM_WD * w)
    return delta, m, v


def _adamw_shard(place, grad, landed, got, w, m, v):
    rows, cols = w.shape
    tr = _row_tile(rows, cols)

    def body(place_ref, a_ref, b_ref, got_ref, w_ref, m_ref, v_ref, g_out, d_out, m_out, v_out):
        g = a_ref[...] + b_ref[...]
        for k in range(3):
            g = g + got_ref[k].astype(F32)
        g_out[...] = g
        d_out[...], m_out[...], v_out[...] = _adamw_math(w_ref[...], g, m_ref[...], v_ref[...])

    spec = pl.BlockSpec((tr, cols), lambda i, pr: (i, 0))
    out = jax.ShapeDtypeStruct((rows, cols), F32)
    return pl.pallas_call(
        body, name="adamw_shard", out_shape=[out] * 4,
        grid_spec=pltpu.PrefetchScalarGridSpec(
            num_scalar_prefetch=1, grid=(rows // tr,),
            in_specs=[pl.BlockSpec((None, None, tr, cols), lambda i, pr: (pr[0], pr[1], i, 0)),
                      pl.BlockSpec((None, None, tr, cols), lambda i, pr: (pr[0], 0, i, 0)),
                      pl.BlockSpec((3, tr, cols), lambda i, pr: (0, i, 0)), spec, spec, spec],
            out_specs=[spec] * 4),
        compiler_params=pltpu.CompilerParams(dimension_semantics=("parallel",)),
    )(place, grad, landed, got, w, m, v)


def _gather_small(grad, landed, got):
    rows, cols = got.shape[1:]

    def body(grad_ref, landed_ref, got_ref, out_ref, send_sems, recv_sems):
        x, y, c = lax.axis_index("x"), lax.axis_index("y"), lax.axis_index("c")
        me, sibling = (x, y, c), (x, y, 1 - c)
        chips = [(1 - x, y), (x, 1 - y), (1 - x, 1 - y)]

        def slot(px, py, pc):
            return out_ref.at[4 * px + 2 * py + pc]

        def copy(k, block, to):
            return pltpu.make_async_remote_copy(
                src_ref=slot(*block), dst_ref=slot(*block), send_sem=send_sems.at[k], recv_sem=recv_sems.at[k],
                device_id=to, device_id_type=MESH)

        mine = grad_ref[2 * x + y, c] + landed_ref[2 * x + y, 0]
        out_ref[4 * x + 2 * y + c] = ((mine + got_ref[0]) + got_ref[1]) + got_ref[2]
        first = [copy(0, me, sibling)] + [copy(1 + j, me, (*chip, c)) for j, chip in enumerate(chips)]
        for cp in first:
            cp.start()
        passed = [copy(4 + j, (*chip, c), sibling) for j, chip in enumerate(chips)]
        for j, chip in enumerate(chips):
            copy(1 + j, (*chip, c), me).wait_recv()
            passed[j].start()
        copy(0, sibling, me).wait_recv()
        for j, chip in enumerate(chips):
            copy(4 + j, (*chip, 1 - c), me).wait_recv()
        for cp in first + passed:
            cp.wait_send()

    return pl.pallas_call(
        body, name="gather_small", out_shape=jax.ShapeDtypeStruct((N_DEV, rows, cols), F32),
        in_specs=[VMEM] * 3, out_specs=VMEM,
        scratch_shapes=[pltpu.SemaphoreType.DMA((7,)), pltpu.SemaphoreType.DMA((7,))],
    )(grad, landed, got)


def _adamw_small(params):
    n = len(params)

    def body(*refs):
        ins, outs = refs[:4 * n], refs[4 * n:]
        for i in range(n):
            w_ref, g_ref, m_ref, v_ref = ins[4 * i:4 * i + 4]
            d, m, v = _adamw_math(w_ref[...], g_ref[...], m_ref[...], v_ref[...])
            outs[3 * i][...] = d
            outs[3 * i + 1][...] = m
            outs[3 * i + 2][...] = v

    flat = [a for p in params for a in p]
    out_shape = [jax.ShapeDtypeStruct(p[0].shape, F32) for p in params for _ in range(3)]
    outs = pl.pallas_call(
        body, name="adamw_small", out_shape=out_shape, in_specs=[VMEM] * (4 * n), out_specs=[VMEM] * (3 * n),
    )(*flat)
    return [tuple(outs[3 * i:3 * i + 3]) for i in range(n)]


def _pack_small(parts):
    rows = []
    for a in parts:
        r, c = a.shape
        rows.append(jnp.pad(a, ((0, (-r) % 8), (0, CHUNK - c))))
    flat = jnp.concatenate(rows, axis=0)
    return jnp.pad(flat, ((0, N_DEV * SMALL_ROWS - flat.shape[0]), (0, 0))).reshape(N_DEV, SMALL_ROWS, CHUNK)


def _unpack_small(flat, shapes):
    out, at = [], 0
    for r, c in shapes:
        out.append(flat[at:at + r, :c])
        at += r + (-r) % 8
    return out


def kernel(x, w_in_a, sgu_ln_g, sgu_ln_b, w_spatial, b_spatial, w_out_a, w_kv, w_in_b, attn_sinks, rel_bias, w_out_b, post_ln_g, post_ln_b, loss_target, m_w_in_a, m_sgu_ln_g, m_sgu_ln_b, m_w_spatial, m_b_spatial, m_w_out_a, m_w_kv, m_w_in_b, m_attn_sinks, m_rel_bias, m_w_out_b, m_post_ln_g, m_post_ln_b, v_w_in_a, v_sgu_ln_g, v_sgu_ln_b, v_w_spatial, v_b_spatial, v_w_out_a, v_w_kv, v_w_in_b, v_attn_sinks, v_rel_bias, v_w_out_b, v_post_ln_g, v_post_ln_b):
    seq = x.shape[1]
    me = 4 * lax.axis_index("x") + 2 * lax.axis_index("y") + lax.axis_index("c")
    bucket = jnp.asarray(_band_tables())

    ln_shard = jnp.concatenate([sgu_ln_g, sgu_ln_b], axis=0)
    x2, tgt = x[0], loss_target[0]
    order = jnp.stack([4 * px + 2 * py + pc for px, py, pc in _use_order()]).astype(jnp.int32)
    p, wa, woa, wkv, wb, wob, ln_all = _gather_in_proj(order, x2, w_in_a[0], w_out_a[0], w_kv, w_in_b[0], w_out_b[0],
                                                       ln_shard)
    woa = woa.reshape(A_WIDTH, D_MODEL)
    wkv = wkv.reshape(D_MODEL, 2 * KV_WIDTH)
    wob = wob.reshape(B_WIDTH, D_MODEL)
    ln_gb = jnp.transpose(ln_all, (1, 0, 2)).reshape(2, A_WIDTH)
    ws = w_spatial[0]
    bsp_t = jnp.pad(b_spatial[0].T, ((0, 0), (0, CHUNK - A_GROUPS)))
    bias = _bias_table(rel_bias, bucket)

    r0, xb = _fwd_a(x2, p, ln_gb, ws, bsp_t, woa)
    q, z, kv, dr1, loss, dg1, db1 = _fwd_b(r0, tgt, wb, wkv, wob, post_ln_g, post_ln_b, attn_sinks, bias)
    dh1, dwb, dwkv, dwob, dsink, dlog = _bwd_b(r0, dr1, q, z, kv, wb, wkv, wob, post_ln_g, post_ln_b, attn_sinks,
                                               bias)
    drel = _rel_bias_grad(dlog, bucket)
    dp, dr0, dwoa, dws, dbsp_t, dln, dpost0 = _bwd_a(dh1, r0, p, ln_gb, ws, bsp_t, woa, post_ln_g)
    dx = _input_grad(dp, dr0, wa)
    dwa = _in_proj_weight_grad(xb, dp, wa.shape[2])

    small_shapes = [(A_GROUPS * CHUNK, CHUNK), (A_GROUPS, CHUNK), (1, N_Q_HEADS), (REL_BUCKETS, N_Q_HEADS),
                    (16, CHUNK), (16, CHUNK), (16, CHUNK), (16, CHUNK)]
    small = _pack_small([
        dws.reshape(A_GROUPS * CHUNK, CHUNK), dbsp_t[:, :A_GROUPS].T, dsink[:, :N_Q_HEADS], drel[:, :N_Q_HEADS],
        jnp.concatenate([dpost0[0:1], dg1], axis=0).reshape(16, CHUNK),
        jnp.concatenate([dpost0[1:2], db1], axis=0).reshape(16, CHUNK),
        dln[0].reshape(16, CHUNK), dln[1].reshape(16, CHUNK)])
    full = [dwa, dwoa.reshape(N_DEV, A_WIDTH // N_DEV, D_MODEL), dwkv.reshape(N_DEV, D_MODEL // N_DEV, 2 * KV_WIDTH),
            dwb, dwob.reshape(N_DEV, B_WIDTH // N_DEV, D_MODEL), small]
    wire = [BF16] * 5 + [F32]
    place = jnp.stack([2 * lax.axis_index("x") + lax.axis_index("y"), lax.axis_index("c")]).astype(jnp.int32)
    full = [g.reshape((4, 2) + g.shape[1:]) for g in full]
    landed = _pair_exchange(full)
    got = _chip_exchange([_pair_sum(place, g, l, dt) for g, l, dt in zip(full, landed, wire)])

    big = [(w_in_a, m_w_in_a, v_w_in_a), (w_out_a, m_w_out_a, v_w_out_a), (w_kv, m_w_kv, v_w_kv),
           (w_in_b, m_w_in_b, v_w_in_b), (w_out_b, m_w_out_b, v_w_out_b)]
    big_out = []
    for i, (w, m, v) in enumerate(big):
        shape2 = got[i].shape[1:]
        res = _adamw_shard(place, full[i], landed[i], got[i], w.reshape(shape2), m.reshape(shape2),
                           v.reshape(shape2))
        big_out.append([r.reshape(w.shape) for r in res])

    small_all = _gather_small(full[5], landed[5], got[5]).reshape(N_DEV * SMALL_ROWS, CHUNK)
    g_ws, g_bsp, g_sink, g_rel, g_pg, g_pb, g_lng, g_lnb = _unpack_small(small_all, small_shapes)
    g_ws = g_ws.reshape(w_spatial.shape)
    g_bsp = g_bsp.reshape(b_spatial.shape)
    g_pg = g_pg.reshape(post_ln_g.shape)
    g_pb = g_pb.reshape(post_ln_b.shape)
    g_lng = lax.dynamic_slice(g_lng.reshape(1, A_WIDTH), (0, me * sgu_ln_g.shape[1]), sgu_ln_g.shape)
    g_lnb = lax.dynamic_slice(g_lnb.reshape(1, A_WIDTH), (0, me * sgu_ln_b.shape[1]), sgu_ln_b.shape)
    small_params = [(sgu_ln_g, g_lng, m_sgu_ln_g, v_sgu_ln_g), (sgu_ln_b, g_lnb, m_sgu_ln_b, v_sgu_ln_b),
                    (w_spatial, g_ws, m_w_spatial, v_w_spatial), (b_spatial, g_bsp, m_b_spatial, v_b_spatial),
                    (attn_sinks, g_sink, m_attn_sinks, v_attn_sinks), (rel_bias, g_rel, m_rel_bias, v_rel_bias),
                    (post_ln_g, g_pg, m_post_ln_g, v_post_ln_g), (post_ln_b, g_pb, m_post_ln_b, v_post_ln_b)]
    flat2 = [tuple(a.reshape(-1, a.shape[-1]) for a in p) for p in small_params]
    small_out = [tuple(r.reshape(p[0].shape) for r in res) for res, p in zip(_adamw_small(flat2), small_params)]

    loss_all = lax.psum(loss[0, 0], ("x", "y", "c"))
    order = [("big", 0), ("small", 0), ("small", 1), ("small", 2), ("small", 3), ("big", 1), ("big", 2), ("big", 3),
             ("small", 4), ("small", 5), ("big", 4), ("small", 6), ("small", 7)]
    grads, deltas, new_m, new_v = [], [], [], []
    for kind, i in order:
        if kind == "big":
            g, d, m, v = big_out[i]
        else:
            g = small_params[i][1]
            d, m, v = small_out[i]
        grads.append(g)
        deltas.append(d)
        new_m.append(m)
        new_v.append(v)
    return (loss_all, dx.reshape(x.shape), *grads, *deltas, *new_m, *new_v)
```

```python
import functools

import jax
import jax.numpy as jnp
import numpy as np
from jax import lax
from jax.experimental import pallas as pl
from jax.experimental.pallas import tpu as pltpu

F32 = jnp.float32
BF16 = jnp.bfloat16

D_MODEL = 1024
A_WIDTH = 2048
A_GROUPS = 8
A_GROUP_DIM = A_WIDTH // A_GROUPS
CHUNK = 128
HEAD_DIM = 64
N_Q_HEADS = 16
N_KV_HEADS = 2
PAIRS_PER_KV = N_Q_HEADS // N_KV_HEADS // 2
N_PAIRS = N_Q_HEADS // 2
B_WIDTH = N_Q_HEADS * HEAD_DIM
KV_WIDTH = N_KV_HEADS * HEAD_DIM
REL_BUCKETS = 32
REL_MAX_DIST = 128
DEPTH = 2
ALPHA = (2.0 * DEPTH) ** 0.25
LN_EPS = 1e-5
NEG_INF = -1e30
SCALE = HEAD_DIM ** -0.5

ADAM_LR = 0.001
ADAM_B1 = 0.9
ADAM_B2 = 0.999
ADAM_EPS = 1e-08
ADAM_WD = 0.01
ADAM_STEP = 10

N_DEV = 8
MESH = pl.DeviceIdType.MESH
TOKENS_IN_PROJ = 512
TOKENS_A = 256
TOKENS_B = 512
TOKENS_B_BWD = 256
TOKENS_DX = 512
VMEM_LIMIT_BYTES = 60 * 1024 * 1024
SMALL_ROWS = 144

ANY = pl.BlockSpec(memory_space=pl.ANY)
VMEM = pl.BlockSpec(memory_space=pltpu.VMEM)
SMEM = pl.BlockSpec(memory_space=pltpu.SMEM)


def _nn(a, b):
    return jnp.dot(a, b, preferred_element_type=F32)


def _nt(a, b):
    return lax.dot_general(a, b, (((1,), (1,)), ((), ())), preferred_element_type=F32)


def _tn(a, b):
    return lax.dot_general(a, b, (((0,), (0,)), ((), ())), preferred_element_type=F32)


def _resident(shape):
    nd = len(shape)
    return pl.BlockSpec(shape, lambda i: (0,) * nd, pipeline_mode=pl.Buffered(1))


def _rows(tm, width, reverse_of=None):
    if reverse_of is None:
        return pl.BlockSpec((tm, width), lambda i: (i, 0))
    return pl.BlockSpec((tm, width), lambda i: (reverse_of - 1 - i, 0))


def _seq_params():
    return pltpu.CompilerParams(dimension_semantics=("arbitrary",), vmem_limit_bytes=VMEM_LIMIT_BYTES)


def _ln_normalize(r):
    mu = jnp.mean(r, axis=-1, keepdims=True)
    xc = r - mu
    var = jnp.mean(xc * xc, axis=-1, keepdims=True)
    rstd = lax.rsqrt(var + LN_EPS)
    return xc * rstd, rstd


def _ln_backward(dxhat, xhat, rstd):
    m1 = jnp.mean(dxhat, axis=-1, keepdims=True)
    m2 = jnp.mean(dxhat * xhat, axis=-1, keepdims=True)
    return rstd * (dxhat - m1 - xhat * m2)


def _silu_and_grad(z):
    sig = jax.nn.sigmoid(z)
    return z * sig, sig * (1.0 + z * (1.0 - sig))


def _band_tables():
    t = np.arange(CHUNK, dtype=np.int32)[:, None]
    j = np.arange(2 * CHUNK, dtype=np.int32)[None, :]
    d = t + CHUNK - j
    in_window = (d >= 0) & (d < CHUNK)
    dc = np.clip(d, 0, REL_MAX_DIST - 1)
    max_exact = REL_BUCKETS // 2
    df = np.maximum(dc, 1).astype(np.float32)
    large = max_exact + (np.log(df / np.float32(max_exact)) / np.float32(np.log(REL_MAX_DIST / max_exact))
                         * np.float32(REL_BUCKETS - max_exact)).astype(np.int32)
    large = np.minimum(large, REL_BUCKETS - 1)
    bucket = np.where(dc < max_exact, dc, large)
    return np.where(in_window, bucket, -1).astype(np.int32)


def _use_order():
    x, y, c = lax.axis_index("x"), lax.axis_index("y"), lax.axis_index("c")
    blocks = [(x, y, c), (x, y, 1 - c)]
    for px, py in [(1 - x, y), (x, 1 - y), (1 - x, 1 - y)]:
        blocks += [(px, py, c), (px, py, 1 - c)]
    return blocks


ARRIVAL_OF_PHASE = (None, 0, 1, 4, 2, 5, 3, 6)


def _gather_in_proj(order, x, w_in_a, w_out_a, w_kv, w_in_b, w_out_b, ln_gb):
    seq = x.shape[0]
    tm = min(TOKENS_IN_PROJ, seq)
    n_blocks = seq // tm
    shard_cols = w_in_a.shape[1]
    small = (w_out_a, w_kv, w_in_b, w_out_b)
    n = len(small) + 2

    def body(order_ref, x_ref, *refs):
        ins, refs = refs[:n], refs[n:]
        p_ref, outs, refs = refs[0], refs[1:n + 1], refs[n + 1:]
        wa_vmem, xb_vmem, stage = refs[0], refs[1], refs[2:n]
        send_sems, recv_sems, local_sems, out_sems = refs[n:]
        phase, blk = pl.program_id(0), pl.program_id(1)
        x_, y_, c_ = lax.axis_index("x"), lax.axis_index("y"), lax.axis_index("c")
        me, sibling = (x_, y_, c_), (x_, y_, 1 - c_)
        chips = [(1 - x_, y_), (x_, 1 - y_), (1 - x_, 1 - y_)]
        use = _use_order()

        def index(px, py, pc):
            return 4 * px + 2 * py + pc

        def slot(a, block):
            return (wa_vmem if a == 0 else outs[a]).at[index(*block)]

        def own(a):
            return slot(0, me) if a == 0 else (ins[a] if a == n - 1 else stage[a - 1])

        def copy(a, k, block, to, src=None):
            return pltpu.make_async_remote_copy(
                src_ref=slot(a, block) if src is None else src, dst_ref=slot(a, block),
                send_sem=send_sems.at[a, k], recv_sem=recv_sems.at[a, k], device_id=to, device_id_type=MESH)

        def keep(a):
            return pltpu.make_async_copy(own(a), slot(a, me), local_sems.at[a])

        def write_out(k):
            return pltpu.make_async_copy(slot(0, use[k]), outs[0].at[index(*use[k])], out_sems.at[k])

        def pass_on(a, j):
            copy(a, 1 + j, (*chips[j], c_), me).wait_recv()
            copy(a, 4 + j, (*chips[j], c_), sibling).start()

        def send(a, j):
            copy(a, 1 + j, me, (*chips[j], c_), own(a)).start()

        @pl.when((phase == 0) & (blk == 0))
        def _():
            wa_vmem[index(*me)] = ins[0][...].astype(BF16)
            for a in range(1, n - 1):
                stage[a - 1][...] = ins[a][...].astype(BF16)
            send(0, 0)
            send(0, 1)
            for a in range(n):
                copy(a, 0, me, sibling, own(a)).start()
                if a:
                    keep(a).start()
            write_out(0).start()

        for k in range(1, N_DEV):
            @pl.when((phase == k) & (blk == 0))
            def _(k=k):
                if k % 2 == 0:
                    pass_on(0, k // 2 - 1)
                else:
                    copy(0, ARRIVAL_OF_PHASE[k], use[k], me).wait_recv()
                write_out(k).start()
                if k == 2:
                    send(0, 2)
                    for a in range(1, n):
                        for j in range(3):
                            send(a, j)
                if k == N_DEV - 2:
                    for a in range(1, n):
                        pass_on(a, 0)
                        pass_on(a, 1)

        rows = pl.ds(pl.multiple_of(blk * tm, tm), tm)

        @pl.when(phase == 0)
        def _():
            xb_vmem[rows, :] = x_ref[...].astype(BF16)

        p_ref[...] = _nn(xb_vmem[rows, :], wa_vmem[order_ref[phase]])

        @pl.when((phase == N_DEV - 1) & (blk == n_blocks - 1))
        def _():
            for a in range(1, n):
                pass_on(a, 2)
            for a in range(1, n):
                copy(a, 0, sibling, me).wait_recv()
                for j, chip in enumerate(chips):
                    copy(a, 4 + j, (*chip, 1 - c_), me).wait_recv()
            for a in range(n):
                for k in range(7):
                    copy(a, k, me, me).wait_send()
                if a:
                    keep(a).wait()
            for k in range(N_DEV):
                write_out(k).wait()

    gathered = [jax.ShapeDtypeStruct((N_DEV,) + w_in_a.shape, BF16)]
    gathered += [jax.ShapeDtypeStruct((N_DEV,) + s.shape, BF16) for s in small]
    gathered.append(jax.ShapeDtypeStruct((N_DEV,) + ln_gb.shape, F32))
    return pl.pallas_call(
        body, name="gather_in_proj",
        out_shape=[jax.ShapeDtypeStruct((seq, N_DEV * shard_cols), F32)] + gathered,
        grid_spec=pltpu.PrefetchScalarGridSpec(
            num_scalar_prefetch=1, grid=(N_DEV, n_blocks),
            in_specs=[pl.BlockSpec((tm, D_MODEL), lambda k, i, order: (jnp.where(k == 0, i, n_blocks - 1), 0))]
            + [VMEM] * n,
            out_specs=[pl.BlockSpec((tm, shard_cols), lambda k, i, order: (i, order[k]))] + [ANY] * n,
            scratch_shapes=[pltpu.VMEM((N_DEV,) + w_in_a.shape, BF16), pltpu.VMEM((seq, D_MODEL), BF16)]
            + [pltpu.VMEM(s.shape, BF16) for s in small] + [
                pltpu.SemaphoreType.DMA((n, 7)), pltpu.SemaphoreType.DMA((n, 7)), pltpu.SemaphoreType.DMA((n,)),
                pltpu.SemaphoreType.DMA((N_DEV,))]),
        compiler_params=pltpu.CompilerParams(dimension_semantics=("arbitrary", "arbitrary"),
                                             vmem_limit_bytes=VMEM_LIMIT_BYTES),
    )(order, x, w_in_a, *small, ln_gb)


def _bias_table(rel_bias, bucket):
    def body(rel_ref, bucket_ref, out_ref):
        bk = bucket_ref[...]
        col = lax.broadcasted_iota(jnp.int32, bk.shape, 1)
        for h in range(N_Q_HEADS):
            def step(b, acc):
                return jnp.where(bk == b, rel_ref[b, h], acc)
            tab = lax.fori_loop(0, REL_BUCKETS, step, jnp.full(bk.shape, NEG_INF, F32))
            pair = h // 2
            rows = slice((pair % PAIRS_PER_KV) * CHUNK, (pair % PAIRS_PER_KV + 1) * CHUNK)
            half = slice((h % 2) * 2 * CHUNK, (h % 2 + 1) * 2 * CHUNK)
            out_ref[1, pair // PAIRS_PER_KV, rows, half] = tab
            out_ref[0, pair // PAIRS_PER_KV, rows, half] = jnp.where(col < CHUNK, NEG_INF, tab)

    return pl.pallas_call(
        body, name="bias_table",
        out_shape=jax.ShapeDtypeStruct((2, N_KV_HEADS, PAIRS_PER_KV * CHUNK, 4 * CHUNK), F32),
        in_specs=[SMEM, VMEM], out_specs=VMEM,
    )(rel_bias, bucket)


def _rel_bias_grad(dlogits, bucket):
    def body(dl_ref, bucket_ref, out_ref):
        bk = bucket_ref[...]
        row = lax.broadcasted_iota(jnp.int32, (REL_BUCKETS, CHUNK), 0)
        lane = lax.broadcasted_iota(jnp.int32, (REL_BUCKETS, CHUNK), 1)

        def step(b, acc):
            for h in range(N_Q_HEADS):
                pair = h // 2
                dl = dl_ref[pair // PAIRS_PER_KV, (pair % PAIRS_PER_KV) * CHUNK:(pair % PAIRS_PER_KV + 1) * CHUNK,
                            (h % 2) * 2 * CHUNK:(h % 2 + 1) * 2 * CHUNK]
                s = jnp.sum(jnp.where(bk == b, dl, 0.0))
                acc = jnp.where((row == b) & (lane == h), s, acc)
            return acc

        out_ref[...] = lax.fori_loop(0, REL_BUCKETS, step, jnp.zeros((REL_BUCKETS, CHUNK), F32))

    return pl.pallas_call(
        body, name="rel_bias_grad", out_shape=jax.ShapeDtypeStruct((REL_BUCKETS, CHUNK), F32),
        in_specs=[VMEM, VMEM], out_specs=VMEM,
    )(dlogits, bucket)


def _masked_spatial(ws_ref, g):
    t = lax.broadcasted_iota(jnp.int32, (CHUNK, CHUNK), 0)
    s = lax.broadcasted_iota(jnp.int32, (CHUNK, CHUNK), 1)
    return jnp.where(s <= t, ws_ref[g], 0.0).astype(BF16)


def _spatial_gate(ws_ref, bsp_t_ref, vn, s_ref, n_chunks):
    for g in range(A_GROUPS):
        w = _masked_spatial(ws_ref, g)
        cols = slice(g * A_GROUP_DIM, (g + 1) * A_GROUP_DIM)
        for ci in range(n_chunks):
            rows = slice(ci * CHUNK, (ci + 1) * CHUNK)
            s_ref[rows, cols] = _nn(w, vn[rows, cols]) + bsp_t_ref[:, g:g + 1]


def _fwd_a(x, p, ln_gb, w_spatial, bsp_t, woa):
    seq = x.shape[0]
    tm = min(TOKENS_A, seq)
    n_chunks = tm // CHUNK

    def body(x_ref, p_ref, ln_ref, ws_ref, bsp_ref, woa_ref, r0_ref, xb_ref, s_ref):
        xv = x_ref[...]
        xb_ref[...] = xv.astype(BF16)
        vhat, _ = _ln_normalize(p_ref[:, A_WIDTH:2 * A_WIDTH])
        vn = (vhat * ln_ref[0:1, :] + ln_ref[1:2, :]).astype(BF16)
        _spatial_gate(ws_ref, bsp_ref, vn, s_ref, n_chunks)
        z = p_ref[:, 2 * A_WIDTH:]
        y = p_ref[:, :A_WIDTH] * s_ref[...] * (z * jax.nn.sigmoid(z))
        r0_ref[...] = ALPHA * xv + _nn(y.astype(BF16), woa_ref[...])

    return pl.pallas_call(
        body, name="fwd_a", grid=(seq // tm,),
        out_shape=[jax.ShapeDtypeStruct((seq, D_MODEL), F32), jax.ShapeDtypeStruct((seq, D_MODEL), BF16)],
        in_specs=[_rows(tm, D_MODEL), _rows(tm, 3 * A_WIDTH), _resident(ln_gb.shape), _resident(w_spatial.shape),
                  _resident(bsp_t.shape), _resident(woa.shape)],
        out_specs=[_rows(tm, D_MODEL), _rows(tm, D_MODEL)],
        scratch_shapes=[pltpu.VMEM((tm, A_WIDTH), F32)],
        compiler_params=_seq_params(),
    )(x, p, ln_gb, w_spatial, bsp_t, woa)


def _expand_heads(kv):
    lo = lax.broadcasted_iota(jnp.int32, kv.shape, 1) < HEAD_DIM
    swapped = pltpu.roll(kv, HEAD_DIM, 1)
    zero = jnp.zeros_like(kv)
    return (jnp.where(lo, kv, zero).astype(BF16), jnp.where(lo, zero, swapped).astype(BF16),
            jnp.where(lo, swapped, zero).astype(BF16), jnp.where(lo, zero, kv).astype(BF16))


def _fold_heads(acc, kvh):
    top, bot = acc[:2 * CHUNK], acc[2 * CHUNK:]
    lo = lax.broadcasted_iota(jnp.int32, top.shape, 1) < HEAD_DIM
    t = jnp.where(lo, top, 0.0)
    b = jnp.where(lo, 0.0, bot)
    if kvh == 0:
        return t + pltpu.roll(b, HEAD_DIM, 1)
    return pltpu.roll(t, HEAD_DIM, 1) + b


STACK = PAIRS_PER_KV * CHUNK


def _pair_cols(kvh, pp):
    pair = kvh * PAIRS_PER_KV + pp
    return slice(pair * CHUNK, (pair + 1) * CHUNK)


def _stack_pairs(ref_or_val, rows, kvh):
    return jnp.concatenate([ref_or_val[rows, _pair_cols(kvh, pp)] for pp in range(PAIRS_PER_KV)], axis=0)


def _sink_columns(sink_ref):
    blk = lax.broadcasted_iota(jnp.int32, (STACK, 1), 0) // CHUNK
    out = []
    for kvh in range(N_KV_HEADS):
        halves = []
        for half in range(2):
            col = jnp.zeros((STACK, 1), F32)
            for pp in range(PAIRS_PER_KV):
                col = jnp.where(blk == pp, sink_ref[0, 2 * (kvh * PAIRS_PER_KV + pp) + half], col)
            halves.append(col)
        out.append(halves)
    return out


def _softmax_stack(logits, sink_cols):
    ps, sinks = [], []
    for half in range(2):
        lg = logits[:, half * 2 * CHUNK:(half + 1) * 2 * CHUNK]
        sk = sink_cols[half]
        m = jnp.maximum(jnp.max(lg, axis=-1, keepdims=True), sk)
        e = jnp.exp(lg - m)
        es = jnp.exp(sk - m)
        inv = 1.0 / (jnp.sum(e, axis=-1, keepdims=True) + es)
        ps.append(e * inv)
        sinks.append(es * inv)
    return jnp.concatenate(ps, axis=1), sinks


def _band_operands(ek_ref, ev_ref, ci, kvh):
    kblk = jnp.concatenate([ek_ref[ci, 2 * kvh], ek_ref[ci + 1, 2 * kvh],
                            ek_ref[ci, 2 * kvh + 1], ek_ref[ci + 1, 2 * kvh + 1]], axis=0)
    vblk = jnp.concatenate([ev_ref[ci, 2 * kvh], ev_ref[ci + 1, 2 * kvh],
                            ev_ref[ci, 2 * kvh + 1], ev_ref[ci + 1, 2 * kvh + 1]], axis=0)
    return kblk, vblk


def _fwd_b(r0, target, wb, wkv, wob, post_g, post_b, sinks, bias):
    seq = r0.shape[0]
    tm = min(TOKENS_B, seq)
    n_chunks = tm // CHUNK
    shard_cols = wb.shape[2]

    def body(r0_ref, tgt_ref, wb_ref, wkv_ref, wob_ref, pg_ref, pb_ref, sink_ref, bias_ref,
             q_ref, z_ref, kv_ref, dr1_ref, loss_ref, dg_ref, db_ref, ek_ref, ev_ref, o_ref):
        step = pl.program_id(0)

        @pl.when(step == 0)
        def _():
            ek_ref[0] = jnp.zeros(ek_ref.shape[1:], BF16)
            ev_ref[0] = jnp.zeros(ev_ref.shape[1:], BF16)
            loss_ref[...] = jnp.zeros_like(loss_ref)
            dg_ref[...] = jnp.zeros_like(dg_ref)
            db_ref[...] = jnp.zeros_like(db_ref)

        @pl.when(step > 0)
        def _():
            ek_ref[0] = ek_ref[n_chunks]
            ev_ref[0] = ev_ref[n_chunks]

        xhat0, _ = _ln_normalize(r0_ref[...])
        h1 = xhat0 * pg_ref[0:1, :] + pb_ref[0:1, :]
        h1b = h1.astype(BF16)
        for k in range(N_DEV):
            qz = _nn(h1b, wb_ref[k])
            lo, hi = k * shard_cols, (k + 1) * shard_cols
            if hi <= B_WIDTH:
                q_ref[:, lo:hi] = (qz * SCALE).astype(BF16)
            else:
                z_ref[:, lo - B_WIDTH:hi - B_WIDTH] = qz
        kv = _nn(h1b, wkv_ref[...])
        kv_ref[...] = kv.astype(BF16)
        for ci in range(n_chunks):
            rows = slice(ci * CHUNK, (ci + 1) * CHUNK)
            for i, e in enumerate(_expand_heads(kv[rows, :KV_WIDTH])):
                ek_ref[ci + 1, i] = e
            for i, e in enumerate(_expand_heads(kv[rows, KV_WIDTH:])):
                ev_ref[ci + 1, i] = e

        sink_cols = _sink_columns(sink_ref)
        for ci in range(n_chunks):
            rows = slice(ci * CHUNK, (ci + 1) * CHUNK)
            sel = jnp.where(step * n_chunks + ci > 0, 1, 0)
            for kvh in range(N_KV_HEADS):
                kblk, vblk = _band_operands(ek_ref, ev_ref, ci, kvh)
                logits = _nt(_stack_pairs(q_ref, rows, kvh), kblk) + bias_ref[sel, kvh]
                p, _ = _softmax_stack(logits, sink_cols[kvh])
                o = _nn(p.astype(BF16), vblk)
                for pp in range(PAIRS_PER_KV):
                    o_ref[rows, _pair_cols(kvh, pp)] = o[pp * CHUNK:(pp + 1) * CHUNK]

        z = z_ref[...]
        y = o_ref[...] * (z * jax.nn.sigmoid(z))
        r1 = ALPHA * h1 + _nn(y.astype(BF16), wob_ref[...])
        xhat1, rstd1 = _ln_normalize(r1)
        g1 = pg_ref[1:2, :]
        diff = xhat1 * g1 + pb_ref[1:2, :] - tgt_ref[...]
        loss_ref[...] += 0.5 * jnp.sum(jnp.mean(diff * diff, axis=-1, keepdims=True), axis=0, keepdims=True)
        dh2 = diff * (1.0 / D_MODEL)
        dg_ref[...] += jnp.sum(dh2 * xhat1, axis=0, keepdims=True)
        db_ref[...] += jnp.sum(dh2, axis=0, keepdims=True)
        dr1_ref[...] = _ln_backward(dh2 * g1, xhat1, rstd1)

    vec = pl.BlockSpec((1, D_MODEL), lambda i: (0, 0))
    return pl.pallas_call(
        body, name="fwd_b", grid=(seq // tm,),
        out_shape=[jax.ShapeDtypeStruct((seq, B_WIDTH), BF16), jax.ShapeDtypeStruct((seq, B_WIDTH), F32),
                   jax.ShapeDtypeStruct((seq, 2 * KV_WIDTH), BF16), jax.ShapeDtypeStruct((seq, D_MODEL), F32),
                   jax.ShapeDtypeStruct((1, 1), F32), jax.ShapeDtypeStruct((1, D_MODEL), F32),
                   jax.ShapeDtypeStruct((1, D_MODEL), F32)],
        in_specs=[_rows(tm, D_MODEL), _rows(tm, D_MODEL), _resident(wb.shape), _resident(wkv.shape),
                  _resident(wob.shape), _resident(post_g.shape), _resident(post_b.shape), SMEM,
                  _resident(bias.shape)],
        out_specs=[_rows(tm, B_WIDTH), _rows(tm, B_WIDTH), _rows(tm, 2 * KV_WIDTH), _rows(tm, D_MODEL),
                   pl.BlockSpec((1, 1), lambda i: (0, 0)), vec, vec],
        scratch_shapes=[pltpu.VMEM((n_chunks + 1, 4, CHUNK, CHUNK), BF16),
                        pltpu.VMEM((n_chunks + 1, 4, CHUNK, CHUNK), BF16),
                        pltpu.VMEM((tm, B_WIDTH), F32)],
        compiler_params=_seq_params(),
    )(r0, target, wb, wkv, wob, post_g, post_b, sinks, bias)


def _bwd_b(r0, dr1, q, z, kv, wb, wkv, wob, post_g, post_b, sinks, bias):
    seq = r0.shape[0]
    tm = min(TOKENS_B_BWD, seq)
    n_chunks = tm // CHUNK
    n_blocks = seq // tm
    shard_cols = wb.shape[2]

    def body(r0_ref, dr1_ref, q_ref, z_ref, kv_ref, kvp_ref, wb_ref, wkv_ref, wob_ref, pg_ref, pb_ref, sink_ref,
             bias_ref, dh1_ref, dwb_out, dwkv_out, dwob_out, dsink_ref, dl_out,
             dwb_acc, dwkv_acc, dwob_acc, dl_acc, dsink_acc, carry_ref, ek_ref, ev_ref, y_ref, dqz_ref, dkv_ref):
        step = pl.program_id(0)
        block = n_blocks - 1 - step

        @pl.when(step == 0)
        def _():
            dwb_acc[...] = jnp.zeros_like(dwb_acc)
            dwkv_acc[...] = jnp.zeros_like(dwkv_acc)
            dwob_acc[...] = jnp.zeros_like(dwob_acc)
            dl_acc[...] = jnp.zeros_like(dl_acc)
            carry_ref[...] = jnp.zeros_like(carry_ref)
            dsink_acc[...] = jnp.zeros_like(dsink_acc)

        xhat0, _ = _ln_normalize(r0_ref[...])
        h1b = (xhat0 * pg_ref[0:1, :] + pb_ref[0:1, :]).astype(BF16)
        dr1 = dr1_ref[...]
        dr1b = dr1.astype(BF16)
        dy = _nt(dr1b, wob_ref[...])
        sg, dsg = _silu_and_grad(z_ref[...])
        do_all = dy * sg
        dyz = dy * dsg

        for ci in range(n_chunks + 1):
            src = kvp_ref[...] if ci == 0 else kv_ref[(ci - 1) * CHUNK:ci * CHUNK, :]
            src = src.astype(F32)
            for i, e in enumerate(_expand_heads(src[:, :KV_WIDTH])):
                ek_ref[ci, i] = e
            for i, e in enumerate(_expand_heads(src[:, KV_WIDTH:])):
                ev_ref[ci, i] = e

        sink_cols = _sink_columns(sink_ref)
        for ci in reversed(range(n_chunks)):
            rows = slice(ci * CHUNK, (ci + 1) * CHUNK)
            sel = jnp.where(block * n_chunks + ci > 0, 1, 0)
            dk_band = jnp.zeros((2 * CHUNK, KV_WIDTH), F32)
            dv_band = jnp.zeros((2 * CHUNK, KV_WIDTH), F32)
            for kvh in range(N_KV_HEADS):
                kblk, vblk = _band_operands(ek_ref, ev_ref, ci, kvh)
                qs = _stack_pairs(q_ref, rows, kvh)
                p, psink = _softmax_stack(_nt(qs, kblk) + bias_ref[sel, kvh], sink_cols[kvh])
                pb16 = p.astype(BF16)
                o = _nn(pb16, vblk)
                y = (o * _stack_pairs(sg, rows, kvh)).astype(BF16)
                dz = (o * _stack_pairs(dyz, rows, kvh)).astype(BF16)
                dob = _stack_pairs(do_all, rows, kvh).astype(BF16)
                dp = _nt(dob, vblk)
                dlogits = []
                for half in range(2):
                    hs = slice(half * 2 * CHUNK, (half + 1) * 2 * CHUNK)
                    delta = jnp.sum(dp[:, hs] * p[:, hs], axis=-1, keepdims=True)
                    dlogits.append(p[:, hs] * (dp[:, hs] - delta))
                    dsink_acc[2 * kvh + half] += psink[half] * delta
                dlog = jnp.concatenate(dlogits, axis=1)
                dl_acc[kvh] += dlog
                dsc = dlog.astype(BF16)
                dq = (_nn(dsc, kblk) * SCALE).astype(BF16)
                for pp in range(PAIRS_PER_KV):
                    blk = slice(pp * CHUNK, (pp + 1) * CHUNK)
                    cols = _pair_cols(kvh, pp)
                    y_ref[rows, cols] = y[blk]
                    dqz_ref[rows, cols] = dq[blk]
                    dqz_ref[rows, B_WIDTH + cols.start:B_WIDTH + cols.stop] = dz[blk]
                dk_band = dk_band + _fold_heads(_tn(dsc, qs), kvh)
                dv_band = dv_band + _fold_heads(_tn(pb16, dob), kvh)
            band = jnp.concatenate([dk_band, dv_band], axis=1)
            dkv_ref[rows, :] = (band[CHUNK:] + carry_ref[...]).astype(BF16)
            carry_ref[...] = band[:CHUNK]

        dqz = dqz_ref[...]
        dkv = dkv_ref[...]
        dh1 = ALPHA * dr1 + _nt(dkv, wkv_ref[...])
        for k in range(N_DEV):
            part = dqz[:, k * shard_cols:(k + 1) * shard_cols]
            dh1 = dh1 + _nt(part, wb_ref[k])
            dwb_acc[k] += _tn(h1b, part)
        dh1_ref[...] = dh1
        dwkv_acc[...] += _tn(h1b, dkv)
        dwob_acc[...] += _tn(y_ref[...], dr1b)

        @pl.when(step == n_blocks - 1)
        def _():
            pltpu.sync_copy(dwb_acc, dwb_out)
            pltpu.sync_copy(dwkv_acc, dwkv_out)
            pltpu.sync_copy(dwob_acc, dwob_out)
            pltpu.sync_copy(dl_acc, dl_out)
            lane = lax.broadcasted_iota(jnp.int32, (1, CHUNK), 1)
            dsink = jnp.zeros((1, CHUNK), F32)
            for kvh in range(N_KV_HEADS):
                for half in range(2):
                    for pp in range(PAIRS_PER_KV):
                        head = 2 * (kvh * PAIRS_PER_KV + pp) + half
                        tot = jnp.sum(dsink_acc[2 * kvh + half, pp * CHUNK:(pp + 1) * CHUNK, :], axis=0, keepdims=True)
                        dsink = dsink - jnp.where(lane == head, tot, 0.0)
            dsink_ref[...] = dsink

    rev = functools.partial(_rows, tm, reverse_of=n_blocks)
    prev_chunk = pl.BlockSpec(
        (CHUNK, 2 * KV_WIDTH), lambda i: (jnp.maximum((n_blocks - 1 - i) * n_chunks - 1, 0), 0))
    return pl.pallas_call(
        body, name="bwd_b", grid=(n_blocks,),
        out_shape=[jax.ShapeDtypeStruct((seq, D_MODEL), F32), jax.ShapeDtypeStruct(wb.shape, F32),
                   jax.ShapeDtypeStruct(wkv.shape, F32), jax.ShapeDtypeStruct(wob.shape, F32),
                   jax.ShapeDtypeStruct((1, CHUNK), F32),
                   jax.ShapeDtypeStruct((N_KV_HEADS, STACK, 4 * CHUNK), F32)],
        in_specs=[rev(D_MODEL), rev(D_MODEL), rev(B_WIDTH), rev(B_WIDTH), rev(2 * KV_WIDTH), prev_chunk,
                  _resident(wb.shape), _resident(wkv.shape), _resident(wob.shape), _resident(post_g.shape),
                  _resident(post_b.shape), SMEM, _resident(bias.shape)],
        out_specs=[rev(D_MODEL), ANY, ANY, ANY, pl.BlockSpec((1, CHUNK), lambda i: (0, 0)), ANY],
        scratch_shapes=[pltpu.VMEM(wb.shape, F32), pltpu.VMEM(wkv.shape, F32), pltpu.VMEM(wob.shape, F32),
                        pltpu.VMEM((N_KV_HEADS, STACK, 4 * CHUNK), F32), pltpu.VMEM((2 * N_KV_HEADS, STACK, 1), F32),
                        pltpu.VMEM((CHUNK, 2 * KV_WIDTH), F32),
                        pltpu.VMEM((n_chunks + 1, 4, CHUNK, CHUNK), BF16),
                        pltpu.VMEM((n_chunks + 1, 4, CHUNK, CHUNK), BF16),
                        pltpu.VMEM((tm, B_WIDTH), BF16), pltpu.VMEM((tm, 2 * B_WIDTH), BF16),
                        pltpu.VMEM((tm, 2 * KV_WIDTH), BF16)],
        compiler_params=_seq_params(),
    )(r0, dr1, q, z, kv, kv, wb, wkv, wob, post_g, post_b, sinks, bias)


def _bwd_a(dh1, r0, p, ln_gb, w_spatial, bsp_t, woa, post_g):
    seq = r0.shape[0]
    tm = min(TOKENS_A, seq)
    n_chunks = tm // CHUNK
    n_blocks = seq // tm

    def body(dh1_ref, r0_ref, p_ref, ln_ref, ws_ref, bsp_ref, woa_ref, pg_ref,
             dp_ref, dr0_ref, dwoa_out, dws_ref, dbsp_ref, dln_ref, dpost_ref,
             dwoa_acc, s_ref, ds_ref, dvn_ref):
        step = pl.program_id(0)

        @pl.when(step == 0)
        def _():
            dwoa_acc[...] = jnp.zeros_like(dwoa_acc)
            dws_ref[...] = jnp.zeros_like(dws_ref)
            dbsp_ref[...] = jnp.zeros_like(dbsp_ref)
            dln_ref[...] = jnp.zeros_like(dln_ref)
            dpost_ref[...] = jnp.zeros_like(dpost_ref)

        dh1 = dh1_ref[...]
        xhat0, rstd0 = _ln_normalize(r0_ref[...])
        dpost_ref[0:1, :] += jnp.sum(dh1 * xhat0, axis=0, keepdims=True)
        dpost_ref[1:2, :] += jnp.sum(dh1, axis=0, keepdims=True)
        dr0 = _ln_backward(dh1 * pg_ref[0:1, :], xhat0, rstd0)
        dr0_ref[...] = dr0
        dr0b = dr0.astype(BF16)

        vhat, rstd_v = _ln_normalize(p_ref[:, A_WIDTH:2 * A_WIDTH])
        ln_g = ln_ref[0:1, :]
        vn = (vhat * ln_g + ln_ref[1:2, :]).astype(BF16)
        _spatial_gate(ws_ref, bsp_ref, vn, s_ref, n_chunks)
        u = p_ref[:, :A_WIDTH]
        s = s_ref[...]
        sg, dsg = _silu_and_grad(p_ref[:, 2 * A_WIDTH:])
        dwoa_acc[...] += _tn((u * s * sg).astype(BF16), dr0b)
        dy = _nt(dr0b, woa_ref[...])
        dp_ref[:, :A_WIDTH] = (dy * s * sg).astype(BF16)
        dp_ref[:, 2 * A_WIDTH:] = (dy * u * s * dsg).astype(BF16)
        ds_ref[...] = dy * u * sg

        lane = lax.broadcasted_iota(jnp.int32, (CHUNK, CHUNK), 1)
        dbsp = jnp.zeros((CHUNK, CHUNK), F32)
        for g in range(A_GROUPS):
            w = _masked_spatial(ws_ref, g)
            cols = slice(g * A_GROUP_DIM, (g + 1) * A_GROUP_DIM)
            dws = jnp.zeros((CHUNK, CHUNK), F32)
            for ci in range(n_chunks):
                rows = slice(ci * CHUNK, (ci + 1) * CHUNK)
                ds = ds_ref[rows, cols]
                dsb = ds.astype(BF16)
                dvn_ref[rows, cols] = _tn(w, dsb)
                dws = dws + _nt(dsb, vn[rows, cols])
                dbsp = dbsp + jnp.where(lane == g, jnp.sum(ds, axis=-1, keepdims=True), 0.0)
            dws_ref[g] += dws
        dbsp_ref[...] += dbsp

        dvn = dvn_ref[...]
        dln_ref[0:1, :] += jnp.sum(dvn * vhat, axis=0, keepdims=True)
        dln_ref[1:2, :] += jnp.sum(dvn, axis=0, keepdims=True)
        dp_ref[:, A_WIDTH:2 * A_WIDTH] = _ln_backward(dvn * ln_g, vhat, rstd_v).astype(BF16)

        @pl.when(step == n_blocks - 1)
        def _():
            t = lax.broadcasted_iota(jnp.int32, (CHUNK, CHUNK), 0)
            for g in range(A_GROUPS):
                dws_ref[g] = jnp.where(lane <= t, dws_ref[g], 0.0)
            pltpu.sync_copy(dwoa_acc, dwoa_out)

    def fixed(shape):
        nd = len(shape)
        return pl.BlockSpec(shape, lambda i: (0,) * nd)

    return pl.pallas_call(
        body, name="bwd_a", grid=(n_blocks,),
        out_shape=[jax.ShapeDtypeStruct((seq, 3 * A_WIDTH), BF16), jax.ShapeDtypeStruct((seq, D_MODEL), F32),
                   jax.ShapeDtypeStruct(woa.shape, F32), jax.ShapeDtypeStruct(w_spatial.shape, F32),
                   jax.ShapeDtypeStruct((CHUNK, CHUNK), F32), jax.ShapeDtypeStruct((2, A_WIDTH), F32),
                   jax.ShapeDtypeStruct((2, D_MODEL), F32)],
        in_specs=[_rows(tm, D_MODEL), _rows(tm, D_MODEL), _rows(tm, 3 * A_WIDTH), _resident(ln_gb.shape),
                  _resident(w_spatial.shape), _resident(bsp_t.shape), _resident(woa.shape),
                  _resident(post_g.shape)],
        out_specs=[_rows(tm, 3 * A_WIDTH), _rows(tm, D_MODEL), ANY, fixed(w_spatial.shape),
                   fixed((CHUNK, CHUNK)), fixed((2, A_WIDTH)), fixed((2, D_MODEL))],
        scratch_shapes=[pltpu.VMEM(woa.shape, F32), pltpu.VMEM((tm, A_WIDTH), F32),
                        pltpu.VMEM((tm, A_WIDTH), F32), pltpu.VMEM((tm, A_WIDTH), F32)],
        compiler_params=_seq_params(),
    )(dh1, r0, p, ln_gb, w_spatial, bsp_t, woa, post_g)


def _input_grad(dp, dr0, wa):
    seq = dr0.shape[0]
    tm = min(TOKENS_DX, seq)
    shard_cols = wa.shape[2]

    def body(dp_ref, dr0_ref, wa_ref, dx_ref):
        dx = ALPHA * dr0_ref[...]
        for k in range(N_DEV):
            dx = dx + _nt(dp_ref[:, k * shard_cols:(k + 1) * shard_cols], wa_ref[k])
        dx_ref[...] = dx

    return pl.pallas_call(
        body, name="input_grad", grid=(seq // tm,),
        out_shape=jax.ShapeDtypeStruct((seq, D_MODEL), F32),
        in_specs=[_rows(tm, 3 * A_WIDTH), _rows(tm, D_MODEL), _resident(wa.shape)],
        out_specs=_rows(tm, D_MODEL),
        compiler_params=_seq_params(),
    )(dp, dr0, wa)


def _in_proj_weight_grad(xb, dp, shard_cols):
    seq = xb.shape[0]
    tk = min(1024, seq)
    n_k = seq // tk

    def body(x_ref, dp_ref, out_ref):
        @pl.when(pl.program_id(1) == 0)
        def _():
            out_ref[...] = jnp.zeros_like(out_ref)

        out_ref[...] += _tn(x_ref[...], dp_ref[...])

    return pl.pallas_call(
        body, name="in_proj_weight_grad", grid=(N_DEV, n_k),
        out_shape=jax.ShapeDtypeStruct((N_DEV, D_MODEL, shard_cols), F32),
        in_specs=[pl.BlockSpec((tk, D_MODEL), lambda j, k: (k, 0)),
                  pl.BlockSpec((tk, shard_cols), lambda j, k: (k, j))],
        out_specs=pl.BlockSpec((None, D_MODEL, shard_cols), lambda j, k: (j, 0, 0)),
        compiler_params=pltpu.CompilerParams(dimension_semantics=("arbitrary", "arbitrary"),
                                             vmem_limit_bytes=VMEM_LIMIT_BYTES),
    )(xb, dp)


def _pair_exchange(grads):
    n = len(grads)

    def body(*refs):
        ins, landed = refs[:n], refs[n:2 * n]
        send_sems, recv_sems = refs[2 * n:]
        x, y, c = lax.axis_index("x"), lax.axis_index("y"), lax.axis_index("c")
        copies = []
        for a in range(n):
            copies.append(pltpu.make_async_remote_copy(
                src_ref=ins[a].at[:, pl.ds(1 - c, 1)], dst_ref=landed[a],
                send_sem=send_sems.at[a], recv_sem=recv_sems.at[a], device_id=(x, y, 1 - c), device_id_type=MESH))
            copies[-1].start()
        for cp in copies:
            cp.wait()

    return pl.pallas_call(
        body, name="pair_exchange", out_shape=[jax.ShapeDtypeStruct((4, 1) + g.shape[2:], F32) for g in grads],
        in_specs=[ANY] * n, out_specs=[ANY] * n,
        scratch_shapes=[pltpu.SemaphoreType.DMA((n,)), pltpu.SemaphoreType.DMA((n,))],
    )(*grads)


def _row_tile(rows, cols, itemsize=4):
    if rows * cols * itemsize <= (1 << 20):
        return rows
    t = 8
    while 2 * t * cols * itemsize <= (1 << 20) and rows % (2 * t) == 0:
        t *= 2
    assert rows % t == 0, (rows, cols)
    return t


def _pair_sum(place, grad, landed, dtype):
    _, _, rows, cols = grad.shape
    tr = _row_tile(rows, cols)

    def body(place_ref, a_ref, b_ref, out_ref):
        out_ref[...] = (a_ref[...] + b_ref[...]).astype(dtype)

    return pl.pallas_call(
        body, name="pair_sum", out_shape=jax.ShapeDtypeStruct((4, rows, cols), dtype),
        grid_spec=pltpu.PrefetchScalarGridSpec(
            num_scalar_prefetch=1, grid=(4, rows // tr),
            in_specs=[pl.BlockSpec((None, None, tr, cols), lambda k, i, pr: (k, pr[1], i, 0)),
                      pl.BlockSpec((None, None, tr, cols), lambda k, i, pr: (k, 0, i, 0))],
            out_specs=pl.BlockSpec((None, tr, cols), lambda k, i, pr: (k, i, 0))),
        compiler_params=pltpu.CompilerParams(dimension_semantics=("parallel", "parallel")),
    )(place, grad, landed)


def _chip_exchange(sums):
    n = len(sums)

    def body(*refs):
        s_in, got = refs[:n], refs[n:2 * n]
        send_sems, recv_sems = refs[2 * n:]
        x, y, c = lax.axis_index("x"), lax.axis_index("y"), lax.axis_index("c")
        chips = [(1 - x, y), (x, 1 - y), (1 - x, 1 - y)]
        copies = []
        for a in range(n):
            for k, (px, py) in enumerate(chips):
                copies.append(pltpu.make_async_remote_copy(
                    src_ref=s_in[a].at[2 * px + py], dst_ref=got[a].at[k],
                    send_sem=send_sems.at[a, k], recv_sem=recv_sems.at[a, k],
                    device_id=(px, py, c), device_id_type=MESH))
                copies[-1].start()
        for cp in copies:
            cp.wait()

    return pl.pallas_call(
        body, name="chip_exchange", out_shape=[jax.ShapeDtypeStruct((3,) + s.shape[1:], s.dtype) for s in sums],
        in_specs=[ANY] * n, out_specs=[ANY] * n,
        scratch_shapes=[pltpu.SemaphoreType.DMA((n, 3)), pltpu.SemaphoreType.DMA((n, 3))],
    )(*sums)


def _adamw_math(w, g, m, v):
    m = ADAM_B1 * m + (1.0 - ADAM_B1) * g
    v = ADAM_B2 * v + (1.0 - ADAM_B2) * (g * g)
    m_hat = m / (1.0 - ADAM_B1 ** ADAM_STEP)
    v_hat = v / (1.0 - ADAM_B2 ** ADAM_STEP)
    delta = -ADAM_LR * (m_hat / (jnp.sqrt(v_hat) + ADAM_EPS) + ADAM_WD * w)
    return delta, m, v


def _adamw_shard(place, grad, landed, got, w, m, v):
    rows, cols = w.shape
    tr = _row_tile(rows, cols)

    def body(place_ref, a_ref, b_ref, got_ref, w_ref, m_ref, v_ref, g_out, d_out, m_out, v_out):
        g = a_ref[...] + b_ref[...]
        for k in range(3):
            g = g + got_ref[k].astype(F32)
        g_out[...] = g
        d_out[...], m_out[...], v_out[...] = _adamw_math(w_ref[...], g, m_ref[...], v_ref[...])

    spec = pl.BlockSpec((tr, cols), lambda i, pr: (i, 0))
    out = jax.ShapeDtypeStruct((rows, cols), F32)
    return pl.pallas_call(
        body, name="adamw_shard", out_shape=[out] * 4,
        grid_spec=pltpu.PrefetchScalarGridSpec(
            num_scalar_prefetch=1, grid=(rows // tr,),
            in_specs=[pl.BlockSpec((None, None, tr, cols), lambda i, pr: (pr[0], pr[1], i, 0)),
                      pl.BlockSpec((None, None, tr, cols), lambda i, pr: (pr[0], 0, i, 0)),
                      pl.BlockSpec((3, tr, cols), lambda i, pr: (0, i, 0)), spec, spec, spec],
            out_specs=[spec] * 4),
        compiler_params=pltpu.CompilerParams(dimension_semantics=("parallel",)),
    )(place, grad, landed, got, w, m, v)


def _gather_small(grad, landed, got):
    rows, cols = got.shape[1:]

    def body(grad_ref, landed_ref, got_ref, out_ref, send_sems, recv_sems):
        x, y, c = lax.axis_index("x"), lax.axis_index("y"), lax.axis_index("c")
        me, sibling = (x, y, c), (x, y, 1 - c)
        chips = [(1 - x, y), (x, 1 - y), (1 - x, 1 - y)]

        def slot(px, py, pc):
            return out_ref.at[4 * px + 2 * py + pc]

        def copy(k, block, to):
            return pltpu.make_async_remote_copy(
                src_ref=slot(*block), dst_ref=slot(*block), send_sem=send_sems.at[k], recv_sem=recv_sems.at[k],
                device_id=to, device_id_type=MESH)

        mine = grad_ref[2 * x + y, c] + landed_ref[2 * x + y, 0]
        out_ref[4 * x + 2 * y + c] = ((mine + got_ref[0]) + got_ref[1]) + got_ref[2]
        first = [copy(0, me, sibling)] + [copy(1 + j, me, (*chip, c)) for j, chip in enumerate(chips)]
        for cp in first:
            cp.start()
        passed = [copy(4 + j, (*chip, c), sibling) for j, chip in enumerate(chips)]
        for j, chip in enumerate(chips):
            copy(1 + j, (*chip, c), me).wait_recv()
            passed[j].start()
        copy(0, sibling, me).wait_recv()
        for j, chip in enumerate(chips):
            copy(4 + j, (*chip, 1 - c), me).wait_recv()
        for cp in first + passed:
            cp.wait_send()

    return pl.pallas_call(
        body, name="gather_small", out_shape=jax.ShapeDtypeStruct((N_DEV, rows, cols), F32),
        in_specs=[VMEM] * 3, out_specs=VMEM,
        scratch_shapes=[pltpu.SemaphoreType.DMA((7,)), pltpu.SemaphoreType.DMA((7,))],
    )(grad, landed, got)


def _adamw_small(params):
    n = len(params)

    def body(*refs):
        ins, outs = refs[:4 * n], refs[4 * n:]
        for i in range(n):
            w_ref, g_ref, m_ref, v_ref = ins[4 * i:4 * i + 4]
            d, m, v = _adamw_math(w_ref[...], g_ref[...], m_ref[...], v_ref[...])
            outs[3 * i][...] = d
            outs[3 * i + 1][...] = m
            outs[3 * i + 2][...] = v

    flat = [a for p in params for a in p]
    out_shape = [jax.ShapeDtypeStruct(p[0].shape, F32) for p in params for _ in range(3)]
    outs = pl.pallas_call(
        body, name="adamw_small", out_shape=out_shape, in_specs=[VMEM] * (4 * n), out_specs=[VMEM] * (3 * n),
    )(*flat)
    return [tuple(outs[3 * i:3 * i + 3]) for i in range(n)]


def _pack_small(parts):
    rows = []
    for a in parts:
        r, c = a.shape
        rows.append(jnp.pad(a, ((0, (-r) % 8), (0, CHUNK - c))))
    flat = jnp.concatenate(rows, axis=0)
    return jnp.pad(flat, ((0, N_DEV * SMALL_ROWS - flat.shape[0]), (0, 0))).reshape(N_DEV, SMALL_ROWS, CHUNK)


def _unpack_small(flat, shapes):
    out, at = [], 0
    for r, c in shapes:
        out.append(flat[at:at + r, :c])
        at += r + (-r) % 8
    return out


def kernel(x, w_in_a, sgu_ln_g, sgu_ln_b, w_spatial, b_spatial, w_out_a, w_kv, w_in_b, attn_sinks, rel_bias, w_out_b, post_ln_g, post_ln_b, loss_target, m_w_in_a, m_sgu_ln_g, m_sgu_ln_b, m_w_spatial, m_b_spatial, m_w_out_a, m_w_kv, m_w_in_b, m_attn_sinks, m_rel_bias, m_w_out_b, m_post_ln_g, m_post_ln_b, v_w_in_a, v_sgu_ln_g, v_sgu_ln_b, v_w_spatial, v_b_spatial, v_w_out_a, v_w_kv, v_w_in_b, v_attn_sinks, v_rel_bias, v_w_out_b, v_post_ln_g, v_post_ln_b):
    seq = x.shape[1]
    me = 4 * lax.axis_index("x") + 2 * lax.axis_index("y") + lax.axis_index("c")
    bucket = jnp.asarray(_band_tables())

    ln_shard = jnp.concatenate([sgu_ln_g, sgu_ln_b], axis=0)
    x2, tgt = x[0], loss_target[0]
    order = jnp.stack([4 * px + 2 * py + pc for px, py, pc in _use_order()]).astype(jnp.int32)
    p, wa, woa, wkv, wb, wob, ln_all = _gather_in_proj(order, x2, w_in_a[0], w_out_a[0], w_kv, w_in_b[0], w_out_b[0],
                                                       ln_shard)
    woa = woa.reshape(A_WIDTH, D_MODEL)
    wkv = wkv.reshape(D_MODEL, 2 * KV_WIDTH)
    wob = wob.reshape(B_WIDTH, D_MODEL)
    ln_gb = jnp.transpose(ln_all, (1, 0, 2)).reshape(2, A_WIDTH)
    ws = w_spatial[0]
    bsp_t = jnp.pad(b_spatial[0].T, ((0, 0), (0, CHUNK - A_GROUPS)))
    bias = _bias_table(rel_bias, bucket)

    r0, xb = _fwd_a(x2, p, ln_gb, ws, bsp_t, woa)
    q, z, kv, dr1, loss, dg1, db1 = _fwd_b(r0, tgt, wb, wkv, wob, post_ln_g, post_ln_b, attn_sinks, bias)
    dh1, dwb, dwkv, dwob, dsink, dlog = _bwd_b(r0, dr1, q, z, kv, wb, wkv, wob, post_ln_g, post_ln_b, attn_sinks,
                                               bias)
    drel = _rel_bias_grad(dlog, bucket)
    dp, dr0, dwoa, dws, dbsp_t, dln, dpost0 = _bwd_a(dh1, r0, p, ln_gb, ws, bsp_t, woa, post_ln_g)
    dx = _input_grad(dp, dr0, wa)
    dwa = _in_proj_weight_grad(xb, dp, wa.shape[2])

    small_shapes = [(A_GROUPS * CHUNK, CHUNK), (A_GROUPS, CHUNK), (1, N_Q_HEADS), (REL_BUCKETS, N_Q_HEADS),
                    (16, CHUNK), (16, CHUNK), (16, CHUNK), (16, CHUNK)]
    small = _pack_small([
        dws.reshape(A_GROUPS * CHUNK, CHUNK), dbsp_t[:, :A_GROUPS].T, dsink[:, :N_Q_HEADS], drel[:, :N_Q_HEADS],
        jnp.concatenate([dpost0[0:1], dg1], axis=0).reshape(16, CHUNK),
        jnp.concatenate([dpost0[1:2], db1], axis=0).reshape(16, CHUNK),
        dln[0].reshape(16, CHUNK), dln[1].reshape(16, CHUNK)])
    full = [dwa, dwoa.reshape(N_DEV, A_WIDTH // N_DEV, D_MODEL), dwkv.reshape(N_DEV, D_MODEL // N_DEV, 2 * KV_WIDTH),
            dwb, dwob.reshape(N_DEV, B_WIDTH // N_DEV, D_MODEL), small]
    wire = [BF16] * 5 + [F32]
    place = jnp.stack([2 * lax.axis_index("x") + lax.axis_index("y"), lax.axis_index("c")]).astype(jnp.int32)
    full = [g.reshape((4, 2) + g.shape[1:]) for g in full]
    landed = _pair_exchange(full)
    got = _chip_exchange([_pair_sum(place, g, l, dt) for g, l, dt in zip(full, landed, wire)])

    big = [(w_in_a, m_w_in_a, v_w_in_a), (w_out_a, m_w_out_a, v_w_out_a), (w_kv, m_w_kv, v_w_kv),
           (w_in_b, m_w_in_b, v_w_in_b), (w_out_b, m_w_out_b, v_w_out_b)]
    big_out = []
    for i, (w, m, v) in enumerate(big):
        shape2 = got[i].shape[1:]
        res = _adamw_shard(place, full[i], landed[i], got[i], w.reshape(shape2), m.reshape(shape2),
                           v.reshape(shape2))
        big_out.append([r.reshape(w.shape) for r in res])

    small_all = _gather_small(full[5], landed[5], got[5]).reshape(N_DEV * SMALL_ROWS, CHUNK)
    g_ws, g_bsp, g_sink, g_rel, g_pg, g_pb, g_lng, g_lnb = _unpack_small(small_all, small_shapes)
    g_ws = g_ws.reshape(w_spatial.shape)
    g_bsp = g_bsp.reshape(b_spatial.shape)
    g_pg = g_pg.reshape(post_ln_g.shape)
    g_pb = g_pb.reshape(post_ln_b.shape)
    g_lng = lax.dynamic_slice(g_lng.reshape(1, A_WIDTH), (0, me * sgu_ln_g.shape[1]), sgu_ln_g.shape)
    g_lnb = lax.dynamic_slice(g_lnb.reshape(1, A_WIDTH), (0, me * sgu_ln_b.shape[1]), sgu_ln_b.shape)
    small_params = [(sgu_ln_g, g_lng, m_sgu_ln_g, v_sgu_ln_g), (sgu_ln_b, g_lnb, m_sgu_ln_b, v_sgu_ln_b),
                    (w_spatial, g_ws, m_w_spatial, v_w_spatial), (b_spatial, g_bsp, m_b_spatial, v_b_spatial),
                    (attn_sinks, g_sink, m_attn_sinks, v_attn_sinks), (rel_bias, g_rel, m_rel_bias, v_rel_bias),
                    (post_ln_g, g_pg, m_post_ln_g, v_post_ln_g), (post_ln_b, g_pb, m_post_ln_b, v_post_ln_b)]
    flat2 = [tuple(a.reshape(-1, a.shape[-1]) for a in p) for p in small_params]
    small_out = [tuple(r.reshape(p[0].shape) for r in res) for res, p in zip(_adamw_small(flat2), small_params)]

    loss_all = lax.psum(loss[0, 0], ("x", "y", "c"))
    order = [("big", 0), ("small", 0), ("small", 1), ("small", 2), ("small", 3), ("big", 1), ("big", 2), ("big", 3),
             ("small", 4), ("small", 5), ("big", 4), ("small", 6), ("small", 7)]
    grads, deltas, new_m, new_v = [], [], [], []
    for kind, i in order:
        if kind == "big":
            g, d, m, v = big_out[i]
        else:
            g = small_params[i][1]
            d, m, v = small_out[i]
        grads.append(g)
        deltas.append(d)
        new_m.append(m)
        new_v.append(v)
    return (loss_all, dx.reshape(x.shape), *grads, *deltas, *new_m, *new_v)
```

```python
import functools

import jax
import jax.numpy as jnp
import numpy as np
from jax import lax
from jax.experimental import pallas as pl
from jax.experimental.pallas import tpu as pltpu

F32 = jnp.float32
BF16 = jnp.bfloat16

D_MODEL = 1024
A_WIDTH = 2048
A_GROUPS = 8
A_GROUP_DIM = A_WIDTH // A_GROUPS
CHUNK = 128
HEAD_DIM = 64
N_Q_HEADS = 16
N_KV_HEADS = 2
PAIRS_PER_KV = N_Q_HEADS // N_KV_HEADS // 2
N_PAIRS = N_Q_HEADS // 2
B_WIDTH = N_Q_HEADS * HEAD_DIM
KV_WIDTH = N_KV_HEADS * HEAD_DIM
REL_BUCKETS = 32
REL_MAX_DIST = 128
DEPTH = 2
ALPHA = (2.0 * DEPTH) ** 0.25
LN_EPS = 1e-5
NEG_INF = -1e30
SCALE = HEAD_DIM ** -0.5

ADAM_LR = 0.001
ADAM_B1 = 0.9
ADAM_B2 = 0.999
ADAM_EPS = 1e-08
ADAM_WD = 0.01
ADAM_STEP = 10

N_DEV = 8
MESH = pl.DeviceIdType.MESH
TOKENS_IN_PROJ = 512
TOKENS_A = 256
TOKENS_B = 512
TOKENS_B_BWD = 256
TOKENS_DX = 512
VMEM_LIMIT_BYTES = 60 * 1024 * 1024
SMALL_ROWS = 144

ANY = pl.BlockSpec(memory_space=pl.ANY)
VMEM = pl.BlockSpec(memory_space=pltpu.VMEM)
SMEM = pl.BlockSpec(memory_space=pltpu.SMEM)


def _nn(a, b):
    return jnp.dot(a, b, preferred_element_type=F32)


def _nt(a, b):
    return lax.dot_general(a, b, (((1,), (1,)), ((), ())), preferred_element_type=F32)


def _tn(a, b):
    return lax.dot_general(a, b, (((0,), (0,)), ((), ())), preferred_element_type=F32)


def _resident(shape):
    nd = len(shape)
    return pl.BlockSpec(shape, lambda i: (0,) * nd, pipeline_mode=pl.Buffered(1))


def _rows(tm, width, reverse_of=None):
    if reverse_of is None:
        return pl.BlockSpec((tm, width), lambda i: (i, 0))
    return pl.BlockSpec((tm, width), lambda i: (reverse_of - 1 - i, 0))


def _seq_params():
    return pltpu.CompilerParams(dimension_semantics=("arbitrary",), vmem_limit_bytes=VMEM_LIMIT_BYTES)


def _ln_normalize(r):
    mu = jnp.mean(r, axis=-1, keepdims=True)
    xc = r - mu
    var = jnp.mean(xc * xc, axis=-1, keepdims=True)
    rstd = lax.rsqrt(var + LN_EPS)
    return xc * rstd, rstd


def _ln_backward(dxhat, xhat, rstd):
    m1 = jnp.mean(dxhat, axis=-1, keepdims=True)
    m2 = jnp.mean(dxhat * xhat, axis=-1, keepdims=True)
    return rstd * (dxhat - m1 - xhat * m2)


def _silu_and_grad(z):
    sig = jax.nn.sigmoid(z)
    return z * sig, sig * (1.0 + z * (1.0 - sig))


def _band_tables():
    t = np.arange(CHUNK, dtype=np.int32)[:, None]
    j = np.arange(2 * CHUNK, dtype=np.int32)[None, :]
    d = t + CHUNK - j
    in_window = (d >= 0) & (d < CHUNK)
    dc = np.clip(d, 0, REL_MAX_DIST - 1)
    max_exact = REL_BUCKETS // 2
    df = np.maximum(dc, 1).astype(np.float32)
    large = max_exact + (np.log(df / np.float32(max_exact)) / np.float32(np.log(REL_MAX_DIST / max_exact))
                         * np.float32(REL_BUCKETS - max_exact)).astype(np.int32)
    large = np.minimum(large, REL_BUCKETS - 1)
    bucket = np.where(dc < max_exact, dc, large)
    return np.where(in_window, bucket, -1).astype(np.int32)


def _use_order():
    x, y, c = lax.axis_index("x"), lax.axis_index("y"), lax.axis_index("c")
    blocks = [(x, y, c), (x, y, 1 - c)]
    for px, py in [(1 - x, y), (x, 1 - y), (1 - x, 1 - y)]:
        blocks += [(px, py, c), (px, py, 1 - c)]
    return blocks


ARRIVAL_OF_PHASE = (None, 0, 1, 4, 2, 5, 3, 6)


def _gather_in_proj(order, x, w_in_a, small, ln_gb):
    seq = x.shape[0]
    tm = min(TOKENS_IN_PROJ, seq)
    n_blocks = seq // tm
    shard_cols = w_in_a.shape[1]
    n = len(small) + 2

    def body(order_ref, x_ref, *refs):
        ins, refs = refs[:n], refs[n:]
        p_ref, outs, refs = refs[0], refs[1:n + 1], refs[n + 1:]
        wa_vmem, xb_vmem, stage = refs[0], refs[1], refs[2:n]
        send_sems, recv_sems, local_sems, out_sems = refs[n:]
        phase, blk = pl.program_id(0), pl.program_id(1)
        x_, y_, c_ = lax.axis_index("x"), lax.axis_index("y"), lax.axis_index("c")
        me, sibling = (x_, y_, c_), (x_, y_, 1 - c_)
        chips = [(1 - x_, y_), (x_, 1 - y_), (1 - x_, 1 - y_)]
        use = _use_order()

        def index(px, py, pc):
            return 4 * px + 2 * py + pc

        def slot(a, block):
            return (wa_vmem if a == 0 else outs[a]).at[index(*block)]

        def own(a):
            return slot(0, me) if a == 0 else (ins[a] if a == n - 1 else stage[a - 1])

        def copy(a, k, block, to, src=None):
            return pltpu.make_async_remote_copy(
                src_ref=slot(a, block) if src is None else src, dst_ref=slot(a, block),
                send_sem=send_sems.at[a, k], recv_sem=recv_sems.at[a, k], device_id=to, device_id_type=MESH)

        def keep(a):
            return pltpu.make_async_copy(own(a), slot(a, me), local_sems.at[a])

        def write_out(k):
            return pltpu.make_async_copy(slot(0, use[k]), outs[0].at[index(*use[k])], out_sems.at[k])

        def pass_on(a, j):
            copy(a, 1 + j, (*chips[j], c_), me).wait_recv()
            copy(a, 4 + j, (*chips[j], c_), sibling).start()

        def send(a, j):
            copy(a, 1 + j, me, (*chips[j], c_), own(a)).start()

        @pl.when((phase == 0) & (blk == 0))
        def _():
            wa_vmem[index(*me)] = ins[0][...].astype(BF16)
            for a in range(1, n - 1):
                stage[a - 1][...] = ins[a][...].astype(BF16)
            send(0, 0)
            send(0, 1)
            for a in range(n):
                copy(a, 0, me, sibling, own(a)).start()
                if a:
                    keep(a).start()
            write_out(0).start()

        for k in range(1, N_DEV):
            @pl.when((phase == k) & (blk == 0))
            def _(k=k):
                if k % 2 == 0:
                    pass_on(0, k // 2 - 1)
                else:
                    copy(0, ARRIVAL_OF_PHASE[k], use[k], me).wait_recv()
                write_out(k).start()
                if k == 2:
                    send(0, 2)
                    for a in range(1, n):
                        for j in range(3):
                            send(a, j)
                if k == N_DEV - 2:
                    for a in range(1, n):
                        pass_on(a, 0)
                        pass_on(a, 1)

        rows = pl.ds(pl.multiple_of(blk * tm, tm), tm)

        @pl.when(phase == 0)
        def _():
            xb_vmem[rows, :] = x_ref[...].astype(BF16)

        p_ref[...] = _nn(xb_vmem[rows, :], wa_vmem[order_ref[phase]])

        @pl.when((phase == N_DEV - 1) & (blk == n_blocks - 1))
        def _():
            for a in range(1, n):
                pass_on(a, 2)
            for a in range(1, n):
                copy(a, 0, sibling, me).wait_recv()
                for j, chip in enumerate(chips):
                    copy(a, 4 + j, (*chip, 1 - c_), me).wait_recv()
            for a in range(n):
                for k in range(7):
                    copy(a, k, me, me).wait_send()
                if a:
                    keep(a).wait()
            for k in range(N_DEV):
                write_out(k).wait()

    gathered = [jax.ShapeDtypeStruct((N_DEV,) + w_in_a.shape, BF16)]
    gathered += [jax.ShapeDtypeStruct((N_DEV,) + s.shape, BF16) for s in small]
    gathered.append(jax.ShapeDtypeStruct((N_DEV,) + ln_gb.shape, F32))
    return pl.pallas_call(
        body, name="gather_in_proj",
        out_shape=[jax.ShapeDtypeStruct((seq, N_DEV * shard_cols), F32)] + gathered,
        grid_spec=pltpu.PrefetchScalarGridSpec(
            num_scalar_prefetch=1, grid=(N_DEV, n_blocks),
            in_specs=[pl.BlockSpec((tm, D_MODEL), lambda k, i, order: (jnp.where(k == 0, i, n_blocks - 1), 0))]
            + [VMEM] * n,
            out_specs=[pl.BlockSpec((tm, shard_cols), lambda k, i, order: (i, order[k]))] + [ANY] * n,
            scratch_shapes=[pltpu.VMEM((N_DEV,) + w_in_a.shape, BF16), pltpu.VMEM((seq, D_MODEL), BF16)]
            + [pltpu.VMEM(s.shape, BF16) for s in small] + [
                pltpu.SemaphoreType.DMA((n, 7)), pltpu.SemaphoreType.DMA((n, 7)), pltpu.SemaphoreType.DMA((n,)),
                pltpu.SemaphoreType.DMA((N_DEV,))]),
        compiler_params=pltpu.CompilerParams(dimension_semantics=("arbitrary", "arbitrary"),
                                             vmem_limit_bytes=VMEM_LIMIT_BYTES),
    )(order, x, w_in_a, *small, ln_gb)


def _bias_table(rel_bias, bucket):
    def body(rel_ref, bucket_ref, out_ref):
        bk = bucket_ref[...]
        col = lax.broadcasted_iota(jnp.int32, bk.shape, 1)
        for h in range(N_Q_HEADS):
            def step(b, acc):
                return jnp.where(bk == b, rel_ref[b, h], acc)
            tab = lax.fori_loop(0, REL_BUCKETS, step, jnp.full(bk.shape, NEG_INF, F32))
            pair = h // 2
            rows = slice((pair % PAIRS_PER_KV) * CHUNK, (pair % PAIRS_PER_KV + 1) * CHUNK)
            half = slice((h % 2) * 2 * CHUNK, (h % 2 + 1) * 2 * CHUNK)
            out_ref[1, pair // PAIRS_PER_KV, rows, half] = tab
            out_ref[0, pair // PAIRS_PER_KV, rows, half] = jnp.where(col < CHUNK, NEG_INF, tab)

    return pl.pallas_call(
        body, name="bias_table",
        out_shape=jax.ShapeDtypeStruct((2, N_KV_HEADS, PAIRS_PER_KV * CHUNK, 4 * CHUNK), F32),
        in_specs=[SMEM, VMEM], out_specs=VMEM,
    )(rel_bias, bucket)


def _rel_bias_grad(dlogits, bucket):
    def body(dl_ref, bucket_ref, out_ref):
        bk = bucket_ref[...]
        row = lax.broadcasted_iota(jnp.int32, (REL_BUCKETS, CHUNK), 0)
        lane = lax.broadcasted_iota(jnp.int32, (REL_BUCKETS, CHUNK), 1)

        def step(b, acc):
            for h in range(N_Q_HEADS):
                pair = h // 2
                dl = dl_ref[pair // PAIRS_PER_KV, (pair % PAIRS_PER_KV) * CHUNK:(pair % PAIRS_PER_KV + 1) * CHUNK,
                            (h % 2) * 2 * CHUNK:(h % 2 + 1) * 2 * CHUNK]
                s = jnp.sum(jnp.where(bk == b, dl, 0.0))
                acc = jnp.where((row == b) & (lane == h), s, acc)
            return acc

        out_ref[...] = lax.fori_loop(0, REL_BUCKETS, step, jnp.zeros((REL_BUCKETS, CHUNK), F32))

    return pl.pallas_call(
        body, name="rel_bias_grad", out_shape=jax.ShapeDtypeStruct((REL_BUCKETS, CHUNK), F32),
        in_specs=[VMEM, VMEM], out_specs=VMEM,
    )(dlogits, bucket)


def _masked_spatial(ws_ref, g):
    t = lax.broadcasted_iota(jnp.int32, (CHUNK, CHUNK), 0)
    s = lax.broadcasted_iota(jnp.int32, (CHUNK, CHUNK), 1)
    return jnp.where(s <= t, ws_ref[g], 0.0).astype(BF16)


def _spatial_gate(ws_ref, bsp_t_ref, vn, s_ref, n_chunks):
    for g in range(A_GROUPS):
        w = _masked_spatial(ws_ref, g)
        cols = slice(g * A_GROUP_DIM, (g + 1) * A_GROUP_DIM)
        for ci in range(n_chunks):
            rows = slice(ci * CHUNK, (ci + 1) * CHUNK)
            s_ref[rows, cols] = _nn(w, vn[rows, cols]) + bsp_t_ref[:, g:g + 1]


def _fwd_a(x, p, ln_gb, w_spatial, bsp_t, woa, later):
    seq = x.shape[0]
    tm = min(TOKENS_A, seq)
    n_chunks = tm // CHUNK
    n_blocks = seq // tm
    n = len(later)

    def body(x_ref, p_ref, ln_ref, ws_ref, bsp_ref, woa_ref, *refs):
        shards, refs = refs[:n], refs[n:]
        r0_ref, xb_ref, gathered, refs = refs[0], refs[1], refs[2:2 + n], refs[2 + n:]
        s_ref, stage, (send_sems, recv_sems, local_sems) = refs[0], refs[1:1 + n], refs[1 + n:]
        step = pl.program_id(0)
        x_, y_, c_ = lax.axis_index("x"), lax.axis_index("y"), lax.axis_index("c")
        mine = 4 * x_ + 2 * y_ + c_

        def copies(a):
            out = [pltpu.make_async_copy(stage[a], gathered[a].at[mine], local_sems.at[a])]
            for r in range(1, N_DEV):
                to = (1 - x_ if r & 4 else x_, 1 - y_ if r & 2 else y_, 1 - c_ if r & 1 else c_)
                out.append(pltpu.make_async_remote_copy(
                    src_ref=stage[a], dst_ref=gathered[a].at[mine], send_sem=send_sems.at[a, r - 1],
                    recv_sem=recv_sems.at[a, r - 1], device_id=to, device_id_type=MESH))
            return out

        @pl.when(step == 0)
        def _():
            for a in range(n):
                stage[a][...] = shards[a][...].astype(BF16)
                for cp in copies(a):
                    cp.start()

        xv = x_ref[...]
        xb_ref[...] = xv.astype(BF16)
        vhat, _ = _ln_normalize(p_ref[:, A_WIDTH:2 * A_WIDTH])
        vn = (vhat * ln_ref[0:1, :] + ln_ref[1:2, :]).astype(BF16)
        _spatial_gate(ws_ref, bsp_ref, vn, s_ref, n_chunks)
        z = p_ref[:, 2 * A_WIDTH:]
        y = p_ref[:, :A_WIDTH] * s_ref[...] * (z * jax.nn.sigmoid(z))
        r0_ref[...] = ALPHA * xv + _nn(y.astype(BF16), woa_ref[...])

        @pl.when(step == n_blocks - 1)
        def _():
            for a in range(n):
                for cp in copies(a):
                    cp.wait()

    return pl.pallas_call(
        body, name="fwd_a", grid=(n_blocks,),
        out_shape=[jax.ShapeDtypeStruct((seq, D_MODEL), F32), jax.ShapeDtypeStruct((seq, D_MODEL), BF16)]
        + [jax.ShapeDtypeStruct((N_DEV,) + s.shape, BF16) for s in later],
        in_specs=[_rows(tm, D_MODEL), _rows(tm, 3 * A_WIDTH), _resident(ln_gb.shape), _resident(w_spatial.shape),
                  _resident(bsp_t.shape), _resident(woa.shape)] + [VMEM] * n,
        out_specs=[_rows(tm, D_MODEL), _rows(tm, D_MODEL)] + [ANY] * n,
        scratch_shapes=[pltpu.VMEM((tm, A_WIDTH), F32)] + [pltpu.VMEM(s.shape, BF16) for s in later] + [
            pltpu.SemaphoreType.DMA((n, N_DEV - 1)), pltpu.SemaphoreType.DMA((n, N_DEV - 1)),
            pltpu.SemaphoreType.DMA((n,))],
        compiler_params=_seq_params(),
    )(x, p, ln_gb, w_spatial, bsp_t, woa, *later)


def _expand_heads(kv):
    lo = lax.broadcasted_iota(jnp.int32, kv.shape, 1) < HEAD_DIM
    swapped = pltpu.roll(kv, HEAD_DIM, 1)
    zero = jnp.zeros_like(kv)
    return (jnp.where(lo, kv, zero).astype(BF16), jnp.where(lo, zero, swapped).astype(BF16),
            jnp.where(lo, swapped, zero).astype(BF16), jnp.where(lo, zero, kv).astype(BF16))


def _fold_heads(acc, kvh):
    top, bot = acc[:2 * CHUNK], acc[2 * CHUNK:]
    lo = lax.broadcasted_iota(jnp.int32, top.shape, 1) < HEAD_DIM
    t = jnp.where(lo, top, 0.0)
    b = jnp.where(lo, 0.0, bot)
    if kvh == 0:
        return t + pltpu.roll(b, HEAD_DIM, 1)
    return pltpu.roll(t, HEAD_DIM, 1) + b


STACK = PAIRS_PER_KV * CHUNK


def _pair_cols(kvh, pp):
    pair = kvh * PAIRS_PER_KV + pp
    return slice(pair * CHUNK, (pair + 1) * CHUNK)


def _stack_pairs(ref_or_val, rows, kvh):
    return jnp.concatenate([ref_or_val[rows, _pair_cols(kvh, pp)] for pp in range(PAIRS_PER_KV)], axis=0)


def _sink_columns(sink_ref):
    blk = lax.broadcasted_iota(jnp.int32, (STACK, 1), 0) // CHUNK
    out = []
    for kvh in range(N_KV_HEADS):
        halves = []
        for half in range(2):
            col = jnp.zeros((STACK, 1), F32)
            for pp in range(PAIRS_PER_KV):
                col = jnp.where(blk == pp, sink_ref[0, 2 * (kvh * PAIRS_PER_KV + pp) + half], col)
            halves.append(col)
        out.append(halves)
    return out


def _softmax_stack(logits, sink_cols):
    ps, sinks = [], []
    for half in range(2):
        lg = logits[:, half * 2 * CHUNK:(half + 1) * 2 * CHUNK]
        sk = sink_cols[half]
        m = jnp.maximum(jnp.max(lg, axis=-1, keepdims=True), sk)
        e = jnp.exp(lg - m)
        es = jnp.exp(sk - m)
        inv = 1.0 / (jnp.sum(e, axis=-1, keepdims=True) + es)
        ps.append(e * inv)
        sinks.append(es * inv)
    return jnp.concatenate(ps, axis=1), sinks


def _band_operands(ek_ref, ev_ref, ci, kvh):
    kblk = jnp.concatenate([ek_ref[ci, 2 * kvh], ek_ref[ci + 1, 2 * kvh],
                            ek_ref[ci, 2 * kvh + 1], ek_ref[ci + 1, 2 * kvh + 1]], axis=0)
    vblk = jnp.concatenate([ev_ref[ci, 2 * kvh], ev_ref[ci + 1, 2 * kvh],
                            ev_ref[ci, 2 * kvh + 1], ev_ref[ci + 1, 2 * kvh + 1]], axis=0)
    return kblk, vblk


def _fwd_b(r0, target, wb, wkv, wob, post_g, post_b, sinks, bias):
    seq = r0.shape[0]
    tm = min(TOKENS_B, seq)
    n_chunks = tm // CHUNK
    shard_cols = wb.shape[2]

    def body(r0_ref, tgt_ref, wb_ref, wkv_ref, wob_ref, pg_ref, pb_ref, sink_ref, bias_ref,
             q_ref, z_ref, kv_ref, dr1_ref, loss_ref, dg_ref, db_ref, ek_ref, ev_ref, o_ref):
        step = pl.program_id(0)

        @pl.when(step == 0)
        def _():
            ek_ref[0] = jnp.zeros(ek_ref.shape[1:], BF16)
            ev_ref[0] = jnp.zeros(ev_ref.shape[1:], BF16)
            loss_ref[...] = jnp.zeros_like(loss_ref)
            dg_ref[...] = jnp.zeros_like(dg_ref)
            db_ref[...] = jnp.zeros_like(db_ref)

        @pl.when(step > 0)
        def _():
            ek_ref[0] = ek_ref[n_chunks]
            ev_ref[0] = ev_ref[n_chunks]

        xhat0, _ = _ln_normalize(r0_ref[...])
        h1 = xhat0 * pg_ref[0:1, :] + pb_ref[0:1, :]
        h1b = h1.astype(BF16)
        for k in range(N_DEV):
            qz = _nn(h1b, wb_ref[k])
            lo, hi = k * shard_cols, (k + 1) * shard_cols
            if hi <= B_WIDTH:
                q_ref[:, lo:hi] = (qz * SCALE).astype(BF16)
            else:
                z_ref[:, lo - B_WIDTH:hi - B_WIDTH] = qz
        kv = _nn(h1b, wkv_ref[...])
        kv_ref[...] = kv.astype(BF16)
        for ci in range(n_chunks):
            rows = slice(ci * CHUNK, (ci + 1) * CHUNK)
            for i, e in enumerate(_expand_heads(kv[rows, :KV_WIDTH])):
                ek_ref[ci + 1, i] = e
            for i, e in enumerate(_expand_heads(kv[rows, KV_WIDTH:])):
                ev_ref[ci + 1, i] = e

        sink_cols = _sink_columns(sink_ref)
        for ci in range(n_chunks):
            rows = slice(ci * CHUNK, (ci + 1) * CHUNK)
            sel = jnp.where(step * n_chunks + ci > 0, 1, 0)
            for kvh in range(N_KV_HEADS):
                kblk, vblk = _band_operands(ek_ref, ev_ref, ci, kvh)
                logits = _nt(_stack_pairs(q_ref, rows, kvh), kblk) + bias_ref[sel, kvh]
                p, _ = _softmax_stack(logits, sink_cols[kvh])
                o = _nn(p.astype(BF16), vblk)
                for pp in range(PAIRS_PER_KV):
                    o_ref[rows, _pair_cols(kvh, pp)] = o[pp * CHUNK:(pp + 1) * CHUNK]

        z = z_ref[...]
        y = o_ref[...] * (z * jax.nn.sigmoid(z))
        r1 = ALPHA * h1 + _nn(y.astype(BF16), wob_ref[...])
        xhat1, rstd1 = _ln_normalize(r1)
        g1 = pg_ref[1:2, :]
        diff = xhat1 * g1 + pb_ref[1:2, :] - tgt_ref[...]
        loss_ref[...] += 0.5 * jnp.sum(jnp.mean(diff * diff, axis=-1, keepdims=True), axis=0, keepdims=True)
        dh2 = diff * (1.0 / D_MODEL)
        dg_ref[...] += jnp.sum(dh2 * xhat1, axis=0, keepdims=True)
        db_ref[...] += jnp.sum(dh2, axis=0, keepdims=True)
        dr1_ref[...] = _ln_backward(dh2 * g1, xhat1, rstd1)

    vec = pl.BlockSpec((1, D_MODEL), lambda i: (0, 0))
    return pl.pallas_call(
        body, name="fwd_b", grid=(seq // tm,),
        out_shape=[jax.ShapeDtypeStruct((seq, B_WIDTH), BF16), jax.ShapeDtypeStruct((seq, B_WIDTH), F32),
                   jax.ShapeDtypeStruct((seq, 2 * KV_WIDTH), BF16), jax.ShapeDtypeStruct((seq, D_MODEL), F32),
                   jax.ShapeDtypeStruct((1, 1), F32), jax.ShapeDtypeStruct((1, D_MODEL), F32),
                   jax.ShapeDtypeStruct((1, D_MODEL), F32)],
        in_specs=[_rows(tm, D_MODEL), _rows(tm, D_MODEL), _resident(wb.shape), _resident(wkv.shape),
                  _resident(wob.shape), _resident(post_g.shape), _resident(post_b.shape), SMEM,
                  _resident(bias.shape)],
        out_specs=[_rows(tm, B_WIDTH), _rows(tm, B_WIDTH), _rows(tm, 2 * KV_WIDTH), _rows(tm, D_MODEL),
                   pl.BlockSpec((1, 1), lambda i: (0, 0)), vec, vec],
        scratch_shapes=[pltpu.VMEM((n_chunks + 1, 4, CHUNK, CHUNK), BF16),
                        pltpu.VMEM((n_chunks + 1, 4, CHUNK, CHUNK), BF16),
                        pltpu.VMEM((tm, B_WIDTH), F32)],
        compiler_params=_seq_params(),
    )(r0, target, wb, wkv, wob, post_g, post_b, sinks, bias)


def _bwd_b(r0, dr1, q, z, kv, wb, wkv, wob, post_g, post_b, sinks, bias):
    seq = r0.shape[0]
    tm = min(TOKENS_B_BWD, seq)
    n_chunks = tm // CHUNK
    n_blocks = seq // tm
    shard_cols = wb.shape[2]

    def body(r0_ref, dr1_ref, q_ref, z_ref, kv_ref, kvp_ref, wb_ref, wkv_ref, wob_ref, pg_ref, pb_ref, sink_ref,
             bias_ref, dh1_ref, dwb_out, dwkv_out, dwob_out, dsink_ref, dl_out,
             dwb_acc, dwkv_acc, dwob_acc, dl_acc, dsink_acc, carry_ref, ek_ref, ev_ref, y_ref, dqz_ref, dkv_ref):
        step = pl.program_id(0)
        block = n_blocks - 1 - step

        @pl.when(step == 0)
        def _():
            dwb_acc[...] = jnp.zeros_like(dwb_acc)
            dwkv_acc[...] = jnp.zeros_like(dwkv_acc)
            dwob_acc[...] = jnp.zeros_like(dwob_acc)
            dl_acc[...] = jnp.zeros_like(dl_acc)
            carry_ref[...] = jnp.zeros_like(carry_ref)
            dsink_acc[...] = jnp.zeros_like(dsink_acc)

        xhat0, _ = _ln_normalize(r0_ref[...])
        h1b = (xhat0 * pg_ref[0:1, :] + pb_ref[0:1, :]).astype(BF16)
        dr1 = dr1_ref[...]
        dr1b = dr1.astype(BF16)
        dy = _nt(dr1b, wob_ref[...])
        sg, dsg = _silu_and_grad(z_ref[...])
        do_all = dy * sg
        dyz = dy * dsg

        for ci in range(n_chunks + 1):
            src = kvp_ref[...] if ci == 0 else kv_ref[(ci - 1) * CHUNK:ci * CHUNK, :]
            src = src.astype(F32)
            for i, e in enumerate(_expand_heads(src[:, :KV_WIDTH])):
                ek_ref[ci, i] = e
            for i, e in enumerate(_expand_heads(src[:, KV_WIDTH:])):
                ev_ref[ci, i] = e

        sink_cols = _sink_columns(sink_ref)
        for ci in reversed(range(n_chunks)):
            rows = slice(ci * CHUNK, (ci + 1) * CHUNK)
            sel = jnp.where(block * n_chunks + ci > 0, 1, 0)
            dk_band = jnp.zeros((2 * CHUNK, KV_WIDTH), F32)
            dv_band = jnp.zeros((2 * CHUNK, KV_WIDTH), F32)
            for kvh in range(N_KV_HEADS):
                kblk, vblk = _band_operands(ek_ref, ev_ref, ci, kvh)
                qs = _stack_pairs(q_ref, rows, kvh)
                p, psink = _softmax_stack(_nt(qs, kblk) + bias_ref[sel, kvh], sink_cols[kvh])
                pb16 = p.astype(BF16)
                o = _nn(pb16, vblk)
                y = (o * _stack_pairs(sg, rows, kvh)).astype(BF16)
                dz = (o * _stack_pairs(dyz, rows, kvh)).astype(BF16)
                dob = _stack_pairs(do_all, rows, kvh).astype(BF16)
                dp = _nt(dob, vblk)
                dlogits = []
                for half in range(2):
                    hs = slice(half * 2 * CHUNK, (half + 1) * 2 * CHUNK)
                    delta = jnp.sum(dp[:, hs] * p[:, hs], axis=-1, keepdims=True)
                    dlogits.append(p[:, hs] * (dp[:, hs] - delta))
                    dsink_acc[2 * kvh + half] += psink[half] * delta
                dlog = jnp.concatenate(dlogits, axis=1)
                dl_acc[kvh] += dlog
                dsc = dlog.astype(BF16)
                dq = (_nn(dsc, kblk) * SCALE).astype(BF16)
                for pp in range(PAIRS_PER_KV):
                    blk = slice(pp * CHUNK, (pp + 1) * CHUNK)
                    cols = _pair_cols(kvh, pp)
                    y_ref[rows, cols] = y[blk]
                    dqz_ref[rows, cols] = dq[blk]
                    dqz_ref[rows, B_WIDTH + cols.start:B_WIDTH + cols.stop] = dz[blk]
                dk_band = dk_band + _fold_heads(_tn(dsc, qs), kvh)
                dv_band = dv_band + _fold_heads(_tn(pb16, dob), kvh)
            band = jnp.concatenate([dk_band, dv_band], axis=1)
            dkv_ref[rows, :] = (band[CHUNK:] + carry_ref[...]).astype(BF16)
            carry_ref[...] = band[:CHUNK]

        dqz = dqz_ref[...]
        dkv = dkv_ref[...]
        dh1 = ALPHA * dr1 + _nt(dkv, wkv_ref[...])
        for k in range(N_DEV):
            part = dqz[:, k * shard_cols:(k + 1) * shard_cols]
            dh1 = dh1 + _nt(part, wb_ref[k])
            dwb_acc[k] += _tn(h1b, part)
        dh1_ref[...] = dh1
        dwkv_acc[...] += _tn(h1b, dkv)
        dwob_acc[...] += _tn(y_ref[...], dr1b)

        @pl.when(step == n_blocks - 1)
        def _():
            pltpu.sync_copy(dwb_acc, dwb_out)
            pltpu.sync_copy(dwkv_acc, dwkv_out)
            pltpu.sync_copy(dwob_acc, dwob_out)
            pltpu.sync_copy(dl_acc, dl_out)
            lane = lax.broadcasted_iota(jnp.int32, (1, CHUNK), 1)
            dsink = jnp.zeros((1, CHUNK), F32)
            for kvh in range(N_KV_HEADS):
                for half in range(2):
                    for pp in range(PAIRS_PER_KV):
                        head = 2 * (kvh * PAIRS_PER_KV + pp) + half
                        tot = jnp.sum(dsink_acc[2 * kvh + half, pp * CHUNK:(pp + 1) * CHUNK, :], axis=0, keepdims=True)
                        dsink = dsink - jnp.where(lane == head, tot, 0.0)
            dsink_ref[...] = dsink

    rev = functools.partial(_rows, tm, reverse_of=n_blocks)
    prev_chunk = pl.BlockSpec(
        (CHUNK, 2 * KV_WIDTH), lambda i: (jnp.maximum((n_blocks - 1 - i) * n_chunks - 1, 0), 0))
    return pl.pallas_call(
        body, name="bwd_b", grid=(n_blocks,),
        out_shape=[jax.ShapeDtypeStruct((seq, D_MODEL), F32), jax.ShapeDtypeStruct(wb.shape, F32),
                   jax.ShapeDtypeStruct(wkv.shape, F32), jax.ShapeDtypeStruct(wob.shape, F32),
                   jax.ShapeDtypeStruct((1, CHUNK), F32),
                   jax.ShapeDtypeStruct((N_KV_HEADS, STACK, 4 * CHUNK), F32)],
        in_specs=[rev(D_MODEL), rev(D_MODEL), rev(B_WIDTH), rev(B_WIDTH), rev(2 * KV_WIDTH), prev_chunk,
                  _resident(wb.shape), _resident(wkv.shape), _resident(wob.shape), _resident(post_g.shape),
                  _resident(post_b.shape), SMEM, _resident(bias.shape)],
        out_specs=[rev(D_MODEL), ANY, ANY, ANY, pl.BlockSpec((1, CHUNK), lambda i: (0, 0)), ANY],
        scratch_shapes=[pltpu.VMEM(wb.shape, F32), pltpu.VMEM(wkv.shape, F32), pltpu.VMEM(wob.shape, F32),
                        pltpu.VMEM((N_KV_HEADS, STACK, 4 * CHUNK), F32), pltpu.VMEM((2 * N_KV_HEADS, STACK, 1), F32),
                        pltpu.VMEM((CHUNK, 2 * KV_WIDTH), F32),
                        pltpu.VMEM((n_chunks + 1, 4, CHUNK, CHUNK), BF16),
                        pltpu.VMEM((n_chunks + 1, 4, CHUNK, CHUNK), BF16),
                        pltpu.VMEM((tm, B_WIDTH), BF16), pltpu.VMEM((tm, 2 * B_WIDTH), BF16),
                        pltpu.VMEM((tm, 2 * KV_WIDTH), BF16)],
        compiler_params=_seq_params(),
    )(r0, dr1, q, z, kv, kv, wb, wkv, wob, post_g, post_b, sinks, bias)


def _bwd_a(dh1, r0, p, ln_gb, w_spatial, bsp_t, woa, post_g):
    seq = r0.shape[0]
    tm = min(TOKENS_A, seq)
    n_chunks = tm // CHUNK
    n_blocks = seq // tm

    def body(dh1_ref, r0_ref, p_ref, ln_ref, ws_ref, bsp_ref, woa_ref, pg_ref,
             dp_ref, dr0_ref, dwoa_out, dws_ref, dbsp_ref, dln_ref, dpost_ref,
             dwoa_acc, s_ref, ds_ref, dvn_ref):
        step = pl.program_id(0)

        @pl.when(step == 0)
        def _():
            dwoa_acc[...] = jnp.zeros_like(dwoa_acc)
            dws_ref[...] = jnp.zeros_like(dws_ref)
            dbsp_ref[...] = jnp.zeros_like(dbsp_ref)
            dln_ref[...] = jnp.zeros_like(dln_ref)
            dpost_ref[...] = jnp.zeros_like(dpost_ref)

        dh1 = dh1_ref[...]
        xhat0, rstd0 = _ln_normalize(r0_ref[...])
        dpost_ref[0:1, :] += jnp.sum(dh1 * xhat0, axis=0, keepdims=True)
        dpost_ref[1:2, :] += jnp.sum(dh1, axis=0, keepdims=True)
        dr0 = _ln_backward(dh1 * pg_ref[0:1, :], xhat0, rstd0)
        dr0_ref[...] = dr0
        dr0b = dr0.astype(BF16)

        vhat, rstd_v = _ln_normalize(p_ref[:, A_WIDTH:2 * A_WIDTH])
        ln_g = ln_ref[0:1, :]
        vn = (vhat * ln_g + ln_ref[1:2, :]).astype(BF16)
        _spatial_gate(ws_ref, bsp_ref, vn, s_ref, n_chunks)
        u = p_ref[:, :A_WIDTH]
        s = s_ref[...]
        sg, dsg = _silu_and_grad(p_ref[:, 2 * A_WIDTH:])
        dwoa_acc[...] += _tn((u * s * sg).astype(BF16), dr0b)
        dy = _nt(dr0b, woa_ref[...])
        dp_ref[:, :A_WIDTH] = (dy * s * sg).astype(BF16)
        dp_ref[:, 2 * A_WIDTH:] = (dy * u * s * dsg).astype(BF16)
        ds_ref[...] = dy * u * sg

        lane = lax.broadcasted_iota(jnp.int32, (CHUNK, CHUNK), 1)
        dbsp = jnp.zeros((CHUNK, CHUNK), F32)
        for g in range(A_GROUPS):
            w = _masked_spatial(ws_ref, g)
            cols = slice(g * A_GROUP_DIM, (g + 1) * A_GROUP_DIM)
            dws = jnp.zeros((CHUNK, CHUNK), F32)
            for ci in range(n_chunks):
                rows = slice(ci * CHUNK, (ci + 1) * CHUNK)
                ds = ds_ref[rows, cols]
                dsb = ds.astype(BF16)
                dvn_ref[rows, cols] = _tn(w, dsb)
                dws = dws + _nt(dsb, vn[rows, cols])
                dbsp = dbsp + jnp.where(lane == g, jnp.sum(ds, axis=-1, keepdims=True), 0.0)
            dws_ref[g] += dws
        dbsp_ref[...] += dbsp

        dvn = dvn_ref[...]
        dln_ref[0:1, :] += jnp.sum(dvn * vhat, axis=0, keepdims=True)
        dln_ref[1:2, :] += jnp.sum(dvn, axis=0, keepdims=True)
        dp_ref[:, A_WIDTH:2 * A_WIDTH] = _ln_backward(dvn * ln_g, vhat, rstd_v).astype(BF16)

        @pl.when(step == n_blocks - 1)
        def _():
            t = lax.broadcasted_iota(jnp.int32, (CHUNK, CHUNK), 0)
            for g in range(A_GROUPS):
                dws_ref[g] = jnp.where(lane <= t, dws_ref[g], 0.0)
            pltpu.sync_copy(dwoa_acc, dwoa_out)

    def fixed(shape):
        nd = len(shape)
        return pl.BlockSpec(shape, lambda i: (0,) * nd)

    return pl.pallas_call(
        body, name="bwd_a", grid=(n_blocks,),
        out_shape=[jax.ShapeDtypeStruct((seq, 3 * A_WIDTH), BF16), jax.ShapeDtypeStruct((seq, D_MODEL), F32),
                   jax.ShapeDtypeStruct(woa.shape, F32), jax.ShapeDtypeStruct(w_spatial.shape, F32),
                   jax.ShapeDtypeStruct((CHUNK, CHUNK), F32), jax.ShapeDtypeStruct((2, A_WIDTH), F32),
                   jax.ShapeDtypeStruct((2, D_MODEL), F32)],
        in_specs=[_rows(tm, D_MODEL), _rows(tm, D_MODEL), _rows(tm, 3 * A_WIDTH), _resident(ln_gb.shape),
                  _resident(w_spatial.shape), _resident(bsp_t.shape), _resident(woa.shape),
                  _resident(post_g.shape)],
        out_specs=[_rows(tm, 3 * A_WIDTH), _rows(tm, D_MODEL), ANY, fixed(w_spatial.shape),
                   fixed((CHUNK, CHUNK)), fixed((2, A_WIDTH)), fixed((2, D_MODEL))],
        scratch_shapes=[pltpu.VMEM(woa.shape, F32), pltpu.VMEM((tm, A_WIDTH), F32),
                        pltpu.VMEM((tm, A_WIDTH), F32), pltpu.VMEM((tm, A_WIDTH), F32)],
        compiler_params=_seq_params(),
    )(dh1, r0, p, ln_gb, w_spatial, bsp_t, woa, post_g)


def _input_grad(dp, dr0, wa):
    seq = dr0.shape[0]
    tm = min(TOKENS_DX, seq)
    shard_cols = wa.shape[2]

    def body(dp_ref, dr0_ref, wa_ref, dx_ref):
        dx = ALPHA * dr0_ref[...]
        for k in range(N_DEV):
            dx = dx + _nt(dp_ref[:, k * shard_cols:(k + 1) * shard_cols], wa_ref[k])
        dx_ref[...] = dx

    return pl.pallas_call(
        body, name="input_grad", grid=(seq // tm,),
        out_shape=jax.ShapeDtypeStruct((seq, D_MODEL), F32),
        in_specs=[_rows(tm, 3 * A_WIDTH), _rows(tm, D_MODEL), _resident(wa.shape)],
        out_specs=_rows(tm, D_MODEL),
        compiler_params=_seq_params(),
    )(dp, dr0, wa)


def _in_proj_weight_grad(xb, dp, shard_cols):
    seq = xb.shape[0]
    tk = min(1024, seq)
    n_k = seq // tk

    def body(x_ref, dp_ref, out_ref):
        @pl.when(pl.program_id(1) == 0)
        def _():
            out_ref[...] = jnp.zeros_like(out_ref)

        out_ref[...] += _tn(x_ref[...], dp_ref[...])

    return pl.pallas_call(
        body, name="in_proj_weight_grad", grid=(N_DEV, n_k),
        out_shape=jax.ShapeDtypeStruct((N_DEV, D_MODEL, shard_cols), F32),
        in_specs=[pl.BlockSpec((tk, D_MODEL), lambda j, k: (k, 0)),
                  pl.BlockSpec((tk, shard_cols), lambda j, k: (k, j))],
        out_specs=pl.BlockSpec((None, D_MODEL, shard_cols), lambda j, k: (j, 0, 0)),
        compiler_params=pltpu.CompilerParams(dimension_semantics=("arbitrary", "arbitrary"),
                                             vmem_limit_bytes=VMEM_LIMIT_BYTES),
    )(xb, dp)


def _pair_exchange(grads):
    n = len(grads)

    def body(*refs):
        ins, landed = refs[:n], refs[n:2 * n]
        send_sems, recv_sems = refs[2 * n:]
        x, y, c = lax.axis_index("x"), lax.axis_index("y"), lax.axis_index("c")
        copies = []
        for a in range(n):
            copies.append(pltpu.make_async_remote_copy(
                src_ref=ins[a].at[:, pl.ds(1 - c, 1)], dst_ref=landed[a],
                send_sem=send_sems.at[a], recv_sem=recv_sems.at[a], device_id=(x, y, 1 - c), device_id_type=MESH))
            copies[-1].start()
        for cp in copies:
            cp.wait()

    return pl.pallas_call(
        body, name="pair_exchange", out_shape=[jax.ShapeDtypeStruct((4, 1) + g.shape[2:], F32) for g in grads],
        in_specs=[ANY] * n, out_specs=[ANY] * n,
        scratch_shapes=[pltpu.SemaphoreType.DMA((n,)), pltpu.SemaphoreType.DMA((n,))],
    )(*grads)


def _row_tile(rows, cols, itemsize=4):
    if rows * cols * itemsize <= (1 << 20):
        return rows
    t = 8
    while 2 * t * cols * itemsize <= (1 << 20) and rows % (2 * t) == 0:
        t *= 2
    assert rows % t == 0, (rows, cols)
    return t


def _pair_sum(place, grad, landed, dtype):
    _, _, rows, cols = grad.shape
    tr = _row_tile(rows, cols)

    def body(place_ref, a_ref, b_ref, out_ref):
        out_ref[...] = (a_ref[...] + b_ref[...]).astype(dtype)

    return pl.pallas_call(
        body, name="pair_sum", out_shape=jax.ShapeDtypeStruct((4, rows, cols), dtype),
        grid_spec=pltpu.PrefetchScalarGridSpec(
            num_scalar_prefetch=1, grid=(4, rows // tr),
            in_specs=[pl.BlockSpec((None, None, tr, cols), lambda k, i, pr: (k, pr[1], i, 0)),
                      pl.BlockSpec((None, None, tr, cols), lambda k, i, pr: (k, 0, i, 0))],
            out_specs=pl.BlockSpec((None, tr, cols), lambda k, i, pr: (k, i, 0))),
        compiler_params=pltpu.CompilerParams(dimension_semantics=("parallel", "parallel")),
    )(place, grad, landed)


def _chip_exchange(sums):
    n = len(sums)

    def body(*refs):
        s_in, got = refs[:n], refs[n:2 * n]
        send_sems, recv_sems = refs[2 * n:]
        x, y, c = lax.axis_index("x"), lax.axis_index("y"), lax.axis_index("c")
        chips = [(1 - x, y), (x, 1 - y), (1 - x, 1 - y)]
        copies = []
        for a in range(n):
            for k, (px, py) in enumerate(chips):
                copies.append(pltpu.make_async_remote_copy(
                    src_ref=s_in[a].at[2 * px + py], dst_ref=got[a].at[k],
                    send_sem=send_sems.at[a, k], recv_sem=recv_sems.at[a, k],
                    device_id=(px, py, c), device_id_type=MESH))
                copies[-1].start()
        for cp in copies:
            cp.wait()

    return pl.pallas_call(
        body, name="chip_exchange", out_shape=[jax.ShapeDtypeStruct((3,) + s.shape[1:], s.dtype) for s in sums],
        in_specs=[ANY] * n, out_specs=[ANY] * n,
        scratch_shapes=[pltpu.SemaphoreType.DMA((n, 3)), pltpu.SemaphoreType.DMA((n, 3))],
    )(*sums)


def _adamw_math(w, g, m, v):
    m = ADAM_B1 * m + (1.0 - ADAM_B1) * g
    v = ADAM_B2 * v + (1.0 - ADAM_B2) * (g * g)
    m_hat = m / (1.0 - ADAM_B1 ** ADAM_STEP)
    v_hat = v / (1.0 - ADAM_B2 ** ADAM_STEP)
    delta = -ADAM_LR * (m_hat / (jnp.sqrt(v_hat) + ADAM_EPS) + ADAM_WD * w)
    return delta, m, v


def _adamw_shard(place, grad, landed, got, w, m, v):
    rows, cols = w.shape
    tr = _row_tile(rows, cols)

    def body(place_ref, a_ref, b_ref, got_ref, w_ref, m_ref, v_ref, g_out, d_out, m_out, v_out):
        g = a_ref[...] + b_ref[...]
        for k in range(3):
            g = g + got_ref[k].astype(F32)
        g_out[...] = g
        d_out[...], m_out[...], v_out[...] = _adamw_math(w_ref[...], g, m_ref[...], v_ref[...])

    spec = pl.BlockSpec((tr, cols), lambda i, pr: (i, 0))
    out = jax.ShapeDtypeStruct((rows, cols), F32)
    return pl.pallas_call(
        body, name="adamw_shard", out_shape=[out] * 4,
        grid_spec=pltpu.PrefetchScalarGridSpec(
            num_scalar_prefetch=1, grid=(rows // tr,),
            in_specs=[pl.BlockSpec((None, None, tr, cols), lambda i, pr: (pr[0], pr[1], i, 0)),
                      pl.BlockSpec((None, None, tr, cols), lambda i, pr: (pr[0], 0, i, 0)),
                      pl.BlockSpec((3, tr, cols), lambda i, pr: (0, i, 0)), spec, spec, spec],
            out_specs=[spec] * 4),
        compiler_params=pltpu.CompilerParams(dimension_semantics=("parallel",)),
    )(place, grad, landed, got, w, m, v)


def _gather_small(grad, landed, got):
    rows, cols = got.shape[1:]

    def body(grad_ref, landed_ref, got_ref, out_ref, send_sems, recv_sems):
        x, y, c = lax.axis_index("x"), lax.axis_index("y"), lax.axis_index("c")
        me, sibling = (x, y, c), (x, y, 1 - c)
        chips = [(1 - x, y), (x, 1 - y), (1 - x, 1 - y)]

        def slot(px, py, pc):
            return out_ref.at[4 * px + 2 * py + pc]

        def copy(k, block, to):
            return pltpu.make_async_remote_copy(
                src_ref=slot(*block), dst_ref=slot(*block), send_sem=send_sems.at[k], recv_sem=recv_sems.at[k],
                device_id=to, device_id_type=MESH)

        mine = grad_ref[2 * x + y, c] + landed_ref[2 * x + y, 0]
        out_ref[4 * x + 2 * y + c] = ((mine + got_ref[0]) + got_ref[1]) + got_ref[2]
        first = [copy(0, me, sibling)] + [copy(1 + j, me, (*chip, c)) for j, chip in enumerate(chips)]
        for cp in first:
            cp.start()
        passed = [copy(4 + j, (*chip, c), sibling) for j, chip in enumerate(chips)]
        for j, chip in enumerate(chips):
            copy(1 + j, (*chip, c), me).wait_recv()
            passed[j].start()
        copy(0, sibling, me).wait_recv()
        for j, chip in enumerate(chips):
            copy(4 + j, (*chip, 1 - c), me).wait_recv()
        for cp in first + passed:
            cp.wait_send()

    return pl.pallas_call(
        body, name="gather_small", out_shape=jax.ShapeDtypeStruct((N_DEV, rows, cols), F32),
        in_specs=[VMEM] * 3, out_specs=VMEM,
        scratch_shapes=[pltpu.SemaphoreType.DMA((7,)), pltpu.SemaphoreType.DMA((7,))],
    )(grad, landed, got)


def _adamw_small(params):
    n = len(params)

    def body(*refs):
        ins, outs = refs[:4 * n], refs[4 * n:]
        for i in range(n):
            w_ref, g_ref, m_ref, v_ref = ins[4 * i:4 * i + 4]
            d, m, v = _adamw_math(w_ref[...], g_ref[...], m_ref[...], v_ref[...])
            outs[3 * i][...] = d
            outs[3 * i + 1][...] = m
            outs[3 * i + 2][...] = v

    flat = [a for p in params for a in p]
    out_shape = [jax.ShapeDtypeStruct(p[0].shape, F32) for p in params for _ in range(3)]
    outs = pl.pallas_call(
        body, name="adamw_small", out_shape=out_shape, in_specs=[VMEM] * (4 * n), out_specs=[VMEM] * (3 * n),
    )(*flat)
    return [tuple(outs[3 * i:3 * i + 3]) for i in range(n)]


def _pack_small(parts):
    rows = []
    for a in parts:
        r, c = a.shape
        rows.append(jnp.pad(a, ((0, (-r) % 8), (0, CHUNK - c))))
    flat = jnp.concatenate(rows, axis=0)
    return jnp.pad(flat, ((0, N_DEV * SMALL_ROWS - flat.shape[0]), (0, 0))).reshape(N_DEV, SMALL_ROWS, CHUNK)


def _unpack_small(flat, shapes):
    out, at = [], 0
    for r, c in shapes:
        out.append(flat[at:at + r, :c])
        at += r + (-r) % 8
    return out


def kernel(x, w_in_a, sgu_ln_g, sgu_ln_b, w_spatial, b_spatial, w_out_a, w_kv, w_in_b, attn_sinks, rel_bias, w_out_b, post_ln_g, post_ln_b, loss_target, m_w_in_a, m_sgu_ln_g, m_sgu_ln_b, m_w_spatial, m_b_spatial, m_w_out_a, m_w_kv, m_w_in_b, m_attn_sinks, m_rel_bias, m_w_out_b, m_post_ln_g, m_post_ln_b, v_w_in_a, v_sgu_ln_g, v_sgu_ln_b, v_w_spatial, v_b_spatial, v_w_out_a, v_w_kv, v_w_in_b, v_attn_sinks, v_rel_bias, v_w_out_b, v_post_ln_g, v_post_ln_b):
    seq = x.shape[1]
    me = 4 * lax.axis_index("x") + 2 * lax.axis_index("y") + lax.axis_index("c")
    bucket = jnp.asarray(_band_tables())

    ln_shard = jnp.concatenate([sgu_ln_g, sgu_ln_b], axis=0)
    x2, tgt = x[0], loss_target[0]
    order = jnp.stack([4 * px + 2 * py + pc for px, py, pc in _use_order()]).astype(jnp.int32)
    p, wa, woa, ln_all = _gather_in_proj(order, x2, w_in_a[0], (w_out_a[0],), ln_shard)
    woa = woa.reshape(A_WIDTH, D_MODEL)
    ln_gb = jnp.transpose(ln_all, (1, 0, 2)).reshape(2, A_WIDTH)
    ws = w_spatial[0]
    bsp_t = jnp.pad(b_spatial[0].T, ((0, 0), (0, CHUNK - A_GROUPS)))
    bias = _bias_table(rel_bias, bucket)

    r0, xb, wkv, wb, wob = _fwd_a(x2, p, ln_gb, ws, bsp_t, woa, (w_kv, w_in_b[0], w_out_b[0]))
    wkv = wkv.reshape(D_MODEL, 2 * KV_WIDTH)
    wob = wob.reshape(B_WIDTH, D_MODEL)
    q, z, kv, dr1, loss, dg1, db1 = _fwd_b(r0, tgt, wb, wkv, wob, post_ln_g, post_ln_b, attn_sinks, bias)
    dh1, dwb, dwkv, dwob, dsink, dlog = _bwd_b(r0, dr1, q, z, kv, wb, wkv, wob, post_ln_g, post_ln_b, attn_sinks,
                                               bias)
    drel = _rel_bias_grad(dlog, bucket)
    dp, dr0, dwoa, dws, dbsp_t, dln, dpost0 = _bwd_a(dh1, r0, p, ln_gb, ws, bsp_t, woa, post_ln_g)
    dx = _input_grad(dp, dr0, wa)
    dwa = _in_proj_weight_grad(xb, dp, wa.shape[2])

    small_shapes = [(A_GROUPS * CHUNK, CHUNK), (A_GROUPS, CHUNK), (1, N_Q_HEADS), (REL_BUCKETS, N_Q_HEADS),
                    (16, CHUNK), (16, CHUNK), (16, CHUNK), (16, CHUNK)]
    small = _pack_small([
        dws.reshape(A_GROUPS * CHUNK, CHUNK), dbsp_t[:, :A_GROUPS].T, dsink[:, :N_Q_HEADS], drel[:, :N_Q_HEADS],
        jnp.concatenate([dpost0[0:1], dg1], axis=0).reshape(16, CHUNK),
        jnp.concatenate([dpost0[1:2], db1], axis=0).reshape(16, CHUNK),
        dln[0].reshape(16, CHUNK), dln[1].reshape(16, CHUNK)])
    full = [dwa, dwoa.reshape(N_DEV, A_WIDTH // N_DEV, D_MODEL), dwkv.reshape(N_DEV, D_MODEL // N_DEV, 2 * KV_WIDTH),
            dwb, dwob.reshape(N_DEV, B_WIDTH // N_DEV, D_MODEL), small]
    wire = [BF16] * 5 + [F32]
    place = jnp.stack([2 * lax.axis_index("x") + lax.axis_index("y"), lax.axis_index("c")]).astype(jnp.int32)
    full = [g.reshape((4, 2) + g.shape[1:]) for g in full]
    landed = _pair_exchange(full)
    got = _chip_exchange([_pair_sum(place, g, l, dt) for g, l, dt in zip(full, landed, wire)])

    big = [(w_in_a, m_w_in_a, v_w_in_a), (w_out_a, m_w_out_a, v_w_out_a), (w_kv, m_w_kv, v_w_kv),
           (w_in_b, m_w_in_b, v_w_in_b), (w_out_b, m_w_out_b, v_w_out_b)]
    big_out = []
    for i, (w, m, v) in enumerate(big):
        shape2 = got[i].shape[1:]
        res = _adamw_shard(place, full[i], landed[i], got[i], w.reshape(shape2), m.reshape(shape2),
                           v.reshape(shape2))
        big_out.append([r.reshape(w.shape) for r in res])

    small_all = _gather_small(full[5], landed[5], got[5]).reshape(N_DEV * SMALL_ROWS, CHUNK)
    g_ws, g_bsp, g_sink, g_rel, g_pg, g_pb, g_lng, g_lnb = _unpack_small(small_all, small_shapes)
    g_ws = g_ws.reshape(w_spatial.shape)
    g_bsp = g_bsp.reshape(b_spatial.shape)
    g_pg = g_pg.reshape(post_ln_g.shape)
    g_pb = g_pb.reshape(post_ln_b.shape)
    g_lng = lax.dynamic_slice(g_lng.reshape(1, A_WIDTH), (0, me * sgu_ln_g.shape[1]), sgu_ln_g.shape)
    g_lnb = lax.dynamic_slice(g_lnb.reshape(1, A_WIDTH), (0, me * sgu_ln_b.shape[1]), sgu_ln_b.shape)
    small_params = [(sgu_ln_g, g_lng, m_sgu_ln_g, v_sgu_ln_g), (sgu_ln_b, g_lnb, m_sgu_ln_b, v_sgu_ln_b),
                    (w_spatial, g_ws, m_w_spatial, v_w_spatial), (b_spatial, g_bsp, m_b_spatial, v_b_spatial),
                    (attn_sinks, g_sink, m_attn_sinks, v_attn_sinks), (rel_bias, g_rel, m_rel_bias, v_rel_bias),
                    (post_ln_g, g_pg, m_post_ln_g, v_post_ln_g), (post_ln_b, g_pb, m_post_ln_b, v_post_ln_b)]
    flat2 = [tuple(a.reshape(-1, a.shape[-1]) for a in p) for p in small_params]
    small_out = [tuple(r.reshape(p[0].shape) for r in res) for res, p in zip(_adamw_small(flat2), small_params)]

    loss_all = lax.psum(loss[0, 0], ("x", "y", "c"))
    order = [("big", 0), ("small", 0), ("small", 1), ("small", 2), ("small", 3), ("big", 1), ("big", 2), ("big", 3),
             ("small", 4), ("small", 5), ("big", 4), ("small", 6), ("small", 7)]
    grads, deltas, new_m, new_v = [], [], [], []
    for kind, i in order:
        if kind == "big":
            g, d, m, v = big_out[i]
        else:
            g = small_params[i][1]
            d, m, v = small_out[i]
        grads.append(g)
        deltas.append(d)
        new_m.append(m)
        new_v.append(v)
    return (loss_all, dx.reshape(x.shape), *grads, *deltas, *new_m, *new_v)
```

```python
import functools

import jax
import jax.numpy as jnp
import numpy as np
from jax import lax
from jax.experimental import pallas as pl
from jax.experimental.pallas import tpu as pltpu

F32 = jnp.float32
BF16 = jnp.bfloat16

D_MODEL = 1024
A_WIDTH = 2048
A_GROUPS = 8
A_GROUP_DIM = A_WIDTH // A_GROUPS
CHUNK = 128
HEAD_DIM = 64
N_Q_HEADS = 16
N_KV_HEADS = 2
PAIRS_PER_KV = N_Q_HEADS // N_KV_HEADS // 2
N_PAIRS = N_Q_HEADS // 2
B_WIDTH = N_Q_HEADS * HEAD_DIM
KV_WIDTH = N_KV_HEADS * HEAD_DIM
REL_BUCKETS = 32
REL_MAX_DIST = 128
DEPTH = 2
ALPHA = (2.0 * DEPTH) ** 0.25
LN_EPS = 1e-5
NEG_INF = -1e30
SCALE = HEAD_DIM ** -0.5

ADAM_LR = 0.001
ADAM_B1 = 0.9
ADAM_B2 = 0.999
ADAM_EPS = 1e-08
ADAM_WD = 0.01
ADAM_STEP = 10

N_DEV = 8
MESH = pl.DeviceIdType.MESH
TOKENS_IN_PROJ = 1024
TOKENS_WGRAD = 2048
TOKENS_A = 256
TOKENS_B = 512
TOKENS_B_BWD = 256
TOKENS_DX = 512
VMEM_LIMIT_BYTES = 60 * 1024 * 1024
SMALL_ROWS = 144

ANY = pl.BlockSpec(memory_space=pl.ANY)
VMEM = pl.BlockSpec(memory_space=pltpu.VMEM)
SMEM = pl.BlockSpec(memory_space=pltpu.SMEM)


def _nn(a, b):
    return jnp.dot(a, b, preferred_element_type=F32)


def _nt(a, b):
    return lax.dot_general(a, b, (((1,), (1,)), ((), ())), preferred_element_type=F32)


def _tn(a, b):
    return lax.dot_general(a, b, (((0,), (0,)), ((), ())), preferred_element_type=F32)


def _resident(shape):
    nd = len(shape)
    return pl.BlockSpec(shape, lambda i: (0,) * nd, pipeline_mode=pl.Buffered(1))


def _rows(tm, width, reverse_of=None):
    if reverse_of is None:
        return pl.BlockSpec((tm, width), lambda i: (i, 0))
    return pl.BlockSpec((tm, width), lambda i: (reverse_of - 1 - i, 0))


def _seq_params():
    return pltpu.CompilerParams(dimension_semantics=("arbitrary",), vmem_limit_bytes=VMEM_LIMIT_BYTES)


def _ln_normalize(r):
    mu = jnp.mean(r, axis=-1, keepdims=True)
    xc = r - mu
    var = jnp.mean(xc * xc, axis=-1, keepdims=True)
    rstd = lax.rsqrt(var + LN_EPS)
    return xc * rstd, rstd


def _ln_backward(dxhat, xhat, rstd):
    m1 = jnp.mean(dxhat, axis=-1, keepdims=True)
    m2 = jnp.mean(dxhat * xhat, axis=-1, keepdims=True)
    return rstd * (dxhat - m1 - xhat * m2)


def _silu_and_grad(z):
    sig = jax.nn.sigmoid(z)
    return z * sig, sig * (1.0 + z * (1.0 - sig))


def _band_tables():
    t = np.arange(CHUNK, dtype=np.int32)[:, None]
    j = np.arange(2 * CHUNK, dtype=np.int32)[None, :]
    d = t + CHUNK - j
    in_window = (d >= 0) & (d < CHUNK)
    dc = np.clip(d, 0, REL_MAX_DIST - 1)
    max_exact = REL_BUCKETS // 2
    df = np.maximum(dc, 1).astype(np.float32)
    large = max_exact + (np.log(df / np.float32(max_exact)) / np.float32(np.log(REL_MAX_DIST / max_exact))
                         * np.float32(REL_BUCKETS - max_exact)).astype(np.int32)
    large = np.minimum(large, REL_BUCKETS - 1)
    bucket = np.where(dc < max_exact, dc, large)
    return np.where(in_window, bucket, -1).astype(np.int32)


def _use_order():
    x, y, c = lax.axis_index("x"), lax.axis_index("y"), lax.axis_index("c")
    blocks = [(x, y, c), (x, y, 1 - c)]
    for px, py in [(1 - x, y), (x, 1 - y), (1 - x, 1 - y)]:
        blocks += [(px, py, c), (px, py, 1 - c)]
    return blocks


ARRIVAL_OF_PHASE = (None, 0, 1, 4, 2, 5, 3, 6)


def _gather_in_proj(order, x, w_in_a, small, ln_gb):
    seq = x.shape[0]
    tm = min(TOKENS_IN_PROJ, seq)
    n_blocks = seq // tm
    shard_cols = w_in_a.shape[1]
    n = len(small) + 2

    def body(order_ref, x_ref, *refs):
        ins, refs = refs[:n], refs[n:]
        p_ref, outs, refs = refs[0], refs[1:n + 1], refs[n + 1:]
        wa_vmem, xb_vmem, stage = refs[0], refs[1], refs[2:n]
        send_sems, recv_sems, local_sems, out_sems = refs[n:]
        phase, blk = pl.program_id(0), pl.program_id(1)
        x_, y_, c_ = lax.axis_index("x"), lax.axis_index("y"), lax.axis_index("c")
        me, sibling = (x_, y_, c_), (x_, y_, 1 - c_)
        chips = [(1 - x_, y_), (x_, 1 - y_), (1 - x_, 1 - y_)]
        use = _use_order()

        def index(px, py, pc):
            return 4 * px + 2 * py + pc

        def slot(a, block):
            return (wa_vmem if a == 0 else outs[a]).at[index(*block)]

        def own(a):
            return slot(0, me) if a == 0 else (ins[a] if a == n - 1 else stage[a - 1])

        def copy(a, k, block, to, src=None):
            return pltpu.make_async_remote_copy(
                src_ref=slot(a, block) if src is None else src, dst_ref=slot(a, block),
                send_sem=send_sems.at[a, k], recv_sem=recv_sems.at[a, k], device_id=to, device_id_type=MESH)

        def keep(a):
            return pltpu.make_async_copy(own(a), slot(a, me), local_sems.at[a])

        def write_out(k):
            return pltpu.make_async_copy(slot(0, use[k]), outs[0].at[index(*use[k])], out_sems.at[k])

        def pass_on(a, j):
            copy(a, 1 + j, (*chips[j], c_), me).wait_recv()
            copy(a, 4 + j, (*chips[j], c_), sibling).start()

        def send(a, j):
            copy(a, 1 + j, me, (*chips[j], c_), own(a)).start()

        @pl.when((phase == 0) & (blk == 0))
        def _():
            wa_vmem[index(*me)] = ins[0][...].astype(BF16)
            for a in range(1, n - 1):
                stage[a - 1][...] = ins[a][...].astype(BF16)
            send(0, 0)
            send(0, 1)
            for a in range(n):
                copy(a, 0, me, sibling, own(a)).start()
                if a:
                    keep(a).start()
            write_out(0).start()

        for k in range(1, N_DEV):
            @pl.when((phase == k) & (blk == 0))
            def _(k=k):
                if k % 2 == 0:
                    pass_on(0, k // 2 - 1)
                else:
                    copy(0, ARRIVAL_OF_PHASE[k], use[k], me).wait_recv()
                write_out(k).start()
                if k == 2:
                    send(0, 2)
                    for a in range(1, n):
                        for j in range(3):
                            send(a, j)
                if k == N_DEV - 2:
                    for a in range(1, n):
                        pass_on(a, 0)
                        pass_on(a, 1)

        rows = pl.ds(pl.multiple_of(blk * tm, tm), tm)

        @pl.when(phase == 0)
        def _():
            xb_vmem[rows, :] = x_ref[...].astype(BF16)

        p_ref[...] = _nn(xb_vmem[rows, :], wa_vmem[order_ref[phase]])

        @pl.when((phase == N_DEV - 1) & (blk == n_blocks - 1))
        def _():
            for a in range(1, n):
                pass_on(a, 2)
            for a in range(1, n):
                copy(a, 0, sibling, me).wait_recv()
                for j, chip in enumerate(chips):
                    copy(a, 4 + j, (*chip, 1 - c_), me).wait_recv()
            for a in range(n):
                for k in range(7):
                    copy(a, k, me, me).wait_send()
                if a:
                    keep(a).wait()
            for k in range(N_DEV):
                write_out(k).wait()

    gathered = [jax.ShapeDtypeStruct((N_DEV,) + w_in_a.shape, BF16)]
    gathered += [jax.ShapeDtypeStruct((N_DEV,) + s.shape, BF16) for s in small]
    gathered.append(jax.ShapeDtypeStruct((N_DEV,) + ln_gb.shape, F32))
    return pl.pallas_call(
        body, name="gather_in_proj",
        out_shape=[jax.ShapeDtypeStruct((seq, N_DEV * shard_cols), F32)] + gathered,
        grid_spec=pltpu.PrefetchScalarGridSpec(
            num_scalar_prefetch=1, grid=(N_DEV, n_blocks),
            in_specs=[pl.BlockSpec((tm, D_MODEL), lambda k, i, order: (jnp.where(k == 0, i, n_blocks - 1), 0))]
            + [VMEM] * n,
            out_specs=[pl.BlockSpec((tm, shard_cols), lambda k, i, order: (i, order[k]))] + [ANY] * n,
            scratch_shapes=[pltpu.VMEM((N_DEV,) + w_in_a.shape, BF16), pltpu.VMEM((seq, D_MODEL), BF16)]
            + [pltpu.VMEM(s.shape, BF16) for s in small] + [
                pltpu.SemaphoreType.DMA((n, 7)), pltpu.SemaphoreType.DMA((n, 7)), pltpu.SemaphoreType.DMA((n,)),
                pltpu.SemaphoreType.DMA((N_DEV,))]),
        compiler_params=pltpu.CompilerParams(dimension_semantics=("arbitrary", "arbitrary"),
                                             vmem_limit_bytes=VMEM_LIMIT_BYTES),
    )(order, x, w_in_a, *small, ln_gb)


def _bias_table(rel_bias, bucket):
    def body(rel_ref, bucket_ref, out_ref):
        bk = bucket_ref[...]
        col = lax.broadcasted_iota(jnp.int32, bk.shape, 1)
        for h in range(N_Q_HEADS):
            def step(b, acc):
                return jnp.where(bk == b, rel_ref[b, h], acc)
            tab = lax.fori_loop(0, REL_BUCKETS, step, jnp.full(bk.shape, NEG_INF, F32))
            pair = h // 2
            rows = slice((pair % PAIRS_PER_KV) * CHUNK, (pair % PAIRS_PER_KV + 1) * CHUNK)
            half = slice((h % 2) * 2 * CHUNK, (h % 2 + 1) * 2 * CHUNK)
            out_ref[1, pair // PAIRS_PER_KV, rows, half] = tab
            out_ref[0, pair // PAIRS_PER_KV, rows, half] = jnp.where(col < CHUNK, NEG_INF, tab)

    return pl.pallas_call(
        body, name="bias_table",
        out_shape=jax.ShapeDtypeStruct((2, N_KV_HEADS, PAIRS_PER_KV * CHUNK, 4 * CHUNK), F32),
        in_specs=[SMEM, VMEM], out_specs=VMEM,
    )(rel_bias, bucket)


def _rel_bias_grad(dlogits, bucket):
    def body(dl_ref, bucket_ref, out_ref):
        bk = bucket_ref[...]
        row = lax.broadcasted_iota(jnp.int32, (REL_BUCKETS, CHUNK), 0)
        lane = lax.broadcasted_iota(jnp.int32, (REL_BUCKETS, CHUNK), 1)

        def step(b, acc):
            for h in range(N_Q_HEADS):
                pair = h // 2
                dl = dl_ref[pair // PAIRS_PER_KV, (pair % PAIRS_PER_KV) * CHUNK:(pair % PAIRS_PER_KV + 1) * CHUNK,
                            (h % 2) * 2 * CHUNK:(h % 2 + 1) * 2 * CHUNK]
                s = jnp.sum(jnp.where(bk == b, dl, 0.0))
                acc = jnp.where((row == b) & (lane == h), s, acc)
            return acc

        out_ref[...] = lax.fori_loop(0, REL_BUCKETS, step, jnp.zeros((REL_BUCKETS, CHUNK), F32))

    return pl.pallas_call(
        body, name="rel_bias_grad", out_shape=jax.ShapeDtypeStruct((REL_BUCKETS, CHUNK), F32),
        in_specs=[VMEM, VMEM], out_specs=VMEM,
    )(dlogits, bucket)


def _masked_spatial(ws_ref, g):
    t = lax.broadcasted_iota(jnp.int32, (CHUNK, CHUNK), 0)
    s = lax.broadcasted_iota(jnp.int32, (CHUNK, CHUNK), 1)
    return jnp.where(s <= t, ws_ref[g], 0.0).astype(BF16)


def _spatial_gate(ws_ref, bsp_t_ref, vn, s_ref, n_chunks):
    for g in range(A_GROUPS):
        w = _masked_spatial(ws_ref, g)
        cols = slice(g * A_GROUP_DIM, (g + 1) * A_GROUP_DIM)
        for ci in range(n_chunks):
            rows = slice(ci * CHUNK, (ci + 1) * CHUNK)
            s_ref[rows, cols] = _nn(w, vn[rows, cols]) + bsp_t_ref[:, g:g + 1]


def _fwd_a(x, p, ln_gb, w_spatial, bsp_t, woa, later):
    seq = x.shape[0]
    tm = min(TOKENS_A, seq)
    n_chunks = tm // CHUNK
    n_blocks = seq // tm
    n = len(later)

    def body(x_ref, p_ref, ln_ref, ws_ref, bsp_ref, woa_ref, *refs):
        shards, refs = refs[:n], refs[n:]
        r0_ref, xb_ref, gathered, refs = refs[0], refs[1], refs[2:2 + n], refs[2 + n:]
        s_ref, stage, (send_sems, recv_sems, local_sems) = refs[0], refs[1:1 + n], refs[1 + n:]
        step = pl.program_id(0)

        def copies(a):
            x_, y_, c_ = lax.axis_index("x"), lax.axis_index("y"), lax.axis_index("c")
            mine = 4 * x_ + 2 * y_ + c_
            out = [pltpu.make_async_copy(stage[a], gathered[a].at[mine], local_sems.at[a])]
            for r in range(1, N_DEV):
                to = (1 - x_ if r & 4 else x_, 1 - y_ if r & 2 else y_, 1 - c_ if r & 1 else c_)
                out.append(pltpu.make_async_remote_copy(
                    src_ref=stage[a], dst_ref=gathered[a].at[mine], send_sem=send_sems.at[a, r - 1],
                    recv_sem=recv_sems.at[a, r - 1], device_id=to, device_id_type=MESH))
            return out

        @pl.when(step == 0)
        def _():
            for a in range(n):
                stage[a][...] = shards[a][...].astype(BF16)
                for cp in copies(a):
                    cp.start()

        xv = x_ref[...]
        xb_ref[...] = xv.astype(BF16)
        vhat, _ = _ln_normalize(p_ref[:, A_WIDTH:2 * A_WIDTH])
        vn = (vhat * ln_ref[0:1, :] + ln_ref[1:2, :]).astype(BF16)
        _spatial_gate(ws_ref, bsp_ref, vn, s_ref, n_chunks)
        z = p_ref[:, 2 * A_WIDTH:]
        y = p_ref[:, :A_WIDTH] * s_ref[...] * (z * jax.nn.sigmoid(z))
        r0_ref[...] = ALPHA * xv + _nn(y.astype(BF16), woa_ref[...])

        @pl.when(step == n_blocks - 1)
        def _():
            for a in range(n):
                for cp in copies(a):
                    cp.wait()

    return pl.pallas_call(
        body, name="fwd_a", grid=(n_blocks,),
        out_shape=[jax.ShapeDtypeStruct((seq, D_MODEL), F32), jax.ShapeDtypeStruct((seq, D_MODEL), BF16)]
        + [jax.ShapeDtypeStruct((N_DEV,) + s.shape, BF16) for s in later],
        in_specs=[_rows(tm, D_MODEL), _rows(tm, 3 * A_WIDTH), _resident(ln_gb.shape), _resident(w_spatial.shape),
                  _resident(bsp_t.shape), _resident(woa.shape)] + [VMEM] * n,
        out_specs=[_rows(tm, D_MODEL), _rows(tm, D_MODEL)] + [ANY] * n,
        scratch_shapes=[pltpu.VMEM((tm, A_WIDTH), F32)] + [pltpu.VMEM(s.shape, BF16) for s in later] + [
            pltpu.SemaphoreType.DMA((n, N_DEV - 1)), pltpu.SemaphoreType.DMA((n, N_DEV - 1)),
            pltpu.SemaphoreType.DMA((n,))],
        compiler_params=_seq_params(),
    )(x, p, ln_gb, w_spatial, bsp_t, woa, *later)


def _expand_heads(kv):
    lo = lax.broadcasted_iota(jnp.int32, kv.shape, 1) < HEAD_DIM
    swapped = pltpu.roll(kv, HEAD_DIM, 1)
    zero = jnp.zeros_like(kv)
    return (jnp.where(lo, kv, zero).astype(BF16), jnp.where(lo, zero, swapped).astype(BF16),
            jnp.where(lo, swapped, zero).astype(BF16), jnp.where(lo, zero, kv).astype(BF16))


def _fold_heads(acc, kvh):
    top, bot = acc[:2 * CHUNK], acc[2 * CHUNK:]
    lo = lax.broadcasted_iota(jnp.int32, top.shape, 1) < HEAD_DIM
    t = jnp.where(lo, top, 0.0)
    b = jnp.where(lo, 0.0, bot)
    if kvh == 0:
        return t + pltpu.roll(b, HEAD_DIM, 1)
    return pltpu.roll(t, HEAD_DIM, 1) + b


STACK = PAIRS_PER_KV * CHUNK


def _pair_cols(kvh, pp):
    pair = kvh * PAIRS_PER_KV + pp
    return slice(pair * CHUNK, (pair + 1) * CHUNK)


def _stack_pairs(ref_or_val, rows, kvh):
    return jnp.concatenate([ref_or_val[rows, _pair_cols(kvh, pp)] for pp in range(PAIRS_PER_KV)], axis=0)


def _sink_columns(sink_ref):
    blk = lax.broadcasted_iota(jnp.int32, (STACK, 1), 0) // CHUNK
    out = []
    for kvh in range(N_KV_HEADS):
        halves = []
        for half in range(2):
            col = jnp.zeros((STACK, 1), F32)
            for pp in range(PAIRS_PER_KV):
                col = jnp.where(blk == pp, sink_ref[0, 2 * (kvh * PAIRS_PER_KV + pp) + half], col)
            halves.append(col)
        out.append(halves)
    return out


def _softmax_stack(logits, sink_cols):
    ps, sinks = [], []
    for half in range(2):
        lg = logits[:, half * 2 * CHUNK:(half + 1) * 2 * CHUNK]
        sk = sink_cols[half]
        m = jnp.maximum(jnp.max(lg, axis=-1, keepdims=True), sk)
        e = jnp.exp(lg - m)
        es = jnp.exp(sk - m)
        inv = 1.0 / (jnp.sum(e, axis=-1, keepdims=True) + es)
        ps.append(e * inv)
        sinks.append(es * inv)
    return jnp.concatenate(ps, axis=1), sinks


def _band_operands(ek_ref, ev_ref, ci, kvh):
    kblk = jnp.concatenate([ek_ref[ci, 2 * kvh], ek_ref[ci + 1, 2 * kvh],
                            ek_ref[ci, 2 * kvh + 1], ek_ref[ci + 1, 2 * kvh + 1]], axis=0)
    vblk = jnp.concatenate([ev_ref[ci, 2 * kvh], ev_ref[ci + 1, 2 * kvh],
                            ev_ref[ci, 2 * kvh + 1], ev_ref[ci + 1, 2 * kvh + 1]], axis=0)
    return kblk, vblk


def _fwd_b(r0, target, wb, wkv, wob, post_g, post_b, sinks, bias):
    seq = r0.shape[0]
    tm = min(TOKENS_B, seq)
    n_chunks = tm // CHUNK
    shard_cols = wb.shape[2]

    def body(r0_ref, tgt_ref, wb_ref, wkv_ref, wob_ref, pg_ref, pb_ref, sink_ref, bias_ref,
             q_ref, z_ref, kv_ref, dr1_ref, loss_ref, dg_ref, db_ref, ek_ref, ev_ref, o_ref):
        step = pl.program_id(0)

        @pl.when(step == 0)
        def _():
            ek_ref[0] = jnp.zeros(ek_ref.shape[1:], BF16)
            ev_ref[0] = jnp.zeros(ev_ref.shape[1:], BF16)
            loss_ref[...] = jnp.zeros_like(loss_ref)
            dg_ref[...] = jnp.zeros_like(dg_ref)
            db_ref[...] = jnp.zeros_like(db_ref)

        @pl.when(step > 0)
        def _():
            ek_ref[0] = ek_ref[n_chunks]
            ev_ref[0] = ev_ref[n_chunks]

        xhat0, _ = _ln_normalize(r0_ref[...])
        h1 = xhat0 * pg_ref[0:1, :] + pb_ref[0:1, :]
        h1b = h1.astype(BF16)
        for k in range(N_DEV):
            qz = _nn(h1b, wb_ref[k])
            lo, hi = k * shard_cols, (k + 1) * shard_cols
            if hi <= B_WIDTH:
                q_ref[:, lo:hi] = (qz * SCALE).astype(BF16)
            else:
                z_ref[:, lo - B_WIDTH:hi - B_WIDTH] = qz
        kv = _nn(h1b, wkv_ref[...])
        kv_ref[...] = kv.astype(BF16)
        for ci in range(n_chunks):
            rows = slice(ci * CHUNK, (ci + 1) * CHUNK)
            for i, e in enumerate(_expand_heads(kv[rows, :KV_WIDTH])):
                ek_ref[ci + 1, i] = e
            for i, e in enumerate(_expand_heads(kv[rows, KV_WIDTH:])):
                ev_ref[ci + 1, i] = e

        sink_cols = _sink_columns(sink_ref)
        for ci in range(n_chunks):
            rows = slice(ci * CHUNK, (ci + 1) * CHUNK)
            sel = jnp.where(step * n_chunks + ci > 0, 1, 0)
            for kvh in range(N_KV_HEADS):
                kblk, vblk = _band_operands(ek_ref, ev_ref, ci, kvh)
                logits = _nt(_stack_pairs(q_ref, rows, kvh), kblk) + bias_ref[sel, kvh]
                p, _ = _softmax_stack(logits, sink_cols[kvh])
                o = _nn(p.astype(BF16), vblk)
                for pp in range(PAIRS_PER_KV):
                    o_ref[rows, _pair_cols(kvh, pp)] = o[pp * CHUNK:(pp + 1) * CHUNK]

        z = z_ref[...]
        y = o_ref[...] * (z * jax.nn.sigmoid(z))
        r1 = ALPHA * h1 + _nn(y.astype(BF16), wob_ref[...])
        xhat1, rstd1 = _ln_normalize(r1)
        g1 = pg_ref[1:2, :]
        diff = xhat1 * g1 + pb_ref[1:2, :] - tgt_ref[...]
        loss_ref[...] += 0.5 * jnp.sum(jnp.mean(diff * diff, axis=-1, keepdims=True), axis=0, keepdims=True)
        dh2 = diff * (1.0 / D_MODEL)
        dg_ref[...] += jnp.sum(dh2 * xhat1, axis=0, keepdims=True)
        db_ref[...] += jnp.sum(dh2, axis=0, keepdims=True)
        dr1_ref[...] = _ln_backward(dh2 * g1, xhat1, rstd1)

    vec = pl.BlockSpec((1, D_MODEL), lambda i: (0, 0))
    return pl.pallas_call(
        body, name="fwd_b", grid=(seq // tm,),
        out_shape=[jax.ShapeDtypeStruct((seq, B_WIDTH), BF16), jax.ShapeDtypeStruct((seq, B_WIDTH), F32),
                   jax.ShapeDtypeStruct((seq, 2 * KV_WIDTH), BF16), jax.ShapeDtypeStruct((seq, D_MODEL), F32),
                   jax.ShapeDtypeStruct((1, 1), F32), jax.ShapeDtypeStruct((1, D_MODEL), F32),
                   jax.ShapeDtypeStruct((1, D_MODEL), F32)],
        in_specs=[_rows(tm, D_MODEL), _rows(tm, D_MODEL), _resident(wb.shape), _resident(wkv.shape),
                  _resident(wob.shape), _resident(post_g.shape), _resident(post_b.shape), SMEM,
                  _resident(bias.shape)],
        out_specs=[_rows(tm, B_WIDTH), _rows(tm, B_WIDTH), _rows(tm, 2 * KV_WIDTH), _rows(tm, D_MODEL),
                   pl.BlockSpec((1, 1), lambda i: (0, 0)), vec, vec],
        scratch_shapes=[pltpu.VMEM((n_chunks + 1, 4, CHUNK, CHUNK), BF16),
                        pltpu.VMEM((n_chunks + 1, 4, CHUNK, CHUNK), BF16),
                        pltpu.VMEM((tm, B_WIDTH), F32)],
        compiler_params=_seq_params(),
    )(r0, target, wb, wkv, wob, post_g, post_b, sinks, bias)


def _bwd_b(r0, dr1, q, z, kv, wb, wkv, wob, post_g, post_b, sinks, bias):
    seq = r0.shape[0]
    tm = min(TOKENS_B_BWD, seq)
    n_chunks = tm // CHUNK
    n_blocks = seq // tm
    shard_cols = wb.shape[2]

    def body(r0_ref, dr1_ref, q_ref, z_ref, kv_ref, kvp_ref, wb_ref, wkv_ref, wob_ref, pg_ref, pb_ref, sink_ref,
             bias_ref, dh1_ref, dwb_out, dwkv_out, dwob_out, dsink_ref, dl_out,
             dwb_acc, dwkv_acc, dwob_acc, dl_acc, dsink_acc, carry_ref, ek_ref, ev_ref, y_ref, dqz_ref, dkv_ref):
        step = pl.program_id(0)
        block = n_blocks - 1 - step

        @pl.when(step == 0)
        def _():
            dwb_acc[...] = jnp.zeros_like(dwb_acc)
            dwkv_acc[...] = jnp.zeros_like(dwkv_acc)
            dwob_acc[...] = jnp.zeros_like(dwob_acc)
            dl_acc[...] = jnp.zeros_like(dl_acc)
            carry_ref[...] = jnp.zeros_like(carry_ref)
            dsink_acc[...] = jnp.zeros_like(dsink_acc)

        xhat0, _ = _ln_normalize(r0_ref[...])
        h1b = (xhat0 * pg_ref[0:1, :] + pb_ref[0:1, :]).astype(BF16)
        dr1 = dr1_ref[...]
        dr1b = dr1.astype(BF16)
        dy = _nt(dr1b, wob_ref[...])
        sg, dsg = _silu_and_grad(z_ref[...])
        do_all = dy * sg
        dyz = dy * dsg

        for ci in range(n_chunks + 1):
            src = kvp_ref[...] if ci == 0 else kv_ref[(ci - 1) * CHUNK:ci * CHUNK, :]
            src = src.astype(F32)
            for i, e in enumerate(_expand_heads(src[:, :KV_WIDTH])):
                ek_ref[ci, i] = e
            for i, e in enumerate(_expand_heads(src[:, KV_WIDTH:])):
                ev_ref[ci, i] = e

        sink_cols = _sink_columns(sink_ref)
        for ci in reversed(range(n_chunks)):
            rows = slice(ci * CHUNK, (ci + 1) * CHUNK)
            sel = jnp.where(block * n_chunks + ci > 0, 1, 0)
            dk_band = jnp.zeros((2 * CHUNK, KV_WIDTH), F32)
            dv_band = jnp.zeros((2 * CHUNK, KV_WIDTH), F32)
            for kvh in range(N_KV_HEADS):
                kblk, vblk = _band_operands(ek_ref, ev_ref, ci, kvh)
                qs = _stack_pairs(q_ref, rows, kvh)
                p, psink = _softmax_stack(_nt(qs, kblk) + bias_ref[sel, kvh], sink_cols[kvh])
                pb16 = p.astype(BF16)
                o = _nn(pb16, vblk)
                y = (o * _stack_pairs(sg, rows, kvh)).astype(BF16)
                dz = (o * _stack_pairs(dyz, rows, kvh)).astype(BF16)
                dob = _stack_pairs(do_all, rows, kvh).astype(BF16)
                dp = _nt(dob, vblk)
                dlogits = []
                for half in range(2):
                    hs = slice(half * 2 * CHUNK, (half + 1) * 2 * CHUNK)
                    delta = jnp.sum(dp[:, hs] * p[:, hs], axis=-1, keepdims=True)
                    dlogits.append(p[:, hs] * (dp[:, hs] - delta))
                    dsink_acc[2 * kvh + half] += psink[half] * delta
                dlog = jnp.concatenate(dlogits, axis=1)
                dl_acc[kvh] += dlog
                dsc = dlog.astype(BF16)
                dq = (_nn(dsc, kblk) * SCALE).astype(BF16)
                for pp in range(PAIRS_PER_KV):
                    blk = slice(pp * CHUNK, (pp + 1) * CHUNK)
                    cols = _pair_cols(kvh, pp)
                    y_ref[rows, cols] = y[blk]
                    dqz_ref[rows, cols] = dq[blk]
                    dqz_ref[rows, B_WIDTH + cols.start:B_WIDTH + cols.stop] = dz[blk]
                dk_band = dk_band + _fold_heads(_tn(dsc, qs), kvh)
                dv_band = dv_band + _fold_heads(_tn(pb16, dob), kvh)
            band = jnp.concatenate([dk_band, dv_band], axis=1)
            dkv_ref[rows, :] = (band[CHUNK:] + carry_ref[...]).astype(BF16)
            carry_ref[...] = band[:CHUNK]

        dqz = dqz_ref[...]
        dkv = dkv_ref[...]
        dh1 = ALPHA * dr1 + _nt(dkv, wkv_ref[...])
        for k in range(N_DEV):
            part = dqz[:, k * shard_cols:(k + 1) * shard_cols]
            dh1 = dh1 + _nt(part, wb_ref[k])
            dwb_acc[k] += _tn(h1b, part)
        dh1_ref[...] = dh1
        dwkv_acc[...] += _tn(h1b, dkv)
        dwob_acc[...] += _tn(y_ref[...], dr1b)

        @pl.when(step == n_blocks - 1)
        def _():
            pltpu.sync_copy(dwb_acc, dwb_out)
            pltpu.sync_copy(dwkv_acc, dwkv_out)
            pltpu.sync_copy(dwob_acc, dwob_out)
            pltpu.sync_copy(dl_acc, dl_out)
            lane = lax.broadcasted_iota(jnp.int32, (1, CHUNK), 1)
            dsink = jnp.zeros((1, CHUNK), F32)
            for kvh in range(N_KV_HEADS):
                for half in range(2):
                    for pp in range(PAIRS_PER_KV):
                        head = 2 * (kvh * PAIRS_PER_KV + pp) + half
                        tot = jnp.sum(dsink_acc[2 * kvh + half, pp * CHUNK:(pp + 1) * CHUNK, :], axis=0, keepdims=True)
                        dsink = dsink - jnp.where(lane == head, tot, 0.0)
            dsink_ref[...] = dsink

    rev = functools.partial(_rows, tm, reverse_of=n_blocks)
    prev_chunk = pl.BlockSpec(
        (CHUNK, 2 * KV_WIDTH), lambda i: (jnp.maximum((n_blocks - 1 - i) * n_chunks - 1, 0), 0))
    return pl.pallas_call(
        body, name="bwd_b", grid=(n_blocks,),
        out_shape=[jax.ShapeDtypeStruct((seq, D_MODEL), F32), jax.ShapeDtypeStruct(wb.shape, F32),
                   jax.ShapeDtypeStruct(wkv.shape, F32), jax.ShapeDtypeStruct(wob.shape, F32),
                   jax.ShapeDtypeStruct((1, CHUNK), F32),
                   jax.ShapeDtypeStruct((N_KV_HEADS, STACK, 4 * CHUNK), F32)],
        in_specs=[rev(D_MODEL), rev(D_MODEL), rev(B_WIDTH), rev(B_WIDTH), rev(2 * KV_WIDTH), prev_chunk,
                  _resident(wb.shape), _resident(wkv.shape), _resident(wob.shape), _resident(post_g.shape),
                  _resident(post_b.shape), SMEM, _resident(bias.shape)],
        out_specs=[rev(D_MODEL), ANY, ANY, ANY, pl.BlockSpec((1, CHUNK), lambda i: (0, 0)), ANY],
        scratch_shapes=[pltpu.VMEM(wb.shape, F32), pltpu.VMEM(wkv.shape, F32), pltpu.VMEM(wob.shape, F32),
                        pltpu.VMEM((N_KV_HEADS, STACK, 4 * CHUNK), F32), pltpu.VMEM((2 * N_KV_HEADS, STACK, 1), F32),
                        pltpu.VMEM((CHUNK, 2 * KV_WIDTH), F32),
                        pltpu.VMEM((n_chunks + 1, 4, CHUNK, CHUNK), BF16),
                        pltpu.VMEM((n_chunks + 1, 4, CHUNK, CHUNK), BF16),
                        pltpu.VMEM((tm, B_WIDTH), BF16), pltpu.VMEM((tm, 2 * B_WIDTH), BF16),
                        pltpu.VMEM((tm, 2 * KV_WIDTH), BF16)],
        compiler_params=_seq_params(),
    )(r0, dr1, q, z, kv, kv, wb, wkv, wob, post_g, post_b, sinks, bias)


def _bwd_a(dh1, r0, p, ln_gb, w_spatial, bsp_t, woa, post_g, exchange):
    seq = r0.shape[0]
    tm = min(TOKENS_A, seq)
    n_chunks = tm // CHUNK
    n_blocks = seq // tm

    def body(dh1_ref, r0_ref, p_ref, ln_ref, ws_ref, bsp_ref, woa_ref, pg_ref, *refs):
        ex_in, refs = refs[:exchange.n], refs[exchange.n:]
        (dp_ref, dr0_ref, dwoa_out, dws_ref, dbsp_ref, dln_ref, dpost_ref), refs = refs[:7], refs[7:]
        ex_out, refs = refs[:exchange.n], refs[exchange.n:]
        (dwoa_acc, s_ref, ds_ref, dvn_ref), ex_sems = refs[:4], refs[4:]
        step = pl.program_id(0)

        @pl.when(step == 0)
        def _():
            exchange.start(ex_in, ex_out, ex_sems)
            dwoa_acc[...] = jnp.zeros_like(dwoa_acc)
            dws_ref[...] = jnp.zeros_like(dws_ref)
            dbsp_ref[...] = jnp.zeros_like(dbsp_ref)
            dln_ref[...] = jnp.zeros_like(dln_ref)
            dpost_ref[...] = jnp.zeros_like(dpost_ref)

        dh1 = dh1_ref[...]
        xhat0, rstd0 = _ln_normalize(r0_ref[...])
        dpost_ref[0:1, :] += jnp.sum(dh1 * xhat0, axis=0, keepdims=True)
        dpost_ref[1:2, :] += jnp.sum(dh1, axis=0, keepdims=True)
        dr0 = _ln_backward(dh1 * pg_ref[0:1, :], xhat0, rstd0)
        dr0_ref[...] = dr0
        dr0b = dr0.astype(BF16)

        vhat, rstd_v = _ln_normalize(p_ref[:, A_WIDTH:2 * A_WIDTH])
        ln_g = ln_ref[0:1, :]
        vn = (vhat * ln_g + ln_ref[1:2, :]).astype(BF16)
        _spatial_gate(ws_ref, bsp_ref, vn, s_ref, n_chunks)
        u = p_ref[:, :A_WIDTH]
        s = s_ref[...]
        sg, dsg = _silu_and_grad(p_ref[:, 2 * A_WIDTH:])
        dwoa_acc[...] += _tn((u * s * sg).astype(BF16), dr0b)
        dy = _nt(dr0b, woa_ref[...])
        dp_ref[:, :A_WIDTH] = (dy * s * sg).astype(BF16)
        dp_ref[:, 2 * A_WIDTH:] = (dy * u * s * dsg).astype(BF16)
        ds_ref[...] = dy * u * sg

        lane = lax.broadcasted_iota(jnp.int32, (CHUNK, CHUNK), 1)
        dbsp = jnp.zeros((CHUNK, CHUNK), F32)
        for g in range(A_GROUPS):
            w = _masked_spatial(ws_ref, g)
            cols = slice(g * A_GROUP_DIM, (g + 1) * A_GROUP_DIM)
            dws = jnp.zeros((CHUNK, CHUNK), F32)
            for ci in range(n_chunks):
                rows = slice(ci * CHUNK, (ci + 1) * CHUNK)
                ds = ds_ref[rows, cols]
                dsb = ds.astype(BF16)
                dvn_ref[rows, cols] = _tn(w, dsb)
                dws = dws + _nt(dsb, vn[rows, cols])
                dbsp = dbsp + jnp.where(lane == g, jnp.sum(ds, axis=-1, keepdims=True), 0.0)
            dws_ref[g] += dws
        dbsp_ref[...] += dbsp

        dvn = dvn_ref[...]
        dln_ref[0:1, :] += jnp.sum(dvn * vhat, axis=0, keepdims=True)
        dln_ref[1:2, :] += jnp.sum(dvn, axis=0, keepdims=True)
        dp_ref[:, A_WIDTH:2 * A_WIDTH] = _ln_backward(dvn * ln_g, vhat, rstd_v).astype(BF16)

        @pl.when(step == n_blocks - 1)
        def _():
            t = lax.broadcasted_iota(jnp.int32, (CHUNK, CHUNK), 0)
            for g in range(A_GROUPS):
                dws_ref[g] = jnp.where(lane <= t, dws_ref[g], 0.0)
            pltpu.sync_copy(dwoa_acc, dwoa_out)
            exchange.wait(ex_in, ex_out, ex_sems)

    def fixed(shape):
        nd = len(shape)
        return pl.BlockSpec(shape, lambda i: (0,) * nd)

    return pl.pallas_call(
        body, name="bwd_a", grid=(n_blocks,),
        out_shape=[jax.ShapeDtypeStruct((seq, 3 * A_WIDTH), BF16), jax.ShapeDtypeStruct((seq, D_MODEL), F32),
                   jax.ShapeDtypeStruct(woa.shape, F32), jax.ShapeDtypeStruct(w_spatial.shape, F32),
                   jax.ShapeDtypeStruct((CHUNK, CHUNK), F32), jax.ShapeDtypeStruct((2, A_WIDTH), F32),
                   jax.ShapeDtypeStruct((2, D_MODEL), F32)] + exchange.out_shape,
        in_specs=[_rows(tm, D_MODEL), _rows(tm, D_MODEL), _rows(tm, 3 * A_WIDTH), _resident(ln_gb.shape),
                  _resident(w_spatial.shape), _resident(bsp_t.shape), _resident(woa.shape),
                  _resident(post_g.shape)] + [ANY] * exchange.n,
        out_specs=[_rows(tm, 3 * A_WIDTH), _rows(tm, D_MODEL), ANY, fixed(w_spatial.shape),
                   fixed((CHUNK, CHUNK)), fixed((2, A_WIDTH)), fixed((2, D_MODEL))] + [ANY] * exchange.n,
        scratch_shapes=[pltpu.VMEM(woa.shape, F32), pltpu.VMEM((tm, A_WIDTH), F32),
                        pltpu.VMEM((tm, A_WIDTH), F32), pltpu.VMEM((tm, A_WIDTH), F32)] + exchange.scratch_shapes,
        compiler_params=_seq_params(),
    )(dh1, r0, p, ln_gb, w_spatial, bsp_t, woa, post_g, *exchange.inputs)


def _input_grad(dp, dr0, wa, part, n_parts, earlier, exchange):
    seq = dr0.shape[0]
    tm = min(TOKENS_DX, seq // n_parts)
    n_blocks = seq // n_parts // tm
    first = part * n_blocks
    shard_cols = wa.shape[2]
    n_own = 3 if earlier is None else 4

    def body(dp_ref, dr0_ref, wa_ref, *refs):
        refs = refs[n_own - 3:]
        ex_in, dx_ref, ex_out, ex_sems = (refs[:exchange.n], refs[exchange.n], refs[exchange.n + 1:2 * exchange.n + 1],
                                          refs[2 * exchange.n + 1:])
        step = pl.program_id(0)

        @pl.when(step == 0)
        def _():
            exchange.start(ex_in, ex_out, ex_sems)

        dx = ALPHA * dr0_ref[...]
        for k in range(N_DEV):
            dx = dx + _nt(dp_ref[:, k * shard_cols:(k + 1) * shard_cols], wa_ref[k])
        dx_ref[...] = dx

        @pl.when(step == n_blocks - 1)
        def _():
            exchange.wait(ex_in, ex_out, ex_sems)

    def rows(width):
        return pl.BlockSpec((tm, width), lambda i: (first + i, 0))

    return pl.pallas_call(
        body, name=f"input_grad_{part}", grid=(n_blocks,),
        out_shape=[jax.ShapeDtypeStruct((seq, D_MODEL), F32)] + exchange.out_shape,
        in_specs=[rows(3 * A_WIDTH), rows(D_MODEL), _resident(wa.shape)] + [ANY] * (n_own - 3 + exchange.n),
        out_specs=[rows(D_MODEL)] + [ANY] * exchange.n,
        scratch_shapes=exchange.scratch_shapes,
        input_output_aliases={} if earlier is None else {3: 0},
        compiler_params=_seq_params(),
    )(dp, dr0, wa, *(() if earlier is None else (earlier,)), *exchange.inputs)


def _in_proj_weight_grad(xb, dp, shard_cols, exchange):
    seq = xb.shape[0]
    tk = min(TOKENS_WGRAD, seq)
    n_k = seq // tk

    def body(x_ref, dp_ref, *refs):
        ex_in, out_ref, ex_out, ex_sems = (refs[:exchange.n], refs[exchange.n], refs[exchange.n + 1:2 * exchange.n + 1],
                                           refs[2 * exchange.n + 1:])
        shard, step = pl.program_id(0), pl.program_id(1)

        @pl.when((shard == 0) & (step == 0))
        def _():
            exchange.start(ex_in, ex_out, ex_sems)

        @pl.when(step == 0)
        def _():
            out_ref[...] = jnp.zeros_like(out_ref)

        out_ref[...] += _tn(x_ref[...], dp_ref[...])

        @pl.when((shard == N_DEV - 1) & (step == n_k - 1))
        def _():
            exchange.wait(ex_in, ex_out, ex_sems)

    return pl.pallas_call(
        body, name="in_proj_weight_grad", grid=(N_DEV, n_k),
        out_shape=[jax.ShapeDtypeStruct((N_DEV, D_MODEL, shard_cols), F32)] + exchange.out_shape,
        in_specs=[pl.BlockSpec((tk, D_MODEL), lambda j, k: (k, 0)),
                  pl.BlockSpec((tk, shard_cols), lambda j, k: (k, j))] + [ANY] * exchange.n,
        out_specs=[pl.BlockSpec((None, D_MODEL, shard_cols), lambda j, k: (j, 0, 0))] + [ANY] * exchange.n,
        scratch_shapes=exchange.scratch_shapes,
        compiler_params=pltpu.CompilerParams(dimension_semantics=("arbitrary", "arbitrary"),
                                             vmem_limit_bytes=VMEM_LIMIT_BYTES),
    )(xb, dp, *exchange.inputs)


class _Exchange:
    def __init__(self, pair=(), chip=()):
        self.pair, self.chip = tuple(pair), tuple(chip)
        self.inputs = self.pair + self.chip
        self.n = len(self.inputs)
        self.out_shape = [jax.ShapeDtypeStruct((4, 1) + g.shape[2:], F32) for g in self.pair]
        self.out_shape += [jax.ShapeDtypeStruct((3,) + s.shape[1:], s.dtype) for s in self.chip]
        self.scratch_shapes = [pltpu.SemaphoreType.DMA((max(len(self.pair), 1),)) for _ in range(2)]
        self.scratch_shapes += [pltpu.SemaphoreType.DMA((max(len(self.chip), 1), 3)) for _ in range(2)]

    def _copies(self, ins, outs, sems):
        if not self.n:
            return []
        pair_send, pair_recv, chip_send, chip_recv = sems
        x, y, c = lax.axis_index("x"), lax.axis_index("y"), lax.axis_index("c")
        copies = []
        for a in range(len(self.pair)):
            copies.append(pltpu.make_async_remote_copy(
                src_ref=ins[a].at[:, pl.ds(1 - c, 1)], dst_ref=outs[a],
                send_sem=pair_send.at[a], recv_sem=pair_recv.at[a], device_id=(x, y, 1 - c), device_id_type=MESH))
        for a in range(len(self.chip)):
            src, dst = ins[len(self.pair) + a], outs[len(self.pair) + a]
            for k, (px, py) in enumerate([(1 - x, y), (x, 1 - y), (1 - x, 1 - y)]):
                copies.append(pltpu.make_async_remote_copy(
                    src_ref=src.at[2 * px + py], dst_ref=dst.at[k], send_sem=chip_send.at[a, k],
                    recv_sem=chip_recv.at[a, k], device_id=(px, py, c), device_id_type=MESH))
        return copies

    def start(self, ins, outs, sems):
        for cp in self._copies(ins, outs, sems):
            cp.start()

    def wait(self, ins, outs, sems):
        for cp in self._copies(ins, outs, sems):
            cp.wait()


def _row_tile(rows, cols, itemsize=4):
    if rows * cols * itemsize <= (1 << 20):
        return rows
    t = 8
    while 2 * t * cols * itemsize <= (1 << 20) and rows % (2 * t) == 0:
        t *= 2
    assert rows % t == 0, (rows, cols)
    return t


def _pair_sum(place, grad, landed, dtype):
    _, _, rows, cols = grad.shape
    tr = _row_tile(rows, cols)

    def body(place_ref, a_ref, b_ref, out_ref):
        out_ref[...] = (a_ref[...] + b_ref[...]).astype(dtype)

    return pl.pallas_call(
        body, name="pair_sum", out_shape=jax.ShapeDtypeStruct((4, rows, cols), dtype),
        grid_spec=pltpu.PrefetchScalarGridSpec(
            num_scalar_prefetch=1, grid=(4, rows // tr),
            in_specs=[pl.BlockSpec((None, None, tr, cols), lambda k, i, pr: (k, pr[1], i, 0)),
                      pl.BlockSpec((None, None, tr, cols), lambda k, i, pr: (k, 0, i, 0))],
            out_specs=pl.BlockSpec((None, tr, cols), lambda k, i, pr: (k, i, 0))),
        compiler_params=pltpu.CompilerParams(dimension_semantics=("parallel", "parallel")),
    )(place, grad, landed)


def _adamw_math(w, g, m, v):
    m = ADAM_B1 * m + (1.0 - ADAM_B1) * g
    v = ADAM_B2 * v + (1.0 - ADAM_B2) * (g * g)
    m_hat = m / (1.0 - ADAM_B1 ** ADAM_STEP)
    v_hat = v / (1.0 - ADAM_B2 ** ADAM_STEP)
    delta = -ADAM_LR * (m_hat / (jnp.sqrt(v_hat) + ADAM_EPS) + ADAM_WD * w)
    return delta, m, v


def _adamw_shard(place, grad, landed, got, w, m, v):
    rows, cols = w.shape
    tr = _row_tile(rows, cols)

    def body(place_ref, a_ref, b_ref, got_ref, w_ref, m_ref, v_ref, g_out, d_out, m_out, v_out):
        g = a_ref[...] + b_ref[...]
        for k in range(3):
            g = g + got_ref[k].astype(F32)
        g_out[...] = g
        d_out[...], m_out[...], v_out[...] = _adamw_math(w_ref[...], g, m_ref[...], v_ref[...])

    spec = pl.BlockSpec((tr, cols), lambda i, pr: (i, 0))
    out = jax.ShapeDtypeStruct((rows, cols), F32)
    return pl.pallas_call(
        body, name="adamw_shard", out_shape=[out] * 4,
        grid_spec=pltpu.PrefetchScalarGridSpec(
            num_scalar_prefetch=1, grid=(rows // tr,),
            in_specs=[pl.BlockSpec((None, None, tr, cols), lambda i, pr: (pr[0], pr[1], i, 0)),
                      pl.BlockSpec((None, None, tr, cols), lambda i, pr: (pr[0], 0, i, 0)),
                      pl.BlockSpec((3, tr, cols), lambda i, pr: (0, i, 0)), spec, spec, spec],
            out_specs=[spec] * 4),
        compiler_params=pltpu.CompilerParams(dimension_semantics=("parallel",)),
    )(place, grad, landed, got, w, m, v)


def _gather_small(grad, landed, got):
    rows, cols = got.shape[1:]

    def body(grad_ref, landed_ref, got_ref, out_ref, send_sems, recv_sems):
        x, y, c = lax.axis_index("x"), lax.axis_index("y"), lax.axis_index("c")
        me, sibling = (x, y, c), (x, y, 1 - c)
        chips = [(1 - x, y), (x, 1 - y), (1 - x, 1 - y)]

        def slot(px, py, pc):
            return out_ref.at[4 * px + 2 * py + pc]

        def copy(k, block, to):
            return pltpu.make_async_remote_copy(
                src_ref=slot(*block), dst_ref=slot(*block), send_sem=send_sems.at[k], recv_sem=recv_sems.at[k],
                device_id=to, device_id_type=MESH)

        mine = grad_ref[2 * x + y, c] + landed_ref[2 * x + y, 0]
        out_ref[4 * x + 2 * y + c] = ((mine + got_ref[0]) + got_ref[1]) + got_ref[2]
        first = [copy(0, me, sibling)] + [copy(1 + j, me, (*chip, c)) for j, chip in enumerate(chips)]
        for cp in first:
            cp.start()
        passed = [copy(4 + j, (*chip, c), sibling) for j, chip in enumerate(chips)]
        for j, chip in enumerate(chips):
            copy(1 + j, (*chip, c), me).wait_recv()
            passed[j].start()
        copy(0, sibling, me).wait_recv()
        for j, chip in enumerate(chips):
            copy(4 + j, (*chip, 1 - c), me).wait_recv()
        for cp in first + passed:
            cp.wait_send()

    return pl.pallas_call(
        body, name="gather_small", out_shape=jax.ShapeDtypeStruct((N_DEV, rows, cols), F32),
        in_specs=[VMEM] * 3, out_specs=VMEM,
        scratch_shapes=[pltpu.SemaphoreType.DMA((7,)), pltpu.SemaphoreType.DMA((7,))],
    )(grad, landed, got)


def _adamw_small(params):
    n = len(params)

    def body(*refs):
        ins, outs = refs[:4 * n], refs[4 * n:]
        for i in range(n):
            w_ref, g_ref, m_ref, v_ref = ins[4 * i:4 * i + 4]
            d, m, v = _adamw_math(w_ref[...], g_ref[...], m_ref[...], v_ref[...])
            outs[3 * i][...] = d
            outs[3 * i + 1][...] = m
            outs[3 * i + 2][...] = v

    flat = [a for p in params for a in p]
    out_shape = [jax.ShapeDtypeStruct(p[0].shape, F32) for p in params for _ in range(3)]
    outs = pl.pallas_call(
        body, name="adamw_small", out_shape=out_shape, in_specs=[VMEM] * (4 * n), out_specs=[VMEM] * (3 * n),
    )(*flat)
    return [tuple(outs[3 * i:3 * i + 3]) for i in range(n)]


def _pack_small(parts):
    rows = []
    for a in parts:
        r, c = a.shape
        rows.append(jnp.pad(a, ((0, (-r) % 8), (0, CHUNK - c))))
    flat = jnp.concatenate(rows, axis=0)
    return jnp.pad(flat, ((0, N_DEV * SMALL_ROWS - flat.shape[0]), (0, 0))).reshape(N_DEV, SMALL_ROWS, CHUNK)


def _unpack_small(flat, shapes):
    out, at = [], 0
    for r, c in shapes:
        out.append(flat[at:at + r, :c])
        at += r + (-r) % 8
    return out


def kernel(x, w_in_a, sgu_ln_g, sgu_ln_b, w_spatial, b_spatial, w_out_a, w_kv, w_in_b, attn_sinks, rel_bias, w_out_b, post_ln_g, post_ln_b, loss_target, m_w_in_a, m_sgu_ln_g, m_sgu_ln_b, m_w_spatial, m_b_spatial, m_w_out_a, m_w_kv, m_w_in_b, m_attn_sinks, m_rel_bias, m_w_out_b, m_post_ln_g, m_post_ln_b, v_w_in_a, v_sgu_ln_g, v_sgu_ln_b, v_w_spatial, v_b_spatial, v_w_out_a, v_w_kv, v_w_in_b, v_attn_sinks, v_rel_bias, v_w_out_b, v_post_ln_g, v_post_ln_b):
    seq = x.shape[1]
    me = 4 * lax.axis_index("x") + 2 * lax.axis_index("y") + lax.axis_index("c")
    bucket = jnp.asarray(_band_tables())

    ln_shard = jnp.concatenate([sgu_ln_g, sgu_ln_b], axis=0)
    x2, tgt = x[0], loss_target[0]
    order = jnp.stack([4 * px + 2 * py + pc for px, py, pc in _use_order()]).astype(jnp.int32)
    p, wa, woa, ln_all = _gather_in_proj(order, x2, w_in_a[0], (w_out_a[0],), ln_shard)
    woa = woa.reshape(A_WIDTH, D_MODEL)
    ln_gb = jnp.transpose(ln_all, (1, 0, 2)).reshape(2, A_WIDTH)
    ws = w_spatial[0]
    bsp_t = jnp.pad(b_spatial[0].T, ((0, 0), (0, CHUNK - A_GROUPS)))
    bias = _bias_table(rel_bias, bucket)

    r0, xb, wkv, wb, wob = _fwd_a(x2, p, ln_gb, ws, bsp_t, woa, (w_kv, w_in_b[0], w_out_b[0]))
    wkv = wkv.reshape(D_MODEL, 2 * KV_WIDTH)
    wob = wob.reshape(B_WIDTH, D_MODEL)
    q, z, kv, dr1, loss, dg1, db1 = _fwd_b(r0, tgt, wb, wkv, wob, post_ln_g, post_ln_b, attn_sinks, bias)
    dh1, dwb, dwkv, dwob, dsink, dlog = _bwd_b(r0, dr1, q, z, kv, wb, wkv, wob, post_ln_g, post_ln_b, attn_sinks,
                                               bias)
    drel = _rel_bias_grad(dlog, bucket)

    place = jnp.stack([2 * lax.axis_index("x") + lax.axis_index("y"), lax.axis_index("c")]).astype(jnp.int32)

    def by_device(g):
        return g.reshape((4, 2, g.shape[0] // N_DEV) + g.shape[1:]) if g.ndim == 2 else g.reshape((4, 2) + g.shape[1:])

    dwb, dwkv, dwob = by_device(dwb), by_device(dwkv), by_device(dwob)
    *outs, l_wb, l_wkv, l_wob = _bwd_a(dh1, r0, p, ln_gb, ws, bsp_t, woa, post_ln_g,
                                       _Exchange(pair=(dwb, dwkv, dwob)))
    dp, dr0, dwoa, dws, dbsp_t, dln, dpost0 = outs
    dwoa = by_device(dwoa)
    small_shapes = [(A_GROUPS * CHUNK, CHUNK), (A_GROUPS, CHUNK), (1, N_Q_HEADS), (REL_BUCKETS, N_Q_HEADS),
                    (16, CHUNK), (16, CHUNK), (16, CHUNK), (16, CHUNK)]
    small = by_device(_pack_small([
        dws.reshape(A_GROUPS * CHUNK, CHUNK), dbsp_t[:, :A_GROUPS].T, dsink[:, :N_Q_HEADS], drel[:, :N_Q_HEADS],
        jnp.concatenate([dpost0[0:1], dg1], axis=0).reshape(16, CHUNK),
        jnp.concatenate([dpost0[1:2], db1], axis=0).reshape(16, CHUNK),
        dln[0].reshape(16, CHUNK), dln[1].reshape(16, CHUNK)]))
    s_wb, s_wkv, s_wob = (_pair_sum(place, g, l, BF16) for g, l in ((dwb, l_wb), (dwkv, l_wkv), (dwob, l_wob)))
    dwa, l_woa, l_small, g_wb, g_wkv, g_wob = _in_proj_weight_grad(
        xb, dp, wa.shape[2], _Exchange(pair=(dwoa, small), chip=(s_wb, s_wkv, s_wob)))
    dwa = by_device(dwa)
    s_woa, s_small = _pair_sum(place, dwoa, l_woa, BF16), _pair_sum(place, small, l_small, F32)
    dx, l_wa, g_woa, g_small = _input_grad(dp, dr0, wa, 0, 2, None, _Exchange(pair=(dwa,), chip=(s_woa, s_small)))
    s_wa = _pair_sum(place, dwa, l_wa, BF16)
    dx, g_wa = _input_grad(dp, dr0, wa, 1, 2, dx, _Exchange(chip=(s_wa,)))
    full = [dwa, dwoa, dwkv, dwb, dwob, small]
    landed = [l_wa, l_woa, l_wkv, l_wb, l_wob, l_small]
    got = [g_wa, g_woa, g_wkv, g_wb, g_wob, g_small]

    big = [(w_in_a, m_w_in_a, v_w_in_a), (w_out_a, m_w_out_a, v_w_out_a), (w_kv, m_w_kv, v_w_kv),
           (w_in_b, m_w_in_b, v_w_in_b), (w_out_b, m_w_out_b, v_w_out_b)]
    big_out = []
    for i, (w, m, v) in enumerate(big):
        shape2 = got[i].shape[1:]
        res = _adamw_shard(place, full[i], landed[i], got[i], w.reshape(shape2), m.reshape(shape2),
                           v.reshape(shape2))
        big_out.append([r.reshape(w.shape) for r in res])

    small_all = _gather_small(full[5], landed[5], got[5]).reshape(N_DEV * SMALL_ROWS, CHUNK)
    g_ws, g_bsp, g_sink, g_rel, g_pg, g_pb, g_lng, g_lnb = _unpack_small(small_all, small_shapes)
    g_ws = g_ws.reshape(w_spatial.shape)
    g_bsp = g_bsp.reshape(b_spatial.shape)
    g_pg = g_pg.reshape(post_ln_g.shape)
    g_pb = g_pb.reshape(post_ln_b.shape)
    g_lng = lax.dynamic_slice(g_lng.reshape(1, A_WIDTH), (0, me * sgu_ln_g.shape[1]), sgu_ln_g.shape)
    g_lnb = lax.dynamic_slice(g_lnb.reshape(1, A_WIDTH), (0, me * sgu_ln_b.shape[1]), sgu_ln_b.shape)
    small_params = [(sgu_ln_g, g_lng, m_sgu_ln_g, v_sgu_ln_g), (sgu_ln_b, g_lnb, m_sgu_ln_b, v_sgu_ln_b),
                    (w_spatial, g_ws, m_w_spatial, v_w_spatial), (b_spatial, g_bsp, m_b_spatial, v_b_spatial),
                    (attn_sinks, g_sink, m_attn_sinks, v_attn_sinks), (rel_bias, g_rel, m_rel_bias, v_rel_bias),
                    (post_ln_g, g_pg, m_post_ln_g, v_post_ln_g), (post_ln_b, g_pb, m_post_ln_b, v_post_ln_b)]
    flat2 = [tuple(a.reshape(-1, a.shape[-1]) for a in p) for p in small_params]
    small_out = [tuple(r.reshape(p[0].shape) for r in res) for res, p in zip(_adamw_small(flat2), small_params)]

    loss_all = lax.psum(loss[0, 0], ("x", "y", "c"))
    order = [("big", 0), ("small", 0), ("small", 1), ("small", 2), ("small", 3), ("big", 1), ("big", 2), ("big", 3),
             ("small", 4), ("small", 5), ("big", 4), ("small", 6), ("small", 7)]
    grads, deltas, new_m, new_v = [], [], [], []
    for kind, i in order:
        if kind == "big":
            g, d, m, v = big_out[i]
        else:
            g = small_params[i][1]
            d, m, v = small_out[i]
        grads.append(g)
        deltas.append(d)
        new_m.append(m)
        new_v.append(v)
    return (loss_all, dx.reshape(x.shape), *grads, *deltas, *new_m, *new_v)
```

```python
import functools

import jax
import jax.numpy as jnp
import numpy as np
from jax import lax
from jax.experimental import pallas as pl
from jax.experimental.pallas import tpu as pltpu

F32 = jnp.float32
BF16 = jnp.bfloat16

D_MODEL = 1024
A_WIDTH = 2048
A_GROUPS = 8
A_GROUP_DIM = A_WIDTH // A_GROUPS
CHUNK = 128
HEAD_DIM = 64
N_Q_HEADS = 16
N_KV_HEADS = 2
PAIRS_PER_KV = N_Q_HEADS // N_KV_HEADS // 2
N_PAIRS = N_Q_HEADS // 2
B_WIDTH = N_Q_HEADS * HEAD_DIM
KV_WIDTH = N_KV_HEADS * HEAD_DIM
REL_BUCKETS = 32
REL_MAX_DIST = 128
DEPTH = 2
ALPHA = (2.0 * DEPTH) ** 0.25
LN_EPS = 1e-5
NEG_INF = -1e30
SCALE = HEAD_DIM ** -0.5

ADAM_LR = 0.001
ADAM_B1 = 0.9
ADAM_B2 = 0.999
ADAM_EPS = 1e-08
ADAM_WD = 0.01
ADAM_STEP = 10

N_DEV = 8
MESH = pl.DeviceIdType.MESH
TOKENS_IN_PROJ = 1024
TOKENS_WGRAD = 2048
TOKENS_A = 256
TOKENS_B = 512
TOKENS_B_BWD = 256
TOKENS_DX = 512
VMEM_LIMIT_BYTES = 60 * 1024 * 1024
SMALL_ROWS = 144

ANY = pl.BlockSpec(memory_space=pl.ANY)
VMEM = pl.BlockSpec(memory_space=pltpu.VMEM)
SMEM = pl.BlockSpec(memory_space=pltpu.SMEM)


def _nn(a, b):
    return jnp.dot(a, b, preferred_element_type=F32)


def _nt(a, b):
    return lax.dot_general(a, b, (((1,), (1,)), ((), ())), preferred_element_type=F32)


def _tn(a, b):
    return lax.dot_general(a, b, (((0,), (0,)), ((), ())), preferred_element_type=F32)


def _resident(shape):
    nd = len(shape)
    return pl.BlockSpec(shape, lambda i: (0,) * nd, pipeline_mode=pl.Buffered(1))


def _rows(tm, width, reverse_of=None):
    if reverse_of is None:
        return pl.BlockSpec((tm, width), lambda i: (i, 0))
    return pl.BlockSpec((tm, width), lambda i: (reverse_of - 1 - i, 0))


def _seq_params():
    return pltpu.CompilerParams(dimension_semantics=("arbitrary",), vmem_limit_bytes=VMEM_LIMIT_BYTES)


def _ln_normalize(r):
    mu = jnp.mean(r, axis=-1, keepdims=True)
    xc = r - mu
    var = jnp.mean(xc * xc, axis=-1, keepdims=True)
    rstd = lax.rsqrt(var + LN_EPS)
    return xc * rstd, rstd


def _ln_backward(dxhat, xhat, rstd):
    m1 = jnp.mean(dxhat, axis=-1, keepdims=True)
    m2 = jnp.mean(dxhat * xhat, axis=-1, keepdims=True)
    return rstd * (dxhat - m1 - xhat * m2)


def _silu_and_grad(z):
    sig = jax.nn.sigmoid(z)
    return z * sig, sig * (1.0 + z * (1.0 - sig))


def _band_tables():
    t = np.arange(CHUNK, dtype=np.int32)[:, None]
    j = np.arange(2 * CHUNK, dtype=np.int32)[None, :]
    d = t + CHUNK - j
    in_window = (d >= 0) & (d < CHUNK)
    dc = np.clip(d, 0, REL_MAX_DIST - 1)
    max_exact = REL_BUCKETS // 2
    df = np.maximum(dc, 1).astype(np.float32)
    large = max_exact + (np.log(df / np.float32(max_exact)) / np.float32(np.log(REL_MAX_DIST / max_exact))
                         * np.float32(REL_BUCKETS - max_exact)).astype(np.int32)
    large = np.minimum(large, REL_BUCKETS - 1)
    bucket = np.where(dc < max_exact, dc, large)
    return np.where(in_window, bucket, -1).astype(np.int32)


def _use_order():
    x, y, c = lax.axis_index("x"), lax.axis_index("y"), lax.axis_index("c")
    blocks = [(x, y, c), (x, y, 1 - c)]
    for px, py in [(1 - x, y), (x, 1 - y), (1 - x, 1 - y)]:
        blocks += [(px, py, c), (px, py, 1 - c)]
    return blocks


ARRIVAL_OF_PHASE = (None, 0, 1, 4, 2, 5, 3, 6)


def _gather_in_proj(order, x, w_in_a, small, ln_gb):
    seq = x.shape[0]
    tm = min(TOKENS_IN_PROJ, seq)
    n_blocks = seq // tm
    shard_cols = w_in_a.shape[1]
    n = len(small) + 2

    def body(order_ref, x_ref, *refs):
        ins, refs = refs[:n], refs[n:]
        p_ref, outs, refs = refs[0], refs[1:n + 1], refs[n + 1:]
        wa_vmem, xb_vmem, stage = refs[0], refs[1], refs[2:n]
        send_sems, recv_sems, local_sems, out_sems = refs[n:]
        phase, blk = pl.program_id(0), pl.program_id(1)
        x_, y_, c_ = lax.axis_index("x"), lax.axis_index("y"), lax.axis_index("c")
        me, sibling = (x_, y_, c_), (x_, y_, 1 - c_)
        chips = [(1 - x_, y_), (x_, 1 - y_), (1 - x_, 1 - y_)]
        use = _use_order()

        def index(px, py, pc):
            return 4 * px + 2 * py + pc

        def slot(a, block):
            return (wa_vmem if a == 0 else outs[a]).at[index(*block)]

        def own(a):
            return slot(0, me) if a == 0 else (ins[a] if a == n - 1 else stage[a - 1])

        def copy(a, k, block, to, src=None):
            return pltpu.make_async_remote_copy(
                src_ref=slot(a, block) if src is None else src, dst_ref=slot(a, block),
                send_sem=send_sems.at[a, k], recv_sem=recv_sems.at[a, k], device_id=to, device_id_type=MESH)

        def keep(a):
            return pltpu.make_async_copy(own(a), slot(a, me), local_sems.at[a])

        def write_out(k):
            return pltpu.make_async_copy(slot(0, use[k]), outs[0].at[index(*use[k])], out_sems.at[k])

        def pass_on(a, j):
            copy(a, 1 + j, (*chips[j], c_), me).wait_recv()
            copy(a, 4 + j, (*chips[j], c_), sibling).start()

        def send(a, j):
            copy(a, 1 + j, me, (*chips[j], c_), own(a)).start()

        @pl.when((phase == 0) & (blk == 0))
        def _():
            wa_vmem[index(*me)] = ins[0][...].astype(BF16)
            for a in range(1, n - 1):
                stage[a - 1][...] = ins[a][...].astype(BF16)
            send(0, 0)
            send(0, 1)
            for a in range(n):
                copy(a, 0, me, sibling, own(a)).start()
                if a:
                    keep(a).start()
            write_out(0).start()

        for k in range(1, N_DEV):
            @pl.when((phase == k) & (blk == 0))
            def _(k=k):
                if k % 2 == 0:
                    pass_on(0, k // 2 - 1)
                else:
                    copy(0, ARRIVAL_OF_PHASE[k], use[k], me).wait_recv()
                write_out(k).start()
                if k == 2:
                    send(0, 2)
                    for a in range(1, n):
                        for j in range(3):
                            send(a, j)
                if k == N_DEV - 2:
                    for a in range(1, n):
                        pass_on(a, 0)
                        pass_on(a, 1)

        rows = pl.ds(pl.multiple_of(blk * tm, tm), tm)

        @pl.when(phase == 0)
        def _():
            xb_vmem[rows, :] = x_ref[...].astype(BF16)

        p_ref[...] = _nn(xb_vmem[rows, :], wa_vmem[order_ref[phase]])

        @pl.when((phase == N_DEV - 1) & (blk == n_blocks - 1))
        def _():
            for a in range(1, n):
                pass_on(a, 2)
            for a in range(1, n):
                copy(a, 0, sibling, me).wait_recv()
                for j, chip in enumerate(chips):
                    copy(a, 4 + j, (*chip, 1 - c_), me).wait_recv()
            for a in range(n):
                for k in range(7):
                    copy(a, k, me, me).wait_send()
                if a:
                    keep(a).wait()
            for k in range(N_DEV):
                write_out(k).wait()

    gathered = [jax.ShapeDtypeStruct((N_DEV,) + w_in_a.shape, BF16)]
    gathered += [jax.ShapeDtypeStruct((N_DEV,) + s.shape, BF16) for s in small]
    gathered.append(jax.ShapeDtypeStruct((N_DEV,) + ln_gb.shape, F32))
    return pl.pallas_call(
        body, name="gather_in_proj",
        out_shape=[jax.ShapeDtypeStruct((seq, N_DEV * shard_cols), F32)] + gathered,
        grid_spec=pltpu.PrefetchScalarGridSpec(
            num_scalar_prefetch=1, grid=(N_DEV, n_blocks),
            in_specs=[pl.BlockSpec((tm, D_MODEL), lambda k, i, order: (jnp.where(k == 0, i, n_blocks - 1), 0))]
            + [VMEM] * n,
            out_specs=[pl.BlockSpec((tm, shard_cols), lambda k, i, order: (i, order[k]))] + [ANY] * n,
            scratch_shapes=[pltpu.VMEM((N_DEV,) + w_in_a.shape, BF16), pltpu.VMEM((seq, D_MODEL), BF16)]
            + [pltpu.VMEM(s.shape, BF16) for s in small] + [
                pltpu.SemaphoreType.DMA((n, 7)), pltpu.SemaphoreType.DMA((n, 7)), pltpu.SemaphoreType.DMA((n,)),
                pltpu.SemaphoreType.DMA((N_DEV,))]),
        compiler_params=pltpu.CompilerParams(dimension_semantics=("arbitrary", "arbitrary"),
                                             vmem_limit_bytes=VMEM_LIMIT_BYTES),
    )(order, x, w_in_a, *small, ln_gb)


def _bias_table(rel_bias, bucket):
    def body(rel_ref, bucket_ref, out_ref):
        bk = bucket_ref[...]
        col = lax.broadcasted_iota(jnp.int32, bk.shape, 1)
        for h in range(N_Q_HEADS):
            def step(b, acc):
                return jnp.where(bk == b, rel_ref[b, h], acc)
            tab = lax.fori_loop(0, REL_BUCKETS, step, jnp.full(bk.shape, NEG_INF, F32))
            pair = h // 2
            rows = slice((pair % PAIRS_PER_KV) * CHUNK, (pair % PAIRS_PER_KV + 1) * CHUNK)
            half = slice((h % 2) * 2 * CHUNK, (h % 2 + 1) * 2 * CHUNK)
            out_ref[1, pair // PAIRS_PER_KV, rows, half] = tab
            out_ref[0, pair // PAIRS_PER_KV, rows, half] = jnp.where(col < CHUNK, NEG_INF, tab)

    return pl.pallas_call(
        body, name="bias_table",
        out_shape=jax.ShapeDtypeStruct((2, N_KV_HEADS, PAIRS_PER_KV * CHUNK, 4 * CHUNK), F32),
        in_specs=[SMEM, VMEM], out_specs=VMEM,
    )(rel_bias, bucket)


def _rel_bias_grad(dlogits, bucket):
    def body(dl_ref, bucket_ref, out_ref):
        bk = bucket_ref[...]
        row = lax.broadcasted_iota(jnp.int32, (REL_BUCKETS, CHUNK), 0)
        lane = lax.broadcasted_iota(jnp.int32, (REL_BUCKETS, CHUNK), 1)

        def step(b, acc):
            for h in range(N_Q_HEADS):
                pair = h // 2
                dl = dl_ref[pair // PAIRS_PER_KV, (pair % PAIRS_PER_KV) * CHUNK:(pair % PAIRS_PER_KV + 1) * CHUNK,
                            (h % 2) * 2 * CHUNK:(h % 2 + 1) * 2 * CHUNK]
                s = jnp.sum(jnp.where(bk == b, dl, 0.0))
                acc = jnp.where((row == b) & (lane == h), s, acc)
            return acc

        out_ref[...] = lax.fori_loop(0, REL_BUCKETS, step, jnp.zeros((REL_BUCKETS, CHUNK), F32))

    return pl.pallas_call(
        body, name="rel_bias_grad", out_shape=jax.ShapeDtypeStruct((REL_BUCKETS, CHUNK), F32),
        in_specs=[VMEM, VMEM], out_specs=VMEM,
    )(dlogits, bucket)


def _masked_spatial(ws_ref, g):
    t = lax.broadcasted_iota(jnp.int32, (CHUNK, CHUNK), 0)
    s = lax.broadcasted_iota(jnp.int32, (CHUNK, CHUNK), 1)
    return jnp.where(s <= t, ws_ref[g], 0.0).astype(BF16)


def _spatial_gate(ws_ref, bsp_t_ref, vn, s_ref, n_chunks):
    for g in range(A_GROUPS):
        w = _masked_spatial(ws_ref, g)
        cols = slice(g * A_GROUP_DIM, (g + 1) * A_GROUP_DIM)
        for ci in range(n_chunks):
            rows = slice(ci * CHUNK, (ci + 1) * CHUNK)
            s_ref[rows, cols] = _nn(w, vn[rows, cols]) + bsp_t_ref[:, g:g + 1]


def _fwd_a(x, p, ln_gb, w_spatial, bsp_t, woa, later):
    seq = x.shape[0]
    tm = min(TOKENS_A, seq)
    n_chunks = tm // CHUNK
    n_blocks = seq // tm
    n = len(later)

    def body(x_ref, p_ref, ln_ref, ws_ref, bsp_ref, woa_ref, *refs):
        shards, refs = refs[:n], refs[n:]
        r0_ref, xb_ref, gathered, refs = refs[0], refs[1], refs[2:2 + n], refs[2 + n:]
        s_ref, stage, (send_sems, recv_sems, local_sems) = refs[0], refs[1:1 + n], refs[1 + n:]
        step = pl.program_id(0)

        def copies(a):
            x_, y_, c_ = lax.axis_index("x"), lax.axis_index("y"), lax.axis_index("c")
            mine = 4 * x_ + 2 * y_ + c_
            out = [pltpu.make_async_copy(stage[a], gathered[a].at[mine], local_sems.at[a])]
            for r in range(1, N_DEV):
                to = (1 - x_ if r & 4 else x_, 1 - y_ if r & 2 else y_, 1 - c_ if r & 1 else c_)
                out.append(pltpu.make_async_remote_copy(
                    src_ref=stage[a], dst_ref=gathered[a].at[mine], send_sem=send_sems.at[a, r - 1],
                    recv_sem=recv_sems.at[a, r - 1], device_id=to, device_id_type=MESH))
            return out

        @pl.when(step == 0)
        def _():
            for a in range(n):
                stage[a][...] = shards[a][...].astype(BF16)
                for cp in copies(a):
                    cp.start()

        xv = x_ref[...]
        xb_ref[...] = xv.astype(BF16)
        vhat, _ = _ln_normalize(p_ref[:, A_WIDTH:2 * A_WIDTH])
        vn = (vhat * ln_ref[0:1, :] + ln_ref[1:2, :]).astype(BF16)
        _spatial_gate(ws_ref, bsp_ref, vn, s_ref, n_chunks)
        z = p_ref[:, 2 * A_WIDTH:]
        y = p_ref[:, :A_WIDTH] * s_ref[...] * (z * jax.nn.sigmoid(z))
        r0_ref[...] = ALPHA * xv + _nn(y.astype(BF16), woa_ref[...])

        @pl.when(step == n_blocks - 1)
        def _():
            for a in range(n):
                for cp in copies(a):
                    cp.wait()

    return pl.pallas_call(
        body, name="fwd_a", grid=(n_blocks,),
        out_shape=[jax.ShapeDtypeStruct((seq, D_MODEL), F32), jax.ShapeDtypeStruct((seq, D_MODEL), BF16)]
        + [jax.ShapeDtypeStruct((N_DEV,) + s.shape, BF16) for s in later],
        in_specs=[_rows(tm, D_MODEL), _rows(tm, 3 * A_WIDTH), _resident(ln_gb.shape), _resident(w_spatial.shape),
                  _resident(bsp_t.shape), _resident(woa.shape)] + [VMEM] * n,
        out_specs=[_rows(tm, D_MODEL), _rows(tm, D_MODEL)] + [ANY] * n,
        scratch_shapes=[pltpu.VMEM((tm, A_WIDTH), F32)] + [pltpu.VMEM(s.shape, BF16) for s in later] + [
            pltpu.SemaphoreType.DMA((n, N_DEV - 1)), pltpu.SemaphoreType.DMA((n, N_DEV - 1)),
            pltpu.SemaphoreType.DMA((n,))],
        compiler_params=_seq_params(),
    )(x, p, ln_gb, w_spatial, bsp_t, woa, *later)


def _expand_heads(kv):
    lo = lax.broadcasted_iota(jnp.int32, kv.shape, 1) < HEAD_DIM
    swapped = pltpu.roll(kv, HEAD_DIM, 1)
    zero = jnp.zeros_like(kv)
    return (jnp.where(lo, kv, zero).astype(BF16), jnp.where(lo, zero, swapped).astype(BF16),
            jnp.where(lo, swapped, zero).astype(BF16), jnp.where(lo, zero, kv).astype(BF16))


def _fold_heads(acc, kvh):
    top, bot = acc[:2 * CHUNK], acc[2 * CHUNK:]
    lo = lax.broadcasted_iota(jnp.int32, top.shape, 1) < HEAD_DIM
    t = jnp.where(lo, top, 0.0)
    b = jnp.where(lo, 0.0, bot)
    if kvh == 0:
        return t + pltpu.roll(b, HEAD_DIM, 1)
    return pltpu.roll(t, HEAD_DIM, 1) + b


STACK = PAIRS_PER_KV * CHUNK


def _pair_cols(kvh, pp):
    pair = kvh * PAIRS_PER_KV + pp
    return slice(pair * CHUNK, (pair + 1) * CHUNK)


def _stack_pairs(ref_or_val, rows, kvh):
    return jnp.concatenate([ref_or_val[rows, _pair_cols(kvh, pp)] for pp in range(PAIRS_PER_KV)], axis=0)


def _sink_columns(sink_ref):
    blk = lax.broadcasted_iota(jnp.int32, (STACK, 1), 0) // CHUNK
    out = []
    for kvh in range(N_KV_HEADS):
        halves = []
        for half in range(2):
            col = jnp.zeros((STACK, 1), F32)
            for pp in range(PAIRS_PER_KV):
                col = jnp.where(blk == pp, sink_ref[0, 2 * (kvh * PAIRS_PER_KV + pp) + half], col)
            halves.append(col)
        out.append(halves)
    return out


def _softmax_stack(logits, sink_cols):
    ps, sinks = [], []
    for half in range(2):
        lg = logits[:, half * 2 * CHUNK:(half + 1) * 2 * CHUNK]
        sk = sink_cols[half]
        m = jnp.maximum(jnp.max(lg, axis=-1, keepdims=True), sk)
        e = jnp.exp(lg - m)
        es = jnp.exp(sk - m)
        inv = 1.0 / (jnp.sum(e, axis=-1, keepdims=True) + es)
        ps.append(e * inv)
        sinks.append(es * inv)
    return jnp.concatenate(ps, axis=1), sinks


def _band_operands(ek_ref, ev_ref, ci, kvh):
    kblk = jnp.concatenate([ek_ref[ci, 2 * kvh], ek_ref[ci + 1, 2 * kvh],
                            ek_ref[ci, 2 * kvh + 1], ek_ref[ci + 1, 2 * kvh + 1]], axis=0)
    vblk = jnp.concatenate([ev_ref[ci, 2 * kvh], ev_ref[ci + 1, 2 * kvh],
                            ev_ref[ci, 2 * kvh + 1], ev_ref[ci + 1, 2 * kvh + 1]], axis=0)
    return kblk, vblk


def _fwd_b(r0, target, wb, wkv, wob, post_g, post_b, sinks, bias):
    seq = r0.shape[0]
    tm = min(TOKENS_B, seq)
    n_chunks = tm // CHUNK
    shard_cols = wb.shape[2]

    def body(r0_ref, tgt_ref, wb_ref, wkv_ref, wob_ref, pg_ref, pb_ref, sink_ref, bias_ref,
             q_ref, z_ref, kv_ref, dr1_ref, loss_ref, dg_ref, db_ref, ek_ref, ev_ref, o_ref):
        step = pl.program_id(0)

        @pl.when(step == 0)
        def _():
            ek_ref[0] = jnp.zeros(ek_ref.shape[1:], BF16)
            ev_ref[0] = jnp.zeros(ev_ref.shape[1:], BF16)
            loss_ref[...] = jnp.zeros_like(loss_ref)
            dg_ref[...] = jnp.zeros_like(dg_ref)
            db_ref[...] = jnp.zeros_like(db_ref)

        @pl.when(step > 0)
        def _():
            ek_ref[0] = ek_ref[n_chunks]
            ev_ref[0] = ev_ref[n_chunks]

        xhat0, _ = _ln_normalize(r0_ref[...])
        h1 = xhat0 * pg_ref[0:1, :] + pb_ref[0:1, :]
        h1b = h1.astype(BF16)
        for k in range(N_DEV):
            qz = _nn(h1b, wb_ref[k])
            lo, hi = k * shard_cols, (k + 1) * shard_cols
            if hi <= B_WIDTH:
                q_ref[:, lo:hi] = (qz * SCALE).astype(BF16)
            else:
                z_ref[:, lo - B_WIDTH:hi - B_WIDTH] = qz
        kv = _nn(h1b, wkv_ref[...])
        kv_ref[...] = kv.astype(BF16)
        for ci in range(n_chunks):
            rows = slice(ci * CHUNK, (ci + 1) * CHUNK)
            for i, e in enumerate(_expand_heads(kv[rows, :KV_WIDTH])):
                ek_ref[ci + 1, i] = e
            for i, e in enumerate(_expand_heads(kv[rows, KV_WIDTH:])):
                ev_ref[ci + 1, i] = e

        sink_cols = _sink_columns(sink_ref)
        for ci in range(n_chunks):
            rows = slice(ci * CHUNK, (ci + 1) * CHUNK)
            sel = jnp.where(step * n_chunks + ci > 0, 1, 0)
            for kvh in range(N_KV_HEADS):
                kblk, vblk = _band_operands(ek_ref, ev_ref, ci, kvh)
                logits = _nt(_stack_pairs(q_ref, rows, kvh), kblk) + bias_ref[sel, kvh]
                p, _ = _softmax_stack(logits, sink_cols[kvh])
                o = _nn(p.astype(BF16), vblk)
                for pp in range(PAIRS_PER_KV):
                    o_ref[rows, _pair_cols(kvh, pp)] = o[pp * CHUNK:(pp + 1) * CHUNK]

        z = z_ref[...]
        y = o_ref[...] * (z * jax.nn.sigmoid(z))
        r1 = ALPHA * h1 + _nn(y.astype(BF16), wob_ref[...])
        xhat1, rstd1 = _ln_normalize(r1)
        g1 = pg_ref[1:2, :]
        diff = xhat1 * g1 + pb_ref[1:2, :] - tgt_ref[...]
        loss_ref[...] += 0.5 * jnp.sum(jnp.mean(diff * diff, axis=-1, keepdims=True), axis=0, keepdims=True)
        dh2 = diff * (1.0 / D_MODEL)
        dg_ref[...] += jnp.sum(dh2 * xhat1, axis=0, keepdims=True)
        db_ref[...] += jnp.sum(dh2, axis=0, keepdims=True)
        dr1_ref[...] = _ln_backward(dh2 * g1, xhat1, rstd1)

    vec = pl.BlockSpec((1, D_MODEL), lambda i: (0, 0))
    return pl.pallas_call(
        body, name="fwd_b", grid=(seq // tm,),
        out_shape=[jax.ShapeDtypeStruct((seq, B_WIDTH), BF16), jax.ShapeDtypeStruct((seq, B_WIDTH), F32),
                   jax.ShapeDtypeStruct((seq, 2 * KV_WIDTH), BF16), jax.ShapeDtypeStruct((seq, D_MODEL), F32),
                   jax.ShapeDtypeStruct((1, 1), F32), jax.ShapeDtypeStruct((1, D_MODEL), F32),
                   jax.ShapeDtypeStruct((1, D_MODEL), F32)],
        in_specs=[_rows(tm, D_MODEL), _rows(tm, D_MODEL), _resident(wb.shape), _resident(wkv.shape),
                  _resident(wob.shape), _resident(post_g.shape), _resident(post_b.shape), SMEM,
                  _resident(bias.shape)],
        out_specs=[_rows(tm, B_WIDTH), _rows(tm, B_WIDTH), _rows(tm, 2 * KV_WIDTH), _rows(tm, D_MODEL),
                   pl.BlockSpec((1, 1), lambda i: (0, 0)), vec, vec],
        scratch_shapes=[pltpu.VMEM((n_chunks + 1, 4, CHUNK, CHUNK), BF16),
                        pltpu.VMEM((n_chunks + 1, 4, CHUNK, CHUNK), BF16),
                        pltpu.VMEM((tm, B_WIDTH), F32)],
        compiler_params=_seq_params(),
    )(r0, target, wb, wkv, wob, post_g, post_b, sinks, bias)


def _bwd_b(r0, dr1, q, z, kv, wb, wkv, wob, post_g, post_b, sinks, bias):
    seq = r0.shape[0]
    tm = min(TOKENS_B_BWD, seq)
    n_chunks = tm // CHUNK
    n_blocks = seq // tm
    shard_cols = wb.shape[2]

    def body(r0_ref, dr1_ref, q_ref, z_ref, kv_ref, kvp_ref, wb_ref, wkv_ref, wob_ref, pg_ref, pb_ref, sink_ref,
             bias_ref, dh1_ref, dwb_out, dwkv_out, dwob_out, dsink_ref, dl_out,
             dwb_acc, dwkv_acc, dwob_acc, dl_acc, dsink_acc, carry_ref, ek_ref, ev_ref, y_ref, dqz_ref, dkv_ref):
        step = pl.program_id(0)
        block = n_blocks - 1 - step

        @pl.when(step == 0)
        def _():
            dwb_acc[...] = jnp.zeros_like(dwb_acc)
            dwkv_acc[...] = jnp.zeros_like(dwkv_acc)
            dwob_acc[...] = jnp.zeros_like(dwob_acc)
            dl_acc[...] = jnp.zeros_like(dl_acc)
            carry_ref[...] = jnp.zeros_like(carry_ref)
            dsink_acc[...] = jnp.zeros_like(dsink_acc)

        xhat0, _ = _ln_normalize(r0_ref[...])
        h1b = (xhat0 * pg_ref[0:1, :] + pb_ref[0:1, :]).astype(BF16)
        dr1 = dr1_ref[...]
        dr1b = dr1.astype(BF16)
        dy = _nt(dr1b, wob_ref[...])
        sg, dsg = _silu_and_grad(z_ref[...])
        do_all = dy * sg
        dyz = dy * dsg

        for ci in range(n_chunks + 1):
            src = kvp_ref[...] if ci == 0 else kv_ref[(ci - 1) * CHUNK:ci * CHUNK, :]
            src = src.astype(F32)
            for i, e in enumerate(_expand_heads(src[:, :KV_WIDTH])):
                ek_ref[ci, i] = e
            for i, e in enumerate(_expand_heads(src[:, KV_WIDTH:])):
                ev_ref[ci, i] = e

        sink_cols = _sink_columns(sink_ref)
        for ci in reversed(range(n_chunks)):
            rows = slice(ci * CHUNK, (ci + 1) * CHUNK)
            sel = jnp.where(block * n_chunks + ci > 0, 1, 0)
            dk_band = jnp.zeros((2 * CHUNK, KV_WIDTH), F32)
            dv_band = jnp.zeros((2 * CHUNK, KV_WIDTH), F32)
            for kvh in range(N_KV_HEADS):
                kblk, vblk = _band_operands(ek_ref, ev_ref, ci, kvh)
                qs = _stack_pairs(q_ref, rows, kvh)
                p, psink = _softmax_stack(_nt(qs, kblk) + bias_ref[sel, kvh], sink_cols[kvh])
                pb16 = p.astype(BF16)
                o = _nn(pb16, vblk)
                y = (o * _stack_pairs(sg, rows, kvh)).astype(BF16)
                dz = (o * _stack_pairs(dyz, rows, kvh)).astype(BF16)
                dob = _stack_pairs(do_all, rows, kvh).astype(BF16)
                dp = _nt(dob, vblk)
                dlogits = []
                for half in range(2):
                    hs = slice(half * 2 * CHUNK, (half + 1) * 2 * CHUNK)
                    delta = jnp.sum(dp[:, hs] * p[:, hs], axis=-1, keepdims=True)
                    dlogits.append(p[:, hs] * (dp[:, hs] - delta))
                    dsink_acc[2 * kvh + half] += psink[half] * delta
                dlog = jnp.concatenate(dlogits, axis=1)
                dl_acc[kvh] += dlog
                dsc = dlog.astype(BF16)
                dq = (_nn(dsc, kblk) * SCALE).astype(BF16)
                for pp in range(PAIRS_PER_KV):
                    blk = slice(pp * CHUNK, (pp + 1) * CHUNK)
                    cols = _pair_cols(kvh, pp)
                    y_ref[rows, cols] = y[blk]
                    dqz_ref[rows, cols] = dq[blk]
                    dqz_ref[rows, B_WIDTH + cols.start:B_WIDTH + cols.stop] = dz[blk]
                dk_band = dk_band + _fold_heads(_tn(dsc, qs), kvh)
                dv_band = dv_band + _fold_heads(_tn(pb16, dob), kvh)
            band = jnp.concatenate([dk_band, dv_band], axis=1)
            dkv_ref[rows, :] = (band[CHUNK:] + carry_ref[...]).astype(BF16)
            carry_ref[...] = band[:CHUNK]

        dqz = dqz_ref[...]
        dkv = dkv_ref[...]
        dh1 = ALPHA * dr1 + _nt(dkv, wkv_ref[...])
        for k in range(N_DEV):
            part = dqz[:, k * shard_cols:(k + 1) * shard_cols]
            dh1 = dh1 + _nt(part, wb_ref[k])
            dwb_acc[k] += _tn(h1b, part)
        dh1_ref[...] = dh1
        dwkv_acc[...] += _tn(h1b, dkv)
        dwob_acc[...] += _tn(y_ref[...], dr1b)

        @pl.when(step == n_blocks - 1)
        def _():
            pltpu.sync_copy(dwb_acc, dwb_out)
            pltpu.sync_copy(dwkv_acc, dwkv_out)
            pltpu.sync_copy(dwob_acc, dwob_out)
            pltpu.sync_copy(dl_acc, dl_out)
            lane = lax.broadcasted_iota(jnp.int32, (1, CHUNK), 1)
            dsink = jnp.zeros((1, CHUNK), F32)
            for kvh in range(N_KV_HEADS):
                for half in range(2):
                    for pp in range(PAIRS_PER_KV):
                        head = 2 * (kvh * PAIRS_PER_KV + pp) + half
                        tot = jnp.sum(dsink_acc[2 * kvh + half, pp * CHUNK:(pp + 1) * CHUNK, :], axis=0, keepdims=True)
                        dsink = dsink - jnp.where(lane == head, tot, 0.0)
            dsink_ref[...] = dsink

    rev = functools.partial(_rows, tm, reverse_of=n_blocks)
    prev_chunk = pl.BlockSpec(
        (CHUNK, 2 * KV_WIDTH), lambda i: (jnp.maximum((n_blocks - 1 - i) * n_chunks - 1, 0), 0))
    return pl.pallas_call(
        body, name="bwd_b", grid=(n_blocks,),
        out_shape=[jax.ShapeDtypeStruct((seq, D_MODEL), F32), jax.ShapeDtypeStruct(wb.shape, F32),
                   jax.ShapeDtypeStruct(wkv.shape, F32), jax.ShapeDtypeStruct(wob.shape, F32),
                   jax.ShapeDtypeStruct((1, CHUNK), F32),
                   jax.ShapeDtypeStruct((N_KV_HEADS, STACK, 4 * CHUNK), F32)],
        in_specs=[rev(D_MODEL), rev(D_MODEL), rev(B_WIDTH), rev(B_WIDTH), rev(2 * KV_WIDTH), prev_chunk,
                  _resident(wb.shape), _resident(wkv.shape), _resident(wob.shape), _resident(post_g.shape),
                  _resident(post_b.shape), SMEM, _resident(bias.shape)],
        out_specs=[rev(D_MODEL), ANY, ANY, ANY, pl.BlockSpec((1, CHUNK), lambda i: (0, 0)), ANY],
        scratch_shapes=[pltpu.VMEM(wb.shape, F32), pltpu.VMEM(wkv.shape, F32), pltpu.VMEM(wob.shape, F32),
                        pltpu.VMEM((N_KV_HEADS, STACK, 4 * CHUNK), F32), pltpu.VMEM((2 * N_KV_HEADS, STACK, 1), F32),
                        pltpu.VMEM((CHUNK, 2 * KV_WIDTH), F32),
                        pltpu.VMEM((n_chunks + 1, 4, CHUNK, CHUNK), BF16),
                        pltpu.VMEM((n_chunks + 1, 4, CHUNK, CHUNK), BF16),
                        pltpu.VMEM((tm, B_WIDTH), BF16), pltpu.VMEM((tm, 2 * B_WIDTH), BF16),
                        pltpu.VMEM((tm, 2 * KV_WIDTH), BF16)],
        compiler_params=_seq_params(),
    )(r0, dr1, q, z, kv, kv, wb, wkv, wob, post_g, post_b, sinks, bias)


def _bwd_a(dh1, r0, p, ln_gb, w_spatial, bsp_t, woa, post_g, exchange):
    seq = r0.shape[0]
    tm = min(TOKENS_A, seq)
    n_chunks = tm // CHUNK
    n_blocks = seq // tm

    def body(dh1_ref, r0_ref, p_ref, ln_ref, ws_ref, bsp_ref, woa_ref, pg_ref, *refs):
        ex_in, refs = refs[:exchange.n], refs[exchange.n:]
        (dp_ref, dr0_ref, dwoa_out, dws_ref, dbsp_ref, dln_ref, dpost_ref), refs = refs[:7], refs[7:]
        ex_out, refs = refs[:exchange.n], refs[exchange.n:]
        (dwoa_acc, s_ref, ds_ref, dvn_ref), ex_sems = refs[:4], refs[4:]
        step = pl.program_id(0)

        @pl.when(step == 0)
        def _():
            exchange.start(ex_in, ex_out, ex_sems)
            dwoa_acc[...] = jnp.zeros_like(dwoa_acc)
            dws_ref[...] = jnp.zeros_like(dws_ref)
            dbsp_ref[...] = jnp.zeros_like(dbsp_ref)
            dln_ref[...] = jnp.zeros_like(dln_ref)
            dpost_ref[...] = jnp.zeros_like(dpost_ref)

        dh1 = dh1_ref[...]
        xhat0, rstd0 = _ln_normalize(r0_ref[...])
        dpost_ref[0:1, :] += jnp.sum(dh1 * xhat0, axis=0, keepdims=True)
        dpost_ref[1:2, :] += jnp.sum(dh1, axis=0, keepdims=True)
        dr0 = _ln_backward(dh1 * pg_ref[0:1, :], xhat0, rstd0)
        dr0_ref[...] = dr0
        dr0b = dr0.astype(BF16)

        vhat, rstd_v = _ln_normalize(p_ref[:, A_WIDTH:2 * A_WIDTH])
        ln_g = ln_ref[0:1, :]
        vn = (vhat * ln_g + ln_ref[1:2, :]).astype(BF16)
        _spatial_gate(ws_ref, bsp_ref, vn, s_ref, n_chunks)
        u = p_ref[:, :A_WIDTH]
        s = s_ref[...]
        sg, dsg = _silu_and_grad(p_ref[:, 2 * A_WIDTH:])
        dwoa_acc[...] += _tn((u * s * sg).astype(BF16), dr0b)
        dy = _nt(dr0b, woa_ref[...])
        dp_ref[:, :A_WIDTH] = (dy * s * sg).astype(BF16)
        dp_ref[:, 2 * A_WIDTH:] = (dy * u * s * dsg).astype(BF16)
        ds_ref[...] = dy * u * sg

        lane = lax.broadcasted_iota(jnp.int32, (CHUNK, CHUNK), 1)
        dbsp = jnp.zeros((CHUNK, CHUNK), F32)
        for g in range(A_GROUPS):
            w = _masked_spatial(ws_ref, g)
            cols = slice(g * A_GROUP_DIM, (g + 1) * A_GROUP_DIM)
            dws = jnp.zeros((CHUNK, CHUNK), F32)
            for ci in range(n_chunks):
                rows = slice(ci * CHUNK, (ci + 1) * CHUNK)
                ds = ds_ref[rows, cols]
                dsb = ds.astype(BF16)
                dvn_ref[rows, cols] = _tn(w, dsb)
                dws = dws + _nt(dsb, vn[rows, cols])
                dbsp = dbsp + jnp.where(lane == g, jnp.sum(ds, axis=-1, keepdims=True), 0.0)
            dws_ref[g] += dws
        dbsp_ref[...] += dbsp

        dvn = dvn_ref[...]
        dln_ref[0:1, :] += jnp.sum(dvn * vhat, axis=0, keepdims=True)
        dln_ref[1:2, :] += jnp.sum(dvn, axis=0, keepdims=True)
        dp_ref[:, A_WIDTH:2 * A_WIDTH] = _ln_backward(dvn * ln_g, vhat, rstd_v).astype(BF16)

        @pl.when(step == n_blocks - 1)
        def _():
            t = lax.broadcasted_iota(jnp.int32, (CHUNK, CHUNK), 0)
            for g in range(A_GROUPS):
                dws_ref[g] = jnp.where(lane <= t, dws_ref[g], 0.0)
            pltpu.sync_copy(dwoa_acc, dwoa_out)
            exchange.wait(ex_in, ex_out, ex_sems)

    def fixed(shape):
        nd = len(shape)
        return pl.BlockSpec(shape, lambda i: (0,) * nd)

    return pl.pallas_call(
        body, name="bwd_a", grid=(n_blocks,),
        out_shape=[jax.ShapeDtypeStruct((seq, 3 * A_WIDTH), BF16), jax.ShapeDtypeStruct((seq, D_MODEL), F32),
                   jax.ShapeDtypeStruct(woa.shape, F32), jax.ShapeDtypeStruct(w_spatial.shape, F32),
                   jax.ShapeDtypeStruct((CHUNK, CHUNK), F32), jax.ShapeDtypeStruct((2, A_WIDTH), F32),
                   jax.ShapeDtypeStruct((2, D_MODEL), F32)] + exchange.out_shape,
        in_specs=[_rows(tm, D_MODEL), _rows(tm, D_MODEL), _rows(tm, 3 * A_WIDTH), _resident(ln_gb.shape),
                  _resident(w_spatial.shape), _resident(bsp_t.shape), _resident(woa.shape),
                  _resident(post_g.shape)] + [ANY] * exchange.n,
        out_specs=[_rows(tm, 3 * A_WIDTH), _rows(tm, D_MODEL), ANY, fixed(w_spatial.shape),
                   fixed((CHUNK, CHUNK)), fixed((2, A_WIDTH)), fixed((2, D_MODEL))] + [ANY] * exchange.n,
        scratch_shapes=[pltpu.VMEM(woa.shape, F32), pltpu.VMEM((tm, A_WIDTH), F32),
                        pltpu.VMEM((tm, A_WIDTH), F32), pltpu.VMEM((tm, A_WIDTH), F32)] + exchange.scratch_shapes,
        compiler_params=_seq_params(),
    )(dh1, r0, p, ln_gb, w_spatial, bsp_t, woa, post_g, *exchange.inputs)


def _input_grad_blocks(seq):
    tm = min(TOKENS_DX, seq // 8)
    return tm, seq // tm


def _input_grad(dp, dr0, wa, first, n_blocks, earlier, exchange):
    seq = dr0.shape[0]
    tm, _ = _input_grad_blocks(seq)
    shard_cols = wa.shape[2]
    n_own = 3 if earlier is None else 4

    def body(dp_ref, dr0_ref, wa_ref, *refs):
        refs = refs[n_own - 3:]
        ex_in, dx_ref, ex_out, ex_sems = (refs[:exchange.n], refs[exchange.n], refs[exchange.n + 1:2 * exchange.n + 1],
                                          refs[2 * exchange.n + 1:])
        step = pl.program_id(0)

        @pl.when(step == 0)
        def _():
            exchange.start(ex_in, ex_out, ex_sems)

        dx = ALPHA * dr0_ref[...]
        for k in range(N_DEV):
            dx = dx + _nt(dp_ref[:, k * shard_cols:(k + 1) * shard_cols], wa_ref[k])
        dx_ref[...] = dx

        @pl.when(step == n_blocks - 1)
        def _():
            exchange.wait(ex_in, ex_out, ex_sems)

    def rows(width):
        return pl.BlockSpec((tm, width), lambda i: (first + i, 0))

    return pl.pallas_call(
        body, name=f"input_grad_{first}", grid=(n_blocks,),
        out_shape=[jax.ShapeDtypeStruct((seq, D_MODEL), F32)] + exchange.out_shape,
        in_specs=[rows(3 * A_WIDTH), rows(D_MODEL), _resident(wa.shape)] + [ANY] * (n_own - 3 + exchange.n),
        out_specs=[rows(D_MODEL)] + [ANY] * exchange.n,
        scratch_shapes=exchange.scratch_shapes,
        input_output_aliases={} if earlier is None else {3: 0},
        compiler_params=_seq_params(),
    )(dp, dr0, wa, *(() if earlier is None else (earlier,)), *exchange.inputs)


def _in_proj_weight_grad(xb, dp, shard_cols, exchange):
    seq = xb.shape[0]
    tk = min(TOKENS_WGRAD, seq)
    n_k = seq // tk

    def body(x_ref, dp_ref, *refs):
        ex_in, out_ref, ex_out, ex_sems = (refs[:exchange.n], refs[exchange.n], refs[exchange.n + 1:2 * exchange.n + 1],
                                           refs[2 * exchange.n + 1:])
        shard, step = pl.program_id(0), pl.program_id(1)

        @pl.when((shard == 0) & (step == 0))
        def _():
            exchange.start(ex_in, ex_out, ex_sems)

        @pl.when(step == 0)
        def _():
            out_ref[...] = jnp.zeros_like(out_ref)

        out_ref[...] += _tn(x_ref[...], dp_ref[...])

        @pl.when((shard == N_DEV - 1) & (step == n_k - 1))
        def _():
            exchange.wait(ex_in, ex_out, ex_sems)

    return pl.pallas_call(
        body, name="in_proj_weight_grad", grid=(N_DEV, n_k),
        out_shape=[jax.ShapeDtypeStruct((N_DEV, D_MODEL, shard_cols), F32)] + exchange.out_shape,
        in_specs=[pl.BlockSpec((tk, D_MODEL), lambda j, k: (k, 0)),
                  pl.BlockSpec((tk, shard_cols), lambda j, k: (k, j))] + [ANY] * exchange.n,
        out_specs=[pl.BlockSpec((None, D_MODEL, shard_cols), lambda j, k: (j, 0, 0))] + [ANY] * exchange.n,
        scratch_shapes=exchange.scratch_shapes,
        compiler_params=pltpu.CompilerParams(dimension_semantics=("arbitrary", "arbitrary"),
                                             vmem_limit_bytes=VMEM_LIMIT_BYTES),
    )(xb, dp, *exchange.inputs)


class _Exchange:
    def __init__(self, pair=(), chip=()):
        self.pair, self.chip = tuple(pair), tuple(chip)
        self.inputs = self.pair + self.chip
        self.n = len(self.inputs)
        self.out_shape = [jax.ShapeDtypeStruct((4, 1) + g.shape[2:], F32) for g in self.pair]
        self.out_shape += [jax.ShapeDtypeStruct((3,) + s.shape[1:], s.dtype) for s in self.chip]
        self.scratch_shapes = [pltpu.SemaphoreType.DMA((max(len(self.pair), 1),)) for _ in range(2)]
        self.scratch_shapes += [pltpu.SemaphoreType.DMA((max(len(self.chip), 1), 3)) for _ in range(2)]

    def _copies(self, ins, outs, sems):
        if not self.n:
            return []
        pair_send, pair_recv, chip_send, chip_recv = sems
        x, y, c = lax.axis_index("x"), lax.axis_index("y"), lax.axis_index("c")
        copies = []
        for a in range(len(self.pair)):
            copies.append(pltpu.make_async_remote_copy(
                src_ref=ins[a].at[:, pl.ds(1 - c, 1)], dst_ref=outs[a],
                send_sem=pair_send.at[a], recv_sem=pair_recv.at[a], device_id=(x, y, 1 - c), device_id_type=MESH))
        for a in range(len(self.chip)):
            src, dst = ins[len(self.pair) + a], outs[len(self.pair) + a]
            for k, (px, py) in enumerate([(1 - x, y), (x, 1 - y), (1 - x, 1 - y)]):
                copies.append(pltpu.make_async_remote_copy(
                    src_ref=src.at[2 * px + py], dst_ref=dst.at[k], send_sem=chip_send.at[a, k],
                    recv_sem=chip_recv.at[a, k], device_id=(px, py, c), device_id_type=MESH))
        return copies

    def start(self, ins, outs, sems):
        for cp in self._copies(ins, outs, sems):
            cp.start()

    def wait(self, ins, outs, sems):
        for cp in self._copies(ins, outs, sems):
            cp.wait()


def _row_tile(rows, cols, itemsize=4):
    if rows * cols * itemsize <= (1 << 20):
        return rows
    t = 8
    while 2 * t * cols * itemsize <= (1 << 20) and rows % (2 * t) == 0:
        t *= 2
    assert rows % t == 0, (rows, cols)
    return t


def _pair_sum(place, grad, landed, dtype):
    _, _, rows, cols = grad.shape
    tr = _row_tile(rows, cols)

    def body(place_ref, a_ref, b_ref, out_ref):
        out_ref[...] = (a_ref[...] + b_ref[...]).astype(dtype)

    return pl.pallas_call(
        body, name="pair_sum", out_shape=jax.ShapeDtypeStruct((4, rows, cols), dtype),
        grid_spec=pltpu.PrefetchScalarGridSpec(
            num_scalar_prefetch=1, grid=(4, rows // tr),
            in_specs=[pl.BlockSpec((None, None, tr, cols), lambda k, i, pr: (k, pr[1], i, 0)),
                      pl.BlockSpec((None, None, tr, cols), lambda k, i, pr: (k, 0, i, 0))],
            out_specs=pl.BlockSpec((None, tr, cols), lambda k, i, pr: (k, i, 0))),
        compiler_params=pltpu.CompilerParams(dimension_semantics=("parallel", "parallel")),
    )(place, grad, landed)


def _adamw_math(w, g, m, v):
    m = ADAM_B1 * m + (1.0 - ADAM_B1) * g
    v = ADAM_B2 * v + (1.0 - ADAM_B2) * (g * g)
    m_hat = m / (1.0 - ADAM_B1 ** ADAM_STEP)
    v_hat = v / (1.0 - ADAM_B2 ** ADAM_STEP)
    delta = -ADAM_LR * (m_hat / (jnp.sqrt(v_hat) + ADAM_EPS) + ADAM_WD * w)
    return delta, m, v


def _adamw_shard(place, grad, landed, got, w, m, v):
    rows, cols = w.shape
    tr = _row_tile(rows, cols)

    def body(place_ref, a_ref, b_ref, got_ref, w_ref, m_ref, v_ref, g_out, d_out, m_out, v_out):
        g = a_ref[...] + b_ref[...]
        for k in range(3):
            g = g + got_ref[k].astype(F32)
        g_out[...] = g
        d_out[...], m_out[...], v_out[...] = _adamw_math(w_ref[...], g, m_ref[...], v_ref[...])

    spec = pl.BlockSpec((tr, cols), lambda i, pr: (i, 0))
    out = jax.ShapeDtypeStruct((rows, cols), F32)
    return pl.pallas_call(
        body, name="adamw_shard", out_shape=[out] * 4,
        grid_spec=pltpu.PrefetchScalarGridSpec(
            num_scalar_prefetch=1, grid=(rows // tr,),
            in_specs=[pl.BlockSpec((None, None, tr, cols), lambda i, pr: (pr[0], pr[1], i, 0)),
                      pl.BlockSpec((None, None, tr, cols), lambda i, pr: (pr[0], 0, i, 0)),
                      pl.BlockSpec((3, tr, cols), lambda i, pr: (0, i, 0)), spec, spec, spec],
            out_specs=[spec] * 4),
        compiler_params=pltpu.CompilerParams(dimension_semantics=("parallel",)),
    )(place, grad, landed, got, w, m, v)


def _gather_small(grad, landed, got):
    rows, cols = got.shape[1:]

    def body(grad_ref, landed_ref, got_ref, out_ref, send_sems, recv_sems):
        x, y, c = lax.axis_index("x"), lax.axis_index("y"), lax.axis_index("c")
        me, sibling = (x, y, c), (x, y, 1 - c)
        chips = [(1 - x, y), (x, 1 - y), (1 - x, 1 - y)]

        def slot(px, py, pc):
            return out_ref.at[4 * px + 2 * py + pc]

        def copy(k, block, to):
            return pltpu.make_async_remote_copy(
                src_ref=slot(*block), dst_ref=slot(*block), send_sem=send_sems.at[k], recv_sem=recv_sems.at[k],
                device_id=to, device_id_type=MESH)

        mine = grad_ref[2 * x + y, c] + landed_ref[2 * x + y, 0]
        out_ref[4 * x + 2 * y + c] = ((mine + got_ref[0]) + got_ref[1]) + got_ref[2]
        first = [copy(0, me, sibling)] + [copy(1 + j, me, (*chip, c)) for j, chip in enumerate(chips)]
        for cp in first:
            cp.start()
        passed = [copy(4 + j, (*chip, c), sibling) for j, chip in enumerate(chips)]
        for j, chip in enumerate(chips):
            copy(1 + j, (*chip, c), me).wait_recv()
            passed[j].start()
        copy(0, sibling, me).wait_recv()
        for j, chip in enumerate(chips):
            copy(4 + j, (*chip, 1 - c), me).wait_recv()
        for cp in first + passed:
            cp.wait_send()

    return pl.pallas_call(
        body, name="gather_small", out_shape=jax.ShapeDtypeStruct((N_DEV, rows, cols), F32),
        in_specs=[VMEM] * 3, out_specs=VMEM,
        scratch_shapes=[pltpu.SemaphoreType.DMA((7,)), pltpu.SemaphoreType.DMA((7,))],
    )(grad, landed, got)


def _adamw_small(params):
    n = len(params)

    def body(*refs):
        ins, outs = refs[:4 * n], refs[4 * n:]
        for i in range(n):
            w_ref, g_ref, m_ref, v_ref = ins[4 * i:4 * i + 4]
            d, m, v = _adamw_math(w_ref[...], g_ref[...], m_ref[...], v_ref[...])
            outs[3 * i][...] = d
            outs[3 * i + 1][...] = m
            outs[3 * i + 2][...] = v

    flat = [a for p in params for a in p]
    out_shape = [jax.ShapeDtypeStruct(p[0].shape, F32) for p in params for _ in range(3)]
    outs = pl.pallas_call(
        body, name="adamw_small", out_shape=out_shape, in_specs=[VMEM] * (4 * n), out_specs=[VMEM] * (3 * n),
    )(*flat)
    return [tuple(outs[3 * i:3 * i + 3]) for i in range(n)]


def _pack_small(parts):
    rows = []
    for a in parts:
        r, c = a.shape
        rows.append(jnp.pad(a, ((0, (-r) % 8), (0, CHUNK - c))))
    flat = jnp.concatenate(rows, axis=0)
    return jnp.pad(flat, ((0, N_DEV * SMALL_ROWS - flat.shape[0]), (0, 0))).reshape(N_DEV, SMALL_ROWS, CHUNK)


def _unpack_small(flat, shapes):
    out, at = [], 0
    for r, c in shapes:
        out.append(flat[at:at + r, :c])
        at += r + (-r) % 8
    return out


def kernel(x, w_in_a, sgu_ln_g, sgu_ln_b, w_spatial, b_spatial, w_out_a, w_kv, w_in_b, attn_sinks, rel_bias, w_out_b, post_ln_g, post_ln_b, loss_target, m_w_in_a, m_sgu_ln_g, m_sgu_ln_b, m_w_spatial, m_b_spatial, m_w_out_a, m_w_kv, m_w_in_b, m_attn_sinks, m_rel_bias, m_w_out_b, m_post_ln_g, m_post_ln_b, v_w_in_a, v_sgu_ln_g, v_sgu_ln_b, v_w_spatial, v_b_spatial, v_w_out_a, v_w_kv, v_w_in_b, v_attn_sinks, v_rel_bias, v_w_out_b, v_post_ln_g, v_post_ln_b):
    seq = x.shape[1]
    me = 4 * lax.axis_index("x") + 2 * lax.axis_index("y") + lax.axis_index("c")
    bucket = jnp.asarray(_band_tables())

    ln_shard = jnp.concatenate([sgu_ln_g, sgu_ln_b], axis=0)
    x2, tgt = x[0], loss_target[0]
    order = jnp.stack([4 * px + 2 * py + pc for px, py, pc in _use_order()]).astype(jnp.int32)
    p, wa, woa, ln_all = _gather_in_proj(order, x2, w_in_a[0], (w_out_a[0],), ln_shard)
    woa = woa.reshape(A_WIDTH, D_MODEL)
    ln_gb = jnp.transpose(ln_all, (1, 0, 2)).reshape(2, A_WIDTH)
    ws = w_spatial[0]
    bsp_t = jnp.pad(b_spatial[0].T, ((0, 0), (0, CHUNK - A_GROUPS)))
    bias = _bias_table(rel_bias, bucket)

    r0, xb, wkv, wb, wob = _fwd_a(x2, p, ln_gb, ws, bsp_t, woa, (w_kv, w_in_b[0], w_out_b[0]))
    wkv = wkv.reshape(D_MODEL, 2 * KV_WIDTH)
    wob = wob.reshape(B_WIDTH, D_MODEL)
    q, z, kv, dr1, loss, dg1, db1 = _fwd_b(r0, tgt, wb, wkv, wob, post_ln_g, post_ln_b, attn_sinks, bias)
    dh1, dwb, dwkv, dwob, dsink, dlog = _bwd_b(r0, dr1, q, z, kv, wb, wkv, wob, post_ln_g, post_ln_b, attn_sinks,
                                               bias)
    drel = _rel_bias_grad(dlog, bucket)

    place = jnp.stack([2 * lax.axis_index("x") + lax.axis_index("y"), lax.axis_index("c")]).astype(jnp.int32)

    def by_device(g):
        return g.reshape((4, 2, g.shape[0] // N_DEV) + g.shape[1:]) if g.ndim == 2 else g.reshape((4, 2) + g.shape[1:])

    dwb, dwkv, dwob = by_device(dwb), by_device(dwkv), by_device(dwob)
    *outs, l_wb, l_wkv, l_wob = _bwd_a(dh1, r0, p, ln_gb, ws, bsp_t, woa, post_ln_g,
                                       _Exchange(pair=(dwb, dwkv, dwob)))
    dp, dr0, dwoa, dws, dbsp_t, dln, dpost0 = outs
    dwoa = by_device(dwoa)
    small_shapes = [(A_GROUPS * CHUNK, CHUNK), (A_GROUPS, CHUNK), (1, N_Q_HEADS), (REL_BUCKETS, N_Q_HEADS),
                    (16, CHUNK), (16, CHUNK), (16, CHUNK), (16, CHUNK), (1, 1)]
    small = by_device(_pack_small([
        dws.reshape(A_GROUPS * CHUNK, CHUNK), dbsp_t[:, :A_GROUPS].T, dsink[:, :N_Q_HEADS], drel[:, :N_Q_HEADS],
        jnp.concatenate([dpost0[0:1], dg1], axis=0).reshape(16, CHUNK),
        jnp.concatenate([dpost0[1:2], db1], axis=0).reshape(16, CHUNK),
        dln[0].reshape(16, CHUNK), dln[1].reshape(16, CHUNK), loss]))
    s_wb, s_wkv, s_wob = (_pair_sum(place, g, l, BF16) for g, l in ((dwb, l_wb), (dwkv, l_wkv), (dwob, l_wob)))
    dwa, l_woa, l_small, g_wb, g_wkv, g_wob = _in_proj_weight_grad(
        xb, dp, wa.shape[2], _Exchange(pair=(dwoa, small), chip=(s_wb, s_wkv, s_wob)))
    dwa = by_device(dwa)
    s_woa, s_small = _pair_sum(place, dwoa, l_woa, BF16), _pair_sum(place, small, l_small, F32)
    _, blocks = _input_grad_blocks(seq)
    cut0, cut1 = blocks // 4, blocks - blocks // 8
    dx, l_wa, g_woa, g_small = _input_grad(dp, dr0, wa, 0, cut0, None,
                                           _Exchange(pair=(dwa,), chip=(s_woa, s_small)))
    s_wa = _pair_sum(place, dwa, l_wa, BF16)
    dx, g_wa = _input_grad(dp, dr0, wa, cut0, cut1 - cut0, dx, _Exchange(chip=(s_wa,)))
    dx, = _input_grad(dp, dr0, wa, cut1, blocks - cut1, dx, _Exchange())
    full = [dwa, dwoa, dwkv, dwb, dwob, small]
    landed = [l_wa, l_woa, l_wkv, l_wb, l_wob, l_small]
    got = [g_wa, g_woa, g_wkv, g_wb, g_wob, g_small]

    big = [(w_in_a, m_w_in_a, v_w_in_a), (w_out_a, m_w_out_a, v_w_out_a), (w_kv, m_w_kv, v_w_kv),
           (w_in_b, m_w_in_b, v_w_in_b), (w_out_b, m_w_out_b, v_w_out_b)]
    big_out = []
    for i, (w, m, v) in enumerate(big):
        shape2 = got[i].shape[1:]
        res = _adamw_shard(place, full[i], landed[i], got[i], w.reshape(shape2), m.reshape(shape2),
                           v.reshape(shape2))
        big_out.append([r.reshape(w.shape) for r in res])

    small_all = _gather_small(full[5], landed[5], got[5]).reshape(N_DEV * SMALL_ROWS, CHUNK)
    g_ws, g_bsp, g_sink, g_rel, g_pg, g_pb, g_lng, g_lnb, loss_all = _unpack_small(small_all, small_shapes)
    g_ws = g_ws.reshape(w_spatial.shape)
    g_bsp = g_bsp.reshape(b_spatial.shape)
    g_pg = g_pg.reshape(post_ln_g.shape)
    g_pb = g_pb.reshape(post_ln_b.shape)
    g_lng = lax.dynamic_slice(g_lng.reshape(1, A_WIDTH), (0, me * sgu_ln_g.shape[1]), sgu_ln_g.shape)
    g_lnb = lax.dynamic_slice(g_lnb.reshape(1, A_WIDTH), (0, me * sgu_ln_b.shape[1]), sgu_ln_b.shape)
    small_params = [(sgu_ln_g, g_lng, m_sgu_ln_g, v_sgu_ln_g), (sgu_ln_b, g_lnb, m_sgu_ln_b, v_sgu_ln_b),
                    (w_spatial, g_ws, m_w_spatial, v_w_spatial), (b_spatial, g_bsp, m_b_spatial, v_b_spatial),
                    (attn_sinks, g_sink, m_attn_sinks, v_attn_sinks), (rel_bias, g_rel, m_rel_bias, v_rel_bias),
                    (post_ln_g, g_pg, m_post_ln_g, v_post_ln_g), (post_ln_b, g_pb, m_post_ln_b, v_post_ln_b)]
    flat2 = [tuple(a.reshape(-1, a.shape[-1]) for a in p) for p in small_params]
    small_out = [tuple(r.reshape(p[0].shape) for r in res) for res, p in zip(_adamw_small(flat2), small_params)]

    order = [("big", 0), ("small", 0), ("small", 1), ("small", 2), ("small", 3), ("big", 1), ("big", 2), ("big", 3),
             ("small", 4), ("small", 5), ("big", 4), ("small", 6), ("small", 7)]
    grads, deltas, new_m, new_v = [], [], [], []
    for kind, i in order:
        if kind == "big":
            g, d, m, v = big_out[i]
        else:
            g = small_params[i][1]
            d, m, v = small_out[i]
        grads.append(g)
        deltas.append(d)
        new_m.append(m)
        new_v.append(v)
    return (loss_all[0, 0], dx.reshape(x.shape), *grads, *deltas, *new_m, *new_v)
```

```python
import functools

import jax
import jax.numpy as jnp
import numpy as np
from jax import lax
from jax.experimental import pallas as pl
from jax.experimental.pallas import tpu as pltpu

F32 = jnp.float32
BF16 = jnp.bfloat16

D_MODEL = 1024
A_WIDTH = 2048
A_GROUPS = 8
A_GROUP_DIM = A_WIDTH // A_GROUPS
CHUNK = 128
HEAD_DIM = 64
N_Q_HEADS = 16
N_KV_HEADS = 2
PAIRS_PER_KV = N_Q_HEADS // N_KV_HEADS // 2
N_PAIRS = N_Q_HEADS // 2
B_WIDTH = N_Q_HEADS * HEAD_DIM
KV_WIDTH = N_KV_HEADS * HEAD_DIM
REL_BUCKETS = 32
REL_MAX_DIST = 128
DEPTH = 2
ALPHA = (2.0 * DEPTH) ** 0.25
LN_EPS = 1e-5
NEG_INF = -1e30
SCALE = HEAD_DIM ** -0.5

ADAM_LR = 0.001
ADAM_B1 = 0.9
ADAM_B2 = 0.999
ADAM_EPS = 1e-08
ADAM_WD = 0.01
ADAM_STEP = 10

N_DEV = 8
MESH = pl.DeviceIdType.MESH
TOKENS_IN_PROJ = 1024
TOKENS_WGRAD = 2048
TOKENS_A = 256
TOKENS_B = 512
TOKENS_B_BWD = 256
TOKENS_DX = 512
VMEM_LIMIT_BYTES = 60 * 1024 * 1024
SMALL_ROWS = 144

ANY = pl.BlockSpec(memory_space=pl.ANY)
VMEM = pl.BlockSpec(memory_space=pltpu.VMEM)
SMEM = pl.BlockSpec(memory_space=pltpu.SMEM)


def _nn(a, b):
    return jnp.dot(a, b, preferred_element_type=F32)


def _nt(a, b):
    return lax.dot_general(a, b, (((1,), (1,)), ((), ())), preferred_element_type=F32)


def _tn(a, b):
    return lax.dot_general(a, b, (((0,), (0,)), ((), ())), preferred_element_type=F32)


def _resident(shape):
    nd = len(shape)
    return pl.BlockSpec(shape, lambda i: (0,) * nd, pipeline_mode=pl.Buffered(1))


def _rows(tm, width, reverse_of=None):
    if reverse_of is None:
        return pl.BlockSpec((tm, width), lambda i: (i, 0))
    return pl.BlockSpec((tm, width), lambda i: (reverse_of - 1 - i, 0))


def _seq_params():
    return pltpu.CompilerParams(dimension_semantics=("arbitrary",), vmem_limit_bytes=VMEM_LIMIT_BYTES)


def _ln_normalize(r):
    mu = jnp.mean(r, axis=-1, keepdims=True)
    xc = r - mu
    var = jnp.mean(xc * xc, axis=-1, keepdims=True)
    rstd = lax.rsqrt(var + LN_EPS)
    return xc * rstd, rstd


def _ln_backward(dxhat, xhat, rstd):
    m1 = jnp.mean(dxhat, axis=-1, keepdims=True)
    m2 = jnp.mean(dxhat * xhat, axis=-1, keepdims=True)
    return rstd * (dxhat - m1 - xhat * m2)


def _silu_and_grad(z):
    sig = jax.nn.sigmoid(z)
    return z * sig, sig * (1.0 + z * (1.0 - sig))


def _band_tables():
    t = np.arange(CHUNK, dtype=np.int32)[:, None]
    j = np.arange(2 * CHUNK, dtype=np.int32)[None, :]
    d = t + CHUNK - j
    in_window = (d >= 0) & (d < CHUNK)
    dc = np.clip(d, 0, REL_MAX_DIST - 1)
    max_exact = REL_BUCKETS // 2
    df = np.maximum(dc, 1).astype(np.float32)
    large = max_exact + (np.log(df / np.float32(max_exact)) / np.float32(np.log(REL_MAX_DIST / max_exact))
                         * np.float32(REL_BUCKETS - max_exact)).astype(np.int32)
    large = np.minimum(large, REL_BUCKETS - 1)
    bucket = np.where(dc < max_exact, dc, large)
    return np.where(in_window, bucket, -1).astype(np.int32)


def _use_order():
    x, y, c = lax.axis_index("x"), lax.axis_index("y"), lax.axis_index("c")
    blocks = [(x, y, c), (x, y, 1 - c)]
    for px, py in [(1 - x, y), (x, 1 - y), (1 - x, 1 - y)]:
        blocks += [(px, py, c), (px, py, 1 - c)]
    return blocks


ARRIVAL_OF_PHASE = (None, 0, 1, 4, 2, 5, 3, 6)


def _gather_in_proj(order, x, w_in_a, small, ln_gb):
    seq = x.shape[0]
    tm = min(TOKENS_IN_PROJ, seq)
    n_blocks = seq // tm
    shard_cols = w_in_a.shape[1]
    n = len(small) + 2

    def body(order_ref, x_ref, *refs):
        ins, refs = refs[:n], refs[n:]
        p_ref, outs, refs = refs[0], refs[1:n + 1], refs[n + 1:]
        wa_vmem, xb_vmem, stage = refs[0], refs[1], refs[2:n]
        send_sems, recv_sems, local_sems, out_sems = refs[n:]
        phase, blk = pl.program_id(0), pl.program_id(1)
        x_, y_, c_ = lax.axis_index("x"), lax.axis_index("y"), lax.axis_index("c")
        me, sibling = (x_, y_, c_), (x_, y_, 1 - c_)
        chips = [(1 - x_, y_), (x_, 1 - y_), (1 - x_, 1 - y_)]
        use = _use_order()

        def index(px, py, pc):
            return 4 * px + 2 * py + pc

        def slot(a, block):
            return (wa_vmem if a == 0 else outs[a]).at[index(*block)]

        def own(a):
            return slot(0, me) if a == 0 else (ins[a] if a == n - 1 else stage[a - 1])

        def copy(a, k, block, to, src=None):
            return pltpu.make_async_remote_copy(
                src_ref=slot(a, block) if src is None else src, dst_ref=slot(a, block),
                send_sem=send_sems.at[a, k], recv_sem=recv_sems.at[a, k], device_id=to, device_id_type=MESH)

        def keep(a):
            return pltpu.make_async_copy(own(a), slot(a, me), local_sems.at[a])

        def write_out(k):
            return pltpu.make_async_copy(slot(0, use[k]), outs[0].at[index(*use[k])], out_sems.at[k])

        def pass_on(a, j):
            copy(a, 1 + j, (*chips[j], c_), me).wait_recv()
            copy(a, 4 + j, (*chips[j], c_), sibling).start()

        def send(a, j):
            copy(a, 1 + j, me, (*chips[j], c_), own(a)).start()

        @pl.when((phase == 0) & (blk == 0))
        def _():
            wa_vmem[index(*me)] = ins[0][...].astype(BF16)
            for a in range(1, n - 1):
                stage[a - 1][...] = ins[a][...].astype(BF16)
            send(0, 0)
            send(0, 1)
            for a in range(n):
                copy(a, 0, me, sibling, own(a)).start()
                if a:
                    keep(a).start()
            write_out(0).start()

        for k in range(1, N_DEV):
            @pl.when((phase == k) & (blk == 0))
            def _(k=k):
                if k % 2 == 0:
                    pass_on(0, k // 2 - 1)
                else:
                    copy(0, ARRIVAL_OF_PHASE[k], use[k], me).wait_recv()
                write_out(k).start()
                if k == 2:
                    send(0, 2)
                    for a in range(1, n):
                        for j in range(3):
                            send(a, j)
                if k == N_DEV - 2:
                    for a in range(1, n):
                        pass_on(a, 0)
                        pass_on(a, 1)

        rows = pl.ds(pl.multiple_of(blk * tm, tm), tm)

        @pl.when(phase == 0)
        def _():
            xb_vmem[rows, :] = x_ref[...].astype(BF16)

        p_ref[...] = _nn(xb_vmem[rows, :], wa_vmem[order_ref[phase]]).astype(BF16)

        @pl.when((phase == N_DEV - 1) & (blk == n_blocks - 1))
        def _():
            for a in range(1, n):
                pass_on(a, 2)
            for a in range(1, n):
                copy(a, 0, sibling, me).wait_recv()
                for j, chip in enumerate(chips):
                    copy(a, 4 + j, (*chip, 1 - c_), me).wait_recv()
            for a in range(n):
                for k in range(7):
                    copy(a, k, me, me).wait_send()
                if a:
                    keep(a).wait()
            for k in range(N_DEV):
                write_out(k).wait()

    gathered = [jax.ShapeDtypeStruct((N_DEV,) + w_in_a.shape, BF16)]
    gathered += [jax.ShapeDtypeStruct((N_DEV,) + s.shape, BF16) for s in small]
    gathered.append(jax.ShapeDtypeStruct((N_DEV,) + ln_gb.shape, F32))
    return pl.pallas_call(
        body, name="gather_in_proj",
        out_shape=[jax.ShapeDtypeStruct((seq, N_DEV * shard_cols), BF16)] + gathered,
        grid_spec=pltpu.PrefetchScalarGridSpec(
            num_scalar_prefetch=1, grid=(N_DEV, n_blocks),
            in_specs=[pl.BlockSpec((tm, D_MODEL), lambda k, i, order: (jnp.where(k == 0, i, n_blocks - 1), 0))]
            + [VMEM] * n,
            out_specs=[pl.BlockSpec((tm, shard_cols), lambda k, i, order: (i, order[k]))] + [ANY] * n,
            scratch_shapes=[pltpu.VMEM((N_DEV,) + w_in_a.shape, BF16), pltpu.VMEM((seq, D_MODEL), BF16)]
            + [pltpu.VMEM(s.shape, BF16) for s in small] + [
                pltpu.SemaphoreType.DMA((n, 7)), pltpu.SemaphoreType.DMA((n, 7)), pltpu.SemaphoreType.DMA((n,)),
                pltpu.SemaphoreType.DMA((N_DEV,))]),
        compiler_params=pltpu.CompilerParams(dimension_semantics=("arbitrary", "arbitrary"),
                                             vmem_limit_bytes=VMEM_LIMIT_BYTES),
    )(order, x, w_in_a, *small, ln_gb)


def _bias_table(rel_bias, bucket):
    def body(rel_ref, bucket_ref, out_ref):
        bk = bucket_ref[...]
        col = lax.broadcasted_iota(jnp.int32, bk.shape, 1)
        for h in range(N_Q_HEADS):
            def step(b, acc):
                return jnp.where(bk == b, rel_ref[b, h], acc)
            tab = lax.fori_loop(0, REL_BUCKETS, step, jnp.full(bk.shape, NEG_INF, F32))
            pair = h // 2
            rows = slice((pair % PAIRS_PER_KV) * CHUNK, (pair % PAIRS_PER_KV + 1) * CHUNK)
            half = slice((h % 2) * 2 * CHUNK, (h % 2 + 1) * 2 * CHUNK)
            out_ref[1, pair // PAIRS_PER_KV, rows, half] = tab
            out_ref[0, pair // PAIRS_PER_KV, rows, half] = jnp.where(col < CHUNK, NEG_INF, tab)

    return pl.pallas_call(
        body, name="bias_table",
        out_shape=jax.ShapeDtypeStruct((2, N_KV_HEADS, PAIRS_PER_KV * CHUNK, 4 * CHUNK), F32),
        in_specs=[SMEM, VMEM], out_specs=VMEM,
    )(rel_bias, bucket)


def _rel_bias_grad(dlogits, bucket):
    def body(dl_ref, bucket_ref, out_ref):
        bk = bucket_ref[...]
        row = lax.broadcasted_iota(jnp.int32, (REL_BUCKETS, CHUNK), 0)
        lane = lax.broadcasted_iota(jnp.int32, (REL_BUCKETS, CHUNK), 1)

        def step(b, acc):
            for h in range(N_Q_HEADS):
                pair = h // 2
                dl = dl_ref[pair // PAIRS_PER_KV, (pair % PAIRS_PER_KV) * CHUNK:(pair % PAIRS_PER_KV + 1) * CHUNK,
                            (h % 2) * 2 * CHUNK:(h % 2 + 1) * 2 * CHUNK]
                s = jnp.sum(jnp.where(bk == b, dl, 0.0))
                acc = jnp.where((row == b) & (lane == h), s, acc)
            return acc

        out_ref[...] = lax.fori_loop(0, REL_BUCKETS, step, jnp.zeros((REL_BUCKETS, CHUNK), F32))

    return pl.pallas_call(
        body, name="rel_bias_grad", out_shape=jax.ShapeDtypeStruct((REL_BUCKETS, CHUNK), F32),
        in_specs=[VMEM, VMEM], out_specs=VMEM,
    )(dlogits, bucket)


def _masked_spatial(ws_ref, g):
    t = lax.broadcasted_iota(jnp.int32, (CHUNK, CHUNK), 0)
    s = lax.broadcasted_iota(jnp.int32, (CHUNK, CHUNK), 1)
    return jnp.where(s <= t, ws_ref[g], 0.0).astype(BF16)


def _spatial_gate(ws_ref, bsp_t_ref, vn, s_ref, n_chunks):
    for g in range(A_GROUPS):
        w = _masked_spatial(ws_ref, g)
        cols = slice(g * A_GROUP_DIM, (g + 1) * A_GROUP_DIM)
        for ci in range(n_chunks):
            rows = slice(ci * CHUNK, (ci + 1) * CHUNK)
            s_ref[rows, cols] = _nn(w, vn[rows, cols]) + bsp_t_ref[:, g:g + 1]


def _fwd_a(x, p, ln_gb, w_spatial, bsp_t, woa, later):
    seq = x.shape[0]
    tm = min(TOKENS_A, seq)
    n_chunks = tm // CHUNK
    n_blocks = seq // tm
    n = len(later)

    def body(x_ref, p_ref, ln_ref, ws_ref, bsp_ref, woa_ref, *refs):
        shards, refs = refs[:n], refs[n:]
        r0_ref, xb_ref, gathered, refs = refs[0], refs[1], refs[2:2 + n], refs[2 + n:]
        s_ref, stage, (send_sems, recv_sems, local_sems) = refs[0], refs[1:1 + n], refs[1 + n:]
        step = pl.program_id(0)

        def copies(a):
            x_, y_, c_ = lax.axis_index("x"), lax.axis_index("y"), lax.axis_index("c")
            mine = 4 * x_ + 2 * y_ + c_
            out = [pltpu.make_async_copy(stage[a], gathered[a].at[mine], local_sems.at[a])]
            for r in range(1, N_DEV):
                to = (1 - x_ if r & 4 else x_, 1 - y_ if r & 2 else y_, 1 - c_ if r & 1 else c_)
                out.append(pltpu.make_async_remote_copy(
                    src_ref=stage[a], dst_ref=gathered[a].at[mine], send_sem=send_sems.at[a, r - 1],
                    recv_sem=recv_sems.at[a, r - 1], device_id=to, device_id_type=MESH))
            return out

        @pl.when(step == 0)
        def _():
            for a in range(n):
                stage[a][...] = shards[a][...].astype(BF16)
                for cp in copies(a):
                    cp.start()

        xv = x_ref[...]
        xb_ref[...] = xv.astype(BF16)
        vhat, _ = _ln_normalize(p_ref[:, A_WIDTH:2 * A_WIDTH].astype(F32))
        vn = (vhat * ln_ref[0:1, :] + ln_ref[1:2, :]).astype(BF16)
        _spatial_gate(ws_ref, bsp_ref, vn, s_ref, n_chunks)
        z = p_ref[:, 2 * A_WIDTH:].astype(F32)
        y = p_ref[:, :A_WIDTH].astype(F32) * s_ref[...] * (z * jax.nn.sigmoid(z))
        r0_ref[...] = ALPHA * xv + _nn(y.astype(BF16), woa_ref[...])

        @pl.when(step == n_blocks - 1)
        def _():
            for a in range(n):
                for cp in copies(a):
                    cp.wait()

    return pl.pallas_call(
        body, name="fwd_a", grid=(n_blocks,),
        out_shape=[jax.ShapeDtypeStruct((seq, D_MODEL), F32), jax.ShapeDtypeStruct((seq, D_MODEL), BF16)]
        + [jax.ShapeDtypeStruct((N_DEV,) + s.shape, BF16) for s in later],
        in_specs=[_rows(tm, D_MODEL), _rows(tm, 3 * A_WIDTH), _resident(ln_gb.shape), _resident(w_spatial.shape),
                  _resident(bsp_t.shape), _resident(woa.shape)] + [VMEM] * n,
        out_specs=[_rows(tm, D_MODEL), _rows(tm, D_MODEL)] + [ANY] * n,
        scratch_shapes=[pltpu.VMEM((tm, A_WIDTH), F32)] + [pltpu.VMEM(s.shape, BF16) for s in later] + [
            pltpu.SemaphoreType.DMA((n, N_DEV - 1)), pltpu.SemaphoreType.DMA((n, N_DEV - 1)),
            pltpu.SemaphoreType.DMA((n,))],
        compiler_params=_seq_params(),
    )(x, p, ln_gb, w_spatial, bsp_t, woa, *later)


def _expand_heads(kv):
    lo = lax.broadcasted_iota(jnp.int32, kv.shape, 1) < HEAD_DIM
    swapped = pltpu.roll(kv, HEAD_DIM, 1)
    zero = jnp.zeros_like(kv)
    return (jnp.where(lo, kv, zero).astype(BF16), jnp.where(lo, zero, swapped).astype(BF16),
            jnp.where(lo, swapped, zero).astype(BF16), jnp.where(lo, zero, kv).astype(BF16))


def _fold_heads(acc, kvh):
    top, bot = acc[:2 * CHUNK], acc[2 * CHUNK:]
    lo = lax.broadcasted_iota(jnp.int32, top.shape, 1) < HEAD_DIM
    t = jnp.where(lo, top, 0.0)
    b = jnp.where(lo, 0.0, bot)
    if kvh == 0:
        return t + pltpu.roll(b, HEAD_DIM, 1)
    return pltpu.roll(t, HEAD_DIM, 1) + b


STACK = PAIRS_PER_KV * CHUNK


def _pair_cols(kvh, pp):
    pair = kvh * PAIRS_PER_KV + pp
    return slice(pair * CHUNK, (pair + 1) * CHUNK)


def _stack_pairs(ref_or_val, rows, kvh):
    return jnp.concatenate([ref_or_val[rows, _pair_cols(kvh, pp)] for pp in range(PAIRS_PER_KV)], axis=0)


def _sink_columns(sink_ref):
    blk = lax.broadcasted_iota(jnp.int32, (STACK, 1), 0) // CHUNK
    out = []
    for kvh in range(N_KV_HEADS):
        halves = []
        for half in range(2):
            col = jnp.zeros((STACK, 1), F32)
            for pp in range(PAIRS_PER_KV):
                col = jnp.where(blk == pp, sink_ref[0, 2 * (kvh * PAIRS_PER_KV + pp) + half], col)
            halves.append(col)
        out.append(halves)
    return out


def _softmax_stack(logits, sink_cols):
    ps, sinks = [], []
    for half in range(2):
        lg = logits[:, half * 2 * CHUNK:(half + 1) * 2 * CHUNK]
        sk = sink_cols[half]
        m = jnp.maximum(jnp.max(lg, axis=-1, keepdims=True), sk)
        e = jnp.exp(lg - m)
        es = jnp.exp(sk - m)
        inv = 1.0 / (jnp.sum(e, axis=-1, keepdims=True) + es)
        ps.append(e * inv)
        sinks.append(es * inv)
    return jnp.concatenate(ps, axis=1), sinks


def _band_operands(ek_ref, ev_ref, ci, kvh):
    kblk = jnp.concatenate([ek_ref[ci, 2 * kvh], ek_ref[ci + 1, 2 * kvh],
                            ek_ref[ci, 2 * kvh + 1], ek_ref[ci + 1, 2 * kvh + 1]], axis=0)
    vblk = jnp.concatenate([ev_ref[ci, 2 * kvh], ev_ref[ci + 1, 2 * kvh],
                            ev_ref[ci, 2 * kvh + 1], ev_ref[ci + 1, 2 * kvh + 1]], axis=0)
    return kblk, vblk


def _fwd_b(r0, target, wb, wkv, wob, post_g, post_b, sinks, bias):
    seq = r0.shape[0]
    tm = min(TOKENS_B, seq)
    n_chunks = tm // CHUNK
    shard_cols = wb.shape[2]

    def body(r0_ref, tgt_ref, wb_ref, wkv_ref, wob_ref, pg_ref, pb_ref, sink_ref, bias_ref,
             q_ref, z_ref, kv_ref, dr1_ref, loss_ref, dg_ref, db_ref, ek_ref, ev_ref, o_ref):
        step = pl.program_id(0)

        @pl.when(step == 0)
        def _():
            ek_ref[0] = jnp.zeros(ek_ref.shape[1:], BF16)
            ev_ref[0] = jnp.zeros(ev_ref.shape[1:], BF16)
            loss_ref[...] = jnp.zeros_like(loss_ref)
            dg_ref[...] = jnp.zeros_like(dg_ref)
            db_ref[...] = jnp.zeros_like(db_ref)

        @pl.when(step > 0)
        def _():
            ek_ref[0] = ek_ref[n_chunks]
            ev_ref[0] = ev_ref[n_chunks]

        xhat0, _ = _ln_normalize(r0_ref[...])
        h1 = xhat0 * pg_ref[0:1, :] + pb_ref[0:1, :]
        h1b = h1.astype(BF16)
        for k in range(N_DEV):
            qz = _nn(h1b, wb_ref[k])
            lo, hi = k * shard_cols, (k + 1) * shard_cols
            if hi <= B_WIDTH:
                q_ref[:, lo:hi] = (qz * SCALE).astype(BF16)
            else:
                z_ref[:, lo - B_WIDTH:hi - B_WIDTH] = qz
        kv = _nn(h1b, wkv_ref[...])
        kv_ref[...] = kv.astype(BF16)
        for ci in range(n_chunks):
            rows = slice(ci * CHUNK, (ci + 1) * CHUNK)
            for i, e in enumerate(_expand_heads(kv[rows, :KV_WIDTH])):
                ek_ref[ci + 1, i] = e
            for i, e in enumerate(_expand_heads(kv[rows, KV_WIDTH:])):
                ev_ref[ci + 1, i] = e

        sink_cols = _sink_columns(sink_ref)
        for ci in range(n_chunks):
            rows = slice(ci * CHUNK, (ci + 1) * CHUNK)
            sel = jnp.where(step * n_chunks + ci > 0, 1, 0)
            for kvh in range(N_KV_HEADS):
                kblk, vblk = _band_operands(ek_ref, ev_ref, ci, kvh)
                logits = _nt(_stack_pairs(q_ref, rows, kvh), kblk) + bias_ref[sel, kvh]
                p, _ = _softmax_stack(logits, sink_cols[kvh])
                o = _nn(p.astype(BF16), vblk)
                for pp in range(PAIRS_PER_KV):
                    o_ref[rows, _pair_cols(kvh, pp)] = o[pp * CHUNK:(pp + 1) * CHUNK]

        z = z_ref[...]
        y = o_ref[...] * (z * jax.nn.sigmoid(z))
        r1 = ALPHA * h1 + _nn(y.astype(BF16), wob_ref[...])
        xhat1, rstd1 = _ln_normalize(r1)
        g1 = pg_ref[1:2, :]
        diff = xhat1 * g1 + pb_ref[1:2, :] - tgt_ref[...]
        loss_ref[...] += 0.5 * jnp.sum(jnp.mean(diff * diff, axis=-1, keepdims=True), axis=0, keepdims=True)
        dh2 = diff * (1.0 / D_MODEL)
        dg_ref[...] += jnp.sum(dh2 * xhat1, axis=0, keepdims=True)
        db_ref[...] += jnp.sum(dh2, axis=0, keepdims=True)
        dr1_ref[...] = _ln_backward(dh2 * g1, xhat1, rstd1)

    vec = pl.BlockSpec((1, D_MODEL), lambda i: (0, 0))
    return pl.pallas_call(
        body, name="fwd_b", grid=(seq // tm,),
        out_shape=[jax.ShapeDtypeStruct((seq, B_WIDTH), BF16), jax.ShapeDtypeStruct((seq, B_WIDTH), F32),
                   jax.ShapeDtypeStruct((seq, 2 * KV_WIDTH), BF16), jax.ShapeDtypeStruct((seq, D_MODEL), F32),
                   jax.ShapeDtypeStruct((1, 1), F32), jax.ShapeDtypeStruct((1, D_MODEL), F32),
                   jax.ShapeDtypeStruct((1, D_MODEL), F32)],
        in_specs=[_rows(tm, D_MODEL), _rows(tm, D_MODEL), _resident(wb.shape), _resident(wkv.shape),
                  _resident(wob.shape), _resident(post_g.shape), _resident(post_b.shape), SMEM,
                  _resident(bias.shape)],
        out_specs=[_rows(tm, B_WIDTH), _rows(tm, B_WIDTH), _rows(tm, 2 * KV_WIDTH), _rows(tm, D_MODEL),
                   pl.BlockSpec((1, 1), lambda i: (0, 0)), vec, vec],
        scratch_shapes=[pltpu.VMEM((n_chunks + 1, 4, CHUNK, CHUNK), BF16),
                        pltpu.VMEM((n_chunks + 1, 4, CHUNK, CHUNK), BF16),
                        pltpu.VMEM((tm, B_WIDTH), F32)],
        compiler_params=_seq_params(),
    )(r0, target, wb, wkv, wob, post_g, post_b, sinks, bias)


def _bwd_b(r0, dr1, q, z, kv, wb, wkv, wob, post_g, post_b, sinks, bias):
    seq = r0.shape[0]
    tm = min(TOKENS_B_BWD, seq)
    n_chunks = tm // CHUNK
    n_blocks = seq // tm
    shard_cols = wb.shape[2]

    def body(r0_ref, dr1_ref, q_ref, z_ref, kv_ref, kvp_ref, wb_ref, wkv_ref, wob_ref, pg_ref, pb_ref, sink_ref,
             bias_ref, dh1_ref, dwb_out, dwkv_out, dwob_out, dsink_ref, dl_out,
             dwb_acc, dwkv_acc, dwob_acc, dl_acc, dsink_acc, carry_ref, ek_ref, ev_ref, y_ref, dqz_ref, dkv_ref):
        step = pl.program_id(0)
        block = n_blocks - 1 - step

        @pl.when(step == 0)
        def _():
            dwb_acc[...] = jnp.zeros_like(dwb_acc)
            dwkv_acc[...] = jnp.zeros_like(dwkv_acc)
            dwob_acc[...] = jnp.zeros_like(dwob_acc)
            dl_acc[...] = jnp.zeros_like(dl_acc)
            carry_ref[...] = jnp.zeros_like(carry_ref)
            dsink_acc[...] = jnp.zeros_like(dsink_acc)

        xhat0, _ = _ln_normalize(r0_ref[...])
        h1b = (xhat0 * pg_ref[0:1, :] + pb_ref[0:1, :]).astype(BF16)
        dr1 = dr1_ref[...]
        dr1b = dr1.astype(BF16)
        dy = _nt(dr1b, wob_ref[...])
        sg, dsg = _silu_and_grad(z_ref[...])
        do_all = dy * sg
        dyz = dy * dsg

        for ci in range(n_chunks + 1):
            src = kvp_ref[...] if ci == 0 else kv_ref[(ci - 1) * CHUNK:ci * CHUNK, :]
            src = src.astype(F32)
            for i, e in enumerate(_expand_heads(src[:, :KV_WIDTH])):
                ek_ref[ci, i] = e
            for i, e in enumerate(_expand_heads(src[:, KV_WIDTH:])):
                ev_ref[ci, i] = e

        sink_cols = _sink_columns(sink_ref)
        for ci in reversed(range(n_chunks)):
            rows = slice(ci * CHUNK, (ci + 1) * CHUNK)
            sel = jnp.where(block * n_chunks + ci > 0, 1, 0)
            dk_band = jnp.zeros((2 * CHUNK, KV_WIDTH), F32)
            dv_band = jnp.zeros((2 * CHUNK, KV_WIDTH), F32)
            for kvh in range(N_KV_HEADS):
                kblk, vblk = _band_operands(ek_ref, ev_ref, ci, kvh)
                qs = _stack_pairs(q_ref, rows, kvh)
                p, psink = _softmax_stack(_nt(qs, kblk) + bias_ref[sel, kvh], sink_cols[kvh])
                pb16 = p.astype(BF16)
                o = _nn(pb16, vblk)
                y = (o * _stack_pairs(sg, rows, kvh)).astype(BF16)
                dz = (o * _stack_pairs(dyz, rows, kvh)).astype(BF16)
                dob = _stack_pairs(do_all, rows, kvh).astype(BF16)
                dp = _nt(dob, vblk)
                dlogits = []
                for half in range(2):
                    hs = slice(half * 2 * CHUNK, (half + 1) * 2 * CHUNK)
                    delta = jnp.sum(dp[:, hs] * p[:, hs], axis=-1, keepdims=True)
                    dlogits.append(p[:, hs] * (dp[:, hs] - delta))
                    dsink_acc[2 * kvh + half] += psink[half] * delta
                dlog = jnp.concatenate(dlogits, axis=1)
                dl_acc[kvh] += dlog
                dsc = dlog.astype(BF16)
                dq = (_nn(dsc, kblk) * SCALE).astype(BF16)
                for pp in range(PAIRS_PER_KV):
                    blk = slice(pp * CHUNK, (pp + 1) * CHUNK)
                    cols = _pair_cols(kvh, pp)
                    y_ref[rows, cols] = y[blk]
                    dqz_ref[rows, cols] = dq[blk]
                    dqz_ref[rows, B_WIDTH + cols.start:B_WIDTH + cols.stop] = dz[blk]
                dk_band = dk_band + _fold_heads(_tn(dsc, qs), kvh)
                dv_band = dv_band + _fold_heads(_tn(pb16, dob), kvh)
            band = jnp.concatenate([dk_band, dv_band], axis=1)
            dkv_ref[rows, :] = (band[CHUNK:] + carry_ref[...]).astype(BF16)
            carry_ref[...] = band[:CHUNK]

        dqz = dqz_ref[...]
        dkv = dkv_ref[...]
        dh1 = ALPHA * dr1 + _nt(dkv, wkv_ref[...])
        for k in range(N_DEV):
            part = dqz[:, k * shard_cols:(k + 1) * shard_cols]
            dh1 = dh1 + _nt(part, wb_ref[k])
            dwb_acc[k] += _tn(h1b, part)
        dh1_ref[...] = dh1
        dwkv_acc[...] += _tn(h1b, dkv)
        dwob_acc[...] += _tn(y_ref[...], dr1b)

        @pl.when(step == n_blocks - 1)
        def _():
            pltpu.sync_copy(dwb_acc, dwb_out)
            pltpu.sync_copy(dwkv_acc, dwkv_out)
            pltpu.sync_copy(dwob_acc, dwob_out)
            pltpu.sync_copy(dl_acc, dl_out)
            lane = lax.broadcasted_iota(jnp.int32, (1, CHUNK), 1)
            dsink = jnp.zeros((1, CHUNK), F32)
            for kvh in range(N_KV_HEADS):
                for half in range(2):
                    for pp in range(PAIRS_PER_KV):
                        head = 2 * (kvh * PAIRS_PER_KV + pp) + half
                        tot = jnp.sum(dsink_acc[2 * kvh + half, pp * CHUNK:(pp + 1) * CHUNK, :], axis=0, keepdims=True)
                        dsink = dsink - jnp.where(lane == head, tot, 0.0)
            dsink_ref[...] = dsink

    rev = functools.partial(_rows, tm, reverse_of=n_blocks)
    prev_chunk = pl.BlockSpec(
        (CHUNK, 2 * KV_WIDTH), lambda i: (jnp.maximum((n_blocks - 1 - i) * n_chunks - 1, 0), 0))
    return pl.pallas_call(
        body, name="bwd_b", grid=(n_blocks,),
        out_shape=[jax.ShapeDtypeStruct((seq, D_MODEL), F32), jax.ShapeDtypeStruct(wb.shape, F32),
                   jax.ShapeDtypeStruct(wkv.shape, F32), jax.ShapeDtypeStruct(wob.shape, F32),
                   jax.ShapeDtypeStruct((1, CHUNK), F32),
                   jax.ShapeDtypeStruct((N_KV_HEADS, STACK, 4 * CHUNK), F32)],
        in_specs=[rev(D_MODEL), rev(D_MODEL), rev(B_WIDTH), rev(B_WIDTH), rev(2 * KV_WIDTH), prev_chunk,
                  _resident(wb.shape), _resident(wkv.shape), _resident(wob.shape), _resident(post_g.shape),
                  _resident(post_b.shape), SMEM, _resident(bias.shape)],
        out_specs=[rev(D_MODEL), ANY, ANY, ANY, pl.BlockSpec((1, CHUNK), lambda i: (0, 0)), ANY],
        scratch_shapes=[pltpu.VMEM(wb.shape, F32), pltpu.VMEM(wkv.shape, F32), pltpu.VMEM(wob.shape, F32),
                        pltpu.VMEM((N_KV_HEADS, STACK, 4 * CHUNK), F32), pltpu.VMEM((2 * N_KV_HEADS, STACK, 1), F32),
                        pltpu.VMEM((CHUNK, 2 * KV_WIDTH), F32),
                        pltpu.VMEM((n_chunks + 1, 4, CHUNK, CHUNK), BF16),
                        pltpu.VMEM((n_chunks + 1, 4, CHUNK, CHUNK), BF16),
                        pltpu.VMEM((tm, B_WIDTH), BF16), pltpu.VMEM((tm, 2 * B_WIDTH), BF16),
                        pltpu.VMEM((tm, 2 * KV_WIDTH), BF16)],
        compiler_params=_seq_params(),
    )(r0, dr1, q, z, kv, kv, wb, wkv, wob, post_g, post_b, sinks, bias)


def _bwd_a(dh1, r0, p, ln_gb, w_spatial, bsp_t, woa, post_g, exchange):
    seq = r0.shape[0]
    tm = min(TOKENS_A, seq)
    n_chunks = tm // CHUNK
    n_blocks = seq // tm

    def body(dh1_ref, r0_ref, p_ref, ln_ref, ws_ref, bsp_ref, woa_ref, pg_ref, *refs):
        ex_in, refs = refs[:exchange.n], refs[exchange.n:]
        (dp_ref, dr0_ref, dwoa_out, dws_ref, dbsp_ref, dln_ref, dpost_ref), refs = refs[:7], refs[7:]
        ex_out, refs = refs[:exchange.n], refs[exchange.n:]
        (dwoa_acc, s_ref, ds_ref, dvn_ref), ex_sems = refs[:4], refs[4:]
        step = pl.program_id(0)

        @pl.when(step == 0)
        def _():
            exchange.start(ex_in, ex_out, ex_sems)
            dwoa_acc[...] = jnp.zeros_like(dwoa_acc)
            dws_ref[...] = jnp.zeros_like(dws_ref)
            dbsp_ref[...] = jnp.zeros_like(dbsp_ref)
            dln_ref[...] = jnp.zeros_like(dln_ref)
            dpost_ref[...] = jnp.zeros_like(dpost_ref)

        dh1 = dh1_ref[...]
        xhat0, rstd0 = _ln_normalize(r0_ref[...])
        dpost_ref[0:1, :] += jnp.sum(dh1 * xhat0, axis=0, keepdims=True)
        dpost_ref[1:2, :] += jnp.sum(dh1, axis=0, keepdims=True)
        dr0 = _ln_backward(dh1 * pg_ref[0:1, :], xhat0, rstd0)
        dr0_ref[...] = dr0
        dr0b = dr0.astype(BF16)

        vhat, rstd_v = _ln_normalize(p_ref[:, A_WIDTH:2 * A_WIDTH].astype(F32))
        ln_g = ln_ref[0:1, :]
        vn = (vhat * ln_g + ln_ref[1:2, :]).astype(BF16)
        _spatial_gate(ws_ref, bsp_ref, vn, s_ref, n_chunks)
        u = p_ref[:, :A_WIDTH].astype(F32)
        s = s_ref[...]
        sg, dsg = _silu_and_grad(p_ref[:, 2 * A_WIDTH:].astype(F32))
        dwoa_acc[...] += _tn((u * s * sg).astype(BF16), dr0b)
        dy = _nt(dr0b, woa_ref[...])
        dp_ref[:, :A_WIDTH] = (dy * s * sg).astype(BF16)
        dp_ref[:, 2 * A_WIDTH:] = (dy * u * s * dsg).astype(BF16)
        ds_ref[...] = dy * u * sg

        lane = lax.broadcasted_iota(jnp.int32, (CHUNK, CHUNK), 1)
        dbsp = jnp.zeros((CHUNK, CHUNK), F32)
        for g in range(A_GROUPS):
            w = _masked_spatial(ws_ref, g)
            cols = slice(g * A_GROUP_DIM, (g + 1) * A_GROUP_DIM)
            dws = jnp.zeros((CHUNK, CHUNK), F32)
            for ci in range(n_chunks):
                rows = slice(ci * CHUNK, (ci + 1) * CHUNK)
                ds = ds_ref[rows, cols]
                dsb = ds.astype(BF16)
                dvn_ref[rows, cols] = _tn(w, dsb)
                dws = dws + _nt(dsb, vn[rows, cols])
                dbsp = dbsp + jnp.where(lane == g, jnp.sum(ds, axis=-1, keepdims=True), 0.0)
            dws_ref[g] += dws
        dbsp_ref[...] += dbsp

        dvn = dvn_ref[...]
        dln_ref[0:1, :] += jnp.sum(dvn * vhat, axis=0, keepdims=True)
        dln_ref[1:2, :] += jnp.sum(dvn, axis=0, keepdims=True)
        dp_ref[:, A_WIDTH:2 * A_WIDTH] = _ln_backward(dvn * ln_g, vhat, rstd_v).astype(BF16)

        @pl.when(step == n_blocks - 1)
        def _():
            t = lax.broadcasted_iota(jnp.int32, (CHUNK, CHUNK), 0)
            for g in range(A_GROUPS):
                dws_ref[g] = jnp.where(lane <= t, dws_ref[g], 0.0)
            pltpu.sync_copy(dwoa_acc, dwoa_out)
            exchange.wait(ex_in, ex_out, ex_sems)

    def fixed(shape):
        nd = len(shape)
        return pl.BlockSpec(shape, lambda i: (0,) * nd)

    return pl.pallas_call(
        body, name="bwd_a", grid=(n_blocks,),
        out_shape=[jax.ShapeDtypeStruct((seq, 3 * A_WIDTH), BF16), jax.ShapeDtypeStruct((seq, D_MODEL), F32),
                   jax.ShapeDtypeStruct(woa.shape, F32), jax.ShapeDtypeStruct(w_spatial.shape, F32),
                   jax.ShapeDtypeStruct((CHUNK, CHUNK), F32), jax.ShapeDtypeStruct((2, A_WIDTH), F32),
                   jax.ShapeDtypeStruct((2, D_MODEL), F32)] + exchange.out_shape,
        in_specs=[_rows(tm, D_MODEL), _rows(tm, D_MODEL), _rows(tm, 3 * A_WIDTH), _resident(ln_gb.shape),
                  _resident(w_spatial.shape), _resident(bsp_t.shape), _resident(woa.shape),
                  _resident(post_g.shape)] + [ANY] * exchange.n,
        out_specs=[_rows(tm, 3 * A_WIDTH), _rows(tm, D_MODEL), ANY, fixed(w_spatial.shape),
                   fixed((CHUNK, CHUNK)), fixed((2, A_WIDTH)), fixed((2, D_MODEL))] + [ANY] * exchange.n,
        scratch_shapes=[pltpu.VMEM(woa.shape, F32), pltpu.VMEM((tm, A_WIDTH), F32),
                        pltpu.VMEM((tm, A_WIDTH), F32), pltpu.VMEM((tm, A_WIDTH), F32)] + exchange.scratch_shapes,
        compiler_params=_seq_params(),
    )(dh1, r0, p, ln_gb, w_spatial, bsp_t, woa, post_g, *exchange.inputs)


def _input_grad_blocks(seq):
    tm = min(TOKENS_DX, seq // 8)
    return tm, seq // tm


def _input_grad(dp, dr0, wa, first, n_blocks, earlier, exchange):
    seq = dr0.shape[0]
    tm, _ = _input_grad_blocks(seq)
    shard_cols = wa.shape[2]
    n_own = 3 if earlier is None else 4

    def body(dp_ref, dr0_ref, wa_ref, *refs):
        refs = refs[n_own - 3:]
        ex_in, dx_ref, ex_out, ex_sems = (refs[:exchange.n], refs[exchange.n], refs[exchange.n + 1:2 * exchange.n + 1],
                                          refs[2 * exchange.n + 1:])
        step = pl.program_id(0)

        @pl.when(step == 0)
        def _():
            exchange.start(ex_in, ex_out, ex_sems)

        dx = ALPHA * dr0_ref[...]
        for k in range(N_DEV):
            dx = dx + _nt(dp_ref[:, k * shard_cols:(k + 1) * shard_cols], wa_ref[k])
        dx_ref[...] = dx

        @pl.when(step == n_blocks - 1)
        def _():
            exchange.wait(ex_in, ex_out, ex_sems)

    def rows(width):
        return pl.BlockSpec((tm, width), lambda i: (first + i, 0))

    return pl.pallas_call(
        body, name=f"input_grad_{first}", grid=(n_blocks,),
        out_shape=[jax.ShapeDtypeStruct((seq, D_MODEL), F32)] + exchange.out_shape,
        in_specs=[rows(3 * A_WIDTH), rows(D_MODEL), _resident(wa.shape)] + [ANY] * (n_own - 3 + exchange.n),
        out_specs=[rows(D_MODEL)] + [ANY] * exchange.n,
        scratch_shapes=exchange.scratch_shapes,
        input_output_aliases={} if earlier is None else {3: 0},
        compiler_params=_seq_params(),
    )(dp, dr0, wa, *(() if earlier is None else (earlier,)), *exchange.inputs)


def _in_proj_weight_grad(xb, dp, shard_cols, exchange):
    seq = xb.shape[0]
    tk = min(TOKENS_WGRAD, seq)
    n_k = seq // tk

    def body(x_ref, dp_ref, *refs):
        ex_in, out_ref, ex_out, ex_sems = (refs[:exchange.n], refs[exchange.n], refs[exchange.n + 1:2 * exchange.n + 1],
                                           refs[2 * exchange.n + 1:])
        shard, step = pl.program_id(0), pl.program_id(1)

        @pl.when((shard == 0) & (step == 0))
        def _():
            exchange.start(ex_in, ex_out, ex_sems)

        @pl.when(step == 0)
        def _():
            out_ref[...] = jnp.zeros_like(out_ref)

        out_ref[...] += _tn(x_ref[...], dp_ref[...])

        @pl.when((shard == N_DEV - 1) & (step == n_k - 1))
        def _():
            exchange.wait(ex_in, ex_out, ex_sems)

    return pl.pallas_call(
        body, name="in_proj_weight_grad", grid=(N_DEV, n_k),
        out_shape=[jax.ShapeDtypeStruct((N_DEV, D_MODEL, shard_cols), F32)] + exchange.out_shape,
        in_specs=[pl.BlockSpec((tk, D_MODEL), lambda j, k: (k, 0)),
                  pl.BlockSpec((tk, shard_cols), lambda j, k: (k, j))] + [ANY] * exchange.n,
        out_specs=[pl.BlockSpec((None, D_MODEL, shard_cols), lambda j, k: (j, 0, 0))] + [ANY] * exchange.n,
        scratch_shapes=exchange.scratch_shapes,
        compiler_params=pltpu.CompilerParams(dimension_semantics=("arbitrary", "arbitrary"),
                                             vmem_limit_bytes=VMEM_LIMIT_BYTES),
    )(xb, dp, *exchange.inputs)


class _Exchange:
    def __init__(self, pair=(), chip=()):
        self.pair, self.chip = tuple(pair), tuple(chip)
        self.inputs = self.pair + self.chip
        self.n = len(self.inputs)
        self.out_shape = [jax.ShapeDtypeStruct((4, 1) + g.shape[2:], F32) for g in self.pair]
        self.out_shape += [jax.ShapeDtypeStruct((3,) + s.shape[1:], s.dtype) for s in self.chip]
        self.scratch_shapes = [pltpu.SemaphoreType.DMA((max(len(self.pair), 1),)) for _ in range(2)]
        self.scratch_shapes += [pltpu.SemaphoreType.DMA((max(len(self.chip), 1), 3)) for _ in range(2)]

    def _copies(self, ins, outs, sems):
        if not self.n:
            return []
        pair_send, pair_recv, chip_send, chip_recv = sems
        x, y, c = lax.axis_index("x"), lax.axis_index("y"), lax.axis_index("c")
        copies = []
        for a in range(len(self.pair)):
            copies.append(pltpu.make_async_remote_copy(
                src_ref=ins[a].at[:, pl.ds(1 - c, 1)], dst_ref=outs[a],
                send_sem=pair_send.at[a], recv_sem=pair_recv.at[a], device_id=(x, y, 1 - c), device_id_type=MESH))
        for a in range(len(self.chip)):
            src, dst = ins[len(self.pair) + a], outs[len(self.pair) + a]
            for k, (px, py) in enumerate([(1 - x, y), (x, 1 - y), (1 - x, 1 - y)]):
                copies.append(pltpu.make_async_remote_copy(
                    src_ref=src.at[2 * px + py], dst_ref=dst.at[k], send_sem=chip_send.at[a, k],
                    recv_sem=chip_recv.at[a, k], device_id=(px, py, c), device_id_type=MESH))
        return copies

    def start(self, ins, outs, sems):
        for cp in self._copies(ins, outs, sems):
            cp.start()

    def wait(self, ins, outs, sems):
        for cp in self._copies(ins, outs, sems):
            cp.wait()


def _row_tile(rows, cols, itemsize=4):
    if rows * cols * itemsize <= (1 << 20):
        return rows
    t = 8
    while 2 * t * cols * itemsize <= (1 << 20) and rows % (2 * t) == 0:
        t *= 2
    assert rows % t == 0, (rows, cols)
    return t


def _pair_sum(place, grad, landed, dtype):
    _, _, rows, cols = grad.shape
    tr = _row_tile(rows, cols)

    def body(place_ref, a_ref, b_ref, out_ref):
        out_ref[...] = (a_ref[...] + b_ref[...]).astype(dtype)

    return pl.pallas_call(
        body, name="pair_sum", out_shape=jax.ShapeDtypeStruct((4, rows, cols), dtype),
        grid_spec=pltpu.PrefetchScalarGridSpec(
            num_scalar_prefetch=1, grid=(4, rows // tr),
            in_specs=[pl.BlockSpec((None, None, tr, cols), lambda k, i, pr: (k, pr[1], i, 0)),
                      pl.BlockSpec((None, None, tr, cols), lambda k, i, pr: (k, 0, i, 0))],
            out_specs=pl.BlockSpec((None, tr, cols), lambda k, i, pr: (k, i, 0))),
        compiler_params=pltpu.CompilerParams(dimension_semantics=("parallel", "parallel")),
    )(place, grad, landed)


def _adamw_math(w, g, m, v):
    m = ADAM_B1 * m + (1.0 - ADAM_B1) * g
    v = ADAM_B2 * v + (1.0 - ADAM_B2) * (g * g)
    m_hat = m / (1.0 - ADAM_B1 ** ADAM_STEP)
    v_hat = v / (1.0 - ADAM_B2 ** ADAM_STEP)
    delta = -ADAM_LR * (m_hat / (jnp.sqrt(v_hat) + ADAM_EPS) + ADAM_WD * w)
    return delta, m, v


def _adamw_shard(place, grad, landed, got, w, m, v):
    rows, cols = w.shape
    tr = _row_tile(rows, cols)

    def body(place_ref, a_ref, b_ref, got_ref, w_ref, m_ref, v_ref, g_out, d_out, m_out, v_out):
        g = a_ref[...] + b_ref[...]
        for k in range(3):
            g = g + got_ref[k].astype(F32)
        g_out[...] = g
        d_out[...], m_out[...], v_out[...] = _adamw_math(w_ref[...], g, m_ref[...], v_ref[...])

    spec = pl.BlockSpec((tr, cols), lambda i, pr: (i, 0))
    out = jax.ShapeDtypeStruct((rows, cols), F32)
    return pl.pallas_call(
        body, name="adamw_shard", out_shape=[out] * 4,
        grid_spec=pltpu.PrefetchScalarGridSpec(
            num_scalar_prefetch=1, grid=(rows // tr,),
            in_specs=[pl.BlockSpec((None, None, tr, cols), lambda i, pr: (pr[0], pr[1], i, 0)),
                      pl.BlockSpec((None, None, tr, cols), lambda i, pr: (pr[0], 0, i, 0)),
                      pl.BlockSpec((3, tr, cols), lambda i, pr: (0, i, 0)), spec, spec, spec],
            out_specs=[spec] * 4),
        compiler_params=pltpu.CompilerParams(dimension_semantics=("parallel",)),
    )(place, grad, landed, got, w, m, v)


def _gather_small(grad, landed, got):
    rows, cols = got.shape[1:]

    def body(grad_ref, landed_ref, got_ref, out_ref, send_sems, recv_sems):
        x, y, c = lax.axis_index("x"), lax.axis_index("y"), lax.axis_index("c")
        me, sibling = (x, y, c), (x, y, 1 - c)
        chips = [(1 - x, y), (x, 1 - y), (1 - x, 1 - y)]

        def slot(px, py, pc):
            return out_ref.at[4 * px + 2 * py + pc]

        def copy(k, block, to):
            return pltpu.make_async_remote_copy(
                src_ref=slot(*block), dst_ref=slot(*block), send_sem=send_sems.at[k], recv_sem=recv_sems.at[k],
                device_id=to, device_id_type=MESH)

        mine = grad_ref[2 * x + y, c] + landed_ref[2 * x + y, 0]
        out_ref[4 * x + 2 * y + c] = ((mine + got_ref[0]) + got_ref[1]) + got_ref[2]
        first = [copy(0, me, sibling)] + [copy(1 + j, me, (*chip, c)) for j, chip in enumerate(chips)]
        for cp in first:
            cp.start()
        passed = [copy(4 + j, (*chip, c), sibling) for j, chip in enumerate(chips)]
        for j, chip in enumerate(chips):
            copy(1 + j, (*chip, c), me).wait_recv()
            passed[j].start()
        copy(0, sibling, me).wait_recv()
        for j, chip in enumerate(chips):
            copy(4 + j, (*chip, 1 - c), me).wait_recv()
        for cp in first + passed:
            cp.wait_send()

    return pl.pallas_call(
        body, name="gather_small", out_shape=jax.ShapeDtypeStruct((N_DEV, rows, cols), F32),
        in_specs=[VMEM] * 3, out_specs=VMEM,
        scratch_shapes=[pltpu.SemaphoreType.DMA((7,)), pltpu.SemaphoreType.DMA((7,))],
    )(grad, landed, got)


def _adamw_small(params):
    n = len(params)

    def body(*refs):
        ins, outs = refs[:4 * n], refs[4 * n:]
        for i in range(n):
            w_ref, g_ref, m_ref, v_ref = ins[4 * i:4 * i + 4]
            d, m, v = _adamw_math(w_ref[...], g_ref[...], m_ref[...], v_ref[...])
            outs[3 * i][...] = d
            outs[3 * i + 1][...] = m
            outs[3 * i + 2][...] = v

    flat = [a for p in params for a in p]
    out_shape = [jax.ShapeDtypeStruct(p[0].shape, F32) for p in params for _ in range(3)]
    outs = pl.pallas_call(
        body, name="adamw_small", out_shape=out_shape, in_specs=[VMEM] * (4 * n), out_specs=[VMEM] * (3 * n),
    )(*flat)
    return [tuple(outs[3 * i:3 * i + 3]) for i in range(n)]


def _pack_small(parts):
    rows = []
    for a in parts:
        r, c = a.shape
        rows.append(jnp.pad(a, ((0, (-r) % 8), (0, CHUNK - c))))
    flat = jnp.concatenate(rows, axis=0)
    return jnp.pad(flat, ((0, N_DEV * SMALL_ROWS - flat.shape[0]), (0, 0))).reshape(N_DEV, SMALL_ROWS, CHUNK)


def _unpack_small(flat, shapes):
    out, at = [], 0
    for r, c in shapes:
        out.append(flat[at:at + r, :c])
        at += r + (-r) % 8
    return out


def kernel(x, w_in_a, sgu_ln_g, sgu_ln_b, w_spatial, b_spatial, w_out_a, w_kv, w_in_b, attn_sinks, rel_bias, w_out_b, post_ln_g, post_ln_b, loss_target, m_w_in_a, m_sgu_ln_g, m_sgu_ln_b, m_w_spatial, m_b_spatial, m_w_out_a, m_w_kv, m_w_in_b, m_attn_sinks, m_rel_bias, m_w_out_b, m_post_ln_g, m_post_ln_b, v_w_in_a, v_sgu_ln_g, v_sgu_ln_b, v_w_spatial, v_b_spatial, v_w_out_a, v_w_kv, v_w_in_b, v_attn_sinks, v_rel_bias, v_w_out_b, v_post_ln_g, v_post_ln_b):
    seq = x.shape[1]
    me = 4 * lax.axis_index("x") + 2 * lax.axis_index("y") + lax.axis_index("c")
    bucket = jnp.asarray(_band_tables())

    ln_shard = jnp.concatenate([sgu_ln_g, sgu_ln_b], axis=0)
    x2, tgt = x[0], loss_target[0]
    order = jnp.stack([4 * px + 2 * py + pc for px, py, pc in _use_order()]).astype(jnp.int32)
    p, wa, woa, ln_all = _gather_in_proj(order, x2, w_in_a[0], (w_out_a[0],), ln_shard)
    woa = woa.reshape(A_WIDTH, D_MODEL)
    ln_gb = jnp.transpose(ln_all, (1, 0, 2)).reshape(2, A_WIDTH)
    ws = w_spatial[0]
    bsp_t = jnp.pad(b_spatial[0].T, ((0, 0), (0, CHUNK - A_GROUPS)))
    bias = _bias_table(rel_bias, bucket)

    r0, xb, wkv, wb, wob = _fwd_a(x2, p, ln_gb, ws, bsp_t, woa, (w_kv, w_in_b[0], w_out_b[0]))
    wkv = wkv.reshape(D_MODEL, 2 * KV_WIDTH)
    wob = wob.reshape(B_WIDTH, D_MODEL)
    q, z, kv, dr1, loss, dg1, db1 = _fwd_b(r0, tgt, wb, wkv, wob, post_ln_g, post_ln_b, attn_sinks, bias)
    dh1, dwb, dwkv, dwob, dsink, dlog = _bwd_b(r0, dr1, q, z, kv, wb, wkv, wob, post_ln_g, post_ln_b, attn_sinks,
                                               bias)
    drel = _rel_bias_grad(dlog, bucket)

    place = jnp.stack([2 * lax.axis_index("x") + lax.axis_index("y"), lax.axis_index("c")]).astype(jnp.int32)

    def by_device(g):
        return g.reshape((4, 2, g.shape[0] // N_DEV) + g.shape[1:]) if g.ndim == 2 else g.reshape((4, 2) + g.shape[1:])

    dwb, dwkv, dwob = by_device(dwb), by_device(dwkv), by_device(dwob)
    *outs, l_wb, l_wkv, l_wob = _bwd_a(dh1, r0, p, ln_gb, ws, bsp_t, woa, post_ln_g,
                                       _Exchange(pair=(dwb, dwkv, dwob)))
    dp, dr0, dwoa, dws, dbsp_t, dln, dpost0 = outs
    dwoa = by_device(dwoa)
    small_shapes = [(A_GROUPS * CHUNK, CHUNK), (A_GROUPS, CHUNK), (1, N_Q_HEADS), (REL_BUCKETS, N_Q_HEADS),
                    (16, CHUNK), (16, CHUNK), (16, CHUNK), (16, CHUNK), (1, 1)]
    small = by_device(_pack_small([
        dws.reshape(A_GROUPS * CHUNK, CHUNK), dbsp_t[:, :A_GROUPS].T, dsink[:, :N_Q_HEADS], drel[:, :N_Q_HEADS],
        jnp.concatenate([dpost0[0:1], dg1], axis=0).reshape(16, CHUNK),
        jnp.concatenate([dpost0[1:2], db1], axis=0).reshape(16, CHUNK),
        dln[0].reshape(16, CHUNK), dln[1].reshape(16, CHUNK), loss]))
    s_wb, s_wkv, s_wob = (_pair_sum(place, g, l, BF16) for g, l in ((dwb, l_wb), (dwkv, l_wkv), (dwob, l_wob)))
    dwa, l_woa, l_small, g_wb, g_wkv, g_wob = _in_proj_weight_grad(
        xb, dp, wa.shape[2], _Exchange(pair=(dwoa, small), chip=(s_wb, s_wkv, s_wob)))
    dwa = by_device(dwa)
    s_woa, s_small = _pair_sum(place, dwoa, l_woa, BF16), _pair_sum(place, small, l_small, F32)
    _, blocks = _input_grad_blocks(seq)
    cut0, cut1 = blocks // 4, blocks - blocks // 8
    dx, l_wa, g_woa, g_small = _input_grad(dp, dr0, wa, 0, cut0, None,
                                           _Exchange(pair=(dwa,), chip=(s_woa, s_small)))
    s_wa = _pair_sum(place, dwa, l_wa, BF16)
    dx, g_wa = _input_grad(dp, dr0, wa, cut0, cut1 - cut0, dx, _Exchange(chip=(s_wa,)))
    dx, = _input_grad(dp, dr0, wa, cut1, blocks - cut1, dx, _Exchange())
    full = [dwa, dwoa, dwkv, dwb, dwob, small]
    landed = [l_wa, l_woa, l_wkv, l_wb, l_wob, l_small]
    got = [g_wa, g_woa, g_wkv, g_wb, g_wob, g_small]

    big = [(w_in_a, m_w_in_a, v_w_in_a), (w_out_a, m_w_out_a, v_w_out_a), (w_kv, m_w_kv, v_w_kv),
           (w_in_b, m_w_in_b, v_w_in_b), (w_out_b, m_w_out_b, v_w_out_b)]
    big_out = []
    for i, (w, m, v) in enumerate(big):
        shape2 = got[i].shape[1:]
        res = _adamw_shard(place, full[i], landed[i], got[i], w.reshape(shape2), m.reshape(shape2),
                           v.reshape(shape2))
        big_out.append([r.reshape(w.shape) for r in res])

    small_all = _gather_small(full[5], landed[5], got[5]).reshape(N_DEV * SMALL_ROWS, CHUNK)
    g_ws, g_bsp, g_sink, g_rel, g_pg, g_pb, g_lng, g_lnb, loss_all = _unpack_small(small_all, small_shapes)
    g_ws = g_ws.reshape(w_spatial.shape)
    g_bsp = g_bsp.reshape(b_spatial.shape)
    g_pg = g_pg.reshape(post_ln_g.shape)
    g_pb = g_pb.reshape(post_ln_b.shape)
    g_lng = lax.dynamic_slice(g_lng.reshape(1, A_WIDTH), (0, me * sgu_ln_g.shape[1]), sgu_ln_g.shape)
    g_lnb = lax.dynamic_slice(g_lnb.reshape(1, A_WIDTH), (0, me * sgu_ln_b.shape[1]), sgu_ln_b.shape)
    small_params = [(sgu_ln_g, g_lng, m_sgu_ln_g, v_sgu_ln_g), (sgu_ln_b, g_lnb, m_sgu_ln_b, v_sgu_ln_b),
                    (w_spatial, g_ws, m_w_spatial, v_w_spatial), (b_spatial, g_bsp, m_b_spatial, v_b_spatial),
                    (attn_sinks, g_sink, m_attn_sinks, v_attn_sinks), (rel_bias, g_rel, m_rel_bias, v_rel_bias),
                    (post_ln_g, g_pg, m_post_ln_g, v_post_ln_g), (post_ln_b, g_pb, m_post_ln_b, v_post_ln_b)]
    flat2 = [tuple(a.reshape(-1, a.shape[-1]) for a in p) for p in small_params]
    small_out = [tuple(r.reshape(p[0].shape) for r in res) for res, p in zip(_adamw_small(flat2), small_params)]

    order = [("big", 0), ("small", 0), ("small", 1), ("small", 2), ("small", 3), ("big", 1), ("big", 2), ("big", 3),
             ("small", 4), ("small", 5), ("big", 4), ("small", 6), ("small", 7)]
    grads, deltas, new_m, new_v = [], [], [], []
    for kind, i in order:
        if kind == "big":
            g, d, m, v = big_out[i]
        else:
            g = small_params[i][1]
            d, m, v = small_out[i]
        grads.append(g)
        deltas.append(d)
        new_m.append(m)
        new_v.append(v)
    return (loss_all[0, 0], dx.reshape(x.shape), *grads, *deltas, *new_m, *new_v)
```

```python
import functools

import jax
import jax.numpy as jnp
import numpy as np
from jax import lax
from jax.experimental import pallas as pl
from jax.experimental.pallas import tpu as pltpu

F32 = jnp.float32
BF16 = jnp.bfloat16

D_MODEL = 1024
A_WIDTH = 2048
A_GROUPS = 8
A_GROUP_DIM = A_WIDTH // A_GROUPS
CHUNK = 128
HEAD_DIM = 64
N_Q_HEADS = 16
N_KV_HEADS = 2
PAIRS_PER_KV = N_Q_HEADS // N_KV_HEADS // 2
N_PAIRS = N_Q_HEADS // 2
B_WIDTH = N_Q_HEADS * HEAD_DIM
KV_WIDTH = N_KV_HEADS * HEAD_DIM
REL_BUCKETS = 32
REL_MAX_DIST = 128
DEPTH = 2
ALPHA = (2.0 * DEPTH) ** 0.25
LN_EPS = 1e-5
NEG_INF = -1e30
SCALE = HEAD_DIM ** -0.5

ADAM_LR = 0.001
ADAM_B1 = 0.9
ADAM_B2 = 0.999
ADAM_EPS = 1e-08
ADAM_WD = 0.01
ADAM_STEP = 10

N_DEV = 8
MESH = pl.DeviceIdType.MESH
TOKENS_IN_PROJ = 1024
TOKENS_WGRAD = 4096
TOKENS_A = 256
TOKENS_B = 512
PART_ROWS_B = 256
TOKENS_B_BWD = 256
TOKENS_DX = 512
VMEM_LIMIT_BYTES = 60 * 1024 * 1024
SMALL_ROWS = 144

ANY = pl.BlockSpec(memory_space=pl.ANY)
VMEM = pl.BlockSpec(memory_space=pltpu.VMEM)
SMEM = pl.BlockSpec(memory_space=pltpu.SMEM)


def _nn(a, b):
    return jnp.dot(a, b, preferred_element_type=F32)


def _nt(a, b):
    return lax.dot_general(a, b, (((1,), (1,)), ((), ())), preferred_element_type=F32)


def _tn(a, b):
    return lax.dot_general(a, b, (((0,), (0,)), ((), ())), preferred_element_type=F32)


def _resident(shape):
    nd = len(shape)
    return pl.BlockSpec(shape, lambda i: (0,) * nd, pipeline_mode=pl.Buffered(1))


def _rows(tm, width, reverse_of=None):
    if reverse_of is None:
        return pl.BlockSpec((tm, width), lambda i: (i, 0))
    return pl.BlockSpec((tm, width), lambda i: (reverse_of - 1 - i, 0))


def _seq_params():
    return pltpu.CompilerParams(dimension_semantics=("arbitrary",), vmem_limit_bytes=VMEM_LIMIT_BYTES)


def _in_turns(first, second):
    merged = [(i / max(len(first), 1), 0, f) for i, f in enumerate(first)]
    merged += [((i + 0.5) / max(len(second), 1), 1, f) for i, f in enumerate(second)]
    return [f for _, _, f in sorted(merged, key=lambda t: t[:2])]


def _ln_normalize(r):
    mu = jnp.mean(r, axis=-1, keepdims=True)
    xc = r - mu
    var = jnp.mean(xc * xc, axis=-1, keepdims=True)
    rstd = lax.rsqrt(var + LN_EPS)
    return xc * rstd, rstd


def _ln_backward(dxhat, xhat, rstd):
    m1 = jnp.mean(dxhat, axis=-1, keepdims=True)
    m2 = jnp.mean(dxhat * xhat, axis=-1, keepdims=True)
    return rstd * (dxhat - m1 - xhat * m2)


def _silu_and_grad(z):
    sig = jax.nn.sigmoid(z)
    return z * sig, sig * (1.0 + z * (1.0 - sig))


def _band_tables():
    t = np.arange(CHUNK, dtype=np.int32)[:, None]
    j = np.arange(2 * CHUNK, dtype=np.int32)[None, :]
    d = t + CHUNK - j
    in_window = (d >= 0) & (d < CHUNK)
    dc = np.clip(d, 0, REL_MAX_DIST - 1)
    max_exact = REL_BUCKETS // 2
    df = np.maximum(dc, 1).astype(np.float32)
    large = max_exact + (np.log(df / np.float32(max_exact)) / np.float32(np.log(REL_MAX_DIST / max_exact))
                         * np.float32(REL_BUCKETS - max_exact)).astype(np.int32)
    large = np.minimum(large, REL_BUCKETS - 1)
    bucket = np.where(dc < max_exact, dc, large)
    return np.where(in_window, bucket, -1).astype(np.int32)


def _use_order():
    x, y, c = lax.axis_index("x"), lax.axis_index("y"), lax.axis_index("c")
    blocks = [(x, y, c), (x, y, 1 - c)]
    for px, py in [(1 - x, y), (x, 1 - y), (1 - x, 1 - y)]:
        blocks += [(px, py, c), (px, py, 1 - c)]
    return blocks


ARRIVAL_OF_PHASE = (None, 0, 1, 4, 2, 5, 3, 6)


def _gather_in_proj(order, x, w_in_a, small, ln_gb):
    seq = x.shape[0]
    tm = min(TOKENS_IN_PROJ, seq)
    n_blocks = seq // tm
    shard_cols = w_in_a.shape[1]
    n = len(small) + 2

    def body(order_ref, x_ref, *refs):
        ins, refs = refs[:n], refs[n:]
        p_ref, outs, refs = refs[0], refs[1:n + 1], refs[n + 1:]
        wa_vmem, xb_vmem, stage = refs[0], refs[1], refs[2:n]
        send_sems, recv_sems, local_sems, out_sems = refs[n:]
        phase, blk = pl.program_id(0), pl.program_id(1)
        x_, y_, c_ = lax.axis_index("x"), lax.axis_index("y"), lax.axis_index("c")
        me, sibling = (x_, y_, c_), (x_, y_, 1 - c_)
        chips = [(1 - x_, y_), (x_, 1 - y_), (1 - x_, 1 - y_)]
        use = _use_order()

        def index(px, py, pc):
            return 4 * px + 2 * py + pc

        def slot(a, block):
            return (wa_vmem if a == 0 else outs[a]).at[index(*block)]

        def own(a):
            return slot(0, me) if a == 0 else (ins[a] if a == n - 1 else stage[a - 1])

        def copy(a, k, block, to, src=None):
            return pltpu.make_async_remote_copy(
                src_ref=slot(a, block) if src is None else src, dst_ref=slot(a, block),
                send_sem=send_sems.at[a, k], recv_sem=recv_sems.at[a, k], device_id=to, device_id_type=MESH)

        def keep(a):
            return pltpu.make_async_copy(own(a), slot(a, me), local_sems.at[a])

        def write_out(k):
            return pltpu.make_async_copy(slot(0, use[k]), outs[0].at[index(*use[k])], out_sems.at[k])

        def pass_on(a, j):
            copy(a, 1 + j, (*chips[j], c_), me).wait_recv()
            copy(a, 4 + j, (*chips[j], c_), sibling).start()

        def send(a, j):
            copy(a, 1 + j, me, (*chips[j], c_), own(a)).start()

        @pl.when((phase == 0) & (blk == 0))
        def _():
            wa_vmem[index(*me)] = ins[0][...].astype(BF16)
            for a in range(1, n - 1):
                stage[a - 1][...] = ins[a][...].astype(BF16)
            send(0, 0)
            send(0, 1)
            for a in range(n):
                copy(a, 0, me, sibling, own(a)).start()
                if a:
                    keep(a).start()
            write_out(0).start()

        for k in range(1, N_DEV):
            @pl.when((phase == k) & (blk == 0))
            def _(k=k):
                if k % 2 == 0:
                    pass_on(0, k // 2 - 1)
                else:
                    copy(0, ARRIVAL_OF_PHASE[k], use[k], me).wait_recv()
                write_out(k).start()
                if k == 2:
                    send(0, 2)
                    for a in range(1, n):
                        for j in range(3):
                            send(a, j)
                if k == N_DEV - 2:
                    for a in range(1, n):
                        pass_on(a, 0)
                        pass_on(a, 1)

        rows = pl.ds(pl.multiple_of(blk * tm, tm), tm)

        @pl.when(phase == 0)
        def _():
            xb_vmem[rows, :] = x_ref[...].astype(BF16)

        p_ref[...] = _nn(xb_vmem[rows, :], wa_vmem[order_ref[phase]])

        @pl.when((phase == N_DEV - 1) & (blk == n_blocks - 1))
        def _():
            for a in range(1, n):
                pass_on(a, 2)
            for a in range(1, n):
                copy(a, 0, sibling, me).wait_recv()
                for j, chip in enumerate(chips):
                    copy(a, 4 + j, (*chip, 1 - c_), me).wait_recv()
            for a in range(n):
                for k in range(7):
                    copy(a, k, me, me).wait_send()
                if a:
                    keep(a).wait()
            for k in range(N_DEV):
                write_out(k).wait()

    gathered = [jax.ShapeDtypeStruct((N_DEV,) + w_in_a.shape, BF16)]
    gathered += [jax.ShapeDtypeStruct((N_DEV,) + s.shape, BF16) for s in small]
    gathered.append(jax.ShapeDtypeStruct((N_DEV,) + ln_gb.shape, F32))
    return pl.pallas_call(
        body, name="gather_in_proj",
        out_shape=[jax.ShapeDtypeStruct((seq, N_DEV * shard_cols), F32)] + gathered,
        grid_spec=pltpu.PrefetchScalarGridSpec(
            num_scalar_prefetch=1, grid=(N_DEV, n_blocks),
            in_specs=[pl.BlockSpec((tm, D_MODEL), lambda k, i, order: (jnp.where(k == 0, i, n_blocks - 1), 0))]
            + [VMEM] * n,
            out_specs=[pl.BlockSpec((tm, shard_cols), lambda k, i, order: (i, order[k]))] + [ANY] * n,
            scratch_shapes=[pltpu.VMEM((N_DEV,) + w_in_a.shape, BF16), pltpu.VMEM((seq, D_MODEL), BF16)]
            + [pltpu.VMEM(s.shape, BF16) for s in small] + [
                pltpu.SemaphoreType.DMA((n, 7)), pltpu.SemaphoreType.DMA((n, 7)), pltpu.SemaphoreType.DMA((n,)),
                pltpu.SemaphoreType.DMA((N_DEV,))]),
        compiler_params=pltpu.CompilerParams(dimension_semantics=("arbitrary", "arbitrary"),
                                             vmem_limit_bytes=VMEM_LIMIT_BYTES),
    )(order, x, w_in_a, *small, ln_gb)


def _bias_table(rel_bias, bucket):
    def body(rel_ref, bucket_ref, out_ref):
        bk = bucket_ref[...]
        col = lax.broadcasted_iota(jnp.int32, bk.shape, 1)
        for h in range(N_Q_HEADS):
            def step(b, acc):
                return jnp.where(bk == b, rel_ref[b, h], acc)
            tab = lax.fori_loop(0, REL_BUCKETS, step, jnp.full(bk.shape, NEG_INF, F32))
            pair = h // 2
            rows = slice((pair % PAIRS_PER_KV) * CHUNK, (pair % PAIRS_PER_KV + 1) * CHUNK)
            half = slice((h % 2) * 2 * CHUNK, (h % 2 + 1) * 2 * CHUNK)
            out_ref[1, pair // PAIRS_PER_KV, rows, half] = tab
            out_ref[0, pair // PAIRS_PER_KV, rows, half] = jnp.where(col < CHUNK, NEG_INF, tab)

    return pl.pallas_call(
        body, name="bias_table",
        out_shape=jax.ShapeDtypeStruct((2, N_KV_HEADS, PAIRS_PER_KV * CHUNK, 4 * CHUNK), F32),
        in_specs=[SMEM, VMEM], out_specs=VMEM,
    )(rel_bias, bucket)


def _rel_bias_grad(dlogits, bucket):
    def body(dl_ref, bucket_ref, out_ref):
        bk = bucket_ref[...]
        row = lax.broadcasted_iota(jnp.int32, (REL_BUCKETS, CHUNK), 0)
        lane = lax.broadcasted_iota(jnp.int32, (REL_BUCKETS, CHUNK), 1)

        def step(b, acc):
            for h in range(N_Q_HEADS):
                pair = h // 2
                dl = dl_ref[pair // PAIRS_PER_KV, (pair % PAIRS_PER_KV) * CHUNK:(pair % PAIRS_PER_KV + 1) * CHUNK,
                            (h % 2) * 2 * CHUNK:(h % 2 + 1) * 2 * CHUNK]
                s = jnp.sum(jnp.where(bk == b, dl, 0.0))
                acc = jnp.where((row == b) & (lane == h), s, acc)
            return acc

        out_ref[...] = lax.fori_loop(0, REL_BUCKETS, step, jnp.zeros((REL_BUCKETS, CHUNK), F32))

    return pl.pallas_call(
        body, name="rel_bias_grad", out_shape=jax.ShapeDtypeStruct((REL_BUCKETS, CHUNK), F32),
        in_specs=[VMEM, VMEM], out_specs=VMEM,
    )(dlogits, bucket)


def _masked_spatial(ws_ref, g):
    t = lax.broadcasted_iota(jnp.int32, (CHUNK, CHUNK), 0)
    s = lax.broadcasted_iota(jnp.int32, (CHUNK, CHUNK), 1)
    return jnp.where(s <= t, ws_ref[g], 0.0).astype(BF16)


def _spatial_gate(ws_ref, bsp_t_ref, vn, s_ref, n_chunks):
    for g in range(A_GROUPS):
        w = _masked_spatial(ws_ref, g)
        cols = slice(g * A_GROUP_DIM, (g + 1) * A_GROUP_DIM)
        for ci in range(n_chunks):
            rows = slice(ci * CHUNK, (ci + 1) * CHUNK)
            s_ref[rows, cols] = _nn(w, vn[rows, cols]) + bsp_t_ref[:, g:g + 1]


def _fwd_a(x, p, ln_gb, w_spatial, bsp_t, woa, later):
    seq = x.shape[0]
    tm = min(TOKENS_A, seq)
    n_chunks = tm // CHUNK
    n_blocks = seq // tm
    n = len(later)

    def body(x_ref, p_ref, ln_ref, ws_ref, bsp_ref, woa_ref, *refs):
        shards, refs = refs[:n], refs[n:]
        r0_ref, xb_ref, gathered, refs = refs[0], refs[1], refs[2:2 + n], refs[2 + n:]
        s_ref, stage, (send_sems, recv_sems, local_sems) = refs[0], refs[1:1 + n], refs[1 + n:]
        step = pl.program_id(0)

        def copies(a):
            x_, y_, c_ = lax.axis_index("x"), lax.axis_index("y"), lax.axis_index("c")
            mine = 4 * x_ + 2 * y_ + c_
            out = [pltpu.make_async_copy(stage[a], gathered[a].at[mine], local_sems.at[a])]
            for r in range(1, N_DEV):
                to = (1 - x_ if r & 4 else x_, 1 - y_ if r & 2 else y_, 1 - c_ if r & 1 else c_)
                out.append(pltpu.make_async_remote_copy(
                    src_ref=stage[a], dst_ref=gathered[a].at[mine], send_sem=send_sems.at[a, r - 1],
                    recv_sem=recv_sems.at[a, r - 1], device_id=to, device_id_type=MESH))
            return out

        @pl.when(step == 0)
        def _():
            for a in range(n):
                stage[a][...] = shards[a][...].astype(BF16)
                for cp in copies(a):
                    cp.start()

        xv = x_ref[...]
        xb_ref[...] = xv.astype(BF16)
        vhat, _ = _ln_normalize(p_ref[:, A_WIDTH:2 * A_WIDTH])
        vn = (vhat * ln_ref[0:1, :] + ln_ref[1:2, :]).astype(BF16)
        _spatial_gate(ws_ref, bsp_ref, vn, s_ref, n_chunks)
        z = p_ref[:, 2 * A_WIDTH:]
        y = p_ref[:, :A_WIDTH] * s_ref[...] * (z * jax.nn.sigmoid(z))
        r0_ref[...] = ALPHA * xv + _nn(y.astype(BF16), woa_ref[...])

        @pl.when(step == n_blocks - 1)
        def _():
            for a in range(n):
                for cp in copies(a):
                    cp.wait()

    return pl.pallas_call(
        body, name="fwd_a", grid=(n_blocks,),
        out_shape=[jax.ShapeDtypeStruct((seq, D_MODEL), F32), jax.ShapeDtypeStruct((seq, D_MODEL), BF16)]
        + [jax.ShapeDtypeStruct((N_DEV,) + s.shape, BF16) for s in later],
        in_specs=[_rows(tm, D_MODEL), _rows(tm, 3 * A_WIDTH), _resident(ln_gb.shape), _resident(w_spatial.shape),
                  _resident(bsp_t.shape), _resident(woa.shape)] + [VMEM] * n,
        out_specs=[_rows(tm, D_MODEL), _rows(tm, D_MODEL)] + [ANY] * n,
        scratch_shapes=[pltpu.VMEM((tm, A_WIDTH), F32)] + [pltpu.VMEM(s.shape, BF16) for s in later] + [
            pltpu.SemaphoreType.DMA((n, N_DEV - 1)), pltpu.SemaphoreType.DMA((n, N_DEV - 1)),
            pltpu.SemaphoreType.DMA((n,))],
        compiler_params=_seq_params(),
    )(x, p, ln_gb, w_spatial, bsp_t, woa, *later)


def _expand_heads(kv):
    lo = lax.broadcasted_iota(jnp.int32, kv.shape, 1) < HEAD_DIM
    swapped = pltpu.roll(kv, HEAD_DIM, 1)
    zero = jnp.zeros_like(kv)
    return (jnp.where(lo, kv, zero).astype(BF16), jnp.where(lo, zero, swapped).astype(BF16),
            jnp.where(lo, swapped, zero).astype(BF16), jnp.where(lo, zero, kv).astype(BF16))


def _fold_heads(acc, kvh):
    top, bot = acc[:2 * CHUNK], acc[2 * CHUNK:]
    lo = lax.broadcasted_iota(jnp.int32, top.shape, 1) < HEAD_DIM
    t = jnp.where(lo, top, 0.0)
    b = jnp.where(lo, 0.0, bot)
    if kvh == 0:
        return t + pltpu.roll(b, HEAD_DIM, 1)
    return pltpu.roll(t, HEAD_DIM, 1) + b


STACK = PAIRS_PER_KV * CHUNK


def _pair_cols(kvh, pp):
    pair = kvh * PAIRS_PER_KV + pp
    return slice(pair * CHUNK, (pair + 1) * CHUNK)


def _stack_pairs(ref_or_val, rows, kvh):
    return jnp.concatenate([ref_or_val[rows, _pair_cols(kvh, pp)] for pp in range(PAIRS_PER_KV)], axis=0)


def _sink_columns(sink_ref):
    blk = lax.broadcasted_iota(jnp.int32, (STACK, 1), 0) // CHUNK
    out = []
    for kvh in range(N_KV_HEADS):
        halves = []
        for half in range(2):
            col = jnp.zeros((STACK, 1), F32)
            for pp in range(PAIRS_PER_KV):
                col = jnp.where(blk == pp, sink_ref[0, 2 * (kvh * PAIRS_PER_KV + pp) + half], col)
            halves.append(col)
        out.append(halves)
    return out


def _softmax_stack(logits, sink_cols):
    ps, sinks = [], []
    for half in range(2):
        lg = logits[:, half * 2 * CHUNK:(half + 1) * 2 * CHUNK]
        sk = sink_cols[half]
        m = jnp.maximum(jnp.max(lg, axis=-1, keepdims=True), sk)
        e = jnp.exp(lg - m)
        es = jnp.exp(sk - m)
        inv = 1.0 / (jnp.sum(e, axis=-1, keepdims=True) + es)
        ps.append(e * inv)
        sinks.append(es * inv)
    return jnp.concatenate(ps, axis=1), sinks


def _band_operands(before_k, this_k, before_v, this_v, kvh):
    kblk = jnp.concatenate([before_k[2 * kvh], this_k[2 * kvh], before_k[2 * kvh + 1], this_k[2 * kvh + 1]], axis=0)
    vblk = jnp.concatenate([before_v[2 * kvh], this_v[2 * kvh], before_v[2 * kvh + 1], this_v[2 * kvh + 1]], axis=0)
    return kblk, vblk


def _fwd_b(r0, target, wb, wkv, wob, post_g, post_b, sinks, bias):
    seq = r0.shape[0]
    tm = min(TOKENS_B, seq)
    n_chunks = tm // CHUNK
    part_rows = min(PART_ROWS_B, tm)
    n_parts = tm // part_rows
    part_chunks = part_rows // CHUNK
    shard_cols = wb.shape[2]

    def body(r0_ref, tgt_ref, wb_ref, wkv_ref, wob_ref, pg_ref, pb_ref, sink_ref, bias_ref,
             q_ref, z_ref, kv_ref, dr1_ref, loss_ref, dg_ref, db_ref, ek_last, ev_last, *scratch):
        q_s, z_s, h1_s, o_s, ek_s, ev_s = (scratch[i * n_parts:(i + 1) * n_parts] for i in range(6))
        step = pl.program_id(0)

        @pl.when(step == 0)
        def _():
            ek_last[...] = jnp.zeros_like(ek_last)
            ev_last[...] = jnp.zeros_like(ev_last)
            loss_ref[...] = jnp.zeros_like(loss_ref)
            dg_ref[...] = jnp.zeros_like(dg_ref)
            db_ref[...] = jnp.zeros_like(db_ref)

        sink_cols = _sink_columns(sink_ref)
        g1 = pg_ref[1:2, :]

        def project_steps(part):
            rows = slice(part * part_rows, (part + 1) * part_rows)
            live = {}

            def normalize():
                xhat0, _ = _ln_normalize(r0_ref[rows, :])
                h1 = xhat0 * pg_ref[0:1, :] + pb_ref[0:1, :]
                h1_s[part][...] = h1
                live["h1b"] = h1.astype(BF16)

            def shard(k):
                def run():
                    qz = _nn(live["h1b"], wb_ref[k])
                    lo, hi = k * shard_cols, (k + 1) * shard_cols
                    if hi <= B_WIDTH:
                        qs = (qz * SCALE).astype(BF16)
                        q_ref[rows, lo:hi] = qs
                        q_s[part][:, lo:hi] = qs
                    else:
                        z_ref[rows, lo - B_WIDTH:hi - B_WIDTH] = qz
                        z_s[part][:, lo - B_WIDTH:hi - B_WIDTH] = qz
                return run

            def keys():
                kv = _nn(live["h1b"], wkv_ref[...])
                kv_ref[rows, :] = kv.astype(BF16)
                for cj in range(part_chunks):
                    sub = slice(cj * CHUNK, (cj + 1) * CHUNK)
                    for i, e in enumerate(_expand_heads(kv[sub, :KV_WIDTH])):
                        ek_s[part][cj, i] = e
                    for i, e in enumerate(_expand_heads(kv[sub, KV_WIDTH:])):
                        ev_s[part][cj, i] = e

            return [normalize, keys] + [shard(k) for k in range(N_DEV)]

        def attend_steps(part):
            def stack(cj, kvh):
                def run():
                    sub = slice(cj * CHUNK, (cj + 1) * CHUNK)
                    sel = jnp.where(step * n_chunks + part * part_chunks + cj > 0, 1, 0)
                    if cj:
                        before_k, before_v = ek_s[part].at[cj - 1], ev_s[part].at[cj - 1]
                    elif part:
                        before_k, before_v = ek_s[part - 1].at[part_chunks - 1], ev_s[part - 1].at[part_chunks - 1]
                    else:
                        before_k, before_v = ek_last, ev_last
                    kblk, vblk = _band_operands(before_k, ek_s[part].at[cj], before_v, ev_s[part].at[cj], kvh)
                    logits = _nt(_stack_pairs(q_s[part], sub, kvh), kblk) + bias_ref[sel, kvh]
                    p, _ = _softmax_stack(logits, sink_cols[kvh])
                    o = _nn(p.astype(BF16), vblk)
                    for pp in range(PAIRS_PER_KV):
                        o_s[part][sub, _pair_cols(kvh, pp)] = o[pp * CHUNK:(pp + 1) * CHUNK]
                return run

            def finish():
                rows = slice(part * part_rows, (part + 1) * part_rows)
                z = z_s[part][...]
                y = o_s[part][...] * (z * jax.nn.sigmoid(z))
                r1 = ALPHA * h1_s[part][...] + _nn(y.astype(BF16), wob_ref[...])
                xhat1, rstd1 = _ln_normalize(r1)
                diff = xhat1 * g1 + pb_ref[1:2, :] - tgt_ref[rows, :]
                loss_ref[...] += 0.5 * jnp.sum(jnp.mean(diff * diff, axis=-1, keepdims=True), axis=0, keepdims=True)
                dh2 = diff * (1.0 / D_MODEL)
                dg_ref[...] += jnp.sum(dh2 * xhat1, axis=0, keepdims=True)
                db_ref[...] += jnp.sum(dh2, axis=0, keepdims=True)
                dr1_ref[rows, :] = _ln_backward(dh2 * g1, xhat1, rstd1)

            return [stack(cj, kvh) for cj in range(part_chunks) for kvh in range(N_KV_HEADS)] + [finish]

        for run in project_steps(0):
            run()
        for part in range(n_parts):
            for run in _in_turns(attend_steps(part), project_steps(part + 1) if part + 1 < n_parts else []):
                run()
        ek_last[...] = ek_s[n_parts - 1][part_chunks - 1]
        ev_last[...] = ev_s[n_parts - 1][part_chunks - 1]

    vec = pl.BlockSpec((1, D_MODEL), lambda i: (0, 0))
    return pl.pallas_call(
        body, name="fwd_b", grid=(seq // tm,),
        out_shape=[jax.ShapeDtypeStruct((seq, B_WIDTH), BF16), jax.ShapeDtypeStruct((seq, B_WIDTH), F32),
                   jax.ShapeDtypeStruct((seq, 2 * KV_WIDTH), BF16), jax.ShapeDtypeStruct((seq, D_MODEL), F32),
                   jax.ShapeDtypeStruct((1, 1), F32), jax.ShapeDtypeStruct((1, D_MODEL), F32),
                   jax.ShapeDtypeStruct((1, D_MODEL), F32)],
        in_specs=[_rows(tm, D_MODEL), _rows(tm, D_MODEL), _resident(wb.shape), _resident(wkv.shape),
                  _resident(wob.shape), _resident(post_g.shape), _resident(post_b.shape), SMEM,
                  _resident(bias.shape)],
        out_specs=[_rows(tm, B_WIDTH), _rows(tm, B_WIDTH), _rows(tm, 2 * KV_WIDTH), _rows(tm, D_MODEL),
                   pl.BlockSpec((1, 1), lambda i: (0, 0)), vec, vec],
        scratch_shapes=[pltpu.VMEM((4, CHUNK, CHUNK), BF16), pltpu.VMEM((4, CHUNK, CHUNK), BF16)]
        + [pltpu.VMEM((part_rows, B_WIDTH), BF16)] * n_parts + [pltpu.VMEM((part_rows, B_WIDTH), F32)] * n_parts
        + [pltpu.VMEM((part_rows, D_MODEL), F32)] * n_parts + [pltpu.VMEM((part_rows, B_WIDTH), F32)] * n_parts
        + [pltpu.VMEM((part_chunks, 4, CHUNK, CHUNK), BF16)] * (2 * n_parts),
        compiler_params=_seq_params(),
    )(r0, target, wb, wkv, wob, post_g, post_b, sinks, bias)


def _bwd_b(r0, dr1, q, z, kv, wb, wkv, wob, post_g, post_b, sinks, bias):
    seq = r0.shape[0]
    tm = min(TOKENS_B_BWD, seq)
    n_chunks = tm // CHUNK
    n_blocks = seq // tm
    shard_cols = wb.shape[2]

    def body(r0_ref, dr1_ref, q_ref, z_ref, kv_ref, kvp_ref, wb_ref, wkv_ref, wob_ref, pg_ref, pb_ref, sink_ref,
             bias_ref, dh1_ref, dwb_out, dwkv_out, dwob_out, dsink_ref, dl_out,
             dwb_acc, dwkv_acc, dwob_acc, dl_acc, dsink_acc, carry_ref, ek_ref, ev_ref, y_ref, dqz_ref, dkv_ref):
        step = pl.program_id(0)
        block = n_blocks - 1 - step

        @pl.when(step == 0)
        def _():
            dwb_acc[...] = jnp.zeros_like(dwb_acc)
            dwkv_acc[...] = jnp.zeros_like(dwkv_acc)
            dwob_acc[...] = jnp.zeros_like(dwob_acc)
            dl_acc[...] = jnp.zeros_like(dl_acc)
            carry_ref[...] = jnp.zeros_like(carry_ref)
            dsink_acc[...] = jnp.zeros_like(dsink_acc)

        xhat0, _ = _ln_normalize(r0_ref[...])
        h1b = (xhat0 * pg_ref[0:1, :] + pb_ref[0:1, :]).astype(BF16)
        dr1 = dr1_ref[...]
        dr1b = dr1.astype(BF16)
        dy = _nt(dr1b, wob_ref[...])
        sg, dsg = _silu_and_grad(z_ref[...])
        do_all = dy * sg
        dyz = dy * dsg

        for ci in range(n_chunks + 1):
            src = kvp_ref[...] if ci == 0 else kv_ref[(ci - 1) * CHUNK:ci * CHUNK, :]
            src = src.astype(F32)
            for i, e in enumerate(_expand_heads(src[:, :KV_WIDTH])):
                ek_ref[ci, i] = e
            for i, e in enumerate(_expand_heads(src[:, KV_WIDTH:])):
                ev_ref[ci, i] = e

        sink_cols = _sink_columns(sink_ref)
        for ci in reversed(range(n_chunks)):
            rows = slice(ci * CHUNK, (ci + 1) * CHUNK)
            sel = jnp.where(block * n_chunks + ci > 0, 1, 0)
            dk_band = jnp.zeros((2 * CHUNK, KV_WIDTH), F32)
            dv_band = jnp.zeros((2 * CHUNK, KV_WIDTH), F32)
            for kvh in range(N_KV_HEADS):
                kblk, vblk = _band_operands(ek_ref.at[ci], ek_ref.at[ci + 1], ev_ref.at[ci], ev_ref.at[ci + 1], kvh)
                qs = _stack_pairs(q_ref, rows, kvh)
                p, psink = _softmax_stack(_nt(qs, kblk) + bias_ref[sel, kvh], sink_cols[kvh])
                pb16 = p.astype(BF16)
                o = _nn(pb16, vblk)
                y = (o * _stack_pairs(sg, rows, kvh)).astype(BF16)
                dz = (o * _stack_pairs(dyz, rows, kvh)).astype(BF16)
                dob = _stack_pairs(do_all, rows, kvh).astype(BF16)
                dp = _nt(dob, vblk)
                dlogits = []
                for half in range(2):
                    hs = slice(half * 2 * CHUNK, (half + 1) * 2 * CHUNK)
                    delta = jnp.sum(dp[:, hs] * p[:, hs], axis=-1, keepdims=True)
                    dlogits.append(p[:, hs] * (dp[:, hs] - delta))
                    dsink_acc[2 * kvh + half] += psink[half] * delta
                dlog = jnp.concatenate(dlogits, axis=1)
                dl_acc[kvh] += dlog
                dsc = dlog.astype(BF16)
                dq = (_nn(dsc, kblk) * SCALE).astype(BF16)
                for pp in range(PAIRS_PER_KV):
                    blk = slice(pp * CHUNK, (pp + 1) * CHUNK)
                    cols = _pair_cols(kvh, pp)
                    y_ref[rows, cols] = y[blk]
                    dqz_ref[rows, cols] = dq[blk]
                    dqz_ref[rows, B_WIDTH + cols.start:B_WIDTH + cols.stop] = dz[blk]
                dk_band = dk_band + _fold_heads(_tn(dsc, qs), kvh)
                dv_band = dv_band + _fold_heads(_tn(pb16, dob), kvh)
            band = jnp.concatenate([dk_band, dv_band], axis=1)
            dkv_ref[rows, :] = (band[CHUNK:] + carry_ref[...]).astype(BF16)
            carry_ref[...] = band[:CHUNK]

        dqz = dqz_ref[...]
        dkv = dkv_ref[...]
        dh1 = ALPHA * dr1 + _nt(dkv, wkv_ref[...])
        for k in range(N_DEV):
            part = dqz[:, k * shard_cols:(k + 1) * shard_cols]
            dh1 = dh1 + _nt(part, wb_ref[k])
            dwb_acc[k] += _tn(h1b, part)
        dh1_ref[...] = dh1
        dwkv_acc[...] += _tn(h1b, dkv)
        dwob_acc[...] += _tn(y_ref[...], dr1b)

        @pl.when(step == n_blocks - 1)
        def _():
            pltpu.sync_copy(dwb_acc, dwb_out)
            pltpu.sync_copy(dwkv_acc, dwkv_out)
            pltpu.sync_copy(dwob_acc, dwob_out)
            pltpu.sync_copy(dl_acc, dl_out)
            lane = lax.broadcasted_iota(jnp.int32, (1, CHUNK), 1)
            dsink = jnp.zeros((1, CHUNK), F32)
            for kvh in range(N_KV_HEADS):
                for half in range(2):
                    for pp in range(PAIRS_PER_KV):
                        head = 2 * (kvh * PAIRS_PER_KV + pp) + half
                        tot = jnp.sum(dsink_acc[2 * kvh + half, pp * CHUNK:(pp + 1) * CHUNK, :], axis=0, keepdims=True)
                        dsink = dsink - jnp.where(lane == head, tot, 0.0)
            dsink_ref[...] = dsink

    rev = functools.partial(_rows, tm, reverse_of=n_blocks)
    prev_chunk = pl.BlockSpec(
        (CHUNK, 2 * KV_WIDTH), lambda i: (jnp.maximum((n_blocks - 1 - i) * n_chunks - 1, 0), 0))
    return pl.pallas_call(
        body, name="bwd_b", grid=(n_blocks,),
        out_shape=[jax.ShapeDtypeStruct((seq, D_MODEL), F32), jax.ShapeDtypeStruct(wb.shape, F32),
                   jax.ShapeDtypeStruct(wkv.shape, F32), jax.ShapeDtypeStruct(wob.shape, F32),
                   jax.ShapeDtypeStruct((1, CHUNK), F32),
                   jax.ShapeDtypeStruct((N_KV_HEADS, STACK, 4 * CHUNK), F32)],
        in_specs=[rev(D_MODEL), rev(D_MODEL), rev(B_WIDTH), rev(B_WIDTH), rev(2 * KV_WIDTH), prev_chunk,
                  _resident(wb.shape), _resident(wkv.shape), _resident(wob.shape), _resident(post_g.shape),
                  _resident(post_b.shape), SMEM, _resident(bias.shape)],
        out_specs=[rev(D_MODEL), ANY, ANY, ANY, pl.BlockSpec((1, CHUNK), lambda i: (0, 0)), ANY],
        scratch_shapes=[pltpu.VMEM(wb.shape, F32), pltpu.VMEM(wkv.shape, F32), pltpu.VMEM(wob.shape, F32),
                        pltpu.VMEM((N_KV_HEADS, STACK, 4 * CHUNK), F32), pltpu.VMEM((2 * N_KV_HEADS, STACK, 1), F32),
                        pltpu.VMEM((CHUNK, 2 * KV_WIDTH), F32),
                        pltpu.VMEM((n_chunks + 1, 4, CHUNK, CHUNK), BF16),
                        pltpu.VMEM((n_chunks + 1, 4, CHUNK, CHUNK), BF16),
                        pltpu.VMEM((tm, B_WIDTH), BF16), pltpu.VMEM((tm, 2 * B_WIDTH), BF16),
                        pltpu.VMEM((tm, 2 * KV_WIDTH), BF16)],
        compiler_params=_seq_params(),
    )(r0, dr1, q, z, kv, kv, wb, wkv, wob, post_g, post_b, sinks, bias)


def _bwd_a(dh1, r0, p, ln_gb, w_spatial, bsp_t, woa, post_g, exchange):
    seq = r0.shape[0]
    tm = min(TOKENS_A, seq)
    n_chunks = tm // CHUNK
    n_blocks = seq // tm

    def body(dh1_ref, r0_ref, p_ref, ln_ref, ws_ref, bsp_ref, woa_ref, pg_ref, *refs):
        ex_in, refs = refs[:exchange.n], refs[exchange.n:]
        (dp_ref, dr0_ref, dwoa_out, dws_ref, dbsp_ref, dln_ref, dpost_ref), refs = refs[:7], refs[7:]
        ex_out, refs = refs[:exchange.n], refs[exchange.n:]
        (dwoa_acc, s_ref, ds_ref, dvn_ref), ex_sems = refs[:4], refs[4:]
        step = pl.program_id(0)

        @pl.when(step == 0)
        def _():
            exchange.start(ex_in, ex_out, ex_sems)
            dwoa_acc[...] = jnp.zeros_like(dwoa_acc)
            dws_ref[...] = jnp.zeros_like(dws_ref)
            dbsp_ref[...] = jnp.zeros_like(dbsp_ref)
            dln_ref[...] = jnp.zeros_like(dln_ref)
            dpost_ref[...] = jnp.zeros_like(dpost_ref)

        dh1 = dh1_ref[...]
        xhat0, rstd0 = _ln_normalize(r0_ref[...])
        dpost_ref[0:1, :] += jnp.sum(dh1 * xhat0, axis=0, keepdims=True)
        dpost_ref[1:2, :] += jnp.sum(dh1, axis=0, keepdims=True)
        dr0 = _ln_backward(dh1 * pg_ref[0:1, :], xhat0, rstd0)
        dr0_ref[...] = dr0
        dr0b = dr0.astype(BF16)

        vhat, rstd_v = _ln_normalize(p_ref[:, A_WIDTH:2 * A_WIDTH])
        ln_g = ln_ref[0:1, :]
        vn = (vhat * ln_g + ln_ref[1:2, :]).astype(BF16)
        _spatial_gate(ws_ref, bsp_ref, vn, s_ref, n_chunks)
        u = p_ref[:, :A_WIDTH]
        s = s_ref[...]
        sg, dsg = _silu_and_grad(p_ref[:, 2 * A_WIDTH:])
        dwoa_acc[...] += _tn((u * s * sg).astype(BF16), dr0b)
        dy = _nt(dr0b, woa_ref[...])
        dp_ref[:, :A_WIDTH] = (dy * s * sg).astype(BF16)
        dp_ref[:, 2 * A_WIDTH:] = (dy * u * s * dsg).astype(BF16)
        ds_ref[...] = dy * u * sg

        lane = lax.broadcasted_iota(jnp.int32, (CHUNK, CHUNK), 1)
        dbsp = jnp.zeros((CHUNK, CHUNK), F32)
        for g in range(A_GROUPS):
            w = _masked_spatial(ws_ref, g)
            cols = slice(g * A_GROUP_DIM, (g + 1) * A_GROUP_DIM)
            dws = jnp.zeros((CHUNK, CHUNK), F32)
            for ci in range(n_chunks):
                rows = slice(ci * CHUNK, (ci + 1) * CHUNK)
                ds = ds_ref[rows, cols]
                dsb = ds.astype(BF16)
                dvn_ref[rows, cols] = _tn(w, dsb)
                dws = dws + _nt(dsb, vn[rows, cols])
                dbsp = dbsp + jnp.where(lane == g, jnp.sum(ds, axis=-1, keepdims=True), 0.0)
            dws_ref[g] += dws
        dbsp_ref[...] += dbsp

        dvn = dvn_ref[...]
        dln_ref[0:1, :] += jnp.sum(dvn * vhat, axis=0, keepdims=True)
        dln_ref[1:2, :] += jnp.sum(dvn, axis=0, keepdims=True)
        dp_ref[:, A_WIDTH:2 * A_WIDTH] = _ln_backward(dvn * ln_g, vhat, rstd_v).astype(BF16)

        @pl.when(step == n_blocks - 1)
        def _():
            t = lax.broadcasted_iota(jnp.int32, (CHUNK, CHUNK), 0)
            for g in range(A_GROUPS):
                dws_ref[g] = jnp.where(lane <= t, dws_ref[g], 0.0)
            pltpu.sync_copy(dwoa_acc, dwoa_out)
            exchange.wait(ex_in, ex_out, ex_sems)

    def fixed(shape):
        nd = len(shape)
        return pl.BlockSpec(shape, lambda i: (0,) * nd)

    return pl.pallas_call(
        body, name="bwd_a", grid=(n_blocks,),
        out_shape=[jax.ShapeDtypeStruct((seq, 3 * A_WIDTH), BF16), jax.ShapeDtypeStruct((seq, D_MODEL), F32),
                   jax.ShapeDtypeStruct(woa.shape, F32), jax.ShapeDtypeStruct(w_spatial.shape, F32),
                   jax.ShapeDtypeStruct((CHUNK, CHUNK), F32), jax.ShapeDtypeStruct((2, A_WIDTH), F32),
                   jax.ShapeDtypeStruct((2, D_MODEL), F32)] + exchange.out_shape,
        in_specs=[_rows(tm, D_MODEL), _rows(tm, D_MODEL), _rows(tm, 3 * A_WIDTH), _resident(ln_gb.shape),
                  _resident(w_spatial.shape), _resident(bsp_t.shape), _resident(woa.shape),
                  _resident(post_g.shape)] + [ANY] * exchange.n,
        out_specs=[_rows(tm, 3 * A_WIDTH), _rows(tm, D_MODEL), ANY, fixed(w_spatial.shape),
                   fixed((CHUNK, CHUNK)), fixed((2, A_WIDTH)), fixed((2, D_MODEL))] + [ANY] * exchange.n,
        scratch_shapes=[pltpu.VMEM(woa.shape, F32), pltpu.VMEM((tm, A_WIDTH), F32),
                        pltpu.VMEM((tm, A_WIDTH), F32), pltpu.VMEM((tm, A_WIDTH), F32)] + exchange.scratch_shapes,
        compiler_params=_seq_params(),
    )(dh1, r0, p, ln_gb, w_spatial, bsp_t, woa, post_g, *exchange.inputs)


def _input_grad_blocks(seq):
    tm = min(TOKENS_DX, seq // 8)
    return tm, seq // tm


def _input_grad(dp, dr0, wa, first, n_blocks, earlier, exchange):
    seq = dr0.shape[0]
    tm, _ = _input_grad_blocks(seq)
    shard_cols = wa.shape[2]
    n_own = 3 if earlier is None else 4

    def body(dp_ref, dr0_ref, wa_ref, *refs):
        refs = refs[n_own - 3:]
        ex_in, dx_ref, ex_out, ex_sems = (refs[:exchange.n], refs[exchange.n], refs[exchange.n + 1:2 * exchange.n + 1],
                                          refs[2 * exchange.n + 1:])
        step = pl.program_id(0)

        @pl.when(step == 0)
        def _():
            exchange.start(ex_in, ex_out, ex_sems)

        dx = ALPHA * dr0_ref[...]
        for k in range(N_DEV):
            dx = dx + _nt(dp_ref[:, k * shard_cols:(k + 1) * shard_cols], wa_ref[k])
        dx_ref[...] = dx

        @pl.when(step == n_blocks - 1)
        def _():
            exchange.wait(ex_in, ex_out, ex_sems)

    def rows(width):
        return pl.BlockSpec((tm, width), lambda i: (first + i, 0))

    return pl.pallas_call(
        body, name=f"input_grad_{first}", grid=(n_blocks,),
        out_shape=[jax.ShapeDtypeStruct((seq, D_MODEL), F32)] + exchange.out_shape,
        in_specs=[rows(3 * A_WIDTH), rows(D_MODEL), _resident(wa.shape)] + [ANY] * (n_own - 3 + exchange.n),
        out_specs=[rows(D_MODEL)] + [ANY] * exchange.n,
        scratch_shapes=exchange.scratch_shapes,
        input_output_aliases={} if earlier is None else {3: 0},
        compiler_params=_seq_params(),
    )(dp, dr0, wa, *(() if earlier is None else (earlier,)), *exchange.inputs)


def _in_proj_weight_grad(xb, dp, shard_cols, exchange):
    seq = xb.shape[0]
    tk = min(TOKENS_WGRAD, seq)
    n_k = seq // tk

    def body(x_ref, dp_ref, *refs):
        ex_in, out_ref, ex_out, ex_sems = (refs[:exchange.n], refs[exchange.n], refs[exchange.n + 1:2 * exchange.n + 1],
                                           refs[2 * exchange.n + 1:])
        shard, step = pl.program_id(0), pl.program_id(1)

        @pl.when((shard == 0) & (step == 0))
        def _():
            exchange.start(ex_in, ex_out, ex_sems)

        @pl.when(step == 0)
        def _():
            out_ref[...] = jnp.zeros_like(out_ref)

        out_ref[...] += _tn(x_ref[...], dp_ref[...])

        @pl.when((shard == N_DEV - 1) & (step == n_k - 1))
        def _():
            exchange.wait(ex_in, ex_out, ex_sems)

    return pl.pallas_call(
        body, name="in_proj_weight_grad", grid=(N_DEV, n_k),
        out_shape=[jax.ShapeDtypeStruct((N_DEV, D_MODEL, shard_cols), F32)] + exchange.out_shape,
        in_specs=[pl.BlockSpec((tk, D_MODEL), lambda j, k: (k, 0)),
                  pl.BlockSpec((tk, shard_cols), lambda j, k: (k, j))] + [ANY] * exchange.n,
        out_specs=[pl.BlockSpec((None, D_MODEL, shard_cols), lambda j, k: (j, 0, 0))] + [ANY] * exchange.n,
        scratch_shapes=exchange.scratch_shapes,
        compiler_params=pltpu.CompilerParams(dimension_semantics=("arbitrary", "arbitrary"),
                                             vmem_limit_bytes=VMEM_LIMIT_BYTES),
    )(xb, dp, *exchange.inputs)


class _Exchange:
    def __init__(self, pair=(), chip=()):
        self.pair, self.chip = tuple(pair), tuple(chip)
        self.inputs = self.pair + self.chip
        self.n = len(self.inputs)
        self.out_shape = [jax.ShapeDtypeStruct((4, 1) + g.shape[2:], F32) for g in self.pair]
        self.out_shape += [jax.ShapeDtypeStruct((3,) + s.shape[1:], s.dtype) for s in self.chip]
        self.scratch_shapes = [pltpu.SemaphoreType.DMA((max(len(self.pair), 1),)) for _ in range(2)]
        self.scratch_shapes += [pltpu.SemaphoreType.DMA((max(len(self.chip), 1), 3)) for _ in range(2)]

    def _copies(self, ins, outs, sems):
        if not self.n:
            return []
        pair_send, pair_recv, chip_send, chip_recv = sems
        x, y, c = lax.axis_index("x"), lax.axis_index("y"), lax.axis_index("c")
        copies = []
        for a in range(len(self.pair)):
            copies.append(pltpu.make_async_remote_copy(
                src_ref=ins[a].at[:, pl.ds(1 - c, 1)], dst_ref=outs[a],
                send_sem=pair_send.at[a], recv_sem=pair_recv.at[a], device_id=(x, y, 1 - c), device_id_type=MESH))
        for a in range(len(self.chip)):
            src, dst = ins[len(self.pair) + a], outs[len(self.pair) + a]
            for k, (px, py) in enumerate([(1 - x, y), (x, 1 - y), (1 - x, 1 - y)]):
                copies.append(pltpu.make_async_remote_copy(
                    src_ref=src.at[2 * px + py], dst_ref=dst.at[k], send_sem=chip_send.at[a, k],
                    recv_sem=chip_recv.at[a, k], device_id=(px, py, c), device_id_type=MESH))
        return copies

    def start(self, ins, outs, sems):
        for cp in self._copies(ins, outs, sems):
            cp.start()

    def wait(self, ins, outs, sems):
        for cp in self._copies(ins, outs, sems):
            cp.wait()


def _row_tile(rows, cols, itemsize=4):
    if rows * cols * itemsize <= (1 << 20):
        return rows
    t = 8
    while 2 * t * cols * itemsize <= (1 << 20) and rows % (2 * t) == 0:
        t *= 2
    assert rows % t == 0, (rows, cols)
    return t


def _pair_sum(place, grad, landed, dtype):
    _, _, rows, cols = grad.shape
    tr = _row_tile(rows, cols)

    def body(place_ref, a_ref, b_ref, out_ref):
        out_ref[...] = (a_ref[...] + b_ref[...]).astype(dtype)

    return pl.pallas_call(
        body, name="pair_sum", out_shape=jax.ShapeDtypeStruct((4, rows, cols), dtype),
        grid_spec=pltpu.PrefetchScalarGridSpec(
            num_scalar_prefetch=1, grid=(4, rows // tr),
            in_specs=[pl.BlockSpec((None, None, tr, cols), lambda k, i, pr: (k, pr[1], i, 0)),
                      pl.BlockSpec((None, None, tr, cols), lambda k, i, pr: (k, 0, i, 0))],
            out_specs=pl.BlockSpec((None, tr, cols), lambda k, i, pr: (k, i, 0))),
        compiler_params=pltpu.CompilerParams(dimension_semantics=("parallel", "parallel")),
    )(place, grad, landed)


def _adamw_math(w, g, m, v):
    m = ADAM_B1 * m + (1.0 - ADAM_B1) * g
    v = ADAM_B2 * v + (1.0 - ADAM_B2) * (g * g)
    m_hat = m / (1.0 - ADAM_B1 ** ADAM_STEP)
    v_hat = v / (1.0 - ADAM_B2 ** ADAM_STEP)
    delta = -ADAM_LR * (m_hat / (jnp.sqrt(v_hat) + ADAM_EPS) + ADAM_WD * w)
    return delta, m, v


def _adamw_shard(place, grad, landed, got, w, m, v):
    rows, cols = w.shape
    tr = _row_tile(rows, cols)

    def body(place_ref, a_ref, b_ref, got_ref, w_ref, m_ref, v_ref, g_out, d_out, m_out, v_out):
        g = a_ref[...] + b_ref[...]
        for k in range(3):
            g = g + got_ref[k].astype(F32)
        g_out[...] = g
        d_out[...], m_out[...], v_out[...] = _adamw_math(w_ref[...], g, m_ref[...], v_ref[...])

    spec = pl.BlockSpec((tr, cols), lambda i, pr: (i, 0))
    out = jax.ShapeDtypeStruct((rows, cols), F32)
    return pl.pallas_call(
        body, name="adamw_shard", out_shape=[out] * 4,
        grid_spec=pltpu.PrefetchScalarGridSpec(
            num_scalar_prefetch=1, grid=(rows // tr,),
            in_specs=[pl.BlockSpec((None, None, tr, cols), lambda i, pr: (pr[0], pr[1], i, 0)),
                      pl.BlockSpec((None, None, tr, cols), lambda i, pr: (pr[0], 0, i, 0)),
                      pl.BlockSpec((3, tr, cols), lambda i, pr: (0, i, 0)), spec, spec, spec],
            out_specs=[spec] * 4),
        compiler_params=pltpu.CompilerParams(dimension_semantics=("parallel",)),
    )(place, grad, landed, got, w, m, v)


def _gather_small(grad, landed, got):
    rows, cols = got.shape[1:]

    def body(grad_ref, landed_ref, got_ref, out_ref, send_sems, recv_sems):
        x, y, c = lax.axis_index("x"), lax.axis_index("y"), lax.axis_index("c")
        me, sibling = (x, y, c), (x, y, 1 - c)
        chips = [(1 - x, y), (x, 1 - y), (1 - x, 1 - y)]

        def slot(px, py, pc):
            return out_ref.at[4 * px + 2 * py + pc]

        def copy(k, block, to):
            return pltpu.make_async_remote_copy(
                src_ref=slot(*block), dst_ref=slot(*block), send_sem=send_sems.at[k], recv_sem=recv_sems.at[k],
                device_id=to, device_id_type=MESH)

        mine = grad_ref[2 * x + y, c] + landed_ref[2 * x + y, 0]
        out_ref[4 * x + 2 * y + c] = ((mine + got_ref[0]) + got_ref[1]) + got_ref[2]
        first = [copy(0, me, sibling)] + [copy(1 + j, me, (*chip, c)) for j, chip in enumerate(chips)]
        for cp in first:
            cp.start()
        passed = [copy(4 + j, (*chip, c), sibling) for j, chip in enumerate(chips)]
        for j, chip in enumerate(chips):
            copy(1 + j, (*chip, c), me).wait_recv()
            passed[j].start()
        copy(0, sibling, me).wait_recv()
        for j, chip in enumerate(chips):
            copy(4 + j, (*chip, 1 - c), me).wait_recv()
        for cp in first + passed:
            cp.wait_send()

    return pl.pallas_call(
        body, name="gather_small", out_shape=jax.ShapeDtypeStruct((N_DEV, rows, cols), F32),
        in_specs=[VMEM] * 3, out_specs=VMEM,
        scratch_shapes=[pltpu.SemaphoreType.DMA((7,)), pltpu.SemaphoreType.DMA((7,))],
    )(grad, landed, got)


def _adamw_small(params):
    n = len(params)

    def body(*refs):
        ins, outs = refs[:4 * n], refs[4 * n:]
        for i in range(n):
            w_ref, g_ref, m_ref, v_ref = ins[4 * i:4 * i + 4]
            d, m, v = _adamw_math(w_ref[...], g_ref[...], m_ref[...], v_ref[...])
            outs[3 * i][...] = d
            outs[3 * i + 1][...] = m
            outs[3 * i + 2][...] = v

    flat = [a for p in params for a in p]
    out_shape = [jax.ShapeDtypeStruct(p[0].shape, F32) for p in params for _ in range(3)]
    outs = pl.pallas_call(
        body, name="adamw_small", out_shape=out_shape, in_specs=[VMEM] * (4 * n), out_specs=[VMEM] * (3 * n),
    )(*flat)
    return [tuple(outs[3 * i:3 * i + 3]) for i in range(n)]


def _pack_small(parts):
    rows = []
    for a in parts:
        r, c = a.shape
        rows.append(jnp.pad(a, ((0, (-r) % 8), (0, CHUNK - c))))
    flat = jnp.concatenate(rows, axis=0)
    return jnp.pad(flat, ((0, N_DEV * SMALL_ROWS - flat.shape[0]), (0, 0))).reshape(N_DEV, SMALL_ROWS, CHUNK)


def _unpack_small(flat, shapes):
    out, at = [], 0
    for r, c in shapes:
        out.append(flat[at:at + r, :c])
        at += r + (-r) % 8
    return out


def kernel(x, w_in_a, sgu_ln_g, sgu_ln_b, w_spatial, b_spatial, w_out_a, w_kv, w_in_b, attn_sinks, rel_bias, w_out_b, post_ln_g, post_ln_b, loss_target, m_w_in_a, m_sgu_ln_g, m_sgu_ln_b, m_w_spatial, m_b_spatial, m_w_out_a, m_w_kv, m_w_in_b, m_attn_sinks, m_rel_bias, m_w_out_b, m_post_ln_g, m_post_ln_b, v_w_in_a, v_sgu_ln_g, v_sgu_ln_b, v_w_spatial, v_b_spatial, v_w_out_a, v_w_kv, v_w_in_b, v_attn_sinks, v_rel_bias, v_w_out_b, v_post_ln_g, v_post_ln_b):
    seq = x.shape[1]
    me = 4 * lax.axis_index("x") + 2 * lax.axis_index("y") + lax.axis_index("c")
    bucket = jnp.asarray(_band_tables())

    ln_shard = jnp.concatenate([sgu_ln_g, sgu_ln_b], axis=0)
    x2, tgt = x[0], loss_target[0]
    order = jnp.stack([4 * px + 2 * py + pc for px, py, pc in _use_order()]).astype(jnp.int32)
    p, wa, woa, ln_all = _gather_in_proj(order, x2, w_in_a[0], (w_out_a[0],), ln_shard)
    woa = woa.reshape(A_WIDTH, D_MODEL)
    ln_gb = jnp.transpose(ln_all, (1, 0, 2)).reshape(2, A_WIDTH)
    ws = w_spatial[0]
    bsp_t = jnp.pad(b_spatial[0].T, ((0, 0), (0, CHUNK - A_GROUPS)))
    bias = _bias_table(rel_bias, bucket)

    r0, xb, wkv, wb, wob = _fwd_a(x2, p, ln_gb, ws, bsp_t, woa, (w_kv, w_in_b[0], w_out_b[0]))
    wkv = wkv.reshape(D_MODEL, 2 * KV_WIDTH)
    wob = wob.reshape(B_WIDTH, D_MODEL)
    q, z, kv, dr1, loss, dg1, db1 = _fwd_b(r0, tgt, wb, wkv, wob, post_ln_g, post_ln_b, attn_sinks, bias)
    dh1, dwb, dwkv, dwob, dsink, dlog = _bwd_b(r0, dr1, q, z, kv, wb, wkv, wob, post_ln_g, post_ln_b, attn_sinks,
                                               bias)
    drel = _rel_bias_grad(dlog, bucket)

    place = jnp.stack([2 * lax.axis_index("x") + lax.axis_index("y"), lax.axis_index("c")]).astype(jnp.int32)

    def by_device(g):
        return g.reshape((4, 2, g.shape[0] // N_DEV) + g.shape[1:]) if g.ndim == 2 else g.reshape((4, 2) + g.shape[1:])

    dwb, dwkv, dwob = by_device(dwb), by_device(dwkv), by_device(dwob)
    *outs, l_wb, l_wkv, l_wob = _bwd_a(dh1, r0, p, ln_gb, ws, bsp_t, woa, post_ln_g,
                                       _Exchange(pair=(dwb, dwkv, dwob)))
    dp, dr0, dwoa, dws, dbsp_t, dln, dpost0 = outs
    dwoa = by_device(dwoa)
    small_shapes = [(A_GROUPS * CHUNK, CHUNK), (A_GROUPS, CHUNK), (1, N_Q_HEADS), (REL_BUCKETS, N_Q_HEADS),
                    (16, CHUNK), (16, CHUNK), (16, CHUNK), (16, CHUNK), (1, 1)]
    small = by_device(_pack_small([
        dws.reshape(A_GROUPS * CHUNK, CHUNK), dbsp_t[:, :A_GROUPS].T, dsink[:, :N_Q_HEADS], drel[:, :N_Q_HEADS],
        jnp.concatenate([dpost0[0:1], dg1], axis=0).reshape(16, CHUNK),
        jnp.concatenate([dpost0[1:2], db1], axis=0).reshape(16, CHUNK),
        dln[0].reshape(16, CHUNK), dln[1].reshape(16, CHUNK), loss]))
    s_wb, s_wkv, s_wob = (_pair_sum(place, g, l, BF16) for g, l in ((dwb, l_wb), (dwkv, l_wkv), (dwob, l_wob)))
    dwa, l_woa, l_small, g_wb, g_wkv, g_wob = _in_proj_weight_grad(
        xb, dp, wa.shape[2], _Exchange(pair=(dwoa, small), chip=(s_wb, s_wkv, s_wob)))
    dwa = by_device(dwa)
    s_woa, s_small = _pair_sum(place, dwoa, l_woa, BF16), _pair_sum(place, small, l_small, F32)
    _, blocks = _input_grad_blocks(seq)
    cut0, cut1 = blocks // 4, blocks - blocks // 8
    dx, l_wa, g_woa, g_small = _input_grad(dp, dr0, wa, 0, cut0, None,
                                           _Exchange(pair=(dwa,), chip=(s_woa, s_small)))
    s_wa = _pair_sum(place, dwa, l_wa, BF16)
    dx, g_wa = _input_grad(dp, dr0, wa, cut0, cut1 - cut0, dx, _Exchange(chip=(s_wa,)))
    dx, = _input_grad(dp, dr0, wa, cut1, blocks - cut1, dx, _Exchange())
    full = [dwa, dwoa, dwkv, dwb, dwob, small]
    landed = [l_wa, l_woa, l_wkv, l_wb, l_wob, l_small]
    got = [g_wa, g_woa, g_wkv, g_wb, g_wob, g_small]

    big = [(w_in_a, m_w_in_a, v_w_in_a), (w_out_a, m_w_out_a, v_w_out_a), (w_kv, m_w_kv, v_w_kv),
           (w_in_b, m_w_in_b, v_w_in_b), (w_out_b, m_w_out_b, v_w_out_b)]
    big_out = []
    for i, (w, m, v) in enumerate(big):
        shape2 = got[i].shape[1:]
        res = _adamw_shard(place, full[i], landed[i], got[i], w.reshape(shape2), m.reshape(shape2),
                           v.reshape(shape2))
        big_out.append([r.reshape(w.shape) for r in res])

    small_all = _gather_small(full[5], landed[5], got[5]).reshape(N_DEV * SMALL_ROWS, CHUNK)
    g_ws, g_bsp, g_sink, g_rel, g_pg, g_pb, g_lng, g_lnb, loss_all = _unpack_small(small_all, small_shapes)
    g_ws = g_ws.reshape(w_spatial.shape)
    g_bsp = g_bsp.reshape(b_spatial.shape)
    g_pg = g_pg.reshape(post_ln_g.shape)
    g_pb = g_pb.reshape(post_ln_b.shape)
    g_lng = lax.dynamic_slice(g_lng.reshape(1, A_WIDTH), (0, me * sgu_ln_g.shape[1]), sgu_ln_g.shape)
    g_lnb = lax.dynamic_slice(g_lnb.reshape(1, A_WIDTH), (0, me * sgu_ln_b.shape[1]), sgu_ln_b.shape)
    small_params = [(sgu_ln_g, g_lng, m_sgu_ln_g, v_sgu_ln_g), (sgu_ln_b, g_lnb, m_sgu_ln_b, v_sgu_ln_b),
                    (w_spatial, g_ws, m_w_spatial, v_w_spatial), (b_spatial, g_bsp, m_b_spatial, v_b_spatial),
                    (attn_sinks, g_sink, m_attn_sinks, v_attn_sinks), (rel_bias, g_rel, m_rel_bias, v_rel_bias),
                    (post_ln_g, g_pg, m_post_ln_g, v_post_ln_g), (post_ln_b, g_pb, m_post_ln_b, v_post_ln_b)]
    flat2 = [tuple(a.reshape(-1, a.shape[-1]) for a in p) for p in small_params]
    small_out = [tuple(r.reshape(p[0].shape) for r in res) for res, p in zip(_adamw_small(flat2), small_params)]

    order = [("big", 0), ("small", 0), ("small", 1), ("small", 2), ("small", 3), ("big", 1), ("big", 2), ("big", 3),
             ("small", 4), ("small", 5), ("big", 4), ("small", 6), ("small", 7)]
    grads, deltas, new_m, new_v = [], [], [], []
    for kind, i in order:
        if kind == "big":
            g, d, m, v = big_out[i]
        else:
            g = small_params[i][1]
            d, m, v = small_out[i]
        grads.append(g)
        deltas.append(d)
        new_m.append(m)
        new_v.append(v)
    return (loss_all[0, 0], dx.reshape(x.shape), *grads, *deltas, *new_m, *new_v)
```

```python
import functools

import jax
import jax.numpy as jnp
import numpy as np
from jax import lax
from jax.experimental import pallas as pl
from jax.experimental.pallas import tpu as pltpu

F32 = jnp.float32
BF16 = jnp.bfloat16

D_MODEL = 1024
A_WIDTH = 2048
A_GROUPS = 8
A_GROUP_DIM = A_WIDTH // A_GROUPS
CHUNK = 128
HEAD_DIM = 64
N_Q_HEADS = 16
N_KV_HEADS = 2
PAIRS_PER_KV = N_Q_HEADS // N_KV_HEADS // 2
N_PAIRS = N_Q_HEADS // 2
B_WIDTH = N_Q_HEADS * HEAD_DIM
KV_WIDTH = N_KV_HEADS * HEAD_DIM
REL_BUCKETS = 32
REL_MAX_DIST = 128
DEPTH = 2
ALPHA = (2.0 * DEPTH) ** 0.25
LN_EPS = 1e-5
NEG_INF = -1e30
SCALE = HEAD_DIM ** -0.5

ADAM_LR = 0.001
ADAM_B1 = 0.9
ADAM_B2 = 0.999
ADAM_EPS = 1e-08
ADAM_WD = 0.01
ADAM_STEP = 10

N_DEV = 8
MESH = pl.DeviceIdType.MESH
TOKENS_IN_PROJ = 1024
TOKENS_WGRAD = 4096
TOKENS_A = 256
TOKENS_A_FWD = 512
TOKENS_B = 512
PART_ROWS_B = 256
TOKENS_B_BWD = 256
TOKENS_DX = 512
VMEM_LIMIT_BYTES = 60 * 1024 * 1024
SMALL_ROWS = 144

ANY = pl.BlockSpec(memory_space=pl.ANY)
VMEM = pl.BlockSpec(memory_space=pltpu.VMEM)
SMEM = pl.BlockSpec(memory_space=pltpu.SMEM)


def _nn(a, b):
    return jnp.dot(a, b, preferred_element_type=F32)


def _nt(a, b):
    return lax.dot_general(a, b, (((1,), (1,)), ((), ())), preferred_element_type=F32)


def _tn(a, b):
    return lax.dot_general(a, b, (((0,), (0,)), ((), ())), preferred_element_type=F32)


def _resident(shape):
    nd = len(shape)
    return pl.BlockSpec(shape, lambda i: (0,) * nd, pipeline_mode=pl.Buffered(1))


def _rows(tm, width, reverse_of=None):
    if reverse_of is None:
        return pl.BlockSpec((tm, width), lambda i: (i, 0))
    return pl.BlockSpec((tm, width), lambda i: (reverse_of - 1 - i, 0))


def _seq_params():
    return pltpu.CompilerParams(dimension_semantics=("arbitrary",), vmem_limit_bytes=VMEM_LIMIT_BYTES)


def _in_turns(first, second):
    merged = [(i / max(len(first), 1), 0, f) for i, f in enumerate(first)]
    merged += [((i + 0.5) / max(len(second), 1), 1, f) for i, f in enumerate(second)]
    return [f for _, _, f in sorted(merged, key=lambda t: t[:2])]


def _ln_normalize(r):
    mu = jnp.mean(r, axis=-1, keepdims=True)
    xc = r - mu
    var = jnp.mean(xc * xc, axis=-1, keepdims=True)
    rstd = lax.rsqrt(var + LN_EPS)
    return xc * rstd, rstd


def _ln_backward(dxhat, xhat, rstd):
    m1 = jnp.mean(dxhat, axis=-1, keepdims=True)
    m2 = jnp.mean(dxhat * xhat, axis=-1, keepdims=True)
    return rstd * (dxhat - m1 - xhat * m2)


def _silu_and_grad(z):
    sig = jax.nn.sigmoid(z)
    return z * sig, sig * (1.0 + z * (1.0 - sig))


def _band_tables():
    t = np.arange(CHUNK, dtype=np.int32)[:, None]
    j = np.arange(2 * CHUNK, dtype=np.int32)[None, :]
    d = t + CHUNK - j
    in_window = (d >= 0) & (d < CHUNK)
    dc = np.clip(d, 0, REL_MAX_DIST - 1)
    max_exact = REL_BUCKETS // 2
    df = np.maximum(dc, 1).astype(np.float32)
    large = max_exact + (np.log(df / np.float32(max_exact)) / np.float32(np.log(REL_MAX_DIST / max_exact))
                         * np.float32(REL_BUCKETS - max_exact)).astype(np.int32)
    large = np.minimum(large, REL_BUCKETS - 1)
    bucket = np.where(dc < max_exact, dc, large)
    return np.where(in_window, bucket, -1).astype(np.int32)


def _use_order():
    x, y, c = lax.axis_index("x"), lax.axis_index("y"), lax.axis_index("c")
    blocks = [(x, y, c), (x, y, 1 - c)]
    for px, py in [(1 - x, y), (x, 1 - y), (1 - x, 1 - y)]:
        blocks += [(px, py, c), (px, py, 1 - c)]
    return blocks


def _gather_in_proj(order, x, w_in_a, small, ln_gb):
    seq = x.shape[0]
    tm = min(TOKENS_IN_PROJ, seq)
    n_blocks = seq // tm
    shard_cols = w_in_a.shape[1]
    n = len(small) + 2

    def body(order_ref, x_ref, *refs):
        ins, refs = refs[:n], refs[n:]
        p_ref, outs, refs = refs[0], refs[1:n + 1], refs[n + 1:]
        wa_vmem, xb_vmem, stage = refs[0], refs[1], refs[2:n]
        send_sems, recv_sems, local_sems, out_sems = refs[n:]
        phase, blk = pl.program_id(0), pl.program_id(1)
        x_, y_, c_ = lax.axis_index("x"), lax.axis_index("y"), lax.axis_index("c")
        me, sibling = (x_, y_, c_), (x_, y_, 1 - c_)
        chips = [(1 - x_, y_), (x_, 1 - y_), (1 - x_, 1 - y_)]
        use = _use_order()

        def index(px, py, pc):
            return 4 * px + 2 * py + pc

        def slot(a, block):
            return (wa_vmem if a == 0 else outs[a]).at[index(*block)]

        def own(a):
            return slot(0, me) if a == 0 else (ins[a] if a == n - 1 else stage[a - 1])

        def copy(a, k, block, to, src=None):
            return pltpu.make_async_remote_copy(
                src_ref=slot(a, block) if src is None else src, dst_ref=slot(a, block),
                send_sem=send_sems.at[a, k], recv_sem=recv_sems.at[a, k], device_id=to, device_id_type=MESH)

        def keep(a):
            return pltpu.make_async_copy(own(a), slot(a, me), local_sems.at[a])

        def write_out(k):
            return pltpu.make_async_copy(slot(0, use[k]), outs[0].at[index(*use[k])], out_sems.at[k])

        def pass_on(a, j):
            copy(a, 1 + j, (*chips[j], c_), me).wait_recv()
            copy(a, 4 + j, (*chips[j], c_), sibling).start()

        def send(a, j):
            copy(a, 1 + j, me, (*chips[j], c_), own(a)).start()

        @pl.when((phase == 0) & (blk == 0))
        def _():
            wa_vmem[index(*me)] = ins[0][...].astype(BF16)
            for a in range(1, n - 1):
                stage[a - 1][...] = ins[a][...].astype(BF16)
            send(0, 0)
            send(0, 1)
            for a in range(n):
                copy(a, 0, me, sibling, own(a)).start()
                if a:
                    keep(a).start()
            write_out(0).start()
            copy(0, 0, sibling, me).wait_recv()
            write_out(1).start()

        for j in range(3):
            @pl.when((phase == 1 + j) & (blk == 0))
            def _(j=j):
                pass_on(0, j)
                write_out(2 + 2 * j).start()
                if j == 0:
                    send(0, 2)
                    for a in range(1, n):
                        for jj in range(3):
                            send(a, jj)
                copy(0, 4 + j, (*chips[j], 1 - c_), me).wait_recv()
                write_out(3 + 2 * j).start()
                if j == 2:
                    for a in range(1, n):
                        pass_on(a, 0)
                        pass_on(a, 1)

        rows = pl.ds(pl.multiple_of(blk * tm, tm), tm)

        @pl.when(phase == 0)
        def _():
            xb_vmem[rows, :] = x_ref[...].astype(BF16)

        chip = order_ref[phase]
        xb = xb_vmem[rows, :]
        p_ref[:, :shard_cols] = _nn(xb, wa_vmem[2 * chip])
        p_ref[:, shard_cols:] = _nn(xb, wa_vmem[2 * chip + 1])

        @pl.when((phase == N_DEV // 2 - 1) & (blk == n_blocks - 1))
        def _():
            for a in range(1, n):
                pass_on(a, 2)
            for a in range(1, n):
                copy(a, 0, sibling, me).wait_recv()
                for j, chip in enumerate(chips):
                    copy(a, 4 + j, (*chip, 1 - c_), me).wait_recv()
            for a in range(n):
                for k in range(7):
                    copy(a, k, me, me).wait_send()
                if a:
                    keep(a).wait()
            for k in range(N_DEV):
                write_out(k).wait()

    gathered = [jax.ShapeDtypeStruct((N_DEV,) + w_in_a.shape, BF16)]
    gathered += [jax.ShapeDtypeStruct((N_DEV,) + s.shape, BF16) for s in small]
    gathered.append(jax.ShapeDtypeStruct((N_DEV,) + ln_gb.shape, F32))
    return pl.pallas_call(
        body, name="gather_in_proj",
        out_shape=[jax.ShapeDtypeStruct((seq, N_DEV * shard_cols), F32)] + gathered,
        grid_spec=pltpu.PrefetchScalarGridSpec(
            num_scalar_prefetch=1, grid=(N_DEV // 2, n_blocks),
            in_specs=[pl.BlockSpec((tm, D_MODEL), lambda k, i, order: (jnp.where(k == 0, i, n_blocks - 1), 0))]
            + [VMEM] * n,
            out_specs=[pl.BlockSpec((tm, 2 * shard_cols), lambda k, i, order: (i, order[k]))] + [ANY] * n,
            scratch_shapes=[pltpu.VMEM((N_DEV,) + w_in_a.shape, BF16), pltpu.VMEM((seq, D_MODEL), BF16)]
            + [pltpu.VMEM(s.shape, BF16) for s in small] + [
                pltpu.SemaphoreType.DMA((n, 7)), pltpu.SemaphoreType.DMA((n, 7)), pltpu.SemaphoreType.DMA((n,)),
                pltpu.SemaphoreType.DMA((N_DEV,))]),
        compiler_params=pltpu.CompilerParams(dimension_semantics=("arbitrary", "arbitrary"),
                                             vmem_limit_bytes=VMEM_LIMIT_BYTES),
    )(order, x, w_in_a, *small, ln_gb)


def _bias_table(rel_bias, bucket):
    def body(rel_ref, bucket_ref, out_ref):
        bk = bucket_ref[...]
        col = lax.broadcasted_iota(jnp.int32, bk.shape, 1)
        for h in range(N_Q_HEADS):
            def step(b, acc):
                return jnp.where(bk == b, rel_ref[b, h], acc)
            tab = lax.fori_loop(0, REL_BUCKETS, step, jnp.full(bk.shape, NEG_INF, F32))
            pair = h // 2
            rows = slice((pair % PAIRS_PER_KV) * CHUNK, (pair % PAIRS_PER_KV + 1) * CHUNK)
            half = slice((h % 2) * 2 * CHUNK, (h % 2 + 1) * 2 * CHUNK)
            out_ref[1, pair // PAIRS_PER_KV, rows, half] = tab
            out_ref[0, pair // PAIRS_PER_KV, rows, half] = jnp.where(col < CHUNK, NEG_INF, tab)

    return pl.pallas_call(
        body, name="bias_table",
        out_shape=jax.ShapeDtypeStruct((2, N_KV_HEADS, PAIRS_PER_KV * CHUNK, 4 * CHUNK), F32),
        in_specs=[SMEM, VMEM], out_specs=VMEM,
    )(rel_bias, bucket)


def _rel_bias_grad(dlogits, bucket):
    def body(dl_ref, bucket_ref, out_ref):
        bk = bucket_ref[...]
        row = lax.broadcasted_iota(jnp.int32, (REL_BUCKETS, CHUNK), 0)
        lane = lax.broadcasted_iota(jnp.int32, (REL_BUCKETS, CHUNK), 1)

        def step(b, acc):
            for h in range(N_Q_HEADS):
                pair = h // 2
                dl = dl_ref[pair // PAIRS_PER_KV, (pair % PAIRS_PER_KV) * CHUNK:(pair % PAIRS_PER_KV + 1) * CHUNK,
                            (h % 2) * 2 * CHUNK:(h % 2 + 1) * 2 * CHUNK]
                s = jnp.sum(jnp.where(bk == b, dl, 0.0))
                acc = jnp.where((row == b) & (lane == h), s, acc)
            return acc

        out_ref[...] = lax.fori_loop(0, REL_BUCKETS, step, jnp.zeros((REL_BUCKETS, CHUNK), F32))

    return pl.pallas_call(
        body, name="rel_bias_grad", out_shape=jax.ShapeDtypeStruct((REL_BUCKETS, CHUNK), F32),
        in_specs=[VMEM, VMEM], out_specs=VMEM,
    )(dlogits, bucket)


def _masked_spatial(ws_ref, g):
    t = lax.broadcasted_iota(jnp.int32, (CHUNK, CHUNK), 0)
    s = lax.broadcasted_iota(jnp.int32, (CHUNK, CHUNK), 1)
    return jnp.where(s <= t, ws_ref[g], 0.0).astype(BF16)


def _spatial_gate(ws_ref, bsp_t_ref, vn, s_ref, n_chunks):
    for g in range(A_GROUPS):
        w = _masked_spatial(ws_ref, g)
        cols = slice(g * A_GROUP_DIM, (g + 1) * A_GROUP_DIM)
        for ci in range(n_chunks):
            rows = slice(ci * CHUNK, (ci + 1) * CHUNK)
            s_ref[rows, cols] = _nn(w, vn[rows, cols]) + bsp_t_ref[:, g:g + 1]


def _fwd_a(x, p, ln_gb, w_spatial, bsp_t, woa, later):
    seq = x.shape[0]
    tm = min(TOKENS_A_FWD, seq)
    n_chunks = tm // CHUNK
    n_blocks = seq // tm
    n = len(later)

    def body(x_ref, p_ref, ln_ref, ws_ref, bsp_ref, woa_ref, *refs):
        shards, refs = refs[:n], refs[n:]
        r0_ref, xb_ref, gathered, refs = refs[0], refs[1], refs[2:2 + n], refs[2 + n:]
        s_ref, stage, (send_sems, recv_sems, local_sems) = refs[0], refs[1:1 + n], refs[1 + n:]
        step = pl.program_id(0)

        def copies(a):
            x_, y_, c_ = lax.axis_index("x"), lax.axis_index("y"), lax.axis_index("c")
            mine = 4 * x_ + 2 * y_ + c_
            out = [pltpu.make_async_copy(stage[a], gathered[a].at[mine], local_sems.at[a])]
            for r in range(1, N_DEV):
                to = (1 - x_ if r & 4 else x_, 1 - y_ if r & 2 else y_, 1 - c_ if r & 1 else c_)
                out.append(pltpu.make_async_remote_copy(
                    src_ref=stage[a], dst_ref=gathered[a].at[mine], send_sem=send_sems.at[a, r - 1],
                    recv_sem=recv_sems.at[a, r - 1], device_id=to, device_id_type=MESH))
            return out

        @pl.when(step == 0)
        def _():
            for a in range(n):
                stage[a][...] = shards[a][...].astype(BF16)
                for cp in copies(a):
                    cp.start()

        xv = x_ref[...]
        xb_ref[...] = xv.astype(BF16)
        vhat, _ = _ln_normalize(p_ref[:, A_WIDTH:2 * A_WIDTH])
        vn = (vhat * ln_ref[0:1, :] + ln_ref[1:2, :]).astype(BF16)
        _spatial_gate(ws_ref, bsp_ref, vn, s_ref, n_chunks)
        z = p_ref[:, 2 * A_WIDTH:]
        y = p_ref[:, :A_WIDTH] * s_ref[...] * (z * jax.nn.sigmoid(z))
        r0_ref[...] = ALPHA * xv + _nn(y.astype(BF16), woa_ref[...])

        @pl.when(step == n_blocks - 1)
        def _():
            for a in range(n):
                for cp in copies(a):
                    cp.wait()

    return pl.pallas_call(
        body, name="fwd_a", grid=(n_blocks,),
        out_shape=[jax.ShapeDtypeStruct((seq, D_MODEL), F32), jax.ShapeDtypeStruct((seq, D_MODEL), BF16)]
        + [jax.ShapeDtypeStruct((N_DEV,) + s.shape, BF16) for s in later],
        in_specs=[_rows(tm, D_MODEL), _rows(tm, 3 * A_WIDTH), _resident(ln_gb.shape), _resident(w_spatial.shape),
                  _resident(bsp_t.shape), _resident(woa.shape)] + [VMEM] * n,
        out_specs=[_rows(tm, D_MODEL), _rows(tm, D_MODEL)] + [ANY] * n,
        scratch_shapes=[pltpu.VMEM((tm, A_WIDTH), F32)] + [pltpu.VMEM(s.shape, BF16) for s in later] + [
            pltpu.SemaphoreType.DMA((n, N_DEV - 1)), pltpu.SemaphoreType.DMA((n, N_DEV - 1)),
            pltpu.SemaphoreType.DMA((n,))],
        compiler_params=_seq_params(),
    )(x, p, ln_gb, w_spatial, bsp_t, woa, *later)


def _expand_heads(kv):
    lo = lax.broadcasted_iota(jnp.int32, kv.shape, 1) < HEAD_DIM
    swapped = pltpu.roll(kv, HEAD_DIM, 1)
    zero = jnp.zeros_like(kv)
    return (jnp.where(lo, kv, zero).astype(BF16), jnp.where(lo, zero, swapped).astype(BF16),
            jnp.where(lo, swapped, zero).astype(BF16), jnp.where(lo, zero, kv).astype(BF16))


def _fold_heads(acc, kvh):
    top, bot = acc[:2 * CHUNK], acc[2 * CHUNK:]
    lo = lax.broadcasted_iota(jnp.int32, top.shape, 1) < HEAD_DIM
    t = jnp.where(lo, top, 0.0)
    b = jnp.where(lo, 0.0, bot)
    if kvh == 0:
        return t + pltpu.roll(b, HEAD_DIM, 1)
    return pltpu.roll(t, HEAD_DIM, 1) + b


STACK = PAIRS_PER_KV * CHUNK


def _pair_cols(kvh, pp):
    pair = kvh * PAIRS_PER_KV + pp
    return slice(pair * CHUNK, (pair + 1) * CHUNK)


def _stack_pairs(ref_or_val, rows, kvh):
    return jnp.concatenate([ref_or_val[rows, _pair_cols(kvh, pp)] for pp in range(PAIRS_PER_KV)], axis=0)


def _sink_columns(sink_ref):
    blk = lax.broadcasted_iota(jnp.int32, (STACK, 1), 0) // CHUNK
    out = []
    for kvh in range(N_KV_HEADS):
        halves = []
        for half in range(2):
            col = jnp.zeros((STACK, 1), F32)
            for pp in range(PAIRS_PER_KV):
                col = jnp.where(blk == pp, sink_ref[0, 2 * (kvh * PAIRS_PER_KV + pp) + half], col)
            halves.append(col)
        out.append(halves)
    return out


def _softmax_stack(logits, sink_cols):
    ps, sinks = [], []
    for half in range(2):
        lg = logits[:, half * 2 * CHUNK:(half + 1) * 2 * CHUNK]
        sk = sink_cols[half]
        m = jnp.maximum(jnp.max(lg, axis=-1, keepdims=True), sk)
        e = jnp.exp(lg - m)
        es = jnp.exp(sk - m)
        inv = 1.0 / (jnp.sum(e, axis=-1, keepdims=True) + es)
        ps.append(e * inv)
        sinks.append(es * inv)
    return jnp.concatenate(ps, axis=1), sinks


def _band_operands(before_k, this_k, before_v, this_v, kvh):
    kblk = jnp.concatenate([before_k[2 * kvh], this_k[2 * kvh], before_k[2 * kvh + 1], this_k[2 * kvh + 1]], axis=0)
    vblk = jnp.concatenate([before_v[2 * kvh], this_v[2 * kvh], before_v[2 * kvh + 1], this_v[2 * kvh + 1]], axis=0)
    return kblk, vblk


def _fwd_b(r0, target, wb, wkv, wob, post_g, post_b, sinks, bias):
    seq = r0.shape[0]
    tm = min(TOKENS_B, seq)
    n_chunks = tm // CHUNK
    part_rows = min(PART_ROWS_B, tm)
    n_parts = tm // part_rows
    part_chunks = part_rows // CHUNK
    shard_cols = wb.shape[2]

    def body(r0_ref, tgt_ref, wb_ref, wkv_ref, wob_ref, pg_ref, pb_ref, sink_ref, bias_ref,
             q_ref, z_ref, kv_ref, dr1_ref, loss_ref, dg_ref, db_ref, ek_last, ev_last, *scratch):
        q_s, z_s, h1_s, o_s, ek_s, ev_s = (scratch[i * n_parts:(i + 1) * n_parts] for i in range(6))
        step = pl.program_id(0)

        @pl.when(step == 0)
        def _():
            ek_last[...] = jnp.zeros_like(ek_last)
            ev_last[...] = jnp.zeros_like(ev_last)
            loss_ref[...] = jnp.zeros_like(loss_ref)
            dg_ref[...] = jnp.zeros_like(dg_ref)
            db_ref[...] = jnp.zeros_like(db_ref)

        sink_cols = _sink_columns(sink_ref)
        g1 = pg_ref[1:2, :]

        def project_steps(part):
            rows = slice(part * part_rows, (part + 1) * part_rows)
            live = {}

            def normalize():
                xhat0, _ = _ln_normalize(r0_ref[rows, :])
                h1 = xhat0 * pg_ref[0:1, :] + pb_ref[0:1, :]
                h1_s[part][...] = h1
                live["h1b"] = h1.astype(BF16)

            def shard(k):
                def run():
                    qz = _nn(live["h1b"], wb_ref[k])
                    lo, hi = k * shard_cols, (k + 1) * shard_cols
                    if hi <= B_WIDTH:
                        qs = (qz * SCALE).astype(BF16)
                        q_ref[rows, lo:hi] = qs
                        q_s[part][:, lo:hi] = qs
                    else:
                        z_ref[rows, lo - B_WIDTH:hi - B_WIDTH] = qz
                        z_s[part][:, lo - B_WIDTH:hi - B_WIDTH] = qz
                return run

            def keys():
                kv = _nn(live["h1b"], wkv_ref[...])
                kv_ref[rows, :] = kv.astype(BF16)
                for cj in range(part_chunks):
                    sub = slice(cj * CHUNK, (cj + 1) * CHUNK)
                    for i, e in enumerate(_expand_heads(kv[sub, :KV_WIDTH])):
                        ek_s[part][cj, i] = e
                    for i, e in enumerate(_expand_heads(kv[sub, KV_WIDTH:])):
                        ev_s[part][cj, i] = e

            return [normalize, keys] + [shard(k) for k in range(N_DEV)]

        def attend_steps(part):
            def stack(cj, kvh):
                def run():
                    sub = slice(cj * CHUNK, (cj + 1) * CHUNK)
                    sel = jnp.where(step * n_chunks + part * part_chunks + cj > 0, 1, 0)
                    if cj:
                        before_k, before_v = ek_s[part].at[cj - 1], ev_s[part].at[cj - 1]
                    elif part:
                        before_k, before_v = ek_s[part - 1].at[part_chunks - 1], ev_s[part - 1].at[part_chunks - 1]
                    else:
                        before_k, before_v = ek_last, ev_last
                    kblk, vblk = _band_operands(before_k, ek_s[part].at[cj], before_v, ev_s[part].at[cj], kvh)
                    logits = _nt(_stack_pairs(q_s[part], sub, kvh), kblk) + bias_ref[sel, kvh]
                    p, _ = _softmax_stack(logits, sink_cols[kvh])
                    o = _nn(p.astype(BF16), vblk)
                    for pp in range(PAIRS_PER_KV):
                        o_s[part][sub, _pair_cols(kvh, pp)] = o[pp * CHUNK:(pp + 1) * CHUNK]
                return run

            def finish():
                rows = slice(part * part_rows, (part + 1) * part_rows)
                z = z_s[part][...]
                y = o_s[part][...] * (z * jax.nn.sigmoid(z))
                r1 = ALPHA * h1_s[part][...] + _nn(y.astype(BF16), wob_ref[...])
                xhat1, rstd1 = _ln_normalize(r1)
                diff = xhat1 * g1 + pb_ref[1:2, :] - tgt_ref[rows, :]
                loss_ref[...] += 0.5 * jnp.sum(jnp.mean(diff * diff, axis=-1, keepdims=True), axis=0, keepdims=True)
                dh2 = diff * (1.0 / D_MODEL)
                dg_ref[...] += jnp.sum(dh2 * xhat1, axis=0, keepdims=True)
                db_ref[...] += jnp.sum(dh2, axis=0, keepdims=True)
                dr1_ref[rows, :] = _ln_backward(dh2 * g1, xhat1, rstd1)

            return [stack(cj, kvh) for cj in range(part_chunks) for kvh in range(N_KV_HEADS)] + [finish]

        for run in project_steps(0):
            run()
        for part in range(n_parts):
            for run in _in_turns(attend_steps(part), project_steps(part + 1) if part + 1 < n_parts else []):
                run()
        ek_last[...] = ek_s[n_parts - 1][part_chunks - 1]
        ev_last[...] = ev_s[n_parts - 1][part_chunks - 1]

    vec = pl.BlockSpec((1, D_MODEL), lambda i: (0, 0))
    return pl.pallas_call(
        body, name="fwd_b", grid=(seq // tm,),
        out_shape=[jax.ShapeDtypeStruct((seq, B_WIDTH), BF16), jax.ShapeDtypeStruct((seq, B_WIDTH), F32),
                   jax.ShapeDtypeStruct((seq, 2 * KV_WIDTH), BF16), jax.ShapeDtypeStruct((seq, D_MODEL), F32),
                   jax.ShapeDtypeStruct((1, 1), F32), jax.ShapeDtypeStruct((1, D_MODEL), F32),
                   jax.ShapeDtypeStruct((1, D_MODEL), F32)],
        in_specs=[_rows(tm, D_MODEL), _rows(tm, D_MODEL), _resident(wb.shape), _resident(wkv.shape),
                  _resident(wob.shape), _resident(post_g.shape), _resident(post_b.shape), SMEM,
                  _resident(bias.shape)],
        out_specs=[_rows(tm, B_WIDTH), _rows(tm, B_WIDTH), _rows(tm, 2 * KV_WIDTH), _rows(tm, D_MODEL),
                   pl.BlockSpec((1, 1), lambda i: (0, 0)), vec, vec],
        scratch_shapes=[pltpu.VMEM((4, CHUNK, CHUNK), BF16), pltpu.VMEM((4, CHUNK, CHUNK), BF16)]
        + [pltpu.VMEM((part_rows, B_WIDTH), BF16)] * n_parts + [pltpu.VMEM((part_rows, B_WIDTH), F32)] * n_parts
        + [pltpu.VMEM((part_rows, D_MODEL), F32)] * n_parts + [pltpu.VMEM((part_rows, B_WIDTH), F32)] * n_parts
        + [pltpu.VMEM((part_chunks, 4, CHUNK, CHUNK), BF16)] * (2 * n_parts),
        compiler_params=_seq_params(),
    )(r0, target, wb, wkv, wob, post_g, post_b, sinks, bias)


def _bwd_b(r0, dr1, q, z, kv, wb, wkv, wob, post_g, post_b, sinks, bias):
    seq = r0.shape[0]
    tm = min(TOKENS_B_BWD, seq)
    n_chunks = tm // CHUNK
    n_blocks = seq // tm
    shard_cols = wb.shape[2]

    def body(r0_ref, dr1_ref, q_ref, z_ref, kv_ref, kvp_ref, wb_ref, wkv_ref, wob_ref, pg_ref, pb_ref, sink_ref,
             bias_ref, dh1_ref, dwb_out, dwkv_out, dwob_out, dsink_ref, dl_out,
             dwb_acc, dwkv_acc, dwob_acc, dl_acc, dsink_acc, carry_ref, ek_ref, ev_ref, y_ref, dqz_ref, dkv_ref):
        step = pl.program_id(0)
        block = n_blocks - 1 - step

        @pl.when(step == 0)
        def _():
            dwb_acc[...] = jnp.zeros_like(dwb_acc)
            dwkv_acc[...] = jnp.zeros_like(dwkv_acc)
            dwob_acc[...] = jnp.zeros_like(dwob_acc)
            dl_acc[...] = jnp.zeros_like(dl_acc)
            carry_ref[...] = jnp.zeros_like(carry_ref)
            dsink_acc[...] = jnp.zeros_like(dsink_acc)

        xhat0, _ = _ln_normalize(r0_ref[...])
        h1b = (xhat0 * pg_ref[0:1, :] + pb_ref[0:1, :]).astype(BF16)
        dr1 = dr1_ref[...]
        dr1b = dr1.astype(BF16)
        dy = _nt(dr1b, wob_ref[...])
        sg, dsg = _silu_and_grad(z_ref[...])
        do_all = dy * sg
        dyz = dy * dsg

        for ci in range(n_chunks + 1):
            src = kvp_ref[...] if ci == 0 else kv_ref[(ci - 1) * CHUNK:ci * CHUNK, :]
            src = src.astype(F32)
            for i, e in enumerate(_expand_heads(src[:, :KV_WIDTH])):
                ek_ref[ci, i] = e
            for i, e in enumerate(_expand_heads(src[:, KV_WIDTH:])):
                ev_ref[ci, i] = e

        sink_cols = _sink_columns(sink_ref)
        for ci in reversed(range(n_chunks)):
            rows = slice(ci * CHUNK, (ci + 1) * CHUNK)
            sel = jnp.where(block * n_chunks + ci > 0, 1, 0)
            dk_band = jnp.zeros((2 * CHUNK, KV_WIDTH), F32)
            dv_band = jnp.zeros((2 * CHUNK, KV_WIDTH), F32)
            for kvh in range(N_KV_HEADS):
                kblk, vblk = _band_operands(ek_ref.at[ci], ek_ref.at[ci + 1], ev_ref.at[ci], ev_ref.at[ci + 1], kvh)
                qs = _stack_pairs(q_ref, rows, kvh)
                p, psink = _softmax_stack(_nt(qs, kblk) + bias_ref[sel, kvh], sink_cols[kvh])
                pb16 = p.astype(BF16)
                o = _nn(pb16, vblk)
                y = (o * _stack_pairs(sg, rows, kvh)).astype(BF16)
                dz = (o * _stack_pairs(dyz, rows, kvh)).astype(BF16)
                dob = _stack_pairs(do_all, rows, kvh).astype(BF16)
                dp = _nt(dob, vblk)
                dlogits = []
                for half in range(2):
                    hs = slice(half * 2 * CHUNK, (half + 1) * 2 * CHUNK)
                    delta = jnp.sum(dp[:, hs] * p[:, hs], axis=-1, keepdims=True)
                    dlogits.append(p[:, hs] * (dp[:, hs] - delta))
                    dsink_acc[2 * kvh + half] += psink[half] * delta
                dlog = jnp.concatenate(dlogits, axis=1)
                dl_acc[kvh] += dlog
                dsc = dlog.astype(BF16)
                dq = (_nn(dsc, kblk) * SCALE).astype(BF16)
                for pp in range(PAIRS_PER_KV):
                    blk = slice(pp * CHUNK, (pp + 1) * CHUNK)
                    cols = _pair_cols(kvh, pp)
                    y_ref[rows, cols] = y[blk]
                    dqz_ref[rows, cols] = dq[blk]
                    dqz_ref[rows, B_WIDTH + cols.start:B_WIDTH + cols.stop] = dz[blk]
                dk_band = dk_band + _fold_heads(_tn(dsc, qs), kvh)
                dv_band = dv_band + _fold_heads(_tn(pb16, dob), kvh)
            band = jnp.concatenate([dk_band, dv_band], axis=1)
            dkv_ref[rows, :] = (band[CHUNK:] + carry_ref[...]).astype(BF16)
            carry_ref[...] = band[:CHUNK]

        dqz = dqz_ref[...]
        dkv = dkv_ref[...]
        dh1 = ALPHA * dr1 + _nt(dkv, wkv_ref[...])
        for k in range(N_DEV):
            part = dqz[:, k * shard_cols:(k + 1) * shard_cols]
            dh1 = dh1 + _nt(part, wb_ref[k])
            dwb_acc[k] += _tn(h1b, part)
        dh1_ref[...] = dh1
        dwkv_acc[...] += _tn(h1b, dkv)
        dwob_acc[...] += _tn(y_ref[...], dr1b)

        @pl.when(step == n_blocks - 1)
        def _():
            pltpu.sync_copy(dwb_acc, dwb_out)
            pltpu.sync_copy(dwkv_acc, dwkv_out)
            pltpu.sync_copy(dwob_acc, dwob_out)
            pltpu.sync_copy(dl_acc, dl_out)
            lane = lax.broadcasted_iota(jnp.int32, (1, CHUNK), 1)
            dsink = jnp.zeros((1, CHUNK), F32)
            for kvh in range(N_KV_HEADS):
                for half in range(2):
                    for pp in range(PAIRS_PER_KV):
                        head = 2 * (kvh * PAIRS_PER_KV + pp) + half
                        tot = jnp.sum(dsink_acc[2 * kvh + half, pp * CHUNK:(pp + 1) * CHUNK, :], axis=0, keepdims=True)
                        dsink = dsink - jnp.where(lane == head, tot, 0.0)
            dsink_ref[...] = dsink

    rev = functools.partial(_rows, tm, reverse_of=n_blocks)
    prev_chunk = pl.BlockSpec(
        (CHUNK, 2 * KV_WIDTH), lambda i: (jnp.maximum((n_blocks - 1 - i) * n_chunks - 1, 0), 0))
    return pl.pallas_call(
        body, name="bwd_b", grid=(n_blocks,),
        out_shape=[jax.ShapeDtypeStruct((seq, D_MODEL), F32), jax.ShapeDtypeStruct(wb.shape, F32),
                   jax.ShapeDtypeStruct(wkv.shape, F32), jax.ShapeDtypeStruct(wob.shape, F32),
                   jax.ShapeDtypeStruct((1, CHUNK), F32),
                   jax.ShapeDtypeStruct((N_KV_HEADS, STACK, 4 * CHUNK), F32)],
        in_specs=[rev(D_MODEL), rev(D_MODEL), rev(B_WIDTH), rev(B_WIDTH), rev(2 * KV_WIDTH), prev_chunk,
                  _resident(wb.shape), _resident(wkv.shape), _resident(wob.shape), _resident(post_g.shape),
                  _resident(post_b.shape), SMEM, _resident(bias.shape)],
        out_specs=[rev(D_MODEL), ANY, ANY, ANY, pl.BlockSpec((1, CHUNK), lambda i: (0, 0)), ANY],
        scratch_shapes=[pltpu.VMEM(wb.shape, F32), pltpu.VMEM(wkv.shape, F32), pltpu.VMEM(wob.shape, F32),
                        pltpu.VMEM((N_KV_HEADS, STACK, 4 * CHUNK), F32), pltpu.VMEM((2 * N_KV_HEADS, STACK, 1), F32),
                        pltpu.VMEM((CHUNK, 2 * KV_WIDTH), F32),
                        pltpu.VMEM((n_chunks + 1, 4, CHUNK, CHUNK), BF16),
                        pltpu.VMEM((n_chunks + 1, 4, CHUNK, CHUNK), BF16),
                        pltpu.VMEM((tm, B_WIDTH), BF16), pltpu.VMEM((tm, 2 * B_WIDTH), BF16),
                        pltpu.VMEM((tm, 2 * KV_WIDTH), BF16)],
        compiler_params=_seq_params(),
    )(r0, dr1, q, z, kv, kv, wb, wkv, wob, post_g, post_b, sinks, bias)


def _bwd_a(dh1, r0, p, ln_gb, w_spatial, bsp_t, woa, post_g, exchange):
    seq = r0.shape[0]
    tm = min(TOKENS_A, seq)
    n_chunks = tm // CHUNK
    n_blocks = seq // tm

    def body(dh1_ref, r0_ref, p_ref, ln_ref, ws_ref, bsp_ref, woa_ref, pg_ref, *refs):
        ex_in, refs = refs[:exchange.n], refs[exchange.n:]
        (dp_ref, dr0_ref, dwoa_out, dws_ref, dbsp_ref, dln_ref, dpost_ref), refs = refs[:7], refs[7:]
        ex_out, refs = refs[:exchange.n], refs[exchange.n:]
        (dwoa_acc, s_ref, ds_ref, dvn_ref), ex_sems = refs[:4], refs[4:]
        step = pl.program_id(0)

        @pl.when(step == 0)
        def _():
            exchange.start(ex_in, ex_out, ex_sems)
            dwoa_acc[...] = jnp.zeros_like(dwoa_acc)
            dws_ref[...] = jnp.zeros_like(dws_ref)
            dbsp_ref[...] = jnp.zeros_like(dbsp_ref)
            dln_ref[...] = jnp.zeros_like(dln_ref)
            dpost_ref[...] = jnp.zeros_like(dpost_ref)

        dh1 = dh1_ref[...]
        xhat0, rstd0 = _ln_normalize(r0_ref[...])
        dpost_ref[0:1, :] += jnp.sum(dh1 * xhat0, axis=0, keepdims=True)
        dpost_ref[1:2, :] += jnp.sum(dh1, axis=0, keepdims=True)
        dr0 = _ln_backward(dh1 * pg_ref[0:1, :], xhat0, rstd0)
        dr0_ref[...] = dr0
        dr0b = dr0.astype(BF16)

        vhat, rstd_v = _ln_normalize(p_ref[:, A_WIDTH:2 * A_WIDTH])
        ln_g = ln_ref[0:1, :]
        vn = (vhat * ln_g + ln_ref[1:2, :]).astype(BF16)
        _spatial_gate(ws_ref, bsp_ref, vn, s_ref, n_chunks)
        u = p_ref[:, :A_WIDTH]
        s = s_ref[...]
        sg, dsg = _silu_and_grad(p_ref[:, 2 * A_WIDTH:])
        dwoa_acc[...] += _tn((u * s * sg).astype(BF16), dr0b)
        dy = _nt(dr0b, woa_ref[...])
        dp_ref[:, :A_WIDTH] = (dy * s * sg).astype(BF16)
        dp_ref[:, 2 * A_WIDTH:] = (dy * u * s * dsg).astype(BF16)
        ds_ref[...] = dy * u * sg

        lane = lax.broadcasted_iota(jnp.int32, (CHUNK, CHUNK), 1)
        dbsp = jnp.zeros((CHUNK, CHUNK), F32)
        for g in range(A_GROUPS):
            w = _masked_spatial(ws_ref, g)
            cols = slice(g * A_GROUP_DIM, (g + 1) * A_GROUP_DIM)
            dws = jnp.zeros((CHUNK, CHUNK), F32)
            for ci in range(n_chunks):
                rows = slice(ci * CHUNK, (ci + 1) * CHUNK)
                ds = ds_ref[rows, cols]
                dsb = ds.astype(BF16)
                dvn_ref[rows, cols] = _tn(w, dsb)
                dws = dws + _nt(dsb, vn[rows, cols])
                dbsp = dbsp + jnp.where(lane == g, jnp.sum(ds, axis=-1, keepdims=True), 0.0)
            dws_ref[g] += dws
        dbsp_ref[...] += dbsp

        dvn = dvn_ref[...]
        dln_ref[0:1, :] += jnp.sum(dvn * vhat, axis=0, keepdims=True)
        dln_ref[1:2, :] += jnp.sum(dvn, axis=0, keepdims=True)
        dp_ref[:, A_WIDTH:2 * A_WIDTH] = _ln_backward(dvn * ln_g, vhat, rstd_v).astype(BF16)

        @pl.when(step == n_blocks - 1)
        def _():
            t = lax.broadcasted_iota(jnp.int32, (CHUNK, CHUNK), 0)
            for g in range(A_GROUPS):
                dws_ref[g] = jnp.where(lane <= t, dws_ref[g], 0.0)
            pltpu.sync_copy(dwoa_acc, dwoa_out)
            exchange.wait(ex_in, ex_out, ex_sems)

    def fixed(shape):
        nd = len(shape)
        return pl.BlockSpec(shape, lambda i: (0,) * nd)

    return pl.pallas_call(
        body, name="bwd_a", grid=(n_blocks,),
        out_shape=[jax.ShapeDtypeStruct((seq, 3 * A_WIDTH), BF16), jax.ShapeDtypeStruct((seq, D_MODEL), F32),
                   jax.ShapeDtypeStruct(woa.shape, F32), jax.ShapeDtypeStruct(w_spatial.shape, F32),
                   jax.ShapeDtypeStruct((CHUNK, CHUNK), F32), jax.ShapeDtypeStruct((2, A_WIDTH), F32),
                   jax.ShapeDtypeStruct((2, D_MODEL), F32)] + exchange.out_shape,
        in_specs=[_rows(tm, D_MODEL), _rows(tm, D_MODEL), _rows(tm, 3 * A_WIDTH), _resident(ln_gb.shape),
                  _resident(w_spatial.shape), _resident(bsp_t.shape), _resident(woa.shape),
                  _resident(post_g.shape)] + [ANY] * exchange.n,
        out_specs=[_rows(tm, 3 * A_WIDTH), _rows(tm, D_MODEL), ANY, fixed(w_spatial.shape),
                   fixed((CHUNK, CHUNK)), fixed((2, A_WIDTH)), fixed((2, D_MODEL))] + [ANY] * exchange.n,
        scratch_shapes=[pltpu.VMEM(woa.shape, F32), pltpu.VMEM((tm, A_WIDTH), F32),
                        pltpu.VMEM((tm, A_WIDTH), F32), pltpu.VMEM((tm, A_WIDTH), F32)] + exchange.scratch_shapes,
        compiler_params=_seq_params(),
    )(dh1, r0, p, ln_gb, w_spatial, bsp_t, woa, post_g, *exchange.inputs)


def _input_grad_blocks(seq):
    tm = min(TOKENS_DX, seq // 8)
    return tm, seq // tm


def _input_grad(dp, dr0, wa, first, n_blocks, earlier, exchange):
    seq = dr0.shape[0]
    tm, _ = _input_grad_blocks(seq)
    shard_cols = wa.shape[2]
    n_own = 3 if earlier is None else 4

    def body(dp_ref, dr0_ref, wa_ref, *refs):
        refs = refs[n_own - 3:]
        ex_in, dx_ref, ex_out, ex_sems = (refs[:exchange.n], refs[exchange.n], refs[exchange.n + 1:2 * exchange.n + 1],
                                          refs[2 * exchange.n + 1:])
        step = pl.program_id(0)

        @pl.when(step == 0)
        def _():
            exchange.start(ex_in, ex_out, ex_sems)

        dx = ALPHA * dr0_ref[...]
        for k in range(N_DEV):
            dx = dx + _nt(dp_ref[:, k * shard_cols:(k + 1) * shard_cols], wa_ref[k])
        dx_ref[...] = dx

        @pl.when(step == n_blocks - 1)
        def _():
            exchange.wait(ex_in, ex_out, ex_sems)

    def rows(width):
        return pl.BlockSpec((tm, width), lambda i: (first + i, 0))

    return pl.pallas_call(
        body, name=f"input_grad_{first}", grid=(n_blocks,),
        out_shape=[jax.ShapeDtypeStruct((seq, D_MODEL), F32)] + exchange.out_shape,
        in_specs=[rows(3 * A_WIDTH), rows(D_MODEL), _resident(wa.shape)] + [ANY] * (n_own - 3 + exchange.n),
        out_specs=[rows(D_MODEL)] + [ANY] * exchange.n,
        scratch_shapes=exchange.scratch_shapes,
        input_output_aliases={} if earlier is None else {3: 0},
        compiler_params=_seq_params(),
    )(dp, dr0, wa, *(() if earlier is None else (earlier,)), *exchange.inputs)


def _in_proj_weight_grad(xb, dp, shard_cols, exchange):
    seq = xb.shape[0]
    tk = min(TOKENS_WGRAD, seq)
    n_k = seq // tk

    def body(x_ref, dp_ref, *refs):
        ex_in, out_ref, ex_out, ex_sems = (refs[:exchange.n], refs[exchange.n], refs[exchange.n + 1:2 * exchange.n + 1],
                                           refs[2 * exchange.n + 1:])
        shard, step = pl.program_id(0), pl.program_id(1)

        @pl.when((shard == 0) & (step == 0))
        def _():
            exchange.start(ex_in, ex_out, ex_sems)

        @pl.when(step == 0)
        def _():
            out_ref[...] = jnp.zeros_like(out_ref)

        out_ref[...] += _tn(x_ref[...], dp_ref[...])

        @pl.when((shard == N_DEV - 1) & (step == n_k - 1))
        def _():
            exchange.wait(ex_in, ex_out, ex_sems)

    return pl.pallas_call(
        body, name="in_proj_weight_grad", grid=(N_DEV, n_k),
        out_shape=[jax.ShapeDtypeStruct((N_DEV, D_MODEL, shard_cols), F32)] + exchange.out_shape,
        in_specs=[pl.BlockSpec((tk, D_MODEL), lambda j, k: (k, 0)),
                  pl.BlockSpec((tk, shard_cols), lambda j, k: (k, j))] + [ANY] * exchange.n,
        out_specs=[pl.BlockSpec((None, D_MODEL, shard_cols), lambda j, k: (j, 0, 0))] + [ANY] * exchange.n,
        scratch_shapes=exchange.scratch_shapes,
        compiler_params=pltpu.CompilerParams(dimension_semantics=("arbitrary", "arbitrary"),
                                             vmem_limit_bytes=VMEM_LIMIT_BYTES),
    )(xb, dp, *exchange.inputs)


class _Exchange:
    def __init__(self, pair=(), chip=()):
        self.pair, self.chip = tuple(pair), tuple(chip)
        self.inputs = self.pair + self.chip
        self.n = len(self.inputs)
        self.out_shape = [jax.ShapeDtypeStruct((4, 1) + g.shape[2:], F32) for g in self.pair]
        self.out_shape += [jax.ShapeDtypeStruct((3,) + s.shape[1:], s.dtype) for s in self.chip]
        self.scratch_shapes = [pltpu.SemaphoreType.DMA((max(len(self.pair), 1),)) for _ in range(2)]
        self.scratch_shapes += [pltpu.SemaphoreType.DMA((max(len(self.chip), 1), 3)) for _ in range(2)]

    def _copies(self, ins, outs, sems):
        if not self.n:
            return []
        pair_send, pair_recv, chip_send, chip_recv = sems
        x, y, c = lax.axis_index("x"), lax.axis_index("y"), lax.axis_index("c")
        copies = []
        for a in range(len(self.pair)):
            copies.append(pltpu.make_async_remote_copy(
                src_ref=ins[a].at[:, pl.ds(1 - c, 1)], dst_ref=outs[a],
                send_sem=pair_send.at[a], recv_sem=pair_recv.at[a], device_id=(x, y, 1 - c), device_id_type=MESH))
        for a in range(len(self.chip)):
            src, dst = ins[len(self.pair) + a], outs[len(self.pair) + a]
            for k, (px, py) in enumerate([(1 - x, y), (x, 1 - y), (1 - x, 1 - y)]):
                copies.append(pltpu.make_async_remote_copy(
                    src_ref=src.at[2 * px + py], dst_ref=dst.at[k], send_sem=chip_send.at[a, k],
                    recv_sem=chip_recv.at[a, k], device_id=(px, py, c), device_id_type=MESH))
        return copies

    def start(self, ins, outs, sems):
        for cp in self._copies(ins, outs, sems):
            cp.start()

    def wait(self, ins, outs, sems):
        for cp in self._copies(ins, outs, sems):
            cp.wait()


def _row_tile(rows, cols, itemsize=4):
    if rows * cols * itemsize <= (1 << 20):
        return rows
    t = 8
    while 2 * t * cols * itemsize <= (1 << 20) and rows % (2 * t) == 0:
        t *= 2
    assert rows % t == 0, (rows, cols)
    return t


def _pair_sum(place, grad, landed, dtype):
    _, _, rows, cols = grad.shape
    tr = _row_tile(rows, cols)

    def body(place_ref, a_ref, b_ref, out_ref):
        out_ref[...] = (a_ref[...] + b_ref[...]).astype(dtype)

    return pl.pallas_call(
        body, name="pair_sum", out_shape=jax.ShapeDtypeStruct((4, rows, cols), dtype),
        grid_spec=pltpu.PrefetchScalarGridSpec(
            num_scalar_prefetch=1, grid=(4, rows // tr),
            in_specs=[pl.BlockSpec((None, None, tr, cols), lambda k, i, pr: (k, pr[1], i, 0)),
                      pl.BlockSpec((None, None, tr, cols), lambda k, i, pr: (k, 0, i, 0))],
            out_specs=pl.BlockSpec((None, tr, cols), lambda k, i, pr: (k, i, 0))),
        compiler_params=pltpu.CompilerParams(dimension_semantics=("parallel", "parallel")),
    )(place, grad, landed)


def _adamw_math(w, g, m, v):
    m = ADAM_B1 * m + (1.0 - ADAM_B1) * g
    v = ADAM_B2 * v + (1.0 - ADAM_B2) * (g * g)
    m_hat = m / (1.0 - ADAM_B1 ** ADAM_STEP)
    v_hat = v / (1.0 - ADAM_B2 ** ADAM_STEP)
    delta = -ADAM_LR * (m_hat / (jnp.sqrt(v_hat) + ADAM_EPS) + ADAM_WD * w)
    return delta, m, v


def _adamw_shard(place, grad, landed, got, w, m, v):
    rows, cols = w.shape
    tr = _row_tile(rows, cols)

    def body(place_ref, a_ref, b_ref, got_ref, w_ref, m_ref, v_ref, g_out, d_out, m_out, v_out):
        g = a_ref[...] + b_ref[...]
        for k in range(3):
            g = g + got_ref[k].astype(F32)
        g_out[...] = g
        d_out[...], m_out[...], v_out[...] = _adamw_math(w_ref[...], g, m_ref[...], v_ref[...])

    spec = pl.BlockSpec((tr, cols), lambda i, pr: (i, 0))
    out = jax.ShapeDtypeStruct((rows, cols), F32)
    return pl.pallas_call(
        body, name="adamw_shard", out_shape=[out] * 4,
        grid_spec=pltpu.PrefetchScalarGridSpec(
            num_scalar_prefetch=1, grid=(rows // tr,),
            in_specs=[pl.BlockSpec((None, None, tr, cols), lambda i, pr: (pr[0], pr[1], i, 0)),
                      pl.BlockSpec((None, None, tr, cols), lambda i, pr: (pr[0], 0, i, 0)),
                      pl.BlockSpec((3, tr, cols), lambda i, pr: (0, i, 0)), spec, spec, spec],
            out_specs=[spec] * 4),
        compiler_params=pltpu.CompilerParams(dimension_semantics=("parallel",)),
    )(place, grad, landed, got, w, m, v)


def _gather_small(grad, landed, got):
    rows, cols = got.shape[1:]

    def body(grad_ref, landed_ref, got_ref, out_ref, send_sems, recv_sems):
        x, y, c = lax.axis_index("x"), lax.axis_index("y"), lax.axis_index("c")
        me, sibling = (x, y, c), (x, y, 1 - c)
        chips = [(1 - x, y), (x, 1 - y), (1 - x, 1 - y)]

        def slot(px, py, pc):
            return out_ref.at[4 * px + 2 * py + pc]

        def copy(k, block, to):
            return pltpu.make_async_remote_copy(
                src_ref=slot(*block), dst_ref=slot(*block), send_sem=send_sems.at[k], recv_sem=recv_sems.at[k],
                device_id=to, device_id_type=MESH)

        mine = grad_ref[2 * x + y, c] + landed_ref[2 * x + y, 0]
        out_ref[4 * x + 2 * y + c] = ((mine + got_ref[0]) + got_ref[1]) + got_ref[2]
        first = [copy(0, me, sibling)] + [copy(1 + j, me, (*chip, c)) for j, chip in enumerate(chips)]
        for cp in first:
            cp.start()
        passed = [copy(4 + j, (*chip, c), sibling) for j, chip in enumerate(chips)]
        for j, chip in enumerate(chips):
            copy(1 + j, (*chip, c), me).wait_recv()
            passed[j].start()
        copy(0, sibling, me).wait_recv()
        for j, chip in enumerate(chips):
            copy(4 + j, (*chip, 1 - c), me).wait_recv()
        for cp in first + passed:
            cp.wait_send()

    return pl.pallas_call(
        body, name="gather_small", out_shape=jax.ShapeDtypeStruct((N_DEV, rows, cols), F32),
        in_specs=[VMEM] * 3, out_specs=VMEM,
        scratch_shapes=[pltpu.SemaphoreType.DMA((7,)), pltpu.SemaphoreType.DMA((7,))],
    )(grad, landed, got)


def _adamw_small(params):
    n = len(params)

    def body(*refs):
        ins, outs = refs[:4 * n], refs[4 * n:]
        for i in range(n):
            w_ref, g_ref, m_ref, v_ref = ins[4 * i:4 * i + 4]
            d, m, v = _adamw_math(w_ref[...], g_ref[...], m_ref[...], v_ref[...])
            outs[3 * i][...] = d
            outs[3 * i + 1][...] = m
            outs[3 * i + 2][...] = v

    flat = [a for p in params for a in p]
    out_shape = [jax.ShapeDtypeStruct(p[0].shape, F32) for p in params for _ in range(3)]
    outs = pl.pallas_call(
        body, name="adamw_small", out_shape=out_shape, in_specs=[VMEM] * (4 * n), out_specs=[VMEM] * (3 * n),
    )(*flat)
    return [tuple(outs[3 * i:3 * i + 3]) for i in range(n)]


def _pack_small(parts):
    rows = []
    for a in parts:
        r, c = a.shape
        rows.append(jnp.pad(a, ((0, (-r) % 8), (0, CHUNK - c))))
    flat = jnp.concatenate(rows, axis=0)
    return jnp.pad(flat, ((0, N_DEV * SMALL_ROWS - flat.shape[0]), (0, 0))).reshape(N_DEV, SMALL_ROWS, CHUNK)


def _unpack_small(flat, shapes):
    out, at = [], 0
    for r, c in shapes:
        out.append(flat[at:at + r, :c])
        at += r + (-r) % 8
    return out


def kernel(x, w_in_a, sgu_ln_g, sgu_ln_b, w_spatial, b_spatial, w_out_a, w_kv, w_in_b, attn_sinks, rel_bias, w_out_b, post_ln_g, post_ln_b, loss_target, m_w_in_a, m_sgu_ln_g, m_sgu_ln_b, m_w_spatial, m_b_spatial, m_w_out_a, m_w_kv, m_w_in_b, m_attn_sinks, m_rel_bias, m_w_out_b, m_post_ln_g, m_post_ln_b, v_w_in_a, v_sgu_ln_g, v_sgu_ln_b, v_w_spatial, v_b_spatial, v_w_out_a, v_w_kv, v_w_in_b, v_attn_sinks, v_rel_bias, v_w_out_b, v_post_ln_g, v_post_ln_b):
    seq = x.shape[1]
    me = 4 * lax.axis_index("x") + 2 * lax.axis_index("y") + lax.axis_index("c")
    bucket = jnp.asarray(_band_tables())

    ln_shard = jnp.concatenate([sgu_ln_g, sgu_ln_b], axis=0)
    x2, tgt = x[0], loss_target[0]
    order = jnp.stack([2 * px + py for px, py, _ in _use_order()[::2]]).astype(jnp.int32)
    p, wa, woa, ln_all = _gather_in_proj(order, x2, w_in_a[0], (w_out_a[0],), ln_shard)
    woa = woa.reshape(A_WIDTH, D_MODEL)
    ln_gb = jnp.transpose(ln_all, (1, 0, 2)).reshape(2, A_WIDTH)
    ws = w_spatial[0]
    bsp_t = jnp.pad(b_spatial[0].T, ((0, 0), (0, CHUNK - A_GROUPS)))
    bias = _bias_table(rel_bias, bucket)

    r0, xb, wkv, wb, wob = _fwd_a(x2, p, ln_gb, ws, bsp_t, woa, (w_kv, w_in_b[0], w_out_b[0]))
    wkv = wkv.reshape(D_MODEL, 2 * KV_WIDTH)
    wob = wob.reshape(B_WIDTH, D_MODEL)
    q, z, kv, dr1, loss, dg1, db1 = _fwd_b(r0, tgt, wb, wkv, wob, post_ln_g, post_ln_b, attn_sinks, bias)
    dh1, dwb, dwkv, dwob, dsink, dlog = _bwd_b(r0, dr1, q, z, kv, wb, wkv, wob, post_ln_g, post_ln_b, attn_sinks,
                                               bias)
    drel = _rel_bias_grad(dlog, bucket)

    place = jnp.stack([2 * lax.axis_index("x") + lax.axis_index("y"), lax.axis_index("c")]).astype(jnp.int32)

    def by_device(g):
        return g.reshape((4, 2, g.shape[0] // N_DEV) + g.shape[1:]) if g.ndim == 2 else g.reshape((4, 2) + g.shape[1:])

    dwb, dwkv, dwob = by_device(dwb), by_device(dwkv), by_device(dwob)
    *outs, l_wb, l_wkv, l_wob = _bwd_a(dh1, r0, p, ln_gb, ws, bsp_t, woa, post_ln_g,
                                       _Exchange(pair=(dwb, dwkv, dwob)))
    dp, dr0, dwoa, dws, dbsp_t, dln, dpost0 = outs
    dwoa = by_device(dwoa)
    small_shapes = [(A_GROUPS * CHUNK, CHUNK), (A_GROUPS, CHUNK), (1, N_Q_HEADS), (REL_BUCKETS, N_Q_HEADS),
                    (16, CHUNK), (16, CHUNK), (16, CHUNK), (16, CHUNK), (1, 1)]
    small = by_device(_pack_small([
        dws.reshape(A_GROUPS * CHUNK, CHUNK), dbsp_t[:, :A_GROUPS].T, dsink[:, :N_Q_HEADS], drel[:, :N_Q_HEADS],
        jnp.concatenate([dpost0[0:1], dg1], axis=0).reshape(16, CHUNK),
        jnp.concatenate([dpost0[1:2], db1], axis=0).reshape(16, CHUNK),
        dln[0].reshape(16, CHUNK), dln[1].reshape(16, CHUNK), loss]))
    s_wb, s_wkv, s_wob = (_pair_sum(place, g, l, BF16) for g, l in ((dwb, l_wb), (dwkv, l_wkv), (dwob, l_wob)))
    dwa, l_woa, l_small, g_wb, g_wkv, g_wob = _in_proj_weight_grad(
        xb, dp, wa.shape[2], _Exchange(pair=(dwoa, small), chip=(s_wb, s_wkv, s_wob)))
    dwa = by_device(dwa)
    s_woa, s_small = _pair_sum(place, dwoa, l_woa, BF16), _pair_sum(place, small, l_small, F32)
    _, blocks = _input_grad_blocks(seq)
    cut0, cut1 = blocks // 4, blocks - blocks // 8
    dx, l_wa, g_woa, g_small = _input_grad(dp, dr0, wa, 0, cut0, None,
                                           _Exchange(pair=(dwa,), chip=(s_woa, s_small)))
    s_wa = _pair_sum(place, dwa, l_wa, BF16)
    dx, g_wa = _input_grad(dp, dr0, wa, cut0, cut1 - cut0, dx, _Exchange(chip=(s_wa,)))
    dx, = _input_grad(dp, dr0, wa, cut1, blocks - cut1, dx, _Exchange())
    full = [dwa, dwoa, dwkv, dwb, dwob, small]
    landed = [l_wa, l_woa, l_wkv, l_wb, l_wob, l_small]
    got = [g_wa, g_woa, g_wkv, g_wb, g_wob, g_small]

    big = [(w_in_a, m_w_in_a, v_w_in_a), (w_out_a, m_w_out_a, v_w_out_a), (w_kv, m_w_kv, v_w_kv),
           (w_in_b, m_w_in_b, v_w_in_b), (w_out_b, m_w_out_b, v_w_out_b)]
    big_out = []
    for i, (w, m, v) in enumerate(big):
        shape2 = got[i].shape[1:]
        res = _adamw_shard(place, full[i], landed[i], got[i], w.reshape(shape2), m.reshape(shape2),
                           v.reshape(shape2))
        big_out.append([r.reshape(w.shape) for r in res])

    small_all = _gather_small(full[5], landed[5], got[5]).reshape(N_DEV * SMALL_ROWS, CHUNK)
    g_ws, g_bsp, g_sink, g_rel, g_pg, g_pb, g_lng, g_lnb, loss_all = _unpack_small(small_all, small_shapes)
    g_ws = g_ws.reshape(w_spatial.shape)
    g_bsp = g_bsp.reshape(b_spatial.shape)
    g_pg = g_pg.reshape(post_ln_g.shape)
    g_pb = g_pb.reshape(post_ln_b.shape)
    g_lng = lax.dynamic_slice(g_lng.reshape(1, A_WIDTH), (0, me * sgu_ln_g.shape[1]), sgu_ln_g.shape)
    g_lnb = lax.dynamic_slice(g_lnb.reshape(1, A_WIDTH), (0, me * sgu_ln_b.shape[1]), sgu_ln_b.shape)
    small_params = [(sgu_ln_g, g_lng, m_sgu_ln_g, v_sgu_ln_g), (sgu_ln_b, g_lnb, m_sgu_ln_b, v_sgu_ln_b),
                    (w_spatial, g_ws, m_w_spatial, v_w_spatial), (b_spatial, g_bsp, m_b_spatial, v_b_spatial),
                    (attn_sinks, g_sink, m_attn_sinks, v_attn_sinks), (rel_bias, g_rel, m_rel_bias, v_rel_bias),
                    (post_ln_g, g_pg, m_post_ln_g, v_post_ln_g), (post_ln_b, g_pb, m_post_ln_b, v_post_ln_b)]
    flat2 = [tuple(a.reshape(-1, a.shape[-1]) for a in p) for p in small_params]
    small_out = [tuple(r.reshape(p[0].shape) for r in res) for res, p in zip(_adamw_small(flat2), small_params)]

    order = [("big", 0), ("small", 0), ("small", 1), ("small", 2), ("small", 3), ("big", 1), ("big", 2), ("big", 3),
             ("small", 4), ("small", 5), ("big", 4), ("small", 6), ("small", 7)]
    grads, deltas, new_m, new_v = [], [], [], []
    for kind, i in order:
        if kind == "big":
            g, d, m, v = big_out[i]
        else:
            g = small_params[i][1]
            d, m, v = small_out[i]
        grads.append(g)
        deltas.append(d)
        new_m.append(m)
        new_v.append(v)
    return (loss_all[0, 0], dx.reshape(x.shape), *grads, *deltas, *new_m, *new_v)
```

```python
import functools

import jax
import jax.numpy as jnp
import numpy as np
from jax import lax
from jax.experimental import pallas as pl
from jax.experimental.pallas import tpu as pltpu

F32 = jnp.float32
BF16 = jnp.bfloat16

D_MODEL = 1024
A_WIDTH = 2048
A_GROUPS = 8
A_GROUP_DIM = A_WIDTH // A_GROUPS
CHUNK = 128
HEAD_DIM = 64
N_Q_HEADS = 16
N_KV_HEADS = 2
PAIRS_PER_KV = N_Q_HEADS // N_KV_HEADS // 2
N_PAIRS = N_Q_HEADS // 2
B_WIDTH = N_Q_HEADS * HEAD_DIM
KV_WIDTH = N_KV_HEADS * HEAD_DIM
REL_BUCKETS = 32
REL_MAX_DIST = 128
DEPTH = 2
ALPHA = (2.0 * DEPTH) ** 0.25
LN_EPS = 1e-5
NEG_INF = -1e30
SCALE = HEAD_DIM ** -0.5

ADAM_LR = 0.001
ADAM_B1 = 0.9
ADAM_B2 = 0.999
ADAM_EPS = 1e-08
ADAM_WD = 0.01
ADAM_STEP = 10

N_DEV = 8
MESH = pl.DeviceIdType.MESH
TOKENS_IN_PROJ = 1024
TOKENS_WGRAD = 4096
TOKENS_A = 256
TOKENS_A_FWD = 512
TOKENS_B = 512
PART_ROWS_B = 256
TOKENS_B_BWD = 256
TOKENS_DX = 512
VMEM_LIMIT_BYTES = 60 * 1024 * 1024
SMALL_ROWS = 144

ANY = pl.BlockSpec(memory_space=pl.ANY)
VMEM = pl.BlockSpec(memory_space=pltpu.VMEM)
SMEM = pl.BlockSpec(memory_space=pltpu.SMEM)


def _nn(a, b):
    return jnp.dot(a, b, preferred_element_type=F32)


def _nt(a, b):
    return lax.dot_general(a, b, (((1,), (1,)), ((), ())), preferred_element_type=F32)


def _tn(a, b):
    return lax.dot_general(a, b, (((0,), (0,)), ((), ())), preferred_element_type=F32)


def _resident(shape):
    nd = len(shape)
    return pl.BlockSpec(shape, lambda i: (0,) * nd, pipeline_mode=pl.Buffered(1))


def _rows(tm, width, reverse_of=None):
    if reverse_of is None:
        return pl.BlockSpec((tm, width), lambda i: (i, 0))
    return pl.BlockSpec((tm, width), lambda i: (reverse_of - 1 - i, 0))


def _seq_params():
    return pltpu.CompilerParams(dimension_semantics=("arbitrary",), vmem_limit_bytes=VMEM_LIMIT_BYTES)


def _in_turns(first, second):
    merged = [(i / max(len(first), 1), 0, f) for i, f in enumerate(first)]
    merged += [((i + 0.5) / max(len(second), 1), 1, f) for i, f in enumerate(second)]
    return [f for _, _, f in sorted(merged, key=lambda t: t[:2])]


def _ln_normalize(r):
    mu = jnp.mean(r, axis=-1, keepdims=True)
    xc = r - mu
    var = jnp.mean(xc * xc, axis=-1, keepdims=True)
    rstd = lax.rsqrt(var + LN_EPS)
    return xc * rstd, rstd


def _ln_backward(dxhat, xhat, rstd):
    m1 = jnp.mean(dxhat, axis=-1, keepdims=True)
    m2 = jnp.mean(dxhat * xhat, axis=-1, keepdims=True)
    return rstd * (dxhat - m1 - xhat * m2)


def _silu_and_grad(z):
    sig = jax.nn.sigmoid(z)
    return z * sig, sig * (1.0 + z * (1.0 - sig))


def _band_tables():
    t = np.arange(CHUNK, dtype=np.int32)[:, None]
    j = np.arange(2 * CHUNK, dtype=np.int32)[None, :]
    d = t + CHUNK - j
    in_window = (d >= 0) & (d < CHUNK)
    dc = np.clip(d, 0, REL_MAX_DIST - 1)
    max_exact = REL_BUCKETS // 2
    df = np.maximum(dc, 1).astype(np.float32)
    large = max_exact + (np.log(df / np.float32(max_exact)) / np.float32(np.log(REL_MAX_DIST / max_exact))
                         * np.float32(REL_BUCKETS - max_exact)).astype(np.int32)
    large = np.minimum(large, REL_BUCKETS - 1)
    bucket = np.where(dc < max_exact, dc, large)
    return np.where(in_window, bucket, -1).astype(np.int32)


def _use_order():
    x, y, c = lax.axis_index("x"), lax.axis_index("y"), lax.axis_index("c")
    blocks = [(x, y, c), (x, y, 1 - c)]
    for px, py in [(1 - x, y), (x, 1 - y), (1 - x, 1 - y)]:
        blocks += [(px, py, c), (px, py, 1 - c)]
    return blocks


def _gather_in_proj(order, x, w_in_a, small, ln_gb):
    seq = x.shape[0]
    tm = min(TOKENS_IN_PROJ, seq)
    n_blocks = seq // tm
    shard_cols = w_in_a.shape[1]
    n = len(small) + 2

    def body(order_ref, x_ref, *refs):
        ins, refs = refs[:n], refs[n:]
        p_ref, outs, refs = refs[0], refs[1:n + 1], refs[n + 1:]
        wa_vmem, xb_vmem, stage = refs[0], refs[1], refs[2:n]
        send_sems, recv_sems, local_sems, out_sems = refs[n:]
        phase, blk = pl.program_id(0), pl.program_id(1)
        x_, y_, c_ = lax.axis_index("x"), lax.axis_index("y"), lax.axis_index("c")
        me, sibling = (x_, y_, c_), (x_, y_, 1 - c_)
        chips = [(1 - x_, y_), (x_, 1 - y_), (1 - x_, 1 - y_)]
        use = _use_order()

        def index(px, py, pc):
            return 4 * px + 2 * py + pc

        def slot(a, block):
            return (wa_vmem if a == 0 else outs[a]).at[index(*block)]

        def own(a):
            return slot(0, me) if a == 0 else (ins[a] if a == n - 1 else stage[a - 1])

        def copy(a, k, block, to, src=None):
            return pltpu.make_async_remote_copy(
                src_ref=slot(a, block) if src is None else src, dst_ref=slot(a, block),
                send_sem=send_sems.at[a, k], recv_sem=recv_sems.at[a, k], device_id=to, device_id_type=MESH)

        def keep(a):
            return pltpu.make_async_copy(own(a), slot(a, me), local_sems.at[a])

        def write_out(k):
            return pltpu.make_async_copy(slot(0, use[k]), outs[0].at[index(*use[k])], out_sems.at[k])

        def pass_on(a, j):
            copy(a, 1 + j, (*chips[j], c_), me).wait_recv()
            copy(a, 4 + j, (*chips[j], c_), sibling).start()

        def send(a, j):
            copy(a, 1 + j, me, (*chips[j], c_), own(a)).start()

        @pl.when((phase == 0) & (blk == 0))
        def _():
            wa_vmem[index(*me)] = ins[0][...].astype(BF16)
            for a in range(1, n - 1):
                stage[a - 1][...] = ins[a][...].astype(BF16)
            send(0, 0)
            send(0, 1)
            for a in range(n):
                copy(a, 0, me, sibling, own(a)).start()
                if a:
                    keep(a).start()
            write_out(0).start()
            copy(0, 0, sibling, me).wait_recv()
            write_out(1).start()

        for j in range(3):
            @pl.when((phase == 1 + j) & (blk == 0))
            def _(j=j):
                pass_on(0, j)
                write_out(2 + 2 * j).start()
                if j == 0:
                    send(0, 2)
                    for a in range(1, n):
                        for jj in range(3):
                            send(a, jj)
                copy(0, 4 + j, (*chips[j], 1 - c_), me).wait_recv()
                write_out(3 + 2 * j).start()
                if j == 2:
                    for a in range(1, n):
                        pass_on(a, 0)
                        pass_on(a, 1)

        rows = pl.ds(pl.multiple_of(blk * tm, tm), tm)

        @pl.when(phase == 0)
        def _():
            xb_vmem[rows, :] = x_ref[...].astype(BF16)

        chip = order_ref[phase]
        xb = xb_vmem[rows, :]
        p_ref[:, :shard_cols] = _nn(xb, wa_vmem[2 * chip])
        p_ref[:, shard_cols:] = _nn(xb, wa_vmem[2 * chip + 1])

        @pl.when((phase == N_DEV // 2 - 1) & (blk == n_blocks - 1))
        def _():
            for a in range(1, n):
                pass_on(a, 2)
            for a in range(1, n):
                copy(a, 0, sibling, me).wait_recv()
                for j, chip in enumerate(chips):
                    copy(a, 4 + j, (*chip, 1 - c_), me).wait_recv()
            for a in range(n):
                for k in range(7):
                    copy(a, k, me, me).wait_send()
                if a:
                    keep(a).wait()
            for k in range(N_DEV):
                write_out(k).wait()

    gathered = [jax.ShapeDtypeStruct((N_DEV,) + w_in_a.shape, BF16)]
    gathered += [jax.ShapeDtypeStruct((N_DEV,) + s.shape, BF16) for s in small]
    gathered.append(jax.ShapeDtypeStruct((N_DEV,) + ln_gb.shape, F32))
    return pl.pallas_call(
        body, name="gather_in_proj",
        out_shape=[jax.ShapeDtypeStruct((seq, N_DEV * shard_cols), F32)] + gathered,
        grid_spec=pltpu.PrefetchScalarGridSpec(
            num_scalar_prefetch=1, grid=(N_DEV // 2, n_blocks),
            in_specs=[pl.BlockSpec((tm, D_MODEL), lambda k, i, order: (jnp.where(k == 0, i, n_blocks - 1), 0))]
            + [VMEM] * n,
            out_specs=[pl.BlockSpec((tm, 2 * shard_cols), lambda k, i, order: (i, order[k]))] + [ANY] * n,
            scratch_shapes=[pltpu.VMEM((N_DEV,) + w_in_a.shape, BF16), pltpu.VMEM((seq, D_MODEL), BF16)]
            + [pltpu.VMEM(s.shape, BF16) for s in small] + [
                pltpu.SemaphoreType.DMA((n, 7)), pltpu.SemaphoreType.DMA((n, 7)), pltpu.SemaphoreType.DMA((n,)),
                pltpu.SemaphoreType.DMA((N_DEV,))]),
        compiler_params=pltpu.CompilerParams(dimension_semantics=("arbitrary", "arbitrary"),
                                             vmem_limit_bytes=VMEM_LIMIT_BYTES),
    )(order, x, w_in_a, *small, ln_gb)


def _bias_table(rel_bias, bucket):
    def body(rel_ref, bucket_ref, out_ref):
        bk = bucket_ref[...]
        col = lax.broadcasted_iota(jnp.int32, bk.shape, 1)
        for h in range(N_Q_HEADS):
            def step(b, acc):
                return jnp.where(bk == b, rel_ref[b, h], acc)
            tab = lax.fori_loop(0, REL_BUCKETS, step, jnp.full(bk.shape, NEG_INF, F32))
            pair = h // 2
            rows = slice((pair % PAIRS_PER_KV) * CHUNK, (pair % PAIRS_PER_KV + 1) * CHUNK)
            half = slice((h % 2) * 2 * CHUNK, (h % 2 + 1) * 2 * CHUNK)
            out_ref[1, pair // PAIRS_PER_KV, rows, half] = tab
            out_ref[0, pair // PAIRS_PER_KV, rows, half] = jnp.where(col < CHUNK, NEG_INF, tab)

    return pl.pallas_call(
        body, name="bias_table",
        out_shape=jax.ShapeDtypeStruct((2, N_KV_HEADS, PAIRS_PER_KV * CHUNK, 4 * CHUNK), F32),
        in_specs=[SMEM, VMEM], out_specs=VMEM,
    )(rel_bias, bucket)


def _rel_bias_grad(dlogits, bucket):
    def body(dl_ref, bucket_ref, out_ref):
        bk = bucket_ref[...]
        row = lax.broadcasted_iota(jnp.int32, (REL_BUCKETS, CHUNK), 0)
        lane = lax.broadcasted_iota(jnp.int32, (REL_BUCKETS, CHUNK), 1)

        def step(b, acc):
            for h in range(N_Q_HEADS):
                pair = h // 2
                dl = dl_ref[pair // PAIRS_PER_KV, (pair % PAIRS_PER_KV) * CHUNK:(pair % PAIRS_PER_KV + 1) * CHUNK,
                            (h % 2) * 2 * CHUNK:(h % 2 + 1) * 2 * CHUNK]
                s = jnp.sum(jnp.where(bk == b, dl, 0.0))
                acc = jnp.where((row == b) & (lane == h), s, acc)
            return acc

        out_ref[...] = lax.fori_loop(0, REL_BUCKETS, step, jnp.zeros((REL_BUCKETS, CHUNK), F32))

    return pl.pallas_call(
        body, name="rel_bias_grad", out_shape=jax.ShapeDtypeStruct((REL_BUCKETS, CHUNK), F32),
        in_specs=[VMEM, VMEM], out_specs=VMEM,
    )(dlogits, bucket)


def _masked_spatial(ws_ref, g):
    t = lax.broadcasted_iota(jnp.int32, (CHUNK, CHUNK), 0)
    s = lax.broadcasted_iota(jnp.int32, (CHUNK, CHUNK), 1)
    return jnp.where(s <= t, ws_ref[g], 0.0).astype(BF16)


def _spatial_gate(ws_ref, bsp_t_ref, vn, s_ref, n_chunks):
    for g in range(A_GROUPS):
        w = _masked_spatial(ws_ref, g)
        cols = slice(g * A_GROUP_DIM, (g + 1) * A_GROUP_DIM)
        for ci in range(n_chunks):
            rows = slice(ci * CHUNK, (ci + 1) * CHUNK)
            s_ref[rows, cols] = _nn(w, vn[rows, cols]) + bsp_t_ref[:, g:g + 1]


def _fwd_a(x, p, ln_gb, w_spatial, bsp_t, woa, later):
    seq = x.shape[0]
    tm = min(TOKENS_A_FWD, seq)
    n_chunks = tm // CHUNK
    n_blocks = seq // tm
    n = len(later)

    def body(x_ref, p_ref, ln_ref, ws_ref, bsp_ref, woa_ref, *refs):
        shards, refs = refs[:n], refs[n:]
        r0_ref, xb_ref, gathered, refs = refs[0], refs[1], refs[2:2 + n], refs[2 + n:]
        s_ref, stage, (send_sems, recv_sems, local_sems) = refs[0], refs[1:1 + n], refs[1 + n:]
        step = pl.program_id(0)

        def copies(a):
            x_, y_, c_ = lax.axis_index("x"), lax.axis_index("y"), lax.axis_index("c")
            mine = 4 * x_ + 2 * y_ + c_
            out = [pltpu.make_async_copy(stage[a], gathered[a].at[mine], local_sems.at[a])]
            for r in range(1, N_DEV):
                to = (1 - x_ if r & 4 else x_, 1 - y_ if r & 2 else y_, 1 - c_ if r & 1 else c_)
                out.append(pltpu.make_async_remote_copy(
                    src_ref=stage[a], dst_ref=gathered[a].at[mine], send_sem=send_sems.at[a, r - 1],
                    recv_sem=recv_sems.at[a, r - 1], device_id=to, device_id_type=MESH))
            return out

        @pl.when(step == 0)
        def _():
            for a in range(n):
                stage[a][...] = shards[a][...].astype(BF16)
                for cp in copies(a):
                    cp.start()

        xv = x_ref[...]
        xb_ref[...] = xv.astype(BF16)
        vhat, _ = _ln_normalize(p_ref[:, A_WIDTH:2 * A_WIDTH])
        vn = (vhat * ln_ref[0:1, :] + ln_ref[1:2, :]).astype(BF16)
        _spatial_gate(ws_ref, bsp_ref, vn, s_ref, n_chunks)
        z = p_ref[:, 2 * A_WIDTH:]
        y = p_ref[:, :A_WIDTH] * s_ref[...] * (z * jax.nn.sigmoid(z))
        r0_ref[...] = ALPHA * xv + _nn(y.astype(BF16), woa_ref[...])

        @pl.when(step == n_blocks - 1)
        def _():
            for a in range(n):
                for cp in copies(a):
                    cp.wait()

    return pl.pallas_call(
        body, name="fwd_a", grid=(n_blocks,),
        out_shape=[jax.ShapeDtypeStruct((seq, D_MODEL), F32), jax.ShapeDtypeStruct((seq, D_MODEL), BF16)]
        + [jax.ShapeDtypeStruct((N_DEV,) + s.shape, BF16) for s in later],
        in_specs=[_rows(tm, D_MODEL), _rows(tm, 3 * A_WIDTH), _resident(ln_gb.shape), _resident(w_spatial.shape),
                  _resident(bsp_t.shape), _resident(woa.shape)] + [VMEM] * n,
        out_specs=[_rows(tm, D_MODEL), _rows(tm, D_MODEL)] + [ANY] * n,
        scratch_shapes=[pltpu.VMEM((tm, A_WIDTH), F32)] + [pltpu.VMEM(s.shape, BF16) for s in later] + [
            pltpu.SemaphoreType.DMA((n, N_DEV - 1)), pltpu.SemaphoreType.DMA((n, N_DEV - 1)),
            pltpu.SemaphoreType.DMA((n,))],
        compiler_params=_seq_params(),
    )(x, p, ln_gb, w_spatial, bsp_t, woa, *later)


def _expand_heads(kv):
    lo = lax.broadcasted_iota(jnp.int32, kv.shape, 1) < HEAD_DIM
    swapped = pltpu.roll(kv, HEAD_DIM, 1)
    zero = jnp.zeros_like(kv)
    return (jnp.where(lo, kv, zero).astype(BF16), jnp.where(lo, zero, swapped).astype(BF16),
            jnp.where(lo, swapped, zero).astype(BF16), jnp.where(lo, zero, kv).astype(BF16))


def _fold_heads(acc, kvh):
    top, bot = acc[:2 * CHUNK], acc[2 * CHUNK:]
    lo = lax.broadcasted_iota(jnp.int32, top.shape, 1) < HEAD_DIM
    t = jnp.where(lo, top, 0.0)
    b = jnp.where(lo, 0.0, bot)
    if kvh == 0:
        return t + pltpu.roll(b, HEAD_DIM, 1)
    return pltpu.roll(t, HEAD_DIM, 1) + b


STACK = PAIRS_PER_KV * CHUNK


def _pair_cols(kvh, pp):
    pair = kvh * PAIRS_PER_KV + pp
    return slice(pair * CHUNK, (pair + 1) * CHUNK)


def _stack_pairs(ref_or_val, rows, kvh):
    return jnp.concatenate([ref_or_val[rows, _pair_cols(kvh, pp)] for pp in range(PAIRS_PER_KV)], axis=0)


def _sink_columns(sink_ref):
    blk = lax.broadcasted_iota(jnp.int32, (STACK, 1), 0) // CHUNK
    out = []
    for kvh in range(N_KV_HEADS):
        halves = []
        for half in range(2):
            col = jnp.zeros((STACK, 1), F32)
            for pp in range(PAIRS_PER_KV):
                col = jnp.where(blk == pp, sink_ref[0, 2 * (kvh * PAIRS_PER_KV + pp) + half], col)
            halves.append(col)
        out.append(halves)
    return out


def _softmax_stack(logits, sink_cols):
    ps, sinks = [], []
    for half in range(2):
        lg = logits[:, half * 2 * CHUNK:(half + 1) * 2 * CHUNK]
        sk = sink_cols[half]
        m = jnp.maximum(jnp.max(lg, axis=-1, keepdims=True), sk)
        e = jnp.exp(lg - m)
        es = jnp.exp(sk - m)
        inv = 1.0 / (jnp.sum(e, axis=-1, keepdims=True) + es)
        ps.append(e * inv)
        sinks.append(es * inv)
    return jnp.concatenate(ps, axis=1), sinks


def _band_operands(before_k, this_k, before_v, this_v, kvh):
    kblk = jnp.concatenate([before_k[2 * kvh], this_k[2 * kvh], before_k[2 * kvh + 1], this_k[2 * kvh + 1]], axis=0)
    vblk = jnp.concatenate([before_v[2 * kvh], this_v[2 * kvh], before_v[2 * kvh + 1], this_v[2 * kvh + 1]], axis=0)
    return kblk, vblk


def _fwd_b(r0, target, wb, wkv, wob, post_g, post_b, sinks, bias):
    seq = r0.shape[0]
    tm = min(TOKENS_B, seq)
    n_chunks = tm // CHUNK
    part_rows = min(PART_ROWS_B, tm)
    n_parts = tm // part_rows
    part_chunks = part_rows // CHUNK
    shard_cols = wb.shape[2]

    def body(r0_ref, tgt_ref, wb_ref, wkv_ref, wob_ref, pg_ref, pb_ref, sink_ref, bias_ref,
             q_ref, z_ref, kv_ref, dr1_ref, loss_ref, dg_ref, db_ref, ek_last, ev_last, *scratch):
        q_s, z_s, h1_s, o_s, ek_s, ev_s = (scratch[i * n_parts:(i + 1) * n_parts] for i in range(6))
        step = pl.program_id(0)

        @pl.when(step == 0)
        def _():
            ek_last[...] = jnp.zeros_like(ek_last)
            ev_last[...] = jnp.zeros_like(ev_last)
            loss_ref[...] = jnp.zeros_like(loss_ref)
            dg_ref[...] = jnp.zeros_like(dg_ref)
            db_ref[...] = jnp.zeros_like(db_ref)

        sink_cols = _sink_columns(sink_ref)
        g1 = pg_ref[1:2, :]

        def project_steps(part):
            rows = slice(part * part_rows, (part + 1) * part_rows)
            live = {}

            def normalize():
                xhat0, _ = _ln_normalize(r0_ref[rows, :])
                h1 = xhat0 * pg_ref[0:1, :] + pb_ref[0:1, :]
                h1_s[part][...] = h1
                live["h1b"] = h1.astype(BF16)

            def shard(k):
                def run():
                    qz = _nn(live["h1b"], wb_ref[k])
                    lo, hi = k * shard_cols, (k + 1) * shard_cols
                    if hi <= B_WIDTH:
                        qs = (qz * SCALE).astype(BF16)
                        q_ref[rows, lo:hi] = qs
                        q_s[part][:, lo:hi] = qs
                    else:
                        z_ref[rows, lo - B_WIDTH:hi - B_WIDTH] = qz
                        z_s[part][:, lo - B_WIDTH:hi - B_WIDTH] = qz
                return run

            def keys():
                kv = _nn(live["h1b"], wkv_ref[...])
                kv_ref[rows, :] = kv.astype(BF16)
                for cj in range(part_chunks):
                    sub = slice(cj * CHUNK, (cj + 1) * CHUNK)
                    for i, e in enumerate(_expand_heads(kv[sub, :KV_WIDTH])):
                        ek_s[part][cj, i] = e
                    for i, e in enumerate(_expand_heads(kv[sub, KV_WIDTH:])):
                        ev_s[part][cj, i] = e

            return [normalize, keys] + [shard(k) for k in range(N_DEV)]

        def attend_steps(part):
            def stack(cj, kvh):
                def run():
                    sub = slice(cj * CHUNK, (cj + 1) * CHUNK)
                    sel = jnp.where(step * n_chunks + part * part_chunks + cj > 0, 1, 0)
                    if cj:
                        before_k, before_v = ek_s[part].at[cj - 1], ev_s[part].at[cj - 1]
                    elif part:
                        before_k, before_v = ek_s[part - 1].at[part_chunks - 1], ev_s[part - 1].at[part_chunks - 1]
                    else:
                        before_k, before_v = ek_last, ev_last
                    kblk, vblk = _band_operands(before_k, ek_s[part].at[cj], before_v, ev_s[part].at[cj], kvh)
                    logits = _nt(_stack_pairs(q_s[part], sub, kvh), kblk) + bias_ref[sel, kvh]
                    p, _ = _softmax_stack(logits, sink_cols[kvh])
                    o = _nn(p.astype(BF16), vblk)
                    for pp in range(PAIRS_PER_KV):
                        o_s[part][sub, _pair_cols(kvh, pp)] = o[pp * CHUNK:(pp + 1) * CHUNK]
                return run

            def finish():
                rows = slice(part * part_rows, (part + 1) * part_rows)
                z = z_s[part][...]
                y = o_s[part][...] * (z * jax.nn.sigmoid(z))
                r1 = ALPHA * h1_s[part][...] + _nn(y.astype(BF16), wob_ref[...])
                xhat1, rstd1 = _ln_normalize(r1)
                diff = xhat1 * g1 + pb_ref[1:2, :] - tgt_ref[rows, :]
                loss_ref[...] += 0.5 * jnp.sum(jnp.mean(diff * diff, axis=-1, keepdims=True), axis=0, keepdims=True)
                dh2 = diff * (1.0 / D_MODEL)
                dg_ref[...] += jnp.sum(dh2 * xhat1, axis=0, keepdims=True)
                db_ref[...] += jnp.sum(dh2, axis=0, keepdims=True)
                dr1_ref[rows, :] = _ln_backward(dh2 * g1, xhat1, rstd1)

            return [stack(cj, kvh) for cj in range(part_chunks) for kvh in range(N_KV_HEADS)] + [finish]

        for run in project_steps(0):
            run()
        for part in range(n_parts):
            for run in _in_turns(attend_steps(part), project_steps(part + 1) if part + 1 < n_parts else []):
                run()
        ek_last[...] = ek_s[n_parts - 1][part_chunks - 1]
        ev_last[...] = ev_s[n_parts - 1][part_chunks - 1]

    vec = pl.BlockSpec((1, D_MODEL), lambda i: (0, 0))
    return pl.pallas_call(
        body, name="fwd_b", grid=(seq // tm,),
        out_shape=[jax.ShapeDtypeStruct((seq, B_WIDTH), BF16), jax.ShapeDtypeStruct((seq, B_WIDTH), F32),
                   jax.ShapeDtypeStruct((seq, 2 * KV_WIDTH), BF16), jax.ShapeDtypeStruct((seq, D_MODEL), F32),
                   jax.ShapeDtypeStruct((1, 1), F32), jax.ShapeDtypeStruct((1, D_MODEL), F32),
                   jax.ShapeDtypeStruct((1, D_MODEL), F32)],
        in_specs=[_rows(tm, D_MODEL), _rows(tm, D_MODEL), _resident(wb.shape), _resident(wkv.shape),
                  _resident(wob.shape), _resident(post_g.shape), _resident(post_b.shape), SMEM,
                  _resident(bias.shape)],
        out_specs=[_rows(tm, B_WIDTH), _rows(tm, B_WIDTH), _rows(tm, 2 * KV_WIDTH), _rows(tm, D_MODEL),
                   pl.BlockSpec((1, 1), lambda i: (0, 0)), vec, vec],
        scratch_shapes=[pltpu.VMEM((4, CHUNK, CHUNK), BF16), pltpu.VMEM((4, CHUNK, CHUNK), BF16)]
        + [pltpu.VMEM((part_rows, B_WIDTH), BF16)] * n_parts + [pltpu.VMEM((part_rows, B_WIDTH), F32)] * n_parts
        + [pltpu.VMEM((part_rows, D_MODEL), F32)] * n_parts + [pltpu.VMEM((part_rows, B_WIDTH), F32)] * n_parts
        + [pltpu.VMEM((part_chunks, 4, CHUNK, CHUNK), BF16)] * (2 * n_parts),
        compiler_params=_seq_params(),
    )(r0, target, wb, wkv, wob, post_g, post_b, sinks, bias)


def _bwd_b(r0, dr1, q, z, kv, wb, wkv, wob, post_g, post_b, sinks, bias):
    seq = r0.shape[0]
    tm = min(TOKENS_B_BWD, seq)
    n_chunks = tm // CHUNK
    n_blocks = seq // tm
    shard_cols = wb.shape[2]

    def body(r0_ref, dr1_ref, q_ref, z_ref, kv_ref, kvp_ref, wb_ref, wkv_ref, wob_ref, pg_ref, pb_ref, sink_ref,
             bias_ref, dr0_ref, dwb_out, dwkv_out, dwob_out, dsink_ref, dl_out, dpost_ref,
             dwb_acc, dwkv_acc, dwob_acc, dl_acc, dsink_acc, carry_ref, ek_ref, ev_ref, y_ref, dqz_ref, dkv_ref):
        step = pl.program_id(0)
        block = n_blocks - 1 - step

        @pl.when(step == 0)
        def _():
            dpost_ref[...] = jnp.zeros_like(dpost_ref)
            dwb_acc[...] = jnp.zeros_like(dwb_acc)
            dwkv_acc[...] = jnp.zeros_like(dwkv_acc)
            dwob_acc[...] = jnp.zeros_like(dwob_acc)
            dl_acc[...] = jnp.zeros_like(dl_acc)
            carry_ref[...] = jnp.zeros_like(carry_ref)
            dsink_acc[...] = jnp.zeros_like(dsink_acc)

        xhat0, rstd0 = _ln_normalize(r0_ref[...])
        h1b = (xhat0 * pg_ref[0:1, :] + pb_ref[0:1, :]).astype(BF16)
        dr1 = dr1_ref[...]
        dr1b = dr1.astype(BF16)
        dy = _nt(dr1b, wob_ref[...])
        sg, dsg = _silu_and_grad(z_ref[...])
        do_all = dy * sg
        dyz = dy * dsg

        for ci in range(n_chunks + 1):
            src = kvp_ref[...] if ci == 0 else kv_ref[(ci - 1) * CHUNK:ci * CHUNK, :]
            src = src.astype(F32)
            for i, e in enumerate(_expand_heads(src[:, :KV_WIDTH])):
                ek_ref[ci, i] = e
            for i, e in enumerate(_expand_heads(src[:, KV_WIDTH:])):
                ev_ref[ci, i] = e

        sink_cols = _sink_columns(sink_ref)
        for ci in reversed(range(n_chunks)):
            rows = slice(ci * CHUNK, (ci + 1) * CHUNK)
            sel = jnp.where(block * n_chunks + ci > 0, 1, 0)
            dk_band = jnp.zeros((2 * CHUNK, KV_WIDTH), F32)
            dv_band = jnp.zeros((2 * CHUNK, KV_WIDTH), F32)
            for kvh in range(N_KV_HEADS):
                kblk, vblk = _band_operands(ek_ref.at[ci], ek_ref.at[ci + 1], ev_ref.at[ci], ev_ref.at[ci + 1], kvh)
                qs = _stack_pairs(q_ref, rows, kvh)
                p, psink = _softmax_stack(_nt(qs, kblk) + bias_ref[sel, kvh], sink_cols[kvh])
                pb16 = p.astype(BF16)
                o = _nn(pb16, vblk)
                y = (o * _stack_pairs(sg, rows, kvh)).astype(BF16)
                dz = (o * _stack_pairs(dyz, rows, kvh)).astype(BF16)
                dob = _stack_pairs(do_all, rows, kvh).astype(BF16)
                dp = _nt(dob, vblk)
                dlogits = []
                for half in range(2):
                    hs = slice(half * 2 * CHUNK, (half + 1) * 2 * CHUNK)
                    delta = jnp.sum(dp[:, hs] * p[:, hs], axis=-1, keepdims=True)
                    dlogits.append(p[:, hs] * (dp[:, hs] - delta))
                    dsink_acc[2 * kvh + half] += psink[half] * delta
                dlog = jnp.concatenate(dlogits, axis=1)
                dl_acc[kvh] += dlog
                dsc = dlog.astype(BF16)
                dq = (_nn(dsc, kblk) * SCALE).astype(BF16)
                for pp in range(PAIRS_PER_KV):
                    blk = slice(pp * CHUNK, (pp + 1) * CHUNK)
                    cols = _pair_cols(kvh, pp)
                    y_ref[rows, cols] = y[blk]
                    dqz_ref[rows, cols] = dq[blk]
                    dqz_ref[rows, B_WIDTH + cols.start:B_WIDTH + cols.stop] = dz[blk]
                dk_band = dk_band + _fold_heads(_tn(dsc, qs), kvh)
                dv_band = dv_band + _fold_heads(_tn(pb16, dob), kvh)
            band = jnp.concatenate([dk_band, dv_band], axis=1)
            dkv_ref[rows, :] = (band[CHUNK:] + carry_ref[...]).astype(BF16)
            carry_ref[...] = band[:CHUNK]

        dqz = dqz_ref[...]
        dkv = dkv_ref[...]
        dh1 = ALPHA * dr1 + _nt(dkv, wkv_ref[...])
        for k in range(N_DEV):
            part = dqz[:, k * shard_cols:(k + 1) * shard_cols]
            dh1 = dh1 + _nt(part, wb_ref[k])
            dwb_acc[k] += _tn(h1b, part)
        dpost_ref[0:1, :] += jnp.sum(dh1 * xhat0, axis=0, keepdims=True)
        dpost_ref[1:2, :] += jnp.sum(dh1, axis=0, keepdims=True)
        dr0_ref[...] = _ln_backward(dh1 * pg_ref[0:1, :], xhat0, rstd0)
        dwkv_acc[...] += _tn(h1b, dkv)
        dwob_acc[...] += _tn(y_ref[...], dr1b)

        @pl.when(step == n_blocks - 1)
        def _():
            pltpu.sync_copy(dwb_acc, dwb_out)
            pltpu.sync_copy(dwkv_acc, dwkv_out)
            pltpu.sync_copy(dwob_acc, dwob_out)
            pltpu.sync_copy(dl_acc, dl_out)
            lane = lax.broadcasted_iota(jnp.int32, (1, CHUNK), 1)
            dsink = jnp.zeros((1, CHUNK), F32)
            for kvh in range(N_KV_HEADS):
                for half in range(2):
                    for pp in range(PAIRS_PER_KV):
                        head = 2 * (kvh * PAIRS_PER_KV + pp) + half
                        tot = jnp.sum(dsink_acc[2 * kvh + half, pp * CHUNK:(pp + 1) * CHUNK, :], axis=0, keepdims=True)
                        dsink = dsink - jnp.where(lane == head, tot, 0.0)
            dsink_ref[...] = dsink

    rev = functools.partial(_rows, tm, reverse_of=n_blocks)
    prev_chunk = pl.BlockSpec(
        (CHUNK, 2 * KV_WIDTH), lambda i: (jnp.maximum((n_blocks - 1 - i) * n_chunks - 1, 0), 0))
    return pl.pallas_call(
        body, name="bwd_b", grid=(n_blocks,),
        out_shape=[jax.ShapeDtypeStruct((seq, D_MODEL), F32), jax.ShapeDtypeStruct(wb.shape, F32),
                   jax.ShapeDtypeStruct(wkv.shape, F32), jax.ShapeDtypeStruct(wob.shape, F32),
                   jax.ShapeDtypeStruct((1, CHUNK), F32),
                   jax.ShapeDtypeStruct((N_KV_HEADS, STACK, 4 * CHUNK), F32), jax.ShapeDtypeStruct((2, D_MODEL), F32)],
        in_specs=[rev(D_MODEL), rev(D_MODEL), rev(B_WIDTH), rev(B_WIDTH), rev(2 * KV_WIDTH), prev_chunk,
                  _resident(wb.shape), _resident(wkv.shape), _resident(wob.shape), _resident(post_g.shape),
                  _resident(post_b.shape), SMEM, _resident(bias.shape)],
        out_specs=[rev(D_MODEL), ANY, ANY, ANY, pl.BlockSpec((1, CHUNK), lambda i: (0, 0)), ANY,
                   pl.BlockSpec((2, D_MODEL), lambda i: (0, 0))],
        scratch_shapes=[pltpu.VMEM(wb.shape, F32), pltpu.VMEM(wkv.shape, F32), pltpu.VMEM(wob.shape, F32),
                        pltpu.VMEM((N_KV_HEADS, STACK, 4 * CHUNK), F32), pltpu.VMEM((2 * N_KV_HEADS, STACK, 1), F32),
                        pltpu.VMEM((CHUNK, 2 * KV_WIDTH), F32),
                        pltpu.VMEM((n_chunks + 1, 4, CHUNK, CHUNK), BF16),
                        pltpu.VMEM((n_chunks + 1, 4, CHUNK, CHUNK), BF16),
                        pltpu.VMEM((tm, B_WIDTH), BF16), pltpu.VMEM((tm, 2 * B_WIDTH), BF16),
                        pltpu.VMEM((tm, 2 * KV_WIDTH), BF16)],
        compiler_params=_seq_params(),
    )(r0, dr1, q, z, kv, kv, wb, wkv, wob, post_g, post_b, sinks, bias)


def _bwd_a(dr0, p, ln_gb, w_spatial, bsp_t, woa, exchange):
    seq = dr0.shape[0]
    tm = min(TOKENS_A, seq)
    n_chunks = tm // CHUNK
    n_blocks = seq // tm

    def body(dr0_ref, p_ref, ln_ref, ws_ref, bsp_ref, woa_ref, *refs):
        ex_in, refs = refs[:exchange.n], refs[exchange.n:]
        (dp_ref, dwoa_out, dws_ref, dbsp_ref, dln_ref), refs = refs[:5], refs[5:]
        ex_out, refs = refs[:exchange.n], refs[exchange.n:]
        (dwoa_acc, s_ref, ds_ref, dvn_ref), ex_sems = refs[:4], refs[4:]
        step = pl.program_id(0)

        @pl.when(step == 0)
        def _():
            exchange.start(ex_in, ex_out, ex_sems)
            dwoa_acc[...] = jnp.zeros_like(dwoa_acc)
            dws_ref[...] = jnp.zeros_like(dws_ref)
            dbsp_ref[...] = jnp.zeros_like(dbsp_ref)
            dln_ref[...] = jnp.zeros_like(dln_ref)

        dr0b = dr0_ref[...].astype(BF16)

        vhat, rstd_v = _ln_normalize(p_ref[:, A_WIDTH:2 * A_WIDTH])
        ln_g = ln_ref[0:1, :]
        vn = (vhat * ln_g + ln_ref[1:2, :]).astype(BF16)
        _spatial_gate(ws_ref, bsp_ref, vn, s_ref, n_chunks)
        u = p_ref[:, :A_WIDTH]
        s = s_ref[...]
        sg, dsg = _silu_and_grad(p_ref[:, 2 * A_WIDTH:])
        dwoa_acc[...] += _tn((u * s * sg).astype(BF16), dr0b)
        dy = _nt(dr0b, woa_ref[...])
        dp_ref[:, :A_WIDTH] = (dy * s * sg).astype(BF16)
        dp_ref[:, 2 * A_WIDTH:] = (dy * u * s * dsg).astype(BF16)
        ds_ref[...] = dy * u * sg

        lane = lax.broadcasted_iota(jnp.int32, (CHUNK, CHUNK), 1)
        dbsp = jnp.zeros((CHUNK, CHUNK), F32)
        for g in range(A_GROUPS):
            w = _masked_spatial(ws_ref, g)
            cols = slice(g * A_GROUP_DIM, (g + 1) * A_GROUP_DIM)
            dws = jnp.zeros((CHUNK, CHUNK), F32)
            for ci in range(n_chunks):
                rows = slice(ci * CHUNK, (ci + 1) * CHUNK)
                ds = ds_ref[rows, cols]
                dsb = ds.astype(BF16)
                dvn_ref[rows, cols] = _tn(w, dsb)
                dws = dws + _nt(dsb, vn[rows, cols])
                dbsp = dbsp + jnp.where(lane == g, jnp.sum(ds, axis=-1, keepdims=True), 0.0)
            dws_ref[g] += dws
        dbsp_ref[...] += dbsp

        dvn = dvn_ref[...]
        dln_ref[0:1, :] += jnp.sum(dvn * vhat, axis=0, keepdims=True)
        dln_ref[1:2, :] += jnp.sum(dvn, axis=0, keepdims=True)
        dp_ref[:, A_WIDTH:2 * A_WIDTH] = _ln_backward(dvn * ln_g, vhat, rstd_v).astype(BF16)

        @pl.when(step == n_blocks - 1)
        def _():
            t = lax.broadcasted_iota(jnp.int32, (CHUNK, CHUNK), 0)
            for g in range(A_GROUPS):
                dws_ref[g] = jnp.where(lane <= t, dws_ref[g], 0.0)
            pltpu.sync_copy(dwoa_acc, dwoa_out)
            exchange.wait(ex_in, ex_out, ex_sems)

    def fixed(shape):
        nd = len(shape)
        return pl.BlockSpec(shape, lambda i: (0,) * nd)

    return pl.pallas_call(
        body, name="bwd_a", grid=(n_blocks,),
        out_shape=[jax.ShapeDtypeStruct((seq, 3 * A_WIDTH), BF16),
                   jax.ShapeDtypeStruct(woa.shape, F32), jax.ShapeDtypeStruct(w_spatial.shape, F32),
                   jax.ShapeDtypeStruct((CHUNK, CHUNK), F32), jax.ShapeDtypeStruct((2, A_WIDTH), F32)]
        + exchange.out_shape,
        in_specs=[_rows(tm, D_MODEL), _rows(tm, 3 * A_WIDTH), _resident(ln_gb.shape),
                  _resident(w_spatial.shape), _resident(bsp_t.shape), _resident(woa.shape)] + [ANY] * exchange.n,
        out_specs=[_rows(tm, 3 * A_WIDTH), ANY, fixed(w_spatial.shape),
                   fixed((CHUNK, CHUNK)), fixed((2, A_WIDTH))] + [ANY] * exchange.n,
        scratch_shapes=[pltpu.VMEM(woa.shape, F32), pltpu.VMEM((tm, A_WIDTH), F32),
                        pltpu.VMEM((tm, A_WIDTH), F32), pltpu.VMEM((tm, A_WIDTH), F32)] + exchange.scratch_shapes,
        compiler_params=_seq_params(),
    )(dr0, p, ln_gb, w_spatial, bsp_t, woa, *exchange.inputs)


def _input_grad_blocks(seq):
    tm = min(TOKENS_DX, seq // 8)
    return tm, seq // tm


def _input_grad(dp, dr0, wa, first, n_blocks, earlier, exchange):
    seq = dr0.shape[0]
    tm, _ = _input_grad_blocks(seq)
    shard_cols = wa.shape[2]
    n_own = 3 if earlier is None else 4

    def body(dp_ref, dr0_ref, wa_ref, *refs):
        refs = refs[n_own - 3:]
        ex_in, dx_ref, ex_out, ex_sems = (refs[:exchange.n], refs[exchange.n], refs[exchange.n + 1:2 * exchange.n + 1],
                                          refs[2 * exchange.n + 1:])
        step = pl.program_id(0)

        @pl.when(step == 0)
        def _():
            exchange.start(ex_in, ex_out, ex_sems)

        dx = ALPHA * dr0_ref[...]
        for k in range(N_DEV):
            dx = dx + _nt(dp_ref[:, k * shard_cols:(k + 1) * shard_cols], wa_ref[k])
        dx_ref[...] = dx

        @pl.when(step == n_blocks - 1)
        def _():
            exchange.wait(ex_in, ex_out, ex_sems)

    def rows(width):
        return pl.BlockSpec((tm, width), lambda i: (first + i, 0))

    return pl.pallas_call(
        body, name=f"input_grad_{first}", grid=(n_blocks,),
        out_shape=[jax.ShapeDtypeStruct((seq, D_MODEL), F32)] + exchange.out_shape,
        in_specs=[rows(3 * A_WIDTH), rows(D_MODEL), _resident(wa.shape)] + [ANY] * (n_own - 3 + exchange.n),
        out_specs=[rows(D_MODEL)] + [ANY] * exchange.n,
        scratch_shapes=exchange.scratch_shapes,
        input_output_aliases={} if earlier is None else {3: 0},
        compiler_params=_seq_params(),
    )(dp, dr0, wa, *(() if earlier is None else (earlier,)), *exchange.inputs)


def _in_proj_weight_grad(xb, dp, shard_cols, exchange):
    seq = xb.shape[0]
    tk = min(TOKENS_WGRAD, seq)
    n_k = seq // tk

    def body(x_ref, dp_ref, *refs):
        ex_in, out_ref, ex_out, ex_sems = (refs[:exchange.n], refs[exchange.n], refs[exchange.n + 1:2 * exchange.n + 1],
                                           refs[2 * exchange.n + 1:])
        shard, step = pl.program_id(0), pl.program_id(1)

        @pl.when((shard == 0) & (step == 0))
        def _():
            exchange.start(ex_in, ex_out, ex_sems)

        @pl.when(step == 0)
        def _():
            out_ref[...] = jnp.zeros_like(out_ref)

        out_ref[...] += _tn(x_ref[...], dp_ref[...])

        @pl.when((shard == N_DEV - 1) & (step == n_k - 1))
        def _():
            exchange.wait(ex_in, ex_out, ex_sems)

    return pl.pallas_call(
        body, name="in_proj_weight_grad", grid=(N_DEV, n_k),
        out_shape=[jax.ShapeDtypeStruct((N_DEV, D_MODEL, shard_cols), F32)] + exchange.out_shape,
        in_specs=[pl.BlockSpec((tk, D_MODEL), lambda j, k: (k, 0)),
                  pl.BlockSpec((tk, shard_cols), lambda j, k: (k, j))] + [ANY] * exchange.n,
        out_specs=[pl.BlockSpec((None, D_MODEL, shard_cols), lambda j, k: (j, 0, 0))] + [ANY] * exchange.n,
        scratch_shapes=exchange.scratch_shapes,
        compiler_params=pltpu.CompilerParams(dimension_semantics=("arbitrary", "arbitrary"),
                                             vmem_limit_bytes=VMEM_LIMIT_BYTES),
    )(xb, dp, *exchange.inputs)


class _Exchange:
    def __init__(self, pair=(), chip=()):
        self.pair, self.chip = tuple(pair), tuple(chip)
        self.inputs = self.pair + self.chip
        self.n = len(self.inputs)
        self.out_shape = [jax.ShapeDtypeStruct((4, 1) + g.shape[2:], F32) for g in self.pair]
        self.out_shape += [jax.ShapeDtypeStruct((3,) + s.shape[1:], s.dtype) for s in self.chip]
        self.scratch_shapes = [pltpu.SemaphoreType.DMA((max(len(self.pair), 1),)) for _ in range(2)]
        self.scratch_shapes += [pltpu.SemaphoreType.DMA((max(len(self.chip), 1), 3)) for _ in range(2)]

    def _copies(self, ins, outs, sems):
        if not self.n:
            return []
        pair_send, pair_recv, chip_send, chip_recv = sems
        x, y, c = lax.axis_index("x"), lax.axis_index("y"), lax.axis_index("c")
        copies = []
        for a in range(len(self.pair)):
            copies.append(pltpu.make_async_remote_copy(
                src_ref=ins[a].at[:, pl.ds(1 - c, 1)], dst_ref=outs[a],
                send_sem=pair_send.at[a], recv_sem=pair_recv.at[a], device_id=(x, y, 1 - c), device_id_type=MESH))
        for a in range(len(self.chip)):
            src, dst = ins[len(self.pair) + a], outs[len(self.pair) + a]
            for k, (px, py) in enumerate([(1 - x, y), (x, 1 - y), (1 - x, 1 - y)]):
                copies.append(pltpu.make_async_remote_copy(
                    src_ref=src.at[2 * px + py], dst_ref=dst.at[k], send_sem=chip_send.at[a, k],
                    recv_sem=chip_recv.at[a, k], device_id=(px, py, c), device_id_type=MESH))
        return copies

    def start(self, ins, outs, sems):
        for cp in self._copies(ins, outs, sems):
            cp.start()

    def wait(self, ins, outs, sems):
        for cp in self._copies(ins, outs, sems):
            cp.wait()


def _row_tile(rows, cols, itemsize=4):
    if rows * cols * itemsize <= (1 << 20):
        return rows
    t = 8
    while 2 * t * cols * itemsize <= (1 << 20) and rows % (2 * t) == 0:
        t *= 2
    assert rows % t == 0, (rows, cols)
    return t


def _pair_sum(place, grad, landed, dtype):
    _, _, rows, cols = grad.shape
    tr = _row_tile(rows, cols)

    def body(place_ref, a_ref, b_ref, out_ref):
        out_ref[...] = (a_ref[...] + b_ref[...]).astype(dtype)

    return pl.pallas_call(
        body, name="pair_sum", out_shape=jax.ShapeDtypeStruct((4, rows, cols), dtype),
        grid_spec=pltpu.PrefetchScalarGridSpec(
            num_scalar_prefetch=1, grid=(4, rows // tr),
            in_specs=[pl.BlockSpec((None, None, tr, cols), lambda k, i, pr: (k, pr[1], i, 0)),
                      pl.BlockSpec((None, None, tr, cols), lambda k, i, pr: (k, 0, i, 0))],
            out_specs=pl.BlockSpec((None, tr, cols), lambda k, i, pr: (k, i, 0))),
        compiler_params=pltpu.CompilerParams(dimension_semantics=("parallel", "parallel")),
    )(place, grad, landed)


def _adamw_math(w, g, m, v):
    m = ADAM_B1 * m + (1.0 - ADAM_B1) * g
    v = ADAM_B2 * v + (1.0 - ADAM_B2) * (g * g)
    m_hat = m / (1.0 - ADAM_B1 ** ADAM_STEP)
    v_hat = v / (1.0 - ADAM_B2 ** ADAM_STEP)
    delta = -ADAM_LR * (m_hat / (jnp.sqrt(v_hat) + ADAM_EPS) + ADAM_WD * w)
    return delta, m, v


def _adamw_shard(place, grad, landed, got, w, m, v):
    rows, cols = w.shape
    tr = _row_tile(rows, cols)

    def body(place_ref, a_ref, b_ref, got_ref, w_ref, m_ref, v_ref, g_out, d_out, m_out, v_out):
        g = a_ref[...] + b_ref[...]
        for k in range(3):
            g = g + got_ref[k].astype(F32)
        g_out[...] = g
        d_out[...], m_out[...], v_out[...] = _adamw_math(w_ref[...], g, m_ref[...], v_ref[...])

    spec = pl.BlockSpec((tr, cols), lambda i, pr: (i, 0))
    out = jax.ShapeDtypeStruct((rows, cols), F32)
    return pl.pallas_call(
        body, name="adamw_shard", out_shape=[out] * 4,
        grid_spec=pltpu.PrefetchScalarGridSpec(
            num_scalar_prefetch=1, grid=(rows // tr,),
            in_specs=[pl.BlockSpec((None, None, tr, cols), lambda i, pr: (pr[0], pr[1], i, 0)),
                      pl.BlockSpec((None, None, tr, cols), lambda i, pr: (pr[0], 0, i, 0)),
                      pl.BlockSpec((3, tr, cols), lambda i, pr: (0, i, 0)), spec, spec, spec],
            out_specs=[spec] * 4),
        compiler_params=pltpu.CompilerParams(dimension_semantics=("parallel",)),
    )(place, grad, landed, got, w, m, v)


def _gather_small(grad, landed, got):
    rows, cols = got.shape[1:]

    def body(grad_ref, landed_ref, got_ref, out_ref, send_sems, recv_sems):
        x, y, c = lax.axis_index("x"), lax.axis_index("y"), lax.axis_index("c")
        me, sibling = (x, y, c), (x, y, 1 - c)
        chips = [(1 - x, y), (x, 1 - y), (1 - x, 1 - y)]

        def slot(px, py, pc):
            return out_ref.at[4 * px + 2 * py + pc]

        def copy(k, block, to):
            return pltpu.make_async_remote_copy(
                src_ref=slot(*block), dst_ref=slot(*block), send_sem=send_sems.at[k], recv_sem=recv_sems.at[k],
                device_id=to, device_id_type=MESH)

        mine = grad_ref[2 * x + y, c] + landed_ref[2 * x + y, 0]
        out_ref[4 * x + 2 * y + c] = ((mine + got_ref[0]) + got_ref[1]) + got_ref[2]
        first = [copy(0, me, sibling)] + [copy(1 + j, me, (*chip, c)) for j, chip in enumerate(chips)]
        for cp in first:
            cp.start()
        passed = [copy(4 + j, (*chip, c), sibling) for j, chip in enumerate(chips)]
        for j, chip in enumerate(chips):
            copy(1 + j, (*chip, c), me).wait_recv()
            passed[j].start()
        copy(0, sibling, me).wait_recv()
        for j, chip in enumerate(chips):
            copy(4 + j, (*chip, 1 - c), me).wait_recv()
        for cp in first + passed:
            cp.wait_send()

    return pl.pallas_call(
        body, name="gather_small", out_shape=jax.ShapeDtypeStruct((N_DEV, rows, cols), F32),
        in_specs=[VMEM] * 3, out_specs=VMEM,
        scratch_shapes=[pltpu.SemaphoreType.DMA((7,)), pltpu.SemaphoreType.DMA((7,))],
    )(grad, landed, got)


def _adamw_small(params):
    n = len(params)

    def body(*refs):
        ins, outs = refs[:4 * n], refs[4 * n:]
        for i in range(n):
            w_ref, g_ref, m_ref, v_ref = ins[4 * i:4 * i + 4]
            d, m, v = _adamw_math(w_ref[...], g_ref[...], m_ref[...], v_ref[...])
            outs[3 * i][...] = d
            outs[3 * i + 1][...] = m
            outs[3 * i + 2][...] = v

    flat = [a for p in params for a in p]
    out_shape = [jax.ShapeDtypeStruct(p[0].shape, F32) for p in params for _ in range(3)]
    outs = pl.pallas_call(
        body, name="adamw_small", out_shape=out_shape, in_specs=[VMEM] * (4 * n), out_specs=[VMEM] * (3 * n),
    )(*flat)
    return [tuple(outs[3 * i:3 * i + 3]) for i in range(n)]


def _pack_small(parts):
    rows = []
    for a in parts:
        r, c = a.shape
        rows.append(jnp.pad(a, ((0, (-r) % 8), (0, CHUNK - c))))
    flat = jnp.concatenate(rows, axis=0)
    return jnp.pad(flat, ((0, N_DEV * SMALL_ROWS - flat.shape[0]), (0, 0))).reshape(N_DEV, SMALL_ROWS, CHUNK)


def _unpack_small(flat, shapes):
    out, at = [], 0
    for r, c in shapes:
        out.append(flat[at:at + r, :c])
        at += r + (-r) % 8
    return out


def kernel(x, w_in_a, sgu_ln_g, sgu_ln_b, w_spatial, b_spatial, w_out_a, w_kv, w_in_b, attn_sinks, rel_bias, w_out_b, post_ln_g, post_ln_b, loss_target, m_w_in_a, m_sgu_ln_g, m_sgu_ln_b, m_w_spatial, m_b_spatial, m_w_out_a, m_w_kv, m_w_in_b, m_attn_sinks, m_rel_bias, m_w_out_b, m_post_ln_g, m_post_ln_b, v_w_in_a, v_sgu_ln_g, v_sgu_ln_b, v_w_spatial, v_b_spatial, v_w_out_a, v_w_kv, v_w_in_b, v_attn_sinks, v_rel_bias, v_w_out_b, v_post_ln_g, v_post_ln_b):
    seq = x.shape[1]
    me = 4 * lax.axis_index("x") + 2 * lax.axis_index("y") + lax.axis_index("c")
    bucket = jnp.asarray(_band_tables())

    ln_shard = jnp.concatenate([sgu_ln_g, sgu_ln_b], axis=0)
    x2, tgt = x[0], loss_target[0]
    order = jnp.stack([2 * px + py for px, py, _ in _use_order()[::2]]).astype(jnp.int32)
    p, wa, woa, ln_all = _gather_in_proj(order, x2, w_in_a[0], (w_out_a[0],), ln_shard)
    woa = woa.reshape(A_WIDTH, D_MODEL)
    ln_gb = jnp.transpose(ln_all, (1, 0, 2)).reshape(2, A_WIDTH)
    ws = w_spatial[0]
    bsp_t = jnp.pad(b_spatial[0].T, ((0, 0), (0, CHUNK - A_GROUPS)))
    bias = _bias_table(rel_bias, bucket)

    r0, xb, wkv, wb, wob = _fwd_a(x2, p, ln_gb, ws, bsp_t, woa, (w_kv, w_in_b[0], w_out_b[0]))
    wkv = wkv.reshape(D_MODEL, 2 * KV_WIDTH)
    wob = wob.reshape(B_WIDTH, D_MODEL)
    q, z, kv, dr1, loss, dg1, db1 = _fwd_b(r0, tgt, wb, wkv, wob, post_ln_g, post_ln_b, attn_sinks, bias)
    dr0, dwb, dwkv, dwob, dsink, dlog, dpost0 = _bwd_b(r0, dr1, q, z, kv, wb, wkv, wob, post_ln_g, post_ln_b,
                                                       attn_sinks, bias)
    drel = _rel_bias_grad(dlog, bucket)

    place = jnp.stack([2 * lax.axis_index("x") + lax.axis_index("y"), lax.axis_index("c")]).astype(jnp.int32)

    def by_device(g):
        return g.reshape((4, 2, g.shape[0] // N_DEV) + g.shape[1:]) if g.ndim == 2 else g.reshape((4, 2) + g.shape[1:])

    dwb, dwkv, dwob = by_device(dwb), by_device(dwkv), by_device(dwob)
    dp, dwoa, dws, dbsp_t, dln, l_wb, l_wkv, l_wob = _bwd_a(dr0, p, ln_gb, ws, bsp_t, woa,
                                                            _Exchange(pair=(dwb, dwkv, dwob)))
    dwoa = by_device(dwoa)
    small_shapes = [(A_GROUPS * CHUNK, CHUNK), (A_GROUPS, CHUNK), (1, N_Q_HEADS), (REL_BUCKETS, N_Q_HEADS),
                    (16, CHUNK), (16, CHUNK), (16, CHUNK), (16, CHUNK), (1, 1)]
    small = by_device(_pack_small([
        dws.reshape(A_GROUPS * CHUNK, CHUNK), dbsp_t[:, :A_GROUPS].T, dsink[:, :N_Q_HEADS], drel[:, :N_Q_HEADS],
        jnp.concatenate([dpost0[0:1], dg1], axis=0).reshape(16, CHUNK),
        jnp.concatenate([dpost0[1:2], db1], axis=0).reshape(16, CHUNK),
        dln[0].reshape(16, CHUNK), dln[1].reshape(16, CHUNK), loss]))
    s_wb, s_wkv, s_wob = (_pair_sum(place, g, l, BF16) for g, l in ((dwb, l_wb), (dwkv, l_wkv), (dwob, l_wob)))
    dwa, l_woa, l_small, g_wb, g_wkv, g_wob = _in_proj_weight_grad(
        xb, dp, wa.shape[2], _Exchange(pair=(dwoa, small), chip=(s_wb, s_wkv, s_wob)))
    dwa = by_device(dwa)
    s_woa, s_small = _pair_sum(place, dwoa, l_woa, BF16), _pair_sum(place, small, l_small, F32)
    _, blocks = _input_grad_blocks(seq)
    cut0, cut1 = blocks // 4, blocks - blocks // 8
    dx, l_wa, g_woa, g_small = _input_grad(dp, dr0, wa, 0, cut0, None,
                                           _Exchange(pair=(dwa,), chip=(s_woa, s_small)))
    s_wa = _pair_sum(place, dwa, l_wa, BF16)
    dx, g_wa = _input_grad(dp, dr0, wa, cut0, cut1 - cut0, dx, _Exchange(chip=(s_wa,)))
    dx, = _input_grad(dp, dr0, wa, cut1, blocks - cut1, dx, _Exchange())
    full = [dwa, dwoa, dwkv, dwb, dwob, small]
    landed = [l_wa, l_woa, l_wkv, l_wb, l_wob, l_small]
    got = [g_wa, g_woa, g_wkv, g_wb, g_wob, g_small]

    big = [(w_in_a, m_w_in_a, v_w_in_a), (w_out_a, m_w_out_a, v_w_out_a), (w_kv, m_w_kv, v_w_kv),
           (w_in_b, m_w_in_b, v_w_in_b), (w_out_b, m_w_out_b, v_w_out_b)]
    big_out = []
    for i, (w, m, v) in enumerate(big):
        shape2 = got[i].shape[1:]
        res = _adamw_shard(place, full[i], landed[i], got[i], w.reshape(shape2), m.reshape(shape2),
                           v.reshape(shape2))
        big_out.append([r.reshape(w.shape) for r in res])

    small_all = _gather_small(full[5], landed[5], got[5]).reshape(N_DEV * SMALL_ROWS, CHUNK)
    g_ws, g_bsp, g_sink, g_rel, g_pg, g_pb, g_lng, g_lnb, loss_all = _unpack_small(small_all, small_shapes)
    g_ws = g_ws.reshape(w_spatial.shape)
    g_bsp = g_bsp.reshape(b_spatial.shape)
    g_pg = g_pg.reshape(post_ln_g.shape)
    g_pb = g_pb.reshape(post_ln_b.shape)
    g_lng = lax.dynamic_slice(g_lng.reshape(1, A_WIDTH), (0, me * sgu_ln_g.shape[1]), sgu_ln_g.shape)
    g_lnb = lax.dynamic_slice(g_lnb.reshape(1, A_WIDTH), (0, me * sgu_ln_b.shape[1]), sgu_ln_b.shape)
    small_params = [(sgu_ln_g, g_lng, m_sgu_ln_g, v_sgu_ln_g), (sgu_ln_b, g_lnb, m_sgu_ln_b, v_sgu_ln_b),
                    (w_spatial, g_ws, m_w_spatial, v_w_spatial), (b_spatial, g_bsp, m_b_spatial, v_b_spatial),
                    (attn_sinks, g_sink, m_attn_sinks, v_attn_sinks), (rel_bias, g_rel, m_rel_bias, v_rel_bias),
                    (post_ln_g, g_pg, m_post_ln_g, v_post_ln_g), (post_ln_b, g_pb, m_post_ln_b, v_post_ln_b)]
    flat2 = [tuple(a.reshape(-1, a.shape[-1]) for a in p) for p in small_params]
    small_out = [tuple(r.reshape(p[0].shape) for r in res) for res, p in zip(_adamw_small(flat2), small_params)]

    order = [("big", 0), ("small", 0), ("small", 1), ("small", 2), ("small", 3), ("big", 1), ("big", 2), ("big", 3),
             ("small", 4), ("small", 5), ("big", 4), ("small", 6), ("small", 7)]
    grads, deltas, new_m, new_v = [], [], [], []
    for kind, i in order:
        if kind == "big":
            g, d, m, v = big_out[i]
        else:
            g = small_params[i][1]
            d, m, v = small_out[i]
        grads.append(g)
        deltas.append(d)
        new_m.append(m)
        new_v.append(v)
    return (loss_all[0, 0], dx.reshape(x.shape), *grads, *deltas, *new_m, *new_v)
```

```python
import functools

import jax
import jax.numpy as jnp
import numpy as np
from jax import lax
from jax.experimental import pallas as pl
from jax.experimental.pallas import tpu as pltpu

F32 = jnp.float32
BF16 = jnp.bfloat16

D_MODEL = 1024
A_WIDTH = 2048
A_GROUPS = 8
A_GROUP_DIM = A_WIDTH // A_GROUPS
CHUNK = 128
HEAD_DIM = 64
N_Q_HEADS = 16
N_KV_HEADS = 2
PAIRS_PER_KV = N_Q_HEADS // N_KV_HEADS // 2
N_PAIRS = N_Q_HEADS // 2
B_WIDTH = N_Q_HEADS * HEAD_DIM
KV_WIDTH = N_KV_HEADS * HEAD_DIM
REL_BUCKETS = 32
REL_MAX_DIST = 128
DEPTH = 2
ALPHA = (2.0 * DEPTH) ** 0.25
LN_EPS = 1e-5
NEG_INF = -1e30
SCALE = HEAD_DIM ** -0.5

ADAM_LR = 0.001
ADAM_B1 = 0.9
ADAM_B2 = 0.999
ADAM_EPS = 1e-08
ADAM_WD = 0.01
ADAM_STEP = 10

N_DEV = 8
MESH = pl.DeviceIdType.MESH
TOKENS_IN_PROJ = 1024
TOKENS_WGRAD = 4096
TOKENS_A = 256
TOKENS_A_FWD = 512
TOKENS_B = 512
PART_ROWS_B = 256
TOKENS_B_BWD = 256
TOKENS_DX = 512
VMEM_LIMIT_BYTES = 60 * 1024 * 1024
SMALL_ROWS = 144

ANY = pl.BlockSpec(memory_space=pl.ANY)
VMEM = pl.BlockSpec(memory_space=pltpu.VMEM)
SMEM = pl.BlockSpec(memory_space=pltpu.SMEM)


def _nn(a, b):
    return jnp.dot(a, b, preferred_element_type=F32)


def _nt(a, b):
    return lax.dot_general(a, b, (((1,), (1,)), ((), ())), preferred_element_type=F32)


def _tn(a, b):
    return lax.dot_general(a, b, (((0,), (0,)), ((), ())), preferred_element_type=F32)


def _resident(shape):
    nd = len(shape)
    return pl.BlockSpec(shape, lambda i: (0,) * nd, pipeline_mode=pl.Buffered(1))


def _rows(tm, width, reverse_of=None):
    if reverse_of is None:
        return pl.BlockSpec((tm, width), lambda i: (i, 0))
    return pl.BlockSpec((tm, width), lambda i: (reverse_of - 1 - i, 0))


def _seq_params():
    return pltpu.CompilerParams(dimension_semantics=("arbitrary",), vmem_limit_bytes=VMEM_LIMIT_BYTES)


def _in_turns(first, second):
    merged = [(i / max(len(first), 1), 0, f) for i, f in enumerate(first)]
    merged += [((i + 0.5) / max(len(second), 1), 1, f) for i, f in enumerate(second)]
    return [f for _, _, f in sorted(merged, key=lambda t: t[:2])]


def _ln_normalize(r):
    mu = jnp.mean(r, axis=-1, keepdims=True)
    xc = r - mu
    var = jnp.mean(xc * xc, axis=-1, keepdims=True)
    rstd = lax.rsqrt(var + LN_EPS)
    return xc * rstd, rstd


def _ln_backward(dxhat, xhat, rstd):
    m1 = jnp.mean(dxhat, axis=-1, keepdims=True)
    m2 = jnp.mean(dxhat * xhat, axis=-1, keepdims=True)
    return rstd * (dxhat - m1 - xhat * m2)


def _silu_and_grad(z):
    sig = jax.nn.sigmoid(z)
    silu = z * sig
    return silu, sig + silu * (1.0 - sig)


def _band_tables():
    t = np.arange(CHUNK, dtype=np.int32)[:, None]
    j = np.arange(2 * CHUNK, dtype=np.int32)[None, :]
    d = t + CHUNK - j
    in_window = (d >= 0) & (d < CHUNK)
    dc = np.clip(d, 0, REL_MAX_DIST - 1)
    max_exact = REL_BUCKETS // 2
    df = np.maximum(dc, 1).astype(np.float32)
    large = max_exact + (np.log(df / np.float32(max_exact)) / np.float32(np.log(REL_MAX_DIST / max_exact))
                         * np.float32(REL_BUCKETS - max_exact)).astype(np.int32)
    large = np.minimum(large, REL_BUCKETS - 1)
    bucket = np.where(dc < max_exact, dc, large)
    return np.where(in_window, bucket, -1).astype(np.int32)


def _use_order():
    x, y, c = lax.axis_index("x"), lax.axis_index("y"), lax.axis_index("c")
    blocks = [(x, y, c), (x, y, 1 - c)]
    for px, py in [(1 - x, y), (x, 1 - y), (1 - x, 1 - y)]:
        blocks += [(px, py, c), (px, py, 1 - c)]
    return blocks


def _gather_in_proj(order, x, w_in_a, small, ln_gb):
    seq = x.shape[0]
    tm = min(TOKENS_IN_PROJ, seq)
    n_blocks = seq // tm
    shard_cols = w_in_a.shape[1]
    n = len(small) + 2

    def body(order_ref, x_ref, *refs):
        ins, refs = refs[:n], refs[n:]
        p_ref, outs, refs = refs[0], refs[1:n + 1], refs[n + 1:]
        wa_vmem, xb_vmem, stage = refs[0], refs[1], refs[2:n]
        send_sems, recv_sems, local_sems, out_sems = refs[n:]
        phase, blk = pl.program_id(0), pl.program_id(1)
        x_, y_, c_ = lax.axis_index("x"), lax.axis_index("y"), lax.axis_index("c")
        me, sibling = (x_, y_, c_), (x_, y_, 1 - c_)
        chips = [(1 - x_, y_), (x_, 1 - y_), (1 - x_, 1 - y_)]
        use = _use_order()

        def index(px, py, pc):
            return 4 * px + 2 * py + pc

        def slot(a, block):
            return (wa_vmem if a == 0 else outs[a]).at[index(*block)]

        def own(a):
            return slot(0, me) if a == 0 else (ins[a] if a == n - 1 else stage[a - 1])

        def copy(a, k, block, to, src=None):
            return pltpu.make_async_remote_copy(
                src_ref=slot(a, block) if src is None else src, dst_ref=slot(a, block),
                send_sem=send_sems.at[a, k], recv_sem=recv_sems.at[a, k], device_id=to, device_id_type=MESH)

        def keep(a):
            return pltpu.make_async_copy(own(a), slot(a, me), local_sems.at[a])

        def write_out(k):
            return pltpu.make_async_copy(slot(0, use[k]), outs[0].at[index(*use[k])], out_sems.at[k])

        def pass_on(a, j):
            copy(a, 1 + j, (*chips[j], c_), me).wait_recv()
            copy(a, 4 + j, (*chips[j], c_), sibling).start()

        def send(a, j):
            copy(a, 1 + j, me, (*chips[j], c_), own(a)).start()

        @pl.when((phase == 0) & (blk == 0))
        def _():
            wa_vmem[index(*me)] = ins[0][...].astype(BF16)
            for a in range(1, n - 1):
                stage[a - 1][...] = ins[a][...].astype(BF16)
            send(0, 0)
            send(0, 1)
            for a in range(n):
                copy(a, 0, me, sibling, own(a)).start()
                if a:
                    keep(a).start()
            write_out(0).start()
            copy(0, 0, sibling, me).wait_recv()
            write_out(1).start()

        for j in range(3):
            @pl.when((phase == 1 + j) & (blk == 0))
            def _(j=j):
                pass_on(0, j)
                write_out(2 + 2 * j).start()
                if j == 0:
                    send(0, 2)
                    for a in range(1, n):
                        for jj in range(3):
                            send(a, jj)
                copy(0, 4 + j, (*chips[j], 1 - c_), me).wait_recv()
                write_out(3 + 2 * j).start()
                if j == 2:
                    for a in range(1, n):
                        pass_on(a, 0)
                        pass_on(a, 1)

        rows = pl.ds(pl.multiple_of(blk * tm, tm), tm)

        @pl.when(phase == 0)
        def _():
            xb_vmem[rows, :] = x_ref[...].astype(BF16)

        chip = order_ref[phase]
        xb = xb_vmem[rows, :]
        p_ref[:, :shard_cols] = _nn(xb, wa_vmem[2 * chip])
        p_ref[:, shard_cols:] = _nn(xb, wa_vmem[2 * chip + 1])

        @pl.when((phase == N_DEV // 2 - 1) & (blk == n_blocks - 1))
        def _():
            for a in range(1, n):
                pass_on(a, 2)
            for a in range(1, n):
                copy(a, 0, sibling, me).wait_recv()
                for j, chip in enumerate(chips):
                    copy(a, 4 + j, (*chip, 1 - c_), me).wait_recv()
            for a in range(n):
                for k in range(7):
                    copy(a, k, me, me).wait_send()
                if a:
                    keep(a).wait()
            for k in range(N_DEV):
                write_out(k).wait()

    gathered = [jax.ShapeDtypeStruct((N_DEV,) + w_in_a.shape, BF16)]
    gathered += [jax.ShapeDtypeStruct((N_DEV,) + s.shape, BF16) for s in small]
    gathered.append(jax.ShapeDtypeStruct((N_DEV,) + ln_gb.shape, F32))
    return pl.pallas_call(
        body, name="gather_in_proj",
        out_shape=[jax.ShapeDtypeStruct((seq, N_DEV * shard_cols), F32)] + gathered,
        grid_spec=pltpu.PrefetchScalarGridSpec(
            num_scalar_prefetch=1, grid=(N_DEV // 2, n_blocks),
            in_specs=[pl.BlockSpec((tm, D_MODEL), lambda k, i, order: (jnp.where(k == 0, i, n_blocks - 1), 0))]
            + [VMEM] * n,
            out_specs=[pl.BlockSpec((tm, 2 * shard_cols), lambda k, i, order: (i, order[k]))] + [ANY] * n,
            scratch_shapes=[pltpu.VMEM((N_DEV,) + w_in_a.shape, BF16), pltpu.VMEM((seq, D_MODEL), BF16)]
            + [pltpu.VMEM(s.shape, BF16) for s in small] + [
                pltpu.SemaphoreType.DMA((n, 7)), pltpu.SemaphoreType.DMA((n, 7)), pltpu.SemaphoreType.DMA((n,)),
                pltpu.SemaphoreType.DMA((N_DEV,))]),
        compiler_params=pltpu.CompilerParams(dimension_semantics=("arbitrary", "arbitrary"),
                                             vmem_limit_bytes=VMEM_LIMIT_BYTES),
    )(order, x, w_in_a, *small, ln_gb)


def _bias_table(rel_bias, bucket):
    def body(rel_ref, bucket_ref, out_ref):
        bk = bucket_ref[...]
        col = lax.broadcasted_iota(jnp.int32, bk.shape, 1)
        for h in range(N_Q_HEADS):
            def step(b, acc):
                return jnp.where(bk == b, rel_ref[b, h], acc)
            tab = lax.fori_loop(0, REL_BUCKETS, step, jnp.full(bk.shape, NEG_INF, F32))
            pair = h // 2
            rows = slice((pair % PAIRS_PER_KV) * CHUNK, (pair % PAIRS_PER_KV + 1) * CHUNK)
            half = slice((h % 2) * 2 * CHUNK, (h % 2 + 1) * 2 * CHUNK)
            out_ref[1, pair // PAIRS_PER_KV, rows, half] = tab
            out_ref[0, pair // PAIRS_PER_KV, rows, half] = jnp.where(col < CHUNK, NEG_INF, tab)

    return pl.pallas_call(
        body, name="bias_table",
        out_shape=jax.ShapeDtypeStruct((2, N_KV_HEADS, PAIRS_PER_KV * CHUNK, 4 * CHUNK), F32),
        in_specs=[SMEM, VMEM], out_specs=VMEM,
    )(rel_bias, bucket)


def _rel_bias_grad(dlogits, bucket):
    def body(dl_ref, bucket_ref, out_ref):
        bk = bucket_ref[...]
        row = lax.broadcasted_iota(jnp.int32, (REL_BUCKETS, CHUNK), 0)
        lane = lax.broadcasted_iota(jnp.int32, (REL_BUCKETS, CHUNK), 1)

        def step(b, acc):
            for h in range(N_Q_HEADS):
                pair = h // 2
                dl = dl_ref[pair // PAIRS_PER_KV, (pair % PAIRS_PER_KV) * CHUNK:(pair % PAIRS_PER_KV + 1) * CHUNK,
                            (h % 2) * 2 * CHUNK:(h % 2 + 1) * 2 * CHUNK]
                s = jnp.sum(jnp.where(bk == b, dl, 0.0))
                acc = jnp.where((row == b) & (lane == h), s, acc)
            return acc

        out_ref[...] = lax.fori_loop(0, REL_BUCKETS, step, jnp.zeros((REL_BUCKETS, CHUNK), F32))

    return pl.pallas_call(
        body, name="rel_bias_grad", out_shape=jax.ShapeDtypeStruct((REL_BUCKETS, CHUNK), F32),
        in_specs=[VMEM, VMEM], out_specs=VMEM,
    )(dlogits, bucket)


def _masked_spatial(ws_ref, g):
    t = lax.broadcasted_iota(jnp.int32, (CHUNK, CHUNK), 0)
    s = lax.broadcasted_iota(jnp.int32, (CHUNK, CHUNK), 1)
    return jnp.where(s <= t, ws_ref[g], 0.0).astype(BF16)


def _spatial_gate(ws_ref, bsp_t_ref, vn, s_ref, n_chunks):
    for g in range(A_GROUPS):
        w = _masked_spatial(ws_ref, g)
        cols = slice(g * A_GROUP_DIM, (g + 1) * A_GROUP_DIM)
        for ci in range(n_chunks):
            rows = slice(ci * CHUNK, (ci + 1) * CHUNK)
            s_ref[rows, cols] = _nn(w, vn[rows, cols]) + bsp_t_ref[:, g:g + 1]


def _fwd_a(x, p, ln_gb, w_spatial, bsp_t, woa, later):
    seq = x.shape[0]
    tm = min(TOKENS_A_FWD, seq)
    n_chunks = tm // CHUNK
    n_blocks = seq // tm
    n = len(later)

    def body(x_ref, p_ref, ln_ref, ws_ref, bsp_ref, woa_ref, *refs):
        shards, refs = refs[:n], refs[n:]
        r0_ref, xb_ref, gathered, refs = refs[0], refs[1], refs[2:2 + n], refs[2 + n:]
        s_ref, stage, (send_sems, recv_sems, local_sems) = refs[0], refs[1:1 + n], refs[1 + n:]
        step = pl.program_id(0)

        def copies(a):
            x_, y_, c_ = lax.axis_index("x"), lax.axis_index("y"), lax.axis_index("c")
            mine = 4 * x_ + 2 * y_ + c_
            out = [pltpu.make_async_copy(stage[a], gathered[a].at[mine], local_sems.at[a])]
            for r in range(1, N_DEV):
                to = (1 - x_ if r & 4 else x_, 1 - y_ if r & 2 else y_, 1 - c_ if r & 1 else c_)
                out.append(pltpu.make_async_remote_copy(
                    src_ref=stage[a], dst_ref=gathered[a].at[mine], send_sem=send_sems.at[a, r - 1],
                    recv_sem=recv_sems.at[a, r - 1], device_id=to, device_id_type=MESH))
            return out

        @pl.when(step == 0)
        def _():
            for a in range(n):
                stage[a][...] = shards[a][...].astype(BF16)
                for cp in copies(a):
                    cp.start()

        xv = x_ref[...]
        xb_ref[...] = xv.astype(BF16)
        vhat, _ = _ln_normalize(p_ref[:, A_WIDTH:2 * A_WIDTH])
        vn = (vhat * ln_ref[0:1, :] + ln_ref[1:2, :]).astype(BF16)
        _spatial_gate(ws_ref, bsp_ref, vn, s_ref, n_chunks)
        z = p_ref[:, 2 * A_WIDTH:]
        y = p_ref[:, :A_WIDTH] * s_ref[...] * (z * jax.nn.sigmoid(z))
        r0_ref[...] = ALPHA * xv + _nn(y.astype(BF16), woa_ref[...])

        @pl.when(step == n_blocks - 1)
        def _():
            for a in range(n):
                for cp in copies(a):
                    cp.wait()

    return pl.pallas_call(
        body, name="fwd_a", grid=(n_blocks,),
        out_shape=[jax.ShapeDtypeStruct((seq, D_MODEL), F32), jax.ShapeDtypeStruct((seq, D_MODEL), BF16)]
        + [jax.ShapeDtypeStruct((N_DEV,) + s.shape, BF16) for s in later],
        in_specs=[_rows(tm, D_MODEL), _rows(tm, 3 * A_WIDTH), _resident(ln_gb.shape), _resident(w_spatial.shape),
                  _resident(bsp_t.shape), _resident(woa.shape)] + [VMEM] * n,
        out_specs=[_rows(tm, D_MODEL), _rows(tm, D_MODEL)] + [ANY] * n,
        scratch_shapes=[pltpu.VMEM((tm, A_WIDTH), F32)] + [pltpu.VMEM(s.shape, BF16) for s in later] + [
            pltpu.SemaphoreType.DMA((n, N_DEV - 1)), pltpu.SemaphoreType.DMA((n, N_DEV - 1)),
            pltpu.SemaphoreType.DMA((n,))],
        compiler_params=_seq_params(),
    )(x, p, ln_gb, w_spatial, bsp_t, woa, *later)


def _expand_heads(kv):
    lo = lax.broadcasted_iota(jnp.int32, kv.shape, 1) < HEAD_DIM
    swapped = pltpu.roll(kv, HEAD_DIM, 1)
    zero = jnp.zeros_like(kv)
    return (jnp.where(lo, kv, zero).astype(BF16), jnp.where(lo, zero, swapped).astype(BF16),
            jnp.where(lo, swapped, zero).astype(BF16), jnp.where(lo, zero, kv).astype(BF16))


def _fold_heads(acc, kvh):
    top, bot = acc[:2 * CHUNK], acc[2 * CHUNK:]
    lo = lax.broadcasted_iota(jnp.int32, top.shape, 1) < HEAD_DIM
    t = jnp.where(lo, top, 0.0)
    b = jnp.where(lo, 0.0, bot)
    if kvh == 0:
        return t + pltpu.roll(b, HEAD_DIM, 1)
    return pltpu.roll(t, HEAD_DIM, 1) + b


STACK = PAIRS_PER_KV * CHUNK


def _pair_cols(kvh, pp):
    pair = kvh * PAIRS_PER_KV + pp
    return slice(pair * CHUNK, (pair + 1) * CHUNK)


def _stack_pairs(ref_or_val, rows, kvh):
    return jnp.concatenate([ref_or_val[rows, _pair_cols(kvh, pp)] for pp in range(PAIRS_PER_KV)], axis=0)


def _sink_columns(sink_ref):
    blk = lax.broadcasted_iota(jnp.int32, (STACK, 1), 0) // CHUNK
    out = []
    for kvh in range(N_KV_HEADS):
        halves = []
        for half in range(2):
            col = jnp.zeros((STACK, 1), F32)
            for pp in range(PAIRS_PER_KV):
                col = jnp.where(blk == pp, sink_ref[0, 2 * (kvh * PAIRS_PER_KV + pp) + half], col)
            halves.append(col)
        out.append(halves)
    return out


def _softmax_stack(logits, sink_cols):
    ps, sinks = [], []
    for half in range(2):
        lg = logits[:, half * 2 * CHUNK:(half + 1) * 2 * CHUNK]
        sk = sink_cols[half]
        m = jnp.maximum(jnp.max(lg, axis=-1, keepdims=True), sk)
        e = jnp.exp(lg - m)
        es = jnp.exp(sk - m)
        inv = 1.0 / (jnp.sum(e, axis=-1, keepdims=True) + es)
        ps.append(e * inv)
        sinks.append(es * inv)
    return jnp.concatenate(ps, axis=1), sinks


def _band_operands(before_k, this_k, before_v, this_v, kvh):
    kblk = jnp.concatenate([before_k[2 * kvh], this_k[2 * kvh], before_k[2 * kvh + 1], this_k[2 * kvh + 1]], axis=0)
    vblk = jnp.concatenate([before_v[2 * kvh], this_v[2 * kvh], before_v[2 * kvh + 1], this_v[2 * kvh + 1]], axis=0)
    return kblk, vblk


def _fwd_b(r0, target, wb, wkv, wob, post_g, post_b, sinks, bias):
    seq = r0.shape[0]
    tm = min(TOKENS_B, seq)
    n_chunks = tm // CHUNK
    part_rows = min(PART_ROWS_B, tm)
    n_parts = tm // part_rows
    part_chunks = part_rows // CHUNK
    shard_cols = wb.shape[2]

    def body(r0_ref, tgt_ref, wb_ref, wkv_ref, wob_ref, pg_ref, pb_ref, sink_ref, bias_ref,
             q_ref, z_ref, kv_ref, dr1_ref, loss_ref, dg_ref, db_ref, ek_last, ev_last, *scratch):
        q_s, z_s, h1_s, o_s, ek_s, ev_s = (scratch[i * n_parts:(i + 1) * n_parts] for i in range(6))
        step = pl.program_id(0)

        @pl.when(step == 0)
        def _():
            ek_last[...] = jnp.zeros_like(ek_last)
            ev_last[...] = jnp.zeros_like(ev_last)
            loss_ref[...] = jnp.zeros_like(loss_ref)
            dg_ref[...] = jnp.zeros_like(dg_ref)
            db_ref[...] = jnp.zeros_like(db_ref)

        sink_cols = _sink_columns(sink_ref)
        g1 = pg_ref[1:2, :]

        def project_steps(part):
            rows = slice(part * part_rows, (part + 1) * part_rows)
            live = {}

            def normalize():
                xhat0, _ = _ln_normalize(r0_ref[rows, :])
                h1 = xhat0 * pg_ref[0:1, :] + pb_ref[0:1, :]
                h1_s[part][...] = h1
                live["h1b"] = h1.astype(BF16)

            def shard(k):
                def run():
                    qz = _nn(live["h1b"], wb_ref[k])
                    lo, hi = k * shard_cols, (k + 1) * shard_cols
                    if hi <= B_WIDTH:
                        qs = (qz * SCALE).astype(BF16)
                        q_ref[rows, lo:hi] = qs
                        q_s[part][:, lo:hi] = qs
                    else:
                        z_ref[rows, lo - B_WIDTH:hi - B_WIDTH] = qz
                        z_s[part][:, lo - B_WIDTH:hi - B_WIDTH] = qz
                return run

            def keys():
                kv = _nn(live["h1b"], wkv_ref[...])
                kv_ref[rows, :] = kv.astype(BF16)
                for cj in range(part_chunks):
                    sub = slice(cj * CHUNK, (cj + 1) * CHUNK)
                    for i, e in enumerate(_expand_heads(kv[sub, :KV_WIDTH])):
                        ek_s[part][cj, i] = e
                    for i, e in enumerate(_expand_heads(kv[sub, KV_WIDTH:])):
                        ev_s[part][cj, i] = e

            return [normalize, keys] + [shard(k) for k in range(N_DEV)]

        def attend_steps(part):
            def stack(cj, kvh):
                def run():
                    sub = slice(cj * CHUNK, (cj + 1) * CHUNK)
                    sel = jnp.where(step * n_chunks + part * part_chunks + cj > 0, 1, 0)
                    if cj:
                        before_k, before_v = ek_s[part].at[cj - 1], ev_s[part].at[cj - 1]
                    elif part:
                        before_k, before_v = ek_s[part - 1].at[part_chunks - 1], ev_s[part - 1].at[part_chunks - 1]
                    else:
                        before_k, before_v = ek_last, ev_last
                    kblk, vblk = _band_operands(before_k, ek_s[part].at[cj], before_v, ev_s[part].at[cj], kvh)
                    logits = _nt(_stack_pairs(q_s[part], sub, kvh), kblk) + bias_ref[sel, kvh]
                    p, _ = _softmax_stack(logits, sink_cols[kvh])
                    o = _nn(p.astype(BF16), vblk)
                    for pp in range(PAIRS_PER_KV):
                        o_s[part][sub, _pair_cols(kvh, pp)] = o[pp * CHUNK:(pp + 1) * CHUNK]
                return run

            def finish():
                rows = slice(part * part_rows, (part + 1) * part_rows)
                z = z_s[part][...]
                y = o_s[part][...] * (z * jax.nn.sigmoid(z))
                r1 = ALPHA * h1_s[part][...] + _nn(y.astype(BF16), wob_ref[...])
                xhat1, rstd1 = _ln_normalize(r1)
                diff = xhat1 * g1 + pb_ref[1:2, :] - tgt_ref[rows, :]
                loss_ref[...] += 0.5 * jnp.sum(jnp.mean(diff * diff, axis=-1, keepdims=True), axis=0, keepdims=True)
                dh2 = diff * (1.0 / D_MODEL)
                dg_ref[...] += jnp.sum(dh2 * xhat1, axis=0, keepdims=True)
                db_ref[...] += jnp.sum(dh2, axis=0, keepdims=True)
                dr1_ref[rows, :] = _ln_backward(dh2 * g1, xhat1, rstd1)

            return [stack(cj, kvh) for cj in range(part_chunks) for kvh in range(N_KV_HEADS)] + [finish]

        for run in project_steps(0):
            run()
        for part in range(n_parts):
            for run in _in_turns(attend_steps(part), project_steps(part + 1) if part + 1 < n_parts else []):
                run()
        ek_last[...] = ek_s[n_parts - 1][part_chunks - 1]
        ev_last[...] = ev_s[n_parts - 1][part_chunks - 1]

    vec = pl.BlockSpec((1, D_MODEL), lambda i: (0, 0))
    return pl.pallas_call(
        body, name="fwd_b", grid=(seq // tm,),
        out_shape=[jax.ShapeDtypeStruct((seq, B_WIDTH), BF16), jax.ShapeDtypeStruct((seq, B_WIDTH), F32),
                   jax.ShapeDtypeStruct((seq, 2 * KV_WIDTH), BF16), jax.ShapeDtypeStruct((seq, D_MODEL), F32),
                   jax.ShapeDtypeStruct((1, 1), F32), jax.ShapeDtypeStruct((1, D_MODEL), F32),
                   jax.ShapeDtypeStruct((1, D_MODEL), F32)],
        in_specs=[_rows(tm, D_MODEL), _rows(tm, D_MODEL), _resident(wb.shape), _resident(wkv.shape),
                  _resident(wob.shape), _resident(post_g.shape), _resident(post_b.shape), SMEM,
                  _resident(bias.shape)],
        out_specs=[_rows(tm, B_WIDTH), _rows(tm, B_WIDTH), _rows(tm, 2 * KV_WIDTH), _rows(tm, D_MODEL),
                   pl.BlockSpec((1, 1), lambda i: (0, 0)), vec, vec],
        scratch_shapes=[pltpu.VMEM((4, CHUNK, CHUNK), BF16), pltpu.VMEM((4, CHUNK, CHUNK), BF16)]
        + [pltpu.VMEM((part_rows, B_WIDTH), BF16)] * n_parts + [pltpu.VMEM((part_rows, B_WIDTH), F32)] * n_parts
        + [pltpu.VMEM((part_rows, D_MODEL), F32)] * n_parts + [pltpu.VMEM((part_rows, B_WIDTH), F32)] * n_parts
        + [pltpu.VMEM((part_chunks, 4, CHUNK, CHUNK), BF16)] * (2 * n_parts),
        compiler_params=_seq_params(),
    )(r0, target, wb, wkv, wob, post_g, post_b, sinks, bias)


def _bwd_b(r0, dr1, q, z, kv, wb, wkv, wob, post_g, post_b, sinks, bias):
    seq = r0.shape[0]
    tm = min(TOKENS_B_BWD, seq)
    n_chunks = tm // CHUNK
    n_blocks = seq // tm
    shard_cols = wb.shape[2]

    def body(r0_ref, dr1_ref, q_ref, z_ref, kv_ref, kvp_ref, wb_ref, wkv_ref, wob_ref, pg_ref, pb_ref, sink_ref,
             bias_ref, dr0_ref, dwb_out, dwkv_out, dwob_out, dsink_ref, dl_out, dpost_ref,
             dwb_acc, dwkv_acc, dwob_acc, dl_acc, dsink_acc, carry_ref, ek_ref, ev_ref, y_ref, dqz_ref, dkv_ref):
        step = pl.program_id(0)
        block = n_blocks - 1 - step

        @pl.when(step == 0)
        def _():
            dpost_ref[...] = jnp.zeros_like(dpost_ref)
            dwb_acc[...] = jnp.zeros_like(dwb_acc)
            dwkv_acc[...] = jnp.zeros_like(dwkv_acc)
            dwob_acc[...] = jnp.zeros_like(dwob_acc)
            dl_acc[...] = jnp.zeros_like(dl_acc)
            carry_ref[...] = jnp.zeros_like(carry_ref)
            dsink_acc[...] = jnp.zeros_like(dsink_acc)

        xhat0, rstd0 = _ln_normalize(r0_ref[...])
        h1b = (xhat0 * pg_ref[0:1, :] + pb_ref[0:1, :]).astype(BF16)
        dr1 = dr1_ref[...]
        dr1b = dr1.astype(BF16)
        dy = _nt(dr1b, wob_ref[...])
        sg, dsg = _silu_and_grad(z_ref[...])
        do_all = dy * sg
        dyz = dy * dsg

        for ci in range(n_chunks + 1):
            src = kvp_ref[...] if ci == 0 else kv_ref[(ci - 1) * CHUNK:ci * CHUNK, :]
            src = src.astype(F32)
            for i, e in enumerate(_expand_heads(src[:, :KV_WIDTH])):
                ek_ref[ci, i] = e
            for i, e in enumerate(_expand_heads(src[:, KV_WIDTH:])):
                ev_ref[ci, i] = e

        sink_cols = _sink_columns(sink_ref)
        for ci in reversed(range(n_chunks)):
            rows = slice(ci * CHUNK, (ci + 1) * CHUNK)
            sel = jnp.where(block * n_chunks + ci > 0, 1, 0)
            dk_band = jnp.zeros((2 * CHUNK, KV_WIDTH), F32)
            dv_band = jnp.zeros((2 * CHUNK, KV_WIDTH), F32)
            for kvh in range(N_KV_HEADS):
                kblk, vblk = _band_operands(ek_ref.at[ci], ek_ref.at[ci + 1], ev_ref.at[ci], ev_ref.at[ci + 1], kvh)
                qs = _stack_pairs(q_ref, rows, kvh)
                p, psink = _softmax_stack(_nt(qs, kblk) + bias_ref[sel, kvh], sink_cols[kvh])
                pb16 = p.astype(BF16)
                o = _nn(pb16, vblk)
                y = (o * _stack_pairs(sg, rows, kvh)).astype(BF16)
                dz = (o * _stack_pairs(dyz, rows, kvh)).astype(BF16)
                dob = _stack_pairs(do_all, rows, kvh).astype(BF16)
                dp = _nt(dob, vblk)
                dlogits = []
                for half in range(2):
                    hs = slice(half * 2 * CHUNK, (half + 1) * 2 * CHUNK)
                    delta = jnp.sum(dp[:, hs] * p[:, hs], axis=-1, keepdims=True)
                    dlogits.append(p[:, hs] * (dp[:, hs] - delta))
                    dsink_acc[2 * kvh + half] += psink[half] * delta
                dlog = jnp.concatenate(dlogits, axis=1)
                dl_acc[kvh] += dlog
                dsc = dlog.astype(BF16)
                dq = (_nn(dsc, kblk) * SCALE).astype(BF16)
                for pp in range(PAIRS_PER_KV):
                    blk = slice(pp * CHUNK, (pp + 1) * CHUNK)
                    cols = _pair_cols(kvh, pp)
                    y_ref[rows, cols] = y[blk]
                    dqz_ref[rows, cols] = dq[blk]
                    dqz_ref[rows, B_WIDTH + cols.start:B_WIDTH + cols.stop] = dz[blk]
                dk_band = dk_band + _fold_heads(_tn(dsc, qs), kvh)
                dv_band = dv_band + _fold_heads(_tn(pb16, dob), kvh)
            band = jnp.concatenate([dk_band, dv_band], axis=1)
            dkv_ref[rows, :] = (band[CHUNK:] + carry_ref[...]).astype(BF16)
            carry_ref[...] = band[:CHUNK]

        dqz = dqz_ref[...]
        dkv = dkv_ref[...]
        dh1 = ALPHA * dr1 + _nt(dkv, wkv_ref[...])
        for k in range(N_DEV):
            part = dqz[:, k * shard_cols:(k + 1) * shard_cols]
            dh1 = dh1 + _nt(part, wb_ref[k])
            dwb_acc[k] += _tn(h1b, part)
        dpost_ref[0:1, :] += jnp.sum(dh1 * xhat0, axis=0, keepdims=True)
        dpost_ref[1:2, :] += jnp.sum(dh1, axis=0, keepdims=True)
        dr0_ref[...] = _ln_backward(dh1 * pg_ref[0:1, :], xhat0, rstd0)
        dwkv_acc[...] += _tn(h1b, dkv)
        dwob_acc[...] += _tn(y_ref[...], dr1b)

        @pl.when(step == n_blocks - 1)
        def _():
            pltpu.sync_copy(dwb_acc, dwb_out)
            pltpu.sync_copy(dwkv_acc, dwkv_out)
            pltpu.sync_copy(dwob_acc, dwob_out)
            pltpu.sync_copy(dl_acc, dl_out)
            lane = lax.broadcasted_iota(jnp.int32, (1, CHUNK), 1)
            dsink = jnp.zeros((1, CHUNK), F32)
            for kvh in range(N_KV_HEADS):
                for half in range(2):
                    for pp in range(PAIRS_PER_KV):
                        head = 2 * (kvh * PAIRS_PER_KV + pp) + half
                        tot = jnp.sum(dsink_acc[2 * kvh + half, pp * CHUNK:(pp + 1) * CHUNK, :], axis=0, keepdims=True)
                        dsink = dsink - jnp.where(lane == head, tot, 0.0)
            dsink_ref[...] = dsink

    rev = functools.partial(_rows, tm, reverse_of=n_blocks)
    prev_chunk = pl.BlockSpec(
        (CHUNK, 2 * KV_WIDTH), lambda i: (jnp.maximum((n_blocks - 1 - i) * n_chunks - 1, 0), 0))
    return pl.pallas_call(
        body, name="bwd_b", grid=(n_blocks,),
        out_shape=[jax.ShapeDtypeStruct((seq, D_MODEL), F32), jax.ShapeDtypeStruct(wb.shape, F32),
                   jax.ShapeDtypeStruct(wkv.shape, F32), jax.ShapeDtypeStruct(wob.shape, F32),
                   jax.ShapeDtypeStruct((1, CHUNK), F32),
                   jax.ShapeDtypeStruct((N_KV_HEADS, STACK, 4 * CHUNK), F32), jax.ShapeDtypeStruct((2, D_MODEL), F32)],
        in_specs=[rev(D_MODEL), rev(D_MODEL), rev(B_WIDTH), rev(B_WIDTH), rev(2 * KV_WIDTH), prev_chunk,
                  _resident(wb.shape), _resident(wkv.shape), _resident(wob.shape), _resident(post_g.shape),
                  _resident(post_b.shape), SMEM, _resident(bias.shape)],
        out_specs=[rev(D_MODEL), ANY, ANY, ANY, pl.BlockSpec((1, CHUNK), lambda i: (0, 0)), ANY,
                   pl.BlockSpec((2, D_MODEL), lambda i: (0, 0))],
        scratch_shapes=[pltpu.VMEM(wb.shape, F32), pltpu.VMEM(wkv.shape, F32), pltpu.VMEM(wob.shape, F32),
                        pltpu.VMEM((N_KV_HEADS, STACK, 4 * CHUNK), F32), pltpu.VMEM((2 * N_KV_HEADS, STACK, 1), F32),
                        pltpu.VMEM((CHUNK, 2 * KV_WIDTH), F32),
                        pltpu.VMEM((n_chunks + 1, 4, CHUNK, CHUNK), BF16),
                        pltpu.VMEM((n_chunks + 1, 4, CHUNK, CHUNK), BF16),
                        pltpu.VMEM((tm, B_WIDTH), BF16), pltpu.VMEM((tm, 2 * B_WIDTH), BF16),
                        pltpu.VMEM((tm, 2 * KV_WIDTH), BF16)],
        compiler_params=_seq_params(),
    )(r0, dr1, q, z, kv, kv, wb, wkv, wob, post_g, post_b, sinks, bias)


def _bwd_a(dr0, p, ln_gb, w_spatial, bsp_t, woa, exchange):
    seq = dr0.shape[0]
    tm = min(TOKENS_A, seq)
    n_chunks = tm // CHUNK
    n_blocks = seq // tm

    def body(dr0_ref, p_ref, ln_ref, ws_ref, bsp_ref, woa_ref, *refs):
        ex_in, refs = refs[:exchange.n], refs[exchange.n:]
        (dp_ref, dwoa_out, dws_ref, dbsp_ref, dln_ref), refs = refs[:5], refs[5:]
        ex_out, refs = refs[:exchange.n], refs[exchange.n:]
        (dwoa_acc, s_ref, ds_ref, dvn_ref), ex_sems = refs[:4], refs[4:]
        step = pl.program_id(0)

        @pl.when(step == 0)
        def _():
            exchange.start(ex_in, ex_out, ex_sems)
            dwoa_acc[...] = jnp.zeros_like(dwoa_acc)
            dws_ref[...] = jnp.zeros_like(dws_ref)
            dbsp_ref[...] = jnp.zeros_like(dbsp_ref)
            dln_ref[...] = jnp.zeros_like(dln_ref)

        dr0b = dr0_ref[...].astype(BF16)

        vhat, rstd_v = _ln_normalize(p_ref[:, A_WIDTH:2 * A_WIDTH])
        ln_g = ln_ref[0:1, :]
        vn = (vhat * ln_g + ln_ref[1:2, :]).astype(BF16)
        _spatial_gate(ws_ref, bsp_ref, vn, s_ref, n_chunks)
        u = p_ref[:, :A_WIDTH]
        s = s_ref[...]
        sg, dsg = _silu_and_grad(p_ref[:, 2 * A_WIDTH:])
        gate = s * sg
        dwoa_acc[...] += _tn((u * gate).astype(BF16), dr0b)
        dy = _nt(dr0b, woa_ref[...])
        dyu = dy * u
        dp_ref[:, :A_WIDTH] = (dy * gate).astype(BF16)
        dp_ref[:, 2 * A_WIDTH:] = (dyu * (s * dsg)).astype(BF16)
        ds_ref[...] = dyu * sg

        lane = lax.broadcasted_iota(jnp.int32, (CHUNK, CHUNK), 1)
        dbsp = jnp.zeros((CHUNK, CHUNK), F32)
        for g in range(A_GROUPS):
            w = _masked_spatial(ws_ref, g)
            cols = slice(g * A_GROUP_DIM, (g + 1) * A_GROUP_DIM)
            dws = jnp.zeros((CHUNK, CHUNK), F32)
            for ci in range(n_chunks):
                rows = slice(ci * CHUNK, (ci + 1) * CHUNK)
                ds = ds_ref[rows, cols]
                dsb = ds.astype(BF16)
                dvn_ref[rows, cols] = _tn(w, dsb)
                dws = dws + _nt(dsb, vn[rows, cols])
                dbsp = dbsp + jnp.where(lane == g, jnp.sum(ds, axis=-1, keepdims=True), 0.0)
            dws_ref[g] += dws
        dbsp_ref[...] += dbsp

        dvn = dvn_ref[...]
        dln_ref[0:1, :] += jnp.sum(dvn * vhat, axis=0, keepdims=True)
        dln_ref[1:2, :] += jnp.sum(dvn, axis=0, keepdims=True)
        dp_ref[:, A_WIDTH:2 * A_WIDTH] = _ln_backward(dvn * ln_g, vhat, rstd_v).astype(BF16)

        @pl.when(step == n_blocks - 1)
        def _():
            t = lax.broadcasted_iota(jnp.int32, (CHUNK, CHUNK), 0)
            for g in range(A_GROUPS):
                dws_ref[g] = jnp.where(lane <= t, dws_ref[g], 0.0)
            pltpu.sync_copy(dwoa_acc, dwoa_out)
            exchange.wait(ex_in, ex_out, ex_sems)

    def fixed(shape):
        nd = len(shape)
        return pl.BlockSpec(shape, lambda i: (0,) * nd)

    return pl.pallas_call(
        body, name="bwd_a", grid=(n_blocks,),
        out_shape=[jax.ShapeDtypeStruct((seq, 3 * A_WIDTH), BF16),
                   jax.ShapeDtypeStruct(woa.shape, F32), jax.ShapeDtypeStruct(w_spatial.shape, F32),
                   jax.ShapeDtypeStruct((CHUNK, CHUNK), F32), jax.ShapeDtypeStruct((2, A_WIDTH), F32)]
        + exchange.out_shape,
        in_specs=[_rows(tm, D_MODEL), _rows(tm, 3 * A_WIDTH), _resident(ln_gb.shape),
                  _resident(w_spatial.shape), _resident(bsp_t.shape), _resident(woa.shape)] + [ANY] * exchange.n,
        out_specs=[_rows(tm, 3 * A_WIDTH), ANY, fixed(w_spatial.shape),
                   fixed((CHUNK, CHUNK)), fixed((2, A_WIDTH))] + [ANY] * exchange.n,
        scratch_shapes=[pltpu.VMEM(woa.shape, F32), pltpu.VMEM((tm, A_WIDTH), F32),
                        pltpu.VMEM((tm, A_WIDTH), F32), pltpu.VMEM((tm, A_WIDTH), F32)] + exchange.scratch_shapes,
        compiler_params=_seq_params(),
    )(dr0, p, ln_gb, w_spatial, bsp_t, woa, *exchange.inputs)


def _input_grad_blocks(seq):
    tm = min(TOKENS_DX, seq // 8)
    return tm, seq // tm


def _input_grad(dp, dr0, wa, first, n_blocks, earlier, exchange):
    seq = dr0.shape[0]
    tm, _ = _input_grad_blocks(seq)
    shard_cols = wa.shape[2]
    n_own = 3 if earlier is None else 4

    def body(dp_ref, dr0_ref, wa_ref, *refs):
        refs = refs[n_own - 3:]
        ex_in, dx_ref, ex_out, ex_sems = (refs[:exchange.n], refs[exchange.n], refs[exchange.n + 1:2 * exchange.n + 1],
                                          refs[2 * exchange.n + 1:])
        step = pl.program_id(0)

        @pl.when(step == 0)
        def _():
            exchange.start(ex_in, ex_out, ex_sems)

        dx = ALPHA * dr0_ref[...]
        for k in range(N_DEV):
            dx = dx + _nt(dp_ref[:, k * shard_cols:(k + 1) * shard_cols], wa_ref[k])
        dx_ref[...] = dx

        @pl.when(step == n_blocks - 1)
        def _():
            exchange.wait(ex_in, ex_out, ex_sems)

    def rows(width):
        return pl.BlockSpec((tm, width), lambda i: (first + i, 0))

    return pl.pallas_call(
        body, name=f"input_grad_{first}", grid=(n_blocks,),
        out_shape=[jax.ShapeDtypeStruct((seq, D_MODEL), F32)] + exchange.out_shape,
        in_specs=[rows(3 * A_WIDTH), rows(D_MODEL), _resident(wa.shape)] + [ANY] * (n_own - 3 + exchange.n),
        out_specs=[rows(D_MODEL)] + [ANY] * exchange.n,
        scratch_shapes=exchange.scratch_shapes,
        input_output_aliases={} if earlier is None else {3: 0},
        compiler_params=_seq_params(),
    )(dp, dr0, wa, *(() if earlier is None else (earlier,)), *exchange.inputs)


def _in_proj_weight_grad(xb, dp, shard_cols, exchange):
    seq = xb.shape[0]
    tk = min(TOKENS_WGRAD, seq)
    n_k = seq // tk

    def body(x_ref, dp_ref, *refs):
        ex_in, out_ref, ex_out, ex_sems = (refs[:exchange.n], refs[exchange.n], refs[exchange.n + 1:2 * exchange.n + 1],
                                           refs[2 * exchange.n + 1:])
        shard, step = pl.program_id(0), pl.program_id(1)

        @pl.when((shard == 0) & (step == 0))
        def _():
            exchange.start(ex_in, ex_out, ex_sems)

        @pl.when(step == 0)
        def _():
            out_ref[...] = jnp.zeros_like(out_ref)

        out_ref[...] += _tn(x_ref[...], dp_ref[...])

        @pl.when((shard == N_DEV - 1) & (step == n_k - 1))
        def _():
            exchange.wait(ex_in, ex_out, ex_sems)

    return pl.pallas_call(
        body, name="in_proj_weight_grad", grid=(N_DEV, n_k),
        out_shape=[jax.ShapeDtypeStruct((N_DEV, D_MODEL, shard_cols), F32)] + exchange.out_shape,
        in_specs=[pl.BlockSpec((tk, D_MODEL), lambda j, k: (k, 0)),
                  pl.BlockSpec((tk, shard_cols), lambda j, k: (k, j))] + [ANY] * exchange.n,
        out_specs=[pl.BlockSpec((None, D_MODEL, shard_cols), lambda j, k: (j, 0, 0))] + [ANY] * exchange.n,
        scratch_shapes=exchange.scratch_shapes,
        compiler_params=pltpu.CompilerParams(dimension_semantics=("arbitrary", "arbitrary"),
                                             vmem_limit_bytes=VMEM_LIMIT_BYTES),
    )(xb, dp, *exchange.inputs)


class _Exchange:
    def __init__(self, pair=(), chip=()):
        self.pair, self.chip = tuple(pair), tuple(chip)
        self.inputs = self.pair + self.chip
        self.n = len(self.inputs)
        self.out_shape = [jax.ShapeDtypeStruct((4, 1) + g.shape[2:], F32) for g in self.pair]
        self.out_shape += [jax.ShapeDtypeStruct((3,) + s.shape[1:], s.dtype) for s in self.chip]
        self.scratch_shapes = [pltpu.SemaphoreType.DMA((max(len(self.pair), 1),)) for _ in range(2)]
        self.scratch_shapes += [pltpu.SemaphoreType.DMA((max(len(self.chip), 1), 3)) for _ in range(2)]

    def _copies(self, ins, outs, sems):
        if not self.n:
            return []
        pair_send, pair_recv, chip_send, chip_recv = sems
        x, y, c = lax.axis_index("x"), lax.axis_index("y"), lax.axis_index("c")
        copies = []
        for a in range(len(self.pair)):
            copies.append(pltpu.make_async_remote_copy(
                src_ref=ins[a].at[:, pl.ds(1 - c, 1)], dst_ref=outs[a],
                send_sem=pair_send.at[a], recv_sem=pair_recv.at[a], device_id=(x, y, 1 - c), device_id_type=MESH))
        for a in range(len(self.chip)):
            src, dst = ins[len(self.pair) + a], outs[len(self.pair) + a]
            for k, (px, py) in enumerate([(1 - x, y), (x, 1 - y), (1 - x, 1 - y)]):
                copies.append(pltpu.make_async_remote_copy(
                    src_ref=src.at[2 * px + py], dst_ref=dst.at[k], send_sem=chip_send.at[a, k],
                    recv_sem=chip_recv.at[a, k], device_id=(px, py, c), device_id_type=MESH))
        return copies

    def start(self, ins, outs, sems):
        for cp in self._copies(ins, outs, sems):
            cp.start()

    def wait(self, ins, outs, sems):
        for cp in self._copies(ins, outs, sems):
            cp.wait()


def _row_tile(rows, cols, block_bytes=1 << 20, itemsize=4):
    if rows * cols * itemsize <= block_bytes:
        return rows
    t = 8
    while 2 * t * cols * itemsize <= block_bytes and rows % (2 * t) == 0:
        t *= 2
    assert rows % t == 0, (rows, cols)
    return t


def _pair_sum(place, grad, landed, dtype):
    _, _, rows, cols = grad.shape
    tr = _row_tile(rows, cols, block_bytes=4 << 20)

    def body(place_ref, a_ref, b_ref, out_ref):
        out_ref[...] = (a_ref[...] + b_ref[...]).astype(dtype)

    return pl.pallas_call(
        body, name="pair_sum", out_shape=jax.ShapeDtypeStruct((4, rows, cols), dtype),
        grid_spec=pltpu.PrefetchScalarGridSpec(
            num_scalar_prefetch=1, grid=(4, rows // tr),
            in_specs=[pl.BlockSpec((None, None, tr, cols), lambda k, i, pr: (k, pr[1], i, 0)),
                      pl.BlockSpec((None, None, tr, cols), lambda k, i, pr: (k, 0, i, 0))],
            out_specs=pl.BlockSpec((None, tr, cols), lambda k, i, pr: (k, i, 0))),
        compiler_params=pltpu.CompilerParams(dimension_semantics=("parallel", "parallel"),
                                             vmem_limit_bytes=VMEM_LIMIT_BYTES),
    )(place, grad, landed)


def _adamw_math(w, g, m, v):
    m = ADAM_B1 * m + (1.0 - ADAM_B1) * g
    v = ADAM_B2 * v + (1.0 - ADAM_B2) * (g * g)
    m_hat = m / (1.0 - ADAM_B1 ** ADAM_STEP)
    v_hat = v / (1.0 - ADAM_B2 ** ADAM_STEP)
    delta = -ADAM_LR * (m_hat / (jnp.sqrt(v_hat) + ADAM_EPS) + ADAM_WD * w)
    return delta, m, v


def _adamw_shard(place, grad, landed, got, w, m, v):
    rows, cols = w.shape
    tr = _row_tile(rows, cols)

    def body(place_ref, a_ref, b_ref, got_ref, w_ref, m_ref, v_ref, g_out, d_out, m_out, v_out):
        g = a_ref[...] + b_ref[...]
        for k in range(3):
            g = g + got_ref[k].astype(F32)
        g_out[...] = g
        d_out[...], m_out[...], v_out[...] = _adamw_math(w_ref[...], g, m_ref[...], v_ref[...])

    spec = pl.BlockSpec((tr, cols), lambda i, pr: (i, 0))
    out = jax.ShapeDtypeStruct((rows, cols), F32)
    return pl.pallas_call(
        body, name="adamw_shard", out_shape=[out] * 4,
        grid_spec=pltpu.PrefetchScalarGridSpec(
            num_scalar_prefetch=1, grid=(rows // tr,),
            in_specs=[pl.BlockSpec((None, None, tr, cols), lambda i, pr: (pr[0], pr[1], i, 0)),
                      pl.BlockSpec((None, None, tr, cols), lambda i, pr: (pr[0], 0, i, 0)),
                      pl.BlockSpec((3, tr, cols), lambda i, pr: (0, i, 0)), spec, spec, spec],
            out_specs=[spec] * 4),
        compiler_params=pltpu.CompilerParams(dimension_semantics=("parallel",)),
    )(place, grad, landed, got, w, m, v)


def _gather_small(grad, landed, got):
    rows, cols = got.shape[1:]

    def body(grad_ref, landed_ref, got_ref, out_ref, send_sems, recv_sems):
        x, y, c = lax.axis_index("x"), lax.axis_index("y"), lax.axis_index("c")
        me, sibling = (x, y, c), (x, y, 1 - c)
        chips = [(1 - x, y), (x, 1 - y), (1 - x, 1 - y)]

        def slot(px, py, pc):
            return out_ref.at[4 * px + 2 * py + pc]

        def copy(k, block, to):
            return pltpu.make_async_remote_copy(
                src_ref=slot(*block), dst_ref=slot(*block), send_sem=send_sems.at[k], recv_sem=recv_sems.at[k],
                device_id=to, device_id_type=MESH)

        mine = grad_ref[2 * x + y, c] + landed_ref[2 * x + y, 0]
        out_ref[4 * x + 2 * y + c] = ((mine + got_ref[0]) + got_ref[1]) + got_ref[2]
        first = [copy(0, me, sibling)] + [copy(1 + j, me, (*chip, c)) for j, chip in enumerate(chips)]
        for cp in first:
            cp.start()
        passed = [copy(4 + j, (*chip, c), sibling) for j, chip in enumerate(chips)]
        for j, chip in enumerate(chips):
            copy(1 + j, (*chip, c), me).wait_recv()
            passed[j].start()
        copy(0, sibling, me).wait_recv()
        for j, chip in enumerate(chips):
            copy(4 + j, (*chip, 1 - c), me).wait_recv()
        for cp in first + passed:
            cp.wait_send()

    return pl.pallas_call(
        body, name="gather_small", out_shape=jax.ShapeDtypeStruct((N_DEV, rows, cols), F32),
        in_specs=[VMEM] * 3, out_specs=VMEM,
        scratch_shapes=[pltpu.SemaphoreType.DMA((7,)), pltpu.SemaphoreType.DMA((7,))],
    )(grad, landed, got)


def _adamw_small(params):
    n = len(params)

    def body(*refs):
        ins, outs = refs[:4 * n], refs[4 * n:]
        for i in range(n):
            w_ref, g_ref, m_ref, v_ref = ins[4 * i:4 * i + 4]
            d, m, v = _adamw_math(w_ref[...], g_ref[...], m_ref[...], v_ref[...])
            outs[3 * i][...] = d
            outs[3 * i + 1][...] = m
            outs[3 * i + 2][...] = v

    flat = [a for p in params for a in p]
    out_shape = [jax.ShapeDtypeStruct(p[0].shape, F32) for p in params for _ in range(3)]
    outs = pl.pallas_call(
        body, name="adamw_small", out_shape=out_shape, in_specs=[VMEM] * (4 * n), out_specs=[VMEM] * (3 * n),
    )(*flat)
    return [tuple(outs[3 * i:3 * i + 3]) for i in range(n)]


def _pack_small(parts):
    rows = []
    for a in parts:
        r, c = a.shape
        rows.append(jnp.pad(a, ((0, (-r) % 8), (0, CHUNK - c))))
    flat = jnp.concatenate(rows, axis=0)
    return jnp.pad(flat, ((0, N_DEV * SMALL_ROWS - flat.shape[0]), (0, 0))).reshape(N_DEV, SMALL_ROWS, CHUNK)


def _unpack_small(flat, shapes):
    out, at = [], 0
    for r, c in shapes:
        out.append(flat[at:at + r, :c])
        at += r + (-r) % 8
    return out


def kernel(x, w_in_a, sgu_ln_g, sgu_ln_b, w_spatial, b_spatial, w_out_a, w_kv, w_in_b, attn_sinks, rel_bias, w_out_b, post_ln_g, post_ln_b, loss_target, m_w_in_a, m_sgu_ln_g, m_sgu_ln_b, m_w_spatial, m_b_spatial, m_w_out_a, m_w_kv, m_w_in_b, m_attn_sinks, m_rel_bias, m_w_out_b, m_post_ln_g, m_post_ln_b, v_w_in_a, v_sgu_ln_g, v_sgu_ln_b, v_w_spatial, v_b_spatial, v_w_out_a, v_w_kv, v_w_in_b, v_attn_sinks, v_rel_bias, v_w_out_b, v_post_ln_g, v_post_ln_b):
    seq = x.shape[1]
    me = 4 * lax.axis_index("x") + 2 * lax.axis_index("y") + lax.axis_index("c")
    bucket = jnp.asarray(_band_tables())

    ln_shard = jnp.concatenate([sgu_ln_g, sgu_ln_b], axis=0)
    x2, tgt = x[0], loss_target[0]
    order = jnp.stack([2 * px + py for px, py, _ in _use_order()[::2]]).astype(jnp.int32)
    p, wa, woa, ln_all = _gather_in_proj(order, x2, w_in_a[0], (w_out_a[0],), ln_shard)
    woa = woa.reshape(A_WIDTH, D_MODEL)
    ln_gb = jnp.transpose(ln_all, (1, 0, 2)).reshape(2, A_WIDTH)
    ws = w_spatial[0]
    bsp_t = jnp.pad(b_spatial[0].T, ((0, 0), (0, CHUNK - A_GROUPS)))
    bias = _bias_table(rel_bias, bucket)

    r0, xb, wkv, wb, wob = _fwd_a(x2, p, ln_gb, ws, bsp_t, woa, (w_kv, w_in_b[0], w_out_b[0]))
    wkv = wkv.reshape(D_MODEL, 2 * KV_WIDTH)
    wob = wob.reshape(B_WIDTH, D_MODEL)
    q, z, kv, dr1, loss, dg1, db1 = _fwd_b(r0, tgt, wb, wkv, wob, post_ln_g, post_ln_b, attn_sinks, bias)
    dr0, dwb, dwkv, dwob, dsink, dlog, dpost0 = _bwd_b(r0, dr1, q, z, kv, wb, wkv, wob, post_ln_g, post_ln_b,
                                                       attn_sinks, bias)
    drel = _rel_bias_grad(dlog, bucket)

    place = jnp.stack([2 * lax.axis_index("x") + lax.axis_index("y"), lax.axis_index("c")]).astype(jnp.int32)

    def by_device(g):
        return g.reshape((4, 2, g.shape[0] // N_DEV) + g.shape[1:]) if g.ndim == 2 else g.reshape((4, 2) + g.shape[1:])

    dwb, dwkv, dwob = by_device(dwb), by_device(dwkv), by_device(dwob)
    dp, dwoa, dws, dbsp_t, dln, l_wb, l_wkv, l_wob = _bwd_a(dr0, p, ln_gb, ws, bsp_t, woa,
                                                            _Exchange(pair=(dwb, dwkv, dwob)))
    dwoa = by_device(dwoa)
    small_shapes = [(A_GROUPS * CHUNK, CHUNK), (A_GROUPS, CHUNK), (1, N_Q_HEADS), (REL_BUCKETS, N_Q_HEADS),
                    (16, CHUNK), (16, CHUNK), (16, CHUNK), (16, CHUNK), (1, 1)]
    small = by_device(_pack_small([
        dws.reshape(A_GROUPS * CHUNK, CHUNK), dbsp_t[:, :A_GROUPS].T, dsink[:, :N_Q_HEADS], drel[:, :N_Q_HEADS],
        jnp.concatenate([dpost0[0:1], dg1], axis=0).reshape(16, CHUNK),
        jnp.concatenate([dpost0[1:2], db1], axis=0).reshape(16, CHUNK),
        dln[0].reshape(16, CHUNK), dln[1].reshape(16, CHUNK), loss]))
    s_wb, s_wkv, s_wob = (_pair_sum(place, g, l, BF16) for g, l in ((dwb, l_wb), (dwkv, l_wkv), (dwob, l_wob)))
    dwa, l_woa, l_small, g_wb, g_wkv, g_wob = _in_proj_weight_grad(
        xb, dp, wa.shape[2], _Exchange(pair=(dwoa, small), chip=(s_wb, s_wkv, s_wob)))
    dwa = by_device(dwa)
    s_woa, s_small = _pair_sum(place, dwoa, l_woa, BF16), _pair_sum(place, small, l_small, F32)
    _, blocks = _input_grad_blocks(seq)
    cut0, cut1 = 3 * blocks // 8, blocks - max(1, blocks // 16)
    dx, l_wa, g_woa, g_small = _input_grad(dp, dr0, wa, 0, cut0, None,
                                           _Exchange(pair=(dwa,), chip=(s_woa, s_small)))
    s_wa = _pair_sum(place, dwa, l_wa, BF16)
    dx, g_wa = _input_grad(dp, dr0, wa, cut0, cut1 - cut0, dx, _Exchange(chip=(s_wa,)))
    dx, = _input_grad(dp, dr0, wa, cut1, blocks - cut1, dx, _Exchange())
    full = [dwa, dwoa, dwkv, dwb, dwob, small]
    landed = [l_wa, l_woa, l_wkv, l_wb, l_wob, l_small]
    got = [g_wa, g_woa, g_wkv, g_wb, g_wob, g_small]

    big = [(w_in_a, m_w_in_a, v_w_in_a), (w_out_a, m_w_out_a, v_w_out_a), (w_kv, m_w_kv, v_w_kv),
           (w_in_b, m_w_in_b, v_w_in_b), (w_out_b, m_w_out_b, v_w_out_b)]
    big_out = []
    for i, (w, m, v) in enumerate(big):
        shape2 = got[i].shape[1:]
        res = _adamw_shard(place, full[i], landed[i], got[i], w.reshape(shape2), m.reshape(shape2),
                           v.reshape(shape2))
        big_out.append([r.reshape(w.shape) for r in res])

    small_all = _gather_small(full[5], landed[5], got[5]).reshape(N_DEV * SMALL_ROWS, CHUNK)
    g_ws, g_bsp, g_sink, g_rel, g_pg, g_pb, g_lng, g_lnb, loss_all = _unpack_small(small_all, small_shapes)
    g_ws = g_ws.reshape(w_spatial.shape)
    g_bsp = g_bsp.reshape(b_spatial.shape)
    g_pg = g_pg.reshape(post_ln_g.shape)
    g_pb = g_pb.reshape(post_ln_b.shape)
    g_lng = lax.dynamic_slice(g_lng.reshape(1, A_WIDTH), (0, me * sgu_ln_g.shape[1]), sgu_ln_g.shape)
    g_lnb = lax.dynamic_slice(g_lnb.reshape(1, A_WIDTH), (0, me * sgu_ln_b.shape[1]), sgu_ln_b.shape)
    small_params = [(sgu_ln_g, g_lng, m_sgu_ln_g, v_sgu_ln_g), (sgu_ln_b, g_lnb, m_sgu_ln_b, v_sgu_ln_b),
                    (w_spatial, g_ws, m_w_spatial, v_w_spatial), (b_spatial, g_bsp, m_b_spatial, v_b_spatial),
                    (attn_sinks, g_sink, m_attn_sinks, v_attn_sinks), (rel_bias, g_rel, m_rel_bias, v_rel_bias),
                    (post_ln_g, g_pg, m_post_ln_g, v_post_ln_g), (post_ln_b, g_pb, m_post_ln_b, v_post_ln_b)]
    flat2 = [tuple(a.reshape(-1, a.shape[-1]) for a in p) for p in small_params]
    small_out = [tuple(r.reshape(p[0].shape) for r in res) for res, p in zip(_adamw_small(flat2), small_params)]

    order = [("big", 0), ("small", 0), ("small", 1), ("small", 2), ("small", 3), ("big", 1), ("big", 2), ("big", 3),
             ("small", 4), ("small", 5), ("big", 4), ("small", 6), ("small", 7)]
    grads, deltas, new_m, new_v = [], [], [], []
    for kind, i in order:
        if kind == "big":
            g, d, m, v = big_out[i]
        else:
            g = small_params[i][1]
            d, m, v = small_out[i]
        grads.append(g)
        deltas.append(d)
        new_m.append(m)
        new_v.append(v)
    return (loss_all[0, 0], dx.reshape(x.shape), *grads, *deltas, *new_m, *new_v)
```

```python
import functools

import jax
import jax.numpy as jnp
import numpy as np
from jax import lax
from jax.experimental import pallas as pl
from jax.experimental.pallas import tpu as pltpu

F32 = jnp.float32
BF16 = jnp.bfloat16

D_MODEL = 1024
A_WIDTH = 2048
A_GROUPS = 8
A_GROUP_DIM = A_WIDTH // A_GROUPS
CHUNK = 128
HEAD_DIM = 64
N_Q_HEADS = 16
N_KV_HEADS = 2
PAIRS_PER_KV = N_Q_HEADS // N_KV_HEADS // 2
N_PAIRS = N_Q_HEADS // 2
B_WIDTH = N_Q_HEADS * HEAD_DIM
KV_WIDTH = N_KV_HEADS * HEAD_DIM
REL_BUCKETS = 32
REL_MAX_DIST = 128
DEPTH = 2
ALPHA = (2.0 * DEPTH) ** 0.25
LN_EPS = 1e-5
NEG_INF = -1e30
SCALE = HEAD_DIM ** -0.5

ADAM_LR = 0.001
ADAM_B1 = 0.9
ADAM_B2 = 0.999
ADAM_EPS = 1e-08
ADAM_WD = 0.01
ADAM_STEP = 10

N_DEV = 8
MESH = pl.DeviceIdType.MESH
TOKENS_IN_PROJ = 1024
TOKENS_WGRAD = 4096
TOKENS_A = 256
TOKENS_A_FWD = 512
TOKENS_B = 512
PART_ROWS_B = 256
TOKENS_B_BWD = 256
TOKENS_DX = 512
VMEM_LIMIT_BYTES = 60 * 1024 * 1024
SMALL_ROWS = 144

ANY = pl.BlockSpec(memory_space=pl.ANY)
VMEM = pl.BlockSpec(memory_space=pltpu.VMEM)
SMEM = pl.BlockSpec(memory_space=pltpu.SMEM)


def _nn(a, b):
    return jnp.dot(a, b, preferred_element_type=F32)


def _nt(a, b):
    return lax.dot_general(a, b, (((1,), (1,)), ((), ())), preferred_element_type=F32)


def _tn(a, b):
    return lax.dot_general(a, b, (((0,), (0,)), ((), ())), preferred_element_type=F32)


def _resident(shape):
    nd = len(shape)
    return pl.BlockSpec(shape, lambda i: (0,) * nd, pipeline_mode=pl.Buffered(1))


def _rows(tm, width, reverse_of=None):
    if reverse_of is None:
        return pl.BlockSpec((tm, width), lambda i: (i, 0))
    return pl.BlockSpec((tm, width), lambda i: (reverse_of - 1 - i, 0))


def _seq_params():
    return pltpu.CompilerParams(dimension_semantics=("arbitrary",), vmem_limit_bytes=VMEM_LIMIT_BYTES)


def _in_turns(first, second):
    merged = [(i / max(len(first), 1), 0, f) for i, f in enumerate(first)]
    merged += [((i + 0.5) / max(len(second), 1), 1, f) for i, f in enumerate(second)]
    return [f for _, _, f in sorted(merged, key=lambda t: t[:2])]


def _ln_normalize(r):
    mu = jnp.mean(r, axis=-1, keepdims=True)
    xc = r - mu
    var = jnp.mean(xc * xc, axis=-1, keepdims=True)
    rstd = lax.rsqrt(var + LN_EPS)
    return xc * rstd, rstd


def _ln_backward(dxhat, xhat, rstd):
    m1 = jnp.mean(dxhat, axis=-1, keepdims=True)
    m2 = jnp.mean(dxhat * xhat, axis=-1, keepdims=True)
    return rstd * (dxhat - m1 - xhat * m2)


def _silu_and_grad(z):
    sig = jax.nn.sigmoid(z)
    silu = z * sig
    return silu, sig + silu * (1.0 - sig)


def _band_tables():
    t = np.arange(CHUNK, dtype=np.int32)[:, None]
    j = np.arange(2 * CHUNK, dtype=np.int32)[None, :]
    d = t + CHUNK - j
    in_window = (d >= 0) & (d < CHUNK)
    dc = np.clip(d, 0, REL_MAX_DIST - 1)
    max_exact = REL_BUCKETS // 2
    df = np.maximum(dc, 1).astype(np.float32)
    large = max_exact + (np.log(df / np.float32(max_exact)) / np.float32(np.log(REL_MAX_DIST / max_exact))
                         * np.float32(REL_BUCKETS - max_exact)).astype(np.int32)
    large = np.minimum(large, REL_BUCKETS - 1)
    bucket = np.where(dc < max_exact, dc, large)
    return np.where(in_window, bucket, -1).astype(np.int32)


def _use_order():
    x, y, c = lax.axis_index("x"), lax.axis_index("y"), lax.axis_index("c")
    blocks = [(x, y, c), (x, y, 1 - c)]
    for px, py in [(1 - x, y), (x, 1 - y), (1 - x, 1 - y)]:
        blocks += [(px, py, c), (px, py, 1 - c)]
    return blocks


def _gather_in_proj(order, x, w_in_a, small, ln_gb):
    seq = x.shape[0]
    tm = min(TOKENS_IN_PROJ, seq)
    n_blocks = seq // tm
    shard_cols = w_in_a.shape[1]
    n = len(small) + 2

    def body(order_ref, x_ref, *refs):
        ins, refs = refs[:n], refs[n:]
        p_ref, outs, refs = refs[0], refs[1:n + 1], refs[n + 1:]
        wa_vmem, xb_vmem, stage = refs[0], refs[1], refs[2:n]
        send_sems, recv_sems, local_sems, out_sems = refs[n:]
        phase, blk = pl.program_id(0), pl.program_id(1)
        x_, y_, c_ = lax.axis_index("x"), lax.axis_index("y"), lax.axis_index("c")
        me, sibling = (x_, y_, c_), (x_, y_, 1 - c_)
        chips = [(1 - x_, y_), (x_, 1 - y_), (1 - x_, 1 - y_)]
        use = _use_order()

        def index(px, py, pc):
            return 4 * px + 2 * py + pc

        def slot(a, block):
            return (wa_vmem if a == 0 else outs[a]).at[index(*block)]

        def own(a):
            return slot(0, me) if a == 0 else (ins[a] if a == n - 1 else stage[a - 1])

        def copy(a, k, block, to, src=None):
            return pltpu.make_async_remote_copy(
                src_ref=slot(a, block) if src is None else src, dst_ref=slot(a, block),
                send_sem=send_sems.at[a, k], recv_sem=recv_sems.at[a, k], device_id=to, device_id_type=MESH)

        def keep(a):
            return pltpu.make_async_copy(own(a), slot(a, me), local_sems.at[a])

        def write_out(k):
            return pltpu.make_async_copy(slot(0, use[k]), outs[0].at[index(*use[k])], out_sems.at[k])

        def pass_on(a, j):
            copy(a, 1 + j, (*chips[j], c_), me).wait_recv()
            copy(a, 4 + j, (*chips[j], c_), sibling).start()

        def send(a, j):
            copy(a, 1 + j, me, (*chips[j], c_), own(a)).start()

        @pl.when((phase == 0) & (blk == 0))
        def _():
            wa_vmem[index(*me)] = ins[0][...].astype(BF16)
            for a in range(1, n - 1):
                stage[a - 1][...] = ins[a][...].astype(BF16)
            send(0, 0)
            send(0, 1)
            for a in range(n):
                copy(a, 0, me, sibling, own(a)).start()
                if a:
                    keep(a).start()
            write_out(0).start()
            copy(0, 0, sibling, me).wait_recv()
            write_out(1).start()

        for j in range(3):
            @pl.when((phase == 1 + j) & (blk == 0))
            def _(j=j):
                pass_on(0, j)
                write_out(2 + 2 * j).start()
                if j == 0:
                    send(0, 2)
                    for a in range(1, n):
                        for jj in range(3):
                            send(a, jj)
                copy(0, 4 + j, (*chips[j], 1 - c_), me).wait_recv()
                write_out(3 + 2 * j).start()
                if j == 2:
                    for a in range(1, n):
                        pass_on(a, 0)
                        pass_on(a, 1)

        rows = pl.ds(pl.multiple_of(blk * tm, tm), tm)

        @pl.when(phase == 0)
        def _():
            xb_vmem[rows, :] = x_ref[...].astype(BF16)

        chip = order_ref[phase]
        xb = xb_vmem[rows, :]
        p_ref[:, :shard_cols] = _nn(xb, wa_vmem[2 * chip])
        p_ref[:, shard_cols:] = _nn(xb, wa_vmem[2 * chip + 1])

        @pl.when((phase == N_DEV // 2 - 1) & (blk == n_blocks - 1))
        def _():
            for a in range(1, n):
                pass_on(a, 2)
            for a in range(1, n):
                copy(a, 0, sibling, me).wait_recv()
                for j, chip in enumerate(chips):
                    copy(a, 4 + j, (*chip, 1 - c_), me).wait_recv()
            for a in range(n):
                for k in range(7):
                    copy(a, k, me, me).wait_send()
                if a:
                    keep(a).wait()
            for k in range(N_DEV):
                write_out(k).wait()

    gathered = [jax.ShapeDtypeStruct((N_DEV,) + w_in_a.shape, BF16)]
    gathered += [jax.ShapeDtypeStruct((N_DEV,) + s.shape, BF16) for s in small]
    gathered.append(jax.ShapeDtypeStruct((N_DEV,) + ln_gb.shape, F32))
    return pl.pallas_call(
        body, name="gather_in_proj",
        out_shape=[jax.ShapeDtypeStruct((seq, N_DEV * shard_cols), F32)] + gathered,
        grid_spec=pltpu.PrefetchScalarGridSpec(
            num_scalar_prefetch=1, grid=(N_DEV // 2, n_blocks),
            in_specs=[pl.BlockSpec((tm, D_MODEL), lambda k, i, order: (jnp.where(k == 0, i, n_blocks - 1), 0))]
            + [VMEM] * n,
            out_specs=[pl.BlockSpec((tm, 2 * shard_cols), lambda k, i, order: (i, order[k]))] + [ANY] * n,
            scratch_shapes=[pltpu.VMEM((N_DEV,) + w_in_a.shape, BF16), pltpu.VMEM((seq, D_MODEL), BF16)]
            + [pltpu.VMEM(s.shape, BF16) for s in small] + [
                pltpu.SemaphoreType.DMA((n, 7)), pltpu.SemaphoreType.DMA((n, 7)), pltpu.SemaphoreType.DMA((n,)),
                pltpu.SemaphoreType.DMA((N_DEV,))]),
        compiler_params=pltpu.CompilerParams(dimension_semantics=("arbitrary", "arbitrary"),
                                             vmem_limit_bytes=VMEM_LIMIT_BYTES),
    )(order, x, w_in_a, *small, ln_gb)


def _bias_table(rel_bias, bucket):
    def body(rel_ref, bucket_ref, out_ref):
        first = jnp.broadcast_to(bucket_ref[0:1, :], (8, 2 * CHUNK))
        col = lax.broadcasted_iota(jnp.int32, (CHUNK, 2 * CHUNK), 1)
        for h in range(N_Q_HEADS):
            row = jnp.full(first.shape, NEG_INF, F32)
            for b in range(REL_BUCKETS):
                row = jnp.where(first == b, rel_ref[b, h], row)
            tab = pltpu.roll(jnp.broadcast_to(row[0:1, :], (CHUNK, 2 * CHUNK)), 0, 1, stride=1, stride_axis=0)
            pair = h // 2
            rows = slice((pair % PAIRS_PER_KV) * CHUNK, (pair % PAIRS_PER_KV + 1) * CHUNK)
            half = slice((h % 2) * 2 * CHUNK, (h % 2 + 1) * 2 * CHUNK)
            out_ref[1, pair // PAIRS_PER_KV, rows, half] = tab
            out_ref[0, pair // PAIRS_PER_KV, rows, half] = jnp.where(col < CHUNK, NEG_INF, tab)

    return pl.pallas_call(
        body, name="bias_table",
        out_shape=jax.ShapeDtypeStruct((2, N_KV_HEADS, PAIRS_PER_KV * CHUNK, 4 * CHUNK), F32),
        in_specs=[SMEM, VMEM], out_specs=VMEM,
    )(rel_bias, bucket)


def _rel_bias_grad(dlogits, bucket):
    def body(dl_ref, bucket_ref, out_ref):
        i = lax.broadcasted_iota(jnp.int32, (CHUNK, CHUNK), 0)
        k = lax.broadcasted_iota(jnp.int32, (CHUNK, CHUNK), 1)
        reverse = jnp.where(i + k == CHUNK - 1, 1.0, 0.0)
        sums = []
        for h in range(N_Q_HEADS):
            pair = h // 2
            dl = dl_ref[pair // PAIRS_PER_KV, (pair % PAIRS_PER_KV) * CHUNK:(pair % PAIRS_PER_KV + 1) * CHUNK,
                        (h % 2) * 2 * CHUNK:(h % 2 + 1) * 2 * CHUNK]
            backwards = jnp.dot(reverse, dl, precision=lax.Precision.HIGHEST, preferred_element_type=F32)
            turned = pltpu.roll(backwards, CHUNK + 1, 1, stride=1, stride_axis=0)
            sums.append(jnp.sum(turned, axis=0, keepdims=True))
        by_offset = jnp.concatenate(sums, axis=0)
        buckets = lax.broadcasted_iota(jnp.int32, (REL_BUCKETS, 2 * CHUNK), 0)
        one_hot = jnp.where(jnp.broadcast_to(bucket_ref[0:1, :], buckets.shape) == buckets, 1.0, 0.0)
        out_ref[...] = lax.dot_general(one_hot, by_offset, (((1,), (1,)), ((), ())),
                                       precision=lax.Precision.HIGHEST, preferred_element_type=F32)

    return pl.pallas_call(
        body, name="rel_bias_grad", out_shape=jax.ShapeDtypeStruct((REL_BUCKETS, N_Q_HEADS), F32),
        in_specs=[VMEM, VMEM], out_specs=VMEM,
    )(dlogits, bucket)


def _masked_spatial(ws_ref, g):
    t = lax.broadcasted_iota(jnp.int32, (CHUNK, CHUNK), 0)
    s = lax.broadcasted_iota(jnp.int32, (CHUNK, CHUNK), 1)
    return jnp.where(s <= t, ws_ref[g], 0.0).astype(BF16)


def _spatial_gate(ws_ref, bsp_t_ref, vn, s_ref, n_chunks):
    for g in range(A_GROUPS):
        w = _masked_spatial(ws_ref, g)
        cols = slice(g * A_GROUP_DIM, (g + 1) * A_GROUP_DIM)
        for ci in range(n_chunks):
            rows = slice(ci * CHUNK, (ci + 1) * CHUNK)
            s_ref[rows, cols] = _nn(w, vn[rows, cols]) + bsp_t_ref[:, g:g + 1]


def _fwd_a(x, p, ln_gb, w_spatial, bsp_t, woa, later):
    seq = x.shape[0]
    tm = min(TOKENS_A_FWD, seq)
    n_chunks = tm // CHUNK
    n_blocks = seq // tm
    n = len(later)

    def body(x_ref, p_ref, ln_ref, ws_ref, bsp_ref, woa_ref, *refs):
        shards, refs = refs[:n], refs[n:]
        r0_ref, xb_ref, gathered, refs = refs[0], refs[1], refs[2:2 + n], refs[2 + n:]
        s_ref, stage, (send_sems, recv_sems, local_sems) = refs[0], refs[1:1 + n], refs[1 + n:]
        step = pl.program_id(0)

        def copies(a):
            x_, y_, c_ = lax.axis_index("x"), lax.axis_index("y"), lax.axis_index("c")
            mine = 4 * x_ + 2 * y_ + c_
            out = [pltpu.make_async_copy(stage[a], gathered[a].at[mine], local_sems.at[a])]
            for r in range(1, N_DEV):
                to = (1 - x_ if r & 4 else x_, 1 - y_ if r & 2 else y_, 1 - c_ if r & 1 else c_)
                out.append(pltpu.make_async_remote_copy(
                    src_ref=stage[a], dst_ref=gathered[a].at[mine], send_sem=send_sems.at[a, r - 1],
                    recv_sem=recv_sems.at[a, r - 1], device_id=to, device_id_type=MESH))
            return out

        @pl.when(step == 0)
        def _():
            for a in range(n):
                stage[a][...] = shards[a][...].astype(BF16)
                for cp in copies(a):
                    cp.start()

        xv = x_ref[...]
        xb_ref[...] = xv.astype(BF16)
        vhat, _ = _ln_normalize(p_ref[:, A_WIDTH:2 * A_WIDTH])
        vn = (vhat * ln_ref[0:1, :] + ln_ref[1:2, :]).astype(BF16)
        _spatial_gate(ws_ref, bsp_ref, vn, s_ref, n_chunks)
        z = p_ref[:, 2 * A_WIDTH:]
        y = p_ref[:, :A_WIDTH] * s_ref[...] * (z * jax.nn.sigmoid(z))
        r0_ref[...] = ALPHA * xv + _nn(y.astype(BF16), woa_ref[...])

        @pl.when(step == n_blocks - 1)
        def _():
            for a in range(n):
                for cp in copies(a):
                    cp.wait()

    return pl.pallas_call(
        body, name="fwd_a", grid=(n_blocks,),
        out_shape=[jax.ShapeDtypeStruct((seq, D_MODEL), F32), jax.ShapeDtypeStruct((seq, D_MODEL), BF16)]
        + [jax.ShapeDtypeStruct((N_DEV,) + s.shape, BF16) for s in later],
        in_specs=[_rows(tm, D_MODEL), _rows(tm, 3 * A_WIDTH), _resident(ln_gb.shape), _resident(w_spatial.shape),
                  _resident(bsp_t.shape), _resident(woa.shape)] + [VMEM] * n,
        out_specs=[_rows(tm, D_MODEL), _rows(tm, D_MODEL)] + [ANY] * n,
        scratch_shapes=[pltpu.VMEM((tm, A_WIDTH), F32)] + [pltpu.VMEM(s.shape, BF16) for s in later] + [
            pltpu.SemaphoreType.DMA((n, N_DEV - 1)), pltpu.SemaphoreType.DMA((n, N_DEV - 1)),
            pltpu.SemaphoreType.DMA((n,))],
        compiler_params=_seq_params(),
    )(x, p, ln_gb, w_spatial, bsp_t, woa, *later)


def _expand_heads(kv):
    lo = lax.broadcasted_iota(jnp.int32, kv.shape, 1) < HEAD_DIM
    swapped = pltpu.roll(kv, HEAD_DIM, 1)
    zero = jnp.zeros_like(kv)
    return (jnp.where(lo, kv, zero).astype(BF16), jnp.where(lo, zero, swapped).astype(BF16),
            jnp.where(lo, swapped, zero).astype(BF16), jnp.where(lo, zero, kv).astype(BF16))


def _fold_heads(acc, kvh):
    top, bot = acc[:2 * CHUNK], acc[2 * CHUNK:]
    lo = lax.broadcasted_iota(jnp.int32, top.shape, 1) < HEAD_DIM
    t = jnp.where(lo, top, 0.0)
    b = jnp.where(lo, 0.0, bot)
    if kvh == 0:
        return t + pltpu.roll(b, HEAD_DIM, 1)
    return pltpu.roll(t, HEAD_DIM, 1) + b


STACK = PAIRS_PER_KV * CHUNK


def _pair_cols(kvh, pp):
    pair = kvh * PAIRS_PER_KV + pp
    return slice(pair * CHUNK, (pair + 1) * CHUNK)


def _stack_pairs(ref_or_val, rows, kvh):
    return jnp.concatenate([ref_or_val[rows, _pair_cols(kvh, pp)] for pp in range(PAIRS_PER_KV)], axis=0)


def _sink_columns(sink_ref):
    blk = lax.broadcasted_iota(jnp.int32, (STACK, 1), 0) // CHUNK
    out = []
    for kvh in range(N_KV_HEADS):
        halves = []
        for half in range(2):
            col = jnp.zeros((STACK, 1), F32)
            for pp in range(PAIRS_PER_KV):
                col = jnp.where(blk == pp, sink_ref[0, 2 * (kvh * PAIRS_PER_KV + pp) + half], col)
            halves.append(col)
        out.append(halves)
    return out


def _softmax_stack(logits, sink_cols):
    ps, sinks = [], []
    for half in range(2):
        lg = logits[:, half * 2 * CHUNK:(half + 1) * 2 * CHUNK]
        sk = sink_cols[half]
        m = jnp.maximum(jnp.max(lg, axis=-1, keepdims=True), sk)
        e = jnp.exp(lg - m)
        es = jnp.exp(sk - m)
        inv = 1.0 / (jnp.sum(e, axis=-1, keepdims=True) + es)
        ps.append(e * inv)
        sinks.append(es * inv)
    return jnp.concatenate(ps, axis=1), sinks


def _band_operands(before_k, this_k, before_v, this_v, kvh):
    kblk = jnp.concatenate([before_k[2 * kvh], this_k[2 * kvh], before_k[2 * kvh + 1], this_k[2 * kvh + 1]], axis=0)
    vblk = jnp.concatenate([before_v[2 * kvh], this_v[2 * kvh], before_v[2 * kvh + 1], this_v[2 * kvh + 1]], axis=0)
    return kblk, vblk


def _fwd_b(r0, target, wb, wkv, wob, post_g, post_b, sinks, bias):
    seq = r0.shape[0]
    tm = min(TOKENS_B, seq)
    n_chunks = tm // CHUNK
    part_rows = min(PART_ROWS_B, tm)
    n_parts = tm // part_rows
    part_chunks = part_rows // CHUNK
    shard_cols = wb.shape[2]

    def body(r0_ref, tgt_ref, wb_ref, wkv_ref, wob_ref, pg_ref, pb_ref, sink_ref, bias_ref,
             q_ref, z_ref, kv_ref, dr1_ref, loss_ref, dg_ref, db_ref, ek_last, ev_last, *scratch):
        q_s, z_s, h1_s, o_s, ek_s, ev_s = (scratch[i * n_parts:(i + 1) * n_parts] for i in range(6))
        step = pl.program_id(0)

        @pl.when(step == 0)
        def _():
            ek_last[...] = jnp.zeros_like(ek_last)
            ev_last[...] = jnp.zeros_like(ev_last)
            loss_ref[...] = jnp.zeros_like(loss_ref)
            dg_ref[...] = jnp.zeros_like(dg_ref)
            db_ref[...] = jnp.zeros_like(db_ref)

        sink_cols = _sink_columns(sink_ref)
        g1 = pg_ref[1:2, :]

        def project_steps(part):
            rows = slice(part * part_rows, (part + 1) * part_rows)
            live = {}

            def normalize():
                xhat0, _ = _ln_normalize(r0_ref[rows, :])
                h1 = xhat0 * pg_ref[0:1, :] + pb_ref[0:1, :]
                h1_s[part][...] = h1
                live["h1b"] = h1.astype(BF16)

            def shard(k):
                def run():
                    qz = _nn(live["h1b"], wb_ref[k])
                    lo, hi = k * shard_cols, (k + 1) * shard_cols
                    if hi <= B_WIDTH:
                        qs = (qz * SCALE).astype(BF16)
                        q_ref[rows, lo:hi] = qs
                        q_s[part][:, lo:hi] = qs
                    else:
                        z_ref[rows, lo - B_WIDTH:hi - B_WIDTH] = qz
                        z_s[part][:, lo - B_WIDTH:hi - B_WIDTH] = qz
                return run

            def keys():
                kv = _nn(live["h1b"], wkv_ref[...])
                kv_ref[rows, :] = kv.astype(BF16)
                for cj in range(part_chunks):
                    sub = slice(cj * CHUNK, (cj + 1) * CHUNK)
                    for i, e in enumerate(_expand_heads(kv[sub, :KV_WIDTH])):
                        ek_s[part][cj, i] = e
                    for i, e in enumerate(_expand_heads(kv[sub, KV_WIDTH:])):
                        ev_s[part][cj, i] = e

            return [normalize, keys] + [shard(k) for k in range(N_DEV)]

        def attend_steps(part):
            def stack(cj, kvh):
                def run():
                    sub = slice(cj * CHUNK, (cj + 1) * CHUNK)
                    sel = jnp.where(step * n_chunks + part * part_chunks + cj > 0, 1, 0)
                    if cj:
                        before_k, before_v = ek_s[part].at[cj - 1], ev_s[part].at[cj - 1]
                    elif part:
                        before_k, before_v = ek_s[part - 1].at[part_chunks - 1], ev_s[part - 1].at[part_chunks - 1]
                    else:
                        before_k, before_v = ek_last, ev_last
                    kblk, vblk = _band_operands(before_k, ek_s[part].at[cj], before_v, ev_s[part].at[cj], kvh)
                    logits = _nt(_stack_pairs(q_s[part], sub, kvh), kblk) + bias_ref[sel, kvh]
                    p, _ = _softmax_stack(logits, sink_cols[kvh])
                    o = _nn(p.astype(BF16), vblk)
                    for pp in range(PAIRS_PER_KV):
                        o_s[part][sub, _pair_cols(kvh, pp)] = o[pp * CHUNK:(pp + 1) * CHUNK]
                return run

            def finish():
                rows = slice(part * part_rows, (part + 1) * part_rows)
                z = z_s[part][...]
                y = o_s[part][...] * (z * jax.nn.sigmoid(z))
                r1 = ALPHA * h1_s[part][...] + _nn(y.astype(BF16), wob_ref[...])
                xhat1, rstd1 = _ln_normalize(r1)
                diff = xhat1 * g1 + pb_ref[1:2, :] - tgt_ref[rows, :]
                loss_ref[...] += 0.5 * jnp.sum(jnp.mean(diff * diff, axis=-1, keepdims=True), axis=0, keepdims=True)
                dh2 = diff * (1.0 / D_MODEL)
                dg_ref[...] += jnp.sum(dh2 * xhat1, axis=0, keepdims=True)
                db_ref[...] += jnp.sum(dh2, axis=0, keepdims=True)
                dr1_ref[rows, :] = _ln_backward(dh2 * g1, xhat1, rstd1)

            return [stack(cj, kvh) for cj in range(part_chunks) for kvh in range(N_KV_HEADS)] + [finish]

        for run in project_steps(0):
            run()
        for part in range(n_parts):
            for run in _in_turns(attend_steps(part), project_steps(part + 1) if part + 1 < n_parts else []):
                run()
        ek_last[...] = ek_s[n_parts - 1][part_chunks - 1]
        ev_last[...] = ev_s[n_parts - 1][part_chunks - 1]

    vec = pl.BlockSpec((1, D_MODEL), lambda i: (0, 0))
    return pl.pallas_call(
        body, name="fwd_b", grid=(seq // tm,),
        out_shape=[jax.ShapeDtypeStruct((seq, B_WIDTH), BF16), jax.ShapeDtypeStruct((seq, B_WIDTH), F32),
                   jax.ShapeDtypeStruct((seq, 2 * KV_WIDTH), BF16), jax.ShapeDtypeStruct((seq, D_MODEL), F32),
                   jax.ShapeDtypeStruct((1, 1), F32), jax.ShapeDtypeStruct((1, D_MODEL), F32),
                   jax.ShapeDtypeStruct((1, D_MODEL), F32)],
        in_specs=[_rows(tm, D_MODEL), _rows(tm, D_MODEL), _resident(wb.shape), _resident(wkv.shape),
                  _resident(wob.shape), _resident(post_g.shape), _resident(post_b.shape), SMEM,
                  _resident(bias.shape)],
        out_specs=[_rows(tm, B_WIDTH), _rows(tm, B_WIDTH), _rows(tm, 2 * KV_WIDTH), _rows(tm, D_MODEL),
                   pl.BlockSpec((1, 1), lambda i: (0, 0)), vec, vec],
        scratch_shapes=[pltpu.VMEM((4, CHUNK, CHUNK), BF16), pltpu.VMEM((4, CHUNK, CHUNK), BF16)]
        + [pltpu.VMEM((part_rows, B_WIDTH), BF16)] * n_parts + [pltpu.VMEM((part_rows, B_WIDTH), F32)] * n_parts
        + [pltpu.VMEM((part_rows, D_MODEL), F32)] * n_parts + [pltpu.VMEM((part_rows, B_WIDTH), F32)] * n_parts
        + [pltpu.VMEM((part_chunks, 4, CHUNK, CHUNK), BF16)] * (2 * n_parts),
        compiler_params=_seq_params(),
    )(r0, target, wb, wkv, wob, post_g, post_b, sinks, bias)


def _bwd_b(r0, dr1, q, z, kv, wb, wkv, wob, post_g, post_b, sinks, bias):
    seq = r0.shape[0]
    tm = min(TOKENS_B_BWD, seq)
    n_chunks = tm // CHUNK
    n_blocks = seq // tm
    shard_cols = wb.shape[2]

    def body(r0_ref, dr1_ref, q_ref, z_ref, kv_ref, kvp_ref, wb_ref, wkv_ref, wob_ref, pg_ref, pb_ref, sink_ref,
             bias_ref, dr0_ref, dwb_out, dwkv_out, dwob_out, dsink_ref, dl_out, dpost_ref,
             dwb_acc, dwkv_acc, dwob_acc, dl_acc, dsink_acc, carry_ref, ek_ref, ev_ref, y_ref, dqz_ref, dkv_ref):
        step = pl.program_id(0)
        block = n_blocks - 1 - step

        @pl.when(step == 0)
        def _():
            dpost_ref[...] = jnp.zeros_like(dpost_ref)
            dwb_acc[...] = jnp.zeros_like(dwb_acc)
            dwkv_acc[...] = jnp.zeros_like(dwkv_acc)
            dwob_acc[...] = jnp.zeros_like(dwob_acc)
            dl_acc[...] = jnp.zeros_like(dl_acc)
            carry_ref[...] = jnp.zeros_like(carry_ref)
            dsink_acc[...] = jnp.zeros_like(dsink_acc)

        xhat0, rstd0 = _ln_normalize(r0_ref[...])
        h1b = (xhat0 * pg_ref[0:1, :] + pb_ref[0:1, :]).astype(BF16)
        dr1 = dr1_ref[...]
        dr1b = dr1.astype(BF16)
        dy = _nt(dr1b, wob_ref[...])
        sg, dsg = _silu_and_grad(z_ref[...])
        do_all = dy * sg
        dyz = dy * dsg

        for ci in range(n_chunks + 1):
            src = kvp_ref[...] if ci == 0 else kv_ref[(ci - 1) * CHUNK:ci * CHUNK, :]
            src = src.astype(F32)
            for i, e in enumerate(_expand_heads(src[:, :KV_WIDTH])):
                ek_ref[ci, i] = e
            for i, e in enumerate(_expand_heads(src[:, KV_WIDTH:])):
                ev_ref[ci, i] = e

        sink_cols = _sink_columns(sink_ref)
        for ci in reversed(range(n_chunks)):
            rows = slice(ci * CHUNK, (ci + 1) * CHUNK)
            sel = jnp.where(block * n_chunks + ci > 0, 1, 0)
            dk_band = jnp.zeros((2 * CHUNK, KV_WIDTH), F32)
            dv_band = jnp.zeros((2 * CHUNK, KV_WIDTH), F32)
            for kvh in range(N_KV_HEADS):
                kblk, vblk = _band_operands(ek_ref.at[ci], ek_ref.at[ci + 1], ev_ref.at[ci], ev_ref.at[ci + 1], kvh)
                qs = _stack_pairs(q_ref, rows, kvh)
                p, psink = _softmax_stack(_nt(qs, kblk) + bias_ref[sel, kvh], sink_cols[kvh])
                pb16 = p.astype(BF16)
                o = _nn(pb16, vblk)
                y = (o * _stack_pairs(sg, rows, kvh)).astype(BF16)
                dz = (o * _stack_pairs(dyz, rows, kvh)).astype(BF16)
                dob = _stack_pairs(do_all, rows, kvh).astype(BF16)
                dp = _nt(dob, vblk)
                dlogits = []
                for half in range(2):
                    hs = slice(half * 2 * CHUNK, (half + 1) * 2 * CHUNK)
                    delta = jnp.sum(dp[:, hs] * p[:, hs], axis=-1, keepdims=True)
                    dlogits.append(p[:, hs] * (dp[:, hs] - delta))
                    dsink_acc[2 * kvh + half] += psink[half] * delta
                dlog = jnp.concatenate(dlogits, axis=1)
                dl_acc[kvh] += dlog
                dsc = dlog.astype(BF16)
                dq = (_nn(dsc, kblk) * SCALE).astype(BF16)
                for pp in range(PAIRS_PER_KV):
                    blk = slice(pp * CHUNK, (pp + 1) * CHUNK)
                    cols = _pair_cols(kvh, pp)
                    y_ref[rows, cols] = y[blk]
                    dqz_ref[rows, cols] = dq[blk]
                    dqz_ref[rows, B_WIDTH + cols.start:B_WIDTH + cols.stop] = dz[blk]
                dk_band = dk_band + _fold_heads(_tn(dsc, qs), kvh)
                dv_band = dv_band + _fold_heads(_tn(pb16, dob), kvh)
            band = jnp.concatenate([dk_band, dv_band], axis=1)
            dkv_ref[rows, :] = (band[CHUNK:] + carry_ref[...]).astype(BF16)
            carry_ref[...] = band[:CHUNK]

        dqz = dqz_ref[...]
        dkv = dkv_ref[...]
        dh1 = ALPHA * dr1 + _nt(dkv, wkv_ref[...])
        for k in range(N_DEV):
            part = dqz[:, k * shard_cols:(k + 1) * shard_cols]
            dh1 = dh1 + _nt(part, wb_ref[k])
            dwb_acc[k] += _tn(h1b, part)
        dpost_ref[0:1, :] += jnp.sum(dh1 * xhat0, axis=0, keepdims=True)
        dpost_ref[1:2, :] += jnp.sum(dh1, axis=0, keepdims=True)
        dr0_ref[...] = _ln_backward(dh1 * pg_ref[0:1, :], xhat0, rstd0)
        dwkv_acc[...] += _tn(h1b, dkv)
        dwob_acc[...] += _tn(y_ref[...], dr1b)

        @pl.when(step == n_blocks - 1)
        def _():
            pltpu.sync_copy(dwb_acc, dwb_out)
            pltpu.sync_copy(dwkv_acc, dwkv_out)
            pltpu.sync_copy(dwob_acc, dwob_out)
            pltpu.sync_copy(dl_acc, dl_out)
            lane = lax.broadcasted_iota(jnp.int32, (1, CHUNK), 1)
            dsink = jnp.zeros((1, CHUNK), F32)
            for kvh in range(N_KV_HEADS):
                for half in range(2):
                    for pp in range(PAIRS_PER_KV):
                        head = 2 * (kvh * PAIRS_PER_KV + pp) + half
                        tot = jnp.sum(dsink_acc[2 * kvh + half, pp * CHUNK:(pp + 1) * CHUNK, :], axis=0, keepdims=True)
                        dsink = dsink - jnp.where(lane == head, tot, 0.0)
            dsink_ref[...] = dsink

    rev = functools.partial(_rows, tm, reverse_of=n_blocks)
    prev_chunk = pl.BlockSpec(
        (CHUNK, 2 * KV_WIDTH), lambda i: (jnp.maximum((n_blocks - 1 - i) * n_chunks - 1, 0), 0))
    return pl.pallas_call(
        body, name="bwd_b", grid=(n_blocks,),
        out_shape=[jax.ShapeDtypeStruct((seq, D_MODEL), F32), jax.ShapeDtypeStruct(wb.shape, F32),
                   jax.ShapeDtypeStruct(wkv.shape, F32), jax.ShapeDtypeStruct(wob.shape, F32),
                   jax.ShapeDtypeStruct((1, CHUNK), F32),
                   jax.ShapeDtypeStruct((N_KV_HEADS, STACK, 4 * CHUNK), F32), jax.ShapeDtypeStruct((2, D_MODEL), F32)],
        in_specs=[rev(D_MODEL), rev(D_MODEL), rev(B_WIDTH), rev(B_WIDTH), rev(2 * KV_WIDTH), prev_chunk,
                  _resident(wb.shape), _resident(wkv.shape), _resident(wob.shape), _resident(post_g.shape),
                  _resident(post_b.shape), SMEM, _resident(bias.shape)],
        out_specs=[rev(D_MODEL), ANY, ANY, ANY, pl.BlockSpec((1, CHUNK), lambda i: (0, 0)), ANY,
                   pl.BlockSpec((2, D_MODEL), lambda i: (0, 0))],
        scratch_shapes=[pltpu.VMEM(wb.shape, F32), pltpu.VMEM(wkv.shape, F32), pltpu.VMEM(wob.shape, F32),
                        pltpu.VMEM((N_KV_HEADS, STACK, 4 * CHUNK), F32), pltpu.VMEM((2 * N_KV_HEADS, STACK, 1), F32),
                        pltpu.VMEM((CHUNK, 2 * KV_WIDTH), F32),
                        pltpu.VMEM((n_chunks + 1, 4, CHUNK, CHUNK), BF16),
                        pltpu.VMEM((n_chunks + 1, 4, CHUNK, CHUNK), BF16),
                        pltpu.VMEM((tm, B_WIDTH), BF16), pltpu.VMEM((tm, 2 * B_WIDTH), BF16),
                        pltpu.VMEM((tm, 2 * KV_WIDTH), BF16)],
        compiler_params=_seq_params(),
    )(r0, dr1, q, z, kv, kv, wb, wkv, wob, post_g, post_b, sinks, bias)


def _bwd_a(dr0, p, ln_gb, w_spatial, bsp_t, woa, exchange):
    seq = dr0.shape[0]
    tm = min(TOKENS_A, seq)
    n_chunks = tm // CHUNK
    n_blocks = seq // tm

    def body(dr0_ref, p_ref, ln_ref, ws_ref, bsp_ref, woa_ref, *refs):
        ex_in, refs = refs[:exchange.n], refs[exchange.n:]
        (dp_ref, dwoa_out, dws_ref, dbsp_ref, dln_ref), refs = refs[:5], refs[5:]
        ex_out, refs = refs[:exchange.n], refs[exchange.n:]
        (dwoa_acc, s_ref, ds_ref, dvn_ref), ex_sems = refs[:4], refs[4:]
        step = pl.program_id(0)

        @pl.when(step == 0)
        def _():
            exchange.start(ex_in, ex_out, ex_sems)
            dwoa_acc[...] = jnp.zeros_like(dwoa_acc)
            dws_ref[...] = jnp.zeros_like(dws_ref)
            dbsp_ref[...] = jnp.zeros_like(dbsp_ref)
            dln_ref[...] = jnp.zeros_like(dln_ref)

        dr0b = dr0_ref[...].astype(BF16)

        vhat, rstd_v = _ln_normalize(p_ref[:, A_WIDTH:2 * A_WIDTH])
        ln_g = ln_ref[0:1, :]
        vn = (vhat * ln_g + ln_ref[1:2, :]).astype(BF16)
        _spatial_gate(ws_ref, bsp_ref, vn, s_ref, n_chunks)
        u = p_ref[:, :A_WIDTH]
        s = s_ref[...]
        sg, dsg = _silu_and_grad(p_ref[:, 2 * A_WIDTH:])
        gate = s * sg
        dwoa_acc[...] += _tn((u * gate).astype(BF16), dr0b)
        dy = _nt(dr0b, woa_ref[...])
        dyu = dy * u
        dp_ref[:, :A_WIDTH] = (dy * gate).astype(BF16)
        dp_ref[:, 2 * A_WIDTH:] = (dyu * (s * dsg)).astype(BF16)
        ds_ref[...] = dyu * sg

        lane = lax.broadcasted_iota(jnp.int32, (CHUNK, CHUNK), 1)
        dbsp = jnp.zeros((CHUNK, CHUNK), F32)
        for g in range(A_GROUPS):
            w = _masked_spatial(ws_ref, g)
            cols = slice(g * A_GROUP_DIM, (g + 1) * A_GROUP_DIM)
            dws = jnp.zeros((CHUNK, CHUNK), F32)
            for ci in range(n_chunks):
                rows = slice(ci * CHUNK, (ci + 1) * CHUNK)
                ds = ds_ref[rows, cols]
                dsb = ds.astype(BF16)
                dvn_ref[rows, cols] = _tn(w, dsb)
                dws = dws + _nt(dsb, vn[rows, cols])
                dbsp = dbsp + jnp.where(lane == g, jnp.sum(ds, axis=-1, keepdims=True), 0.0)
            dws_ref[g] += dws
        dbsp_ref[...] += dbsp

        dvn = dvn_ref[...]
        dln_ref[0:1, :] += jnp.sum(dvn * vhat, axis=0, keepdims=True)
        dln_ref[1:2, :] += jnp.sum(dvn, axis=0, keepdims=True)
        dp_ref[:, A_WIDTH:2 * A_WIDTH] = _ln_backward(dvn * ln_g, vhat, rstd_v).astype(BF16)

        @pl.when(step == n_blocks - 1)
        def _():
            t = lax.broadcasted_iota(jnp.int32, (CHUNK, CHUNK), 0)
            for g in range(A_GROUPS):
                dws_ref[g] = jnp.where(lane <= t, dws_ref[g], 0.0)
            pltpu.sync_copy(dwoa_acc, dwoa_out)
            exchange.wait(ex_in, ex_out, ex_sems)

    def fixed(shape):
        nd = len(shape)
        return pl.BlockSpec(shape, lambda i: (0,) * nd)

    return pl.pallas_call(
        body, name="bwd_a", grid=(n_blocks,),
        out_shape=[jax.ShapeDtypeStruct((seq, 3 * A_WIDTH), BF16),
                   jax.ShapeDtypeStruct(woa.shape, F32), jax.ShapeDtypeStruct(w_spatial.shape, F32),
                   jax.ShapeDtypeStruct((CHUNK, CHUNK), F32), jax.ShapeDtypeStruct((2, A_WIDTH), F32)]
        + exchange.out_shape,
        in_specs=[_rows(tm, D_MODEL), _rows(tm, 3 * A_WIDTH), _resident(ln_gb.shape),
                  _resident(w_spatial.shape), _resident(bsp_t.shape), _resident(woa.shape)] + [ANY] * exchange.n,
        out_specs=[_rows(tm, 3 * A_WIDTH), ANY, fixed(w_spatial.shape),
                   fixed((CHUNK, CHUNK)), fixed((2, A_WIDTH))] + [ANY] * exchange.n,
        scratch_shapes=[pltpu.VMEM(woa.shape, F32), pltpu.VMEM((tm, A_WIDTH), F32),
                        pltpu.VMEM((tm, A_WIDTH), F32), pltpu.VMEM((tm, A_WIDTH), F32)] + exchange.scratch_shapes,
        compiler_params=_seq_params(),
    )(dr0, p, ln_gb, w_spatial, bsp_t, woa, *exchange.inputs)


def _input_grad_blocks(seq):
    tm = min(TOKENS_DX, seq // 8)
    return tm, seq // tm


def _input_grad(dp, dr0, wa, first, n_blocks, earlier, exchange):
    seq = dr0.shape[0]
    tm, _ = _input_grad_blocks(seq)
    shard_cols = wa.shape[2]
    n_own = 3 if earlier is None else 4

    def body(dp_ref, dr0_ref, wa_ref, *refs):
        refs = refs[n_own - 3:]
        ex_in, dx_ref, ex_out, ex_sems = (refs[:exchange.n], refs[exchange.n], refs[exchange.n + 1:2 * exchange.n + 1],
                                          refs[2 * exchange.n + 1:])
        step = pl.program_id(0)

        @pl.when(step == 0)
        def _():
            exchange.start(ex_in, ex_out, ex_sems)

        dx = ALPHA * dr0_ref[...]
        for k in range(N_DEV):
            dx = dx + _nt(dp_ref[:, k * shard_cols:(k + 1) * shard_cols], wa_ref[k])
        dx_ref[...] = dx

        @pl.when(step == n_blocks - 1)
        def _():
            exchange.wait(ex_in, ex_out, ex_sems)

    def rows(width):
        return pl.BlockSpec((tm, width), lambda i: (first + i, 0))

    return pl.pallas_call(
        body, name=f"input_grad_{first}", grid=(n_blocks,),
        out_shape=[jax.ShapeDtypeStruct((seq, D_MODEL), F32)] + exchange.out_shape,
        in_specs=[rows(3 * A_WIDTH), rows(D_MODEL), _resident(wa.shape)] + [ANY] * (n_own - 3 + exchange.n),
        out_specs=[rows(D_MODEL)] + [ANY] * exchange.n,
        scratch_shapes=exchange.scratch_shapes,
        input_output_aliases={} if earlier is None else {3: 0},
        compiler_params=_seq_params(),
    )(dp, dr0, wa, *(() if earlier is None else (earlier,)), *exchange.inputs)


def _in_proj_weight_grad(xb, dp, shard_cols, exchange):
    seq = xb.shape[0]
    tk = min(TOKENS_WGRAD, seq)
    n_k = seq // tk

    def body(x_ref, dp_ref, *refs):
        ex_in, out_ref, ex_out, ex_sems = (refs[:exchange.n], refs[exchange.n], refs[exchange.n + 1:2 * exchange.n + 1],
                                           refs[2 * exchange.n + 1:])
        shard, step = pl.program_id(0), pl.program_id(1)

        @pl.when((shard == 0) & (step == 0))
        def _():
            exchange.start(ex_in, ex_out, ex_sems)

        @pl.when(step == 0)
        def _():
            out_ref[...] = jnp.zeros_like(out_ref)

        out_ref[...] += _tn(x_ref[...], dp_ref[...])

        @pl.when((shard == N_DEV - 1) & (step == n_k - 1))
        def _():
            exchange.wait(ex_in, ex_out, ex_sems)

    return pl.pallas_call(
        body, name="in_proj_weight_grad", grid=(N_DEV, n_k),
        out_shape=[jax.ShapeDtypeStruct((N_DEV, D_MODEL, shard_cols), F32)] + exchange.out_shape,
        in_specs=[pl.BlockSpec((tk, D_MODEL), lambda j, k: (k, 0)),
                  pl.BlockSpec((tk, shard_cols), lambda j, k: (k, j))] + [ANY] * exchange.n,
        out_specs=[pl.BlockSpec((None, D_MODEL, shard_cols), lambda j, k: (j, 0, 0))] + [ANY] * exchange.n,
        scratch_shapes=exchange.scratch_shapes,
        compiler_params=pltpu.CompilerParams(dimension_semantics=("arbitrary", "arbitrary"),
                                             vmem_limit_bytes=VMEM_LIMIT_BYTES),
    )(xb, dp, *exchange.inputs)


class _Exchange:
    def __init__(self, pair=(), chip=()):
        self.pair, self.chip = tuple(pair), tuple(chip)
        self.inputs = self.pair + self.chip
        self.n = len(self.inputs)
        self.out_shape = [jax.ShapeDtypeStruct((4, 1) + g.shape[2:], F32) for g in self.pair]
        self.out_shape += [jax.ShapeDtypeStruct((3,) + s.shape[1:], s.dtype) for s in self.chip]
        self.scratch_shapes = [pltpu.SemaphoreType.DMA((max(len(self.pair), 1),)) for _ in range(2)]
        self.scratch_shapes += [pltpu.SemaphoreType.DMA((max(len(self.chip), 1), 3)) for _ in range(2)]

    def _copies(self, ins, outs, sems):
        if not self.n:
            return []
        pair_send, pair_recv, chip_send, chip_recv = sems
        x, y, c = lax.axis_index("x"), lax.axis_index("y"), lax.axis_index("c")
        copies = []
        for a in range(len(self.pair)):
            copies.append(pltpu.make_async_remote_copy(
                src_ref=ins[a].at[:, pl.ds(1 - c, 1)], dst_ref=outs[a],
                send_sem=pair_send.at[a], recv_sem=pair_recv.at[a], device_id=(x, y, 1 - c), device_id_type=MESH))
        for a in range(len(self.chip)):
            src, dst = ins[len(self.pair) + a], outs[len(self.pair) + a]
            for k, (px, py) in enumerate([(1 - x, y), (x, 1 - y), (1 - x, 1 - y)]):
                copies.append(pltpu.make_async_remote_copy(
                    src_ref=src.at[2 * px + py], dst_ref=dst.at[k], send_sem=chip_send.at[a, k],
                    recv_sem=chip_recv.at[a, k], device_id=(px, py, c), device_id_type=MESH))
        return copies

    def start(self, ins, outs, sems):
        for cp in self._copies(ins, outs, sems):
            cp.start()

    def wait(self, ins, outs, sems):
        for cp in self._copies(ins, outs, sems):
            cp.wait()


def _row_tile(rows, cols, block_bytes=1 << 20, itemsize=4):
    if rows * cols * itemsize <= block_bytes:
        return rows
    t = 8
    while 2 * t * cols * itemsize <= block_bytes and rows % (2 * t) == 0:
        t *= 2
    assert rows % t == 0, (rows, cols)
    return t


def _pair_sum(place, grad, landed, dtype):
    _, _, rows, cols = grad.shape
    tr = _row_tile(rows, cols, block_bytes=4 << 20)

    def body(place_ref, a_ref, b_ref, out_ref):
        out_ref[...] = (a_ref[...] + b_ref[...]).astype(dtype)

    return pl.pallas_call(
        body, name="pair_sum", out_shape=jax.ShapeDtypeStruct((4, rows, cols), dtype),
        grid_spec=pltpu.PrefetchScalarGridSpec(
            num_scalar_prefetch=1, grid=(4, rows // tr),
            in_specs=[pl.BlockSpec((None, None, tr, cols), lambda k, i, pr: (k, pr[1], i, 0)),
                      pl.BlockSpec((None, None, tr, cols), lambda k, i, pr: (k, 0, i, 0))],
            out_specs=pl.BlockSpec((None, tr, cols), lambda k, i, pr: (k, i, 0))),
        compiler_params=pltpu.CompilerParams(dimension_semantics=("parallel", "parallel"),
                                             vmem_limit_bytes=VMEM_LIMIT_BYTES),
    )(place, grad, landed)


def _adamw_math(w, g, m, v):
    m = ADAM_B1 * m + (1.0 - ADAM_B1) * g
    v = ADAM_B2 * v + (1.0 - ADAM_B2) * (g * g)
    m_hat = m / (1.0 - ADAM_B1 ** ADAM_STEP)
    v_hat = v / (1.0 - ADAM_B2 ** ADAM_STEP)
    delta = -ADAM_LR * (m_hat / (jnp.sqrt(v_hat) + ADAM_EPS) + ADAM_WD * w)
    return delta, m, v


def _adamw_shard(place, grad, landed, got, w, m, v):
    rows, cols = w.shape
    tr = _row_tile(rows, cols)

    def body(place_ref, a_ref, b_ref, got_ref, w_ref, m_ref, v_ref, g_out, d_out, m_out, v_out):
        g = a_ref[...] + b_ref[...]
        for k in range(3):
            g = g + got_ref[k].astype(F32)
        g_out[...] = g
        d_out[...], m_out[...], v_out[...] = _adamw_math(w_ref[...], g, m_ref[...], v_ref[...])

    spec = pl.BlockSpec((tr, cols), lambda i, pr: (i, 0))
    out = jax.ShapeDtypeStruct((rows, cols), F32)
    return pl.pallas_call(
        body, name="adamw_shard", out_shape=[out] * 4,
        grid_spec=pltpu.PrefetchScalarGridSpec(
            num_scalar_prefetch=1, grid=(rows // tr,),
            in_specs=[pl.BlockSpec((None, None, tr, cols), lambda i, pr: (pr[0], pr[1], i, 0)),
                      pl.BlockSpec((None, None, tr, cols), lambda i, pr: (pr[0], 0, i, 0)),
                      pl.BlockSpec((3, tr, cols), lambda i, pr: (0, i, 0)), spec, spec, spec],
            out_specs=[spec] * 4),
        compiler_params=pltpu.CompilerParams(dimension_semantics=("parallel",)),
    )(place, grad, landed, got, w, m, v)


def _gather_small(grad, landed, got):
    rows, cols = got.shape[1:]

    def body(grad_ref, landed_ref, got_ref, out_ref, send_sems, recv_sems):
        x, y, c = lax.axis_index("x"), lax.axis_index("y"), lax.axis_index("c")
        me, sibling = (x, y, c), (x, y, 1 - c)
        chips = [(1 - x, y), (x, 1 - y), (1 - x, 1 - y)]

        def slot(px, py, pc):
            return out_ref.at[4 * px + 2 * py + pc]

        def copy(k, block, to):
            return pltpu.make_async_remote_copy(
                src_ref=slot(*block), dst_ref=slot(*block), send_sem=send_sems.at[k], recv_sem=recv_sems.at[k],
                device_id=to, device_id_type=MESH)

        mine = grad_ref[2 * x + y, c] + landed_ref[2 * x + y, 0]
        out_ref[4 * x + 2 * y + c] = ((mine + got_ref[0]) + got_ref[1]) + got_ref[2]
        first = [copy(0, me, sibling)] + [copy(1 + j, me, (*chip, c)) for j, chip in enumerate(chips)]
        for cp in first:
            cp.start()
        passed = [copy(4 + j, (*chip, c), sibling) for j, chip in enumerate(chips)]
        for j, chip in enumerate(chips):
            copy(1 + j, (*chip, c), me).wait_recv()
            passed[j].start()
        copy(0, sibling, me).wait_recv()
        for j, chip in enumerate(chips):
            copy(4 + j, (*chip, 1 - c), me).wait_recv()
        for cp in first + passed:
            cp.wait_send()

    return pl.pallas_call(
        body, name="gather_small", out_shape=jax.ShapeDtypeStruct((N_DEV, rows, cols), F32),
        in_specs=[VMEM] * 3, out_specs=VMEM,
        scratch_shapes=[pltpu.SemaphoreType.DMA((7,)), pltpu.SemaphoreType.DMA((7,))],
    )(grad, landed, got)


def _adamw_small(params):
    n = len(params)

    def body(*refs):
        ins, outs = refs[:4 * n], refs[4 * n:]
        for i in range(n):
            w_ref, g_ref, m_ref, v_ref = ins[4 * i:4 * i + 4]
            d, m, v = _adamw_math(w_ref[...], g_ref[...], m_ref[...], v_ref[...])
            outs[3 * i][...] = d
            outs[3 * i + 1][...] = m
            outs[3 * i + 2][...] = v

    flat = [a for p in params for a in p]
    out_shape = [jax.ShapeDtypeStruct(p[0].shape, F32) for p in params for _ in range(3)]
    outs = pl.pallas_call(
        body, name="adamw_small", out_shape=out_shape, in_specs=[VMEM] * (4 * n), out_specs=[VMEM] * (3 * n),
    )(*flat)
    return [tuple(outs[3 * i:3 * i + 3]) for i in range(n)]


def _pack_small(parts):
    rows = []
    for a in parts:
        r, c = a.shape
        rows.append(jnp.pad(a, ((0, (-r) % 8), (0, CHUNK - c))))
    flat = jnp.concatenate(rows, axis=0)
    return jnp.pad(flat, ((0, N_DEV * SMALL_ROWS - flat.shape[0]), (0, 0))).reshape(N_DEV, SMALL_ROWS, CHUNK)


def _unpack_small(flat, shapes):
    out, at = [], 0
    for r, c in shapes:
        out.append(flat[at:at + r, :c])
        at += r + (-r) % 8
    return out


def kernel(x, w_in_a, sgu_ln_g, sgu_ln_b, w_spatial, b_spatial, w_out_a, w_kv, w_in_b, attn_sinks, rel_bias, w_out_b, post_ln_g, post_ln_b, loss_target, m_w_in_a, m_sgu_ln_g, m_sgu_ln_b, m_w_spatial, m_b_spatial, m_w_out_a, m_w_kv, m_w_in_b, m_attn_sinks, m_rel_bias, m_w_out_b, m_post_ln_g, m_post_ln_b, v_w_in_a, v_sgu_ln_g, v_sgu_ln_b, v_w_spatial, v_b_spatial, v_w_out_a, v_w_kv, v_w_in_b, v_attn_sinks, v_rel_bias, v_w_out_b, v_post_ln_g, v_post_ln_b):
    seq = x.shape[1]
    me = 4 * lax.axis_index("x") + 2 * lax.axis_index("y") + lax.axis_index("c")
    bucket = jnp.asarray(_band_tables())

    ln_shard = jnp.concatenate([sgu_ln_g, sgu_ln_b], axis=0)
    x2, tgt = x[0], loss_target[0]
    order = jnp.stack([2 * px + py for px, py, _ in _use_order()[::2]]).astype(jnp.int32)
    p, wa, woa, ln_all = _gather_in_proj(order, x2, w_in_a[0], (w_out_a[0],), ln_shard)
    woa = woa.reshape(A_WIDTH, D_MODEL)
    ln_gb = jnp.transpose(ln_all, (1, 0, 2)).reshape(2, A_WIDTH)
    ws = w_spatial[0]
    bsp_t = jnp.pad(b_spatial[0].T, ((0, 0), (0, CHUNK - A_GROUPS)))
    bias = _bias_table(rel_bias, bucket)

    r0, xb, wkv, wb, wob = _fwd_a(x2, p, ln_gb, ws, bsp_t, woa, (w_kv, w_in_b[0], w_out_b[0]))
    wkv = wkv.reshape(D_MODEL, 2 * KV_WIDTH)
    wob = wob.reshape(B_WIDTH, D_MODEL)
    q, z, kv, dr1, loss, dg1, db1 = _fwd_b(r0, tgt, wb, wkv, wob, post_ln_g, post_ln_b, attn_sinks, bias)
    dr0, dwb, dwkv, dwob, dsink, dlog, dpost0 = _bwd_b(r0, dr1, q, z, kv, wb, wkv, wob, post_ln_g, post_ln_b,
                                                       attn_sinks, bias)
    drel = _rel_bias_grad(dlog, bucket)

    place = jnp.stack([2 * lax.axis_index("x") + lax.axis_index("y"), lax.axis_index("c")]).astype(jnp.int32)

    def by_device(g):
        return g.reshape((4, 2, g.shape[0] // N_DEV) + g.shape[1:]) if g.ndim == 2 else g.reshape((4, 2) + g.shape[1:])

    dwb, dwkv, dwob = by_device(dwb), by_device(dwkv), by_device(dwob)
    dp, dwoa, dws, dbsp_t, dln, l_wb, l_wkv, l_wob = _bwd_a(dr0, p, ln_gb, ws, bsp_t, woa,
                                                            _Exchange(pair=(dwb, dwkv, dwob)))
    dwoa = by_device(dwoa)
    small_shapes = [(A_GROUPS * CHUNK, CHUNK), (A_GROUPS, CHUNK), (1, N_Q_HEADS), (REL_BUCKETS, N_Q_HEADS),
                    (16, CHUNK), (16, CHUNK), (16, CHUNK), (16, CHUNK), (1, 1)]
    small = by_device(_pack_small([
        dws.reshape(A_GROUPS * CHUNK, CHUNK), dbsp_t[:, :A_GROUPS].T, dsink[:, :N_Q_HEADS], drel[:, :N_Q_HEADS],
        jnp.concatenate([dpost0[0:1], dg1], axis=0).reshape(16, CHUNK),
        jnp.concatenate([dpost0[1:2], db1], axis=0).reshape(16, CHUNK),
        dln[0].reshape(16, CHUNK), dln[1].reshape(16, CHUNK), loss]))
    s_wb, s_wkv, s_wob = (_pair_sum(place, g, l, BF16) for g, l in ((dwb, l_wb), (dwkv, l_wkv), (dwob, l_wob)))
    dwa, l_woa, l_small, g_wb, g_wkv, g_wob = _in_proj_weight_grad(
        xb, dp, wa.shape[2], _Exchange(pair=(dwoa, small), chip=(s_wb, s_wkv, s_wob)))
    dwa = by_device(dwa)
    s_woa, s_small = _pair_sum(place, dwoa, l_woa, BF16), _pair_sum(place, small, l_small, F32)
    _, blocks = _input_grad_blocks(seq)
    cut0, cut1 = blocks // 4, blocks - max(1, blocks // 16)
    dx, l_wa, g_woa, g_small = _input_grad(dp, dr0, wa, 0, cut0, None,
                                           _Exchange(pair=(dwa,), chip=(s_woa, s_small)))
    s_wa = _pair_sum(place, dwa, l_wa, BF16)
    dx, g_wa = _input_grad(dp, dr0, wa, cut0, cut1 - cut0, dx, _Exchange(chip=(s_wa,)))
    dx, = _input_grad(dp, dr0, wa, cut1, blocks - cut1, dx, _Exchange())
    full = [dwa, dwoa, dwkv, dwb, dwob, small]
    landed = [l_wa, l_woa, l_wkv, l_wb, l_wob, l_small]
    got = [g_wa, g_woa, g_wkv, g_wb, g_wob, g_small]

    big = [(w_in_a, m_w_in_a, v_w_in_a), (w_out_a, m_w_out_a, v_w_out_a), (w_kv, m_w_kv, v_w_kv),
           (w_in_b, m_w_in_b, v_w_in_b), (w_out_b, m_w_out_b, v_w_out_b)]
    big_out = []
    for i, (w, m, v) in enumerate(big):
        shape2 = got[i].shape[1:]
        res = _adamw_shard(place, full[i], landed[i], got[i], w.reshape(shape2), m.reshape(shape2),
                           v.reshape(shape2))
        big_out.append([r.reshape(w.shape) for r in res])

    small_all = _gather_small(full[5], landed[5], got[5]).reshape(N_DEV * SMALL_ROWS, CHUNK)
    g_ws, g_bsp, g_sink, g_rel, g_pg, g_pb, g_lng, g_lnb, loss_all = _unpack_small(small_all, small_shapes)
    g_ws = g_ws.reshape(w_spatial.shape)
    g_bsp = g_bsp.reshape(b_spatial.shape)
    g_pg = g_pg.reshape(post_ln_g.shape)
    g_pb = g_pb.reshape(post_ln_b.shape)
    g_lng = lax.dynamic_slice(g_lng.reshape(1, A_WIDTH), (0, me * sgu_ln_g.shape[1]), sgu_ln_g.shape)
    g_lnb = lax.dynamic_slice(g_lnb.reshape(1, A_WIDTH), (0, me * sgu_ln_b.shape[1]), sgu_ln_b.shape)
    small_params = [(sgu_ln_g, g_lng, m_sgu_ln_g, v_sgu_ln_g), (sgu_ln_b, g_lnb, m_sgu_ln_b, v_sgu_ln_b),
                    (w_spatial, g_ws, m_w_spatial, v_w_spatial), (b_spatial, g_bsp, m_b_spatial, v_b_spatial),
                    (attn_sinks, g_sink, m_attn_sinks, v_attn_sinks), (rel_bias, g_rel, m_rel_bias, v_rel_bias),
                    (post_ln_g, g_pg, m_post_ln_g, v_post_ln_g), (post_ln_b, g_pb, m_post_ln_b, v_post_ln_b)]
    flat2 = [tuple(a.reshape(-1, a.shape[-1]) for a in p) for p in small_params]
    small_out = [tuple(r.reshape(p[0].shape) for r in res) for res, p in zip(_adamw_small(flat2), small_params)]

    order = [("big", 0), ("small", 0), ("small", 1), ("small", 2), ("small", 3), ("big", 1), ("big", 2), ("big", 3),
             ("small", 4), ("small", 5), ("big", 4), ("small", 6), ("small", 7)]
    grads, deltas, new_m, new_v = [], [], [], []
    for kind, i in order:
        if kind == "big":
            g, d, m, v = big_out[i]
        else:
            g = small_params[i][1]
            d, m, v = small_out[i]
        grads.append(g)
        deltas.append(d)
        new_m.append(m)
        new_v.append(v)
    return (loss_all[0, 0], dx.reshape(x.shape), *grads, *deltas, *new_m, *new_v)
```

```python
import functools

import jax
import jax.numpy as jnp
import numpy as np
from jax import lax
from jax.experimental import pallas as pl
from jax.experimental.pallas import tpu as pltpu

F32 = jnp.float32
BF16 = jnp.bfloat16

D_MODEL = 1024
A_WIDTH = 2048
A_GROUPS = 8
A_GROUP_DIM = A_WIDTH // A_GROUPS
CHUNK = 128
HEAD_DIM = 64
N_Q_HEADS = 16
N_KV_HEADS = 2
PAIRS_PER_KV = N_Q_HEADS // N_KV_HEADS // 2
N_PAIRS = N_Q_HEADS // 2
B_WIDTH = N_Q_HEADS * HEAD_DIM
KV_WIDTH = N_KV_HEADS * HEAD_DIM
REL_BUCKETS = 32
REL_MAX_DIST = 128
DEPTH = 2
ALPHA = (2.0 * DEPTH) ** 0.25
LN_EPS = 1e-5
NEG_INF = -1e30
SCALE = HEAD_DIM ** -0.5

ADAM_LR = 0.001
ADAM_B1 = 0.9
ADAM_B2 = 0.999
ADAM_EPS = 1e-08
ADAM_WD = 0.01
ADAM_STEP = 10

N_DEV = 8
MESH = pl.DeviceIdType.MESH
TOKENS_IN_PROJ = 1024
TOKENS_WGRAD = 4096
TOKENS_A = 256
TOKENS_A_FWD = 512
TOKENS_B = 512
PART_ROWS_B = 256
TOKENS_B_BWD = 256
TOKENS_DX = 512
VMEM_LIMIT_BYTES = 60 * 1024 * 1024
SMALL_ROWS = 144

ANY = pl.BlockSpec(memory_space=pl.ANY)
VMEM = pl.BlockSpec(memory_space=pltpu.VMEM)
SMEM = pl.BlockSpec(memory_space=pltpu.SMEM)


def _nn(a, b):
    return jnp.dot(a, b, preferred_element_type=F32)


def _nt(a, b):
    return lax.dot_general(a, b, (((1,), (1,)), ((), ())), preferred_element_type=F32)


def _tn(a, b):
    return lax.dot_general(a, b, (((0,), (0,)), ((), ())), preferred_element_type=F32)


def _resident(shape):
    nd = len(shape)
    return pl.BlockSpec(shape, lambda i: (0,) * nd, pipeline_mode=pl.Buffered(1))


def _rows(tm, width, reverse_of=None):
    if reverse_of is None:
        return pl.BlockSpec((tm, width), lambda i: (i, 0))
    return pl.BlockSpec((tm, width), lambda i: (reverse_of - 1 - i, 0))


def _seq_params():
    return pltpu.CompilerParams(dimension_semantics=("arbitrary",), vmem_limit_bytes=VMEM_LIMIT_BYTES)


def _in_turns(first, second):
    merged = [(i / max(len(first), 1), 0, f) for i, f in enumerate(first)]
    merged += [((i + 0.5) / max(len(second), 1), 1, f) for i, f in enumerate(second)]
    return [f for _, _, f in sorted(merged, key=lambda t: t[:2])]


def _ln_normalize(r):
    mu = jnp.mean(r, axis=-1, keepdims=True)
    xc = r - mu
    var = jnp.mean(xc * xc, axis=-1, keepdims=True)
    rstd = lax.rsqrt(var + LN_EPS)
    return xc * rstd, rstd


def _ln_backward(dxhat, xhat, rstd):
    m1 = jnp.mean(dxhat, axis=-1, keepdims=True)
    m2 = jnp.mean(dxhat * xhat, axis=-1, keepdims=True)
    return rstd * (dxhat - m1 - xhat * m2)


def _silu_and_grad(z):
    sig = jax.nn.sigmoid(z)
    silu = z * sig
    return silu, sig + silu * (1.0 - sig)


def _band_tables():
    t = np.arange(CHUNK, dtype=np.int32)[:, None]
    j = np.arange(2 * CHUNK, dtype=np.int32)[None, :]
    d = t + CHUNK - j
    in_window = (d >= 0) & (d < CHUNK)
    dc = np.clip(d, 0, REL_MAX_DIST - 1)
    max_exact = REL_BUCKETS // 2
    df = np.maximum(dc, 1).astype(np.float32)
    large = max_exact + (np.log(df / np.float32(max_exact)) / np.float32(np.log(REL_MAX_DIST / max_exact))
                         * np.float32(REL_BUCKETS - max_exact)).astype(np.int32)
    large = np.minimum(large, REL_BUCKETS - 1)
    bucket = np.where(dc < max_exact, dc, large)
    return np.where(in_window, bucket, -1).astype(np.int32)


def _use_order():
    x, y, c = lax.axis_index("x"), lax.axis_index("y"), lax.axis_index("c")
    blocks = [(x, y, c), (x, y, 1 - c)]
    for px, py in [(1 - x, y), (x, 1 - y), (1 - x, 1 - y)]:
        blocks += [(px, py, c), (px, py, 1 - c)]
    return blocks


def _gather_in_proj(order, x, w_in_a, small, ln_gb):
    seq = x.shape[0]
    tm = min(TOKENS_IN_PROJ, seq)
    n_blocks = seq // tm
    shard_cols = w_in_a.shape[1]
    n = len(small) + 2

    def body(order_ref, x_ref, *refs):
        ins, refs = refs[:n], refs[n:]
        p_ref, outs, refs = refs[0], refs[1:n + 1], refs[n + 1:]
        wa_vmem, xb_vmem, stage = refs[0], refs[1], refs[2:n]
        send_sems, recv_sems, local_sems, out_sems = refs[n:]
        phase, blk = pl.program_id(0), pl.program_id(1)
        x_, y_, c_ = lax.axis_index("x"), lax.axis_index("y"), lax.axis_index("c")
        me, sibling = (x_, y_, c_), (x_, y_, 1 - c_)
        chips = [(1 - x_, y_), (x_, 1 - y_), (1 - x_, 1 - y_)]
        use = _use_order()

        def index(px, py, pc):
            return 4 * px + 2 * py + pc

        def slot(a, block):
            return (wa_vmem if a == 0 else outs[a]).at[index(*block)]

        def own(a):
            return slot(0, me) if a == 0 else (ins[a] if a == n - 1 else stage[a - 1])

        def copy(a, k, block, to, src=None):
            return pltpu.make_async_remote_copy(
                src_ref=slot(a, block) if src is None else src, dst_ref=slot(a, block),
                send_sem=send_sems.at[a, k], recv_sem=recv_sems.at[a, k], device_id=to, device_id_type=MESH)

        def keep(a):
            return pltpu.make_async_copy(own(a), slot(a, me), local_sems.at[a])

        def write_out(k):
            return pltpu.make_async_copy(slot(0, use[k]), outs[0].at[index(*use[k])], out_sems.at[k])

        def pass_on(a, j):
            copy(a, 1 + j, (*chips[j], c_), me).wait_recv()
            copy(a, 4 + j, (*chips[j], c_), sibling).start()

        def send(a, j):
            copy(a, 1 + j, me, (*chips[j], c_), own(a)).start()

        @pl.when((phase == 0) & (blk == 0))
        def _():
            wa_vmem[index(*me)] = ins[0][...].astype(BF16)
            for a in range(1, n - 1):
                stage[a - 1][...] = ins[a][...].astype(BF16)
            send(0, 0)
            send(0, 1)
            for a in range(n):
                copy(a, 0, me, sibling, own(a)).start()
                if a:
                    keep(a).start()
            write_out(0).start()
            copy(0, 0, sibling, me).wait_recv()
            write_out(1).start()

        for j in range(3):
            @pl.when((phase == 1 + j) & (blk == 0))
            def _(j=j):
                pass_on(0, j)
                write_out(2 + 2 * j).start()
                if j == 0:
                    send(0, 2)
                    for a in range(1, n):
                        for jj in range(3):
                            send(a, jj)
                copy(0, 4 + j, (*chips[j], 1 - c_), me).wait_recv()
                write_out(3 + 2 * j).start()
                if j == 2:
                    for a in range(1, n):
                        pass_on(a, 0)
                        pass_on(a, 1)

        rows = pl.ds(pl.multiple_of(blk * tm, tm), tm)

        @pl.when(phase == 0)
        def _():
            xb_vmem[rows, :] = x_ref[...].astype(BF16)

        chip = order_ref[phase]
        xb = xb_vmem[rows, :]
        p_ref[:, :shard_cols] = _nn(xb, wa_vmem[2 * chip])
        p_ref[:, shard_cols:] = _nn(xb, wa_vmem[2 * chip + 1])

        @pl.when((phase == N_DEV // 2 - 1) & (blk == n_blocks - 1))
        def _():
            for a in range(1, n):
                pass_on(a, 2)
            for a in range(1, n):
                copy(a, 0, sibling, me).wait_recv()
                for j, chip in enumerate(chips):
                    copy(a, 4 + j, (*chip, 1 - c_), me).wait_recv()
            for a in range(n):
                for k in range(7):
                    copy(a, k, me, me).wait_send()
                if a:
                    keep(a).wait()
            for k in range(N_DEV):
                write_out(k).wait()

    gathered = [jax.ShapeDtypeStruct((N_DEV,) + w_in_a.shape, BF16)]
    gathered += [jax.ShapeDtypeStruct((N_DEV,) + s.shape, BF16) for s in small]
    gathered.append(jax.ShapeDtypeStruct((N_DEV,) + ln_gb.shape, F32))
    return pl.pallas_call(
        body, name="gather_in_proj",
        out_shape=[jax.ShapeDtypeStruct((seq, N_DEV * shard_cols), F32)] + gathered,
        grid_spec=pltpu.PrefetchScalarGridSpec(
            num_scalar_prefetch=1, grid=(N_DEV // 2, n_blocks),
            in_specs=[pl.BlockSpec((tm, D_MODEL), lambda k, i, order: (jnp.where(k == 0, i, n_blocks - 1), 0))]
            + [VMEM] * n,
            out_specs=[pl.BlockSpec((tm, 2 * shard_cols), lambda k, i, order: (i, order[k]))] + [ANY] * n,
            scratch_shapes=[pltpu.VMEM((N_DEV,) + w_in_a.shape, BF16), pltpu.VMEM((seq, D_MODEL), BF16)]
            + [pltpu.VMEM(s.shape, BF16) for s in small] + [
                pltpu.SemaphoreType.DMA((n, 7)), pltpu.SemaphoreType.DMA((n, 7)), pltpu.SemaphoreType.DMA((n,)),
                pltpu.SemaphoreType.DMA((N_DEV,))]),
        compiler_params=pltpu.CompilerParams(dimension_semantics=("arbitrary", "arbitrary"),
                                             vmem_limit_bytes=VMEM_LIMIT_BYTES),
    )(order, x, w_in_a, *small, ln_gb)


def _bias_table(rel_bias, bucket):
    def body(rel_ref, bucket_ref, out_ref):
        first = jnp.broadcast_to(bucket_ref[0:1, :], (8, 2 * CHUNK))
        col = lax.broadcasted_iota(jnp.int32, (CHUNK, 2 * CHUNK), 1)
        for h in range(N_Q_HEADS):
            row = jnp.full(first.shape, NEG_INF, F32)
            for b in range(REL_BUCKETS):
                row = jnp.where(first == b, rel_ref[b, h], row)
            tab = pltpu.roll(jnp.broadcast_to(row[0:1, :], (CHUNK, 2 * CHUNK)), 0, 1, stride=1, stride_axis=0)
            pair = h // 2
            rows = slice((pair % PAIRS_PER_KV) * CHUNK, (pair % PAIRS_PER_KV + 1) * CHUNK)
            half = slice((h % 2) * 2 * CHUNK, (h % 2 + 1) * 2 * CHUNK)
            out_ref[1, pair // PAIRS_PER_KV, rows, half] = tab
            out_ref[0, pair // PAIRS_PER_KV, rows, half] = jnp.where(col < CHUNK, NEG_INF, tab)

    return pl.pallas_call(
        body, name="bias_table",
        out_shape=jax.ShapeDtypeStruct((2, N_KV_HEADS, PAIRS_PER_KV * CHUNK, 4 * CHUNK), F32),
        in_specs=[SMEM, VMEM], out_specs=VMEM,
    )(rel_bias, bucket)


def _rel_bias_grad(dlogits, bucket):
    def body(dl_ref, bucket_ref, out_ref):
        i = lax.broadcasted_iota(jnp.int32, (CHUNK, CHUNK), 0)
        k = lax.broadcasted_iota(jnp.int32, (CHUNK, CHUNK), 1)
        reverse = jnp.where(i + k == CHUNK - 1, 1.0, 0.0)
        sums = []
        for h in range(N_Q_HEADS):
            pair = h // 2
            dl = dl_ref[pair // PAIRS_PER_KV, (pair % PAIRS_PER_KV) * CHUNK:(pair % PAIRS_PER_KV + 1) * CHUNK,
                        (h % 2) * 2 * CHUNK:(h % 2 + 1) * 2 * CHUNK]
            backwards = jnp.dot(reverse, dl, precision=lax.Precision.HIGHEST, preferred_element_type=F32)
            turned = pltpu.roll(backwards, CHUNK + 1, 1, stride=1, stride_axis=0)
            sums.append(jnp.sum(turned, axis=0, keepdims=True))
        by_offset = jnp.concatenate(sums, axis=0)
        buckets = lax.broadcasted_iota(jnp.int32, (REL_BUCKETS, 2 * CHUNK), 0)
        one_hot = jnp.where(jnp.broadcast_to(bucket_ref[0:1, :], buckets.shape) == buckets, 1.0, 0.0)
        out_ref[...] = lax.dot_general(one_hot, by_offset, (((1,), (1,)), ((), ())),
                                       precision=lax.Precision.HIGHEST, preferred_element_type=F32)

    return pl.pallas_call(
        body, name="rel_bias_grad", out_shape=jax.ShapeDtypeStruct((REL_BUCKETS, N_Q_HEADS), F32),
        in_specs=[VMEM, VMEM], out_specs=VMEM,
    )(dlogits, bucket)


def _masked_spatial(ws_ref, g):
    t = lax.broadcasted_iota(jnp.int32, (CHUNK, CHUNK), 0)
    s = lax.broadcasted_iota(jnp.int32, (CHUNK, CHUNK), 1)
    return jnp.where(s <= t, ws_ref[g], 0.0).astype(BF16)


def _spatial_gate(ws_ref, bsp_t_ref, vn, s_ref, n_chunks):
    for g in range(A_GROUPS):
        w = _masked_spatial(ws_ref, g)
        cols = slice(g * A_GROUP_DIM, (g + 1) * A_GROUP_DIM)
        for ci in range(n_chunks):
            rows = slice(ci * CHUNK, (ci + 1) * CHUNK)
            s_ref[rows, cols] = _nn(w, vn[rows, cols]) + bsp_t_ref[:, g:g + 1]


def _fwd_a(x, p, ln_gb, w_spatial, bsp_t, woa, later):
    seq = x.shape[0]
    tm = min(TOKENS_A_FWD, seq)
    n_chunks = tm // CHUNK
    n_blocks = seq // tm
    n = len(later)

    def body(x_ref, p_ref, ln_ref, ws_ref, bsp_ref, woa_ref, *refs):
        shards, refs = refs[:n], refs[n:]
        r0_ref, xb_ref, gathered, refs = refs[0], refs[1], refs[2:2 + n], refs[2 + n:]
        s_ref, stage, (send_sems, recv_sems, local_sems) = refs[0], refs[1:1 + n], refs[1 + n:]
        step = pl.program_id(0)

        def copies(a):
            x_, y_, c_ = lax.axis_index("x"), lax.axis_index("y"), lax.axis_index("c")
            mine = 4 * x_ + 2 * y_ + c_
            out = [pltpu.make_async_copy(stage[a], gathered[a].at[mine], local_sems.at[a])]
            for r in range(1, N_DEV):
                to = (1 - x_ if r & 4 else x_, 1 - y_ if r & 2 else y_, 1 - c_ if r & 1 else c_)
                out.append(pltpu.make_async_remote_copy(
                    src_ref=stage[a], dst_ref=gathered[a].at[mine], send_sem=send_sems.at[a, r - 1],
                    recv_sem=recv_sems.at[a, r - 1], device_id=to, device_id_type=MESH))
            return out

        @pl.when(step == 0)
        def _():
            for a in range(n):
                stage[a][...] = shards[a][...].astype(BF16)
                for cp in copies(a):
                    cp.start()

        xv = x_ref[...]
        xb_ref[...] = xv.astype(BF16)
        vhat, _ = _ln_normalize(p_ref[:, A_WIDTH:2 * A_WIDTH])
        vn = (vhat * ln_ref[0:1, :] + ln_ref[1:2, :]).astype(BF16)
        _spatial_gate(ws_ref, bsp_ref, vn, s_ref, n_chunks)
        z = p_ref[:, 2 * A_WIDTH:]
        y = p_ref[:, :A_WIDTH] * s_ref[...] * (z * jax.nn.sigmoid(z))
        r0_ref[...] = ALPHA * xv + _nn(y.astype(BF16), woa_ref[...])

        @pl.when(step == n_blocks - 1)
        def _():
            for a in range(n):
                for cp in copies(a):
                    cp.wait()

    return pl.pallas_call(
        body, name="fwd_a", grid=(n_blocks,),
        out_shape=[jax.ShapeDtypeStruct((seq, D_MODEL), F32), jax.ShapeDtypeStruct((seq, D_MODEL), BF16)]
        + [jax.ShapeDtypeStruct((N_DEV,) + s.shape, BF16) for s in later],
        in_specs=[_rows(tm, D_MODEL), _rows(tm, 3 * A_WIDTH), _resident(ln_gb.shape), _resident(w_spatial.shape),
                  _resident(bsp_t.shape), _resident(woa.shape)] + [VMEM] * n,
        out_specs=[_rows(tm, D_MODEL), _rows(tm, D_MODEL)] + [ANY] * n,
        scratch_shapes=[pltpu.VMEM((tm, A_WIDTH), F32)] + [pltpu.VMEM(s.shape, BF16) for s in later] + [
            pltpu.SemaphoreType.DMA((n, N_DEV - 1)), pltpu.SemaphoreType.DMA((n, N_DEV - 1)),
            pltpu.SemaphoreType.DMA((n,))],
        compiler_params=_seq_params(),
    )(x, p, ln_gb, w_spatial, bsp_t, woa, *later)


def _expand_heads(kv):
    lo = lax.broadcasted_iota(jnp.int32, kv.shape, 1) < HEAD_DIM
    swapped = pltpu.roll(kv, HEAD_DIM, 1)
    zero = jnp.zeros_like(kv)
    return (jnp.where(lo, kv, zero).astype(BF16), jnp.where(lo, zero, swapped).astype(BF16),
            jnp.where(lo, swapped, zero).astype(BF16), jnp.where(lo, zero, kv).astype(BF16))


def _fold_heads(acc, kvh):
    top, bot = acc[:2 * CHUNK], acc[2 * CHUNK:]
    lo = lax.broadcasted_iota(jnp.int32, top.shape, 1) < HEAD_DIM
    t = jnp.where(lo, top, 0.0)
    b = jnp.where(lo, 0.0, bot)
    if kvh == 0:
        return t + pltpu.roll(b, HEAD_DIM, 1)
    return pltpu.roll(t, HEAD_DIM, 1) + b


STACK = PAIRS_PER_KV * CHUNK


def _pair_cols(kvh, pp):
    pair = kvh * PAIRS_PER_KV + pp
    return slice(pair * CHUNK, (pair + 1) * CHUNK)


def _stack_pairs(ref_or_val, rows, kvh):
    return jnp.concatenate([ref_or_val[rows, _pair_cols(kvh, pp)] for pp in range(PAIRS_PER_KV)], axis=0)


def _sink_columns(sink_ref):
    blk = lax.broadcasted_iota(jnp.int32, (STACK, 1), 0) // CHUNK
    out = []
    for kvh in range(N_KV_HEADS):
        halves = []
        for half in range(2):
            col = jnp.zeros((STACK, 1), F32)
            for pp in range(PAIRS_PER_KV):
                col = jnp.where(blk == pp, sink_ref[0, 2 * (kvh * PAIRS_PER_KV + pp) + half], col)
            halves.append(col)
        out.append(halves)
    return out


def _softmax_stack(logits, sink_cols):
    ps, sinks = [], []
    for half in range(2):
        lg = logits[:, half * 2 * CHUNK:(half + 1) * 2 * CHUNK]
        sk = sink_cols[half]
        m = jnp.maximum(jnp.max(lg, axis=-1, keepdims=True), sk)
        e = jnp.exp(lg - m)
        es = jnp.exp(sk - m)
        inv = 1.0 / (jnp.sum(e, axis=-1, keepdims=True) + es)
        ps.append(e * inv)
        sinks.append(es * inv)
    return jnp.concatenate(ps, axis=1), sinks


def _band_operands(before_k, this_k, before_v, this_v, kvh):
    kblk = jnp.concatenate([before_k[2 * kvh], this_k[2 * kvh], before_k[2 * kvh + 1], this_k[2 * kvh + 1]], axis=0)
    vblk = jnp.concatenate([before_v[2 * kvh], this_v[2 * kvh], before_v[2 * kvh + 1], this_v[2 * kvh + 1]], axis=0)
    return kblk, vblk


def _fwd_b(r0, target, wb, wkv, wob, post_g, post_b, sinks, bias):
    seq = r0.shape[0]
    tm = min(TOKENS_B, seq)
    n_chunks = tm // CHUNK
    part_rows = min(PART_ROWS_B, tm)
    n_parts = tm // part_rows
    part_chunks = part_rows // CHUNK
    shard_cols = wb.shape[2]

    def body(r0_ref, tgt_ref, wb_ref, wkv_ref, wob_ref, pg_ref, pb_ref, sink_ref, bias_ref,
             q_ref, z_ref, kv_ref, dr1_ref, loss_ref, dg_ref, db_ref, ek_last, ev_last, *scratch):
        q_s, z_s, h1_s, o_s, ek_s, ev_s = (scratch[i * n_parts:(i + 1) * n_parts] for i in range(6))
        step = pl.program_id(0)

        @pl.when(step == 0)
        def _():
            ek_last[...] = jnp.zeros_like(ek_last)
            ev_last[...] = jnp.zeros_like(ev_last)
            loss_ref[...] = jnp.zeros_like(loss_ref)
            dg_ref[...] = jnp.zeros_like(dg_ref)
            db_ref[...] = jnp.zeros_like(db_ref)

        sink_cols = _sink_columns(sink_ref)
        g1 = pg_ref[1:2, :]

        def project_steps(part):
            rows = slice(part * part_rows, (part + 1) * part_rows)
            live = {}

            def normalize():
                xhat0, _ = _ln_normalize(r0_ref[rows, :])
                h1 = xhat0 * pg_ref[0:1, :] + pb_ref[0:1, :]
                h1_s[part][...] = h1
                live["h1b"] = h1.astype(BF16)

            def shard(k):
                def run():
                    qz = _nn(live["h1b"], wb_ref[k])
                    lo, hi = k * shard_cols, (k + 1) * shard_cols
                    if hi <= B_WIDTH:
                        qs = (qz * SCALE).astype(BF16)
                        q_ref[rows, lo:hi] = qs
                        q_s[part][:, lo:hi] = qs
                    else:
                        z_ref[rows, lo - B_WIDTH:hi - B_WIDTH] = qz
                        z_s[part][:, lo - B_WIDTH:hi - B_WIDTH] = qz
                return run

            def keys():
                kv = _nn(live["h1b"], wkv_ref[...])
                kv_ref[rows, :] = kv.astype(BF16)
                for cj in range(part_chunks):
                    sub = slice(cj * CHUNK, (cj + 1) * CHUNK)
                    for i, e in enumerate(_expand_heads(kv[sub, :KV_WIDTH])):
                        ek_s[part][cj, i] = e
                    for i, e in enumerate(_expand_heads(kv[sub, KV_WIDTH:])):
                        ev_s[part][cj, i] = e

            return [normalize, keys] + [shard(k) for k in range(N_DEV)]

        def attend_steps(part):
            def stack(cj, kvh):
                def run():
                    sub = slice(cj * CHUNK, (cj + 1) * CHUNK)
                    sel = jnp.where(step * n_chunks + part * part_chunks + cj > 0, 1, 0)
                    if cj:
                        before_k, before_v = ek_s[part].at[cj - 1], ev_s[part].at[cj - 1]
                    elif part:
                        before_k, before_v = ek_s[part - 1].at[part_chunks - 1], ev_s[part - 1].at[part_chunks - 1]
                    else:
                        before_k, before_v = ek_last, ev_last
                    kblk, vblk = _band_operands(before_k, ek_s[part].at[cj], before_v, ev_s[part].at[cj], kvh)
                    logits = _nt(_stack_pairs(q_s[part], sub, kvh), kblk) + bias_ref[sel, kvh]
                    p, _ = _softmax_stack(logits, sink_cols[kvh])
                    o = _nn(p.astype(BF16), vblk)
                    for pp in range(PAIRS_PER_KV):
                        o_s[part][sub, _pair_cols(kvh, pp)] = o[pp * CHUNK:(pp + 1) * CHUNK]
                return run

            def finish():
                rows = slice(part * part_rows, (part + 1) * part_rows)
                z = z_s[part][...]
                y = o_s[part][...] * (z * jax.nn.sigmoid(z))
                r1 = ALPHA * h1_s[part][...] + _nn(y.astype(BF16), wob_ref[...])
                xhat1, rstd1 = _ln_normalize(r1)
                diff = xhat1 * g1 + pb_ref[1:2, :] - tgt_ref[rows, :]
                loss_ref[...] += 0.5 * jnp.sum(jnp.mean(diff * diff, axis=-1, keepdims=True), axis=0, keepdims=True)
                dh2 = diff * (1.0 / D_MODEL)
                dg_ref[...] += jnp.sum(dh2 * xhat1, axis=0, keepdims=True)
                db_ref[...] += jnp.sum(dh2, axis=0, keepdims=True)
                dr1_ref[rows, :] = _ln_backward(dh2 * g1, xhat1, rstd1)

            return [stack(cj, kvh) for cj in range(part_chunks) for kvh in range(N_KV_HEADS)] + [finish]

        for run in project_steps(0):
            run()
        for part in range(n_parts):
            for run in _in_turns(attend_steps(part), project_steps(part + 1) if part + 1 < n_parts else []):
                run()
        ek_last[...] = ek_s[n_parts - 1][part_chunks - 1]
        ev_last[...] = ev_s[n_parts - 1][part_chunks - 1]

    vec = pl.BlockSpec((1, D_MODEL), lambda i: (0, 0))
    return pl.pallas_call(
        body, name="fwd_b", grid=(seq // tm,),
        out_shape=[jax.ShapeDtypeStruct((seq, B_WIDTH), BF16), jax.ShapeDtypeStruct((seq, B_WIDTH), F32),
                   jax.ShapeDtypeStruct((seq, 2 * KV_WIDTH), BF16), jax.ShapeDtypeStruct((seq, D_MODEL), F32),
                   jax.ShapeDtypeStruct((1, 1), F32), jax.ShapeDtypeStruct((1, D_MODEL), F32),
                   jax.ShapeDtypeStruct((1, D_MODEL), F32)],
        in_specs=[_rows(tm, D_MODEL), _rows(tm, D_MODEL), _resident(wb.shape), _resident(wkv.shape),
                  _resident(wob.shape), _resident(post_g.shape), _resident(post_b.shape), SMEM,
                  _resident(bias.shape)],
        out_specs=[_rows(tm, B_WIDTH), _rows(tm, B_WIDTH), _rows(tm, 2 * KV_WIDTH), _rows(tm, D_MODEL),
                   pl.BlockSpec((1, 1), lambda i: (0, 0)), vec, vec],
        scratch_shapes=[pltpu.VMEM((4, CHUNK, CHUNK), BF16), pltpu.VMEM((4, CHUNK, CHUNK), BF16)]
        + [pltpu.VMEM((part_rows, B_WIDTH), BF16)] * n_parts + [pltpu.VMEM((part_rows, B_WIDTH), F32)] * n_parts
        + [pltpu.VMEM((part_rows, D_MODEL), F32)] * n_parts + [pltpu.VMEM((part_rows, B_WIDTH), F32)] * n_parts
        + [pltpu.VMEM((part_chunks, 4, CHUNK, CHUNK), BF16)] * (2 * n_parts),
        compiler_params=_seq_params(),
    )(r0, target, wb, wkv, wob, post_g, post_b, sinks, bias)


def _bwd_b(r0, dr1, q, z, kv, wb, wkv, wob, post_g, post_b, sinks, bias):
    seq = r0.shape[0]
    tm = min(TOKENS_B_BWD, seq)
    n_chunks = tm // CHUNK
    n_blocks = seq // tm
    shard_cols = wb.shape[2]

    def body(r0_ref, dr1_ref, q_ref, z_ref, kv_ref, kvp_ref, wb_ref, wkv_ref, wob_ref, pg_ref, pb_ref, sink_ref,
             bias_ref, dr0_ref, dwb_out, dwkv_out, dwob_out, dsink_ref, dl_out, dpost_ref,
             dwb_acc, dwkv_acc, dwob_acc, dl_acc, dsink_acc, carry_ref, ek_ref, ev_ref, y_ref, dqz_ref, dkv_ref):
        step = pl.program_id(0)
        block = n_blocks - 1 - step

        @pl.when(step == 0)
        def _():
            dpost_ref[...] = jnp.zeros_like(dpost_ref)
            dwb_acc[...] = jnp.zeros_like(dwb_acc)
            dwkv_acc[...] = jnp.zeros_like(dwkv_acc)
            dwob_acc[...] = jnp.zeros_like(dwob_acc)
            dl_acc[...] = jnp.zeros_like(dl_acc)
            carry_ref[...] = jnp.zeros_like(carry_ref)
            dsink_acc[...] = jnp.zeros_like(dsink_acc)

        xhat0, rstd0 = _ln_normalize(r0_ref[...])
        h1b = (xhat0 * pg_ref[0:1, :] + pb_ref[0:1, :]).astype(BF16)
        dr1 = dr1_ref[...]
        dr1b = dr1.astype(BF16)
        dy = _nt(dr1b, wob_ref[...])
        sg, dsg = _silu_and_grad(z_ref[...])
        do_all = dy * sg
        dyz = dy * dsg

        for ci in range(n_chunks + 1):
            src = kvp_ref[...] if ci == 0 else kv_ref[(ci - 1) * CHUNK:ci * CHUNK, :]
            src = src.astype(F32)
            for i, e in enumerate(_expand_heads(src[:, :KV_WIDTH])):
                ek_ref[ci, i] = e
            for i, e in enumerate(_expand_heads(src[:, KV_WIDTH:])):
                ev_ref[ci, i] = e

        sink_cols = _sink_columns(sink_ref)
        for ci in reversed(range(n_chunks)):
            rows = slice(ci * CHUNK, (ci + 1) * CHUNK)
            sel = jnp.where(block * n_chunks + ci > 0, 1, 0)
            dk_band = jnp.zeros((2 * CHUNK, KV_WIDTH), F32)
            dv_band = jnp.zeros((2 * CHUNK, KV_WIDTH), F32)
            for kvh in range(N_KV_HEADS):
                kblk, vblk = _band_operands(ek_ref.at[ci], ek_ref.at[ci + 1], ev_ref.at[ci], ev_ref.at[ci + 1], kvh)
                qs = _stack_pairs(q_ref, rows, kvh)
                p, psink = _softmax_stack(_nt(qs, kblk) + bias_ref[sel, kvh], sink_cols[kvh])
                pb16 = p.astype(BF16)
                o = _nn(pb16, vblk)
                y = (o * _stack_pairs(sg, rows, kvh)).astype(BF16)
                dz = (o * _stack_pairs(dyz, rows, kvh)).astype(BF16)
                dob = _stack_pairs(do_all, rows, kvh).astype(BF16)
                dp = _nt(dob, vblk)
                dlogits = []
                for half in range(2):
                    hs = slice(half * 2 * CHUNK, (half + 1) * 2 * CHUNK)
                    delta = jnp.sum(dp[:, hs] * p[:, hs], axis=-1, keepdims=True)
                    dlogits.append(p[:, hs] * (dp[:, hs] - delta))
                    dsink_acc[2 * kvh + half] += psink[half] * delta
                dlog = jnp.concatenate(dlogits, axis=1)
                dl_acc[kvh] += dlog
                dsc = dlog.astype(BF16)
                dq = (_nn(dsc, kblk) * SCALE).astype(BF16)
                for pp in range(PAIRS_PER_KV):
                    blk = slice(pp * CHUNK, (pp + 1) * CHUNK)
                    cols = _pair_cols(kvh, pp)
                    y_ref[rows, cols] = y[blk]
                    dqz_ref[rows, cols] = dq[blk]
                    dqz_ref[rows, B_WIDTH + cols.start:B_WIDTH + cols.stop] = dz[blk]
                dk_band = dk_band + _fold_heads(_tn(dsc, qs), kvh)
                dv_band = dv_band + _fold_heads(_tn(pb16, dob), kvh)
            band = jnp.concatenate([dk_band, dv_band], axis=1)
            dkv_ref[rows, :] = (band[CHUNK:] + carry_ref[...]).astype(BF16)
            carry_ref[...] = band[:CHUNK]

        dqz = dqz_ref[...]
        dkv = dkv_ref[...]
        dh1 = ALPHA * dr1 + _nt(dkv, wkv_ref[...])
        for k in range(N_DEV):
            part = dqz[:, k * shard_cols:(k + 1) * shard_cols]
            dh1 = dh1 + _nt(part, wb_ref[k])
            dwb_acc[k] += _tn(h1b, part)
        dpost_ref[0:1, :] += jnp.sum(dh1 * xhat0, axis=0, keepdims=True)
        dpost_ref[1:2, :] += jnp.sum(dh1, axis=0, keepdims=True)
        dr0_ref[...] = _ln_backward(dh1 * pg_ref[0:1, :], xhat0, rstd0)
        dwkv_acc[...] += _tn(h1b, dkv)
        dwob_acc[...] += _tn(y_ref[...], dr1b)

        @pl.when(step == n_blocks - 1)
        def _():
            pltpu.sync_copy(dwb_acc, dwb_out)
            pltpu.sync_copy(dwkv_acc, dwkv_out)
            pltpu.sync_copy(dwob_acc, dwob_out)
            pltpu.sync_copy(dl_acc, dl_out)
            lane = lax.broadcasted_iota(jnp.int32, (1, CHUNK), 1)
            dsink = jnp.zeros((1, CHUNK), F32)
            for kvh in range(N_KV_HEADS):
                for half in range(2):
                    for pp in range(PAIRS_PER_KV):
                        head = 2 * (kvh * PAIRS_PER_KV + pp) + half
                        tot = jnp.sum(dsink_acc[2 * kvh + half, pp * CHUNK:(pp + 1) * CHUNK, :], axis=0, keepdims=True)
                        dsink = dsink - jnp.where(lane == head, tot, 0.0)
            dsink_ref[...] = dsink

    rev = functools.partial(_rows, tm, reverse_of=n_blocks)
    prev_chunk = pl.BlockSpec(
        (CHUNK, 2 * KV_WIDTH), lambda i: (jnp.maximum((n_blocks - 1 - i) * n_chunks - 1, 0), 0))
    return pl.pallas_call(
        body, name="bwd_b", grid=(n_blocks,),
        out_shape=[jax.ShapeDtypeStruct((seq, D_MODEL), F32), jax.ShapeDtypeStruct(wb.shape, F32),
                   jax.ShapeDtypeStruct(wkv.shape, F32), jax.ShapeDtypeStruct(wob.shape, F32),
                   jax.ShapeDtypeStruct((1, CHUNK), F32),
                   jax.ShapeDtypeStruct((N_KV_HEADS, STACK, 4 * CHUNK), F32), jax.ShapeDtypeStruct((2, D_MODEL), F32)],
        in_specs=[rev(D_MODEL), rev(D_MODEL), rev(B_WIDTH), rev(B_WIDTH), rev(2 * KV_WIDTH), prev_chunk,
                  _resident(wb.shape), _resident(wkv.shape), _resident(wob.shape), _resident(post_g.shape),
                  _resident(post_b.shape), SMEM, _resident(bias.shape)],
        out_specs=[rev(D_MODEL), ANY, ANY, ANY, pl.BlockSpec((1, CHUNK), lambda i: (0, 0)), ANY,
                   pl.BlockSpec((2, D_MODEL), lambda i: (0, 0))],
        scratch_shapes=[pltpu.VMEM(wb.shape, F32), pltpu.VMEM(wkv.shape, F32), pltpu.VMEM(wob.shape, F32),
                        pltpu.VMEM((N_KV_HEADS, STACK, 4 * CHUNK), F32), pltpu.VMEM((2 * N_KV_HEADS, STACK, 1), F32),
                        pltpu.VMEM((CHUNK, 2 * KV_WIDTH), F32),
                        pltpu.VMEM((n_chunks + 1, 4, CHUNK, CHUNK), BF16),
                        pltpu.VMEM((n_chunks + 1, 4, CHUNK, CHUNK), BF16),
                        pltpu.VMEM((tm, B_WIDTH), BF16), pltpu.VMEM((tm, 2 * B_WIDTH), BF16),
                        pltpu.VMEM((tm, 2 * KV_WIDTH), BF16)],
        compiler_params=_seq_params(),
    )(r0, dr1, q, z, kv, kv, wb, wkv, wob, post_g, post_b, sinks, bias)


def _bwd_a(dr0, p, ln_gb, w_spatial, bsp_t, woa, exchange):
    seq = dr0.shape[0]
    tm = min(TOKENS_A, seq)
    n_chunks = tm // CHUNK
    n_blocks = seq // tm

    def body(dr0_ref, p_ref, ln_ref, ws_ref, bsp_ref, woa_ref, *refs):
        ex_in, refs = refs[:exchange.n], refs[exchange.n:]
        (dp_ref, dwoa_out, dws_ref, dbsp_ref, dln_ref), refs = refs[:5], refs[5:]
        ex_out, refs = refs[:exchange.n], refs[exchange.n:]
        (dwoa_acc, s_ref, ds_ref, dvn_ref), ex_sems = refs[:4], refs[4:]
        step = pl.program_id(0)

        @pl.when(step == 0)
        def _():
            exchange.start(ex_in, ex_out, ex_sems)
            dwoa_acc[...] = jnp.zeros_like(dwoa_acc)
            dws_ref[...] = jnp.zeros_like(dws_ref)
            dbsp_ref[...] = jnp.zeros_like(dbsp_ref)
            dln_ref[...] = jnp.zeros_like(dln_ref)

        dr0b = dr0_ref[...].astype(BF16)

        vhat, rstd_v = _ln_normalize(p_ref[:, A_WIDTH:2 * A_WIDTH])
        ln_g = ln_ref[0:1, :]
        vn = (vhat * ln_g + ln_ref[1:2, :]).astype(BF16)
        _spatial_gate(ws_ref, bsp_ref, vn, s_ref, n_chunks)
        u = p_ref[:, :A_WIDTH]
        s = s_ref[...]
        sg, dsg = _silu_and_grad(p_ref[:, 2 * A_WIDTH:])
        gate = s * sg
        dwoa_acc[...] += _tn((u * gate).astype(BF16), dr0b)
        dy = _nt(dr0b, woa_ref[...])
        dyu = dy * u
        dp_ref[:, :A_WIDTH] = (dy * gate).astype(BF16)
        dp_ref[:, 2 * A_WIDTH:] = (dyu * (s * dsg)).astype(BF16)
        ds_ref[...] = dyu * sg

        lane = lax.broadcasted_iota(jnp.int32, (CHUNK, CHUNK), 1)
        dbsp = jnp.zeros((CHUNK, CHUNK), F32)
        for g in range(A_GROUPS):
            w = _masked_spatial(ws_ref, g)
            cols = slice(g * A_GROUP_DIM, (g + 1) * A_GROUP_DIM)
            dws = jnp.zeros((CHUNK, CHUNK), F32)
            for ci in range(n_chunks):
                rows = slice(ci * CHUNK, (ci + 1) * CHUNK)
                ds = ds_ref[rows, cols]
                dsb = ds.astype(BF16)
                dvn_ref[rows, cols] = _tn(w, dsb)
                dws = dws + _nt(dsb, vn[rows, cols])
                dbsp = dbsp + jnp.where(lane == g, jnp.sum(ds, axis=-1, keepdims=True), 0.0)
            dws_ref[g] += dws
        dbsp_ref[...] += dbsp

        dvn = dvn_ref[...]
        dln_ref[0:1, :] += jnp.sum(dvn * vhat, axis=0, keepdims=True)
        dln_ref[1:2, :] += jnp.sum(dvn, axis=0, keepdims=True)
        dp_ref[:, A_WIDTH:2 * A_WIDTH] = _ln_backward(dvn * ln_g, vhat, rstd_v).astype(BF16)

        @pl.when(step == n_blocks - 1)
        def _():
            t = lax.broadcasted_iota(jnp.int32, (CHUNK, CHUNK), 0)
            for g in range(A_GROUPS):
                dws_ref[g] = jnp.where(lane <= t, dws_ref[g], 0.0)
            pltpu.sync_copy(dwoa_acc, dwoa_out)
            exchange.wait(ex_in, ex_out, ex_sems)

    def fixed(shape):
        nd = len(shape)
        return pl.BlockSpec(shape, lambda i: (0,) * nd)

    return pl.pallas_call(
        body, name="bwd_a", grid=(n_blocks,),
        out_shape=[jax.ShapeDtypeStruct((seq, 3 * A_WIDTH), BF16),
                   jax.ShapeDtypeStruct(woa.shape, F32), jax.ShapeDtypeStruct(w_spatial.shape, F32),
                   jax.ShapeDtypeStruct((CHUNK, CHUNK), F32), jax.ShapeDtypeStruct((2, A_WIDTH), F32)]
        + exchange.out_shape,
        in_specs=[_rows(tm, D_MODEL), _rows(tm, 3 * A_WIDTH), _resident(ln_gb.shape),
                  _resident(w_spatial.shape), _resident(bsp_t.shape), _resident(woa.shape)] + [ANY] * exchange.n,
        out_specs=[_rows(tm, 3 * A_WIDTH), ANY, fixed(w_spatial.shape),
                   fixed((CHUNK, CHUNK)), fixed((2, A_WIDTH))] + [ANY] * exchange.n,
        scratch_shapes=[pltpu.VMEM(woa.shape, F32), pltpu.VMEM((tm, A_WIDTH), F32),
                        pltpu.VMEM((tm, A_WIDTH), F32), pltpu.VMEM((tm, A_WIDTH), F32)] + exchange.scratch_shapes,
        compiler_params=_seq_params(),
    )(dr0, p, ln_gb, w_spatial, bsp_t, woa, *exchange.inputs)


def _input_grad_blocks(seq):
    tm = min(TOKENS_DX, seq // 8)
    return tm, seq // tm


def _input_grad(dp, dr0, wa, first, n_blocks, earlier, exchange):
    seq = dr0.shape[0]
    tm, _ = _input_grad_blocks(seq)
    shard_cols = wa.shape[2]
    n_own = 3 if earlier is None else 4

    def body(dp_ref, dr0_ref, wa_ref, *refs):
        refs = refs[n_own - 3:]
        ex_in, dx_ref, ex_out, ex_sems = (refs[:exchange.n], refs[exchange.n], refs[exchange.n + 1:2 * exchange.n + 1],
                                          refs[2 * exchange.n + 1:])
        step = pl.program_id(0)

        @pl.when(step == 0)
        def _():
            exchange.start(ex_in, ex_out, ex_sems)

        dx = ALPHA * dr0_ref[...]
        for k in range(N_DEV):
            dx = dx + _nt(dp_ref[:, k * shard_cols:(k + 1) * shard_cols], wa_ref[k])
        dx_ref[...] = dx

        @pl.when(step == n_blocks - 1)
        def _():
            exchange.wait(ex_in, ex_out, ex_sems)

    def rows(width):
        return pl.BlockSpec((tm, width), lambda i: (first + i, 0))

    return pl.pallas_call(
        body, name=f"input_grad_{first}", grid=(n_blocks,),
        out_shape=[jax.ShapeDtypeStruct((seq, D_MODEL), F32)] + exchange.out_shape,
        in_specs=[rows(3 * A_WIDTH), rows(D_MODEL), _resident(wa.shape)] + [ANY] * (n_own - 3 + exchange.n),
        out_specs=[rows(D_MODEL)] + [ANY] * exchange.n,
        scratch_shapes=exchange.scratch_shapes,
        input_output_aliases={} if earlier is None else {3: 0},
        compiler_params=_seq_params(),
    )(dp, dr0, wa, *(() if earlier is None else (earlier,)), *exchange.inputs)


def _in_proj_weight_grad(xb, dp, shard_cols, exchange):
    seq = xb.shape[0]
    tk = min(TOKENS_WGRAD, seq)
    n_k = seq // tk

    def body(x_ref, dp_ref, *refs):
        ex_in, out_ref, ex_out, ex_sems = (refs[:exchange.n], refs[exchange.n], refs[exchange.n + 1:2 * exchange.n + 1],
                                           refs[2 * exchange.n + 1:])
        shard, step = pl.program_id(0), pl.program_id(1)

        @pl.when((shard == 0) & (step == 0))
        def _():
            exchange.start(ex_in, ex_out, ex_sems)

        @pl.when(step == 0)
        def _():
            out_ref[...] = jnp.zeros_like(out_ref)

        out_ref[...] += _tn(x_ref[...], dp_ref[...])

        @pl.when((shard == N_DEV - 1) & (step == n_k - 1))
        def _():
            exchange.wait(ex_in, ex_out, ex_sems)

    return pl.pallas_call(
        body, name="in_proj_weight_grad", grid=(N_DEV, n_k),
        out_shape=[jax.ShapeDtypeStruct((N_DEV, D_MODEL, shard_cols), F32)] + exchange.out_shape,
        in_specs=[pl.BlockSpec((tk, D_MODEL), lambda j, k: (k, 0)),
                  pl.BlockSpec((tk, shard_cols), lambda j, k: (k, j))] + [ANY] * exchange.n,
        out_specs=[pl.BlockSpec((None, D_MODEL, shard_cols), lambda j, k: (j, 0, 0))] + [ANY] * exchange.n,
        scratch_shapes=exchange.scratch_shapes,
        compiler_params=pltpu.CompilerParams(dimension_semantics=("arbitrary", "arbitrary"),
                                             vmem_limit_bytes=VMEM_LIMIT_BYTES),
    )(xb, dp, *exchange.inputs)


class _Exchange:
    def __init__(self, pair=(), chip=()):
        self.pair, self.chip = tuple(pair), tuple(chip)
        self.inputs = self.pair + self.chip
        self.n = len(self.inputs)
        self.out_shape = [jax.ShapeDtypeStruct((4, 1) + g.shape[2:], F32) for g in self.pair]
        self.out_shape += [jax.ShapeDtypeStruct((3,) + s.shape[1:], s.dtype) for s in self.chip]
        self.scratch_shapes = [pltpu.SemaphoreType.DMA((max(len(self.pair), 1),)) for _ in range(2)]
        self.scratch_shapes += [pltpu.SemaphoreType.DMA((max(len(self.chip), 1), 3)) for _ in range(2)]

    def _copies(self, ins, outs, sems):
        if not self.n:
            return []
        pair_send, pair_recv, chip_send, chip_recv = sems
        x, y, c = lax.axis_index("x"), lax.axis_index("y"), lax.axis_index("c")
        copies = []
        for a in range(len(self.pair)):
            copies.append(pltpu.make_async_remote_copy(
                src_ref=ins[a].at[:, pl.ds(1 - c, 1)], dst_ref=outs[a],
                send_sem=pair_send.at[a], recv_sem=pair_recv.at[a], device_id=(x, y, 1 - c), device_id_type=MESH))
        for a in range(len(self.chip)):
            src, dst = ins[len(self.pair) + a], outs[len(self.pair) + a]
            for k, (px, py) in enumerate([(1 - x, y), (x, 1 - y), (1 - x, 1 - y)]):
                copies.append(pltpu.make_async_remote_copy(
                    src_ref=src.at[2 * px + py], dst_ref=dst.at[k], send_sem=chip_send.at[a, k],
                    recv_sem=chip_recv.at[a, k], device_id=(px, py, c), device_id_type=MESH))
        return copies

    def start(self, ins, outs, sems):
        for cp in self._copies(ins, outs, sems):
            cp.start()

    def wait(self, ins, outs, sems):
        for cp in self._copies(ins, outs, sems):
            cp.wait()


def _row_tile(rows, cols, itemsize=4):
    if rows * cols * itemsize <= (1 << 20):
        return rows
    t = 8
    while 2 * t * cols * itemsize <= (1 << 20) and rows % (2 * t) == 0:
        t *= 2
    assert rows % t == 0, (rows, cols)
    return t


def _pair_sum(place, grads, landeds, dtypes):
    n = len(grads)

    def body(place_ref, *refs):
        for a in range(n):
            refs[2 * n + a][...] = (refs[2 * a][...] + refs[2 * a + 1][...]).astype(dtypes[a])

    in_specs, out_specs, operands = [], [], []
    for g, l in zip(grads, landeds):
        rows, cols = g.shape[2:]
        in_specs += [pl.BlockSpec((None, None, rows, cols), lambda k, pr: (k, pr[1], 0, 0)),
                     pl.BlockSpec((None, None, rows, cols), lambda k, pr: (k, 0, 0, 0))]
        out_specs.append(pl.BlockSpec((None, rows, cols), lambda k, pr: (k, 0, 0)))
        operands += [g, l]
    return pl.pallas_call(
        body, name="pair_sum",
        out_shape=[jax.ShapeDtypeStruct((4,) + g.shape[2:], dt) for g, dt in zip(grads, dtypes)],
        grid_spec=pltpu.PrefetchScalarGridSpec(num_scalar_prefetch=1, grid=(4,), in_specs=in_specs,
                                               out_specs=out_specs),
        compiler_params=pltpu.CompilerParams(dimension_semantics=("parallel",), vmem_limit_bytes=VMEM_LIMIT_BYTES),
    )(place, *operands)


def _adamw_math(w, g, m, v):
    m = ADAM_B1 * m + (1.0 - ADAM_B1) * g
    v = ADAM_B2 * v + (1.0 - ADAM_B2) * (g * g)
    m_hat = m / (1.0 - ADAM_B1 ** ADAM_STEP)
    v_hat = v / (1.0 - ADAM_B2 ** ADAM_STEP)
    delta = -ADAM_LR * (m_hat / (jnp.sqrt(v_hat) + ADAM_EPS) + ADAM_WD * w)
    return delta, m, v


def _adamw_shard(place, grad, landed, got, w, m, v):
    rows, cols = w.shape
    tr = _row_tile(rows, cols)

    def body(place_ref, a_ref, b_ref, got_ref, w_ref, m_ref, v_ref, g_out, d_out, m_out, v_out):
        g = a_ref[...] + b_ref[...]
        for k in range(3):
            g = g + got_ref[k].astype(F32)
        g_out[...] = g
        d_out[...], m_out[...], v_out[...] = _adamw_math(w_ref[...], g, m_ref[...], v_ref[...])

    spec = pl.BlockSpec((tr, cols), lambda i, pr: (i, 0))
    out = jax.ShapeDtypeStruct((rows, cols), F32)
    return pl.pallas_call(
        body, name="adamw_shard", out_shape=[out] * 4,
        grid_spec=pltpu.PrefetchScalarGridSpec(
            num_scalar_prefetch=1, grid=(rows // tr,),
            in_specs=[pl.BlockSpec((None, None, tr, cols), lambda i, pr: (pr[0], pr[1], i, 0)),
                      pl.BlockSpec((None, None, tr, cols), lambda i, pr: (pr[0], 0, i, 0)),
                      pl.BlockSpec((3, tr, cols), lambda i, pr: (0, i, 0)), spec, spec, spec],
            out_specs=[spec] * 4),
        compiler_params=pltpu.CompilerParams(dimension_semantics=("parallel",)),
    )(place, grad, landed, got, w, m, v)


def _gather_small(grad, landed, got):
    rows, cols = got.shape[1:]

    def body(grad_ref, landed_ref, got_ref, out_ref, send_sems, recv_sems):
        x, y, c = lax.axis_index("x"), lax.axis_index("y"), lax.axis_index("c")
        me, sibling = (x, y, c), (x, y, 1 - c)
        chips = [(1 - x, y), (x, 1 - y), (1 - x, 1 - y)]

        def slot(px, py, pc):
            return out_ref.at[4 * px + 2 * py + pc]

        def copy(k, block, to):
            return pltpu.make_async_remote_copy(
                src_ref=slot(*block), dst_ref=slot(*block), send_sem=send_sems.at[k], recv_sem=recv_sems.at[k],
                device_id=to, device_id_type=MESH)

        mine = grad_ref[2 * x + y, c] + landed_ref[2 * x + y, 0]
        out_ref[4 * x + 2 * y + c] = ((mine + got_ref[0]) + got_ref[1]) + got_ref[2]
        first = [copy(0, me, sibling)] + [copy(1 + j, me, (*chip, c)) for j, chip in enumerate(chips)]
        for cp in first:
            cp.start()
        passed = [copy(4 + j, (*chip, c), sibling) for j, chip in enumerate(chips)]
        for j, chip in enumerate(chips):
            copy(1 + j, (*chip, c), me).wait_recv()
            passed[j].start()
        copy(0, sibling, me).wait_recv()
        for j, chip in enumerate(chips):
            copy(4 + j, (*chip, 1 - c), me).wait_recv()
        for cp in first + passed:
            cp.wait_send()

    return pl.pallas_call(
        body, name="gather_small", out_shape=jax.ShapeDtypeStruct((N_DEV, rows, cols), F32),
        in_specs=[VMEM] * 3, out_specs=VMEM,
        scratch_shapes=[pltpu.SemaphoreType.DMA((7,)), pltpu.SemaphoreType.DMA((7,))],
    )(grad, landed, got)


def _adamw_small(params):
    n = len(params)

    def body(*refs):
        ins, outs = refs[:4 * n], refs[4 * n:]
        for i in range(n):
            w_ref, g_ref, m_ref, v_ref = ins[4 * i:4 * i + 4]
            d, m, v = _adamw_math(w_ref[...], g_ref[...], m_ref[...], v_ref[...])
            outs[3 * i][...] = d
            outs[3 * i + 1][...] = m
            outs[3 * i + 2][...] = v

    flat = [a for p in params for a in p]
    out_shape = [jax.ShapeDtypeStruct(p[0].shape, F32) for p in params for _ in range(3)]
    outs = pl.pallas_call(
        body, name="adamw_small", out_shape=out_shape, in_specs=[VMEM] * (4 * n), out_specs=[VMEM] * (3 * n),
    )(*flat)
    return [tuple(outs[3 * i:3 * i + 3]) for i in range(n)]


def _pack_small(parts):
    rows = []
    for a in parts:
        r, c = a.shape
        rows.append(jnp.pad(a, ((0, (-r) % 8), (0, CHUNK - c))))
    flat = jnp.concatenate(rows, axis=0)
    return jnp.pad(flat, ((0, N_DEV * SMALL_ROWS - flat.shape[0]), (0, 0))).reshape(N_DEV, SMALL_ROWS, CHUNK)


def _unpack_small(flat, shapes):
    out, at = [], 0
    for r, c in shapes:
        out.append(flat[at:at + r, :c])
        at += r + (-r) % 8
    return out


def kernel(x, w_in_a, sgu_ln_g, sgu_ln_b, w_spatial, b_spatial, w_out_a, w_kv, w_in_b, attn_sinks, rel_bias, w_out_b, post_ln_g, post_ln_b, loss_target, m_w_in_a, m_sgu_ln_g, m_sgu_ln_b, m_w_spatial, m_b_spatial, m_w_out_a, m_w_kv, m_w_in_b, m_attn_sinks, m_rel_bias, m_w_out_b, m_post_ln_g, m_post_ln_b, v_w_in_a, v_sgu_ln_g, v_sgu_ln_b, v_w_spatial, v_b_spatial, v_w_out_a, v_w_kv, v_w_in_b, v_attn_sinks, v_rel_bias, v_w_out_b, v_post_ln_g, v_post_ln_b):
    seq = x.shape[1]
    me = 4 * lax.axis_index("x") + 2 * lax.axis_index("y") + lax.axis_index("c")
    bucket = jnp.asarray(_band_tables())

    ln_shard = jnp.concatenate([sgu_ln_g, sgu_ln_b], axis=0)
    x2, tgt = x[0], loss_target[0]
    order = jnp.stack([2 * px + py for px, py, _ in _use_order()[::2]]).astype(jnp.int32)
    p, wa, woa, ln_all = _gather_in_proj(order, x2, w_in_a[0], (w_out_a[0],), ln_shard)
    woa = woa.reshape(A_WIDTH, D_MODEL)
    ln_gb = jnp.transpose(ln_all, (1, 0, 2)).reshape(2, A_WIDTH)
    ws = w_spatial[0]
    bsp_t = jnp.pad(b_spatial[0].T, ((0, 0), (0, CHUNK - A_GROUPS)))
    bias = _bias_table(rel_bias, bucket)

    r0, xb, wkv, wb, wob = _fwd_a(x2, p, ln_gb, ws, bsp_t, woa, (w_kv, w_in_b[0], w_out_b[0]))
    wkv = wkv.reshape(D_MODEL, 2 * KV_WIDTH)
    wob = wob.reshape(B_WIDTH, D_MODEL)
    q, z, kv, dr1, loss, dg1, db1 = _fwd_b(r0, tgt, wb, wkv, wob, post_ln_g, post_ln_b, attn_sinks, bias)
    dr0, dwb, dwkv, dwob, dsink, dlog, dpost0 = _bwd_b(r0, dr1, q, z, kv, wb, wkv, wob, post_ln_g, post_ln_b,
                                                       attn_sinks, bias)
    drel = _rel_bias_grad(dlog, bucket)

    place = jnp.stack([2 * lax.axis_index("x") + lax.axis_index("y"), lax.axis_index("c")]).astype(jnp.int32)

    def by_device(g):
        return g.reshape((4, 2, g.shape[0] // N_DEV) + g.shape[1:]) if g.ndim == 2 else g.reshape((4, 2) + g.shape[1:])

    dwb, dwkv, dwob = by_device(dwb), by_device(dwkv), by_device(dwob)
    dp, dwoa, dws, dbsp_t, dln, l_wb, l_wkv, l_wob = _bwd_a(dr0, p, ln_gb, ws, bsp_t, woa,
                                                            _Exchange(pair=(dwb, dwkv, dwob)))
    dwoa = by_device(dwoa)
    small_shapes = [(A_GROUPS * CHUNK, CHUNK), (A_GROUPS, CHUNK), (1, N_Q_HEADS), (REL_BUCKETS, N_Q_HEADS),
                    (16, CHUNK), (16, CHUNK), (16, CHUNK), (16, CHUNK), (1, 1)]
    small = by_device(_pack_small([
        dws.reshape(A_GROUPS * CHUNK, CHUNK), dbsp_t[:, :A_GROUPS].T, dsink[:, :N_Q_HEADS], drel[:, :N_Q_HEADS],
        jnp.concatenate([dpost0[0:1], dg1], axis=0).reshape(16, CHUNK),
        jnp.concatenate([dpost0[1:2], db1], axis=0).reshape(16, CHUNK),
        dln[0].reshape(16, CHUNK), dln[1].reshape(16, CHUNK), loss]))
    s_wb, s_wkv, s_wob = _pair_sum(place, (dwb, dwkv, dwob), (l_wb, l_wkv, l_wob), (BF16,) * 3)
    dwa, l_woa, l_small, g_wb, g_wkv, g_wob = _in_proj_weight_grad(
        xb, dp, wa.shape[2], _Exchange(pair=(dwoa, small), chip=(s_wb, s_wkv, s_wob)))
    dwa = by_device(dwa)
    s_woa, s_small = _pair_sum(place, (dwoa, small), (l_woa, l_small), (BF16, F32))
    _, blocks = _input_grad_blocks(seq)
    cut = blocks // 4
    dx, l_wa, g_woa, g_small = _input_grad(dp, dr0, wa, 0, cut, None, _Exchange(pair=(dwa,), chip=(s_woa, s_small)))
    s_wa, = _pair_sum(place, (dwa,), (l_wa,), (BF16,))
    dx, g_wa = _input_grad(dp, dr0, wa, cut, blocks - cut, dx, _Exchange(chip=(s_wa,)))
    full = [dwa, dwoa, dwkv, dwb, dwob, small]
    landed = [l_wa, l_woa, l_wkv, l_wb, l_wob, l_small]
    got = [g_wa, g_woa, g_wkv, g_wb, g_wob, g_small]

    big = [(w_in_a, m_w_in_a, v_w_in_a), (w_out_a, m_w_out_a, v_w_out_a), (w_kv, m_w_kv, v_w_kv),
           (w_in_b, m_w_in_b, v_w_in_b), (w_out_b, m_w_out_b, v_w_out_b)]
    big_out = []
    for i, (w, m, v) in enumerate(big):
        shape2 = got[i].shape[1:]
        res = _adamw_shard(place, full[i], landed[i], got[i], w.reshape(shape2), m.reshape(shape2),
                           v.reshape(shape2))
        big_out.append([r.reshape(w.shape) for r in res])

    small_all = _gather_small(full[5], landed[5], got[5]).reshape(N_DEV * SMALL_ROWS, CHUNK)
    g_ws, g_bsp, g_sink, g_rel, g_pg, g_pb, g_lng, g_lnb, loss_all = _unpack_small(small_all, small_shapes)
    g_ws = g_ws.reshape(w_spatial.shape)
    g_bsp = g_bsp.reshape(b_spatial.shape)
    g_pg = g_pg.reshape(post_ln_g.shape)
    g_pb = g_pb.reshape(post_ln_b.shape)
    g_lng = lax.dynamic_slice(g_lng.reshape(1, A_WIDTH), (0, me * sgu_ln_g.shape[1]), sgu_ln_g.shape)
    g_lnb = lax.dynamic_slice(g_lnb.reshape(1, A_WIDTH), (0, me * sgu_ln_b.shape[1]), sgu_ln_b.shape)
    small_params = [(sgu_ln_g, g_lng, m_sgu_ln_g, v_sgu_ln_g), (sgu_ln_b, g_lnb, m_sgu_ln_b, v_sgu_ln_b),
                    (w_spatial, g_ws, m_w_spatial, v_w_spatial), (b_spatial, g_bsp, m_b_spatial, v_b_spatial),
                    (attn_sinks, g_sink, m_attn_sinks, v_attn_sinks), (rel_bias, g_rel, m_rel_bias, v_rel_bias),
                    (post_ln_g, g_pg, m_post_ln_g, v_post_ln_g), (post_ln_b, g_pb, m_post_ln_b, v_post_ln_b)]
    flat2 = [tuple(a.reshape(-1, a.shape[-1]) for a in p) for p in small_params]
    small_out = [tuple(r.reshape(p[0].shape) for r in res) for res, p in zip(_adamw_small(flat2), small_params)]

    order = [("big", 0), ("small", 0), ("small", 1), ("small", 2), ("small", 3), ("big", 1), ("big", 2), ("big", 3),
             ("small", 4), ("small", 5), ("big", 4), ("small", 6), ("small", 7)]
    grads, deltas, new_m, new_v = [], [], [], []
    for kind, i in order:
        if kind == "big":
            g, d, m, v = big_out[i]
        else:
            g = small_params[i][1]
            d, m, v = small_out[i]
        grads.append(g)
        deltas.append(d)
        new_m.append(m)
        new_v.append(v)
    return (loss_all[0, 0], dx.reshape(x.shape), *grads, *deltas, *new_m, *new_v)
```

```python
import functools

import jax
import jax.numpy as jnp
import numpy as np
from jax import lax
from jax.experimental import pallas as pl
from jax.experimental.pallas import tpu as pltpu

F32 = jnp.float32
BF16 = jnp.bfloat16

D_MODEL = 1024
A_WIDTH = 2048
A_GROUPS = 8
A_GROUP_DIM = A_WIDTH // A_GROUPS
CHUNK = 128
HEAD_DIM = 64
N_Q_HEADS = 16
N_KV_HEADS = 2
PAIRS_PER_KV = N_Q_HEADS // N_KV_HEADS // 2
B_WIDTH = N_Q_HEADS * HEAD_DIM
KV_WIDTH = N_KV_HEADS * HEAD_DIM
REL_BUCKETS = 32
REL_MAX_DIST = 128
DEPTH = 2
ALPHA = (2.0 * DEPTH) ** 0.25
LN_EPS = 1e-5
NEG_INF = -1e30
SCALE = HEAD_DIM ** -0.5

ADAM_LR = 0.001
ADAM_B1 = 0.9
ADAM_B2 = 0.999
ADAM_EPS = 1e-08
ADAM_WD = 0.01
ADAM_STEP = 10

N_DEV = 8
MESH = pl.DeviceIdType.MESH
TOKENS_IN_PROJ = 1024
TOKENS_WGRAD = 4096
TOKENS_A = 256
TOKENS_A_FWD = 512
TOKENS_B = 512
PART_ROWS_B = 256
TOKENS_B_BWD = 256
TOKENS_DX = 512
VMEM_LIMIT_BYTES = 60 * 1024 * 1024
SMALL_ROWS = 144

ANY = pl.BlockSpec(memory_space=pl.ANY)
VMEM = pl.BlockSpec(memory_space=pltpu.VMEM)
SMEM = pl.BlockSpec(memory_space=pltpu.SMEM)


def _nn(a, b):
    return jnp.dot(a, b, preferred_element_type=F32)


def _nt(a, b):
    return lax.dot_general(a, b, (((1,), (1,)), ((), ())), preferred_element_type=F32)


def _tn(a, b):
    return lax.dot_general(a, b, (((0,), (0,)), ((), ())), preferred_element_type=F32)


def _resident(shape):
    nd = len(shape)
    return pl.BlockSpec(shape, lambda i: (0,) * nd, pipeline_mode=pl.Buffered(1))


def _rows(tm, width, reverse_of=None):
    if reverse_of is None:
        return pl.BlockSpec((tm, width), lambda i: (i, 0))
    return pl.BlockSpec((tm, width), lambda i: (reverse_of - 1 - i, 0))


def _seq_params():
    return pltpu.CompilerParams(dimension_semantics=("arbitrary",), vmem_limit_bytes=VMEM_LIMIT_BYTES)


def _in_turns(first, second):
    merged = [(i / max(len(first), 1), 0, f) for i, f in enumerate(first)]
    merged += [((i + 0.5) / max(len(second), 1), 1, f) for i, f in enumerate(second)]
    return [f for _, _, f in sorted(merged, key=lambda t: t[:2])]


def _ln_normalize(r):
    mu = jnp.mean(r, axis=-1, keepdims=True)
    xc = r - mu
    var = jnp.mean(xc * xc, axis=-1, keepdims=True)
    rstd = lax.rsqrt(var + LN_EPS)
    return xc * rstd, rstd


def _ln_backward(dxhat, xhat, rstd):
    m1 = jnp.mean(dxhat, axis=-1, keepdims=True)
    m2 = jnp.mean(dxhat * xhat, axis=-1, keepdims=True)
    return rstd * (dxhat - m1 - xhat * m2)


def _silu_and_grad(z):
    sig = jax.nn.sigmoid(z)
    silu = z * sig
    return silu, sig + silu * (1.0 - sig)


def _band_tables():
    t = np.arange(CHUNK, dtype=np.int32)[:, None]
    j = np.arange(2 * CHUNK, dtype=np.int32)[None, :]
    d = t + CHUNK - j
    in_window = (d >= 0) & (d < CHUNK)
    dc = np.clip(d, 0, REL_MAX_DIST - 1)
    max_exact = REL_BUCKETS // 2
    df = np.maximum(dc, 1).astype(np.float32)
    large = max_exact + (np.log(df / np.float32(max_exact)) / np.float32(np.log(REL_MAX_DIST / max_exact))
                         * np.float32(REL_BUCKETS - max_exact)).astype(np.int32)
    large = np.minimum(large, REL_BUCKETS - 1)
    bucket = np.where(dc < max_exact, dc, large)
    return np.where(in_window, bucket, -1).astype(np.int32)


def _use_order():
    x, y, c = lax.axis_index("x"), lax.axis_index("y"), lax.axis_index("c")
    blocks = [(x, y, c), (x, y, 1 - c)]
    for px, py in [(1 - x, y), (x, 1 - y), (1 - x, 1 - y)]:
        blocks += [(px, py, c), (px, py, 1 - c)]
    return blocks


def _gather_in_proj(order, x, w_in_a, small, ln_gb):
    seq = x.shape[0]
    tm = min(TOKENS_IN_PROJ, seq)
    n_blocks = seq // tm
    shard_cols = w_in_a.shape[1]
    n = len(small) + 2

    def body(order_ref, x_ref, *refs):
        ins, refs = refs[:n], refs[n:]
        p_ref, outs, xb_out, refs = refs[0], refs[1:n + 1], refs[n + 1], refs[n + 2:]
        wa_vmem, xb_vmem, stage = refs[0], refs[1], refs[2:n]
        send_sems, recv_sems, local_sems, out_sems, xb_sem = refs[n:]
        xb_copy = pltpu.make_async_copy(xb_vmem, xb_out, xb_sem)
        phase, blk = pl.program_id(0), pl.program_id(1)
        x_, y_, c_ = lax.axis_index("x"), lax.axis_index("y"), lax.axis_index("c")
        me, sibling = (x_, y_, c_), (x_, y_, 1 - c_)
        chips = [(1 - x_, y_), (x_, 1 - y_), (1 - x_, 1 - y_)]
        use = _use_order()

        def index(px, py, pc):
            return 4 * px + 2 * py + pc

        def slot(a, block):
            return (wa_vmem if a == 0 else outs[a]).at[index(*block)]

        def own(a):
            return slot(0, me) if a == 0 else (ins[a] if a == n - 1 else stage[a - 1])

        def copy(a, k, block, to, src=None):
            return pltpu.make_async_remote_copy(
                src_ref=slot(a, block) if src is None else src, dst_ref=slot(a, block),
                send_sem=send_sems.at[a, k], recv_sem=recv_sems.at[a, k], device_id=to, device_id_type=MESH)

        def keep(a):
            return pltpu.make_async_copy(own(a), slot(a, me), local_sems.at[a])

        def write_out(k):
            return pltpu.make_async_copy(slot(0, use[k]), outs[0].at[index(*use[k])], out_sems.at[k])

        def pass_on(a, j):
            copy(a, 1 + j, (*chips[j], c_), me).wait_recv()
            copy(a, 4 + j, (*chips[j], c_), sibling).start()

        def send(a, j):
            copy(a, 1 + j, me, (*chips[j], c_), own(a)).start()

        @pl.when((phase == 0) & (blk == 0))
        def _():
            wa_vmem[index(*me)] = ins[0][...].astype(BF16)
            for a in range(1, n - 1):
                stage[a - 1][...] = ins[a][...].astype(BF16)
            send(0, 0)
            send(0, 1)
            for a in range(n):
                copy(a, 0, me, sibling, own(a)).start()
                if a:
                    keep(a).start()
            write_out(0).start()
            copy(0, 0, sibling, me).wait_recv()
            write_out(1).start()

        for j in range(3):
            @pl.when((phase == 1 + j) & (blk == 0))
            def _(j=j):
                pass_on(0, j)
                write_out(2 + 2 * j).start()
                if j == 0:
                    xb_copy.start()
                    send(0, 2)
                    for a in range(1, n):
                        for jj in range(3):
                            send(a, jj)
                copy(0, 4 + j, (*chips[j], 1 - c_), me).wait_recv()
                write_out(3 + 2 * j).start()
                if j == 2:
                    for a in range(1, n):
                        pass_on(a, 0)
                        pass_on(a, 1)

        rows = pl.ds(pl.multiple_of(blk * tm, tm), tm)

        @pl.when(phase == 0)
        def _():
            xb_vmem[rows, :] = x_ref[...].astype(BF16)

        chip = order_ref[phase]
        xb = xb_vmem[rows, :]
        p_ref[:, :shard_cols] = _nn(xb, wa_vmem[2 * chip])
        p_ref[:, shard_cols:] = _nn(xb, wa_vmem[2 * chip + 1])

        @pl.when((phase == N_DEV // 2 - 1) & (blk == n_blocks - 1))
        def _():
            for a in range(1, n):
                pass_on(a, 2)
            for a in range(1, n):
                copy(a, 0, sibling, me).wait_recv()
                for j, chip in enumerate(chips):
                    copy(a, 4 + j, (*chip, 1 - c_), me).wait_recv()
            for a in range(n):
                for k in range(7):
                    copy(a, k, me, me).wait_send()
                if a:
                    keep(a).wait()
            for k in range(N_DEV):
                write_out(k).wait()
            xb_copy.wait()

    gathered = [jax.ShapeDtypeStruct((N_DEV,) + w_in_a.shape, BF16)]
    gathered += [jax.ShapeDtypeStruct((N_DEV,) + s.shape, BF16) for s in small]
    gathered.append(jax.ShapeDtypeStruct((N_DEV,) + ln_gb.shape, F32))
    return pl.pallas_call(
        body, name="gather_in_proj",
        out_shape=[jax.ShapeDtypeStruct((seq, N_DEV * shard_cols), F32)] + gathered
        + [jax.ShapeDtypeStruct((seq, D_MODEL), BF16)],
        grid_spec=pltpu.PrefetchScalarGridSpec(
            num_scalar_prefetch=1, grid=(N_DEV // 2, n_blocks),
            in_specs=[pl.BlockSpec((tm, D_MODEL), lambda k, i, order: (jnp.where(k == 0, i, n_blocks - 1), 0))]
            + [VMEM] * n,
            out_specs=[pl.BlockSpec((tm, 2 * shard_cols), lambda k, i, order: (i, order[k]))] + [ANY] * (n + 1),
            scratch_shapes=[pltpu.VMEM((N_DEV,) + w_in_a.shape, BF16), pltpu.VMEM((seq, D_MODEL), BF16)]
            + [pltpu.VMEM(s.shape, BF16) for s in small] + [
                pltpu.SemaphoreType.DMA((n, 7)), pltpu.SemaphoreType.DMA((n, 7)), pltpu.SemaphoreType.DMA((n,)),
                pltpu.SemaphoreType.DMA((N_DEV,)), pltpu.SemaphoreType.DMA(())]),
        compiler_params=pltpu.CompilerParams(dimension_semantics=("arbitrary", "arbitrary"),
                                             vmem_limit_bytes=VMEM_LIMIT_BYTES),
    )(order, x, w_in_a, *small, ln_gb)


def _bias_table(rel_bias, bucket):
    def body(rel_ref, bucket_ref, out_ref):
        first = jnp.broadcast_to(bucket_ref[0:1, :], (8, 2 * CHUNK))
        col = lax.broadcasted_iota(jnp.int32, (CHUNK, 2 * CHUNK), 1)
        for h in range(N_Q_HEADS):
            row = jnp.full(first.shape, NEG_INF, F32)
            for b in range(REL_BUCKETS):
                row = jnp.where(first == b, rel_ref[b, h], row)
            tab = pltpu.roll(jnp.broadcast_to(row[0:1, :], (CHUNK, 2 * CHUNK)), 0, 1, stride=1, stride_axis=0)
            pair = h // 2
            rows = slice((pair % PAIRS_PER_KV) * CHUNK, (pair % PAIRS_PER_KV + 1) * CHUNK)
            half = slice((h % 2) * 2 * CHUNK, (h % 2 + 1) * 2 * CHUNK)
            out_ref[1, pair // PAIRS_PER_KV, rows, half] = tab
            out_ref[0, pair // PAIRS_PER_KV, rows, half] = jnp.where(col < CHUNK, NEG_INF, tab)

    return pl.pallas_call(
        body, name="bias_table",
        out_shape=jax.ShapeDtypeStruct((2, N_KV_HEADS, PAIRS_PER_KV * CHUNK, 4 * CHUNK), F32),
        in_specs=[SMEM, VMEM], out_specs=VMEM,
    )(rel_bias, bucket)


def _rel_bias_grad(dlogits, bucket):
    def body(dl_ref, bucket_ref, out_ref):
        i = lax.broadcasted_iota(jnp.int32, (CHUNK, CHUNK), 0)
        k = lax.broadcasted_iota(jnp.int32, (CHUNK, CHUNK), 1)
        reverse = jnp.where(i + k == CHUNK - 1, 1.0, 0.0)
        sums = []
        for h in range(N_Q_HEADS):
            pair = h // 2
            dl = dl_ref[pair // PAIRS_PER_KV, (pair % PAIRS_PER_KV) * CHUNK:(pair % PAIRS_PER_KV + 1) * CHUNK,
                        (h % 2) * 2 * CHUNK:(h % 2 + 1) * 2 * CHUNK]
            backwards = jnp.dot(reverse, dl, precision=lax.Precision.HIGHEST, preferred_element_type=F32)
            turned = pltpu.roll(backwards, CHUNK + 1, 1, stride=1, stride_axis=0)
            sums.append(jnp.sum(turned, axis=0, keepdims=True))
        by_offset = jnp.concatenate(sums, axis=0)
        buckets = lax.broadcasted_iota(jnp.int32, (REL_BUCKETS, 2 * CHUNK), 0)
        one_hot = jnp.where(jnp.broadcast_to(bucket_ref[0:1, :], buckets.shape) == buckets, 1.0, 0.0)
        out_ref[...] = lax.dot_general(one_hot, by_offset, (((1,), (1,)), ((), ())),
                                       precision=lax.Precision.HIGHEST, preferred_element_type=F32)

    return pl.pallas_call(
        body, name="rel_bias_grad", out_shape=jax.ShapeDtypeStruct((REL_BUCKETS, N_Q_HEADS), F32),
        in_specs=[VMEM, VMEM], out_specs=VMEM,
    )(dlogits, bucket)


def _masked_spatial(ws_ref, g):
    t = lax.broadcasted_iota(jnp.int32, (CHUNK, CHUNK), 0)
    s = lax.broadcasted_iota(jnp.int32, (CHUNK, CHUNK), 1)
    return jnp.where(s <= t, ws_ref[g], 0.0).astype(BF16)


def _spatial_gate(ws_ref, bsp_t_ref, vn, s_ref, n_chunks):
    for g in range(A_GROUPS):
        w = _masked_spatial(ws_ref, g)
        cols = slice(g * A_GROUP_DIM, (g + 1) * A_GROUP_DIM)
        for ci in range(n_chunks):
            rows = slice(ci * CHUNK, (ci + 1) * CHUNK)
            s_ref[rows, cols] = _nn(w, vn[rows, cols]) + bsp_t_ref[:, g:g + 1]


def _fwd_a(x, p, ln_gb, w_spatial, bsp_t, woa, later):
    seq = x.shape[0]
    tm = min(TOKENS_A_FWD, seq)
    n_chunks = tm // CHUNK
    n_blocks = seq // tm
    n = len(later)

    def body(x_ref, p_ref, ln_ref, ws_ref, bsp_ref, woa_ref, *refs):
        shards, refs = refs[:n], refs[n:]
        r0_ref, gathered, refs = refs[0], refs[1:1 + n], refs[1 + n:]
        s_ref, stage, (send_sems, recv_sems, local_sems) = refs[0], refs[1:1 + n], refs[1 + n:]
        step = pl.program_id(0)

        def copies(a):
            x_, y_, c_ = lax.axis_index("x"), lax.axis_index("y"), lax.axis_index("c")
            mine = 4 * x_ + 2 * y_ + c_
            out = [pltpu.make_async_copy(stage[a], gathered[a].at[mine], local_sems.at[a])]
            for r in range(1, N_DEV):
                to = (1 - x_ if r & 4 else x_, 1 - y_ if r & 2 else y_, 1 - c_ if r & 1 else c_)
                out.append(pltpu.make_async_remote_copy(
                    src_ref=stage[a], dst_ref=gathered[a].at[mine], send_sem=send_sems.at[a, r - 1],
                    recv_sem=recv_sems.at[a, r - 1], device_id=to, device_id_type=MESH))
            return out

        @pl.when(step == 0)
        def _():
            for a in range(n):
                stage[a][...] = shards[a][...].astype(BF16)
                for cp in copies(a):
                    cp.start()

        xv = x_ref[...]
        vhat, _ = _ln_normalize(p_ref[:, A_WIDTH:2 * A_WIDTH])
        vn = (vhat * ln_ref[0:1, :] + ln_ref[1:2, :]).astype(BF16)
        _spatial_gate(ws_ref, bsp_ref, vn, s_ref, n_chunks)
        z = p_ref[:, 2 * A_WIDTH:]
        y = p_ref[:, :A_WIDTH] * s_ref[...] * (z * jax.nn.sigmoid(z))
        r0_ref[...] = ALPHA * xv + _nn(y.astype(BF16), woa_ref[...])

        @pl.when(step == n_blocks - 1)
        def _():
            for a in range(n):
                for cp in copies(a):
                    cp.wait()

    return pl.pallas_call(
        body, name="fwd_a", grid=(n_blocks,),
        out_shape=[jax.ShapeDtypeStruct((seq, D_MODEL), F32)]
        + [jax.ShapeDtypeStruct((N_DEV,) + s.shape, BF16) for s in later],
        in_specs=[_rows(tm, D_MODEL), _rows(tm, 3 * A_WIDTH), _resident(ln_gb.shape), _resident(w_spatial.shape),
                  _resident(bsp_t.shape), _resident(woa.shape)] + [VMEM] * n,
        out_specs=[_rows(tm, D_MODEL)] + [ANY] * n,
        scratch_shapes=[pltpu.VMEM((tm, A_WIDTH), F32)] + [pltpu.VMEM(s.shape, BF16) for s in later] + [
            pltpu.SemaphoreType.DMA((n, N_DEV - 1)), pltpu.SemaphoreType.DMA((n, N_DEV - 1)),
            pltpu.SemaphoreType.DMA((n,))],
        compiler_params=_seq_params(),
    )(x, p, ln_gb, w_spatial, bsp_t, woa, *later)


def _expand_heads(kv):
    lo = lax.broadcasted_iota(jnp.int32, kv.shape, 1) < HEAD_DIM
    swapped = pltpu.roll(kv, HEAD_DIM, 1)
    zero = jnp.zeros_like(kv)
    return (jnp.where(lo, kv, zero).astype(BF16), jnp.where(lo, zero, swapped).astype(BF16),
            jnp.where(lo, swapped, zero).astype(BF16), jnp.where(lo, zero, kv).astype(BF16))


def _fold_heads(acc, kvh):
    top, bot = acc[:2 * CHUNK], acc[2 * CHUNK:]
    lo = lax.broadcasted_iota(jnp.int32, top.shape, 1) < HEAD_DIM
    t = jnp.where(lo, top, 0.0)
    b = jnp.where(lo, 0.0, bot)
    if kvh == 0:
        return t + pltpu.roll(b, HEAD_DIM, 1)
    return pltpu.roll(t, HEAD_DIM, 1) + b


STACK = PAIRS_PER_KV * CHUNK


def _pair_cols(kvh, pp):
    pair = kvh * PAIRS_PER_KV + pp
    return slice(pair * CHUNK, (pair + 1) * CHUNK)


def _stack_pairs(ref_or_val, rows, kvh):
    return jnp.concatenate([ref_or_val[rows, _pair_cols(kvh, pp)] for pp in range(PAIRS_PER_KV)], axis=0)


def _sink_columns(sink_ref):
    blk = lax.broadcasted_iota(jnp.int32, (STACK, 1), 0) // CHUNK
    out = []
    for kvh in range(N_KV_HEADS):
        halves = []
        for half in range(2):
            col = jnp.zeros((STACK, 1), F32)
            for pp in range(PAIRS_PER_KV):
                col = jnp.where(blk == pp, sink_ref[0, 2 * (kvh * PAIRS_PER_KV + pp) + half], col)
            halves.append(col)
        out.append(halves)
    return out


def _softmax_stack(logits, sink_cols):
    ps, sinks = [], []
    for half in range(2):
        lg = logits[:, half * 2 * CHUNK:(half + 1) * 2 * CHUNK]
        sk = sink_cols[half]
        m = jnp.maximum(jnp.max(lg, axis=-1, keepdims=True), sk)
        e = jnp.exp(lg - m)
        es = jnp.exp(sk - m)
        inv = 1.0 / (jnp.sum(e, axis=-1, keepdims=True) + es)
        ps.append(e * inv)
        sinks.append(es * inv)
    return jnp.concatenate(ps, axis=1), sinks


def _band_operands(before_k, this_k, before_v, this_v, kvh):
    kblk = jnp.concatenate([before_k[2 * kvh], this_k[2 * kvh], before_k[2 * kvh + 1], this_k[2 * kvh + 1]], axis=0)
    vblk = jnp.concatenate([before_v[2 * kvh], this_v[2 * kvh], before_v[2 * kvh + 1], this_v[2 * kvh + 1]], axis=0)
    return kblk, vblk


def _fwd_b(r0, target, wb, wkv, wob, post_g, post_b, sinks, bias):
    seq = r0.shape[0]
    tm = min(TOKENS_B, seq)
    n_chunks = tm // CHUNK
    part_rows = min(PART_ROWS_B, tm)
    n_parts = tm // part_rows
    part_chunks = part_rows // CHUNK
    shard_cols = wb.shape[2]

    def body(r0_ref, tgt_ref, wb_ref, wkv_ref, wob_ref, pg_ref, pb_ref, sink_ref, bias_ref,
             q_ref, z_ref, kv_ref, dr1_ref, loss_ref, dg_ref, db_ref, ek_last, ev_last, *scratch):
        q_s, z_s, h1_s, o_s, ek_s, ev_s = (scratch[i * n_parts:(i + 1) * n_parts] for i in range(6))
        step = pl.program_id(0)

        @pl.when(step == 0)
        def _():
            ek_last[...] = jnp.zeros_like(ek_last)
            ev_last[...] = jnp.zeros_like(ev_last)
            loss_ref[...] = jnp.zeros_like(loss_ref)
            dg_ref[...] = jnp.zeros_like(dg_ref)
            db_ref[...] = jnp.zeros_like(db_ref)

        sink_cols = _sink_columns(sink_ref)
        g1 = pg_ref[1:2, :]

        def project_steps(part):
            rows = slice(part * part_rows, (part + 1) * part_rows)
            live = {}

            def normalize():
                xhat0, _ = _ln_normalize(r0_ref[rows, :])
                h1 = xhat0 * pg_ref[0:1, :] + pb_ref[0:1, :]
                h1_s[part][...] = h1
                live["h1b"] = h1.astype(BF16)

            def shard(k):
                def run():
                    qz = _nn(live["h1b"], wb_ref[k])
                    lo, hi = k * shard_cols, (k + 1) * shard_cols
                    if hi <= B_WIDTH:
                        qs = (qz * SCALE).astype(BF16)
                        q_ref[rows, lo:hi] = qs
                        q_s[part][:, lo:hi] = qs
                    else:
                        z_ref[rows, lo - B_WIDTH:hi - B_WIDTH] = qz
                        z_s[part][:, lo - B_WIDTH:hi - B_WIDTH] = qz
                return run

            def keys():
                kv = _nn(live["h1b"], wkv_ref[...])
                kv_ref[rows, :] = kv.astype(BF16)
                for cj in range(part_chunks):
                    sub = slice(cj * CHUNK, (cj + 1) * CHUNK)
                    for i, e in enumerate(_expand_heads(kv[sub, :KV_WIDTH])):
                        ek_s[part][cj, i] = e
                    for i, e in enumerate(_expand_heads(kv[sub, KV_WIDTH:])):
                        ev_s[part][cj, i] = e

            return [normalize, keys] + [shard(k) for k in range(N_DEV)]

        def attend_steps(part):
            def stack(cj, kvh):
                def run():
                    sub = slice(cj * CHUNK, (cj + 1) * CHUNK)
                    sel = jnp.where(step * n_chunks + part * part_chunks + cj > 0, 1, 0)
                    if cj:
                        before_k, before_v = ek_s[part].at[cj - 1], ev_s[part].at[cj - 1]
                    elif part:
                        before_k, before_v = ek_s[part - 1].at[part_chunks - 1], ev_s[part - 1].at[part_chunks - 1]
                    else:
                        before_k, before_v = ek_last, ev_last
                    kblk, vblk = _band_operands(before_k, ek_s[part].at[cj], before_v, ev_s[part].at[cj], kvh)
                    logits = _nt(_stack_pairs(q_s[part], sub, kvh), kblk) + bias_ref[sel, kvh]
                    p, _ = _softmax_stack(logits, sink_cols[kvh])
                    o = _nn(p.astype(BF16), vblk)
                    for pp in range(PAIRS_PER_KV):
                        o_s[part][sub, _pair_cols(kvh, pp)] = o[pp * CHUNK:(pp + 1) * CHUNK]
                return run

            def finish():
                rows = slice(part * part_rows, (part + 1) * part_rows)
                z = z_s[part][...]
                y = o_s[part][...] * (z * jax.nn.sigmoid(z))
                r1 = ALPHA * h1_s[part][...] + _nn(y.astype(BF16), wob_ref[...])
                xhat1, rstd1 = _ln_normalize(r1)
                diff = xhat1 * g1 + pb_ref[1:2, :] - tgt_ref[rows, :]
                loss_ref[...] += 0.5 * jnp.sum(jnp.mean(diff * diff, axis=-1, keepdims=True), axis=0, keepdims=True)
                dh2 = diff * (1.0 / D_MODEL)
                dg_ref[...] += jnp.sum(dh2 * xhat1, axis=0, keepdims=True)
                db_ref[...] += jnp.sum(dh2, axis=0, keepdims=True)
                dr1_ref[rows, :] = _ln_backward(dh2 * g1, xhat1, rstd1)

            return [stack(cj, kvh) for cj in range(part_chunks) for kvh in range(N_KV_HEADS)] + [finish]

        for run in project_steps(0):
            run()
        for part in range(n_parts):
            for run in _in_turns(attend_steps(part), project_steps(part + 1) if part + 1 < n_parts else []):
                run()
        ek_last[...] = ek_s[n_parts - 1][part_chunks - 1]
        ev_last[...] = ev_s[n_parts - 1][part_chunks - 1]

    vec = pl.BlockSpec((1, D_MODEL), lambda i: (0, 0))
    return pl.pallas_call(
        body, name="fwd_b", grid=(seq // tm,),
        out_shape=[jax.ShapeDtypeStruct((seq, B_WIDTH), BF16), jax.ShapeDtypeStruct((seq, B_WIDTH), F32),
                   jax.ShapeDtypeStruct((seq, 2 * KV_WIDTH), BF16), jax.ShapeDtypeStruct((seq, D_MODEL), F32),
                   jax.ShapeDtypeStruct((1, 1), F32), jax.ShapeDtypeStruct((1, D_MODEL), F32),
                   jax.ShapeDtypeStruct((1, D_MODEL), F32)],
        in_specs=[_rows(tm, D_MODEL), _rows(tm, D_MODEL), _resident(wb.shape), _resident(wkv.shape),
                  _resident(wob.shape), _resident(post_g.shape), _resident(post_b.shape), SMEM,
                  _resident(bias.shape)],
        out_specs=[_rows(tm, B_WIDTH), _rows(tm, B_WIDTH), _rows(tm, 2 * KV_WIDTH), _rows(tm, D_MODEL),
                   pl.BlockSpec((1, 1), lambda i: (0, 0)), vec, vec],
        scratch_shapes=[pltpu.VMEM((4, CHUNK, CHUNK), BF16), pltpu.VMEM((4, CHUNK, CHUNK), BF16)]
        + [pltpu.VMEM((part_rows, B_WIDTH), BF16)] * n_parts + [pltpu.VMEM((part_rows, B_WIDTH), F32)] * n_parts
        + [pltpu.VMEM((part_rows, D_MODEL), F32)] * n_parts + [pltpu.VMEM((part_rows, B_WIDTH), F32)] * n_parts
        + [pltpu.VMEM((part_chunks, 4, CHUNK, CHUNK), BF16)] * (2 * n_parts),
        compiler_params=_seq_params(),
    )(r0, target, wb, wkv, wob, post_g, post_b, sinks, bias)


def _bwd_b(r0, dr1, q, z, kv, wb, wkv, wob, post_g, post_b, sinks, bias):
    seq = r0.shape[0]
    tm = min(TOKENS_B_BWD, seq)
    n_chunks = tm // CHUNK
    n_blocks = seq // tm
    shard_cols = wb.shape[2]

    def body(r0_ref, dr1_ref, q_ref, z_ref, kv_ref, kvp_ref, wb_ref, wkv_ref, wob_ref, pg_ref, pb_ref, sink_ref,
             bias_ref, dr0_ref, dwb_out, dwkv_out, dwob_out, dsink_ref, dl_out, dpost_ref,
             dwb_acc, dwkv_acc, dwob_acc, dl_acc, dsink_acc, carry_ref, ek_ref, ev_ref, y_ref, dqz_ref, dkv_ref):
        step = pl.program_id(0)
        block = n_blocks - 1 - step

        @pl.when(step == 0)
        def _():
            dpost_ref[...] = jnp.zeros_like(dpost_ref)
            dwb_acc[...] = jnp.zeros_like(dwb_acc)
            dwkv_acc[...] = jnp.zeros_like(dwkv_acc)
            dwob_acc[...] = jnp.zeros_like(dwob_acc)
            dl_acc[...] = jnp.zeros_like(dl_acc)
            carry_ref[...] = jnp.zeros_like(carry_ref)
            dsink_acc[...] = jnp.zeros_like(dsink_acc)

        xhat0, rstd0 = _ln_normalize(r0_ref[...])
        h1b = (xhat0 * pg_ref[0:1, :] + pb_ref[0:1, :]).astype(BF16)
        dr1 = dr1_ref[...]
        dr1b = dr1.astype(BF16)
        dy = _nt(dr1b, wob_ref[...])
        sg, dsg = _silu_and_grad(z_ref[...])
        do_all = dy * sg
        dyz = dy * dsg

        for ci in range(n_chunks + 1):
            src = kvp_ref[...] if ci == 0 else kv_ref[(ci - 1) * CHUNK:ci * CHUNK, :]
            src = src.astype(F32)
            for i, e in enumerate(_expand_heads(src[:, :KV_WIDTH])):
                ek_ref[ci, i] = e
            for i, e in enumerate(_expand_heads(src[:, KV_WIDTH:])):
                ev_ref[ci, i] = e

        sink_cols = _sink_columns(sink_ref)
        for ci in reversed(range(n_chunks)):
            rows = slice(ci * CHUNK, (ci + 1) * CHUNK)
            sel = jnp.where(block * n_chunks + ci > 0, 1, 0)
            dk_band = jnp.zeros((2 * CHUNK, KV_WIDTH), F32)
            dv_band = jnp.zeros((2 * CHUNK, KV_WIDTH), F32)
            for kvh in range(N_KV_HEADS):
                kblk, vblk = _band_operands(ek_ref.at[ci], ek_ref.at[ci + 1], ev_ref.at[ci], ev_ref.at[ci + 1], kvh)
                qs = _stack_pairs(q_ref, rows, kvh)
                p, psink = _softmax_stack(_nt(qs, kblk) + bias_ref[sel, kvh], sink_cols[kvh])
                pb16 = p.astype(BF16)
                o = _nn(pb16, vblk)
                y = (o * _stack_pairs(sg, rows, kvh)).astype(BF16)
                dz = (o * _stack_pairs(dyz, rows, kvh)).astype(BF16)
                dob = _stack_pairs(do_all, rows, kvh).astype(BF16)
                dp = _nt(dob, vblk)
                dlogits = []
                for half in range(2):
                    hs = slice(half * 2 * CHUNK, (half + 1) * 2 * CHUNK)
                    delta = jnp.sum(dp[:, hs] * p[:, hs], axis=-1, keepdims=True)
                    dlogits.append(p[:, hs] * (dp[:, hs] - delta))
                    dsink_acc[2 * kvh + half] += psink[half] * delta
                dlog = jnp.concatenate(dlogits, axis=1)
                dl_acc[kvh] += dlog
                dsc = dlog.astype(BF16)
                dq = (_nn(dsc, kblk) * SCALE).astype(BF16)
                for pp in range(PAIRS_PER_KV):
                    blk = slice(pp * CHUNK, (pp + 1) * CHUNK)
                    cols = _pair_cols(kvh, pp)
                    y_ref[rows, cols] = y[blk]
                    dqz_ref[rows, cols] = dq[blk]
                    dqz_ref[rows, B_WIDTH + cols.start:B_WIDTH + cols.stop] = dz[blk]
                dk_band = dk_band + _fold_heads(_tn(dsc, qs), kvh)
                dv_band = dv_band + _fold_heads(_tn(pb16, dob), kvh)
            band = jnp.concatenate([dk_band, dv_band], axis=1)
            dkv_ref[rows, :] = (band[CHUNK:] + carry_ref[...]).astype(BF16)
            carry_ref[...] = band[:CHUNK]

        dqz = dqz_ref[...]
        dkv = dkv_ref[...]
        dh1 = ALPHA * dr1 + _nt(dkv, wkv_ref[...])
        for k in range(N_DEV):
            part = dqz[:, k * shard_cols:(k + 1) * shard_cols]
            dh1 = dh1 + _nt(part, wb_ref[k])
            dwb_acc[k] += _tn(h1b, part)
        dpost_ref[0:1, :] += jnp.sum(dh1 * xhat0, axis=0, keepdims=True)
        dpost_ref[1:2, :] += jnp.sum(dh1, axis=0, keepdims=True)
        dr0_ref[...] = _ln_backward(dh1 * pg_ref[0:1, :], xhat0, rstd0)
        dwkv_acc[...] += _tn(h1b, dkv)
        dwob_acc[...] += _tn(y_ref[...], dr1b)

        @pl.when(step == n_blocks - 1)
        def _():
            pltpu.sync_copy(dwb_acc, dwb_out)
            pltpu.sync_copy(dwkv_acc, dwkv_out)
            pltpu.sync_copy(dwob_acc, dwob_out)
            pltpu.sync_copy(dl_acc, dl_out)
            lane = lax.broadcasted_iota(jnp.int32, (1, CHUNK), 1)
            dsink = jnp.zeros((1, CHUNK), F32)
            for kvh in range(N_KV_HEADS):
                for half in range(2):
                    for pp in range(PAIRS_PER_KV):
                        head = 2 * (kvh * PAIRS_PER_KV + pp) + half
                        tot = jnp.sum(dsink_acc[2 * kvh + half, pp * CHUNK:(pp + 1) * CHUNK, :], axis=0, keepdims=True)
                        dsink = dsink - jnp.where(lane == head, tot, 0.0)
            dsink_ref[...] = dsink

    rev = functools.partial(_rows, tm, reverse_of=n_blocks)
    prev_chunk = pl.BlockSpec(
        (CHUNK, 2 * KV_WIDTH), lambda i: (jnp.maximum((n_blocks - 1 - i) * n_chunks - 1, 0), 0))
    return pl.pallas_call(
        body, name="bwd_b", grid=(n_blocks,),
        out_shape=[jax.ShapeDtypeStruct((seq, D_MODEL), F32), jax.ShapeDtypeStruct(wb.shape, F32),
                   jax.ShapeDtypeStruct(wkv.shape, F32), jax.ShapeDtypeStruct(wob.shape, F32),
                   jax.ShapeDtypeStruct((1, CHUNK), F32),
                   jax.ShapeDtypeStruct((N_KV_HEADS, STACK, 4 * CHUNK), F32), jax.ShapeDtypeStruct((2, D_MODEL), F32)],
        in_specs=[rev(D_MODEL), rev(D_MODEL), rev(B_WIDTH), rev(B_WIDTH), rev(2 * KV_WIDTH), prev_chunk,
                  _resident(wb.shape), _resident(wkv.shape), _resident(wob.shape), _resident(post_g.shape),
                  _resident(post_b.shape), SMEM, _resident(bias.shape)],
        out_specs=[rev(D_MODEL), ANY, ANY, ANY, pl.BlockSpec((1, CHUNK), lambda i: (0, 0)), ANY,
                   pl.BlockSpec((2, D_MODEL), lambda i: (0, 0))],
        scratch_shapes=[pltpu.VMEM(wb.shape, F32), pltpu.VMEM(wkv.shape, F32), pltpu.VMEM(wob.shape, F32),
                        pltpu.VMEM((N_KV_HEADS, STACK, 4 * CHUNK), F32), pltpu.VMEM((2 * N_KV_HEADS, STACK, 1), F32),
                        pltpu.VMEM((CHUNK, 2 * KV_WIDTH), F32),
                        pltpu.VMEM((n_chunks + 1, 4, CHUNK, CHUNK), BF16),
                        pltpu.VMEM((n_chunks + 1, 4, CHUNK, CHUNK), BF16),
                        pltpu.VMEM((tm, B_WIDTH), BF16), pltpu.VMEM((tm, 2 * B_WIDTH), BF16),
                        pltpu.VMEM((tm, 2 * KV_WIDTH), BF16)],
        compiler_params=_seq_params(),
    )(r0, dr1, q, z, kv, kv, wb, wkv, wob, post_g, post_b, sinks, bias)


def _bwd_a(dr0, p, ln_gb, w_spatial, bsp_t, woa, exchange):
    seq = dr0.shape[0]
    tm = min(TOKENS_A, seq)
    n_chunks = tm // CHUNK
    n_blocks = seq // tm

    def body(dr0_ref, p_ref, ln_ref, ws_ref, bsp_ref, woa_ref, *refs):
        ex_in, refs = refs[:exchange.n], refs[exchange.n:]
        (dp_ref, dwoa_out, dws_ref, dbsp_ref, dln_ref), refs = refs[:5], refs[5:]
        ex_out, refs = refs[:exchange.n], refs[exchange.n:]
        (dwoa_acc, s_ref, ds_ref, dvn_ref), ex_sems = refs[:4], refs[4:]
        step = pl.program_id(0)

        @pl.when(step == 0)
        def _():
            exchange.start(ex_in, ex_out, ex_sems)
            dwoa_acc[...] = jnp.zeros_like(dwoa_acc)
            dws_ref[...] = jnp.zeros_like(dws_ref)
            dbsp_ref[...] = jnp.zeros_like(dbsp_ref)
            dln_ref[...] = jnp.zeros_like(dln_ref)

        dr0b = dr0_ref[...].astype(BF16)

        vhat, rstd_v = _ln_normalize(p_ref[:, A_WIDTH:2 * A_WIDTH])
        ln_g = ln_ref[0:1, :]
        vn = (vhat * ln_g + ln_ref[1:2, :]).astype(BF16)
        _spatial_gate(ws_ref, bsp_ref, vn, s_ref, n_chunks)
        u = p_ref[:, :A_WIDTH]
        s = s_ref[...]
        sg, dsg = _silu_and_grad(p_ref[:, 2 * A_WIDTH:])
        gate = s * sg
        dwoa_acc[...] += _tn((u * gate).astype(BF16), dr0b)
        dy = _nt(dr0b, woa_ref[...])
        dyu = dy * u
        dp_ref[:, :A_WIDTH] = (dy * gate).astype(BF16)
        dp_ref[:, 2 * A_WIDTH:] = (dyu * (s * dsg)).astype(BF16)
        ds_ref[...] = dyu * sg

        lane = lax.broadcasted_iota(jnp.int32, (CHUNK, CHUNK), 1)
        dbsp = jnp.zeros((CHUNK, CHUNK), F32)
        for g in range(A_GROUPS):
            w = _masked_spatial(ws_ref, g)
            cols = slice(g * A_GROUP_DIM, (g + 1) * A_GROUP_DIM)
            dws = jnp.zeros((CHUNK, CHUNK), F32)
            for ci in range(n_chunks):
                rows = slice(ci * CHUNK, (ci + 1) * CHUNK)
                ds = ds_ref[rows, cols]
                dsb = ds.astype(BF16)
                dvn_ref[rows, cols] = _tn(w, dsb)
                dws = dws + _nt(dsb, vn[rows, cols])
                dbsp = dbsp + jnp.where(lane == g, jnp.sum(ds, axis=-1, keepdims=True), 0.0)
            dws_ref[g] += dws
        dbsp_ref[...] += dbsp

        dvn = dvn_ref[...]
        dln_ref[0:1, :] += jnp.sum(dvn * vhat, axis=0, keepdims=True)
        dln_ref[1:2, :] += jnp.sum(dvn, axis=0, keepdims=True)
        dp_ref[:, A_WIDTH:2 * A_WIDTH] = _ln_backward(dvn * ln_g, vhat, rstd_v).astype(BF16)

        @pl.when(step == n_blocks - 1)
        def _():
            t = lax.broadcasted_iota(jnp.int32, (CHUNK, CHUNK), 0)
            for g in range(A_GROUPS):
                dws_ref[g] = jnp.where(lane <= t, dws_ref[g], 0.0)
            pltpu.sync_copy(dwoa_acc, dwoa_out)
            exchange.wait(ex_in, ex_out, ex_sems)

    def fixed(shape):
        nd = len(shape)
        return pl.BlockSpec(shape, lambda i: (0,) * nd)

    return pl.pallas_call(
        body, name="bwd_a", grid=(n_blocks,),
        out_shape=[jax.ShapeDtypeStruct((seq, 3 * A_WIDTH), BF16),
                   jax.ShapeDtypeStruct(woa.shape, F32), jax.ShapeDtypeStruct(w_spatial.shape, F32),
                   jax.ShapeDtypeStruct((CHUNK, CHUNK), F32), jax.ShapeDtypeStruct((2, A_WIDTH), F32)]
        + exchange.out_shape,
        in_specs=[_rows(tm, D_MODEL), _rows(tm, 3 * A_WIDTH), _resident(ln_gb.shape),
                  _resident(w_spatial.shape), _resident(bsp_t.shape), _resident(woa.shape)] + [ANY] * exchange.n,
        out_specs=[_rows(tm, 3 * A_WIDTH), ANY, fixed(w_spatial.shape),
                   fixed((CHUNK, CHUNK)), fixed((2, A_WIDTH))] + [ANY] * exchange.n,
        scratch_shapes=[pltpu.VMEM(woa.shape, F32), pltpu.VMEM((tm, A_WIDTH), F32),
                        pltpu.VMEM((tm, A_WIDTH), F32), pltpu.VMEM((tm, A_WIDTH), F32)] + exchange.scratch_shapes,
        compiler_params=_seq_params(),
    )(dr0, p, ln_gb, w_spatial, bsp_t, woa, *exchange.inputs)


def _input_grad_blocks(seq):
    tm = min(TOKENS_DX, seq // 8)
    return tm, seq // tm


def _input_grad(dp, dr0, wa, first, n_blocks, earlier, exchange):
    seq = dr0.shape[0]
    tm, _ = _input_grad_blocks(seq)
    shard_cols = wa.shape[2]
    n_own = 3 if earlier is None else 4

    def body(dp_ref, dr0_ref, wa_ref, *refs):
        refs = refs[n_own - 3:]
        ex_in, dx_ref, ex_out, ex_sems = (refs[:exchange.n], refs[exchange.n], refs[exchange.n + 1:2 * exchange.n + 1],
                                          refs[2 * exchange.n + 1:])
        step = pl.program_id(0)

        @pl.when(step == 0)
        def _():
            exchange.start(ex_in, ex_out, ex_sems)

        dx = ALPHA * dr0_ref[...]
        for k in range(N_DEV):
            dx = dx + _nt(dp_ref[:, k * shard_cols:(k + 1) * shard_cols], wa_ref[k])
        dx_ref[...] = dx

        @pl.when(step == n_blocks - 1)
        def _():
            exchange.wait(ex_in, ex_out, ex_sems)

    def rows(width):
        return pl.BlockSpec((tm, width), lambda i: (first + i, 0))

    return pl.pallas_call(
        body, name=f"input_grad_{first}", grid=(n_blocks,),
        out_shape=[jax.ShapeDtypeStruct((seq, D_MODEL), F32)] + exchange.out_shape,
        in_specs=[rows(3 * A_WIDTH), rows(D_MODEL), _resident(wa.shape)] + [ANY] * (n_own - 3 + exchange.n),
        out_specs=[rows(D_MODEL)] + [ANY] * exchange.n,
        scratch_shapes=exchange.scratch_shapes,
        input_output_aliases={} if earlier is None else {3: 0},
        compiler_params=_seq_params(),
    )(dp, dr0, wa, *(() if earlier is None else (earlier,)), *exchange.inputs)


def _in_proj_weight_grad(xb, dp, shard_cols, exchange):
    seq = xb.shape[0]
    tk = min(TOKENS_WGRAD, seq)
    n_k = seq // tk

    def body(x_ref, dp_ref, *refs):
        ex_in, out_ref, ex_out, ex_sems = (refs[:exchange.n], refs[exchange.n], refs[exchange.n + 1:2 * exchange.n + 1],
                                           refs[2 * exchange.n + 1:])
        shard, step = pl.program_id(0), pl.program_id(1)

        @pl.when((shard == 0) & (step == 0))
        def _():
            exchange.start(ex_in, ex_out, ex_sems)

        @pl.when(step == 0)
        def _():
            out_ref[...] = jnp.zeros_like(out_ref)

        out_ref[...] += _tn(x_ref[...], dp_ref[...])

        @pl.when((shard == N_DEV - 1) & (step == n_k - 1))
        def _():
            exchange.wait(ex_in, ex_out, ex_sems)

    return pl.pallas_call(
        body, name="in_proj_weight_grad", grid=(N_DEV, n_k),
        out_shape=[jax.ShapeDtypeStruct((N_DEV, D_MODEL, shard_cols), F32)] + exchange.out_shape,
        in_specs=[pl.BlockSpec((tk, D_MODEL), lambda j, k: (k, 0)),
                  pl.BlockSpec((tk, shard_cols), lambda j, k: (k, j))] + [ANY] * exchange.n,
        out_specs=[pl.BlockSpec((None, D_MODEL, shard_cols), lambda j, k: (j, 0, 0))] + [ANY] * exchange.n,
        scratch_shapes=exchange.scratch_shapes,
        compiler_params=pltpu.CompilerParams(dimension_semantics=("arbitrary", "arbitrary"),
                                             vmem_limit_bytes=VMEM_LIMIT_BYTES),
    )(xb, dp, *exchange.inputs)


class _Exchange:
    def __init__(self, pair=(), chip=()):
        self.pair, self.chip = tuple(pair), tuple(chip)
        self.inputs = self.pair + self.chip
        self.n = len(self.inputs)
        self.out_shape = [jax.ShapeDtypeStruct((4, 1) + g.shape[2:], F32) for g in self.pair]
        self.out_shape += [jax.ShapeDtypeStruct((3,) + s.shape[1:], s.dtype) for s in self.chip]
        self.scratch_shapes = [pltpu.SemaphoreType.DMA((max(len(self.pair), 1),)) for _ in range(2)]
        self.scratch_shapes += [pltpu.SemaphoreType.DMA((max(len(self.chip), 1), 3)) for _ in range(2)]

    def _copies(self, ins, outs, sems):
        if not self.n:
            return []
        pair_send, pair_recv, chip_send, chip_recv = sems
        x, y, c = lax.axis_index("x"), lax.axis_index("y"), lax.axis_index("c")
        copies = []
        for a in range(len(self.pair)):
            copies.append(pltpu.make_async_remote_copy(
                src_ref=ins[a].at[:, pl.ds(1 - c, 1)], dst_ref=outs[a],
                send_sem=pair_send.at[a], recv_sem=pair_recv.at[a], device_id=(x, y, 1 - c), device_id_type=MESH))
        for a in range(len(self.chip)):
            src, dst = ins[len(self.pair) + a], outs[len(self.pair) + a]
            for k, (px, py) in enumerate([(1 - x, y), (x, 1 - y), (1 - x, 1 - y)]):
                copies.append(pltpu.make_async_remote_copy(
                    src_ref=src.at[2 * px + py], dst_ref=dst.at[k], send_sem=chip_send.at[a, k],
                    recv_sem=chip_recv.at[a, k], device_id=(px, py, c), device_id_type=MESH))
        return copies

    def start(self, ins, outs, sems):
        for cp in self._copies(ins, outs, sems):
            cp.start()

    def wait(self, ins, outs, sems):
        for cp in self._copies(ins, outs, sems):
            cp.wait()


def _row_tile(rows, cols, itemsize=4):
    if rows * cols * itemsize <= (1 << 20):
        return rows
    t = 8
    while 2 * t * cols * itemsize <= (1 << 20) and rows % (2 * t) == 0:
        t *= 2
    assert rows % t == 0, (rows, cols)
    return t


def _pair_sum(place, grads, landeds, dtypes):
    n = len(grads)

    def body(place_ref, *refs):
        for a in range(n):
            refs[2 * n + a][...] = (refs[2 * a][...] + refs[2 * a + 1][...]).astype(dtypes[a])

    in_specs, out_specs, operands = [], [], []
    for g, l in zip(grads, landeds):
        rows, cols = g.shape[2:]
        in_specs += [pl.BlockSpec((None, None, rows, cols), lambda k, pr: (k, pr[1], 0, 0)),
                     pl.BlockSpec((None, None, rows, cols), lambda k, pr: (k, 0, 0, 0))]
        out_specs.append(pl.BlockSpec((None, rows, cols), lambda k, pr: (k, 0, 0)))
        operands += [g, l]
    return pl.pallas_call(
        body, name="pair_sum",
        out_shape=[jax.ShapeDtypeStruct((4,) + g.shape[2:], dt) for g, dt in zip(grads, dtypes)],
        grid_spec=pltpu.PrefetchScalarGridSpec(num_scalar_prefetch=1, grid=(4,), in_specs=in_specs,
                                               out_specs=out_specs),
        compiler_params=pltpu.CompilerParams(dimension_semantics=("parallel",), vmem_limit_bytes=VMEM_LIMIT_BYTES),
    )(place, *operands)


def _adamw_math(w, g, m, v):
    m = ADAM_B1 * m + (1.0 - ADAM_B1) * g
    v = ADAM_B2 * v + (1.0 - ADAM_B2) * (g * g)
    m_hat = m / (1.0 - ADAM_B1 ** ADAM_STEP)
    v_hat = v / (1.0 - ADAM_B2 ** ADAM_STEP)
    delta = -ADAM_LR * (m_hat / (jnp.sqrt(v_hat) + ADAM_EPS) + ADAM_WD * w)
    return delta, m, v


def _adamw_shard(place, grad, landed, got, w, m, v):
    rows, cols = w.shape
    tr = _row_tile(rows, cols)

    def body(place_ref, a_ref, b_ref, got_ref, w_ref, m_ref, v_ref, g_out, d_out, m_out, v_out):
        g = a_ref[...] + b_ref[...]
        for k in range(3):
            g = g + got_ref[k].astype(F32)
        g_out[...] = g
        d_out[...], m_out[...], v_out[...] = _adamw_math(w_ref[...], g, m_ref[...], v_ref[...])

    spec = pl.BlockSpec((tr, cols), lambda i, pr: (i, 0))
    out = jax.ShapeDtypeStruct((rows, cols), F32)
    return pl.pallas_call(
        body, name="adamw_shard", out_shape=[out] * 4,
        grid_spec=pltpu.PrefetchScalarGridSpec(
            num_scalar_prefetch=1, grid=(rows // tr,),
            in_specs=[pl.BlockSpec((None, None, tr, cols), lambda i, pr: (pr[0], pr[1], i, 0)),
                      pl.BlockSpec((None, None, tr, cols), lambda i, pr: (pr[0], 0, i, 0)),
                      pl.BlockSpec((3, tr, cols), lambda i, pr: (0, i, 0)), spec, spec, spec],
            out_specs=[spec] * 4),
        compiler_params=pltpu.CompilerParams(dimension_semantics=("parallel",)),
    )(place, grad, landed, got, w, m, v)


def _gather_small(grad, landed, got):
    rows, cols = got.shape[1:]

    def body(grad_ref, landed_ref, got_ref, out_ref, send_sems, recv_sems):
        x, y, c = lax.axis_index("x"), lax.axis_index("y"), lax.axis_index("c")
        me, sibling = (x, y, c), (x, y, 1 - c)
        chips = [(1 - x, y), (x, 1 - y), (1 - x, 1 - y)]

        def slot(px, py, pc):
            return out_ref.at[4 * px + 2 * py + pc]

        def copy(k, block, to):
            return pltpu.make_async_remote_copy(
                src_ref=slot(*block), dst_ref=slot(*block), send_sem=send_sems.at[k], recv_sem=recv_sems.at[k],
                device_id=to, device_id_type=MESH)

        mine = grad_ref[2 * x + y, c] + landed_ref[2 * x + y, 0]
        out_ref[4 * x + 2 * y + c] = ((mine + got_ref[0]) + got_ref[1]) + got_ref[2]
        first = [copy(0, me, sibling)] + [copy(1 + j, me, (*chip, c)) for j, chip in enumerate(chips)]
        for cp in first:
            cp.start()
        passed = [copy(4 + j, (*chip, c), sibling) for j, chip in enumerate(chips)]
        for j, chip in enumerate(chips):
            copy(1 + j, (*chip, c), me).wait_recv()
            passed[j].start()
        copy(0, sibling, me).wait_recv()
        for j, chip in enumerate(chips):
            copy(4 + j, (*chip, 1 - c), me).wait_recv()
        for cp in first + passed:
            cp.wait_send()

    return pl.pallas_call(
        body, name="gather_small", out_shape=jax.ShapeDtypeStruct((N_DEV, rows, cols), F32),
        in_specs=[VMEM] * 3, out_specs=VMEM,
        scratch_shapes=[pltpu.SemaphoreType.DMA((7,)), pltpu.SemaphoreType.DMA((7,))],
    )(grad, landed, got)


def _adamw_small(params):
    n = len(params)

    def body(*refs):
        ins, outs = refs[:4 * n], refs[4 * n:]
        for i in range(n):
            w_ref, g_ref, m_ref, v_ref = ins[4 * i:4 * i + 4]
            d, m, v = _adamw_math(w_ref[...], g_ref[...], m_ref[...], v_ref[...])
            outs[3 * i][...] = d
            outs[3 * i + 1][...] = m
            outs[3 * i + 2][...] = v

    flat = [a for p in params for a in p]
    out_shape = [jax.ShapeDtypeStruct(p[0].shape, F32) for p in params for _ in range(3)]
    outs = pl.pallas_call(
        body, name="adamw_small", out_shape=out_shape, in_specs=[VMEM] * (4 * n), out_specs=[VMEM] * (3 * n),
    )(*flat)
    return [tuple(outs[3 * i:3 * i + 3]) for i in range(n)]


def _pack_small(parts):
    rows = []
    for a in parts:
        r, c = a.shape
        rows.append(jnp.pad(a, ((0, (-r) % 8), (0, CHUNK - c))))
    flat = jnp.concatenate(rows, axis=0)
    return jnp.pad(flat, ((0, N_DEV * SMALL_ROWS - flat.shape[0]), (0, 0))).reshape(N_DEV, SMALL_ROWS, CHUNK)


def _unpack_small(flat, shapes):
    out, at = [], 0
    for r, c in shapes:
        out.append(flat[at:at + r, :c])
        at += r + (-r) % 8
    return out


def kernel(x, w_in_a, sgu_ln_g, sgu_ln_b, w_spatial, b_spatial, w_out_a, w_kv, w_in_b, attn_sinks, rel_bias, w_out_b, post_ln_g, post_ln_b, loss_target, m_w_in_a, m_sgu_ln_g, m_sgu_ln_b, m_w_spatial, m_b_spatial, m_w_out_a, m_w_kv, m_w_in_b, m_attn_sinks, m_rel_bias, m_w_out_b, m_post_ln_g, m_post_ln_b, v_w_in_a, v_sgu_ln_g, v_sgu_ln_b, v_w_spatial, v_b_spatial, v_w_out_a, v_w_kv, v_w_in_b, v_attn_sinks, v_rel_bias, v_w_out_b, v_post_ln_g, v_post_ln_b):
    seq = x.shape[1]
    me = 4 * lax.axis_index("x") + 2 * lax.axis_index("y") + lax.axis_index("c")
    bucket = jnp.asarray(_band_tables())

    ln_shard = jnp.concatenate([sgu_ln_g, sgu_ln_b], axis=0)
    x2, tgt = x[0], loss_target[0]
    order = jnp.stack([2 * px + py for px, py, _ in _use_order()[::2]]).astype(jnp.int32)
    p, wa, woa, ln_all, xb = _gather_in_proj(order, x2, w_in_a[0], (w_out_a[0],), ln_shard)
    woa = woa.reshape(A_WIDTH, D_MODEL)
    ln_gb = jnp.transpose(ln_all, (1, 0, 2)).reshape(2, A_WIDTH)
    ws = w_spatial[0]
    bsp_t = jnp.pad(b_spatial[0].T, ((0, 0), (0, CHUNK - A_GROUPS)))
    bias = _bias_table(rel_bias, bucket)

    r0, wkv, wb, wob = _fwd_a(x2, p, ln_gb, ws, bsp_t, woa, (w_kv, w_in_b[0], w_out_b[0]))
    wkv = wkv.reshape(D_MODEL, 2 * KV_WIDTH)
    wob = wob.reshape(B_WIDTH, D_MODEL)
    q, z, kv, dr1, loss, dg1, db1 = _fwd_b(r0, tgt, wb, wkv, wob, post_ln_g, post_ln_b, attn_sinks, bias)
    dr0, dwb, dwkv, dwob, dsink, dlog, dpost0 = _bwd_b(r0, dr1, q, z, kv, wb, wkv, wob, post_ln_g, post_ln_b,
                                                       attn_sinks, bias)
    drel = _rel_bias_grad(dlog, bucket)

    place = jnp.stack([2 * lax.axis_index("x") + lax.axis_index("y"), lax.axis_index("c")]).astype(jnp.int32)

    def by_device(g):
        return g.reshape((4, 2, g.shape[0] // N_DEV) + g.shape[1:]) if g.ndim == 2 else g.reshape((4, 2) + g.shape[1:])

    dwb, dwkv, dwob = by_device(dwb), by_device(dwkv), by_device(dwob)
    dp, dwoa, dws, dbsp_t, dln, l_wb, l_wkv, l_wob = _bwd_a(dr0, p, ln_gb, ws, bsp_t, woa,
                                                            _Exchange(pair=(dwb, dwkv, dwob)))
    dwoa = by_device(dwoa)
    small_shapes = [(A_GROUPS * CHUNK, CHUNK), (A_GROUPS, CHUNK), (1, N_Q_HEADS), (REL_BUCKETS, N_Q_HEADS),
                    (16, CHUNK), (16, CHUNK), (16, CHUNK), (16, CHUNK), (1, 1)]
    small = by_device(_pack_small([
        dws.reshape(A_GROUPS * CHUNK, CHUNK), dbsp_t[:, :A_GROUPS].T, dsink[:, :N_Q_HEADS], drel[:, :N_Q_HEADS],
        jnp.concatenate([dpost0[0:1], dg1], axis=0).reshape(16, CHUNK),
        jnp.concatenate([dpost0[1:2], db1], axis=0).reshape(16, CHUNK),
        dln[0].reshape(16, CHUNK), dln[1].reshape(16, CHUNK), loss]))
    s_wb, s_wkv, s_wob = _pair_sum(place, (dwb, dwkv, dwob), (l_wb, l_wkv, l_wob), (BF16,) * 3)
    dwa, l_woa, l_small, g_wb, g_wkv, g_wob = _in_proj_weight_grad(
        xb, dp, wa.shape[2], _Exchange(pair=(dwoa, small), chip=(s_wb, s_wkv, s_wob)))
    dwa = by_device(dwa)
    s_woa, s_small = _pair_sum(place, (dwoa, small), (l_woa, l_small), (BF16, F32))
    _, blocks = _input_grad_blocks(seq)
    cut = blocks // 4
    dx, l_wa, g_woa, g_small = _input_grad(dp, dr0, wa, 0, cut, None, _Exchange(pair=(dwa,), chip=(s_woa, s_small)))
    s_wa, = _pair_sum(place, (dwa,), (l_wa,), (BF16,))
    dx, g_wa = _input_grad(dp, dr0, wa, cut, blocks - cut, dx, _Exchange(chip=(s_wa,)))
    full = [dwa, dwoa, dwkv, dwb, dwob, small]
    landed = [l_wa, l_woa, l_wkv, l_wb, l_wob, l_small]
    got = [g_wa, g_woa, g_wkv, g_wb, g_wob, g_small]

    big = [(w_in_a, m_w_in_a, v_w_in_a), (w_out_a, m_w_out_a, v_w_out_a), (w_kv, m_w_kv, v_w_kv),
           (w_in_b, m_w_in_b, v_w_in_b), (w_out_b, m_w_out_b, v_w_out_b)]
    big_out = []
    for i, (w, m, v) in enumerate(big):
        shape2 = got[i].shape[1:]
        res = _adamw_shard(place, full[i], landed[i], got[i], w.reshape(shape2), m.reshape(shape2),
                           v.reshape(shape2))
        big_out.append([r.reshape(w.shape) for r in res])

    small_all = _gather_small(full[5], landed[5], got[5]).reshape(N_DEV * SMALL_ROWS, CHUNK)
    g_ws, g_bsp, g_sink, g_rel, g_pg, g_pb, g_lng, g_lnb, loss_all = _unpack_small(small_all, small_shapes)
    g_ws = g_ws.reshape(w_spatial.shape)
    g_bsp = g_bsp.reshape(b_spatial.shape)
    g_pg = g_pg.reshape(post_ln_g.shape)
    g_pb = g_pb.reshape(post_ln_b.shape)
    g_lng = lax.dynamic_slice(g_lng.reshape(1, A_WIDTH), (0, me * sgu_ln_g.shape[1]), sgu_ln_g.shape)
    g_lnb = lax.dynamic_slice(g_lnb.reshape(1, A_WIDTH), (0, me * sgu_ln_b.shape[1]), sgu_ln_b.shape)
    small_params = [(sgu_ln_g, g_lng, m_sgu_ln_g, v_sgu_ln_g), (sgu_ln_b, g_lnb, m_sgu_ln_b, v_sgu_ln_b),
                    (w_spatial, g_ws, m_w_spatial, v_w_spatial), (b_spatial, g_bsp, m_b_spatial, v_b_spatial),
                    (attn_sinks, g_sink, m_attn_sinks, v_attn_sinks), (rel_bias, g_rel, m_rel_bias, v_rel_bias),
                    (post_ln_g, g_pg, m_post_ln_g, v_post_ln_g), (post_ln_b, g_pb, m_post_ln_b, v_post_ln_b)]
    flat2 = [tuple(a.reshape(-1, a.shape[-1]) for a in p) for p in small_params]
    small_out = [tuple(r.reshape(p[0].shape) for r in res) for res, p in zip(_adamw_small(flat2), small_params)]

    order = [("big", 0), ("small", 0), ("small", 1), ("small", 2), ("small", 3), ("big", 1), ("big", 2), ("big", 3),
             ("small", 4), ("small", 5), ("big", 4), ("small", 6), ("small", 7)]
    grads, deltas, new_m, new_v = [], [], [], []
    for kind, i in order:
        if kind == "big":
            g, d, m, v = big_out[i]
        else:
            g = small_params[i][1]
            d, m, v = small_out[i]
        grads.append(g)
        deltas.append(d)
        new_m.append(m)
        new_v.append(v)
    return (loss_all[0, 0], dx.reshape(x.shape), *grads, *deltas, *new_m, *new_v)
```

```python
import functools

import jax
import jax.numpy as jnp
import numpy as np
from jax import lax
from jax.experimental import pallas as pl
from jax.experimental.pallas import tpu as pltpu

F32 = jnp.float32
BF16 = jnp.bfloat16

D_MODEL = 1024
A_WIDTH = 2048
A_GROUPS = 8
A_GROUP_DIM = A_WIDTH // A_GROUPS
CHUNK = 128
HEAD_DIM = 64
N_Q_HEADS = 16
N_KV_HEADS = 2
PAIRS_PER_KV = N_Q_HEADS // N_KV_HEADS // 2
B_WIDTH = N_Q_HEADS * HEAD_DIM
KV_WIDTH = N_KV_HEADS * HEAD_DIM
REL_BUCKETS = 32
REL_MAX_DIST = 128
DEPTH = 2
ALPHA = (2.0 * DEPTH) ** 0.25
LN_EPS = 1e-5
NEG_INF = -1e30
SCALE = HEAD_DIM ** -0.5

ADAM_LR = 0.001
ADAM_B1 = 0.9
ADAM_B2 = 0.999
ADAM_EPS = 1e-08
ADAM_WD = 0.01
ADAM_STEP = 10

N_DEV = 8
MESH = pl.DeviceIdType.MESH
TOKENS_IN_PROJ = 1024
TOKENS_WGRAD = 4096
TOKENS_A = 256
TOKENS_A_FWD = 256
P_RING = 3
TOKENS_B = 512
PART_ROWS_B = 256
TOKENS_B_BWD = 256
TOKENS_DX = 512
VMEM_LIMIT_BYTES = 60 * 1024 * 1024
SMALL_ROWS = 144

ANY = pl.BlockSpec(memory_space=pl.ANY)
VMEM = pl.BlockSpec(memory_space=pltpu.VMEM)
SMEM = pl.BlockSpec(memory_space=pltpu.SMEM)


def _nn(a, b):
    return jnp.dot(a, b, preferred_element_type=F32)


def _nt(a, b):
    return lax.dot_general(a, b, (((1,), (1,)), ((), ())), preferred_element_type=F32)


def _tn(a, b):
    return lax.dot_general(a, b, (((0,), (0,)), ((), ())), preferred_element_type=F32)


def _resident(shape):
    nd = len(shape)
    return pl.BlockSpec(shape, lambda i: (0,) * nd, pipeline_mode=pl.Buffered(1))


def _rows(tm, width, reverse_of=None):
    if reverse_of is None:
        return pl.BlockSpec((tm, width), lambda i: (i, 0))
    return pl.BlockSpec((tm, width), lambda i: (reverse_of - 1 - i, 0))


def _seq_params():
    return pltpu.CompilerParams(dimension_semantics=("arbitrary",), vmem_limit_bytes=VMEM_LIMIT_BYTES)


def _in_turns(first, second):
    merged = [(i / max(len(first), 1), 0, f) for i, f in enumerate(first)]
    merged += [((i + 0.5) / max(len(second), 1), 1, f) for i, f in enumerate(second)]
    return [f for _, _, f in sorted(merged, key=lambda t: t[:2])]


def _ln_normalize(r):
    mu = jnp.mean(r, axis=-1, keepdims=True)
    xc = r - mu
    var = jnp.mean(xc * xc, axis=-1, keepdims=True)
    rstd = lax.rsqrt(var + LN_EPS)
    return xc * rstd, rstd


def _ln_backward(dxhat, xhat, rstd):
    m1 = jnp.mean(dxhat, axis=-1, keepdims=True)
    m2 = jnp.mean(dxhat * xhat, axis=-1, keepdims=True)
    return rstd * (dxhat - m1 - xhat * m2)


def _silu_and_grad(z):
    sig = jax.nn.sigmoid(z)
    silu = z * sig
    return silu, sig + silu * (1.0 - sig)


def _band_tables():
    t = np.arange(CHUNK, dtype=np.int32)[:, None]
    j = np.arange(2 * CHUNK, dtype=np.int32)[None, :]
    d = t + CHUNK - j
    in_window = (d >= 0) & (d < CHUNK)
    dc = np.clip(d, 0, REL_MAX_DIST - 1)
    max_exact = REL_BUCKETS // 2
    df = np.maximum(dc, 1).astype(np.float32)
    large = max_exact + (np.log(df / np.float32(max_exact)) / np.float32(np.log(REL_MAX_DIST / max_exact))
                         * np.float32(REL_BUCKETS - max_exact)).astype(np.int32)
    large = np.minimum(large, REL_BUCKETS - 1)
    bucket = np.where(dc < max_exact, dc, large)
    return np.where(in_window, bucket, -1).astype(np.int32)


def _use_order():
    x, y, c = lax.axis_index("x"), lax.axis_index("y"), lax.axis_index("c")
    blocks = [(x, y, c), (x, y, 1 - c)]
    for px, py in [(1 - x, y), (x, 1 - y), (1 - x, 1 - y)]:
        blocks += [(px, py, c), (px, py, 1 - c)]
    return blocks


def _gather_in_proj(order, x, w_in_a, small, ln_gb):
    seq = x.shape[0]
    tm = min(TOKENS_IN_PROJ, seq)
    n_blocks = seq // tm
    shard_cols = w_in_a.shape[1]
    n = len(small) + 2

    def body(order_ref, x_ref, *refs):
        ins, refs = refs[:n], refs[n:]
        p_ref, outs, xb_out, refs = refs[0], refs[1:n + 1], refs[n + 1], refs[n + 2:]
        wa_vmem, xb_vmem, stage = refs[0], refs[1], refs[2:n]
        send_sems, recv_sems, local_sems, out_sems, xb_sem = refs[n:]
        xb_copy = pltpu.make_async_copy(xb_vmem, xb_out, xb_sem)
        phase, blk = pl.program_id(0), pl.program_id(1)
        x_, y_, c_ = lax.axis_index("x"), lax.axis_index("y"), lax.axis_index("c")
        me, sibling = (x_, y_, c_), (x_, y_, 1 - c_)
        chips = [(1 - x_, y_), (x_, 1 - y_), (1 - x_, 1 - y_)]
        use = _use_order()

        def index(px, py, pc):
            return 4 * px + 2 * py + pc

        def slot(a, block):
            return (wa_vmem if a == 0 else outs[a]).at[index(*block)]

        def own(a):
            return slot(0, me) if a == 0 else (ins[a] if a == n - 1 else stage[a - 1])

        def copy(a, k, block, to, src=None):
            return pltpu.make_async_remote_copy(
                src_ref=slot(a, block) if src is None else src, dst_ref=slot(a, block),
                send_sem=send_sems.at[a, k], recv_sem=recv_sems.at[a, k], device_id=to, device_id_type=MESH)

        def keep(a):
            return pltpu.make_async_copy(own(a), slot(a, me), local_sems.at[a])

        def write_out(k):
            return pltpu.make_async_copy(slot(0, use[k]), outs[0].at[index(*use[k])], out_sems.at[k])

        def pass_on(a, j):
            copy(a, 1 + j, (*chips[j], c_), me).wait_recv()
            copy(a, 4 + j, (*chips[j], c_), sibling).start()

        def send(a, j):
            copy(a, 1 + j, me, (*chips[j], c_), own(a)).start()

        @pl.when((phase == 0) & (blk == 0))
        def _():
            wa_vmem[index(*me)] = ins[0][...].astype(BF16)
            for a in range(1, n - 1):
                stage[a - 1][...] = ins[a][...].astype(BF16)
            send(0, 0)
            send(0, 1)
            for a in range(n):
                copy(a, 0, me, sibling, own(a)).start()
                if a:
                    keep(a).start()
            write_out(0).start()
            copy(0, 0, sibling, me).wait_recv()
            write_out(1).start()

        for j in range(3):
            @pl.when((phase == 1 + j) & (blk == 0))
            def _(j=j):
                pass_on(0, j)
                write_out(2 + 2 * j).start()
                if j == 0:
                    xb_copy.start()
                    send(0, 2)
                    for a in range(1, n):
                        for jj in range(3):
                            send(a, jj)
                copy(0, 4 + j, (*chips[j], 1 - c_), me).wait_recv()
                write_out(3 + 2 * j).start()
                if j == 2:
                    for a in range(1, n):
                        pass_on(a, 0)
                        pass_on(a, 1)

        rows = pl.ds(pl.multiple_of(blk * tm, tm), tm)

        @pl.when(phase == 0)
        def _():
            xb_vmem[rows, :] = x_ref[...].astype(BF16)

        chip = order_ref[phase]
        xb = xb_vmem[rows, :]
        p_ref[:, :shard_cols] = _nn(xb, wa_vmem[2 * chip])
        p_ref[:, shard_cols:] = _nn(xb, wa_vmem[2 * chip + 1])

        @pl.when((phase == N_DEV // 2 - 1) & (blk == n_blocks - 1))
        def _():
            for a in range(1, n):
                pass_on(a, 2)
            for a in range(1, n):
                copy(a, 0, sibling, me).wait_recv()
                for j, chip in enumerate(chips):
                    copy(a, 4 + j, (*chip, 1 - c_), me).wait_recv()
            for a in range(n):
                for k in range(7):
                    copy(a, k, me, me).wait_send()
                if a:
                    keep(a).wait()
            for k in range(N_DEV):
                write_out(k).wait()
            xb_copy.wait()

    gathered = [jax.ShapeDtypeStruct((N_DEV,) + w_in_a.shape, BF16)]
    gathered += [jax.ShapeDtypeStruct((N_DEV,) + s.shape, BF16) for s in small]
    gathered.append(jax.ShapeDtypeStruct((N_DEV,) + ln_gb.shape, F32))
    return pl.pallas_call(
        body, name="gather_in_proj",
        out_shape=[jax.ShapeDtypeStruct((seq, N_DEV * shard_cols), F32)] + gathered
        + [jax.ShapeDtypeStruct((seq, D_MODEL), BF16)],
        grid_spec=pltpu.PrefetchScalarGridSpec(
            num_scalar_prefetch=1, grid=(N_DEV // 2, n_blocks),
            in_specs=[pl.BlockSpec((tm, D_MODEL), lambda k, i, order: (jnp.where(k == 0, i, n_blocks - 1), 0))]
            + [VMEM] * n,
            out_specs=[pl.BlockSpec((tm, 2 * shard_cols), lambda k, i, order: (i, order[k]))] + [ANY] * (n + 1),
            scratch_shapes=[pltpu.VMEM((N_DEV,) + w_in_a.shape, BF16), pltpu.VMEM((seq, D_MODEL), BF16)]
            + [pltpu.VMEM(s.shape, BF16) for s in small] + [
                pltpu.SemaphoreType.DMA((n, 7)), pltpu.SemaphoreType.DMA((n, 7)), pltpu.SemaphoreType.DMA((n,)),
                pltpu.SemaphoreType.DMA((N_DEV,)), pltpu.SemaphoreType.DMA(())]),
        compiler_params=pltpu.CompilerParams(dimension_semantics=("arbitrary", "arbitrary"),
                                             vmem_limit_bytes=VMEM_LIMIT_BYTES),
    )(order, x, w_in_a, *small, ln_gb)


def _bias_table(rel_bias, bucket):
    def body(rel_ref, bucket_ref, out_ref):
        first = jnp.broadcast_to(bucket_ref[0:1, :], (8, 2 * CHUNK))
        col = lax.broadcasted_iota(jnp.int32, (CHUNK, 2 * CHUNK), 1)
        for h in range(N_Q_HEADS):
            row = jnp.full(first.shape, NEG_INF, F32)
            for b in range(REL_BUCKETS):
                row = jnp.where(first == b, rel_ref[b, h], row)
            tab = pltpu.roll(jnp.broadcast_to(row[0:1, :], (CHUNK, 2 * CHUNK)), 0, 1, stride=1, stride_axis=0)
            pair = h // 2
            rows = slice((pair % PAIRS_PER_KV) * CHUNK, (pair % PAIRS_PER_KV + 1) * CHUNK)
            half = slice((h % 2) * 2 * CHUNK, (h % 2 + 1) * 2 * CHUNK)
            out_ref[1, pair // PAIRS_PER_KV, rows, half] = tab
            out_ref[0, pair // PAIRS_PER_KV, rows, half] = jnp.where(col < CHUNK, NEG_INF, tab)

    return pl.pallas_call(
        body, name="bias_table",
        out_shape=jax.ShapeDtypeStruct((2, N_KV_HEADS, PAIRS_PER_KV * CHUNK, 4 * CHUNK), F32),
        in_specs=[SMEM, VMEM], out_specs=VMEM,
    )(rel_bias, bucket)


def _rel_bias_grad(dlogits, bucket):
    def body(dl_ref, bucket_ref, out_ref):
        i = lax.broadcasted_iota(jnp.int32, (CHUNK, CHUNK), 0)
        k = lax.broadcasted_iota(jnp.int32, (CHUNK, CHUNK), 1)
        reverse = jnp.where(i + k == CHUNK - 1, 1.0, 0.0)
        sums = []
        for h in range(N_Q_HEADS):
            pair = h // 2
            dl = dl_ref[pair // PAIRS_PER_KV, (pair % PAIRS_PER_KV) * CHUNK:(pair % PAIRS_PER_KV + 1) * CHUNK,
                        (h % 2) * 2 * CHUNK:(h % 2 + 1) * 2 * CHUNK]
            backwards = jnp.dot(reverse, dl, precision=lax.Precision.HIGHEST, preferred_element_type=F32)
            turned = pltpu.roll(backwards, CHUNK + 1, 1, stride=1, stride_axis=0)
            sums.append(jnp.sum(turned, axis=0, keepdims=True))
        by_offset = jnp.concatenate(sums, axis=0)
        buckets = lax.broadcasted_iota(jnp.int32, (REL_BUCKETS, 2 * CHUNK), 0)
        one_hot = jnp.where(jnp.broadcast_to(bucket_ref[0:1, :], buckets.shape) == buckets, 1.0, 0.0)
        out_ref[...] = lax.dot_general(one_hot, by_offset, (((1,), (1,)), ((), ())),
                                       precision=lax.Precision.HIGHEST, preferred_element_type=F32)

    return pl.pallas_call(
        body, name="rel_bias_grad", out_shape=jax.ShapeDtypeStruct((REL_BUCKETS, N_Q_HEADS), F32),
        in_specs=[VMEM, VMEM], out_specs=VMEM,
    )(dlogits, bucket)


def _masked_spatial(ws_ref, g):
    t = lax.broadcasted_iota(jnp.int32, (CHUNK, CHUNK), 0)
    s = lax.broadcasted_iota(jnp.int32, (CHUNK, CHUNK), 1)
    return jnp.where(s <= t, ws_ref[g], 0.0).astype(BF16)


def _spatial_gate(ws_ref, bsp_t_ref, vn, s_ref, n_chunks):
    for g in range(A_GROUPS):
        w = _masked_spatial(ws_ref, g)
        cols = slice(g * A_GROUP_DIM, (g + 1) * A_GROUP_DIM)
        for ci in range(n_chunks):
            rows = slice(ci * CHUNK, (ci + 1) * CHUNK)
            s_ref[rows, cols] = _nn(w, vn[rows, cols]) + bsp_t_ref[:, g:g + 1]


def _fwd_a(x, p, ln_gb, w_spatial, bsp_t, woa, later):
    seq = x.shape[0]
    tm = min(TOKENS_A_FWD, seq)
    n_chunks = tm // CHUNK
    n_blocks = seq // tm
    n = len(later)

    def body(x_ref, p_hbm, ln_ref, ws_ref, bsp_ref, woa_ref, *refs):
        shards, refs = refs[:n], refs[n:]
        r0_ref, gathered, refs = refs[0], refs[1:1 + n], refs[1 + n:]
        s_ref, p_ring, stage, (send_sems, recv_sems, local_sems, ring_sems) = refs[0], refs[1], refs[2:2 + n], refs[2 + n:]
        step = pl.program_id(0)

        def fetch(block):
            slot = block % P_RING
            start = block * tm if isinstance(block, int) else pl.multiple_of(block * tm, tm)
            return pltpu.make_async_copy(p_hbm.at[pl.ds(start, tm), :], p_ring.at[slot], ring_sems.at[slot])

        @pl.when(step == 0)
        def _():
            for ahead in range(min(P_RING - 1, n_blocks)):
                fetch(ahead).start()

        @pl.when(step + P_RING - 1 < n_blocks)
        def _():
            fetch(step + P_RING - 1).start()

        fetch(step).wait()
        p_ref = p_ring.at[step % P_RING]

        def copies(a):
            x_, y_, c_ = lax.axis_index("x"), lax.axis_index("y"), lax.axis_index("c")
            mine = 4 * x_ + 2 * y_ + c_
            out = [pltpu.make_async_copy(stage[a], gathered[a].at[mine], local_sems.at[a])]
            for r in range(1, N_DEV):
                to = (1 - x_ if r & 4 else x_, 1 - y_ if r & 2 else y_, 1 - c_ if r & 1 else c_)
                out.append(pltpu.make_async_remote_copy(
                    src_ref=stage[a], dst_ref=gathered[a].at[mine], send_sem=send_sems.at[a, r - 1],
                    recv_sem=recv_sems.at[a, r - 1], device_id=to, device_id_type=MESH))
            return out

        @pl.when(step == 0)
        def _():
            for a in range(n):
                stage[a][...] = shards[a][...].astype(BF16)
                for cp in copies(a):
                    cp.start()

        xv = x_ref[...]
        vhat, _ = _ln_normalize(p_ref[:, A_WIDTH:2 * A_WIDTH])
        vn = (vhat * ln_ref[0:1, :] + ln_ref[1:2, :]).astype(BF16)
        _spatial_gate(ws_ref, bsp_ref, vn, s_ref, n_chunks)
        z = p_ref[:, 2 * A_WIDTH:]
        y = p_ref[:, :A_WIDTH] * s_ref[...] * (z * jax.nn.sigmoid(z))
        r0_ref[...] = ALPHA * xv + _nn(y.astype(BF16), woa_ref[...])

        @pl.when(step == n_blocks - 1)
        def _():
            for a in range(n):
                for cp in copies(a):
                    cp.wait()

    return pl.pallas_call(
        body, name="fwd_a", grid=(n_blocks,),
        out_shape=[jax.ShapeDtypeStruct((seq, D_MODEL), F32)]
        + [jax.ShapeDtypeStruct((N_DEV,) + s.shape, BF16) for s in later],
        in_specs=[_rows(tm, D_MODEL), ANY, _resident(ln_gb.shape), _resident(w_spatial.shape),
                  _resident(bsp_t.shape), _resident(woa.shape)] + [VMEM] * n,
        out_specs=[_rows(tm, D_MODEL)] + [ANY] * n,
        scratch_shapes=[pltpu.VMEM((tm, A_WIDTH), F32), pltpu.VMEM((P_RING, tm, 3 * A_WIDTH), F32)]
        + [pltpu.VMEM(s.shape, BF16) for s in later] + [
            pltpu.SemaphoreType.DMA((n, N_DEV - 1)), pltpu.SemaphoreType.DMA((n, N_DEV - 1)),
            pltpu.SemaphoreType.DMA((n,)), pltpu.SemaphoreType.DMA((P_RING,))],
        compiler_params=_seq_params(),
    )(x, p, ln_gb, w_spatial, bsp_t, woa, *later)


def _expand_heads(kv):
    lo = lax.broadcasted_iota(jnp.int32, kv.shape, 1) < HEAD_DIM
    swapped = pltpu.roll(kv, HEAD_DIM, 1)
    zero = jnp.zeros_like(kv)
    return (jnp.where(lo, kv, zero).astype(BF16), jnp.where(lo, zero, swapped).astype(BF16),
            jnp.where(lo, swapped, zero).astype(BF16), jnp.where(lo, zero, kv).astype(BF16))


def _fold_heads(acc, kvh):
    top, bot = acc[:2 * CHUNK], acc[2 * CHUNK:]
    lo = lax.broadcasted_iota(jnp.int32, top.shape, 1) < HEAD_DIM
    t = jnp.where(lo, top, 0.0)
    b = jnp.where(lo, 0.0, bot)
    if kvh == 0:
        return t + pltpu.roll(b, HEAD_DIM, 1)
    return pltpu.roll(t, HEAD_DIM, 1) + b


STACK = PAIRS_PER_KV * CHUNK


def _pair_cols(kvh, pp):
    pair = kvh * PAIRS_PER_KV + pp
    return slice(pair * CHUNK, (pair + 1) * CHUNK)


def _stack_pairs(ref_or_val, rows, kvh):
    return jnp.concatenate([ref_or_val[rows, _pair_cols(kvh, pp)] for pp in range(PAIRS_PER_KV)], axis=0)


def _sink_columns(sink_ref):
    blk = lax.broadcasted_iota(jnp.int32, (STACK, 1), 0) // CHUNK
    out = []
    for kvh in range(N_KV_HEADS):
        halves = []
        for half in range(2):
            col = jnp.zeros((STACK, 1), F32)
            for pp in range(PAIRS_PER_KV):
                col = jnp.where(blk == pp, sink_ref[0, 2 * (kvh * PAIRS_PER_KV + pp) + half], col)
            halves.append(col)
        out.append(halves)
    return out


def _softmax_stack(logits, sink_cols):
    ps, sinks = [], []
    for half in range(2):
        lg = logits[:, half * 2 * CHUNK:(half + 1) * 2 * CHUNK]
        sk = sink_cols[half]
        m = jnp.maximum(jnp.max(lg, axis=-1, keepdims=True), sk)
        e = jnp.exp(lg - m)
        es = jnp.exp(sk - m)
        inv = 1.0 / (jnp.sum(e, axis=-1, keepdims=True) + es)
        ps.append(e * inv)
        sinks.append(es * inv)
    return jnp.concatenate(ps, axis=1), sinks


def _band_operands(before_k, this_k, before_v, this_v, kvh):
    kblk = jnp.concatenate([before_k[2 * kvh], this_k[2 * kvh], before_k[2 * kvh + 1], this_k[2 * kvh + 1]], axis=0)
    vblk = jnp.concatenate([before_v[2 * kvh], this_v[2 * kvh], before_v[2 * kvh + 1], this_v[2 * kvh + 1]], axis=0)
    return kblk, vblk


def _fwd_b(r0, target, wb, wkv, wob, post_g, post_b, sinks, bias):
    seq = r0.shape[0]
    tm = min(TOKENS_B, seq)
    n_chunks = tm // CHUNK
    part_rows = min(PART_ROWS_B, tm)
    n_parts = tm // part_rows
    part_chunks = part_rows // CHUNK
    shard_cols = wb.shape[2]

    def body(r0_ref, tgt_ref, wb_ref, wkv_ref, wob_ref, pg_ref, pb_ref, sink_ref, bias_ref,
             q_ref, z_ref, kv_ref, dr1_ref, loss_ref, dg_ref, db_ref, ek_last, ev_last, *scratch):
        q_s, z_s, h1_s, o_s, ek_s, ev_s = (scratch[i * n_parts:(i + 1) * n_parts] for i in range(6))
        step = pl.program_id(0)

        @pl.when(step == 0)
        def _():
            ek_last[...] = jnp.zeros_like(ek_last)
            ev_last[...] = jnp.zeros_like(ev_last)
            loss_ref[...] = jnp.zeros_like(loss_ref)
            dg_ref[...] = jnp.zeros_like(dg_ref)
            db_ref[...] = jnp.zeros_like(db_ref)

        sink_cols = _sink_columns(sink_ref)
        g1 = pg_ref[1:2, :]

        def project_steps(part):
            rows = slice(part * part_rows, (part + 1) * part_rows)
            live = {}

            def normalize():
                xhat0, _ = _ln_normalize(r0_ref[rows, :])
                h1 = xhat0 * pg_ref[0:1, :] + pb_ref[0:1, :]
                h1_s[part][...] = h1
                live["h1b"] = h1.astype(BF16)

            def shard(k):
                def run():
                    qz = _nn(live["h1b"], wb_ref[k])
                    lo, hi = k * shard_cols, (k + 1) * shard_cols
                    if hi <= B_WIDTH:
                        qs = (qz * SCALE).astype(BF16)
                        q_ref[rows, lo:hi] = qs
                        q_s[part][:, lo:hi] = qs
                    else:
                        z_ref[rows, lo - B_WIDTH:hi - B_WIDTH] = qz
                        z_s[part][:, lo - B_WIDTH:hi - B_WIDTH] = qz
                return run

            def keys():
                kv = _nn(live["h1b"], wkv_ref[...])
                kv_ref[rows, :] = kv.astype(BF16)
                for cj in range(part_chunks):
                    sub = slice(cj * CHUNK, (cj + 1) * CHUNK)
                    for i, e in enumerate(_expand_heads(kv[sub, :KV_WIDTH])):
                        ek_s[part][cj, i] = e
                    for i, e in enumerate(_expand_heads(kv[sub, KV_WIDTH:])):
                        ev_s[part][cj, i] = e

            return [normalize, keys] + [shard(k) for k in range(N_DEV)]

        def attend_steps(part):
            def stack(cj, kvh):
                def run():
                    sub = slice(cj * CHUNK, (cj + 1) * CHUNK)
                    sel = jnp.where(step * n_chunks + part * part_chunks + cj > 0, 1, 0)
                    if cj:
                        before_k, before_v = ek_s[part].at[cj - 1], ev_s[part].at[cj - 1]
                    elif part:
                        before_k, before_v = ek_s[part - 1].at[part_chunks - 1], ev_s[part - 1].at[part_chunks - 1]
                    else:
                        before_k, before_v = ek_last, ev_last
                    kblk, vblk = _band_operands(before_k, ek_s[part].at[cj], before_v, ev_s[part].at[cj], kvh)
                    logits = _nt(_stack_pairs(q_s[part], sub, kvh), kblk) + bias_ref[sel, kvh]
                    p, _ = _softmax_stack(logits, sink_cols[kvh])
                    o = _nn(p.astype(BF16), vblk)
                    for pp in range(PAIRS_PER_KV):
                        o_s[part][sub, _pair_cols(kvh, pp)] = o[pp * CHUNK:(pp + 1) * CHUNK]
                return run

            def finish():
                rows = slice(part * part_rows, (part + 1) * part_rows)
                z = z_s[part][...]
                y = o_s[part][...] * (z * jax.nn.sigmoid(z))
                r1 = ALPHA * h1_s[part][...] + _nn(y.astype(BF16), wob_ref[...])
                xhat1, rstd1 = _ln_normalize(r1)
                diff = xhat1 * g1 + pb_ref[1:2, :] - tgt_ref[rows, :]
                loss_ref[...] += 0.5 * jnp.sum(jnp.mean(diff * diff, axis=-1, keepdims=True), axis=0, keepdims=True)
                dh2 = diff * (1.0 / D_MODEL)
                dg_ref[...] += jnp.sum(dh2 * xhat1, axis=0, keepdims=True)
                db_ref[...] += jnp.sum(dh2, axis=0, keepdims=True)
                dr1_ref[rows, :] = _ln_backward(dh2 * g1, xhat1, rstd1)

            return [stack(cj, kvh) for cj in range(part_chunks) for kvh in range(N_KV_HEADS)] + [finish]

        for run in project_steps(0):
            run()
        for part in range(n_parts):
            for run in _in_turns(attend_steps(part), project_steps(part + 1) if part + 1 < n_parts else []):
                run()
        ek_last[...] = ek_s[n_parts - 1][part_chunks - 1]
        ev_last[...] = ev_s[n_parts - 1][part_chunks - 1]

    vec = pl.BlockSpec((1, D_MODEL), lambda i: (0, 0))
    return pl.pallas_call(
        body, name="fwd_b", grid=(seq // tm,),
        out_shape=[jax.ShapeDtypeStruct((seq, B_WIDTH), BF16), jax.ShapeDtypeStruct((seq, B_WIDTH), F32),
                   jax.ShapeDtypeStruct((seq, 2 * KV_WIDTH), BF16), jax.ShapeDtypeStruct((seq, D_MODEL), F32),
                   jax.ShapeDtypeStruct((1, 1), F32), jax.ShapeDtypeStruct((1, D_MODEL), F32),
                   jax.ShapeDtypeStruct((1, D_MODEL), F32)],
        in_specs=[_rows(tm, D_MODEL), _rows(tm, D_MODEL), _resident(wb.shape), _resident(wkv.shape),
                  _resident(wob.shape), _resident(post_g.shape), _resident(post_b.shape), SMEM,
                  _resident(bias.shape)],
        out_specs=[_rows(tm, B_WIDTH), _rows(tm, B_WIDTH), _rows(tm, 2 * KV_WIDTH), _rows(tm, D_MODEL),
                   pl.BlockSpec((1, 1), lambda i: (0, 0)), vec, vec],
        scratch_shapes=[pltpu.VMEM((4, CHUNK, CHUNK), BF16), pltpu.VMEM((4, CHUNK, CHUNK), BF16)]
        + [pltpu.VMEM((part_rows, B_WIDTH), BF16)] * n_parts + [pltpu.VMEM((part_rows, B_WIDTH), F32)] * n_parts
        + [pltpu.VMEM((part_rows, D_MODEL), F32)] * n_parts + [pltpu.VMEM((part_rows, B_WIDTH), F32)] * n_parts
        + [pltpu.VMEM((part_chunks, 4, CHUNK, CHUNK), BF16)] * (2 * n_parts),
        compiler_params=_seq_params(),
    )(r0, target, wb, wkv, wob, post_g, post_b, sinks, bias)


def _bwd_b(r0, dr1, q, z, kv, wb, wkv, wob, post_g, post_b, sinks, bias):
    seq = r0.shape[0]
    tm = min(TOKENS_B_BWD, seq)
    n_chunks = tm // CHUNK
    n_blocks = seq // tm
    shard_cols = wb.shape[2]

    def body(r0_ref, dr1_ref, q_ref, z_ref, kv_ref, kvp_ref, wb_ref, wkv_ref, wob_ref, pg_ref, pb_ref, sink_ref,
             bias_ref, dr0_ref, dwb_out, dwkv_out, dwob_out, dsink_ref, dl_out, dpost_ref,
             dwb_acc, dwkv_acc, dwob_acc, dl_acc, dsink_acc, carry_ref, ek_ref, ev_ref, y_ref, dqz_ref, dkv_ref):
        step = pl.program_id(0)
        block = n_blocks - 1 - step

        @pl.when(step == 0)
        def _():
            dpost_ref[...] = jnp.zeros_like(dpost_ref)
            dwb_acc[...] = jnp.zeros_like(dwb_acc)
            dwkv_acc[...] = jnp.zeros_like(dwkv_acc)
            dwob_acc[...] = jnp.zeros_like(dwob_acc)
            dl_acc[...] = jnp.zeros_like(dl_acc)
            carry_ref[...] = jnp.zeros_like(carry_ref)
            dsink_acc[...] = jnp.zeros_like(dsink_acc)

        xhat0, rstd0 = _ln_normalize(r0_ref[...])
        h1b = (xhat0 * pg_ref[0:1, :] + pb_ref[0:1, :]).astype(BF16)
        dr1 = dr1_ref[...]
        dr1b = dr1.astype(BF16)
        dy = _nt(dr1b, wob_ref[...])
        sg, dsg = _silu_and_grad(z_ref[...])
        do_all = dy * sg
        dyz = dy * dsg

        for ci in range(n_chunks + 1):
            src = kvp_ref[...] if ci == 0 else kv_ref[(ci - 1) * CHUNK:ci * CHUNK, :]
            src = src.astype(F32)
            for i, e in enumerate(_expand_heads(src[:, :KV_WIDTH])):
                ek_ref[ci, i] = e
            for i, e in enumerate(_expand_heads(src[:, KV_WIDTH:])):
                ev_ref[ci, i] = e

        sink_cols = _sink_columns(sink_ref)
        for ci in reversed(range(n_chunks)):
            rows = slice(ci * CHUNK, (ci + 1) * CHUNK)
            sel = jnp.where(block * n_chunks + ci > 0, 1, 0)
            dk_band = jnp.zeros((2 * CHUNK, KV_WIDTH), F32)
            dv_band = jnp.zeros((2 * CHUNK, KV_WIDTH), F32)
            for kvh in range(N_KV_HEADS):
                kblk, vblk = _band_operands(ek_ref.at[ci], ek_ref.at[ci + 1], ev_ref.at[ci], ev_ref.at[ci + 1], kvh)
                qs = _stack_pairs(q_ref, rows, kvh)
                p, psink = _softmax_stack(_nt(qs, kblk) + bias_ref[sel, kvh], sink_cols[kvh])
                pb16 = p.astype(BF16)
                o = _nn(pb16, vblk)
                y = (o * _stack_pairs(sg, rows, kvh)).astype(BF16)
                dz = (o * _stack_pairs(dyz, rows, kvh)).astype(BF16)
                dob = _stack_pairs(do_all, rows, kvh).astype(BF16)
                dp = _nt(dob, vblk)
                dlogits = []
                for half in range(2):
                    hs = slice(half * 2 * CHUNK, (half + 1) * 2 * CHUNK)
                    delta = jnp.sum(dp[:, hs] * p[:, hs], axis=-1, keepdims=True)
                    dlogits.append(p[:, hs] * (dp[:, hs] - delta))
                    dsink_acc[2 * kvh + half] += psink[half] * delta
                dlog = jnp.concatenate(dlogits, axis=1)
                dl_acc[kvh] += dlog
                dsc = dlog.astype(BF16)
                dq = (_nn(dsc, kblk) * SCALE).astype(BF16)
                for pp in range(PAIRS_PER_KV):
                    blk = slice(pp * CHUNK, (pp + 1) * CHUNK)
                    cols = _pair_cols(kvh, pp)
                    y_ref[rows, cols] = y[blk]
                    dqz_ref[rows, cols] = dq[blk]
                    dqz_ref[rows, B_WIDTH + cols.start:B_WIDTH + cols.stop] = dz[blk]
                dk_band = dk_band + _fold_heads(_tn(dsc, qs), kvh)
                dv_band = dv_band + _fold_heads(_tn(pb16, dob), kvh)
            band = jnp.concatenate([dk_band, dv_band], axis=1)
            dkv_ref[rows, :] = (band[CHUNK:] + carry_ref[...]).astype(BF16)
            carry_ref[...] = band[:CHUNK]

        dqz = dqz_ref[...]
        dkv = dkv_ref[...]
        dh1 = ALPHA * dr1 + _nt(dkv, wkv_ref[...])
        for k in range(N_DEV):
            part = dqz[:, k * shard_cols:(k + 1) * shard_cols]
            dh1 = dh1 + _nt(part, wb_ref[k])
            dwb_acc[k] += _tn(h1b, part)
        dpost_ref[0:1, :] += jnp.sum(dh1 * xhat0, axis=0, keepdims=True)
        dpost_ref[1:2, :] += jnp.sum(dh1, axis=0, keepdims=True)
        dr0_ref[...] = _ln_backward(dh1 * pg_ref[0:1, :], xhat0, rstd0)
        dwkv_acc[...] += _tn(h1b, dkv)
        dwob_acc[...] += _tn(y_ref[...], dr1b)

        @pl.when(step == n_blocks - 1)
        def _():
            pltpu.sync_copy(dwb_acc, dwb_out)
            pltpu.sync_copy(dwkv_acc, dwkv_out)
            pltpu.sync_copy(dwob_acc, dwob_out)
            pltpu.sync_copy(dl_acc, dl_out)
            lane = lax.broadcasted_iota(jnp.int32, (1, CHUNK), 1)
            dsink = jnp.zeros((1, CHUNK), F32)
            for kvh in range(N_KV_HEADS):
                for half in range(2):
                    for pp in range(PAIRS_PER_KV):
                        head = 2 * (kvh * PAIRS_PER_KV + pp) + half
                        tot = jnp.sum(dsink_acc[2 * kvh + half, pp * CHUNK:(pp + 1) * CHUNK, :], axis=0, keepdims=True)
                        dsink = dsink - jnp.where(lane == head, tot, 0.0)
            dsink_ref[...] = dsink

    rev = functools.partial(_rows, tm, reverse_of=n_blocks)
    prev_chunk = pl.BlockSpec(
        (CHUNK, 2 * KV_WIDTH), lambda i: (jnp.maximum((n_blocks - 1 - i) * n_chunks - 1, 0), 0))
    return pl.pallas_call(
        body, name="bwd_b", grid=(n_blocks,),
        out_shape=[jax.ShapeDtypeStruct((seq, D_MODEL), F32), jax.ShapeDtypeStruct(wb.shape, F32),
                   jax.ShapeDtypeStruct(wkv.shape, F32), jax.ShapeDtypeStruct(wob.shape, F32),
                   jax.ShapeDtypeStruct((1, CHUNK), F32),
                   jax.ShapeDtypeStruct((N_KV_HEADS, STACK, 4 * CHUNK), F32), jax.ShapeDtypeStruct((2, D_MODEL), F32)],
        in_specs=[rev(D_MODEL), rev(D_MODEL), rev(B_WIDTH), rev(B_WIDTH), rev(2 * KV_WIDTH), prev_chunk,
                  _resident(wb.shape), _resident(wkv.shape), _resident(wob.shape), _resident(post_g.shape),
                  _resident(post_b.shape), SMEM, _resident(bias.shape)],
        out_specs=[rev(D_MODEL), ANY, ANY, ANY, pl.BlockSpec((1, CHUNK), lambda i: (0, 0)), ANY,
                   pl.BlockSpec((2, D_MODEL), lambda i: (0, 0))],
        scratch_shapes=[pltpu.VMEM(wb.shape, F32), pltpu.VMEM(wkv.shape, F32), pltpu.VMEM(wob.shape, F32),
                        pltpu.VMEM((N_KV_HEADS, STACK, 4 * CHUNK), F32), pltpu.VMEM((2 * N_KV_HEADS, STACK, 1), F32),
                        pltpu.VMEM((CHUNK, 2 * KV_WIDTH), F32),
                        pltpu.VMEM((n_chunks + 1, 4, CHUNK, CHUNK), BF16),
                        pltpu.VMEM((n_chunks + 1, 4, CHUNK, CHUNK), BF16),
                        pltpu.VMEM((tm, B_WIDTH), BF16), pltpu.VMEM((tm, 2 * B_WIDTH), BF16),
                        pltpu.VMEM((tm, 2 * KV_WIDTH), BF16)],
        compiler_params=_seq_params(),
    )(r0, dr1, q, z, kv, kv, wb, wkv, wob, post_g, post_b, sinks, bias)


def _bwd_a(dr0, p, ln_gb, w_spatial, bsp_t, woa, exchange):
    seq = dr0.shape[0]
    tm = min(TOKENS_A, seq)
    n_chunks = tm // CHUNK
    n_blocks = seq // tm

    def body(dr0_ref, p_ref, ln_ref, ws_ref, bsp_ref, woa_ref, *refs):
        ex_in, refs = refs[:exchange.n], refs[exchange.n:]
        (dp_ref, dwoa_out, dws_ref, dbsp_ref, dln_ref), refs = refs[:5], refs[5:]
        ex_out, refs = refs[:exchange.n], refs[exchange.n:]
        (dwoa_acc, s_ref, ds_ref, dvn_ref), ex_sems = refs[:4], refs[4:]
        step = pl.program_id(0)

        @pl.when(step == 0)
        def _():
            exchange.start(ex_in, ex_out, ex_sems)
            dwoa_acc[...] = jnp.zeros_like(dwoa_acc)
            dws_ref[...] = jnp.zeros_like(dws_ref)
            dbsp_ref[...] = jnp.zeros_like(dbsp_ref)
            dln_ref[...] = jnp.zeros_like(dln_ref)

        dr0b = dr0_ref[...].astype(BF16)

        vhat, rstd_v = _ln_normalize(p_ref[:, A_WIDTH:2 * A_WIDTH])
        ln_g = ln_ref[0:1, :]
        vn = (vhat * ln_g + ln_ref[1:2, :]).astype(BF16)
        _spatial_gate(ws_ref, bsp_ref, vn, s_ref, n_chunks)
        u = p_ref[:, :A_WIDTH]
        s = s_ref[...]
        sg, dsg = _silu_and_grad(p_ref[:, 2 * A_WIDTH:])
        gate = s * sg
        dwoa_acc[...] += _tn((u * gate).astype(BF16), dr0b)
        dy = _nt(dr0b, woa_ref[...])
        dyu = dy * u
        dp_ref[:, :A_WIDTH] = (dy * gate).astype(BF16)
        dp_ref[:, 2 * A_WIDTH:] = (dyu * (s * dsg)).astype(BF16)
        ds_ref[...] = dyu * sg

        lane = lax.broadcasted_iota(jnp.int32, (CHUNK, CHUNK), 1)
        dbsp = jnp.zeros((CHUNK, CHUNK), F32)
        for g in range(A_GROUPS):
            w = _masked_spatial(ws_ref, g)
            cols = slice(g * A_GROUP_DIM, (g + 1) * A_GROUP_DIM)
            dws = jnp.zeros((CHUNK, CHUNK), F32)
            for ci in range(n_chunks):
                rows = slice(ci * CHUNK, (ci + 1) * CHUNK)
                ds = ds_ref[rows, cols]
                dsb = ds.astype(BF16)
                dvn_ref[rows, cols] = _tn(w, dsb)
                dws = dws + _nt(dsb, vn[rows, cols])
                dbsp = dbsp + jnp.where(lane == g, jnp.sum(ds, axis=-1, keepdims=True), 0.0)
            dws_ref[g] += dws
        dbsp_ref[...] += dbsp

        dvn = dvn_ref[...]
        dln_ref[0:1, :] += jnp.sum(dvn * vhat, axis=0, keepdims=True)
        dln_ref[1:2, :] += jnp.sum(dvn, axis=0, keepdims=True)
        dp_ref[:, A_WIDTH:2 * A_WIDTH] = _ln_backward(dvn * ln_g, vhat, rstd_v).astype(BF16)

        @pl.when(step == n_blocks - 1)
        def _():
            t = lax.broadcasted_iota(jnp.int32, (CHUNK, CHUNK), 0)
            for g in range(A_GROUPS):
                dws_ref[g] = jnp.where(lane <= t, dws_ref[g], 0.0)
            pltpu.sync_copy(dwoa_acc, dwoa_out)
            exchange.wait(ex_in, ex_out, ex_sems)

    def fixed(shape):
        nd = len(shape)
        return pl.BlockSpec(shape, lambda i: (0,) * nd)

    return pl.pallas_call(
        body, name="bwd_a", grid=(n_blocks,),
        out_shape=[jax.ShapeDtypeStruct((seq, 3 * A_WIDTH), BF16),
                   jax.ShapeDtypeStruct(woa.shape, F32), jax.ShapeDtypeStruct(w_spatial.shape, F32),
                   jax.ShapeDtypeStruct((CHUNK, CHUNK), F32), jax.ShapeDtypeStruct((2, A_WIDTH), F32)]
        + exchange.out_shape,
        in_specs=[_rows(tm, D_MODEL), _rows(tm, 3 * A_WIDTH), _resident(ln_gb.shape),
                  _resident(w_spatial.shape), _resident(bsp_t.shape), _resident(woa.shape)] + [ANY] * exchange.n,
        out_specs=[_rows(tm, 3 * A_WIDTH), ANY, fixed(w_spatial.shape),
                   fixed((CHUNK, CHUNK)), fixed((2, A_WIDTH))] + [ANY] * exchange.n,
        scratch_shapes=[pltpu.VMEM(woa.shape, F32), pltpu.VMEM((tm, A_WIDTH), F32),
                        pltpu.VMEM((tm, A_WIDTH), F32), pltpu.VMEM((tm, A_WIDTH), F32)] + exchange.scratch_shapes,
        compiler_params=_seq_params(),
    )(dr0, p, ln_gb, w_spatial, bsp_t, woa, *exchange.inputs)


def _input_grad_blocks(seq):
    tm = min(TOKENS_DX, seq // 8)
    return tm, seq // tm


def _input_grad(dp, dr0, wa, first, n_blocks, earlier, exchange):
    seq = dr0.shape[0]
    tm, _ = _input_grad_blocks(seq)
    shard_cols = wa.shape[2]
    n_own = 3 if earlier is None else 4

    def body(dp_ref, dr0_ref, wa_ref, *refs):
        refs = refs[n_own - 3:]
        ex_in, dx_ref, ex_out, ex_sems = (refs[:exchange.n], refs[exchange.n], refs[exchange.n + 1:2 * exchange.n + 1],
                                          refs[2 * exchange.n + 1:])
        step = pl.program_id(0)

        @pl.when(step == 0)
        def _():
            exchange.start(ex_in, ex_out, ex_sems)

        dx = ALPHA * dr0_ref[...]
        for k in range(N_DEV):
            dx = dx + _nt(dp_ref[:, k * shard_cols:(k + 1) * shard_cols], wa_ref[k])
        dx_ref[...] = dx

        @pl.when(step == n_blocks - 1)
        def _():
            exchange.wait(ex_in, ex_out, ex_sems)

    def rows(width):
        return pl.BlockSpec((tm, width), lambda i: (first + i, 0))

    return pl.pallas_call(
        body, name=f"input_grad_{first}", grid=(n_blocks,),
        out_shape=[jax.ShapeDtypeStruct((seq, D_MODEL), F32)] + exchange.out_shape,
        in_specs=[rows(3 * A_WIDTH), rows(D_MODEL), _resident(wa.shape)] + [ANY] * (n_own - 3 + exchange.n),
        out_specs=[rows(D_MODEL)] + [ANY] * exchange.n,
        scratch_shapes=exchange.scratch_shapes,
        input_output_aliases={} if earlier is None else {3: 0},
        compiler_params=_seq_params(),
    )(dp, dr0, wa, *(() if earlier is None else (earlier,)), *exchange.inputs)


def _in_proj_weight_grad(xb, dp, shard_cols, exchange):
    seq = xb.shape[0]
    tk = min(TOKENS_WGRAD, seq)
    n_k = seq // tk

    def body(x_ref, dp_ref, *refs):
        ex_in, out_ref, ex_out, ex_sems = (refs[:exchange.n], refs[exchange.n], refs[exchange.n + 1:2 * exchange.n + 1],
                                           refs[2 * exchange.n + 1:])
        shard, step = pl.program_id(0), pl.program_id(1)

        @pl.when((shard == 0) & (step == 0))
        def _():
            exchange.start(ex_in, ex_out, ex_sems)

        @pl.when(step == 0)
        def _():
            out_ref[...] = jnp.zeros_like(out_ref)

        out_ref[...] += _tn(x_ref[...], dp_ref[...])

        @pl.when((shard == N_DEV - 1) & (step == n_k - 1))
        def _():
            exchange.wait(ex_in, ex_out, ex_sems)

    return pl.pallas_call(
        body, name="in_proj_weight_grad", grid=(N_DEV, n_k),
        out_shape=[jax.ShapeDtypeStruct((N_DEV, D_MODEL, shard_cols), F32)] + exchange.out_shape,
        in_specs=[pl.BlockSpec((tk, D_MODEL), lambda j, k: (k, 0)),
                  pl.BlockSpec((tk, shard_cols), lambda j, k: (k, j))] + [ANY] * exchange.n,
        out_specs=[pl.BlockSpec((None, D_MODEL, shard_cols), lambda j, k: (j, 0, 0))] + [ANY] * exchange.n,
        scratch_shapes=exchange.scratch_shapes,
        compiler_params=pltpu.CompilerParams(dimension_semantics=("arbitrary", "arbitrary"),
                                             vmem_limit_bytes=VMEM_LIMIT_BYTES),
    )(xb, dp, *exchange.inputs)


class _Exchange:
    def __init__(self, pair=(), chip=()):
        self.pair, self.chip = tuple(pair), tuple(chip)
        self.inputs = self.pair + self.chip
        self.n = len(self.inputs)
        self.out_shape = [jax.ShapeDtypeStruct((4, 1) + g.shape[2:], F32) for g in self.pair]
        self.out_shape += [jax.ShapeDtypeStruct((3,) + s.shape[1:], s.dtype) for s in self.chip]
        self.scratch_shapes = [pltpu.SemaphoreType.DMA((max(len(self.pair), 1),)) for _ in range(2)]
        self.scratch_shapes += [pltpu.SemaphoreType.DMA((max(len(self.chip), 1), 3)) for _ in range(2)]

    def _copies(self, ins, outs, sems):
        if not self.n:
            return []
        pair_send, pair_recv, chip_send, chip_recv = sems
        x, y, c = lax.axis_index("x"), lax.axis_index("y"), lax.axis_index("c")
        copies = []
        for a in range(len(self.pair)):
            copies.append(pltpu.make_async_remote_copy(
                src_ref=ins[a].at[:, pl.ds(1 - c, 1)], dst_ref=outs[a],
                send_sem=pair_send.at[a], recv_sem=pair_recv.at[a], device_id=(x, y, 1 - c), device_id_type=MESH))
        for a in range(len(self.chip)):
            src, dst = ins[len(self.pair) + a], outs[len(self.pair) + a]
            for k, (px, py) in enumerate([(1 - x, y), (x, 1 - y), (1 - x, 1 - y)]):
                copies.append(pltpu.make_async_remote_copy(
                    src_ref=src.at[2 * px + py], dst_ref=dst.at[k], send_sem=chip_send.at[a, k],
                    recv_sem=chip_recv.at[a, k], device_id=(px, py, c), device_id_type=MESH))
        return copies

    def start(self, ins, outs, sems):
        for cp in self._copies(ins, outs, sems):
            cp.start()

    def wait(self, ins, outs, sems):
        for cp in self._copies(ins, outs, sems):
            cp.wait()


def _row_tile(rows, cols, itemsize=4):
    if rows * cols * itemsize <= (1 << 20):
        return rows
    t = 8
    while 2 * t * cols * itemsize <= (1 << 20) and rows % (2 * t) == 0:
        t *= 2
    assert rows % t == 0, (rows, cols)
    return t


def _pair_sum(place, grads, landeds, dtypes):
    n = len(grads)

    def body(place_ref, *refs):
        for a in range(n):
            refs[2 * n + a][...] = (refs[2 * a][...] + refs[2 * a + 1][...]).astype(dtypes[a])

    in_specs, out_specs, operands = [], [], []
    for g, l in zip(grads, landeds):
        rows, cols = g.shape[2:]
        in_specs += [pl.BlockSpec((None, None, rows, cols), lambda k, pr: (k, pr[1], 0, 0)),
                     pl.BlockSpec((None, None, rows, cols), lambda k, pr: (k, 0, 0, 0))]
        out_specs.append(pl.BlockSpec((None, rows, cols), lambda k, pr: (k, 0, 0)))
        operands += [g, l]
    return pl.pallas_call(
        body, name="pair_sum",
        out_shape=[jax.ShapeDtypeStruct((4,) + g.shape[2:], dt) for g, dt in zip(grads, dtypes)],
        grid_spec=pltpu.PrefetchScalarGridSpec(num_scalar_prefetch=1, grid=(4,), in_specs=in_specs,
                                               out_specs=out_specs),
        compiler_params=pltpu.CompilerParams(dimension_semantics=("parallel",), vmem_limit_bytes=VMEM_LIMIT_BYTES),
    )(place, *operands)


def _adamw_math(w, g, m, v):
    m = ADAM_B1 * m + (1.0 - ADAM_B1) * g
    v = ADAM_B2 * v + (1.0 - ADAM_B2) * (g * g)
    m_hat = m / (1.0 - ADAM_B1 ** ADAM_STEP)
    v_hat = v / (1.0 - ADAM_B2 ** ADAM_STEP)
    delta = -ADAM_LR * (m_hat / (jnp.sqrt(v_hat) + ADAM_EPS) + ADAM_WD * w)
    return delta, m, v


def _adamw_shard(place, grad, landed, got, w, m, v):
    rows, cols = w.shape
    tr = _row_tile(rows, cols)

    def body(place_ref, a_ref, b_ref, got_ref, w_ref, m_ref, v_ref, g_out, d_out, m_out, v_out):
        g = a_ref[...] + b_ref[...]
        for k in range(3):
            g = g + got_ref[k].astype(F32)
        g_out[...] = g
        d_out[...], m_out[...], v_out[...] = _adamw_math(w_ref[...], g, m_ref[...], v_ref[...])

    spec = pl.BlockSpec((tr, cols), lambda i, pr: (i, 0))
    out = jax.ShapeDtypeStruct((rows, cols), F32)
    return pl.pallas_call(
        body, name="adamw_shard", out_shape=[out] * 4,
        grid_spec=pltpu.PrefetchScalarGridSpec(
            num_scalar_prefetch=1, grid=(rows // tr,),
            in_specs=[pl.BlockSpec((None, None, tr, cols), lambda i, pr: (pr[0], pr[1], i, 0)),
                      pl.BlockSpec((None, None, tr, cols), lambda i, pr: (pr[0], 0, i, 0)),
                      pl.BlockSpec((3, tr, cols), lambda i, pr: (0, i, 0)), spec, spec, spec],
            out_specs=[spec] * 4),
        compiler_params=pltpu.CompilerParams(dimension_semantics=("parallel",)),
    )(place, grad, landed, got, w, m, v)


def _gather_small(grad, landed, got):
    rows, cols = got.shape[1:]

    def body(grad_ref, landed_ref, got_ref, out_ref, send_sems, recv_sems):
        x, y, c = lax.axis_index("x"), lax.axis_index("y"), lax.axis_index("c")
        me, sibling = (x, y, c), (x, y, 1 - c)
        chips = [(1 - x, y), (x, 1 - y), (1 - x, 1 - y)]

        def slot(px, py, pc):
            return out_ref.at[4 * px + 2 * py + pc]

        def copy(k, block, to):
            return pltpu.make_async_remote_copy(
                src_ref=slot(*block), dst_ref=slot(*block), send_sem=send_sems.at[k], recv_sem=recv_sems.at[k],
                device_id=to, device_id_type=MESH)

        mine = grad_ref[2 * x + y, c] + landed_ref[2 * x + y, 0]
        out_ref[4 * x + 2 * y + c] = ((mine + got_ref[0]) + got_ref[1]) + got_ref[2]
        first = [copy(0, me, sibling)] + [copy(1 + j, me, (*chip, c)) for j, chip in enumerate(chips)]
        for cp in first:
            cp.start()
        passed = [copy(4 + j, (*chip, c), sibling) for j, chip in enumerate(chips)]
        for j, chip in enumerate(chips):
            copy(1 + j, (*chip, c), me).wait_recv()
            passed[j].start()
        copy(0, sibling, me).wait_recv()
        for j, chip in enumerate(chips):
            copy(4 + j, (*chip, 1 - c), me).wait_recv()
        for cp in first + passed:
            cp.wait_send()

    return pl.pallas_call(
        body, name="gather_small", out_shape=jax.ShapeDtypeStruct((N_DEV, rows, cols), F32),
        in_specs=[VMEM] * 3, out_specs=VMEM,
        scratch_shapes=[pltpu.SemaphoreType.DMA((7,)), pltpu.SemaphoreType.DMA((7,))],
    )(grad, landed, got)


def _adamw_small(params):
    n = len(params)

    def body(*refs):
        ins, outs = refs[:4 * n], refs[4 * n:]
        for i in range(n):
            w_ref, g_ref, m_ref, v_ref = ins[4 * i:4 * i + 4]
            d, m, v = _adamw_math(w_ref[...], g_ref[...], m_ref[...], v_ref[...])
            outs[3 * i][...] = d
            outs[3 * i + 1][...] = m
            outs[3 * i + 2][...] = v

    flat = [a for p in params for a in p]
    out_shape = [jax.ShapeDtypeStruct(p[0].shape, F32) for p in params for _ in range(3)]
    outs = pl.pallas_call(
        body, name="adamw_small", out_shape=out_shape, in_specs=[VMEM] * (4 * n), out_specs=[VMEM] * (3 * n),
    )(*flat)
    return [tuple(outs[3 * i:3 * i + 3]) for i in range(n)]


def _pack_small(parts):
    rows = []
    for a in parts:
        r, c = a.shape
        rows.append(jnp.pad(a, ((0, (-r) % 8), (0, CHUNK - c))))
    flat = jnp.concatenate(rows, axis=0)
    return jnp.pad(flat, ((0, N_DEV * SMALL_ROWS - flat.shape[0]), (0, 0))).reshape(N_DEV, SMALL_ROWS, CHUNK)


def _unpack_small(flat, shapes):
    out, at = [], 0
    for r, c in shapes:
        out.append(flat[at:at + r, :c])
        at += r + (-r) % 8
    return out


def kernel(x, w_in_a, sgu_ln_g, sgu_ln_b, w_spatial, b_spatial, w_out_a, w_kv, w_in_b, attn_sinks, rel_bias, w_out_b, post_ln_g, post_ln_b, loss_target, m_w_in_a, m_sgu_ln_g, m_sgu_ln_b, m_w_spatial, m_b_spatial, m_w_out_a, m_w_kv, m_w_in_b, m_attn_sinks, m_rel_bias, m_w_out_b, m_post_ln_g, m_post_ln_b, v_w_in_a, v_sgu_ln_g, v_sgu_ln_b, v_w_spatial, v_b_spatial, v_w_out_a, v_w_kv, v_w_in_b, v_attn_sinks, v_rel_bias, v_w_out_b, v_post_ln_g, v_post_ln_b):
    seq = x.shape[1]
    me = 4 * lax.axis_index("x") + 2 * lax.axis_index("y") + lax.axis_index("c")
    bucket = jnp.asarray(_band_tables())

    ln_shard = jnp.concatenate([sgu_ln_g, sgu_ln_b], axis=0)
    x2, tgt = x[0], loss_target[0]
    order = jnp.stack([2 * px + py for px, py, _ in _use_order()[::2]]).astype(jnp.int32)
    p, wa, woa, ln_all, xb = _gather_in_proj(order, x2, w_in_a[0], (w_out_a[0],), ln_shard)
    woa = woa.reshape(A_WIDTH, D_MODEL)
    ln_gb = jnp.transpose(ln_all, (1, 0, 2)).reshape(2, A_WIDTH)
    ws = w_spatial[0]
    bsp_t = jnp.pad(b_spatial[0].T, ((0, 0), (0, CHUNK - A_GROUPS)))
    bias = _bias_table(rel_bias, bucket)

    r0, wkv, wb, wob = _fwd_a(x2, p, ln_gb, ws, bsp_t, woa, (w_kv, w_in_b[0], w_out_b[0]))
    wkv = wkv.reshape(D_MODEL, 2 * KV_WIDTH)
    wob = wob.reshape(B_WIDTH, D_MODEL)
    q, z, kv, dr1, loss, dg1, db1 = _fwd_b(r0, tgt, wb, wkv, wob, post_ln_g, post_ln_b, attn_sinks, bias)
    dr0, dwb, dwkv, dwob, dsink, dlog, dpost0 = _bwd_b(r0, dr1, q, z, kv, wb, wkv, wob, post_ln_g, post_ln_b,
                                                       attn_sinks, bias)
    drel = _rel_bias_grad(dlog, bucket)

    place = jnp.stack([2 * lax.axis_index("x") + lax.axis_index("y"), lax.axis_index("c")]).astype(jnp.int32)

    def by_device(g):
        return g.reshape((4, 2, g.shape[0] // N_DEV) + g.shape[1:]) if g.ndim == 2 else g.reshape((4, 2) + g.shape[1:])

    dwb, dwkv, dwob = by_device(dwb), by_device(dwkv), by_device(dwob)
    dp, dwoa, dws, dbsp_t, dln, l_wb, l_wkv, l_wob = _bwd_a(dr0, p, ln_gb, ws, bsp_t, woa,
                                                            _Exchange(pair=(dwb, dwkv, dwob)))
    dwoa = by_device(dwoa)
    small_shapes = [(A_GROUPS * CHUNK, CHUNK), (A_GROUPS, CHUNK), (1, N_Q_HEADS), (REL_BUCKETS, N_Q_HEADS),
                    (16, CHUNK), (16, CHUNK), (16, CHUNK), (16, CHUNK), (1, 1)]
    small = by_device(_pack_small([
        dws.reshape(A_GROUPS * CHUNK, CHUNK), dbsp_t[:, :A_GROUPS].T, dsink[:, :N_Q_HEADS], drel[:, :N_Q_HEADS],
        jnp.concatenate([dpost0[0:1], dg1], axis=0).reshape(16, CHUNK),
        jnp.concatenate([dpost0[1:2], db1], axis=0).reshape(16, CHUNK),
        dln[0].reshape(16, CHUNK), dln[1].reshape(16, CHUNK), loss]))
    s_wb, s_wkv, s_wob = _pair_sum(place, (dwb, dwkv, dwob), (l_wb, l_wkv, l_wob), (BF16,) * 3)
    dwa, l_woa, l_small, g_wb, g_wkv, g_wob = _in_proj_weight_grad(
        xb, dp, wa.shape[2], _Exchange(pair=(dwoa, small), chip=(s_wb, s_wkv, s_wob)))
    dwa = by_device(dwa)
    s_woa, s_small = _pair_sum(place, (dwoa, small), (l_woa, l_small), (BF16, F32))
    _, blocks = _input_grad_blocks(seq)
    cut = blocks // 4
    dx, l_wa, g_woa, g_small = _input_grad(dp, dr0, wa, 0, cut, None, _Exchange(pair=(dwa,), chip=(s_woa, s_small)))
    s_wa, = _pair_sum(place, (dwa,), (l_wa,), (BF16,))
    dx, g_wa = _input_grad(dp, dr0, wa, cut, blocks - cut, dx, _Exchange(chip=(s_wa,)))
    full = [dwa, dwoa, dwkv, dwb, dwob, small]
    landed = [l_wa, l_woa, l_wkv, l_wb, l_wob, l_small]
    got = [g_wa, g_woa, g_wkv, g_wb, g_wob, g_small]

    big = [(w_in_a, m_w_in_a, v_w_in_a), (w_out_a, m_w_out_a, v_w_out_a), (w_kv, m_w_kv, v_w_kv),
           (w_in_b, m_w_in_b, v_w_in_b), (w_out_b, m_w_out_b, v_w_out_b)]
    big_out = []
    for i, (w, m, v) in enumerate(big):
        shape2 = got[i].shape[1:]
        res = _adamw_shard(place, full[i], landed[i], got[i], w.reshape(shape2), m.reshape(shape2),
                           v.reshape(shape2))
        big_out.append([r.reshape(w.shape) for r in res])

    small_all = _gather_small(full[5], landed[5], got[5]).reshape(N_DEV * SMALL_ROWS, CHUNK)
    g_ws, g_bsp, g_sink, g_rel, g_pg, g_pb, g_lng, g_lnb, loss_all = _unpack_small(small_all, small_shapes)
    g_ws = g_ws.reshape(w_spatial.shape)
    g_bsp = g_bsp.reshape(b_spatial.shape)
    g_pg = g_pg.reshape(post_ln_g.shape)
    g_pb = g_pb.reshape(post_ln_b.shape)
    g_lng = lax.dynamic_slice(g_lng.reshape(1, A_WIDTH), (0, me * sgu_ln_g.shape[1]), sgu_ln_g.shape)
    g_lnb = lax.dynamic_slice(g_lnb.reshape(1, A_WIDTH), (0, me * sgu_ln_b.shape[1]), sgu_ln_b.shape)
    small_params = [(sgu_ln_g, g_lng, m_sgu_ln_g, v_sgu_ln_g), (sgu_ln_b, g_lnb, m_sgu_ln_b, v_sgu_ln_b),
                    (w_spatial, g_ws, m_w_spatial, v_w_spatial), (b_spatial, g_bsp, m_b_spatial, v_b_spatial),
                    (attn_sinks, g_sink, m_attn_sinks, v_attn_sinks), (rel_bias, g_rel, m_rel_bias, v_rel_bias),
                    (post_ln_g, g_pg, m_post_ln_g, v_post_ln_g), (post_ln_b, g_pb, m_post_ln_b, v_post_ln_b)]
    flat2 = [tuple(a.reshape(-1, a.shape[-1]) for a in p) for p in small_params]
    small_out = [tuple(r.reshape(p[0].shape) for r in res) for res, p in zip(_adamw_small(flat2), small_params)]

    order = [("big", 0), ("small", 0), ("small", 1), ("small", 2), ("small", 3), ("big", 1), ("big", 2), ("big", 3),
             ("small", 4), ("small", 5), ("big", 4), ("small", 6), ("small", 7)]
    grads, deltas, new_m, new_v = [], [], [], []
    for kind, i in order:
        if kind == "big":
            g, d, m, v = big_out[i]
        else:
            g = small_params[i][1]
            d, m, v = small_out[i]
        grads.append(g)
        deltas.append(d)
        new_m.append(m)
        new_v.append(v)
    return (loss_all[0, 0], dx.reshape(x.shape), *grads, *deltas, *new_m, *new_v)
```

```python
import functools

import jax
import jax.numpy as jnp
import numpy as np
from jax import lax
from jax.experimental import pallas as pl
from jax.experimental.pallas import tpu as pltpu

F32 = jnp.float32
BF16 = jnp.bfloat16

D_MODEL = 1024
A_WIDTH = 2048
A_GROUPS = 8
A_GROUP_DIM = A_WIDTH // A_GROUPS
CHUNK = 128
HEAD_DIM = 64
N_Q_HEADS = 16
N_KV_HEADS = 2
PAIRS_PER_KV = N_Q_HEADS // N_KV_HEADS // 2
B_WIDTH = N_Q_HEADS * HEAD_DIM
KV_WIDTH = N_KV_HEADS * HEAD_DIM
REL_BUCKETS = 32
REL_MAX_DIST = 128
DEPTH = 2
ALPHA = (2.0 * DEPTH) ** 0.25
LN_EPS = 1e-5
NEG_INF = -1e30
SCALE = HEAD_DIM ** -0.5

ADAM_LR = 0.001
ADAM_B1 = 0.9
ADAM_B2 = 0.999
ADAM_EPS = 1e-08
ADAM_WD = 0.01
ADAM_STEP = 10

N_DEV = 8
MESH = pl.DeviceIdType.MESH
TOKENS_IN_PROJ = 1024
TOKENS_WGRAD = 4096
TOKENS_A = 256
TOKENS_A_FWD = 256
P_RING = 3
TOKENS_B = 512
PART_ROWS_B = 256
TOKENS_B_BWD = 256
TOKENS_DX = 512
VMEM_LIMIT_BYTES = 60 * 1024 * 1024
SMALL_ROWS = 144

ANY = pl.BlockSpec(memory_space=pl.ANY)
VMEM = pl.BlockSpec(memory_space=pltpu.VMEM)
SMEM = pl.BlockSpec(memory_space=pltpu.SMEM)


def _nn(a, b):
    return jnp.dot(a, b, preferred_element_type=F32)


def _nt(a, b):
    return lax.dot_general(a, b, (((1,), (1,)), ((), ())), preferred_element_type=F32)


def _tn(a, b):
    return lax.dot_general(a, b, (((0,), (0,)), ((), ())), preferred_element_type=F32)


def _resident(shape):
    nd = len(shape)
    return pl.BlockSpec(shape, lambda i: (0,) * nd, pipeline_mode=pl.Buffered(1))


def _rows(tm, width, reverse_of=None):
    if reverse_of is None:
        return pl.BlockSpec((tm, width), lambda i: (i, 0))
    return pl.BlockSpec((tm, width), lambda i: (reverse_of - 1 - i, 0))


def _seq_params():
    return pltpu.CompilerParams(dimension_semantics=("arbitrary",), vmem_limit_bytes=VMEM_LIMIT_BYTES)


def _in_turns(first, second):
    merged = [(i / max(len(first), 1), 0, f) for i, f in enumerate(first)]
    merged += [((i + 0.5) / max(len(second), 1), 1, f) for i, f in enumerate(second)]
    return [f for _, _, f in sorted(merged, key=lambda t: t[:2])]


def _ln_normalize(r):
    mu = jnp.mean(r, axis=-1, keepdims=True)
    xc = r - mu
    var = jnp.mean(xc * xc, axis=-1, keepdims=True)
    rstd = lax.rsqrt(var + LN_EPS)
    return xc * rstd, rstd


def _ln_backward(dxhat, xhat, rstd):
    m1 = jnp.mean(dxhat, axis=-1, keepdims=True)
    m2 = jnp.mean(dxhat * xhat, axis=-1, keepdims=True)
    return rstd * (dxhat - m1 - xhat * m2)


def _silu_and_grad(z):
    sig = jax.nn.sigmoid(z)
    silu = z * sig
    return silu, sig + silu * (1.0 - sig)


def _band_tables():
    t = np.arange(CHUNK, dtype=np.int32)[:, None]
    j = np.arange(2 * CHUNK, dtype=np.int32)[None, :]
    d = t + CHUNK - j
    in_window = (d >= 0) & (d < CHUNK)
    dc = np.clip(d, 0, REL_MAX_DIST - 1)
    max_exact = REL_BUCKETS // 2
    df = np.maximum(dc, 1).astype(np.float32)
    large = max_exact + (np.log(df / np.float32(max_exact)) / np.float32(np.log(REL_MAX_DIST / max_exact))
                         * np.float32(REL_BUCKETS - max_exact)).astype(np.int32)
    large = np.minimum(large, REL_BUCKETS - 1)
    bucket = np.where(dc < max_exact, dc, large)
    return np.where(in_window, bucket, -1).astype(np.int32)


def _use_order():
    x, y, c = lax.axis_index("x"), lax.axis_index("y"), lax.axis_index("c")
    blocks = [(x, y, c), (x, y, 1 - c)]
    for px, py in [(1 - x, y), (x, 1 - y), (1 - x, 1 - y)]:
        blocks += [(px, py, c), (px, py, 1 - c)]
    return blocks


def _gather_in_proj(order, x, w_in_a, small, ln_gb):
    seq = x.shape[0]
    tm = min(TOKENS_IN_PROJ, seq)
    n_blocks = seq // tm
    shard_cols = w_in_a.shape[1]
    n = len(small) + 2

    def body(order_ref, x_ref, *refs):
        ins, refs = refs[:n], refs[n:]
        p_ref, outs, xb_out, refs = refs[0], refs[1:n + 1], refs[n + 1], refs[n + 2:]
        wa_vmem, xb_vmem, stage = refs[0], refs[1], refs[2:n]
        send_sems, recv_sems, local_sems, out_sems, xb_sem = refs[n:]
        xb_copy = pltpu.make_async_copy(xb_vmem, xb_out, xb_sem)
        phase, blk = pl.program_id(0), pl.program_id(1)
        x_, y_, c_ = lax.axis_index("x"), lax.axis_index("y"), lax.axis_index("c")
        me, sibling = (x_, y_, c_), (x_, y_, 1 - c_)
        chips = [(1 - x_, y_), (x_, 1 - y_), (1 - x_, 1 - y_)]
        use = _use_order()

        def index(px, py, pc):
            return 4 * px + 2 * py + pc

        def slot(a, block):
            return (wa_vmem if a == 0 else outs[a]).at[index(*block)]

        def own(a):
            return slot(0, me) if a == 0 else (ins[a] if a == n - 1 else stage[a - 1])

        def copy(a, k, block, to, src=None):
            return pltpu.make_async_remote_copy(
                src_ref=slot(a, block) if src is None else src, dst_ref=slot(a, block),
                send_sem=send_sems.at[a, k], recv_sem=recv_sems.at[a, k], device_id=to, device_id_type=MESH)

        def keep(a):
            return pltpu.make_async_copy(own(a), slot(a, me), local_sems.at[a])

        def write_out(k):
            return pltpu.make_async_copy(slot(0, use[k]), outs[0].at[index(*use[k])], out_sems.at[k])

        def pass_on(a, j):
            copy(a, 1 + j, (*chips[j], c_), me).wait_recv()
            copy(a, 4 + j, (*chips[j], c_), sibling).start()

        def send(a, j):
            copy(a, 1 + j, me, (*chips[j], c_), own(a)).start()

        @pl.when((phase == 0) & (blk == 0))
        def _():
            wa_vmem[index(*me)] = ins[0][...].astype(BF16)
            for a in range(1, n - 1):
                stage[a - 1][...] = ins[a][...].astype(BF16)
            send(0, 0)
            send(0, 1)
            for a in range(n):
                copy(a, 0, me, sibling, own(a)).start()
                if a:
                    keep(a).start()
            write_out(0).start()
            copy(0, 0, sibling, me).wait_recv()
            write_out(1).start()

        @pl.when((phase == 1) & (blk == 0))
        def _():
            pass_on(0, 0)
            write_out(2).start()
            xb_copy.start()
            send(0, 2)
            for a in range(1, n):
                for j in range(3):
                    send(a, j)
            pass_on(0, 1)
            write_out(4).start()
            copy(0, 4, (*chips[0], 1 - c_), me).wait_recv()
            write_out(3).start()

        @pl.when((phase == 2) & (blk == 0))
        def _():
            copy(0, 5, (*chips[1], 1 - c_), me).wait_recv()
            write_out(5).start()

        @pl.when((phase == 2) & (blk == n_blocks // 2))
        def _():
            pass_on(0, 2)
            write_out(6).start()

        @pl.when((phase == 3) & (blk == 0))
        def _():
            copy(0, 6, (*chips[2], 1 - c_), me).wait_recv()
            write_out(7).start()
            for a in range(1, n):
                pass_on(a, 0)
                pass_on(a, 1)

        rows = pl.ds(pl.multiple_of(blk * tm, tm), tm)

        @pl.when(phase == 0)
        def _():
            xb_vmem[rows, :] = x_ref[...].astype(BF16)

        chip = order_ref[phase]
        xb = xb_vmem[rows, :]
        p_ref[:, :shard_cols] = _nn(xb, wa_vmem[2 * chip])
        p_ref[:, shard_cols:] = _nn(xb, wa_vmem[2 * chip + 1])

        @pl.when((phase == N_DEV // 2 - 1) & (blk == n_blocks - 1))
        def _():
            for a in range(1, n):
                pass_on(a, 2)
            for a in range(1, n):
                copy(a, 0, sibling, me).wait_recv()
                for j, chip in enumerate(chips):
                    copy(a, 4 + j, (*chip, 1 - c_), me).wait_recv()
            for a in range(n):
                for k in range(7):
                    copy(a, k, me, me).wait_send()
                if a:
                    keep(a).wait()
            for k in range(N_DEV):
                write_out(k).wait()
            xb_copy.wait()

    gathered = [jax.ShapeDtypeStruct((N_DEV,) + w_in_a.shape, BF16)]
    gathered += [jax.ShapeDtypeStruct((N_DEV,) + s.shape, BF16) for s in small]
    gathered.append(jax.ShapeDtypeStruct((N_DEV,) + ln_gb.shape, F32))
    return pl.pallas_call(
        body, name="gather_in_proj",
        out_shape=[jax.ShapeDtypeStruct((seq, N_DEV * shard_cols), F32)] + gathered
        + [jax.ShapeDtypeStruct((seq, D_MODEL), BF16)],
        grid_spec=pltpu.PrefetchScalarGridSpec(
            num_scalar_prefetch=1, grid=(N_DEV // 2, n_blocks),
            in_specs=[pl.BlockSpec((tm, D_MODEL), lambda k, i, order: (jnp.where(k == 0, i, n_blocks - 1), 0))]
            + [VMEM] * n,
            out_specs=[pl.BlockSpec((tm, 2 * shard_cols), lambda k, i, order: (i, order[k]))] + [ANY] * (n + 1),
            scratch_shapes=[pltpu.VMEM((N_DEV,) + w_in_a.shape, BF16), pltpu.VMEM((seq, D_MODEL), BF16)]
            + [pltpu.VMEM(s.shape, BF16) for s in small] + [
                pltpu.SemaphoreType.DMA((n, 7)), pltpu.SemaphoreType.DMA((n, 7)), pltpu.SemaphoreType.DMA((n,)),
                pltpu.SemaphoreType.DMA((N_DEV,)), pltpu.SemaphoreType.DMA(())]),
        compiler_params=pltpu.CompilerParams(dimension_semantics=("arbitrary", "arbitrary"),
                                             vmem_limit_bytes=VMEM_LIMIT_BYTES),
    )(order, x, w_in_a, *small, ln_gb)


def _bias_table(rel_bias, bucket):
    def body(rel_ref, bucket_ref, out_ref):
        first = jnp.broadcast_to(bucket_ref[0:1, :], (8, 2 * CHUNK))
        col = lax.broadcasted_iota(jnp.int32, (CHUNK, 2 * CHUNK), 1)
        for h in range(N_Q_HEADS):
            row = jnp.full(first.shape, NEG_INF, F32)
            for b in range(REL_BUCKETS):
                row = jnp.where(first == b, rel_ref[b, h], row)
            tab = pltpu.roll(jnp.broadcast_to(row[0:1, :], (CHUNK, 2 * CHUNK)), 0, 1, stride=1, stride_axis=0)
            pair = h // 2
            rows = slice((pair % PAIRS_PER_KV) * CHUNK, (pair % PAIRS_PER_KV + 1) * CHUNK)
            half = slice((h % 2) * 2 * CHUNK, (h % 2 + 1) * 2 * CHUNK)
            out_ref[1, pair // PAIRS_PER_KV, rows, half] = tab
            out_ref[0, pair // PAIRS_PER_KV, rows, half] = jnp.where(col < CHUNK, NEG_INF, tab)

    return pl.pallas_call(
        body, name="bias_table",
        out_shape=jax.ShapeDtypeStruct((2, N_KV_HEADS, PAIRS_PER_KV * CHUNK, 4 * CHUNK), F32),
        in_specs=[SMEM, VMEM], out_specs=VMEM,
    )(rel_bias, bucket)


def _rel_bias_grad(dlogits, bucket):
    def body(dl_ref, bucket_ref, out_ref):
        i = lax.broadcasted_iota(jnp.int32, (CHUNK, CHUNK), 0)
        k = lax.broadcasted_iota(jnp.int32, (CHUNK, CHUNK), 1)
        reverse = jnp.where(i + k == CHUNK - 1, 1.0, 0.0)
        sums = []
        for h in range(N_Q_HEADS):
            pair = h // 2
            dl = dl_ref[pair // PAIRS_PER_KV, (pair % PAIRS_PER_KV) * CHUNK:(pair % PAIRS_PER_KV + 1) * CHUNK,
                        (h % 2) * 2 * CHUNK:(h % 2 + 1) * 2 * CHUNK]
            backwards = jnp.dot(reverse, dl, precision=lax.Precision.HIGHEST, preferred_element_type=F32)
            turned = pltpu.roll(backwards, CHUNK + 1, 1, stride=1, stride_axis=0)
            sums.append(jnp.sum(turned, axis=0, keepdims=True))
        by_offset = jnp.concatenate(sums, axis=0)
        buckets = lax.broadcasted_iota(jnp.int32, (REL_BUCKETS, 2 * CHUNK), 0)
        one_hot = jnp.where(jnp.broadcast_to(bucket_ref[0:1, :], buckets.shape) == buckets, 1.0, 0.0)
        out_ref[...] = lax.dot_general(one_hot, by_offset, (((1,), (1,)), ((), ())),
                                       precision=lax.Precision.HIGHEST, preferred_element_type=F32)

    return pl.pallas_call(
        body, name="rel_bias_grad", out_shape=jax.ShapeDtypeStruct((REL_BUCKETS, N_Q_HEADS), F32),
        in_specs=[VMEM, VMEM], out_specs=VMEM,
    )(dlogits, bucket)


def _masked_spatial(ws_ref, g):
    t = lax.broadcasted_iota(jnp.int32, (CHUNK, CHUNK), 0)
    s = lax.broadcasted_iota(jnp.int32, (CHUNK, CHUNK), 1)
    return jnp.where(s <= t, ws_ref[g], 0.0).astype(BF16)


def _spatial_gate(ws_ref, bsp_t_ref, vn, s_ref, n_chunks):
    for g in range(A_GROUPS):
        w = _masked_spatial(ws_ref, g)
        cols = slice(g * A_GROUP_DIM, (g + 1) * A_GROUP_DIM)
        for ci in range(n_chunks):
            rows = slice(ci * CHUNK, (ci + 1) * CHUNK)
            s_ref[rows, cols] = _nn(w, vn[rows, cols]) + bsp_t_ref[:, g:g + 1]


def _fwd_a(x, p, ln_gb, w_spatial, bsp_t, woa, later):
    seq = x.shape[0]
    tm = min(TOKENS_A_FWD, seq)
    n_chunks = tm // CHUNK
    n_blocks = seq // tm
    n = len(later)

    def body(x_ref, p_hbm, ln_ref, ws_ref, bsp_ref, woa_ref, *refs):
        shards, refs = refs[:n], refs[n:]
        r0_ref, gathered, refs = refs[0], refs[1:1 + n], refs[1 + n:]
        s_ref, p_ring, stage, (send_sems, recv_sems, local_sems, ring_sems) = refs[0], refs[1], refs[2:2 + n], refs[2 + n:]
        step = pl.program_id(0)

        def fetch(block):
            slot = block % P_RING
            start = block * tm if isinstance(block, int) else pl.multiple_of(block * tm, tm)
            return pltpu.make_async_copy(p_hbm.at[pl.ds(start, tm), :], p_ring.at[slot], ring_sems.at[slot])

        @pl.when(step == 0)
        def _():
            for ahead in range(min(P_RING - 1, n_blocks)):
                fetch(ahead).start()

        @pl.when(step + P_RING - 1 < n_blocks)
        def _():
            fetch(step + P_RING - 1).start()

        fetch(step).wait()
        p_ref = p_ring.at[step % P_RING]

        def copies(a):
            x_, y_, c_ = lax.axis_index("x"), lax.axis_index("y"), lax.axis_index("c")
            mine = 4 * x_ + 2 * y_ + c_
            out = [pltpu.make_async_copy(stage[a], gathered[a].at[mine], local_sems.at[a])]
            for r in range(1, N_DEV):
                to = (1 - x_ if r & 4 else x_, 1 - y_ if r & 2 else y_, 1 - c_ if r & 1 else c_)
                out.append(pltpu.make_async_remote_copy(
                    src_ref=stage[a], dst_ref=gathered[a].at[mine], send_sem=send_sems.at[a, r - 1],
                    recv_sem=recv_sems.at[a, r - 1], device_id=to, device_id_type=MESH))
            return out

        @pl.when(step == 0)
        def _():
            for a in range(n):
                stage[a][...] = shards[a][...].astype(BF16)
                for cp in copies(a):
                    cp.start()

        xv = x_ref[...]
        vhat, _ = _ln_normalize(p_ref[:, A_WIDTH:2 * A_WIDTH])
        vn = (vhat * ln_ref[0:1, :] + ln_ref[1:2, :]).astype(BF16)
        _spatial_gate(ws_ref, bsp_ref, vn, s_ref, n_chunks)
        z = p_ref[:, 2 * A_WIDTH:]
        y = p_ref[:, :A_WIDTH] * s_ref[...] * (z * jax.nn.sigmoid(z))
        r0_ref[...] = ALPHA * xv + _nn(y.astype(BF16), woa_ref[...])

        @pl.when(step == n_blocks - 1)
        def _():
            for a in range(n):
                for cp in copies(a):
                    cp.wait()

    return pl.pallas_call(
        body, name="fwd_a", grid=(n_blocks,),
        out_shape=[jax.ShapeDtypeStruct((seq, D_MODEL), F32)]
        + [jax.ShapeDtypeStruct((N_DEV,) + s.shape, BF16) for s in later],
        in_specs=[_rows(tm, D_MODEL), ANY, _resident(ln_gb.shape), _resident(w_spatial.shape),
                  _resident(bsp_t.shape), _resident(woa.shape)] + [VMEM] * n,
        out_specs=[_rows(tm, D_MODEL)] + [ANY] * n,
        scratch_shapes=[pltpu.VMEM((tm, A_WIDTH), F32), pltpu.VMEM((P_RING, tm, 3 * A_WIDTH), F32)]
        + [pltpu.VMEM(s.shape, BF16) for s in later] + [
            pltpu.SemaphoreType.DMA((n, N_DEV - 1)), pltpu.SemaphoreType.DMA((n, N_DEV - 1)),
            pltpu.SemaphoreType.DMA((n,)), pltpu.SemaphoreType.DMA((P_RING,))],
        compiler_params=_seq_params(),
    )(x, p, ln_gb, w_spatial, bsp_t, woa, *later)


def _expand_heads(kv):
    lo = lax.broadcasted_iota(jnp.int32, kv.shape, 1) < HEAD_DIM
    swapped = pltpu.roll(kv, HEAD_DIM, 1)
    zero = jnp.zeros_like(kv)
    return (jnp.where(lo, kv, zero).astype(BF16), jnp.where(lo, zero, swapped).astype(BF16),
            jnp.where(lo, swapped, zero).astype(BF16), jnp.where(lo, zero, kv).astype(BF16))


def _fold_heads(acc, kvh):
    top, bot = acc[:2 * CHUNK], acc[2 * CHUNK:]
    lo = lax.broadcasted_iota(jnp.int32, top.shape, 1) < HEAD_DIM
    t = jnp.where(lo, top, 0.0)
    b = jnp.where(lo, 0.0, bot)
    if kvh == 0:
        return t + pltpu.roll(b, HEAD_DIM, 1)
    return pltpu.roll(t, HEAD_DIM, 1) + b


STACK = PAIRS_PER_KV * CHUNK


def _pair_cols(kvh, pp):
    pair = kvh * PAIRS_PER_KV + pp
    return slice(pair * CHUNK, (pair + 1) * CHUNK)


def _stack_pairs(ref_or_val, rows, kvh):
    return jnp.concatenate([ref_or_val[rows, _pair_cols(kvh, pp)] for pp in range(PAIRS_PER_KV)], axis=0)


def _sink_columns(sink_ref):
    blk = lax.broadcasted_iota(jnp.int32, (STACK, 1), 0) // CHUNK
    out = []
    for kvh in range(N_KV_HEADS):
        halves = []
        for half in range(2):
            col = jnp.zeros((STACK, 1), F32)
            for pp in range(PAIRS_PER_KV):
                col = jnp.where(blk == pp, sink_ref[0, 2 * (kvh * PAIRS_PER_KV + pp) + half], col)
            halves.append(col)
        out.append(halves)
    return out


def _softmax_stack(logits, sink_cols):
    ps, sinks = [], []
    for half in range(2):
        lg = logits[:, half * 2 * CHUNK:(half + 1) * 2 * CHUNK]
        sk = sink_cols[half]
        m = jnp.maximum(jnp.max(lg, axis=-1, keepdims=True), sk)
        e = jnp.exp(lg - m)
        es = jnp.exp(sk - m)
        inv = 1.0 / (jnp.sum(e, axis=-1, keepdims=True) + es)
        ps.append(e * inv)
        sinks.append(es * inv)
    return jnp.concatenate(ps, axis=1), sinks


def _band_operands(before_k, this_k, before_v, this_v, kvh):
    kblk = jnp.concatenate([before_k[2 * kvh], this_k[2 * kvh], before_k[2 * kvh + 1], this_k[2 * kvh + 1]], axis=0)
    vblk = jnp.concatenate([before_v[2 * kvh], this_v[2 * kvh], before_v[2 * kvh + 1], this_v[2 * kvh + 1]], axis=0)
    return kblk, vblk


def _fwd_b(r0, target, wb, wkv, wob, post_g, post_b, sinks, bias):
    seq = r0.shape[0]
    tm = min(TOKENS_B, seq)
    n_chunks = tm // CHUNK
    part_rows = min(PART_ROWS_B, tm)
    n_parts = tm // part_rows
    part_chunks = part_rows // CHUNK
    shard_cols = wb.shape[2]

    def body(r0_ref, tgt_ref, wb_ref, wkv_ref, wob_ref, pg_ref, pb_ref, sink_ref, bias_ref,
             q_ref, z_ref, kv_ref, dr1_ref, loss_ref, dg_ref, db_ref, ek_last, ev_last, *scratch):
        q_s, z_s, h1_s, o_s, ek_s, ev_s = (scratch[i * n_parts:(i + 1) * n_parts] for i in range(6))
        step = pl.program_id(0)

        @pl.when(step == 0)
        def _():
            ek_last[...] = jnp.zeros_like(ek_last)
            ev_last[...] = jnp.zeros_like(ev_last)
            loss_ref[...] = jnp.zeros_like(loss_ref)
            dg_ref[...] = jnp.zeros_like(dg_ref)
            db_ref[...] = jnp.zeros_like(db_ref)

        sink_cols = _sink_columns(sink_ref)
        g1 = pg_ref[1:2, :]

        def project_steps(part):
            rows = slice(part * part_rows, (part + 1) * part_rows)
            live = {}

            def normalize():
                xhat0, _ = _ln_normalize(r0_ref[rows, :])
                h1 = xhat0 * pg_ref[0:1, :] + pb_ref[0:1, :]
                h1_s[part][...] = h1
                live["h1b"] = h1.astype(BF16)

            def shard(k):
                def run():
                    qz = _nn(live["h1b"], wb_ref[k])
                    lo, hi = k * shard_cols, (k + 1) * shard_cols
                    if hi <= B_WIDTH:
                        qs = (qz * SCALE).astype(BF16)
                        q_ref[rows, lo:hi] = qs
                        q_s[part][:, lo:hi] = qs
                    else:
                        z_ref[rows, lo - B_WIDTH:hi - B_WIDTH] = qz
                        z_s[part][:, lo - B_WIDTH:hi - B_WIDTH] = qz
                return run

            def keys():
                kv = _nn(live["h1b"], wkv_ref[...])
                kv_ref[rows, :] = kv.astype(BF16)
                for cj in range(part_chunks):
                    sub = slice(cj * CHUNK, (cj + 1) * CHUNK)
                    for i, e in enumerate(_expand_heads(kv[sub, :KV_WIDTH])):
                        ek_s[part][cj, i] = e
                    for i, e in enumerate(_expand_heads(kv[sub, KV_WIDTH:])):
                        ev_s[part][cj, i] = e

            return [normalize, keys] + [shard(k) for k in range(N_DEV)]

        def attend_steps(part):
            def stack(cj, kvh):
                def run():
                    sub = slice(cj * CHUNK, (cj + 1) * CHUNK)
                    sel = jnp.where(step * n_chunks + part * part_chunks + cj > 0, 1, 0)
                    if cj:
                        before_k, before_v = ek_s[part].at[cj - 1], ev_s[part].at[cj - 1]
                    elif part:
                        before_k, before_v = ek_s[part - 1].at[part_chunks - 1], ev_s[part - 1].at[part_chunks - 1]
                    else:
                        before_k, before_v = ek_last, ev_last
                    kblk, vblk = _band_operands(before_k, ek_s[part].at[cj], before_v, ev_s[part].at[cj], kvh)
                    logits = _nt(_stack_pairs(q_s[part], sub, kvh), kblk) + bias_ref[sel, kvh]
                    p, _ = _softmax_stack(logits, sink_cols[kvh])
                    o = _nn(p.astype(BF16), vblk)
                    for pp in range(PAIRS_PER_KV):
                        o_s[part][sub, _pair_cols(kvh, pp)] = o[pp * CHUNK:(pp + 1) * CHUNK]
                return run

            def finish():
                rows = slice(part * part_rows, (part + 1) * part_rows)
                z = z_s[part][...]
                y = o_s[part][...] * (z * jax.nn.sigmoid(z))
                r1 = ALPHA * h1_s[part][...] + _nn(y.astype(BF16), wob_ref[...])
                xhat1, rstd1 = _ln_normalize(r1)
                diff = xhat1 * g1 + pb_ref[1:2, :] - tgt_ref[rows, :]
                loss_ref[...] += 0.5 * jnp.sum(jnp.mean(diff * diff, axis=-1, keepdims=True), axis=0, keepdims=True)
                dh2 = diff * (1.0 / D_MODEL)
                dg_ref[...] += jnp.sum(dh2 * xhat1, axis=0, keepdims=True)
                db_ref[...] += jnp.sum(dh2, axis=0, keepdims=True)
                dr1_ref[rows, :] = _ln_backward(dh2 * g1, xhat1, rstd1)

            return [stack(cj, kvh) for cj in range(part_chunks) for kvh in range(N_KV_HEADS)] + [finish]

        for run in project_steps(0):
            run()
        for part in range(n_parts):
            for run in _in_turns(attend_steps(part), project_steps(part + 1) if part + 1 < n_parts else []):
                run()
        ek_last[...] = ek_s[n_parts - 1][part_chunks - 1]
        ev_last[...] = ev_s[n_parts - 1][part_chunks - 1]

    vec = pl.BlockSpec((1, D_MODEL), lambda i: (0, 0))
    return pl.pallas_call(
        body, name="fwd_b", grid=(seq // tm,),
        out_shape=[jax.ShapeDtypeStruct((seq, B_WIDTH), BF16), jax.ShapeDtypeStruct((seq, B_WIDTH), F32),
                   jax.ShapeDtypeStruct((seq, 2 * KV_WIDTH), BF16), jax.ShapeDtypeStruct((seq, D_MODEL), F32),
                   jax.ShapeDtypeStruct((1, 1), F32), jax.ShapeDtypeStruct((1, D_MODEL), F32),
                   jax.ShapeDtypeStruct((1, D_MODEL), F32)],
        in_specs=[_rows(tm, D_MODEL), _rows(tm, D_MODEL), _resident(wb.shape), _resident(wkv.shape),
                  _resident(wob.shape), _resident(post_g.shape), _resident(post_b.shape), SMEM,
                  _resident(bias.shape)],
        out_specs=[_rows(tm, B_WIDTH), _rows(tm, B_WIDTH), _rows(tm, 2 * KV_WIDTH), _rows(tm, D_MODEL),
                   pl.BlockSpec((1, 1), lambda i: (0, 0)), vec, vec],
        scratch_shapes=[pltpu.VMEM((4, CHUNK, CHUNK), BF16), pltpu.VMEM((4, CHUNK, CHUNK), BF16)]
        + [pltpu.VMEM((part_rows, B_WIDTH), BF16)] * n_parts + [pltpu.VMEM((part_rows, B_WIDTH), F32)] * n_parts
        + [pltpu.VMEM((part_rows, D_MODEL), F32)] * n_parts + [pltpu.VMEM((part_rows, B_WIDTH), F32)] * n_parts
        + [pltpu.VMEM((part_chunks, 4, CHUNK, CHUNK), BF16)] * (2 * n_parts),
        compiler_params=_seq_params(),
    )(r0, target, wb, wkv, wob, post_g, post_b, sinks, bias)


def _bwd_b(r0, dr1, q, z, kv, wb, wkv, wob, post_g, post_b, sinks, bias):
    seq = r0.shape[0]
    tm = min(TOKENS_B_BWD, seq)
    n_chunks = tm // CHUNK
    n_blocks = seq // tm
    shard_cols = wb.shape[2]

    def body(r0_ref, dr1_ref, q_ref, z_ref, kv_ref, kvp_ref, wb_ref, wkv_ref, wob_ref, pg_ref, pb_ref, sink_ref,
             bias_ref, dr0_ref, dwb_out, dwkv_out, dwob_out, dsink_ref, dl_out, dpost_ref,
             dwb_acc, dwkv_acc, dwob_acc, dl_acc, dsink_acc, carry_ref, ek_ref, ev_ref, y_ref, dqz_ref, dkv_ref):
        step = pl.program_id(0)
        block = n_blocks - 1 - step

        @pl.when(step == 0)
        def _():
            dpost_ref[...] = jnp.zeros_like(dpost_ref)
            dwb_acc[...] = jnp.zeros_like(dwb_acc)
            dwkv_acc[...] = jnp.zeros_like(dwkv_acc)
            dwob_acc[...] = jnp.zeros_like(dwob_acc)
            dl_acc[...] = jnp.zeros_like(dl_acc)
            carry_ref[...] = jnp.zeros_like(carry_ref)
            dsink_acc[...] = jnp.zeros_like(dsink_acc)

        xhat0, rstd0 = _ln_normalize(r0_ref[...])
        h1b = (xhat0 * pg_ref[0:1, :] + pb_ref[0:1, :]).astype(BF16)
        dr1 = dr1_ref[...]
        dr1b = dr1.astype(BF16)
        dy = _nt(dr1b, wob_ref[...])
        sg, dsg = _silu_and_grad(z_ref[...])
        do_all = dy * sg
        dyz = dy * dsg

        for ci in range(n_chunks + 1):
            src = kvp_ref[...] if ci == 0 else kv_ref[(ci - 1) * CHUNK:ci * CHUNK, :]
            src = src.astype(F32)
            for i, e in enumerate(_expand_heads(src[:, :KV_WIDTH])):
                ek_ref[ci, i] = e
            for i, e in enumerate(_expand_heads(src[:, KV_WIDTH:])):
                ev_ref[ci, i] = e

        sink_cols = _sink_columns(sink_ref)
        for ci in reversed(range(n_chunks)):
            rows = slice(ci * CHUNK, (ci + 1) * CHUNK)
            sel = jnp.where(block * n_chunks + ci > 0, 1, 0)
            dk_band = jnp.zeros((2 * CHUNK, KV_WIDTH), F32)
            dv_band = jnp.zeros((2 * CHUNK, KV_WIDTH), F32)
            for kvh in range(N_KV_HEADS):
                kblk, vblk = _band_operands(ek_ref.at[ci], ek_ref.at[ci + 1], ev_ref.at[ci], ev_ref.at[ci + 1], kvh)
                qs = _stack_pairs(q_ref, rows, kvh)
                p, psink = _softmax_stack(_nt(qs, kblk) + bias_ref[sel, kvh], sink_cols[kvh])
                pb16 = p.astype(BF16)
                o = _nn(pb16, vblk)
                y = (o * _stack_pairs(sg, rows, kvh)).astype(BF16)
                dz = (o * _stack_pairs(dyz, rows, kvh)).astype(BF16)
                dob = _stack_pairs(do_all, rows, kvh).astype(BF16)
                dp = _nt(dob, vblk)
                dlogits = []
                for half in range(2):
                    hs = slice(half * 2 * CHUNK, (half + 1) * 2 * CHUNK)
                    delta = jnp.sum(dp[:, hs] * p[:, hs], axis=-1, keepdims=True)
                    dlogits.append(p[:, hs] * (dp[:, hs] - delta))
                    dsink_acc[2 * kvh + half] += psink[half] * delta
                dlog = jnp.concatenate(dlogits, axis=1)
                dl_acc[kvh] += dlog
                dsc = dlog.astype(BF16)
                dq = (_nn(dsc, kblk) * SCALE).astype(BF16)
                for pp in range(PAIRS_PER_KV):
                    blk = slice(pp * CHUNK, (pp + 1) * CHUNK)
                    cols = _pair_cols(kvh, pp)
                    y_ref[rows, cols] = y[blk]
                    dqz_ref[rows, cols] = dq[blk]
                    dqz_ref[rows, B_WIDTH + cols.start:B_WIDTH + cols.stop] = dz[blk]
                dk_band = dk_band + _fold_heads(_tn(dsc, qs), kvh)
                dv_band = dv_band + _fold_heads(_tn(pb16, dob), kvh)
            band = jnp.concatenate([dk_band, dv_band], axis=1)
            dkv_ref[rows, :] = (band[CHUNK:] + carry_ref[...]).astype(BF16)
            carry_ref[...] = band[:CHUNK]

        dqz = dqz_ref[...]
        dkv = dkv_ref[...]
        dh1 = ALPHA * dr1 + _nt(dkv, wkv_ref[...])
        for k in range(N_DEV):
            part = dqz[:, k * shard_cols:(k + 1) * shard_cols]
            dh1 = dh1 + _nt(part, wb_ref[k])
            dwb_acc[k] += _tn(h1b, part)
        dpost_ref[0:1, :] += jnp.sum(dh1 * xhat0, axis=0, keepdims=True)
        dpost_ref[1:2, :] += jnp.sum(dh1, axis=0, keepdims=True)
        dr0_ref[...] = _ln_backward(dh1 * pg_ref[0:1, :], xhat0, rstd0)
        dwkv_acc[...] += _tn(h1b, dkv)
        dwob_acc[...] += _tn(y_ref[...], dr1b)

        @pl.when(step == n_blocks - 1)
        def _():
            pltpu.sync_copy(dwb_acc, dwb_out)
            pltpu.sync_copy(dwkv_acc, dwkv_out)
            pltpu.sync_copy(dwob_acc, dwob_out)
            pltpu.sync_copy(dl_acc, dl_out)
            lane = lax.broadcasted_iota(jnp.int32, (1, CHUNK), 1)
            dsink = jnp.zeros((1, CHUNK), F32)
            for kvh in range(N_KV_HEADS):
                for half in range(2):
                    for pp in range(PAIRS_PER_KV):
                        head = 2 * (kvh * PAIRS_PER_KV + pp) + half
                        tot = jnp.sum(dsink_acc[2 * kvh + half, pp * CHUNK:(pp + 1) * CHUNK, :], axis=0, keepdims=True)
                        dsink = dsink - jnp.where(lane == head, tot, 0.0)
            dsink_ref[...] = dsink

    rev = functools.partial(_rows, tm, reverse_of=n_blocks)
    prev_chunk = pl.BlockSpec(
        (CHUNK, 2 * KV_WIDTH), lambda i: (jnp.maximum((n_blocks - 1 - i) * n_chunks - 1, 0), 0))
    return pl.pallas_call(
        body, name="bwd_b", grid=(n_blocks,),
        out_shape=[jax.ShapeDtypeStruct((seq, D_MODEL), F32), jax.ShapeDtypeStruct(wb.shape, F32),
                   jax.ShapeDtypeStruct(wkv.shape, F32), jax.ShapeDtypeStruct(wob.shape, F32),
                   jax.ShapeDtypeStruct((1, CHUNK), F32),
                   jax.ShapeDtypeStruct((N_KV_HEADS, STACK, 4 * CHUNK), F32), jax.ShapeDtypeStruct((2, D_MODEL), F32)],
        in_specs=[rev(D_MODEL), rev(D_MODEL), rev(B_WIDTH), rev(B_WIDTH), rev(2 * KV_WIDTH), prev_chunk,
                  _resident(wb.shape), _resident(wkv.shape), _resident(wob.shape), _resident(post_g.shape),
                  _resident(post_b.shape), SMEM, _resident(bias.shape)],
        out_specs=[rev(D_MODEL), ANY, ANY, ANY, pl.BlockSpec((1, CHUNK), lambda i: (0, 0)), ANY,
                   pl.BlockSpec((2, D_MODEL), lambda i: (0, 0))],
        scratch_shapes=[pltpu.VMEM(wb.shape, F32), pltpu.VMEM(wkv.shape, F32), pltpu.VMEM(wob.shape, F32),
                        pltpu.VMEM((N_KV_HEADS, STACK, 4 * CHUNK), F32), pltpu.VMEM((2 * N_KV_HEADS, STACK, 1), F32),
                        pltpu.VMEM((CHUNK, 2 * KV_WIDTH), F32),
                        pltpu.VMEM((n_chunks + 1, 4, CHUNK, CHUNK), BF16),
                        pltpu.VMEM((n_chunks + 1, 4, CHUNK, CHUNK), BF16),
                        pltpu.VMEM((tm, B_WIDTH), BF16), pltpu.VMEM((tm, 2 * B_WIDTH), BF16),
                        pltpu.VMEM((tm, 2 * KV_WIDTH), BF16)],
        compiler_params=_seq_params(),
    )(r0, dr1, q, z, kv, kv, wb, wkv, wob, post_g, post_b, sinks, bias)


def _bwd_a(dr0, p, ln_gb, w_spatial, bsp_t, woa, exchange):
    seq = dr0.shape[0]
    tm = min(TOKENS_A, seq)
    n_chunks = tm // CHUNK
    n_blocks = seq // tm

    def body(dr0_ref, p_ref, ln_ref, ws_ref, bsp_ref, woa_ref, *refs):
        ex_in, refs = refs[:exchange.n], refs[exchange.n:]
        (dp_ref, dwoa_out, dws_ref, dbsp_ref, dln_ref), refs = refs[:5], refs[5:]
        ex_out, refs = refs[:exchange.n], refs[exchange.n:]
        (dwoa_acc, s_ref, ds_ref, dvn_ref), ex_sems = refs[:4], refs[4:]
        step = pl.program_id(0)

        @pl.when(step == 0)
        def _():
            exchange.start(ex_in, ex_out, ex_sems)
            dwoa_acc[...] = jnp.zeros_like(dwoa_acc)
            dws_ref[...] = jnp.zeros_like(dws_ref)
            dbsp_ref[...] = jnp.zeros_like(dbsp_ref)
            dln_ref[...] = jnp.zeros_like(dln_ref)

        dr0b = dr0_ref[...].astype(BF16)

        vhat, rstd_v = _ln_normalize(p_ref[:, A_WIDTH:2 * A_WIDTH])
        ln_g = ln_ref[0:1, :]
        vn = (vhat * ln_g + ln_ref[1:2, :]).astype(BF16)
        _spatial_gate(ws_ref, bsp_ref, vn, s_ref, n_chunks)
        u = p_ref[:, :A_WIDTH]
        s = s_ref[...]
        sg, dsg = _silu_and_grad(p_ref[:, 2 * A_WIDTH:])
        gate = s * sg
        dwoa_acc[...] += _tn((u * gate).astype(BF16), dr0b)
        dy = _nt(dr0b, woa_ref[...])
        dyu = dy * u
        dp_ref[:, :A_WIDTH] = (dy * gate).astype(BF16)
        dp_ref[:, 2 * A_WIDTH:] = (dyu * (s * dsg)).astype(BF16)
        ds_ref[...] = dyu * sg

        lane = lax.broadcasted_iota(jnp.int32, (CHUNK, CHUNK), 1)
        dbsp = jnp.zeros((CHUNK, CHUNK), F32)
        for g in range(A_GROUPS):
            w = _masked_spatial(ws_ref, g)
            cols = slice(g * A_GROUP_DIM, (g + 1) * A_GROUP_DIM)
            dws = jnp.zeros((CHUNK, CHUNK), F32)
            for ci in range(n_chunks):
                rows = slice(ci * CHUNK, (ci + 1) * CHUNK)
                ds = ds_ref[rows, cols]
                dsb = ds.astype(BF16)
                dvn_ref[rows, cols] = _tn(w, dsb)
                dws = dws + _nt(dsb, vn[rows, cols])
                dbsp = dbsp + jnp.where(lane == g, jnp.sum(ds, axis=-1, keepdims=True), 0.0)
            dws_ref[g] += dws
        dbsp_ref[...] += dbsp

        dvn = dvn_ref[...]
        dln_ref[0:1, :] += jnp.sum(dvn * vhat, axis=0, keepdims=True)
        dln_ref[1:2, :] += jnp.sum(dvn, axis=0, keepdims=True)
        dp_ref[:, A_WIDTH:2 * A_WIDTH] = _ln_backward(dvn * ln_g, vhat, rstd_v).astype(BF16)

        @pl.when(step == n_blocks - 1)
        def _():
            t = lax.broadcasted_iota(jnp.int32, (CHUNK, CHUNK), 0)
            for g in range(A_GROUPS):
                dws_ref[g] = jnp.where(lane <= t, dws_ref[g], 0.0)
            pltpu.sync_copy(dwoa_acc, dwoa_out)
            exchange.wait(ex_in, ex_out, ex_sems)

    def fixed(shape):
        nd = len(shape)
        return pl.BlockSpec(shape, lambda i: (0,) * nd)

    return pl.pallas_call(
        body, name="bwd_a", grid=(n_blocks,),
        out_shape=[jax.ShapeDtypeStruct((seq, 3 * A_WIDTH), BF16),
                   jax.ShapeDtypeStruct(woa.shape, F32), jax.ShapeDtypeStruct(w_spatial.shape, F32),
                   jax.ShapeDtypeStruct((CHUNK, CHUNK), F32), jax.ShapeDtypeStruct((2, A_WIDTH), F32)]
        + exchange.out_shape,
        in_specs=[_rows(tm, D_MODEL), _rows(tm, 3 * A_WIDTH), _resident(ln_gb.shape),
                  _resident(w_spatial.shape), _resident(bsp_t.shape), _resident(woa.shape)] + [ANY] * exchange.n,
        out_specs=[_rows(tm, 3 * A_WIDTH), ANY, fixed(w_spatial.shape),
                   fixed((CHUNK, CHUNK)), fixed((2, A_WIDTH))] + [ANY] * exchange.n,
        scratch_shapes=[pltpu.VMEM(woa.shape, F32), pltpu.VMEM((tm, A_WIDTH), F32),
                        pltpu.VMEM((tm, A_WIDTH), F32), pltpu.VMEM((tm, A_WIDTH), F32)] + exchange.scratch_shapes,
        compiler_params=_seq_params(),
    )(dr0, p, ln_gb, w_spatial, bsp_t, woa, *exchange.inputs)


def _input_grad_blocks(seq):
    tm = min(TOKENS_DX, seq // 8)
    return tm, seq // tm


def _input_grad(dp, dr0, wa, first, n_blocks, earlier, exchange):
    seq = dr0.shape[0]
    tm, _ = _input_grad_blocks(seq)
    shard_cols = wa.shape[2]
    n_own = 3 if earlier is None else 4

    def body(dp_ref, dr0_ref, wa_ref, *refs):
        refs = refs[n_own - 3:]
        ex_in, dx_ref, ex_out, ex_sems = (refs[:exchange.n], refs[exchange.n], refs[exchange.n + 1:2 * exchange.n + 1],
                                          refs[2 * exchange.n + 1:])
        step = pl.program_id(0)

        @pl.when(step == 0)
        def _():
            exchange.start(ex_in, ex_out, ex_sems)

        dx = ALPHA * dr0_ref[...]
        for k in range(N_DEV):
            dx = dx + _nt(dp_ref[:, k * shard_cols:(k + 1) * shard_cols], wa_ref[k])
        dx_ref[...] = dx

        @pl.when(step == n_blocks - 1)
        def _():
            exchange.wait(ex_in, ex_out, ex_sems)

    def rows(width):
        return pl.BlockSpec((tm, width), lambda i: (first + i, 0))

    return pl.pallas_call(
        body, name=f"input_grad_{first}", grid=(n_blocks,),
        out_shape=[jax.ShapeDtypeStruct((seq, D_MODEL), F32)] + exchange.out_shape,
        in_specs=[rows(3 * A_WIDTH), rows(D_MODEL), _resident(wa.shape)] + [ANY] * (n_own - 3 + exchange.n),
        out_specs=[rows(D_MODEL)] + [ANY] * exchange.n,
        scratch_shapes=exchange.scratch_shapes,
        input_output_aliases={} if earlier is None else {3: 0},
        compiler_params=_seq_params(),
    )(dp, dr0, wa, *(() if earlier is None else (earlier,)), *exchange.inputs)


def _in_proj_weight_grad(xb, dp, shard_cols, exchange):
    seq = xb.shape[0]
    tk = min(TOKENS_WGRAD, seq)
    n_k = seq // tk

    def body(x_ref, dp_ref, *refs):
        ex_in, out_ref, ex_out, ex_sems = (refs[:exchange.n], refs[exchange.n], refs[exchange.n + 1:2 * exchange.n + 1],
                                           refs[2 * exchange.n + 1:])
        shard, step = pl.program_id(0), pl.program_id(1)

        @pl.when((shard == 0) & (step == 0))
        def _():
            exchange.start(ex_in, ex_out, ex_sems)

        @pl.when(step == 0)
        def _():
            out_ref[...] = jnp.zeros_like(out_ref)

        out_ref[...] += _tn(x_ref[...], dp_ref[...])

        @pl.when((shard == N_DEV - 1) & (step == n_k - 1))
        def _():
            exchange.wait(ex_in, ex_out, ex_sems)

    return pl.pallas_call(
        body, name="in_proj_weight_grad", grid=(N_DEV, n_k),
        out_shape=[jax.ShapeDtypeStruct((N_DEV, D_MODEL, shard_cols), F32)] + exchange.out_shape,
        in_specs=[pl.BlockSpec((tk, D_MODEL), lambda j, k: (k, 0)),
                  pl.BlockSpec((tk, shard_cols), lambda j, k: (k, j))] + [ANY] * exchange.n,
        out_specs=[pl.BlockSpec((None, D_MODEL, shard_cols), lambda j, k: (j, 0, 0))] + [ANY] * exchange.n,
        scratch_shapes=exchange.scratch_shapes,
        compiler_params=pltpu.CompilerParams(dimension_semantics=("arbitrary", "arbitrary"),
                                             vmem_limit_bytes=VMEM_LIMIT_BYTES),
    )(xb, dp, *exchange.inputs)


class _Exchange:
    def __init__(self, pair=(), chip=()):
        self.pair, self.chip = tuple(pair), tuple(chip)
        self.inputs = self.pair + self.chip
        self.n = len(self.inputs)
        self.out_shape = [jax.ShapeDtypeStruct((4, 1) + g.shape[2:], F32) for g in self.pair]
        self.out_shape += [jax.ShapeDtypeStruct((3,) + s.shape[1:], s.dtype) for s in self.chip]
        self.scratch_shapes = [pltpu.SemaphoreType.DMA((max(len(self.pair), 1),)) for _ in range(2)]
        self.scratch_shapes += [pltpu.SemaphoreType.DMA((max(len(self.chip), 1), 3)) for _ in range(2)]

    def _copies(self, ins, outs, sems):
        if not self.n:
            return []
        pair_send, pair_recv, chip_send, chip_recv = sems
        x, y, c = lax.axis_index("x"), lax.axis_index("y"), lax.axis_index("c")
        copies = []
        for a in range(len(self.pair)):
            copies.append(pltpu.make_async_remote_copy(
                src_ref=ins[a].at[:, pl.ds(1 - c, 1)], dst_ref=outs[a],
                send_sem=pair_send.at[a], recv_sem=pair_recv.at[a], device_id=(x, y, 1 - c), device_id_type=MESH))
        for a in range(len(self.chip)):
            src, dst = ins[len(self.pair) + a], outs[len(self.pair) + a]
            for k, (px, py) in enumerate([(1 - x, y), (x, 1 - y), (1 - x, 1 - y)]):
                copies.append(pltpu.make_async_remote_copy(
                    src_ref=src.at[2 * px + py], dst_ref=dst.at[k], send_sem=chip_send.at[a, k],
                    recv_sem=chip_recv.at[a, k], device_id=(px, py, c), device_id_type=MESH))
        return copies

    def start(self, ins, outs, sems):
        for cp in self._copies(ins, outs, sems):
            cp.start()

    def wait(self, ins, outs, sems):
        for cp in self._copies(ins, outs, sems):
            cp.wait()


def _row_tile(rows, cols, itemsize=4):
    if rows * cols * itemsize <= (1 << 20):
        return rows
    t = 8
    while 2 * t * cols * itemsize <= (1 << 20) and rows % (2 * t) == 0:
        t *= 2
    assert rows % t == 0, (rows, cols)
    return t


def _pair_sum(place, grads, landeds, dtypes):
    n = len(grads)

    def body(place_ref, *refs):
        for a in range(n):
            refs[2 * n + a][...] = (refs[2 * a][...] + refs[2 * a + 1][...]).astype(dtypes[a])

    in_specs, out_specs, operands = [], [], []
    for g, l in zip(grads, landeds):
        rows, cols = g.shape[2:]
        in_specs += [pl.BlockSpec((None, None, rows, cols), lambda k, pr: (k, pr[1], 0, 0)),
                     pl.BlockSpec((None, None, rows, cols), lambda k, pr: (k, 0, 0, 0))]
        out_specs.append(pl.BlockSpec((None, rows, cols), lambda k, pr: (k, 0, 0)))
        operands += [g, l]
    return pl.pallas_call(
        body, name="pair_sum",
        out_shape=[jax.ShapeDtypeStruct((4,) + g.shape[2:], dt) for g, dt in zip(grads, dtypes)],
        grid_spec=pltpu.PrefetchScalarGridSpec(num_scalar_prefetch=1, grid=(4,), in_specs=in_specs,
                                               out_specs=out_specs),
        compiler_params=pltpu.CompilerParams(dimension_semantics=("parallel",), vmem_limit_bytes=VMEM_LIMIT_BYTES),
    )(place, *operands)


def _adamw_math(w, g, m, v):
    m = ADAM_B1 * m + (1.0 - ADAM_B1) * g
    v = ADAM_B2 * v + (1.0 - ADAM_B2) * (g * g)
    m_hat = m / (1.0 - ADAM_B1 ** ADAM_STEP)
    v_hat = v / (1.0 - ADAM_B2 ** ADAM_STEP)
    delta = -ADAM_LR * (m_hat / (jnp.sqrt(v_hat) + ADAM_EPS) + ADAM_WD * w)
    return delta, m, v


def _adamw_shard(place, grad, landed, got, w, m, v):
    rows, cols = w.shape
    tr = _row_tile(rows, cols)

    def body(place_ref, a_ref, b_ref, got_ref, w_ref, m_ref, v_ref, g_out, d_out, m_out, v_out):
        g = a_ref[...] + b_ref[...]
        for k in range(3):
            g = g + got_ref[k].astype(F32)
        g_out[...] = g
        d_out[...], m_out[...], v_out[...] = _adamw_math(w_ref[...], g, m_ref[...], v_ref[...])

    spec = pl.BlockSpec((tr, cols), lambda i, pr: (i, 0))
    out = jax.ShapeDtypeStruct((rows, cols), F32)
    return pl.pallas_call(
        body, name="adamw_shard", out_shape=[out] * 4,
        grid_spec=pltpu.PrefetchScalarGridSpec(
            num_scalar_prefetch=1, grid=(rows // tr,),
            in_specs=[pl.BlockSpec((None, None, tr, cols), lambda i, pr: (pr[0], pr[1], i, 0)),
                      pl.BlockSpec((None, None, tr, cols), lambda i, pr: (pr[0], 0, i, 0)),
                      pl.BlockSpec((3, tr, cols), lambda i, pr: (0, i, 0)), spec, spec, spec],
            out_specs=[spec] * 4),
        compiler_params=pltpu.CompilerParams(dimension_semantics=("parallel",)),
    )(place, grad, landed, got, w, m, v)


def _gather_small(grad, landed, got):
    rows, cols = got.shape[1:]

    def body(grad_ref, landed_ref, got_ref, out_ref, send_sems, recv_sems):
        x, y, c = lax.axis_index("x"), lax.axis_index("y"), lax.axis_index("c")
        me, sibling = (x, y, c), (x, y, 1 - c)
        chips = [(1 - x, y), (x, 1 - y), (1 - x, 1 - y)]

        def slot(px, py, pc):
            return out_ref.at[4 * px + 2 * py + pc]

        def copy(k, block, to):
            return pltpu.make_async_remote_copy(
                src_ref=slot(*block), dst_ref=slot(*block), send_sem=send_sems.at[k], recv_sem=recv_sems.at[k],
                device_id=to, device_id_type=MESH)

        mine = grad_ref[2 * x + y, c] + landed_ref[2 * x + y, 0]
        out_ref[4 * x + 2 * y + c] = ((mine + got_ref[0]) + got_ref[1]) + got_ref[2]
        first = [copy(0, me, sibling)] + [copy(1 + j, me, (*chip, c)) for j, chip in enumerate(chips)]
        for cp in first:
            cp.start()
        passed = [copy(4 + j, (*chip, c), sibling) for j, chip in enumerate(chips)]
        for j, chip in enumerate(chips):
            copy(1 + j, (*chip, c), me).wait_recv()
            passed[j].start()
        copy(0, sibling, me).wait_recv()
        for j, chip in enumerate(chips):
            copy(4 + j, (*chip, 1 - c), me).wait_recv()
        for cp in first + passed:
            cp.wait_send()

    return pl.pallas_call(
        body, name="gather_small", out_shape=jax.ShapeDtypeStruct((N_DEV, rows, cols), F32),
        in_specs=[VMEM] * 3, out_specs=VMEM,
        scratch_shapes=[pltpu.SemaphoreType.DMA((7,)), pltpu.SemaphoreType.DMA((7,))],
    )(grad, landed, got)


def _adamw_small(params):
    n = len(params)

    def body(*refs):
        ins, outs = refs[:4 * n], refs[4 * n:]
        for i in range(n):
            w_ref, g_ref, m_ref, v_ref = ins[4 * i:4 * i + 4]
            d, m, v = _adamw_math(w_ref[...], g_ref[...], m_ref[...], v_ref[...])
            outs[3 * i][...] = d
            outs[3 * i + 1][...] = m
            outs[3 * i + 2][...] = v

    flat = [a for p in params for a in p]
    out_shape = [jax.ShapeDtypeStruct(p[0].shape, F32) for p in params for _ in range(3)]
    outs = pl.pallas_call(
        body, name="adamw_small", out_shape=out_shape, in_specs=[VMEM] * (4 * n), out_specs=[VMEM] * (3 * n),
    )(*flat)
    return [tuple(outs[3 * i:3 * i + 3]) for i in range(n)]


def _pack_small(parts):
    rows = []
    for a in parts:
        r, c = a.shape
        rows.append(jnp.pad(a, ((0, (-r) % 8), (0, CHUNK - c))))
    flat = jnp.concatenate(rows, axis=0)
    return jnp.pad(flat, ((0, N_DEV * SMALL_ROWS - flat.shape[0]), (0, 0))).reshape(N_DEV, SMALL_ROWS, CHUNK)


def _unpack_small(flat, shapes):
    out, at = [], 0
    for r, c in shapes:
        out.append(flat[at:at + r, :c])
        at += r + (-r) % 8
    return out


def kernel(x, w_in_a, sgu_ln_g, sgu_ln_b, w_spatial, b_spatial, w_out_a, w_kv, w_in_b, attn_sinks, rel_bias, w_out_b, post_ln_g, post_ln_b, loss_target, m_w_in_a, m_sgu_ln_g, m_sgu_ln_b, m_w_spatial, m_b_spatial, m_w_out_a, m_w_kv, m_w_in_b, m_attn_sinks, m_rel_bias, m_w_out_b, m_post_ln_g, m_post_ln_b, v_w_in_a, v_sgu_ln_g, v_sgu_ln_b, v_w_spatial, v_b_spatial, v_w_out_a, v_w_kv, v_w_in_b, v_attn_sinks, v_rel_bias, v_w_out_b, v_post_ln_g, v_post_ln_b):
    seq = x.shape[1]
    me = 4 * lax.axis_index("x") + 2 * lax.axis_index("y") + lax.axis_index("c")
    bucket = jnp.asarray(_band_tables())

    ln_shard = jnp.concatenate([sgu_ln_g, sgu_ln_b], axis=0)
    x2, tgt = x[0], loss_target[0]
    order = jnp.stack([2 * px + py for px, py, _ in _use_order()[::2]]).astype(jnp.int32)
    p, wa, woa, ln_all, xb = _gather_in_proj(order, x2, w_in_a[0], (w_out_a[0],), ln_shard)
    woa = woa.reshape(A_WIDTH, D_MODEL)
    ln_gb = jnp.transpose(ln_all, (1, 0, 2)).reshape(2, A_WIDTH)
    ws = w_spatial[0]
    bsp_t = jnp.pad(b_spatial[0].T, ((0, 0), (0, CHUNK - A_GROUPS)))
    bias = _bias_table(rel_bias, bucket)

    r0, wkv, wb, wob = _fwd_a(x2, p, ln_gb, ws, bsp_t, woa, (w_kv, w_in_b[0], w_out_b[0]))
    wkv = wkv.reshape(D_MODEL, 2 * KV_WIDTH)
    wob = wob.reshape(B_WIDTH, D_MODEL)
    q, z, kv, dr1, loss, dg1, db1 = _fwd_b(r0, tgt, wb, wkv, wob, post_ln_g, post_ln_b, attn_sinks, bias)
    dr0, dwb, dwkv, dwob, dsink, dlog, dpost0 = _bwd_b(r0, dr1, q, z, kv, wb, wkv, wob, post_ln_g, post_ln_b,
                                                       attn_sinks, bias)
    drel = _rel_bias_grad(dlog, bucket)

    place = jnp.stack([2 * lax.axis_index("x") + lax.axis_index("y"), lax.axis_index("c")]).astype(jnp.int32)

    def by_device(g):
        return g.reshape((4, 2, g.shape[0] // N_DEV) + g.shape[1:]) if g.ndim == 2 else g.reshape((4, 2) + g.shape[1:])

    dwb, dwkv, dwob = by_device(dwb), by_device(dwkv), by_device(dwob)
    dp, dwoa, dws, dbsp_t, dln, l_wb, l_wkv, l_wob = _bwd_a(dr0, p, ln_gb, ws, bsp_t, woa,
                                                            _Exchange(pair=(dwb, dwkv, dwob)))
    dwoa = by_device(dwoa)
    small_shapes = [(A_GROUPS * CHUNK, CHUNK), (A_GROUPS, CHUNK), (1, N_Q_HEADS), (REL_BUCKETS, N_Q_HEADS),
                    (16, CHUNK), (16, CHUNK), (16, CHUNK), (16, CHUNK), (1, 1)]
    small = by_device(_pack_small([
        dws.reshape(A_GROUPS * CHUNK, CHUNK), dbsp_t[:, :A_GROUPS].T, dsink[:, :N_Q_HEADS], drel[:, :N_Q_HEADS],
        jnp.concatenate([dpost0[0:1], dg1], axis=0).reshape(16, CHUNK),
        jnp.concatenate([dpost0[1:2], db1], axis=0).reshape(16, CHUNK),
        dln[0].reshape(16, CHUNK), dln[1].reshape(16, CHUNK), loss]))
    s_wb, s_wkv, s_wob = _pair_sum(place, (dwb, dwkv, dwob), (l_wb, l_wkv, l_wob), (BF16,) * 3)
    dwa, l_woa, l_small, g_wb, g_wkv, g_wob = _in_proj_weight_grad(
        xb, dp, wa.shape[2], _Exchange(pair=(dwoa, small), chip=(s_wb, s_wkv, s_wob)))
    dwa = by_device(dwa)
    s_woa, s_small = _pair_sum(place, (dwoa, small), (l_woa, l_small), (BF16, F32))
    _, blocks = _input_grad_blocks(seq)
    cut = blocks // 4
    dx, l_wa, g_woa, g_small = _input_grad(dp, dr0, wa, 0, cut, None, _Exchange(pair=(dwa,), chip=(s_woa, s_small)))
    s_wa, = _pair_sum(place, (dwa,), (l_wa,), (BF16,))
    dx, g_wa = _input_grad(dp, dr0, wa, cut, blocks - cut, dx, _Exchange(chip=(s_wa,)))
    full = [dwa, dwoa, dwkv, dwb, dwob, small]
    landed = [l_wa, l_woa, l_wkv, l_wb, l_wob, l_small]
    got = [g_wa, g_woa, g_wkv, g_wb, g_wob, g_small]

    big = [(w_in_a, m_w_in_a, v_w_in_a), (w_out_a, m_w_out_a, v_w_out_a), (w_kv, m_w_kv, v_w_kv),
           (w_in_b, m_w_in_b, v_w_in_b), (w_out_b, m_w_out_b, v_w_out_b)]
    big_out = []
    for i, (w, m, v) in enumerate(big):
        shape2 = got[i].shape[1:]
        res = _adamw_shard(place, full[i], landed[i], got[i], w.reshape(shape2), m.reshape(shape2),
                           v.reshape(shape2))
        big_out.append([r.reshape(w.shape) for r in res])

    small_all = _gather_small(full[5], landed[5], got[5]).reshape(N_DEV * SMALL_ROWS, CHUNK)
    g_ws, g_bsp, g_sink, g_rel, g_pg, g_pb, g_lng, g_lnb, loss_all = _unpack_small(small_all, small_shapes)
    g_ws = g_ws.reshape(w_spatial.shape)
    g_bsp = g_bsp.reshape(b_spatial.shape)
    g_pg = g_pg.reshape(post_ln_g.shape)
    g_pb = g_pb.reshape(post_ln_b.shape)
    g_lng = lax.dynamic_slice(g_lng.reshape(1, A_WIDTH), (0, me * sgu_ln_g.shape[1]), sgu_ln_g.shape)
    g_lnb = lax.dynamic_slice(g_lnb.reshape(1, A_WIDTH), (0, me * sgu_ln_b.shape[1]), sgu_ln_b.shape)
    small_params = [(sgu_ln_g, g_lng, m_sgu_ln_g, v_sgu_ln_g), (sgu_ln_b, g_lnb, m_sgu_ln_b, v_sgu_ln_b),
                    (w_spatial, g_ws, m_w_spatial, v_w_spatial), (b_spatial, g_bsp, m_b_spatial, v_b_spatial),
                    (attn_sinks, g_sink, m_attn_sinks, v_attn_sinks), (rel_bias, g_rel, m_rel_bias, v_rel_bias),
                    (post_ln_g, g_pg, m_post_ln_g, v_post_ln_g), (post_ln_b, g_pb, m_post_ln_b, v_post_ln_b)]
    flat2 = [tuple(a.reshape(-1, a.shape[-1]) for a in p) for p in small_params]
    small_out = [tuple(r.reshape(p[0].shape) for r in res) for res, p in zip(_adamw_small(flat2), small_params)]

    order = [("big", 0), ("small", 0), ("small", 1), ("small", 2), ("small", 3), ("big", 1), ("big", 2), ("big", 3),
             ("small", 4), ("small", 5), ("big", 4), ("small", 6), ("small", 7)]
    grads, deltas, new_m, new_v = [], [], [], []
    for kind, i in order:
        if kind == "big":
            g, d, m, v = big_out[i]
        else:
            g = small_params[i][1]
            d, m, v = small_out[i]
        grads.append(g)
        deltas.append(d)
        new_m.append(m)
        new_v.append(v)
    return (loss_all[0, 0], dx.reshape(x.shape), *grads, *deltas, *new_m, *new_v)
```

```python
import functools

import jax
import jax.numpy as jnp
import numpy as np
from jax import lax
from jax.experimental import pallas as pl
from jax.experimental.pallas import tpu as pltpu

F32 = jnp.float32
BF16 = jnp.bfloat16

D_MODEL = 1024
A_WIDTH = 2048
A_GROUPS = 8
A_GROUP_DIM = A_WIDTH // A_GROUPS
CHUNK = 128
HEAD_DIM = 64
N_Q_HEADS = 16
N_KV_HEADS = 2
PAIRS_PER_KV = N_Q_HEADS // N_KV_HEADS // 2
B_WIDTH = N_Q_HEADS * HEAD_DIM
KV_WIDTH = N_KV_HEADS * HEAD_DIM
REL_BUCKETS = 32
REL_MAX_DIST = 128
DEPTH = 2
ALPHA = (2.0 * DEPTH) ** 0.25
LN_EPS = 1e-5
NEG_INF = -1e30
SCALE = HEAD_DIM ** -0.5

ADAM_LR = 0.001
ADAM_B1 = 0.9
ADAM_B2 = 0.999
ADAM_EPS = 1e-08
ADAM_WD = 0.01
ADAM_STEP = 10

N_DEV = 8
MESH = pl.DeviceIdType.MESH
TOKENS_IN_PROJ = 1024
TOKENS_WGRAD = 4096
TOKENS_A = 256
TOKENS_A_FWD = 256
P_RING = 3
TOKENS_B = 512
PART_ROWS_B = 256
TOKENS_B_BWD = 256
TOKENS_DX = 512
VMEM_LIMIT_BYTES = 60 * 1024 * 1024
SMALL_ROWS = 144

ANY = pl.BlockSpec(memory_space=pl.ANY)
VMEM = pl.BlockSpec(memory_space=pltpu.VMEM)
SMEM = pl.BlockSpec(memory_space=pltpu.SMEM)


def _nn(a, b):
    return jnp.dot(a, b, preferred_element_type=F32)


def _nt(a, b):
    return lax.dot_general(a, b, (((1,), (1,)), ((), ())), preferred_element_type=F32)


def _tn(a, b):
    return lax.dot_general(a, b, (((0,), (0,)), ((), ())), preferred_element_type=F32)


def _resident(shape):
    nd = len(shape)
    return pl.BlockSpec(shape, lambda i: (0,) * nd, pipeline_mode=pl.Buffered(1))


def _rows(tm, width, reverse_of=None):
    if reverse_of is None:
        return pl.BlockSpec((tm, width), lambda i: (i, 0))
    return pl.BlockSpec((tm, width), lambda i: (reverse_of - 1 - i, 0))


def _seq_params():
    return pltpu.CompilerParams(dimension_semantics=("arbitrary",), vmem_limit_bytes=VMEM_LIMIT_BYTES)


def _in_turns(first, second):
    merged = [(i / max(len(first), 1), 0, f) for i, f in enumerate(first)]
    merged += [((i + 0.5) / max(len(second), 1), 1, f) for i, f in enumerate(second)]
    return [f for _, _, f in sorted(merged, key=lambda t: t[:2])]


def _ln_normalize(r):
    mu = jnp.mean(r, axis=-1, keepdims=True)
    xc = r - mu
    var = jnp.mean(xc * xc, axis=-1, keepdims=True)
    rstd = lax.rsqrt(var + LN_EPS)
    return xc * rstd, rstd


def _ln_backward(dxhat, xhat, rstd):
    m1 = jnp.mean(dxhat, axis=-1, keepdims=True)
    m2 = jnp.mean(dxhat * xhat, axis=-1, keepdims=True)
    return rstd * (dxhat - m1 - xhat * m2)


def _silu_and_grad(z):
    sig = jax.nn.sigmoid(z)
    silu = z * sig
    return silu, sig + silu * (1.0 - sig)


def _band_tables():
    t = np.arange(CHUNK, dtype=np.int32)[:, None]
    j = np.arange(2 * CHUNK, dtype=np.int32)[None, :]
    d = t + CHUNK - j
    in_window = (d >= 0) & (d < CHUNK)
    dc = np.clip(d, 0, REL_MAX_DIST - 1)
    max_exact = REL_BUCKETS // 2
    df = np.maximum(dc, 1).astype(np.float32)
    large = max_exact + (np.log(df / np.float32(max_exact)) / np.float32(np.log(REL_MAX_DIST / max_exact))
                         * np.float32(REL_BUCKETS - max_exact)).astype(np.int32)
    large = np.minimum(large, REL_BUCKETS - 1)
    bucket = np.where(dc < max_exact, dc, large)
    return np.where(in_window, bucket, -1).astype(np.int32)


def _use_order():
    x, y, c = lax.axis_index("x"), lax.axis_index("y"), lax.axis_index("c")
    blocks = [(x, y, c), (x, y, 1 - c)]
    for px, py in [(1 - x, y), (x, 1 - y), (1 - x, 1 - y)]:
        blocks += [(px, py, c), (px, py, 1 - c)]
    return blocks


def _gather_in_proj(order, x, w_in_a, small, ln_gb):
    seq = x.shape[0]
    tm = min(TOKENS_IN_PROJ, seq)
    n_blocks = seq // tm
    shard_cols = w_in_a.shape[1]
    n = len(small) + 2

    def body(order_ref, x_ref, *refs):
        ins, refs = refs[:n], refs[n:]
        p_ref, outs, xb_out, refs = refs[0], refs[1:n + 1], refs[n + 1], refs[n + 2:]
        wa_vmem, xb_vmem, stage = refs[0], refs[1], refs[2:n]
        send_sems, recv_sems, local_sems, out_sems, xb_sem = refs[n:]
        xb_copy = pltpu.make_async_copy(xb_vmem, xb_out, xb_sem)
        phase, blk = pl.program_id(0), pl.program_id(1)
        x_, y_, c_ = lax.axis_index("x"), lax.axis_index("y"), lax.axis_index("c")
        me, sibling = (x_, y_, c_), (x_, y_, 1 - c_)
        chips = [(1 - x_, y_), (x_, 1 - y_), (1 - x_, 1 - y_)]
        use = _use_order()

        def index(px, py, pc):
            return 4 * px + 2 * py + pc

        def slot(a, block):
            return (wa_vmem if a == 0 else outs[a]).at[index(*block)]

        def own(a):
            return slot(0, me) if a == 0 else (ins[a] if a == n - 1 else stage[a - 1])

        def copy(a, k, block, to, src=None):
            return pltpu.make_async_remote_copy(
                src_ref=slot(a, block) if src is None else src, dst_ref=slot(a, block),
                send_sem=send_sems.at[a, k], recv_sem=recv_sems.at[a, k], device_id=to, device_id_type=MESH)

        def keep(a):
            return pltpu.make_async_copy(own(a), slot(a, me), local_sems.at[a])

        def write_out(k):
            return pltpu.make_async_copy(slot(0, use[k]), outs[0].at[index(*use[k])], out_sems.at[k])

        def pass_on(a, j):
            copy(a, 1 + j, (*chips[j], c_), me).wait_recv()
            copy(a, 4 + j, (*chips[j], c_), sibling).start()

        def send(a, j):
            copy(a, 1 + j, me, (*chips[j], c_), own(a)).start()

        @pl.when((phase == 0) & (blk == 0))
        def _():
            wa_vmem[index(*me)] = ins[0][...].astype(BF16)
            for a in range(1, n - 1):
                stage[a - 1][...] = ins[a][...].astype(BF16)
            send(0, 0)
            send(0, 1)
            for a in range(n):
                copy(a, 0, me, sibling, own(a)).start()
                if a:
                    keep(a).start()
            write_out(0).start()
            copy(0, 0, sibling, me).wait_recv()
            write_out(1).start()

        @pl.when((phase == 1) & (blk == 0))
        def _():
            pass_on(0, 0)
            write_out(2).start()
            xb_copy.start()
            send(0, 2)
            for a in range(1, n):
                for j in range(3):
                    send(a, j)
            pass_on(0, 1)
            write_out(4).start()
            copy(0, 4, (*chips[0], 1 - c_), me).wait_recv()
            write_out(3).start()

        @pl.when((phase == 2) & (blk == 0))
        def _():
            copy(0, 5, (*chips[1], 1 - c_), me).wait_recv()
            write_out(5).start()

        @pl.when((phase == 2) & (blk == n_blocks // 2))
        def _():
            pass_on(0, 2)
            write_out(6).start()

        @pl.when((phase == 3) & (blk == 0))
        def _():
            copy(0, 6, (*chips[2], 1 - c_), me).wait_recv()
            write_out(7).start()
            for a in range(1, n):
                pass_on(a, 0)
                pass_on(a, 1)

        rows = pl.ds(pl.multiple_of(blk * tm, tm), tm)

        @pl.when(phase == 0)
        def _():
            xb_vmem[rows, :] = x_ref[...].astype(BF16)

        chip = order_ref[phase]
        xb = xb_vmem[rows, :]
        p_ref[:, :shard_cols] = _nn(xb, wa_vmem[2 * chip])
        p_ref[:, shard_cols:] = _nn(xb, wa_vmem[2 * chip + 1])

        @pl.when((phase == N_DEV // 2 - 1) & (blk == n_blocks - 1))
        def _():
            for a in range(1, n):
                pass_on(a, 2)
            for a in range(1, n):
                copy(a, 0, sibling, me).wait_recv()
                for j, chip in enumerate(chips):
                    copy(a, 4 + j, (*chip, 1 - c_), me).wait_recv()
            for a in range(n):
                for k in range(7):
                    copy(a, k, me, me).wait_send()
                if a:
                    keep(a).wait()
            for k in range(N_DEV):
                write_out(k).wait()
            xb_copy.wait()

    gathered = [jax.ShapeDtypeStruct((N_DEV,) + w_in_a.shape, BF16)]
    gathered += [jax.ShapeDtypeStruct((N_DEV,) + s.shape, BF16) for s in small]
    gathered.append(jax.ShapeDtypeStruct((N_DEV,) + ln_gb.shape, F32))
    return pl.pallas_call(
        body, name="gather_in_proj",
        out_shape=[jax.ShapeDtypeStruct((seq, N_DEV * shard_cols), F32)] + gathered
        + [jax.ShapeDtypeStruct((seq, D_MODEL), BF16)],
        grid_spec=pltpu.PrefetchScalarGridSpec(
            num_scalar_prefetch=1, grid=(N_DEV // 2, n_blocks),
            in_specs=[pl.BlockSpec((tm, D_MODEL), lambda k, i, order: (jnp.where(k == 0, i, n_blocks - 1), 0))]
            + [VMEM] * n,
            out_specs=[pl.BlockSpec((tm, 2 * shard_cols), lambda k, i, order: (i, order[k]))] + [ANY] * (n + 1),
            scratch_shapes=[pltpu.VMEM((N_DEV,) + w_in_a.shape, BF16), pltpu.VMEM((seq, D_MODEL), BF16)]
            + [pltpu.VMEM(s.shape, BF16) for s in small] + [
                pltpu.SemaphoreType.DMA((n, 7)), pltpu.SemaphoreType.DMA((n, 7)), pltpu.SemaphoreType.DMA((n,)),
                pltpu.SemaphoreType.DMA((N_DEV,)), pltpu.SemaphoreType.DMA(())]),
        compiler_params=pltpu.CompilerParams(dimension_semantics=("arbitrary", "arbitrary"),
                                             vmem_limit_bytes=VMEM_LIMIT_BYTES),
    )(order, x, w_in_a, *small, ln_gb)


def _bias_table(rel_bias_t, bucket):
    def body(rel_ref, bucket_ref, out_ref):
        first = jnp.broadcast_to(bucket_ref[0:1, :], (8, 2 * CHUNK))
        col = lax.broadcasted_iota(jnp.int32, (CHUNK, 2 * CHUNK), 1)
        for h in range(N_Q_HEADS):
            row = jnp.full(first.shape, NEG_INF, F32)
            for b in range(REL_BUCKETS):
                row = jnp.where(first == b, rel_ref[h, b], row)
            tab = pltpu.roll(jnp.broadcast_to(row[0:1, :], (CHUNK, 2 * CHUNK)), 0, 1, stride=1, stride_axis=0)
            pair = h // 2
            rows = slice((pair % PAIRS_PER_KV) * CHUNK, (pair % PAIRS_PER_KV + 1) * CHUNK)
            half = slice((h % 2) * 2 * CHUNK, (h % 2 + 1) * 2 * CHUNK)
            out_ref[1, pair // PAIRS_PER_KV, rows, half] = tab
            out_ref[0, pair // PAIRS_PER_KV, rows, half] = jnp.where(col < CHUNK, NEG_INF, tab)

    return pl.pallas_call(
        body, name="bias_table",
        out_shape=jax.ShapeDtypeStruct((2, N_KV_HEADS, PAIRS_PER_KV * CHUNK, 4 * CHUNK), F32),
        in_specs=[SMEM, VMEM], out_specs=VMEM,
    )(rel_bias_t, bucket)


def _rel_bias_grad(dlogits, bucket):
    def body(dl_ref, bucket_ref, out_ref):
        i = lax.broadcasted_iota(jnp.int32, (CHUNK, CHUNK), 0)
        k = lax.broadcasted_iota(jnp.int32, (CHUNK, CHUNK), 1)
        reverse = jnp.where(i + k == CHUNK - 1, 1.0, 0.0)
        sums = []
        for h in range(N_Q_HEADS):
            pair = h // 2
            dl = dl_ref[pair // PAIRS_PER_KV, (pair % PAIRS_PER_KV) * CHUNK:(pair % PAIRS_PER_KV + 1) * CHUNK,
                        (h % 2) * 2 * CHUNK:(h % 2 + 1) * 2 * CHUNK]
            backwards = jnp.dot(reverse, dl, precision=lax.Precision.HIGHEST, preferred_element_type=F32)
            turned = pltpu.roll(backwards, CHUNK + 1, 1, stride=1, stride_axis=0)
            sums.append(jnp.sum(turned, axis=0, keepdims=True))
        by_offset = jnp.concatenate(sums, axis=0)
        buckets = lax.broadcasted_iota(jnp.int32, (REL_BUCKETS, 2 * CHUNK), 0)
        one_hot = jnp.where(jnp.broadcast_to(bucket_ref[0:1, :], buckets.shape) == buckets, 1.0, 0.0)
        out_ref[...] = lax.dot_general(by_offset, one_hot, (((1,), (1,)), ((), ())),
                                       precision=lax.Precision.HIGHEST, preferred_element_type=F32)

    return pl.pallas_call(
        body, name="rel_bias_grad", out_shape=jax.ShapeDtypeStruct((N_Q_HEADS, REL_BUCKETS), F32),
        in_specs=[VMEM, VMEM], out_specs=VMEM,
    )(dlogits, bucket)


def _masked_spatial(ws_ref, g):
    t = lax.broadcasted_iota(jnp.int32, (CHUNK, CHUNK), 0)
    s = lax.broadcasted_iota(jnp.int32, (CHUNK, CHUNK), 1)
    return jnp.where(s <= t, ws_ref[g], 0.0).astype(BF16)


def _spatial_gate(ws_ref, bsp_t_ref, vn, s_ref, n_chunks):
    for g in range(A_GROUPS):
        w = _masked_spatial(ws_ref, g)
        cols = slice(g * A_GROUP_DIM, (g + 1) * A_GROUP_DIM)
        for ci in range(n_chunks):
            rows = slice(ci * CHUNK, (ci + 1) * CHUNK)
            s_ref[rows, cols] = _nn(w, vn[rows, cols]) + bsp_t_ref[:, g:g + 1]


def _fwd_a(x, p, ln_gb, w_spatial, bsp_t, woa, later):
    seq = x.shape[0]
    tm = min(TOKENS_A_FWD, seq)
    n_chunks = tm // CHUNK
    n_blocks = seq // tm
    n = len(later)

    def body(x_ref, p_hbm, ln_ref, ws_ref, bsp_ref, woa_ref, *refs):
        shards, refs = refs[:n], refs[n:]
        r0_ref, gathered, refs = refs[0], refs[1:1 + n], refs[1 + n:]
        s_ref, p_ring, stage, (send_sems, recv_sems, local_sems, ring_sems) = refs[0], refs[1], refs[2:2 + n], refs[2 + n:]
        step = pl.program_id(0)

        def fetch(block):
            slot = block % P_RING
            start = block * tm if isinstance(block, int) else pl.multiple_of(block * tm, tm)
            return pltpu.make_async_copy(p_hbm.at[pl.ds(start, tm), :], p_ring.at[slot], ring_sems.at[slot])

        @pl.when(step == 0)
        def _():
            for ahead in range(min(P_RING - 1, n_blocks)):
                fetch(ahead).start()

        @pl.when(step + P_RING - 1 < n_blocks)
        def _():
            fetch(step + P_RING - 1).start()

        fetch(step).wait()
        p_ref = p_ring.at[step % P_RING]

        def copies(a):
            x_, y_, c_ = lax.axis_index("x"), lax.axis_index("y"), lax.axis_index("c")
            mine = 4 * x_ + 2 * y_ + c_
            out = [pltpu.make_async_copy(stage[a], gathered[a].at[mine], local_sems.at[a])]
            for r in range(1, N_DEV):
                to = (1 - x_ if r & 4 else x_, 1 - y_ if r & 2 else y_, 1 - c_ if r & 1 else c_)
                out.append(pltpu.make_async_remote_copy(
                    src_ref=stage[a], dst_ref=gathered[a].at[mine], send_sem=send_sems.at[a, r - 1],
                    recv_sem=recv_sems.at[a, r - 1], device_id=to, device_id_type=MESH))
            return out

        @pl.when(step == 0)
        def _():
            for a in range(n):
                stage[a][...] = shards[a][...].astype(BF16)
                for cp in copies(a):
                    cp.start()

        xv = x_ref[...]
        vhat, _ = _ln_normalize(p_ref[:, A_WIDTH:2 * A_WIDTH])
        vn = (vhat * ln_ref[0:1, :] + ln_ref[1:2, :]).astype(BF16)
        _spatial_gate(ws_ref, bsp_ref, vn, s_ref, n_chunks)
        z = p_ref[:, 2 * A_WIDTH:]
        y = p_ref[:, :A_WIDTH] * s_ref[...] * (z * jax.nn.sigmoid(z))
        r0_ref[...] = ALPHA * xv + _nn(y.astype(BF16), woa_ref[...])

        @pl.when(step == n_blocks - 1)
        def _():
            for a in range(n):
                for cp in copies(a):
                    cp.wait()

    return pl.pallas_call(
        body, name="fwd_a", grid=(n_blocks,),
        out_shape=[jax.ShapeDtypeStruct((seq, D_MODEL), F32)]
        + [jax.ShapeDtypeStruct((N_DEV,) + s.shape, BF16) for s in later],
        in_specs=[_rows(tm, D_MODEL), ANY, _resident(ln_gb.shape), _resident(w_spatial.shape),
                  _resident(bsp_t.shape), _resident(woa.shape)] + [VMEM] * n,
        out_specs=[_rows(tm, D_MODEL)] + [ANY] * n,
        scratch_shapes=[pltpu.VMEM((tm, A_WIDTH), F32), pltpu.VMEM((P_RING, tm, 3 * A_WIDTH), F32)]
        + [pltpu.VMEM(s.shape, BF16) for s in later] + [
            pltpu.SemaphoreType.DMA((n, N_DEV - 1)), pltpu.SemaphoreType.DMA((n, N_DEV - 1)),
            pltpu.SemaphoreType.DMA((n,)), pltpu.SemaphoreType.DMA((P_RING,))],
        compiler_params=_seq_params(),
    )(x, p, ln_gb, w_spatial, bsp_t, woa, *later)


def _expand_heads(kv):
    lo = lax.broadcasted_iota(jnp.int32, kv.shape, 1) < HEAD_DIM
    swapped = pltpu.roll(kv, HEAD_DIM, 1)
    zero = jnp.zeros_like(kv)
    return (jnp.where(lo, kv, zero).astype(BF16), jnp.where(lo, zero, swapped).astype(BF16),
            jnp.where(lo, swapped, zero).astype(BF16), jnp.where(lo, zero, kv).astype(BF16))


def _fold_heads(acc, kvh):
    top, bot = acc[:2 * CHUNK], acc[2 * CHUNK:]
    lo = lax.broadcasted_iota(jnp.int32, top.shape, 1) < HEAD_DIM
    t = jnp.where(lo, top, 0.0)
    b = jnp.where(lo, 0.0, bot)
    if kvh == 0:
        return t + pltpu.roll(b, HEAD_DIM, 1)
    return pltpu.roll(t, HEAD_DIM, 1) + b


STACK = PAIRS_PER_KV * CHUNK


def _pair_cols(kvh, pp):
    pair = kvh * PAIRS_PER_KV + pp
    return slice(pair * CHUNK, (pair + 1) * CHUNK)


def _stack_pairs(ref_or_val, rows, kvh):
    return jnp.concatenate([ref_or_val[rows, _pair_cols(kvh, pp)] for pp in range(PAIRS_PER_KV)], axis=0)


def _sink_columns(sink_ref):
    blk = lax.broadcasted_iota(jnp.int32, (STACK, 1), 0) // CHUNK
    out = []
    for kvh in range(N_KV_HEADS):
        halves = []
        for half in range(2):
            col = jnp.zeros((STACK, 1), F32)
            for pp in range(PAIRS_PER_KV):
                col = jnp.where(blk == pp, sink_ref[0, 2 * (kvh * PAIRS_PER_KV + pp) + half], col)
            halves.append(col)
        out.append(halves)
    return out


def _softmax_stack(logits, sink_cols):
    ps, sinks = [], []
    for half in range(2):
        lg = logits[:, half * 2 * CHUNK:(half + 1) * 2 * CHUNK]
        sk = sink_cols[half]
        m = jnp.maximum(jnp.max(lg, axis=-1, keepdims=True), sk)
        e = jnp.exp(lg - m)
        es = jnp.exp(sk - m)
        inv = 1.0 / (jnp.sum(e, axis=-1, keepdims=True) + es)
        ps.append(e * inv)
        sinks.append(es * inv)
    return jnp.concatenate(ps, axis=1), sinks


def _band_operands(before_k, this_k, before_v, this_v, kvh):
    kblk = jnp.concatenate([before_k[2 * kvh], this_k[2 * kvh], before_k[2 * kvh + 1], this_k[2 * kvh + 1]], axis=0)
    vblk = jnp.concatenate([before_v[2 * kvh], this_v[2 * kvh], before_v[2 * kvh + 1], this_v[2 * kvh + 1]], axis=0)
    return kblk, vblk


def _fwd_b(r0, target, wb, wkv, wob, post_g, post_b, sinks, bias):
    seq = r0.shape[0]
    tm = min(TOKENS_B, seq)
    n_chunks = tm // CHUNK
    part_rows = min(PART_ROWS_B, tm)
    n_parts = tm // part_rows
    part_chunks = part_rows // CHUNK
    shard_cols = wb.shape[2]

    def body(r0_ref, tgt_ref, wb_ref, wkv_ref, wob_ref, pg_ref, pb_ref, sink_ref, bias_ref,
             q_ref, z_ref, kv_ref, dr1_ref, loss_ref, dg_ref, db_ref, ek_last, ev_last, *scratch):
        q_s, z_s, h1_s, o_s, ek_s, ev_s = (scratch[i * n_parts:(i + 1) * n_parts] for i in range(6))
        step = pl.program_id(0)

        @pl.when(step == 0)
        def _():
            ek_last[...] = jnp.zeros_like(ek_last)
            ev_last[...] = jnp.zeros_like(ev_last)
            loss_ref[...] = jnp.zeros_like(loss_ref)
            dg_ref[...] = jnp.zeros_like(dg_ref)
            db_ref[...] = jnp.zeros_like(db_ref)

        sink_cols = _sink_columns(sink_ref)
        g1 = pg_ref[1:2, :]

        def project_steps(part):
            rows = slice(part * part_rows, (part + 1) * part_rows)
            live = {}

            def normalize():
                xhat0, _ = _ln_normalize(r0_ref[rows, :])
                h1 = xhat0 * pg_ref[0:1, :] + pb_ref[0:1, :]
                h1_s[part][...] = h1
                live["h1b"] = h1.astype(BF16)

            def shard(k):
                def run():
                    qz = _nn(live["h1b"], wb_ref[k])
                    lo, hi = k * shard_cols, (k + 1) * shard_cols
                    if hi <= B_WIDTH:
                        qs = (qz * SCALE).astype(BF16)
                        q_ref[rows, lo:hi] = qs
                        q_s[part][:, lo:hi] = qs
                    else:
                        z_ref[rows, lo - B_WIDTH:hi - B_WIDTH] = qz
                        z_s[part][:, lo - B_WIDTH:hi - B_WIDTH] = qz
                return run

            def keys():
                kv = _nn(live["h1b"], wkv_ref[...])
                kv_ref[rows, :] = kv.astype(BF16)
                for cj in range(part_chunks):
                    sub = slice(cj * CHUNK, (cj + 1) * CHUNK)
                    for i, e in enumerate(_expand_heads(kv[sub, :KV_WIDTH])):
                        ek_s[part][cj, i] = e
                    for i, e in enumerate(_expand_heads(kv[sub, KV_WIDTH:])):
                        ev_s[part][cj, i] = e

            return [normalize, keys] + [shard(k) for k in range(N_DEV)]

        def attend_steps(part):
            def stack(cj, kvh):
                def run():
                    sub = slice(cj * CHUNK, (cj + 1) * CHUNK)
                    sel = jnp.where(step * n_chunks + part * part_chunks + cj > 0, 1, 0)
                    if cj:
                        before_k, before_v = ek_s[part].at[cj - 1], ev_s[part].at[cj - 1]
                    elif part:
                        before_k, before_v = ek_s[part - 1].at[part_chunks - 1], ev_s[part - 1].at[part_chunks - 1]
                    else:
                        before_k, before_v = ek_last, ev_last
                    kblk, vblk = _band_operands(before_k, ek_s[part].at[cj], before_v, ev_s[part].at[cj], kvh)
                    logits = _nt(_stack_pairs(q_s[part], sub, kvh), kblk) + bias_ref[sel, kvh]
                    p, _ = _softmax_stack(logits, sink_cols[kvh])
                    o = _nn(p.astype(BF16), vblk)
                    for pp in range(PAIRS_PER_KV):
                        o_s[part][sub, _pair_cols(kvh, pp)] = o[pp * CHUNK:(pp + 1) * CHUNK]
                return run

            def finish():
                rows = slice(part * part_rows, (part + 1) * part_rows)
                z = z_s[part][...]
                y = o_s[part][...] * (z * jax.nn.sigmoid(z))
                r1 = ALPHA * h1_s[part][...] + _nn(y.astype(BF16), wob_ref[...])
                xhat1, rstd1 = _ln_normalize(r1)
                diff = xhat1 * g1 + pb_ref[1:2, :] - tgt_ref[rows, :]
                loss_ref[...] += 0.5 * jnp.sum(jnp.mean(diff * diff, axis=-1, keepdims=True), axis=0, keepdims=True)
                dh2 = diff * (1.0 / D_MODEL)
                dg_ref[...] += jnp.sum(dh2 * xhat1, axis=0, keepdims=True)
                db_ref[...] += jnp.sum(dh2, axis=0, keepdims=True)
                dr1_ref[rows, :] = _ln_backward(dh2 * g1, xhat1, rstd1)

            return [stack(cj, kvh) for cj in range(part_chunks) for kvh in range(N_KV_HEADS)] + [finish]

        for run in project_steps(0):
            run()
        for part in range(n_parts):
            for run in _in_turns(attend_steps(part), project_steps(part + 1) if part + 1 < n_parts else []):
                run()
        ek_last[...] = ek_s[n_parts - 1][part_chunks - 1]
        ev_last[...] = ev_s[n_parts - 1][part_chunks - 1]

    vec = pl.BlockSpec((1, D_MODEL), lambda i: (0, 0))
    return pl.pallas_call(
        body, name="fwd_b", grid=(seq // tm,),
        out_shape=[jax.ShapeDtypeStruct((seq, B_WIDTH), BF16), jax.ShapeDtypeStruct((seq, B_WIDTH), F32),
                   jax.ShapeDtypeStruct((seq, 2 * KV_WIDTH), BF16), jax.ShapeDtypeStruct((seq, D_MODEL), F32),
                   jax.ShapeDtypeStruct((1, 1), F32), jax.ShapeDtypeStruct((1, D_MODEL), F32),
                   jax.ShapeDtypeStruct((1, D_MODEL), F32)],
        in_specs=[_rows(tm, D_MODEL), _rows(tm, D_MODEL), _resident(wb.shape), _resident(wkv.shape),
                  _resident(wob.shape), _resident(post_g.shape), _resident(post_b.shape), SMEM,
                  _resident(bias.shape)],
        out_specs=[_rows(tm, B_WIDTH), _rows(tm, B_WIDTH), _rows(tm, 2 * KV_WIDTH), _rows(tm, D_MODEL),
                   pl.BlockSpec((1, 1), lambda i: (0, 0)), vec, vec],
        scratch_shapes=[pltpu.VMEM((4, CHUNK, CHUNK), BF16), pltpu.VMEM((4, CHUNK, CHUNK), BF16)]
        + [pltpu.VMEM((part_rows, B_WIDTH), BF16)] * n_parts + [pltpu.VMEM((part_rows, B_WIDTH), F32)] * n_parts
        + [pltpu.VMEM((part_rows, D_MODEL), F32)] * n_parts + [pltpu.VMEM((part_rows, B_WIDTH), F32)] * n_parts
        + [pltpu.VMEM((part_chunks, 4, CHUNK, CHUNK), BF16)] * (2 * n_parts),
        compiler_params=_seq_params(),
    )(r0, target, wb, wkv, wob, post_g, post_b, sinks, bias)


def _bwd_b(r0, dr1, q, z, kv, wb, wkv, wob, post_g, post_b, sinks, bias):
    seq = r0.shape[0]
    tm = min(TOKENS_B_BWD, seq)
    n_chunks = tm // CHUNK
    n_blocks = seq // tm
    shard_cols = wb.shape[2]

    def body(r0_ref, dr1_ref, q_ref, z_ref, kv_ref, kvp_ref, wb_ref, wkv_ref, wob_ref, pg_ref, pb_ref, sink_ref,
             bias_ref, dr0_ref, dwb_out, dwkv_out, dwob_out, dsink_ref, dl_out, dpost_ref,
             dwb_acc, dwkv_acc, dwob_acc, dl_acc, dsink_acc, carry_ref, ek_ref, ev_ref, y_ref, dqz_ref, dkv_ref):
        step = pl.program_id(0)
        block = n_blocks - 1 - step

        @pl.when(step == 0)
        def _():
            dpost_ref[...] = jnp.zeros_like(dpost_ref)
            dwb_acc[...] = jnp.zeros_like(dwb_acc)
            dwkv_acc[...] = jnp.zeros_like(dwkv_acc)
            dwob_acc[...] = jnp.zeros_like(dwob_acc)
            dl_acc[...] = jnp.zeros_like(dl_acc)
            carry_ref[...] = jnp.zeros_like(carry_ref)
            dsink_acc[...] = jnp.zeros_like(dsink_acc)

        xhat0, rstd0 = _ln_normalize(r0_ref[...])
        h1b = (xhat0 * pg_ref[0:1, :] + pb_ref[0:1, :]).astype(BF16)
        dr1 = dr1_ref[...]
        dr1b = dr1.astype(BF16)
        dy = _nt(dr1b, wob_ref[...])
        sg, dsg = _silu_and_grad(z_ref[...])
        do_all = dy * sg
        dyz = dy * dsg

        for ci in range(n_chunks + 1):
            src = kvp_ref[...] if ci == 0 else kv_ref[(ci - 1) * CHUNK:ci * CHUNK, :]
            src = src.astype(F32)
            for i, e in enumerate(_expand_heads(src[:, :KV_WIDTH])):
                ek_ref[ci, i] = e
            for i, e in enumerate(_expand_heads(src[:, KV_WIDTH:])):
                ev_ref[ci, i] = e

        sink_cols = _sink_columns(sink_ref)
        for ci in reversed(range(n_chunks)):
            rows = slice(ci * CHUNK, (ci + 1) * CHUNK)
            sel = jnp.where(block * n_chunks + ci > 0, 1, 0)
            dk_band = jnp.zeros((2 * CHUNK, KV_WIDTH), F32)
            dv_band = jnp.zeros((2 * CHUNK, KV_WIDTH), F32)
            for kvh in range(N_KV_HEADS):
                kblk, vblk = _band_operands(ek_ref.at[ci], ek_ref.at[ci + 1], ev_ref.at[ci], ev_ref.at[ci + 1], kvh)
                qs = _stack_pairs(q_ref, rows, kvh)
                p, psink = _softmax_stack(_nt(qs, kblk) + bias_ref[sel, kvh], sink_cols[kvh])
                pb16 = p.astype(BF16)
                o = _nn(pb16, vblk)
                y = (o * _stack_pairs(sg, rows, kvh)).astype(BF16)
                dz = (o * _stack_pairs(dyz, rows, kvh)).astype(BF16)
                dob = _stack_pairs(do_all, rows, kvh).astype(BF16)
                dp = _nt(dob, vblk)
                dlogits = []
                for half in range(2):
                    hs = slice(half * 2 * CHUNK, (half + 1) * 2 * CHUNK)
                    delta = jnp.sum(dp[:, hs] * p[:, hs], axis=-1, keepdims=True)
                    dlogits.append(p[:, hs] * (dp[:, hs] - delta))
                    dsink_acc[2 * kvh + half] += psink[half] * delta
                dlog = jnp.concatenate(dlogits, axis=1)
                dl_acc[kvh] += dlog
                dsc = dlog.astype(BF16)
                dq = (_nn(dsc, kblk) * SCALE).astype(BF16)
                for pp in range(PAIRS_PER_KV):
                    blk = slice(pp * CHUNK, (pp + 1) * CHUNK)
                    cols = _pair_cols(kvh, pp)
                    y_ref[rows, cols] = y[blk]
                    dqz_ref[rows, cols] = dq[blk]
                    dqz_ref[rows, B_WIDTH + cols.start:B_WIDTH + cols.stop] = dz[blk]
                dk_band = dk_band + _fold_heads(_tn(dsc, qs), kvh)
                dv_band = dv_band + _fold_heads(_tn(pb16, dob), kvh)
            band = jnp.concatenate([dk_band, dv_band], axis=1)
            dkv_ref[rows, :] = (band[CHUNK:] + carry_ref[...]).astype(BF16)
            carry_ref[...] = band[:CHUNK]

        dqz = dqz_ref[...]
        dkv = dkv_ref[...]
        dh1 = ALPHA * dr1 + _nt(dkv, wkv_ref[...])
        for k in range(N_DEV):
            part = dqz[:, k * shard_cols:(k + 1) * shard_cols]
            dh1 = dh1 + _nt(part, wb_ref[k])
            dwb_acc[k] += _tn(h1b, part)
        dpost_ref[0:1, :] += jnp.sum(dh1 * xhat0, axis=0, keepdims=True)
        dpost_ref[1:2, :] += jnp.sum(dh1, axis=0, keepdims=True)
        dr0_ref[...] = _ln_backward(dh1 * pg_ref[0:1, :], xhat0, rstd0)
        dwkv_acc[...] += _tn(h1b, dkv)
        dwob_acc[...] += _tn(y_ref[...], dr1b)

        @pl.when(step == n_blocks - 1)
        def _():
            pltpu.sync_copy(dwb_acc, dwb_out)
            pltpu.sync_copy(dwkv_acc, dwkv_out)
            pltpu.sync_copy(dwob_acc, dwob_out)
            pltpu.sync_copy(dl_acc, dl_out)
            lane = lax.broadcasted_iota(jnp.int32, (1, CHUNK), 1)
            dsink = jnp.zeros((1, CHUNK), F32)
            for kvh in range(N_KV_HEADS):
                for half in range(2):
                    for pp in range(PAIRS_PER_KV):
                        head = 2 * (kvh * PAIRS_PER_KV + pp) + half
                        tot = jnp.sum(dsink_acc[2 * kvh + half, pp * CHUNK:(pp + 1) * CHUNK, :], axis=0, keepdims=True)
                        dsink = dsink - jnp.where(lane == head, tot, 0.0)
            dsink_ref[...] = dsink

    rev = functools.partial(_rows, tm, reverse_of=n_blocks)
    prev_chunk = pl.BlockSpec(
        (CHUNK, 2 * KV_WIDTH), lambda i: (jnp.maximum((n_blocks - 1 - i) * n_chunks - 1, 0), 0))
    return pl.pallas_call(
        body, name="bwd_b", grid=(n_blocks,),
        out_shape=[jax.ShapeDtypeStruct((seq, D_MODEL), F32), jax.ShapeDtypeStruct(wb.shape, F32),
                   jax.ShapeDtypeStruct(wkv.shape, F32), jax.ShapeDtypeStruct(wob.shape, F32),
                   jax.ShapeDtypeStruct((1, CHUNK), F32),
                   jax.ShapeDtypeStruct((N_KV_HEADS, STACK, 4 * CHUNK), F32), jax.ShapeDtypeStruct((2, D_MODEL), F32)],
        in_specs=[rev(D_MODEL), rev(D_MODEL), rev(B_WIDTH), rev(B_WIDTH), rev(2 * KV_WIDTH), prev_chunk,
                  _resident(wb.shape), _resident(wkv.shape), _resident(wob.shape), _resident(post_g.shape),
                  _resident(post_b.shape), SMEM, _resident(bias.shape)],
        out_specs=[rev(D_MODEL), ANY, ANY, ANY, pl.BlockSpec((1, CHUNK), lambda i: (0, 0)), ANY,
                   pl.BlockSpec((2, D_MODEL), lambda i: (0, 0))],
        scratch_shapes=[pltpu.VMEM(wb.shape, F32), pltpu.VMEM(wkv.shape, F32), pltpu.VMEM(wob.shape, F32),
                        pltpu.VMEM((N_KV_HEADS, STACK, 4 * CHUNK), F32), pltpu.VMEM((2 * N_KV_HEADS, STACK, 1), F32),
                        pltpu.VMEM((CHUNK, 2 * KV_WIDTH), F32),
                        pltpu.VMEM((n_chunks + 1, 4, CHUNK, CHUNK), BF16),
                        pltpu.VMEM((n_chunks + 1, 4, CHUNK, CHUNK), BF16),
                        pltpu.VMEM((tm, B_WIDTH), BF16), pltpu.VMEM((tm, 2 * B_WIDTH), BF16),
                        pltpu.VMEM((tm, 2 * KV_WIDTH), BF16)],
        compiler_params=_seq_params(),
    )(r0, dr1, q, z, kv, kv, wb, wkv, wob, post_g, post_b, sinks, bias)


def _bwd_a(dr0, p, ln_gb, w_spatial, bsp_t, woa, exchange):
    seq = dr0.shape[0]
    tm = min(TOKENS_A, seq)
    n_chunks = tm // CHUNK
    n_blocks = seq // tm

    def body(dr0_ref, p_ref, ln_ref, ws_ref, bsp_ref, woa_ref, *refs):
        ex_in, refs = refs[:exchange.n], refs[exchange.n:]
        (dp_ref, dwoa_out, dws_ref, dbsp_ref, dln_ref), refs = refs[:5], refs[5:]
        ex_out, refs = refs[:exchange.n], refs[exchange.n:]
        (dwoa_acc, s_ref, ds_ref, dvn_ref), ex_sems = refs[:4], refs[4:]
        step = pl.program_id(0)

        @pl.when(step == 0)
        def _():
            exchange.start(ex_in, ex_out, ex_sems)
            dwoa_acc[...] = jnp.zeros_like(dwoa_acc)
            dws_ref[...] = jnp.zeros_like(dws_ref)
            dbsp_ref[...] = jnp.zeros_like(dbsp_ref)
            dln_ref[...] = jnp.zeros_like(dln_ref)

        dr0b = dr0_ref[...].astype(BF16)

        vhat, rstd_v = _ln_normalize(p_ref[:, A_WIDTH:2 * A_WIDTH])
        ln_g = ln_ref[0:1, :]
        vn = (vhat * ln_g + ln_ref[1:2, :]).astype(BF16)
        _spatial_gate(ws_ref, bsp_ref, vn, s_ref, n_chunks)
        u = p_ref[:, :A_WIDTH]
        s = s_ref[...]
        sg, dsg = _silu_and_grad(p_ref[:, 2 * A_WIDTH:])
        gate = s * sg
        dwoa_acc[...] += _tn((u * gate).astype(BF16), dr0b)
        dy = _nt(dr0b, woa_ref[...])
        dyu = dy * u
        dp_ref[:, :A_WIDTH] = (dy * gate).astype(BF16)
        dp_ref[:, 2 * A_WIDTH:] = (dyu * (s * dsg)).astype(BF16)
        ds_ref[...] = dyu * sg

        lane = lax.broadcasted_iota(jnp.int32, (CHUNK, CHUNK), 1)
        dbsp = jnp.zeros((CHUNK, CHUNK), F32)
        for g in range(A_GROUPS):
            w = _masked_spatial(ws_ref, g)
            cols = slice(g * A_GROUP_DIM, (g + 1) * A_GROUP_DIM)
            dws = jnp.zeros((CHUNK, CHUNK), F32)
            for ci in range(n_chunks):
                rows = slice(ci * CHUNK, (ci + 1) * CHUNK)
                ds = ds_ref[rows, cols]
                dsb = ds.astype(BF16)
                dvn_ref[rows, cols] = _tn(w, dsb)
                dws = dws + _nt(dsb, vn[rows, cols])
                dbsp = dbsp + jnp.where(lane == g, jnp.sum(ds, axis=-1, keepdims=True), 0.0)
            dws_ref[g] += dws
        dbsp_ref[...] += dbsp

        dvn = dvn_ref[...]
        dln_ref[0:1, :] += jnp.sum(dvn * vhat, axis=0, keepdims=True)
        dln_ref[1:2, :] += jnp.sum(dvn, axis=0, keepdims=True)
        dp_ref[:, A_WIDTH:2 * A_WIDTH] = _ln_backward(dvn * ln_g, vhat, rstd_v).astype(BF16)

        @pl.when(step == n_blocks - 1)
        def _():
            t = lax.broadcasted_iota(jnp.int32, (CHUNK, CHUNK), 0)
            for g in range(A_GROUPS):
                dws_ref[g] = jnp.where(lane <= t, dws_ref[g], 0.0)
            pltpu.sync_copy(dwoa_acc, dwoa_out)
            exchange.wait(ex_in, ex_out, ex_sems)

    def fixed(shape):
        nd = len(shape)
        return pl.BlockSpec(shape, lambda i: (0,) * nd)

    return pl.pallas_call(
        body, name="bwd_a", grid=(n_blocks,),
        out_shape=[jax.ShapeDtypeStruct((seq, 3 * A_WIDTH), BF16),
                   jax.ShapeDtypeStruct(woa.shape, F32), jax.ShapeDtypeStruct(w_spatial.shape, F32),
                   jax.ShapeDtypeStruct((CHUNK, CHUNK), F32), jax.ShapeDtypeStruct((2, A_WIDTH), F32)]
        + exchange.out_shape,
        in_specs=[_rows(tm, D_MODEL), _rows(tm, 3 * A_WIDTH), _resident(ln_gb.shape),
                  _resident(w_spatial.shape), _resident(bsp_t.shape), _resident(woa.shape)] + [ANY] * exchange.n,
        out_specs=[_rows(tm, 3 * A_WIDTH), ANY, fixed(w_spatial.shape),
                   fixed((CHUNK, CHUNK)), fixed((2, A_WIDTH))] + [ANY] * exchange.n,
        scratch_shapes=[pltpu.VMEM(woa.shape, F32), pltpu.VMEM((tm, A_WIDTH), F32),
                        pltpu.VMEM((tm, A_WIDTH), F32), pltpu.VMEM((tm, A_WIDTH), F32)] + exchange.scratch_shapes,
        compiler_params=_seq_params(),
    )(dr0, p, ln_gb, w_spatial, bsp_t, woa, *exchange.inputs)


def _input_grad_blocks(seq):
    tm = min(TOKENS_DX, seq // 8)
    return tm, seq // tm


def _input_grad(dp, dr0, wa, first, n_blocks, earlier, exchange):
    seq = dr0.shape[0]
    tm, _ = _input_grad_blocks(seq)
    shard_cols = wa.shape[2]
    n_own = 3 if earlier is None else 4

    def body(dp_ref, dr0_ref, wa_ref, *refs):
        refs = refs[n_own - 3:]
        ex_in, dx_ref, ex_out, ex_sems = (refs[:exchange.n], refs[exchange.n], refs[exchange.n + 1:2 * exchange.n + 1],
                                          refs[2 * exchange.n + 1:])
        step = pl.program_id(0)

        @pl.when(step == 0)
        def _():
            exchange.start(ex_in, ex_out, ex_sems)

        dx = ALPHA * dr0_ref[...]
        for k in range(N_DEV):
            dx = dx + _nt(dp_ref[:, k * shard_cols:(k + 1) * shard_cols], wa_ref[k])
        dx_ref[...] = dx

        @pl.when(step == n_blocks - 1)
        def _():
            exchange.wait(ex_in, ex_out, ex_sems)

    def rows(width):
        return pl.BlockSpec((tm, width), lambda i: (first + i, 0))

    return pl.pallas_call(
        body, name=f"input_grad_{first}", grid=(n_blocks,),
        out_shape=[jax.ShapeDtypeStruct((seq, D_MODEL), F32)] + exchange.out_shape,
        in_specs=[rows(3 * A_WIDTH), rows(D_MODEL), _resident(wa.shape)] + [ANY] * (n_own - 3 + exchange.n),
        out_specs=[rows(D_MODEL)] + [ANY] * exchange.n,
        scratch_shapes=exchange.scratch_shapes,
        input_output_aliases={} if earlier is None else {3: 0},
        compiler_params=_seq_params(),
    )(dp, dr0, wa, *(() if earlier is None else (earlier,)), *exchange.inputs)


def _in_proj_weight_grad(xb, dp, shard_cols, exchange):
    seq = xb.shape[0]
    tk = min(TOKENS_WGRAD, seq)
    n_k = seq // tk

    def body(x_ref, dp_ref, *refs):
        ex_in, out_ref, ex_out, ex_sems = (refs[:exchange.n], refs[exchange.n], refs[exchange.n + 1:2 * exchange.n + 1],
                                           refs[2 * exchange.n + 1:])
        shard, step = pl.program_id(0), pl.program_id(1)

        @pl.when((shard == 0) & (step == 0))
        def _():
            exchange.start(ex_in, ex_out, ex_sems)

        @pl.when(step == 0)
        def _():
            out_ref[...] = jnp.zeros_like(out_ref)

        out_ref[...] += _tn(x_ref[...], dp_ref[...])

        @pl.when((shard == N_DEV - 1) & (step == n_k - 1))
        def _():
            exchange.wait(ex_in, ex_out, ex_sems)

    return pl.pallas_call(
        body, name="in_proj_weight_grad", grid=(N_DEV, n_k),
        out_shape=[jax.ShapeDtypeStruct((N_DEV, D_MODEL, shard_cols), F32)] + exchange.out_shape,
        in_specs=[pl.BlockSpec((tk, D_MODEL), lambda j, k: (k, 0)),
                  pl.BlockSpec((tk, shard_cols), lambda j, k: (k, j))] + [ANY] * exchange.n,
        out_specs=[pl.BlockSpec((None, D_MODEL, shard_cols), lambda j, k: (j, 0, 0))] + [ANY] * exchange.n,
        scratch_shapes=exchange.scratch_shapes,
        compiler_params=pltpu.CompilerParams(dimension_semantics=("arbitrary", "arbitrary"),
                                             vmem_limit_bytes=VMEM_LIMIT_BYTES),
    )(xb, dp, *exchange.inputs)


class _Exchange:
    def __init__(self, pair=(), chip=()):
        self.pair, self.chip = tuple(pair), tuple(chip)
        self.inputs = self.pair + self.chip
        self.n = len(self.inputs)
        self.out_shape = [jax.ShapeDtypeStruct((4, 1) + g.shape[2:], F32) for g in self.pair]
        self.out_shape += [jax.ShapeDtypeStruct((3,) + s.shape[1:], s.dtype) for s in self.chip]
        self.scratch_shapes = [pltpu.SemaphoreType.DMA((max(len(self.pair), 1),)) for _ in range(2)]
        self.scratch_shapes += [pltpu.SemaphoreType.DMA((max(len(self.chip), 1), 3)) for _ in range(2)]

    def _copies(self, ins, outs, sems):
        if not self.n:
            return []
        pair_send, pair_recv, chip_send, chip_recv = sems
        x, y, c = lax.axis_index("x"), lax.axis_index("y"), lax.axis_index("c")
        copies = []
        for a in range(len(self.pair)):
            copies.append(pltpu.make_async_remote_copy(
                src_ref=ins[a].at[:, pl.ds(1 - c, 1)], dst_ref=outs[a],
                send_sem=pair_send.at[a], recv_sem=pair_recv.at[a], device_id=(x, y, 1 - c), device_id_type=MESH))
        for a in range(len(self.chip)):
            src, dst = ins[len(self.pair) + a], outs[len(self.pair) + a]
            for k, (px, py) in enumerate([(1 - x, y), (x, 1 - y), (1 - x, 1 - y)]):
                copies.append(pltpu.make_async_remote_copy(
                    src_ref=src.at[2 * px + py], dst_ref=dst.at[k], send_sem=chip_send.at[a, k],
                    recv_sem=chip_recv.at[a, k], device_id=(px, py, c), device_id_type=MESH))
        return copies

    def start(self, ins, outs, sems):
        for cp in self._copies(ins, outs, sems):
            cp.start()

    def wait(self, ins, outs, sems):
        for cp in self._copies(ins, outs, sems):
            cp.wait()


def _row_tile(rows, cols, itemsize=4):
    if rows * cols * itemsize <= (1 << 20):
        return rows
    t = 8
    while 2 * t * cols * itemsize <= (1 << 20) and rows % (2 * t) == 0:
        t *= 2
    assert rows % t == 0, (rows, cols)
    return t


def _pair_sum(place, grads, landeds, dtypes):
    n = len(grads)

    def body(place_ref, *refs):
        for a in range(n):
            refs[2 * n + a][...] = (refs[2 * a][...] + refs[2 * a + 1][...]).astype(dtypes[a])

    in_specs, out_specs, operands = [], [], []
    for g, l in zip(grads, landeds):
        rows, cols = g.shape[2:]
        in_specs += [pl.BlockSpec((None, None, rows, cols), lambda k, pr: (k, pr[1], 0, 0)),
                     pl.BlockSpec((None, None, rows, cols), lambda k, pr: (k, 0, 0, 0))]
        out_specs.append(pl.BlockSpec((None, rows, cols), lambda k, pr: (k, 0, 0)))
        operands += [g, l]
    return pl.pallas_call(
        body, name="pair_sum",
        out_shape=[jax.ShapeDtypeStruct((4,) + g.shape[2:], dt) for g, dt in zip(grads, dtypes)],
        grid_spec=pltpu.PrefetchScalarGridSpec(num_scalar_prefetch=1, grid=(4,), in_specs=in_specs,
                                               out_specs=out_specs),
        compiler_params=pltpu.CompilerParams(dimension_semantics=("parallel",), vmem_limit_bytes=VMEM_LIMIT_BYTES),
    )(place, *operands)


def _adamw_math(w, g, m, v):
    m = ADAM_B1 * m + (1.0 - ADAM_B1) * g
    v = ADAM_B2 * v + (1.0 - ADAM_B2) * (g * g)
    m_hat = m / (1.0 - ADAM_B1 ** ADAM_STEP)
    v_hat = v / (1.0 - ADAM_B2 ** ADAM_STEP)
    delta = -ADAM_LR * (m_hat / (jnp.sqrt(v_hat) + ADAM_EPS) + ADAM_WD * w)
    return delta, m, v


def _adamw_shard(place, grad, landed, got, w, m, v):
    rows, cols = w.shape
    tr = _row_tile(rows, cols)

    def body(place_ref, a_ref, b_ref, got_ref, w_ref, m_ref, v_ref, g_out, d_out, m_out, v_out):
        g = a_ref[...] + b_ref[...]
        for k in range(3):
            g = g + got_ref[k].astype(F32)
        g_out[...] = g
        d_out[...], m_out[...], v_out[...] = _adamw_math(w_ref[...], g, m_ref[...], v_ref[...])

    spec = pl.BlockSpec((tr, cols), lambda i, pr: (i, 0))
    out = jax.ShapeDtypeStruct((rows, cols), F32)
    return pl.pallas_call(
        body, name="adamw_shard", out_shape=[out] * 4,
        grid_spec=pltpu.PrefetchScalarGridSpec(
            num_scalar_prefetch=1, grid=(rows // tr,),
            in_specs=[pl.BlockSpec((None, None, tr, cols), lambda i, pr: (pr[0], pr[1], i, 0)),
                      pl.BlockSpec((None, None, tr, cols), lambda i, pr: (pr[0], 0, i, 0)),
                      pl.BlockSpec((3, tr, cols), lambda i, pr: (0, i, 0)), spec, spec, spec],
            out_specs=[spec] * 4),
        compiler_params=pltpu.CompilerParams(dimension_semantics=("parallel",)),
    )(place, grad, landed, got, w, m, v)


def _gather_small(grad, landed, got):
    rows, cols = got.shape[1:]

    def body(grad_ref, landed_ref, got_ref, out_ref, send_sems, recv_sems):
        x, y, c = lax.axis_index("x"), lax.axis_index("y"), lax.axis_index("c")
        me, sibling = (x, y, c), (x, y, 1 - c)
        chips = [(1 - x, y), (x, 1 - y), (1 - x, 1 - y)]

        def slot(px, py, pc):
            return out_ref.at[4 * px + 2 * py + pc]

        def copy(k, block, to):
            return pltpu.make_async_remote_copy(
                src_ref=slot(*block), dst_ref=slot(*block), send_sem=send_sems.at[k], recv_sem=recv_sems.at[k],
                device_id=to, device_id_type=MESH)

        mine = grad_ref[2 * x + y, c] + landed_ref[2 * x + y, 0]
        out_ref[4 * x + 2 * y + c] = ((mine + got_ref[0]) + got_ref[1]) + got_ref[2]
        first = [copy(0, me, sibling)] + [copy(1 + j, me, (*chip, c)) for j, chip in enumerate(chips)]
        for cp in first:
            cp.start()
        passed = [copy(4 + j, (*chip, c), sibling) for j, chip in enumerate(chips)]
        for j, chip in enumerate(chips):
            copy(1 + j, (*chip, c), me).wait_recv()
            passed[j].start()
        copy(0, sibling, me).wait_recv()
        for j, chip in enumerate(chips):
            copy(4 + j, (*chip, 1 - c), me).wait_recv()
        for cp in first + passed:
            cp.wait_send()

    return pl.pallas_call(
        body, name="gather_small", out_shape=jax.ShapeDtypeStruct((N_DEV, rows, cols), F32),
        in_specs=[VMEM] * 3, out_specs=VMEM,
        scratch_shapes=[pltpu.SemaphoreType.DMA((7,)), pltpu.SemaphoreType.DMA((7,))],
    )(grad, landed, got)


def _adamw_small(params):
    n = len(params)

    def body(*refs):
        ins, outs = refs[:4 * n], refs[4 * n:]
        for i in range(n):
            w_ref, g_ref, m_ref, v_ref = ins[4 * i:4 * i + 4]
            d, m, v = _adamw_math(w_ref[...], g_ref[...], m_ref[...], v_ref[...])
            outs[3 * i][...] = d
            outs[3 * i + 1][...] = m
            outs[3 * i + 2][...] = v

    flat = [a for p in params for a in p]
    out_shape = [jax.ShapeDtypeStruct(p[0].shape, F32) for p in params for _ in range(3)]
    outs = pl.pallas_call(
        body, name="adamw_small", out_shape=out_shape, in_specs=[VMEM] * (4 * n), out_specs=[VMEM] * (3 * n),
    )(*flat)
    return [tuple(outs[3 * i:3 * i + 3]) for i in range(n)]


def _pack_small(parts):
    rows = []
    for a in parts:
        r, c = a.shape
        rows.append(jnp.pad(a, ((0, (-r) % 8), (0, CHUNK - c))))
    flat = jnp.concatenate(rows, axis=0)
    return jnp.pad(flat, ((0, N_DEV * SMALL_ROWS - flat.shape[0]), (0, 0))).reshape(N_DEV, SMALL_ROWS, CHUNK)


def _unpack_small(flat, shapes):
    out, at = [], 0
    for r, c in shapes:
        out.append(flat[at:at + r, :c])
        at += r + (-r) % 8
    return out


def kernel(x, w_in_a, sgu_ln_g, sgu_ln_b, w_spatial, b_spatial, w_out_a, w_kv, w_in_b, attn_sinks, rel_bias, w_out_b, post_ln_g, post_ln_b, loss_target, m_w_in_a, m_sgu_ln_g, m_sgu_ln_b, m_w_spatial, m_b_spatial, m_w_out_a, m_w_kv, m_w_in_b, m_attn_sinks, m_rel_bias, m_w_out_b, m_post_ln_g, m_post_ln_b, v_w_in_a, v_sgu_ln_g, v_sgu_ln_b, v_w_spatial, v_b_spatial, v_w_out_a, v_w_kv, v_w_in_b, v_attn_sinks, v_rel_bias, v_w_out_b, v_post_ln_g, v_post_ln_b):
    seq = x.shape[1]
    me = 4 * lax.axis_index("x") + 2 * lax.axis_index("y") + lax.axis_index("c")
    bucket = jnp.asarray(_band_tables())

    ln_shard = jnp.concatenate([sgu_ln_g, sgu_ln_b], axis=0)
    x2, tgt = x[0], loss_target[0]
    order = jnp.stack([2 * px + py for px, py, _ in _use_order()[::2]]).astype(jnp.int32)
    p, wa, woa, ln_all, xb = _gather_in_proj(order, x2, w_in_a[0], (w_out_a[0],), ln_shard)
    woa = woa.reshape(A_WIDTH, D_MODEL)
    ln_gb = jnp.transpose(ln_all, (1, 0, 2)).reshape(2, A_WIDTH)
    ws = w_spatial[0]
    bsp_t = jnp.pad(b_spatial[0].T, ((0, 0), (0, CHUNK - A_GROUPS)))
    rel_t = rel_bias.T
    bias = _bias_table(rel_t, bucket)

    r0, wkv, wb, wob = _fwd_a(x2, p, ln_gb, ws, bsp_t, woa, (w_kv, w_in_b[0], w_out_b[0]))
    wkv = wkv.reshape(D_MODEL, 2 * KV_WIDTH)
    wob = wob.reshape(B_WIDTH, D_MODEL)
    q, z, kv, dr1, loss, dg1, db1 = _fwd_b(r0, tgt, wb, wkv, wob, post_ln_g, post_ln_b, attn_sinks, bias)
    dr0, dwb, dwkv, dwob, dsink, dlog, dpost0 = _bwd_b(r0, dr1, q, z, kv, wb, wkv, wob, post_ln_g, post_ln_b,
                                                       attn_sinks, bias)
    drel = _rel_bias_grad(dlog, bucket)

    place = jnp.stack([2 * lax.axis_index("x") + lax.axis_index("y"), lax.axis_index("c")]).astype(jnp.int32)

    def by_device(g):
        return g.reshape((4, 2, g.shape[0] // N_DEV) + g.shape[1:]) if g.ndim == 2 else g.reshape((4, 2) + g.shape[1:])

    dwb, dwkv, dwob = by_device(dwb), by_device(dwkv), by_device(dwob)
    dp, dwoa, dws, dbsp_t, dln, l_wb, l_wkv, l_wob = _bwd_a(dr0, p, ln_gb, ws, bsp_t, woa,
                                                            _Exchange(pair=(dwb, dwkv, dwob)))
    dwoa = by_device(dwoa)
    small_shapes = [(A_GROUPS * CHUNK, CHUNK), (A_GROUPS, CHUNK), (1, N_Q_HEADS), (N_Q_HEADS, REL_BUCKETS),
                    (16, CHUNK), (16, CHUNK), (16, CHUNK), (16, CHUNK), (1, 1)]
    small = by_device(_pack_small([
        dws.reshape(A_GROUPS * CHUNK, CHUNK), dbsp_t[:, :A_GROUPS].T, dsink[:, :N_Q_HEADS], drel,
        jnp.concatenate([dpost0[0:1], dg1], axis=0).reshape(16, CHUNK),
        jnp.concatenate([dpost0[1:2], db1], axis=0).reshape(16, CHUNK),
        dln[0].reshape(16, CHUNK), dln[1].reshape(16, CHUNK), loss]))
    s_wb, s_wkv, s_wob = _pair_sum(place, (dwb, dwkv, dwob), (l_wb, l_wkv, l_wob), (BF16,) * 3)
    dwa, l_woa, l_small, g_wb, g_wkv, g_wob = _in_proj_weight_grad(
        xb, dp, wa.shape[2], _Exchange(pair=(dwoa, small), chip=(s_wb, s_wkv, s_wob)))
    dwa = by_device(dwa)
    s_woa, s_small = _pair_sum(place, (dwoa, small), (l_woa, l_small), (BF16, F32))
    _, blocks = _input_grad_blocks(seq)
    cut = blocks // 4
    dx, l_wa, g_woa, g_small = _input_grad(dp, dr0, wa, 0, cut, None, _Exchange(pair=(dwa,), chip=(s_woa, s_small)))
    s_wa, = _pair_sum(place, (dwa,), (l_wa,), (BF16,))
    dx, g_wa = _input_grad(dp, dr0, wa, cut, blocks - cut, dx, _Exchange(chip=(s_wa,)))
    full = [dwa, dwoa, dwkv, dwb, dwob, small]
    landed = [l_wa, l_woa, l_wkv, l_wb, l_wob, l_small]
    got = [g_wa, g_woa, g_wkv, g_wb, g_wob, g_small]

    big = [(w_in_a, m_w_in_a, v_w_in_a), (w_out_a, m_w_out_a, v_w_out_a), (w_kv, m_w_kv, v_w_kv),
           (w_in_b, m_w_in_b, v_w_in_b), (w_out_b, m_w_out_b, v_w_out_b)]
    big_out = []
    for i, (w, m, v) in enumerate(big):
        shape2 = got[i].shape[1:]
        res = _adamw_shard(place, full[i], landed[i], got[i], w.reshape(shape2), m.reshape(shape2),
                           v.reshape(shape2))
        big_out.append([r.reshape(w.shape) for r in res])

    small_all = _gather_small(full[5], landed[5], got[5]).reshape(N_DEV * SMALL_ROWS, CHUNK)
    g_ws, g_bsp, g_sink, g_rel, g_pg, g_pb, g_lng, g_lnb, loss_all = _unpack_small(small_all, small_shapes)
    g_ws = g_ws.reshape(w_spatial.shape)
    g_bsp = g_bsp.reshape(b_spatial.shape)
    g_pg = g_pg.reshape(post_ln_g.shape)
    g_pb = g_pb.reshape(post_ln_b.shape)
    g_lng = lax.dynamic_slice(g_lng.reshape(1, A_WIDTH), (0, me * sgu_ln_g.shape[1]), sgu_ln_g.shape)
    g_lnb = lax.dynamic_slice(g_lnb.reshape(1, A_WIDTH), (0, me * sgu_ln_b.shape[1]), sgu_ln_b.shape)
    small_params = [(sgu_ln_g, g_lng, m_sgu_ln_g, v_sgu_ln_g), (sgu_ln_b, g_lnb, m_sgu_ln_b, v_sgu_ln_b),
                    (w_spatial, g_ws, m_w_spatial, v_w_spatial), (b_spatial, g_bsp, m_b_spatial, v_b_spatial),
                    (attn_sinks, g_sink, m_attn_sinks, v_attn_sinks), (rel_t, g_rel, m_rel_bias.T, v_rel_bias.T),
                    (post_ln_g, g_pg, m_post_ln_g, v_post_ln_g), (post_ln_b, g_pb, m_post_ln_b, v_post_ln_b)]
    flat2 = [tuple(a.reshape(-1, a.shape[-1]) for a in p) for p in small_params]
    small_out = [tuple(r.reshape(p[0].shape) for r in res) for res, p in zip(_adamw_small(flat2), small_params)]

    order = [("big", 0), ("small", 0), ("small", 1), ("small", 2), ("small", 3), ("big", 1), ("big", 2), ("big", 3),
             ("small", 4), ("small", 5), ("big", 4), ("small", 6), ("small", 7)]
    grads, deltas, new_m, new_v = [], [], [], []
    for kind, i in order:
        if kind == "big":
            g, d, m, v = big_out[i]
        else:
            g = small_params[i][1]
            d, m, v = small_out[i]
            if small_params[i][0] is rel_t:
                g, d, m, v = g.T, d.T, m.T, v.T
        grads.append(g)
        deltas.append(d)
        new_m.append(m)
        new_v.append(v)
    return (loss_all[0, 0], dx.reshape(x.shape), *grads, *deltas, *new_m, *new_v)
```

```python
import functools

import jax
import jax.numpy as jnp
import numpy as np
from jax import lax
from jax.experimental import pallas as pl
from jax.experimental.pallas import tpu as pltpu

F32 = jnp.float32
BF16 = jnp.bfloat16

D_MODEL = 1024
A_WIDTH = 2048
A_GROUPS = 8
A_GROUP_DIM = A_WIDTH // A_GROUPS
CHUNK = 128
HEAD_DIM = 64
N_Q_HEADS = 16
N_KV_HEADS = 2
PAIRS_PER_KV = N_Q_HEADS // N_KV_HEADS // 2
B_WIDTH = N_Q_HEADS * HEAD_DIM
KV_WIDTH = N_KV_HEADS * HEAD_DIM
REL_BUCKETS = 32
REL_MAX_DIST = 128
DEPTH = 2
ALPHA = (2.0 * DEPTH) ** 0.25
LN_EPS = 1e-5
NEG_INF = -1e30
SCALE = HEAD_DIM ** -0.5

ADAM_LR = 0.001
ADAM_B1 = 0.9
ADAM_B2 = 0.999
ADAM_EPS = 1e-08
ADAM_WD = 0.01
ADAM_STEP = 10

N_DEV = 8
MESH = pl.DeviceIdType.MESH
TOKENS_IN_PROJ = 1024
TOKENS_WGRAD = 4096
TOKENS_A = 256
TOKENS_A_FWD = 256
P_RING = 3
TOKENS_B = 512
PART_ROWS_B = 256
TOKENS_B_BWD = 256
TOKENS_DX = 512
VMEM_LIMIT_BYTES = 60 * 1024 * 1024
SMALL_ROWS = 144

ANY = pl.BlockSpec(memory_space=pl.ANY)
VMEM = pl.BlockSpec(memory_space=pltpu.VMEM)
SMEM = pl.BlockSpec(memory_space=pltpu.SMEM)


def _nn(a, b):
    return jnp.dot(a, b, preferred_element_type=F32)


def _nt(a, b):
    return lax.dot_general(a, b, (((1,), (1,)), ((), ())), preferred_element_type=F32)


def _tn(a, b):
    return lax.dot_general(a, b, (((0,), (0,)), ((), ())), preferred_element_type=F32)


def _resident(shape):
    nd = len(shape)
    return pl.BlockSpec(shape, lambda i: (0,) * nd, pipeline_mode=pl.Buffered(1))


def _rows(tm, width, reverse_of=None):
    if reverse_of is None:
        return pl.BlockSpec((tm, width), lambda i: (i, 0))
    return pl.BlockSpec((tm, width), lambda i: (reverse_of - 1 - i, 0))


def _seq_params():
    return pltpu.CompilerParams(dimension_semantics=("arbitrary",), vmem_limit_bytes=VMEM_LIMIT_BYTES)


def _in_turns(first, second):
    merged = [(i / max(len(first), 1), 0, f) for i, f in enumerate(first)]
    merged += [((i + 0.5) / max(len(second), 1), 1, f) for i, f in enumerate(second)]
    return [f for _, _, f in sorted(merged, key=lambda t: t[:2])]


def _ln_normalize(r):
    mu = jnp.mean(r, axis=-1, keepdims=True)
    xc = r - mu
    var = jnp.mean(xc * xc, axis=-1, keepdims=True)
    rstd = lax.rsqrt(var + LN_EPS)
    return xc * rstd, rstd


def _ln_backward(dxhat, xhat, rstd):
    m1 = jnp.mean(dxhat, axis=-1, keepdims=True)
    m2 = jnp.mean(dxhat * xhat, axis=-1, keepdims=True)
    return rstd * (dxhat - m1 - xhat * m2)


def _silu_and_grad(z):
    sig = jax.nn.sigmoid(z)
    silu = z * sig
    return silu, sig + silu * (1.0 - sig)


def _band_tables():
    t = np.arange(CHUNK, dtype=np.int32)[:, None]
    j = np.arange(2 * CHUNK, dtype=np.int32)[None, :]
    d = t + CHUNK - j
    in_window = (d >= 0) & (d < CHUNK)
    dc = np.clip(d, 0, REL_MAX_DIST - 1)
    max_exact = REL_BUCKETS // 2
    df = np.maximum(dc, 1).astype(np.float32)
    large = max_exact + (np.log(df / np.float32(max_exact)) / np.float32(np.log(REL_MAX_DIST / max_exact))
                         * np.float32(REL_BUCKETS - max_exact)).astype(np.int32)
    large = np.minimum(large, REL_BUCKETS - 1)
    bucket = np.where(dc < max_exact, dc, large)
    return np.where(in_window, bucket, -1).astype(np.int32)


def _use_order():
    x, y, c = lax.axis_index("x"), lax.axis_index("y"), lax.axis_index("c")
    blocks = [(x, y, c), (x, y, 1 - c)]
    for px, py in [(1 - x, y), (x, 1 - y), (1 - x, 1 - y)]:
        blocks += [(px, py, c), (px, py, 1 - c)]
    return blocks


def _gather_in_proj(order, x, w_in_a, small, ln_gb):
    seq = x.shape[0]
    tm = min(TOKENS_IN_PROJ, seq)
    n_blocks = seq // tm
    shard_cols = w_in_a.shape[1]
    n = len(small) + 2

    def body(order_ref, x_ref, *refs):
        ins, refs = refs[:n], refs[n:]
        p_ref, outs, xb_out, refs = refs[0], refs[1:n + 1], refs[n + 1], refs[n + 2:]
        wa_vmem, xb_vmem, stage = refs[0], refs[1], refs[2:n]
        send_sems, recv_sems, local_sems, out_sems, xb_sem = refs[n:]
        xb_copy = pltpu.make_async_copy(xb_vmem, xb_out, xb_sem)
        phase, blk = pl.program_id(0), pl.program_id(1)
        x_, y_, c_ = lax.axis_index("x"), lax.axis_index("y"), lax.axis_index("c")
        me, sibling = (x_, y_, c_), (x_, y_, 1 - c_)
        chips = [(1 - x_, y_), (x_, 1 - y_), (1 - x_, 1 - y_)]
        use = _use_order()

        def index(px, py, pc):
            return 4 * px + 2 * py + pc

        def slot(a, block):
            return (wa_vmem if a == 0 else outs[a]).at[index(*block)]

        def own(a):
            return slot(0, me) if a == 0 else (ins[a] if a == n - 1 else stage[a - 1])

        def copy(a, k, block, to, src=None):
            return pltpu.make_async_remote_copy(
                src_ref=slot(a, block) if src is None else src, dst_ref=slot(a, block),
                send_sem=send_sems.at[a, k], recv_sem=recv_sems.at[a, k], device_id=to, device_id_type=MESH)

        def keep(a):
            return pltpu.make_async_copy(own(a), slot(a, me), local_sems.at[a])

        def write_out(k):
            return pltpu.make_async_copy(slot(0, use[k]), outs[0].at[index(*use[k])], out_sems.at[k])

        def pass_on(a, j):
            copy(a, 1 + j, (*chips[j], c_), me).wait_recv()
            copy(a, 4 + j, (*chips[j], c_), sibling).start()

        def send(a, j):
            copy(a, 1 + j, me, (*chips[j], c_), own(a)).start()

        @pl.when((phase == 0) & (blk == 0))
        def _():
            wa_vmem[index(*me)] = ins[0][...].astype(BF16)
            for a in range(1, n - 1):
                stage[a - 1][...] = ins[a][...].astype(BF16)
            send(0, 0)
            send(0, 1)
            for a in range(n):
                copy(a, 0, me, sibling, own(a)).start()
                if a:
                    keep(a).start()
            write_out(0).start()
            copy(0, 0, sibling, me).wait_recv()
            write_out(1).start()

        @pl.when((phase == 1) & (blk == 0))
        def _():
            pass_on(0, 0)
            write_out(2).start()
            xb_copy.start()
            send(0, 2)
            for a in range(1, n):
                for j in range(3):
                    send(a, j)
            pass_on(0, 1)
            write_out(4).start()
            copy(0, 4, (*chips[0], 1 - c_), me).wait_recv()
            write_out(3).start()

        @pl.when((phase == 2) & (blk == 0))
        def _():
            copy(0, 5, (*chips[1], 1 - c_), me).wait_recv()
            write_out(5).start()

        @pl.when((phase == 2) & (blk == n_blocks // 2))
        def _():
            pass_on(0, 2)
            write_out(6).start()

        @pl.when((phase == 3) & (blk == 0))
        def _():
            copy(0, 6, (*chips[2], 1 - c_), me).wait_recv()
            write_out(7).start()
            for a in range(1, n):
                pass_on(a, 0)
                pass_on(a, 1)

        rows = pl.ds(pl.multiple_of(blk * tm, tm), tm)

        @pl.when(phase == 0)
        def _():
            xb_vmem[rows, :] = x_ref[...].astype(BF16)

        chip = order_ref[phase]
        xb = xb_vmem[rows, :]
        p_ref[:, :shard_cols] = _nn(xb, wa_vmem[2 * chip])
        p_ref[:, shard_cols:] = _nn(xb, wa_vmem[2 * chip + 1])

        @pl.when((phase == N_DEV // 2 - 1) & (blk == n_blocks - 1))
        def _():
            for a in range(1, n):
                pass_on(a, 2)
            for a in range(1, n):
                copy(a, 0, sibling, me).wait_recv()
                for j, chip in enumerate(chips):
                    copy(a, 4 + j, (*chip, 1 - c_), me).wait_recv()
            for a in range(n):
                for k in range(7):
                    copy(a, k, me, me).wait_send()
                if a:
                    keep(a).wait()
            for k in range(N_DEV):
                write_out(k).wait()
            xb_copy.wait()

    gathered = [jax.ShapeDtypeStruct((N_DEV,) + w_in_a.shape, BF16)]
    gathered += [jax.ShapeDtypeStruct((N_DEV,) + s.shape, BF16) for s in small]
    gathered.append(jax.ShapeDtypeStruct((N_DEV,) + ln_gb.shape, F32))
    return pl.pallas_call(
        body, name="gather_in_proj",
        out_shape=[jax.ShapeDtypeStruct((seq, N_DEV * shard_cols), F32)] + gathered
        + [jax.ShapeDtypeStruct((seq, D_MODEL), BF16)],
        grid_spec=pltpu.PrefetchScalarGridSpec(
            num_scalar_prefetch=1, grid=(N_DEV // 2, n_blocks),
            in_specs=[pl.BlockSpec((tm, D_MODEL), lambda k, i, order: (jnp.where(k == 0, i, n_blocks - 1), 0))]
            + [VMEM] * n,
            out_specs=[pl.BlockSpec((tm, 2 * shard_cols), lambda k, i, order: (i, order[k]))] + [ANY] * (n + 1),
            scratch_shapes=[pltpu.VMEM((N_DEV,) + w_in_a.shape, BF16), pltpu.VMEM((seq, D_MODEL), BF16)]
            + [pltpu.VMEM(s.shape, BF16) for s in small] + [
                pltpu.SemaphoreType.DMA((n, 7)), pltpu.SemaphoreType.DMA((n, 7)), pltpu.SemaphoreType.DMA((n,)),
                pltpu.SemaphoreType.DMA((N_DEV,)), pltpu.SemaphoreType.DMA(())]),
        compiler_params=pltpu.CompilerParams(dimension_semantics=("arbitrary", "arbitrary"),
                                             vmem_limit_bytes=VMEM_LIMIT_BYTES),
    )(order, x, w_in_a, *small, ln_gb)


def _bias_table(rel_bias_t, bucket):
    def body(rel_ref, bucket_ref, out_ref):
        first = jnp.broadcast_to(bucket_ref[0:1, :], (8, 2 * CHUNK))
        col = lax.broadcasted_iota(jnp.int32, (CHUNK, 2 * CHUNK), 1)
        for h in range(N_Q_HEADS):
            row = jnp.full(first.shape, NEG_INF, F32)
            for b in range(REL_BUCKETS):
                row = jnp.where(first == b, rel_ref[h, b], row)
            tab = pltpu.roll(jnp.broadcast_to(row[0:1, :], (CHUNK, 2 * CHUNK)), 0, 1, stride=1, stride_axis=0)
            pair = h // 2
            rows = slice((pair % PAIRS_PER_KV) * CHUNK, (pair % PAIRS_PER_KV + 1) * CHUNK)
            half = slice((h % 2) * 2 * CHUNK, (h % 2 + 1) * 2 * CHUNK)
            out_ref[1, pair // PAIRS_PER_KV, rows, half] = tab
            out_ref[0, pair // PAIRS_PER_KV, rows, half] = jnp.where(col < CHUNK, NEG_INF, tab)

    return pl.pallas_call(
        body, name="bias_table",
        out_shape=jax.ShapeDtypeStruct((2, N_KV_HEADS, PAIRS_PER_KV * CHUNK, 4 * CHUNK), F32),
        in_specs=[SMEM, VMEM], out_specs=VMEM,
    )(rel_bias_t, bucket)


def _rel_bias_grad(dlogits, bucket):
    def body(dl_ref, bucket_ref, out_ref):
        i = lax.broadcasted_iota(jnp.int32, (CHUNK, CHUNK), 0)
        k = lax.broadcasted_iota(jnp.int32, (CHUNK, CHUNK), 1)
        reverse = jnp.where(i + k == CHUNK - 1, 1.0, 0.0)
        sums = []
        for h in range(N_Q_HEADS):
            pair = h // 2
            dl = dl_ref[pair // PAIRS_PER_KV, (pair % PAIRS_PER_KV) * CHUNK:(pair % PAIRS_PER_KV + 1) * CHUNK,
                        (h % 2) * 2 * CHUNK:(h % 2 + 1) * 2 * CHUNK]
            backwards = jnp.dot(reverse, dl, precision=lax.Precision.HIGHEST, preferred_element_type=F32)
            turned = pltpu.roll(backwards, CHUNK + 1, 1, stride=1, stride_axis=0)
            sums.append(jnp.sum(turned, axis=0, keepdims=True))
        by_offset = jnp.concatenate(sums, axis=0)
        buckets = lax.broadcasted_iota(jnp.int32, (REL_BUCKETS, 2 * CHUNK), 0)
        one_hot = jnp.where(jnp.broadcast_to(bucket_ref[0:1, :], buckets.shape) == buckets, 1.0, 0.0)
        out_ref[...] = lax.dot_general(by_offset, one_hot, (((1,), (1,)), ((), ())),
                                       precision=lax.Precision.HIGHEST, preferred_element_type=F32)

    return pl.pallas_call(
        body, name="rel_bias_grad", out_shape=jax.ShapeDtypeStruct((N_Q_HEADS, REL_BUCKETS), F32),
        in_specs=[VMEM, VMEM], out_specs=VMEM,
    )(dlogits, bucket)


def _masked_spatial(ws_ref, g):
    t = lax.broadcasted_iota(jnp.int32, (CHUNK, CHUNK), 0)
    s = lax.broadcasted_iota(jnp.int32, (CHUNK, CHUNK), 1)
    return jnp.where(s <= t, ws_ref[g], 0.0).astype(BF16)


def _spatial_gate(ws_ref, bsp_t_ref, vn, s_ref, n_chunks):
    for g in range(A_GROUPS):
        w = _masked_spatial(ws_ref, g)
        cols = slice(g * A_GROUP_DIM, (g + 1) * A_GROUP_DIM)
        for ci in range(n_chunks):
            rows = slice(ci * CHUNK, (ci + 1) * CHUNK)
            s_ref[rows, cols] = _nn(w, vn[rows, cols]) + bsp_t_ref[:, g:g + 1]


def _fwd_a(x, p, ln_gb, w_spatial, bsp_t, woa, later):
    seq = x.shape[0]
    tm = min(TOKENS_A_FWD, seq)
    n_chunks = tm // CHUNK
    n_blocks = seq // tm
    n = len(later)

    def body(x_ref, p_hbm, ln_ref, ws_ref, bsp_ref, woa_ref, *refs):
        shards, refs = refs[:n], refs[n:]
        r0_ref, gathered, refs = refs[0], refs[1:1 + n], refs[1 + n:]
        s_ref, p_ring, stage, (send_sems, recv_sems, local_sems, ring_sems) = refs[0], refs[1], refs[2:2 + n], refs[2 + n:]
        step = pl.program_id(0)

        def fetch(block):
            slot = block % P_RING
            start = block * tm if isinstance(block, int) else pl.multiple_of(block * tm, tm)
            return pltpu.make_async_copy(p_hbm.at[pl.ds(start, tm), :], p_ring.at[slot], ring_sems.at[slot])

        @pl.when(step == 0)
        def _():
            for ahead in range(min(P_RING - 1, n_blocks)):
                fetch(ahead).start()

        @pl.when(step + P_RING - 1 < n_blocks)
        def _():
            fetch(step + P_RING - 1).start()

        fetch(step).wait()
        p_ref = p_ring.at[step % P_RING]

        def copies(a):
            x_, y_, c_ = lax.axis_index("x"), lax.axis_index("y"), lax.axis_index("c")
            mine = 4 * x_ + 2 * y_ + c_
            out = [pltpu.make_async_copy(stage[a], gathered[a].at[mine], local_sems.at[a])]
            for r in range(1, N_DEV):
                to = (1 - x_ if r & 4 else x_, 1 - y_ if r & 2 else y_, 1 - c_ if r & 1 else c_)
                out.append(pltpu.make_async_remote_copy(
                    src_ref=stage[a], dst_ref=gathered[a].at[mine], send_sem=send_sems.at[a, r - 1],
                    recv_sem=recv_sems.at[a, r - 1], device_id=to, device_id_type=MESH))
            return out

        @pl.when(step == 0)
        def _():
            for a in range(n):
                stage[a][...] = shards[a][...].astype(BF16)
                for cp in copies(a):
                    cp.start()

        xv = x_ref[...]
        vhat, _ = _ln_normalize(p_ref[:, A_WIDTH:2 * A_WIDTH])
        vn = (vhat * ln_ref[0:1, :] + ln_ref[1:2, :]).astype(BF16)
        _spatial_gate(ws_ref, bsp_ref, vn, s_ref, n_chunks)
        z = p_ref[:, 2 * A_WIDTH:]
        y = p_ref[:, :A_WIDTH] * s_ref[...] * (z * jax.nn.sigmoid(z))
        r0_ref[...] = ALPHA * xv + _nn(y.astype(BF16), woa_ref[...])

        @pl.when(step == n_blocks - 1)
        def _():
            for a in range(n):
                for cp in copies(a):
                    cp.wait()

    return pl.pallas_call(
        body, name="fwd_a", grid=(n_blocks,),
        out_shape=[jax.ShapeDtypeStruct((seq, D_MODEL), F32)]
        + [jax.ShapeDtypeStruct((N_DEV,) + s.shape, BF16) for s in later],
        in_specs=[_rows(tm, D_MODEL), ANY, _resident(ln_gb.shape), _resident(w_spatial.shape),
                  _resident(bsp_t.shape), _resident(woa.shape)] + [VMEM] * n,
        out_specs=[_rows(tm, D_MODEL)] + [ANY] * n,
        scratch_shapes=[pltpu.VMEM((tm, A_WIDTH), F32), pltpu.VMEM((P_RING, tm, 3 * A_WIDTH), F32)]
        + [pltpu.VMEM(s.shape, BF16) for s in later] + [
            pltpu.SemaphoreType.DMA((n, N_DEV - 1)), pltpu.SemaphoreType.DMA((n, N_DEV - 1)),
            pltpu.SemaphoreType.DMA((n,)), pltpu.SemaphoreType.DMA((P_RING,))],
        compiler_params=_seq_params(),
    )(x, p, ln_gb, w_spatial, bsp_t, woa, *later)


def _expand_heads(kv):
    lo = lax.broadcasted_iota(jnp.int32, kv.shape, 1) < HEAD_DIM
    swapped = pltpu.roll(kv, HEAD_DIM, 1)
    zero = jnp.zeros_like(kv)
    return (jnp.where(lo, kv, zero).astype(BF16), jnp.where(lo, zero, swapped).astype(BF16),
            jnp.where(lo, swapped, zero).astype(BF16), jnp.where(lo, zero, kv).astype(BF16))


def _fold_heads(acc, kvh):
    top, bot = acc[:2 * CHUNK], acc[2 * CHUNK:]
    lo = lax.broadcasted_iota(jnp.int32, top.shape, 1) < HEAD_DIM
    t = jnp.where(lo, top, 0.0)
    b = jnp.where(lo, 0.0, bot)
    if kvh == 0:
        return t + pltpu.roll(b, HEAD_DIM, 1)
    return pltpu.roll(t, HEAD_DIM, 1) + b


STACK = PAIRS_PER_KV * CHUNK


def _pair_cols(kvh, pp):
    pair = kvh * PAIRS_PER_KV + pp
    return slice(pair * CHUNK, (pair + 1) * CHUNK)


def _stack_pairs(ref_or_val, rows, kvh):
    return jnp.concatenate([ref_or_val[rows, _pair_cols(kvh, pp)] for pp in range(PAIRS_PER_KV)], axis=0)


def _sink_columns(sink_ref):
    blk = lax.broadcasted_iota(jnp.int32, (STACK, 1), 0) // CHUNK
    out = []
    for kvh in range(N_KV_HEADS):
        halves = []
        for half in range(2):
            col = jnp.zeros((STACK, 1), F32)
            for pp in range(PAIRS_PER_KV):
                col = jnp.where(blk == pp, sink_ref[0, 2 * (kvh * PAIRS_PER_KV + pp) + half], col)
            halves.append(col)
        out.append(halves)
    return out


def _softmax_stack(logits, sink_cols):
    ps, sinks = [], []
    for half in range(2):
        lg = logits[:, half * 2 * CHUNK:(half + 1) * 2 * CHUNK]
        sk = sink_cols[half]
        m = jnp.maximum(jnp.max(lg, axis=-1, keepdims=True), sk)
        e = jnp.exp(lg - m)
        es = jnp.exp(sk - m)
        inv = 1.0 / (jnp.sum(e, axis=-1, keepdims=True) + es)
        ps.append(e * inv)
        sinks.append(es * inv)
    return jnp.concatenate(ps, axis=1), sinks


def _band_operands(before_k, this_k, before_v, this_v, kvh):
    kblk = jnp.concatenate([before_k[2 * kvh], this_k[2 * kvh], before_k[2 * kvh + 1], this_k[2 * kvh + 1]], axis=0)
    vblk = jnp.concatenate([before_v[2 * kvh], this_v[2 * kvh], before_v[2 * kvh + 1], this_v[2 * kvh + 1]], axis=0)
    return kblk, vblk


def _fwd_b(r0, target, wb, wkv, wob, post_g, post_b, sinks, bias):
    seq = r0.shape[0]
    tm = min(TOKENS_B, seq)
    n_chunks = tm // CHUNK
    part_rows = min(PART_ROWS_B, tm)
    n_parts = tm // part_rows
    part_chunks = part_rows // CHUNK
    shard_cols = wb.shape[2]

    def body(r0_ref, tgt_ref, wb_ref, wkv_ref, wob_ref, pg_ref, pb_ref, sink_ref, bias_ref,
             q_ref, z_ref, kv_ref, dr1_ref, loss_ref, dg_ref, db_ref, ek_last, ev_last, *scratch):
        q_s, z_s, h1_s, o_s, ek_s, ev_s = (scratch[i * n_parts:(i + 1) * n_parts] for i in range(6))
        step = pl.program_id(0)

        @pl.when(step == 0)
        def _():
            ek_last[...] = jnp.zeros_like(ek_last)
            ev_last[...] = jnp.zeros_like(ev_last)
            loss_ref[...] = jnp.zeros_like(loss_ref)
            dg_ref[...] = jnp.zeros_like(dg_ref)
            db_ref[...] = jnp.zeros_like(db_ref)

        sink_cols = _sink_columns(sink_ref)
        g1 = pg_ref[1:2, :]

        def project_steps(part):
            rows = slice(part * part_rows, (part + 1) * part_rows)
            live = {}

            def normalize():
                xhat0, _ = _ln_normalize(r0_ref[rows, :])
                h1 = xhat0 * pg_ref[0:1, :] + pb_ref[0:1, :]
                h1_s[part][...] = h1
                live["h1b"] = h1.astype(BF16)

            def shard(k):
                def run():
                    qz = _nn(live["h1b"], wb_ref[k])
                    lo, hi = k * shard_cols, (k + 1) * shard_cols
                    if hi <= B_WIDTH:
                        qs = (qz * SCALE).astype(BF16)
                        q_ref[rows, lo:hi] = qs
                        q_s[part][:, lo:hi] = qs
                    else:
                        z_ref[rows, lo - B_WIDTH:hi - B_WIDTH] = qz
                        z_s[part][:, lo - B_WIDTH:hi - B_WIDTH] = qz
                return run

            def keys():
                kv = _nn(live["h1b"], wkv_ref[...])
                kv_ref[rows, :] = kv.astype(BF16)
                for cj in range(part_chunks):
                    sub = slice(cj * CHUNK, (cj + 1) * CHUNK)
                    for i, e in enumerate(_expand_heads(kv[sub, :KV_WIDTH])):
                        ek_s[part][cj, i] = e
                    for i, e in enumerate(_expand_heads(kv[sub, KV_WIDTH:])):
                        ev_s[part][cj, i] = e

            return [normalize, keys] + [shard(k) for k in range(N_DEV)]

        def attend_steps(part):
            def stack(cj, kvh):
                def run():
                    sub = slice(cj * CHUNK, (cj + 1) * CHUNK)
                    sel = jnp.where(step * n_chunks + part * part_chunks + cj > 0, 1, 0)
                    if cj:
                        before_k, before_v = ek_s[part].at[cj - 1], ev_s[part].at[cj - 1]
                    elif part:
                        before_k, before_v = ek_s[part - 1].at[part_chunks - 1], ev_s[part - 1].at[part_chunks - 1]
                    else:
                        before_k, before_v = ek_last, ev_last
                    kblk, vblk = _band_operands(before_k, ek_s[part].at[cj], before_v, ev_s[part].at[cj], kvh)
                    logits = _nt(_stack_pairs(q_s[part], sub, kvh), kblk) + bias_ref[sel, kvh]
                    p, _ = _softmax_stack(logits, sink_cols[kvh])
                    o = _nn(p.astype(BF16), vblk)
                    for pp in range(PAIRS_PER_KV):
                        o_s[part][sub, _pair_cols(kvh, pp)] = o[pp * CHUNK:(pp + 1) * CHUNK]
                return run

            def finish():
                rows = slice(part * part_rows, (part + 1) * part_rows)
                z = z_s[part][...]
                y = o_s[part][...] * (z * jax.nn.sigmoid(z))
                r1 = ALPHA * h1_s[part][...] + _nn(y.astype(BF16), wob_ref[...])
                xhat1, rstd1 = _ln_normalize(r1)
                diff = xhat1 * g1 + pb_ref[1:2, :] - tgt_ref[rows, :]
                loss_ref[...] += 0.5 * jnp.sum(jnp.mean(diff * diff, axis=-1, keepdims=True), axis=0, keepdims=True)
                dh2 = diff * (1.0 / D_MODEL)
                dg_ref[...] += jnp.sum(dh2 * xhat1, axis=0, keepdims=True)
                db_ref[...] += jnp.sum(dh2, axis=0, keepdims=True)
                dr1_ref[rows, :] = _ln_backward(dh2 * g1, xhat1, rstd1)

            return [stack(cj, kvh) for cj in range(part_chunks) for kvh in range(N_KV_HEADS)] + [finish]

        for run in project_steps(0):
            run()
        for part in range(n_parts):
            for run in _in_turns(attend_steps(part), project_steps(part + 1) if part + 1 < n_parts else []):
                run()
        ek_last[...] = ek_s[n_parts - 1][part_chunks - 1]
        ev_last[...] = ev_s[n_parts - 1][part_chunks - 1]

    vec = pl.BlockSpec((1, D_MODEL), lambda i: (0, 0))
    return pl.pallas_call(
        body, name="fwd_b", grid=(seq // tm,),
        out_shape=[jax.ShapeDtypeStruct((seq, B_WIDTH), BF16), jax.ShapeDtypeStruct((seq, B_WIDTH), F32),
                   jax.ShapeDtypeStruct((seq, 2 * KV_WIDTH), BF16), jax.ShapeDtypeStruct((seq, D_MODEL), F32),
                   jax.ShapeDtypeStruct((1, 1), F32), jax.ShapeDtypeStruct((1, D_MODEL), F32),
                   jax.ShapeDtypeStruct((1, D_MODEL), F32)],
        in_specs=[_rows(tm, D_MODEL), _rows(tm, D_MODEL), _resident(wb.shape), _resident(wkv.shape),
                  _resident(wob.shape), _resident(post_g.shape), _resident(post_b.shape), SMEM,
                  _resident(bias.shape)],
        out_specs=[_rows(tm, B_WIDTH), _rows(tm, B_WIDTH), _rows(tm, 2 * KV_WIDTH), _rows(tm, D_MODEL),
                   pl.BlockSpec((1, 1), lambda i: (0, 0)), vec, vec],
        scratch_shapes=[pltpu.VMEM((4, CHUNK, CHUNK), BF16), pltpu.VMEM((4, CHUNK, CHUNK), BF16)]
        + [pltpu.VMEM((part_rows, B_WIDTH), BF16)] * n_parts + [pltpu.VMEM((part_rows, B_WIDTH), F32)] * n_parts
        + [pltpu.VMEM((part_rows, D_MODEL), F32)] * n_parts + [pltpu.VMEM((part_rows, B_WIDTH), F32)] * n_parts
        + [pltpu.VMEM((part_chunks, 4, CHUNK, CHUNK), BF16)] * (2 * n_parts),
        compiler_params=_seq_params(),
    )(r0, target, wb, wkv, wob, post_g, post_b, sinks, bias)


def _bwd_b(r0, dr1, q, z, kv, wb, wkv, wob, post_g, post_b, sinks, bias):
    seq = r0.shape[0]
    tm = min(TOKENS_B_BWD, seq)
    n_chunks = tm // CHUNK
    n_blocks = seq // tm
    shard_cols = wb.shape[2]

    def body(r0_ref, dr1_ref, q_ref, z_ref, kv_ref, kvp_ref, wb_ref, wkv_ref, wob_ref, pg_ref, pb_ref, sink_ref,
             bias_ref, dr0_ref, dwb_out, dwkv_out, dwob_out, dsink_ref, dl_out, dpost_ref,
             dwb_acc, dwkv_acc, dwob_acc, dl_acc, dsink_acc, carry_ref, ek_ref, ev_ref, y_ref, dqz_ref, dkv_ref):
        step = pl.program_id(0)
        block = n_blocks - 1 - step

        @pl.when(step == 0)
        def _():
            dpost_ref[...] = jnp.zeros_like(dpost_ref)
            dwb_acc[...] = jnp.zeros_like(dwb_acc)
            dwkv_acc[...] = jnp.zeros_like(dwkv_acc)
            dwob_acc[...] = jnp.zeros_like(dwob_acc)
            dl_acc[...] = jnp.zeros_like(dl_acc)
            carry_ref[...] = jnp.zeros_like(carry_ref)
            dsink_acc[...] = jnp.zeros_like(dsink_acc)

        xhat0, rstd0 = _ln_normalize(r0_ref[...])
        h1b = (xhat0 * pg_ref[0:1, :] + pb_ref[0:1, :]).astype(BF16)
        dr1 = dr1_ref[...]
        dr1b = dr1.astype(BF16)
        dy = _nt(dr1b, wob_ref[...])
        sg, dsg = _silu_and_grad(z_ref[...])
        do_all = dy * sg
        dyz = dy * dsg

        for ci in range(n_chunks + 1):
            src = kvp_ref[...] if ci == 0 else kv_ref[(ci - 1) * CHUNK:ci * CHUNK, :]
            src = src.astype(F32)
            for i, e in enumerate(_expand_heads(src[:, :KV_WIDTH])):
                ek_ref[ci, i] = e
            for i, e in enumerate(_expand_heads(src[:, KV_WIDTH:])):
                ev_ref[ci, i] = e

        sink_cols = _sink_columns(sink_ref)
        for ci in reversed(range(n_chunks)):
            rows = slice(ci * CHUNK, (ci + 1) * CHUNK)
            sel = jnp.where(block * n_chunks + ci > 0, 1, 0)
            dk_band = jnp.zeros((2 * CHUNK, KV_WIDTH), F32)
            dv_band = jnp.zeros((2 * CHUNK, KV_WIDTH), F32)
            for kvh in range(N_KV_HEADS):
                kblk, vblk = _band_operands(ek_ref.at[ci], ek_ref.at[ci + 1], ev_ref.at[ci], ev_ref.at[ci + 1], kvh)
                qs = _stack_pairs(q_ref, rows, kvh)
                p, psink = _softmax_stack(_nt(qs, kblk) + bias_ref[sel, kvh], sink_cols[kvh])
                pb16 = p.astype(BF16)
                o = _nn(pb16, vblk)
                y = (o * _stack_pairs(sg, rows, kvh)).astype(BF16)
                dz = (o * _stack_pairs(dyz, rows, kvh)).astype(BF16)
                dob = _stack_pairs(do_all, rows, kvh).astype(BF16)
                dp = _nt(dob, vblk)
                dlogits = []
                for half in range(2):
                    hs = slice(half * 2 * CHUNK, (half + 1) * 2 * CHUNK)
                    delta = jnp.sum(dp[:, hs] * p[:, hs], axis=-1, keepdims=True)
                    dlogits.append(p[:, hs] * (dp[:, hs] - delta))
                    dsink_acc[2 * kvh + half] += psink[half] * delta
                dlog = jnp.concatenate(dlogits, axis=1)
                dl_acc[kvh] += dlog
                dsc = dlog.astype(BF16)
                dq = (_nn(dsc, kblk) * SCALE).astype(BF16)
                for pp in range(PAIRS_PER_KV):
                    blk = slice(pp * CHUNK, (pp + 1) * CHUNK)
                    cols = _pair_cols(kvh, pp)
                    y_ref[rows, cols] = y[blk]
                    dqz_ref[rows, cols] = dq[blk]
                    dqz_ref[rows, B_WIDTH + cols.start:B_WIDTH + cols.stop] = dz[blk]
                dk_band = dk_band + _fold_heads(_tn(dsc, qs), kvh)
                dv_band = dv_band + _fold_heads(_tn(pb16, dob), kvh)
            band = jnp.concatenate([dk_band, dv_band], axis=1)
            dkv_ref[rows, :] = (band[CHUNK:] + carry_ref[...]).astype(BF16)
            carry_ref[...] = band[:CHUNK]

        dqz = dqz_ref[...]
        dkv = dkv_ref[...]
        dh1 = ALPHA * dr1 + _nt(dkv, wkv_ref[...])
        for k in range(N_DEV):
            part = dqz[:, k * shard_cols:(k + 1) * shard_cols]
            dh1 = dh1 + _nt(part, wb_ref[k])
            dwb_acc[k] += _tn(h1b, part)
        dpost_ref[0:1, :] += jnp.sum(dh1 * xhat0, axis=0, keepdims=True)
        dpost_ref[1:2, :] += jnp.sum(dh1, axis=0, keepdims=True)
        dr0_ref[...] = _ln_backward(dh1 * pg_ref[0:1, :], xhat0, rstd0)
        dwkv_acc[...] += _tn(h1b, dkv)
        dwob_acc[...] += _tn(y_ref[...], dr1b)

        @pl.when(step == n_blocks - 1)
        def _():
            pltpu.sync_copy(dwb_acc, dwb_out)
            pltpu.sync_copy(dwkv_acc, dwkv_out)
            pltpu.sync_copy(dwob_acc, dwob_out)
            pltpu.sync_copy(dl_acc, dl_out)
            lane = lax.broadcasted_iota(jnp.int32, (1, CHUNK), 1)
            dsink = jnp.zeros((1, CHUNK), F32)
            for kvh in range(N_KV_HEADS):
                for half in range(2):
                    for pp in range(PAIRS_PER_KV):
                        head = 2 * (kvh * PAIRS_PER_KV + pp) + half
                        tot = jnp.sum(dsink_acc[2 * kvh + half, pp * CHUNK:(pp + 1) * CHUNK, :], axis=0, keepdims=True)
                        dsink = dsink - jnp.where(lane == head, tot, 0.0)
            dsink_ref[...] = dsink

    rev = functools.partial(_rows, tm, reverse_of=n_blocks)
    prev_chunk = pl.BlockSpec(
        (CHUNK, 2 * KV_WIDTH), lambda i: (jnp.maximum((n_blocks - 1 - i) * n_chunks - 1, 0), 0))
    return pl.pallas_call(
        body, name="bwd_b", grid=(n_blocks,),
        out_shape=[jax.ShapeDtypeStruct((seq, D_MODEL), F32), jax.ShapeDtypeStruct(wb.shape, F32),
                   jax.ShapeDtypeStruct(wkv.shape, F32), jax.ShapeDtypeStruct(wob.shape, F32),
                   jax.ShapeDtypeStruct((1, CHUNK), F32),
                   jax.ShapeDtypeStruct((N_KV_HEADS, STACK, 4 * CHUNK), F32), jax.ShapeDtypeStruct((2, D_MODEL), F32)],
        in_specs=[rev(D_MODEL), rev(D_MODEL), rev(B_WIDTH), rev(B_WIDTH), rev(2 * KV_WIDTH), prev_chunk,
                  _resident(wb.shape), _resident(wkv.shape), _resident(wob.shape), _resident(post_g.shape),
                  _resident(post_b.shape), SMEM, _resident(bias.shape)],
        out_specs=[rev(D_MODEL), ANY, ANY, ANY, pl.BlockSpec((1, CHUNK), lambda i: (0, 0)), ANY,
                   pl.BlockSpec((2, D_MODEL), lambda i: (0, 0))],
        scratch_shapes=[pltpu.VMEM(wb.shape, F32), pltpu.VMEM(wkv.shape, F32), pltpu.VMEM(wob.shape, F32),
                        pltpu.VMEM((N_KV_HEADS, STACK, 4 * CHUNK), F32), pltpu.VMEM((2 * N_KV_HEADS, STACK, 1), F32),
                        pltpu.VMEM((CHUNK, 2 * KV_WIDTH), F32),
                        pltpu.VMEM((n_chunks + 1, 4, CHUNK, CHUNK), BF16),
                        pltpu.VMEM((n_chunks + 1, 4, CHUNK, CHUNK), BF16),
                        pltpu.VMEM((tm, B_WIDTH), BF16), pltpu.VMEM((tm, 2 * B_WIDTH), BF16),
                        pltpu.VMEM((tm, 2 * KV_WIDTH), BF16)],
        compiler_params=_seq_params(),
    )(r0, dr1, q, z, kv, kv, wb, wkv, wob, post_g, post_b, sinks, bias)


def _bwd_a(dr0, p, ln_gb, w_spatial, bsp_t, woa, exchange):
    seq = dr0.shape[0]
    tm = min(TOKENS_A, seq)
    n_chunks = tm // CHUNK
    n_blocks = seq // tm

    def body(dr0_ref, p_ref, ln_ref, ws_ref, bsp_ref, woa_ref, *refs):
        ex_in, refs = refs[:exchange.n], refs[exchange.n:]
        (dp_ref, dwoa_out, dws_ref, dbsp_ref, dln_ref), refs = refs[:5], refs[5:]
        ex_out, refs = refs[:exchange.n], refs[exchange.n:]
        (dwoa_acc, s_ref, ds_ref, dvn_ref), ex_sems = refs[:4], refs[4:]
        step = pl.program_id(0)

        @pl.when(step == 0)
        def _():
            exchange.start(ex_in, ex_out, ex_sems)
            dwoa_acc[...] = jnp.zeros_like(dwoa_acc)
            dws_ref[...] = jnp.zeros_like(dws_ref)
            dbsp_ref[...] = jnp.zeros_like(dbsp_ref)
            dln_ref[...] = jnp.zeros_like(dln_ref)

        dr0b = dr0_ref[...].astype(BF16)

        vhat, rstd_v = _ln_normalize(p_ref[:, A_WIDTH:2 * A_WIDTH])
        ln_g = ln_ref[0:1, :]
        vn = (vhat * ln_g + ln_ref[1:2, :]).astype(BF16)
        _spatial_gate(ws_ref, bsp_ref, vn, s_ref, n_chunks)
        u = p_ref[:, :A_WIDTH]
        s = s_ref[...]
        sg, dsg = _silu_and_grad(p_ref[:, 2 * A_WIDTH:])
        gate = s * sg
        dwoa_acc[...] += _tn((u * gate).astype(BF16), dr0b)
        dy = _nt(dr0b, woa_ref[...])
        dyu = dy * u
        dp_ref[:, :A_WIDTH] = (dy * gate).astype(BF16)
        dp_ref[:, 2 * A_WIDTH:] = (dyu * (s * dsg)).astype(BF16)
        ds_ref[...] = dyu * sg

        lane = lax.broadcasted_iota(jnp.int32, (CHUNK, CHUNK), 1)
        dbsp = jnp.zeros((CHUNK, CHUNK), F32)
        for g in range(A_GROUPS):
            w = _masked_spatial(ws_ref, g)
            cols = slice(g * A_GROUP_DIM, (g + 1) * A_GROUP_DIM)
            dws = jnp.zeros((CHUNK, CHUNK), F32)
            for ci in range(n_chunks):
                rows = slice(ci * CHUNK, (ci + 1) * CHUNK)
                ds = ds_ref[rows, cols]
                dsb = ds.astype(BF16)
                dvn_ref[rows, cols] = _tn(w, dsb)
                dws = dws + _nt(dsb, vn[rows, cols])
                dbsp = dbsp + jnp.where(lane == g, jnp.sum(ds, axis=-1, keepdims=True), 0.0)
            dws_ref[g] += dws
        dbsp_ref[...] += dbsp

        dvn = dvn_ref[...]
        dln_ref[0:1, :] += jnp.sum(dvn * vhat, axis=0, keepdims=True)
        dln_ref[1:2, :] += jnp.sum(dvn, axis=0, keepdims=True)
        dp_ref[:, A_WIDTH:2 * A_WIDTH] = _ln_backward(dvn * ln_g, vhat, rstd_v).astype(BF16)

        @pl.when(step == n_blocks - 1)
        def _():
            t = lax.broadcasted_iota(jnp.int32, (CHUNK, CHUNK), 0)
            for g in range(A_GROUPS):
                dws_ref[g] = jnp.where(lane <= t, dws_ref[g], 0.0)
            pltpu.sync_copy(dwoa_acc, dwoa_out)
            exchange.wait(ex_in, ex_out, ex_sems)

    def fixed(shape):
        nd = len(shape)
        return pl.BlockSpec(shape, lambda i: (0,) * nd)

    return pl.pallas_call(
        body, name="bwd_a", grid=(n_blocks,),
        out_shape=[jax.ShapeDtypeStruct((seq, 3 * A_WIDTH), BF16),
                   jax.ShapeDtypeStruct(woa.shape, F32), jax.ShapeDtypeStruct(w_spatial.shape, F32),
                   jax.ShapeDtypeStruct((CHUNK, CHUNK), F32), jax.ShapeDtypeStruct((2, A_WIDTH), F32)]
        + exchange.out_shape,
        in_specs=[_rows(tm, D_MODEL), _rows(tm, 3 * A_WIDTH), _resident(ln_gb.shape),
                  _resident(w_spatial.shape), _resident(bsp_t.shape), _resident(woa.shape)] + [ANY] * exchange.n,
        out_specs=[_rows(tm, 3 * A_WIDTH), ANY, fixed(w_spatial.shape),
                   fixed((CHUNK, CHUNK)), fixed((2, A_WIDTH))] + [ANY] * exchange.n,
        scratch_shapes=[pltpu.VMEM(woa.shape, F32), pltpu.VMEM((tm, A_WIDTH), F32),
                        pltpu.VMEM((tm, A_WIDTH), F32), pltpu.VMEM((tm, A_WIDTH), F32)] + exchange.scratch_shapes,
        compiler_params=_seq_params(),
    )(dr0, p, ln_gb, w_spatial, bsp_t, woa, *exchange.inputs)


def _input_grad_blocks(seq):
    tm = min(TOKENS_DX, seq // 8)
    return tm, seq // tm


def _input_grad(dp, dr0, wa, first, n_blocks, earlier, exchange):
    seq = dr0.shape[0]
    tm, _ = _input_grad_blocks(seq)
    shard_cols = wa.shape[2]
    n_own = 3 if earlier is None else 4

    def body(dp_ref, dr0_ref, wa_ref, *refs):
        refs = refs[n_own - 3:]
        ex_in, dx_ref, ex_out, ex_sems = (refs[:exchange.n], refs[exchange.n], refs[exchange.n + 1:2 * exchange.n + 1],
                                          refs[2 * exchange.n + 1:])
        step = pl.program_id(0)

        @pl.when(step == 0)
        def _():
            exchange.start(ex_in, ex_out, ex_sems)

        dx = ALPHA * dr0_ref[...]
        for k in range(N_DEV):
            dx = dx + _nt(dp_ref[:, k * shard_cols:(k + 1) * shard_cols], wa_ref[k])
        dx_ref[...] = dx

        @pl.when(step == n_blocks - 1)
        def _():
            exchange.wait(ex_in, ex_out, ex_sems)

    def rows(width):
        return pl.BlockSpec((tm, width), lambda i: (first + i, 0))

    return pl.pallas_call(
        body, name=f"input_grad_{first}", grid=(n_blocks,),
        out_shape=[jax.ShapeDtypeStruct((seq, D_MODEL), F32)] + exchange.out_shape,
        in_specs=[rows(3 * A_WIDTH), rows(D_MODEL), _resident(wa.shape)] + [ANY] * (n_own - 3 + exchange.n),
        out_specs=[rows(D_MODEL)] + [ANY] * exchange.n,
        scratch_shapes=exchange.scratch_shapes,
        input_output_aliases={} if earlier is None else {3: 0},
        compiler_params=_seq_params(),
    )(dp, dr0, wa, *(() if earlier is None else (earlier,)), *exchange.inputs)


def _in_proj_weight_grad(xb, dp, shard_cols, exchange):
    seq = xb.shape[0]
    tk = min(TOKENS_WGRAD, seq)
    n_k = seq // tk

    def body(x_ref, dp_ref, *refs):
        ex_in, out_ref, ex_out, ex_sems = (refs[:exchange.n], refs[exchange.n], refs[exchange.n + 1:2 * exchange.n + 1],
                                           refs[2 * exchange.n + 1:])
        shard, step = pl.program_id(0), pl.program_id(1)

        @pl.when((shard == 0) & (step == 0))
        def _():
            exchange.start(ex_in, ex_out, ex_sems)

        @pl.when(step == 0)
        def _():
            out_ref[...] = jnp.zeros_like(out_ref)

        out_ref[...] += _tn(x_ref[...], dp_ref[...])

        @pl.when((shard == N_DEV - 1) & (step == n_k - 1))
        def _():
            exchange.wait(ex_in, ex_out, ex_sems)

    return pl.pallas_call(
        body, name="in_proj_weight_grad", grid=(N_DEV, n_k),
        out_shape=[jax.ShapeDtypeStruct((N_DEV, D_MODEL, shard_cols), F32)] + exchange.out_shape,
        in_specs=[pl.BlockSpec((tk, D_MODEL), lambda j, k: (k, 0)),
                  pl.BlockSpec((tk, shard_cols), lambda j, k: (k, j))] + [ANY] * exchange.n,
        out_specs=[pl.BlockSpec((None, D_MODEL, shard_cols), lambda j, k: (j, 0, 0))] + [ANY] * exchange.n,
        scratch_shapes=exchange.scratch_shapes,
        compiler_params=pltpu.CompilerParams(dimension_semantics=("arbitrary", "arbitrary"),
                                             vmem_limit_bytes=VMEM_LIMIT_BYTES),
    )(xb, dp, *exchange.inputs)


class _Exchange:
    def __init__(self, pair=(), chip=(), gather=()):
        self.pair, self.chip, self.gather = tuple(pair), tuple(chip), tuple(gather)
        self.inputs = self.pair + self.chip + self.gather
        self.n = len(self.inputs)
        self.out_shape = [jax.ShapeDtypeStruct((4, 1) + g.shape[2:], F32) for g in self.pair]
        self.out_shape += [jax.ShapeDtypeStruct((3,) + s.shape[1:], s.dtype) for s in self.chip]
        self.out_shape += [jax.ShapeDtypeStruct((N_DEV,) + g.shape, g.dtype) for g in self.gather]
        self.scratch_shapes = [pltpu.SemaphoreType.DMA((max(len(self.pair), 1),)) for _ in range(2)]
        self.scratch_shapes += [pltpu.SemaphoreType.DMA((max(len(self.chip), 1), 3)) for _ in range(2)]
        self.scratch_shapes += [pltpu.SemaphoreType.DMA((max(len(self.gather), 1), N_DEV)) for _ in range(2)]

    def _copies(self, ins, outs, sems):
        if not self.n:
            return []
        pair_send, pair_recv, chip_send, chip_recv, gather_send, gather_recv = sems
        x, y, c = lax.axis_index("x"), lax.axis_index("y"), lax.axis_index("c")
        copies = []
        for a in range(len(self.gather)):
            src = ins[len(self.pair) + len(self.chip) + a]
            dst = outs[len(self.pair) + len(self.chip) + a].at[4 * x + 2 * y + c]
            copies.append(pltpu.make_async_copy(src, dst, gather_send.at[a, 0]))
            for r in range(1, N_DEV):
                to = (1 - x if r & 4 else x, 1 - y if r & 2 else y, 1 - c if r & 1 else c)
                copies.append(pltpu.make_async_remote_copy(
                    src_ref=src, dst_ref=dst, send_sem=gather_send.at[a, r], recv_sem=gather_recv.at[a, r],
                    device_id=to, device_id_type=MESH))
        for a in range(len(self.pair)):
            copies.append(pltpu.make_async_remote_copy(
                src_ref=ins[a].at[:, pl.ds(1 - c, 1)], dst_ref=outs[a],
                send_sem=pair_send.at[a], recv_sem=pair_recv.at[a], device_id=(x, y, 1 - c), device_id_type=MESH))
        for a in range(len(self.chip)):
            src, dst = ins[len(self.pair) + a], outs[len(self.pair) + a]
            for k, (px, py) in enumerate([(1 - x, y), (x, 1 - y), (1 - x, 1 - y)]):
                copies.append(pltpu.make_async_remote_copy(
                    src_ref=src.at[2 * px + py], dst_ref=dst.at[k], send_sem=chip_send.at[a, k],
                    recv_sem=chip_recv.at[a, k], device_id=(px, py, c), device_id_type=MESH))
        return copies

    def start(self, ins, outs, sems):
        for cp in self._copies(ins, outs, sems):
            cp.start()

    def wait(self, ins, outs, sems):
        for cp in self._copies(ins, outs, sems):
            cp.wait()


def _row_tile(rows, cols, itemsize=4):
    if rows * cols * itemsize <= (1 << 20):
        return rows
    t = 8
    while 2 * t * cols * itemsize <= (1 << 20) and rows % (2 * t) == 0:
        t *= 2
    assert rows % t == 0, (rows, cols)
    return t


def _pair_sum(place, grads, landeds, dtypes):
    n = len(grads)

    def body(place_ref, *refs):
        for a in range(n):
            refs[2 * n + a][...] = (refs[2 * a][...] + refs[2 * a + 1][...]).astype(dtypes[a])

    in_specs, out_specs, operands = [], [], []
    for g, l in zip(grads, landeds):
        rows, cols = g.shape[2:]
        in_specs += [pl.BlockSpec((None, None, rows, cols), lambda k, pr: (k, pr[1], 0, 0)),
                     pl.BlockSpec((None, None, rows, cols), lambda k, pr: (k, 0, 0, 0))]
        out_specs.append(pl.BlockSpec((None, rows, cols), lambda k, pr: (k, 0, 0)))
        operands += [g, l]
    return pl.pallas_call(
        body, name="pair_sum",
        out_shape=[jax.ShapeDtypeStruct((4,) + g.shape[2:], dt) for g, dt in zip(grads, dtypes)],
        grid_spec=pltpu.PrefetchScalarGridSpec(num_scalar_prefetch=1, grid=(4,), in_specs=in_specs,
                                               out_specs=out_specs),
        compiler_params=pltpu.CompilerParams(dimension_semantics=("parallel",), vmem_limit_bytes=VMEM_LIMIT_BYTES),
    )(place, *operands)


def _adamw_math(w, g, m, v):
    m = ADAM_B1 * m + (1.0 - ADAM_B1) * g
    v = ADAM_B2 * v + (1.0 - ADAM_B2) * (g * g)
    m_hat = m / (1.0 - ADAM_B1 ** ADAM_STEP)
    v_hat = v / (1.0 - ADAM_B2 ** ADAM_STEP)
    delta = -ADAM_LR * (m_hat / (jnp.sqrt(v_hat) + ADAM_EPS) + ADAM_WD * w)
    return delta, m, v


def _adamw_shard(place, grad, landed, got, w, m, v):
    rows, cols = w.shape
    tr = _row_tile(rows, cols)

    def body(place_ref, a_ref, b_ref, got_ref, w_ref, m_ref, v_ref, g_out, d_out, m_out, v_out):
        g = a_ref[...] + b_ref[...]
        for k in range(3):
            g = g + got_ref[k].astype(F32)
        g_out[...] = g
        d_out[...], m_out[...], v_out[...] = _adamw_math(w_ref[...], g, m_ref[...], v_ref[...])

    spec = pl.BlockSpec((tr, cols), lambda i, pr: (i, 0))
    out = jax.ShapeDtypeStruct((rows, cols), F32)
    return pl.pallas_call(
        body, name="adamw_shard", out_shape=[out] * 4,
        grid_spec=pltpu.PrefetchScalarGridSpec(
            num_scalar_prefetch=1, grid=(rows // tr,),
            in_specs=[pl.BlockSpec((None, None, tr, cols), lambda i, pr: (pr[0], pr[1], i, 0)),
                      pl.BlockSpec((None, None, tr, cols), lambda i, pr: (pr[0], 0, i, 0)),
                      pl.BlockSpec((3, tr, cols), lambda i, pr: (0, i, 0)), spec, spec, spec],
            out_specs=[spec] * 4),
        compiler_params=pltpu.CompilerParams(dimension_semantics=("parallel",)),
    )(place, grad, landed, got, w, m, v)


def _finish_small(place, grad, landed, got):
    rows, cols = got.shape[1:]

    def body(place_ref, grad_ref, landed_ref, got_ref, out_ref):
        out_ref[...] = (((grad_ref[...] + landed_ref[...]) + got_ref[0]) + got_ref[1]) + got_ref[2]

    return pl.pallas_call(
        body, name="finish_small", out_shape=jax.ShapeDtypeStruct((rows, cols), F32),
        grid_spec=pltpu.PrefetchScalarGridSpec(
            num_scalar_prefetch=1, grid=(1,),
            in_specs=[pl.BlockSpec((None, None, rows, cols), lambda i, pr: (pr[0], pr[1], 0, 0)),
                      pl.BlockSpec((None, None, rows, cols), lambda i, pr: (pr[0], 0, 0, 0)),
                      pl.BlockSpec((3, rows, cols), lambda i, pr: (0, 0, 0))],
            out_specs=pl.BlockSpec((rows, cols), lambda i, pr: (0, 0))),
    )(place, grad, landed, got)


def _adamw_small(params):
    n = len(params)

    def body(*refs):
        ins, outs = refs[:4 * n], refs[4 * n:]
        for i in range(n):
            w_ref, g_ref, m_ref, v_ref = ins[4 * i:4 * i + 4]
            d, m, v = _adamw_math(w_ref[...], g_ref[...], m_ref[...], v_ref[...])
            outs[3 * i][...] = d
            outs[3 * i + 1][...] = m
            outs[3 * i + 2][...] = v

    flat = [a for p in params for a in p]
    out_shape = [jax.ShapeDtypeStruct(p[0].shape, F32) for p in params for _ in range(3)]
    outs = pl.pallas_call(
        body, name="adamw_small", out_shape=out_shape, in_specs=[VMEM] * (4 * n), out_specs=[VMEM] * (3 * n),
    )(*flat)
    return [tuple(outs[3 * i:3 * i + 3]) for i in range(n)]


def _pack_small(parts):
    rows = []
    for a in parts:
        r, c = a.shape
        rows.append(jnp.pad(a, ((0, (-r) % 8), (0, CHUNK - c))))
    flat = jnp.concatenate(rows, axis=0)
    return jnp.pad(flat, ((0, N_DEV * SMALL_ROWS - flat.shape[0]), (0, 0))).reshape(N_DEV, SMALL_ROWS, CHUNK)


def _unpack_small(flat, shapes):
    out, at = [], 0
    for r, c in shapes:
        out.append(flat[at:at + r, :c])
        at += r + (-r) % 8
    return out


def kernel(x, w_in_a, sgu_ln_g, sgu_ln_b, w_spatial, b_spatial, w_out_a, w_kv, w_in_b, attn_sinks, rel_bias, w_out_b, post_ln_g, post_ln_b, loss_target, m_w_in_a, m_sgu_ln_g, m_sgu_ln_b, m_w_spatial, m_b_spatial, m_w_out_a, m_w_kv, m_w_in_b, m_attn_sinks, m_rel_bias, m_w_out_b, m_post_ln_g, m_post_ln_b, v_w_in_a, v_sgu_ln_g, v_sgu_ln_b, v_w_spatial, v_b_spatial, v_w_out_a, v_w_kv, v_w_in_b, v_attn_sinks, v_rel_bias, v_w_out_b, v_post_ln_g, v_post_ln_b):
    seq = x.shape[1]
    me = 4 * lax.axis_index("x") + 2 * lax.axis_index("y") + lax.axis_index("c")
    bucket = jnp.asarray(_band_tables())

    ln_shard = jnp.concatenate([sgu_ln_g, sgu_ln_b], axis=0)
    x2, tgt = x[0], loss_target[0]
    order = jnp.stack([2 * px + py for px, py, _ in _use_order()[::2]]).astype(jnp.int32)
    p, wa, woa, ln_all, xb = _gather_in_proj(order, x2, w_in_a[0], (w_out_a[0],), ln_shard)
    woa = woa.reshape(A_WIDTH, D_MODEL)
    ln_gb = jnp.transpose(ln_all, (1, 0, 2)).reshape(2, A_WIDTH)
    ws = w_spatial[0]
    bsp_t = jnp.pad(b_spatial[0].T, ((0, 0), (0, CHUNK - A_GROUPS)))
    rel_t = rel_bias.T
    bias = _bias_table(rel_t, bucket)

    r0, wkv, wb, wob = _fwd_a(x2, p, ln_gb, ws, bsp_t, woa, (w_kv, w_in_b[0], w_out_b[0]))
    wkv = wkv.reshape(D_MODEL, 2 * KV_WIDTH)
    wob = wob.reshape(B_WIDTH, D_MODEL)
    q, z, kv, dr1, loss, dg1, db1 = _fwd_b(r0, tgt, wb, wkv, wob, post_ln_g, post_ln_b, attn_sinks, bias)
    dr0, dwb, dwkv, dwob, dsink, dlog, dpost0 = _bwd_b(r0, dr1, q, z, kv, wb, wkv, wob, post_ln_g, post_ln_b,
                                                       attn_sinks, bias)
    drel = _rel_bias_grad(dlog, bucket)

    place = jnp.stack([2 * lax.axis_index("x") + lax.axis_index("y"), lax.axis_index("c")]).astype(jnp.int32)

    def by_device(g):
        return g.reshape((4, 2, g.shape[0] // N_DEV) + g.shape[1:]) if g.ndim == 2 else g.reshape((4, 2) + g.shape[1:])

    dwb, dwkv, dwob = by_device(dwb), by_device(dwkv), by_device(dwob)
    dp, dwoa, dws, dbsp_t, dln, l_wb, l_wkv, l_wob = _bwd_a(dr0, p, ln_gb, ws, bsp_t, woa,
                                                            _Exchange(pair=(dwb, dwkv, dwob)))
    dwoa = by_device(dwoa)
    small_shapes = [(A_GROUPS * CHUNK, CHUNK), (A_GROUPS, CHUNK), (1, N_Q_HEADS), (N_Q_HEADS, REL_BUCKETS),
                    (16, CHUNK), (16, CHUNK), (16, CHUNK), (16, CHUNK), (1, 1)]
    small = by_device(_pack_small([
        dws.reshape(A_GROUPS * CHUNK, CHUNK), dbsp_t[:, :A_GROUPS].T, dsink[:, :N_Q_HEADS], drel,
        jnp.concatenate([dpost0[0:1], dg1], axis=0).reshape(16, CHUNK),
        jnp.concatenate([dpost0[1:2], db1], axis=0).reshape(16, CHUNK),
        dln[0].reshape(16, CHUNK), dln[1].reshape(16, CHUNK), loss]))
    s_wb, s_wkv, s_wob = _pair_sum(place, (dwb, dwkv, dwob), (l_wb, l_wkv, l_wob), (BF16,) * 3)
    dwa, l_woa, l_small, g_wb, g_wkv, g_wob = _in_proj_weight_grad(
        xb, dp, wa.shape[2], _Exchange(pair=(dwoa, small), chip=(s_wb, s_wkv, s_wob)))
    dwa = by_device(dwa)
    s_woa, s_small = _pair_sum(place, (dwoa, small), (l_woa, l_small), (BF16, F32))
    _, blocks = _input_grad_blocks(seq)
    cut = blocks // 4
    dx, l_wa, g_woa, g_small = _input_grad(dp, dr0, wa, 0, cut, None, _Exchange(pair=(dwa,), chip=(s_woa, s_small)))
    s_wa, = _pair_sum(place, (dwa,), (l_wa,), (BF16,))
    my_small = _finish_small(place, small, l_small, g_small)
    dx, g_wa, small_all = _input_grad(dp, dr0, wa, cut, blocks - cut, dx,
                                      _Exchange(chip=(s_wa,), gather=(my_small,)))
    full = [dwa, dwoa, dwkv, dwb, dwob]
    landed = [l_wa, l_woa, l_wkv, l_wb, l_wob]
    got = [g_wa, g_woa, g_wkv, g_wb, g_wob]

    big = [(w_in_a, m_w_in_a, v_w_in_a), (w_out_a, m_w_out_a, v_w_out_a), (w_kv, m_w_kv, v_w_kv),
           (w_in_b, m_w_in_b, v_w_in_b), (w_out_b, m_w_out_b, v_w_out_b)]
    big_out = []
    for i, (w, m, v) in enumerate(big):
        shape2 = got[i].shape[1:]
        res = _adamw_shard(place, full[i], landed[i], got[i], w.reshape(shape2), m.reshape(shape2),
                           v.reshape(shape2))
        big_out.append([r.reshape(w.shape) for r in res])

    small_all = small_all.reshape(N_DEV * SMALL_ROWS, CHUNK)
    g_ws, g_bsp, g_sink, g_rel, g_pg, g_pb, g_lng, g_lnb, loss_all = _unpack_small(small_all, small_shapes)
    g_ws = g_ws.reshape(w_spatial.shape)
    g_bsp = g_bsp.reshape(b_spatial.shape)
    g_pg = g_pg.reshape(post_ln_g.shape)
    g_pb = g_pb.reshape(post_ln_b.shape)
    g_lng = lax.dynamic_slice(g_lng.reshape(1, A_WIDTH), (0, me * sgu_ln_g.shape[1]), sgu_ln_g.shape)
    g_lnb = lax.dynamic_slice(g_lnb.reshape(1, A_WIDTH), (0, me * sgu_ln_b.shape[1]), sgu_ln_b.shape)
    small_params = [(sgu_ln_g, g_lng, m_sgu_ln_g, v_sgu_ln_g), (sgu_ln_b, g_lnb, m_sgu_ln_b, v_sgu_ln_b),
                    (w_spatial, g_ws, m_w_spatial, v_w_spatial), (b_spatial, g_bsp, m_b_spatial, v_b_spatial),
                    (attn_sinks, g_sink, m_attn_sinks, v_attn_sinks), (rel_t, g_rel, m_rel_bias.T, v_rel_bias.T),
                    (post_ln_g, g_pg, m_post_ln_g, v_post_ln_g), (post_ln_b, g_pb, m_post_ln_b, v_post_ln_b)]
    flat2 = [tuple(a.reshape(-1, a.shape[-1]) for a in p) for p in small_params]
    small_out = [tuple(r.reshape(p[0].shape) for r in res) for res, p in zip(_adamw_small(flat2), small_params)]

    order = [("big", 0), ("small", 0), ("small", 1), ("small", 2), ("small", 3), ("big", 1), ("big", 2), ("big", 3),
             ("small", 4), ("small", 5), ("big", 4), ("small", 6), ("small", 7)]
    grads, deltas, new_m, new_v = [], [], [], []
    for kind, i in order:
        if kind == "big":
            g, d, m, v = big_out[i]
        else:
            g = small_params[i][1]
            d, m, v = small_out[i]
            if small_params[i][0] is rel_t:
                g, d, m, v = g.T, d.T, m.T, v.T
        grads.append(g)
        deltas.append(d)
        new_m.append(m)
        new_v.append(v)
    return (loss_all[0, 0], dx.reshape(x.shape), *grads, *deltas, *new_m, *new_v)
```

```python
import functools

import jax
import jax.numpy as jnp
import numpy as np
from jax import lax
from jax.experimental import pallas as pl
from jax.experimental.pallas import tpu as pltpu

F32 = jnp.float32
BF16 = jnp.bfloat16

D_MODEL = 1024
A_WIDTH = 2048
A_GROUPS = 8
A_GROUP_DIM = A_WIDTH // A_GROUPS
CHUNK = 128
HEAD_DIM = 64
N_Q_HEADS = 16
N_KV_HEADS = 2
PAIRS_PER_KV = N_Q_HEADS // N_KV_HEADS // 2
B_WIDTH = N_Q_HEADS * HEAD_DIM
KV_WIDTH = N_KV_HEADS * HEAD_DIM
REL_BUCKETS = 32
REL_MAX_DIST = 128
DEPTH = 2
ALPHA = (2.0 * DEPTH) ** 0.25
LN_EPS = 1e-5
NEG_INF = -1e30
SCALE = HEAD_DIM ** -0.5

ADAM_LR = 0.001
ADAM_B1 = 0.9
ADAM_B2 = 0.999
ADAM_EPS = 1e-08
ADAM_WD = 0.01
ADAM_STEP = 10

N_DEV = 8
MESH = pl.DeviceIdType.MESH
TOKENS_IN_PROJ = 1024
TOKENS_WGRAD = 4096
TOKENS_A = 256
TOKENS_A_FWD = 256
P_RING = 3
TOKENS_B = 512
PART_ROWS_B = 256
TOKENS_B_BWD = 256
TOKENS_DX = 512
VMEM_LIMIT_BYTES = 60 * 1024 * 1024
SMALL_ROWS = 144

ANY = pl.BlockSpec(memory_space=pl.ANY)
VMEM = pl.BlockSpec(memory_space=pltpu.VMEM)
SMEM = pl.BlockSpec(memory_space=pltpu.SMEM)


def _nn(a, b):
    return jnp.dot(a, b, preferred_element_type=F32)


def _nt(a, b):
    return lax.dot_general(a, b, (((1,), (1,)), ((), ())), preferred_element_type=F32)


def _tn(a, b):
    return lax.dot_general(a, b, (((0,), (0,)), ((), ())), preferred_element_type=F32)


def _resident(shape):
    nd = len(shape)
    return pl.BlockSpec(shape, lambda i: (0,) * nd, pipeline_mode=pl.Buffered(1))


def _rows(tm, width, reverse_of=None):
    if reverse_of is None:
        return pl.BlockSpec((tm, width), lambda i: (i, 0))
    return pl.BlockSpec((tm, width), lambda i: (reverse_of - 1 - i, 0))


def _seq_params():
    return pltpu.CompilerParams(dimension_semantics=("arbitrary",), vmem_limit_bytes=VMEM_LIMIT_BYTES)


def _in_turns(first, second):
    merged = [(i / max(len(first), 1), 0, f) for i, f in enumerate(first)]
    merged += [((i + 0.5) / max(len(second), 1), 1, f) for i, f in enumerate(second)]
    return [f for _, _, f in sorted(merged, key=lambda t: t[:2])]


def _ln_normalize(r):
    mu = jnp.mean(r, axis=-1, keepdims=True)
    xc = r - mu
    var = jnp.mean(xc * xc, axis=-1, keepdims=True)
    rstd = lax.rsqrt(var + LN_EPS)
    return xc * rstd, rstd


def _ln_backward(dxhat, xhat, rstd):
    m1 = jnp.mean(dxhat, axis=-1, keepdims=True)
    m2 = jnp.mean(dxhat * xhat, axis=-1, keepdims=True)
    return rstd * (dxhat - m1 - xhat * m2)


def _silu_and_grad(z):
    sig = jax.nn.sigmoid(z)
    silu = z * sig
    return silu, sig + silu * (1.0 - sig)


def _band_tables():
    t = np.arange(CHUNK, dtype=np.int32)[:, None]
    j = np.arange(2 * CHUNK, dtype=np.int32)[None, :]
    d = t + CHUNK - j
    in_window = (d >= 0) & (d < CHUNK)
    dc = np.clip(d, 0, REL_MAX_DIST - 1)
    max_exact = REL_BUCKETS // 2
    df = np.maximum(dc, 1).astype(np.float32)
    large = max_exact + (np.log(df / np.float32(max_exact)) / np.float32(np.log(REL_MAX_DIST / max_exact))
                         * np.float32(REL_BUCKETS - max_exact)).astype(np.int32)
    large = np.minimum(large, REL_BUCKETS - 1)
    bucket = np.where(dc < max_exact, dc, large)
    return np.where(in_window, bucket, -1).astype(np.int32)


def _use_order():
    x, y, c = lax.axis_index("x"), lax.axis_index("y"), lax.axis_index("c")
    blocks = [(x, y, c), (x, y, 1 - c)]
    for px, py in [(1 - x, y), (x, 1 - y), (1 - x, 1 - y)]:
        blocks += [(px, py, c), (px, py, 1 - c)]
    return blocks


def _gather_in_proj(order, x, w_in_a, small, ln_gb):
    seq = x.shape[0]
    tm = min(TOKENS_IN_PROJ, seq)
    n_blocks = seq // tm
    shard_cols = w_in_a.shape[1]
    n = len(small) + 2

    def body(order_ref, x_ref, *refs):
        ins, refs = refs[:n], refs[n:]
        p_ref, outs, xb_out, refs = refs[0], refs[1:n + 1], refs[n + 1], refs[n + 2:]
        wa_vmem, xb_vmem, stage = refs[0], refs[1], refs[2:n]
        send_sems, recv_sems, local_sems, out_sems, xb_sem = refs[n:]
        xb_copy = pltpu.make_async_copy(xb_vmem, xb_out, xb_sem)
        phase, blk = pl.program_id(0), pl.program_id(1)
        x_, y_, c_ = lax.axis_index("x"), lax.axis_index("y"), lax.axis_index("c")
        me, sibling = (x_, y_, c_), (x_, y_, 1 - c_)
        chips = [(1 - x_, y_), (x_, 1 - y_), (1 - x_, 1 - y_)]
        use = _use_order()

        def index(px, py, pc):
            return 4 * px + 2 * py + pc

        def slot(a, block):
            return (wa_vmem if a == 0 else outs[a]).at[index(*block)]

        def own(a):
            return slot(0, me) if a == 0 else (ins[a] if a == n - 1 else stage[a - 1])

        def copy(a, k, block, to, src=None):
            return pltpu.make_async_remote_copy(
                src_ref=slot(a, block) if src is None else src, dst_ref=slot(a, block),
                send_sem=send_sems.at[a, k], recv_sem=recv_sems.at[a, k], device_id=to, device_id_type=MESH)

        def keep(a):
            return pltpu.make_async_copy(own(a), slot(a, me), local_sems.at[a])

        def write_out(k):
            return pltpu.make_async_copy(slot(0, use[k]), outs[0].at[index(*use[k])], out_sems.at[k])

        def pass_on(a, j):
            copy(a, 1 + j, (*chips[j], c_), me).wait_recv()
            copy(a, 4 + j, (*chips[j], c_), sibling).start()

        def send(a, j):
            copy(a, 1 + j, me, (*chips[j], c_), own(a)).start()

        @pl.when((phase == 0) & (blk == 0))
        def _():
            wa_vmem[index(*me)] = ins[0][...].astype(BF16)
            for a in range(1, n - 1):
                stage[a - 1][...] = ins[a][...].astype(BF16)
            send(0, 0)
            send(0, 1)
            for a in range(n):
                copy(a, 0, me, sibling, own(a)).start()
                if a:
                    keep(a).start()
            write_out(0).start()
            copy(0, 0, sibling, me).wait_recv()
            write_out(1).start()

        @pl.when((phase == 1) & (blk == 0))
        def _():
            pass_on(0, 0)
            write_out(2).start()
            xb_copy.start()
            send(0, 2)
            for a in range(1, n):
                for j in range(3):
                    send(a, j)
            pass_on(0, 1)
            write_out(4).start()
            copy(0, 4, (*chips[0], 1 - c_), me).wait_recv()
            write_out(3).start()

        @pl.when((phase == 2) & (blk == 0))
        def _():
            copy(0, 5, (*chips[1], 1 - c_), me).wait_recv()
            write_out(5).start()

        @pl.when((phase == 2) & (blk == n_blocks // 2))
        def _():
            pass_on(0, 2)
            write_out(6).start()

        @pl.when((phase == 3) & (blk == 0))
        def _():
            copy(0, 6, (*chips[2], 1 - c_), me).wait_recv()
            write_out(7).start()
            for a in range(1, n):
                pass_on(a, 0)
                pass_on(a, 1)

        rows = pl.ds(pl.multiple_of(blk * tm, tm), tm)

        @pl.when(phase == 0)
        def _():
            xb_vmem[rows, :] = x_ref[...].astype(BF16)

        chip = order_ref[phase]
        xb = xb_vmem[rows, :]
        p_ref[:, :shard_cols] = _nn(xb, wa_vmem[2 * chip])
        p_ref[:, shard_cols:] = _nn(xb, wa_vmem[2 * chip + 1])

        @pl.when((phase == N_DEV // 2 - 1) & (blk == n_blocks - 1))
        def _():
            for a in range(1, n):
                pass_on(a, 2)
            for a in range(1, n):
                copy(a, 0, sibling, me).wait_recv()
                for j, far in enumerate(chips):
                    copy(a, 4 + j, (*far, 1 - c_), me).wait_recv()
            for a in range(n):
                for k in range(7):
                    copy(a, k, me, me).wait_send()
                if a:
                    keep(a).wait()
            for k in range(N_DEV):
                write_out(k).wait()
            xb_copy.wait()

    gathered = [jax.ShapeDtypeStruct((N_DEV,) + w_in_a.shape, BF16)]
    gathered += [jax.ShapeDtypeStruct((N_DEV,) + s.shape, BF16) for s in small]
    gathered.append(jax.ShapeDtypeStruct((N_DEV,) + ln_gb.shape, F32))
    return pl.pallas_call(
        body, name="gather_in_proj",
        out_shape=[jax.ShapeDtypeStruct((seq, N_DEV * shard_cols), F32)] + gathered
        + [jax.ShapeDtypeStruct((seq, D_MODEL), BF16)],
        grid_spec=pltpu.PrefetchScalarGridSpec(
            num_scalar_prefetch=1, grid=(N_DEV // 2, n_blocks),
            in_specs=[pl.BlockSpec((tm, D_MODEL), lambda k, i, order: (jnp.where(k == 0, i, n_blocks - 1), 0))]
            + [VMEM] * n,
            out_specs=[pl.BlockSpec((tm, 2 * shard_cols), lambda k, i, order: (i, order[k]))] + [ANY] * (n + 1),
            scratch_shapes=[pltpu.VMEM((N_DEV,) + w_in_a.shape, BF16), pltpu.VMEM((seq, D_MODEL), BF16)]
            + [pltpu.VMEM(s.shape, BF16) for s in small] + [
                pltpu.SemaphoreType.DMA((n, 7)), pltpu.SemaphoreType.DMA((n, 7)), pltpu.SemaphoreType.DMA((n,)),
                pltpu.SemaphoreType.DMA((N_DEV,)), pltpu.SemaphoreType.DMA(())]),
        compiler_params=pltpu.CompilerParams(dimension_semantics=("arbitrary", "arbitrary"),
                                             vmem_limit_bytes=VMEM_LIMIT_BYTES),
    )(order, x, w_in_a, *small, ln_gb)


def _bias_table(rel_bias_t, bucket):
    def body(rel_ref, bucket_ref, out_ref):
        first = jnp.broadcast_to(bucket_ref[0:1, :], (8, 2 * CHUNK))
        col = lax.broadcasted_iota(jnp.int32, (CHUNK, 2 * CHUNK), 1)
        for h in range(N_Q_HEADS):
            row = jnp.full(first.shape, NEG_INF, F32)
            for b in range(REL_BUCKETS):
                row = jnp.where(first == b, rel_ref[h, b], row)
            tab = pltpu.roll(jnp.broadcast_to(row[0:1, :], (CHUNK, 2 * CHUNK)), 0, 1, stride=1, stride_axis=0)
            pair = h // 2
            rows = slice((pair % PAIRS_PER_KV) * CHUNK, (pair % PAIRS_PER_KV + 1) * CHUNK)
            half = slice((h % 2) * 2 * CHUNK, (h % 2 + 1) * 2 * CHUNK)
            out_ref[1, pair // PAIRS_PER_KV, rows, half] = tab
            out_ref[0, pair // PAIRS_PER_KV, rows, half] = jnp.where(col < CHUNK, NEG_INF, tab)

    return pl.pallas_call(
        body, name="bias_table",
        out_shape=jax.ShapeDtypeStruct((2, N_KV_HEADS, PAIRS_PER_KV * CHUNK, 4 * CHUNK), F32),
        in_specs=[SMEM, VMEM], out_specs=VMEM,
    )(rel_bias_t, bucket)


def _rel_bias_grad(dlogits, bucket):
    def body(dl_ref, bucket_ref, out_ref):
        i = lax.broadcasted_iota(jnp.int32, (CHUNK, CHUNK), 0)
        k = lax.broadcasted_iota(jnp.int32, (CHUNK, CHUNK), 1)
        reverse = jnp.where(i + k == CHUNK - 1, 1.0, 0.0)
        sums = []
        for h in range(N_Q_HEADS):
            pair = h // 2
            dl = dl_ref[pair // PAIRS_PER_KV, (pair % PAIRS_PER_KV) * CHUNK:(pair % PAIRS_PER_KV + 1) * CHUNK,
                        (h % 2) * 2 * CHUNK:(h % 2 + 1) * 2 * CHUNK]
            backwards = jnp.dot(reverse, dl, precision=lax.Precision.HIGHEST, preferred_element_type=F32)
            turned = pltpu.roll(backwards, CHUNK + 1, 1, stride=1, stride_axis=0)
            sums.append(jnp.sum(turned, axis=0, keepdims=True))
        by_offset = jnp.concatenate(sums, axis=0)
        buckets = lax.broadcasted_iota(jnp.int32, (REL_BUCKETS, 2 * CHUNK), 0)
        one_hot = jnp.where(jnp.broadcast_to(bucket_ref[0:1, :], buckets.shape) == buckets, 1.0, 0.0)
        out_ref[...] = lax.dot_general(by_offset, one_hot, (((1,), (1,)), ((), ())),
                                       precision=lax.Precision.HIGHEST, preferred_element_type=F32)

    return pl.pallas_call(
        body, name="rel_bias_grad", out_shape=jax.ShapeDtypeStruct((N_Q_HEADS, REL_BUCKETS), F32),
        in_specs=[VMEM, VMEM], out_specs=VMEM,
    )(dlogits, bucket)


def _masked_spatial(ws_ref, g):
    t = lax.broadcasted_iota(jnp.int32, (CHUNK, CHUNK), 0)
    s = lax.broadcasted_iota(jnp.int32, (CHUNK, CHUNK), 1)
    return jnp.where(s <= t, ws_ref[g], 0.0).astype(BF16)


def _spatial_gate(ws_ref, bsp_t_ref, vn, s_ref, n_chunks):
    for g in range(A_GROUPS):
        w = _masked_spatial(ws_ref, g)
        cols = slice(g * A_GROUP_DIM, (g + 1) * A_GROUP_DIM)
        for ci in range(n_chunks):
            rows = slice(ci * CHUNK, (ci + 1) * CHUNK)
            s_ref[rows, cols] = _nn(w, vn[rows, cols]) + bsp_t_ref[:, g:g + 1]


def _fwd_a(x, p, ln_gb, w_spatial, bsp_t, woa, later):
    seq = x.shape[0]
    tm = min(TOKENS_A_FWD, seq)
    n_chunks = tm // CHUNK
    n_blocks = seq // tm
    n = len(later)

    def body(x_ref, p_hbm, ln_ref, ws_ref, bsp_ref, woa_ref, *refs):
        shards, refs = refs[:n], refs[n:]
        r0_ref, gathered, refs = refs[0], refs[1:1 + n], refs[1 + n:]
        s_ref, p_ring, stage, (send_sems, recv_sems, local_sems, ring_sems) = refs[0], refs[1], refs[2:2 + n], refs[2 + n:]
        step = pl.program_id(0)

        def fetch(block):
            slot = block % P_RING
            start = block * tm if isinstance(block, int) else pl.multiple_of(block * tm, tm)
            return pltpu.make_async_copy(p_hbm.at[pl.ds(start, tm), :], p_ring.at[slot], ring_sems.at[slot])

        @pl.when(step == 0)
        def _():
            for ahead in range(min(P_RING - 1, n_blocks)):
                fetch(ahead).start()

        @pl.when(step + P_RING - 1 < n_blocks)
        def _():
            fetch(step + P_RING - 1).start()

        fetch(step).wait()
        p_ref = p_ring.at[step % P_RING]

        def copies(a):
            x_, y_, c_ = lax.axis_index("x"), lax.axis_index("y"), lax.axis_index("c")
            mine = 4 * x_ + 2 * y_ + c_
            out = [pltpu.make_async_copy(stage[a], gathered[a].at[mine], local_sems.at[a])]
            for r in range(1, N_DEV):
                to = (1 - x_ if r & 4 else x_, 1 - y_ if r & 2 else y_, 1 - c_ if r & 1 else c_)
                out.append(pltpu.make_async_remote_copy(
                    src_ref=stage[a], dst_ref=gathered[a].at[mine], send_sem=send_sems.at[a, r - 1],
                    recv_sem=recv_sems.at[a, r - 1], device_id=to, device_id_type=MESH))
            return out

        @pl.when(step == 0)
        def _():
            for a in range(n):
                stage[a][...] = shards[a][...].astype(BF16)
                for cp in copies(a):
                    cp.start()

        xv = x_ref[...]
        vhat, _ = _ln_normalize(p_ref[:, A_WIDTH:2 * A_WIDTH])
        vn = (vhat * ln_ref[0:1, :] + ln_ref[1:2, :]).astype(BF16)
        _spatial_gate(ws_ref, bsp_ref, vn, s_ref, n_chunks)
        z = p_ref[:, 2 * A_WIDTH:]
        y = p_ref[:, :A_WIDTH] * s_ref[...] * (z * jax.nn.sigmoid(z))
        r0_ref[...] = ALPHA * xv + _nn(y.astype(BF16), woa_ref[...])

        @pl.when(step == n_blocks - 1)
        def _():
            for a in range(n):
                for cp in copies(a):
                    cp.wait()

    return pl.pallas_call(
        body, name="fwd_a", grid=(n_blocks,),
        out_shape=[jax.ShapeDtypeStruct((seq, D_MODEL), F32)]
        + [jax.ShapeDtypeStruct((N_DEV,) + s.shape, BF16) for s in later],
        in_specs=[_rows(tm, D_MODEL), ANY, _resident(ln_gb.shape), _resident(w_spatial.shape),
                  _resident(bsp_t.shape), _resident(woa.shape)] + [VMEM] * n,
        out_specs=[_rows(tm, D_MODEL)] + [ANY] * n,
        scratch_shapes=[pltpu.VMEM((tm, A_WIDTH), F32), pltpu.VMEM((P_RING, tm, 3 * A_WIDTH), F32)]
        + [pltpu.VMEM(s.shape, BF16) for s in later] + [
            pltpu.SemaphoreType.DMA((n, N_DEV - 1)), pltpu.SemaphoreType.DMA((n, N_DEV - 1)),
            pltpu.SemaphoreType.DMA((n,)), pltpu.SemaphoreType.DMA((P_RING,))],
        compiler_params=_seq_params(),
    )(x, p, ln_gb, w_spatial, bsp_t, woa, *later)


def _expand_heads(kv):
    lo = lax.broadcasted_iota(jnp.int32, kv.shape, 1) < HEAD_DIM
    swapped = pltpu.roll(kv, HEAD_DIM, 1)
    zero = jnp.zeros_like(kv)
    return (jnp.where(lo, kv, zero).astype(BF16), jnp.where(lo, zero, swapped).astype(BF16),
            jnp.where(lo, swapped, zero).astype(BF16), jnp.where(lo, zero, kv).astype(BF16))


def _fold_heads(acc, kvh):
    top, bot = acc[:2 * CHUNK], acc[2 * CHUNK:]
    lo = lax.broadcasted_iota(jnp.int32, top.shape, 1) < HEAD_DIM
    t = jnp.where(lo, top, 0.0)
    b = jnp.where(lo, 0.0, bot)
    if kvh == 0:
        return t + pltpu.roll(b, HEAD_DIM, 1)
    return pltpu.roll(t, HEAD_DIM, 1) + b


STACK = PAIRS_PER_KV * CHUNK


def _pair_cols(kvh, pp):
    pair = kvh * PAIRS_PER_KV + pp
    return slice(pair * CHUNK, (pair + 1) * CHUNK)


def _stack_pairs(ref_or_val, rows, kvh):
    return jnp.concatenate([ref_or_val[rows, _pair_cols(kvh, pp)] for pp in range(PAIRS_PER_KV)], axis=0)


def _sink_columns(sink_ref):
    blk = lax.broadcasted_iota(jnp.int32, (STACK, 1), 0) // CHUNK
    out = []
    for kvh in range(N_KV_HEADS):
        halves = []
        for half in range(2):
            col = jnp.zeros((STACK, 1), F32)
            for pp in range(PAIRS_PER_KV):
                col = jnp.where(blk == pp, sink_ref[0, 2 * (kvh * PAIRS_PER_KV + pp) + half], col)
            halves.append(col)
        out.append(halves)
    return out


def _softmax_stack(logits, sink_cols):
    ps, sinks = [], []
    for half in range(2):
        lg = logits[:, half * 2 * CHUNK:(half + 1) * 2 * CHUNK]
        sk = sink_cols[half]
        m = jnp.maximum(jnp.max(lg, axis=-1, keepdims=True), sk)
        e = jnp.exp(lg - m)
        es = jnp.exp(sk - m)
        inv = 1.0 / (jnp.sum(e, axis=-1, keepdims=True) + es)
        ps.append(e * inv)
        sinks.append(es * inv)
    return jnp.concatenate(ps, axis=1), sinks


def _band_operands(before_k, this_k, before_v, this_v, kvh):
    kblk = jnp.concatenate([before_k[2 * kvh], this_k[2 * kvh], before_k[2 * kvh + 1], this_k[2 * kvh + 1]], axis=0)
    vblk = jnp.concatenate([before_v[2 * kvh], this_v[2 * kvh], before_v[2 * kvh + 1], this_v[2 * kvh + 1]], axis=0)
    return kblk, vblk


def _fwd_b(r0, target, wb, wkv, wob, post_g, post_b, sinks, bias):
    seq = r0.shape[0]
    tm = min(TOKENS_B, seq)
    n_chunks = tm // CHUNK
    part_rows = min(PART_ROWS_B, tm)
    n_parts = tm // part_rows
    part_chunks = part_rows // CHUNK
    shard_cols = wb.shape[2]

    def body(r0_ref, tgt_ref, wb_ref, wkv_ref, wob_ref, pg_ref, pb_ref, sink_ref, bias_ref,
             q_ref, z_ref, kv_ref, dr1_ref, loss_ref, dg_ref, db_ref, ek_last, ev_last, *scratch):
        q_s, z_s, h1_s, o_s, ek_s, ev_s = (scratch[i * n_parts:(i + 1) * n_parts] for i in range(6))
        step = pl.program_id(0)

        @pl.when(step == 0)
        def _():
            ek_last[...] = jnp.zeros_like(ek_last)
            ev_last[...] = jnp.zeros_like(ev_last)
            loss_ref[...] = jnp.zeros_like(loss_ref)
            dg_ref[...] = jnp.zeros_like(dg_ref)
            db_ref[...] = jnp.zeros_like(db_ref)

        sink_cols = _sink_columns(sink_ref)
        g1 = pg_ref[1:2, :]

        def project_steps(part):
            rows = slice(part * part_rows, (part + 1) * part_rows)
            live = {}

            def normalize():
                xhat0, _ = _ln_normalize(r0_ref[rows, :])
                h1 = xhat0 * pg_ref[0:1, :] + pb_ref[0:1, :]
                h1_s[part][...] = h1
                live["h1b"] = h1.astype(BF16)

            def shard(k):
                def run():
                    qz = _nn(live["h1b"], wb_ref[k])
                    lo, hi = k * shard_cols, (k + 1) * shard_cols
                    if hi <= B_WIDTH:
                        qs = (qz * SCALE).astype(BF16)
                        q_ref[rows, lo:hi] = qs
                        q_s[part][:, lo:hi] = qs
                    else:
                        z_ref[rows, lo - B_WIDTH:hi - B_WIDTH] = qz
                        z_s[part][:, lo - B_WIDTH:hi - B_WIDTH] = qz
                return run

            def keys():
                kv = _nn(live["h1b"], wkv_ref[...])
                kv_ref[rows, :] = kv.astype(BF16)
                for cj in range(part_chunks):
                    sub = slice(cj * CHUNK, (cj + 1) * CHUNK)
                    for i, e in enumerate(_expand_heads(kv[sub, :KV_WIDTH])):
                        ek_s[part][cj, i] = e
                    for i, e in enumerate(_expand_heads(kv[sub, KV_WIDTH:])):
                        ev_s[part][cj, i] = e

            return [normalize, keys] + [shard(k) for k in range(N_DEV)]

        def attend_steps(part):
            def stack(cj, kvh):
                def run():
                    sub = slice(cj * CHUNK, (cj + 1) * CHUNK)
                    sel = jnp.where(step * n_chunks + part * part_chunks + cj > 0, 1, 0)
                    if cj:
                        before_k, before_v = ek_s[part].at[cj - 1], ev_s[part].at[cj - 1]
                    elif part:
                        before_k, before_v = ek_s[part - 1].at[part_chunks - 1], ev_s[part - 1].at[part_chunks - 1]
                    else:
                        before_k, before_v = ek_last, ev_last
                    kblk, vblk = _band_operands(before_k, ek_s[part].at[cj], before_v, ev_s[part].at[cj], kvh)
                    logits = _nt(_stack_pairs(q_s[part], sub, kvh), kblk) + bias_ref[sel, kvh]
                    p, _ = _softmax_stack(logits, sink_cols[kvh])
                    o = _nn(p.astype(BF16), vblk)
                    for pp in range(PAIRS_PER_KV):
                        o_s[part][sub, _pair_cols(kvh, pp)] = o[pp * CHUNK:(pp + 1) * CHUNK]
                return run

            def finish():
                rows = slice(part * part_rows, (part + 1) * part_rows)
                z = z_s[part][...]
                y = o_s[part][...] * (z * jax.nn.sigmoid(z))
                r1 = ALPHA * h1_s[part][...] + _nn(y.astype(BF16), wob_ref[...])
                xhat1, rstd1 = _ln_normalize(r1)
                diff = xhat1 * g1 + pb_ref[1:2, :] - tgt_ref[rows, :]
                loss_ref[...] += 0.5 * jnp.sum(jnp.mean(diff * diff, axis=-1, keepdims=True), axis=0, keepdims=True)
                dh2 = diff * (1.0 / D_MODEL)
                dg_ref[...] += jnp.sum(dh2 * xhat1, axis=0, keepdims=True)
                db_ref[...] += jnp.sum(dh2, axis=0, keepdims=True)
                dr1_ref[rows, :] = _ln_backward(dh2 * g1, xhat1, rstd1)

            return [stack(cj, kvh) for cj in range(part_chunks) for kvh in range(N_KV_HEADS)] + [finish]

        for run in project_steps(0):
            run()
        for part in range(n_parts):
            for run in _in_turns(attend_steps(part), project_steps(part + 1) if part + 1 < n_parts else []):
                run()
        ek_last[...] = ek_s[n_parts - 1][part_chunks - 1]
        ev_last[...] = ev_s[n_parts - 1][part_chunks - 1]

    vec = pl.BlockSpec((1, D_MODEL), lambda i: (0, 0))
    return pl.pallas_call(
        body, name="fwd_b", grid=(seq // tm,),
        out_shape=[jax.ShapeDtypeStruct((seq, B_WIDTH), BF16), jax.ShapeDtypeStruct((seq, B_WIDTH), F32),
                   jax.ShapeDtypeStruct((seq, 2 * KV_WIDTH), BF16), jax.ShapeDtypeStruct((seq, D_MODEL), F32),
                   jax.ShapeDtypeStruct((1, 1), F32), jax.ShapeDtypeStruct((1, D_MODEL), F32),
                   jax.ShapeDtypeStruct((1, D_MODEL), F32)],
        in_specs=[_rows(tm, D_MODEL), _rows(tm, D_MODEL), _resident(wb.shape), _resident(wkv.shape),
                  _resident(wob.shape), _resident(post_g.shape), _resident(post_b.shape), SMEM,
                  _resident(bias.shape)],
        out_specs=[_rows(tm, B_WIDTH), _rows(tm, B_WIDTH), _rows(tm, 2 * KV_WIDTH), _rows(tm, D_MODEL),
                   pl.BlockSpec((1, 1), lambda i: (0, 0)), vec, vec],
        scratch_shapes=[pltpu.VMEM((4, CHUNK, CHUNK), BF16), pltpu.VMEM((4, CHUNK, CHUNK), BF16)]
        + [pltpu.VMEM((part_rows, B_WIDTH), BF16)] * n_parts + [pltpu.VMEM((part_rows, B_WIDTH), F32)] * n_parts
        + [pltpu.VMEM((part_rows, D_MODEL), F32)] * n_parts + [pltpu.VMEM((part_rows, B_WIDTH), F32)] * n_parts
        + [pltpu.VMEM((part_chunks, 4, CHUNK, CHUNK), BF16)] * (2 * n_parts),
        compiler_params=_seq_params(),
    )(r0, target, wb, wkv, wob, post_g, post_b, sinks, bias)


def _bwd_b(r0, dr1, q, z, kv, wb, wkv, wob, post_g, post_b, sinks, bias):
    seq = r0.shape[0]
    tm = min(TOKENS_B_BWD, seq)
    n_chunks = tm // CHUNK
    n_blocks = seq // tm
    shard_cols = wb.shape[2]

    def body(r0_ref, dr1_ref, q_ref, z_ref, kv_ref, kvp_ref, wb_ref, wkv_ref, wob_ref, pg_ref, pb_ref, sink_ref,
             bias_ref, dr0_ref, dwb_out, dwkv_out, dwob_out, dsink_ref, dl_out, dpost_ref,
             dwb_acc, dwkv_acc, dwob_acc, dl_acc, dsink_acc, carry_ref, ek_ref, ev_ref, y_ref, dqz_ref, dkv_ref):
        step = pl.program_id(0)
        block = n_blocks - 1 - step

        @pl.when(step == 0)
        def _():
            dpost_ref[...] = jnp.zeros_like(dpost_ref)
            dwb_acc[...] = jnp.zeros_like(dwb_acc)
            dwkv_acc[...] = jnp.zeros_like(dwkv_acc)
            dwob_acc[...] = jnp.zeros_like(dwob_acc)
            dl_acc[...] = jnp.zeros_like(dl_acc)
            carry_ref[...] = jnp.zeros_like(carry_ref)
            dsink_acc[...] = jnp.zeros_like(dsink_acc)

        xhat0, rstd0 = _ln_normalize(r0_ref[...])
        h1b = (xhat0 * pg_ref[0:1, :] + pb_ref[0:1, :]).astype(BF16)
        dr1 = dr1_ref[...]
        dr1b = dr1.astype(BF16)
        dy = _nt(dr1b, wob_ref[...])
        sg, dsg = _silu_and_grad(z_ref[...])
        do_all = dy * sg
        dyz = dy * dsg

        for ci in range(n_chunks + 1):
            src = kvp_ref[...] if ci == 0 else kv_ref[(ci - 1) * CHUNK:ci * CHUNK, :]
            src = src.astype(F32)
            for i, e in enumerate(_expand_heads(src[:, :KV_WIDTH])):
                ek_ref[ci, i] = e
            for i, e in enumerate(_expand_heads(src[:, KV_WIDTH:])):
                ev_ref[ci, i] = e

        sink_cols = _sink_columns(sink_ref)
        for ci in reversed(range(n_chunks)):
            rows = slice(ci * CHUNK, (ci + 1) * CHUNK)
            sel = jnp.where(block * n_chunks + ci > 0, 1, 0)
            dk_band = jnp.zeros((2 * CHUNK, KV_WIDTH), F32)
            dv_band = jnp.zeros((2 * CHUNK, KV_WIDTH), F32)
            for kvh in range(N_KV_HEADS):
                kblk, vblk = _band_operands(ek_ref.at[ci], ek_ref.at[ci + 1], ev_ref.at[ci], ev_ref.at[ci + 1], kvh)
                qs = _stack_pairs(q_ref, rows, kvh)
                p, psink = _softmax_stack(_nt(qs, kblk) + bias_ref[sel, kvh], sink_cols[kvh])
                pb16 = p.astype(BF16)
                o = _nn(pb16, vblk)
                y = (o * _stack_pairs(sg, rows, kvh)).astype(BF16)
                dz = (o * _stack_pairs(dyz, rows, kvh)).astype(BF16)
                dob = _stack_pairs(do_all, rows, kvh).astype(BF16)
                dp = _nt(dob, vblk)
                dlogits = []
                for half in range(2):
                    hs = slice(half * 2 * CHUNK, (half + 1) * 2 * CHUNK)
                    delta = jnp.sum(dp[:, hs] * p[:, hs], axis=-1, keepdims=True)
                    dlogits.append(p[:, hs] * (dp[:, hs] - delta))
                    dsink_acc[2 * kvh + half] += psink[half] * delta
                dlog = jnp.concatenate(dlogits, axis=1)
                dl_acc[kvh] += dlog
                dsc = dlog.astype(BF16)
                dq = (_nn(dsc, kblk) * SCALE).astype(BF16)
                for pp in range(PAIRS_PER_KV):
                    blk = slice(pp * CHUNK, (pp + 1) * CHUNK)
                    cols = _pair_cols(kvh, pp)
                    y_ref[rows, cols] = y[blk]
                    dqz_ref[rows, cols] = dq[blk]
                    dqz_ref[rows, B_WIDTH + cols.start:B_WIDTH + cols.stop] = dz[blk]
                dk_band = dk_band + _fold_heads(_tn(dsc, qs), kvh)
                dv_band = dv_band + _fold_heads(_tn(pb16, dob), kvh)
            band = jnp.concatenate([dk_band, dv_band], axis=1)
            dkv_ref[rows, :] = (band[CHUNK:] + carry_ref[...]).astype(BF16)
            carry_ref[...] = band[:CHUNK]

        dqz = dqz_ref[...]
        dkv = dkv_ref[...]
        dh1 = ALPHA * dr1 + _nt(dkv, wkv_ref[...])
        for k in range(N_DEV):
            part = dqz[:, k * shard_cols:(k + 1) * shard_cols]
            dh1 = dh1 + _nt(part, wb_ref[k])
            dwb_acc[k] += _tn(h1b, part)
        dpost_ref[0:1, :] += jnp.sum(dh1 * xhat0, axis=0, keepdims=True)
        dpost_ref[1:2, :] += jnp.sum(dh1, axis=0, keepdims=True)
        dr0_ref[...] = _ln_backward(dh1 * pg_ref[0:1, :], xhat0, rstd0)
        dwkv_acc[...] += _tn(h1b, dkv)
        dwob_acc[...] += _tn(y_ref[...], dr1b)

        @pl.when(step == n_blocks - 1)
        def _():
            pltpu.sync_copy(dwb_acc, dwb_out)
            pltpu.sync_copy(dwkv_acc, dwkv_out)
            pltpu.sync_copy(dwob_acc, dwob_out)
            pltpu.sync_copy(dl_acc, dl_out)
            lane = lax.broadcasted_iota(jnp.int32, (1, CHUNK), 1)
            dsink = jnp.zeros((1, CHUNK), F32)
            for kvh in range(N_KV_HEADS):
                for half in range(2):
                    for pp in range(PAIRS_PER_KV):
                        head = 2 * (kvh * PAIRS_PER_KV + pp) + half
                        tot = jnp.sum(dsink_acc[2 * kvh + half, pp * CHUNK:(pp + 1) * CHUNK, :], axis=0, keepdims=True)
                        dsink = dsink - jnp.where(lane == head, tot, 0.0)
            dsink_ref[...] = dsink

    rev = functools.partial(_rows, tm, reverse_of=n_blocks)
    prev_chunk = pl.BlockSpec(
        (CHUNK, 2 * KV_WIDTH), lambda i: (jnp.maximum((n_blocks - 1 - i) * n_chunks - 1, 0), 0))
    return pl.pallas_call(
        body, name="bwd_b", grid=(n_blocks,),
        out_shape=[jax.ShapeDtypeStruct((seq, D_MODEL), F32), jax.ShapeDtypeStruct(wb.shape, F32),
                   jax.ShapeDtypeStruct(wkv.shape, F32), jax.ShapeDtypeStruct(wob.shape, F32),
                   jax.ShapeDtypeStruct((1, CHUNK), F32),
                   jax.ShapeDtypeStruct((N_KV_HEADS, STACK, 4 * CHUNK), F32), jax.ShapeDtypeStruct((2, D_MODEL), F32)],
        in_specs=[rev(D_MODEL), rev(D_MODEL), rev(B_WIDTH), rev(B_WIDTH), rev(2 * KV_WIDTH), prev_chunk,
                  _resident(wb.shape), _resident(wkv.shape), _resident(wob.shape), _resident(post_g.shape),
                  _resident(post_b.shape), SMEM, _resident(bias.shape)],
        out_specs=[rev(D_MODEL), ANY, ANY, ANY, pl.BlockSpec((1, CHUNK), lambda i: (0, 0)), ANY,
                   pl.BlockSpec((2, D_MODEL), lambda i: (0, 0))],
        scratch_shapes=[pltpu.VMEM(wb.shape, F32), pltpu.VMEM(wkv.shape, F32), pltpu.VMEM(wob.shape, F32),
                        pltpu.VMEM((N_KV_HEADS, STACK, 4 * CHUNK), F32), pltpu.VMEM((2 * N_KV_HEADS, STACK, 1), F32),
                        pltpu.VMEM((CHUNK, 2 * KV_WIDTH), F32),
                        pltpu.VMEM((n_chunks + 1, 4, CHUNK, CHUNK), BF16),
                        pltpu.VMEM((n_chunks + 1, 4, CHUNK, CHUNK), BF16),
                        pltpu.VMEM((tm, B_WIDTH), BF16), pltpu.VMEM((tm, 2 * B_WIDTH), BF16),
                        pltpu.VMEM((tm, 2 * KV_WIDTH), BF16)],
        compiler_params=_seq_params(),
    )(r0, dr1, q, z, kv, kv, wb, wkv, wob, post_g, post_b, sinks, bias)


def _bwd_a(dr0, p, ln_gb, w_spatial, bsp_t, woa, exchange):
    seq = dr0.shape[0]
    tm = min(TOKENS_A, seq)
    n_chunks = tm // CHUNK
    n_blocks = seq // tm

    def body(dr0_ref, p_ref, ln_ref, ws_ref, bsp_ref, woa_ref, *refs):
        ex_in, refs = refs[:exchange.n], refs[exchange.n:]
        (dp_ref, dwoa_out, dws_ref, dbsp_ref, dln_ref), refs = refs[:5], refs[5:]
        ex_out, refs = refs[:exchange.n], refs[exchange.n:]
        (dwoa_acc, s_ref, ds_ref, dvn_ref), ex_sems = refs[:4], refs[4:]
        step = pl.program_id(0)

        @pl.when(step == 0)
        def _():
            exchange.start(ex_in, ex_out, ex_sems)
            dwoa_acc[...] = jnp.zeros_like(dwoa_acc)
            dws_ref[...] = jnp.zeros_like(dws_ref)
            dbsp_ref[...] = jnp.zeros_like(dbsp_ref)
            dln_ref[...] = jnp.zeros_like(dln_ref)

        dr0b = dr0_ref[...].astype(BF16)

        vhat, rstd_v = _ln_normalize(p_ref[:, A_WIDTH:2 * A_WIDTH])
        ln_g = ln_ref[0:1, :]
        vn = (vhat * ln_g + ln_ref[1:2, :]).astype(BF16)
        _spatial_gate(ws_ref, bsp_ref, vn, s_ref, n_chunks)
        u = p_ref[:, :A_WIDTH]
        s = s_ref[...]
        sg, dsg = _silu_and_grad(p_ref[:, 2 * A_WIDTH:])
        gate = s * sg
        dwoa_acc[...] += _tn((u * gate).astype(BF16), dr0b)
        dy = _nt(dr0b, woa_ref[...])
        dyu = dy * u
        dp_ref[:, :A_WIDTH] = (dy * gate).astype(BF16)
        dp_ref[:, 2 * A_WIDTH:] = (dyu * (s * dsg)).astype(BF16)
        ds_ref[...] = dyu * sg

        lane = lax.broadcasted_iota(jnp.int32, (CHUNK, CHUNK), 1)
        dbsp = jnp.zeros((CHUNK, CHUNK), F32)
        for g in range(A_GROUPS):
            w = _masked_spatial(ws_ref, g)
            cols = slice(g * A_GROUP_DIM, (g + 1) * A_GROUP_DIM)
            dws = jnp.zeros((CHUNK, CHUNK), F32)
            for ci in range(n_chunks):
                rows = slice(ci * CHUNK, (ci + 1) * CHUNK)
                ds = ds_ref[rows, cols]
                dsb = ds.astype(BF16)
                dvn_ref[rows, cols] = _tn(w, dsb)
                dws = dws + _nt(dsb, vn[rows, cols])
                dbsp = dbsp + jnp.where(lane == g, jnp.sum(ds, axis=-1, keepdims=True), 0.0)
            dws_ref[g] += dws
        dbsp_ref[...] += dbsp

        dvn = dvn_ref[...]
        dln_ref[0:1, :] += jnp.sum(dvn * vhat, axis=0, keepdims=True)
        dln_ref[1:2, :] += jnp.sum(dvn, axis=0, keepdims=True)
        dp_ref[:, A_WIDTH:2 * A_WIDTH] = _ln_backward(dvn * ln_g, vhat, rstd_v).astype(BF16)

        @pl.when(step == n_blocks - 1)
        def _():
            t = lax.broadcasted_iota(jnp.int32, (CHUNK, CHUNK), 0)
            for g in range(A_GROUPS):
                dws_ref[g] = jnp.where(lane <= t, dws_ref[g], 0.0)
            pltpu.sync_copy(dwoa_acc, dwoa_out)
            exchange.wait(ex_in, ex_out, ex_sems)

    def fixed(shape):
        nd = len(shape)
        return pl.BlockSpec(shape, lambda i: (0,) * nd)

    return pl.pallas_call(
        body, name="bwd_a", grid=(n_blocks,),
        out_shape=[jax.ShapeDtypeStruct((seq, 3 * A_WIDTH), BF16),
                   jax.ShapeDtypeStruct(woa.shape, F32), jax.ShapeDtypeStruct(w_spatial.shape, F32),
                   jax.ShapeDtypeStruct((CHUNK, CHUNK), F32), jax.ShapeDtypeStruct((2, A_WIDTH), F32)]
        + exchange.out_shape,
        in_specs=[_rows(tm, D_MODEL), _rows(tm, 3 * A_WIDTH), _resident(ln_gb.shape),
                  _resident(w_spatial.shape), _resident(bsp_t.shape), _resident(woa.shape)] + [ANY] * exchange.n,
        out_specs=[_rows(tm, 3 * A_WIDTH), ANY, fixed(w_spatial.shape),
                   fixed((CHUNK, CHUNK)), fixed((2, A_WIDTH))] + [ANY] * exchange.n,
        scratch_shapes=[pltpu.VMEM(woa.shape, F32), pltpu.VMEM((tm, A_WIDTH), F32),
                        pltpu.VMEM((tm, A_WIDTH), F32), pltpu.VMEM((tm, A_WIDTH), F32)] + exchange.scratch_shapes,
        compiler_params=_seq_params(),
    )(dr0, p, ln_gb, w_spatial, bsp_t, woa, *exchange.inputs)


def _input_grad_blocks(seq):
    tm = min(TOKENS_DX, seq // 8)
    return tm, seq // tm


def _input_grad(dp, dr0, wa, first, n_blocks, earlier, exchange):
    seq = dr0.shape[0]
    tm, _ = _input_grad_blocks(seq)
    shard_cols = wa.shape[2]
    n_own = 3 if earlier is None else 4

    def body(dp_ref, dr0_ref, wa_ref, *refs):
        refs = refs[n_own - 3:]
        ex_in, dx_ref, ex_out, ex_sems = (refs[:exchange.n], refs[exchange.n], refs[exchange.n + 1:2 * exchange.n + 1],
                                          refs[2 * exchange.n + 1:])
        step = pl.program_id(0)

        @pl.when(step == 0)
        def _():
            exchange.start(ex_in, ex_out, ex_sems)

        dx = ALPHA * dr0_ref[...]
        for k in range(N_DEV):
            dx = dx + _nt(dp_ref[:, k * shard_cols:(k + 1) * shard_cols], wa_ref[k])
        dx_ref[...] = dx

        @pl.when(step == n_blocks - 1)
        def _():
            exchange.wait(ex_in, ex_out, ex_sems)

    def rows(width):
        return pl.BlockSpec((tm, width), lambda i: (first + i, 0))

    return pl.pallas_call(
        body, name=f"input_grad_{first}", grid=(n_blocks,),
        out_shape=[jax.ShapeDtypeStruct((seq, D_MODEL), F32)] + exchange.out_shape,
        in_specs=[rows(3 * A_WIDTH), rows(D_MODEL), _resident(wa.shape)] + [ANY] * (n_own - 3 + exchange.n),
        out_specs=[rows(D_MODEL)] + [ANY] * exchange.n,
        scratch_shapes=exchange.scratch_shapes,
        input_output_aliases={} if earlier is None else {3: 0},
        compiler_params=_seq_params(),
    )(dp, dr0, wa, *(() if earlier is None else (earlier,)), *exchange.inputs)


def _in_proj_weight_grad(xb, dp, shard_cols, exchange):
    seq = xb.shape[0]
    tk = min(TOKENS_WGRAD, seq)
    n_k = seq // tk

    def body(x_ref, dp_ref, *refs):
        ex_in, out_ref, ex_out, ex_sems = (refs[:exchange.n], refs[exchange.n], refs[exchange.n + 1:2 * exchange.n + 1],
                                           refs[2 * exchange.n + 1:])
        shard, step = pl.program_id(0), pl.program_id(1)

        @pl.when((shard == 0) & (step == 0))
        def _():
            exchange.start(ex_in, ex_out, ex_sems)

        @pl.when(step == 0)
        def _():
            out_ref[...] = jnp.zeros_like(out_ref)

        out_ref[...] += _tn(x_ref[...], dp_ref[...])

        @pl.when((shard == N_DEV - 1) & (step == n_k - 1))
        def _():
            exchange.wait(ex_in, ex_out, ex_sems)

    return pl.pallas_call(
        body, name="in_proj_weight_grad", grid=(N_DEV, n_k),
        out_shape=[jax.ShapeDtypeStruct((N_DEV, D_MODEL, shard_cols), F32)] + exchange.out_shape,
        in_specs=[pl.BlockSpec((tk, D_MODEL), lambda j, k: (k, 0)),
                  pl.BlockSpec((tk, shard_cols), lambda j, k: (k, j))] + [ANY] * exchange.n,
        out_specs=[pl.BlockSpec((None, D_MODEL, shard_cols), lambda j, k: (j, 0, 0))] + [ANY] * exchange.n,
        scratch_shapes=exchange.scratch_shapes,
        compiler_params=pltpu.CompilerParams(dimension_semantics=("arbitrary", "arbitrary"),
                                             vmem_limit_bytes=VMEM_LIMIT_BYTES),
    )(xb, dp, *exchange.inputs)


class _Exchange:
    def __init__(self, pair=(), chip=(), gather=()):
        self.pair, self.chip, self.gather = tuple(pair), tuple(chip), tuple(gather)
        self.inputs = self.pair + self.chip + self.gather
        self.n = len(self.inputs)
        self.out_shape = [jax.ShapeDtypeStruct((4, 1) + g.shape[2:], F32) for g in self.pair]
        self.out_shape += [jax.ShapeDtypeStruct((3,) + s.shape[1:], s.dtype) for s in self.chip]
        self.out_shape += [jax.ShapeDtypeStruct((N_DEV,) + g.shape, g.dtype) for g in self.gather]
        self.scratch_shapes = [pltpu.SemaphoreType.DMA((max(len(self.pair), 1),)) for _ in range(2)]
        self.scratch_shapes += [pltpu.SemaphoreType.DMA((max(len(self.chip), 1), 3)) for _ in range(2)]
        self.scratch_shapes += [pltpu.SemaphoreType.DMA((max(len(self.gather), 1), N_DEV)) for _ in range(2)]

    def _copies(self, ins, outs, sems):
        if not self.n:
            return []
        pair_send, pair_recv, chip_send, chip_recv, gather_send, gather_recv = sems
        x, y, c = lax.axis_index("x"), lax.axis_index("y"), lax.axis_index("c")
        copies = []
        for a in range(len(self.gather)):
            src = ins[len(self.pair) + len(self.chip) + a]
            dst = outs[len(self.pair) + len(self.chip) + a].at[4 * x + 2 * y + c]
            copies.append(pltpu.make_async_copy(src, dst, gather_send.at[a, 0]))
            for r in range(1, N_DEV):
                to = (1 - x if r & 4 else x, 1 - y if r & 2 else y, 1 - c if r & 1 else c)
                copies.append(pltpu.make_async_remote_copy(
                    src_ref=src, dst_ref=dst, send_sem=gather_send.at[a, r], recv_sem=gather_recv.at[a, r],
                    device_id=to, device_id_type=MESH))
        for a in range(len(self.pair)):
            copies.append(pltpu.make_async_remote_copy(
                src_ref=ins[a].at[:, pl.ds(1 - c, 1)], dst_ref=outs[a],
                send_sem=pair_send.at[a], recv_sem=pair_recv.at[a], device_id=(x, y, 1 - c), device_id_type=MESH))
        for a in range(len(self.chip)):
            src, dst = ins[len(self.pair) + a], outs[len(self.pair) + a]
            for k, (px, py) in enumerate([(1 - x, y), (x, 1 - y), (1 - x, 1 - y)]):
                copies.append(pltpu.make_async_remote_copy(
                    src_ref=src.at[2 * px + py], dst_ref=dst.at[k], send_sem=chip_send.at[a, k],
                    recv_sem=chip_recv.at[a, k], device_id=(px, py, c), device_id_type=MESH))
        return copies

    def start(self, ins, outs, sems):
        for cp in self._copies(ins, outs, sems):
            cp.start()

    def wait(self, ins, outs, sems):
        for cp in self._copies(ins, outs, sems):
            cp.wait()


def _pair_sum(place, grads, landeds, dtypes):
    n = len(grads)

    def body(place_ref, *refs):
        for a in range(n):
            refs[2 * n + a][...] = (refs[2 * a][...] + refs[2 * a + 1][...]).astype(dtypes[a])

    in_specs, out_specs, operands = [], [], []
    for g, l in zip(grads, landeds):
        rows, cols = g.shape[2:]
        in_specs += [pl.BlockSpec((None, None, rows, cols), lambda k, pr: (k, pr[1], 0, 0)),
                     pl.BlockSpec((None, None, rows, cols), lambda k, pr: (k, 0, 0, 0))]
        out_specs.append(pl.BlockSpec((None, rows, cols), lambda k, pr: (k, 0, 0)))
        operands += [g, l]
    return pl.pallas_call(
        body, name="pair_sum",
        out_shape=[jax.ShapeDtypeStruct((4,) + g.shape[2:], dt) for g, dt in zip(grads, dtypes)],
        grid_spec=pltpu.PrefetchScalarGridSpec(num_scalar_prefetch=1, grid=(4,), in_specs=in_specs,
                                               out_specs=out_specs),
        compiler_params=pltpu.CompilerParams(dimension_semantics=("parallel",), vmem_limit_bytes=VMEM_LIMIT_BYTES),
    )(place, *operands)


def _adamw_math(w, g, m, v):
    m = ADAM_B1 * m + (1.0 - ADAM_B1) * g
    v = ADAM_B2 * v + (1.0 - ADAM_B2) * (g * g)
    m_hat = m / (1.0 - ADAM_B1 ** ADAM_STEP)
    v_hat = v / (1.0 - ADAM_B2 ** ADAM_STEP)
    delta = -ADAM_LR * (m_hat / (jnp.sqrt(v_hat) + ADAM_EPS) + ADAM_WD * w)
    return delta, m, v


def _adamw_shards(place, items, steps):
    n = len(items)

    def body(place_ref, *refs):
        ins, outs = refs[:6 * n], refs[6 * n:]
        for i in range(n):
            a_ref, b_ref, got_ref, w_ref, m_ref, v_ref = ins[6 * i:6 * i + 6]
            g = a_ref[...] + b_ref[...]
            for k in range(3):
                g = g + got_ref[k].astype(F32)
            outs[4 * i][...] = g
            outs[4 * i + 1][...], outs[4 * i + 2][...], outs[4 * i + 3][...] = _adamw_math(
                w_ref[...], g, m_ref[...], v_ref[...])

    in_specs, out_specs, out_shape, operands = [], [], [], []
    for grad, landed, got, w, m, v in items:
        rows, cols = w.shape
        tr = rows // steps
        assert tr * steps == rows and tr % 16 == 0, (rows, steps)
        spec = pl.BlockSpec((tr, cols), lambda i, pr: (i, 0))
        in_specs += [pl.BlockSpec((None, None, tr, cols), lambda i, pr: (pr[0], pr[1], i, 0)),
                     pl.BlockSpec((None, None, tr, cols), lambda i, pr: (pr[0], 0, i, 0)),
                     pl.BlockSpec((3, tr, cols), lambda i, pr: (0, i, 0)), spec, spec, spec]
        out_specs += [spec] * 4
        out_shape += [jax.ShapeDtypeStruct((rows, cols), F32)] * 4
        operands += [grad, landed, got, w, m, v]
    outs = pl.pallas_call(
        body, name="adamw_shards", out_shape=out_shape,
        grid_spec=pltpu.PrefetchScalarGridSpec(num_scalar_prefetch=1, grid=(steps,), in_specs=in_specs,
                                               out_specs=out_specs),
        compiler_params=pltpu.CompilerParams(dimension_semantics=("parallel",), vmem_limit_bytes=VMEM_LIMIT_BYTES),
    )(place, *operands)
    return [outs[4 * i:4 * i + 4] for i in range(n)]


def _finish_small(place, grad, landed, got):
    rows, cols = got.shape[1:]

    def body(place_ref, grad_ref, landed_ref, got_ref, out_ref):
        out_ref[...] = (((grad_ref[...] + landed_ref[...]) + got_ref[0]) + got_ref[1]) + got_ref[2]

    return pl.pallas_call(
        body, name="finish_small", out_shape=jax.ShapeDtypeStruct((rows, cols), F32),
        grid_spec=pltpu.PrefetchScalarGridSpec(
            num_scalar_prefetch=1, grid=(1,),
            in_specs=[pl.BlockSpec((None, None, rows, cols), lambda i, pr: (pr[0], pr[1], 0, 0)),
                      pl.BlockSpec((None, None, rows, cols), lambda i, pr: (pr[0], 0, 0, 0)),
                      pl.BlockSpec((3, rows, cols), lambda i, pr: (0, 0, 0))],
            out_specs=pl.BlockSpec((rows, cols), lambda i, pr: (0, 0))),
    )(place, grad, landed, got)


def _adamw_small(params):
    n = len(params)

    def body(*refs):
        ins, outs = refs[:4 * n], refs[4 * n:]
        for i in range(n):
            w_ref, g_ref, m_ref, v_ref = ins[4 * i:4 * i + 4]
            d, m, v = _adamw_math(w_ref[...], g_ref[...], m_ref[...], v_ref[...])
            outs[3 * i][...] = d
            outs[3 * i + 1][...] = m
            outs[3 * i + 2][...] = v

    flat = [a for p in params for a in p]
    out_shape = [jax.ShapeDtypeStruct(p[0].shape, F32) for p in params for _ in range(3)]
    outs = pl.pallas_call(
        body, name="adamw_small", out_shape=out_shape, in_specs=[VMEM] * (4 * n), out_specs=[VMEM] * (3 * n),
    )(*flat)
    return [tuple(outs[3 * i:3 * i + 3]) for i in range(n)]


def _pack_small(parts):
    rows = []
    for a in parts:
        r, c = a.shape
        rows.append(jnp.pad(a, ((0, (-r) % 8), (0, CHUNK - c))))
    flat = jnp.concatenate(rows, axis=0)
    return jnp.pad(flat, ((0, N_DEV * SMALL_ROWS - flat.shape[0]), (0, 0))).reshape(N_DEV, SMALL_ROWS, CHUNK)


def _unpack_small(flat, shapes):
    out, at = [], 0
    for r, c in shapes:
        out.append(flat[at:at + r, :c])
        at += r + (-r) % 8
    return out


def kernel(x, w_in_a, sgu_ln_g, sgu_ln_b, w_spatial, b_spatial, w_out_a, w_kv, w_in_b, attn_sinks, rel_bias, w_out_b, post_ln_g, post_ln_b, loss_target, m_w_in_a, m_sgu_ln_g, m_sgu_ln_b, m_w_spatial, m_b_spatial, m_w_out_a, m_w_kv, m_w_in_b, m_attn_sinks, m_rel_bias, m_w_out_b, m_post_ln_g, m_post_ln_b, v_w_in_a, v_sgu_ln_g, v_sgu_ln_b, v_w_spatial, v_b_spatial, v_w_out_a, v_w_kv, v_w_in_b, v_attn_sinks, v_rel_bias, v_w_out_b, v_post_ln_g, v_post_ln_b):
    seq = x.shape[1]
    me = 4 * lax.axis_index("x") + 2 * lax.axis_index("y") + lax.axis_index("c")
    bucket = jnp.asarray(_band_tables())

    ln_shard = jnp.concatenate([sgu_ln_g, sgu_ln_b], axis=0)
    x2, tgt = x[0], loss_target[0]
    order = jnp.stack([2 * px + py for px, py, _ in _use_order()[::2]]).astype(jnp.int32)
    p, wa, woa, ln_all, xb = _gather_in_proj(order, x2, w_in_a[0], (w_out_a[0],), ln_shard)
    woa = woa.reshape(A_WIDTH, D_MODEL)
    ln_gb = jnp.transpose(ln_all, (1, 0, 2)).reshape(2, A_WIDTH)
    ws = w_spatial[0]
    bsp_t = jnp.pad(b_spatial[0].T, ((0, 0), (0, CHUNK - A_GROUPS)))
    rel_t = rel_bias.T
    bias = _bias_table(rel_t, bucket)

    r0, wkv, wb, wob = _fwd_a(x2, p, ln_gb, ws, bsp_t, woa, (w_kv, w_in_b[0], w_out_b[0]))
    wkv = wkv.reshape(D_MODEL, 2 * KV_WIDTH)
    wob = wob.reshape(B_WIDTH, D_MODEL)
    q, z, kv, dr1, loss, dg1, db1 = _fwd_b(r0, tgt, wb, wkv, wob, post_ln_g, post_ln_b, attn_sinks, bias)
    dr0, dwb, dwkv, dwob, dsink, dlog, dpost0 = _bwd_b(r0, dr1, q, z, kv, wb, wkv, wob, post_ln_g, post_ln_b,
                                                       attn_sinks, bias)
    drel = _rel_bias_grad(dlog, bucket)

    place = jnp.stack([2 * lax.axis_index("x") + lax.axis_index("y"), lax.axis_index("c")]).astype(jnp.int32)

    def by_device(g):
        return g.reshape((4, 2, g.shape[0] // N_DEV) + g.shape[1:]) if g.ndim == 2 else g.reshape((4, 2) + g.shape[1:])

    dwb, dwkv, dwob = by_device(dwb), by_device(dwkv), by_device(dwob)
    dp, dwoa, dws, dbsp_t, dln, l_wb, l_wkv, l_wob = _bwd_a(dr0, p, ln_gb, ws, bsp_t, woa,
                                                            _Exchange(pair=(dwb, dwkv, dwob)))
    dwoa = by_device(dwoa)
    small_shapes = [(A_GROUPS * CHUNK, CHUNK), (A_GROUPS, CHUNK), (1, N_Q_HEADS), (N_Q_HEADS, REL_BUCKETS),
                    (16, CHUNK), (16, CHUNK), (16, CHUNK), (16, CHUNK), (1, 1)]
    small = by_device(_pack_small([
        dws.reshape(A_GROUPS * CHUNK, CHUNK), dbsp_t[:, :A_GROUPS].T, dsink[:, :N_Q_HEADS], drel,
        jnp.concatenate([dpost0[0:1], dg1], axis=0).reshape(16, CHUNK),
        jnp.concatenate([dpost0[1:2], db1], axis=0).reshape(16, CHUNK),
        dln[0].reshape(16, CHUNK), dln[1].reshape(16, CHUNK), loss]))
    s_wb, s_wkv, s_wob = _pair_sum(place, (dwb, dwkv, dwob), (l_wb, l_wkv, l_wob), (BF16,) * 3)
    dwa, l_woa, l_small, g_wb, g_wkv, g_wob = _in_proj_weight_grad(
        xb, dp, wa.shape[2], _Exchange(pair=(dwoa, small), chip=(s_wb, s_wkv, s_wob)))
    dwa = by_device(dwa)
    s_woa, s_small = _pair_sum(place, (dwoa, small), (l_woa, l_small), (BF16, F32))
    _, blocks = _input_grad_blocks(seq)
    cut = blocks // 4
    dx, l_wa, g_woa, g_small = _input_grad(dp, dr0, wa, 0, cut, None, _Exchange(pair=(dwa,), chip=(s_woa, s_small)))
    s_wa, = _pair_sum(place, (dwa,), (l_wa,), (BF16,))
    my_small = _finish_small(place, small, l_small, g_small)
    dx, g_wa, small_all = _input_grad(dp, dr0, wa, cut, blocks - cut, dx,
                                      _Exchange(chip=(s_wa,), gather=(my_small,)))
    full = [dwa, dwoa, dwkv, dwb, dwob]
    landed = [l_wa, l_woa, l_wkv, l_wb, l_wob]
    got = [g_wa, g_woa, g_wkv, g_wb, g_wob]

    big = [(w_in_a, m_w_in_a, v_w_in_a), (w_out_a, m_w_out_a, v_w_out_a), (w_kv, m_w_kv, v_w_kv),
           (w_in_b, m_w_in_b, v_w_in_b), (w_out_b, m_w_out_b, v_w_out_b)]
    items = [(full[i], landed[i], got[i]) + tuple(a.reshape(got[i].shape[1:]) for a in big[i]) for i in range(5)]
    res = _adamw_shards(place, items[:1], 4) + _adamw_shards(place, items[1:], 2)
    big_out = [[r.reshape(big[i][0].shape) for r in res[i]] for i in range(5)]

    small_all = small_all.reshape(N_DEV * SMALL_ROWS, CHUNK)
    g_ws, g_bsp, g_sink, g_rel, g_pg, g_pb, g_lng, g_lnb, loss_all = _unpack_small(small_all, small_shapes)
    g_ws = g_ws.reshape(w_spatial.shape)
    g_bsp = g_bsp.reshape(b_spatial.shape)
    g_pg = g_pg.reshape(post_ln_g.shape)
    g_pb = g_pb.reshape(post_ln_b.shape)
    g_lng = lax.dynamic_slice(g_lng.reshape(1, A_WIDTH), (0, me * sgu_ln_g.shape[1]), sgu_ln_g.shape)
    g_lnb = lax.dynamic_slice(g_lnb.reshape(1, A_WIDTH), (0, me * sgu_ln_b.shape[1]), sgu_ln_b.shape)
    small_params = [(sgu_ln_g, g_lng, m_sgu_ln_g, v_sgu_ln_g), (sgu_ln_b, g_lnb, m_sgu_ln_b, v_sgu_ln_b),
                    (w_spatial, g_ws, m_w_spatial, v_w_spatial), (b_spatial, g_bsp, m_b_spatial, v_b_spatial),
                    (attn_sinks, g_sink, m_attn_sinks, v_attn_sinks), (rel_t, g_rel, m_rel_bias.T, v_rel_bias.T),
                    (post_ln_g, g_pg, m_post_ln_g, v_post_ln_g), (post_ln_b, g_pb, m_post_ln_b, v_post_ln_b)]
    flat2 = [tuple(a.reshape(-1, a.shape[-1]) for a in p) for p in small_params]
    small_out = [tuple(r.reshape(p[0].shape) for r in res) for res, p in zip(_adamw_small(flat2), small_params)]

    order = [("big", 0), ("small", 0), ("small", 1), ("small", 2), ("small", 3), ("big", 1), ("big", 2), ("big", 3),
             ("small", 4), ("small", 5), ("big", 4), ("small", 6), ("small", 7)]
    grads, deltas, new_m, new_v = [], [], [], []
    for kind, i in order:
        if kind == "big":
            g, d, m, v = big_out[i]
        else:
            g = small_params[i][1]
            d, m, v = small_out[i]
            if small_params[i][0] is rel_t:
                g, d, m, v = g.T, d.T, m.T, v.T
        grads.append(g)
        deltas.append(d)
        new_m.append(m)
        new_v.append(v)
    return (loss_all[0, 0], dx.reshape(x.shape), *grads, *deltas, *new_m, *new_v)
```

```python
import functools

import jax
import jax.numpy as jnp
import numpy as np
from jax import lax
from jax.experimental import pallas as pl
from jax.experimental.pallas import tpu as pltpu

F32 = jnp.float32
BF16 = jnp.bfloat16

D_MODEL = 1024
A_WIDTH = 2048
A_GROUPS = 8
A_GROUP_DIM = A_WIDTH // A_GROUPS
CHUNK = 128
HEAD_DIM = 64
N_Q_HEADS = 16
N_KV_HEADS = 2
PAIRS_PER_KV = N_Q_HEADS // N_KV_HEADS // 2
B_WIDTH = N_Q_HEADS * HEAD_DIM
KV_WIDTH = N_KV_HEADS * HEAD_DIM
REL_BUCKETS = 32
REL_MAX_DIST = 128
DEPTH = 2
ALPHA = (2.0 * DEPTH) ** 0.25
LN_EPS = 1e-5
NEG_INF = -1e30
SCALE = HEAD_DIM ** -0.5

ADAM_LR = 0.001
ADAM_B1 = 0.9
ADAM_B2 = 0.999
ADAM_EPS = 1e-08
ADAM_WD = 0.01
ADAM_STEP = 10

N_DEV = 8
MESH = pl.DeviceIdType.MESH
TOKENS_IN_PROJ = 1024
TOKENS_WGRAD = 4096
TOKENS_A = 256
TOKENS_A_FWD = 256
P_RING = 3
TOKENS_B = 512
PART_ROWS_B = 256
TOKENS_B_BWD = 256
TOKENS_DX = 512
VMEM_LIMIT_BYTES = 60 * 1024 * 1024
SMALL_ROWS = 144

ANY = pl.BlockSpec(memory_space=pl.ANY)
VMEM = pl.BlockSpec(memory_space=pltpu.VMEM)
SMEM = pl.BlockSpec(memory_space=pltpu.SMEM)


def _nn(a, b):
    return jnp.dot(a, b, preferred_element_type=F32)


def _nt(a, b):
    return lax.dot_general(a, b, (((1,), (1,)), ((), ())), preferred_element_type=F32)


def _tn(a, b):
    return lax.dot_general(a, b, (((0,), (0,)), ((), ())), preferred_element_type=F32)


def _resident(shape):
    nd = len(shape)
    return pl.BlockSpec(shape, lambda i: (0,) * nd, pipeline_mode=pl.Buffered(1))


def _rows(tm, width, reverse_of=None):
    if reverse_of is None:
        return pl.BlockSpec((tm, width), lambda i: (i, 0))
    return pl.BlockSpec((tm, width), lambda i: (reverse_of - 1 - i, 0))


def _seq_params():
    return pltpu.CompilerParams(dimension_semantics=("arbitrary",), vmem_limit_bytes=VMEM_LIMIT_BYTES)


def _in_turns(first, second):
    merged = [(i / max(len(first), 1), 0, f) for i, f in enumerate(first)]
    merged += [((i + 0.5) / max(len(second), 1), 1, f) for i, f in enumerate(second)]
    return [f for _, _, f in sorted(merged, key=lambda t: t[:2])]


def _ln_normalize(r):
    mu = jnp.mean(r, axis=-1, keepdims=True)
    xc = r - mu
    var = jnp.mean(xc * xc, axis=-1, keepdims=True)
    rstd = lax.rsqrt(var + LN_EPS)
    return xc * rstd, rstd


def _ln_backward(dxhat, xhat, rstd):
    m1 = jnp.mean(dxhat, axis=-1, keepdims=True)
    m2 = jnp.mean(dxhat * xhat, axis=-1, keepdims=True)
    return rstd * (dxhat - m1 - xhat * m2)


def _silu_and_grad(z):
    sig = jax.nn.sigmoid(z)
    silu = z * sig
    return silu, sig + silu * (1.0 - sig)


def _band_tables():
    t = np.arange(CHUNK, dtype=np.int32)[:, None]
    j = np.arange(2 * CHUNK, dtype=np.int32)[None, :]
    d = t + CHUNK - j
    in_window = (d >= 0) & (d < CHUNK)
    dc = np.clip(d, 0, REL_MAX_DIST - 1)
    max_exact = REL_BUCKETS // 2
    df = np.maximum(dc, 1).astype(np.float32)
    large = max_exact + (np.log(df / np.float32(max_exact)) / np.float32(np.log(REL_MAX_DIST / max_exact))
                         * np.float32(REL_BUCKETS - max_exact)).astype(np.int32)
    large = np.minimum(large, REL_BUCKETS - 1)
    bucket = np.where(dc < max_exact, dc, large)
    return np.where(in_window, bucket, -1).astype(np.int32)


def _use_order():
    x, y, c = lax.axis_index("x"), lax.axis_index("y"), lax.axis_index("c")
    blocks = [(x, y, c), (x, y, 1 - c)]
    for px, py in [(1 - x, y), (x, 1 - y), (1 - x, 1 - y)]:
        blocks += [(px, py, c), (px, py, 1 - c)]
    return blocks


def _gather_in_proj(order, x, w_in_a, small, ln_gb):
    seq = x.shape[0]
    tm = min(TOKENS_IN_PROJ, seq)
    n_blocks = seq // tm
    shard_cols = w_in_a.shape[1]
    n = len(small) + 2

    def body(order_ref, x_ref, *refs):
        ins, refs = refs[:n], refs[n:]
        p_ref, outs, xb_out, refs = refs[0], refs[1:n + 1], refs[n + 1], refs[n + 2:]
        wa_vmem, xb_vmem, stage = refs[0], refs[1], refs[2:n]
        send_sems, recv_sems, local_sems, out_sems, xb_sem = refs[n:]
        xb_copy = pltpu.make_async_copy(xb_vmem, xb_out, xb_sem)
        phase, blk = pl.program_id(0), pl.program_id(1)
        x_, y_, c_ = lax.axis_index("x"), lax.axis_index("y"), lax.axis_index("c")
        me, sibling = (x_, y_, c_), (x_, y_, 1 - c_)
        chips = [(1 - x_, y_), (x_, 1 - y_), (1 - x_, 1 - y_)]
        use = _use_order()

        def index(px, py, pc):
            return 4 * px + 2 * py + pc

        def slot(a, block):
            return (wa_vmem if a == 0 else outs[a]).at[index(*block)]

        def own(a):
            return slot(0, me) if a == 0 else (ins[a] if a == n - 1 else stage[a - 1])

        def copy(a, k, block, to, src=None):
            return pltpu.make_async_remote_copy(
                src_ref=slot(a, block) if src is None else src, dst_ref=slot(a, block),
                send_sem=send_sems.at[a, k], recv_sem=recv_sems.at[a, k], device_id=to, device_id_type=MESH)

        def keep(a):
            return pltpu.make_async_copy(own(a), slot(a, me), local_sems.at[a])

        def write_out(k):
            return pltpu.make_async_copy(slot(0, use[k]), outs[0].at[index(*use[k])], out_sems.at[k])

        def pass_on(a, j):
            copy(a, 1 + j, (*chips[j], c_), me).wait_recv()
            copy(a, 4 + j, (*chips[j], c_), sibling).start()

        def send(a, j):
            copy(a, 1 + j, me, (*chips[j], c_), own(a)).start()

        @pl.when((phase == 0) & (blk == 0))
        def _():
            wa_vmem[index(*me)] = ins[0][...].astype(BF16)
            for a in range(1, n - 1):
                stage[a - 1][...] = ins[a][...].astype(BF16)
            send(0, 0)
            send(0, 1)
            for a in range(n):
                copy(a, 0, me, sibling, own(a)).start()
                if a:
                    keep(a).start()
            write_out(0).start()
            copy(0, 0, sibling, me).wait_recv()
            write_out(1).start()

        @pl.when((phase == 1) & (blk == 0))
        def _():
            pass_on(0, 0)
            write_out(2).start()
            xb_copy.start()
            send(0, 2)
            for a in range(1, n):
                for j in range(3):
                    send(a, j)
            pass_on(0, 1)
            write_out(4).start()
            copy(0, 4, (*chips[0], 1 - c_), me).wait_recv()
            write_out(3).start()

        @pl.when((phase == 2) & (blk == 0))
        def _():
            copy(0, 5, (*chips[1], 1 - c_), me).wait_recv()
            write_out(5).start()

        @pl.when((phase == 2) & (blk == n_blocks // 2))
        def _():
            pass_on(0, 2)
            write_out(6).start()

        @pl.when((phase == 3) & (blk == 0))
        def _():
            copy(0, 6, (*chips[2], 1 - c_), me).wait_recv()
            write_out(7).start()
            for a in range(1, n):
                pass_on(a, 0)
                pass_on(a, 1)

        rows = pl.ds(pl.multiple_of(blk * tm, tm), tm)

        @pl.when(phase == 0)
        def _():
            xb_vmem[rows, :] = x_ref[...].astype(BF16)

        chip = order_ref[phase]
        xb = xb_vmem[rows, :]
        p_ref[:, :shard_cols] = _nn(xb, wa_vmem[2 * chip])
        p_ref[:, shard_cols:] = _nn(xb, wa_vmem[2 * chip + 1])

        @pl.when((phase == N_DEV // 2 - 1) & (blk == n_blocks - 1))
        def _():
            for a in range(1, n):
                pass_on(a, 2)
            for a in range(1, n):
                copy(a, 0, sibling, me).wait_recv()
                for j, far in enumerate(chips):
                    copy(a, 4 + j, (*far, 1 - c_), me).wait_recv()
            for a in range(n):
                for k in range(7):
                    copy(a, k, me, me).wait_send()
                if a:
                    keep(a).wait()
            for k in range(N_DEV):
                write_out(k).wait()
            xb_copy.wait()

    gathered = [jax.ShapeDtypeStruct((N_DEV,) + w_in_a.shape, BF16)]
    gathered += [jax.ShapeDtypeStruct((N_DEV,) + s.shape, BF16) for s in small]
    gathered.append(jax.ShapeDtypeStruct((N_DEV,) + ln_gb.shape, F32))
    return pl.pallas_call(
        body, name="gather_in_proj",
        out_shape=[jax.ShapeDtypeStruct((seq, N_DEV * shard_cols), F32)] + gathered
        + [jax.ShapeDtypeStruct((seq, D_MODEL), BF16)],
        grid_spec=pltpu.PrefetchScalarGridSpec(
            num_scalar_prefetch=1, grid=(N_DEV // 2, n_blocks),
            in_specs=[pl.BlockSpec((tm, D_MODEL), lambda k, i, order: (jnp.where(k == 0, i, n_blocks - 1), 0))]
            + [VMEM] * n,
            out_specs=[pl.BlockSpec((tm, 2 * shard_cols), lambda k, i, order: (i, order[k]))] + [ANY] * (n + 1),
            scratch_shapes=[pltpu.VMEM((N_DEV,) + w_in_a.shape, BF16), pltpu.VMEM((seq, D_MODEL), BF16)]
            + [pltpu.VMEM(s.shape, BF16) for s in small] + [
                pltpu.SemaphoreType.DMA((n, 7)), pltpu.SemaphoreType.DMA((n, 7)), pltpu.SemaphoreType.DMA((n,)),
                pltpu.SemaphoreType.DMA((N_DEV,)), pltpu.SemaphoreType.DMA(())]),
        compiler_params=pltpu.CompilerParams(dimension_semantics=("arbitrary", "arbitrary"),
                                             vmem_limit_bytes=VMEM_LIMIT_BYTES),
    )(order, x, w_in_a, *small, ln_gb)


def _bias_table(rel_bias_t, bucket):
    def body(rel_ref, bucket_ref, out_ref):
        first = jnp.broadcast_to(bucket_ref[0:1, :], (8, 2 * CHUNK))
        col = lax.broadcasted_iota(jnp.int32, (CHUNK, 2 * CHUNK), 1)
        for h in range(N_Q_HEADS):
            row = jnp.full(first.shape, NEG_INF, F32)
            for b in range(REL_BUCKETS):
                row = jnp.where(first == b, rel_ref[h, b], row)
            tab = pltpu.roll(jnp.broadcast_to(row[0:1, :], (CHUNK, 2 * CHUNK)), 0, 1, stride=1, stride_axis=0)
            pair = h // 2
            rows = slice((pair % PAIRS_PER_KV) * CHUNK, (pair % PAIRS_PER_KV + 1) * CHUNK)
            half = slice((h % 2) * 2 * CHUNK, (h % 2 + 1) * 2 * CHUNK)
            out_ref[1, pair // PAIRS_PER_KV, rows, half] = tab
            out_ref[0, pair // PAIRS_PER_KV, rows, half] = jnp.where(col < CHUNK, NEG_INF, tab)

    return pl.pallas_call(
        body, name="bias_table",
        out_shape=jax.ShapeDtypeStruct((2, N_KV_HEADS, PAIRS_PER_KV * CHUNK, 4 * CHUNK), F32),
        in_specs=[SMEM, VMEM], out_specs=VMEM,
    )(rel_bias_t, bucket)


def _rel_bias_grad(dlogits, bucket):
    def body(dl_ref, bucket_ref, out_ref):
        i = lax.broadcasted_iota(jnp.int32, (CHUNK, CHUNK), 0)
        k = lax.broadcasted_iota(jnp.int32, (CHUNK, CHUNK), 1)
        reverse = jnp.where(i + k == CHUNK - 1, 1.0, 0.0)
        sums = []
        for h in range(N_Q_HEADS):
            pair = h // 2
            dl = dl_ref[pair // PAIRS_PER_KV, (pair % PAIRS_PER_KV) * CHUNK:(pair % PAIRS_PER_KV + 1) * CHUNK,
                        (h % 2) * 2 * CHUNK:(h % 2 + 1) * 2 * CHUNK]
            backwards = jnp.dot(reverse, dl, precision=lax.Precision.HIGHEST, preferred_element_type=F32)
            turned = pltpu.roll(backwards, CHUNK + 1, 1, stride=1, stride_axis=0)
            sums.append(jnp.sum(turned, axis=0, keepdims=True))
        by_offset = jnp.concatenate(sums, axis=0)
        buckets = lax.broadcasted_iota(jnp.int32, (REL_BUCKETS, 2 * CHUNK), 0)
        one_hot = jnp.where(jnp.broadcast_to(bucket_ref[0:1, :], buckets.shape) == buckets, 1.0, 0.0)
        out_ref[...] = lax.dot_general(by_offset, one_hot, (((1,), (1,)), ((), ())),
                                       precision=lax.Precision.HIGHEST, preferred_element_type=F32)

    return pl.pallas_call(
        body, name="rel_bias_grad", out_shape=jax.ShapeDtypeStruct((N_Q_HEADS, REL_BUCKETS), F32),
        in_specs=[VMEM, VMEM], out_specs=VMEM,
    )(dlogits, bucket)


def _masked_spatial(ws_ref, g):
    t = lax.broadcasted_iota(jnp.int32, (CHUNK, CHUNK), 0)
    s = lax.broadcasted_iota(jnp.int32, (CHUNK, CHUNK), 1)
    return jnp.where(s <= t, ws_ref[g], 0.0).astype(BF16)


def _spatial_gate(ws_ref, bsp_t_ref, vn, s_ref, n_chunks):
    for g in range(A_GROUPS):
        w = _masked_spatial(ws_ref, g)
        cols = slice(g * A_GROUP_DIM, (g + 1) * A_GROUP_DIM)
        for ci in range(n_chunks):
            rows = slice(ci * CHUNK, (ci + 1) * CHUNK)
            s_ref[rows, cols] = _nn(w, vn[rows, cols]) + bsp_t_ref[:, g:g + 1]


def _fwd_a(x, p, ln_gb, w_spatial, bsp_t, woa, later):
    seq = x.shape[0]
    tm = min(TOKENS_A_FWD, seq)
    n_chunks = tm // CHUNK
    n_blocks = seq // tm
    n = len(later)

    def body(x_ref, p_hbm, ln_ref, ws_ref, bsp_ref, woa_ref, *refs):
        shards, refs = refs[:n], refs[n:]
        r0_ref, gathered, refs = refs[0], refs[1:1 + n], refs[1 + n:]
        s_ref, p_ring, stage, (send_sems, recv_sems, local_sems, ring_sems) = refs[0], refs[1], refs[2:2 + n], refs[2 + n:]
        step = pl.program_id(0)

        def fetch(block):
            slot = block % P_RING
            start = block * tm if isinstance(block, int) else pl.multiple_of(block * tm, tm)
            return pltpu.make_async_copy(p_hbm.at[pl.ds(start, tm), :], p_ring.at[slot], ring_sems.at[slot])

        @pl.when(step == 0)
        def _():
            for ahead in range(min(P_RING - 1, n_blocks)):
                fetch(ahead).start()

        @pl.when(step + P_RING - 1 < n_blocks)
        def _():
            fetch(step + P_RING - 1).start()

        fetch(step).wait()
        p_ref = p_ring.at[step % P_RING]

        def copies(a):
            x_, y_, c_ = lax.axis_index("x"), lax.axis_index("y"), lax.axis_index("c")
            mine = 4 * x_ + 2 * y_ + c_
            out = [pltpu.make_async_copy(stage[a], gathered[a].at[mine], local_sems.at[a])]
            for r in range(1, N_DEV):
                to = (1 - x_ if r & 4 else x_, 1 - y_ if r & 2 else y_, 1 - c_ if r & 1 else c_)
                out.append(pltpu.make_async_remote_copy(
                    src_ref=stage[a], dst_ref=gathered[a].at[mine], send_sem=send_sems.at[a, r - 1],
                    recv_sem=recv_sems.at[a, r - 1], device_id=to, device_id_type=MESH))
            return out

        @pl.when(step == 0)
        def _():
            for a in range(n):
                stage[a][...] = shards[a][...].astype(BF16)
                for cp in copies(a):
                    cp.start()

        xv = x_ref[...]
        vhat, _ = _ln_normalize(p_ref[:, A_WIDTH:2 * A_WIDTH])
        vn = (vhat * ln_ref[0:1, :] + ln_ref[1:2, :]).astype(BF16)
        _spatial_gate(ws_ref, bsp_ref, vn, s_ref, n_chunks)
        z = p_ref[:, 2 * A_WIDTH:]
        y = p_ref[:, :A_WIDTH] * s_ref[...] * (z * jax.nn.sigmoid(z))
        r0_ref[...] = ALPHA * xv + _nn(y.astype(BF16), woa_ref[...])

        @pl.when(step == n_blocks - 1)
        def _():
            for a in range(n):
                for cp in copies(a):
                    cp.wait()

    return pl.pallas_call(
        body, name="fwd_a", grid=(n_blocks,),
        out_shape=[jax.ShapeDtypeStruct((seq, D_MODEL), F32)]
        + [jax.ShapeDtypeStruct((N_DEV,) + s.shape, BF16) for s in later],
        in_specs=[_rows(tm, D_MODEL), ANY, _resident(ln_gb.shape), _resident(w_spatial.shape),
                  _resident(bsp_t.shape), _resident(woa.shape)] + [VMEM] * n,
        out_specs=[_rows(tm, D_MODEL)] + [ANY] * n,
        scratch_shapes=[pltpu.VMEM((tm, A_WIDTH), F32), pltpu.VMEM((P_RING, tm, 3 * A_WIDTH), F32)]
        + [pltpu.VMEM(s.shape, BF16) for s in later] + [
            pltpu.SemaphoreType.DMA((n, N_DEV - 1)), pltpu.SemaphoreType.DMA((n, N_DEV - 1)),
            pltpu.SemaphoreType.DMA((n,)), pltpu.SemaphoreType.DMA((P_RING,))],
        compiler_params=_seq_params(),
    )(x, p, ln_gb, w_spatial, bsp_t, woa, *later)


def _expand_heads(kv):
    lo = lax.broadcasted_iota(jnp.int32, kv.shape, 1) < HEAD_DIM
    swapped = pltpu.roll(kv, HEAD_DIM, 1)
    zero = jnp.zeros_like(kv)
    return (jnp.where(lo, kv, zero).astype(BF16), jnp.where(lo, zero, swapped).astype(BF16),
            jnp.where(lo, swapped, zero).astype(BF16), jnp.where(lo, zero, kv).astype(BF16))


def _fold_heads(acc, kvh):
    top, bot = acc[:2 * CHUNK], acc[2 * CHUNK:]
    lo = lax.broadcasted_iota(jnp.int32, top.shape, 1) < HEAD_DIM
    t = jnp.where(lo, top, 0.0)
    b = jnp.where(lo, 0.0, bot)
    if kvh == 0:
        return t + pltpu.roll(b, HEAD_DIM, 1)
    return pltpu.roll(t, HEAD_DIM, 1) + b


STACK = PAIRS_PER_KV * CHUNK


def _pair_cols(kvh, pp):
    pair = kvh * PAIRS_PER_KV + pp
    return slice(pair * CHUNK, (pair + 1) * CHUNK)


def _stack_pairs(ref_or_val, rows, kvh):
    return jnp.concatenate([ref_or_val[rows, _pair_cols(kvh, pp)] for pp in range(PAIRS_PER_KV)], axis=0)


def _sink_columns(sink_ref):
    blk = lax.broadcasted_iota(jnp.int32, (STACK, 1), 0) // CHUNK
    out = []
    for kvh in range(N_KV_HEADS):
        halves = []
        for half in range(2):
            col = jnp.zeros((STACK, 1), F32)
            for pp in range(PAIRS_PER_KV):
                col = jnp.where(blk == pp, sink_ref[0, 2 * (kvh * PAIRS_PER_KV + pp) + half], col)
            halves.append(col)
        out.append(halves)
    return out


def _softmax_stack(logits, sink_cols):
    ps, sinks = [], []
    for half in range(2):
        lg = logits[:, half * 2 * CHUNK:(half + 1) * 2 * CHUNK]
        sk = sink_cols[half]
        m = jnp.maximum(jnp.max(lg, axis=-1, keepdims=True), sk)
        e = jnp.exp(lg - m)
        es = jnp.exp(sk - m)
        inv = 1.0 / (jnp.sum(e, axis=-1, keepdims=True) + es)
        ps.append(e * inv)
        sinks.append(es * inv)
    return jnp.concatenate(ps, axis=1), sinks


def _band_operands(before_k, this_k, before_v, this_v, kvh):
    kblk = jnp.concatenate([before_k[2 * kvh], this_k[2 * kvh], before_k[2 * kvh + 1], this_k[2 * kvh + 1]], axis=0)
    vblk = jnp.concatenate([before_v[2 * kvh], this_v[2 * kvh], before_v[2 * kvh + 1], this_v[2 * kvh + 1]], axis=0)
    return kblk, vblk


def _fwd_b(r0, target, wb, wkv, wob, post_g, post_b, sinks, bias):
    seq = r0.shape[0]
    tm = min(TOKENS_B, seq)
    n_chunks = tm // CHUNK
    part_rows = min(PART_ROWS_B, tm)
    n_parts = tm // part_rows
    part_chunks = part_rows // CHUNK
    shard_cols = wb.shape[2]

    def body(r0_ref, tgt_ref, wb_ref, wkv_ref, wob_ref, pg_ref, pb_ref, sink_ref, bias_ref,
             q_ref, z_ref, kv_ref, dr1_ref, loss_ref, dg_ref, db_ref, ek_last, ev_last, *scratch):
        q_s, z_s, h1_s, o_s, ek_s, ev_s = (scratch[i * n_parts:(i + 1) * n_parts] for i in range(6))
        step = pl.program_id(0)

        @pl.when(step == 0)
        def _():
            ek_last[...] = jnp.zeros_like(ek_last)
            ev_last[...] = jnp.zeros_like(ev_last)
            loss_ref[...] = jnp.zeros_like(loss_ref)
            dg_ref[...] = jnp.zeros_like(dg_ref)
            db_ref[...] = jnp.zeros_like(db_ref)

        sink_cols = _sink_columns(sink_ref)
        g1 = pg_ref[1:2, :]

        def project_steps(part):
            rows = slice(part * part_rows, (part + 1) * part_rows)
            live = {}

            def normalize():
                xhat0, _ = _ln_normalize(r0_ref[rows, :])
                h1 = xhat0 * pg_ref[0:1, :] + pb_ref[0:1, :]
                h1_s[part][...] = h1
                live["h1b"] = h1.astype(BF16)

            def shard(k):
                def run():
                    qz = _nn(live["h1b"], wb_ref[k])
                    lo, hi = k * shard_cols, (k + 1) * shard_cols
                    if hi <= B_WIDTH:
                        qs = (qz * SCALE).astype(BF16)
                        q_ref[rows, lo:hi] = qs
                        q_s[part][:, lo:hi] = qs
                    else:
                        z_ref[rows, lo - B_WIDTH:hi - B_WIDTH] = qz
                        z_s[part][:, lo - B_WIDTH:hi - B_WIDTH] = qz
                return run

            def keys():
                kv = _nn(live["h1b"], wkv_ref[...])
                kv_ref[rows, :] = kv.astype(BF16)
                for cj in range(part_chunks):
                    sub = slice(cj * CHUNK, (cj + 1) * CHUNK)
                    for i, e in enumerate(_expand_heads(kv[sub, :KV_WIDTH])):
                        ek_s[part][cj, i] = e
                    for i, e in enumerate(_expand_heads(kv[sub, KV_WIDTH:])):
                        ev_s[part][cj, i] = e

            return [normalize, keys] + [shard(k) for k in range(N_DEV)]

        def attend_steps(part):
            def stack(cj, kvh):
                def run():
                    sub = slice(cj * CHUNK, (cj + 1) * CHUNK)
                    sel = jnp.where(step * n_chunks + part * part_chunks + cj > 0, 1, 0)
                    if cj:
                        before_k, before_v = ek_s[part].at[cj - 1], ev_s[part].at[cj - 1]
                    elif part:
                        before_k, before_v = ek_s[part - 1].at[part_chunks - 1], ev_s[part - 1].at[part_chunks - 1]
                    else:
                        before_k, before_v = ek_last, ev_last
                    kblk, vblk = _band_operands(before_k, ek_s[part].at[cj], before_v, ev_s[part].at[cj], kvh)
                    logits = _nt(_stack_pairs(q_s[part], sub, kvh), kblk) + bias_ref[sel, kvh]
                    p, _ = _softmax_stack(logits, sink_cols[kvh])
                    o = _nn(p.astype(BF16), vblk)
                    for pp in range(PAIRS_PER_KV):
                        o_s[part][sub, _pair_cols(kvh, pp)] = o[pp * CHUNK:(pp + 1) * CHUNK]
                return run

            def finish():
                rows = slice(part * part_rows, (part + 1) * part_rows)
                z = z_s[part][...]
                y = o_s[part][...] * (z * jax.nn.sigmoid(z))
                r1 = ALPHA * h1_s[part][...] + _nn(y.astype(BF16), wob_ref[...])
                xhat1, rstd1 = _ln_normalize(r1)
                diff = xhat1 * g1 + pb_ref[1:2, :] - tgt_ref[rows, :]
                loss_ref[...] += 0.5 * jnp.sum(jnp.mean(diff * diff, axis=-1, keepdims=True), axis=0, keepdims=True)
                dh2 = diff * (1.0 / D_MODEL)
                dg_ref[...] += jnp.sum(dh2 * xhat1, axis=0, keepdims=True)
                db_ref[...] += jnp.sum(dh2, axis=0, keepdims=True)
                dr1_ref[rows, :] = _ln_backward(dh2 * g1, xhat1, rstd1)

            return [stack(cj, kvh) for cj in range(part_chunks) for kvh in range(N_KV_HEADS)] + [finish]

        for run in project_steps(0):
            run()
        for part in range(n_parts):
            for run in _in_turns(attend_steps(part), project_steps(part + 1) if part + 1 < n_parts else []):
                run()
        ek_last[...] = ek_s[n_parts - 1][part_chunks - 1]
        ev_last[...] = ev_s[n_parts - 1][part_chunks - 1]

    vec = pl.BlockSpec((1, D_MODEL), lambda i: (0, 0))
    return pl.pallas_call(
        body, name="fwd_b", grid=(seq // tm,),
        out_shape=[jax.ShapeDtypeStruct((seq, B_WIDTH), BF16), jax.ShapeDtypeStruct((seq, B_WIDTH), F32),
                   jax.ShapeDtypeStruct((seq, 2 * KV_WIDTH), BF16), jax.ShapeDtypeStruct((seq, D_MODEL), F32),
                   jax.ShapeDtypeStruct((1, 1), F32), jax.ShapeDtypeStruct((1, D_MODEL), F32),
                   jax.ShapeDtypeStruct((1, D_MODEL), F32)],
        in_specs=[_rows(tm, D_MODEL), _rows(tm, D_MODEL), _resident(wb.shape), _resident(wkv.shape),
                  _resident(wob.shape), _resident(post_g.shape), _resident(post_b.shape), SMEM,
                  _resident(bias.shape)],
        out_specs=[_rows(tm, B_WIDTH), _rows(tm, B_WIDTH), _rows(tm, 2 * KV_WIDTH), _rows(tm, D_MODEL),
                   pl.BlockSpec((1, 1), lambda i: (0, 0)), vec, vec],
        scratch_shapes=[pltpu.VMEM((4, CHUNK, CHUNK), BF16), pltpu.VMEM((4, CHUNK, CHUNK), BF16)]
        + [pltpu.VMEM((part_rows, B_WIDTH), BF16)] * n_parts + [pltpu.VMEM((part_rows, B_WIDTH), F32)] * n_parts
        + [pltpu.VMEM((part_rows, D_MODEL), F32)] * n_parts + [pltpu.VMEM((part_rows, B_WIDTH), F32)] * n_parts
        + [pltpu.VMEM((part_chunks, 4, CHUNK, CHUNK), BF16)] * (2 * n_parts),
        compiler_params=_seq_params(),
    )(r0, target, wb, wkv, wob, post_g, post_b, sinks, bias)


def _bwd_b(r0, dr1, q, z, kv, wb, wkv, wob, post_g, post_b, sinks, bias):
    seq = r0.shape[0]
    tm = min(TOKENS_B_BWD, seq)
    n_chunks = tm // CHUNK
    n_blocks = seq // tm
    shard_cols = wb.shape[2]

    def body(r0_ref, dr1_ref, q_ref, z_ref, kv_ref, kvp_ref, wb_ref, wkv_ref, wob_ref, pg_ref, pb_ref, sink_ref,
             bias_ref, dr0_ref, dwb_out, dwkv_out, dwob_out, dsink_ref, dl_out, dpost_ref,
             dwb_acc, dwkv_acc, dwob_acc, dl_acc, dsink_acc, carry_ref, ek_ref, ev_ref, y_ref, dqz_ref, dkv_ref):
        step = pl.program_id(0)
        block = n_blocks - 1 - step

        @pl.when(step == 0)
        def _():
            dpost_ref[...] = jnp.zeros_like(dpost_ref)
            dwb_acc[...] = jnp.zeros_like(dwb_acc)
            dwkv_acc[...] = jnp.zeros_like(dwkv_acc)
            dwob_acc[...] = jnp.zeros_like(dwob_acc)
            dl_acc[...] = jnp.zeros_like(dl_acc)
            carry_ref[...] = jnp.zeros_like(carry_ref)
            dsink_acc[...] = jnp.zeros_like(dsink_acc)

        xhat0, rstd0 = _ln_normalize(r0_ref[...])
        h1b = (xhat0 * pg_ref[0:1, :] + pb_ref[0:1, :]).astype(BF16)
        dr1 = dr1_ref[...]
        dr1b = dr1.astype(BF16)
        dy = _nt(dr1b, wob_ref[...])
        sg, dsg = _silu_and_grad(z_ref[...])
        do_all = dy * sg
        dyz = dy * dsg

        for ci in range(n_chunks + 1):
            src = kvp_ref[...] if ci == 0 else kv_ref[(ci - 1) * CHUNK:ci * CHUNK, :]
            src = src.astype(F32)
            for i, e in enumerate(_expand_heads(src[:, :KV_WIDTH])):
                ek_ref[ci, i] = e
            for i, e in enumerate(_expand_heads(src[:, KV_WIDTH:])):
                ev_ref[ci, i] = e

        sink_cols = _sink_columns(sink_ref)
        for ci in reversed(range(n_chunks)):
            rows = slice(ci * CHUNK, (ci + 1) * CHUNK)
            sel = jnp.where(block * n_chunks + ci > 0, 1, 0)
            dk_band = jnp.zeros((2 * CHUNK, KV_WIDTH), F32)
            dv_band = jnp.zeros((2 * CHUNK, KV_WIDTH), F32)
            for kvh in range(N_KV_HEADS):
                kblk, vblk = _band_operands(ek_ref.at[ci], ek_ref.at[ci + 1], ev_ref.at[ci], ev_ref.at[ci + 1], kvh)
                qs = _stack_pairs(q_ref, rows, kvh)
                p, psink = _softmax_stack(_nt(qs, kblk) + bias_ref[sel, kvh], sink_cols[kvh])
                pb16 = p.astype(BF16)
                o = _nn(pb16, vblk)
                y = (o * _stack_pairs(sg, rows, kvh)).astype(BF16)
                dz = (o * _stack_pairs(dyz, rows, kvh)).astype(BF16)
                dob = _stack_pairs(do_all, rows, kvh).astype(BF16)
                dp = _nt(dob, vblk)
                dlogits = []
                for half in range(2):
                    hs = slice(half * 2 * CHUNK, (half + 1) * 2 * CHUNK)
                    delta = jnp.sum(dp[:, hs] * p[:, hs], axis=-1, keepdims=True)
                    dlogits.append(p[:, hs] * (dp[:, hs] - delta))
                    dsink_acc[2 * kvh + half] += psink[half] * delta
                dlog = jnp.concatenate(dlogits, axis=1)
                dl_acc[kvh] += dlog
                dsc = dlog.astype(BF16)
                dq = (_nn(dsc, kblk) * SCALE).astype(BF16)
                for pp in range(PAIRS_PER_KV):
                    blk = slice(pp * CHUNK, (pp + 1) * CHUNK)
                    cols = _pair_cols(kvh, pp)
                    y_ref[rows, cols] = y[blk]
                    dqz_ref[rows, cols] = dq[blk]
                    dqz_ref[rows, B_WIDTH + cols.start:B_WIDTH + cols.stop] = dz[blk]
                dk_band = dk_band + _fold_heads(_tn(dsc, qs), kvh)
                dv_band = dv_band + _fold_heads(_tn(pb16, dob), kvh)
            band = jnp.concatenate([dk_band, dv_band], axis=1)
            dkv_ref[rows, :] = (band[CHUNK:] + carry_ref[...]).astype(BF16)
            carry_ref[...] = band[:CHUNK]

        dqz = dqz_ref[...]
        dkv = dkv_ref[...]
        dh1 = ALPHA * dr1 + _nt(dkv, wkv_ref[...])
        for k in range(N_DEV):
            part = dqz[:, k * shard_cols:(k + 1) * shard_cols]
            dh1 = dh1 + _nt(part, wb_ref[k])
            dwb_acc[k] += _tn(h1b, part)
        dpost_ref[0:1, :] += jnp.sum(dh1 * xhat0, axis=0, keepdims=True)
        dpost_ref[1:2, :] += jnp.sum(dh1, axis=0, keepdims=True)
        dr0_ref[...] = _ln_backward(dh1 * pg_ref[0:1, :], xhat0, rstd0)
        dwkv_acc[...] += _tn(h1b, dkv)
        dwob_acc[...] += _tn(y_ref[...], dr1b)

        @pl.when(step == n_blocks - 1)
        def _():
            pltpu.sync_copy(dwb_acc, dwb_out)
            pltpu.sync_copy(dwkv_acc, dwkv_out)
            pltpu.sync_copy(dwob_acc, dwob_out)
            pltpu.sync_copy(dl_acc, dl_out)
            lane = lax.broadcasted_iota(jnp.int32, (1, CHUNK), 1)
            dsink = jnp.zeros((1, CHUNK), F32)
            for kvh in range(N_KV_HEADS):
                for half in range(2):
                    for pp in range(PAIRS_PER_KV):
                        head = 2 * (kvh * PAIRS_PER_KV + pp) + half
                        tot = jnp.sum(dsink_acc[2 * kvh + half, pp * CHUNK:(pp + 1) * CHUNK, :], axis=0, keepdims=True)
                        dsink = dsink - jnp.where(lane == head, tot, 0.0)
            dsink_ref[...] = dsink

    rev = functools.partial(_rows, tm, reverse_of=n_blocks)
    prev_chunk = pl.BlockSpec(
        (CHUNK, 2 * KV_WIDTH), lambda i: (jnp.maximum((n_blocks - 1 - i) * n_chunks - 1, 0), 0))
    return pl.pallas_call(
        body, name="bwd_b", grid=(n_blocks,),
        out_shape=[jax.ShapeDtypeStruct((seq, D_MODEL), F32), jax.ShapeDtypeStruct(wb.shape, F32),
                   jax.ShapeDtypeStruct(wkv.shape, F32), jax.ShapeDtypeStruct(wob.shape, F32),
                   jax.ShapeDtypeStruct((1, CHUNK), F32),
                   jax.ShapeDtypeStruct((N_KV_HEADS, STACK, 4 * CHUNK), F32), jax.ShapeDtypeStruct((2, D_MODEL), F32)],
        in_specs=[rev(D_MODEL), rev(D_MODEL), rev(B_WIDTH), rev(B_WIDTH), rev(2 * KV_WIDTH), prev_chunk,
                  _resident(wb.shape), _resident(wkv.shape), _resident(wob.shape), _resident(post_g.shape),
                  _resident(post_b.shape), SMEM, _resident(bias.shape)],
        out_specs=[rev(D_MODEL), ANY, ANY, ANY, pl.BlockSpec((1, CHUNK), lambda i: (0, 0)), ANY,
                   pl.BlockSpec((2, D_MODEL), lambda i: (0, 0))],
        scratch_shapes=[pltpu.VMEM(wb.shape, F32), pltpu.VMEM(wkv.shape, F32), pltpu.VMEM(wob.shape, F32),
                        pltpu.VMEM((N_KV_HEADS, STACK, 4 * CHUNK), F32), pltpu.VMEM((2 * N_KV_HEADS, STACK, 1), F32),
                        pltpu.VMEM((CHUNK, 2 * KV_WIDTH), F32),
                        pltpu.VMEM((n_chunks + 1, 4, CHUNK, CHUNK), BF16),
                        pltpu.VMEM((n_chunks + 1, 4, CHUNK, CHUNK), BF16),
                        pltpu.VMEM((tm, B_WIDTH), BF16), pltpu.VMEM((tm, 2 * B_WIDTH), BF16),
                        pltpu.VMEM((tm, 2 * KV_WIDTH), BF16)],
        compiler_params=_seq_params(),
    )(r0, dr1, q, z, kv, kv, wb, wkv, wob, post_g, post_b, sinks, bias)


def _bwd_a(dr0, p, ln_gb, w_spatial, bsp_t, woa, exchange):
    seq = dr0.shape[0]
    tm = min(TOKENS_A, seq)
    n_chunks = tm // CHUNK
    n_blocks = seq // tm

    def body(dr0_ref, p_ref, ln_ref, ws_ref, bsp_ref, woa_ref, *refs):
        ex_in, refs = refs[:exchange.n], refs[exchange.n:]
        (dp_ref, dwoa_out, dws_ref, dbsp_ref, dln_ref), refs = refs[:5], refs[5:]
        ex_out, refs = refs[:exchange.n], refs[exchange.n:]
        (dwoa_acc, s_ref, ds_ref, dvn_ref), ex_sems = refs[:4], refs[4:]
        step = pl.program_id(0)

        @pl.when(step == 0)
        def _():
            exchange.start(ex_in, ex_out, ex_sems)
            dwoa_acc[...] = jnp.zeros_like(dwoa_acc)
            dws_ref[...] = jnp.zeros_like(dws_ref)
            dbsp_ref[...] = jnp.zeros_like(dbsp_ref)
            dln_ref[...] = jnp.zeros_like(dln_ref)

        dr0b = dr0_ref[...].astype(BF16)

        vhat, rstd_v = _ln_normalize(p_ref[:, A_WIDTH:2 * A_WIDTH])
        ln_g = ln_ref[0:1, :]
        vn = (vhat * ln_g + ln_ref[1:2, :]).astype(BF16)
        _spatial_gate(ws_ref, bsp_ref, vn, s_ref, n_chunks)
        u = p_ref[:, :A_WIDTH]
        s = s_ref[...]
        sg, dsg = _silu_and_grad(p_ref[:, 2 * A_WIDTH:])
        gate = s * sg
        dwoa_acc[...] += _tn((u * gate).astype(BF16), dr0b)
        dy = _nt(dr0b, woa_ref[...])
        dyu = dy * u
        dp_ref[:, :A_WIDTH] = (dy * gate).astype(BF16)
        dp_ref[:, 2 * A_WIDTH:] = (dyu * (s * dsg)).astype(BF16)
        ds_ref[...] = dyu * sg

        lane = lax.broadcasted_iota(jnp.int32, (CHUNK, CHUNK), 1)
        dbsp = jnp.zeros((CHUNK, CHUNK), F32)
        for g in range(A_GROUPS):
            w = _masked_spatial(ws_ref, g)
            cols = slice(g * A_GROUP_DIM, (g + 1) * A_GROUP_DIM)
            dws = jnp.zeros((CHUNK, CHUNK), F32)
            for ci in range(n_chunks):
                rows = slice(ci * CHUNK, (ci + 1) * CHUNK)
                ds = ds_ref[rows, cols]
                dsb = ds.astype(BF16)
                dvn_ref[rows, cols] = _tn(w, dsb)
                dws = dws + _nt(dsb, vn[rows, cols])
                dbsp = dbsp + jnp.where(lane == g, jnp.sum(ds, axis=-1, keepdims=True), 0.0)
            dws_ref[g] += dws
        dbsp_ref[...] += dbsp

        dvn = dvn_ref[...]
        dln_ref[0:1, :] += jnp.sum(dvn * vhat, axis=0, keepdims=True)
        dln_ref[1:2, :] += jnp.sum(dvn, axis=0, keepdims=True)
        dp_ref[:, A_WIDTH:2 * A_WIDTH] = _ln_backward(dvn * ln_g, vhat, rstd_v).astype(BF16)

        @pl.when(step == n_blocks - 1)
        def _():
            t = lax.broadcasted_iota(jnp.int32, (CHUNK, CHUNK), 0)
            for g in range(A_GROUPS):
                dws_ref[g] = jnp.where(lane <= t, dws_ref[g], 0.0)
            pltpu.sync_copy(dwoa_acc, dwoa_out)
            exchange.wait(ex_in, ex_out, ex_sems)

    def fixed(shape):
        nd = len(shape)
        return pl.BlockSpec(shape, lambda i: (0,) * nd)

    return pl.pallas_call(
        body, name="bwd_a", grid=(n_blocks,),
        out_shape=[jax.ShapeDtypeStruct((seq, 3 * A_WIDTH), BF16),
                   jax.ShapeDtypeStruct(woa.shape, F32), jax.ShapeDtypeStruct(w_spatial.shape, F32),
                   jax.ShapeDtypeStruct((CHUNK, CHUNK), F32), jax.ShapeDtypeStruct((2, A_WIDTH), F32)]
        + exchange.out_shape,
        in_specs=[_rows(tm, D_MODEL), _rows(tm, 3 * A_WIDTH), _resident(ln_gb.shape),
                  _resident(w_spatial.shape), _resident(bsp_t.shape), _resident(woa.shape)] + [ANY] * exchange.n,
        out_specs=[_rows(tm, 3 * A_WIDTH), ANY, fixed(w_spatial.shape),
                   fixed((CHUNK, CHUNK)), fixed((2, A_WIDTH))] + [ANY] * exchange.n,
        scratch_shapes=[pltpu.VMEM(woa.shape, F32), pltpu.VMEM((tm, A_WIDTH), F32),
                        pltpu.VMEM((tm, A_WIDTH), F32), pltpu.VMEM((tm, A_WIDTH), F32)] + exchange.scratch_shapes,
        compiler_params=_seq_params(),
    )(dr0, p, ln_gb, w_spatial, bsp_t, woa, *exchange.inputs)


def _input_grad_blocks(seq):
    tm = min(TOKENS_DX, seq // 8)
    return tm, seq // tm


def _input_grad(dp, dr0, wa, first, n_blocks, earlier, exchange):
    seq = dr0.shape[0]
    tm, _ = _input_grad_blocks(seq)
    shard_cols = wa.shape[2]
    n_own = 3 if earlier is None else 4

    def body(dp_ref, dr0_ref, wa_ref, *refs):
        refs = refs[n_own - 3:]
        ex_in, dx_ref, ex_out, ex_sems = (refs[:exchange.n], refs[exchange.n], refs[exchange.n + 1:2 * exchange.n + 1],
                                          refs[2 * exchange.n + 1:])
        step = pl.program_id(0)

        @pl.when(step == 0)
        def _():
            exchange.start(ex_in, ex_out, ex_sems)

        dx = ALPHA * dr0_ref[...]
        for k in range(N_DEV):
            dx = dx + _nt(dp_ref[:, k * shard_cols:(k + 1) * shard_cols], wa_ref[k])
        dx_ref[...] = dx

        @pl.when(step == n_blocks - 1)
        def _():
            exchange.wait(ex_in, ex_out, ex_sems)

    def rows(width):
        return pl.BlockSpec((tm, width), lambda i: (first + i, 0))

    return pl.pallas_call(
        body, name=f"input_grad_{first}", grid=(n_blocks,),
        out_shape=[jax.ShapeDtypeStruct((seq, D_MODEL), F32)] + exchange.out_shape,
        in_specs=[rows(3 * A_WIDTH), rows(D_MODEL), _resident(wa.shape)] + [ANY] * (n_own - 3 + exchange.n),
        out_specs=[rows(D_MODEL)] + [ANY] * exchange.n,
        scratch_shapes=exchange.scratch_shapes,
        input_output_aliases={} if earlier is None else {3: 0},
        compiler_params=_seq_params(),
    )(dp, dr0, wa, *(() if earlier is None else (earlier,)), *exchange.inputs)


def _in_proj_weight_grad(xb, dp, shard_cols, exchange):
    seq = xb.shape[0]
    tk = min(TOKENS_WGRAD, seq)
    n_k = seq // tk

    def body(x_ref, dp_ref, *refs):
        ex_in, out_ref, ex_out, ex_sems = (refs[:exchange.n], refs[exchange.n], refs[exchange.n + 1:2 * exchange.n + 1],
                                           refs[2 * exchange.n + 1:])
        shard, step = pl.program_id(0), pl.program_id(1)

        @pl.when((shard == 0) & (step == 0))
        def _():
            exchange.start(ex_in, ex_out, ex_sems)

        @pl.when(step == 0)
        def _():
            out_ref[...] = jnp.zeros_like(out_ref)

        out_ref[...] += _tn(x_ref[...], dp_ref[...])

        @pl.when((shard == N_DEV - 1) & (step == n_k - 1))
        def _():
            exchange.wait(ex_in, ex_out, ex_sems)

    return pl.pallas_call(
        body, name="in_proj_weight_grad", grid=(N_DEV, n_k),
        out_shape=[jax.ShapeDtypeStruct((N_DEV, D_MODEL, shard_cols), F32)] + exchange.out_shape,
        in_specs=[pl.BlockSpec((tk, D_MODEL), lambda j, k: (k, 0)),
                  pl.BlockSpec((tk, shard_cols), lambda j, k: (k, j))] + [ANY] * exchange.n,
        out_specs=[pl.BlockSpec((None, D_MODEL, shard_cols), lambda j, k: (j, 0, 0))] + [ANY] * exchange.n,
        scratch_shapes=exchange.scratch_shapes,
        compiler_params=pltpu.CompilerParams(dimension_semantics=("arbitrary", "arbitrary"),
                                             vmem_limit_bytes=VMEM_LIMIT_BYTES),
    )(xb, dp, *exchange.inputs)


class _Exchange:
    def __init__(self, pair=(), chip=(), gather=()):
        self.pair, self.chip, self.gather = tuple(pair), tuple(chip), tuple(gather)
        self.inputs = self.pair + self.chip + self.gather
        self.n = len(self.inputs)
        self.out_shape = [jax.ShapeDtypeStruct((4, 1) + g.shape[2:], F32) for g in self.pair]
        self.out_shape += [jax.ShapeDtypeStruct((3,) + s.shape[1:], s.dtype) for s in self.chip]
        self.out_shape += [jax.ShapeDtypeStruct((N_DEV,) + g.shape, g.dtype) for g in self.gather]
        self.scratch_shapes = [pltpu.SemaphoreType.DMA((max(len(self.pair), 1),)) for _ in range(2)]
        self.scratch_shapes += [pltpu.SemaphoreType.DMA((max(len(self.chip), 1), 3)) for _ in range(2)]
        self.scratch_shapes += [pltpu.SemaphoreType.DMA((max(len(self.gather), 1), N_DEV)) for _ in range(2)]

    def _copies(self, ins, outs, sems):
        if not self.n:
            return []
        pair_send, pair_recv, chip_send, chip_recv, gather_send, gather_recv = sems
        x, y, c = lax.axis_index("x"), lax.axis_index("y"), lax.axis_index("c")
        copies = []
        for a in range(len(self.gather)):
            src = ins[len(self.pair) + len(self.chip) + a]
            dst = outs[len(self.pair) + len(self.chip) + a].at[4 * x + 2 * y + c]
            copies.append(pltpu.make_async_copy(src, dst, gather_send.at[a, 0]))
            for r in range(1, N_DEV):
                to = (1 - x if r & 4 else x, 1 - y if r & 2 else y, 1 - c if r & 1 else c)
                copies.append(pltpu.make_async_remote_copy(
                    src_ref=src, dst_ref=dst, send_sem=gather_send.at[a, r], recv_sem=gather_recv.at[a, r],
                    device_id=to, device_id_type=MESH))
        for a in range(len(self.pair)):
            copies.append(pltpu.make_async_remote_copy(
                src_ref=ins[a].at[:, pl.ds(1 - c, 1)], dst_ref=outs[a],
                send_sem=pair_send.at[a], recv_sem=pair_recv.at[a], device_id=(x, y, 1 - c), device_id_type=MESH))
        for a in range(len(self.chip)):
            src, dst = ins[len(self.pair) + a], outs[len(self.pair) + a]
            for k, (px, py) in enumerate([(1 - x, y), (x, 1 - y), (1 - x, 1 - y)]):
                copies.append(pltpu.make_async_remote_copy(
                    src_ref=src.at[2 * px + py], dst_ref=dst.at[k], send_sem=chip_send.at[a, k],
                    recv_sem=chip_recv.at[a, k], device_id=(px, py, c), device_id_type=MESH))
        return copies

    def start(self, ins, outs, sems):
        for cp in self._copies(ins, outs, sems):
            cp.start()

    def wait(self, ins, outs, sems):
        for cp in self._copies(ins, outs, sems):
            cp.wait()


def _pair_sum(place, grads, landeds, dtypes, finish=None):
    n = len(grads)

    def body(place_ref, *refs):
        ins, outs = refs[:len(in_specs)], refs[len(in_specs):]
        for a in range(n):
            outs[a][...] = (ins[2 * a][...] + ins[2 * a + 1][...]).astype(dtypes[a])
        if finish is not None:
            grad_ref, landed_ref, got_ref = ins[2 * n:]

            @pl.when(pl.program_id(0) == 0)
            def _():
                outs[n][...] = (((grad_ref[...] + landed_ref[...]) + got_ref[0]) + got_ref[1]) + got_ref[2]

    in_specs, out_specs, operands = [], [], []
    for g, l in zip(grads, landeds):
        rows, cols = g.shape[2:]
        in_specs += [pl.BlockSpec((None, None, rows, cols), lambda k, pr: (k, pr[1], 0, 0)),
                     pl.BlockSpec((None, None, rows, cols), lambda k, pr: (k, 0, 0, 0))]
        out_specs.append(pl.BlockSpec((None, rows, cols), lambda k, pr: (k, 0, 0)))
        operands += [g, l]
    out_shape = [jax.ShapeDtypeStruct((4,) + g.shape[2:], dt) for g, dt in zip(grads, dtypes)]
    if finish is not None:
        rows, cols = finish[2].shape[1:]
        in_specs += [pl.BlockSpec((None, None, rows, cols), lambda k, pr: (pr[0], pr[1], 0, 0)),
                     pl.BlockSpec((None, None, rows, cols), lambda k, pr: (pr[0], 0, 0, 0)),
                     pl.BlockSpec((3, rows, cols), lambda k, pr: (0, 0, 0))]
        out_specs.append(pl.BlockSpec((rows, cols), lambda k, pr: (0, 0)))
        out_shape.append(jax.ShapeDtypeStruct((rows, cols), F32))
        operands += list(finish)
    return pl.pallas_call(
        body, name="pair_sum", out_shape=out_shape,
        grid_spec=pltpu.PrefetchScalarGridSpec(num_scalar_prefetch=1, grid=(4,), in_specs=in_specs,
                                               out_specs=out_specs),
        compiler_params=pltpu.CompilerParams(dimension_semantics=("arbitrary",), vmem_limit_bytes=VMEM_LIMIT_BYTES),
    )(place, *operands)


def _adamw_math(w, g, m, v):
    m = ADAM_B1 * m + (1.0 - ADAM_B1) * g
    v = ADAM_B2 * v + (1.0 - ADAM_B2) * (g * g)
    m_hat = m / (1.0 - ADAM_B1 ** ADAM_STEP)
    v_hat = v / (1.0 - ADAM_B2 ** ADAM_STEP)
    delta = -ADAM_LR * (m_hat / (jnp.sqrt(v_hat) + ADAM_EPS) + ADAM_WD * w)
    return delta, m, v


def _adamw_shards(place, items, steps, small=()):
    n, n_small = len(items), len(small)

    def body(place_ref, *refs):
        ins, small_ins = refs[:6 * n], refs[6 * n:6 * n + 4 * n_small]
        outs, small_outs = refs[6 * n + 4 * n_small:10 * n + 4 * n_small], refs[10 * n + 4 * n_small:]
        for i in range(n):
            a_ref, b_ref, got_ref, w_ref, m_ref, v_ref = ins[6 * i:6 * i + 6]
            g = a_ref[...] + b_ref[...]
            for k in range(3):
                g = g + got_ref[k].astype(F32)
            outs[4 * i][...] = g
            outs[4 * i + 1][...], outs[4 * i + 2][...], outs[4 * i + 3][...] = _adamw_math(
                w_ref[...], g, m_ref[...], v_ref[...])

        @pl.when(pl.program_id(0) == 0)
        def _():
            for i in range(n_small):
                w_ref, g_ref, m_ref, v_ref = small_ins[4 * i:4 * i + 4]
                small_outs[3 * i][...], small_outs[3 * i + 1][...], small_outs[3 * i + 2][...] = _adamw_math(
                    w_ref[...], g_ref[...], m_ref[...], v_ref[...])

    def whole(a):
        return pl.BlockSpec(a.shape, lambda i, pr: (0, 0))

    in_specs, out_specs, out_shape, operands = [], [], [], []
    for grad, landed, got, w, m, v in items:
        rows, cols = w.shape
        tr = rows // steps
        assert tr * steps == rows and tr % 16 == 0, (rows, steps)
        spec = pl.BlockSpec((tr, cols), lambda i, pr: (i, 0))
        in_specs += [pl.BlockSpec((None, None, tr, cols), lambda i, pr: (pr[0], pr[1], i, 0)),
                     pl.BlockSpec((None, None, tr, cols), lambda i, pr: (pr[0], 0, i, 0)),
                     pl.BlockSpec((3, tr, cols), lambda i, pr: (0, i, 0)), spec, spec, spec]
        out_specs += [spec] * 4
        out_shape += [jax.ShapeDtypeStruct((rows, cols), F32)] * 4
        operands += [grad, landed, got, w, m, v]
    for p in small:
        in_specs += [whole(a) for a in p]
        out_specs += [whole(p[0])] * 3
        out_shape += [jax.ShapeDtypeStruct(p[0].shape, F32)] * 3
        operands += list(p)
    outs = pl.pallas_call(
        body, name="adamw_shards", out_shape=out_shape,
        grid_spec=pltpu.PrefetchScalarGridSpec(num_scalar_prefetch=1, grid=(steps,), in_specs=in_specs,
                                               out_specs=out_specs),
        compiler_params=pltpu.CompilerParams(dimension_semantics=("arbitrary",), vmem_limit_bytes=VMEM_LIMIT_BYTES),
    )(place, *operands)
    small_outs = outs[4 * n:]
    return ([outs[4 * i:4 * i + 4] for i in range(n)],
            [tuple(small_outs[3 * i:3 * i + 3]) for i in range(n_small)])


def _pack_small(parts):
    rows = []
    for a in parts:
        r, c = a.shape
        rows.append(jnp.pad(a, ((0, (-r) % 8), (0, CHUNK - c))))
    flat = jnp.concatenate(rows, axis=0)
    return jnp.pad(flat, ((0, N_DEV * SMALL_ROWS - flat.shape[0]), (0, 0))).reshape(N_DEV, SMALL_ROWS, CHUNK)


def _unpack_small(flat, shapes):
    out, at = [], 0
    for r, c in shapes:
        out.append(flat[at:at + r, :c])
        at += r + (-r) % 8
    return out


def kernel(x, w_in_a, sgu_ln_g, sgu_ln_b, w_spatial, b_spatial, w_out_a, w_kv, w_in_b, attn_sinks, rel_bias, w_out_b, post_ln_g, post_ln_b, loss_target, m_w_in_a, m_sgu_ln_g, m_sgu_ln_b, m_w_spatial, m_b_spatial, m_w_out_a, m_w_kv, m_w_in_b, m_attn_sinks, m_rel_bias, m_w_out_b, m_post_ln_g, m_post_ln_b, v_w_in_a, v_sgu_ln_g, v_sgu_ln_b, v_w_spatial, v_b_spatial, v_w_out_a, v_w_kv, v_w_in_b, v_attn_sinks, v_rel_bias, v_w_out_b, v_post_ln_g, v_post_ln_b):
    seq = x.shape[1]
    me = 4 * lax.axis_index("x") + 2 * lax.axis_index("y") + lax.axis_index("c")
    bucket = jnp.asarray(_band_tables())

    ln_shard = jnp.concatenate([sgu_ln_g, sgu_ln_b], axis=0)
    x2, tgt = x[0], loss_target[0]
    order = jnp.stack([2 * px + py for px, py, _ in _use_order()[::2]]).astype(jnp.int32)
    p, wa, woa, ln_all, xb = _gather_in_proj(order, x2, w_in_a[0], (w_out_a[0],), ln_shard)
    woa = woa.reshape(A_WIDTH, D_MODEL)
    ln_gb = jnp.transpose(ln_all, (1, 0, 2)).reshape(2, A_WIDTH)
    ws = w_spatial[0]
    bsp_t = jnp.pad(b_spatial[0].T, ((0, 0), (0, CHUNK - A_GROUPS)))
    rel_t = rel_bias.T
    bias = _bias_table(rel_t, bucket)

    r0, wkv, wb, wob = _fwd_a(x2, p, ln_gb, ws, bsp_t, woa, (w_kv, w_in_b[0], w_out_b[0]))
    wkv = wkv.reshape(D_MODEL, 2 * KV_WIDTH)
    wob = wob.reshape(B_WIDTH, D_MODEL)
    q, z, kv, dr1, loss, dg1, db1 = _fwd_b(r0, tgt, wb, wkv, wob, post_ln_g, post_ln_b, attn_sinks, bias)
    dr0, dwb, dwkv, dwob, dsink, dlog, dpost0 = _bwd_b(r0, dr1, q, z, kv, wb, wkv, wob, post_ln_g, post_ln_b,
                                                       attn_sinks, bias)
    drel = _rel_bias_grad(dlog, bucket)

    place = jnp.stack([2 * lax.axis_index("x") + lax.axis_index("y"), lax.axis_index("c")]).astype(jnp.int32)

    def by_device(g):
        return g.reshape((4, 2, g.shape[0] // N_DEV) + g.shape[1:]) if g.ndim == 2 else g.reshape((4, 2) + g.shape[1:])

    dwb, dwkv, dwob = by_device(dwb), by_device(dwkv), by_device(dwob)
    dp, dwoa, dws, dbsp_t, dln, l_wb, l_wkv, l_wob = _bwd_a(dr0, p, ln_gb, ws, bsp_t, woa,
                                                            _Exchange(pair=(dwb, dwkv, dwob)))
    dwoa = by_device(dwoa)
    small_shapes = [(A_GROUPS * CHUNK, CHUNK), (A_GROUPS, CHUNK), (1, N_Q_HEADS), (N_Q_HEADS, REL_BUCKETS),
                    (16, CHUNK), (16, CHUNK), (16, CHUNK), (16, CHUNK), (1, 1)]
    small = by_device(_pack_small([
        dws.reshape(A_GROUPS * CHUNK, CHUNK), dbsp_t[:, :A_GROUPS].T, dsink[:, :N_Q_HEADS], drel,
        jnp.concatenate([dpost0[0:1], dg1], axis=0).reshape(16, CHUNK),
        jnp.concatenate([dpost0[1:2], db1], axis=0).reshape(16, CHUNK),
        dln[0].reshape(16, CHUNK), dln[1].reshape(16, CHUNK), loss]))
    s_wb, s_wkv, s_wob = _pair_sum(place, (dwb, dwkv, dwob), (l_wb, l_wkv, l_wob), (BF16,) * 3)
    dwa, l_woa, l_small, g_wb, g_wkv, g_wob = _in_proj_weight_grad(
        xb, dp, wa.shape[2], _Exchange(pair=(dwoa, small), chip=(s_wb, s_wkv, s_wob)))
    dwa = by_device(dwa)
    s_woa, s_small = _pair_sum(place, (dwoa, small), (l_woa, l_small), (BF16, F32))
    _, blocks = _input_grad_blocks(seq)
    cut = blocks // 4
    dx, l_wa, g_woa, g_small = _input_grad(dp, dr0, wa, 0, cut, None, _Exchange(pair=(dwa,), chip=(s_woa, s_small)))
    s_wa, my_small = _pair_sum(place, (dwa,), (l_wa,), (BF16,), finish=(small, l_small, g_small))
    dx, g_wa, small_all = _input_grad(dp, dr0, wa, cut, blocks - cut, dx,
                                      _Exchange(chip=(s_wa,), gather=(my_small,)))
    full = [dwa, dwoa, dwkv, dwb, dwob]
    landed = [l_wa, l_woa, l_wkv, l_wb, l_wob]
    got = [g_wa, g_woa, g_wkv, g_wb, g_wob]

    big = [(w_in_a, m_w_in_a, v_w_in_a), (w_out_a, m_w_out_a, v_w_out_a), (w_kv, m_w_kv, v_w_kv),
           (w_in_b, m_w_in_b, v_w_in_b), (w_out_b, m_w_out_b, v_w_out_b)]
    items = [(full[i], landed[i], got[i]) + tuple(a.reshape(got[i].shape[1:]) for a in big[i]) for i in range(5)]

    small_all = small_all.reshape(N_DEV * SMALL_ROWS, CHUNK)
    g_ws, g_bsp, g_sink, g_rel, g_pg, g_pb, g_lng, g_lnb, loss_all = _unpack_small(small_all, small_shapes)
    g_ws = g_ws.reshape(w_spatial.shape)
    g_bsp = g_bsp.reshape(b_spatial.shape)
    g_pg = g_pg.reshape(post_ln_g.shape)
    g_pb = g_pb.reshape(post_ln_b.shape)
    g_lng = lax.dynamic_slice(g_lng.reshape(1, A_WIDTH), (0, me * sgu_ln_g.shape[1]), sgu_ln_g.shape)
    g_lnb = lax.dynamic_slice(g_lnb.reshape(1, A_WIDTH), (0, me * sgu_ln_b.shape[1]), sgu_ln_b.shape)
    small_params = [(sgu_ln_g, g_lng, m_sgu_ln_g, v_sgu_ln_g), (sgu_ln_b, g_lnb, m_sgu_ln_b, v_sgu_ln_b),
                    (w_spatial, g_ws, m_w_spatial, v_w_spatial), (b_spatial, g_bsp, m_b_spatial, v_b_spatial),
                    (attn_sinks, g_sink, m_attn_sinks, v_attn_sinks), (rel_t, g_rel, m_rel_bias.T, v_rel_bias.T),
                    (post_ln_g, g_pg, m_post_ln_g, v_post_ln_g), (post_ln_b, g_pb, m_post_ln_b, v_post_ln_b)]
    flat2 = [tuple(a.reshape(-1, a.shape[-1]) for a in p) for p in small_params]
    res_in, _ = _adamw_shards(place, items[:1], 4)
    res_rest, small_res = _adamw_shards(place, items[1:], 2, flat2)
    big_out = [[r.reshape(big[i][0].shape) for r in res] for i, res in enumerate(res_in + res_rest)]
    small_out = [tuple(r.reshape(p[0].shape) for r in res) for res, p in zip(small_res, small_params)]

    order = [("big", 0), ("small", 0), ("small", 1), ("small", 2), ("small", 3), ("big", 1), ("big", 2), ("big", 3),
             ("small", 4), ("small", 5), ("big", 4), ("small", 6), ("small", 7)]
    grads, deltas, new_m, new_v = [], [], [], []
    for kind, i in order:
        if kind == "big":
            g, d, m, v = big_out[i]
        else:
            g = small_params[i][1]
            d, m, v = small_out[i]
            if small_params[i][0] is rel_t:
                g, d, m, v = g.T, d.T, m.T, v.T
        grads.append(g)
        deltas.append(d)
        new_m.append(m)
        new_v.append(v)
    return (loss_all[0, 0], dx.reshape(x.shape), *grads, *deltas, *new_m, *new_v)
```

```python
import functools

import jax
import jax.numpy as jnp
import numpy as np
from jax import lax
from jax.experimental import pallas as pl
from jax.experimental.pallas import tpu as pltpu

F32 = jnp.float32
BF16 = jnp.bfloat16

D_MODEL = 1024
A_WIDTH = 2048
A_GROUPS = 8
A_GROUP_DIM = A_WIDTH // A_GROUPS
CHUNK = 128
HEAD_DIM = 64
N_Q_HEADS = 16
N_KV_HEADS = 2
PAIRS_PER_KV = N_Q_HEADS // N_KV_HEADS // 2
B_WIDTH = N_Q_HEADS * HEAD_DIM
KV_WIDTH = N_KV_HEADS * HEAD_DIM
REL_BUCKETS = 32
REL_MAX_DIST = 128
DEPTH = 2
ALPHA = (2.0 * DEPTH) ** 0.25
LN_EPS = 1e-5
NEG_INF = -1e30
SCALE = HEAD_DIM ** -0.5

ADAM_LR = 0.001
ADAM_B1 = 0.9
ADAM_B2 = 0.999
ADAM_EPS = 1e-08
ADAM_WD = 0.01
ADAM_STEP = 10

N_DEV = 8
MESH = pl.DeviceIdType.MESH
TOKENS_IN_PROJ = 1024
TOKENS_WGRAD = 4096
TOKENS_A = 256
TOKENS_A_FWD = 256
P_RING = 3
TOKENS_B = 512
PART_ROWS_B = 256
TOKENS_B_BWD = 256
TOKENS_DX = 512
VMEM_LIMIT_BYTES = 60 * 1024 * 1024
SMALL_ROWS = 144

ANY = pl.BlockSpec(memory_space=pl.ANY)
VMEM = pl.BlockSpec(memory_space=pltpu.VMEM)
SMEM = pl.BlockSpec(memory_space=pltpu.SMEM)


def _nn(a, b):
    return jnp.dot(a, b, preferred_element_type=F32)


def _nt(a, b):
    return lax.dot_general(a, b, (((1,), (1,)), ((), ())), preferred_element_type=F32)


def _tn(a, b):
    return lax.dot_general(a, b, (((0,), (0,)), ((), ())), preferred_element_type=F32)


def _resident(shape):
    nd = len(shape)
    return pl.BlockSpec(shape, lambda i: (0,) * nd, pipeline_mode=pl.Buffered(1))


def _rows(tm, width, reverse_of=None):
    if reverse_of is None:
        return pl.BlockSpec((tm, width), lambda i: (i, 0))
    return pl.BlockSpec((tm, width), lambda i: (reverse_of - 1 - i, 0))


def _seq_params():
    return pltpu.CompilerParams(dimension_semantics=("arbitrary",), vmem_limit_bytes=VMEM_LIMIT_BYTES)


def _in_turns(first, second):
    merged = [(i / max(len(first), 1), 0, f) for i, f in enumerate(first)]
    merged += [((i + 0.5) / max(len(second), 1), 1, f) for i, f in enumerate(second)]
    return [f for _, _, f in sorted(merged, key=lambda t: t[:2])]


def _ln_normalize(r):
    mu = jnp.mean(r, axis=-1, keepdims=True)
    xc = r - mu
    var = jnp.mean(xc * xc, axis=-1, keepdims=True)
    rstd = lax.rsqrt(var + LN_EPS)
    return xc * rstd, rstd


def _ln_backward(dxhat, xhat, rstd):
    m1 = jnp.mean(dxhat, axis=-1, keepdims=True)
    m2 = jnp.mean(dxhat * xhat, axis=-1, keepdims=True)
    return rstd * (dxhat - m1 - xhat * m2)


def _silu_and_grad(z):
    sig = jax.nn.sigmoid(z)
    silu = z * sig
    return silu, sig + silu * (1.0 - sig)


def _band_tables():
    t = np.arange(CHUNK, dtype=np.int32)[:, None]
    j = np.arange(2 * CHUNK, dtype=np.int32)[None, :]
    d = t + CHUNK - j
    in_window = (d >= 0) & (d < CHUNK)
    dc = np.clip(d, 0, REL_MAX_DIST - 1)
    max_exact = REL_BUCKETS // 2
    df = np.maximum(dc, 1).astype(np.float32)
    large = max_exact + (np.log(df / np.float32(max_exact)) / np.float32(np.log(REL_MAX_DIST / max_exact))
                         * np.float32(REL_BUCKETS - max_exact)).astype(np.int32)
    large = np.minimum(large, REL_BUCKETS - 1)
    bucket = np.where(dc < max_exact, dc, large)
    return np.where(in_window, bucket, -1).astype(np.int32)


def _use_order():
    x, y, c = lax.axis_index("x"), lax.axis_index("y"), lax.axis_index("c")
    blocks = [(x, y, c), (x, y, 1 - c)]
    for px, py in [(1 - x, y), (x, 1 - y), (1 - x, 1 - y)]:
        blocks += [(px, py, c), (px, py, 1 - c)]
    return blocks


def _gather_in_proj(order, x, w_in_a, small, ln_gb):
    seq = x.shape[0]
    tm = min(TOKENS_IN_PROJ, seq)
    n_blocks = seq // tm
    shard_cols = w_in_a.shape[1]
    n = len(small) + 2

    def body(order_ref, x_ref, *refs):
        ins, refs = refs[:n], refs[n:]
        p_ref, outs, xb_out, refs = refs[0], refs[1:n + 1], refs[n + 1], refs[n + 2:]
        wa_vmem, xb_vmem, stage = refs[0], refs[1], refs[2:n]
        send_sems, recv_sems, local_sems, out_sems, xb_sem = refs[n:]
        xb_copy = pltpu.make_async_copy(xb_vmem, xb_out, xb_sem)
        phase, blk = pl.program_id(0), pl.program_id(1)
        x_, y_, c_ = lax.axis_index("x"), lax.axis_index("y"), lax.axis_index("c")
        me, sibling = (x_, y_, c_), (x_, y_, 1 - c_)
        chips = [(1 - x_, y_), (x_, 1 - y_), (1 - x_, 1 - y_)]
        use = _use_order()

        def index(px, py, pc):
            return 4 * px + 2 * py + pc

        def slot(a, block):
            return (wa_vmem if a == 0 else outs[a]).at[index(*block)]

        def own(a):
            return slot(0, me) if a == 0 else (ins[a] if a == n - 1 else stage[a - 1])

        def copy(a, k, block, to, src=None):
            return pltpu.make_async_remote_copy(
                src_ref=slot(a, block) if src is None else src, dst_ref=slot(a, block),
                send_sem=send_sems.at[a, k], recv_sem=recv_sems.at[a, k], device_id=to, device_id_type=MESH)

        def keep(a):
            return pltpu.make_async_copy(own(a), slot(a, me), local_sems.at[a])

        def write_out(k):
            return pltpu.make_async_copy(slot(0, use[k]), outs[0].at[index(*use[k])], out_sems.at[k])

        def pass_on(a, j):
            copy(a, 1 + j, (*chips[j], c_), me).wait_recv()
            copy(a, 4 + j, (*chips[j], c_), sibling).start()

        def send(a, j):
            copy(a, 1 + j, me, (*chips[j], c_), own(a)).start()

        @pl.when((phase == 0) & (blk == 0))
        def _():
            wa_vmem[index(*me)] = ins[0][...].astype(BF16)
            for a in range(1, n - 1):
                stage[a - 1][...] = ins[a][...].astype(BF16)
            send(0, 0)
            send(0, 1)
            for a in range(n):
                copy(a, 0, me, sibling, own(a)).start()
                if a:
                    keep(a).start()
            write_out(0).start()
            copy(0, 0, sibling, me).wait_recv()
            write_out(1).start()

        @pl.when((phase == 1) & (blk == 0))
        def _():
            pass_on(0, 0)
            write_out(2).start()
            xb_copy.start()
            send(0, 2)
            for a in range(1, n):
                for j in range(3):
                    send(a, j)
            pass_on(0, 1)
            write_out(4).start()
            copy(0, 4, (*chips[0], 1 - c_), me).wait_recv()
            write_out(3).start()

        @pl.when((phase == 2) & (blk == 0))
        def _():
            copy(0, 5, (*chips[1], 1 - c_), me).wait_recv()
            write_out(5).start()

        @pl.when((phase == 2) & (blk == n_blocks // 2))
        def _():
            pass_on(0, 2)
            write_out(6).start()

        @pl.when((phase == 3) & (blk == 0))
        def _():
            copy(0, 6, (*chips[2], 1 - c_), me).wait_recv()
            write_out(7).start()
            for a in range(1, n):
                pass_on(a, 0)
                pass_on(a, 1)

        rows = pl.ds(pl.multiple_of(blk * tm, tm), tm)

        @pl.when(phase == 0)
        def _():
            xb_vmem[rows, :] = x_ref[...].astype(BF16)

        chip = order_ref[phase]
        xb = xb_vmem[rows, :]
        p_ref[:, :shard_cols] = _nn(xb, wa_vmem[2 * chip])
        p_ref[:, shard_cols:] = _nn(xb, wa_vmem[2 * chip + 1])

        @pl.when((phase == N_DEV // 2 - 1) & (blk == n_blocks - 1))
        def _():
            for a in range(1, n):
                pass_on(a, 2)
            for a in range(1, n):
                copy(a, 0, sibling, me).wait_recv()
                for j, far in enumerate(chips):
                    copy(a, 4 + j, (*far, 1 - c_), me).wait_recv()
            for a in range(n):
                for k in range(7):
                    copy(a, k, me, me).wait_send()
                if a:
                    keep(a).wait()
            for k in range(N_DEV):
                write_out(k).wait()
            xb_copy.wait()

    gathered = [jax.ShapeDtypeStruct((N_DEV,) + w_in_a.shape, BF16)]
    gathered += [jax.ShapeDtypeStruct((N_DEV,) + s.shape, BF16) for s in small]
    gathered.append(jax.ShapeDtypeStruct((N_DEV,) + ln_gb.shape, F32))
    return pl.pallas_call(
        body, name="gather_in_proj",
        out_shape=[jax.ShapeDtypeStruct((seq, N_DEV * shard_cols), F32)] + gathered
        + [jax.ShapeDtypeStruct((seq, D_MODEL), BF16)],
        grid_spec=pltpu.PrefetchScalarGridSpec(
            num_scalar_prefetch=1, grid=(N_DEV // 2, n_blocks),
            in_specs=[pl.BlockSpec((tm, D_MODEL), lambda k, i, order: (jnp.where(k == 0, i, n_blocks - 1), 0))]
            + [VMEM] * n,
            out_specs=[pl.BlockSpec((tm, 2 * shard_cols), lambda k, i, order: (i, order[k]))] + [ANY] * (n + 1),
            scratch_shapes=[pltpu.VMEM((N_DEV,) + w_in_a.shape, BF16), pltpu.VMEM((seq, D_MODEL), BF16)]
            + [pltpu.VMEM(s.shape, BF16) for s in small] + [
                pltpu.SemaphoreType.DMA((n, 7)), pltpu.SemaphoreType.DMA((n, 7)), pltpu.SemaphoreType.DMA((n,)),
                pltpu.SemaphoreType.DMA((N_DEV,)), pltpu.SemaphoreType.DMA(())]),
        compiler_params=pltpu.CompilerParams(dimension_semantics=("arbitrary", "arbitrary"),
                                             vmem_limit_bytes=VMEM_LIMIT_BYTES),
    )(order, x, w_in_a, *small, ln_gb)


def _bias_table(rel_bias_t, bucket):
    def body(rel_ref, bucket_ref, out_ref):
        first = jnp.broadcast_to(bucket_ref[0:1, :], (8, 2 * CHUNK))
        col = lax.broadcasted_iota(jnp.int32, (CHUNK, 2 * CHUNK), 1)
        for h in range(N_Q_HEADS):
            row = jnp.full(first.shape, NEG_INF, F32)
            for b in range(REL_BUCKETS):
                row = jnp.where(first == b, rel_ref[h, b], row)
            tab = pltpu.roll(jnp.broadcast_to(row[0:1, :], (CHUNK, 2 * CHUNK)), 0, 1, stride=1, stride_axis=0)
            pair = h // 2
            rows = slice((pair % PAIRS_PER_KV) * CHUNK, (pair % PAIRS_PER_KV + 1) * CHUNK)
            half = slice((h % 2) * 2 * CHUNK, (h % 2 + 1) * 2 * CHUNK)
            out_ref[1, pair // PAIRS_PER_KV, rows, half] = tab
            out_ref[0, pair // PAIRS_PER_KV, rows, half] = jnp.where(col < CHUNK, NEG_INF, tab)

    return pl.pallas_call(
        body, name="bias_table",
        out_shape=jax.ShapeDtypeStruct((2, N_KV_HEADS, PAIRS_PER_KV * CHUNK, 4 * CHUNK), F32),
        in_specs=[SMEM, VMEM], out_specs=VMEM,
    )(rel_bias_t, bucket)


def _rel_bias_grad(dlogits, bucket):
    def body(dl_ref, bucket_ref, out_ref):
        i = lax.broadcasted_iota(jnp.int32, (CHUNK, CHUNK), 0)
        k = lax.broadcasted_iota(jnp.int32, (CHUNK, CHUNK), 1)
        reverse = jnp.where(i + k == CHUNK - 1, 1.0, 0.0)
        sums = []
        for h in range(N_Q_HEADS):
            pair = h // 2
            dl = dl_ref[pair // PAIRS_PER_KV, (pair % PAIRS_PER_KV) * CHUNK:(pair % PAIRS_PER_KV + 1) * CHUNK,
                        (h % 2) * 2 * CHUNK:(h % 2 + 1) * 2 * CHUNK]
            backwards = jnp.dot(reverse, dl, precision=lax.Precision.HIGHEST, preferred_element_type=F32)
            turned = pltpu.roll(backwards, CHUNK + 1, 1, stride=1, stride_axis=0)
            sums.append(jnp.sum(turned, axis=0, keepdims=True))
        by_offset = jnp.concatenate(sums, axis=0)
        buckets = lax.broadcasted_iota(jnp.int32, (REL_BUCKETS, 2 * CHUNK), 0)
        one_hot = jnp.where(jnp.broadcast_to(bucket_ref[0:1, :], buckets.shape) == buckets, 1.0, 0.0)
        out_ref[...] = lax.dot_general(by_offset, one_hot, (((1,), (1,)), ((), ())),
                                       precision=lax.Precision.HIGHEST, preferred_element_type=F32)

    return pl.pallas_call(
        body, name="rel_bias_grad", out_shape=jax.ShapeDtypeStruct((N_Q_HEADS, REL_BUCKETS), F32),
        in_specs=[VMEM, VMEM], out_specs=VMEM,
    )(dlogits, bucket)


def _masked_spatial(ws_ref, g):
    t = lax.broadcasted_iota(jnp.int32, (CHUNK, CHUNK), 0)
    s = lax.broadcasted_iota(jnp.int32, (CHUNK, CHUNK), 1)
    return jnp.where(s <= t, ws_ref[g], 0.0).astype(BF16)


def _spatial_gate(ws_ref, bsp_t_ref, vn, s_ref, n_chunks):
    for g in range(A_GROUPS):
        w = _masked_spatial(ws_ref, g)
        cols = slice(g * A_GROUP_DIM, (g + 1) * A_GROUP_DIM)
        for ci in range(n_chunks):
            rows = slice(ci * CHUNK, (ci + 1) * CHUNK)
            s_ref[rows, cols] = _nn(w, vn[rows, cols]) + bsp_t_ref[:, g:g + 1]


def _fwd_a(x, p, ln_gb, w_spatial, bsp_t, woa, later):
    seq = x.shape[0]
    tm = min(TOKENS_A_FWD, seq)
    n_chunks = tm // CHUNK
    n_blocks = seq // tm
    n = len(later)

    def body(x_ref, p_hbm, ln_ref, ws_ref, bsp_ref, woa_ref, *refs):
        shards, refs = refs[:n], refs[n:]
        r0_ref, gathered, refs = refs[0], refs[1:1 + n], refs[1 + n:]
        s_ref, p_ring, stage, (send_sems, recv_sems, local_sems, ring_sems) = refs[0], refs[1], refs[2:2 + n], refs[2 + n:]
        step = pl.program_id(0)

        def fetch(block):
            slot = block % P_RING
            start = block * tm if isinstance(block, int) else pl.multiple_of(block * tm, tm)
            return pltpu.make_async_copy(p_hbm.at[pl.ds(start, tm), :], p_ring.at[slot], ring_sems.at[slot])

        @pl.when(step == 0)
        def _():
            for ahead in range(min(P_RING - 1, n_blocks)):
                fetch(ahead).start()

        @pl.when(step + P_RING - 1 < n_blocks)
        def _():
            fetch(step + P_RING - 1).start()

        fetch(step).wait()
        p_ref = p_ring.at[step % P_RING]

        def copies(a):
            x_, y_, c_ = lax.axis_index("x"), lax.axis_index("y"), lax.axis_index("c")
            mine = 4 * x_ + 2 * y_ + c_
            out = [pltpu.make_async_copy(stage[a], gathered[a].at[mine], local_sems.at[a])]
            for r in range(1, N_DEV):
                to = (1 - x_ if r & 4 else x_, 1 - y_ if r & 2 else y_, 1 - c_ if r & 1 else c_)
                out.append(pltpu.make_async_remote_copy(
                    src_ref=stage[a], dst_ref=gathered[a].at[mine], send_sem=send_sems.at[a, r - 1],
                    recv_sem=recv_sems.at[a, r - 1], device_id=to, device_id_type=MESH))
            return out

        @pl.when(step == 0)
        def _():
            for a in range(n):
                stage[a][...] = shards[a][...].astype(BF16)
                for cp in copies(a):
                    cp.start()

        xv = x_ref[...]
        vhat, _ = _ln_normalize(p_ref[:, A_WIDTH:2 * A_WIDTH])
        vn = (vhat * ln_ref[0:1, :] + ln_ref[1:2, :]).astype(BF16)
        _spatial_gate(ws_ref, bsp_ref, vn, s_ref, n_chunks)
        z = p_ref[:, 2 * A_WIDTH:]
        y = p_ref[:, :A_WIDTH] * s_ref[...] * (z * jax.nn.sigmoid(z))
        r0_ref[...] = ALPHA * xv + _nn(y.astype(BF16), woa_ref[...])

        @pl.when(step == n_blocks - 1)
        def _():
            for a in range(n):
                for cp in copies(a):
                    cp.wait()

    return pl.pallas_call(
        body, name="fwd_a", grid=(n_blocks,),
        out_shape=[jax.ShapeDtypeStruct((seq, D_MODEL), F32)]
        + [jax.ShapeDtypeStruct((N_DEV,) + s.shape, BF16) for s in later],
        in_specs=[_rows(tm, D_MODEL), ANY, _resident(ln_gb.shape), _resident(w_spatial.shape),
                  _resident(bsp_t.shape), _resident(woa.shape)] + [VMEM] * n,
        out_specs=[_rows(tm, D_MODEL)] + [ANY] * n,
        scratch_shapes=[pltpu.VMEM((tm, A_WIDTH), F32), pltpu.VMEM((P_RING, tm, 3 * A_WIDTH), F32)]
        + [pltpu.VMEM(s.shape, BF16) for s in later] + [
            pltpu.SemaphoreType.DMA((n, N_DEV - 1)), pltpu.SemaphoreType.DMA((n, N_DEV - 1)),
            pltpu.SemaphoreType.DMA((n,)), pltpu.SemaphoreType.DMA((P_RING,))],
        compiler_params=_seq_params(),
    )(x, p, ln_gb, w_spatial, bsp_t, woa, *later)


def _expand_heads(kv):
    lo = lax.broadcasted_iota(jnp.int32, kv.shape, 1) < HEAD_DIM
    swapped = pltpu.roll(kv, HEAD_DIM, 1)
    zero = jnp.zeros_like(kv)
    return (jnp.where(lo, kv, zero).astype(BF16), jnp.where(lo, zero, swapped).astype(BF16),
            jnp.where(lo, swapped, zero).astype(BF16), jnp.where(lo, zero, kv).astype(BF16))


def _fold_heads(acc, kvh):
    top, bot = acc[:2 * CHUNK], acc[2 * CHUNK:]
    lo = lax.broadcasted_iota(jnp.int32, top.shape, 1) < HEAD_DIM
    t = jnp.where(lo, top, 0.0)
    b = jnp.where(lo, 0.0, bot)
    if kvh == 0:
        return t + pltpu.roll(b, HEAD_DIM, 1)
    return pltpu.roll(t, HEAD_DIM, 1) + b


STACK = PAIRS_PER_KV * CHUNK


def _pair_cols(kvh, pp):
    pair = kvh * PAIRS_PER_KV + pp
    return slice(pair * CHUNK, (pair + 1) * CHUNK)


def _stack_pairs(ref_or_val, rows, kvh):
    return jnp.concatenate([ref_or_val[rows, _pair_cols(kvh, pp)] for pp in range(PAIRS_PER_KV)], axis=0)


def _sink_columns(sink_ref):
    blk = lax.broadcasted_iota(jnp.int32, (STACK, 1), 0) // CHUNK
    out = []
    for kvh in range(N_KV_HEADS):
        halves = []
        for half in range(2):
            col = jnp.zeros((STACK, 1), F32)
            for pp in range(PAIRS_PER_KV):
                col = jnp.where(blk == pp, sink_ref[0, 2 * (kvh * PAIRS_PER_KV + pp) + half], col)
            halves.append(col)
        out.append(halves)
    return out


def _softmax_stack(logits, sink_cols):
    ps, sinks = [], []
    for half in range(2):
        lg = logits[:, half * 2 * CHUNK:(half + 1) * 2 * CHUNK]
        sk = sink_cols[half]
        m = jnp.maximum(jnp.max(lg, axis=-1, keepdims=True), sk)
        e = jnp.exp(lg - m)
        es = jnp.exp(sk - m)
        inv = 1.0 / (jnp.sum(e, axis=-1, keepdims=True) + es)
        ps.append(e * inv)
        sinks.append(es * inv)
    return jnp.concatenate(ps, axis=1), sinks


def _band_operands(before_k, this_k, before_v, this_v, kvh):
    kblk = jnp.concatenate([before_k[2 * kvh], this_k[2 * kvh], before_k[2 * kvh + 1], this_k[2 * kvh + 1]], axis=0)
    vblk = jnp.concatenate([before_v[2 * kvh], this_v[2 * kvh], before_v[2 * kvh + 1], this_v[2 * kvh + 1]], axis=0)
    return kblk, vblk


def _fwd_b(r0, target, wb, wkv, wob, post_g, post_b, sinks, bias):
    seq = r0.shape[0]
    tm = min(TOKENS_B, seq)
    n_chunks = tm // CHUNK
    part_rows = min(PART_ROWS_B, tm)
    n_parts = tm // part_rows
    part_chunks = part_rows // CHUNK
    shard_cols = wb.shape[2]

    def body(r0_ref, tgt_ref, wb_ref, wkv_ref, wob_ref, pg_ref, pb_ref, sink_ref, bias_ref,
             q_ref, z_ref, kv_ref, dr1_ref, loss_ref, dg_ref, db_ref, ek_last, ev_last, *scratch):
        q_s, z_s, h1_s, o_s, ek_s, ev_s = (scratch[i * n_parts:(i + 1) * n_parts] for i in range(6))
        step = pl.program_id(0)

        @pl.when(step == 0)
        def _():
            ek_last[...] = jnp.zeros_like(ek_last)
            ev_last[...] = jnp.zeros_like(ev_last)
            loss_ref[...] = jnp.zeros_like(loss_ref)
            dg_ref[...] = jnp.zeros_like(dg_ref)
            db_ref[...] = jnp.zeros_like(db_ref)

        sink_cols = _sink_columns(sink_ref)
        g1 = pg_ref[1:2, :]

        def project_steps(part):
            rows = slice(part * part_rows, (part + 1) * part_rows)
            live = {}

            def normalize():
                xhat0, _ = _ln_normalize(r0_ref[rows, :])
                h1 = xhat0 * pg_ref[0:1, :] + pb_ref[0:1, :]
                h1_s[part][...] = h1
                live["h1b"] = h1.astype(BF16)

            def shard(k):
                def run():
                    qz = _nn(live["h1b"], wb_ref[k])
                    lo, hi = k * shard_cols, (k + 1) * shard_cols
                    if hi <= B_WIDTH:
                        qs = (qz * SCALE).astype(BF16)
                        q_ref[rows, lo:hi] = qs
                        q_s[part][:, lo:hi] = qs
                    else:
                        z_ref[rows, lo - B_WIDTH:hi - B_WIDTH] = qz
                        z_s[part][:, lo - B_WIDTH:hi - B_WIDTH] = qz
                return run

            def keys():
                kv = _nn(live["h1b"], wkv_ref[...])
                kv_ref[rows, :] = kv.astype(BF16)
                for cj in range(part_chunks):
                    sub = slice(cj * CHUNK, (cj + 1) * CHUNK)
                    for i, e in enumerate(_expand_heads(kv[sub, :KV_WIDTH])):
                        ek_s[part][cj, i] = e
                    for i, e in enumerate(_expand_heads(kv[sub, KV_WIDTH:])):
                        ev_s[part][cj, i] = e

            return [normalize, keys] + [shard(k) for k in range(N_DEV)]

        def attend_steps(part):
            def stack(cj, kvh):
                def run():
                    sub = slice(cj * CHUNK, (cj + 1) * CHUNK)
                    sel = jnp.where(step * n_chunks + part * part_chunks + cj > 0, 1, 0)
                    if cj:
                        before_k, before_v = ek_s[part].at[cj - 1], ev_s[part].at[cj - 1]
                    elif part:
                        before_k, before_v = ek_s[part - 1].at[part_chunks - 1], ev_s[part - 1].at[part_chunks - 1]
                    else:
                        before_k, before_v = ek_last, ev_last
                    kblk, vblk = _band_operands(before_k, ek_s[part].at[cj], before_v, ev_s[part].at[cj], kvh)
                    logits = _nt(_stack_pairs(q_s[part], sub, kvh), kblk) + bias_ref[sel, kvh]
                    p, _ = _softmax_stack(logits, sink_cols[kvh])
                    o = _nn(p.astype(BF16), vblk)
                    for pp in range(PAIRS_PER_KV):
                        o_s[part][sub, _pair_cols(kvh, pp)] = o[pp * CHUNK:(pp + 1) * CHUNK]
                return run

            def finish():
                rows = slice(part * part_rows, (part + 1) * part_rows)
                z = z_s[part][...]
                y = o_s[part][...] * (z * jax.nn.sigmoid(z))
                r1 = ALPHA * h1_s[part][...] + _nn(y.astype(BF16), wob_ref[...])
                xhat1, rstd1 = _ln_normalize(r1)
                diff = xhat1 * g1 + pb_ref[1:2, :] - tgt_ref[rows, :]
                loss_ref[...] += 0.5 * jnp.sum(jnp.mean(diff * diff, axis=-1, keepdims=True), axis=0, keepdims=True)
                dh2 = diff * (1.0 / D_MODEL)
                dg_ref[...] += jnp.sum(dh2 * xhat1, axis=0, keepdims=True)
                db_ref[...] += jnp.sum(dh2, axis=0, keepdims=True)
                dr1_ref[rows, :] = _ln_backward(dh2 * g1, xhat1, rstd1)

            return [stack(cj, kvh) for cj in range(part_chunks) for kvh in range(N_KV_HEADS)] + [finish]

        for run in project_steps(0):
            run()
        for part in range(n_parts):
            for run in _in_turns(attend_steps(part), project_steps(part + 1) if part + 1 < n_parts else []):
                run()
        ek_last[...] = ek_s[n_parts - 1][part_chunks - 1]
        ev_last[...] = ev_s[n_parts - 1][part_chunks - 1]

    vec = pl.BlockSpec((1, D_MODEL), lambda i: (0, 0))
    return pl.pallas_call(
        body, name="fwd_b", grid=(seq // tm,),
        out_shape=[jax.ShapeDtypeStruct((seq, B_WIDTH), BF16), jax.ShapeDtypeStruct((seq, B_WIDTH), F32),
                   jax.ShapeDtypeStruct((seq, 2 * KV_WIDTH), BF16), jax.ShapeDtypeStruct((seq, D_MODEL), F32),
                   jax.ShapeDtypeStruct((1, 1), F32), jax.ShapeDtypeStruct((1, D_MODEL), F32),
                   jax.ShapeDtypeStruct((1, D_MODEL), F32)],
        in_specs=[_rows(tm, D_MODEL), _rows(tm, D_MODEL), _resident(wb.shape), _resident(wkv.shape),
                  _resident(wob.shape), _resident(post_g.shape), _resident(post_b.shape), SMEM,
                  _resident(bias.shape)],
        out_specs=[_rows(tm, B_WIDTH), _rows(tm, B_WIDTH), _rows(tm, 2 * KV_WIDTH), _rows(tm, D_MODEL),
                   pl.BlockSpec((1, 1), lambda i: (0, 0)), vec, vec],
        scratch_shapes=[pltpu.VMEM((4, CHUNK, CHUNK), BF16), pltpu.VMEM((4, CHUNK, CHUNK), BF16)]
        + [pltpu.VMEM((part_rows, B_WIDTH), BF16)] * n_parts + [pltpu.VMEM((part_rows, B_WIDTH), F32)] * n_parts
        + [pltpu.VMEM((part_rows, D_MODEL), F32)] * n_parts + [pltpu.VMEM((part_rows, B_WIDTH), F32)] * n_parts
        + [pltpu.VMEM((part_chunks, 4, CHUNK, CHUNK), BF16)] * (2 * n_parts),
        compiler_params=_seq_params(),
    )(r0, target, wb, wkv, wob, post_g, post_b, sinks, bias)


def _bwd_b(r0, dr1, q, z, kv, wb, wkv, wob, post_g, post_b, sinks, bias):
    seq = r0.shape[0]
    tm = min(TOKENS_B_BWD, seq)
    n_chunks = tm // CHUNK
    n_blocks = seq // tm
    shard_cols = wb.shape[2]

    def body(r0_ref, dr1_ref, q_ref, z_ref, kv_ref, kvp_ref, wb_ref, wkv_ref, wob_ref, pg_ref, pb_ref, sink_ref,
             bias_ref, dr0_ref, dwb_out, dwkv_out, dwob_out, dsink_ref, dl_out, dpost_ref,
             dwb_acc, dwkv_acc, dwob_acc, dl_acc, dsink_acc, carry_ref, ek_ref, ev_ref, y_ref, dqz_ref, dkv_ref):
        step = pl.program_id(0)
        block = n_blocks - 1 - step

        @pl.when(step == 0)
        def _():
            dpost_ref[...] = jnp.zeros_like(dpost_ref)
            dwb_acc[...] = jnp.zeros_like(dwb_acc)
            dwkv_acc[...] = jnp.zeros_like(dwkv_acc)
            dwob_acc[...] = jnp.zeros_like(dwob_acc)
            dl_acc[...] = jnp.zeros_like(dl_acc)
            carry_ref[...] = jnp.zeros_like(carry_ref)
            dsink_acc[...] = jnp.zeros_like(dsink_acc)

        xhat0, rstd0 = _ln_normalize(r0_ref[...])
        h1b = (xhat0 * pg_ref[0:1, :] + pb_ref[0:1, :]).astype(BF16)
        dr1 = dr1_ref[...]
        dr1b = dr1.astype(BF16)
        dy = _nt(dr1b, wob_ref[...])
        sg, dsg = _silu_and_grad(z_ref[...])
        do_all = dy * sg
        dyz = dy * dsg

        for ci in range(n_chunks + 1):
            src = kvp_ref[...] if ci == 0 else kv_ref[(ci - 1) * CHUNK:ci * CHUNK, :]
            src = src.astype(F32)
            for i, e in enumerate(_expand_heads(src[:, :KV_WIDTH])):
                ek_ref[ci, i] = e
            for i, e in enumerate(_expand_heads(src[:, KV_WIDTH:])):
                ev_ref[ci, i] = e

        sink_cols = _sink_columns(sink_ref)
        for ci in reversed(range(n_chunks)):
            rows = slice(ci * CHUNK, (ci + 1) * CHUNK)
            sel = jnp.where(block * n_chunks + ci > 0, 1, 0)
            dk_band = jnp.zeros((2 * CHUNK, KV_WIDTH), F32)
            dv_band = jnp.zeros((2 * CHUNK, KV_WIDTH), F32)
            for kvh in range(N_KV_HEADS):
                kblk, vblk = _band_operands(ek_ref.at[ci], ek_ref.at[ci + 1], ev_ref.at[ci], ev_ref.at[ci + 1], kvh)
                qs = _stack_pairs(q_ref, rows, kvh)
                p, psink = _softmax_stack(_nt(qs, kblk) + bias_ref[sel, kvh], sink_cols[kvh])
                pb16 = p.astype(BF16)
                o = _nn(pb16, vblk)
                y = (o * _stack_pairs(sg, rows, kvh)).astype(BF16)
                dz = (o * _stack_pairs(dyz, rows, kvh)).astype(BF16)
                dob = _stack_pairs(do_all, rows, kvh).astype(BF16)
                dp = _nt(dob, vblk)
                dlogits = []
                for half in range(2):
                    hs = slice(half * 2 * CHUNK, (half + 1) * 2 * CHUNK)
                    delta = jnp.sum(dp[:, hs] * p[:, hs], axis=-1, keepdims=True)
                    dlogits.append(p[:, hs] * (dp[:, hs] - delta))
                    dsink_acc[2 * kvh + half] += psink[half] * delta
                dlog = jnp.concatenate(dlogits, axis=1)
                dl_acc[kvh] += dlog
                dsc = dlog.astype(BF16)
                dq = (_nn(dsc, kblk) * SCALE).astype(BF16)
                for pp in range(PAIRS_PER_KV):
                    blk = slice(pp * CHUNK, (pp + 1) * CHUNK)
                    cols = _pair_cols(kvh, pp)
                    y_ref[rows, cols] = y[blk]
                    dqz_ref[rows, cols] = dq[blk]
                    dqz_ref[rows, B_WIDTH + cols.start:B_WIDTH + cols.stop] = dz[blk]
                dk_band = dk_band + _fold_heads(_tn(dsc, qs), kvh)
                dv_band = dv_band + _fold_heads(_tn(pb16, dob), kvh)
            band = jnp.concatenate([dk_band, dv_band], axis=1)
            dkv_ref[rows, :] = (band[CHUNK:] + carry_ref[...]).astype(BF16)
            carry_ref[...] = band[:CHUNK]

        dqz = dqz_ref[...]
        dkv = dkv_ref[...]
        dh1 = ALPHA * dr1 + _nt(dkv, wkv_ref[...])
        for k in range(N_DEV):
            part = dqz[:, k * shard_cols:(k + 1) * shard_cols]
            dh1 = dh1 + _nt(part, wb_ref[k])
            dwb_acc[k] += _tn(h1b, part)
        dpost_ref[0:1, :] += jnp.sum(dh1 * xhat0, axis=0, keepdims=True)
        dpost_ref[1:2, :] += jnp.sum(dh1, axis=0, keepdims=True)
        dr0_ref[...] = _ln_backward(dh1 * pg_ref[0:1, :], xhat0, rstd0)
        dwkv_acc[...] += _tn(h1b, dkv)
        dwob_acc[...] += _tn(y_ref[...], dr1b)

        @pl.when(step == n_blocks - 1)
        def _():
            pltpu.sync_copy(dwb_acc, dwb_out)
            pltpu.sync_copy(dwkv_acc, dwkv_out)
            pltpu.sync_copy(dwob_acc, dwob_out)
            pltpu.sync_copy(dl_acc, dl_out)
            lane = lax.broadcasted_iota(jnp.int32, (1, CHUNK), 1)
            dsink = jnp.zeros((1, CHUNK), F32)
            for kvh in range(N_KV_HEADS):
                for half in range(2):
                    for pp in range(PAIRS_PER_KV):
                        head = 2 * (kvh * PAIRS_PER_KV + pp) + half
                        tot = jnp.sum(dsink_acc[2 * kvh + half, pp * CHUNK:(pp + 1) * CHUNK, :], axis=0, keepdims=True)
                        dsink = dsink - jnp.where(lane == head, tot, 0.0)
            dsink_ref[...] = dsink

    rev = functools.partial(_rows, tm, reverse_of=n_blocks)
    prev_chunk = pl.BlockSpec(
        (CHUNK, 2 * KV_WIDTH), lambda i: (jnp.maximum((n_blocks - 1 - i) * n_chunks - 1, 0), 0))
    return pl.pallas_call(
        body, name="bwd_b", grid=(n_blocks,),
        out_shape=[jax.ShapeDtypeStruct((seq, D_MODEL), F32), jax.ShapeDtypeStruct(wb.shape, F32),
                   jax.ShapeDtypeStruct(wkv.shape, F32), jax.ShapeDtypeStruct(wob.shape, F32),
                   jax.ShapeDtypeStruct((1, CHUNK), F32),
                   jax.ShapeDtypeStruct((N_KV_HEADS, STACK, 4 * CHUNK), F32), jax.ShapeDtypeStruct((2, D_MODEL), F32)],
        in_specs=[rev(D_MODEL), rev(D_MODEL), rev(B_WIDTH), rev(B_WIDTH), rev(2 * KV_WIDTH), prev_chunk,
                  _resident(wb.shape), _resident(wkv.shape), _resident(wob.shape), _resident(post_g.shape),
                  _resident(post_b.shape), SMEM, _resident(bias.shape)],
        out_specs=[rev(D_MODEL), ANY, ANY, ANY, pl.BlockSpec((1, CHUNK), lambda i: (0, 0)), ANY,
                   pl.BlockSpec((2, D_MODEL), lambda i: (0, 0))],
        scratch_shapes=[pltpu.VMEM(wb.shape, F32), pltpu.VMEM(wkv.shape, F32), pltpu.VMEM(wob.shape, F32),
                        pltpu.VMEM((N_KV_HEADS, STACK, 4 * CHUNK), F32), pltpu.VMEM((2 * N_KV_HEADS, STACK, 1), F32),
                        pltpu.VMEM((CHUNK, 2 * KV_WIDTH), F32),
                        pltpu.VMEM((n_chunks + 1, 4, CHUNK, CHUNK), BF16),
                        pltpu.VMEM((n_chunks + 1, 4, CHUNK, CHUNK), BF16),
                        pltpu.VMEM((tm, B_WIDTH), BF16), pltpu.VMEM((tm, 2 * B_WIDTH), BF16),
                        pltpu.VMEM((tm, 2 * KV_WIDTH), BF16)],
        compiler_params=_seq_params(),
    )(r0, dr1, q, z, kv, kv, wb, wkv, wob, post_g, post_b, sinks, bias)


def _bwd_a(dr0, p, ln_gb, w_spatial, bsp_t, woa, exchange):
    seq = dr0.shape[0]
    tm = min(TOKENS_A, seq)
    n_chunks = tm // CHUNK
    n_blocks = seq // tm

    def body(dr0_ref, p_ref, ln_ref, ws_ref, bsp_ref, woa_ref, *refs):
        ex_in, refs = refs[:exchange.n], refs[exchange.n:]
        (dp_ref, dwoa_out, dws_ref, dbsp_ref, dln_ref), refs = refs[:5], refs[5:]
        ex_out, refs = refs[:exchange.n], refs[exchange.n:]
        (dwoa_acc, s_ref, ds_ref, dvn_ref), ex_sems = refs[:4], refs[4:]
        step = pl.program_id(0)

        @pl.when(step == 0)
        def _():
            exchange.start(ex_in, ex_out, ex_sems)
            dwoa_acc[...] = jnp.zeros_like(dwoa_acc)
            dws_ref[...] = jnp.zeros_like(dws_ref)
            dbsp_ref[...] = jnp.zeros_like(dbsp_ref)
            dln_ref[...] = jnp.zeros_like(dln_ref)

        dr0b = dr0_ref[...].astype(BF16)

        vhat, rstd_v = _ln_normalize(p_ref[:, A_WIDTH:2 * A_WIDTH])
        ln_g = ln_ref[0:1, :]
        vn = (vhat * ln_g + ln_ref[1:2, :]).astype(BF16)
        _spatial_gate(ws_ref, bsp_ref, vn, s_ref, n_chunks)
        u = p_ref[:, :A_WIDTH]
        s = s_ref[...]
        sg, dsg = _silu_and_grad(p_ref[:, 2 * A_WIDTH:])
        gate = s * sg
        dwoa_acc[...] += _tn((u * gate).astype(BF16), dr0b)
        dy = _nt(dr0b, woa_ref[...])
        dyu = dy * u
        dp_ref[:, :A_WIDTH] = (dy * gate).astype(BF16)
        dp_ref[:, 2 * A_WIDTH:] = (dyu * (s * dsg)).astype(BF16)
        ds_ref[...] = dyu * sg

        lane = lax.broadcasted_iota(jnp.int32, (CHUNK, CHUNK), 1)
        dbsp = jnp.zeros((CHUNK, CHUNK), F32)
        for g in range(A_GROUPS):
            w = _masked_spatial(ws_ref, g)
            cols = slice(g * A_GROUP_DIM, (g + 1) * A_GROUP_DIM)
            dws = jnp.zeros((CHUNK, CHUNK), F32)
            for ci in range(n_chunks):
                rows = slice(ci * CHUNK, (ci + 1) * CHUNK)
                ds = ds_ref[rows, cols]
                dsb = ds.astype(BF16)
                dvn_ref[rows, cols] = _tn(w, dsb)
                dws = dws + _nt(dsb, vn[rows, cols])
                dbsp = dbsp + jnp.where(lane == g, jnp.sum(ds, axis=-1, keepdims=True), 0.0)
            dws_ref[g] += dws
        dbsp_ref[...] += dbsp

        dvn = dvn_ref[...]
        dln_ref[0:1, :] += jnp.sum(dvn * vhat, axis=0, keepdims=True)
        dln_ref[1:2, :] += jnp.sum(dvn, axis=0, keepdims=True)
        dp_ref[:, A_WIDTH:2 * A_WIDTH] = _ln_backward(dvn * ln_g, vhat, rstd_v).astype(BF16)

        @pl.when(step == n_blocks - 1)
        def _():
            t = lax.broadcasted_iota(jnp.int32, (CHUNK, CHUNK), 0)
            for g in range(A_GROUPS):
                dws_ref[g] = jnp.where(lane <= t, dws_ref[g], 0.0)
            pltpu.sync_copy(dwoa_acc, dwoa_out)
            exchange.wait(ex_in, ex_out, ex_sems)

    def fixed(shape):
        nd = len(shape)
        return pl.BlockSpec(shape, lambda i: (0,) * nd)

    return pl.pallas_call(
        body, name="bwd_a", grid=(n_blocks,),
        out_shape=[jax.ShapeDtypeStruct((seq, 3 * A_WIDTH), BF16),
                   jax.ShapeDtypeStruct(woa.shape, F32), jax.ShapeDtypeStruct(w_spatial.shape, F32),
                   jax.ShapeDtypeStruct((CHUNK, CHUNK), F32), jax.ShapeDtypeStruct((2, A_WIDTH), F32)]
        + exchange.out_shape,
        in_specs=[_rows(tm, D_MODEL), _rows(tm, 3 * A_WIDTH), _resident(ln_gb.shape),
                  _resident(w_spatial.shape), _resident(bsp_t.shape), _resident(woa.shape)] + [ANY] * exchange.n,
        out_specs=[_rows(tm, 3 * A_WIDTH), ANY, fixed(w_spatial.shape),
                   fixed((CHUNK, CHUNK)), fixed((2, A_WIDTH))] + [ANY] * exchange.n,
        scratch_shapes=[pltpu.VMEM(woa.shape, F32), pltpu.VMEM((tm, A_WIDTH), F32),
                        pltpu.VMEM((tm, A_WIDTH), F32), pltpu.VMEM((tm, A_WIDTH), F32)] + exchange.scratch_shapes,
        compiler_params=_seq_params(),
    )(dr0, p, ln_gb, w_spatial, bsp_t, woa, *exchange.inputs)


def _input_grad_blocks(seq):
    tm = min(TOKENS_DX, seq // 8)
    return tm, seq // tm


def _input_grad(dp, dr0, wa, first, n_blocks, earlier, exchange):
    seq = dr0.shape[0]
    tm, _ = _input_grad_blocks(seq)
    shard_cols = wa.shape[2]
    n_own = 3 if earlier is None else 4

    def body(dp_ref, dr0_ref, wa_ref, *refs):
        refs = refs[n_own - 3:]
        ex_in, dx_ref, ex_out, ex_sems = (refs[:exchange.n], refs[exchange.n], refs[exchange.n + 1:2 * exchange.n + 1],
                                          refs[2 * exchange.n + 1:])
        step = pl.program_id(0)

        @pl.when(step == 0)
        def _():
            exchange.start(ex_in, ex_out, ex_sems)

        dx = ALPHA * dr0_ref[...]
        for k in range(N_DEV):
            dx = dx + _nt(dp_ref[:, k * shard_cols:(k + 1) * shard_cols], wa_ref[k])
        dx_ref[...] = dx

        @pl.when(step == n_blocks - 1)
        def _():
            exchange.wait(ex_in, ex_out, ex_sems)

    def rows(width):
        return pl.BlockSpec((tm, width), lambda i: (first + i, 0))

    return pl.pallas_call(
        body, name=f"input_grad_{first}", grid=(n_blocks,),
        out_shape=[jax.ShapeDtypeStruct((seq, D_MODEL), F32)] + exchange.out_shape,
        in_specs=[rows(3 * A_WIDTH), rows(D_MODEL), _resident(wa.shape)] + [ANY] * (n_own - 3 + exchange.n),
        out_specs=[rows(D_MODEL)] + [ANY] * exchange.n,
        scratch_shapes=exchange.scratch_shapes,
        input_output_aliases={} if earlier is None else {3: 0},
        compiler_params=_seq_params(),
    )(dp, dr0, wa, *(() if earlier is None else (earlier,)), *exchange.inputs)


def _in_proj_weight_grad(xb, dp, shard_cols, exchange):
    seq = xb.shape[0]
    tk = min(TOKENS_WGRAD, seq)
    n_k = seq // tk

    def body(x_ref, dp_ref, *refs):
        ex_in, out_ref, ex_out, ex_sems = (refs[:exchange.n], refs[exchange.n], refs[exchange.n + 1:2 * exchange.n + 1],
                                           refs[2 * exchange.n + 1:])
        shard, step = pl.program_id(0), pl.program_id(1)

        @pl.when((shard == 0) & (step == 0))
        def _():
            exchange.start(ex_in, ex_out, ex_sems)

        @pl.when(step == 0)
        def _():
            out_ref[...] = jnp.zeros_like(out_ref)

        out_ref[...] += _tn(x_ref[...], dp_ref[...])

        @pl.when((shard == N_DEV - 1) & (step == n_k - 1))
        def _():
            exchange.wait(ex_in, ex_out, ex_sems)

    return pl.pallas_call(
        body, name="in_proj_weight_grad", grid=(N_DEV, n_k),
        out_shape=[jax.ShapeDtypeStruct((N_DEV, D_MODEL, shard_cols), F32)] + exchange.out_shape,
        in_specs=[pl.BlockSpec((tk, D_MODEL), lambda j, k: (k, 0)),
                  pl.BlockSpec((tk, shard_cols), lambda j, k: (k, j))] + [ANY] * exchange.n,
        out_specs=[pl.BlockSpec((None, D_MODEL, shard_cols), lambda j, k: (j, 0, 0))] + [ANY] * exchange.n,
        scratch_shapes=exchange.scratch_shapes,
        compiler_params=pltpu.CompilerParams(dimension_semantics=("arbitrary", "arbitrary"),
                                             vmem_limit_bytes=VMEM_LIMIT_BYTES),
    )(xb, dp, *exchange.inputs)


class _Exchange:
    def __init__(self, pair=(), chip=(), gather=()):
        self.pair, self.chip, self.gather = tuple(pair), tuple(chip), tuple(gather)
        self.inputs = self.pair + self.chip + self.gather
        self.n = len(self.inputs)
        self.out_shape = [jax.ShapeDtypeStruct((4, 1) + g.shape[2:], F32) for g in self.pair]
        self.out_shape += [jax.ShapeDtypeStruct((3,) + s.shape[1:], s.dtype) for s in self.chip]
        self.out_shape += [jax.ShapeDtypeStruct((N_DEV,) + g.shape, g.dtype) for g in self.gather]
        self.scratch_shapes = [pltpu.SemaphoreType.DMA((max(len(self.pair), 1),)) for _ in range(2)]
        self.scratch_shapes += [pltpu.SemaphoreType.DMA((max(len(self.chip), 1), 3)) for _ in range(2)]
        self.scratch_shapes += [pltpu.SemaphoreType.DMA((max(len(self.gather), 1), N_DEV)) for _ in range(2)]

    def _copies(self, ins, outs, sems):
        if not self.n:
            return []
        pair_send, pair_recv, chip_send, chip_recv, gather_send, gather_recv = sems
        x, y, c = lax.axis_index("x"), lax.axis_index("y"), lax.axis_index("c")
        copies = []
        for a in range(len(self.gather)):
            src = ins[len(self.pair) + len(self.chip) + a]
            dst = outs[len(self.pair) + len(self.chip) + a].at[4 * x + 2 * y + c]
            copies.append(pltpu.make_async_copy(src, dst, gather_send.at[a, 0]))
            for r in range(1, N_DEV):
                to = (1 - x if r & 4 else x, 1 - y if r & 2 else y, 1 - c if r & 1 else c)
                copies.append(pltpu.make_async_remote_copy(
                    src_ref=src, dst_ref=dst, send_sem=gather_send.at[a, r], recv_sem=gather_recv.at[a, r],
                    device_id=to, device_id_type=MESH))
        for a in range(len(self.pair)):
            copies.append(pltpu.make_async_remote_copy(
                src_ref=ins[a].at[:, pl.ds(1 - c, 1)], dst_ref=outs[a],
                send_sem=pair_send.at[a], recv_sem=pair_recv.at[a], device_id=(x, y, 1 - c), device_id_type=MESH))
        for a in range(len(self.chip)):
            src, dst = ins[len(self.pair) + a], outs[len(self.pair) + a]
            for k, (px, py) in enumerate([(1 - x, y), (x, 1 - y), (1 - x, 1 - y)]):
                copies.append(pltpu.make_async_remote_copy(
                    src_ref=src.at[2 * px + py], dst_ref=dst.at[k], send_sem=chip_send.at[a, k],
                    recv_sem=chip_recv.at[a, k], device_id=(px, py, c), device_id_type=MESH))
        return copies

    def start(self, ins, outs, sems):
        for cp in self._copies(ins, outs, sems):
            cp.start()

    def wait(self, ins, outs, sems):
        for cp in self._copies(ins, outs, sems):
            cp.wait()


def _pair_sum(place, grads, landeds, dtypes, finish=None):
    n = len(grads)

    def body(place_ref, *refs):
        ins, outs = refs[:len(in_specs)], refs[len(in_specs):]
        for a in range(n):
            outs[a][...] = (ins[2 * a][...] + ins[2 * a + 1][...]).astype(dtypes[a])
        if finish is not None:
            grad_ref, landed_ref, got_ref = ins[2 * n:]

            @pl.when(pl.program_id(0) == 0)
            def _():
                outs[n][...] = (((grad_ref[...] + landed_ref[...]) + got_ref[0]) + got_ref[1]) + got_ref[2]

    in_specs, out_specs, operands = [], [], []
    for g, l in zip(grads, landeds):
        rows, cols = g.shape[2:]
        in_specs += [pl.BlockSpec((None, None, rows, cols), lambda k, pr: (k, pr[1], 0, 0)),
                     pl.BlockSpec((None, None, rows, cols), lambda k, pr: (k, 0, 0, 0))]
        out_specs.append(pl.BlockSpec((None, rows, cols), lambda k, pr: (k, 0, 0)))
        operands += [g, l]
    out_shape = [jax.ShapeDtypeStruct((4,) + g.shape[2:], dt) for g, dt in zip(grads, dtypes)]
    if finish is not None:
        rows, cols = finish[2].shape[1:]
        in_specs += [pl.BlockSpec((None, None, rows, cols), lambda k, pr: (pr[0], pr[1], 0, 0)),
                     pl.BlockSpec((None, None, rows, cols), lambda k, pr: (pr[0], 0, 0, 0)),
                     pl.BlockSpec((3, rows, cols), lambda k, pr: (0, 0, 0))]
        out_specs.append(pl.BlockSpec((rows, cols), lambda k, pr: (0, 0)))
        out_shape.append(jax.ShapeDtypeStruct((rows, cols), F32))
        operands += list(finish)
    return pl.pallas_call(
        body, name="pair_sum", out_shape=out_shape,
        grid_spec=pltpu.PrefetchScalarGridSpec(num_scalar_prefetch=1, grid=(4,), in_specs=in_specs,
                                               out_specs=out_specs),
        compiler_params=pltpu.CompilerParams(dimension_semantics=("arbitrary",), vmem_limit_bytes=VMEM_LIMIT_BYTES),
    )(place, *operands)


def _adamw_math(w, g, m, v):
    m = ADAM_B1 * m + (1.0 - ADAM_B1) * g
    v = ADAM_B2 * v + (1.0 - ADAM_B2) * (g * g)
    m_hat = m / (1.0 - ADAM_B1 ** ADAM_STEP)
    v_hat = v / (1.0 - ADAM_B2 ** ADAM_STEP)
    delta = -ADAM_LR * (m_hat / (jnp.sqrt(v_hat) + ADAM_EPS) + ADAM_WD * w)
    return delta, m, v


def _adamw_shards(place, items, steps, small=(), packed=None):
    n, n_small = len(items), len(small)
    n_in = 6 * n + (1 + 3 * n_small if n_small else 0)

    def body(place_ref, *refs):
        ins, small_ins = refs[:6 * n], refs[6 * n + 1:n_in]
        outs, small_outs = refs[n_in:n_in + 4 * n], refs[n_in + 4 * n:]
        for i in range(n):
            a_ref, b_ref, got_ref, w_ref, m_ref, v_ref = ins[6 * i:6 * i + 6]
            g = a_ref[...] + b_ref[...]
            for k in range(3):
                g = g + got_ref[k].astype(F32)
            outs[4 * i][...] = g
            outs[4 * i + 1][...], outs[4 * i + 2][...], outs[4 * i + 3][...] = _adamw_math(
                w_ref[...], g, m_ref[...], v_ref[...])

        if n_small:
            packed_ref = refs[6 * n]

            @pl.when(pl.program_id(0) == 0)
            def _():
                me = 2 * place_ref[0] + place_ref[1]
                for i, (w, (at, mine), _, _) in enumerate(small):
                    w_ref, m_ref, v_ref = small_ins[3 * i:3 * i + 3]
                    g_ref, d_ref, new_m_ref, new_v_ref = small_outs[4 * i:4 * i + 4]
                    rows, cols = w.shape
                    if cols <= CHUNK:
                        g_ref[...] = packed_ref[at:at + rows, :cols]
                    else:
                        per_row = cols // CHUNK
                        start = at + me * (rows * per_row) if mine else at
                        for r in range(rows * per_row):
                            lane = (r % per_row) * CHUNK
                            row = r // per_row
                            g_ref[row:row + 1, lane:lane + CHUNK] = packed_ref[pl.ds(start + r, 1), :]
                    d_ref[...], new_m_ref[...], new_v_ref[...] = _adamw_math(
                        w_ref[...], g_ref[...], m_ref[...], v_ref[...])

    def whole(a):
        return pl.BlockSpec(a.shape, lambda i, pr: (0, 0))

    in_specs, out_specs, out_shape, operands = [], [], [], []
    for grad, landed, got, w, m, v in items:
        rows, cols = w.shape
        tr = rows // steps
        assert tr * steps == rows and tr % 16 == 0, (rows, steps)
        spec = pl.BlockSpec((tr, cols), lambda i, pr: (i, 0))
        in_specs += [pl.BlockSpec((None, None, tr, cols), lambda i, pr: (pr[0], pr[1], i, 0)),
                     pl.BlockSpec((None, None, tr, cols), lambda i, pr: (pr[0], 0, i, 0)),
                     pl.BlockSpec((3, tr, cols), lambda i, pr: (0, i, 0)), spec, spec, spec]
        out_specs += [spec] * 4
        out_shape += [jax.ShapeDtypeStruct((rows, cols), F32)] * 4
        operands += [grad, landed, got, w, m, v]
    if n_small:
        in_specs.append(whole(packed))
        operands.append(packed)
    for w, _, m, v in small:
        in_specs += [whole(w)] * 3
        out_specs += [whole(w)] * 4
        out_shape += [jax.ShapeDtypeStruct(w.shape, F32)] * 4
        operands += [w, m, v]
    outs = pl.pallas_call(
        body, name="adamw_shards", out_shape=out_shape,
        grid_spec=pltpu.PrefetchScalarGridSpec(num_scalar_prefetch=1, grid=(steps,), in_specs=in_specs,
                                               out_specs=out_specs),
        compiler_params=pltpu.CompilerParams(dimension_semantics=("arbitrary",), vmem_limit_bytes=VMEM_LIMIT_BYTES),
    )(place, *operands)
    small_outs = outs[4 * n:]
    return ([outs[4 * i:4 * i + 4] for i in range(n)],
            [tuple(small_outs[4 * i:4 * i + 4]) for i in range(n_small)])


def _pack_small(parts):
    rows = []
    for a in parts:
        r, c = a.shape
        rows.append(jnp.pad(a, ((0, (-r) % 8), (0, CHUNK - c))))
    flat = jnp.concatenate(rows, axis=0)
    return jnp.pad(flat, ((0, N_DEV * SMALL_ROWS - flat.shape[0]), (0, 0))).reshape(N_DEV, SMALL_ROWS, CHUNK)


def _small_rows(shapes):
    out, at = [], 0
    for r, _ in shapes:
        out.append(at)
        at += r + (-r) % 8
    return out


def kernel(x, w_in_a, sgu_ln_g, sgu_ln_b, w_spatial, b_spatial, w_out_a, w_kv, w_in_b, attn_sinks, rel_bias, w_out_b, post_ln_g, post_ln_b, loss_target, m_w_in_a, m_sgu_ln_g, m_sgu_ln_b, m_w_spatial, m_b_spatial, m_w_out_a, m_w_kv, m_w_in_b, m_attn_sinks, m_rel_bias, m_w_out_b, m_post_ln_g, m_post_ln_b, v_w_in_a, v_sgu_ln_g, v_sgu_ln_b, v_w_spatial, v_b_spatial, v_w_out_a, v_w_kv, v_w_in_b, v_attn_sinks, v_rel_bias, v_w_out_b, v_post_ln_g, v_post_ln_b):
    seq = x.shape[1]
    bucket =jnp.asarray(_band_tables())

    ln_shard = jnp.concatenate([sgu_ln_g, sgu_ln_b], axis=0)
    x2, tgt = x[0], loss_target[0]
    order = jnp.stack([2 * px + py for px, py, _ in _use_order()[::2]]).astype(jnp.int32)
    p, wa, woa, ln_all, xb = _gather_in_proj(order, x2, w_in_a[0], (w_out_a[0],), ln_shard)
    woa = woa.reshape(A_WIDTH, D_MODEL)
    ln_gb = jnp.transpose(ln_all, (1, 0, 2)).reshape(2, A_WIDTH)
    ws = w_spatial[0]
    bsp_t = jnp.pad(b_spatial[0].T, ((0, 0), (0, CHUNK - A_GROUPS)))
    rel_t = rel_bias.T
    bias = _bias_table(rel_t, bucket)

    r0, wkv, wb, wob = _fwd_a(x2, p, ln_gb, ws, bsp_t, woa, (w_kv, w_in_b[0], w_out_b[0]))
    wkv = wkv.reshape(D_MODEL, 2 * KV_WIDTH)
    wob = wob.reshape(B_WIDTH, D_MODEL)
    q, z, kv, dr1, loss, dg1, db1 = _fwd_b(r0, tgt, wb, wkv, wob, post_ln_g, post_ln_b, attn_sinks, bias)
    dr0, dwb, dwkv, dwob, dsink, dlog, dpost0 = _bwd_b(r0, dr1, q, z, kv, wb, wkv, wob, post_ln_g, post_ln_b,
                                                       attn_sinks, bias)
    drel = _rel_bias_grad(dlog, bucket)

    place = jnp.stack([2 * lax.axis_index("x") + lax.axis_index("y"), lax.axis_index("c")]).astype(jnp.int32)

    def by_device(g):
        return g.reshape((4, 2, g.shape[0] // N_DEV) + g.shape[1:]) if g.ndim == 2 else g.reshape((4, 2) + g.shape[1:])

    dwb, dwkv, dwob = by_device(dwb), by_device(dwkv), by_device(dwob)
    dp, dwoa, dws, dbsp_t, dln, l_wb, l_wkv, l_wob = _bwd_a(dr0, p, ln_gb, ws, bsp_t, woa,
                                                            _Exchange(pair=(dwb, dwkv, dwob)))
    dwoa = by_device(dwoa)
    small_shapes = [(A_GROUPS * CHUNK, CHUNK), (A_GROUPS, CHUNK), (1, N_Q_HEADS), (N_Q_HEADS, REL_BUCKETS),
                    (16, CHUNK), (16, CHUNK), (16, CHUNK), (16, CHUNK), (1, 1)]
    small = by_device(_pack_small([
        dws.reshape(A_GROUPS * CHUNK, CHUNK), dbsp_t[:, :A_GROUPS].T, dsink[:, :N_Q_HEADS], drel,
        jnp.concatenate([dpost0[0:1], dg1], axis=0).reshape(16, CHUNK),
        jnp.concatenate([dpost0[1:2], db1], axis=0).reshape(16, CHUNK),
        dln[0].reshape(16, CHUNK), dln[1].reshape(16, CHUNK), loss]))
    s_wb, s_wkv, s_wob = _pair_sum(place, (dwb, dwkv, dwob), (l_wb, l_wkv, l_wob), (BF16,) * 3)
    dwa, l_woa, l_small, g_wb, g_wkv, g_wob = _in_proj_weight_grad(
        xb, dp, wa.shape[2], _Exchange(pair=(dwoa, small), chip=(s_wb, s_wkv, s_wob)))
    dwa = by_device(dwa)
    s_woa, s_small = _pair_sum(place, (dwoa, small), (l_woa, l_small), (BF16, F32))
    _, blocks = _input_grad_blocks(seq)
    cut = blocks // 4
    dx, l_wa, g_woa, g_small = _input_grad(dp, dr0, wa, 0, cut, None, _Exchange(pair=(dwa,), chip=(s_woa, s_small)))
    s_wa, my_small = _pair_sum(place, (dwa,), (l_wa,), (BF16,), finish=(small, l_small, g_small))
    dx, g_wa, small_all = _input_grad(dp, dr0, wa, cut, blocks - cut, dx,
                                      _Exchange(chip=(s_wa,), gather=(my_small,)))
    full = [dwa, dwoa, dwkv, dwb, dwob]
    landed = [l_wa, l_woa, l_wkv, l_wb, l_wob]
    got = [g_wa, g_woa, g_wkv, g_wb, g_wob]

    big = [(w_in_a, m_w_in_a, v_w_in_a), (w_out_a, m_w_out_a, v_w_out_a), (w_kv, m_w_kv, v_w_kv),
           (w_in_b, m_w_in_b, v_w_in_b), (w_out_b, m_w_out_b, v_w_out_b)]
    items = [(full[i], landed[i], got[i]) + tuple(a.reshape(got[i].shape[1:]) for a in big[i]) for i in range(5)]

    small_all = small_all.reshape(N_DEV * SMALL_ROWS, CHUNK)
    at_ws, at_bsp, at_sink, at_rel, at_pg, at_pb, at_lng, at_lnb, at_loss = _small_rows(small_shapes)
    small_params = [(sgu_ln_g, (at_lng, True), m_sgu_ln_g, v_sgu_ln_g),
                    (sgu_ln_b, (at_lnb, True), m_sgu_ln_b, v_sgu_ln_b),
                    (w_spatial, (at_ws, False), m_w_spatial, v_w_spatial),
                    (b_spatial, (at_bsp, False), m_b_spatial, v_b_spatial),
                    (attn_sinks, (at_sink, False), m_attn_sinks, v_attn_sinks),
                    (rel_t, (at_rel, False), m_rel_bias.T, v_rel_bias.T),
                    (post_ln_g, (at_pg, False), m_post_ln_g, v_post_ln_g),
                    (post_ln_b, (at_pb, False), m_post_ln_b, v_post_ln_b)]
    flat2 = [(w.reshape(-1, w.shape[-1]), at, m.reshape(-1, m.shape[-1]), v.reshape(-1, v.shape[-1]))
             for w, at, m, v in small_params]
    res_in, _ = _adamw_shards(place, items[:1], 4)
    res_rest, small_res = _adamw_shards(place, items[1:], 2, flat2, small_all)
    big_out = [[r.reshape(big[i][0].shape) for r in res] for i, res in enumerate(res_in + res_rest)]
    small_out = [tuple(r.reshape(p[0].shape) for r in res) for res, p in zip(small_res, small_params)]

    order = [("big", 0), ("small", 0), ("small", 1), ("small", 2), ("small", 3), ("big", 1), ("big", 2), ("big", 3),
             ("small", 4), ("small", 5), ("big", 4), ("small", 6), ("small", 7)]
    grads, deltas, new_m, new_v = [], [], [], []
    for kind, i in order:
        if kind == "big":
            g, d, m, v = big_out[i]
        else:
            g, d, m, v = small_out[i]
            if small_params[i][0] is rel_t:
                g, d, m, v = g.T, d.T, m.T, v.T
        grads.append(g)
        deltas.append(d)
        new_m.append(m)
        new_v.append(v)
    return (small_all[at_loss, 0], dx.reshape(x.shape), *grads, *deltas, *new_m, *new_v)
```

```python
import functools

import jax
import jax.numpy as jnp
import numpy as np
from jax import lax
from jax.experimental import pallas as pl
from jax.experimental.pallas import tpu as pltpu

F32 = jnp.float32
BF16 = jnp.bfloat16

D_MODEL = 1024
A_WIDTH = 2048
A_GROUPS = 8
A_GROUP_DIM = A_WIDTH // A_GROUPS
CHUNK = 128
HEAD_DIM = 64
N_Q_HEADS = 16
N_KV_HEADS = 2
PAIRS_PER_KV = N_Q_HEADS // N_KV_HEADS // 2
B_WIDTH = N_Q_HEADS * HEAD_DIM
KV_WIDTH = N_KV_HEADS * HEAD_DIM
REL_BUCKETS = 32
REL_MAX_DIST = 128
DEPTH = 2
ALPHA = (2.0 * DEPTH) ** 0.25
LN_EPS = 1e-5
NEG_INF = -1e30
SCALE = HEAD_DIM ** -0.5

ADAM_LR = 0.001
ADAM_B1 = 0.9
ADAM_B2 = 0.999
ADAM_EPS = 1e-08
ADAM_WD = 0.01
ADAM_STEP = 10

N_DEV = 8
MESH = pl.DeviceIdType.MESH
TOKENS_IN_PROJ = 1024
TOKENS_WGRAD = 4096
TOKENS_A = 256
TOKENS_A_FWD = 256
P_RING = 3
TOKENS_B = 512
PART_ROWS_B = 256
TOKENS_B_BWD = 256
TOKENS_DX = 512
VMEM_LIMIT_BYTES = 60 * 1024 * 1024
SMALL_ROWS = 144

ANY = pl.BlockSpec(memory_space=pl.ANY)
VMEM = pl.BlockSpec(memory_space=pltpu.VMEM)
SMEM = pl.BlockSpec(memory_space=pltpu.SMEM)


def _nn(a, b):
    return jnp.dot(a, b, preferred_element_type=F32)


def _nt(a, b):
    return lax.dot_general(a, b, (((1,), (1,)), ((), ())), preferred_element_type=F32)


def _tn(a, b):
    return lax.dot_general(a, b, (((0,), (0,)), ((), ())), preferred_element_type=F32)


def _resident(shape):
    nd = len(shape)
    return pl.BlockSpec(shape, lambda i: (0,) * nd, pipeline_mode=pl.Buffered(1))


def _rows(tm, width, reverse_of=None):
    if reverse_of is None:
        return pl.BlockSpec((tm, width), lambda i: (i, 0))
    return pl.BlockSpec((tm, width), lambda i: (reverse_of - 1 - i, 0))


def _seq_params():
    return pltpu.CompilerParams(dimension_semantics=("arbitrary",), vmem_limit_bytes=VMEM_LIMIT_BYTES)


def _in_turns(first, second):
    merged = [(i / max(len(first), 1), 0, f) for i, f in enumerate(first)]
    merged += [((i + 0.5) / max(len(second), 1), 1, f) for i, f in enumerate(second)]
    return [f for _, _, f in sorted(merged, key=lambda t: t[:2])]


def _ln_normalize(r):
    mu = jnp.mean(r, axis=-1, keepdims=True)
    xc = r - mu
    var = jnp.mean(xc * xc, axis=-1, keepdims=True)
    rstd = lax.rsqrt(var + LN_EPS)
    return xc * rstd, rstd


def _ln_backward(dxhat, xhat, rstd):
    m1 = jnp.mean(dxhat, axis=-1, keepdims=True)
    m2 = jnp.mean(dxhat * xhat, axis=-1, keepdims=True)
    return rstd * (dxhat - m1 - xhat * m2)


def _silu_and_grad(z):
    sig = jax.nn.sigmoid(z)
    silu = z * sig
    return silu, sig + silu * (1.0 - sig)


def _band_tables():
    t = np.arange(CHUNK, dtype=np.int32)[:, None]
    j = np.arange(2 * CHUNK, dtype=np.int32)[None, :]
    d = t + CHUNK - j
    in_window = (d >= 0) & (d < CHUNK)
    dc = np.clip(d, 0, REL_MAX_DIST - 1)
    max_exact = REL_BUCKETS // 2
    df = np.maximum(dc, 1).astype(np.float32)
    large = max_exact + (np.log(df / np.float32(max_exact)) / np.float32(np.log(REL_MAX_DIST / max_exact))
                         * np.float32(REL_BUCKETS - max_exact)).astype(np.int32)
    large = np.minimum(large, REL_BUCKETS - 1)
    bucket = np.where(dc < max_exact, dc, large)
    return np.where(in_window, bucket, -1).astype(np.int32)


def _use_order():
    x, y, c = lax.axis_index("x"), lax.axis_index("y"), lax.axis_index("c")
    blocks = [(x, y, c), (x, y, 1 - c)]
    for px, py in [(1 - x, y), (x, 1 - y), (1 - x, 1 - y)]:
        blocks += [(px, py, c), (px, py, 1 - c)]
    return blocks


def _gather_in_proj(order, x, w_in_a, small, ln_gb):
    seq = x.shape[0]
    tm = min(TOKENS_IN_PROJ, seq)
    n_blocks = seq // tm
    shard_cols = w_in_a.shape[1]
    n = len(small) + 2

    def body(order_ref, x_ref, *refs):
        ins, refs = refs[:n], refs[n:]
        p_ref, outs, xb_out, refs = refs[0], refs[1:n + 1], refs[n + 1], refs[n + 2:]
        wa_vmem, xb_vmem, stage = refs[0], refs[1], refs[2:n]
        send_sems, recv_sems, local_sems, out_sems, xb_sem = refs[n:]
        xb_copy = pltpu.make_async_copy(xb_vmem, xb_out, xb_sem)
        phase, blk = pl.program_id(0), pl.program_id(1)
        x_, y_, c_ = lax.axis_index("x"), lax.axis_index("y"), lax.axis_index("c")
        me, sibling = (x_, y_, c_), (x_, y_, 1 - c_)
        chips = [(1 - x_, y_), (x_, 1 - y_), (1 - x_, 1 - y_)]
        use = _use_order()

        def index(px, py, pc):
            return 4 * px + 2 * py + pc

        def slot(a, block):
            return (wa_vmem if a == 0 else outs[a]).at[index(*block)]

        def own(a):
            return slot(0, me) if a == 0 else (ins[a] if a == n - 1 else stage[a - 1])

        def copy(a, k, block, to, src=None):
            return pltpu.make_async_remote_copy(
                src_ref=slot(a, block) if src is None else src, dst_ref=slot(a, block),
                send_sem=send_sems.at[a, k], recv_sem=recv_sems.at[a, k], device_id=to, device_id_type=MESH)

        def keep(a):
            return pltpu.make_async_copy(own(a), slot(a, me), local_sems.at[a])

        def write_out(k):
            return pltpu.make_async_copy(slot(0, use[k]), outs[0].at[index(*use[k])], out_sems.at[k])

        def pass_on(a, j):
            copy(a, 1 + j, (*chips[j], c_), me).wait_recv()
            copy(a, 4 + j, (*chips[j], c_), sibling).start()

        def send(a, j):
            copy(a, 1 + j, me, (*chips[j], c_), own(a)).start()

        @pl.when((phase == 0) & (blk == 0))
        def _():
            wa_vmem[index(*me)] = ins[0][...].astype(BF16)
            for a in range(1, n - 1):
                stage[a - 1][...] = ins[a][...].astype(BF16)
            send(0, 0)
            send(0, 1)
            for a in range(n):
                copy(a, 0, me, sibling, own(a)).start()
                if a:
                    keep(a).start()
            write_out(0).start()
            copy(0, 0, sibling, me).wait_recv()
            write_out(1).start()

        @pl.when((phase == 1) & (blk == 0))
        def _():
            pass_on(0, 0)
            write_out(2).start()
            xb_copy.start()
            send(0, 2)
            for a in range(1, n):
                for j in range(3):
                    send(a, j)
            pass_on(0, 1)
            write_out(4).start()
            copy(0, 4, (*chips[0], 1 - c_), me).wait_recv()
            write_out(3).start()

        @pl.when((phase == 2) & (blk == 0))
        def _():
            copy(0, 5, (*chips[1], 1 - c_), me).wait_recv()
            write_out(5).start()

        @pl.when((phase == 2) & (blk == n_blocks // 2))
        def _():
            pass_on(0, 2)
            write_out(6).start()

        @pl.when((phase == 3) & (blk == 0))
        def _():
            copy(0, 6, (*chips[2], 1 - c_), me).wait_recv()
            write_out(7).start()
            for a in range(1, n):
                pass_on(a, 0)
                pass_on(a, 1)

        rows = pl.ds(pl.multiple_of(blk * tm, tm), tm)

        @pl.when(phase == 0)
        def _():
            xb_vmem[rows, :] = x_ref[...].astype(BF16)

        chip = order_ref[phase]
        xb = xb_vmem[rows, :]
        p_ref[:, :shard_cols] = _nn(xb, wa_vmem[2 * chip])
        p_ref[:, shard_cols:] = _nn(xb, wa_vmem[2 * chip + 1])

        @pl.when((phase == N_DEV // 2 - 1) & (blk == n_blocks - 1))
        def _():
            for a in range(1, n):
                pass_on(a, 2)
            for a in range(1, n):
                copy(a, 0, sibling, me).wait_recv()
                for j, far in enumerate(chips):
                    copy(a, 4 + j, (*far, 1 - c_), me).wait_recv()
            for a in range(n):
                for k in range(7):
                    copy(a, k, me, me).wait_send()
                if a:
                    keep(a).wait()
            for k in range(N_DEV):
                write_out(k).wait()
            xb_copy.wait()

    gathered = [jax.ShapeDtypeStruct((N_DEV,) + w_in_a.shape, BF16)]
    gathered += [jax.ShapeDtypeStruct((N_DEV,) + s.shape, BF16) for s in small]
    gathered.append(jax.ShapeDtypeStruct((N_DEV,) + ln_gb.shape, F32))
    return pl.pallas_call(
        body, name="gather_in_proj",
        out_shape=[jax.ShapeDtypeStruct((seq, N_DEV * shard_cols), F32)] + gathered
        + [jax.ShapeDtypeStruct((seq, D_MODEL), BF16)],
        grid_spec=pltpu.PrefetchScalarGridSpec(
            num_scalar_prefetch=1, grid=(N_DEV // 2, n_blocks),
            in_specs=[pl.BlockSpec((tm, D_MODEL), lambda k, i, order: (jnp.where(k == 0, i, n_blocks - 1), 0))]
            + [VMEM] * n,
            out_specs=[pl.BlockSpec((tm, 2 * shard_cols), lambda k, i, order: (i, order[k]))] + [ANY] * (n + 1),
            scratch_shapes=[pltpu.VMEM((N_DEV,) + w_in_a.shape, BF16), pltpu.VMEM((seq, D_MODEL), BF16)]
            + [pltpu.VMEM(s.shape, BF16) for s in small] + [
                pltpu.SemaphoreType.DMA((n, 7)), pltpu.SemaphoreType.DMA((n, 7)), pltpu.SemaphoreType.DMA((n,)),
                pltpu.SemaphoreType.DMA((N_DEV,)), pltpu.SemaphoreType.DMA(())]),
        compiler_params=pltpu.CompilerParams(dimension_semantics=("arbitrary", "arbitrary"),
                                             vmem_limit_bytes=VMEM_LIMIT_BYTES),
    )(order, x, w_in_a, *small, ln_gb)


def _bias_table(rel_bias_t, bucket):
    def body(rel_ref, bucket_ref, out_ref):
        first = jnp.broadcast_to(bucket_ref[0:1, :], (8, 2 * CHUNK))
        col = lax.broadcasted_iota(jnp.int32, (CHUNK, 2 * CHUNK), 1)
        for h in range(N_Q_HEADS):
            row = jnp.full(first.shape, NEG_INF, F32)
            for b in range(REL_BUCKETS):
                row = jnp.where(first == b, rel_ref[h, b], row)
            tab = pltpu.roll(jnp.broadcast_to(row[0:1, :], (CHUNK, 2 * CHUNK)), 0, 1, stride=1, stride_axis=0)
            pair = h // 2
            rows = slice((pair % PAIRS_PER_KV) * CHUNK, (pair % PAIRS_PER_KV + 1) * CHUNK)
            half = slice((h % 2) * 2 * CHUNK, (h % 2 + 1) * 2 * CHUNK)
            out_ref[1, pair // PAIRS_PER_KV, rows, half] = tab
            out_ref[0, pair // PAIRS_PER_KV, rows, half] = jnp.where(col < CHUNK, NEG_INF, tab)

    return pl.pallas_call(
        body, name="bias_table",
        out_shape=jax.ShapeDtypeStruct((2, N_KV_HEADS, PAIRS_PER_KV * CHUNK, 4 * CHUNK), F32),
        in_specs=[SMEM, VMEM], out_specs=VMEM,
    )(rel_bias_t, bucket)


def _rel_bias_grad(dlogits, bucket):
    def body(dl_ref, bucket_ref, out_ref):
        i = lax.broadcasted_iota(jnp.int32, (CHUNK, CHUNK), 0)
        k = lax.broadcasted_iota(jnp.int32, (CHUNK, CHUNK), 1)
        reverse = jnp.where(i + k == CHUNK - 1, 1.0, 0.0)
        sums = []
        for h in range(N_Q_HEADS):
            pair = h // 2
            dl = dl_ref[pair // PAIRS_PER_KV, (pair % PAIRS_PER_KV) * CHUNK:(pair % PAIRS_PER_KV + 1) * CHUNK,
                        (h % 2) * 2 * CHUNK:(h % 2 + 1) * 2 * CHUNK]
            backwards = jnp.dot(reverse, dl, precision=lax.Precision.HIGHEST, preferred_element_type=F32)
            turned = pltpu.roll(backwards, CHUNK + 1, 1, stride=1, stride_axis=0)
            sums.append(jnp.sum(turned, axis=0, keepdims=True))
        by_offset = jnp.concatenate(sums, axis=0)
        buckets = lax.broadcasted_iota(jnp.int32, (REL_BUCKETS, 2 * CHUNK), 0)
        one_hot = jnp.where(jnp.broadcast_to(bucket_ref[0:1, :], buckets.shape) == buckets, 1.0, 0.0)
        out_ref[...] = lax.dot_general(by_offset, one_hot, (((1,), (1,)), ((), ())),
                                       precision=lax.Precision.HIGHEST, preferred_element_type=F32)

    return pl.pallas_call(
        body, name="rel_bias_grad", out_shape=jax.ShapeDtypeStruct((N_Q_HEADS, REL_BUCKETS), F32),
        in_specs=[VMEM, VMEM], out_specs=VMEM,
    )(dlogits, bucket)


def _masked_spatial(ws_ref, g):
    t = lax.broadcasted_iota(jnp.int32, (CHUNK, CHUNK), 0)
    s = lax.broadcasted_iota(jnp.int32, (CHUNK, CHUNK), 1)
    return jnp.where(s <= t, ws_ref[g], 0.0).astype(BF16)


def _spatial_gate(ws_ref, bsp_t_ref, vn, s_ref, n_chunks):
    for g in range(A_GROUPS):
        w = _masked_spatial(ws_ref, g)
        cols = slice(g * A_GROUP_DIM, (g + 1) * A_GROUP_DIM)
        for ci in range(n_chunks):
            rows = slice(ci * CHUNK, (ci + 1) * CHUNK)
            s_ref[rows, cols] = _nn(w, vn[rows, cols]) + bsp_t_ref[:, g:g + 1]


def _fwd_a(x, p, ln_gb, w_spatial, bsp_t, woa, later):
    seq = x.shape[0]
    tm = min(TOKENS_A_FWD, seq)
    n_chunks = tm // CHUNK
    n_blocks = seq // tm
    n = len(later)

    def body(x_ref, p_hbm, ln_ref, ws_ref, bsp_ref, woa_ref, *refs):
        shards, refs = refs[:n], refs[n:]
        r0_ref, gathered, refs = refs[0], refs[1:1 + n], refs[1 + n:]
        s_ref, p_ring, stage, (send_sems, recv_sems, local_sems, ring_sems) = refs[0], refs[1], refs[2:2 + n], refs[2 + n:]
        step = pl.program_id(0)

        def fetch(block):
            slot = block % P_RING
            start = block * tm if isinstance(block, int) else pl.multiple_of(block * tm, tm)
            return pltpu.make_async_copy(p_hbm.at[pl.ds(start, tm), :], p_ring.at[slot], ring_sems.at[slot])

        @pl.when(step == 0)
        def _():
            for ahead in range(min(P_RING - 1, n_blocks)):
                fetch(ahead).start()

        @pl.when(step + P_RING - 1 < n_blocks)
        def _():
            fetch(step + P_RING - 1).start()

        fetch(step).wait()
        p_ref = p_ring.at[step % P_RING]

        def copies(a):
            x_, y_, c_ = lax.axis_index("x"), lax.axis_index("y"), lax.axis_index("c")
            mine = 4 * x_ + 2 * y_ + c_
            out = [pltpu.make_async_copy(stage[a], gathered[a].at[mine], local_sems.at[a])]
            for r in range(1, N_DEV):
                to = (1 - x_ if r & 4 else x_, 1 - y_ if r & 2 else y_, 1 - c_ if r & 1 else c_)
                out.append(pltpu.make_async_remote_copy(
                    src_ref=stage[a], dst_ref=gathered[a].at[mine], send_sem=send_sems.at[a, r - 1],
                    recv_sem=recv_sems.at[a, r - 1], device_id=to, device_id_type=MESH))
            return out

        @pl.when(step == 0)
        def _():
            for a in range(n):
                stage[a][...] = shards[a][...].astype(BF16)
                for cp in copies(a):
                    cp.start()

        xv = x_ref[...]
        vhat, _ = _ln_normalize(p_ref[:, A_WIDTH:2 * A_WIDTH])
        vn = (vhat * ln_ref[0:1, :] + ln_ref[1:2, :]).astype(BF16)
        _spatial_gate(ws_ref, bsp_ref, vn, s_ref, n_chunks)
        z = p_ref[:, 2 * A_WIDTH:]
        y = p_ref[:, :A_WIDTH] * s_ref[...] * (z * jax.nn.sigmoid(z))
        r0_ref[...] = ALPHA * xv + _nn(y.astype(BF16), woa_ref[...])

        @pl.when(step == n_blocks - 1)
        def _():
            for a in range(n):
                for cp in copies(a):
                    cp.wait()

    return pl.pallas_call(
        body, name="fwd_a", grid=(n_blocks,),
        out_shape=[jax.ShapeDtypeStruct((seq, D_MODEL), F32)]
        + [jax.ShapeDtypeStruct((N_DEV,) + s.shape, BF16) for s in later],
        in_specs=[_rows(tm, D_MODEL), ANY, _resident(ln_gb.shape), _resident(w_spatial.shape),
                  _resident(bsp_t.shape), _resident(woa.shape)] + [VMEM] * n,
        out_specs=[_rows(tm, D_MODEL)] + [ANY] * n,
        scratch_shapes=[pltpu.VMEM((tm, A_WIDTH), F32), pltpu.VMEM((P_RING, tm, 3 * A_WIDTH), F32)]
        + [pltpu.VMEM(s.shape, BF16) for s in later] + [
            pltpu.SemaphoreType.DMA((n, N_DEV - 1)), pltpu.SemaphoreType.DMA((n, N_DEV - 1)),
            pltpu.SemaphoreType.DMA((n,)), pltpu.SemaphoreType.DMA((P_RING,))],
        compiler_params=_seq_params(),
    )(x, p, ln_gb, w_spatial, bsp_t, woa, *later)


def _expand_heads(kv):
    lo = lax.broadcasted_iota(jnp.int32, kv.shape, 1) < HEAD_DIM
    swapped = pltpu.roll(kv, HEAD_DIM, 1)
    zero = jnp.zeros_like(kv)
    return (jnp.where(lo, kv, zero).astype(BF16), jnp.where(lo, zero, swapped).astype(BF16),
            jnp.where(lo, swapped, zero).astype(BF16), jnp.where(lo, zero, kv).astype(BF16))


def _fold_heads(acc, kvh):
    top, bot = acc[:2 * CHUNK], acc[2 * CHUNK:]
    lo = lax.broadcasted_iota(jnp.int32, top.shape, 1) < HEAD_DIM
    t = jnp.where(lo, top, 0.0)
    b = jnp.where(lo, 0.0, bot)
    if kvh == 0:
        return t + pltpu.roll(b, HEAD_DIM, 1)
    return pltpu.roll(t, HEAD_DIM, 1) + b


STACK = PAIRS_PER_KV * CHUNK


def _pair_cols(kvh, pp):
    pair = kvh * PAIRS_PER_KV + pp
    return slice(pair * CHUNK, (pair + 1) * CHUNK)


def _stack_pairs(ref_or_val, rows, kvh):
    return jnp.concatenate([ref_or_val[rows, _pair_cols(kvh, pp)] for pp in range(PAIRS_PER_KV)], axis=0)


def _sink_columns(sink_ref):
    blk = lax.broadcasted_iota(jnp.int32, (STACK, 1), 0) // CHUNK
    out = []
    for kvh in range(N_KV_HEADS):
        halves = []
        for half in range(2):
            col = jnp.zeros((STACK, 1), F32)
            for pp in range(PAIRS_PER_KV):
                col = jnp.where(blk == pp, sink_ref[0, 2 * (kvh * PAIRS_PER_KV + pp) + half], col)
            halves.append(col)
        out.append(halves)
    return out


def _softmax_stack(logits, sink_cols):
    ps, sinks = [], []
    for half in range(2):
        lg = logits[:, half * 2 * CHUNK:(half + 1) * 2 * CHUNK]
        sk = sink_cols[half]
        m = jnp.maximum(jnp.max(lg, axis=-1, keepdims=True), sk)
        e = jnp.exp(lg - m)
        es = jnp.exp(sk - m)
        inv = 1.0 / (jnp.sum(e, axis=-1, keepdims=True) + es)
        ps.append(e * inv)
        sinks.append(es * inv)
    return jnp.concatenate(ps, axis=1), sinks


def _band_operands(before_k, this_k, before_v, this_v, kvh):
    kblk = jnp.concatenate([before_k[2 * kvh], this_k[2 * kvh], before_k[2 * kvh + 1], this_k[2 * kvh + 1]], axis=0)
    vblk = jnp.concatenate([before_v[2 * kvh], this_v[2 * kvh], before_v[2 * kvh + 1], this_v[2 * kvh + 1]], axis=0)
    return kblk, vblk


def _fwd_b(r0, target, wb, wkv, wob, post_g, post_b, sinks, bias):
    seq = r0.shape[0]
    tm = min(TOKENS_B, seq)
    n_chunks = tm // CHUNK
    part_rows = min(PART_ROWS_B, tm)
    n_parts = tm // part_rows
    part_chunks = part_rows // CHUNK
    shard_cols = wb.shape[2]

    def body(r0_ref, tgt_ref, wb_ref, wkv_ref, wob_ref, pg_ref, pb_ref, sink_ref, bias_ref,
             q_ref, z_ref, kv_ref, dr1_ref, loss_ref, dg_ref, db_ref, ek_last, ev_last, *scratch):
        q_s, z_s, h1_s, o_s, ek_s, ev_s = (scratch[i * n_parts:(i + 1) * n_parts] for i in range(6))
        step = pl.program_id(0)

        @pl.when(step == 0)
        def _():
            ek_last[...] = jnp.zeros_like(ek_last)
            ev_last[...] = jnp.zeros_like(ev_last)
            loss_ref[...] = jnp.zeros_like(loss_ref)
            dg_ref[...] = jnp.zeros_like(dg_ref)
            db_ref[...] = jnp.zeros_like(db_ref)

        sink_cols = _sink_columns(sink_ref)
        g1 = pg_ref[1:2, :]

        def project_steps(part):
            rows = slice(part * part_rows, (part + 1) * part_rows)
            live = {}

            def normalize():
                xhat0, _ = _ln_normalize(r0_ref[rows, :])
                h1 = xhat0 * pg_ref[0:1, :] + pb_ref[0:1, :]
                h1_s[part][...] = h1
                live["h1b"] = h1.astype(BF16)

            def shard(k):
                def run():
                    qz = _nn(live["h1b"], wb_ref[k])
                    lo, hi = k * shard_cols, (k + 1) * shard_cols
                    if hi <= B_WIDTH:
                        qs = (qz * SCALE).astype(BF16)
                        q_ref[rows, lo:hi] = qs
                        q_s[part][:, lo:hi] = qs
                    else:
                        z_ref[rows, lo - B_WIDTH:hi - B_WIDTH] = qz
                        z_s[part][:, lo - B_WIDTH:hi - B_WIDTH] = qz
                return run

            def keys():
                kv = _nn(live["h1b"], wkv_ref[...])
                kv_ref[rows, :] = kv.astype(BF16)
                for cj in range(part_chunks):
                    sub = slice(cj * CHUNK, (cj + 1) * CHUNK)
                    for i, e in enumerate(_expand_heads(kv[sub, :KV_WIDTH])):
                        ek_s[part][cj, i] = e
                    for i, e in enumerate(_expand_heads(kv[sub, KV_WIDTH:])):
                        ev_s[part][cj, i] = e

            return [normalize, keys] + [shard(k) for k in range(N_DEV)]

        def attend_steps(part):
            def stack(cj, kvh):
                def run():
                    sub = slice(cj * CHUNK, (cj + 1) * CHUNK)
                    sel = jnp.where(step * n_chunks + part * part_chunks + cj > 0, 1, 0)
                    if cj:
                        before_k, before_v = ek_s[part].at[cj - 1], ev_s[part].at[cj - 1]
                    elif part:
                        before_k, before_v = ek_s[part - 1].at[part_chunks - 1], ev_s[part - 1].at[part_chunks - 1]
                    else:
                        before_k, before_v = ek_last, ev_last
                    kblk, vblk = _band_operands(before_k, ek_s[part].at[cj], before_v, ev_s[part].at[cj], kvh)
                    logits = _nt(_stack_pairs(q_s[part], sub, kvh), kblk) + bias_ref[sel, kvh]
                    p, _ = _softmax_stack(logits, sink_cols[kvh])
                    o = _nn(p.astype(BF16), vblk)
                    for pp in range(PAIRS_PER_KV):
                        o_s[part][sub, _pair_cols(kvh, pp)] = o[pp * CHUNK:(pp + 1) * CHUNK]
                return run

            def finish():
                rows = slice(part * part_rows, (part + 1) * part_rows)
                z = z_s[part][...]
                y = o_s[part][...] * (z * jax.nn.sigmoid(z))
                r1 = ALPHA * h1_s[part][...] + _nn(y.astype(BF16), wob_ref[...])
                xhat1, rstd1 = _ln_normalize(r1)
                diff = xhat1 * g1 + pb_ref[1:2, :] - tgt_ref[rows, :]
                loss_ref[...] += 0.5 * jnp.sum(jnp.mean(diff * diff, axis=-1, keepdims=True), axis=0, keepdims=True)
                dh2 = diff * (1.0 / D_MODEL)
                dg_ref[...] += jnp.sum(dh2 * xhat1, axis=0, keepdims=True)
                db_ref[...] += jnp.sum(dh2, axis=0, keepdims=True)
                dr1_ref[rows, :] = _ln_backward(dh2 * g1, xhat1, rstd1)

            return [stack(cj, kvh) for cj in range(part_chunks) for kvh in range(N_KV_HEADS)] + [finish]

        for run in project_steps(0):
            run()
        for part in range(n_parts):
            for run in _in_turns(attend_steps(part), project_steps(part + 1) if part + 1 < n_parts else []):
                run()
        ek_last[...] = ek_s[n_parts - 1][part_chunks - 1]
        ev_last[...] = ev_s[n_parts - 1][part_chunks - 1]

    vec = pl.BlockSpec((1, D_MODEL), lambda i: (0, 0))
    return pl.pallas_call(
        body, name="fwd_b", grid=(seq // tm,),
        out_shape=[jax.ShapeDtypeStruct((seq, B_WIDTH), BF16), jax.ShapeDtypeStruct((seq, B_WIDTH), F32),
                   jax.ShapeDtypeStruct((seq, 2 * KV_WIDTH), BF16), jax.ShapeDtypeStruct((seq, D_MODEL), F32),
                   jax.ShapeDtypeStruct((1, 1), F32), jax.ShapeDtypeStruct((1, D_MODEL), F32),
                   jax.ShapeDtypeStruct((1, D_MODEL), F32)],
        in_specs=[_rows(tm, D_MODEL), _rows(tm, D_MODEL), _resident(wb.shape), _resident(wkv.shape),
                  _resident(wob.shape), _resident(post_g.shape), _resident(post_b.shape), SMEM,
                  _resident(bias.shape)],
        out_specs=[_rows(tm, B_WIDTH), _rows(tm, B_WIDTH), _rows(tm, 2 * KV_WIDTH), _rows(tm, D_MODEL),
                   pl.BlockSpec((1, 1), lambda i: (0, 0)), vec, vec],
        scratch_shapes=[pltpu.VMEM((4, CHUNK, CHUNK), BF16), pltpu.VMEM((4, CHUNK, CHUNK), BF16)]
        + [pltpu.VMEM((part_rows, B_WIDTH), BF16)] * n_parts + [pltpu.VMEM((part_rows, B_WIDTH), F32)] * n_parts
        + [pltpu.VMEM((part_rows, D_MODEL), F32)] * n_parts + [pltpu.VMEM((part_rows, B_WIDTH), F32)] * n_parts
        + [pltpu.VMEM((part_chunks, 4, CHUNK, CHUNK), BF16)] * (2 * n_parts),
        compiler_params=_seq_params(),
    )(r0, target, wb, wkv, wob, post_g, post_b, sinks, bias)


def _bwd_b(r0, dr1, q, z, kv, wb, wkv, wob, post_g, post_b, sinks, bias):
    seq = r0.shape[0]
    tm = min(TOKENS_B_BWD, seq)
    n_chunks = tm // CHUNK
    n_blocks = seq // tm
    shard_cols = wb.shape[2]

    def body(r0_ref, dr1_ref, q_ref, z_ref, kv_ref, kvp_ref, wb_ref, wkv_ref, wob_ref, pg_ref, pb_ref, sink_ref,
             bias_ref, dr0_ref, dwb_out, dwkv_out, dwob_out, dsink_ref, dl_out, dpost_ref,
             dwb_acc, dwkv_acc, dwob_acc, dl_acc, dsink_acc, carry_ref, ek_ref, ev_ref, y_ref, dqz_ref, dkv_ref):
        step = pl.program_id(0)
        block = n_blocks - 1 - step

        @pl.when(step == 0)
        def _():
            dpost_ref[...] = jnp.zeros_like(dpost_ref)
            dwb_acc[...] = jnp.zeros_like(dwb_acc)
            dwkv_acc[...] = jnp.zeros_like(dwkv_acc)
            dwob_acc[...] = jnp.zeros_like(dwob_acc)
            dl_acc[...] = jnp.zeros_like(dl_acc)
            carry_ref[...] = jnp.zeros_like(carry_ref)
            dsink_acc[...] = jnp.zeros_like(dsink_acc)

        xhat0, rstd0 = _ln_normalize(r0_ref[...])
        h1b = (xhat0 * pg_ref[0:1, :] + pb_ref[0:1, :]).astype(BF16)
        dr1 = dr1_ref[...]
        dr1b = dr1.astype(BF16)
        dy = _nt(dr1b, wob_ref[...])
        sg, dsg = _silu_and_grad(z_ref[...])
        do_all = dy * sg
        dyz = dy * dsg

        for ci in range(n_chunks + 1):
            src = kvp_ref[...] if ci == 0 else kv_ref[(ci - 1) * CHUNK:ci * CHUNK, :]
            src = src.astype(F32)
            for i, e in enumerate(_expand_heads(src[:, :KV_WIDTH])):
                ek_ref[ci, i] = e
            for i, e in enumerate(_expand_heads(src[:, KV_WIDTH:])):
                ev_ref[ci, i] = e

        sink_cols = _sink_columns(sink_ref)
        for ci in reversed(range(n_chunks)):
            rows = slice(ci * CHUNK, (ci + 1) * CHUNK)
            sel = jnp.where(block * n_chunks + ci > 0, 1, 0)
            dk_band = jnp.zeros((2 * CHUNK, KV_WIDTH), F32)
            dv_band = jnp.zeros((2 * CHUNK, KV_WIDTH), F32)
            for kvh in range(N_KV_HEADS):
                kblk, vblk = _band_operands(ek_ref.at[ci], ek_ref.at[ci + 1], ev_ref.at[ci], ev_ref.at[ci + 1], kvh)
                qs = _stack_pairs(q_ref, rows, kvh)
                p, psink = _softmax_stack(_nt(qs, kblk) + bias_ref[sel, kvh], sink_cols[kvh])
                pb16 = p.astype(BF16)
                o = _nn(pb16, vblk)
                y = (o * _stack_pairs(sg, rows, kvh)).astype(BF16)
                dz = (o * _stack_pairs(dyz, rows, kvh)).astype(BF16)
                dob = _stack_pairs(do_all, rows, kvh).astype(BF16)
                dp = _nt(dob, vblk)
                dlogits = []
                for half in range(2):
                    hs = slice(half * 2 * CHUNK, (half + 1) * 2 * CHUNK)
                    delta = jnp.sum(dp[:, hs] * p[:, hs], axis=-1, keepdims=True)
                    dlogits.append(p[:, hs] * (dp[:, hs] - delta))
                    dsink_acc[2 * kvh + half] += psink[half] * delta
                dlog = jnp.concatenate(dlogits, axis=1)
                dl_acc[kvh] += dlog
                dsc = dlog.astype(BF16)
                dq = (_nn(dsc, kblk) * SCALE).astype(BF16)
                for pp in range(PAIRS_PER_KV):
                    blk = slice(pp * CHUNK, (pp + 1) * CHUNK)
                    cols = _pair_cols(kvh, pp)
                    y_ref[rows, cols] = y[blk]
                    dqz_ref[rows, cols] = dq[blk]
                    dqz_ref[rows, B_WIDTH + cols.start:B_WIDTH + cols.stop] = dz[blk]
                dk_band = dk_band + _fold_heads(_tn(dsc, qs), kvh)
                dv_band = dv_band + _fold_heads(_tn(pb16, dob), kvh)
            band = jnp.concatenate([dk_band, dv_band], axis=1)
            dkv_ref[rows, :] = (band[CHUNK:] + carry_ref[...]).astype(BF16)
            carry_ref[...] = band[:CHUNK]

        dqz = dqz_ref[...]
        dkv = dkv_ref[...]
        dh1 = ALPHA * dr1 + _nt(dkv, wkv_ref[...])
        for k in range(N_DEV):
            part = dqz[:, k * shard_cols:(k + 1) * shard_cols]
            dh1 = dh1 + _nt(part, wb_ref[k])
            dwb_acc[k] += _tn(h1b, part)
        dpost_ref[0:1, :] += jnp.sum(dh1 * xhat0, axis=0, keepdims=True)
        dpost_ref[1:2, :] += jnp.sum(dh1, axis=0, keepdims=True)
        dr0_ref[...] = _ln_backward(dh1 * pg_ref[0:1, :], xhat0, rstd0)
        dwkv_acc[...] += _tn(h1b, dkv)
        dwob_acc[...] += _tn(y_ref[...], dr1b)

        @pl.when(step == n_blocks - 1)
        def _():
            pltpu.sync_copy(dwb_acc, dwb_out)
            pltpu.sync_copy(dwkv_acc, dwkv_out)
            pltpu.sync_copy(dwob_acc, dwob_out)
            pltpu.sync_copy(dl_acc, dl_out)
            lane = lax.broadcasted_iota(jnp.int32, (1, CHUNK), 1)
            dsink = jnp.zeros((1, CHUNK), F32)
            for kvh in range(N_KV_HEADS):
                for half in range(2):
                    for pp in range(PAIRS_PER_KV):
                        head = 2 * (kvh * PAIRS_PER_KV + pp) + half
                        tot = jnp.sum(dsink_acc[2 * kvh + half, pp * CHUNK:(pp + 1) * CHUNK, :], axis=0, keepdims=True)
                        dsink = dsink - jnp.where(lane == head, tot, 0.0)
            dsink_ref[...] = dsink

    rev = functools.partial(_rows, tm, reverse_of=n_blocks)
    prev_chunk = pl.BlockSpec(
        (CHUNK, 2 * KV_WIDTH), lambda i: (jnp.maximum((n_blocks - 1 - i) * n_chunks - 1, 0), 0))
    return pl.pallas_call(
        body, name="bwd_b", grid=(n_blocks,),
        out_shape=[jax.ShapeDtypeStruct((seq, D_MODEL), F32), jax.ShapeDtypeStruct(wb.shape, F32),
                   jax.ShapeDtypeStruct(wkv.shape, F32), jax.ShapeDtypeStruct(wob.shape, F32),
                   jax.ShapeDtypeStruct((1, CHUNK), F32),
                   jax.ShapeDtypeStruct((N_KV_HEADS, STACK, 4 * CHUNK), F32), jax.ShapeDtypeStruct((2, D_MODEL), F32)],
        in_specs=[rev(D_MODEL), rev(D_MODEL), rev(B_WIDTH), rev(B_WIDTH), rev(2 * KV_WIDTH), prev_chunk,
                  _resident(wb.shape), _resident(wkv.shape), _resident(wob.shape), _resident(post_g.shape),
                  _resident(post_b.shape), SMEM, _resident(bias.shape)],
        out_specs=[rev(D_MODEL), ANY, ANY, ANY, pl.BlockSpec((1, CHUNK), lambda i: (0, 0)), ANY,
                   pl.BlockSpec((2, D_MODEL), lambda i: (0, 0))],
        scratch_shapes=[pltpu.VMEM(wb.shape, F32), pltpu.VMEM(wkv.shape, F32), pltpu.VMEM(wob.shape, F32),
                        pltpu.VMEM((N_KV_HEADS, STACK, 4 * CHUNK), F32), pltpu.VMEM((2 * N_KV_HEADS, STACK, 1), F32),
                        pltpu.VMEM((CHUNK, 2 * KV_WIDTH), F32),
                        pltpu.VMEM((n_chunks + 1, 4, CHUNK, CHUNK), BF16),
                        pltpu.VMEM((n_chunks + 1, 4, CHUNK, CHUNK), BF16),
                        pltpu.VMEM((tm, B_WIDTH), BF16), pltpu.VMEM((tm, 2 * B_WIDTH), BF16),
                        pltpu.VMEM((tm, 2 * KV_WIDTH), BF16)],
        compiler_params=_seq_params(),
    )(r0, dr1, q, z, kv, kv, wb, wkv, wob, post_g, post_b, sinks, bias)


def _bwd_a(dr0, p, ln_gb, w_spatial, bsp_t, woa, exchange):
    seq = dr0.shape[0]
    tm = min(TOKENS_A, seq)
    n_chunks = tm // CHUNK
    n_blocks = seq // tm

    def body(dr0_ref, p_ref, ln_ref, ws_ref, bsp_ref, woa_ref, *refs):
        ex_in, refs = refs[:exchange.n], refs[exchange.n:]
        (dp_ref, dwoa_out, dws_ref, dbsp_ref, dln_ref), refs = refs[:5], refs[5:]
        ex_out, refs = refs[:exchange.n], refs[exchange.n:]
        (dwoa_acc, s_ref, ds_ref, dvn_ref), ex_sems = refs[:4], refs[4:]
        step = pl.program_id(0)

        @pl.when(step == 0)
        def _():
            exchange.start(ex_in, ex_out, ex_sems)
            dwoa_acc[...] = jnp.zeros_like(dwoa_acc)
            dws_ref[...] = jnp.zeros_like(dws_ref)
            dbsp_ref[...] = jnp.zeros_like(dbsp_ref)
            dln_ref[...] = jnp.zeros_like(dln_ref)

        dr0b = dr0_ref[...].astype(BF16)

        vhat, rstd_v = _ln_normalize(p_ref[:, A_WIDTH:2 * A_WIDTH])
        ln_g = ln_ref[0:1, :]
        vn = (vhat * ln_g + ln_ref[1:2, :]).astype(BF16)
        _spatial_gate(ws_ref, bsp_ref, vn, s_ref, n_chunks)
        u = p_ref[:, :A_WIDTH]
        s = s_ref[...]
        sg, dsg = _silu_and_grad(p_ref[:, 2 * A_WIDTH:])
        gate = s * sg
        dwoa_acc[...] += _tn((u * gate).astype(BF16), dr0b)
        dy = _nt(dr0b, woa_ref[...])
        dyu = dy * u
        dp_ref[:, :A_WIDTH] = (dy * gate).astype(BF16)
        dp_ref[:, 2 * A_WIDTH:] = (dyu * (s * dsg)).astype(BF16)
        ds_ref[...] = dyu * sg

        lane = lax.broadcasted_iota(jnp.int32, (CHUNK, CHUNK), 1)
        dbsp = jnp.zeros((CHUNK, CHUNK), F32)
        for g in range(A_GROUPS):
            w = _masked_spatial(ws_ref, g)
            cols = slice(g * A_GROUP_DIM, (g + 1) * A_GROUP_DIM)
            dws = jnp.zeros((CHUNK, CHUNK), F32)
            for ci in range(n_chunks):
                rows = slice(ci * CHUNK, (ci + 1) * CHUNK)
                ds = ds_ref[rows, cols]
                dsb = ds.astype(BF16)
                dvn_ref[rows, cols] = _tn(w, dsb)
                dws = dws + _nt(dsb, vn[rows, cols])
                dbsp = dbsp + jnp.where(lane == g, jnp.sum(ds, axis=-1, keepdims=True), 0.0)
            dws_ref[g] += dws
        dbsp_ref[...] += dbsp

        dvn = dvn_ref[...]
        dln_ref[0:1, :] += jnp.sum(dvn * vhat, axis=0, keepdims=True)
        dln_ref[1:2, :] += jnp.sum(dvn, axis=0, keepdims=True)
        dp_ref[:, A_WIDTH:2 * A_WIDTH] = _ln_backward(dvn * ln_g, vhat, rstd_v).astype(BF16)

        @pl.when(step == n_blocks - 1)
        def _():
            t = lax.broadcasted_iota(jnp.int32, (CHUNK, CHUNK), 0)
            for g in range(A_GROUPS):
                dws_ref[g] = jnp.where(lane <= t, dws_ref[g], 0.0)
            pltpu.sync_copy(dwoa_acc, dwoa_out)
            exchange.wait(ex_in, ex_out, ex_sems)

    def fixed(shape):
        nd = len(shape)
        return pl.BlockSpec(shape, lambda i: (0,) * nd)

    return pl.pallas_call(
        body, name="bwd_a", grid=(n_blocks,),
        out_shape=[jax.ShapeDtypeStruct((seq, 3 * A_WIDTH), BF16),
                   jax.ShapeDtypeStruct(woa.shape, F32), jax.ShapeDtypeStruct(w_spatial.shape, F32),
                   jax.ShapeDtypeStruct((CHUNK, CHUNK), F32), jax.ShapeDtypeStruct((2, A_WIDTH), F32)]
        + exchange.out_shape,
        in_specs=[_rows(tm, D_MODEL), _rows(tm, 3 * A_WIDTH), _resident(ln_gb.shape),
                  _resident(w_spatial.shape), _resident(bsp_t.shape), _resident(woa.shape)] + [ANY] * exchange.n,
        out_specs=[_rows(tm, 3 * A_WIDTH), ANY, fixed(w_spatial.shape),
                   fixed((CHUNK, CHUNK)), fixed((2, A_WIDTH))] + [ANY] * exchange.n,
        scratch_shapes=[pltpu.VMEM(woa.shape, F32), pltpu.VMEM((tm, A_WIDTH), F32),
                        pltpu.VMEM((tm, A_WIDTH), F32), pltpu.VMEM((tm, A_WIDTH), F32)] + exchange.scratch_shapes,
        compiler_params=_seq_params(),
    )(dr0, p, ln_gb, w_spatial, bsp_t, woa, *exchange.inputs)


def _input_grad_blocks(seq):
    tm = min(TOKENS_DX, seq // 8)
    return tm, seq // tm


def _input_grad(dp, dr0, wa, first, n_blocks, earlier, exchange):
    seq = dr0.shape[0]
    tm, _ = _input_grad_blocks(seq)
    shard_cols = wa.shape[2]
    n_own = 3 if earlier is None else 4

    def body(dp_ref, dr0_ref, wa_ref, *refs):
        refs = refs[n_own - 3:]
        ex_in, dx_ref, ex_out, ex_sems = (refs[:exchange.n], refs[exchange.n], refs[exchange.n + 1:2 * exchange.n + 1],
                                          refs[2 * exchange.n + 1:])
        step = pl.program_id(0)

        @pl.when(step == 0)
        def _():
            exchange.start(ex_in, ex_out, ex_sems)

        dx = ALPHA * dr0_ref[...]
        for k in range(N_DEV):
            dx = dx + _nt(dp_ref[:, k * shard_cols:(k + 1) * shard_cols], wa_ref[k])
        dx_ref[...] = dx

        @pl.when(step == n_blocks - 1)
        def _():
            exchange.wait(ex_in, ex_out, ex_sems)

    def rows(width):
        return pl.BlockSpec((tm, width), lambda i: (first + i, 0))

    return pl.pallas_call(
        body, name=f"input_grad_{first}", grid=(n_blocks,),
        out_shape=[jax.ShapeDtypeStruct((seq, D_MODEL), F32)] + exchange.out_shape,
        in_specs=[rows(3 * A_WIDTH), rows(D_MODEL), _resident(wa.shape)] + [ANY] * (n_own - 3 + exchange.n),
        out_specs=[rows(D_MODEL)] + [ANY] * exchange.n,
        scratch_shapes=exchange.scratch_shapes,
        input_output_aliases={} if earlier is None else {3: 0},
        compiler_params=_seq_params(),
    )(dp, dr0, wa, *(() if earlier is None else (earlier,)), *exchange.inputs)


def _in_proj_weight_grad(xb, dp, shard_cols, exchange):
    seq = xb.shape[0]
    tk = min(TOKENS_WGRAD, seq)
    n_k = seq // tk

    def body(x_ref, dp_ref, *refs):
        ex_in, out_ref, ex_out, ex_sems = (refs[:exchange.n], refs[exchange.n], refs[exchange.n + 1:2 * exchange.n + 1],
                                           refs[2 * exchange.n + 1:])
        shard, step = pl.program_id(0), pl.program_id(1)

        @pl.when((shard == 0) & (step == 0))
        def _():
            exchange.start(ex_in, ex_out, ex_sems)

        @pl.when(step == 0)
        def _():
            out_ref[...] = jnp.zeros_like(out_ref)

        out_ref[...] += _tn(x_ref[...], dp_ref[...])

        @pl.when((shard == N_DEV - 1) & (step == n_k - 1))
        def _():
            exchange.wait(ex_in, ex_out, ex_sems)

    return pl.pallas_call(
        body, name="in_proj_weight_grad", grid=(N_DEV, n_k),
        out_shape=[jax.ShapeDtypeStruct((N_DEV, D_MODEL, shard_cols), F32)] + exchange.out_shape,
        in_specs=[pl.BlockSpec((tk, D_MODEL), lambda j, k: (k, 0)),
                  pl.BlockSpec((tk, shard_cols), lambda j, k: (k, j))] + [ANY] * exchange.n,
        out_specs=[pl.BlockSpec((None, D_MODEL, shard_cols), lambda j, k: (j, 0, 0))] + [ANY] * exchange.n,
        scratch_shapes=exchange.scratch_shapes,
        compiler_params=pltpu.CompilerParams(dimension_semantics=("arbitrary", "arbitrary"),
                                             vmem_limit_bytes=VMEM_LIMIT_BYTES),
    )(xb, dp, *exchange.inputs)


class _Exchange:
    def __init__(self, pair=(), chip=(), gather=()):
        self.pair, self.chip, self.gather = tuple(pair), tuple(chip), tuple(gather)
        self.inputs = self.pair + self.chip + self.gather
        self.n = len(self.inputs)
        self.out_shape = [jax.ShapeDtypeStruct((4, 1) + g.shape[2:], F32) for g in self.pair]
        self.out_shape += [jax.ShapeDtypeStruct((3,) + s.shape[1:], s.dtype) for s in self.chip]
        self.out_shape += [jax.ShapeDtypeStruct((N_DEV,) + g.shape, g.dtype) for g in self.gather]
        self.scratch_shapes = [pltpu.SemaphoreType.DMA((max(len(self.pair), 1),)) for _ in range(2)]
        self.scratch_shapes += [pltpu.SemaphoreType.DMA((max(len(self.chip), 1), 3)) for _ in range(2)]
        self.scratch_shapes += [pltpu.SemaphoreType.DMA((max(len(self.gather), 1), N_DEV)) for _ in range(2)]

    def _copies(self, ins, outs, sems):
        if not self.n:
            return []
        pair_send, pair_recv, chip_send, chip_recv, gather_send, gather_recv = sems
        x, y, c = lax.axis_index("x"), lax.axis_index("y"), lax.axis_index("c")
        copies = []
        for a in range(len(self.gather)):
            src = ins[len(self.pair) + len(self.chip) + a]
            dst = outs[len(self.pair) + len(self.chip) + a].at[4 * x + 2 * y + c]
            copies.append(pltpu.make_async_copy(src, dst, gather_send.at[a, 0]))
            for r in range(1, N_DEV):
                to = (1 - x if r & 4 else x, 1 - y if r & 2 else y, 1 - c if r & 1 else c)
                copies.append(pltpu.make_async_remote_copy(
                    src_ref=src, dst_ref=dst, send_sem=gather_send.at[a, r], recv_sem=gather_recv.at[a, r],
                    device_id=to, device_id_type=MESH))
        for a in range(len(self.pair)):
            copies.append(pltpu.make_async_remote_copy(
                src_ref=ins[a].at[:, pl.ds(1 - c, 1)], dst_ref=outs[a],
                send_sem=pair_send.at[a], recv_sem=pair_recv.at[a], device_id=(x, y, 1 - c), device_id_type=MESH))
        for a in range(len(self.chip)):
            src, dst = ins[len(self.pair) + a], outs[len(self.pair) + a]
            for k, (px, py) in enumerate([(1 - x, y), (x, 1 - y), (1 - x, 1 - y)]):
                copies.append(pltpu.make_async_remote_copy(
                    src_ref=src.at[2 * px + py], dst_ref=dst.at[k], send_sem=chip_send.at[a, k],
                    recv_sem=chip_recv.at[a, k], device_id=(px, py, c), device_id_type=MESH))
        return copies

    def start(self, ins, outs, sems):
        for cp in self._copies(ins, outs, sems):
            cp.start()

    def wait(self, ins, outs, sems):
        for cp in self._copies(ins, outs, sems):
            cp.wait()


def _pair_sum(place, grads, landeds, dtypes, finish=None):
    n = len(grads)

    def body(place_ref, *refs):
        ins, outs = refs[:len(in_specs)], refs[len(in_specs):]
        for a in range(n):
            outs[a][...] = (ins[2 * a][...] + ins[2 * a + 1][...]).astype(dtypes[a])
        if finish is not None:
            grad_ref, landed_ref, got_ref = ins[2 * n:]

            @pl.when(pl.program_id(0) == 0)
            def _():
                outs[n][...] = (((grad_ref[...] + landed_ref[...]) + got_ref[0]) + got_ref[1]) + got_ref[2]

    in_specs, out_specs, operands = [], [], []
    for g, l in zip(grads, landeds):
        rows, cols = g.shape[2:]
        in_specs += [pl.BlockSpec((None, None, rows, cols), lambda k, pr: (k, pr[1], 0, 0)),
                     pl.BlockSpec((None, None, rows, cols), lambda k, pr: (k, 0, 0, 0))]
        out_specs.append(pl.BlockSpec((None, rows, cols), lambda k, pr: (k, 0, 0)))
        operands += [g, l]
    out_shape = [jax.ShapeDtypeStruct((4,) + g.shape[2:], dt) for g, dt in zip(grads, dtypes)]
    if finish is not None:
        rows, cols = finish[2].shape[1:]
        in_specs += [pl.BlockSpec((None, None, rows, cols), lambda k, pr: (pr[0], pr[1], 0, 0)),
                     pl.BlockSpec((None, None, rows, cols), lambda k, pr: (pr[0], 0, 0, 0)),
                     pl.BlockSpec((3, rows, cols), lambda k, pr: (0, 0, 0))]
        out_specs.append(pl.BlockSpec((rows, cols), lambda k, pr: (0, 0)))
        out_shape.append(jax.ShapeDtypeStruct((rows, cols), F32))
        operands += list(finish)
    return pl.pallas_call(
        body, name="pair_sum", out_shape=out_shape,
        grid_spec=pltpu.PrefetchScalarGridSpec(num_scalar_prefetch=1, grid=(4,), in_specs=in_specs,
                                               out_specs=out_specs),
        compiler_params=pltpu.CompilerParams(dimension_semantics=("arbitrary",), vmem_limit_bytes=VMEM_LIMIT_BYTES),
    )(place, *operands)


def _adamw_math(w, g, m, v):
    m = ADAM_B1 * m + (1.0 - ADAM_B1) * g
    v = ADAM_B2 * v + (1.0 - ADAM_B2) * (g * g)
    m_hat = m / (1.0 - ADAM_B1 ** ADAM_STEP)
    v_hat = v / (1.0 - ADAM_B2 ** ADAM_STEP)
    delta = -ADAM_LR * (m_hat / (jnp.sqrt(v_hat) + ADAM_EPS) + ADAM_WD * w)
    return delta, m, v


def _adamw_shards(place, items, steps, small=(), packed=None, scalar_at=None):
    n, n_small = len(items), len(small)
    n_in = 6 * n + (1 + 3 * n_small if n_small else 0)

    def body(place_ref, *refs):
        ins, small_ins = refs[:6 * n], refs[6 * n + 1:n_in]
        outs, small_outs = refs[n_in:n_in + 4 * n], refs[n_in + 4 * n:]
        for i in range(n):
            a_ref, b_ref, got_ref, w_ref, m_ref, v_ref = ins[6 * i:6 * i + 6]
            g = a_ref[...] + b_ref[...]
            for k in range(3):
                g = g + got_ref[k].astype(F32)
            outs[4 * i][...] = g
            outs[4 * i + 1][...], outs[4 * i + 2][...], outs[4 * i + 3][...] = _adamw_math(
                w_ref[...], g, m_ref[...], v_ref[...])

        if n_small:
            packed_ref = refs[6 * n]

            @pl.when(pl.program_id(0) == 0)
            def _():
                me = 2 * place_ref[0] + place_ref[1]
                small_outs[4 * n_small][...] = packed_ref[scalar_at:scalar_at + 1, 0:1]
                for i, (w, (at, mine), _, _) in enumerate(small):
                    w_ref, m_ref, v_ref = small_ins[3 * i:3 * i + 3]
                    g_ref, d_ref, new_m_ref, new_v_ref = small_outs[4 * i:4 * i + 4]
                    rows, cols = w.shape
                    if cols <= CHUNK:
                        g_ref[...] = packed_ref[at:at + rows, :cols]
                    else:
                        per_row = cols // CHUNK
                        start = at + me * (rows * per_row) if mine else at
                        for r in range(rows * per_row):
                            lane = (r % per_row) * CHUNK
                            row = r // per_row
                            g_ref[row:row + 1, lane:lane + CHUNK] = packed_ref[pl.ds(start + r, 1), :]
                    d_ref[...], new_m_ref[...], new_v_ref[...] = _adamw_math(
                        w_ref[...], g_ref[...], m_ref[...], v_ref[...])

    def whole(a):
        return pl.BlockSpec(a.shape, lambda i, pr: (0, 0))

    in_specs, out_specs, out_shape, operands = [], [], [], []
    for grad, landed, got, w, m, v in items:
        rows, cols = w.shape
        tr = rows // steps
        assert tr * steps == rows and tr % 16 == 0, (rows, steps)
        spec = pl.BlockSpec((tr, cols), lambda i, pr: (i, 0))
        in_specs += [pl.BlockSpec((None, None, tr, cols), lambda i, pr: (pr[0], pr[1], i, 0)),
                     pl.BlockSpec((None, None, tr, cols), lambda i, pr: (pr[0], 0, i, 0)),
                     pl.BlockSpec((3, tr, cols), lambda i, pr: (0, i, 0)), spec, spec, spec]
        out_specs += [spec] * 4
        out_shape += [jax.ShapeDtypeStruct((rows, cols), F32)] * 4
        operands += [grad, landed, got, w, m, v]
    if n_small:
        in_specs.append(whole(packed))
        operands.append(packed)
    for w, _, m, v in small:
        in_specs += [whole(w)] * 3
        out_specs += [whole(w)] * 4
        out_shape += [jax.ShapeDtypeStruct(w.shape, F32)] * 4
        operands += [w, m, v]
    if n_small:
        out_specs.append(pl.BlockSpec((1, 1), lambda i, pr: (0, 0)))
        out_shape.append(jax.ShapeDtypeStruct((1, 1), F32))
    outs = pl.pallas_call(
        body, name="adamw_shards", out_shape=out_shape,
        grid_spec=pltpu.PrefetchScalarGridSpec(num_scalar_prefetch=1, grid=(steps,), in_specs=in_specs,
                                               out_specs=out_specs),
        compiler_params=pltpu.CompilerParams(dimension_semantics=("arbitrary",), vmem_limit_bytes=VMEM_LIMIT_BYTES),
    )(place, *operands)
    small_outs = outs[4 * n:]
    return ([outs[4 * i:4 * i + 4] for i in range(n)],
            [tuple(small_outs[4 * i:4 * i + 4]) for i in range(n_small)], small_outs[4 * n_small:])


def _pack_small(parts):
    rows = []
    for a in parts:
        r, c = a.shape
        rows.append(jnp.pad(a, ((0, (-r) % 8), (0, CHUNK - c))))
    flat = jnp.concatenate(rows, axis=0)
    return jnp.pad(flat, ((0, N_DEV * SMALL_ROWS - flat.shape[0]), (0, 0))).reshape(N_DEV, SMALL_ROWS, CHUNK)


def _small_rows(shapes):
    out, at = [], 0
    for r, _ in shapes:
        out.append(at)
        at += r + (-r) % 8
    return out


def kernel(x, w_in_a, sgu_ln_g, sgu_ln_b, w_spatial, b_spatial, w_out_a, w_kv, w_in_b, attn_sinks, rel_bias, w_out_b, post_ln_g, post_ln_b, loss_target, m_w_in_a, m_sgu_ln_g, m_sgu_ln_b, m_w_spatial, m_b_spatial, m_w_out_a, m_w_kv, m_w_in_b, m_attn_sinks, m_rel_bias, m_w_out_b, m_post_ln_g, m_post_ln_b, v_w_in_a, v_sgu_ln_g, v_sgu_ln_b, v_w_spatial, v_b_spatial, v_w_out_a, v_w_kv, v_w_in_b, v_attn_sinks, v_rel_bias, v_w_out_b, v_post_ln_g, v_post_ln_b):
    seq = x.shape[1]
    bucket =jnp.asarray(_band_tables())

    ln_shard = jnp.concatenate([sgu_ln_g, sgu_ln_b], axis=0)
    x2, tgt = x[0], loss_target[0]
    order = jnp.stack([2 * px + py for px, py, _ in _use_order()[::2]]).astype(jnp.int32)
    p, wa, woa, ln_all, xb = _gather_in_proj(order, x2, w_in_a[0], (w_out_a[0],), ln_shard)
    woa = woa.reshape(A_WIDTH, D_MODEL)
    ln_gb = jnp.transpose(ln_all, (1, 0, 2)).reshape(2, A_WIDTH)
    ws = w_spatial[0]
    bsp_t = jnp.pad(b_spatial[0].T, ((0, 0), (0, CHUNK - A_GROUPS)))
    rel_t = rel_bias.T
    bias = _bias_table(rel_t, bucket)

    r0, wkv, wb, wob = _fwd_a(x2, p, ln_gb, ws, bsp_t, woa, (w_kv, w_in_b[0], w_out_b[0]))
    wkv = wkv.reshape(D_MODEL, 2 * KV_WIDTH)
    wob = wob.reshape(B_WIDTH, D_MODEL)
    q, z, kv, dr1, loss, dg1, db1 = _fwd_b(r0, tgt, wb, wkv, wob, post_ln_g, post_ln_b, attn_sinks, bias)
    dr0, dwb, dwkv, dwob, dsink, dlog, dpost0 = _bwd_b(r0, dr1, q, z, kv, wb, wkv, wob, post_ln_g, post_ln_b,
                                                       attn_sinks, bias)
    drel = _rel_bias_grad(dlog, bucket)

    place = jnp.stack([2 * lax.axis_index("x") + lax.axis_index("y"), lax.axis_index("c")]).astype(jnp.int32)

    def by_device(g):
        return g.reshape((4, 2, g.shape[0] // N_DEV) + g.shape[1:]) if g.ndim == 2 else g.reshape((4, 2) + g.shape[1:])

    dwb, dwkv, dwob = by_device(dwb), by_device(dwkv), by_device(dwob)
    dp, dwoa, dws, dbsp_t, dln, l_wb, l_wkv, l_wob = _bwd_a(dr0, p, ln_gb, ws, bsp_t, woa,
                                                            _Exchange(pair=(dwb, dwkv, dwob)))
    dwoa = by_device(dwoa)
    small_shapes = [(A_GROUPS * CHUNK, CHUNK), (A_GROUPS, CHUNK), (1, N_Q_HEADS), (N_Q_HEADS, REL_BUCKETS),
                    (16, CHUNK), (16, CHUNK), (16, CHUNK), (16, CHUNK), (1, 1)]
    small = by_device(_pack_small([
        dws.reshape(A_GROUPS * CHUNK, CHUNK), dbsp_t[:, :A_GROUPS].T, dsink[:, :N_Q_HEADS], drel,
        jnp.concatenate([dpost0[0:1], dg1], axis=0).reshape(16, CHUNK),
        jnp.concatenate([dpost0[1:2], db1], axis=0).reshape(16, CHUNK),
        dln[0].reshape(16, CHUNK), dln[1].reshape(16, CHUNK), loss]))
    s_wb, s_wkv, s_wob = _pair_sum(place, (dwb, dwkv, dwob), (l_wb, l_wkv, l_wob), (BF16,) * 3)
    dwa, l_woa, l_small, g_wb, g_wkv, g_wob = _in_proj_weight_grad(
        xb, dp, wa.shape[2], _Exchange(pair=(dwoa, small), chip=(s_wb, s_wkv, s_wob)))
    dwa = by_device(dwa)
    s_woa, s_small = _pair_sum(place, (dwoa, small), (l_woa, l_small), (BF16, F32))
    _, blocks = _input_grad_blocks(seq)
    cut = blocks // 4
    dx, l_wa, g_woa, g_small = _input_grad(dp, dr0, wa, 0, cut, None, _Exchange(pair=(dwa,), chip=(s_woa, s_small)))
    s_wa, my_small = _pair_sum(place, (dwa,), (l_wa,), (BF16,), finish=(small, l_small, g_small))
    dx, g_wa, small_all = _input_grad(dp, dr0, wa, cut, blocks - cut, dx,
                                      _Exchange(chip=(s_wa,), gather=(my_small,)))
    full = [dwa, dwoa, dwkv, dwb, dwob]
    landed = [l_wa, l_woa, l_wkv, l_wb, l_wob]
    got = [g_wa, g_woa, g_wkv, g_wb, g_wob]

    big = [(w_in_a, m_w_in_a, v_w_in_a), (w_out_a, m_w_out_a, v_w_out_a), (w_kv, m_w_kv, v_w_kv),
           (w_in_b, m_w_in_b, v_w_in_b), (w_out_b, m_w_out_b, v_w_out_b)]
    items = [(full[i], landed[i], got[i]) + tuple(a.reshape(got[i].shape[1:]) for a in big[i]) for i in range(5)]

    small_all = small_all.reshape(N_DEV * SMALL_ROWS, CHUNK)
    at_ws, at_bsp, at_sink, at_rel, at_pg, at_pb, at_lng, at_lnb, at_loss = _small_rows(small_shapes)
    small_params = [(sgu_ln_g, (at_lng, True), m_sgu_ln_g, v_sgu_ln_g),
                    (sgu_ln_b, (at_lnb, True), m_sgu_ln_b, v_sgu_ln_b),
                    (w_spatial, (at_ws, False), m_w_spatial, v_w_spatial),
                    (b_spatial, (at_bsp, False), m_b_spatial, v_b_spatial),
                    (attn_sinks, (at_sink, False), m_attn_sinks, v_attn_sinks),
                    (rel_t, (at_rel, False), m_rel_bias.T, v_rel_bias.T),
                    (post_ln_g, (at_pg, False), m_post_ln_g, v_post_ln_g),
                    (post_ln_b, (at_pb, False), m_post_ln_b, v_post_ln_b)]
    flat2 = [(w.reshape(-1, w.shape[-1]), at, m.reshape(-1, m.shape[-1]), v.reshape(-1, v.shape[-1]))
             for w, at, m, v in small_params]
    res_in, _, _ = _adamw_shards(place, items[:1], 4)
    res_rest, small_res, (loss_all,) = _adamw_shards(place, items[1:], 2, flat2, small_all, at_loss)
    big_out = [[r.reshape(big[i][0].shape) for r in res] for i, res in enumerate(res_in + res_rest)]
    small_out = [tuple(r.reshape(p[0].shape) for r in res) for res, p in zip(small_res, small_params)]

    order = [("big", 0), ("small", 0), ("small", 1), ("small", 2), ("small", 3), ("big", 1), ("big", 2), ("big", 3),
             ("small", 4), ("small", 5), ("big", 4), ("small", 6), ("small", 7)]
    grads, deltas, new_m, new_v = [], [], [], []
    for kind, i in order:
        if kind == "big":
            g, d, m, v = big_out[i]
        else:
            g, d, m, v = small_out[i]
            if small_params[i][0] is rel_t:
                g, d, m, v = g.T, d.T, m.T, v.T
        grads.append(g)
        deltas.append(d)
        new_m.append(m)
        new_v.append(v)
    return (loss_all.reshape(()), dx.reshape(x.shape), *grads, *deltas, *new_m, *new_v)
```

```python
import functools

import jax
import jax.numpy as jnp
import numpy as np
from jax import lax
from jax.experimental import pallas as pl
from jax.experimental.pallas import tpu as pltpu

F32 = jnp.float32
BF16 = jnp.bfloat16

D_MODEL = 1024
A_WIDTH = 2048
A_GROUPS = 8
A_GROUP_DIM = A_WIDTH // A_GROUPS
CHUNK = 128
HEAD_DIM = 64
N_Q_HEADS = 16
N_KV_HEADS = 2
PAIRS_PER_KV = N_Q_HEADS // N_KV_HEADS // 2
B_WIDTH = N_Q_HEADS * HEAD_DIM
KV_WIDTH = N_KV_HEADS * HEAD_DIM
REL_BUCKETS = 32
REL_MAX_DIST = 128
DEPTH = 2
ALPHA = (2.0 * DEPTH) ** 0.25
LN_EPS = 1e-5
NEG_INF = -1e30
SCALE = HEAD_DIM ** -0.5

ADAM_LR = 0.001
ADAM_B1 = 0.9
ADAM_B2 = 0.999
ADAM_EPS = 1e-08
ADAM_WD = 0.01
ADAM_STEP = 10

N_DEV = 8
MESH = pl.DeviceIdType.MESH
TOKENS_IN_PROJ = 1024
TOKENS_WGRAD = 4096
TOKENS_A = 256
TOKENS_A_FWD = 256
P_RING = 3
TOKENS_B = 512
PART_ROWS_B = 256
TOKENS_B_BWD = 256
TOKENS_DX = 512
VMEM_LIMIT_BYTES = 60 * 1024 * 1024
SMALL_ROWS = 144

ANY = pl.BlockSpec(memory_space=pl.ANY)
VMEM = pl.BlockSpec(memory_space=pltpu.VMEM)
SMEM = pl.BlockSpec(memory_space=pltpu.SMEM)


def _nn(a, b):
    return jnp.dot(a, b, preferred_element_type=F32)


def _nt(a, b):
    return lax.dot_general(a, b, (((1,), (1,)), ((), ())), preferred_element_type=F32)


def _tn(a, b):
    return lax.dot_general(a, b, (((0,), (0,)), ((), ())), preferred_element_type=F32)


def _resident(shape):
    nd = len(shape)
    return pl.BlockSpec(shape, lambda i: (0,) * nd, pipeline_mode=pl.Buffered(1))


def _rows(tm, width, reverse_of=None):
    if reverse_of is None:
        return pl.BlockSpec((tm, width), lambda i: (i, 0))
    return pl.BlockSpec((tm, width), lambda i: (reverse_of - 1 - i, 0))


def _seq_params():
    return pltpu.CompilerParams(dimension_semantics=("arbitrary",), vmem_limit_bytes=VMEM_LIMIT_BYTES)


def _in_turns(first, second):
    merged = [(i / max(len(first), 1), 0, f) for i, f in enumerate(first)]
    merged += [((i + 0.5) / max(len(second), 1), 1, f) for i, f in enumerate(second)]
    return [f for _, _, f in sorted(merged, key=lambda t: t[:2])]


def _ln_normalize(r):
    mu = jnp.mean(r, axis=-1, keepdims=True)
    xc = r - mu
    var = jnp.mean(xc * xc, axis=-1, keepdims=True)
    rstd = lax.rsqrt(var + LN_EPS)
    return xc * rstd, rstd


def _ln_backward(dxhat, xhat, rstd):
    m1 = jnp.mean(dxhat, axis=-1, keepdims=True)
    m2 = jnp.mean(dxhat * xhat, axis=-1, keepdims=True)
    return rstd * (dxhat - m1 - xhat * m2)


def _silu_and_grad(z):
    sig = jax.nn.sigmoid(z)
    silu = z * sig
    return silu, sig + silu * (1.0 - sig)


def _band_tables():
    t = np.arange(CHUNK, dtype=np.int32)[:, None]
    j = np.arange(2 * CHUNK, dtype=np.int32)[None, :]
    d = t + CHUNK - j
    in_window = (d >= 0) & (d < CHUNK)
    dc = np.clip(d, 0, REL_MAX_DIST - 1)
    max_exact = REL_BUCKETS // 2
    df = np.maximum(dc, 1).astype(np.float32)
    large = max_exact + (np.log(df / np.float32(max_exact)) / np.float32(np.log(REL_MAX_DIST / max_exact))
                         * np.float32(REL_BUCKETS - max_exact)).astype(np.int32)
    large = np.minimum(large, REL_BUCKETS - 1)
    bucket = np.where(dc < max_exact, dc, large)
    return np.where(in_window, bucket, -1).astype(np.int32)


def _use_order():
    x, y, c = lax.axis_index("x"), lax.axis_index("y"), lax.axis_index("c")
    blocks = [(x, y, c), (x, y, 1 - c)]
    for px, py in [(1 - x, y), (x, 1 - y), (1 - x, 1 - y)]:
        blocks += [(px, py, c), (px, py, 1 - c)]
    return blocks


def _gather_in_proj(order, x, w_in_a, small, ln_gb):
    seq = x.shape[0]
    tm = min(TOKENS_IN_PROJ, seq)
    n_blocks = seq // tm
    shard_cols = w_in_a.shape[1]
    n = len(small) + 2

    def body(order_ref, x_ref, *refs):
        ins, refs = refs[:n], refs[n:]
        p_ref, outs, xb_out, refs = refs[0], refs[1:n + 1], refs[n + 1], refs[n + 2:]
        wa_vmem, xb_vmem, stage = refs[0], refs[1], refs[2:n]
        send_sems, recv_sems, local_sems, out_sems, xb_sem = refs[n:]
        xb_copy = pltpu.make_async_copy(xb_vmem, xb_out, xb_sem)
        phase, blk = pl.program_id(0), pl.program_id(1)
        x_, y_, c_ = lax.axis_index("x"), lax.axis_index("y"), lax.axis_index("c")
        me, sibling = (x_, y_, c_), (x_, y_, 1 - c_)
        chips = [(1 - x_, y_), (x_, 1 - y_), (1 - x_, 1 - y_)]
        use = _use_order()

        def index(px, py, pc):
            return 4 * px + 2 * py + pc

        def slot(a, block):
            return (wa_vmem if a == 0 else outs[a]).at[index(*block)]

        def own(a):
            return slot(0, me) if a == 0 else (ins[a] if a == n - 1 else stage[a - 1])

        def copy(a, k, block, to, src=None):
            return pltpu.make_async_remote_copy(
                src_ref=slot(a, block) if src is None else src, dst_ref=slot(a, block),
                send_sem=send_sems.at[a, k], recv_sem=recv_sems.at[a, k], device_id=to, device_id_type=MESH)

        def keep(a):
            return pltpu.make_async_copy(own(a), slot(a, me), local_sems.at[a])

        def write_out(k):
            return pltpu.make_async_copy(slot(0, use[k]), outs[0].at[index(*use[k])], out_sems.at[k])

        def pass_on(a, j):
            copy(a, 1 + j, (*chips[j], c_), me).wait_recv()
            copy(a, 4 + j, (*chips[j], c_), sibling).start()

        def send(a, j):
            copy(a, 1 + j, me, (*chips[j], c_), own(a)).start()

        @pl.when((phase == 0) & (blk == 0))
        def _():
            wa_vmem[index(*me)] = ins[0][...].astype(BF16)
            for a in range(1, n - 1):
                stage[a - 1][...] = ins[a][...].astype(BF16)
            send(0, 0)
            send(0, 1)
            for a in range(n):
                copy(a, 0, me, sibling, own(a)).start()
                if a:
                    keep(a).start()
            write_out(0).start()
            copy(0, 0, sibling, me).wait_recv()
            write_out(1).start()

        @pl.when((phase == 1) & (blk == 0))
        def _():
            pass_on(0, 0)
            write_out(2).start()
            xb_copy.start()
            send(0, 2)
            for a in range(1, n):
                for j in range(3):
                    send(a, j)
            pass_on(0, 1)
            write_out(4).start()
            copy(0, 4, (*chips[0], 1 - c_), me).wait_recv()
            write_out(3).start()

        @pl.when((phase == 2) & (blk == 0))
        def _():
            copy(0, 5, (*chips[1], 1 - c_), me).wait_recv()
            write_out(5).start()

        @pl.when((phase == 2) & (blk == n_blocks // 2))
        def _():
            pass_on(0, 2)
            write_out(6).start()

        @pl.when((phase == 3) & (blk == 0))
        def _():
            copy(0, 6, (*chips[2], 1 - c_), me).wait_recv()
            write_out(7).start()
            for a in range(1, n):
                pass_on(a, 0)
                pass_on(a, 1)

        rows = pl.ds(pl.multiple_of(blk * tm, tm), tm)

        @pl.when(phase == 0)
        def _():
            xb_vmem[rows, :] = x_ref[...].astype(BF16)

        chip = order_ref[phase]
        xb = xb_vmem[rows, :]
        p_ref[:, :shard_cols] = _nn(xb, wa_vmem[2 * chip])
        p_ref[:, shard_cols:] = _nn(xb, wa_vmem[2 * chip + 1])

        @pl.when((phase == N_DEV // 2 - 1) & (blk == n_blocks - 1))
        def _():
            for a in range(1, n):
                pass_on(a, 2)
            for a in range(1, n):
                copy(a, 0, sibling, me).wait_recv()
                for j, far in enumerate(chips):
                    copy(a, 4 + j, (*far, 1 - c_), me).wait_recv()
            for a in range(n):
                for k in range(7):
                    copy(a, k, me, me).wait_send()
                if a:
                    keep(a).wait()
            for k in range(N_DEV):
                write_out(k).wait()
            xb_copy.wait()

    gathered = [jax.ShapeDtypeStruct((N_DEV,) + w_in_a.shape, BF16)]
    gathered += [jax.ShapeDtypeStruct((N_DEV,) + s.shape, BF16) for s in small]
    gathered.append(jax.ShapeDtypeStruct((N_DEV,) + ln_gb.shape, F32))
    return pl.pallas_call(
        body, name="gather_in_proj",
        out_shape=[jax.ShapeDtypeStruct((seq, N_DEV * shard_cols), F32)] + gathered
        + [jax.ShapeDtypeStruct((seq, D_MODEL), BF16)],
        grid_spec=pltpu.PrefetchScalarGridSpec(
            num_scalar_prefetch=1, grid=(N_DEV // 2, n_blocks),
            in_specs=[pl.BlockSpec((tm, D_MODEL), lambda k, i, order: (jnp.where(k == 0, i, n_blocks - 1), 0))]
            + [VMEM] * n,
            out_specs=[pl.BlockSpec((tm, 2 * shard_cols), lambda k, i, order: (i, order[k]))] + [ANY] * (n + 1),
            scratch_shapes=[pltpu.VMEM((N_DEV,) + w_in_a.shape, BF16), pltpu.VMEM((seq, D_MODEL), BF16)]
            + [pltpu.VMEM(s.shape, BF16) for s in small] + [
                pltpu.SemaphoreType.DMA((n, 7)), pltpu.SemaphoreType.DMA((n, 7)), pltpu.SemaphoreType.DMA((n,)),
                pltpu.SemaphoreType.DMA((N_DEV,)), pltpu.SemaphoreType.DMA(())]),
        compiler_params=pltpu.CompilerParams(dimension_semantics=("arbitrary", "arbitrary"),
                                             vmem_limit_bytes=VMEM_LIMIT_BYTES),
    )(order, x, w_in_a, *small, ln_gb)


def _bias_table(rel_bias_t, bucket):
    def body(rel_ref, bucket_ref, out_ref):
        first = jnp.broadcast_to(bucket_ref[0:1, :], (8, 2 * CHUNK))
        col = lax.broadcasted_iota(jnp.int32, (CHUNK, 2 * CHUNK), 1)
        for h in range(N_Q_HEADS):
            row = jnp.full(first.shape, NEG_INF, F32)
            for b in range(REL_BUCKETS):
                row = jnp.where(first == b, rel_ref[h, b], row)
            tab = pltpu.roll(jnp.broadcast_to(row[0:1, :], (CHUNK, 2 * CHUNK)), 0, 1, stride=1, stride_axis=0)
            pair = h // 2
            rows = slice((pair % PAIRS_PER_KV) * CHUNK, (pair % PAIRS_PER_KV + 1) * CHUNK)
            half = slice((h % 2) * 2 * CHUNK, (h % 2 + 1) * 2 * CHUNK)
            out_ref[1, pair // PAIRS_PER_KV, rows, half] = tab
            out_ref[0, pair // PAIRS_PER_KV, rows, half] = jnp.where(col < CHUNK, NEG_INF, tab)

    return pl.pallas_call(
        body, name="bias_table",
        out_shape=jax.ShapeDtypeStruct((2, N_KV_HEADS, PAIRS_PER_KV * CHUNK, 4 * CHUNK), F32),
        in_specs=[SMEM, VMEM], out_specs=VMEM,
    )(rel_bias_t, bucket)


def _rel_bias_grad(dlogits, bucket):
    def body(dl_ref, bucket_ref, out_ref):
        i = lax.broadcasted_iota(jnp.int32, (CHUNK, CHUNK), 0)
        k = lax.broadcasted_iota(jnp.int32, (CHUNK, CHUNK), 1)
        reverse = jnp.where(i + k == CHUNK - 1, 1.0, 0.0)
        sums = []
        for h in range(N_Q_HEADS):
            pair = h // 2
            dl = dl_ref[pair // PAIRS_PER_KV, (pair % PAIRS_PER_KV) * CHUNK:(pair % PAIRS_PER_KV + 1) * CHUNK,
                        (h % 2) * 2 * CHUNK:(h % 2 + 1) * 2 * CHUNK]
            backwards = jnp.dot(reverse, dl, precision=lax.Precision.HIGHEST, preferred_element_type=F32)
            turned = pltpu.roll(backwards, CHUNK + 1, 1, stride=1, stride_axis=0)
            sums.append(jnp.sum(turned, axis=0, keepdims=True))
        by_offset = jnp.concatenate(sums, axis=0)
        buckets = lax.broadcasted_iota(jnp.int32, (REL_BUCKETS, 2 * CHUNK), 0)
        one_hot = jnp.where(jnp.broadcast_to(bucket_ref[0:1, :], buckets.shape) == buckets, 1.0, 0.0)
        out_ref[...] = lax.dot_general(by_offset, one_hot, (((1,), (1,)), ((), ())),
                                       precision=lax.Precision.HIGHEST, preferred_element_type=F32)

    return pl.pallas_call(
        body, name="rel_bias_grad", out_shape=jax.ShapeDtypeStruct((N_Q_HEADS, REL_BUCKETS), F32),
        in_specs=[VMEM, VMEM], out_specs=VMEM,
    )(dlogits, bucket)


def _masked_spatial(ws_ref, g):
    t = lax.broadcasted_iota(jnp.int32, (CHUNK, CHUNK), 0)
    s = lax.broadcasted_iota(jnp.int32, (CHUNK, CHUNK), 1)
    return jnp.where(s <= t, ws_ref[g], 0.0).astype(BF16)


def _spatial_gate(ws_ref, bsp_t_ref, vn, s_ref, n_chunks):
    for g in range(A_GROUPS):
        w = _masked_spatial(ws_ref, g)
        cols = slice(g * A_GROUP_DIM, (g + 1) * A_GROUP_DIM)
        for ci in range(n_chunks):
            rows = slice(ci * CHUNK, (ci + 1) * CHUNK)
            s_ref[rows, cols] = _nn(w, vn[rows, cols]) + bsp_t_ref[:, g:g + 1]


def _fwd_a(x, p, ln_gb, w_spatial, bsp_t, woa, later):
    seq = x.shape[0]
    tm = min(TOKENS_A_FWD, seq)
    n_chunks = tm // CHUNK
    n_blocks = seq // tm
    n = len(later)

    def body(x_ref, p_hbm, ln_ref, ws_ref, bsp_ref, woa_ref, *refs):
        shards, refs = refs[:n], refs[n:]
        r0_ref, gathered, refs = refs[0], refs[1:1 + n], refs[1 + n:]
        s_ref, p_ring, stage, (send_sems, recv_sems, local_sems, ring_sems) = refs[0], refs[1], refs[2:2 + n], refs[2 + n:]
        step = pl.program_id(0)

        def fetch(block):
            slot = block % P_RING
            start = block * tm if isinstance(block, int) else pl.multiple_of(block * tm, tm)
            return pltpu.make_async_copy(p_hbm.at[pl.ds(start, tm), :], p_ring.at[slot], ring_sems.at[slot])

        @pl.when(step == 0)
        def _():
            for ahead in range(min(P_RING - 1, n_blocks)):
                fetch(ahead).start()

        @pl.when(step + P_RING - 1 < n_blocks)
        def _():
            fetch(step + P_RING - 1).start()

        fetch(step).wait()
        p_ref = p_ring.at[step % P_RING]

        def copies(a):
            x_, y_, c_ = lax.axis_index("x"), lax.axis_index("y"), lax.axis_index("c")
            mine = 4 * x_ + 2 * y_ + c_
            out = [pltpu.make_async_copy(stage[a], gathered[a].at[mine], local_sems.at[a])]
            for r in range(1, N_DEV):
                to = (1 - x_ if r & 4 else x_, 1 - y_ if r & 2 else y_, 1 - c_ if r & 1 else c_)
                out.append(pltpu.make_async_remote_copy(
                    src_ref=stage[a], dst_ref=gathered[a].at[mine], send_sem=send_sems.at[a, r - 1],
                    recv_sem=recv_sems.at[a, r - 1], device_id=to, device_id_type=MESH))
            return out

        @pl.when(step == 0)
        def _():
            for a in range(n):
                stage[a][...] = shards[a][...].astype(BF16)
                for cp in copies(a):
                    cp.start()

        xv = x_ref[...]
        vhat, _ = _ln_normalize(p_ref[:, A_WIDTH:2 * A_WIDTH])
        vn = (vhat * ln_ref[0:1, :] + ln_ref[1:2, :]).astype(BF16)
        _spatial_gate(ws_ref, bsp_ref, vn, s_ref, n_chunks)
        z = p_ref[:, 2 * A_WIDTH:]
        y = p_ref[:, :A_WIDTH] * s_ref[...] * (z * jax.nn.sigmoid(z))
        r0_ref[...] = ALPHA * xv + _nn(y.astype(BF16), woa_ref[...])

        @pl.when(step == n_blocks - 1)
        def _():
            for a in range(n):
                for cp in copies(a):
                    cp.wait()

    return pl.pallas_call(
        body, name="fwd_a", grid=(n_blocks,),
        out_shape=[jax.ShapeDtypeStruct((seq, D_MODEL), F32)]
        + [jax.ShapeDtypeStruct((N_DEV,) + s.shape, BF16) for s in later],
        in_specs=[_rows(tm, D_MODEL), ANY, _resident(ln_gb.shape), _resident(w_spatial.shape),
                  _resident(bsp_t.shape), _resident(woa.shape)] + [VMEM] * n,
        out_specs=[_rows(tm, D_MODEL)] + [ANY] * n,
        scratch_shapes=[pltpu.VMEM((tm, A_WIDTH), F32), pltpu.VMEM((P_RING, tm, 3 * A_WIDTH), F32)]
        + [pltpu.VMEM(s.shape, BF16) for s in later] + [
            pltpu.SemaphoreType.DMA((n, N_DEV - 1)), pltpu.SemaphoreType.DMA((n, N_DEV - 1)),
            pltpu.SemaphoreType.DMA((n,)), pltpu.SemaphoreType.DMA((P_RING,))],
        compiler_params=_seq_params(),
    )(x, p, ln_gb, w_spatial, bsp_t, woa, *later)


def _expand_heads(kv):
    lo = lax.broadcasted_iota(jnp.int32, kv.shape, 1) < HEAD_DIM
    swapped = pltpu.roll(kv, HEAD_DIM, 1)
    zero = jnp.zeros_like(kv)
    return (jnp.where(lo, kv, zero).astype(BF16), jnp.where(lo, zero, swapped).astype(BF16),
            jnp.where(lo, swapped, zero).astype(BF16), jnp.where(lo, zero, kv).astype(BF16))


def _fold_heads(acc, kvh):
    top, bot = acc[:2 * CHUNK], acc[2 * CHUNK:]
    lo = lax.broadcasted_iota(jnp.int32, top.shape, 1) < HEAD_DIM
    t = jnp.where(lo, top, 0.0)
    b = jnp.where(lo, 0.0, bot)
    if kvh == 0:
        return t + pltpu.roll(b, HEAD_DIM, 1)
    return pltpu.roll(t, HEAD_DIM, 1) + b


STACK = PAIRS_PER_KV * CHUNK


def _pair_cols(kvh, pp):
    pair = kvh * PAIRS_PER_KV + pp
    return slice(pair * CHUNK, (pair + 1) * CHUNK)


def _stack_pairs(ref_or_val, rows, kvh):
    return jnp.concatenate([ref_or_val[rows, _pair_cols(kvh, pp)] for pp in range(PAIRS_PER_KV)], axis=0)


def _sink_columns(sink_ref):
    blk = lax.broadcasted_iota(jnp.int32, (STACK, 1), 0) // CHUNK
    out = []
    for kvh in range(N_KV_HEADS):
        halves = []
        for half in range(2):
            col = jnp.zeros((STACK, 1), F32)
            for pp in range(PAIRS_PER_KV):
                col = jnp.where(blk == pp, sink_ref[0, 2 * (kvh * PAIRS_PER_KV + pp) + half], col)
            halves.append(col)
        out.append(halves)
    return out


def _softmax_stack(logits, sink_cols):
    ps, sinks = [], []
    for half in range(2):
        lg = logits[:, half * 2 * CHUNK:(half + 1) * 2 * CHUNK]
        sk = sink_cols[half]
        m = jnp.maximum(jnp.max(lg, axis=-1, keepdims=True), sk)
        e = jnp.exp(lg - m)
        es = jnp.exp(sk - m)
        inv = 1.0 / (jnp.sum(e, axis=-1, keepdims=True) + es)
        ps.append(e * inv)
        sinks.append(es * inv)
    return jnp.concatenate(ps, axis=1), sinks


def _band_operands(before_k, this_k, before_v, this_v, kvh):
    kblk = jnp.concatenate([before_k[2 * kvh], this_k[2 * kvh], before_k[2 * kvh + 1], this_k[2 * kvh + 1]], axis=0)
    vblk = jnp.concatenate([before_v[2 * kvh], this_v[2 * kvh], before_v[2 * kvh + 1], this_v[2 * kvh + 1]], axis=0)
    return kblk, vblk


def _fwd_b(r0, target, wb, wkv, wob, post_g, post_b, sinks, bias):
    seq = r0.shape[0]
    tm = min(TOKENS_B, seq)
    n_chunks = tm // CHUNK
    part_rows = min(PART_ROWS_B, tm)
    n_parts = tm // part_rows
    part_chunks = part_rows // CHUNK
    shard_cols = wb.shape[2]

    def body(r0_ref, tgt_ref, wb_ref, wkv_ref, wob_ref, pg_ref, pb_ref, sink_ref, bias_ref,
             q_ref, z_ref, kv_ref, dr1_ref, loss_ref, dg_ref, db_ref, ek_last, ev_last, *scratch):
        q_s, z_s, h1_s, o_s, ek_s, ev_s = (scratch[i * n_parts:(i + 1) * n_parts] for i in range(6))
        step = pl.program_id(0)

        @pl.when(step == 0)
        def _():
            ek_last[...] = jnp.zeros_like(ek_last)
            ev_last[...] = jnp.zeros_like(ev_last)
            loss_ref[...] = jnp.zeros_like(loss_ref)
            dg_ref[...] = jnp.zeros_like(dg_ref)
            db_ref[...] = jnp.zeros_like(db_ref)

        sink_cols = _sink_columns(sink_ref)
        g1 = pg_ref[1:2, :]

        def project_steps(part):
            rows = slice(part * part_rows, (part + 1) * part_rows)
            live = {}

            def normalize():
                xhat0, _ = _ln_normalize(r0_ref[rows, :])
                h1 = xhat0 * pg_ref[0:1, :] + pb_ref[0:1, :]
                h1_s[part][...] = h1
                live["h1b"] = h1.astype(BF16)

            def shard(k):
                def run():
                    qz = _nn(live["h1b"], wb_ref[k])
                    lo, hi = k * shard_cols, (k + 1) * shard_cols
                    if hi <= B_WIDTH:
                        qs = (qz * SCALE).astype(BF16)
                        q_ref[rows, lo:hi] = qs
                        q_s[part][:, lo:hi] = qs
                    else:
                        z_ref[rows, lo - B_WIDTH:hi - B_WIDTH] = qz
                        z_s[part][:, lo - B_WIDTH:hi - B_WIDTH] = qz
                return run

            def keys():
                kv = _nn(live["h1b"], wkv_ref[...])
                kv_ref[rows, :] = kv.astype(BF16)
                for cj in range(part_chunks):
                    sub = slice(cj * CHUNK, (cj + 1) * CHUNK)
                    for i, e in enumerate(_expand_heads(kv[sub, :KV_WIDTH])):
                        ek_s[part][cj, i] = e
                    for i, e in enumerate(_expand_heads(kv[sub, KV_WIDTH:])):
                        ev_s[part][cj, i] = e

            return [normalize, keys] + [shard(k) for k in range(N_DEV)]

        def attend_steps(part):
            def stack(cj, kvh):
                def run():
                    sub = slice(cj * CHUNK, (cj + 1) * CHUNK)
                    sel = jnp.where(step * n_chunks + part * part_chunks + cj > 0, 1, 0)
                    if cj:
                        before_k, before_v = ek_s[part].at[cj - 1], ev_s[part].at[cj - 1]
                    elif part:
                        before_k, before_v = ek_s[part - 1].at[part_chunks - 1], ev_s[part - 1].at[part_chunks - 1]
                    else:
                        before_k, before_v = ek_last, ev_last
                    kblk, vblk = _band_operands(before_k, ek_s[part].at[cj], before_v, ev_s[part].at[cj], kvh)
                    logits = _nt(_stack_pairs(q_s[part], sub, kvh), kblk) + bias_ref[sel, kvh]
                    p, _ = _softmax_stack(logits, sink_cols[kvh])
                    o = _nn(p.astype(BF16), vblk)
                    for pp in range(PAIRS_PER_KV):
                        o_s[part][sub, _pair_cols(kvh, pp)] = o[pp * CHUNK:(pp + 1) * CHUNK]
                return run

            def finish():
                rows = slice(part * part_rows, (part + 1) * part_rows)
                z = z_s[part][...]
                y = o_s[part][...] * (z * jax.nn.sigmoid(z))
                r1 = ALPHA * h1_s[part][...] + _nn(y.astype(BF16), wob_ref[...])
                xhat1, rstd1 = _ln_normalize(r1)
                diff = xhat1 * g1 + pb_ref[1:2, :] - tgt_ref[rows, :]
                loss_ref[...] += 0.5 * jnp.sum(jnp.mean(diff * diff, axis=-1, keepdims=True), axis=0, keepdims=True)
                dh2 = diff * (1.0 / D_MODEL)
                dg_ref[...] += jnp.sum(dh2 * xhat1, axis=0, keepdims=True)
                db_ref[...] += jnp.sum(dh2, axis=0, keepdims=True)
                dr1_ref[rows, :] = _ln_backward(dh2 * g1, xhat1, rstd1)

            return [stack(cj, kvh) for cj in range(part_chunks) for kvh in range(N_KV_HEADS)] + [finish]

        for run in project_steps(0):
            run()
        for part in range(n_parts):
            for run in _in_turns(attend_steps(part), project_steps(part + 1) if part + 1 < n_parts else []):
                run()
        ek_last[...] = ek_s[n_parts - 1][part_chunks - 1]
        ev_last[...] = ev_s[n_parts - 1][part_chunks - 1]

    vec = pl.BlockSpec((1, D_MODEL), lambda i: (0, 0))
    return pl.pallas_call(
        body, name="fwd_b", grid=(seq // tm,),
        out_shape=[jax.ShapeDtypeStruct((seq, B_WIDTH), BF16), jax.ShapeDtypeStruct((seq, B_WIDTH), F32),
                   jax.ShapeDtypeStruct((seq, 2 * KV_WIDTH), BF16), jax.ShapeDtypeStruct((seq, D_MODEL), F32),
                   jax.ShapeDtypeStruct((1, 1), F32), jax.ShapeDtypeStruct((1, D_MODEL), F32),
                   jax.ShapeDtypeStruct((1, D_MODEL), F32)],
        in_specs=[_rows(tm, D_MODEL), _rows(tm, D_MODEL), _resident(wb.shape), _resident(wkv.shape),
                  _resident(wob.shape), _resident(post_g.shape), _resident(post_b.shape), SMEM,
                  _resident(bias.shape)],
        out_specs=[_rows(tm, B_WIDTH), _rows(tm, B_WIDTH), _rows(tm, 2 * KV_WIDTH), _rows(tm, D_MODEL),
                   pl.BlockSpec((1, 1), lambda i: (0, 0)), vec, vec],
        scratch_shapes=[pltpu.VMEM((4, CHUNK, CHUNK), BF16), pltpu.VMEM((4, CHUNK, CHUNK), BF16)]
        + [pltpu.VMEM((part_rows, B_WIDTH), BF16)] * n_parts + [pltpu.VMEM((part_rows, B_WIDTH), F32)] * n_parts
        + [pltpu.VMEM((part_rows, D_MODEL), F32)] * n_parts + [pltpu.VMEM((part_rows, B_WIDTH), F32)] * n_parts
        + [pltpu.VMEM((part_chunks, 4, CHUNK, CHUNK), BF16)] * (2 * n_parts),
        compiler_params=_seq_params(),
    )(r0, target, wb, wkv, wob, post_g, post_b, sinks, bias)


def _bwd_b(r0, dr1, q, z, kv, wb, wkv, wob, post_g, post_b, sinks, bias):
    seq = r0.shape[0]
    tm = min(TOKENS_B_BWD, seq)
    n_chunks = tm // CHUNK
    n_blocks = seq // tm
    shard_cols = wb.shape[2]

    def body(r0_ref, dr1_ref, q_ref, z_ref, kv_ref, kvp_ref, wb_ref, wkv_ref, wob_ref, pg_ref, pb_ref, sink_ref,
             bias_ref, dr0_ref, dwb_out, dwkv_out, dwob_out, dsink_ref, dl_out, dpost_ref,
             dwb_acc, dwkv_acc, dwob_acc, dl_acc, dsink_acc, carry_ref, ek_ref, ev_ref, y_ref, dqz_ref, dkv_ref):
        step = pl.program_id(0)
        block = n_blocks - 1 - step

        @pl.when(step == 0)
        def _():
            dpost_ref[...] = jnp.zeros_like(dpost_ref)
            dwb_acc[...] = jnp.zeros_like(dwb_acc)
            dwkv_acc[...] = jnp.zeros_like(dwkv_acc)
            dwob_acc[...] = jnp.zeros_like(dwob_acc)
            dl_acc[...] = jnp.zeros_like(dl_acc)
            carry_ref[...] = jnp.zeros_like(carry_ref)
            dsink_acc[...] = jnp.zeros_like(dsink_acc)

        xhat0, rstd0 = _ln_normalize(r0_ref[...])
        h1b = (xhat0 * pg_ref[0:1, :] + pb_ref[0:1, :]).astype(BF16)
        dr1 = dr1_ref[...]
        dr1b = dr1.astype(BF16)
        dy = _nt(dr1b, wob_ref[...])
        sg, dsg = _silu_and_grad(z_ref[...])
        do_all = dy * sg
        dyz = dy * dsg

        for ci in range(n_chunks + 1):
            src = kvp_ref[...] if ci == 0 else kv_ref[(ci - 1) * CHUNK:ci * CHUNK, :]
            src = src.astype(F32)
            for i, e in enumerate(_expand_heads(src[:, :KV_WIDTH])):
                ek_ref[ci, i] = e
            for i, e in enumerate(_expand_heads(src[:, KV_WIDTH:])):
                ev_ref[ci, i] = e

        sink_cols = _sink_columns(sink_ref)
        for ci in reversed(range(n_chunks)):
            rows = slice(ci * CHUNK, (ci + 1) * CHUNK)
            sel = jnp.where(block * n_chunks + ci > 0, 1, 0)
            dk_band = jnp.zeros((2 * CHUNK, KV_WIDTH), F32)
            dv_band = jnp.zeros((2 * CHUNK, KV_WIDTH), F32)
            for kvh in range(N_KV_HEADS):
                kblk, vblk = _band_operands(ek_ref.at[ci], ek_ref.at[ci + 1], ev_ref.at[ci], ev_ref.at[ci + 1], kvh)
                qs = _stack_pairs(q_ref, rows, kvh)
                p, psink = _softmax_stack(_nt(qs, kblk) + bias_ref[sel, kvh], sink_cols[kvh])
                pb16 = p.astype(BF16)
                o = _nn(pb16, vblk)
                y = (o * _stack_pairs(sg, rows, kvh)).astype(BF16)
                dz = (o * _stack_pairs(dyz, rows, kvh)).astype(BF16)
                dob = _stack_pairs(do_all, rows, kvh).astype(BF16)
                dp = _nt(dob, vblk)
                dlogits = []
                for half in range(2):
                    hs = slice(half * 2 * CHUNK, (half + 1) * 2 * CHUNK)
                    delta = jnp.sum(dp[:, hs] * p[:, hs], axis=-1, keepdims=True)
                    dlogits.append(p[:, hs] * (dp[:, hs] - delta))
                    dsink_acc[2 * kvh + half] += psink[half] * delta
                dlog = jnp.concatenate(dlogits, axis=1)
                dl_acc[kvh] += dlog
                dsc = dlog.astype(BF16)
                dq = (_nn(dsc, kblk) * SCALE).astype(BF16)
                for pp in range(PAIRS_PER_KV):
                    blk = slice(pp * CHUNK, (pp + 1) * CHUNK)
                    cols = _pair_cols(kvh, pp)
                    y_ref[rows, cols] = y[blk]
                    dqz_ref[rows, cols] = dq[blk]
                    dqz_ref[rows, B_WIDTH + cols.start:B_WIDTH + cols.stop] = dz[blk]
                dk_band = dk_band + _fold_heads(_tn(dsc, qs), kvh)
                dv_band = dv_band + _fold_heads(_tn(pb16, dob), kvh)
            band = jnp.concatenate([dk_band, dv_band], axis=1)
            dkv_ref[rows, :] = (band[CHUNK:] + carry_ref[...]).astype(BF16)
            carry_ref[...] = band[:CHUNK]

        dqz = dqz_ref[...]
        dkv = dkv_ref[...]
        dh1 = ALPHA * dr1 + _nt(dkv, wkv_ref[...])
        for k in range(N_DEV):
            part = dqz[:, k * shard_cols:(k + 1) * shard_cols]
            dh1 = dh1 + _nt(part, wb_ref[k])
            dwb_acc[k] += _tn(h1b, part)
        dpost_ref[0:1, :] += jnp.sum(dh1 * xhat0, axis=0, keepdims=True)
        dpost_ref[1:2, :] += jnp.sum(dh1, axis=0, keepdims=True)
        dr0_ref[...] = _ln_backward(dh1 * pg_ref[0:1, :], xhat0, rstd0)
        dwkv_acc[...] += _tn(h1b, dkv)
        dwob_acc[...] += _tn(y_ref[...], dr1b)

        @pl.when(step == n_blocks - 1)
        def _():
            pltpu.sync_copy(dwb_acc, dwb_out)
            pltpu.sync_copy(dwkv_acc, dwkv_out)
            pltpu.sync_copy(dwob_acc, dwob_out)
            pltpu.sync_copy(dl_acc, dl_out)
            lane = lax.broadcasted_iota(jnp.int32, (1, CHUNK), 1)
            dsink = jnp.zeros((1, CHUNK), F32)
            for kvh in range(N_KV_HEADS):
                for half in range(2):
                    for pp in range(PAIRS_PER_KV):
                        head = 2 * (kvh * PAIRS_PER_KV + pp) + half
                        tot = jnp.sum(dsink_acc[2 * kvh + half, pp * CHUNK:(pp + 1) * CHUNK, :], axis=0, keepdims=True)
                        dsink = dsink - jnp.where(lane == head, tot, 0.0)
            dsink_ref[...] = dsink

    rev = functools.partial(_rows, tm, reverse_of=n_blocks)
    prev_chunk = pl.BlockSpec(
        (CHUNK, 2 * KV_WIDTH), lambda i: (jnp.maximum((n_blocks - 1 - i) * n_chunks - 1, 0), 0))
    return pl.pallas_call(
        body, name="bwd_b", grid=(n_blocks,),
        out_shape=[jax.ShapeDtypeStruct((seq, D_MODEL), F32), jax.ShapeDtypeStruct(wb.shape, F32),
                   jax.ShapeDtypeStruct(wkv.shape, F32), jax.ShapeDtypeStruct(wob.shape, F32),
                   jax.ShapeDtypeStruct((1, CHUNK), F32),
                   jax.ShapeDtypeStruct((N_KV_HEADS, STACK, 4 * CHUNK), F32), jax.ShapeDtypeStruct((2, D_MODEL), F32)],
        in_specs=[rev(D_MODEL), rev(D_MODEL), rev(B_WIDTH), rev(B_WIDTH), rev(2 * KV_WIDTH), prev_chunk,
                  _resident(wb.shape), _resident(wkv.shape), _resident(wob.shape), _resident(post_g.shape),
                  _resident(post_b.shape), SMEM, _resident(bias.shape)],
        out_specs=[rev(D_MODEL), ANY, ANY, ANY, pl.BlockSpec((1, CHUNK), lambda i: (0, 0)), ANY,
                   pl.BlockSpec((2, D_MODEL), lambda i: (0, 0))],
        scratch_shapes=[pltpu.VMEM(wb.shape, F32), pltpu.VMEM(wkv.shape, F32), pltpu.VMEM(wob.shape, F32),
                        pltpu.VMEM((N_KV_HEADS, STACK, 4 * CHUNK), F32), pltpu.VMEM((2 * N_KV_HEADS, STACK, 1), F32),
                        pltpu.VMEM((CHUNK, 2 * KV_WIDTH), F32),
                        pltpu.VMEM((n_chunks + 1, 4, CHUNK, CHUNK), BF16),
                        pltpu.VMEM((n_chunks + 1, 4, CHUNK, CHUNK), BF16),
                        pltpu.VMEM((tm, B_WIDTH), BF16), pltpu.VMEM((tm, 2 * B_WIDTH), BF16),
                        pltpu.VMEM((tm, 2 * KV_WIDTH), BF16)],
        compiler_params=_seq_params(),
    )(r0, dr1, q, z, kv, kv, wb, wkv, wob, post_g, post_b, sinks, bias)


def _bwd_a(dr0, p, ln_gb, w_spatial, bsp_t, woa, exchange):
    seq = dr0.shape[0]
    tm = min(TOKENS_A, seq)
    n_chunks = tm // CHUNK
    n_blocks = seq // tm

    def body(dr0_ref, p_ref, ln_ref, ws_ref, bsp_ref, woa_ref, *refs):
        ex_in, refs = refs[:exchange.n], refs[exchange.n:]
        (dp_ref, dwoa_out, dws_ref, dbsp_ref, dln_ref), refs = refs[:5], refs[5:]
        ex_out, refs = refs[:exchange.n], refs[exchange.n:]
        (dwoa_acc, s_ref, ds_ref, dvn_ref), ex_sems = refs[:4], refs[4:]
        step = pl.program_id(0)

        @pl.when(step == 0)
        def _():
            exchange.start(ex_in, ex_out, ex_sems)
            dwoa_acc[...] = jnp.zeros_like(dwoa_acc)
            dws_ref[...] = jnp.zeros_like(dws_ref)
            dbsp_ref[...] = jnp.zeros_like(dbsp_ref)
            dln_ref[...] = jnp.zeros_like(dln_ref)

        dr0b = dr0_ref[...].astype(BF16)

        vhat, rstd_v = _ln_normalize(p_ref[:, A_WIDTH:2 * A_WIDTH])
        ln_g = ln_ref[0:1, :]
        vn = (vhat * ln_g + ln_ref[1:2, :]).astype(BF16)
        _spatial_gate(ws_ref, bsp_ref, vn, s_ref, n_chunks)
        u = p_ref[:, :A_WIDTH]
        s = s_ref[...]
        sg, dsg = _silu_and_grad(p_ref[:, 2 * A_WIDTH:])
        gate = s * sg
        dwoa_acc[...] += _tn((u * gate).astype(BF16), dr0b)
        dy = _nt(dr0b, woa_ref[...])
        dyu = dy * u
        dp_ref[:, :A_WIDTH] = (dy * gate).astype(BF16)
        dp_ref[:, 2 * A_WIDTH:] = (dyu * (s * dsg)).astype(BF16)
        ds_ref[...] = dyu * sg

        lane = lax.broadcasted_iota(jnp.int32, (CHUNK, CHUNK), 1)
        dbsp = jnp.zeros((CHUNK, CHUNK), F32)
        for g in range(A_GROUPS):
            w = _masked_spatial(ws_ref, g)
            cols = slice(g * A_GROUP_DIM, (g + 1) * A_GROUP_DIM)
            dws = jnp.zeros((CHUNK, CHUNK), F32)
            for ci in range(n_chunks):
                rows = slice(ci * CHUNK, (ci + 1) * CHUNK)
                ds = ds_ref[rows, cols]
                dsb = ds.astype(BF16)
                dvn_ref[rows, cols] = _tn(w, dsb)
                dws = dws + _nt(dsb, vn[rows, cols])
                dbsp = dbsp + jnp.where(lane == g, jnp.sum(ds, axis=-1, keepdims=True), 0.0)
            dws_ref[g] += dws
        dbsp_ref[...] += dbsp

        dvn = dvn_ref[...]
        dln_ref[0:1, :] += jnp.sum(dvn * vhat, axis=0, keepdims=True)
        dln_ref[1:2, :] += jnp.sum(dvn, axis=0, keepdims=True)
        dp_ref[:, A_WIDTH:2 * A_WIDTH] = _ln_backward(dvn * ln_g, vhat, rstd_v).astype(BF16)

        @pl.when(step == n_blocks - 1)
        def _():
            t = lax.broadcasted_iota(jnp.int32, (CHUNK, CHUNK), 0)
            for g in range(A_GROUPS):
                dws_ref[g] = jnp.where(lane <= t, dws_ref[g], 0.0)
            pltpu.sync_copy(dwoa_acc, dwoa_out)
            exchange.wait(ex_in, ex_out, ex_sems)

    def fixed(shape):
        nd = len(shape)
        return pl.BlockSpec(shape, lambda i: (0,) * nd)

    return pl.pallas_call(
        body, name="bwd_a", grid=(n_blocks,),
        out_shape=[jax.ShapeDtypeStruct((seq, 3 * A_WIDTH), BF16),
                   jax.ShapeDtypeStruct(woa.shape, F32), jax.ShapeDtypeStruct(w_spatial.shape, F32),
                   jax.ShapeDtypeStruct((CHUNK, CHUNK), F32), jax.ShapeDtypeStruct((2, A_WIDTH), F32)]
        + exchange.out_shape,
        in_specs=[_rows(tm, D_MODEL), _rows(tm, 3 * A_WIDTH), _resident(ln_gb.shape),
                  _resident(w_spatial.shape), _resident(bsp_t.shape), _resident(woa.shape)] + [ANY] * exchange.n,
        out_specs=[_rows(tm, 3 * A_WIDTH), ANY, fixed(w_spatial.shape),
                   fixed((CHUNK, CHUNK)), fixed((2, A_WIDTH))] + [ANY] * exchange.n,
        scratch_shapes=[pltpu.VMEM(woa.shape, F32), pltpu.VMEM((tm, A_WIDTH), F32),
                        pltpu.VMEM((tm, A_WIDTH), F32), pltpu.VMEM((tm, A_WIDTH), F32)] + exchange.scratch_shapes,
        compiler_params=_seq_params(),
    )(dr0, p, ln_gb, w_spatial, bsp_t, woa, *exchange.inputs)


def _input_grad_blocks(seq):
    tm = min(TOKENS_DX, seq // 8)
    return tm, seq // tm


def _input_grad(dp, dr0, wa, first, n_blocks, earlier, exchange):
    seq = dr0.shape[0]
    tm, _ = _input_grad_blocks(seq)
    shard_cols = wa.shape[2]
    n_own = 3 if earlier is None else 4

    def body(dp_ref, dr0_ref, wa_ref, *refs):
        refs = refs[n_own - 3:]
        ex_in, dx_ref, ex_out, ex_sems = (refs[:exchange.n], refs[exchange.n], refs[exchange.n + 1:2 * exchange.n + 1],
                                          refs[2 * exchange.n + 1:])
        step = pl.program_id(0)

        @pl.when(step == 0)
        def _():
            exchange.start(ex_in, ex_out, ex_sems)

        dx = ALPHA * dr0_ref[...]
        for k in range(N_DEV):
            dx = dx + _nt(dp_ref[:, k * shard_cols:(k + 1) * shard_cols], wa_ref[k])
        dx_ref[...] = dx

        @pl.when(step == n_blocks - 1)
        def _():
            exchange.wait(ex_in, ex_out, ex_sems)

    def rows(width):
        return pl.BlockSpec((tm, width), lambda i: (first + i, 0))

    return pl.pallas_call(
        body, name=f"input_grad_{first}", grid=(n_blocks,),
        out_shape=[jax.ShapeDtypeStruct((seq, D_MODEL), F32)] + exchange.out_shape,
        in_specs=[rows(3 * A_WIDTH), rows(D_MODEL), _resident(wa.shape)] + [ANY] * (n_own - 3 + exchange.n),
        out_specs=[rows(D_MODEL)] + [ANY] * exchange.n,
        scratch_shapes=exchange.scratch_shapes,
        input_output_aliases={} if earlier is None else {3: 0},
        compiler_params=_seq_params(),
    )(dp, dr0, wa, *(() if earlier is None else (earlier,)), *exchange.inputs)


def _in_proj_weight_grad(place, xb, dp, shard_cols, exchange, to_sibling=True):
    seq = xb.shape[0]
    tk = min(TOKENS_WGRAD, seq)
    n_k = seq // tk
    n = exchange.n

    def shard_of(turn, place_ref):
        if not to_sibling:
            return turn
        return 2 * (turn % 4) + jnp.where(turn < 4, 1 - place_ref[1], place_ref[1])

    def body(place_ref, x_ref, dp_ref, *refs):
        ex_in, out_ref, landed_ref, ex_out = refs[:n], refs[n], refs[n + 1], refs[n + 2:2 * n + 2]
        ex_sems, stage_ref, send_sems, recv_sems = refs[2 * n + 2:-3], refs[-3], refs[-2], refs[-1]
        turn, step = pl.program_id(0), pl.program_id(1)

        def send(chip):
            x, y, c = lax.axis_index("x"), lax.axis_index("y"), lax.axis_index("c")
            return pltpu.make_async_remote_copy(
                src_ref=stage_ref.at[chip % 2], dst_ref=landed_ref.at[chip, 0], send_sem=send_sems.at[chip],
                recv_sem=recv_sems.at[chip], device_id=(x, y, 1 - c), device_id_type=MESH)

        @pl.when((turn == 0) & (step == 0))
        def _():
            exchange.start(ex_in, ex_out, ex_sems)

        @pl.when(step == 0)
        def _():
            out_ref[...] = jnp.zeros_like(out_ref)

        out_ref[...] += _tn(x_ref[...], dp_ref[...])

        if to_sibling:
            @pl.when((turn >= 2) & (turn < 4) & (step == n_k - 1))
            def _():
                send(turn - 2).wait_send()

            @pl.when((turn < 4) & (step == n_k - 1))
            def _():
                stage_ref[turn % 2] = out_ref[...]
                send(turn).start()

        @pl.when((turn == N_DEV - 1) & (step == n_k - 1))
        def _():
            if to_sibling:
                send(2).wait_send()
                send(3).wait_send()
                for chip in range(4):
                    send(chip).wait_recv()
            exchange.wait(ex_in, ex_out, ex_sems)

    outs = pl.pallas_call(
        body, name="in_proj_weight_grad",
        out_shape=[jax.ShapeDtypeStruct((N_DEV, D_MODEL, shard_cols), F32),
                   jax.ShapeDtypeStruct((4, 1, D_MODEL, shard_cols), F32)] + exchange.out_shape,
        grid_spec=pltpu.PrefetchScalarGridSpec(
            num_scalar_prefetch=1, grid=(N_DEV, n_k),
            in_specs=[pl.BlockSpec((tk, D_MODEL), lambda j, k, pr: (k, 0)),
                      pl.BlockSpec((tk, shard_cols), lambda j, k, pr: (k, shard_of(j, pr)))] + [ANY] * n,
            out_specs=[pl.BlockSpec((None, D_MODEL, shard_cols), lambda j, k, pr: (shard_of(j, pr), 0, 0)), ANY]
            + [ANY] * n,
            scratch_shapes=exchange.scratch_shapes + [
                pltpu.VMEM((2, D_MODEL, shard_cols), F32), pltpu.SemaphoreType.DMA((4,)), pltpu.SemaphoreType.DMA((4,))]),
        compiler_params=pltpu.CompilerParams(dimension_semantics=("arbitrary", "arbitrary"),
                                             vmem_limit_bytes=VMEM_LIMIT_BYTES),
    )(place, xb, dp, *exchange.inputs)
    return outs if to_sibling else [outs[0], *outs[2:]]


class _Exchange:
    def __init__(self, pair=(), chip=(), gather=()):
        self.pair, self.chip, self.gather = tuple(pair), tuple(chip), tuple(gather)
        self.inputs = self.pair + self.chip + self.gather
        self.n = len(self.inputs)
        self.out_shape = [jax.ShapeDtypeStruct((4, 1) + g.shape[2:], F32) for g in self.pair]
        self.out_shape += [jax.ShapeDtypeStruct((3,) + s.shape[1:], s.dtype) for s in self.chip]
        self.out_shape += [jax.ShapeDtypeStruct((N_DEV,) + g.shape, g.dtype) for g in self.gather]
        self.scratch_shapes = [pltpu.SemaphoreType.DMA((max(len(self.pair), 1),)) for _ in range(2)]
        self.scratch_shapes += [pltpu.SemaphoreType.DMA((max(len(self.chip), 1), 3)) for _ in range(2)]
        self.scratch_shapes += [pltpu.SemaphoreType.DMA((max(len(self.gather), 1), N_DEV)) for _ in range(2)]

    def _copies(self, ins, outs, sems):
        if not self.n:
            return []
        pair_send, pair_recv, chip_send, chip_recv, gather_send, gather_recv = sems
        x, y, c = lax.axis_index("x"), lax.axis_index("y"), lax.axis_index("c")
        copies = []
        for a in range(len(self.gather)):
            src = ins[len(self.pair) + len(self.chip) + a]
            dst = outs[len(self.pair) + len(self.chip) + a].at[4 * x + 2 * y + c]
            copies.append(pltpu.make_async_copy(src, dst, gather_send.at[a, 0]))
            for r in range(1, N_DEV):
                to = (1 - x if r & 4 else x, 1 - y if r & 2 else y, 1 - c if r & 1 else c)
                copies.append(pltpu.make_async_remote_copy(
                    src_ref=src, dst_ref=dst, send_sem=gather_send.at[a, r], recv_sem=gather_recv.at[a, r],
                    device_id=to, device_id_type=MESH))
        for a in range(len(self.pair)):
            copies.append(pltpu.make_async_remote_copy(
                src_ref=ins[a].at[:, pl.ds(1 - c, 1)], dst_ref=outs[a],
                send_sem=pair_send.at[a], recv_sem=pair_recv.at[a], device_id=(x, y, 1 - c), device_id_type=MESH))
        for a in range(len(self.chip)):
            src, dst = ins[len(self.pair) + a], outs[len(self.pair) + a]
            for k, (px, py) in enumerate([(1 - x, y), (x, 1 - y), (1 - x, 1 - y)]):
                copies.append(pltpu.make_async_remote_copy(
                    src_ref=src.at[2 * px + py], dst_ref=dst.at[k], send_sem=chip_send.at[a, k],
                    recv_sem=chip_recv.at[a, k], device_id=(px, py, c), device_id_type=MESH))
        return copies

    def start(self, ins, outs, sems):
        for cp in self._copies(ins, outs, sems):
            cp.start()

    def wait(self, ins, outs, sems):
        for cp in self._copies(ins, outs, sems):
            cp.wait()


def _pair_sum(place, grads, landeds, dtypes, finish=None):
    n = len(grads)

    def body(place_ref, *refs):
        ins, outs = refs[:len(in_specs)], refs[len(in_specs):]
        for a in range(n):
            outs[a][...] = (ins[2 * a][...] + ins[2 * a + 1][...]).astype(dtypes[a])
        if finish is not None:
            grad_ref, landed_ref, got_ref = ins[2 * n:]

            @pl.when(pl.program_id(0) == 0)
            def _():
                outs[n][...] = (((grad_ref[...] + landed_ref[...]) + got_ref[0]) + got_ref[1]) + got_ref[2]

    in_specs, out_specs, operands = [], [], []
    for g, l in zip(grads, landeds):
        rows, cols = g.shape[2:]
        in_specs += [pl.BlockSpec((None, None, rows, cols), lambda k, pr: (k, pr[1], 0, 0)),
                     pl.BlockSpec((None, None, rows, cols), lambda k, pr: (k, 0, 0, 0))]
        out_specs.append(pl.BlockSpec((None, rows, cols), lambda k, pr: (k, 0, 0)))
        operands += [g, l]
    out_shape = [jax.ShapeDtypeStruct((4,) + g.shape[2:], dt) for g, dt in zip(grads, dtypes)]
    if finish is not None:
        rows, cols = finish[2].shape[1:]
        in_specs += [pl.BlockSpec((None, None, rows, cols), lambda k, pr: (pr[0], pr[1], 0, 0)),
                     pl.BlockSpec((None, None, rows, cols), lambda k, pr: (pr[0], 0, 0, 0)),
                     pl.BlockSpec((3, rows, cols), lambda k, pr: (0, 0, 0))]
        out_specs.append(pl.BlockSpec((rows, cols), lambda k, pr: (0, 0)))
        out_shape.append(jax.ShapeDtypeStruct((rows, cols), F32))
        operands += list(finish)
    return pl.pallas_call(
        body, name="pair_sum", out_shape=out_shape,
        grid_spec=pltpu.PrefetchScalarGridSpec(num_scalar_prefetch=1, grid=(4,), in_specs=in_specs,
                                               out_specs=out_specs),
        compiler_params=pltpu.CompilerParams(dimension_semantics=("arbitrary",), vmem_limit_bytes=VMEM_LIMIT_BYTES),
    )(place, *operands)


def _adamw_math(w, g, m, v):
    m = ADAM_B1 * m + (1.0 - ADAM_B1) * g
    v = ADAM_B2 * v + (1.0 - ADAM_B2) * (g * g)
    m_hat = m / (1.0 - ADAM_B1 ** ADAM_STEP)
    v_hat = v / (1.0 - ADAM_B2 ** ADAM_STEP)
    delta = -ADAM_LR * (m_hat / (jnp.sqrt(v_hat) + ADAM_EPS) + ADAM_WD * w)
    return delta, m, v


def _adamw_shards(place, items, steps, small=(), packed=None, scalar_at=None):
    n, n_small = len(items), len(small)
    n_in = 6 * n + (1 + 3 * n_small if n_small else 0)

    def body(place_ref, *refs):
        ins, small_ins = refs[:6 * n], refs[6 * n + 1:n_in]
        outs, small_outs = refs[n_in:n_in + 4 * n], refs[n_in + 4 * n:]
        for i in range(n):
            a_ref, b_ref, got_ref, w_ref, m_ref, v_ref = ins[6 * i:6 * i + 6]
            g = a_ref[...] + b_ref[...]
            for k in range(3):
                g = g + got_ref[k].astype(F32)
            outs[4 * i][...] = g
            outs[4 * i + 1][...], outs[4 * i + 2][...], outs[4 * i + 3][...] = _adamw_math(
                w_ref[...], g, m_ref[...], v_ref[...])

        if n_small:
            packed_ref = refs[6 * n]

            @pl.when(pl.program_id(0) == 0)
            def _():
                me = 2 * place_ref[0] + place_ref[1]
                small_outs[4 * n_small][...] = packed_ref[scalar_at:scalar_at + 1, 0:1]
                for i, (w, (at, mine), _, _) in enumerate(small):
                    w_ref, m_ref, v_ref = small_ins[3 * i:3 * i + 3]
                    g_ref, d_ref, new_m_ref, new_v_ref = small_outs[4 * i:4 * i + 4]
                    rows, cols = w.shape
                    if cols <= CHUNK:
                        g_ref[...] = packed_ref[at:at + rows, :cols]
                    else:
                        per_row = cols // CHUNK
                        start = at + me * (rows * per_row) if mine else at
                        for r in range(rows * per_row):
                            lane = (r % per_row) * CHUNK
                            row = r // per_row
                            g_ref[row:row + 1, lane:lane + CHUNK] = packed_ref[pl.ds(start + r, 1), :]
                    d_ref[...], new_m_ref[...], new_v_ref[...] = _adamw_math(
                        w_ref[...], g_ref[...], m_ref[...], v_ref[...])

    def whole(a):
        return pl.BlockSpec(a.shape, lambda i, pr: (0, 0))

    in_specs, out_specs, out_shape, operands = [], [], [], []
    for grad, landed, got, w, m, v in items:
        rows, cols = w.shape
        tr = rows // steps
        assert tr * steps == rows and tr % 16 == 0, (rows, steps)
        spec = pl.BlockSpec((tr, cols), lambda i, pr: (i, 0))
        in_specs += [pl.BlockSpec((None, None, tr, cols), lambda i, pr: (pr[0], pr[1], i, 0)),
                     pl.BlockSpec((None, None, tr, cols), lambda i, pr: (pr[0], 0, i, 0)),
                     pl.BlockSpec((3, tr, cols), lambda i, pr: (0, i, 0)), spec, spec, spec]
        out_specs += [spec] * 4
        out_shape += [jax.ShapeDtypeStruct((rows, cols), F32)] * 4
        operands += [grad, landed, got, w, m, v]
    if n_small:
        in_specs.append(whole(packed))
        operands.append(packed)
    for w, _, m, v in small:
        in_specs += [whole(w)] * 3
        out_specs += [whole(w)] * 4
        out_shape += [jax.ShapeDtypeStruct(w.shape, F32)] * 4
        operands += [w, m, v]
    if n_small:
        out_specs.append(pl.BlockSpec((1, 1), lambda i, pr: (0, 0)))
        out_shape.append(jax.ShapeDtypeStruct((1, 1), F32))
    outs = pl.pallas_call(
        body, name="adamw_shards", out_shape=out_shape,
        grid_spec=pltpu.PrefetchScalarGridSpec(num_scalar_prefetch=1, grid=(steps,), in_specs=in_specs,
                                               out_specs=out_specs),
        compiler_params=pltpu.CompilerParams(dimension_semantics=("arbitrary",), vmem_limit_bytes=VMEM_LIMIT_BYTES),
    )(place, *operands)
    small_outs = outs[4 * n:]
    return ([outs[4 * i:4 * i + 4] for i in range(n)],
            [tuple(small_outs[4 * i:4 * i + 4]) for i in range(n_small)], small_outs[4 * n_small:])


def _pack_small(parts):
    rows = []
    for a in parts:
        r, c = a.shape
        rows.append(jnp.pad(a, ((0, (-r) % 8), (0, CHUNK - c))))
    flat = jnp.concatenate(rows, axis=0)
    return jnp.pad(flat, ((0, N_DEV * SMALL_ROWS - flat.shape[0]), (0, 0))).reshape(N_DEV, SMALL_ROWS, CHUNK)


def _small_rows(shapes):
    out, at = [], 0
    for r, _ in shapes:
        out.append(at)
        at += r + (-r) % 8
    return out


def kernel(x, w_in_a, sgu_ln_g, sgu_ln_b, w_spatial, b_spatial, w_out_a, w_kv, w_in_b, attn_sinks, rel_bias, w_out_b, post_ln_g, post_ln_b, loss_target, m_w_in_a, m_sgu_ln_g, m_sgu_ln_b, m_w_spatial, m_b_spatial, m_w_out_a, m_w_kv, m_w_in_b, m_attn_sinks, m_rel_bias, m_w_out_b, m_post_ln_g, m_post_ln_b, v_w_in_a, v_sgu_ln_g, v_sgu_ln_b, v_w_spatial, v_b_spatial, v_w_out_a, v_w_kv, v_w_in_b, v_attn_sinks, v_rel_bias, v_w_out_b, v_post_ln_g, v_post_ln_b):
    seq = x.shape[1]
    bucket =jnp.asarray(_band_tables())

    ln_shard = jnp.concatenate([sgu_ln_g, sgu_ln_b], axis=0)
    x2, tgt = x[0], loss_target[0]
    order = jnp.stack([2 * px + py for px, py, _ in _use_order()[::2]]).astype(jnp.int32)
    p, wa, woa, ln_all, xb = _gather_in_proj(order, x2, w_in_a[0], (w_out_a[0],), ln_shard)
    woa = woa.reshape(A_WIDTH, D_MODEL)
    ln_gb = jnp.transpose(ln_all, (1, 0, 2)).reshape(2, A_WIDTH)
    ws = w_spatial[0]
    bsp_t = jnp.pad(b_spatial[0].T, ((0, 0), (0, CHUNK - A_GROUPS)))
    rel_t = rel_bias.T
    bias = _bias_table(rel_t, bucket)

    r0, wkv, wb, wob = _fwd_a(x2, p, ln_gb, ws, bsp_t, woa, (w_kv, w_in_b[0], w_out_b[0]))
    wkv = wkv.reshape(D_MODEL, 2 * KV_WIDTH)
    wob = wob.reshape(B_WIDTH, D_MODEL)
    q, z, kv, dr1, loss, dg1, db1 = _fwd_b(r0, tgt, wb, wkv, wob, post_ln_g, post_ln_b, attn_sinks, bias)
    dr0, dwb, dwkv, dwob, dsink, dlog, dpost0 = _bwd_b(r0, dr1, q, z, kv, wb, wkv, wob, post_ln_g, post_ln_b,
                                                       attn_sinks, bias)
    drel = _rel_bias_grad(dlog, bucket)

    place = jnp.stack([2 * lax.axis_index("x") + lax.axis_index("y"), lax.axis_index("c")]).astype(jnp.int32)

    def by_device(g):
        return g.reshape((4, 2, g.shape[0] // N_DEV) + g.shape[1:]) if g.ndim == 2 else g.reshape((4, 2) + g.shape[1:])

    dwb, dwkv, dwob = by_device(dwb), by_device(dwkv), by_device(dwob)
    dp, dwoa, dws, dbsp_t, dln, l_wb, l_wkv, l_wob = _bwd_a(dr0, p, ln_gb, ws, bsp_t, woa,
                                                            _Exchange(pair=(dwb, dwkv, dwob)))
    dwoa = by_device(dwoa)
    small_shapes = [(A_GROUPS * CHUNK, CHUNK), (A_GROUPS, CHUNK), (1, N_Q_HEADS), (N_Q_HEADS, REL_BUCKETS),
                    (16, CHUNK), (16, CHUNK), (16, CHUNK), (16, CHUNK), (1, 1)]
    small = by_device(_pack_small([
        dws.reshape(A_GROUPS * CHUNK, CHUNK), dbsp_t[:, :A_GROUPS].T, dsink[:, :N_Q_HEADS], drel,
        jnp.concatenate([dpost0[0:1], dg1], axis=0).reshape(16, CHUNK),
        jnp.concatenate([dpost0[1:2], db1], axis=0).reshape(16, CHUNK),
        dln[0].reshape(16, CHUNK), dln[1].reshape(16, CHUNK), loss]))
    s_wb, s_wkv, s_wob = _pair_sum(place, (dwb, dwkv, dwob), (l_wb, l_wkv, l_wob), (BF16,) * 3)
    dwa, l_wa, l_woa, l_small, g_wb, g_wkv, g_wob = _in_proj_weight_grad(
        place, xb, dp, wa.shape[2], _Exchange(pair=(dwoa, small), chip=(s_wb, s_wkv, s_wob)))
    dwa = by_device(dwa)
    s_woa, s_small, s_wa = _pair_sum(place, (dwoa, small, dwa), (l_woa, l_small, l_wa), (BF16, F32, BF16))
    _, blocks = _input_grad_blocks(seq)
    cut = 3 * blocks // 4
    dx, g_woa, g_small, g_wa = _input_grad(dp, dr0, wa, 0, cut, None, _Exchange(chip=(s_woa, s_small, s_wa)))
    my_small, = _pair_sum(place, (), (), (), finish=(small, l_small, g_small))
    dx, small_all = _input_grad(dp, dr0, wa, cut, blocks - cut, dx, _Exchange(gather=(my_small,)))
    full = [dwa, dwoa, dwkv, dwb, dwob]
    landed = [l_wa, l_woa, l_wkv, l_wb, l_wob]
    got = [g_wa, g_woa, g_wkv, g_wb, g_wob]

    big = [(w_in_a, m_w_in_a, v_w_in_a), (w_out_a, m_w_out_a, v_w_out_a), (w_kv, m_w_kv, v_w_kv),
           (w_in_b, m_w_in_b, v_w_in_b), (w_out_b, m_w_out_b, v_w_out_b)]
    items = [(full[i], landed[i], got[i]) + tuple(a.reshape(got[i].shape[1:]) for a in big[i]) for i in range(5)]

    small_all = small_all.reshape(N_DEV * SMALL_ROWS, CHUNK)
    at_ws, at_bsp, at_sink, at_rel, at_pg, at_pb, at_lng, at_lnb, at_loss = _small_rows(small_shapes)
    small_params = [(sgu_ln_g, (at_lng, True), m_sgu_ln_g, v_sgu_ln_g),
                    (sgu_ln_b, (at_lnb, True), m_sgu_ln_b, v_sgu_ln_b),
                    (w_spatial, (at_ws, False), m_w_spatial, v_w_spatial),
                    (b_spatial, (at_bsp, False), m_b_spatial, v_b_spatial),
                    (attn_sinks, (at_sink, False), m_attn_sinks, v_attn_sinks),
                    (rel_t, (at_rel, False), m_rel_bias.T, v_rel_bias.T),
                    (post_ln_g, (at_pg, False), m_post_ln_g, v_post_ln_g),
                    (post_ln_b, (at_pb, False), m_post_ln_b, v_post_ln_b)]
    flat2 = [(w.reshape(-1, w.shape[-1]), at, m.reshape(-1, m.shape[-1]), v.reshape(-1, v.shape[-1]))
             for w, at, m, v in small_params]
    res_in, _, _ = _adamw_shards(place, items[:1], 4)
    res_rest, small_res, (loss_all,) = _adamw_shards(place, items[1:], 2, flat2, small_all, at_loss)
    big_out = [[r.reshape(big[i][0].shape) for r in res] for i, res in enumerate(res_in + res_rest)]
    small_out = [tuple(r.reshape(p[0].shape) for r in res) for res, p in zip(small_res, small_params)]

    order = [("big", 0), ("small", 0), ("small", 1), ("small", 2), ("small", 3), ("big", 1), ("big", 2), ("big", 3),
             ("small", 4), ("small", 5), ("big", 4), ("small", 6), ("small", 7)]
    grads, deltas, new_m, new_v = [], [], [], []
    for kind, i in order:
        if kind == "big":
            g, d, m, v = big_out[i]
        else:
            g, d, m, v = small_out[i]
            if small_params[i][0] is rel_t:
                g, d, m, v = g.T, d.T, m.T, v.T
        grads.append(g)
        deltas.append(d)
        new_m.append(m)
        new_v.append(v)
    return (loss_all.reshape(()), dx.reshape(x.shape), *grads, *deltas, *new_m, *new_v)
```

```python
import functools

import jax
import jax.numpy as jnp
import numpy as np
from jax import lax
from jax.experimental import pallas as pl
from jax.experimental.pallas import tpu as pltpu

F32 = jnp.float32
BF16 = jnp.bfloat16

D_MODEL = 1024
A_WIDTH = 2048
A_GROUPS = 8
A_GROUP_DIM = A_WIDTH // A_GROUPS
CHUNK = 128
HEAD_DIM = 64
N_Q_HEADS = 16
N_KV_HEADS = 2
PAIRS_PER_KV = N_Q_HEADS // N_KV_HEADS // 2
B_WIDTH = N_Q_HEADS * HEAD_DIM
KV_WIDTH = N_KV_HEADS * HEAD_DIM
REL_BUCKETS = 32
REL_MAX_DIST = 128
DEPTH = 2
ALPHA = (2.0 * DEPTH) ** 0.25
LN_EPS = 1e-5
NEG_INF = -1e30
SCALE = HEAD_DIM ** -0.5

ADAM_LR = 0.001
ADAM_B1 = 0.9
ADAM_B2 = 0.999
ADAM_EPS = 1e-08
ADAM_WD = 0.01
ADAM_STEP = 10

N_DEV = 8
MESH = pl.DeviceIdType.MESH
TOKENS_IN_PROJ = 1024
TOKENS_WGRAD = 4096
TOKENS_A = 256
TOKENS_A_FWD = 256
P_RING = 3
TOKENS_B = 512
PART_ROWS_B = 256
TOKENS_B_BWD = 256
TOKENS_DX = 512
VMEM_LIMIT_BYTES = 60 * 1024 * 1024
SMALL_ROWS = 144

ANY = pl.BlockSpec(memory_space=pl.ANY)
VMEM = pl.BlockSpec(memory_space=pltpu.VMEM)
SMEM = pl.BlockSpec(memory_space=pltpu.SMEM)


def _nn(a, b):
    return jnp.dot(a, b, preferred_element_type=F32)


def _nt(a, b):
    return lax.dot_general(a, b, (((1,), (1,)), ((), ())), preferred_element_type=F32)


def _tn(a, b):
    return lax.dot_general(a, b, (((0,), (0,)), ((), ())), preferred_element_type=F32)


def _resident(shape):
    nd = len(shape)
    return pl.BlockSpec(shape, lambda i: (0,) * nd, pipeline_mode=pl.Buffered(1))


def _rows(tm, width, reverse_of=None):
    if reverse_of is None:
        return pl.BlockSpec((tm, width), lambda i: (i, 0))
    return pl.BlockSpec((tm, width), lambda i: (reverse_of - 1 - i, 0))


def _seq_params():
    return pltpu.CompilerParams(dimension_semantics=("arbitrary",), vmem_limit_bytes=VMEM_LIMIT_BYTES)


def _in_turns(first, second):
    merged = [(i / max(len(first), 1), 0, f) for i, f in enumerate(first)]
    merged += [((i + 0.5) / max(len(second), 1), 1, f) for i, f in enumerate(second)]
    return [f for _, _, f in sorted(merged, key=lambda t: t[:2])]


def _ln_normalize(r):
    mu = jnp.mean(r, axis=-1, keepdims=True)
    xc = r - mu
    var = jnp.mean(xc * xc, axis=-1, keepdims=True)
    rstd = lax.rsqrt(var + LN_EPS)
    return xc * rstd, rstd


def _ln_backward(dxhat, xhat, rstd):
    m1 = jnp.mean(dxhat, axis=-1, keepdims=True)
    m2 = jnp.mean(dxhat * xhat, axis=-1, keepdims=True)
    return rstd * (dxhat - m1 - xhat * m2)


def _silu_and_grad(z):
    sig = jax.nn.sigmoid(z)
    silu = z * sig
    return silu, sig + silu * (1.0 - sig)


def _band_tables():
    t = np.arange(CHUNK, dtype=np.int32)[:, None]
    j = np.arange(2 * CHUNK, dtype=np.int32)[None, :]
    d = t + CHUNK - j
    in_window = (d >= 0) & (d < CHUNK)
    dc = np.clip(d, 0, REL_MAX_DIST - 1)
    max_exact = REL_BUCKETS // 2
    df = np.maximum(dc, 1).astype(np.float32)
    large = max_exact + (np.log(df / np.float32(max_exact)) / np.float32(np.log(REL_MAX_DIST / max_exact))
                         * np.float32(REL_BUCKETS - max_exact)).astype(np.int32)
    large = np.minimum(large, REL_BUCKETS - 1)
    bucket = np.where(dc < max_exact, dc, large)
    return np.where(in_window, bucket, -1).astype(np.int32)


def _use_order():
    x, y, c = lax.axis_index("x"), lax.axis_index("y"), lax.axis_index("c")
    blocks = [(x, y, c), (x, y, 1 - c)]
    for px, py in [(1 - x, y), (x, 1 - y), (1 - x, 1 - y)]:
        blocks += [(px, py, c), (px, py, 1 - c)]
    return blocks


def _gather_in_proj(order, x, w_in_a, small, ln_gb):
    seq = x.shape[0]
    tm = min(TOKENS_IN_PROJ, seq)
    n_blocks = seq // tm
    shard_cols = w_in_a.shape[1]
    n = len(small) + 2

    def body(order_ref, x_ref, *refs):
        ins, refs = refs[:n], refs[n:]
        p_ref, outs, xb_out, refs = refs[0], refs[1:n + 1], refs[n + 1], refs[n + 2:]
        wa_vmem, xb_vmem, stage = refs[0], refs[1], refs[2:n]
        send_sems, recv_sems, local_sems, out_sems, xb_sem = refs[n:]
        xb_copy = pltpu.make_async_copy(xb_vmem, xb_out, xb_sem)
        phase, blk = pl.program_id(0), pl.program_id(1)
        x_, y_, c_ = lax.axis_index("x"), lax.axis_index("y"), lax.axis_index("c")
        me, sibling = (x_, y_, c_), (x_, y_, 1 - c_)
        chips = [(1 - x_, y_), (x_, 1 - y_), (1 - x_, 1 - y_)]
        use = _use_order()

        def index(px, py, pc):
            return 4 * px + 2 * py + pc

        def slot(a, block):
            return (wa_vmem if a == 0 else outs[a]).at[index(*block)]

        def own(a):
            return slot(0, me) if a == 0 else (ins[a] if a == n - 1 else stage[a - 1])

        def copy(a, k, block, to, src=None):
            return pltpu.make_async_remote_copy(
                src_ref=slot(a, block) if src is None else src, dst_ref=slot(a, block),
                send_sem=send_sems.at[a, k], recv_sem=recv_sems.at[a, k], device_id=to, device_id_type=MESH)

        def keep(a):
            return pltpu.make_async_copy(own(a), slot(a, me), local_sems.at[a])

        def write_out(k):
            return pltpu.make_async_copy(slot(0, use[k]), outs[0].at[index(*use[k])], out_sems.at[k])

        def pass_on(a, j):
            copy(a, 1 + j, (*chips[j], c_), me).wait_recv()
            copy(a, 4 + j, (*chips[j], c_), sibling).start()

        def send(a, j):
            copy(a, 1 + j, me, (*chips[j], c_), own(a)).start()

        @pl.when((phase == 0) & (blk == 0))
        def _():
            wa_vmem[index(*me)] = ins[0][...].astype(BF16)
            for a in range(1, n - 1):
                stage[a - 1][...] = ins[a][...].astype(BF16)
            send(0, 0)
            send(0, 1)
            for a in range(n):
                copy(a, 0, me, sibling, own(a)).start()
                if a:
                    keep(a).start()
            write_out(0).start()
            copy(0, 0, sibling, me).wait_recv()
            write_out(1).start()

        @pl.when((phase == 1) & (blk == 0))
        def _():
            pass_on(0, 0)
            write_out(2).start()
            xb_copy.start()
            send(0, 2)
            for a in range(1, n):
                for j in range(3):
                    send(a, j)
            pass_on(0, 1)
            write_out(4).start()
            copy(0, 4, (*chips[0], 1 - c_), me).wait_recv()
            write_out(3).start()

        @pl.when((phase == 2) & (blk == 0))
        def _():
            copy(0, 5, (*chips[1], 1 - c_), me).wait_recv()
            write_out(5).start()

        @pl.when((phase == 2) & (blk == n_blocks // 2))
        def _():
            pass_on(0, 2)
            write_out(6).start()

        @pl.when((phase == 3) & (blk == 0))
        def _():
            copy(0, 6, (*chips[2], 1 - c_), me).wait_recv()
            write_out(7).start()
            for a in range(1, n):
                pass_on(a, 0)
                pass_on(a, 1)

        rows = pl.ds(pl.multiple_of(blk * tm, tm), tm)

        @pl.when(phase == 0)
        def _():
            xb_vmem[rows, :] = x_ref[...].astype(BF16)

        chip = order_ref[phase]
        xb = xb_vmem[rows, :]
        p_ref[:, :shard_cols] = _nn(xb, wa_vmem[2 * chip])
        p_ref[:, shard_cols:] = _nn(xb, wa_vmem[2 * chip + 1])

        @pl.when((phase == N_DEV // 2 - 1) & (blk == n_blocks - 1))
        def _():
            for a in range(1, n):
                pass_on(a, 2)
            for a in range(1, n):
                copy(a, 0, sibling, me).wait_recv()
                for j, far in enumerate(chips):
                    copy(a, 4 + j, (*far, 1 - c_), me).wait_recv()
            for a in range(n):
                for k in range(7):
                    copy(a, k, me, me).wait_send()
                if a:
                    keep(a).wait()
            for k in range(N_DEV):
                write_out(k).wait()
            xb_copy.wait()

    gathered = [jax.ShapeDtypeStruct((N_DEV,) + w_in_a.shape, BF16)]
    gathered += [jax.ShapeDtypeStruct((N_DEV,) + s.shape, BF16) for s in small]
    gathered.append(jax.ShapeDtypeStruct((N_DEV,) + ln_gb.shape, F32))
    return pl.pallas_call(
        body, name="gather_in_proj",
        out_shape=[jax.ShapeDtypeStruct((seq, N_DEV * shard_cols), F32)] + gathered
        + [jax.ShapeDtypeStruct((seq, D_MODEL), BF16)],
        grid_spec=pltpu.PrefetchScalarGridSpec(
            num_scalar_prefetch=1, grid=(N_DEV // 2, n_blocks),
            in_specs=[pl.BlockSpec((tm, D_MODEL), lambda k, i, order: (jnp.where(k == 0, i, n_blocks - 1), 0))]
            + [VMEM] * n,
            out_specs=[pl.BlockSpec((tm, 2 * shard_cols), lambda k, i, order: (i, order[k]))] + [ANY] * (n + 1),
            scratch_shapes=[pltpu.VMEM((N_DEV,) + w_in_a.shape, BF16), pltpu.VMEM((seq, D_MODEL), BF16)]
            + [pltpu.VMEM(s.shape, BF16) for s in small] + [
                pltpu.SemaphoreType.DMA((n, 7)), pltpu.SemaphoreType.DMA((n, 7)), pltpu.SemaphoreType.DMA((n,)),
                pltpu.SemaphoreType.DMA((N_DEV,)), pltpu.SemaphoreType.DMA(())]),
        compiler_params=pltpu.CompilerParams(dimension_semantics=("arbitrary", "arbitrary"),
                                             vmem_limit_bytes=VMEM_LIMIT_BYTES),
    )(order, x, w_in_a, *small, ln_gb)


def _bias_table(rel_bias_t, bucket):
    def body(rel_ref, bucket_ref, out_ref):
        first = jnp.broadcast_to(bucket_ref[0:1, :], (8, 2 * CHUNK))
        col = lax.broadcasted_iota(jnp.int32, (CHUNK, 2 * CHUNK), 1)
        for h in range(N_Q_HEADS):
            row = jnp.full(first.shape, NEG_INF, F32)
            for b in range(REL_BUCKETS):
                row = jnp.where(first == b, rel_ref[h, b], row)
            tab = pltpu.roll(jnp.broadcast_to(row[0:1, :], (CHUNK, 2 * CHUNK)), 0, 1, stride=1, stride_axis=0)
            pair = h // 2
            rows = slice((pair % PAIRS_PER_KV) * CHUNK, (pair % PAIRS_PER_KV + 1) * CHUNK)
            half = slice((h % 2) * 2 * CHUNK, (h % 2 + 1) * 2 * CHUNK)
            out_ref[1, pair // PAIRS_PER_KV, rows, half] = tab
            out_ref[0, pair // PAIRS_PER_KV, rows, half] = jnp.where(col < CHUNK, NEG_INF, tab)

    return pl.pallas_call(
        body, name="bias_table",
        out_shape=jax.ShapeDtypeStruct((2, N_KV_HEADS, PAIRS_PER_KV * CHUNK, 4 * CHUNK), F32),
        in_specs=[SMEM, VMEM], out_specs=VMEM,
    )(rel_bias_t, bucket)


def _rel_bias_grad(dlogits, bucket):
    def body(dl_ref, bucket_ref, out_ref):
        i = lax.broadcasted_iota(jnp.int32, (CHUNK, CHUNK), 0)
        k = lax.broadcasted_iota(jnp.int32, (CHUNK, CHUNK), 1)
        reverse = jnp.where(i + k == CHUNK - 1, 1.0, 0.0)
        sums = []
        for h in range(N_Q_HEADS):
            pair = h // 2
            dl = dl_ref[pair // PAIRS_PER_KV, (pair % PAIRS_PER_KV) * CHUNK:(pair % PAIRS_PER_KV + 1) * CHUNK,
                        (h % 2) * 2 * CHUNK:(h % 2 + 1) * 2 * CHUNK]
            backwards = jnp.dot(reverse, dl, precision=lax.Precision.HIGHEST, preferred_element_type=F32)
            turned = pltpu.roll(backwards, CHUNK + 1, 1, stride=1, stride_axis=0)
            sums.append(jnp.sum(turned, axis=0, keepdims=True))
        by_offset = jnp.concatenate(sums, axis=0)
        buckets = lax.broadcasted_iota(jnp.int32, (REL_BUCKETS, 2 * CHUNK), 0)
        one_hot = jnp.where(jnp.broadcast_to(bucket_ref[0:1, :], buckets.shape) == buckets, 1.0, 0.0)
        out_ref[...] = lax.dot_general(by_offset, one_hot, (((1,), (1,)), ((), ())),
                                       precision=lax.Precision.HIGHEST, preferred_element_type=F32)

    return pl.pallas_call(
        body, name="rel_bias_grad", out_shape=jax.ShapeDtypeStruct((N_Q_HEADS, REL_BUCKETS), F32),
        in_specs=[VMEM, VMEM], out_specs=VMEM,
    )(dlogits, bucket)


def _masked_spatial(ws_ref, g):
    t = lax.broadcasted_iota(jnp.int32, (CHUNK, CHUNK), 0)
    s = lax.broadcasted_iota(jnp.int32, (CHUNK, CHUNK), 1)
    return jnp.where(s <= t, ws_ref[g], 0.0).astype(BF16)


def _spatial_gate(ws_ref, bsp_t_ref, vn, s_ref, n_chunks):
    for g in range(A_GROUPS):
        w = _masked_spatial(ws_ref, g)
        cols = slice(g * A_GROUP_DIM, (g + 1) * A_GROUP_DIM)
        for ci in range(n_chunks):
            rows = slice(ci * CHUNK, (ci + 1) * CHUNK)
            s_ref[rows, cols] = _nn(w, vn[rows, cols]) + bsp_t_ref[:, g:g + 1]


def _fwd_a(x, p, ln_gb, w_spatial, bsp_t, woa, later):
    seq = x.shape[0]
    tm = min(TOKENS_A_FWD, seq)
    n_chunks = tm // CHUNK
    n_blocks = seq // tm
    n = len(later)

    def body(x_ref, p_hbm, ln_ref, ws_ref, bsp_ref, woa_ref, *refs):
        shards, refs = refs[:n], refs[n:]
        r0_ref, gathered, refs = refs[0], refs[1:1 + n], refs[1 + n:]
        s_ref, p_ring, stage, (send_sems, recv_sems, local_sems, ring_sems) = refs[0], refs[1], refs[2:2 + n], refs[2 + n:]
        step = pl.program_id(0)

        def fetch(block):
            slot = block % P_RING
            start = block * tm if isinstance(block, int) else pl.multiple_of(block * tm, tm)
            return pltpu.make_async_copy(p_hbm.at[pl.ds(start, tm), :], p_ring.at[slot], ring_sems.at[slot])

        @pl.when(step == 0)
        def _():
            for ahead in range(min(P_RING - 1, n_blocks)):
                fetch(ahead).start()

        @pl.when(step + P_RING - 1 < n_blocks)
        def _():
            fetch(step + P_RING - 1).start()

        fetch(step).wait()
        p_ref = p_ring.at[step % P_RING]

        def copies(a):
            x_, y_, c_ = lax.axis_index("x"), lax.axis_index("y"), lax.axis_index("c")
            mine = 4 * x_ + 2 * y_ + c_
            out = [pltpu.make_async_copy(stage[a], gathered[a].at[mine], local_sems.at[a])]
            for r in range(1, N_DEV):
                to = (1 - x_ if r & 4 else x_, 1 - y_ if r & 2 else y_, 1 - c_ if r & 1 else c_)
                out.append(pltpu.make_async_remote_copy(
                    src_ref=stage[a], dst_ref=gathered[a].at[mine], send_sem=send_sems.at[a, r - 1],
                    recv_sem=recv_sems.at[a, r - 1], device_id=to, device_id_type=MESH))
            return out

        @pl.when(step == 0)
        def _():
            for a in range(n):
                stage[a][...] = shards[a][...].astype(BF16)
                for cp in copies(a):
                    cp.start()

        xv = x_ref[...]
        vhat, _ = _ln_normalize(p_ref[:, A_WIDTH:2 * A_WIDTH])
        vn = (vhat * ln_ref[0:1, :] + ln_ref[1:2, :]).astype(BF16)
        _spatial_gate(ws_ref, bsp_ref, vn, s_ref, n_chunks)
        z = p_ref[:, 2 * A_WIDTH:]
        y = p_ref[:, :A_WIDTH] * s_ref[...] * (z * jax.nn.sigmoid(z))
        r0_ref[...] = ALPHA * xv + _nn(y.astype(BF16), woa_ref[...])

        @pl.when(step == n_blocks - 1)
        def _():
            for a in range(n):
                for cp in copies(a):
                    cp.wait()

    return pl.pallas_call(
        body, name="fwd_a", grid=(n_blocks,),
        out_shape=[jax.ShapeDtypeStruct((seq, D_MODEL), F32)]
        + [jax.ShapeDtypeStruct((N_DEV,) + s.shape, BF16) for s in later],
        in_specs=[_rows(tm, D_MODEL), ANY, _resident(ln_gb.shape), _resident(w_spatial.shape),
                  _resident(bsp_t.shape), _resident(woa.shape)] + [VMEM] * n,
        out_specs=[_rows(tm, D_MODEL)] + [ANY] * n,
        scratch_shapes=[pltpu.VMEM((tm, A_WIDTH), F32), pltpu.VMEM((P_RING, tm, 3 * A_WIDTH), F32)]
        + [pltpu.VMEM(s.shape, BF16) for s in later] + [
            pltpu.SemaphoreType.DMA((n, N_DEV - 1)), pltpu.SemaphoreType.DMA((n, N_DEV - 1)),
            pltpu.SemaphoreType.DMA((n,)), pltpu.SemaphoreType.DMA((P_RING,))],
        compiler_params=_seq_params(),
    )(x, p, ln_gb, w_spatial, bsp_t, woa, *later)


def _expand_heads(kv):
    lo = lax.broadcasted_iota(jnp.int32, kv.shape, 1) < HEAD_DIM
    swapped = pltpu.roll(kv, HEAD_DIM, 1)
    zero = jnp.zeros_like(kv)
    return (jnp.where(lo, kv, zero).astype(BF16), jnp.where(lo, zero, swapped).astype(BF16),
            jnp.where(lo, swapped, zero).astype(BF16), jnp.where(lo, zero, kv).astype(BF16))


def _fold_heads(acc, kvh):
    top, bot = acc[:2 * CHUNK], acc[2 * CHUNK:]
    lo = lax.broadcasted_iota(jnp.int32, top.shape, 1) < HEAD_DIM
    t = jnp.where(lo, top, 0.0)
    b = jnp.where(lo, 0.0, bot)
    if kvh == 0:
        return t + pltpu.roll(b, HEAD_DIM, 1)
    return pltpu.roll(t, HEAD_DIM, 1) + b


STACK = PAIRS_PER_KV * CHUNK


def _pair_cols(kvh, pp):
    pair = kvh * PAIRS_PER_KV + pp
    return slice(pair * CHUNK, (pair + 1) * CHUNK)


def _stack_pairs(ref_or_val, rows, kvh):
    return jnp.concatenate([ref_or_val[rows, _pair_cols(kvh, pp)] for pp in range(PAIRS_PER_KV)], axis=0)


def _sink_columns(sink_ref):
    blk = lax.broadcasted_iota(jnp.int32, (STACK, 1), 0) // CHUNK
    out = []
    for kvh in range(N_KV_HEADS):
        halves = []
        for half in range(2):
            col = jnp.zeros((STACK, 1), F32)
            for pp in range(PAIRS_PER_KV):
                col = jnp.where(blk == pp, sink_ref[0, 2 * (kvh * PAIRS_PER_KV + pp) + half], col)
            halves.append(col)
        out.append(halves)
    return out


def _softmax_stack(logits, sink_cols):
    ps, sinks = [], []
    for half in range(2):
        lg = logits[:, half * 2 * CHUNK:(half + 1) * 2 * CHUNK]
        sk = sink_cols[half]
        m = jnp.maximum(jnp.max(lg, axis=-1, keepdims=True), sk)
        e = jnp.exp(lg - m)
        es = jnp.exp(sk - m)
        inv = 1.0 / (jnp.sum(e, axis=-1, keepdims=True) + es)
        ps.append(e * inv)
        sinks.append(es * inv)
    return jnp.concatenate(ps, axis=1), sinks


def _band_operands(before_k, this_k, before_v, this_v, kvh):
    kblk = jnp.concatenate([before_k[2 * kvh], this_k[2 * kvh], before_k[2 * kvh + 1], this_k[2 * kvh + 1]], axis=0)
    vblk = jnp.concatenate([before_v[2 * kvh], this_v[2 * kvh], before_v[2 * kvh + 1], this_v[2 * kvh + 1]], axis=0)
    return kblk, vblk


def _fwd_b(r0, target, wb, wkv, wob, post_g, post_b, sinks, bias):
    seq = r0.shape[0]
    tm = min(TOKENS_B, seq)
    n_chunks = tm // CHUNK
    part_rows = min(PART_ROWS_B, tm)
    n_parts = tm // part_rows
    part_chunks = part_rows // CHUNK
    shard_cols = wb.shape[2]

    def body(r0_ref, tgt_ref, wb_ref, wkv_ref, wob_ref, pg_ref, pb_ref, sink_ref, bias_ref,
             q_ref, z_ref, kv_ref, dr1_ref, loss_ref, dg_ref, db_ref, ek_last, ev_last, *scratch):
        q_s, z_s, h1_s, o_s, ek_s, ev_s = (scratch[i * n_parts:(i + 1) * n_parts] for i in range(6))
        step = pl.program_id(0)

        @pl.when(step == 0)
        def _():
            ek_last[...] = jnp.zeros_like(ek_last)
            ev_last[...] = jnp.zeros_like(ev_last)
            loss_ref[...] = jnp.zeros_like(loss_ref)
            dg_ref[...] = jnp.zeros_like(dg_ref)
            db_ref[...] = jnp.zeros_like(db_ref)

        sink_cols = _sink_columns(sink_ref)
        g1 = pg_ref[1:2, :]

        def project_steps(part):
            rows = slice(part * part_rows, (part + 1) * part_rows)
            live = {}

            def normalize():
                xhat0, _ = _ln_normalize(r0_ref[rows, :])
                h1 = xhat0 * pg_ref[0:1, :] + pb_ref[0:1, :]
                h1_s[part][...] = h1
                live["h1b"] = h1.astype(BF16)

            def shard(k):
                def run():
                    qz = _nn(live["h1b"], wb_ref[k])
                    lo, hi = k * shard_cols, (k + 1) * shard_cols
                    if hi <= B_WIDTH:
                        qs = (qz * SCALE).astype(BF16)
                        q_ref[rows, lo:hi] = qs
                        q_s[part][:, lo:hi] = qs
                    else:
                        z_ref[rows, lo - B_WIDTH:hi - B_WIDTH] = qz
                        z_s[part][:, lo - B_WIDTH:hi - B_WIDTH] = qz
                return run

            def keys():
                kv = _nn(live["h1b"], wkv_ref[...])
                kv_ref[rows, :] = kv.astype(BF16)
                for cj in range(part_chunks):
                    sub = slice(cj * CHUNK, (cj + 1) * CHUNK)
                    for i, e in enumerate(_expand_heads(kv[sub, :KV_WIDTH])):
                        ek_s[part][cj, i] = e
                    for i, e in enumerate(_expand_heads(kv[sub, KV_WIDTH:])):
                        ev_s[part][cj, i] = e

            return [normalize, keys] + [shard(k) for k in range(N_DEV)]

        def attend_steps(part):
            def stack(cj, kvh):
                def run():
                    sub = slice(cj * CHUNK, (cj + 1) * CHUNK)
                    sel = jnp.where(step * n_chunks + part * part_chunks + cj > 0, 1, 0)
                    if cj:
                        before_k, before_v = ek_s[part].at[cj - 1], ev_s[part].at[cj - 1]
                    elif part:
                        before_k, before_v = ek_s[part - 1].at[part_chunks - 1], ev_s[part - 1].at[part_chunks - 1]
                    else:
                        before_k, before_v = ek_last, ev_last
                    kblk, vblk = _band_operands(before_k, ek_s[part].at[cj], before_v, ev_s[part].at[cj], kvh)
                    logits = _nt(_stack_pairs(q_s[part], sub, kvh), kblk) + bias_ref[sel, kvh]
                    p, _ = _softmax_stack(logits, sink_cols[kvh])
                    o = _nn(p.astype(BF16), vblk)
                    for pp in range(PAIRS_PER_KV):
                        o_s[part][sub, _pair_cols(kvh, pp)] = o[pp * CHUNK:(pp + 1) * CHUNK]
                return run

            def finish():
                rows = slice(part * part_rows, (part + 1) * part_rows)
                z = z_s[part][...]
                y = o_s[part][...] * (z * jax.nn.sigmoid(z))
                r1 = ALPHA * h1_s[part][...] + _nn(y.astype(BF16), wob_ref[...])
                xhat1, rstd1 = _ln_normalize(r1)
                diff = xhat1 * g1 + pb_ref[1:2, :] - tgt_ref[rows, :]
                loss_ref[...] += 0.5 * jnp.sum(jnp.mean(diff * diff, axis=-1, keepdims=True), axis=0, keepdims=True)
                dh2 = diff * (1.0 / D_MODEL)
                dg_ref[...] += jnp.sum(dh2 * xhat1, axis=0, keepdims=True)
                db_ref[...] += jnp.sum(dh2, axis=0, keepdims=True)
                dr1_ref[rows, :] = _ln_backward(dh2 * g1, xhat1, rstd1)

            return [stack(cj, kvh) for cj in range(part_chunks) for kvh in range(N_KV_HEADS)] + [finish]

        for run in project_steps(0):
            run()
        for part in range(n_parts):
            for run in _in_turns(attend_steps(part), project_steps(part + 1) if part + 1 < n_parts else []):
                run()
        ek_last[...] = ek_s[n_parts - 1][part_chunks - 1]
        ev_last[...] = ev_s[n_parts - 1][part_chunks - 1]

    vec = pl.BlockSpec((1, D_MODEL), lambda i: (0, 0))
    return pl.pallas_call(
        body, name="fwd_b", grid=(seq // tm,),
        out_shape=[jax.ShapeDtypeStruct((seq, B_WIDTH), BF16), jax.ShapeDtypeStruct((seq, B_WIDTH), F32),
                   jax.ShapeDtypeStruct((seq, 2 * KV_WIDTH), BF16), jax.ShapeDtypeStruct((seq, D_MODEL), F32),
                   jax.ShapeDtypeStruct((1, 1), F32), jax.ShapeDtypeStruct((1, D_MODEL), F32),
                   jax.ShapeDtypeStruct((1, D_MODEL), F32)],
        in_specs=[_rows(tm, D_MODEL), _rows(tm, D_MODEL), _resident(wb.shape), _resident(wkv.shape),
                  _resident(wob.shape), _resident(post_g.shape), _resident(post_b.shape), SMEM,
                  _resident(bias.shape)],
        out_specs=[_rows(tm, B_WIDTH), _rows(tm, B_WIDTH), _rows(tm, 2 * KV_WIDTH), _rows(tm, D_MODEL),
                   pl.BlockSpec((1, 1), lambda i: (0, 0)), vec, vec],
        scratch_shapes=[pltpu.VMEM((4, CHUNK, CHUNK), BF16), pltpu.VMEM((4, CHUNK, CHUNK), BF16)]
        + [pltpu.VMEM((part_rows, B_WIDTH), BF16)] * n_parts + [pltpu.VMEM((part_rows, B_WIDTH), F32)] * n_parts
        + [pltpu.VMEM((part_rows, D_MODEL), F32)] * n_parts + [pltpu.VMEM((part_rows, B_WIDTH), F32)] * n_parts
        + [pltpu.VMEM((part_chunks, 4, CHUNK, CHUNK), BF16)] * (2 * n_parts),
        compiler_params=_seq_params(),
    )(r0, target, wb, wkv, wob, post_g, post_b, sinks, bias)


def _bwd_b(r0, dr1, q, z, kv, wb, wkv, wob, post_g, post_b, sinks, bias):
    seq = r0.shape[0]
    tm = min(TOKENS_B_BWD, seq)
    n_chunks = tm // CHUNK
    n_blocks = seq // tm
    shard_cols = wb.shape[2]

    def body(r0_ref, dr1_ref, q_ref, z_ref, kv_ref, kvp_ref, wb_ref, wkv_ref, wob_ref, pg_ref, pb_ref, sink_ref,
             bias_ref, dr0_ref, dwb_out, dwkv_out, dwob_out, dsink_ref, dl_out, dpost_ref,
             dwb_acc, dwkv_acc, dwob_acc, dl_acc, dsink_acc, carry_ref, ek_ref, ev_ref, y_ref, dqz_ref, dkv_ref):
        step = pl.program_id(0)
        block = n_blocks - 1 - step

        @pl.when(step == 0)
        def _():
            dpost_ref[...] = jnp.zeros_like(dpost_ref)
            dwb_acc[...] = jnp.zeros_like(dwb_acc)
            dwkv_acc[...] = jnp.zeros_like(dwkv_acc)
            dwob_acc[...] = jnp.zeros_like(dwob_acc)
            dl_acc[...] = jnp.zeros_like(dl_acc)
            carry_ref[...] = jnp.zeros_like(carry_ref)
            dsink_acc[...] = jnp.zeros_like(dsink_acc)

        xhat0, rstd0 = _ln_normalize(r0_ref[...])
        h1b = (xhat0 * pg_ref[0:1, :] + pb_ref[0:1, :]).astype(BF16)
        dr1 = dr1_ref[...]
        dr1b = dr1.astype(BF16)
        dy = _nt(dr1b, wob_ref[...])
        sg, dsg = _silu_and_grad(z_ref[...])
        do_all = dy * sg
        dyz = dy * dsg

        for ci in range(n_chunks + 1):
            src = kvp_ref[...] if ci == 0 else kv_ref[(ci - 1) * CHUNK:ci * CHUNK, :]
            src = src.astype(F32)
            for i, e in enumerate(_expand_heads(src[:, :KV_WIDTH])):
                ek_ref[ci, i] = e
            for i, e in enumerate(_expand_heads(src[:, KV_WIDTH:])):
                ev_ref[ci, i] = e

        sink_cols = _sink_columns(sink_ref)
        for ci in reversed(range(n_chunks)):
            rows = slice(ci * CHUNK, (ci + 1) * CHUNK)
            sel = jnp.where(block * n_chunks + ci > 0, 1, 0)
            dk_band = jnp.zeros((2 * CHUNK, KV_WIDTH), F32)
            dv_band = jnp.zeros((2 * CHUNK, KV_WIDTH), F32)
            for kvh in range(N_KV_HEADS):
                kblk, vblk = _band_operands(ek_ref.at[ci], ek_ref.at[ci + 1], ev_ref.at[ci], ev_ref.at[ci + 1], kvh)
                qs = _stack_pairs(q_ref, rows, kvh)
                p, psink = _softmax_stack(_nt(qs, kblk) + bias_ref[sel, kvh], sink_cols[kvh])
                pb16 = p.astype(BF16)
                o = _nn(pb16, vblk)
                y = (o * _stack_pairs(sg, rows, kvh)).astype(BF16)
                dz = (o * _stack_pairs(dyz, rows, kvh)).astype(BF16)
                dob = _stack_pairs(do_all, rows, kvh).astype(BF16)
                dp = _nt(dob, vblk)
                dlogits = []
                for half in range(2):
                    hs = slice(half * 2 * CHUNK, (half + 1) * 2 * CHUNK)
                    delta = jnp.sum(dp[:, hs] * p[:, hs], axis=-1, keepdims=True)
                    dlogits.append(p[:, hs] * (dp[:, hs] - delta))
                    dsink_acc[2 * kvh + half] += psink[half] * delta
                dlog = jnp.concatenate(dlogits, axis=1)
                dl_acc[kvh] += dlog
                dsc = dlog.astype(BF16)
                dq = (_nn(dsc, kblk) * SCALE).astype(BF16)
                for pp in range(PAIRS_PER_KV):
                    blk = slice(pp * CHUNK, (pp + 1) * CHUNK)
                    cols = _pair_cols(kvh, pp)
                    y_ref[rows, cols] = y[blk]
                    dqz_ref[rows, cols] = dq[blk]
                    dqz_ref[rows, B_WIDTH + cols.start:B_WIDTH + cols.stop] = dz[blk]
                dk_band = dk_band + _fold_heads(_tn(dsc, qs), kvh)
                dv_band = dv_band + _fold_heads(_tn(pb16, dob), kvh)
            band = jnp.concatenate([dk_band, dv_band], axis=1)
            dkv_ref[rows, :] = (band[CHUNK:] + carry_ref[...]).astype(BF16)
            carry_ref[...] = band[:CHUNK]

        dqz = dqz_ref[...]
        dkv = dkv_ref[...]
        dh1 = ALPHA * dr1 + _nt(dkv, wkv_ref[...])
        for k in range(N_DEV):
            part = dqz[:, k * shard_cols:(k + 1) * shard_cols]
            dh1 = dh1 + _nt(part, wb_ref[k])
            dwb_acc[k] += _tn(h1b, part)
        dpost_ref[0:1, :] += jnp.sum(dh1 * xhat0, axis=0, keepdims=True)
        dpost_ref[1:2, :] += jnp.sum(dh1, axis=0, keepdims=True)
        dr0_ref[...] = _ln_backward(dh1 * pg_ref[0:1, :], xhat0, rstd0)
        dwkv_acc[...] += _tn(h1b, dkv)
        dwob_acc[...] += _tn(y_ref[...], dr1b)

        @pl.when(step == n_blocks - 1)
        def _():
            pltpu.sync_copy(dwb_acc, dwb_out)
            pltpu.sync_copy(dwkv_acc, dwkv_out)
            pltpu.sync_copy(dwob_acc, dwob_out)
            pltpu.sync_copy(dl_acc, dl_out)
            lane = lax.broadcasted_iota(jnp.int32, (1, CHUNK), 1)
            dsink = jnp.zeros((1, CHUNK), F32)
            for kvh in range(N_KV_HEADS):
                for half in range(2):
                    for pp in range(PAIRS_PER_KV):
                        head = 2 * (kvh * PAIRS_PER_KV + pp) + half
                        tot = jnp.sum(dsink_acc[2 * kvh + half, pp * CHUNK:(pp + 1) * CHUNK, :], axis=0, keepdims=True)
                        dsink = dsink - jnp.where(lane == head, tot, 0.0)
            dsink_ref[...] = dsink

    rev = functools.partial(_rows, tm, reverse_of=n_blocks)
    prev_chunk = pl.BlockSpec(
        (CHUNK, 2 * KV_WIDTH), lambda i: (jnp.maximum((n_blocks - 1 - i) * n_chunks - 1, 0), 0))
    return pl.pallas_call(
        body, name="bwd_b", grid=(n_blocks,),
        out_shape=[jax.ShapeDtypeStruct((seq, D_MODEL), F32), jax.ShapeDtypeStruct(wb.shape, F32),
                   jax.ShapeDtypeStruct(wkv.shape, F32), jax.ShapeDtypeStruct(wob.shape, F32),
                   jax.ShapeDtypeStruct((1, CHUNK), F32),
                   jax.ShapeDtypeStruct((N_KV_HEADS, STACK, 4 * CHUNK), F32), jax.ShapeDtypeStruct((2, D_MODEL), F32)],
        in_specs=[rev(D_MODEL), rev(D_MODEL), rev(B_WIDTH), rev(B_WIDTH), rev(2 * KV_WIDTH), prev_chunk,
                  _resident(wb.shape), _resident(wkv.shape), _resident(wob.shape), _resident(post_g.shape),
                  _resident(post_b.shape), SMEM, _resident(bias.shape)],
        out_specs=[rev(D_MODEL), ANY, ANY, ANY, pl.BlockSpec((1, CHUNK), lambda i: (0, 0)), ANY,
                   pl.BlockSpec((2, D_MODEL), lambda i: (0, 0))],
        scratch_shapes=[pltpu.VMEM(wb.shape, F32), pltpu.VMEM(wkv.shape, F32), pltpu.VMEM(wob.shape, F32),
                        pltpu.VMEM((N_KV_HEADS, STACK, 4 * CHUNK), F32), pltpu.VMEM((2 * N_KV_HEADS, STACK, 1), F32),
                        pltpu.VMEM((CHUNK, 2 * KV_WIDTH), F32),
                        pltpu.VMEM((n_chunks + 1, 4, CHUNK, CHUNK), BF16),
                        pltpu.VMEM((n_chunks + 1, 4, CHUNK, CHUNK), BF16),
                        pltpu.VMEM((tm, B_WIDTH), BF16), pltpu.VMEM((tm, 2 * B_WIDTH), BF16),
                        pltpu.VMEM((tm, 2 * KV_WIDTH), BF16)],
        compiler_params=_seq_params(),
    )(r0, dr1, q, z, kv, kv, wb, wkv, wob, post_g, post_b, sinks, bias)


def _bwd_a(dr0, p, ln_gb, w_spatial, bsp_t, woa, exchange):
    seq = dr0.shape[0]
    tm = min(TOKENS_A, seq)
    n_chunks = tm // CHUNK
    n_blocks = seq // tm

    def body(dr0_ref, p_ref, ln_ref, ws_ref, bsp_ref, woa_ref, *refs):
        ex_in, refs = refs[:exchange.n], refs[exchange.n:]
        (dp_ref, dwoa_out, dws_ref, dbsp_ref, dln_ref), refs = refs[:5], refs[5:]
        ex_out, refs = refs[:exchange.n], refs[exchange.n:]
        (dwoa_acc, s_ref, ds_ref, dvn_ref), ex_sems = refs[:4], refs[4:]
        step = pl.program_id(0)

        @pl.when(step == 0)
        def _():
            exchange.start(ex_in, ex_out, ex_sems)
            dwoa_acc[...] = jnp.zeros_like(dwoa_acc)
            dws_ref[...] = jnp.zeros_like(dws_ref)
            dbsp_ref[...] = jnp.zeros_like(dbsp_ref)
            dln_ref[...] = jnp.zeros_like(dln_ref)

        dr0b = dr0_ref[...].astype(BF16)

        vhat, rstd_v = _ln_normalize(p_ref[:, A_WIDTH:2 * A_WIDTH])
        ln_g = ln_ref[0:1, :]
        vn = (vhat * ln_g + ln_ref[1:2, :]).astype(BF16)
        _spatial_gate(ws_ref, bsp_ref, vn, s_ref, n_chunks)
        u = p_ref[:, :A_WIDTH]
        s = s_ref[...]
        sg, dsg = _silu_and_grad(p_ref[:, 2 * A_WIDTH:])
        gate = s * sg
        dwoa_acc[...] += _tn((u * gate).astype(BF16), dr0b)
        dy = _nt(dr0b, woa_ref[...])
        dyu = dy * u
        dp_ref[:, :A_WIDTH] = (dy * gate).astype(BF16)
        dp_ref[:, 2 * A_WIDTH:] = (dyu * (s * dsg)).astype(BF16)
        ds_ref[...] = dyu * sg

        lane = lax.broadcasted_iota(jnp.int32, (CHUNK, CHUNK), 1)
        dbsp = jnp.zeros((CHUNK, CHUNK), F32)
        for g in range(A_GROUPS):
            w = _masked_spatial(ws_ref, g)
            cols = slice(g * A_GROUP_DIM, (g + 1) * A_GROUP_DIM)
            dws = jnp.zeros((CHUNK, CHUNK), F32)
            for ci in range(n_chunks):
                rows = slice(ci * CHUNK, (ci + 1) * CHUNK)
                ds = ds_ref[rows, cols]
                dsb = ds.astype(BF16)
                dvn_ref[rows, cols] = _tn(w, dsb)
                dws = dws + _nt(dsb, vn[rows, cols])
                dbsp = dbsp + jnp.where(lane == g, jnp.sum(ds, axis=-1, keepdims=True), 0.0)
            dws_ref[g] += dws
        dbsp_ref[...] += dbsp

        dvn = dvn_ref[...]
        dln_ref[0:1, :] += jnp.sum(dvn * vhat, axis=0, keepdims=True)
        dln_ref[1:2, :] += jnp.sum(dvn, axis=0, keepdims=True)
        dp_ref[:, A_WIDTH:2 * A_WIDTH] = _ln_backward(dvn * ln_g, vhat, rstd_v).astype(BF16)

        @pl.when(step == n_blocks - 1)
        def _():
            t = lax.broadcasted_iota(jnp.int32, (CHUNK, CHUNK), 0)
            for g in range(A_GROUPS):
                dws_ref[g] = jnp.where(lane <= t, dws_ref[g], 0.0)
            pltpu.sync_copy(dwoa_acc, dwoa_out)
            exchange.wait(ex_in, ex_out, ex_sems)

    def fixed(shape):
        nd = len(shape)
        return pl.BlockSpec(shape, lambda i: (0,) * nd)

    return pl.pallas_call(
        body, name="bwd_a", grid=(n_blocks,),
        out_shape=[jax.ShapeDtypeStruct((seq, 3 * A_WIDTH), BF16),
                   jax.ShapeDtypeStruct(woa.shape, F32), jax.ShapeDtypeStruct(w_spatial.shape, F32),
                   jax.ShapeDtypeStruct((CHUNK, CHUNK), F32), jax.ShapeDtypeStruct((2, A_WIDTH), F32)]
        + exchange.out_shape,
        in_specs=[_rows(tm, D_MODEL), _rows(tm, 3 * A_WIDTH), _resident(ln_gb.shape),
                  _resident(w_spatial.shape), _resident(bsp_t.shape), _resident(woa.shape)] + [ANY] * exchange.n,
        out_specs=[_rows(tm, 3 * A_WIDTH), ANY, fixed(w_spatial.shape),
                   fixed((CHUNK, CHUNK)), fixed((2, A_WIDTH))] + [ANY] * exchange.n,
        scratch_shapes=[pltpu.VMEM(woa.shape, F32), pltpu.VMEM((tm, A_WIDTH), F32),
                        pltpu.VMEM((tm, A_WIDTH), F32), pltpu.VMEM((tm, A_WIDTH), F32)] + exchange.scratch_shapes,
        compiler_params=_seq_params(),
    )(dr0, p, ln_gb, w_spatial, bsp_t, woa, *exchange.inputs)


def _input_grad_blocks(seq):
    tm = min(TOKENS_DX, seq // 8)
    return tm, seq // tm


def _input_grad(dp, dr0, wa, first, n_blocks, earlier, exchange):
    seq = dr0.shape[0]
    tm, _ = _input_grad_blocks(seq)
    shard_cols = wa.shape[2]
    n_own = 3 if earlier is None else 4

    def body(dp_ref, dr0_ref, wa_ref, *refs):
        refs = refs[n_own - 3:]
        ex_in, dx_ref, ex_out, ex_sems = (refs[:exchange.n], refs[exchange.n], refs[exchange.n + 1:2 * exchange.n + 1],
                                          refs[2 * exchange.n + 1:])
        step = pl.program_id(0)

        @pl.when(step == 0)
        def _():
            exchange.start(ex_in, ex_out, ex_sems)

        dx = ALPHA * dr0_ref[...]
        for k in range(N_DEV):
            dx = dx + _nt(dp_ref[:, k * shard_cols:(k + 1) * shard_cols], wa_ref[k])
        dx_ref[...] = dx

        @pl.when(step == n_blocks - 1)
        def _():
            exchange.wait(ex_in, ex_out, ex_sems)

    def rows(width):
        return pl.BlockSpec((tm, width), lambda i: (first + i, 0))

    return pl.pallas_call(
        body, name=f"input_grad_{first}", grid=(n_blocks,),
        out_shape=[jax.ShapeDtypeStruct((seq, D_MODEL), F32)] + exchange.out_shape,
        in_specs=[rows(3 * A_WIDTH), rows(D_MODEL), _resident(wa.shape)] + [ANY] * (n_own - 3 + exchange.n),
        out_specs=[rows(D_MODEL)] + [ANY] * exchange.n,
        scratch_shapes=exchange.scratch_shapes,
        input_output_aliases={} if earlier is None else {3: 0},
        compiler_params=_seq_params(),
    )(dp, dr0, wa, *(() if earlier is None else (earlier,)), *exchange.inputs)


def _in_proj_weight_grad(place, xb, dp, shard_cols, exchange, to_sibling=True):
    seq = xb.shape[0]
    tk = min(TOKENS_WGRAD, seq)
    n_k = seq // tk
    n = exchange.n

    def shard_of(turn, place_ref):
        if not to_sibling:
            return turn
        return 2 * (turn % 4) + jnp.where(turn < 4, 1 - place_ref[1], place_ref[1])

    def body(place_ref, x_ref, dp_ref, *refs):
        ex_in, out_ref, landed_ref, ex_out = refs[:n], refs[n], refs[n + 1], refs[n + 2:2 * n + 2]
        ex_sems, stage_ref, send_sems, recv_sems = refs[2 * n + 2:-3], refs[-3], refs[-2], refs[-1]
        turn, step = pl.program_id(0), pl.program_id(1)

        def send(chip):
            x, y, c = lax.axis_index("x"), lax.axis_index("y"), lax.axis_index("c")
            return pltpu.make_async_remote_copy(
                src_ref=stage_ref.at[chip % 2], dst_ref=landed_ref.at[chip, 0], send_sem=send_sems.at[chip],
                recv_sem=recv_sems.at[chip], device_id=(x, y, 1 - c), device_id_type=MESH)

        @pl.when((turn == 0) & (step == 0))
        def _():
            exchange.start(ex_in, ex_out, ex_sems)

        @pl.when(step == 0)
        def _():
            out_ref[...] = jnp.zeros_like(out_ref)

        out_ref[...] += _tn(x_ref[...], dp_ref[...])

        if to_sibling:
            @pl.when((turn >= 2) & (turn < 4) & (step == n_k - 1))
            def _():
                send(turn - 2).wait_send()

            @pl.when((turn < 4) & (step == n_k - 1))
            def _():
                stage_ref[turn % 2] = out_ref[...]
                send(turn).start()

        @pl.when((turn == N_DEV - 1) & (step == n_k - 1))
        def _():
            if to_sibling:
                send(2).wait_send()
                send(3).wait_send()
                for chip in range(4):
                    send(chip).wait_recv()
            exchange.wait(ex_in, ex_out, ex_sems)

    outs = pl.pallas_call(
        body, name="in_proj_weight_grad",
        out_shape=[jax.ShapeDtypeStruct((N_DEV, D_MODEL, shard_cols), F32),
                   jax.ShapeDtypeStruct((4, 1, D_MODEL, shard_cols), F32)] + exchange.out_shape,
        grid_spec=pltpu.PrefetchScalarGridSpec(
            num_scalar_prefetch=1, grid=(N_DEV, n_k),
            in_specs=[pl.BlockSpec((tk, D_MODEL), lambda j, k, pr: (k, 0)),
                      pl.BlockSpec((tk, shard_cols), lambda j, k, pr: (k, shard_of(j, pr)))] + [ANY] * n,
            out_specs=[pl.BlockSpec((None, D_MODEL, shard_cols), lambda j, k, pr: (shard_of(j, pr), 0, 0)), ANY]
            + [ANY] * n,
            scratch_shapes=exchange.scratch_shapes + [
                pltpu.VMEM((2, D_MODEL, shard_cols), F32), pltpu.SemaphoreType.DMA((4,)), pltpu.SemaphoreType.DMA((4,))]),
        compiler_params=pltpu.CompilerParams(dimension_semantics=("arbitrary", "arbitrary"),
                                             vmem_limit_bytes=VMEM_LIMIT_BYTES),
    )(place, xb, dp, *exchange.inputs)
    return outs if to_sibling else [outs[0], *outs[2:]]


class _Exchange:
    def __init__(self, pair=(), chip=(), gather=(), scatter=()):
        self.pair, self.chip, self.gather, self.scatter = tuple(pair), tuple(chip), tuple(gather), tuple(scatter)
        self.inputs = self.pair + self.chip + self.gather + self.scatter
        self.n = len(self.inputs)
        self.out_shape = [jax.ShapeDtypeStruct((4, 1) + g.shape[2:], F32) for g in self.pair]
        self.out_shape += [jax.ShapeDtypeStruct((3,) + s.shape[1:], s.dtype) for s in self.chip]
        self.out_shape += [jax.ShapeDtypeStruct((N_DEV,) + g.shape, g.dtype) for g in self.gather]
        self.out_shape += [jax.ShapeDtypeStruct((N_DEV,) + g.shape[2:], g.dtype) for g in self.scatter]
        self.scratch_shapes = [pltpu.SemaphoreType.DMA((max(len(self.pair), 1),)) for _ in range(2)]
        self.scratch_shapes += [pltpu.SemaphoreType.DMA((max(len(self.chip), 1), 3)) for _ in range(2)]
        self.scratch_shapes += [pltpu.SemaphoreType.DMA((max(len(self.gather), 1), N_DEV)) for _ in range(2)]
        self.scratch_shapes += [pltpu.SemaphoreType.DMA((max(len(self.scatter), 1), N_DEV)) for _ in range(2)]

    def _copies(self, ins, outs, sems):
        if not self.n:
            return []
        pair_send, pair_recv, chip_send, chip_recv, gather_send, gather_recv, scatter_send, scatter_recv = sems
        x, y, c = lax.axis_index("x"), lax.axis_index("y"), lax.axis_index("c")
        copies = []
        for a in range(len(self.scatter)):
            at = len(self.pair) + len(self.chip) + len(self.gather) + a
            dst = outs[at].at[4 * x + 2 * y + c]
            copies.append(pltpu.make_async_copy(ins[at].at[2 * x + y, c], dst, scatter_send.at[a, 0]))
            for r in range(1, N_DEV):
                to = (1 - x if r & 4 else x, 1 - y if r & 2 else y, 1 - c if r & 1 else c)
                copies.append(pltpu.make_async_remote_copy(
                    src_ref=ins[at].at[2 * to[0] + to[1], to[2]], dst_ref=dst, send_sem=scatter_send.at[a, r],
                    recv_sem=scatter_recv.at[a, r], device_id=to, device_id_type=MESH))
        for a in range(len(self.gather)):
            src = ins[len(self.pair) + len(self.chip) + a]
            dst = outs[len(self.pair) + len(self.chip) + a].at[4 * x + 2 * y + c]
            copies.append(pltpu.make_async_copy(src, dst, gather_send.at[a, 0]))
            for r in range(1, N_DEV):
                to = (1 - x if r & 4 else x, 1 - y if r & 2 else y, 1 - c if r & 1 else c)
                copies.append(pltpu.make_async_remote_copy(
                    src_ref=src, dst_ref=dst, send_sem=gather_send.at[a, r], recv_sem=gather_recv.at[a, r],
                    device_id=to, device_id_type=MESH))
        for a in range(len(self.pair)):
            copies.append(pltpu.make_async_remote_copy(
                src_ref=ins[a].at[:, pl.ds(1 - c, 1)], dst_ref=outs[a],
                send_sem=pair_send.at[a], recv_sem=pair_recv.at[a], device_id=(x, y, 1 - c), device_id_type=MESH))
        for a in range(len(self.chip)):
            src, dst = ins[len(self.pair) + a], outs[len(self.pair) + a]
            for k, (px, py) in enumerate([(1 - x, y), (x, 1 - y), (1 - x, 1 - y)]):
                copies.append(pltpu.make_async_remote_copy(
                    src_ref=src.at[2 * px + py], dst_ref=dst.at[k], send_sem=chip_send.at[a, k],
                    recv_sem=chip_recv.at[a, k], device_id=(px, py, c), device_id_type=MESH))
        return copies

    def start(self, ins, outs, sems):
        for cp in self._copies(ins, outs, sems):
            cp.start()

    def wait(self, ins, outs, sems):
        for cp in self._copies(ins, outs, sems):
            cp.wait()


def _pair_sum(place, grads, landeds, dtypes, finish=None):
    n = len(grads)

    def body(place_ref, *refs):
        ins, outs = refs[:len(in_specs)], refs[len(in_specs):]
        for a in range(n):
            outs[a][...] = (ins[2 * a][...] + ins[2 * a + 1][...]).astype(dtypes[a])
        if finish is not None:
            parts_ref, = ins[2 * n:]

            @pl.when(pl.program_id(0) == 0)
            def _():
                total = parts_ref[0]
                for k in range(1, N_DEV):
                    total = total + parts_ref[k]
                outs[n][...] = total

    in_specs, out_specs, operands = [], [], []
    for g, l in zip(grads, landeds):
        rows, cols = g.shape[2:]
        in_specs += [pl.BlockSpec((None, None, rows, cols), lambda k, pr: (k, pr[1], 0, 0)),
                     pl.BlockSpec((None, None, rows, cols), lambda k, pr: (k, 0, 0, 0))]
        out_specs.append(pl.BlockSpec((None, rows, cols), lambda k, pr: (k, 0, 0)))
        operands += [g, l]
    out_shape = [jax.ShapeDtypeStruct((4,) + g.shape[2:], dt) for g, dt in zip(grads, dtypes)]
    if finish is not None:
        rows, cols = finish.shape[1:]
        in_specs.append(pl.BlockSpec((N_DEV, rows, cols), lambda k, pr: (0, 0, 0)))
        out_specs.append(pl.BlockSpec((rows, cols), lambda k, pr: (0, 0)))
        out_shape.append(jax.ShapeDtypeStruct((rows, cols), F32))
        operands.append(finish)
    return pl.pallas_call(
        body, name="pair_sum", out_shape=out_shape,
        grid_spec=pltpu.PrefetchScalarGridSpec(num_scalar_prefetch=1, grid=(4,), in_specs=in_specs,
                                               out_specs=out_specs),
        compiler_params=pltpu.CompilerParams(dimension_semantics=("arbitrary",), vmem_limit_bytes=VMEM_LIMIT_BYTES),
    )(place, *operands)


def _adamw_math(w, g, m, v):
    m = ADAM_B1 * m + (1.0 - ADAM_B1) * g
    v = ADAM_B2 * v + (1.0 - ADAM_B2) * (g * g)
    m_hat = m / (1.0 - ADAM_B1 ** ADAM_STEP)
    v_hat = v / (1.0 - ADAM_B2 ** ADAM_STEP)
    delta = -ADAM_LR * (m_hat / (jnp.sqrt(v_hat) + ADAM_EPS) + ADAM_WD * w)
    return delta, m, v


def _adamw_shards(place, items, steps, small=(), packed=None, scalar_at=None):
    n, n_small = len(items), len(small)
    n_in = 6 * n + (1 + 3 * n_small if n_small else 0)

    def body(place_ref, *refs):
        ins, small_ins = refs[:6 * n], refs[6 * n + 1:n_in]
        outs, small_outs = refs[n_in:n_in + 4 * n], refs[n_in + 4 * n:]
        for i in range(n):
            a_ref, b_ref, got_ref, w_ref, m_ref, v_ref = ins[6 * i:6 * i + 6]
            g = a_ref[...] + b_ref[...]
            for k in range(3):
                g = g + got_ref[k].astype(F32)
            outs[4 * i][...] = g
            outs[4 * i + 1][...], outs[4 * i + 2][...], outs[4 * i + 3][...] = _adamw_math(
                w_ref[...], g, m_ref[...], v_ref[...])

        if n_small:
            packed_ref = refs[6 * n]

            @pl.when(pl.program_id(0) == 0)
            def _():
                me = 2 * place_ref[0] + place_ref[1]
                small_outs[4 * n_small][...] = packed_ref[scalar_at:scalar_at + 1, 0:1]
                for i, (w, (at, mine), _, _) in enumerate(small):
                    w_ref, m_ref, v_ref = small_ins[3 * i:3 * i + 3]
                    g_ref, d_ref, new_m_ref, new_v_ref = small_outs[4 * i:4 * i + 4]
                    rows, cols = w.shape
                    if cols <= CHUNK:
                        g_ref[...] = packed_ref[at:at + rows, :cols]
                    else:
                        per_row = cols // CHUNK
                        start = at + me * (rows * per_row) if mine else at
                        for r in range(rows * per_row):
                            lane = (r % per_row) * CHUNK
                            row = r // per_row
                            g_ref[row:row + 1, lane:lane + CHUNK] = packed_ref[pl.ds(start + r, 1), :]
                    d_ref[...], new_m_ref[...], new_v_ref[...] = _adamw_math(
                        w_ref[...], g_ref[...], m_ref[...], v_ref[...])

    def whole(a):
        return pl.BlockSpec(a.shape, lambda i, pr: (0, 0))

    in_specs, out_specs, out_shape, operands = [], [], [], []
    for grad, landed, got, w, m, v in items:
        rows, cols = w.shape
        tr = rows // steps
        assert tr * steps == rows and tr % 16 == 0, (rows, steps)
        spec = pl.BlockSpec((tr, cols), lambda i, pr: (i, 0))
        in_specs += [pl.BlockSpec((None, None, tr, cols), lambda i, pr: (pr[0], pr[1], i, 0)),
                     pl.BlockSpec((None, None, tr, cols), lambda i, pr: (pr[0], 0, i, 0)),
                     pl.BlockSpec((3, tr, cols), lambda i, pr: (0, i, 0)), spec, spec, spec]
        out_specs += [spec] * 4
        out_shape += [jax.ShapeDtypeStruct((rows, cols), F32)] * 4
        operands += [grad, landed, got, w, m, v]
    if n_small:
        in_specs.append(whole(packed))
        operands.append(packed)
    for w, _, m, v in small:
        in_specs += [whole(w)] * 3
        out_specs += [whole(w)] * 4
        out_shape += [jax.ShapeDtypeStruct(w.shape, F32)] * 4
        operands += [w, m, v]
    if n_small:
        out_specs.append(pl.BlockSpec((1, 1), lambda i, pr: (0, 0)))
        out_shape.append(jax.ShapeDtypeStruct((1, 1), F32))
    outs = pl.pallas_call(
        body, name="adamw_shards", out_shape=out_shape,
        grid_spec=pltpu.PrefetchScalarGridSpec(num_scalar_prefetch=1, grid=(steps,), in_specs=in_specs,
                                               out_specs=out_specs),
        compiler_params=pltpu.CompilerParams(dimension_semantics=("arbitrary",), vmem_limit_bytes=VMEM_LIMIT_BYTES),
    )(place, *operands)
    small_outs = outs[4 * n:]
    return ([outs[4 * i:4 * i + 4] for i in range(n)],
            [tuple(small_outs[4 * i:4 * i + 4]) for i in range(n_small)], small_outs[4 * n_small:])


def _pack_small(parts):
    rows = []
    for a in parts:
        r, c = a.shape
        rows.append(jnp.pad(a, ((0, (-r) % 8), (0, CHUNK - c))))
    flat = jnp.concatenate(rows, axis=0)
    return jnp.pad(flat, ((0, N_DEV * SMALL_ROWS - flat.shape[0]), (0, 0))).reshape(N_DEV, SMALL_ROWS, CHUNK)


def _small_rows(shapes):
    out, at = [], 0
    for r, _ in shapes:
        out.append(at)
        at += r + (-r) % 8
    return out


def kernel(x, w_in_a, sgu_ln_g, sgu_ln_b, w_spatial, b_spatial, w_out_a, w_kv, w_in_b, attn_sinks, rel_bias, w_out_b, post_ln_g, post_ln_b, loss_target, m_w_in_a, m_sgu_ln_g, m_sgu_ln_b, m_w_spatial, m_b_spatial, m_w_out_a, m_w_kv, m_w_in_b, m_attn_sinks, m_rel_bias, m_w_out_b, m_post_ln_g, m_post_ln_b, v_w_in_a, v_sgu_ln_g, v_sgu_ln_b, v_w_spatial, v_b_spatial, v_w_out_a, v_w_kv, v_w_in_b, v_attn_sinks, v_rel_bias, v_w_out_b, v_post_ln_g, v_post_ln_b):
    seq = x.shape[1]
    bucket =jnp.asarray(_band_tables())

    ln_shard = jnp.concatenate([sgu_ln_g, sgu_ln_b], axis=0)
    x2, tgt = x[0], loss_target[0]
    order = jnp.stack([2 * px + py for px, py, _ in _use_order()[::2]]).astype(jnp.int32)
    p, wa, woa, ln_all, xb = _gather_in_proj(order, x2, w_in_a[0], (w_out_a[0],), ln_shard)
    woa = woa.reshape(A_WIDTH, D_MODEL)
    ln_gb = jnp.transpose(ln_all, (1, 0, 2)).reshape(2, A_WIDTH)
    ws = w_spatial[0]
    bsp_t = jnp.pad(b_spatial[0].T, ((0, 0), (0, CHUNK - A_GROUPS)))
    rel_t = rel_bias.T
    bias = _bias_table(rel_t, bucket)

    r0, wkv, wb, wob = _fwd_a(x2, p, ln_gb, ws, bsp_t, woa, (w_kv, w_in_b[0], w_out_b[0]))
    wkv = wkv.reshape(D_MODEL, 2 * KV_WIDTH)
    wob = wob.reshape(B_WIDTH, D_MODEL)
    q, z, kv, dr1, loss, dg1, db1 = _fwd_b(r0, tgt, wb, wkv, wob, post_ln_g, post_ln_b, attn_sinks, bias)
    dr0, dwb, dwkv, dwob, dsink, dlog, dpost0 = _bwd_b(r0, dr1, q, z, kv, wb, wkv, wob, post_ln_g, post_ln_b,
                                                       attn_sinks, bias)
    drel = _rel_bias_grad(dlog, bucket)

    place = jnp.stack([2 * lax.axis_index("x") + lax.axis_index("y"), lax.axis_index("c")]).astype(jnp.int32)

    def by_device(g):
        return g.reshape((4, 2, g.shape[0] // N_DEV) + g.shape[1:]) if g.ndim == 2 else g.reshape((4, 2) + g.shape[1:])

    dwb, dwkv, dwob = by_device(dwb), by_device(dwkv), by_device(dwob)
    dp, dwoa, dws, dbsp_t, dln, l_wb, l_wkv, l_wob = _bwd_a(dr0, p, ln_gb, ws, bsp_t, woa,
                                                            _Exchange(pair=(dwb, dwkv, dwob)))
    dwoa = by_device(dwoa)
    small_shapes = [(A_GROUPS * CHUNK, CHUNK), (A_GROUPS, CHUNK), (1, N_Q_HEADS), (N_Q_HEADS, REL_BUCKETS),
                    (16, CHUNK), (16, CHUNK), (16, CHUNK), (16, CHUNK), (1, 1)]
    small = by_device(_pack_small([
        dws.reshape(A_GROUPS * CHUNK, CHUNK), dbsp_t[:, :A_GROUPS].T, dsink[:, :N_Q_HEADS], drel,
        jnp.concatenate([dpost0[0:1], dg1], axis=0).reshape(16, CHUNK),
        jnp.concatenate([dpost0[1:2], db1], axis=0).reshape(16, CHUNK),
        dln[0].reshape(16, CHUNK), dln[1].reshape(16, CHUNK), loss]))
    s_wb, s_wkv, s_wob = _pair_sum(place, (dwb, dwkv, dwob), (l_wb, l_wkv, l_wob), (BF16,) * 3)
    dwa, l_wa, l_woa, g_wb, g_wkv, g_wob, small_parts = _in_proj_weight_grad(
        place, xb, dp, wa.shape[2], _Exchange(pair=(dwoa,), chip=(s_wb, s_wkv, s_wob), scatter=(small,)))
    dwa = by_device(dwa)
    s_woa, s_wa, my_small = _pair_sum(place, (dwoa, dwa), (l_woa, l_wa), (BF16, BF16), finish=small_parts)
    _, blocks = _input_grad_blocks(seq)
    dx, g_woa, g_wa, small_all = _input_grad(dp, dr0, wa, 0, blocks, None,
                                             _Exchange(chip=(s_woa, s_wa), gather=(my_small,)))
    full = [dwa, dwoa, dwkv, dwb, dwob]
    landed = [l_wa, l_woa, l_wkv, l_wb, l_wob]
    got = [g_wa, g_woa, g_wkv, g_wb, g_wob]

    big = [(w_in_a, m_w_in_a, v_w_in_a), (w_out_a, m_w_out_a, v_w_out_a), (w_kv, m_w_kv, v_w_kv),
           (w_in_b, m_w_in_b, v_w_in_b), (w_out_b, m_w_out_b, v_w_out_b)]
    items = [(full[i], landed[i], got[i]) + tuple(a.reshape(got[i].shape[1:]) for a in big[i]) for i in range(5)]

    small_all = small_all.reshape(N_DEV * SMALL_ROWS, CHUNK)
    at_ws, at_bsp, at_sink, at_rel, at_pg, at_pb, at_lng, at_lnb, at_loss = _small_rows(small_shapes)
    small_params = [(sgu_ln_g, (at_lng, True), m_sgu_ln_g, v_sgu_ln_g),
                    (sgu_ln_b, (at_lnb, True), m_sgu_ln_b, v_sgu_ln_b),
                    (w_spatial, (at_ws, False), m_w_spatial, v_w_spatial),
                    (b_spatial, (at_bsp, False), m_b_spatial, v_b_spatial),
                    (attn_sinks, (at_sink, False), m_attn_sinks, v_attn_sinks),
                    (rel_t, (at_rel, False), m_rel_bias.T, v_rel_bias.T),
                    (post_ln_g, (at_pg, False), m_post_ln_g, v_post_ln_g),
                    (post_ln_b, (at_pb, False), m_post_ln_b, v_post_ln_b)]
    flat2 = [(w.reshape(-1, w.shape[-1]), at, m.reshape(-1, m.shape[-1]), v.reshape(-1, v.shape[-1]))
             for w, at, m, v in small_params]
    res_in, _, _ = _adamw_shards(place, items[:1], 4)
    res_rest, small_res, (loss_all,) = _adamw_shards(place, items[1:], 2, flat2, small_all, at_loss)
    big_out = [[r.reshape(big[i][0].shape) for r in res] for i, res in enumerate(res_in + res_rest)]
    small_out = [tuple(r.reshape(p[0].shape) for r in res) for res, p in zip(small_res, small_params)]

    order = [("big", 0), ("small", 0), ("small", 1), ("small", 2), ("small", 3), ("big", 1), ("big", 2), ("big", 3),
             ("small", 4), ("small", 5), ("big", 4), ("small", 6), ("small", 7)]
    grads, deltas, new_m, new_v = [], [], [], []
    for kind, i in order:
        if kind == "big":
            g, d, m, v = big_out[i]
        else:
            g, d, m, v = small_out[i]
            if small_params[i][0] is rel_t:
                g, d, m, v = g.T, d.T, m.T, v.T
        grads.append(g)
        deltas.append(d)
        new_m.append(m)
        new_v.append(v)
    return (loss_all.reshape(()), dx.reshape(x.shape), *grads, *deltas, *new_m, *new_v)
```
